```python
import math
import jax, jax.numpy as jnp
from jax import lax
import numpy as np

D_MODEL = 1024
BATCH = 8
SEQ = 4096
DEPTH = 1

SSM_GROUPS = 32
SSM_GROUP_CH = 16
SSM_WIDTH = SSM_GROUPS * SSM_GROUP_CH
SSM_STATE = 64
DT_MIN = 1e-3
DT_MAX = 1e-1
N_HEADS = 8
QK_NOPE = 128
QK_ROPE = 64
QK_HEAD = QK_NOPE + QK_ROPE
V_HEAD = 128
Q_LORA = 384
KV_LORA = 256
ROPE_THETA = 10000.0
Q_BLOCK = 128
MAX_POS_OFFSET = 1024
D_FF = 4 * D_MODEL
EPS = 1e-6
IN_SIZES = (SSM_WIDTH, Q_LORA, KV_LORA + QK_ROPE, D_MODEL, D_MODEL)
IN_OFFSETS = tuple(int(v) for v in np.cumsum(IN_SIZES)[:-1])
D_IN = sum(IN_SIZES)

kernel_name = "hybrid_s5_mla_gated_block"


def rms_norm(x, gain):
    xf = x.astype(jnp.float32)
    inv = lax.rsqrt(jnp.mean(xf * xf, axis=-1, keepdims=True) + EPS)
    return (xf * inv * gain.astype(jnp.float32)).astype(x.dtype)


def rope_tables(positions):
    half = QK_ROPE // 2
    inv_freq = ROPE_THETA ** (-jnp.arange(half, dtype=jnp.float32) / half)
    ang = positions.astype(jnp.float32)[..., None] * inv_freq
    return jnp.cos(ang)[:, :, None, :], jnp.sin(ang)[:, :, None, :]


def apply_rope(x, cos, sin):
    xf = x.astype(jnp.float32)
    x1, x2 = jnp.split(xf, 2, axis=-1)
    out = jnp.concatenate([x1 * cos - x2 * sin, x2 * cos + x1 * sin], axis=-1)
    return out.astype(x.dtype)


def causal_block_attention(q, k, v):
    b, l, h, dq = q.shape
    nblk = l // Q_BLOCK
    qb = q.reshape(b, nblk, Q_BLOCK, h, dq).transpose(1, 0, 2, 3, 4)
    key_idx = jnp.arange(l)
    scale = QK_HEAD ** -0.5

    def one_block(args):
        q_blk, blk = args
        s = jnp.einsum('bqhd,bkhd->bhqk', q_blk, k, preferred_element_type=jnp.float32) * scale
        q_idx = blk * Q_BLOCK + jnp.arange(Q_BLOCK)
        s = jnp.where(key_idx[None, :] <= q_idx[:, None], s, -jnp.inf)
        p = jax.nn.softmax(s, axis=-1).astype(v.dtype)
        return jnp.einsum('bhqk,bkhd->bqhd', p, v)

    out = lax.map(one_block, (qb, jnp.arange(nblk)))
    return out.transpose(1, 0, 2, 3, 4).reshape(b, l, h, -1)


def s5_ssm(u, a_re, a_im, log_dt, b_re, b_im, c_re, c_im, d_skip):
    f32 = jnp.float32
    dt = jnp.exp(log_dt.astype(f32))[:, None]
    lam = lax.complex(a_re.astype(f32), a_im.astype(f32))
    lam_bar = jnp.exp(lam * dt)
    b_mat = lax.complex(b_re.astype(f32), b_im.astype(f32))
    b_bar = ((lam_bar - 1.0) / lam)[..., None] * b_mat
    bu = jnp.einsum('gpc,blgc->blgp', b_bar, u.astype(f32).astype(jnp.complex64))
    a = jnp.broadcast_to(lam_bar, (1, u.shape[1]) + lam_bar.shape)

    def combine(left, right):
        a_l, b_l = left
        a_r, b_r = right
        return a_r * a_l, a_r * b_l + b_r

    _, states = lax.associative_scan(combine, (a, bu), axis=1)
    c_mat = lax.complex(c_re.astype(f32), c_im.astype(f32))
    y = jnp.real(jnp.einsum('gcp,blgp->blgc', c_mat, states)) + d_skip.astype(f32) * u.astype(f32)
    return y.astype(u.dtype)


def hybrid_layer(x, positions, norm_mix, w_in, q_a_norm, kv_a_norm, w_q_b, w_kv_b, q_norm, k_norm, w_o_mla,
                 ssm_a_re, ssm_a_im, ssm_log_dt, ssm_b_re, ssm_b_im, ssm_c_re, ssm_c_im, ssm_d,
                 w_glu, b_glu, w_o_ssm, w_out, norm_mlp, w_up, w_down):
    b, l, _ = x.shape
    xn = rms_norm(x, norm_mix)
    proj = xn @ w_in
    u, q_lat, kv_lat, gate_ssm, gate_mla = jnp.split(proj, IN_OFFSETS, axis=-1)

    y = s5_ssm(u.reshape(b, l, SSM_GROUPS, SSM_GROUP_CH), ssm_a_re, ssm_a_im, ssm_log_dt,
               ssm_b_re, ssm_b_im, ssm_c_re, ssm_c_im, ssm_d).reshape(b, l, SSM_WIDTH)
    z = jax.nn.gelu(y)
    z = z * jax.nn.sigmoid(z @ w_glu + b_glu)
    y_ssm = z @ w_o_ssm

    q = (rms_norm(q_lat, q_a_norm) @ w_q_b).reshape(b, l, N_HEADS, QK_HEAD)
    c_kv, k_pe = kv_lat[..., :KV_LORA], kv_lat[..., KV_LORA:]
    kv = (rms_norm(c_kv, kv_a_norm) @ w_kv_b).reshape(b, l, N_HEADS, QK_NOPE + V_HEAD)
    k_nope, v = kv[..., :QK_NOPE], kv[..., QK_NOPE:]
    k = jnp.concatenate([k_nope, jnp.broadcast_to(k_pe[:, :, None, :], (b, l, N_HEADS, QK_ROPE))], axis=-1)
    q = rms_norm(q, q_norm)
    k = rms_norm(k, k_norm)
    cos, sin = rope_tables(positions)
    q = jnp.concatenate([q[..., :QK_NOPE], apply_rope(q[..., QK_NOPE:], cos, sin)], axis=-1)
    k = jnp.concatenate([k[..., :QK_NOPE], apply_rope(k[..., QK_NOPE:], cos, sin)], axis=-1)
    attn = causal_block_attention(q, k, v).reshape(b, l, N_HEADS * V_HEAD)
    y_mla = attn @ w_o_mla

    mixed = jax.nn.sigmoid(gate_ssm) * y_ssm + jax.nn.sigmoid(gate_mla) * y_mla
    h = x + mixed @ w_out

    hidden = jnp.square(jax.nn.relu(rms_norm(h, norm_mlp) @ w_up))
    return h + hidden @ w_down


def _fwd_setup_inputs(seed: int = 0) -> dict:
    key = jax.random.key(seed)
    ks = jax.random.split(key, 32)
    f32 = jnp.float32

    def dense(k, fan_in, shape):
        return jax.random.normal(k, (DEPTH,) + shape, f32) * (fan_in ** -0.5)

    def gain(k, n):
        return 1.0 + 0.02 * jax.random.normal(k, (DEPTH, n), f32)

    x = jax.random.normal(ks[0], (BATCH, SEQ, D_MODEL), f32)
    offset = jax.random.randint(ks[1], (BATCH, 1), 0, MAX_POS_OFFSET, dtype=jnp.int32)
    positions = (offset + jnp.arange(SEQ, dtype=jnp.int32)[None, :]).astype(jnp.int32)
    n_idx = jnp.arange(SSM_STATE, dtype=f32)
    ssm_a_re = -0.5 + 0.01 * jax.random.normal(ks[2], (DEPTH, SSM_GROUPS, SSM_STATE), f32)
    ssm_a_im = math.pi * n_idx[None, None, :] + 0.01 * jax.random.normal(ks[3], (DEPTH, SSM_GROUPS, SSM_STATE), f32)
    ssm_log_dt = jax.random.uniform(ks[4], (DEPTH, SSM_GROUPS), f32, math.log(DT_MIN), math.log(DT_MAX))
    return {
        "x": x,
        "positions": positions,
        "norm_mix": gain(ks[5], D_MODEL),
        "w_in": dense(ks[6], D_MODEL, (D_MODEL, D_IN)),
        "q_a_norm": gain(ks[7], Q_LORA),
        "kv_a_norm": gain(ks[8], KV_LORA),
        "w_q_b": dense(ks[9], Q_LORA, (Q_LORA, N_HEADS * QK_HEAD)),
        "w_kv_b": dense(ks[10], KV_LORA, (KV_LORA, N_HEADS * (QK_NOPE + V_HEAD))),
        "q_norm": gain(ks[11], QK_HEAD),
        "k_norm": gain(ks[12], QK_HEAD),
        "w_o_mla": dense(ks[13], N_HEADS * V_HEAD, (N_HEADS * V_HEAD, D_MODEL)),
        "ssm_a_re": ssm_a_re,
        "ssm_a_im": ssm_a_im,
        "ssm_log_dt": ssm_log_dt,
        "ssm_b_re": dense(ks[14], 2 * SSM_GROUP_CH, (SSM_GROUPS, SSM_STATE, SSM_GROUP_CH)),
        "ssm_b_im": dense(ks[15], 2 * SSM_GROUP_CH, (SSM_GROUPS, SSM_STATE, SSM_GROUP_CH)),
        "ssm_c_re": dense(ks[16], 2 * SSM_STATE, (SSM_GROUPS, SSM_GROUP_CH, SSM_STATE)),
        "ssm_c_im": dense(ks[17], 2 * SSM_STATE, (SSM_GROUPS, SSM_GROUP_CH, SSM_STATE)),
        "ssm_d": jax.random.normal(ks[18], (DEPTH, SSM_GROUPS, SSM_GROUP_CH), f32),
        "w_glu": dense(ks[19], SSM_WIDTH, (SSM_WIDTH, SSM_WIDTH)),
        "b_glu": 0.01 * jax.random.normal(ks[20], (DEPTH, SSM_WIDTH), f32),
        "w_o_ssm": dense(ks[21], SSM_WIDTH, (SSM_WIDTH, D_MODEL)),
        "w_out": dense(ks[22], D_MODEL, (D_MODEL, D_MODEL)),
        "norm_mlp": gain(ks[23], D_MODEL),
        "w_up": dense(ks[24], D_MODEL, (D_MODEL, D_FF)),
        "w_down": dense(ks[25], D_FF, (D_FF, D_MODEL)),
    }


def _fwd_reference(x, positions, norm_mix, w_in, q_a_norm, kv_a_norm, w_q_b, w_kv_b, q_norm, k_norm, w_o_mla,
              ssm_a_re, ssm_a_im, ssm_log_dt, ssm_b_re, ssm_b_im, ssm_c_re, ssm_c_im, ssm_d,
              w_glu, b_glu, w_o_ssm, w_out, norm_mlp, w_up, w_down):
    h = x
    for layer in range(DEPTH):
        h = hybrid_layer(h, positions, norm_mix[layer], w_in[layer], q_a_norm[layer], kv_a_norm[layer],
                         w_q_b[layer], w_kv_b[layer], q_norm[layer], k_norm[layer], w_o_mla[layer],
                         ssm_a_re[layer], ssm_a_im[layer], ssm_log_dt[layer], ssm_b_re[layer], ssm_b_im[layer],
                         ssm_c_re[layer], ssm_c_im[layer], ssm_d[layer], w_glu[layer], b_glu[layer],
                         w_o_ssm[layer], w_out[layer], norm_mlp[layer], w_up[layer], w_down[layer])
    return h


import jax as _jax
import jax.numpy as _jnp

TWIN_FORMAT = 'train_step'
FWD_PARAMS = ['x', 'positions', 'norm_mix', 'w_in', 'q_a_norm', 'kv_a_norm', 'w_q_b', 'w_kv_b', 'q_norm', 'k_norm', 'w_o_mla', 'ssm_a_re', 'ssm_a_im', 'ssm_log_dt', 'ssm_b_re', 'ssm_b_im', 'ssm_c_re', 'ssm_c_im', 'ssm_d', 'w_glu', 'b_glu', 'w_o_ssm', 'w_out', 'norm_mlp', 'w_up', 'w_down']
TWIN_WEIGHTS = ['norm_mix', 'w_in', 'q_a_norm', 'kv_a_norm', 'w_q_b', 'w_kv_b', 'q_norm', 'k_norm', 'w_o_mla', 'ssm_a_re', 'ssm_a_im', 'ssm_log_dt', 'ssm_b_re', 'ssm_b_im', 'ssm_c_re', 'ssm_c_im', 'ssm_d', 'w_glu', 'b_glu', 'w_o_ssm', 'w_out', 'norm_mlp', 'w_up', 'w_down']
TWIN_DIFF_INPUT = 'x'
TWIN_INPUTS = ['x', 'positions', 'norm_mix', 'w_in', 'q_a_norm', 'kv_a_norm', 'w_q_b', 'w_kv_b', 'q_norm', 'k_norm', 'w_o_mla', 'ssm_a_re', 'ssm_a_im', 'ssm_log_dt', 'ssm_b_re', 'ssm_b_im', 'ssm_c_re', 'ssm_c_im', 'ssm_d', 'w_glu', 'b_glu', 'w_o_ssm', 'w_out', 'norm_mlp', 'w_up', 'w_down', 'loss_target', 'm_norm_mix', 'm_w_in', 'm_q_a_norm', 'm_kv_a_norm', 'm_w_q_b', 'm_w_kv_b', 'm_q_norm', 'm_k_norm', 'm_w_o_mla', 'm_ssm_a_re', 'm_ssm_a_im', 'm_ssm_log_dt', 'm_ssm_b_re', 'm_ssm_b_im', 'm_ssm_c_re', 'm_ssm_c_im', 'm_ssm_d', 'm_w_glu', 'm_b_glu', 'm_w_o_ssm', 'm_w_out', 'm_norm_mlp', 'm_w_up', 'm_w_down', 'v_norm_mix', 'v_w_in', 'v_q_a_norm', 'v_kv_a_norm', 'v_w_q_b', 'v_w_kv_b', 'v_q_norm', 'v_k_norm', 'v_w_o_mla', 'v_ssm_a_re', 'v_ssm_a_im', 'v_ssm_log_dt', 'v_ssm_b_re', 'v_ssm_b_im', 'v_ssm_c_re', 'v_ssm_c_im', 'v_ssm_d', 'v_w_glu', 'v_b_glu', 'v_w_o_ssm', 'v_w_out', 'v_norm_mlp', 'v_w_up', 'v_w_down']
TWIN_OUTPUTS = ['loss', 'grad_x', 'grad_norm_mix', 'grad_w_in', 'grad_q_a_norm', 'grad_kv_a_norm', 'grad_w_q_b', 'grad_w_kv_b', 'grad_q_norm', 'grad_k_norm', 'grad_w_o_mla', 'grad_ssm_a_re', 'grad_ssm_a_im', 'grad_ssm_log_dt', 'grad_ssm_b_re', 'grad_ssm_b_im', 'grad_ssm_c_re', 'grad_ssm_c_im', 'grad_ssm_d', 'grad_w_glu', 'grad_b_glu', 'grad_w_o_ssm', 'grad_w_out', 'grad_norm_mlp', 'grad_w_up', 'grad_w_down', 'delta_norm_mix', 'delta_w_in', 'delta_q_a_norm', 'delta_kv_a_norm', 'delta_w_q_b', 'delta_w_kv_b', 'delta_q_norm', 'delta_k_norm', 'delta_w_o_mla', 'delta_ssm_a_re', 'delta_ssm_a_im', 'delta_ssm_log_dt', 'delta_ssm_b_re', 'delta_ssm_b_im', 'delta_ssm_c_re', 'delta_ssm_c_im', 'delta_ssm_d', 'delta_w_glu', 'delta_b_glu', 'delta_w_o_ssm', 'delta_w_out', 'delta_norm_mlp', 'delta_w_up', 'delta_w_down', 'new_m_norm_mix', 'new_m_w_in', 'new_m_q_a_norm', 'new_m_kv_a_norm', 'new_m_w_q_b', 'new_m_w_kv_b', 'new_m_q_norm', 'new_m_k_norm', 'new_m_w_o_mla', 'new_m_ssm_a_re', 'new_m_ssm_a_im', 'new_m_ssm_log_dt', 'new_m_ssm_b_re', 'new_m_ssm_b_im', 'new_m_ssm_c_re', 'new_m_ssm_c_im', 'new_m_ssm_d', 'new_m_w_glu', 'new_m_b_glu', 'new_m_w_o_ssm', 'new_m_w_out', 'new_m_norm_mlp', 'new_m_w_up', 'new_m_w_down', 'new_v_norm_mix', 'new_v_w_in', 'new_v_q_a_norm', 'new_v_kv_a_norm', 'new_v_w_q_b', 'new_v_w_kv_b', 'new_v_q_norm', 'new_v_k_norm', 'new_v_w_o_mla', 'new_v_ssm_a_re', 'new_v_ssm_a_im', 'new_v_ssm_log_dt', 'new_v_ssm_b_re', 'new_v_ssm_b_im', 'new_v_ssm_c_re', 'new_v_ssm_c_im', 'new_v_ssm_d', 'new_v_w_glu', 'new_v_b_glu', 'new_v_w_o_ssm', 'new_v_w_out', 'new_v_norm_mlp', 'new_v_w_up', 'new_v_w_down']
TWIN_LEAF_KINDS = {'loss': 'loss', 'grad_x': 'grad_x', 'grad_norm_mix': 'grad_w', 'grad_w_in': 'grad_w', 'grad_q_a_norm': 'grad_w', 'grad_kv_a_norm': 'grad_w', 'grad_w_q_b': 'grad_w', 'grad_w_kv_b': 'grad_w', 'grad_q_norm': 'grad_w', 'grad_k_norm': 'grad_w', 'grad_w_o_mla': 'grad_w', 'grad_ssm_a_re': 'grad_w', 'grad_ssm_a_im': 'grad_w', 'grad_ssm_log_dt': 'grad_w', 'grad_ssm_b_re': 'grad_w', 'grad_ssm_b_im': 'grad_w', 'grad_ssm_c_re': 'grad_w', 'grad_ssm_c_im': 'grad_w', 'grad_ssm_d': 'grad_w', 'grad_w_glu': 'grad_w', 'grad_b_glu': 'grad_w', 'grad_w_o_ssm': 'grad_w', 'grad_w_out': 'grad_w', 'grad_norm_mlp': 'grad_w', 'grad_w_up': 'grad_w', 'grad_w_down': 'grad_w', 'delta_norm_mix': 'delta_w', 'delta_w_in': 'delta_w', 'delta_q_a_norm': 'delta_w', 'delta_kv_a_norm': 'delta_w', 'delta_w_q_b': 'delta_w', 'delta_w_kv_b': 'delta_w', 'delta_q_norm': 'delta_w', 'delta_k_norm': 'delta_w', 'delta_w_o_mla': 'delta_w', 'delta_ssm_a_re': 'delta_w', 'delta_ssm_a_im': 'delta_w', 'delta_ssm_log_dt': 'delta_w', 'delta_ssm_b_re': 'delta_w', 'delta_ssm_b_im': 'delta_w', 'delta_ssm_c_re': 'delta_w', 'delta_ssm_c_im': 'delta_w', 'delta_ssm_d': 'delta_w', 'delta_w_glu': 'delta_w', 'delta_b_glu': 'delta_w', 'delta_w_o_ssm': 'delta_w', 'delta_w_out': 'delta_w', 'delta_norm_mlp': 'delta_w', 'delta_w_up': 'delta_w', 'delta_w_down': 'delta_w', 'new_m_norm_mix': 'new_m', 'new_m_w_in': 'new_m', 'new_m_q_a_norm': 'new_m', 'new_m_kv_a_norm': 'new_m', 'new_m_w_q_b': 'new_m', 'new_m_w_kv_b': 'new_m', 'new_m_q_norm': 'new_m', 'new_m_k_norm': 'new_m', 'new_m_w_o_mla': 'new_m', 'new_m_ssm_a_re': 'new_m', 'new_m_ssm_a_im': 'new_m', 'new_m_ssm_log_dt': 'new_m', 'new_m_ssm_b_re': 'new_m', 'new_m_ssm_b_im': 'new_m', 'new_m_ssm_c_re': 'new_m', 'new_m_ssm_c_im': 'new_m', 'new_m_ssm_d': 'new_m', 'new_m_w_glu': 'new_m', 'new_m_b_glu': 'new_m', 'new_m_w_o_ssm': 'new_m', 'new_m_w_out': 'new_m', 'new_m_norm_mlp': 'new_m', 'new_m_w_up': 'new_m', 'new_m_w_down': 'new_m', 'new_v_norm_mix': 'new_v', 'new_v_w_in': 'new_v', 'new_v_q_a_norm': 'new_v', 'new_v_kv_a_norm': 'new_v', 'new_v_w_q_b': 'new_v', 'new_v_w_kv_b': 'new_v', 'new_v_q_norm': 'new_v', 'new_v_k_norm': 'new_v', 'new_v_w_o_mla': 'new_v', 'new_v_ssm_a_re': 'new_v', 'new_v_ssm_a_im': 'new_v', 'new_v_ssm_log_dt': 'new_v', 'new_v_ssm_b_re': 'new_v', 'new_v_ssm_b_im': 'new_v', 'new_v_ssm_c_re': 'new_v', 'new_v_ssm_c_im': 'new_v', 'new_v_ssm_d': 'new_v', 'new_v_w_glu': 'new_v', 'new_v_b_glu': 'new_v', 'new_v_w_o_ssm': 'new_v', 'new_v_w_out': 'new_v', 'new_v_norm_mlp': 'new_v', 'new_v_w_up': 'new_v', 'new_v_w_down': 'new_v'}


def _forward(args):
    return _fwd_reference(*[args[k] for k in FWD_PARAMS])


def _output_shape():
    out = _jax.eval_shape(lambda: _forward(_fwd_setup_inputs(0)))
    return out.shape, out.dtype

N_MICROBATCH = 1
ADAM_LR = 0.001
ADAM_B1 = 0.9
ADAM_B2 = 0.999
ADAM_EPS = 1e-08
ADAM_WD = 0.01
ADAM_STEP = 10
PER_EXAMPLE_BATCH_AXIS = {'x': 0, 'positions': 0, 'loss_target': 0}
SHARED_INPUTS = []
_WEIGHT_DTYPES = {'norm_mix': _jnp.float32, 'w_in': _jnp.float32, 'q_a_norm': _jnp.float32, 'kv_a_norm': _jnp.float32, 'w_q_b': _jnp.float32, 'w_kv_b': _jnp.float32, 'q_norm': _jnp.float32, 'k_norm': _jnp.float32, 'w_o_mla': _jnp.float32, 'ssm_a_re': _jnp.float32, 'ssm_a_im': _jnp.float32, 'ssm_log_dt': _jnp.float32, 'ssm_b_re': _jnp.float32, 'ssm_b_im': _jnp.float32, 'ssm_c_re': _jnp.float32, 'ssm_c_im': _jnp.float32, 'ssm_d': _jnp.float32, 'w_glu': _jnp.float32, 'b_glu': _jnp.float32, 'w_o_ssm': _jnp.float32, 'w_out': _jnp.float32, 'norm_mlp': _jnp.float32, 'w_up': _jnp.float32, 'w_down': _jnp.float32}
MOMENT_SCALE = {'norm_mix': 9.555336e-01, 'w_in': 1.450265e-01, 'q_a_norm': 1.297076e-01, 'kv_a_norm': 4.861782e-01, 'w_q_b': 6.159875e-02, 'w_kv_b': 1.270290e-01, 'q_norm': 3.057013e-01, 'k_norm': 3.051076e-01, 'w_o_mla': 1.650411e-01, 'ssm_a_re': 2.345834e-02, 'ssm_a_im': 1.496608e-02, 'ssm_log_dt': 1.152906e+01, 'ssm_b_re': 8.794716e-03, 'ssm_b_im': 8.824089e-03, 'ssm_c_re': 1.880479e-02, 'ssm_c_im': 2.039960e-02, 'ssm_d': 4.668950e+00, 'w_glu': 8.209927e-01, 'b_glu': 2.536240e+00, 'w_o_ssm': 2.595584e+00, 'w_out': 2.489505e+00, 'norm_mlp': 9.516213e+01, 'w_up': 1.102769e+00, 'w_down': 8.076429e+00}


def _to_microbatches(a, axis):
    t = _jnp.moveaxis(a, axis, 0)
    t = t.reshape((N_MICROBATCH, t.shape[0] // N_MICROBATCH) + t.shape[1:])
    return _jnp.moveaxis(t, 1, axis + 1)


def setup_inputs(seed: int = 0) -> dict:
    inp = _fwd_setup_inputs(seed)
    key = _jax.random.fold_in(_jax.random.key(seed), 7919)
    shape, _ = _output_shape()
    out = dict(inp)
    out["loss_target"] = _jax.random.normal(_jax.random.fold_in(key, 0), shape, _jnp.float32)
    for i, name in enumerate(TWIN_WEIGHTS):
        w = inp[name].astype(_jnp.float32)
        if MOMENT_SCALE is None:
            s = _jnp.sqrt(_jnp.mean(_jnp.square(w)) + 1e-30)
        else:
            s = MOMENT_SCALE[name]
        km, kv = _jax.random.split(_jax.random.fold_in(key, i + 1))
        out[name] = w
        out["m_" + name] = s * _jax.random.normal(km, w.shape, _jnp.float32)
        out["v_" + name] = (s * s) * _jax.random.uniform(kv, w.shape, _jnp.float32, 0.5, 1.5)
    if N_MICROBATCH > 1:
        for name, axis in PER_EXAMPLE_BATCH_AXIS.items():
            out[name] = _to_microbatches(out[name], axis)
    return {'x': out['x'], 'positions': out['positions'], 'norm_mix': out['norm_mix'], 'w_in': out['w_in'], 'q_a_norm': out['q_a_norm'], 'kv_a_norm': out['kv_a_norm'], 'w_q_b': out['w_q_b'], 'w_kv_b': out['w_kv_b'], 'q_norm': out['q_norm'], 'k_norm': out['k_norm'], 'w_o_mla': out['w_o_mla'], 'ssm_a_re': out['ssm_a_re'], 'ssm_a_im': out['ssm_a_im'], 'ssm_log_dt': out['ssm_log_dt'], 'ssm_b_re': out['ssm_b_re'], 'ssm_b_im': out['ssm_b_im'], 'ssm_c_re': out['ssm_c_re'], 'ssm_c_im': out['ssm_c_im'], 'ssm_d': out['ssm_d'], 'w_glu': out['w_glu'], 'b_glu': out['b_glu'], 'w_o_ssm': out['w_o_ssm'], 'w_out': out['w_out'], 'norm_mlp': out['norm_mlp'], 'w_up': out['w_up'], 'w_down': out['w_down'], 'loss_target': out['loss_target'], 'm_norm_mix': out['m_norm_mix'], 'm_w_in': out['m_w_in'], 'm_q_a_norm': out['m_q_a_norm'], 'm_kv_a_norm': out['m_kv_a_norm'], 'm_w_q_b': out['m_w_q_b'], 'm_w_kv_b': out['m_w_kv_b'], 'm_q_norm': out['m_q_norm'], 'm_k_norm': out['m_k_norm'], 'm_w_o_mla': out['m_w_o_mla'], 'm_ssm_a_re': out['m_ssm_a_re'], 'm_ssm_a_im': out['m_ssm_a_im'], 'm_ssm_log_dt': out['m_ssm_log_dt'], 'm_ssm_b_re': out['m_ssm_b_re'], 'm_ssm_b_im': out['m_ssm_b_im'], 'm_ssm_c_re': out['m_ssm_c_re'], 'm_ssm_c_im': out['m_ssm_c_im'], 'm_ssm_d': out['m_ssm_d'], 'm_w_glu': out['m_w_glu'], 'm_b_glu': out['m_b_glu'], 'm_w_o_ssm': out['m_w_o_ssm'], 'm_w_out': out['m_w_out'], 'm_norm_mlp': out['m_norm_mlp'], 'm_w_up': out['m_w_up'], 'm_w_down': out['m_w_down'], 'v_norm_mix': out['v_norm_mix'], 'v_w_in': out['v_w_in'], 'v_q_a_norm': out['v_q_a_norm'], 'v_kv_a_norm': out['v_kv_a_norm'], 'v_w_q_b': out['v_w_q_b'], 'v_w_kv_b': out['v_w_kv_b'], 'v_q_norm': out['v_q_norm'], 'v_k_norm': out['v_k_norm'], 'v_w_o_mla': out['v_w_o_mla'], 'v_ssm_a_re': out['v_ssm_a_re'], 'v_ssm_a_im': out['v_ssm_a_im'], 'v_ssm_log_dt': out['v_ssm_log_dt'], 'v_ssm_b_re': out['v_ssm_b_re'], 'v_ssm_b_im': out['v_ssm_b_im'], 'v_ssm_c_re': out['v_ssm_c_re'], 'v_ssm_c_im': out['v_ssm_c_im'], 'v_ssm_d': out['v_ssm_d'], 'v_w_glu': out['v_w_glu'], 'v_b_glu': out['v_b_glu'], 'v_w_o_ssm': out['v_w_o_ssm'], 'v_w_out': out['v_w_out'], 'v_norm_mlp': out['v_norm_mlp'], 'v_w_up': out['v_w_up'], 'v_w_down': out['v_w_down']}


def _loss(weights, diff, rest, loss_target):
    with _jax.named_scope("forward"):
        args = {**rest, TWIN_DIFF_INPUT: diff, **{k: w.astype(_WEIGHT_DTYPES[k]) for k, w in weights.items()}}
        y = _forward(args)
    with _jax.named_scope("loss_head"):
        err = _jnp.square(y.astype(_jnp.float32) - loss_target)
        return 0.5 * _jnp.sum(_jnp.mean(err, axis=-1)) if err.ndim else 0.5 * err


def _adamw(w, g, m, v):
    m = ADAM_B1 * m + (1.0 - ADAM_B1) * g
    v = ADAM_B2 * v + (1.0 - ADAM_B2) * _jnp.square(g)
    m_hat = m / (1.0 - ADAM_B1 ** ADAM_STEP)
    v_hat = v / (1.0 - ADAM_B2 ** ADAM_STEP)
    delta = -ADAM_LR * (m_hat / (_jnp.sqrt(v_hat) + ADAM_EPS) + ADAM_WD * w)
    return delta, m, v


def reference(x, positions, norm_mix, w_in, q_a_norm, kv_a_norm, w_q_b, w_kv_b, q_norm, k_norm, w_o_mla, ssm_a_re, ssm_a_im, ssm_log_dt, ssm_b_re, ssm_b_im, ssm_c_re, ssm_c_im, ssm_d, w_glu, b_glu, w_o_ssm, w_out, norm_mlp, w_up, w_down, loss_target, m_norm_mix, m_w_in, m_q_a_norm, m_kv_a_norm, m_w_q_b, m_w_kv_b, m_q_norm, m_k_norm, m_w_o_mla, m_ssm_a_re, m_ssm_a_im, m_ssm_log_dt, m_ssm_b_re, m_ssm_b_im, m_ssm_c_re, m_ssm_c_im, m_ssm_d, m_w_glu, m_b_glu, m_w_o_ssm, m_w_out, m_norm_mlp, m_w_up, m_w_down, v_norm_mix, v_w_in, v_q_a_norm, v_kv_a_norm, v_w_q_b, v_w_kv_b, v_q_norm, v_k_norm, v_w_o_mla, v_ssm_a_re, v_ssm_a_im, v_ssm_log_dt, v_ssm_b_re, v_ssm_b_im, v_ssm_c_re, v_ssm_c_im, v_ssm_d, v_w_glu, v_b_glu, v_w_o_ssm, v_w_out, v_norm_mlp, v_w_up, v_w_down):
    given = dict(x=x, positions=positions, norm_mix=norm_mix, w_in=w_in, q_a_norm=q_a_norm, kv_a_norm=kv_a_norm, w_q_b=w_q_b, w_kv_b=w_kv_b, q_norm=q_norm, k_norm=k_norm, w_o_mla=w_o_mla, ssm_a_re=ssm_a_re, ssm_a_im=ssm_a_im, ssm_log_dt=ssm_log_dt, ssm_b_re=ssm_b_re, ssm_b_im=ssm_b_im, ssm_c_re=ssm_c_re, ssm_c_im=ssm_c_im, ssm_d=ssm_d, w_glu=w_glu, b_glu=b_glu, w_o_ssm=w_o_ssm, w_out=w_out, norm_mlp=norm_mlp, w_up=w_up, w_down=w_down, loss_target=loss_target, m_norm_mix=m_norm_mix, m_w_in=m_w_in, m_q_a_norm=m_q_a_norm, m_kv_a_norm=m_kv_a_norm, m_w_q_b=m_w_q_b, m_w_kv_b=m_w_kv_b, m_q_norm=m_q_norm, m_k_norm=m_k_norm, m_w_o_mla=m_w_o_mla, m_ssm_a_re=m_ssm_a_re, m_ssm_a_im=m_ssm_a_im, m_ssm_log_dt=m_ssm_log_dt, m_ssm_b_re=m_ssm_b_re, m_ssm_b_im=m_ssm_b_im, m_ssm_c_re=m_ssm_c_re, m_ssm_c_im=m_ssm_c_im, m_ssm_d=m_ssm_d, m_w_glu=m_w_glu, m_b_glu=m_b_glu, m_w_o_ssm=m_w_o_ssm, m_w_out=m_w_out, m_norm_mlp=m_norm_mlp, m_w_up=m_w_up, m_w_down=m_w_down, v_norm_mix=v_norm_mix, v_w_in=v_w_in, v_q_a_norm=v_q_a_norm, v_kv_a_norm=v_kv_a_norm, v_w_q_b=v_w_q_b, v_w_kv_b=v_w_kv_b, v_q_norm=v_q_norm, v_k_norm=v_k_norm, v_w_o_mla=v_w_o_mla, v_ssm_a_re=v_ssm_a_re, v_ssm_a_im=v_ssm_a_im, v_ssm_log_dt=v_ssm_log_dt, v_ssm_b_re=v_ssm_b_re, v_ssm_b_im=v_ssm_b_im, v_ssm_c_re=v_ssm_c_re, v_ssm_c_im=v_ssm_c_im, v_ssm_d=v_ssm_d, v_w_glu=v_w_glu, v_b_glu=v_b_glu, v_w_o_ssm=v_w_o_ssm, v_w_out=v_w_out, v_norm_mlp=v_norm_mlp, v_w_up=v_w_up, v_w_down=v_w_down)
    weights = {n: given[n] for n in TWIN_WEIGHTS}
    shared = {n: given[n] for n in SHARED_INPUTS}
    per_example = {n: given[n] for n in ['x', 'positions']}
    grad_fn = _jax.value_and_grad(_loss, argnums=(0, 1))

    def one_microbatch(ex, loss_target):
        ex = dict(ex)
        diff = ex.pop(TWIN_DIFF_INPUT)
        return grad_fn(weights, diff, {**shared, **ex}, loss_target)

    if N_MICROBATCH == 1:
        loss, (grad_w, grad_x) = one_microbatch(per_example, given["loss_target"])
    else:
        def body(carry, xs):
            loss_sum, grad_sum = carry
            l_k, (gw_k, gx_k) = one_microbatch(xs[0], xs[1])
            with _jax.named_scope("update"):
                return (loss_sum + l_k, _jax.tree.map(_jnp.add, grad_sum, gw_k)), gx_k

        init = (_jnp.zeros((), _jnp.float32), _jax.tree.map(_jnp.zeros_like, weights))
        (loss, grad_w), grad_x = _jax.lax.scan(body, init, (per_example, given["loss_target"]))
    with _jax.named_scope("update"):
        delta_w, new_m, new_v = {}, {}, {}
        for n in TWIN_WEIGHTS:
            delta_w[n], new_m[n], new_v[n] = _adamw(weights[n], grad_w[n], given["m_" + n], given["v_" + n])
    return (loss, grad_x, *[grad_w[n] for n in TWIN_WEIGHTS], *[delta_w[n] for n in TWIN_WEIGHTS],
            *[new_m[n] for n in TWIN_WEIGHTS], *[new_v[n] for n in TWIN_WEIGHTS])
```

```python
import functools
import math

import jax
import jax.numpy as jnp
import numpy as np
from jax import lax
from jax.experimental import pallas as pl
from jax.experimental.pallas import tpu as pltpu

F32 = jnp.float32
BF16 = jnp.bfloat16

D_MODEL = 1024
SSM_GROUPS = 32
SSM_GROUP_CH = 16
SSM_WIDTH = 512
SSM_STATE = 64
GP = SSM_GROUPS * SSM_STATE
N_HEADS = 8
QK_NOPE = 128
QK_ROPE = 64
QK_HEAD = 192
HEAD_PAD = 256
V_HEAD = 128
Q_LORA = 384
KV_LORA = 256
LAT_W = 768
D_IN = 3264
D_IN_PAD = 3328
D_FF = 4096
ROPE_THETA = 10000.0
EPS = 1e-6
ATT_SCALE = QK_HEAD ** -0.5

ADAM_LR = 0.001
ADAM_B1 = 0.9
ADAM_B2 = 0.999
ADAM_EPS = 1e-08
ADAM_WD = 0.01
ADAM_STEP = 10

VMEM_LIMIT_V7X = 56 * 1024 * 1024
MESH = pl.DeviceIdType.MESH

BIG_WEIGHTS = (
    ("w_in", 1024, 3264, "col"),
    ("w_q_b", 384, 1536, "col"),
    ("w_kv_b", 256, 2048, "col"),
    ("w_o_mla", 1024, 1024, "row"),
    ("w_glu", 512, 512, "row"),
    ("w_o_ssm", 512, 1024, "col"),
    ("w_out", 1024, 1024, "row"),
    ("w_up", 1024, 4096, "col"),
    ("w_down", 4096, 1024, "row"),
)
SMALL_WEIGHTS = (
    ("norm_mix", (1024,)), ("q_a_norm", (384,)), ("kv_a_norm", (256,)), ("q_norm", (192,)), ("k_norm", (192,)),
    ("ssm_a_re", (32, 64)), ("ssm_a_im", (32, 64)), ("ssm_log_dt", (32,)),
    ("ssm_b_re", (32, 64, 16)), ("ssm_b_im", (32, 64, 16)), ("ssm_c_re", (32, 16, 64)), ("ssm_c_im", (32, 16, 64)),
    ("ssm_d", (32, 16)), ("b_glu", (512,)), ("norm_mlp", (1024,)),
)
WEIGHT_ORDER = ('norm_mix', 'w_in', 'q_a_norm', 'kv_a_norm', 'w_q_b', 'w_kv_b', 'q_norm', 'k_norm', 'w_o_mla', 'ssm_a_re',
                'ssm_a_im', 'ssm_log_dt', 'ssm_b_re', 'ssm_b_im', 'ssm_c_re', 'ssm_c_im', 'ssm_d', 'w_glu', 'b_glu',
                'w_o_ssm', 'w_out', 'norm_mlp', 'w_up', 'w_down')


def _cparams(*sem):
    return pltpu.CompilerParams(dimension_semantics=sem if sem else None, vmem_limit_bytes=VMEM_LIMIT_V7X)


def _resident(shape):
    zeros = (0,) * len(shape)
    return pl.BlockSpec(shape, lambda *_: zeros, pipeline_mode=pl.Buffered(1))


def _rows(t, width):
    return pl.BlockSpec((t, width), lambda i: (i, 0))


def _mm(a, b):
    return jnp.dot(a.astype(BF16), b.astype(BF16), preferred_element_type=F32)


def _mm_nt(a, b):
    return lax.dot_general(a.astype(BF16), b.astype(BF16), (((1,), (1,)), ((), ())), preferred_element_type=F32)


def _mm_tn(a, b):
    return lax.dot_general(a.astype(BF16), b.astype(BF16), (((0,), (0,)), ((), ())), preferred_element_type=F32)


def _rms_fwd(x, g, n):
    r = lax.rsqrt(jnp.sum(x * x, axis=-1, keepdims=True) * (1.0 / n) + EPS)
    return x * r * g


def _rms_bwd(x, g, dy, n):
    r = lax.rsqrt(jnp.sum(x * x, axis=-1, keepdims=True) * (1.0 / n) + EPS)
    xh = x * r
    dxh = dy * g
    dx = r * (dxh - xh * (jnp.sum(dxh * xh, axis=-1, keepdims=True) * (1.0 / n)))
    return dx, dy * xh


def _colsum(a):
    return jnp.sum(a, axis=0, keepdims=True)


def _accumulate(ref, value, first):
    @pl.when(first)
    def _():
        ref[...] = value

    @pl.when(jnp.logical_not(first))
    def _():
        ref[...] += value


def _sigmoid(a):
    return 1.0 / (1.0 + jnp.exp(-a))


GELU_C = math.sqrt(2.0 / math.pi)
GELU_A = 0.044715


def _gelu(y):
    return 0.5 * y * (1.0 + jnp.tanh(GELU_C * (y + GELU_A * y * y * y)))


def _gelu_grad(y):
    t = jnp.tanh(GELU_C * (y + GELU_A * y * y * y))
    return 0.5 * (1.0 + t) + 0.5 * y * (1.0 - t * t) * GELU_C * (1.0 + 3.0 * GELU_A * y * y)


def _in_proj_fwd(x, g1, w_in_p, t):
    l = x.shape[0]

    def body(x_ref, g_ref, w_ref, u_ref, lat_ref, gs_ref, gm_ref):
        xn = _rms_fwd(x_ref[...], g_ref[...], D_MODEL).astype(BF16)
        u_ref[...] = _mm(xn, w_ref[:, 0:512])
        lat_ref[...] = _mm(xn, w_ref[:, 512:1280])
        gs_ref[...] = _mm(xn, w_ref[:, 1280:2304])
        gm_ref[...] = _mm(xn, w_ref[:, 2304:3328])

    return pl.pallas_call(
        body, name="in_proj_fwd", grid=(l // t,),
        in_specs=[_rows(t, D_MODEL), _resident((1, D_MODEL)), _resident((D_MODEL, D_IN_PAD))],
        out_specs=[_rows(t, 512), _rows(t, LAT_W), _rows(t, D_MODEL), _rows(t, D_MODEL)],
        out_shape=[jax.ShapeDtypeStruct((l, 512), F32), jax.ShapeDtypeStruct((l, LAT_W), F32),
                   jax.ShapeDtypeStruct((l, D_MODEL), F32), jax.ShapeDtypeStruct((l, D_MODEL), F32)],
        compiler_params=_cparams("parallel"),
    )(x, g1, w_in_p)


def _in_proj_bwd(x, g1, w_in_p, d_u, d_lat, d_gs, d_gm, dh, t):
    l = x.shape[0]

    def body(x_ref, g_ref, w_ref, du_ref, dlat_ref, dgs_ref, dgm_ref, dh_ref, gx_ref, xn_ref, dproj_ref, dg_ref):
        xv = x_ref[...]
        g = g_ref[...]
        xn_ref[...] = _rms_fwd(xv, g, D_MODEL).astype(BF16)
        dproj_ref[:, 0:512] = du_ref[...]
        dproj_ref[:, 512:1280] = dlat_ref[...]
        dproj_ref[:, 1280:2304] = dgs_ref[...]
        dproj_ref[:, 2304:3328] = dgm_ref[...]
        dxn = _mm_nt(dproj_ref[...], w_ref[...])
        dx, dg_rows = _rms_bwd(xv, g, dxn, D_MODEL)
        gx_ref[...] = dh_ref[...] + dx
        _accumulate(dg_ref, _colsum(dg_rows), pl.program_id(0) == 0)

    return pl.pallas_call(
        body, name="in_proj_bwd", grid=(l // t,),
        in_specs=[_rows(t, D_MODEL), _resident((1, D_MODEL)), _resident((D_MODEL, D_IN_PAD)), _rows(t, 512),
                  _rows(t, LAT_W), _rows(t, D_MODEL), _rows(t, D_MODEL), _rows(t, D_MODEL)],
        out_specs=[_rows(t, D_MODEL), _rows(t, D_MODEL), _rows(t, D_IN_PAD), pl.BlockSpec((1, D_MODEL), lambda i: (0, 0))],
        out_shape=[jax.ShapeDtypeStruct((l, D_MODEL), F32), jax.ShapeDtypeStruct((l, D_MODEL), BF16),
                   jax.ShapeDtypeStruct((l, D_IN_PAD), BF16), jax.ShapeDtypeStruct((1, D_MODEL), F32)],
        compiler_params=_cparams("arbitrary"),
    )(x, g1, w_in_p, d_u, d_lat, d_gs, d_gm, dh)


def _ssm_param_fn(a_re, a_im, log_dt, bt_re, bt_im):
    dt = jnp.exp(log_dt)
    er = jnp.exp(a_re * dt)
    lr = er * jnp.cos(a_im * dt)
    li = er * jnp.sin(a_im * dt)
    den = a_re * a_re + a_im * a_im
    nr = lr - 1.0
    kr = (nr * a_re + li * a_im) / den
    ki = (li * a_re - nr * a_im) / den
    krt = jnp.concatenate([kr] * SSM_GROUP_CH, axis=0)
    kit = jnp.concatenate([ki] * SSM_GROUP_CH, axis=0)
    return lr, li, krt * bt_re - kit * bt_im, krt * bt_im + kit * bt_re


def _ssm_param_fwd(a_re, a_im, log_dt, bt_re, bt_im):
    def body(ar_ref, ai_ref, ldt_ref, br_ref, bi_ref, pr_ref, pi_ref, bbr_ref, bbi_ref):
        lr, li, bbr, bbi = _ssm_param_fn(ar_ref[...], ai_ref[...], ldt_ref[...], br_ref[...], bi_ref[...])
        bbr_ref[...] = bbr
        bbi_ref[...] = bbi
        qr, qi = lr, li
        for j in range(8):
            pr_ref[j] = qr
            pi_ref[j] = qi
            qr, qi = qr * lr - qi * li, qr * li + qi * lr

    g, p = SSM_GROUPS, SSM_STATE
    return pl.pallas_call(
        body, name="ssm_param_fwd",
        out_shape=[jax.ShapeDtypeStruct((8, g, p), F32), jax.ShapeDtypeStruct((8, g, p), F32),
                   jax.ShapeDtypeStruct((SSM_GROUP_CH * g, p), F32), jax.ShapeDtypeStruct((SSM_GROUP_CH * g, p), F32)],
    )(a_re, a_im, log_dt, bt_re, bt_im)


def _ssm_param_bwd(a_re, a_im, log_dt, bt_re, bt_im, g_lr, g_li, g_bbr, g_bbi):
    def body(ar_ref, ai_ref, ldt_ref, br_ref, bi_ref, glr_ref, gli_ref, gbr_ref, gbi_ref, o_ar, o_ai, o_ldt, o_br, o_bi):
        _, vjp = jax.vjp(_ssm_param_fn, ar_ref[...], ai_ref[...], ldt_ref[...], br_ref[...], bi_ref[...])
        d_ar, d_ai, d_ldt, d_br, d_bi = vjp((glr_ref[...], gli_ref[...], gbr_ref[...], gbi_ref[...]))
        o_ar[...] = d_ar
        o_ai[...] = d_ai
        o_ldt[...] = d_ldt
        o_br[...] = d_br
        o_bi[...] = d_bi

    g, p = SSM_GROUPS, SSM_STATE
    return pl.pallas_call(
        body, name="ssm_param_bwd",
        out_shape=[jax.ShapeDtypeStruct((g, p), F32), jax.ShapeDtypeStruct((g, p), F32), jax.ShapeDtypeStruct((g, 1), F32),
                   jax.ShapeDtypeStruct((SSM_GROUP_CH * g, p), F32), jax.ShapeDtypeStruct((SSM_GROUP_CH * g, p), F32)],
    )(a_re, a_im, log_dt, bt_re, bt_im, g_lr, g_li, g_bbr, g_bbi)


def _block_diag(t_gcp):
    eye = jnp.eye(SSM_GROUPS, dtype=t_gcp.dtype)
    return (t_gcp[:, :, None, :] * eye[:, None, :, None]).reshape(SSM_WIDTH, GP)


def _block_diag_take(m):
    t = m.reshape(SSM_GROUPS, SSM_GROUP_CH, SSM_GROUPS, SSM_STATE)
    return jnp.transpose(jnp.diagonal(t, axis1=0, axis2=2), (2, 0, 1))


def _scan_tables(pr, pi, reverse):
    pr = pr.reshape(8, GP)
    pi = pi.reshape(8, GP)
    if reverse:
        pi = -pi
    row = jnp.arange(8)[:, None]
    tabs = []
    for k in (1, 2, 4):
        keep = (row < 8 - k) if reverse else (row >= k)
        tabs.append(jnp.where(keep, pr[k - 1][None, :], 0.0))
        tabs.append(jnp.where(keep, pi[k - 1][None, :], 0.0))
    if reverse:
        tabs += [pr[::-1], pi[::-1]]
    else:
        tabs += [pr, pi]
    return jnp.stack(tabs).astype(F32)


SCAN_STRIP = 512


def _scan_chunk(inr_ref, ini_ref, outr_ref, outi_ref, cr_ref, ci_ref, tab_ref, tc, reverse):
    n_blocks = tc // 8

    def block(j, _):
        i = (n_blocks - 1 - j) if reverse else j
        rows = pl.ds(pl.multiple_of(i * 8, 8), 8)
        for s in range(GP // SCAN_STRIP):
            sl = pl.ds(s * SCAN_STRIP, SCAN_STRIP)
            xr = inr_ref[rows, sl]
            xi = ini_ref[rows, sl]
            for n, k in enumerate((1, 2, 4)):
                shift = (8 - k) if reverse else k
                sr = pltpu.roll(xr, shift, 0)
                si = pltpu.roll(xi, shift, 0)
                mr = tab_ref[2 * n, :, sl]
                mi = tab_ref[2 * n + 1, :, sl]
                xr, xi = xr + mr * sr - mi * si, xi + mr * si + mi * sr
            qr = tab_ref[6, :, sl]
            qi = tab_ref[7, :, sl]
            cr = cr_ref[:, sl]
            ci = ci_ref[:, sl]
            xr, xi = xr + qr * cr - qi * ci, xi + qr * ci + qi * cr
            outr_ref[rows, sl] = xr
            outi_ref[rows, sl] = xi
            edge = 0 if reverse else 7
            cr_ref[:, sl] = jnp.broadcast_to(xr[edge:edge + 1, :], (8, SCAN_STRIP))
            ci_ref[:, sl] = jnp.broadcast_to(xi[edge:edge + 1, :], (8, SCAN_STRIP))
        return 0

    lax.fori_loop(0, n_blocks, block, 0)


def _ssm_fwd(u, wb, wc, tabs, dskip, w_glu, b_glu, w_o_ssm, tc):
    l = u.shape[0]

    def body(u_ref, wb_ref, wc_ref, tab_ref, d_ref, wg_ref, bg_ref, wo_ref, xr_ref, xi_ref, y_ref, ys_ref,
             bur, bui, cr, ci):
        @pl.when(pl.program_id(0) == 0)
        def _():
            cr[...] = jnp.zeros_like(cr)
            ci[...] = jnp.zeros_like(ci)

        uv = u_ref[...]
        ub = uv.astype(BF16)
        bur[...] = _mm(ub, wb_ref[:, 0:GP])
        bui[...] = _mm(ub, wb_ref[:, GP:2 * GP])
        _scan_chunk(bur, bui, xr_ref, xi_ref, cr, ci, tab_ref, tc, False)
        y = _mm(xr_ref[...], wc_ref[0:GP, :]) + _mm(xi_ref[...], wc_ref[GP:2 * GP, :]) + d_ref[...] * uv
        y_ref[...] = y
        z = _gelu(y)
        z2 = z * _sigmoid(_mm(z, wg_ref[...]) + bg_ref[...])
        ys_ref[...] = _mm(z2, wo_ref[...])

    return pl.pallas_call(
        body, name="ssm_fwd", grid=(l // tc,),
        in_specs=[_rows(tc, 512), _resident((512, 2 * GP)), _resident((2 * GP, 512)), _resident((8, 8, GP)),
                  _resident((1, 512)), _resident((512, 512)), _resident((1, 512)), _resident((512, D_MODEL))],
        out_specs=[_rows(tc, GP), _rows(tc, GP), _rows(tc, 512), _rows(tc, D_MODEL)],
        out_shape=[jax.ShapeDtypeStruct((l, GP), F32), jax.ShapeDtypeStruct((l, GP), F32),
                   jax.ShapeDtypeStruct((l, 512), F32), jax.ShapeDtypeStruct((l, D_MODEL), F32)],
        scratch_shapes=[pltpu.VMEM((tc, GP), F32), pltpu.VMEM((tc, GP), F32), pltpu.VMEM((8, GP), F32),
                        pltpu.VMEM((8, GP), F32)],
        compiler_params=_cparams("arbitrary"),
    )(u, wb, wc, tabs, dskip, w_glu, b_glu, w_o_ssm)


def _ssm_bwd(dys, y, u, xr, xi, wb, wc, tabs_rev, dskip, w_glu, b_glu, w_o_ssm, tc):
    l = u.shape[0]
    nc = l // tc

    def body(dys_ref, y_ref, u_ref, xr_ref, xi_ref, wb_ref, wc_ref, tab_ref, d_ref, wg_ref, bg_ref, wo_ref,
             du_ref, a_ref, dy_ref, z_ref, z2_ref, dpre_ref, gb_ref, gd_ref, glr_ref, gli_ref,
             dxr, dxi, ar, ai, cr, ci):
        first = pl.program_id(0) == 0

        @pl.when(first)
        def _():
            cr[...] = jnp.zeros_like(cr)
            ci[...] = jnp.zeros_like(ci)

        yv = y_ref[...]
        uv = u_ref[...]
        dz2 = _mm_nt(dys_ref[...], wo_ref[...])
        z = _gelu(yv)
        s = _sigmoid(_mm(z, wg_ref[...]) + bg_ref[...])
        dpre = dz2 * z * s * (1.0 - s)
        dz = dz2 * s + _mm_nt(dpre, wg_ref[...])
        dy = dz * _gelu_grad(yv)
        z_ref[...] = z.astype(BF16)
        z2_ref[...] = (z * s).astype(BF16)
        dpre_ref[...] = dpre.astype(BF16)
        dy_ref[...] = dy.astype(BF16)
        _accumulate(gb_ref, _colsum(dpre), first)
        _accumulate(gd_ref, _colsum(dy * uv), first)

        dyb = dy.astype(BF16)
        dxr[...] = _mm_nt(dyb, wc_ref[0:GP, :])
        dxi[...] = _mm_nt(dyb, wc_ref[GP:2 * GP, :])
        ar[pl.ds(tc, 8), :] = cr[...]
        ai[pl.ds(tc, 8), :] = ci[...]
        _scan_chunk(dxr, dxi, ar, ai, cr, ci, tab_ref, tc, True)
        a_ref[:, 0:GP] = ar[pl.ds(0, tc), :].astype(BF16)
        a_ref[:, GP:2 * GP] = ai[pl.ds(0, tc), :].astype(BF16)
        du_ref[...] = (dy * d_ref[...] + _mm_nt(a_ref[...], wb_ref[...])).astype(BF16)
        anr = ar[pl.ds(1, tc), :]
        ani = ai[pl.ds(1, tc), :]
        xrv = xr_ref[...]
        xiv = xi_ref[...]
        _accumulate(glr_ref, _colsum(anr * xrv + ani * xiv), first)
        _accumulate(gli_ref, _colsum(ani * xrv - anr * xiv), first)

    rev = lambda w: pl.BlockSpec((tc, w), lambda i: (nc - 1 - i, 0))
    acc = lambda w: pl.BlockSpec((1, w), lambda i: (0, 0))
    return pl.pallas_call(
        body, name="ssm_bwd", grid=(nc,),
        in_specs=[rev(D_MODEL), rev(512), rev(512), rev(GP), rev(GP), _resident((512, 2 * GP)), _resident((2 * GP, 512)),
                  _resident((8, 8, GP)), _resident((1, 512)), _resident((512, 512)), _resident((1, 512)),
                  _resident((512, D_MODEL))],
        out_specs=[rev(512), rev(2 * GP), rev(512), rev(512), rev(512), rev(512), acc(512), acc(512), acc(GP), acc(GP)],
        out_shape=[jax.ShapeDtypeStruct((l, 512), BF16), jax.ShapeDtypeStruct((l, 2 * GP), BF16),
                   jax.ShapeDtypeStruct((l, 512), BF16), jax.ShapeDtypeStruct((l, 512), BF16),
                   jax.ShapeDtypeStruct((l, 512), BF16), jax.ShapeDtypeStruct((l, 512), BF16),
                   jax.ShapeDtypeStruct((1, 512), F32), jax.ShapeDtypeStruct((1, 512), F32),
                   jax.ShapeDtypeStruct((1, GP), F32), jax.ShapeDtypeStruct((1, GP), F32)],
        scratch_shapes=[pltpu.VMEM((tc, GP), F32), pltpu.VMEM((tc, GP), F32), pltpu.VMEM((tc + 8, GP), F32),
                        pltpu.VMEM((tc + 8, GP), F32), pltpu.VMEM((8, GP), F32), pltpu.VMEM((8, GP), F32)],
        compiler_params=_cparams("arbitrary"),
    )(dys, y, u, xr, xi, wb, wc, tabs_rev, dskip, w_glu, b_glu, w_o_ssm)


def _swap_halves(b):
    lane = lax.broadcasted_iota(jnp.int32, b.shape, 1)
    return jnp.where(lane < 32, pltpu.roll(b, 96, 1), pltpu.roll(b, 32, 1))


def _rope_tables(pos_ref, invf_ref, sgn_ref):
    ang = pos_ref[...].astype(F32) * invf_ref[...]
    return jnp.cos(ang), jnp.sin(ang) * sgn_ref[...]


def _mla_pre_fwd(lat, pos, invf, sgn, gqa, gkva, gq, gk, w_qb_p, w_kvb, t):
    l = lat.shape[0]

    def body(lat_ref, pos_ref, invf_ref, sgn_ref, gqa_ref, gkva_ref, gq_ref, gk_ref, wq_ref, wkv_ref, q_ref, k_ref, v_ref):
        cs, sn = _rope_tables(pos_ref, invf_ref, sgn_ref)
        ql = _rms_fwd(lat_ref[:, 0:Q_LORA], gqa_ref[...], Q_LORA)
        ckn = _rms_fwd(lat_ref[:, Q_LORA:Q_LORA + KV_LORA], gkva_ref[...], KV_LORA)
        kpe = lat_ref[:, 640:768]
        q0 = _mm(ql, wq_ref[...])
        kv = _mm(ckn, wkv_ref[...])
        for h in range(N_HEADS):
            q1 = _rms_fwd(q0[:, HEAD_PAD * h:HEAD_PAD * (h + 1)], gq_ref[...], QK_HEAD)
            b = q1[:, 128:256]
            q_ref[h, :, 0:128] = (q1[:, 0:128] * ATT_SCALE).astype(BF16)
            q_ref[h, :, 128:256] = ((b * cs + _swap_halves(b) * sn) * ATT_SCALE).astype(BF16)
            k0 = jnp.concatenate([kv[:, 256 * h:256 * h + 128], kpe], axis=-1)
            k1 = _rms_fwd(k0, gk_ref[...], QK_HEAD)
            b = k1[:, 128:256]
            k_ref[h, :, 0:128] = k1[:, 0:128].astype(BF16)
            k_ref[h, :, 128:256] = (b * cs + _swap_halves(b) * sn).astype(BF16)
            v_ref[h] = kv[:, 256 * h + 128:256 * h + 256].astype(BF16)

    heads = lambda w: pl.BlockSpec((N_HEADS, t, w), lambda i: (0, i, 0))
    return pl.pallas_call(
        body, name="mla_pre_fwd", grid=(l // t,),
        in_specs=[_rows(t, LAT_W), _rows(t, 1), _resident((1, 128)), _resident((1, 128)), _resident((1, Q_LORA)),
                  _resident((1, KV_LORA)), _resident((1, HEAD_PAD)), _resident((1, HEAD_PAD)),
                  _resident((Q_LORA, N_HEADS * HEAD_PAD)), _resident((KV_LORA, N_HEADS * 256))],
        out_specs=[heads(HEAD_PAD), heads(HEAD_PAD), heads(V_HEAD)],
        out_shape=[jax.ShapeDtypeStruct((N_HEADS, l, HEAD_PAD), BF16), jax.ShapeDtypeStruct((N_HEADS, l, HEAD_PAD), BF16),
                   jax.ShapeDtypeStruct((N_HEADS, l, V_HEAD), BF16)],
        compiler_params=_cparams("parallel"),
    )(lat, pos, invf, sgn, gqa, gkva, gq, gk, w_qb_p, w_kvb)


def _mla_pre_bwd(lat, pos, invf, sgn, gqa, gkva, gq, gk, w_qb_p, w_kvb, dq, dk, dv, t):
    l = lat.shape[0]

    def body(lat_ref, pos_ref, invf_ref, sgn_ref, gqa_ref, gkva_ref, gq_ref, gk_ref, wq_ref, wkv_ref, dq_ref, dk_ref, dv_ref,
             dlat_ref, ql_ref, dq0_ref, ckn_ref, dkv_ref, ggqa_ref, ggkva_ref, ggq_ref, ggk_ref):
        first = pl.program_id(0) == 0
        cs, sn = _rope_tables(pos_ref, invf_ref, sgn_ref)
        q_lat = lat_ref[:, 0:Q_LORA]
        c_kv = lat_ref[:, Q_LORA:Q_LORA + KV_LORA]
        kpe = lat_ref[:, 640:768]
        ql = _rms_fwd(q_lat, gqa_ref[...], Q_LORA)
        ckn = _rms_fwd(c_kv, gkva_ref[...], KV_LORA)
        ql_ref[...] = ql.astype(BF16)
        ckn_ref[...] = ckn.astype(BF16)
        q0 = _mm(ql, wq_ref[...])
        kv = _mm(ckn, wkv_ref[...])
        dkpe = jnp.zeros_like(kpe)
        ggq = jnp.zeros((1, HEAD_PAD), F32)
        ggk = jnp.zeros((1, HEAD_PAD), F32)

        def unrope(d):
            b = d[:, 128:256]
            return jnp.concatenate([d[:, 0:128], b * cs + _swap_halves(b * sn)], axis=-1)

        for h in range(N_HEADS):
            dq1 = unrope(dq_ref[h] * ATT_SCALE)
            dq0h, gq_rows = _rms_bwd(q0[:, HEAD_PAD * h:HEAD_PAD * (h + 1)], gq_ref[...], dq1, QK_HEAD)
            ggq = ggq + _colsum(gq_rows)
            dq0_ref[:, HEAD_PAD * h:HEAD_PAD * (h + 1)] = dq0h.astype(BF16)
            k0 = jnp.concatenate([kv[:, 256 * h:256 * h + 128], kpe], axis=-1)
            dk0, gk_rows = _rms_bwd(k0, gk_ref[...], unrope(dk_ref[h]), QK_HEAD)
            ggk = ggk + _colsum(gk_rows)
            dkpe = dkpe + dk0[:, 128:256]
            dkv_ref[:, 256 * h:256 * h + 128] = dk0[:, 0:128].astype(BF16)
            dkv_ref[:, 256 * h + 128:256 * h + 256] = dv_ref[h].astype(BF16)
        dql = _mm_nt(dq0_ref[...], wq_ref[...])
        dckn = _mm_nt(dkv_ref[...], wkv_ref[...])
        dq_lat, gqa_rows = _rms_bwd(q_lat, gqa_ref[...], dql, Q_LORA)
        dc_kv, gkva_rows = _rms_bwd(c_kv, gkva_ref[...], dckn, KV_LORA)
        dlat_ref[:, 0:Q_LORA] = dq_lat.astype(BF16)
        dlat_ref[:, Q_LORA:Q_LORA + KV_LORA] = dc_kv.astype(BF16)
        dlat_ref[:, 640:768] = dkpe.astype(BF16)
        _accumulate(ggqa_ref, _colsum(gqa_rows), first)
        _accumulate(ggkva_ref, _colsum(gkva_rows), first)
        _accumulate(ggq_ref, ggq, first)
        _accumulate(ggk_ref, ggk, first)

    heads = lambda w: pl.BlockSpec((N_HEADS, t, w), lambda i: (0, i, 0))
    acc = lambda w: pl.BlockSpec((1, w), lambda i: (0, 0))
    return pl.pallas_call(
        body, name="mla_pre_bwd", grid=(l // t,),
        in_specs=[_rows(t, LAT_W), _rows(t, 1), _resident((1, 128)), _resident((1, 128)), _resident((1, Q_LORA)),
                  _resident((1, KV_LORA)), _resident((1, HEAD_PAD)), _resident((1, HEAD_PAD)),
                  _resident((Q_LORA, N_HEADS * HEAD_PAD)), _resident((KV_LORA, N_HEADS * 256)),
                  heads(HEAD_PAD), heads(HEAD_PAD), heads(V_HEAD)],
        out_specs=[_rows(t, LAT_W), _rows(t, Q_LORA), _rows(t, N_HEADS * HEAD_PAD), _rows(t, KV_LORA), _rows(t, N_HEADS * 256),
                   acc(Q_LORA), acc(KV_LORA), acc(HEAD_PAD), acc(HEAD_PAD)],
        out_shape=[jax.ShapeDtypeStruct((l, LAT_W), BF16), jax.ShapeDtypeStruct((l, Q_LORA), BF16),
                   jax.ShapeDtypeStruct((l, N_HEADS * HEAD_PAD), BF16), jax.ShapeDtypeStruct((l, KV_LORA), BF16),
                   jax.ShapeDtypeStruct((l, N_HEADS * 256), BF16), jax.ShapeDtypeStruct((1, Q_LORA), F32),
                   jax.ShapeDtypeStruct((1, KV_LORA), F32), jax.ShapeDtypeStruct((1, HEAD_PAD), F32),
                   jax.ShapeDtypeStruct((1, HEAD_PAD), F32)],
        compiler_params=_cparams("arbitrary"),
    )(lat, pos, invf, sgn, gqa, gkva, gq, gk, w_qb_p, w_kvb, dq, dk, dv)


def _causal(s, transposed):
    row = lax.broadcasted_iota(jnp.int32, s.shape, 0)
    col = lax.broadcasted_iota(jnp.int32, s.shape, 1)
    keep = (row <= col) if transposed else (col <= row)
    return jnp.where(keep, s, -jnp.inf)


def _attn_fwd(q, k, v, tq):
    l = q.shape[1]

    def body(q_ref, k_ref, v_ref, o_ref, lse_ref):
        qi = pl.program_id(1)
        qv = q_ref[0]

        def step(kb, carry, masked):
            m, den, acc = carry
            rows = pl.ds(pl.multiple_of(kb * tq, tq), tq)
            s = _mm_nt(qv, k_ref[0, rows, :])
            if masked:
                s = _causal(s, False)
            m_new = jnp.maximum(m, jnp.max(s, axis=-1, keepdims=True))
            alpha = jnp.exp(m - m_new)
            p = jnp.exp(s - m_new)
            den = alpha * den + jnp.sum(p, axis=-1, keepdims=True)
            acc = alpha * acc + _mm(p, v_ref[0, rows, :])
            return m_new, den, acc

        init = (jnp.full((tq, 1), -jnp.inf, F32), jnp.zeros((tq, 1), F32), jnp.zeros((tq, V_HEAD), F32))
        carry = lax.fori_loop(0, qi, lambda kb, c: step(kb, c, False), init)
        m, den, acc = step(qi, carry, True)
        o_ref[...] = acc / den
        lse_ref[0] = m + jnp.log(den)

    return pl.pallas_call(
        body, name="attn_fwd", grid=(N_HEADS, l // tq),
        in_specs=[pl.BlockSpec((1, tq, HEAD_PAD), lambda h, i: (h, i, 0)), pl.BlockSpec((1, l, HEAD_PAD), lambda h, i: (h, 0, 0)),
                  pl.BlockSpec((1, l, V_HEAD), lambda h, i: (h, 0, 0))],
        out_specs=[pl.BlockSpec((tq, V_HEAD), lambda h, i: (i, h)), pl.BlockSpec((1, tq, 1), lambda h, i: (h, i, 0))],
        out_shape=[jax.ShapeDtypeStruct((l, N_HEADS * V_HEAD), F32), jax.ShapeDtypeStruct((N_HEADS, l, 1), F32)],
        compiler_params=_cparams("parallel", "arbitrary"),
    )(q, k, v)


def _attn_bwd_dq(q, k, v, o, do, lse, tq):
    l = q.shape[1]

    def body(q_ref, k_ref, v_ref, o_ref, do_ref, lse_ref, dq_ref, delta_ref):
        qi = pl.program_id(1)
        qv = q_ref[0]
        dov = do_ref[...]
        delta = jnp.sum(dov * o_ref[...], axis=-1, keepdims=True)
        delta_ref[0] = delta
        dob = dov.astype(BF16)
        lse = lse_ref[0]

        def step(kb, dq, masked):
            rows = pl.ds(pl.multiple_of(kb * tq, tq), tq)
            kblk = k_ref[0, rows, :]
            s = _mm_nt(qv, kblk)
            if masked:
                s = _causal(s, False)
            p = jnp.exp(s - lse)
            dp = _mm_nt(dob, v_ref[0, rows, :])
            return dq + _mm(p * (dp - delta), kblk)

        dq = lax.fori_loop(0, qi, lambda kb, c: step(kb, c, False), jnp.zeros((tq, HEAD_PAD), F32))
        dq_ref[0] = step(qi, dq, True)

    return pl.pallas_call(
        body, name="attn_bwd_dq", grid=(N_HEADS, l // tq),
        in_specs=[pl.BlockSpec((1, tq, HEAD_PAD), lambda h, i: (h, i, 0)), pl.BlockSpec((1, l, HEAD_PAD), lambda h, i: (h, 0, 0)),
                  pl.BlockSpec((1, l, V_HEAD), lambda h, i: (h, 0, 0)), pl.BlockSpec((tq, V_HEAD), lambda h, i: (i, h)),
                  pl.BlockSpec((tq, V_HEAD), lambda h, i: (i, h)), pl.BlockSpec((1, tq, 1), lambda h, i: (h, i, 0))],
        out_specs=[pl.BlockSpec((1, tq, HEAD_PAD), lambda h, i: (h, i, 0)), pl.BlockSpec((1, tq, 1), lambda h, i: (h, i, 0))],
        out_shape=[jax.ShapeDtypeStruct((N_HEADS, l, HEAD_PAD), F32), jax.ShapeDtypeStruct((N_HEADS, l, 1), F32)],
        compiler_params=_cparams("parallel", "arbitrary"),
    )(q, k, v, o, do, lse)


def _attn_bwd_dkv(q, k, v, do, lse_t, delta_t, tq):
    l = q.shape[1]
    nq = l // tq

    def body(q_ref, k_ref, v_ref, do_ref, lse_ref, delta_ref, dk_ref, dv_ref):
        ki = pl.program_id(1)
        kblk = k_ref[0]
        vblk = v_ref[0]

        def step(qb, carry, masked):
            dk, dv = carry
            rows = pl.ds(pl.multiple_of(qb * tq, tq), tq)
            qblk = q_ref[0, rows, :]
            dob = do_ref[rows, :].astype(BF16)
            st = _mm_nt(kblk, qblk)
            if masked:
                st = _causal(st, True)
            pt = jnp.exp(st - lse_ref[0, qb])
            dv = dv + _mm(pt, dob)
            dpt = _mm_nt(vblk, dob)
            dk = dk + _mm(pt * (dpt - delta_ref[0, qb]), qblk)
            return dk, dv

        carry = step(ki, (jnp.zeros((tq, HEAD_PAD), F32), jnp.zeros((tq, V_HEAD), F32)), True)
        dk, dv = lax.fori_loop(ki + 1, nq, lambda qb, c: step(qb, c, False), carry)
        dk_ref[0] = dk
        dv_ref[0] = dv

    return pl.pallas_call(
        body, name="attn_bwd_dkv", grid=(N_HEADS, nq),
        in_specs=[pl.BlockSpec((1, l, HEAD_PAD), lambda h, i: (h, 0, 0)), pl.BlockSpec((1, tq, HEAD_PAD), lambda h, i: (h, i, 0)),
                  pl.BlockSpec((1, tq, V_HEAD), lambda h, i: (h, i, 0)), pl.BlockSpec((l, V_HEAD), lambda h, i: (0, h)),
                  pl.BlockSpec((1, nq, 1, tq), lambda h, i: (h, 0, 0, 0)), pl.BlockSpec((1, nq, 1, tq), lambda h, i: (h, 0, 0, 0))],
        out_specs=[pl.BlockSpec((1, tq, HEAD_PAD), lambda h, i: (h, i, 0)), pl.BlockSpec((1, tq, V_HEAD), lambda h, i: (h, i, 0))],
        out_shape=[jax.ShapeDtypeStruct((N_HEADS, l, HEAD_PAD), F32), jax.ShapeDtypeStruct((N_HEADS, l, V_HEAD), F32)],
        compiler_params=_cparams("parallel", "arbitrary"),
    )(q, k, v, do, lse_t, delta_t)


def _merge_fwd(attn, y_ssm, gs, gm, x, w_o_mla, w_out, t):
    l = x.shape[0]

    def body(attn_ref, ys_ref, gs_ref, gm_ref, x_ref, wo_ref, wout_ref, ym_ref, mixed_ref, h_ref):
        y_mla = _mm(attn_ref[...], wo_ref[...])
        ym_ref[...] = y_mla
        mixed = (_sigmoid(gs_ref[...]) * ys_ref[...] + _sigmoid(gm_ref[...]) * y_mla).astype(BF16)
        mixed_ref[...] = mixed
        h_ref[...] = x_ref[...] + _mm(mixed, wout_ref[...])

    r = lambda: _rows(t, D_MODEL)
    return pl.pallas_call(
        body, name="merge_fwd", grid=(l // t,),
        in_specs=[r(), r(), r(), r(), r(), _resident((D_MODEL, D_MODEL)), _resident((D_MODEL, D_MODEL))],
        out_specs=[r(), r(), r()],
        out_shape=[jax.ShapeDtypeStruct((l, D_MODEL), F32), jax.ShapeDtypeStruct((l, D_MODEL), BF16),
                   jax.ShapeDtypeStruct((l, D_MODEL), F32)],
        compiler_params=_cparams("parallel"),
    )(attn, y_ssm, gs, gm, x, w_o_mla, w_out)


def _merge_bwd(dh, y_ssm, y_mla, gs, gm, w_o_mla, w_out, t):
    l = dh.shape[0]

    def body(dh_ref, ys_ref, ym_ref, gs_ref, gm_ref, wo_ref, wout_ref, dys_ref, dym_ref, dgs_ref, dgm_ref, dattn_ref):
        dmixed = _mm_nt(dh_ref[...], wout_ref[...])
        sg = _sigmoid(gs_ref[...])
        sm = _sigmoid(gm_ref[...])
        dys_ref[...] = (dmixed * sg).astype(BF16)
        dgs_ref[...] = (dmixed * ys_ref[...] * sg * (1.0 - sg)).astype(BF16)
        dym = (dmixed * sm).astype(BF16)
        dym_ref[...] = dym
        dgm_ref[...] = (dmixed * ym_ref[...] * sm * (1.0 - sm)).astype(BF16)
        dattn_ref[...] = _mm_nt(dym, wo_ref[...])

    r = lambda: _rows(t, D_MODEL)
    bf = jax.ShapeDtypeStruct((l, D_MODEL), BF16)
    return pl.pallas_call(
        body, name="merge_bwd", grid=(l // t,),
        in_specs=[r(), r(), r(), r(), r(), _resident((D_MODEL, D_MODEL)), _resident((D_MODEL, D_MODEL))],
        out_specs=[r(), r(), r(), r(), r()],
        out_shape=[bf, bf, bf, bf, jax.ShapeDtypeStruct((l, D_MODEL), F32)],
        compiler_params=_cparams("parallel"),
    )(dh, y_ssm, y_mla, gs, gm, w_o_mla, w_out)


def _mlp_fwd_bwd(h, tgt, g2, w_up, w_down, t):
    l = h.shape[0]

    def body(h_ref, tgt_ref, g_ref, wu_ref, wd_ref, dh_ref, hn_ref, da_ref, hid_ref, dout_ref, loss_ref, dg_ref):
        first = pl.program_id(0) == 0
        hv = h_ref[...]
        g = g_ref[...]
        hn = _rms_fwd(hv, g, D_MODEL).astype(BF16)
        hn_ref[...] = hn
        a = _mm(hn, wu_ref[...])
        relu = jnp.maximum(a, 0.0)
        hid = (relu * relu).astype(BF16)
        hid_ref[...] = hid
        err = hv + _mm(hid, wd_ref[...]) - tgt_ref[...]
        _accumulate(loss_ref, jnp.full((8, 128), jnp.sum(err * err) * (0.5 / D_MODEL), F32), first)
        dout = err * (1.0 / D_MODEL)
        doutb = dout.astype(BF16)
        dout_ref[...] = doutb
        da = (_mm_nt(doutb, wd_ref[...]) * (2.0 * relu)).astype(BF16)
        da_ref[...] = da
        dhn = _mm_nt(da, wu_ref[...])
        dx, dg_rows = _rms_bwd(hv, g, dhn, D_MODEL)
        dh_ref[...] = dout + dx
        _accumulate(dg_ref, _colsum(dg_rows), first)

    r = lambda w: _rows(t, w)
    return pl.pallas_call(
        body, name="mlp_fwd_bwd", grid=(l // t,),
        in_specs=[r(D_MODEL), r(D_MODEL), _resident((1, D_MODEL)), _resident((D_MODEL, D_FF)), _resident((D_FF, D_MODEL))],
        out_specs=[r(D_MODEL), r(D_MODEL), r(D_FF), r(D_FF), r(D_MODEL), pl.BlockSpec((8, 128), lambda i: (0, 0)),
                   pl.BlockSpec((1, D_MODEL), lambda i: (0, 0))],
        out_shape=[jax.ShapeDtypeStruct((l, D_MODEL), F32), jax.ShapeDtypeStruct((l, D_MODEL), BF16),
                   jax.ShapeDtypeStruct((l, D_FF), BF16), jax.ShapeDtypeStruct((l, D_FF), BF16),
                   jax.ShapeDtypeStruct((l, D_MODEL), BF16), jax.ShapeDtypeStruct((8, 128), F32),
                   jax.ShapeDtypeStruct((1, D_MODEL), F32)],
        compiler_params=_cparams("arbitrary"),
    )(h, tgt, g2, w_up, w_down)


def _wgrad(a, b, name):
    l, m = a.shape
    n = b.shape[1]
    bm = m if m <= 512 else 512
    bl = min(l, 512)

    def body(a_ref, b_ref, o_ref):
        _accumulate(o_ref, _mm_tn(a_ref[...], b_ref[...]), pl.program_id(1) == 0)

    return pl.pallas_call(
        body, name=name, grid=(m // bm, l // bl),
        in_specs=[pl.BlockSpec((bl, bm), lambda i, j: (j, i)), pl.BlockSpec((bl, n), lambda i, j: (j, 0))],
        out_specs=pl.BlockSpec((bm, n), lambda i, j: (i, 0)),
        out_shape=jax.ShapeDtypeStruct((m, n), F32),
        compiler_params=_cparams("parallel", "arbitrary"),
    )(a, b)


def _adamw(w, g, m, v, name):
    r, c = w.shape
    br = r
    for cand in (256, 128, 64, 32, 16, 8):
        if r % cand == 0 and r > cand:
            br = cand
            break

    def body(w_ref, g_ref, m_ref, v_ref, d_ref, nm_ref, nv_ref):
        gv = g_ref[...]
        nm = ADAM_B1 * m_ref[...] + (1.0 - ADAM_B1) * gv
        nv = ADAM_B2 * v_ref[...] + (1.0 - ADAM_B2) * (gv * gv)
        m_hat = nm / (1.0 - ADAM_B1 ** ADAM_STEP)
        v_hat = nv / (1.0 - ADAM_B2 ** ADAM_STEP)
        d_ref[...] = -ADAM_LR * (m_hat / (jnp.sqrt(v_hat) + ADAM_EPS) + ADAM_WD * w_ref[...])
        nm_ref[...] = nm
        nv_ref[...] = nv

    spec = lambda: pl.BlockSpec((br, c), lambda i: (i, 0))
    shp = jax.ShapeDtypeStruct((r, c), F32)
    return pl.pallas_call(
        body, name=name, grid=(r // br,), in_specs=[spec(), spec(), spec(), spec()], out_specs=[spec(), spec(), spec()],
        out_shape=[shp, shp, shp], compiler_params=_cparams("parallel"),
    )(w, g, m, v)


def _place():
    return lax.axis_index("x"), lax.axis_index("y"), lax.axis_index("c")


def _other_chips(x, y):
    return [(1 - x, y), (x, 1 - y), (1 - x, 1 - y)]


ANY = pl.BlockSpec(memory_space=pl.ANY)


def _gather_weights(packed):
    rows = packed.shape[0]
    half = rows // 2

    def body(in_ref, out_ref, send_sems, recv_sems, local_sem):
        x, y, c = _place()
        chips = _other_chips(x, y)

        def part(px, py, pc):
            return out_ref.at[2 * px + py, pl.ds(pl.multiple_of(pc * half, 16), half), :]

        def copy(k, src, dst, to):
            return pltpu.make_async_remote_copy(src_ref=src, dst_ref=dst, send_sem=send_sems.at[k], recv_sem=recv_sems.at[k],
                                                device_id=to, device_id_type=MESH)

        mine = pltpu.make_async_copy(in_ref, out_ref.at[2 * x + y], local_sem)
        mine.start()
        my_half = in_ref.at[pl.ds(pl.multiple_of(c * half, 16), half), :]
        first = [copy(j, my_half, part(x, y, c), (*chip, c)) for j, chip in enumerate(chips)]
        for cp in first:
            cp.start()
        passed = [copy(3 + j, part(*chip, c), part(*chip, c), (x, y, 1 - c)) for j, chip in enumerate(chips)]
        for j, chip in enumerate(chips):
            copy(j, part(*chip, c), part(*chip, c), (x, y, c)).wait_recv()
            passed[j].start()
        for j, chip in enumerate(chips):
            copy(3 + j, part(*chip, 1 - c), part(*chip, 1 - c), (x, y, c)).wait_recv()
        for cp in first + passed:
            cp.wait_send()
        mine.wait()

    return pl.pallas_call(
        body, name="gather_weights", in_specs=[ANY], out_specs=ANY,
        out_shape=jax.ShapeDtypeStruct((4, rows, packed.shape[1]), packed.dtype),
        scratch_shapes=[pltpu.SemaphoreType.DMA((6,)), pltpu.SemaphoreType.DMA((6,)), pltpu.SemaphoreType.DMA],
    )(packed)


def _swap_with_sibling(mine):
    def body(in_ref, out_ref, send_sem, recv_sem):
        x, y, c = _place()
        cp = pltpu.make_async_remote_copy(src_ref=in_ref, dst_ref=out_ref, send_sem=send_sem, recv_sem=recv_sem,
                                          device_id=(x, y, 1 - c), device_id_type=MESH)
        cp.start()
        cp.wait()

    return pl.pallas_call(
        body, name="swap_with_sibling", in_specs=[ANY], out_specs=ANY,
        out_shape=jax.ShapeDtypeStruct(mine.shape, mine.dtype),
        scratch_shapes=[pltpu.SemaphoreType.DMA, pltpu.SemaphoreType.DMA],
    )(mine)


def _scatter_to_chips(parts):
    def body(in_ref, out_ref, send_sems, recv_sems):
        x, y, c = _place()
        copies = [pltpu.make_async_remote_copy(src_ref=in_ref.at[2 * px + py], dst_ref=out_ref.at[j], send_sem=send_sems.at[j],
                                               recv_sem=recv_sems.at[j], device_id=(px, py, c), device_id_type=MESH)
                  for j, (px, py) in enumerate(_other_chips(x, y))]
        for cp in copies:
            cp.start()
        for cp in copies:
            cp.wait()

    return pl.pallas_call(
        body, name="scatter_to_chips", in_specs=[ANY], out_specs=ANY,
        out_shape=jax.ShapeDtypeStruct((3,) + parts.shape[1:], parts.dtype),
        scratch_shapes=[pltpu.SemaphoreType.DMA((3,)), pltpu.SemaphoreType.DMA((3,))],
    )(parts)


def _join_halves(mine):
    half = mine.shape[0]

    def body(in_ref, out_ref, send_sem, recv_sem, local_sem):
        x, y, c = _place()
        own = out_ref.at[pl.ds(pl.multiple_of(c * half, 8), half), :]
        keep = pltpu.make_async_copy(in_ref, own, local_sem)
        keep.start()
        cp = pltpu.make_async_remote_copy(src_ref=in_ref, dst_ref=own, send_sem=send_sem, recv_sem=recv_sem,
                                          device_id=(x, y, 1 - c), device_id_type=MESH)
        cp.start()
        other = out_ref.at[pl.ds(pl.multiple_of((1 - c) * half, 8), half), :]
        pltpu.make_async_remote_copy(src_ref=other, dst_ref=other, send_sem=send_sem, recv_sem=recv_sem,
                                     device_id=(x, y, 1 - c), device_id_type=MESH).wait_recv()
        cp.wait_send()
        keep.wait()

    return pl.pallas_call(
        body, name="join_halves", in_specs=[ANY], out_specs=ANY,
        out_shape=jax.ShapeDtypeStruct((2 * half, mine.shape[1]), mine.dtype),
        scratch_shapes=[pltpu.SemaphoreType.DMA, pltpu.SemaphoreType.DMA, pltpu.SemaphoreType.DMA],
    )(mine)


def _add_pair(a, b):
    n, h, w = a.shape
    bh = 384 if h % 384 == 0 else h

    def body(a_ref, b_ref, s_ref, sb_ref):
        s = a_ref[...] + b_ref[...]
        s_ref[...] = s
        sb_ref[...] = s.astype(BF16)

    spec = lambda: pl.BlockSpec((1, bh, w), lambda j, i: (j, i, 0))
    return pl.pallas_call(
        body, name="add_pair", grid=(n, h // bh), in_specs=[spec(), spec()], out_specs=[spec(), spec()],
        out_shape=[jax.ShapeDtypeStruct(a.shape, F32), jax.ShapeDtypeStruct(a.shape, BF16)],
        compiler_params=_cparams("parallel", "parallel"),
    )(a, b)


def _add_received(own, got):
    h, w = own.shape
    bh = 384 if h % 384 == 0 else h

    def body(own_ref, got_ref, o_ref):
        o_ref[...] = ((own_ref[...] + got_ref[0].astype(F32)) + got_ref[1].astype(F32)) + got_ref[2].astype(F32)

    return pl.pallas_call(
        body, name="add_received", grid=(h // bh,),
        in_specs=[pl.BlockSpec((bh, w), lambda i: (i, 0)), pl.BlockSpec((3, bh, w), lambda i: (0, i, 0))],
        out_specs=pl.BlockSpec((bh, w), lambda i: (i, 0)), out_shape=jax.ShapeDtypeStruct((h, w), F32),
        compiler_params=_cparams("parallel"),
    )(own, got)


def _all_sum_small(mine):
    rows, w = mine.shape

    def body(in_ref, out_ref, slots, send_sems, recv_sems):
        x, y, c = _place()
        me = 4 * x + 2 * y + c
        slots[me] = in_ref[...]
        copies = []
        for k in range(1, 8):
            peer = (1 - x if k & 4 else x, 1 - y if k & 2 else y, 1 - c if k & 1 else c)
            copies.append(pltpu.make_async_remote_copy(src_ref=in_ref, dst_ref=slots.at[me], send_sem=send_sems.at[k - 1],
                                                       recv_sem=recv_sems.at[k - 1], device_id=peer, device_id_type=MESH))
        for cp in copies:
            cp.start()
        for cp in copies:
            cp.wait()
        total = slots[0]
        for d in range(1, 8):
            total = total + slots[d]
        out_ref[...] = total

    return pl.pallas_call(
        body, name="all_sum_small", out_shape=jax.ShapeDtypeStruct((rows, w), F32),
        in_specs=[pl.BlockSpec(memory_space=pltpu.VMEM)], out_specs=pl.BlockSpec(memory_space=pltpu.VMEM),
        scratch_shapes=[pltpu.VMEM((8, rows, w), F32), pltpu.SemaphoreType.DMA((7,)), pltpu.SemaphoreType.DMA((7,))],
        compiler_params=pltpu.CompilerParams(vmem_limit_bytes=VMEM_LIMIT_V7X),
    )(mine)


def _shard_shape(rows, cols, cut):
    return (rows, cols // 4) if cut == "col" else (rows // 4, cols)


def _pack_shards(shards):
    return jnp.concatenate([shards[name].reshape(-1, 1024) for name, *_ in BIG_WEIGHTS], axis=0)


def _unpack_gathered(g):
    out, off = {}, 0
    for name, rows, cols, cut in BIG_WEIGHTS:
        rs, cs = _shard_shape(rows, cols, cut)
        n = rs * cs // 1024
        seg = g[:, off:off + n, :].reshape(4, rs, cs)
        out[name] = seg.reshape(rows, cols) if cut == "row" else jnp.transpose(seg, (1, 0, 2)).reshape(rows, cols)
        off += n
    return out


def _pack_grads(grads):
    segs = []
    for name, rows, cols, cut in BIG_WEIGHTS:
        rs, cs = _shard_shape(rows, cols, cut)
        gw = grads[name]
        seg = gw.reshape(4, rs, cs) if cut == "row" else jnp.transpose(gw.reshape(rows, 4, cs), (1, 0, 2))
        segs.append(seg.reshape(4, rs * cs // 1024, 1024))
    return jnp.concatenate(segs, axis=1)


def _unpack_shard(r):
    out, off = {}, 0
    for name, rows, cols, cut in BIG_WEIGHTS:
        rs, cs = _shard_shape(rows, cols, cut)
        n = rs * cs // 1024
        out[name] = r[off:off + n].reshape(rs, cs)
        off += n
    return out


def _small_rows(shape):
    return -(-int(np.prod(shape)) // 1024)


def _pack_small(vals):
    segs = []
    for name, shape in SMALL_WEIGHTS:
        flat = vals[name].reshape(-1)
        rows = _small_rows(shape)
        segs.append(jnp.pad(flat, (0, rows * 1024 - flat.shape[0])).reshape(rows, 1024))
    total = sum(s.shape[0] for s in segs)
    segs.append(jnp.zeros((-total % 8, 1024), F32))
    return jnp.concatenate(segs, axis=0)


def _unpack_small(packed):
    out, off = {}, 0
    for name, shape in SMALL_WEIGHTS:
        rows = _small_rows(shape)
        out[name] = packed[off:off + rows].reshape(-1)[:int(np.prod(shape))].reshape(shape)
        off += rows
    return out


def _pad_w_in(w):
    return jnp.concatenate([w[:, :1216], jnp.zeros((w.shape[0], 64), w.dtype), w[:, 1216:]], axis=1)


def _unpad_w_in(g):
    return jnp.concatenate([g[:, :1216], g[:, 1280:]], axis=1)


def _pad_heads(w):
    r = w.shape[0]
    return jnp.pad(w.reshape(r, N_HEADS, QK_HEAD), ((0, 0), (0, 0), (0, HEAD_PAD - QK_HEAD))).reshape(r, N_HEADS * HEAD_PAD)


def _unpad_heads(g):
    r = g.shape[0]
    return g.reshape(r, N_HEADS, HEAD_PAD)[:, :, :QK_HEAD].reshape(r, N_HEADS * QK_HEAD)


def _local_step(x, positions, tgt, big, small):
    l = x.shape[0]
    t = min(l, 512)
    t_mlp = min(l, 256)
    tq = min(l, 256)
    tc = min(l, 256)
    row = lambda v: v.reshape(1, -1).astype(F32)

    w_in_p = _pad_w_in(big["w_in"])
    w_qb_p = _pad_heads(big["w_q_b"])
    g1, g2 = row(small["norm_mix"]), row(small["norm_mlp"])
    gqa, gkva = row(small["q_a_norm"]), row(small["kv_a_norm"])
    gq = jnp.pad(row(small["q_norm"]), ((0, 0), (0, HEAD_PAD - QK_HEAD)))
    gk = jnp.pad(row(small["k_norm"]), ((0, 0), (0, HEAD_PAD - QK_HEAD)))
    half = QK_ROPE // 2
    inv_freq = ROPE_THETA ** (-jnp.arange(half, dtype=F32) / half)
    invf = jnp.concatenate([inv_freq, inv_freq, jnp.zeros((64,), F32)]).reshape(1, 128)
    sgn = jnp.concatenate([-jnp.ones((half,), F32), jnp.ones((half,), F32), jnp.zeros((64,), F32)]).reshape(1, 128)
    pos = positions.reshape(l, 1)

    a_re, a_im = small["ssm_a_re"], small["ssm_a_im"]
    log_dt = small["ssm_log_dt"].reshape(SSM_GROUPS, 1)
    to_cgp = lambda b: jnp.transpose(b, (2, 0, 1)).reshape(SSM_GROUP_CH * SSM_GROUPS, SSM_STATE)
    from_cgp = lambda b: jnp.transpose(b.reshape(SSM_GROUP_CH, SSM_GROUPS, SSM_STATE), (1, 2, 0))
    bt_re, bt_im = to_cgp(small["ssm_b_re"]), to_cgp(small["ssm_b_im"])
    pow_r, pow_i, bb_re, bb_im = _ssm_param_fwd(a_re, a_im, log_dt, bt_re, bt_im)
    to_gcp = lambda b: jnp.transpose(b.reshape(SSM_GROUP_CH, SSM_GROUPS, SSM_STATE), (1, 0, 2))
    wb = jnp.concatenate([_block_diag(to_gcp(bb_re)), _block_diag(to_gcp(bb_im))], axis=1).astype(BF16)
    wc = jnp.concatenate([_block_diag(small["ssm_c_re"]).T, -_block_diag(small["ssm_c_im"]).T], axis=0).astype(BF16)
    dskip = row(small["ssm_d"])
    b_glu = row(small["b_glu"])

    u, lat, gs, gm = _in_proj_fwd(x, g1, w_in_p, t)
    xr, xi, y, y_ssm = _ssm_fwd(u, wb, wc, _scan_tables(pow_r, pow_i, False), dskip, big["w_glu"], b_glu, big["w_o_ssm"], tc)
    q, k, v = _mla_pre_fwd(lat, pos, invf, sgn, gqa, gkva, gq, gk, w_qb_p, big["w_kv_b"], t)
    attn, lse = _attn_fwd(q, k, v, tq)
    y_mla, mixed, h = _merge_fwd(attn, y_ssm, gs, gm, x, big["w_o_mla"], big["w_out"], t)
    dh, hn, da, hid, dout, loss_blk, g_norm_mlp = _mlp_fwd_bwd(h, tgt, g2, big["w_up"], big["w_down"], t_mlp)

    grads = {"w_down": _wgrad(hid, dout, "wgrad_down"), "w_up": _wgrad(hn, da, "wgrad_up")}
    dys, dym, dgs, dgm, dattn = _merge_bwd(dh, y_ssm, y_mla, gs, gm, big["w_o_mla"], big["w_out"], t)
    grads["w_out"] = _wgrad(mixed, dh, "wgrad_out")
    grads["w_o_mla"] = _wgrad(attn, dym, "wgrad_o_mla")

    dq, delta = _attn_bwd_dq(q, k, v, attn, dattn, lse, tq)
    lanes = lambda a: a.reshape(N_HEADS, l // tq, 1, tq)
    dk, dv = _attn_bwd_dkv(q, k, v, dattn, lanes(lse), lanes(delta), tq)
    d_lat, ql, dq0, ckn, dkv, g_qa, g_kva, g_q, g_k = _mla_pre_bwd(lat, pos, invf, sgn, gqa, gkva, gq, gk, w_qb_p,
                                                                    big["w_kv_b"], dq, dk, dv, t)
    grads["w_q_b"] = _unpad_heads(_wgrad(ql, dq0, "wgrad_q_b"))
    grads["w_kv_b"] = _wgrad(ckn, dkv, "wgrad_kv_b")

    d_u, adj, dy, z, z2, dpre, g_b_glu, g_d, g_lr, g_li = _ssm_bwd(
        dys, y, u, xr, xi, wb, wc, _scan_tables(pow_r, pow_i, True), dskip, big["w_glu"], b_glu, big["w_o_ssm"], tc)
    grads["w_o_ssm"] = _wgrad(z2, dys, "wgrad_o_ssm")
    grads["w_glu"] = _wgrad(z, dpre, "wgrad_glu")
    g_wb = _wgrad(u, adj, "wgrad_ssm_b")
    g_wcr = _wgrad(xr, dy, "wgrad_ssm_c_re")
    g_wci = _wgrad(xi, dy, "wgrad_ssm_c_im")
    to_cgp_rows = lambda m: jnp.transpose(_block_diag_take(m), (1, 0, 2)).reshape(SSM_GROUP_CH * SSM_GROUPS, SSM_STATE)
    g_ar, g_ai, g_ldt, g_btr, g_bti = _ssm_param_bwd(
        a_re, a_im, log_dt, bt_re, bt_im, g_lr.reshape(SSM_GROUPS, SSM_STATE), g_li.reshape(SSM_GROUPS, SSM_STATE),
        to_cgp_rows(g_wb[:, :GP]), to_cgp_rows(g_wb[:, GP:]))

    grad_x, xn, dproj, g_norm_mix = _in_proj_bwd(x, g1, w_in_p, d_u, d_lat, dgs, dgm, dh, t)
    grads["w_in"] = _unpad_w_in(_wgrad(xn, dproj, "wgrad_in"))

    g_small = {
        "norm_mix": g_norm_mix.reshape(-1), "norm_mlp": g_norm_mlp.reshape(-1), "q_a_norm": g_qa.reshape(-1),
        "kv_a_norm": g_kva.reshape(-1), "q_norm": g_q.reshape(-1)[:QK_HEAD], "k_norm": g_k.reshape(-1)[:QK_HEAD],
        "ssm_a_re": g_ar, "ssm_a_im": g_ai, "ssm_log_dt": g_ldt.reshape(-1),
        "ssm_b_re": from_cgp(g_btr), "ssm_b_im": from_cgp(g_bti),
        "ssm_c_re": _block_diag_take(g_wcr.T), "ssm_c_im": -_block_diag_take(g_wci.T),
        "ssm_d": g_d.reshape(SSM_GROUPS, SSM_GROUP_CH), "b_glu": g_b_glu.reshape(-1),
    }
    return loss_blk[0, 0], grad_x, grads, g_small


def kernel(x, positions, norm_mix, w_in, q_a_norm, kv_a_norm, w_q_b, w_kv_b, q_norm, k_norm, w_o_mla, ssm_a_re, ssm_a_im, ssm_log_dt, ssm_b_re, ssm_b_im, ssm_c_re, ssm_c_im, ssm_d, w_glu, b_glu, w_o_ssm, w_out, norm_mlp, w_up, w_down, loss_target, m_norm_mix, m_w_in, m_q_a_norm, m_kv_a_norm, m_w_q_b, m_w_kv_b, m_q_norm, m_k_norm, m_w_o_mla, m_ssm_a_re, m_ssm_a_im, m_ssm_log_dt, m_ssm_b_re, m_ssm_b_im, m_ssm_c_re, m_ssm_c_im, m_ssm_d, m_w_glu, m_b_glu, m_w_o_ssm, m_w_out, m_norm_mlp, m_w_up, m_w_down, v_norm_mix, v_w_in, v_q_a_norm, v_kv_a_norm, v_w_q_b, v_w_kv_b, v_q_norm, v_k_norm, v_w_o_mla, v_ssm_a_re, v_ssm_a_im, v_ssm_log_dt, v_ssm_b_re, v_ssm_b_im, v_ssm_c_re, v_ssm_c_im, v_ssm_d, v_w_glu, v_b_glu, v_w_o_ssm, v_w_out, v_norm_mlp, v_w_up, v_w_down):
    args = dict(locals())
    w = {n: args[n][0] for n in WEIGHT_ORDER}
    m = {n: args["m_" + n][0] for n in WEIGHT_ORDER}
    v = {n: args["v_" + n][0] for n in WEIGHT_ORDER}
    big_names = [n for n, *_ in BIG_WEIGHTS]
    small_names = [n for n, _ in SMALL_WEIGHTS]

    gathered = _gather_weights(_pack_shards({n: w[n] for n in big_names}).astype(BF16))
    big = _unpack_gathered(gathered)
    small = {n: w[n] for n in small_names}

    loss_local, grad_x, grads, g_small = _local_step(x[0], positions[0], loss_target[0], big, small)
    loss = lax.psum(loss_local, ("x", "y", "c"))

    c = lax.axis_index("c")
    chip = 2 * lax.axis_index("x") + lax.axis_index("y")
    packed = _pack_grads(grads)
    half = packed.shape[1] // 2
    keep = lax.dynamic_slice_in_dim(packed, c * half, half, axis=1)
    give = lax.dynamic_slice_in_dim(packed, (1 - c) * half, half, axis=1)
    pair_f32, pair_bf16 = _add_pair(keep, _swap_with_sibling(give))
    own = lax.dynamic_index_in_dim(pair_f32, chip, axis=0, keepdims=False)
    reduced = _join_halves(_add_received(own, _scatter_to_chips(pair_bf16)))
    g_big = _unpack_shard(reduced)

    g_small_sum = _unpack_small(_all_sum_small(_pack_small(g_small)))

    grad_w, delta_w, new_m, new_v = {}, {}, {}, {}
    for n in big_names:
        grad_w[n] = g_big[n]
        delta_w[n], new_m[n], new_v[n] = _adamw(w[n], g_big[n], m[n], v[n], "adamw_" + n)
    ps = _pack_small(g_small_sum)
    d_s, m_s, v_s = _adamw(_pack_small(small), ps, _pack_small({n: m[n] for n in small_names}),
                           _pack_small({n: v[n] for n in small_names}), "adamw_small")
    d_s, m_s, v_s = _unpack_small(d_s), _unpack_small(m_s), _unpack_small(v_s)
    for n in small_names:
        grad_w[n], delta_w[n], new_m[n], new_v[n] = g_small_sum[n], d_s[n], m_s[n], v_s[n]

    lead = lambda d: [d[n][None] for n in WEIGHT_ORDER]
    return (loss, grad_x[None], *lead(grad_w), *lead(delta_w), *lead(new_m), *lead(new_v))
```

```python
import functools
import math

import jax
import jax.numpy as jnp
import numpy as np
from jax import lax
from jax.experimental import pallas as pl
from jax.experimental.pallas import tpu as pltpu

F32 = jnp.float32
BF16 = jnp.bfloat16

D_MODEL = 1024
SSM_GROUPS = 32
SSM_GROUP_CH = 16
SSM_WIDTH = 512
SSM_STATE = 64
GP = SSM_GROUPS * SSM_STATE
N_HEADS = 8
QK_NOPE = 128
QK_ROPE = 64
QK_HEAD = 192
HEAD_PAD = 256
V_HEAD = 128
Q_LORA = 384
KV_LORA = 256
LAT_W = 768
D_IN = 3264
D_IN_PAD = 3328
D_FF = 4096
ROPE_THETA = 10000.0
EPS = 1e-6
ATT_SCALE = QK_HEAD ** -0.5

ADAM_LR = 0.001
ADAM_B1 = 0.9
ADAM_B2 = 0.999
ADAM_EPS = 1e-08
ADAM_WD = 0.01
ADAM_STEP = 10

VMEM_LIMIT_V7X = 56 * 1024 * 1024
MESH = pl.DeviceIdType.MESH

BIG_WEIGHTS = (
    ("w_in", 1024, 3264, "col"),
    ("w_q_b", 384, 1536, "col"),
    ("w_kv_b", 256, 2048, "col"),
    ("w_o_mla", 1024, 1024, "row"),
    ("w_glu", 512, 512, "row"),
    ("w_o_ssm", 512, 1024, "col"),
    ("w_out", 1024, 1024, "row"),
    ("w_up", 1024, 4096, "col"),
    ("w_down", 4096, 1024, "row"),
)
SMALL_WEIGHTS = (
    ("norm_mix", (1024,)), ("q_a_norm", (384,)), ("kv_a_norm", (256,)), ("q_norm", (192,)), ("k_norm", (192,)),
    ("ssm_a_re", (32, 64)), ("ssm_a_im", (32, 64)), ("ssm_log_dt", (32,)),
    ("ssm_b_re", (32, 64, 16)), ("ssm_b_im", (32, 64, 16)), ("ssm_c_re", (32, 16, 64)), ("ssm_c_im", (32, 16, 64)),
    ("ssm_d", (32, 16)), ("b_glu", (512,)), ("norm_mlp", (1024,)),
)
WEIGHT_ORDER = ('norm_mix', 'w_in', 'q_a_norm', 'kv_a_norm', 'w_q_b', 'w_kv_b', 'q_norm', 'k_norm', 'w_o_mla', 'ssm_a_re',
                'ssm_a_im', 'ssm_log_dt', 'ssm_b_re', 'ssm_b_im', 'ssm_c_re', 'ssm_c_im', 'ssm_d', 'w_glu', 'b_glu',
                'w_o_ssm', 'w_out', 'norm_mlp', 'w_up', 'w_down')


def _cparams(*sem):
    return pltpu.CompilerParams(dimension_semantics=sem if sem else None, vmem_limit_bytes=VMEM_LIMIT_V7X)


def _resident(shape):
    zeros = (0,) * len(shape)
    return pl.BlockSpec(shape, lambda *_: zeros, pipeline_mode=pl.Buffered(1))


def _rows(t, width):
    return pl.BlockSpec((t, width), lambda i: (i, 0))


def _mm(a, b):
    return jnp.dot(a.astype(BF16), b.astype(BF16), preferred_element_type=F32)


def _mm_nt(a, b):
    return lax.dot_general(a.astype(BF16), b.astype(BF16), (((1,), (1,)), ((), ())), preferred_element_type=F32)


def _mm_tn(a, b):
    return lax.dot_general(a.astype(BF16), b.astype(BF16), (((0,), (0,)), ((), ())), preferred_element_type=F32)


def _rms_fwd(x, g, n):
    r = lax.rsqrt(jnp.sum(x * x, axis=-1, keepdims=True) * (1.0 / n) + EPS)
    return x * r * g


def _rms_bwd(x, g, dy, n):
    r = lax.rsqrt(jnp.sum(x * x, axis=-1, keepdims=True) * (1.0 / n) + EPS)
    xh = x * r
    dxh = dy * g
    dx = r * (dxh - xh * (jnp.sum(dxh * xh, axis=-1, keepdims=True) * (1.0 / n)))
    return dx, dy * xh


def _colsum(a):
    return jnp.sum(a, axis=0, keepdims=True)


def _accumulate(ref, value, first):
    @pl.when(first)
    def _():
        ref[...] = value

    @pl.when(jnp.logical_not(first))
    def _():
        ref[...] += value


def _sigmoid(a):
    return 1.0 / (1.0 + jnp.exp(-a))


GELU_C = math.sqrt(2.0 / math.pi)
GELU_A = 0.044715


def _gelu(y):
    return 0.5 * y * (1.0 + jnp.tanh(GELU_C * (y + GELU_A * y * y * y)))


def _gelu_grad(y):
    t = jnp.tanh(GELU_C * (y + GELU_A * y * y * y))
    return 0.5 * (1.0 + t) + 0.5 * y * (1.0 - t * t) * GELU_C * (1.0 + 3.0 * GELU_A * y * y)


def _in_proj_fwd(x, g1, w_in_p, t):
    l = x.shape[0]

    def body(x_ref, g_ref, w_ref, u_ref, lat_ref, gs_ref, gm_ref):
        xn = _rms_fwd(x_ref[...], g_ref[...], D_MODEL).astype(BF16)
        u_ref[...] = _mm(xn, w_ref[:, 0:512])
        lat_ref[...] = _mm(xn, w_ref[:, 512:1280])
        gs_ref[...] = _mm(xn, w_ref[:, 1280:2304])
        gm_ref[...] = _mm(xn, w_ref[:, 2304:3328])

    return pl.pallas_call(
        body, name="in_proj_fwd", grid=(l // t,),
        in_specs=[_rows(t, D_MODEL), _resident((1, D_MODEL)), _resident((D_MODEL, D_IN_PAD))],
        out_specs=[_rows(t, 512), _rows(t, LAT_W), _rows(t, D_MODEL), _rows(t, D_MODEL)],
        out_shape=[jax.ShapeDtypeStruct((l, 512), F32), jax.ShapeDtypeStruct((l, LAT_W), F32),
                   jax.ShapeDtypeStruct((l, D_MODEL), F32), jax.ShapeDtypeStruct((l, D_MODEL), F32)],
        compiler_params=_cparams("parallel"),
    )(x, g1, w_in_p)


def _in_proj_bwd(x, g1, w_in_p, d_u, d_lat, d_gs, d_gm, dh, t):
    l = x.shape[0]

    def body(x_ref, g_ref, w_ref, du_ref, dlat_ref, dgs_ref, dgm_ref, dh_ref, gx_ref, xn_ref, dproj_ref, dg_ref):
        xv = x_ref[...]
        g = g_ref[...]
        xn_ref[...] = _rms_fwd(xv, g, D_MODEL).astype(BF16)
        dproj_ref[:, 0:512] = du_ref[...]
        dproj_ref[:, 512:1280] = dlat_ref[...]
        dproj_ref[:, 1280:2304] = dgs_ref[...]
        dproj_ref[:, 2304:3328] = dgm_ref[...]
        dxn = _mm_nt(dproj_ref[...], w_ref[...])
        dx, dg_rows = _rms_bwd(xv, g, dxn, D_MODEL)
        gx_ref[...] = dh_ref[...] + dx
        _accumulate(dg_ref, _colsum(dg_rows), pl.program_id(0) == 0)

    return pl.pallas_call(
        body, name="in_proj_bwd", grid=(l // t,),
        in_specs=[_rows(t, D_MODEL), _resident((1, D_MODEL)), _resident((D_MODEL, D_IN_PAD)), _rows(t, 512),
                  _rows(t, LAT_W), _rows(t, D_MODEL), _rows(t, D_MODEL), _rows(t, D_MODEL)],
        out_specs=[_rows(t, D_MODEL), _rows(t, D_MODEL), _rows(t, D_IN_PAD), pl.BlockSpec((1, D_MODEL), lambda i: (0, 0))],
        out_shape=[jax.ShapeDtypeStruct((l, D_MODEL), F32), jax.ShapeDtypeStruct((l, D_MODEL), BF16),
                   jax.ShapeDtypeStruct((l, D_IN_PAD), BF16), jax.ShapeDtypeStruct((1, D_MODEL), F32)],
        compiler_params=_cparams("arbitrary"),
    )(x, g1, w_in_p, d_u, d_lat, d_gs, d_gm, dh)


def _ssm_param_fn(a_re, a_im, log_dt, bt_re, bt_im):
    dt = jnp.exp(log_dt)
    er = jnp.exp(a_re * dt)
    lr = er * jnp.cos(a_im * dt)
    li = er * jnp.sin(a_im * dt)
    den = a_re * a_re + a_im * a_im
    nr = lr - 1.0
    kr = (nr * a_re + li * a_im) / den
    ki = (li * a_re - nr * a_im) / den
    krt = jnp.concatenate([kr] * SSM_GROUP_CH, axis=0)
    kit = jnp.concatenate([ki] * SSM_GROUP_CH, axis=0)
    return lr, li, krt * bt_re - kit * bt_im, krt * bt_im + kit * bt_re


def _ssm_param_fwd(a_re, a_im, log_dt, bt_re, bt_im):
    def body(ar_ref, ai_ref, ldt_ref, br_ref, bi_ref, pr_ref, pi_ref, bbr_ref, bbi_ref):
        lr, li, bbr, bbi = _ssm_param_fn(ar_ref[...], ai_ref[...], ldt_ref[...], br_ref[...], bi_ref[...])
        bbr_ref[...] = bbr
        bbi_ref[...] = bbi
        qr, qi = lr, li
        for j in range(8):
            pr_ref[j] = qr
            pi_ref[j] = qi
            qr, qi = qr * lr - qi * li, qr * li + qi * lr

    g, p = SSM_GROUPS, SSM_STATE
    return pl.pallas_call(
        body, name="ssm_param_fwd",
        out_shape=[jax.ShapeDtypeStruct((8, g, p), F32), jax.ShapeDtypeStruct((8, g, p), F32),
                   jax.ShapeDtypeStruct((SSM_GROUP_CH * g, p), F32), jax.ShapeDtypeStruct((SSM_GROUP_CH * g, p), F32)],
    )(a_re, a_im, log_dt, bt_re, bt_im)


def _ssm_param_bwd(a_re, a_im, log_dt, bt_re, bt_im, g_lr, g_li, g_bbr, g_bbi):
    def body(ar_ref, ai_ref, ldt_ref, br_ref, bi_ref, glr_ref, gli_ref, gbr_ref, gbi_ref, o_ar, o_ai, o_ldt, o_br, o_bi):
        _, vjp = jax.vjp(_ssm_param_fn, ar_ref[...], ai_ref[...], ldt_ref[...], br_ref[...], bi_ref[...])
        d_ar, d_ai, d_ldt, d_br, d_bi = vjp((glr_ref[...], gli_ref[...], gbr_ref[...], gbi_ref[...]))
        o_ar[...] = d_ar
        o_ai[...] = d_ai
        o_ldt[...] = d_ldt
        o_br[...] = d_br
        o_bi[...] = d_bi

    g, p = SSM_GROUPS, SSM_STATE
    return pl.pallas_call(
        body, name="ssm_param_bwd",
        out_shape=[jax.ShapeDtypeStruct((g, p), F32), jax.ShapeDtypeStruct((g, p), F32), jax.ShapeDtypeStruct((g, 1), F32),
                   jax.ShapeDtypeStruct((SSM_GROUP_CH * g, p), F32), jax.ShapeDtypeStruct((SSM_GROUP_CH * g, p), F32)],
    )(a_re, a_im, log_dt, bt_re, bt_im, g_lr, g_li, g_bbr, g_bbi)


def _block_diag(t_gcp):
    eye = jnp.eye(SSM_GROUPS, dtype=t_gcp.dtype)
    return (t_gcp[:, :, None, :] * eye[:, None, :, None]).reshape(SSM_WIDTH, GP)


def _block_diag_take(m):
    t = m.reshape(SSM_GROUPS, SSM_GROUP_CH, SSM_GROUPS, SSM_STATE)
    return jnp.transpose(jnp.diagonal(t, axis1=0, axis2=2), (2, 0, 1))


def _scan_tables(pr, pi, reverse):
    pr = pr.reshape(8, GP)
    pi = pi.reshape(8, GP)
    if reverse:
        pi = -pi
    row = jnp.arange(8)[:, None]
    tabs = []
    for k in (1, 2, 4):
        keep = (row < 8 - k) if reverse else (row >= k)
        tabs.append(jnp.where(keep, pr[k - 1][None, :], 0.0))
        tabs.append(jnp.where(keep, pi[k - 1][None, :], 0.0))
    if reverse:
        tabs += [pr[::-1], pi[::-1]]
    else:
        tabs += [pr, pi]
    return jnp.stack(tabs).astype(F32)


SCAN_STRIP = 512


def _scan_chunk(inr_ref, ini_ref, outr_ref, outi_ref, cr_ref, ci_ref, tab_ref, tc, reverse):
    n_blocks = tc // 8

    def block(j, _):
        i = (n_blocks - 1 - j) if reverse else j
        rows = pl.ds(pl.multiple_of(i * 8, 8), 8)
        for s in range(GP // SCAN_STRIP):
            sl = pl.ds(s * SCAN_STRIP, SCAN_STRIP)
            xr = inr_ref[rows, sl]
            xi = ini_ref[rows, sl]
            for n, k in enumerate((1, 2, 4)):
                shift = (8 - k) if reverse else k
                sr = pltpu.roll(xr, shift, 0)
                si = pltpu.roll(xi, shift, 0)
                mr = tab_ref[2 * n, :, sl]
                mi = tab_ref[2 * n + 1, :, sl]
                xr, xi = xr + mr * sr - mi * si, xi + mr * si + mi * sr
            qr = tab_ref[6, :, sl]
            qi = tab_ref[7, :, sl]
            cr = cr_ref[:, sl]
            ci = ci_ref[:, sl]
            xr, xi = xr + qr * cr - qi * ci, xi + qr * ci + qi * cr
            outr_ref[rows, sl] = xr
            outi_ref[rows, sl] = xi
            edge = 0 if reverse else 7
            cr_ref[:, sl] = jnp.broadcast_to(xr[edge:edge + 1, :], (8, SCAN_STRIP))
            ci_ref[:, sl] = jnp.broadcast_to(xi[edge:edge + 1, :], (8, SCAN_STRIP))
        return 0

    lax.fori_loop(0, n_blocks, block, 0)


def _ssm_fwd(u, wb, wc, tabs, dskip, w_glu, b_glu, w_o_ssm, tc):
    l = u.shape[0]

    def body(u_ref, wb_ref, wc_ref, tab_ref, d_ref, wg_ref, bg_ref, wo_ref, xr_ref, xi_ref, y_ref, ys_ref,
             bur, bui, cr, ci):
        @pl.when(pl.program_id(0) == 0)
        def _():
            cr[...] = jnp.zeros_like(cr)
            ci[...] = jnp.zeros_like(ci)

        uv = u_ref[...]
        ub = uv.astype(BF16)
        bur[...] = _mm(ub, wb_ref[:, 0:GP])
        bui[...] = _mm(ub, wb_ref[:, GP:2 * GP])
        _scan_chunk(bur, bui, xr_ref, xi_ref, cr, ci, tab_ref, tc, False)
        y = _mm(xr_ref[...], wc_ref[0:GP, :]) + _mm(xi_ref[...], wc_ref[GP:2 * GP, :]) + d_ref[...] * uv
        y_ref[...] = y
        z = _gelu(y)
        z2 = z * _sigmoid(_mm(z, wg_ref[...]) + bg_ref[...])
        ys_ref[...] = _mm(z2, wo_ref[...])

    return pl.pallas_call(
        body, name="ssm_fwd", grid=(l // tc,),
        in_specs=[_rows(tc, 512), _resident((512, 2 * GP)), _resident((2 * GP, 512)), _resident((8, 8, GP)),
                  _resident((1, 512)), _resident((512, 512)), _resident((1, 512)), _resident((512, D_MODEL))],
        out_specs=[_rows(tc, GP), _rows(tc, GP), _rows(tc, 512), _rows(tc, D_MODEL)],
        out_shape=[jax.ShapeDtypeStruct((l, GP), F32), jax.ShapeDtypeStruct((l, GP), F32),
                   jax.ShapeDtypeStruct((l, 512), F32), jax.ShapeDtypeStruct((l, D_MODEL), F32)],
        scratch_shapes=[pltpu.VMEM((tc, GP), F32), pltpu.VMEM((tc, GP), F32), pltpu.VMEM((8, GP), F32),
                        pltpu.VMEM((8, GP), F32)],
        compiler_params=_cparams("arbitrary"),
    )(u, wb, wc, tabs, dskip, w_glu, b_glu, w_o_ssm)


def _ssm_bwd(dys, y, u, xr, xi, wb, wc, tabs_rev, dskip, w_glu, b_glu, w_o_ssm, tc):
    l = u.shape[0]
    nc = l // tc

    def body(dys_ref, y_ref, u_ref, xr_ref, xi_ref, wb_ref, wc_ref, tab_ref, d_ref, wg_ref, bg_ref, wo_ref,
             du_ref, a_ref, dy_ref, z_ref, z2_ref, dpre_ref, gb_ref, gd_ref, glr_ref, gli_ref,
             dxr, dxi, ar, ai, cr, ci):
        first = pl.program_id(0) == 0

        @pl.when(first)
        def _():
            cr[...] = jnp.zeros_like(cr)
            ci[...] = jnp.zeros_like(ci)

        yv = y_ref[...]
        uv = u_ref[...]
        dz2 = _mm_nt(dys_ref[...], wo_ref[...])
        z = _gelu(yv)
        s = _sigmoid(_mm(z, wg_ref[...]) + bg_ref[...])
        dpre = dz2 * z * s * (1.0 - s)
        dz = dz2 * s + _mm_nt(dpre, wg_ref[...])
        dy = dz * _gelu_grad(yv)
        z_ref[...] = z.astype(BF16)
        z2_ref[...] = (z * s).astype(BF16)
        dpre_ref[...] = dpre.astype(BF16)
        dy_ref[...] = dy.astype(BF16)
        _accumulate(gb_ref, _colsum(dpre), first)
        _accumulate(gd_ref, _colsum(dy * uv), first)

        dyb = dy.astype(BF16)
        dxr[...] = _mm_nt(dyb, wc_ref[0:GP, :])
        dxi[...] = _mm_nt(dyb, wc_ref[GP:2 * GP, :])
        ar[pl.ds(tc, 8), :] = cr[...]
        ai[pl.ds(tc, 8), :] = ci[...]
        _scan_chunk(dxr, dxi, ar, ai, cr, ci, tab_ref, tc, True)
        a_ref[:, 0:GP] = ar[pl.ds(0, tc), :].astype(BF16)
        a_ref[:, GP:2 * GP] = ai[pl.ds(0, tc), :].astype(BF16)
        du_ref[...] = (dy * d_ref[...] + _mm_nt(a_ref[...], wb_ref[...])).astype(BF16)
        anr = ar[pl.ds(1, tc), :]
        ani = ai[pl.ds(1, tc), :]
        xrv = xr_ref[...]
        xiv = xi_ref[...]
        _accumulate(glr_ref, _colsum(anr * xrv + ani * xiv), first)
        _accumulate(gli_ref, _colsum(ani * xrv - anr * xiv), first)

    rev = lambda w: pl.BlockSpec((tc, w), lambda i: (nc - 1 - i, 0))
    acc = lambda w: pl.BlockSpec((1, w), lambda i: (0, 0))
    return pl.pallas_call(
        body, name="ssm_bwd", grid=(nc,),
        in_specs=[rev(D_MODEL), rev(512), rev(512), rev(GP), rev(GP), _resident((512, 2 * GP)), _resident((2 * GP, 512)),
                  _resident((8, 8, GP)), _resident((1, 512)), _resident((512, 512)), _resident((1, 512)),
                  _resident((512, D_MODEL))],
        out_specs=[rev(512), rev(2 * GP), rev(512), rev(512), rev(512), rev(512), acc(512), acc(512), acc(GP), acc(GP)],
        out_shape=[jax.ShapeDtypeStruct((l, 512), BF16), jax.ShapeDtypeStruct((l, 2 * GP), BF16),
                   jax.ShapeDtypeStruct((l, 512), BF16), jax.ShapeDtypeStruct((l, 512), BF16),
                   jax.ShapeDtypeStruct((l, 512), BF16), jax.ShapeDtypeStruct((l, 512), BF16),
                   jax.ShapeDtypeStruct((1, 512), F32), jax.ShapeDtypeStruct((1, 512), F32),
                   jax.ShapeDtypeStruct((1, GP), F32), jax.ShapeDtypeStruct((1, GP), F32)],
        scratch_shapes=[pltpu.VMEM((tc, GP), F32), pltpu.VMEM((tc, GP), F32), pltpu.VMEM((tc + 8, GP), F32),
                        pltpu.VMEM((tc + 8, GP), F32), pltpu.VMEM((8, GP), F32), pltpu.VMEM((8, GP), F32)],
        compiler_params=_cparams("arbitrary"),
    )(dys, y, u, xr, xi, wb, wc, tabs_rev, dskip, w_glu, b_glu, w_o_ssm)


def _swap_halves(b):
    lane = lax.broadcasted_iota(jnp.int32, b.shape, 1)
    return jnp.where(lane < 32, pltpu.roll(b, 96, 1), pltpu.roll(b, 32, 1))


def _rope_tables(pos_ref, invf_ref, sgn_ref):
    ang = pos_ref[...].astype(F32) * invf_ref[...]
    return jnp.cos(ang), jnp.sin(ang) * sgn_ref[...]


def _mla_pre_fwd(lat, pos, invf, sgn, gqa, gkva, gq, gk, w_qb_p, w_kvb, t):
    l = lat.shape[0]

    def body(lat_ref, pos_ref, invf_ref, sgn_ref, gqa_ref, gkva_ref, gq_ref, gk_ref, wq_ref, wkv_ref, q_ref, k_ref, v_ref):
        cs, sn = _rope_tables(pos_ref, invf_ref, sgn_ref)
        ql = _rms_fwd(lat_ref[:, 0:Q_LORA], gqa_ref[...], Q_LORA)
        ckn = _rms_fwd(lat_ref[:, Q_LORA:Q_LORA + KV_LORA], gkva_ref[...], KV_LORA)
        kpe = lat_ref[:, 640:768]
        q0 = _mm(ql, wq_ref[...])
        kv = _mm(ckn, wkv_ref[...])
        for h in range(N_HEADS):
            q1 = _rms_fwd(q0[:, HEAD_PAD * h:HEAD_PAD * (h + 1)], gq_ref[...], QK_HEAD)
            b = q1[:, 128:256]
            q_ref[h, :, 0:128] = (q1[:, 0:128] * ATT_SCALE).astype(BF16)
            q_ref[h, :, 128:256] = ((b * cs + _swap_halves(b) * sn) * ATT_SCALE).astype(BF16)
            k0 = jnp.concatenate([kv[:, 256 * h:256 * h + 128], kpe], axis=-1)
            k1 = _rms_fwd(k0, gk_ref[...], QK_HEAD)
            b = k1[:, 128:256]
            k_ref[h, :, 0:128] = k1[:, 0:128].astype(BF16)
            k_ref[h, :, 128:256] = (b * cs + _swap_halves(b) * sn).astype(BF16)
            v_ref[h] = kv[:, 256 * h + 128:256 * h + 256].astype(BF16)

    heads = lambda w: pl.BlockSpec((N_HEADS, t, w), lambda i: (0, i, 0))
    return pl.pallas_call(
        body, name="mla_pre_fwd", grid=(l // t,),
        in_specs=[_rows(t, LAT_W), _rows(t, 1), _resident((1, 128)), _resident((1, 128)), _resident((1, Q_LORA)),
                  _resident((1, KV_LORA)), _resident((1, HEAD_PAD)), _resident((1, HEAD_PAD)),
                  _resident((Q_LORA, N_HEADS * HEAD_PAD)), _resident((KV_LORA, N_HEADS * 256))],
        out_specs=[heads(HEAD_PAD), heads(HEAD_PAD), heads(V_HEAD)],
        out_shape=[jax.ShapeDtypeStruct((N_HEADS, l, HEAD_PAD), BF16), jax.ShapeDtypeStruct((N_HEADS, l, HEAD_PAD), BF16),
                   jax.ShapeDtypeStruct((N_HEADS, l, V_HEAD), BF16)],
        compiler_params=_cparams("parallel"),
    )(lat, pos, invf, sgn, gqa, gkva, gq, gk, w_qb_p, w_kvb)


def _mla_pre_bwd(lat, pos, invf, sgn, gqa, gkva, gq, gk, w_qb_p, w_kvb, dq, dk, dv, t):
    l = lat.shape[0]

    def body(lat_ref, pos_ref, invf_ref, sgn_ref, gqa_ref, gkva_ref, gq_ref, gk_ref, wq_ref, wkv_ref, dq_ref, dk_ref, dv_ref,
             dlat_ref, ql_ref, dq0_ref, ckn_ref, dkv_ref, ggqa_ref, ggkva_ref, ggq_ref, ggk_ref):
        first = pl.program_id(0) == 0
        cs, sn = _rope_tables(pos_ref, invf_ref, sgn_ref)
        q_lat = lat_ref[:, 0:Q_LORA]
        c_kv = lat_ref[:, Q_LORA:Q_LORA + KV_LORA]
        kpe = lat_ref[:, 640:768]
        ql = _rms_fwd(q_lat, gqa_ref[...], Q_LORA)
        ckn = _rms_fwd(c_kv, gkva_ref[...], KV_LORA)
        ql_ref[...] = ql.astype(BF16)
        ckn_ref[...] = ckn.astype(BF16)
        q0 = _mm(ql, wq_ref[...])
        kv = _mm(ckn, wkv_ref[...])
        dkpe = jnp.zeros_like(kpe)
        ggq = jnp.zeros((1, HEAD_PAD), F32)
        ggk = jnp.zeros((1, HEAD_PAD), F32)

        def unrope(d):
            b = d[:, 128:256]
            return jnp.concatenate([d[:, 0:128], b * cs + _swap_halves(b * sn)], axis=-1)

        for h in range(N_HEADS):
            dq1 = unrope(dq_ref[h] * ATT_SCALE)
            dq0h, gq_rows = _rms_bwd(q0[:, HEAD_PAD * h:HEAD_PAD * (h + 1)], gq_ref[...], dq1, QK_HEAD)
            ggq = ggq + _colsum(gq_rows)
            dq0_ref[:, HEAD_PAD * h:HEAD_PAD * (h + 1)] = dq0h.astype(BF16)
            k0 = jnp.concatenate([kv[:, 256 * h:256 * h + 128], kpe], axis=-1)
            dk0, gk_rows = _rms_bwd(k0, gk_ref[...], unrope(dk_ref[h]), QK_HEAD)
            ggk = ggk + _colsum(gk_rows)
            dkpe = dkpe + dk0[:, 128:256]
            dkv_ref[:, 256 * h:256 * h + 128] = dk0[:, 0:128].astype(BF16)
            dkv_ref[:, 256 * h + 128:256 * h + 256] = dv_ref[h].astype(BF16)
        dql = _mm_nt(dq0_ref[...], wq_ref[...])
        dckn = _mm_nt(dkv_ref[...], wkv_ref[...])
        dq_lat, gqa_rows = _rms_bwd(q_lat, gqa_ref[...], dql, Q_LORA)
        dc_kv, gkva_rows = _rms_bwd(c_kv, gkva_ref[...], dckn, KV_LORA)
        dlat_ref[:, 0:Q_LORA] = dq_lat.astype(BF16)
        dlat_ref[:, Q_LORA:Q_LORA + KV_LORA] = dc_kv.astype(BF16)
        dlat_ref[:, 640:768] = dkpe.astype(BF16)
        _accumulate(ggqa_ref, _colsum(gqa_rows), first)
        _accumulate(ggkva_ref, _colsum(gkva_rows), first)
        _accumulate(ggq_ref, ggq, first)
        _accumulate(ggk_ref, ggk, first)

    heads = lambda w: pl.BlockSpec((N_HEADS, t, w), lambda i: (0, i, 0))
    acc = lambda w: pl.BlockSpec((1, w), lambda i: (0, 0))
    return pl.pallas_call(
        body, name="mla_pre_bwd", grid=(l // t,),
        in_specs=[_rows(t, LAT_W), _rows(t, 1), _resident((1, 128)), _resident((1, 128)), _resident((1, Q_LORA)),
                  _resident((1, KV_LORA)), _resident((1, HEAD_PAD)), _resident((1, HEAD_PAD)),
                  _resident((Q_LORA, N_HEADS * HEAD_PAD)), _resident((KV_LORA, N_HEADS * 256)),
                  heads(HEAD_PAD), heads(HEAD_PAD), heads(V_HEAD)],
        out_specs=[_rows(t, LAT_W), _rows(t, Q_LORA), _rows(t, N_HEADS * HEAD_PAD), _rows(t, KV_LORA), _rows(t, N_HEADS * 256),
                   acc(Q_LORA), acc(KV_LORA), acc(HEAD_PAD), acc(HEAD_PAD)],
        out_shape=[jax.ShapeDtypeStruct((l, LAT_W), BF16), jax.ShapeDtypeStruct((l, Q_LORA), BF16),
                   jax.ShapeDtypeStruct((l, N_HEADS * HEAD_PAD), BF16), jax.ShapeDtypeStruct((l, KV_LORA), BF16),
                   jax.ShapeDtypeStruct((l, N_HEADS * 256), BF16), jax.ShapeDtypeStruct((1, Q_LORA), F32),
                   jax.ShapeDtypeStruct((1, KV_LORA), F32), jax.ShapeDtypeStruct((1, HEAD_PAD), F32),
                   jax.ShapeDtypeStruct((1, HEAD_PAD), F32)],
        compiler_params=_cparams("arbitrary"),
    )(lat, pos, invf, sgn, gqa, gkva, gq, gk, w_qb_p, w_kvb, dq, dk, dv)


def _causal(s, transposed):
    row = lax.broadcasted_iota(jnp.int32, s.shape, 0)
    col = lax.broadcasted_iota(jnp.int32, s.shape, 1)
    keep = (row <= col) if transposed else (col <= row)
    return jnp.where(keep, s, -jnp.inf)


def _attn_fwd(q, k, v, tq):
    l = q.shape[1]

    def body(q_ref, k_ref, v_ref, o_ref, lse_ref):
        qi = pl.program_id(1)
        qv = q_ref[0]

        def step(kb, carry, masked):
            m, den, acc = carry
            rows = pl.ds(pl.multiple_of(kb * tq, tq), tq)
            s = _mm_nt(qv, k_ref[0, rows, :])
            if masked:
                s = _causal(s, False)
            m_new = jnp.maximum(m, jnp.max(s, axis=-1, keepdims=True))
            alpha = jnp.exp(m - m_new)
            p = jnp.exp(s - m_new)
            den = alpha * den + jnp.sum(p, axis=-1, keepdims=True)
            acc = alpha * acc + _mm(p, v_ref[0, rows, :])
            return m_new, den, acc

        init = (jnp.full((tq, 1), -jnp.inf, F32), jnp.zeros((tq, 1), F32), jnp.zeros((tq, V_HEAD), F32))
        carry = lax.fori_loop(0, qi, lambda kb, c: step(kb, c, False), init)
        m, den, acc = step(qi, carry, True)
        o_ref[...] = acc / den
        lse_ref[0] = m + jnp.log(den)

    return pl.pallas_call(
        body, name="attn_fwd", grid=(N_HEADS, l // tq),
        in_specs=[pl.BlockSpec((1, tq, HEAD_PAD), lambda h, i: (h, i, 0)), pl.BlockSpec((1, l, HEAD_PAD), lambda h, i: (h, 0, 0)),
                  pl.BlockSpec((1, l, V_HEAD), lambda h, i: (h, 0, 0))],
        out_specs=[pl.BlockSpec((tq, V_HEAD), lambda h, i: (i, h)), pl.BlockSpec((1, tq, 1), lambda h, i: (h, i, 0))],
        out_shape=[jax.ShapeDtypeStruct((l, N_HEADS * V_HEAD), F32), jax.ShapeDtypeStruct((N_HEADS, l, 1), F32)],
        compiler_params=_cparams("parallel", "arbitrary"),
    )(q, k, v)


def _attn_bwd_dq(q, k, v, o, do, lse, tq):
    l = q.shape[1]

    def body(q_ref, k_ref, v_ref, o_ref, do_ref, lse_ref, dq_ref, delta_ref):
        qi = pl.program_id(1)
        qv = q_ref[0]
        dov = do_ref[...]
        delta = jnp.sum(dov * o_ref[...], axis=-1, keepdims=True)
        delta_ref[0] = delta
        dob = dov.astype(BF16)
        lse = lse_ref[0]

        def step(kb, dq, masked):
            rows = pl.ds(pl.multiple_of(kb * tq, tq), tq)
            kblk = k_ref[0, rows, :]
            s = _mm_nt(qv, kblk)
            if masked:
                s = _causal(s, False)
            p = jnp.exp(s - lse)
            dp = _mm_nt(dob, v_ref[0, rows, :])
            return dq + _mm(p * (dp - delta), kblk)

        dq = lax.fori_loop(0, qi, lambda kb, c: step(kb, c, False), jnp.zeros((tq, HEAD_PAD), F32))
        dq_ref[0] = step(qi, dq, True)

    return pl.pallas_call(
        body, name="attn_bwd_dq", grid=(N_HEADS, l // tq),
        in_specs=[pl.BlockSpec((1, tq, HEAD_PAD), lambda h, i: (h, i, 0)), pl.BlockSpec((1, l, HEAD_PAD), lambda h, i: (h, 0, 0)),
                  pl.BlockSpec((1, l, V_HEAD), lambda h, i: (h, 0, 0)), pl.BlockSpec((tq, V_HEAD), lambda h, i: (i, h)),
                  pl.BlockSpec((tq, V_HEAD), lambda h, i: (i, h)), pl.BlockSpec((1, tq, 1), lambda h, i: (h, i, 0))],
        out_specs=[pl.BlockSpec((1, tq, HEAD_PAD), lambda h, i: (h, i, 0)), pl.BlockSpec((1, tq, 1), lambda h, i: (h, i, 0))],
        out_shape=[jax.ShapeDtypeStruct((N_HEADS, l, HEAD_PAD), F32), jax.ShapeDtypeStruct((N_HEADS, l, 1), F32)],
        compiler_params=_cparams("parallel", "arbitrary"),
    )(q, k, v, o, do, lse)


def _attn_bwd_dkv(q, k, v, do, lse_t, delta_t, tq):
    l = q.shape[1]
    nq = l // tq

    def body(q_ref, k_ref, v_ref, do_ref, lse_ref, delta_ref, dk_ref, dv_ref):
        ki = pl.program_id(1)
        kblk = k_ref[0]
        vblk = v_ref[0]

        def step(qb, carry, masked):
            dk, dv = carry
            rows = pl.ds(pl.multiple_of(qb * tq, tq), tq)
            qblk = q_ref[0, rows, :]
            dob = do_ref[rows, :].astype(BF16)
            st = _mm_nt(kblk, qblk)
            if masked:
                st = _causal(st, True)
            pt = jnp.exp(st - lse_ref[0, qb])
            dv = dv + _mm(pt, dob)
            dpt = _mm_nt(vblk, dob)
            dk = dk + _mm(pt * (dpt - delta_ref[0, qb]), qblk)
            return dk, dv

        carry = step(ki, (jnp.zeros((tq, HEAD_PAD), F32), jnp.zeros((tq, V_HEAD), F32)), True)
        dk, dv = lax.fori_loop(ki + 1, nq, lambda qb, c: step(qb, c, False), carry)
        dk_ref[0] = dk
        dv_ref[0] = dv

    return pl.pallas_call(
        body, name="attn_bwd_dkv", grid=(N_HEADS, nq),
        in_specs=[pl.BlockSpec((1, l, HEAD_PAD), lambda h, i: (h, 0, 0)), pl.BlockSpec((1, tq, HEAD_PAD), lambda h, i: (h, i, 0)),
                  pl.BlockSpec((1, tq, V_HEAD), lambda h, i: (h, i, 0)), pl.BlockSpec((l, V_HEAD), lambda h, i: (0, h)),
                  pl.BlockSpec((1, nq, 1, tq), lambda h, i: (h, 0, 0, 0)), pl.BlockSpec((1, nq, 1, tq), lambda h, i: (h, 0, 0, 0))],
        out_specs=[pl.BlockSpec((1, tq, HEAD_PAD), lambda h, i: (h, i, 0)), pl.BlockSpec((1, tq, V_HEAD), lambda h, i: (h, i, 0))],
        out_shape=[jax.ShapeDtypeStruct((N_HEADS, l, HEAD_PAD), F32), jax.ShapeDtypeStruct((N_HEADS, l, V_HEAD), F32)],
        compiler_params=_cparams("parallel", "arbitrary"),
    )(q, k, v, do, lse_t, delta_t)


def _merge_fwd(attn, y_ssm, gs, gm, x, w_o_mla, w_out, t):
    l = x.shape[0]

    def body(attn_ref, ys_ref, gs_ref, gm_ref, x_ref, wo_ref, wout_ref, ym_ref, mixed_ref, h_ref):
        y_mla = _mm(attn_ref[...], wo_ref[...])
        ym_ref[...] = y_mla
        mixed = (_sigmoid(gs_ref[...]) * ys_ref[...] + _sigmoid(gm_ref[...]) * y_mla).astype(BF16)
        mixed_ref[...] = mixed
        h_ref[...] = x_ref[...] + _mm(mixed, wout_ref[...])

    r = lambda: _rows(t, D_MODEL)
    return pl.pallas_call(
        body, name="merge_fwd", grid=(l // t,),
        in_specs=[r(), r(), r(), r(), r(), _resident((D_MODEL, D_MODEL)), _resident((D_MODEL, D_MODEL))],
        out_specs=[r(), r(), r()],
        out_shape=[jax.ShapeDtypeStruct((l, D_MODEL), F32), jax.ShapeDtypeStruct((l, D_MODEL), BF16),
                   jax.ShapeDtypeStruct((l, D_MODEL), F32)],
        compiler_params=_cparams("parallel"),
    )(attn, y_ssm, gs, gm, x, w_o_mla, w_out)


def _merge_bwd(dh, y_ssm, y_mla, gs, gm, w_o_mla, w_out, t):
    l = dh.shape[0]

    def body(dh_ref, ys_ref, ym_ref, gs_ref, gm_ref, wo_ref, wout_ref, dys_ref, dym_ref, dgs_ref, dgm_ref, dattn_ref):
        dmixed = _mm_nt(dh_ref[...], wout_ref[...])
        sg = _sigmoid(gs_ref[...])
        sm = _sigmoid(gm_ref[...])
        dys_ref[...] = (dmixed * sg).astype(BF16)
        dgs_ref[...] = (dmixed * ys_ref[...] * sg * (1.0 - sg)).astype(BF16)
        dym = (dmixed * sm).astype(BF16)
        dym_ref[...] = dym
        dgm_ref[...] = (dmixed * ym_ref[...] * sm * (1.0 - sm)).astype(BF16)
        dattn_ref[...] = _mm_nt(dym, wo_ref[...])

    r = lambda: _rows(t, D_MODEL)
    bf = jax.ShapeDtypeStruct((l, D_MODEL), BF16)
    return pl.pallas_call(
        body, name="merge_bwd", grid=(l // t,),
        in_specs=[r(), r(), r(), r(), r(), _resident((D_MODEL, D_MODEL)), _resident((D_MODEL, D_MODEL))],
        out_specs=[r(), r(), r(), r(), r()],
        out_shape=[bf, bf, bf, bf, jax.ShapeDtypeStruct((l, D_MODEL), F32)],
        compiler_params=_cparams("parallel"),
    )(dh, y_ssm, y_mla, gs, gm, w_o_mla, w_out)


def _mlp_fwd_bwd(h, tgt, g2, w_up, w_down, t):
    l = h.shape[0]

    def body(h_ref, tgt_ref, g_ref, wu_ref, wd_ref, dh_ref, hn_ref, da_ref, hid_ref, dout_ref, loss_ref, dg_ref):
        first = pl.program_id(0) == 0
        hv = h_ref[...]
        g = g_ref[...]
        hn = _rms_fwd(hv, g, D_MODEL).astype(BF16)
        hn_ref[...] = hn
        a = _mm(hn, wu_ref[...])
        relu = jnp.maximum(a, 0.0)
        hid = (relu * relu).astype(BF16)
        hid_ref[...] = hid
        err = hv + _mm(hid, wd_ref[...]) - tgt_ref[...]
        _accumulate(loss_ref, jnp.full((8, 128), jnp.sum(err * err) * (0.5 / D_MODEL), F32), first)
        dout = err * (1.0 / D_MODEL)
        doutb = dout.astype(BF16)
        dout_ref[...] = doutb
        da = (_mm_nt(doutb, wd_ref[...]) * (2.0 * relu)).astype(BF16)
        da_ref[...] = da
        dhn = _mm_nt(da, wu_ref[...])
        dx, dg_rows = _rms_bwd(hv, g, dhn, D_MODEL)
        dh_ref[...] = dout + dx
        _accumulate(dg_ref, _colsum(dg_rows), first)

    r = lambda w: _rows(t, w)
    return pl.pallas_call(
        body, name="mlp_fwd_bwd", grid=(l // t,),
        in_specs=[r(D_MODEL), r(D_MODEL), _resident((1, D_MODEL)), _resident((D_MODEL, D_FF)), _resident((D_FF, D_MODEL))],
        out_specs=[r(D_MODEL), r(D_MODEL), r(D_FF), r(D_FF), r(D_MODEL), pl.BlockSpec((8, 128), lambda i: (0, 0)),
                   pl.BlockSpec((1, D_MODEL), lambda i: (0, 0))],
        out_shape=[jax.ShapeDtypeStruct((l, D_MODEL), F32), jax.ShapeDtypeStruct((l, D_MODEL), BF16),
                   jax.ShapeDtypeStruct((l, D_FF), BF16), jax.ShapeDtypeStruct((l, D_FF), BF16),
                   jax.ShapeDtypeStruct((l, D_MODEL), BF16), jax.ShapeDtypeStruct((8, 128), F32),
                   jax.ShapeDtypeStruct((1, D_MODEL), F32)],
        compiler_params=_cparams("arbitrary"),
    )(h, tgt, g2, w_up, w_down)


def _wgrad(a, b, name):
    l, m = a.shape
    n = b.shape[1]
    bm = m if m <= 512 else 512
    bl = min(l, 512)

    def body(a_ref, b_ref, o_ref):
        _accumulate(o_ref, _mm_tn(a_ref[...], b_ref[...]), pl.program_id(1) == 0)

    return pl.pallas_call(
        body, name=name, grid=(m // bm, l // bl),
        in_specs=[pl.BlockSpec((bl, bm), lambda i, j: (j, i)), pl.BlockSpec((bl, n), lambda i, j: (j, 0))],
        out_specs=pl.BlockSpec((bm, n), lambda i, j: (i, 0)),
        out_shape=jax.ShapeDtypeStruct((m, n), F32),
        compiler_params=_cparams("parallel", "arbitrary"),
    )(a, b)


def _adamw(w, g, m, v, name):
    r, c = w.shape
    br = r
    for cand in (256, 128, 64, 32, 16, 8):
        if r % cand == 0 and r > cand:
            br = cand
            break

    def body(w_ref, g_ref, m_ref, v_ref, d_ref, nm_ref, nv_ref):
        gv = g_ref[...]
        nm = ADAM_B1 * m_ref[...] + (1.0 - ADAM_B1) * gv
        nv = ADAM_B2 * v_ref[...] + (1.0 - ADAM_B2) * (gv * gv)
        m_hat = nm / (1.0 - ADAM_B1 ** ADAM_STEP)
        v_hat = nv / (1.0 - ADAM_B2 ** ADAM_STEP)
        d_ref[...] = -ADAM_LR * (m_hat / (jnp.sqrt(v_hat) + ADAM_EPS) + ADAM_WD * w_ref[...])
        nm_ref[...] = nm
        nv_ref[...] = nv

    spec = lambda: pl.BlockSpec((br, c), lambda i: (i, 0))
    shp = jax.ShapeDtypeStruct((r, c), F32)
    return pl.pallas_call(
        body, name=name, grid=(r // br,), in_specs=[spec(), spec(), spec(), spec()], out_specs=[spec(), spec(), spec()],
        out_shape=[shp, shp, shp], compiler_params=_cparams("parallel"),
    )(w, g, m, v)


def _place():
    return lax.axis_index("x"), lax.axis_index("y"), lax.axis_index("c")


def _other_chips(x, y):
    return [(1 - x, y), (x, 1 - y), (1 - x, 1 - y)]


ANY = pl.BlockSpec(memory_space=pl.ANY)


def _gather_weights(packed):
    rows = packed.shape[0]
    half = rows // 2

    def body(in_ref, out_ref, send_sems, recv_sems):
        x, y, c = _place()
        chips = _other_chips(x, y)

        def part(px, py, pc):
            return out_ref.at[2 * px + py, pl.ds(pl.multiple_of(pc * half, 16), half), :]

        def copy(k, src, dst, to):
            return pltpu.make_async_remote_copy(src_ref=src, dst_ref=dst, send_sem=send_sems.at[k], recv_sem=recv_sems.at[k],
                                                device_id=to, device_id_type=MESH)

        my_half = in_ref.at[pl.ds(pl.multiple_of(c * half, 16), half), :]
        first = [copy(j, my_half, part(x, y, c), (*chip, c)) for j, chip in enumerate(chips)]
        for cp in first:
            cp.start()
        passed = [copy(3 + j, part(*chip, c), part(*chip, c), (x, y, 1 - c)) for j, chip in enumerate(chips)]
        for j, chip in enumerate(chips):
            copy(j, part(*chip, c), part(*chip, c), (x, y, c)).wait_recv()
            passed[j].start()
        for j, chip in enumerate(chips):
            copy(3 + j, part(*chip, 1 - c), part(*chip, 1 - c), (x, y, c)).wait_recv()
        for cp in first + passed:
            cp.wait_send()

    others = pl.pallas_call(
        body, name="gather_weights", in_specs=[ANY], out_specs=ANY,
        out_shape=jax.ShapeDtypeStruct((4, rows, packed.shape[1]), packed.dtype),
        scratch_shapes=[pltpu.SemaphoreType.DMA((6,)), pltpu.SemaphoreType.DMA((6,))],
    )(packed)
    chip = 2 * lax.axis_index("x") + lax.axis_index("y")
    return lax.dynamic_update_slice(others, packed[None], (chip, 0, 0))


def _swap_with_sibling(mine, name):
    def body(in_ref, out_ref, send_sem, recv_sem):
        x, y, c = _place()
        cp = pltpu.make_async_remote_copy(src_ref=in_ref, dst_ref=out_ref, send_sem=send_sem, recv_sem=recv_sem,
                                          device_id=(x, y, 1 - c), device_id_type=MESH)
        cp.start()
        cp.wait()

    return pl.pallas_call(
        body, name=name, in_specs=[ANY], out_specs=ANY,
        out_shape=jax.ShapeDtypeStruct(mine.shape, mine.dtype),
        scratch_shapes=[pltpu.SemaphoreType.DMA, pltpu.SemaphoreType.DMA],
    )(mine)


def _scatter_to_chips(parts):
    def body(in_ref, out_ref, send_sems, recv_sems):
        x, y, c = _place()
        copies = [pltpu.make_async_remote_copy(src_ref=in_ref.at[2 * px + py], dst_ref=out_ref.at[j], send_sem=send_sems.at[j],
                                               recv_sem=recv_sems.at[j], device_id=(px, py, c), device_id_type=MESH)
                  for j, (px, py) in enumerate(_other_chips(x, y))]
        for cp in copies:
            cp.start()
        for cp in copies:
            cp.wait()

    return pl.pallas_call(
        body, name="scatter_to_chips", in_specs=[ANY], out_specs=ANY,
        out_shape=jax.ShapeDtypeStruct((3,) + parts.shape[1:], parts.dtype),
        scratch_shapes=[pltpu.SemaphoreType.DMA((3,)), pltpu.SemaphoreType.DMA((3,))],
    )(parts)


def _join_halves(mine):
    half = mine.shape[0]
    c = lax.axis_index("c")
    got = _swap_with_sibling(mine, "swap_reduced_halves")
    out = jnp.zeros((2 * half, mine.shape[1]), mine.dtype)
    out = lax.dynamic_update_slice(out, mine, (c * half, 0))
    return lax.dynamic_update_slice(out, got, ((1 - c) * half, 0))


def _add_pair(a, b):
    n, h, w = a.shape
    bh = 384 if h % 384 == 0 else h

    def body(a_ref, b_ref, s_ref, sb_ref):
        s = a_ref[...] + b_ref[...]
        s_ref[...] = s
        sb_ref[...] = s.astype(BF16)

    spec = lambda: pl.BlockSpec((1, bh, w), lambda j, i: (j, i, 0))
    return pl.pallas_call(
        body, name="add_pair", grid=(n, h // bh), in_specs=[spec(), spec()], out_specs=[spec(), spec()],
        out_shape=[jax.ShapeDtypeStruct(a.shape, F32), jax.ShapeDtypeStruct(a.shape, BF16)],
        compiler_params=_cparams("parallel", "parallel"),
    )(a, b)


def _add_received(own, got):
    h, w = own.shape
    bh = 384 if h % 384 == 0 else h

    def body(own_ref, got_ref, o_ref):
        o_ref[...] = ((own_ref[...] + got_ref[0].astype(F32)) + got_ref[1].astype(F32)) + got_ref[2].astype(F32)

    return pl.pallas_call(
        body, name="add_received", grid=(h // bh,),
        in_specs=[pl.BlockSpec((bh, w), lambda i: (i, 0)), pl.BlockSpec((3, bh, w), lambda i: (0, i, 0))],
        out_specs=pl.BlockSpec((bh, w), lambda i: (i, 0)), out_shape=jax.ShapeDtypeStruct((h, w), F32),
        compiler_params=_cparams("parallel"),
    )(own, got)


def _all_sum_small(mine):
    rows, w = mine.shape

    def body(in_ref, out_ref, slots, send_sems, recv_sems):
        x, y, c = _place()
        me = 4 * x + 2 * y + c
        slots[me] = in_ref[...]
        copies = []
        for k in range(1, 8):
            peer = (1 - x if k & 4 else x, 1 - y if k & 2 else y, 1 - c if k & 1 else c)
            copies.append(pltpu.make_async_remote_copy(src_ref=in_ref, dst_ref=slots.at[me], send_sem=send_sems.at[k - 1],
                                                       recv_sem=recv_sems.at[k - 1], device_id=peer, device_id_type=MESH))
        for cp in copies:
            cp.start()
        for cp in copies:
            cp.wait()
        total = slots[0]
        for d in range(1, 8):
            total = total + slots[d]
        out_ref[...] = total

    return pl.pallas_call(
        body, name="all_sum_small", out_shape=jax.ShapeDtypeStruct((rows, w), F32),
        in_specs=[pl.BlockSpec(memory_space=pltpu.VMEM)], out_specs=pl.BlockSpec(memory_space=pltpu.VMEM),
        scratch_shapes=[pltpu.VMEM((8, rows, w), F32), pltpu.SemaphoreType.DMA((7,)), pltpu.SemaphoreType.DMA((7,))],
        compiler_params=pltpu.CompilerParams(vmem_limit_bytes=VMEM_LIMIT_V7X),
    )(mine)


def _shard_shape(rows, cols, cut):
    return (rows, cols // 4) if cut == "col" else (rows // 4, cols)


def _pack_shards(shards):
    return jnp.concatenate([shards[name].reshape(-1, 1024) for name, *_ in BIG_WEIGHTS], axis=0)


def _unpack_gathered(g):
    out, off = {}, 0
    for name, rows, cols, cut in BIG_WEIGHTS:
        rs, cs = _shard_shape(rows, cols, cut)
        n = rs * cs // 1024
        seg = g[:, off:off + n, :].reshape(4, rs, cs)
        out[name] = seg.reshape(rows, cols) if cut == "row" else jnp.transpose(seg, (1, 0, 2)).reshape(rows, cols)
        off += n
    return out


def _pack_grads(grads):
    segs = []
    for name, rows, cols, cut in BIG_WEIGHTS:
        rs, cs = _shard_shape(rows, cols, cut)
        gw = grads[name]
        seg = gw.reshape(4, rs, cs) if cut == "row" else jnp.transpose(gw.reshape(rows, 4, cs), (1, 0, 2))
        segs.append(seg.reshape(4, rs * cs // 1024, 1024))
    return jnp.concatenate(segs, axis=1)


def _unpack_shard(r):
    out, off = {}, 0
    for name, rows, cols, cut in BIG_WEIGHTS:
        rs, cs = _shard_shape(rows, cols, cut)
        n = rs * cs // 1024
        out[name] = r[off:off + n].reshape(rs, cs)
        off += n
    return out


def _small_rows(shape):
    return -(-int(np.prod(shape)) // 1024)


def _pack_small(vals):
    segs = []
    for name, shape in SMALL_WEIGHTS:
        flat = vals[name].reshape(-1)
        rows = _small_rows(shape)
        segs.append(jnp.pad(flat, (0, rows * 1024 - flat.shape[0])).reshape(rows, 1024))
    total = sum(s.shape[0] for s in segs)
    segs.append(jnp.zeros((-total % 8, 1024), F32))
    return jnp.concatenate(segs, axis=0)


def _unpack_small(packed):
    out, off = {}, 0
    for name, shape in SMALL_WEIGHTS:
        rows = _small_rows(shape)
        out[name] = packed[off:off + rows].reshape(-1)[:int(np.prod(shape))].reshape(shape)
        off += rows
    return out


def _pad_w_in(w):
    return jnp.concatenate([w[:, :1216], jnp.zeros((w.shape[0], 64), w.dtype), w[:, 1216:]], axis=1)


def _unpad_w_in(g):
    return jnp.concatenate([g[:, :1216], g[:, 1280:]], axis=1)


def _pad_heads(w):
    r = w.shape[0]
    return jnp.pad(w.reshape(r, N_HEADS, QK_HEAD), ((0, 0), (0, 0), (0, HEAD_PAD - QK_HEAD))).reshape(r, N_HEADS * HEAD_PAD)


def _unpad_heads(g):
    r = g.shape[0]
    return g.reshape(r, N_HEADS, HEAD_PAD)[:, :, :QK_HEAD].reshape(r, N_HEADS * QK_HEAD)


def _local_step(x, positions, tgt, big, small):
    l = x.shape[0]
    t = min(l, 512)
    t_mlp = min(l, 256)
    tq = min(l, 512)
    tc = min(l, 256)
    row = lambda v: v.reshape(1, -1).astype(F32)

    w_in_p = _pad_w_in(big["w_in"])
    w_qb_p = _pad_heads(big["w_q_b"])
    g1, g2 = row(small["norm_mix"]), row(small["norm_mlp"])
    gqa, gkva = row(small["q_a_norm"]), row(small["kv_a_norm"])
    gq = jnp.pad(row(small["q_norm"]), ((0, 0), (0, HEAD_PAD - QK_HEAD)))
    gk = jnp.pad(row(small["k_norm"]), ((0, 0), (0, HEAD_PAD - QK_HEAD)))
    half = QK_ROPE // 2
    inv_freq = ROPE_THETA ** (-jnp.arange(half, dtype=F32) / half)
    invf = jnp.concatenate([inv_freq, inv_freq, jnp.zeros((64,), F32)]).reshape(1, 128)
    sgn = jnp.concatenate([-jnp.ones((half,), F32), jnp.ones((half,), F32), jnp.zeros((64,), F32)]).reshape(1, 128)
    pos = positions.reshape(l, 1)

    a_re, a_im = small["ssm_a_re"], small["ssm_a_im"]
    log_dt = small["ssm_log_dt"].reshape(SSM_GROUPS, 1)
    to_cgp = lambda b: jnp.transpose(b, (2, 0, 1)).reshape(SSM_GROUP_CH * SSM_GROUPS, SSM_STATE)
    from_cgp = lambda b: jnp.transpose(b.reshape(SSM_GROUP_CH, SSM_GROUPS, SSM_STATE), (1, 2, 0))
    bt_re, bt_im = to_cgp(small["ssm_b_re"]), to_cgp(small["ssm_b_im"])
    pow_r, pow_i, bb_re, bb_im = _ssm_param_fwd(a_re, a_im, log_dt, bt_re, bt_im)
    to_gcp = lambda b: jnp.transpose(b.reshape(SSM_GROUP_CH, SSM_GROUPS, SSM_STATE), (1, 0, 2))
    wb = jnp.concatenate([_block_diag(to_gcp(bb_re)), _block_diag(to_gcp(bb_im))], axis=1).astype(BF16)
    wc = jnp.concatenate([_block_diag(small["ssm_c_re"]).T, -_block_diag(small["ssm_c_im"]).T], axis=0).astype(BF16)
    dskip = row(small["ssm_d"])
    b_glu = row(small["b_glu"])

    u, lat, gs, gm = _in_proj_fwd(x, g1, w_in_p, t)
    xr, xi, y, y_ssm = _ssm_fwd(u, wb, wc, _scan_tables(pow_r, pow_i, False), dskip, big["w_glu"], b_glu, big["w_o_ssm"], tc)
    q, k, v = _mla_pre_fwd(lat, pos, invf, sgn, gqa, gkva, gq, gk, w_qb_p, big["w_kv_b"], t)
    attn, lse = _attn_fwd(q, k, v, tq)
    y_mla, mixed, h = _merge_fwd(attn, y_ssm, gs, gm, x, big["w_o_mla"], big["w_out"], t)
    dh, hn, da, hid, dout, loss_blk, g_norm_mlp = _mlp_fwd_bwd(h, tgt, g2, big["w_up"], big["w_down"], t_mlp)

    grads = {"w_down": _wgrad(hid, dout, "wgrad_down"), "w_up": _wgrad(hn, da, "wgrad_up")}
    dys, dym, dgs, dgm, dattn = _merge_bwd(dh, y_ssm, y_mla, gs, gm, big["w_o_mla"], big["w_out"], t)
    grads["w_out"] = _wgrad(mixed, dh, "wgrad_out")
    grads["w_o_mla"] = _wgrad(attn, dym, "wgrad_o_mla")

    dq, delta = _attn_bwd_dq(q, k, v, attn, dattn, lse, tq)
    lanes = lambda a: a.reshape(N_HEADS, l // tq, 1, tq)
    dk, dv = _attn_bwd_dkv(q, k, v, dattn, lanes(lse), lanes(delta), tq)
    d_lat, ql, dq0, ckn, dkv, g_qa, g_kva, g_q, g_k = _mla_pre_bwd(lat, pos, invf, sgn, gqa, gkva, gq, gk, w_qb_p,
                                                                    big["w_kv_b"], dq, dk, dv, t)
    grads["w_q_b"] = _unpad_heads(_wgrad(ql, dq0, "wgrad_q_b"))
    grads["w_kv_b"] = _wgrad(ckn, dkv, "wgrad_kv_b")

    d_u, adj, dy, z, z2, dpre, g_b_glu, g_d, g_lr, g_li = _ssm_bwd(
        dys, y, u, xr, xi, wb, wc, _scan_tables(pow_r, pow_i, True), dskip, big["w_glu"], b_glu, big["w_o_ssm"], tc)
    grads["w_o_ssm"] = _wgrad(z2, dys, "wgrad_o_ssm")
    grads["w_glu"] = _wgrad(z, dpre, "wgrad_glu")
    g_wb = _wgrad(u, adj, "wgrad_ssm_b")
    g_wcr = _wgrad(xr, dy, "wgrad_ssm_c_re")
    g_wci = _wgrad(xi, dy, "wgrad_ssm_c_im")
    to_cgp_rows = lambda m: jnp.transpose(_block_diag_take(m), (1, 0, 2)).reshape(SSM_GROUP_CH * SSM_GROUPS, SSM_STATE)
    g_ar, g_ai, g_ldt, g_btr, g_bti = _ssm_param_bwd(
        a_re, a_im, log_dt, bt_re, bt_im, g_lr.reshape(SSM_GROUPS, SSM_STATE), g_li.reshape(SSM_GROUPS, SSM_STATE),
        to_cgp_rows(g_wb[:, :GP]), to_cgp_rows(g_wb[:, GP:]))

    grad_x, xn, dproj, g_norm_mix = _in_proj_bwd(x, g1, w_in_p, d_u, d_lat, dgs, dgm, dh, t)
    grads["w_in"] = _unpad_w_in(_wgrad(xn, dproj, "wgrad_in"))

    g_small = {
        "norm_mix": g_norm_mix.reshape(-1), "norm_mlp": g_norm_mlp.reshape(-1), "q_a_norm": g_qa.reshape(-1),
        "kv_a_norm": g_kva.reshape(-1), "q_norm": g_q.reshape(-1)[:QK_HEAD], "k_norm": g_k.reshape(-1)[:QK_HEAD],
        "ssm_a_re": g_ar, "ssm_a_im": g_ai, "ssm_log_dt": g_ldt.reshape(-1),
        "ssm_b_re": from_cgp(g_btr), "ssm_b_im": from_cgp(g_bti),
        "ssm_c_re": _block_diag_take(g_wcr.T), "ssm_c_im": -_block_diag_take(g_wci.T),
        "ssm_d": g_d.reshape(SSM_GROUPS, SSM_GROUP_CH), "b_glu": g_b_glu.reshape(-1),
    }
    return loss_blk[0, 0], grad_x, grads, g_small


def kernel(x, positions, norm_mix, w_in, q_a_norm, kv_a_norm, w_q_b, w_kv_b, q_norm, k_norm, w_o_mla, ssm_a_re, ssm_a_im, ssm_log_dt, ssm_b_re, ssm_b_im, ssm_c_re, ssm_c_im, ssm_d, w_glu, b_glu, w_o_ssm, w_out, norm_mlp, w_up, w_down, loss_target, m_norm_mix, m_w_in, m_q_a_norm, m_kv_a_norm, m_w_q_b, m_w_kv_b, m_q_norm, m_k_norm, m_w_o_mla, m_ssm_a_re, m_ssm_a_im, m_ssm_log_dt, m_ssm_b_re, m_ssm_b_im, m_ssm_c_re, m_ssm_c_im, m_ssm_d, m_w_glu, m_b_glu, m_w_o_ssm, m_w_out, m_norm_mlp, m_w_up, m_w_down, v_norm_mix, v_w_in, v_q_a_norm, v_kv_a_norm, v_w_q_b, v_w_kv_b, v_q_norm, v_k_norm, v_w_o_mla, v_ssm_a_re, v_ssm_a_im, v_ssm_log_dt, v_ssm_b_re, v_ssm_b_im, v_ssm_c_re, v_ssm_c_im, v_ssm_d, v_w_glu, v_b_glu, v_w_o_ssm, v_w_out, v_norm_mlp, v_w_up, v_w_down):
    args = dict(locals())
    w = {n: args[n][0] for n in WEIGHT_ORDER}
    m = {n: args["m_" + n][0] for n in WEIGHT_ORDER}
    v = {n: args["v_" + n][0] for n in WEIGHT_ORDER}
    big_names = [n for n, *_ in BIG_WEIGHTS]
    small_names = [n for n, _ in SMALL_WEIGHTS]

    gathered = _gather_weights(_pack_shards({n: w[n] for n in big_names}).astype(BF16))
    big = _unpack_gathered(gathered)
    small = {n: w[n] for n in small_names}

    loss_local, grad_x, grads, g_small = _local_step(x[0], positions[0], loss_target[0], big, small)
    loss = lax.psum(loss_local, ("x", "y", "c"))

    c = lax.axis_index("c")
    chip = 2 * lax.axis_index("x") + lax.axis_index("y")
    packed = _pack_grads(grads)
    half = packed.shape[1] // 2
    keep = lax.dynamic_slice_in_dim(packed, c * half, half, axis=1)
    give = lax.dynamic_slice_in_dim(packed, (1 - c) * half, half, axis=1)
    pair_f32, pair_bf16 = _add_pair(keep, _swap_with_sibling(give, "swap_gradient_halves"))
    own = lax.dynamic_index_in_dim(pair_f32, chip, axis=0, keepdims=False)
    reduced = _join_halves(_add_received(own, _scatter_to_chips(pair_bf16)))
    g_big = _unpack_shard(reduced)

    g_small_sum = _unpack_small(_all_sum_small(_pack_small(g_small)))

    grad_w, delta_w, new_m, new_v = {}, {}, {}, {}
    for n in big_names:
        grad_w[n] = g_big[n]
        delta_w[n], new_m[n], new_v[n] = _adamw(w[n], g_big[n], m[n], v[n], "adamw_" + n)
    ps = _pack_small(g_small_sum)
    d_s, m_s, v_s = _adamw(_pack_small(small), ps, _pack_small({n: m[n] for n in small_names}),
                           _pack_small({n: v[n] for n in small_names}), "adamw_small")
    d_s, m_s, v_s = _unpack_small(d_s), _unpack_small(m_s), _unpack_small(v_s)
    for n in small_names:
        grad_w[n], delta_w[n], new_m[n], new_v[n] = g_small_sum[n], d_s[n], m_s[n], v_s[n]

    lead = lambda d: [d[n][None] for n in WEIGHT_ORDER]
    return (loss, grad_x[None], *lead(grad_w), *lead(delta_w), *lead(new_m), *lead(new_v))
```

```python
import functools
import math

import jax
import jax.numpy as jnp
import numpy as np
from jax import lax
from jax.experimental import pallas as pl
from jax.experimental.pallas import tpu as pltpu

F32 = jnp.float32
BF16 = jnp.bfloat16

D_MODEL = 1024
SSM_GROUPS = 32
SSM_GROUP_CH = 16
SSM_WIDTH = 512
SSM_STATE = 64
GP = SSM_GROUPS * SSM_STATE
N_HEADS = 8
QK_NOPE = 128
QK_ROPE = 64
QK_HEAD = 192
HEAD_PAD = 256
V_HEAD = 128
Q_LORA = 384
KV_LORA = 256
LAT_W = 768
D_IN = 3264
D_IN_PAD = 3328
D_FF = 4096
ROPE_THETA = 10000.0
EPS = 1e-6
ATT_SCALE = QK_HEAD ** -0.5

ADAM_LR = 0.001
ADAM_B1 = 0.9
ADAM_B2 = 0.999
ADAM_EPS = 1e-08
ADAM_WD = 0.01
ADAM_STEP = 10

VMEM_LIMIT_V7X = 56 * 1024 * 1024
MESH = pl.DeviceIdType.MESH

BIG_WEIGHTS = (
    ("w_in", 1024, 3264, "col"),
    ("w_q_b", 384, 1536, "col"),
    ("w_kv_b", 256, 2048, "col"),
    ("w_o_mla", 1024, 1024, "row"),
    ("w_glu", 512, 512, "row"),
    ("w_o_ssm", 512, 1024, "col"),
    ("w_out", 1024, 1024, "row"),
    ("w_up", 1024, 4096, "col"),
    ("w_down", 4096, 1024, "row"),
)
GROUPS = {
    "a": (1024, (("w_down", 1024), ("w_up", 1024), ("w_o_mla", 256), ("w_out", 256))),
    "b": (816, (("w_in", 1024),)),
    "c": (384, (("w_q_b", 384),)),
    "d": (512, (("w_kv_b", 256), ("w_glu", 128))),
    "e": (256, (("w_o_ssm", 512),)),
}


def _group_rows(group):
    return sum(r for _, r in GROUPS[group][1])


def _place_in_group(name):
    for group, (width, members) in GROUPS.items():
        off = 0
        for member, rows in members:
            if member == name:
                return group, off, rows, width
            off += rows
    raise KeyError(name)


SMALL_WEIGHTS = (
    ("norm_mix", (1024,)), ("q_a_norm", (384,)), ("kv_a_norm", (256,)), ("q_norm", (192,)), ("k_norm", (192,)),
    ("ssm_a_re", (32, 64)), ("ssm_a_im", (32, 64)), ("ssm_log_dt", (32,)),
    ("ssm_b_re", (32, 64, 16)), ("ssm_b_im", (32, 64, 16)), ("ssm_c_re", (32, 16, 64)), ("ssm_c_im", (32, 16, 64)),
    ("ssm_d", (32, 16)), ("b_glu", (512,)), ("norm_mlp", (1024,)),
)
WEIGHT_ORDER = ('norm_mix', 'w_in', 'q_a_norm', 'kv_a_norm', 'w_q_b', 'w_kv_b', 'q_norm', 'k_norm', 'w_o_mla', 'ssm_a_re',
                'ssm_a_im', 'ssm_log_dt', 'ssm_b_re', 'ssm_b_im', 'ssm_c_re', 'ssm_c_im', 'ssm_d', 'w_glu', 'b_glu',
                'w_o_ssm', 'w_out', 'norm_mlp', 'w_up', 'w_down')


def _cparams(*sem):
    return pltpu.CompilerParams(dimension_semantics=sem if sem else None, vmem_limit_bytes=VMEM_LIMIT_V7X)


def _resident(shape, index=None):
    index = (0,) * len(shape) if index is None else index
    return pl.BlockSpec(shape, lambda *_: index, pipeline_mode=pl.Buffered(1))


def _member_block(name):
    _, off, rows, width = _place_in_group(name)
    return _resident((4, rows, width), (0, off // rows, 0))


def _rows(t, width):
    return pl.BlockSpec((t, width), lambda i: (i, 0))


def _mm(a, b):
    return jnp.dot(a.astype(BF16), b.astype(BF16), preferred_element_type=F32)


def _mm_nt(a, b):
    return lax.dot_general(a.astype(BF16), b.astype(BF16), (((1,), (1,)), ((), ())), preferred_element_type=F32)


def _mm_tn(a, b):
    return lax.dot_general(a.astype(BF16), b.astype(BF16), (((0,), (0,)), ((), ())), preferred_element_type=F32)


def _rms_fwd(x, g, n):
    r = lax.rsqrt(jnp.sum(x * x, axis=-1, keepdims=True) * (1.0 / n) + EPS)
    return x * r * g


def _rms_bwd(x, g, dy, n):
    r = lax.rsqrt(jnp.sum(x * x, axis=-1, keepdims=True) * (1.0 / n) + EPS)
    xh = x * r
    dxh = dy * g
    dx = r * (dxh - xh * (jnp.sum(dxh * xh, axis=-1, keepdims=True) * (1.0 / n)))
    return dx, dy * xh


def _colsum(a):
    return jnp.sum(a, axis=0, keepdims=True)


def _accumulate(ref, value, first):
    @pl.when(first)
    def _():
        ref[...] = value

    @pl.when(jnp.logical_not(first))
    def _():
        ref[...] += value


def _sigmoid(a):
    return 1.0 / (1.0 + jnp.exp(-a))


GELU_C = math.sqrt(2.0 / math.pi)
GELU_A = 0.044715


def _gelu(y):
    return 0.5 * y * (1.0 + jnp.tanh(GELU_C * (y + GELU_A * y * y * y)))


def _gelu_grad(y):
    t = jnp.tanh(GELU_C * (y + GELU_A * y * y * y))
    return 0.5 * (1.0 + t) + 0.5 * y * (1.0 - t * t) * GELU_C * (1.0 + 3.0 * GELU_A * y * y)


def _in_proj_fwd(x, g1, w_in_p, t):
    l = x.shape[0]

    def body(x_ref, g_ref, w_ref, u_ref, lat_ref, gs_ref, gm_ref):
        xn = _rms_fwd(x_ref[...], g_ref[...], D_MODEL).astype(BF16)
        u_ref[...] = _mm(xn, w_ref[:, 0:512])
        lat_ref[...] = _mm(xn, w_ref[:, 512:1280])
        gs_ref[...] = _mm(xn, w_ref[:, 1280:2304])
        gm_ref[...] = _mm(xn, w_ref[:, 2304:3328])

    return pl.pallas_call(
        body, name="in_proj_fwd", grid=(l // t,),
        in_specs=[_rows(t, D_MODEL), _resident((1, D_MODEL)), _resident((D_MODEL, D_IN_PAD))],
        out_specs=[_rows(t, 512), _rows(t, LAT_W), _rows(t, D_MODEL), _rows(t, D_MODEL)],
        out_shape=[jax.ShapeDtypeStruct((l, 512), F32), jax.ShapeDtypeStruct((l, LAT_W), F32),
                   jax.ShapeDtypeStruct((l, D_MODEL), F32), jax.ShapeDtypeStruct((l, D_MODEL), F32)],
        compiler_params=_cparams("parallel"),
    )(x, g1, w_in_p)


def _in_proj_bwd(x, g1, w_in_p, d_u, d_lat, d_gs, d_gm, dh, t):
    l = x.shape[0]

    def body(x_ref, g_ref, w_ref, du_ref, dlat_ref, dgs_ref, dgm_ref, dh_ref, gx_ref, xn_ref, dproj_ref, dg_ref):
        xv = x_ref[...]
        g = g_ref[...]
        xn_ref[...] = _rms_fwd(xv, g, D_MODEL).astype(BF16)
        dproj_ref[:, 0:512] = du_ref[...]
        dproj_ref[:, 512:1280] = dlat_ref[...]
        dproj_ref[:, 1280:2304] = dgs_ref[...]
        dproj_ref[:, 2304:3328] = dgm_ref[...]
        dxn = _mm_nt(dproj_ref[...], w_ref[...])
        dx, dg_rows = _rms_bwd(xv, g, dxn, D_MODEL)
        gx_ref[...] = dh_ref[...] + dx
        _accumulate(dg_ref, _colsum(dg_rows), pl.program_id(0) == 0)

    return pl.pallas_call(
        body, name="in_proj_bwd", grid=(l // t,),
        in_specs=[_rows(t, D_MODEL), _resident((1, D_MODEL)), _resident((D_MODEL, D_IN_PAD)), _rows(t, 512),
                  _rows(t, LAT_W), _rows(t, D_MODEL), _rows(t, D_MODEL), _rows(t, D_MODEL)],
        out_specs=[_rows(t, D_MODEL), _rows(t, D_MODEL), _rows(t, D_IN_PAD), pl.BlockSpec((1, D_MODEL), lambda i: (0, 0))],
        out_shape=[jax.ShapeDtypeStruct((l, D_MODEL), F32), jax.ShapeDtypeStruct((l, D_MODEL), BF16),
                   jax.ShapeDtypeStruct((l, D_IN_PAD), BF16), jax.ShapeDtypeStruct((1, D_MODEL), F32)],
        compiler_params=_cparams("arbitrary"),
    )(x, g1, w_in_p, d_u, d_lat, d_gs, d_gm, dh)


def _ssm_param_fn(a_re, a_im, log_dt, bt_re, bt_im):
    dt = jnp.exp(log_dt)
    er = jnp.exp(a_re * dt)
    lr = er * jnp.cos(a_im * dt)
    li = er * jnp.sin(a_im * dt)
    den = a_re * a_re + a_im * a_im
    nr = lr - 1.0
    kr = (nr * a_re + li * a_im) / den
    ki = (li * a_re - nr * a_im) / den
    krt = jnp.concatenate([kr] * SSM_GROUP_CH, axis=0)
    kit = jnp.concatenate([ki] * SSM_GROUP_CH, axis=0)
    return lr, li, krt * bt_re - kit * bt_im, krt * bt_im + kit * bt_re


def _ssm_param_fwd(a_re, a_im, log_dt, bt_re, bt_im):
    def body(ar_ref, ai_ref, ldt_ref, br_ref, bi_ref, pr_ref, pi_ref, bbr_ref, bbi_ref):
        lr, li, bbr, bbi = _ssm_param_fn(ar_ref[...], ai_ref[...], ldt_ref[...], br_ref[...], bi_ref[...])
        bbr_ref[...] = bbr
        bbi_ref[...] = bbi
        qr, qi = lr, li
        for j in range(8):
            pr_ref[j] = qr
            pi_ref[j] = qi
            qr, qi = qr * lr - qi * li, qr * li + qi * lr

    g, p = SSM_GROUPS, SSM_STATE
    return pl.pallas_call(
        body, name="ssm_param_fwd",
        out_shape=[jax.ShapeDtypeStruct((8, g, p), F32), jax.ShapeDtypeStruct((8, g, p), F32),
                   jax.ShapeDtypeStruct((SSM_GROUP_CH * g, p), F32), jax.ShapeDtypeStruct((SSM_GROUP_CH * g, p), F32)],
    )(a_re, a_im, log_dt, bt_re, bt_im)


def _ssm_param_bwd(a_re, a_im, log_dt, bt_re, bt_im, g_lr, g_li, g_bbr, g_bbi):
    def body(ar_ref, ai_ref, ldt_ref, br_ref, bi_ref, glr_ref, gli_ref, gbr_ref, gbi_ref, o_ar, o_ai, o_ldt, o_br, o_bi):
        _, vjp = jax.vjp(_ssm_param_fn, ar_ref[...], ai_ref[...], ldt_ref[...], br_ref[...], bi_ref[...])
        d_ar, d_ai, d_ldt, d_br, d_bi = vjp((glr_ref[...], gli_ref[...], gbr_ref[...], gbi_ref[...]))
        o_ar[...] = d_ar
        o_ai[...] = d_ai
        o_ldt[...] = d_ldt
        o_br[...] = d_br
        o_bi[...] = d_bi

    g, p = SSM_GROUPS, SSM_STATE
    return pl.pallas_call(
        body, name="ssm_param_bwd",
        out_shape=[jax.ShapeDtypeStruct((g, p), F32), jax.ShapeDtypeStruct((g, p), F32), jax.ShapeDtypeStruct((g, 1), F32),
                   jax.ShapeDtypeStruct((SSM_GROUP_CH * g, p), F32), jax.ShapeDtypeStruct((SSM_GROUP_CH * g, p), F32)],
    )(a_re, a_im, log_dt, bt_re, bt_im, g_lr, g_li, g_bbr, g_bbi)


def _block_diag(t_gcp):
    eye = jnp.eye(SSM_GROUPS, dtype=t_gcp.dtype)
    return (t_gcp[:, :, None, :] * eye[:, None, :, None]).reshape(SSM_WIDTH, GP)


def _block_diag_take(m):
    t = m.reshape(SSM_GROUPS, SSM_GROUP_CH, SSM_GROUPS, SSM_STATE)
    return jnp.transpose(jnp.diagonal(t, axis1=0, axis2=2), (2, 0, 1))


def _scan_tables(pr, pi, reverse):
    pr = pr.reshape(8, GP)
    pi = pi.reshape(8, GP)
    if reverse:
        pi = -pi
    row = jnp.arange(8)[:, None]
    tabs = []
    for k in (1, 2, 4):
        keep = (row < 8 - k) if reverse else (row >= k)
        tabs.append(jnp.where(keep, pr[k - 1][None, :], 0.0))
        tabs.append(jnp.where(keep, pi[k - 1][None, :], 0.0))
    if reverse:
        tabs += [pr[::-1], pi[::-1]]
    else:
        tabs += [pr, pi]
    return jnp.stack(tabs).astype(F32)


SCAN_STRIP = 512


def _scan_chunk(inr_ref, ini_ref, outr_ref, outi_ref, cr_ref, ci_ref, tab_ref, tc, reverse):
    n_blocks = tc // 8

    def block(j, _):
        i = (n_blocks - 1 - j) if reverse else j
        rows = pl.ds(pl.multiple_of(i * 8, 8), 8)
        for s in range(GP // SCAN_STRIP):
            sl = pl.ds(s * SCAN_STRIP, SCAN_STRIP)
            xr = inr_ref[rows, sl]
            xi = ini_ref[rows, sl]
            for n, k in enumerate((1, 2, 4)):
                shift = (8 - k) if reverse else k
                sr = pltpu.roll(xr, shift, 0)
                si = pltpu.roll(xi, shift, 0)
                mr = tab_ref[2 * n, :, sl]
                mi = tab_ref[2 * n + 1, :, sl]
                xr, xi = xr + mr * sr - mi * si, xi + mr * si + mi * sr
            qr = tab_ref[6, :, sl]
            qi = tab_ref[7, :, sl]
            cr = cr_ref[:, sl]
            ci = ci_ref[:, sl]
            xr, xi = xr + qr * cr - qi * ci, xi + qr * ci + qi * cr
            outr_ref[rows, sl] = xr
            outi_ref[rows, sl] = xi
            edge = 0 if reverse else 7
            cr_ref[:, sl] = jnp.broadcast_to(xr[edge:edge + 1, :], (8, SCAN_STRIP))
            ci_ref[:, sl] = jnp.broadcast_to(xi[edge:edge + 1, :], (8, SCAN_STRIP))
        return 0

    lax.fori_loop(0, n_blocks, block, 0)


def _glu_pre(z, wg_ref):
    return sum(_mm(z[:, 128 * j:128 * (j + 1)], wg_ref[j]) for j in range(4))


def _ssm_fwd(u, wb, wc, tabs, dskip, grp_d, b_glu, grp_e, tc):
    l = u.shape[0]

    def body(u_ref, wb_ref, wc_ref, tab_ref, d_ref, wg_ref, bg_ref, wo_ref, xr_ref, xi_ref, y_ref, ys_ref,
             bur, bui, cr, ci):
        @pl.when(pl.program_id(0) == 0)
        def _():
            cr[...] = jnp.zeros_like(cr)
            ci[...] = jnp.zeros_like(ci)

        uv = u_ref[...]
        ub = uv.astype(BF16)
        bur[...] = _mm(ub, wb_ref[:, 0:GP])
        bui[...] = _mm(ub, wb_ref[:, GP:2 * GP])
        _scan_chunk(bur, bui, xr_ref, xi_ref, cr, ci, tab_ref, tc, False)
        y = _mm(xr_ref[...], wc_ref[0:GP, :]) + _mm(xi_ref[...], wc_ref[GP:2 * GP, :]) + d_ref[...] * uv
        y_ref[...] = y
        z = _gelu(y)
        z2 = z * _sigmoid(_glu_pre(z, wg_ref) + bg_ref[...])
        for s in range(4):
            ys_ref[:, 256 * s:256 * (s + 1)] = _mm(z2, wo_ref[s])

    return pl.pallas_call(
        body, name="ssm_fwd", grid=(l // tc,),
        in_specs=[_rows(tc, 512), _resident((512, 2 * GP)), _resident((2 * GP, 512)), _resident((8, 8, GP)),
                  _resident((1, 512)), _member_block("w_glu"), _resident((1, 512)), _member_block("w_o_ssm")],
        out_specs=[_rows(tc, GP), _rows(tc, GP), _rows(tc, 512), _rows(tc, D_MODEL)],
        out_shape=[jax.ShapeDtypeStruct((l, GP), F32), jax.ShapeDtypeStruct((l, GP), F32),
                   jax.ShapeDtypeStruct((l, 512), F32), jax.ShapeDtypeStruct((l, D_MODEL), F32)],
        scratch_shapes=[pltpu.VMEM((tc, GP), F32), pltpu.VMEM((tc, GP), F32), pltpu.VMEM((8, GP), F32),
                        pltpu.VMEM((8, GP), F32)],
        compiler_params=_cparams("arbitrary"),
    )(u, wb, wc, tabs, dskip, grp_d, b_glu, grp_e)


def _ssm_bwd(dys, y, u, xr, xi, wb, wc, tabs_rev, dskip, grp_d, b_glu, grp_e, tc):
    l = u.shape[0]
    nc = l // tc

    def body(dys_ref, y_ref, u_ref, xr_ref, xi_ref, wb_ref, wc_ref, tab_ref, d_ref, wg_ref, bg_ref, wo_ref,
             du_ref, a_ref, dy_ref, z_ref, z2_ref, dpre_ref, gb_ref, gd_ref, glr_ref, gli_ref,
             dxr, dxi, ar, ai, cr, ci):
        first = pl.program_id(0) == 0

        @pl.when(first)
        def _():
            cr[...] = jnp.zeros_like(cr)
            ci[...] = jnp.zeros_like(ci)

        yv = y_ref[...]
        uv = u_ref[...]
        dz2 = sum(_mm_nt(dys_ref[:, 256 * j:256 * (j + 1)], wo_ref[j]) for j in range(4))
        z = _gelu(yv)
        s = _sigmoid(_glu_pre(z, wg_ref) + bg_ref[...])
        dpre = dz2 * z * s * (1.0 - s)
        dpreb = dpre.astype(BF16)
        dz = dz2 * s + jnp.concatenate([_mm_nt(dpreb, wg_ref[j]) for j in range(4)], axis=-1)
        dy = dz * _gelu_grad(yv)
        z_ref[...] = z.astype(BF16)
        z2_ref[...] = (z * s).astype(BF16)
        dpre_ref[...] = dpre.astype(BF16)
        dy_ref[...] = dy.astype(BF16)
        _accumulate(gb_ref, _colsum(dpre), first)
        _accumulate(gd_ref, _colsum(dy * uv), first)

        dyb = dy.astype(BF16)
        dxr[...] = _mm_nt(dyb, wc_ref[0:GP, :])
        dxi[...] = _mm_nt(dyb, wc_ref[GP:2 * GP, :])
        ar[pl.ds(tc, 8), :] = cr[...]
        ai[pl.ds(tc, 8), :] = ci[...]
        _scan_chunk(dxr, dxi, ar, ai, cr, ci, tab_ref, tc, True)
        a_ref[:, 0:GP] = ar[pl.ds(0, tc), :].astype(BF16)
        a_ref[:, GP:2 * GP] = ai[pl.ds(0, tc), :].astype(BF16)
        du_ref[...] = (dy * d_ref[...] + _mm_nt(a_ref[...], wb_ref[...])).astype(BF16)
        anr = ar[pl.ds(1, tc), :]
        ani = ai[pl.ds(1, tc), :]
        xrv = xr_ref[...]
        xiv = xi_ref[...]
        _accumulate(glr_ref, _colsum(anr * xrv + ani * xiv), first)
        _accumulate(gli_ref, _colsum(ani * xrv - anr * xiv), first)

    rev = lambda w: pl.BlockSpec((tc, w), lambda i: (nc - 1 - i, 0))
    acc = lambda w: pl.BlockSpec((1, w), lambda i: (0, 0))
    return pl.pallas_call(
        body, name="ssm_bwd", grid=(nc,),
        in_specs=[rev(D_MODEL), rev(512), rev(512), rev(GP), rev(GP), _resident((512, 2 * GP)), _resident((2 * GP, 512)),
                  _resident((8, 8, GP)), _resident((1, 512)), _member_block("w_glu"), _resident((1, 512)),
                  _member_block("w_o_ssm")],
        out_specs=[rev(512), rev(2 * GP), rev(512), rev(512), rev(512), rev(512), acc(512), acc(512), acc(GP), acc(GP)],
        out_shape=[jax.ShapeDtypeStruct((l, 512), BF16), jax.ShapeDtypeStruct((l, 2 * GP), BF16),
                   jax.ShapeDtypeStruct((l, 512), BF16), jax.ShapeDtypeStruct((l, 512), BF16),
                   jax.ShapeDtypeStruct((l, 512), BF16), jax.ShapeDtypeStruct((l, 512), BF16),
                   jax.ShapeDtypeStruct((1, 512), F32), jax.ShapeDtypeStruct((1, 512), F32),
                   jax.ShapeDtypeStruct((1, GP), F32), jax.ShapeDtypeStruct((1, GP), F32)],
        scratch_shapes=[pltpu.VMEM((tc, GP), F32), pltpu.VMEM((tc, GP), F32), pltpu.VMEM((tc + 8, GP), F32),
                        pltpu.VMEM((tc + 8, GP), F32), pltpu.VMEM((8, GP), F32), pltpu.VMEM((8, GP), F32)],
        compiler_params=_cparams("arbitrary"),
    )(dys, y, u, xr, xi, wb, wc, tabs_rev, dskip, grp_d, b_glu, grp_e)


def _swap_halves(b):
    lane = lax.broadcasted_iota(jnp.int32, b.shape, 1)
    return jnp.where(lane < 32, pltpu.roll(b, 96, 1), pltpu.roll(b, 32, 1))


def _rope_tables(pos_ref, invf_ref, sgn_ref):
    ang = pos_ref[...].astype(F32) * invf_ref[...]
    return jnp.cos(ang), jnp.sin(ang) * sgn_ref[...]


def _mla_pre_fwd(lat, pos, invf, sgn, gqa, gkva, gq, gk, w_qb_p, w_kvb, t):
    l = lat.shape[0]

    def body(lat_ref, pos_ref, invf_ref, sgn_ref, gqa_ref, gkva_ref, gq_ref, gk_ref, wq_ref, wkv_ref, q_ref, k_ref, v_ref):
        cs, sn = _rope_tables(pos_ref, invf_ref, sgn_ref)
        ql = _rms_fwd(lat_ref[:, 0:Q_LORA], gqa_ref[...], Q_LORA)
        ckn = _rms_fwd(lat_ref[:, Q_LORA:Q_LORA + KV_LORA], gkva_ref[...], KV_LORA)
        kpe = lat_ref[:, 640:768]
        q0 = _mm(ql, wq_ref[...])
        cknb = ckn.astype(BF16)
        kv = jnp.concatenate([_mm(cknb, wkv_ref[s]) for s in range(4)], axis=-1)
        for h in range(N_HEADS):
            q1 = _rms_fwd(q0[:, HEAD_PAD * h:HEAD_PAD * (h + 1)], gq_ref[...], QK_HEAD)
            b = q1[:, 128:256]
            q_ref[h, :, 0:128] = (q1[:, 0:128] * ATT_SCALE).astype(BF16)
            q_ref[h, :, 128:256] = ((b * cs + _swap_halves(b) * sn) * ATT_SCALE).astype(BF16)
            k0 = jnp.concatenate([kv[:, 256 * h:256 * h + 128], kpe], axis=-1)
            k1 = _rms_fwd(k0, gk_ref[...], QK_HEAD)
            b = k1[:, 128:256]
            k_ref[h, :, 0:128] = k1[:, 0:128].astype(BF16)
            k_ref[h, :, 128:256] = (b * cs + _swap_halves(b) * sn).astype(BF16)
            v_ref[h] = kv[:, 256 * h + 128:256 * h + 256].astype(BF16)

    heads = lambda w: pl.BlockSpec((N_HEADS, t, w), lambda i: (0, i, 0))
    return pl.pallas_call(
        body, name="mla_pre_fwd", grid=(l // t,),
        in_specs=[_rows(t, LAT_W), _rows(t, 1), _resident((1, 128)), _resident((1, 128)), _resident((1, Q_LORA)),
                  _resident((1, KV_LORA)), _resident((1, HEAD_PAD)), _resident((1, HEAD_PAD)),
                  _resident((Q_LORA, N_HEADS * HEAD_PAD)), _member_block("w_kv_b")],
        out_specs=[heads(HEAD_PAD), heads(HEAD_PAD), heads(V_HEAD)],
        out_shape=[jax.ShapeDtypeStruct((N_HEADS, l, HEAD_PAD), BF16), jax.ShapeDtypeStruct((N_HEADS, l, HEAD_PAD), BF16),
                   jax.ShapeDtypeStruct((N_HEADS, l, V_HEAD), BF16)],
        compiler_params=_cparams("parallel"),
    )(lat, pos, invf, sgn, gqa, gkva, gq, gk, w_qb_p, w_kvb)


def _mla_pre_bwd(lat, pos, invf, sgn, gqa, gkva, gq, gk, w_qb_p, w_kvb, dq, dk, dv, t):
    l = lat.shape[0]

    def body(lat_ref, pos_ref, invf_ref, sgn_ref, gqa_ref, gkva_ref, gq_ref, gk_ref, wq_ref, wkv_ref, dq_ref, dk_ref, dv_ref,
             dlat_ref, ql_ref, dq0_ref, ckn_ref, dkv_ref, ggqa_ref, ggkva_ref, ggq_ref, ggk_ref):
        first = pl.program_id(0) == 0
        cs, sn = _rope_tables(pos_ref, invf_ref, sgn_ref)
        q_lat = lat_ref[:, 0:Q_LORA]
        c_kv = lat_ref[:, Q_LORA:Q_LORA + KV_LORA]
        kpe = lat_ref[:, 640:768]
        ql = _rms_fwd(q_lat, gqa_ref[...], Q_LORA)
        ckn = _rms_fwd(c_kv, gkva_ref[...], KV_LORA)
        ql_ref[...] = ql.astype(BF16)
        ckn_ref[...] = ckn.astype(BF16)
        q0 = _mm(ql, wq_ref[...])
        cknb = ckn.astype(BF16)
        kv = jnp.concatenate([_mm(cknb, wkv_ref[s]) for s in range(4)], axis=-1)
        dkpe = jnp.zeros_like(kpe)
        ggq = jnp.zeros((1, HEAD_PAD), F32)
        ggk = jnp.zeros((1, HEAD_PAD), F32)

        def unrope(d):
            b = d[:, 128:256]
            return jnp.concatenate([d[:, 0:128], b * cs + _swap_halves(b * sn)], axis=-1)

        for h in range(N_HEADS):
            dq1 = unrope(dq_ref[h] * ATT_SCALE)
            dq0h, gq_rows = _rms_bwd(q0[:, HEAD_PAD * h:HEAD_PAD * (h + 1)], gq_ref[...], dq1, QK_HEAD)
            ggq = ggq + _colsum(gq_rows)
            dq0_ref[:, HEAD_PAD * h:HEAD_PAD * (h + 1)] = dq0h.astype(BF16)
            k0 = jnp.concatenate([kv[:, 256 * h:256 * h + 128], kpe], axis=-1)
            dk0, gk_rows = _rms_bwd(k0, gk_ref[...], unrope(dk_ref[h]), QK_HEAD)
            ggk = ggk + _colsum(gk_rows)
            dkpe = dkpe + dk0[:, 128:256]
            dkv_ref[:, 256 * h:256 * h + 128] = dk0[:, 0:128].astype(BF16)
            dkv_ref[:, 256 * h + 128:256 * h + 256] = dv_ref[h].astype(BF16)
        dql = _mm_nt(dq0_ref[...], wq_ref[...])
        dckn = sum(_mm_nt(dkv_ref[:, 512 * s:512 * (s + 1)], wkv_ref[s]) for s in range(4))
        dq_lat, gqa_rows = _rms_bwd(q_lat, gqa_ref[...], dql, Q_LORA)
        dc_kv, gkva_rows = _rms_bwd(c_kv, gkva_ref[...], dckn, KV_LORA)
        dlat_ref[:, 0:Q_LORA] = dq_lat.astype(BF16)
        dlat_ref[:, Q_LORA:Q_LORA + KV_LORA] = dc_kv.astype(BF16)
        dlat_ref[:, 640:768] = dkpe.astype(BF16)
        _accumulate(ggqa_ref, _colsum(gqa_rows), first)
        _accumulate(ggkva_ref, _colsum(gkva_rows), first)
        _accumulate(ggq_ref, ggq, first)
        _accumulate(ggk_ref, ggk, first)

    heads = lambda w: pl.BlockSpec((N_HEADS, t, w), lambda i: (0, i, 0))
    acc = lambda w: pl.BlockSpec((1, w), lambda i: (0, 0))
    return pl.pallas_call(
        body, name="mla_pre_bwd", grid=(l // t,),
        in_specs=[_rows(t, LAT_W), _rows(t, 1), _resident((1, 128)), _resident((1, 128)), _resident((1, Q_LORA)),
                  _resident((1, KV_LORA)), _resident((1, HEAD_PAD)), _resident((1, HEAD_PAD)),
                  _resident((Q_LORA, N_HEADS * HEAD_PAD)), _member_block("w_kv_b"),
                  heads(HEAD_PAD), heads(HEAD_PAD), heads(V_HEAD)],
        out_specs=[_rows(t, LAT_W), _rows(t, Q_LORA), _rows(t, N_HEADS * HEAD_PAD), _rows(t, KV_LORA), _rows(t, N_HEADS * 256),
                   acc(Q_LORA), acc(KV_LORA), acc(HEAD_PAD), acc(HEAD_PAD)],
        out_shape=[jax.ShapeDtypeStruct((l, LAT_W), BF16), jax.ShapeDtypeStruct((l, Q_LORA), BF16),
                   jax.ShapeDtypeStruct((l, N_HEADS * HEAD_PAD), BF16), jax.ShapeDtypeStruct((l, KV_LORA), BF16),
                   jax.ShapeDtypeStruct((l, N_HEADS * 256), BF16), jax.ShapeDtypeStruct((1, Q_LORA), F32),
                   jax.ShapeDtypeStruct((1, KV_LORA), F32), jax.ShapeDtypeStruct((1, HEAD_PAD), F32),
                   jax.ShapeDtypeStruct((1, HEAD_PAD), F32)],
        compiler_params=_cparams("arbitrary"),
    )(lat, pos, invf, sgn, gqa, gkva, gq, gk, w_qb_p, w_kvb, dq, dk, dv)


def _causal(s, transposed):
    row = lax.broadcasted_iota(jnp.int32, s.shape, 0)
    col = lax.broadcasted_iota(jnp.int32, s.shape, 1)
    keep = (row <= col) if transposed else (col <= row)
    return jnp.where(keep, s, -jnp.inf)


def _attn_fwd(q, k, v, tq):
    l = q.shape[1]

    def body(q_ref, k_ref, v_ref, o_ref, lse_ref):
        qi = pl.program_id(1)
        qv = q_ref[0]

        def step(kb, carry, masked):
            m, den, acc = carry
            rows = pl.ds(pl.multiple_of(kb * tq, tq), tq)
            s = _mm_nt(qv, k_ref[0, rows, :])
            if masked:
                s = _causal(s, False)
            m_new = jnp.maximum(m, jnp.max(s, axis=-1, keepdims=True))
            alpha = jnp.exp(m - m_new)
            p = jnp.exp(s - m_new)
            den = alpha * den + jnp.sum(p, axis=-1, keepdims=True)
            acc = alpha * acc + _mm(p, v_ref[0, rows, :])
            return m_new, den, acc

        init = (jnp.full((tq, 1), -jnp.inf, F32), jnp.zeros((tq, 1), F32), jnp.zeros((tq, V_HEAD), F32))
        carry = lax.fori_loop(0, qi, lambda kb, c: step(kb, c, False), init)
        m, den, acc = step(qi, carry, True)
        o_ref[...] = acc / den
        lse_ref[0] = m + jnp.log(den)

    return pl.pallas_call(
        body, name="attn_fwd", grid=(N_HEADS, l // tq),
        in_specs=[pl.BlockSpec((1, tq, HEAD_PAD), lambda h, i: (h, i, 0)), pl.BlockSpec((1, l, HEAD_PAD), lambda h, i: (h, 0, 0)),
                  pl.BlockSpec((1, l, V_HEAD), lambda h, i: (h, 0, 0))],
        out_specs=[pl.BlockSpec((tq, V_HEAD), lambda h, i: (i, h)), pl.BlockSpec((1, tq, 1), lambda h, i: (h, i, 0))],
        out_shape=[jax.ShapeDtypeStruct((l, N_HEADS * V_HEAD), F32), jax.ShapeDtypeStruct((N_HEADS, l, 1), F32)],
        compiler_params=_cparams("parallel", "arbitrary"),
    )(q, k, v)


def _attn_bwd_dq(q, k, v, o, do, lse, tq):
    l = q.shape[1]

    def body(q_ref, k_ref, v_ref, o_ref, do_ref, lse_ref, dq_ref, delta_ref):
        qi = pl.program_id(1)
        qv = q_ref[0]
        dov = do_ref[...]
        delta = jnp.sum(dov * o_ref[...], axis=-1, keepdims=True)
        delta_ref[0] = delta
        dob = dov.astype(BF16)
        lse = lse_ref[0]

        def step(kb, dq, masked):
            rows = pl.ds(pl.multiple_of(kb * tq, tq), tq)
            kblk = k_ref[0, rows, :]
            s = _mm_nt(qv, kblk)
            if masked:
                s = _causal(s, False)
            p = jnp.exp(s - lse)
            dp = _mm_nt(dob, v_ref[0, rows, :])
            return dq + _mm(p * (dp - delta), kblk)

        dq = lax.fori_loop(0, qi, lambda kb, c: step(kb, c, False), jnp.zeros((tq, HEAD_PAD), F32))
        dq_ref[0] = step(qi, dq, True)

    return pl.pallas_call(
        body, name="attn_bwd_dq", grid=(N_HEADS, l // tq),
        in_specs=[pl.BlockSpec((1, tq, HEAD_PAD), lambda h, i: (h, i, 0)), pl.BlockSpec((1, l, HEAD_PAD), lambda h, i: (h, 0, 0)),
                  pl.BlockSpec((1, l, V_HEAD), lambda h, i: (h, 0, 0)), pl.BlockSpec((tq, V_HEAD), lambda h, i: (i, h)),
                  pl.BlockSpec((tq, V_HEAD), lambda h, i: (i, h)), pl.BlockSpec((1, tq, 1), lambda h, i: (h, i, 0))],
        out_specs=[pl.BlockSpec((1, tq, HEAD_PAD), lambda h, i: (h, i, 0)), pl.BlockSpec((1, tq, 1), lambda h, i: (h, i, 0))],
        out_shape=[jax.ShapeDtypeStruct((N_HEADS, l, HEAD_PAD), F32), jax.ShapeDtypeStruct((N_HEADS, l, 1), F32)],
        compiler_params=_cparams("parallel", "arbitrary"),
    )(q, k, v, o, do, lse)


def _attn_bwd_dkv(q, k, v, do, lse_t, delta_t, tq):
    l = q.shape[1]
    nq = l // tq

    def body(q_ref, k_ref, v_ref, do_ref, lse_ref, delta_ref, dk_ref, dv_ref):
        ki = pl.program_id(1)
        kblk = k_ref[0]
        vblk = v_ref[0]

        def step(qb, carry, masked):
            dk, dv = carry
            rows = pl.ds(pl.multiple_of(qb * tq, tq), tq)
            qblk = q_ref[0, rows, :]
            dob = do_ref[rows, :].astype(BF16)
            st = _mm_nt(kblk, qblk)
            if masked:
                st = _causal(st, True)
            pt = jnp.exp(st - lse_ref[0, qb])
            dv = dv + _mm(pt, dob)
            dpt = _mm_nt(vblk, dob)
            dk = dk + _mm(pt * (dpt - delta_ref[0, qb]), qblk)
            return dk, dv

        carry = step(ki, (jnp.zeros((tq, HEAD_PAD), F32), jnp.zeros((tq, V_HEAD), F32)), True)
        dk, dv = lax.fori_loop(ki + 1, nq, lambda qb, c: step(qb, c, False), carry)
        dk_ref[0] = dk
        dv_ref[0] = dv

    return pl.pallas_call(
        body, name="attn_bwd_dkv", grid=(N_HEADS, nq),
        in_specs=[pl.BlockSpec((1, l, HEAD_PAD), lambda h, i: (h, 0, 0)), pl.BlockSpec((1, tq, HEAD_PAD), lambda h, i: (h, i, 0)),
                  pl.BlockSpec((1, tq, V_HEAD), lambda h, i: (h, i, 0)), pl.BlockSpec((l, V_HEAD), lambda h, i: (0, h)),
                  pl.BlockSpec((1, nq, 1, tq), lambda h, i: (h, 0, 0, 0)), pl.BlockSpec((1, nq, 1, tq), lambda h, i: (h, 0, 0, 0))],
        out_specs=[pl.BlockSpec((1, tq, HEAD_PAD), lambda h, i: (h, i, 0)), pl.BlockSpec((1, tq, V_HEAD), lambda h, i: (h, i, 0))],
        out_shape=[jax.ShapeDtypeStruct((N_HEADS, l, HEAD_PAD), F32), jax.ShapeDtypeStruct((N_HEADS, l, V_HEAD), F32)],
        compiler_params=_cparams("parallel", "arbitrary"),
    )(q, k, v, do, lse_t, delta_t)


def _row_shards_mm(a, w_ref):
    a = a.astype(BF16)
    return sum(_mm(a[:, 256 * j:256 * (j + 1)], w_ref[j]) for j in range(4))


def _row_shards_mm_nt(a, w_ref):
    a = a.astype(BF16)
    return jnp.concatenate([_mm_nt(a, w_ref[j]) for j in range(4)], axis=-1)


def _merge_fwd(attn, y_ssm, gs, gm, x, grp_a, t):
    l = x.shape[0]

    def body(attn_ref, ys_ref, gs_ref, gm_ref, x_ref, wo_ref, wout_ref, ym_ref, mixed_ref, h_ref):
        y_mla = _row_shards_mm(attn_ref[...], wo_ref)
        ym_ref[...] = y_mla
        mixed = (_sigmoid(gs_ref[...]) * ys_ref[...] + _sigmoid(gm_ref[...]) * y_mla).astype(BF16)
        mixed_ref[...] = mixed
        h_ref[...] = x_ref[...] + _row_shards_mm(mixed, wout_ref)

    r = lambda: _rows(t, D_MODEL)
    return pl.pallas_call(
        body, name="merge_fwd", grid=(l // t,),
        in_specs=[r(), r(), r(), r(), r(), _member_block("w_o_mla"), _member_block("w_out")],
        out_specs=[r(), r(), r()],
        out_shape=[jax.ShapeDtypeStruct((l, D_MODEL), F32), jax.ShapeDtypeStruct((l, D_MODEL), BF16),
                   jax.ShapeDtypeStruct((l, D_MODEL), F32)],
        compiler_params=_cparams("parallel"),
    )(attn, y_ssm, gs, gm, x, grp_a, grp_a)


def _merge_bwd(dh, y_ssm, y_mla, gs, gm, grp_a, t):
    l = dh.shape[0]

    def body(dh_ref, ys_ref, ym_ref, gs_ref, gm_ref, wo_ref, wout_ref, dys_ref, dym_ref, dgs_ref, dgm_ref, dattn_ref):
        dmixed = _row_shards_mm_nt(dh_ref[...], wout_ref)
        sg = _sigmoid(gs_ref[...])
        sm = _sigmoid(gm_ref[...])
        dys_ref[...] = (dmixed * sg).astype(BF16)
        dgs_ref[...] = (dmixed * ys_ref[...] * sg * (1.0 - sg)).astype(BF16)
        dym = (dmixed * sm).astype(BF16)
        dym_ref[...] = dym
        dgm_ref[...] = (dmixed * ym_ref[...] * sm * (1.0 - sm)).astype(BF16)
        dattn_ref[...] = _row_shards_mm_nt(dym, wo_ref)

    r = lambda: _rows(t, D_MODEL)
    bf = jax.ShapeDtypeStruct((l, D_MODEL), BF16)
    return pl.pallas_call(
        body, name="merge_bwd", grid=(l // t,),
        in_specs=[r(), r(), r(), r(), r(), _member_block("w_o_mla"), _member_block("w_out")],
        out_specs=[r(), r(), r(), r(), r()],
        out_shape=[bf, bf, bf, bf, jax.ShapeDtypeStruct((l, D_MODEL), F32)],
        compiler_params=_cparams("parallel"),
    )(dh, y_ssm, y_mla, gs, gm, grp_a, grp_a)


def _mlp_fwd_bwd(h, tgt, g2, grp_a, t):
    l = h.shape[0]

    def body(h_ref, tgt_ref, g_ref, wu_ref, wd_ref, dh_ref, hn_ref, da_ref, hid_ref, dout_ref, loss_ref, dg_ref):
        first = pl.program_id(0) == 0
        hv = h_ref[...]
        g = g_ref[...]
        hn = _rms_fwd(hv, g, D_MODEL).astype(BF16)
        hn_ref[...] = hn
        out = hv
        relus = []
        for s in range(4):
            cols = slice(1024 * s, 1024 * (s + 1))
            relu = jnp.maximum(_mm(hn, wu_ref[s]), 0.0)
            relus.append(relu)
            hid = (relu * relu).astype(BF16)
            hid_ref[:, cols] = hid
            out = out + _mm(hid, wd_ref[s])
        err = out - tgt_ref[...]
        _accumulate(loss_ref, jnp.full((8, 128), jnp.sum(err * err) * (0.5 / D_MODEL), F32), first)
        dout = err * (1.0 / D_MODEL)
        doutb = dout.astype(BF16)
        dout_ref[...] = doutb
        dhn = jnp.zeros_like(hv)
        for s in range(4):
            da = (_mm_nt(doutb, wd_ref[s]) * (2.0 * relus[s])).astype(BF16)
            da_ref[:, 1024 * s:1024 * (s + 1)] = da
            dhn = dhn + _mm_nt(da, wu_ref[s])
        dx, dg_rows = _rms_bwd(hv, g, dhn, D_MODEL)
        dh_ref[...] = dout + dx
        _accumulate(dg_ref, _colsum(dg_rows), first)

    r = lambda w: _rows(t, w)
    return pl.pallas_call(
        body, name="mlp_fwd_bwd", grid=(l // t,),
        in_specs=[r(D_MODEL), r(D_MODEL), _resident((1, D_MODEL)), _member_block("w_up"), _member_block("w_down")],
        out_specs=[r(D_MODEL), r(D_MODEL), r(D_FF), r(D_FF), r(D_MODEL), pl.BlockSpec((8, 128), lambda i: (0, 0)),
                   pl.BlockSpec((1, D_MODEL), lambda i: (0, 0))],
        out_shape=[jax.ShapeDtypeStruct((l, D_MODEL), F32), jax.ShapeDtypeStruct((l, D_MODEL), BF16),
                   jax.ShapeDtypeStruct((l, D_FF), BF16), jax.ShapeDtypeStruct((l, D_FF), BF16),
                   jax.ShapeDtypeStruct((l, D_MODEL), BF16), jax.ShapeDtypeStruct((8, 128), F32),
                   jax.ShapeDtypeStruct((1, D_MODEL), F32)],
        compiler_params=_cparams("arbitrary"),
    )(h, tgt, g2, grp_a, grp_a)


def _wgrad(a, b, name):
    l, m = a.shape
    n = b.shape[1]
    bm = m if m <= 512 else 512
    bl = min(l, 512)

    def body(a_ref, b_ref, o_ref):
        _accumulate(o_ref, _mm_tn(a_ref[...], b_ref[...]), pl.program_id(1) == 0)

    return pl.pallas_call(
        body, name=name, grid=(m // bm, l // bl),
        in_specs=[pl.BlockSpec((bl, bm), lambda i, j: (j, i)), pl.BlockSpec((bl, n), lambda i, j: (j, 0))],
        out_specs=pl.BlockSpec((bm, n), lambda i, j: (i, 0)),
        out_shape=jax.ShapeDtypeStruct((m, n), F32),
        compiler_params=_cparams("parallel", "arbitrary"),
    )(a, b)


def _wgrad_into(a, b, member, cut, dest=None):
    group, off, rs, cs = _place_in_group(member)
    l = a.shape[0]
    bm = min(rs, 512)
    bl = min(l, 512)
    nb = rs // bm
    if cut == "row":
        a_spec = pl.BlockSpec((bl, bm), lambda j, i, k: (k, j * nb + i))
        b_spec = pl.BlockSpec((bl, cs), lambda j, i, k: (k, 0))
    else:
        a_spec = pl.BlockSpec((bl, bm), lambda j, i, k: (k, i))
        b_spec = pl.BlockSpec((bl, cs), lambda j, i, k: (k, j))

    def body(a_ref, b_ref, *rest):
        o_ref = rest[-1]
        part = _mm_tn(a_ref[...], b_ref[...])

        @pl.when(pl.program_id(2) == 0)
        def _():
            o_ref[0] = part

        @pl.when(pl.program_id(2) != 0)
        def _():
            o_ref[0] += part

    operands, in_specs, aliases = [a, b], [a_spec, b_spec], {}
    if dest is not None:
        operands.append(dest)
        in_specs.append(ANY)
        aliases = {2: 0}
    return pl.pallas_call(
        body, name="wgrad_" + member, grid=(4, nb, l // bl), in_specs=in_specs,
        out_specs=pl.BlockSpec((1, bm, cs), lambda j, i, k: (j, off // bm + i, 0)),
        out_shape=jax.ShapeDtypeStruct((4, _group_rows(group), cs), F32), input_output_aliases=aliases,
        compiler_params=_cparams("parallel", "parallel", "arbitrary"),
    )(*operands)


def _adamw(w, g, m, v, name, g_off=0):
    r, c = w.shape
    br = r
    for cand in (256, 128, 64, 32, 16, 8):
        if r % cand == 0 and g_off % cand == 0:
            br = cand
            break

    def body(w_ref, g_ref, m_ref, v_ref, go_ref, d_ref, nm_ref, nv_ref):
        gv = g_ref[...]
        go_ref[...] = gv
        nm = ADAM_B1 * m_ref[...] + (1.0 - ADAM_B1) * gv
        nv = ADAM_B2 * v_ref[...] + (1.0 - ADAM_B2) * (gv * gv)
        m_hat = nm / (1.0 - ADAM_B1 ** ADAM_STEP)
        v_hat = nv / (1.0 - ADAM_B2 ** ADAM_STEP)
        d_ref[...] = -ADAM_LR * (m_hat / (jnp.sqrt(v_hat) + ADAM_EPS) + ADAM_WD * w_ref[...])
        nm_ref[...] = nm
        nv_ref[...] = nv

    spec = lambda: pl.BlockSpec((br, c), lambda i: (i, 0))
    g_spec = pl.BlockSpec((br, c), lambda i: (g_off // br + i, 0))
    shp = jax.ShapeDtypeStruct((r, c), F32)
    return pl.pallas_call(
        body, name=name, grid=(r // br,), in_specs=[spec(), g_spec, spec(), spec()],
        out_specs=[spec(), spec(), spec(), spec()], out_shape=[shp, shp, shp, shp], compiler_params=_cparams("parallel"),
    )(w, g, m, v)


def _place():
    return lax.axis_index("x"), lax.axis_index("y"), lax.axis_index("c")


def _other_chips(x, y):
    return [(1 - x, y), (x, 1 - y), (1 - x, 1 - y)]


ANY = pl.BlockSpec(memory_space=pl.ANY)


def _gather_weights(bufs):
    n = len(bufs)

    def body(*refs):
        outs, send_sems, recv_sems = refs[n:2 * n], refs[2 * n], refs[2 * n + 1]
        x, y, c = _place()
        chips = _other_chips(x, y)

        def part(g, px, py, pc):
            half = outs[g].shape[1] // 2
            return outs[g].at[2 * px + py, pl.ds(pl.multiple_of(pc * half, 16), half), :]

        def copy(k, src, dst, to):
            return pltpu.make_async_remote_copy(src_ref=src, dst_ref=dst, send_sem=send_sems.at[k], recv_sem=recv_sems.at[k],
                                                device_id=to, device_id_type=MESH)

        first = [copy(6 * g + j, part(g, x, y, c), part(g, x, y, c), (*chip, c)) for g in range(n) for j, chip in enumerate(chips)]
        for cp in first:
            cp.start()
        passed = []
        for g in range(n):
            for j, chip in enumerate(chips):
                landed = part(g, *chip, c)
                copy(6 * g + j, landed, landed, (x, y, c)).wait_recv()
                passed.append(copy(6 * g + 3 + j, landed, landed, (x, y, 1 - c)))
                passed[-1].start()
        for g in range(n):
            for j, chip in enumerate(chips):
                other = part(g, *chip, 1 - c)
                copy(6 * g + 3 + j, other, other, (x, y, c)).wait_recv()
        for cp in first + passed:
            cp.wait_send()

    return pl.pallas_call(
        body, name="gather_weights", in_specs=[ANY] * n, out_specs=[ANY] * n,
        out_shape=[jax.ShapeDtypeStruct(b.shape, b.dtype) for b in bufs], input_output_aliases={g: g for g in range(n)},
        scratch_shapes=[pltpu.SemaphoreType.DMA((6 * n,)), pltpu.SemaphoreType.DMA((6 * n,))],
    )(*bufs)


def _cast_shards(shards, group, place):
    width, members = GROUPS[group]
    rows = _group_rows(group)

    def body(place_ref, *refs):
        out = refs[-1]
        off = 0
        for ref, (_, r) in zip(refs[:-1], members):
            out[0, off:off + r, :] = ref[...].astype(BF16)
            off += r

    grid_spec = pltpu.PrefetchScalarGridSpec(
        num_scalar_prefetch=1, grid=(1,),
        in_specs=[pl.BlockSpec((r, width), lambda i, p: (0, 0)) for _, r in members],
        out_specs=pl.BlockSpec((1, rows, width), lambda i, p: (p[0], 0, 0)))
    return pl.pallas_call(
        body, name="cast_shards_" + group, grid_spec=grid_spec, out_shape=jax.ShapeDtypeStruct((4, rows, width), BF16),
        compiler_params=_cparams("arbitrary"),
    )(place, *[shards[name] for name, _ in members])


def _swap_gradient_halves(bufs):
    n = len(bufs)

    def body(*refs):
        ins, outs, send_sems, recv_sems = refs[:n], refs[n:2 * n], refs[2 * n], refs[2 * n + 1]
        x, y, c = _place()
        copies = []
        for g in range(n):
            half = ins[g].shape[1] // 2
            give = ins[g].at[:, pl.ds(pl.multiple_of((1 - c) * half, 8), half), :]
            copies.append(pltpu.make_async_remote_copy(src_ref=give, dst_ref=outs[g], send_sem=send_sems.at[g],
                                                       recv_sem=recv_sems.at[g], device_id=(x, y, 1 - c), device_id_type=MESH))
        for cp in copies:
            cp.start()
        for cp in copies:
            cp.wait()

    return pl.pallas_call(
        body, name="swap_gradient_halves", in_specs=[ANY] * n, out_specs=[ANY] * n,
        out_shape=[jax.ShapeDtypeStruct((4, b.shape[1] // 2, b.shape[2]), b.dtype) for b in bufs],
        scratch_shapes=[pltpu.SemaphoreType.DMA((n,)), pltpu.SemaphoreType.DMA((n,))],
    )(*bufs)


def _block_rows(h):
    return next(cand for cand in (256, 192, 128, 64, 32, 16) if h % cand == 0)


def _add_pair(buf, got, place, name):
    n, h, w = got.shape
    bh = _block_rows(h)
    nb = h // bh

    def body(place_ref, a_ref, b_ref, s_ref, sb_ref):
        s = a_ref[...] + b_ref[...]
        s_ref[...] = s
        sb_ref[...] = s.astype(BF16)

    spec = lambda: pl.BlockSpec((1, bh, w), lambda j, i, p: (j, i, 0))
    grid_spec = pltpu.PrefetchScalarGridSpec(
        num_scalar_prefetch=1, grid=(n, nb),
        in_specs=[pl.BlockSpec((1, bh, w), lambda j, i, p: (j, p[1] * nb + i, 0)), spec()], out_specs=[spec(), spec()])
    return pl.pallas_call(
        body, name=name, grid_spec=grid_spec,
        out_shape=[jax.ShapeDtypeStruct(got.shape, F32), jax.ShapeDtypeStruct(got.shape, BF16)],
        compiler_params=_cparams("parallel", "parallel"),
    )(place, buf, got)


def _scatter_to_chips(bufs):
    n = len(bufs)

    def body(*refs):
        ins, outs, send_sems, recv_sems = refs[:n], refs[n:2 * n], refs[2 * n], refs[2 * n + 1]
        x, y, c = _place()
        copies = [pltpu.make_async_remote_copy(src_ref=ins[g].at[2 * px + py], dst_ref=outs[g].at[j],
                                               send_sem=send_sems.at[3 * g + j], recv_sem=recv_sems.at[3 * g + j],
                                               device_id=(px, py, c), device_id_type=MESH)
                  for g in range(n) for j, (px, py) in enumerate(_other_chips(x, y))]
        for cp in copies:
            cp.start()
        for cp in copies:
            cp.wait()

    return pl.pallas_call(
        body, name="scatter_to_chips", in_specs=[ANY] * n, out_specs=[ANY] * n,
        out_shape=[jax.ShapeDtypeStruct((3,) + b.shape[1:], b.dtype) for b in bufs],
        scratch_shapes=[pltpu.SemaphoreType.DMA((3 * n,)), pltpu.SemaphoreType.DMA((3 * n,))],
    )(*bufs)


def _add_received(pair, got, place, name):
    _, h, w = pair.shape
    bh = _block_rows(h)
    nb = h // bh

    def body(place_ref, own_ref, got_ref, o_ref):
        o_ref[...] = ((own_ref[0] + got_ref[0].astype(F32)) + got_ref[1].astype(F32)) + got_ref[2].astype(F32)

    grid_spec = pltpu.PrefetchScalarGridSpec(
        num_scalar_prefetch=1, grid=(nb,),
        in_specs=[pl.BlockSpec((1, bh, w), lambda i, p: (p[0], i, 0)), pl.BlockSpec((3, bh, w), lambda i, p: (0, i, 0))],
        out_specs=pl.BlockSpec((bh, w), lambda i, p: (p[1] * nb + i, 0)))
    return pl.pallas_call(
        body, name=name, grid_spec=grid_spec, out_shape=jax.ShapeDtypeStruct((2 * h, w), F32),
        compiler_params=_cparams("parallel"),
    )(place, pair, got)


def _swap_reduced_halves(bufs):
    n = len(bufs)

    def body(*refs):
        outs, send_sems, recv_sems = refs[n:2 * n], refs[2 * n], refs[2 * n + 1]
        x, y, c = _place()
        copies = []
        for g in range(n):
            half = outs[g].shape[0] // 2
            own = outs[g].at[pl.ds(pl.multiple_of(c * half, 8), half), :]
            copies.append(pltpu.make_async_remote_copy(src_ref=own, dst_ref=own, send_sem=send_sems.at[g],
                                                       recv_sem=recv_sems.at[g], device_id=(x, y, 1 - c), device_id_type=MESH))
        for cp in copies:
            cp.start()
        for g in range(n):
            half = outs[g].shape[0] // 2
            other = outs[g].at[pl.ds(pl.multiple_of((1 - c) * half, 8), half), :]
            pltpu.make_async_remote_copy(src_ref=other, dst_ref=other, send_sem=send_sems.at[g], recv_sem=recv_sems.at[g],
                                         device_id=(x, y, 1 - c), device_id_type=MESH).wait_recv()
        for cp in copies:
            cp.wait_send()

    return pl.pallas_call(
        body, name="swap_reduced_halves", in_specs=[ANY] * n, out_specs=[ANY] * n,
        out_shape=[jax.ShapeDtypeStruct(b.shape, b.dtype) for b in bufs], input_output_aliases={g: g for g in range(n)},
        scratch_shapes=[pltpu.SemaphoreType.DMA((n,)), pltpu.SemaphoreType.DMA((n,))],
    )(*bufs)


def _all_sum_small(mine):
    rows, w = mine.shape

    def body(in_ref, out_ref, slots, send_sems, recv_sems):
        x, y, c = _place()
        me = 4 * x + 2 * y + c
        slots[me] = in_ref[...]
        copies = []
        for k in range(1, 8):
            peer = (1 - x if k & 4 else x, 1 - y if k & 2 else y, 1 - c if k & 1 else c)
            copies.append(pltpu.make_async_remote_copy(src_ref=in_ref, dst_ref=slots.at[me], send_sem=send_sems.at[k - 1],
                                                       recv_sem=recv_sems.at[k - 1], device_id=peer, device_id_type=MESH))
        for cp in copies:
            cp.start()
        for cp in copies:
            cp.wait()
        total = slots[0]
        for d in range(1, 8):
            total = total + slots[d]
        out_ref[...] = total

    return pl.pallas_call(
        body, name="all_sum_small", out_shape=jax.ShapeDtypeStruct((rows, w), F32),
        in_specs=[pl.BlockSpec(memory_space=pltpu.VMEM)], out_specs=pl.BlockSpec(memory_space=pltpu.VMEM),
        scratch_shapes=[pltpu.VMEM((8, rows, w), F32), pltpu.SemaphoreType.DMA((7,)), pltpu.SemaphoreType.DMA((7,))],
        compiler_params=pltpu.CompilerParams(vmem_limit_bytes=VMEM_LIMIT_V7X),
    )(mine)


def _join_column_shards(g):
    return jnp.transpose(g, (1, 0, 2)).reshape(g.shape[1], 4 * g.shape[2])


def _split_column_shards(w):
    r = w.shape[0]
    return jnp.transpose(w.reshape(r, 4, w.shape[1] // 4), (1, 0, 2))


def _small_rows(shape):
    return -(-int(np.prod(shape)) // 1024)


def _pack_small(vals):
    segs = []
    for name, shape in SMALL_WEIGHTS:
        flat = vals[name].reshape(-1)
        rows = _small_rows(shape)
        segs.append(jnp.pad(flat, (0, rows * 1024 - flat.shape[0])).reshape(rows, 1024))
    total = sum(s.shape[0] for s in segs)
    segs.append(jnp.zeros((-total % 8, 1024), F32))
    return jnp.concatenate(segs, axis=0)


def _unpack_small(packed):
    out, off = {}, 0
    for name, shape in SMALL_WEIGHTS:
        rows = _small_rows(shape)
        out[name] = packed[off:off + rows].reshape(-1)[:int(np.prod(shape))].reshape(shape)
        off += rows
    return out


def _pad_w_in(w):
    return jnp.concatenate([w[:, :1216], jnp.zeros((w.shape[0], 64), w.dtype), w[:, 1216:]], axis=1)


def _unpad_w_in(g):
    return jnp.concatenate([g[:, :1216], g[:, 1280:]], axis=1)


def _pad_heads(w):
    r = w.shape[0]
    return jnp.pad(w.reshape(r, N_HEADS, QK_HEAD), ((0, 0), (0, 0), (0, HEAD_PAD - QK_HEAD))).reshape(r, N_HEADS * HEAD_PAD)


def _unpad_heads(g):
    r = g.shape[0]
    return g.reshape(r, N_HEADS, HEAD_PAD)[:, :, :QK_HEAD].reshape(r, N_HEADS * QK_HEAD)


def _local_step(x, positions, tgt, grp, small):
    l = x.shape[0]
    t = min(l, 512)
    t_mlp = min(l, 256)
    tq = min(l, 512)
    tc = min(l, 256)
    row = lambda v: v.reshape(1, -1).astype(F32)

    w_in_p = _pad_w_in(_join_column_shards(grp["b"]))
    w_qb_p = _pad_heads(_join_column_shards(grp["c"]))
    g1, g2 = row(small["norm_mix"]), row(small["norm_mlp"])
    gqa, gkva = row(small["q_a_norm"]), row(small["kv_a_norm"])
    gq = jnp.pad(row(small["q_norm"]), ((0, 0), (0, HEAD_PAD - QK_HEAD)))
    gk = jnp.pad(row(small["k_norm"]), ((0, 0), (0, HEAD_PAD - QK_HEAD)))
    half = QK_ROPE // 2
    inv_freq = ROPE_THETA ** (-jnp.arange(half, dtype=F32) / half)
    invf = jnp.concatenate([inv_freq, inv_freq, jnp.zeros((64,), F32)]).reshape(1, 128)
    sgn = jnp.concatenate([-jnp.ones((half,), F32), jnp.ones((half,), F32), jnp.zeros((64,), F32)]).reshape(1, 128)
    pos = positions.reshape(l, 1)

    a_re, a_im = small["ssm_a_re"], small["ssm_a_im"]
    log_dt = small["ssm_log_dt"].reshape(SSM_GROUPS, 1)
    to_cgp = lambda b: jnp.transpose(b, (2, 0, 1)).reshape(SSM_GROUP_CH * SSM_GROUPS, SSM_STATE)
    from_cgp = lambda b: jnp.transpose(b.reshape(SSM_GROUP_CH, SSM_GROUPS, SSM_STATE), (1, 2, 0))
    bt_re, bt_im = to_cgp(small["ssm_b_re"]), to_cgp(small["ssm_b_im"])
    pow_r, pow_i, bb_re, bb_im = _ssm_param_fwd(a_re, a_im, log_dt, bt_re, bt_im)
    to_gcp = lambda b: jnp.transpose(b.reshape(SSM_GROUP_CH, SSM_GROUPS, SSM_STATE), (1, 0, 2))
    wb = jnp.concatenate([_block_diag(to_gcp(bb_re)), _block_diag(to_gcp(bb_im))], axis=1).astype(BF16)
    wc = jnp.concatenate([_block_diag(small["ssm_c_re"]).T, -_block_diag(small["ssm_c_im"]).T], axis=0).astype(BF16)
    dskip = row(small["ssm_d"])
    b_glu = row(small["b_glu"])

    u, lat, gs, gm = _in_proj_fwd(x, g1, w_in_p, t)
    xr, xi, y, y_ssm = _ssm_fwd(u, wb, wc, _scan_tables(pow_r, pow_i, False), dskip, grp["d"], b_glu, grp["e"], tc)
    q, k, v = _mla_pre_fwd(lat, pos, invf, sgn, gqa, gkva, gq, gk, w_qb_p, grp["d"], t)
    attn, lse = _attn_fwd(q, k, v, tq)
    y_mla, mixed, h = _merge_fwd(attn, y_ssm, gs, gm, x, grp["a"], t)
    dh, hn, da, hid, dout, loss_blk, g_norm_mlp = _mlp_fwd_bwd(h, tgt, g2, grp["a"], t_mlp)

    grads = {}
    grads["a"] = _wgrad_into(hn, da, "w_up", "col", _wgrad_into(hid, dout, "w_down", "row"))
    dys, dym, dgs, dgm, dattn = _merge_bwd(dh, y_ssm, y_mla, gs, gm, grp["a"], t)
    grads["a"] = _wgrad_into(attn, dym, "w_o_mla", "row", _wgrad_into(mixed, dh, "w_out", "row", grads["a"]))

    dq, delta = _attn_bwd_dq(q, k, v, attn, dattn, lse, tq)
    lanes = lambda a: a.reshape(N_HEADS, l // tq, 1, tq)
    dk, dv = _attn_bwd_dkv(q, k, v, dattn, lanes(lse), lanes(delta), tq)
    d_lat, ql, dq0, ckn, dkv, g_qa, g_kva, g_q, g_k = _mla_pre_bwd(lat, pos, invf, sgn, gqa, gkva, gq, gk, w_qb_p,
                                                                    grp["d"], dq, dk, dv, t)
    grads["c"] = _split_column_shards(_unpad_heads(_wgrad(ql, dq0, "wgrad_q_b")))

    d_u, adj, dy, z, z2, dpre, g_b_glu, g_d, g_lr, g_li = _ssm_bwd(
        dys, y, u, xr, xi, wb, wc, _scan_tables(pow_r, pow_i, True), dskip, grp["d"], b_glu, grp["e"], tc)
    grads["d"] = _wgrad_into(z, dpre, "w_glu", "row", _wgrad_into(ckn, dkv, "w_kv_b", "col"))
    grads["e"] = _wgrad_into(z2, dys, "w_o_ssm", "col")
    g_wb = _wgrad(u, adj, "wgrad_ssm_b")
    g_wcr = _wgrad(xr, dy, "wgrad_ssm_c_re")
    g_wci = _wgrad(xi, dy, "wgrad_ssm_c_im")
    to_cgp_rows = lambda m: jnp.transpose(_block_diag_take(m), (1, 0, 2)).reshape(SSM_GROUP_CH * SSM_GROUPS, SSM_STATE)
    g_ar, g_ai, g_ldt, g_btr, g_bti = _ssm_param_bwd(
        a_re, a_im, log_dt, bt_re, bt_im, g_lr.reshape(SSM_GROUPS, SSM_STATE), g_li.reshape(SSM_GROUPS, SSM_STATE),
        to_cgp_rows(g_wb[:, :GP]), to_cgp_rows(g_wb[:, GP:]))

    grad_x, xn, dproj, g_norm_mix = _in_proj_bwd(x, g1, w_in_p, d_u, d_lat, dgs, dgm, dh, t)
    grads["b"] = _split_column_shards(_unpad_w_in(_wgrad(xn, dproj, "wgrad_in")))

    g_small = {
        "norm_mix": g_norm_mix.reshape(-1), "norm_mlp": g_norm_mlp.reshape(-1), "q_a_norm": g_qa.reshape(-1),
        "kv_a_norm": g_kva.reshape(-1), "q_norm": g_q.reshape(-1)[:QK_HEAD], "k_norm": g_k.reshape(-1)[:QK_HEAD],
        "ssm_a_re": g_ar, "ssm_a_im": g_ai, "ssm_log_dt": g_ldt.reshape(-1),
        "ssm_b_re": from_cgp(g_btr), "ssm_b_im": from_cgp(g_bti),
        "ssm_c_re": _block_diag_take(g_wcr.T), "ssm_c_im": -_block_diag_take(g_wci.T),
        "ssm_d": g_d.reshape(SSM_GROUPS, SSM_GROUP_CH), "b_glu": g_b_glu.reshape(-1),
    }
    return loss_blk[0, 0], grad_x, grads, g_small


def kernel(x, positions, norm_mix, w_in, q_a_norm, kv_a_norm, w_q_b, w_kv_b, q_norm, k_norm, w_o_mla, ssm_a_re, ssm_a_im, ssm_log_dt, ssm_b_re, ssm_b_im, ssm_c_re, ssm_c_im, ssm_d, w_glu, b_glu, w_o_ssm, w_out, norm_mlp, w_up, w_down, loss_target, m_norm_mix, m_w_in, m_q_a_norm, m_kv_a_norm, m_w_q_b, m_w_kv_b, m_q_norm, m_k_norm, m_w_o_mla, m_ssm_a_re, m_ssm_a_im, m_ssm_log_dt, m_ssm_b_re, m_ssm_b_im, m_ssm_c_re, m_ssm_c_im, m_ssm_d, m_w_glu, m_b_glu, m_w_o_ssm, m_w_out, m_norm_mlp, m_w_up, m_w_down, v_norm_mix, v_w_in, v_q_a_norm, v_kv_a_norm, v_w_q_b, v_w_kv_b, v_q_norm, v_k_norm, v_w_o_mla, v_ssm_a_re, v_ssm_a_im, v_ssm_log_dt, v_ssm_b_re, v_ssm_b_im, v_ssm_c_re, v_ssm_c_im, v_ssm_d, v_w_glu, v_b_glu, v_w_o_ssm, v_w_out, v_norm_mlp, v_w_up, v_w_down):
    args = dict(locals())
    w = {n: args[n][0] for n in WEIGHT_ORDER}
    m = {n: args["m_" + n][0] for n in WEIGHT_ORDER}
    v = {n: args["v_" + n][0] for n in WEIGHT_ORDER}
    big_names = [n for n, *_ in BIG_WEIGHTS]
    small_names = [n for n, _ in SMALL_WEIGHTS]

    place = jnp.stack([2 * lax.axis_index("x") + lax.axis_index("y"), lax.axis_index("c")]).astype(jnp.int32)
    groups = sorted(GROUPS)

    gathered = _gather_weights([_cast_shards(w, g, place) for g in groups])
    grp = dict(zip(groups, gathered))
    small = {n: w[n] for n in small_names}

    loss_local, grad_x, grads, g_small = _local_step(x[0], positions[0], loss_target[0], grp, small)
    loss = lax.psum(loss_local, ("x", "y", "c"))

    bufs = [grads[g] for g in groups]
    pairs = [_add_pair(b, got, place, "add_pair_" + g) for g, b, got in zip(groups, bufs, _swap_gradient_halves(bufs))]
    landed = _scatter_to_chips([p[1] for p in pairs])
    halves = [_add_received(p[0], got, place, "add_received_" + g) for g, p, got in zip(groups, pairs, landed)]
    reduced = dict(zip(groups, _swap_reduced_halves(halves)))

    small_sum = _all_sum_small(_pack_small(g_small))

    grad_w, delta_w, new_m, new_v = {}, {}, {}, {}
    for n in big_names:
        g, off, _, _ = _place_in_group(n)
        grad_w[n], delta_w[n], new_m[n], new_v[n] = _adamw(w[n], reduced[g], m[n], v[n], "adamw_" + n, off)
    g_s, d_s, m_s, v_s = _adamw(_pack_small(small), small_sum, _pack_small({n: m[n] for n in small_names}),
                                _pack_small({n: v[n] for n in small_names}), "adamw_small")
    g_s, d_s, m_s, v_s = _unpack_small(g_s), _unpack_small(d_s), _unpack_small(m_s), _unpack_small(v_s)
    for n in small_names:
        grad_w[n], delta_w[n], new_m[n], new_v[n] = g_s[n], d_s[n], m_s[n], v_s[n]

    lead = lambda d: [d[n][None] for n in WEIGHT_ORDER]
    return (loss, grad_x[None], *lead(grad_w), *lead(delta_w), *lead(new_m), *lead(new_v))
```

```python
import functools
import math

import jax
import jax.numpy as jnp
import numpy as np
from jax import lax
from jax.experimental import pallas as pl
from jax.experimental.pallas import tpu as pltpu

F32 = jnp.float32
BF16 = jnp.bfloat16

D_MODEL = 1024
SSM_GROUPS = 32
SSM_GROUP_CH = 16
SSM_WIDTH = 512
SSM_STATE = 64
GP = SSM_GROUPS * SSM_STATE
N_HEADS = 8
QK_NOPE = 128
QK_ROPE = 64
QK_HEAD = 192
HEAD_PAD = 256
V_HEAD = 128
Q_LORA = 384
KV_LORA = 256
LAT_W = 768
D_IN = 3264
D_IN_PAD = 3328
D_FF = 4096
ROPE_THETA = 10000.0
EPS = 1e-6
ATT_SCALE = QK_HEAD ** -0.5

ADAM_LR = 0.001
ADAM_B1 = 0.9
ADAM_B2 = 0.999
ADAM_EPS = 1e-08
ADAM_WD = 0.01
ADAM_STEP = 10

VMEM_LIMIT_V7X = 56 * 1024 * 1024
MESH = pl.DeviceIdType.MESH

BIG_WEIGHTS = (
    ("w_in", 1024, 3264, "col"),
    ("w_q_b", 384, 1536, "col"),
    ("w_kv_b", 256, 2048, "col"),
    ("w_o_mla", 1024, 1024, "row"),
    ("w_glu", 512, 512, "row"),
    ("w_o_ssm", 512, 1024, "col"),
    ("w_out", 1024, 1024, "row"),
    ("w_up", 1024, 4096, "col"),
    ("w_down", 4096, 1024, "row"),
)
GROUPS = {
    "a": (1024, (("w_down", 1024), ("w_up", 1024), ("w_o_mla", 256), ("w_out", 256))),
    "b": (816, (("w_in", 1024),)),
    "c": (384, (("w_q_b", 384),)),
    "d": (512, (("w_kv_b", 256), ("w_glu", 128))),
    "e": (256, (("w_o_ssm", 512),)),
}


def _group_rows(group):
    return sum(r for _, r in GROUPS[group][1])


def _place_in_group(name):
    for group, (width, members) in GROUPS.items():
        off = 0
        for member, rows in members:
            if member == name:
                return group, off, rows, width
            off += rows
    raise KeyError(name)


SMALL_WEIGHTS = (
    ("norm_mix", (1024,)), ("q_a_norm", (384,)), ("kv_a_norm", (256,)), ("q_norm", (192,)), ("k_norm", (192,)),
    ("ssm_a_re", (32, 64)), ("ssm_a_im", (32, 64)), ("ssm_log_dt", (32,)),
    ("ssm_b_re", (32, 64, 16)), ("ssm_b_im", (32, 64, 16)), ("ssm_c_re", (32, 16, 64)), ("ssm_c_im", (32, 16, 64)),
    ("ssm_d", (32, 16)), ("b_glu", (512,)), ("norm_mlp", (1024,)),
)
WEIGHT_ORDER = ('norm_mix', 'w_in', 'q_a_norm', 'kv_a_norm', 'w_q_b', 'w_kv_b', 'q_norm', 'k_norm', 'w_o_mla', 'ssm_a_re',
                'ssm_a_im', 'ssm_log_dt', 'ssm_b_re', 'ssm_b_im', 'ssm_c_re', 'ssm_c_im', 'ssm_d', 'w_glu', 'b_glu',
                'w_o_ssm', 'w_out', 'norm_mlp', 'w_up', 'w_down')


def _cparams(*sem):
    return pltpu.CompilerParams(dimension_semantics=sem if sem else None, vmem_limit_bytes=VMEM_LIMIT_V7X)


def _resident(shape, index=None):
    index = (0,) * len(shape) if index is None else index
    return pl.BlockSpec(shape, lambda *_: index, pipeline_mode=pl.Buffered(1))


def _member_block(name):
    _, off, rows, width = _place_in_group(name)
    return _resident((4, rows, width), (0, off // rows, 0))


def _rows(t, width):
    return pl.BlockSpec((t, width), lambda i: (i, 0))


def _mm(a, b):
    return jnp.dot(a.astype(BF16), b.astype(BF16), preferred_element_type=F32)


def _mm_nt(a, b):
    return lax.dot_general(a.astype(BF16), b.astype(BF16), (((1,), (1,)), ((), ())), preferred_element_type=F32)


def _mm_tn(a, b):
    return lax.dot_general(a.astype(BF16), b.astype(BF16), (((0,), (0,)), ((), ())), preferred_element_type=F32)


def _rms_fwd(x, g, n):
    r = lax.rsqrt(jnp.sum(x * x, axis=-1, keepdims=True) * (1.0 / n) + EPS)
    return x * r * g


def _rms_bwd(x, g, dy, n):
    r = lax.rsqrt(jnp.sum(x * x, axis=-1, keepdims=True) * (1.0 / n) + EPS)
    xh = x * r
    dxh = dy * g
    dx = r * (dxh - xh * (jnp.sum(dxh * xh, axis=-1, keepdims=True) * (1.0 / n)))
    return dx, dy * xh


def _colsum(a):
    return jnp.sum(a, axis=0, keepdims=True)


def _accumulate(ref, value, first):
    @pl.when(first)
    def _():
        ref[...] = value

    @pl.when(jnp.logical_not(first))
    def _():
        ref[...] += value


def _sigmoid(a):
    return 1.0 / (1.0 + jnp.exp(-a))


GELU_C = math.sqrt(2.0 / math.pi)
GELU_A = 0.044715


def _gelu(y):
    return 0.5 * y * (1.0 + jnp.tanh(GELU_C * (y + GELU_A * y * y * y)))


def _gelu_grad(y):
    t = jnp.tanh(GELU_C * (y + GELU_A * y * y * y))
    return 0.5 * (1.0 + t) + 0.5 * y * (1.0 - t * t) * GELU_C * (1.0 + 3.0 * GELU_A * y * y)


def _in_proj_fwd(x, g1, w_in_p, t):
    l = x.shape[0]

    def body(x_ref, g_ref, w_ref, u_ref, lat_ref, gs_ref, gm_ref):
        xn = _rms_fwd(x_ref[...], g_ref[...], D_MODEL).astype(BF16)
        u_ref[...] = _mm(xn, w_ref[:, 0:512])
        lat_ref[...] = _mm(xn, w_ref[:, 512:1280])
        gs_ref[...] = _mm(xn, w_ref[:, 1280:2304])
        gm_ref[...] = _mm(xn, w_ref[:, 2304:3328])

    return pl.pallas_call(
        body, name="in_proj_fwd", grid=(l // t,),
        in_specs=[_rows(t, D_MODEL), _resident((1, D_MODEL)), _resident((D_MODEL, D_IN_PAD))],
        out_specs=[_rows(t, 512), _rows(t, LAT_W), _rows(t, D_MODEL), _rows(t, D_MODEL)],
        out_shape=[jax.ShapeDtypeStruct((l, 512), F32), jax.ShapeDtypeStruct((l, LAT_W), F32),
                   jax.ShapeDtypeStruct((l, D_MODEL), F32), jax.ShapeDtypeStruct((l, D_MODEL), F32)],
        compiler_params=_cparams("parallel"),
    )(x, g1, w_in_p)


def _in_proj_bwd(x, g1, w_in_p, d_u, d_lat, d_gs, d_gm, dh, t):
    l = x.shape[0]

    def body(x_ref, g_ref, w_ref, du_ref, dlat_ref, dgs_ref, dgm_ref, dh_ref, gx_ref, xn_ref, dproj_ref, dg_ref):
        xv = x_ref[...]
        g = g_ref[...]
        xn_ref[...] = _rms_fwd(xv, g, D_MODEL).astype(BF16)
        dproj_ref[:, 0:512] = du_ref[...]
        dproj_ref[:, 512:1280] = dlat_ref[...]
        dproj_ref[:, 1280:2304] = dgs_ref[...]
        dproj_ref[:, 2304:3328] = dgm_ref[...]
        dxn = _mm_nt(dproj_ref[...], w_ref[...])
        dx, dg_rows = _rms_bwd(xv, g, dxn, D_MODEL)
        gx_ref[...] = dh_ref[...] + dx
        _accumulate(dg_ref, _colsum(dg_rows), pl.program_id(0) == 0)

    return pl.pallas_call(
        body, name="in_proj_bwd", grid=(l // t,),
        in_specs=[_rows(t, D_MODEL), _resident((1, D_MODEL)), _resident((D_MODEL, D_IN_PAD)), _rows(t, 512),
                  _rows(t, LAT_W), _rows(t, D_MODEL), _rows(t, D_MODEL), _rows(t, D_MODEL)],
        out_specs=[_rows(t, D_MODEL), _rows(t, D_MODEL), _rows(t, D_IN_PAD), pl.BlockSpec((1, D_MODEL), lambda i: (0, 0))],
        out_shape=[jax.ShapeDtypeStruct((l, D_MODEL), F32), jax.ShapeDtypeStruct((l, D_MODEL), BF16),
                   jax.ShapeDtypeStruct((l, D_IN_PAD), BF16), jax.ShapeDtypeStruct((1, D_MODEL), F32)],
        compiler_params=_cparams("arbitrary"),
    )(x, g1, w_in_p, d_u, d_lat, d_gs, d_gm, dh)


def _ssm_param_fn(a_re, a_im, log_dt, b_re, b_im):
    dt = jnp.exp(log_dt)
    er = jnp.exp(a_re * dt)
    lr = er * jnp.cos(a_im * dt)
    li = er * jnp.sin(a_im * dt)
    den = a_re * a_re + a_im * a_im
    nr = lr - 1.0
    kr = (nr * a_re + li * a_im) / den
    ki = (li * a_re - nr * a_im) / den
    rows = lambda k: jnp.broadcast_to(k[:, None, :], (SSM_GROUPS, SSM_GROUP_CH, SSM_STATE)).reshape(SSM_WIDTH, SSM_STATE)
    krt, kit = rows(kr), rows(ki)
    return lr, li, krt * b_re - kit * b_im, krt * b_im + kit * b_re


def _state_selector():
    row = lax.broadcasted_iota(jnp.int32, (SSM_STATE, GP), 0)
    col = lax.broadcasted_iota(jnp.int32, (SSM_STATE, GP), 1)
    return jnp.where(jnp.bitwise_and(col, SSM_STATE - 1) == row, 1.0, 0.0).astype(BF16)


def _own_group(rows, rows_per_group_log2):
    row = lax.broadcasted_iota(jnp.int32, (rows, GP), 0)
    col = lax.broadcasted_iota(jnp.int32, (rows, GP), 1)
    return jnp.right_shift(row, rows_per_group_log2) == jnp.right_shift(col, 6)


def _three_bf16(x):
    hi = x.astype(BF16)
    rest = x - hi.astype(F32)
    mid = rest.astype(BF16)
    return hi, mid, (rest - mid.astype(F32)).astype(BF16)


def _spread(x, sel):
    return sum(jnp.dot(part, sel, preferred_element_type=F32) for part in _three_bf16(x))


def _collect(xw, sel):
    return sum(lax.dot_general(part, sel, (((1,), (1,)), ((), ())), preferred_element_type=F32) for part in _three_bf16(xw))


def _ssm_param_fwd(a_re, a_im, log_dt, b_re, b_im, c_re, c_im):
    def body(ar_ref, ai_ref, ldt_ref, br_ref, bi_ref, cr_ref, ci_ref, wb_ref, wct_ref, tf_ref, tr_ref):
        lr, li, bbr, bbi = _ssm_param_fn(ar_ref[...], ai_ref[...], ldt_ref[...], br_ref[...], bi_ref[...])
        sel = _state_selector()
        own16 = _own_group(SSM_WIDTH, 4)
        own1 = _own_group(SSM_GROUPS, 0)
        block = lambda m: jnp.where(own16, jnp.dot(m.astype(BF16), sel, preferred_element_type=F32), 0.0).astype(BF16)
        wb_ref[:, 0:GP] = block(bbr)
        wb_ref[:, GP:2 * GP] = block(bbi)
        wct_ref[:, 0:GP] = block(cr_ref[...])
        wct_ref[:, GP:2 * GP] = block(-ci_ref[...])
        flat = lambda m: _colsum(jnp.where(own1, _spread(m, sel), 0.0))
        pr, pi = [], []
        qr, qi = lr, li
        for _ in range(8):
            pr.append(flat(qr))
            pi.append(flat(qi))
            qr, qi = qr * lr - qi * li, qr * li + qi * lr
        row = lax.broadcasted_iota(jnp.int32, (8, GP), 0)
        for n, k in enumerate((1, 2, 4)):
            tf_ref[2 * n] = jnp.where(row >= k, pr[k - 1], 0.0)
            tf_ref[2 * n + 1] = jnp.where(row >= k, pi[k - 1], 0.0)
            tr_ref[2 * n] = jnp.where(row < 8 - k, pr[k - 1], 0.0)
            tr_ref[2 * n + 1] = jnp.where(row < 8 - k, -pi[k - 1], 0.0)
        pick = lambda vals: sum(jnp.where(row == j, v, 0.0) for j, v in enumerate(vals))
        tf_ref[6] = pick(pr)
        tf_ref[7] = pick(pi)
        tr_ref[6] = pick(pr[::-1])
        tr_ref[7] = pick([-v for v in pi[::-1]])

    return pl.pallas_call(
        body, name="ssm_param_fwd",
        out_shape=[jax.ShapeDtypeStruct((SSM_WIDTH, 2 * GP), BF16), jax.ShapeDtypeStruct((SSM_WIDTH, 2 * GP), BF16),
                   jax.ShapeDtypeStruct((8, 8, GP), F32), jax.ShapeDtypeStruct((8, 8, GP), F32)],
        compiler_params=_cparams(),
    )(a_re, a_im, log_dt, b_re, b_im, c_re, c_im)


def _ssm_param_bwd(a_re, a_im, log_dt, b_re, b_im, g_lr, g_li, g_wb, g_wct_re, g_wct_im):
    def body(ar_ref, ai_ref, ldt_ref, br_ref, bi_ref, glr_ref, gli_ref, gwb_ref, gcr_ref, gci_ref,
             o_ar, o_ai, o_ldt, o_br, o_bi, o_cr, o_ci):
        sel = _state_selector()
        own16 = _own_group(SSM_WIDTH, 4)
        own1 = _own_group(SSM_GROUPS, 0)
        blocks = lambda m: _collect(jnp.where(own16, m, 0.0), sel)
        unflat = lambda v: _collect(jnp.where(own1, v, 0.0), sel)
        _, vjp = jax.vjp(_ssm_param_fn, ar_ref[...], ai_ref[...], ldt_ref[...], br_ref[...], bi_ref[...])
        d_ar, d_ai, d_ldt, d_br, d_bi = vjp((unflat(glr_ref[...]), unflat(gli_ref[...]),
                                             blocks(gwb_ref[:, 0:GP]), blocks(gwb_ref[:, GP:2 * GP])))
        o_ar[...] = d_ar
        o_ai[...] = d_ai
        o_ldt[...] = d_ldt
        o_br[...] = d_br
        o_bi[...] = d_bi
        o_cr[...] = blocks(gcr_ref[...])
        o_ci[...] = -blocks(gci_ref[...])

    g, p = SSM_GROUPS, SSM_STATE
    gp = jax.ShapeDtypeStruct((g, p), F32)
    gcp = jax.ShapeDtypeStruct((SSM_WIDTH, p), F32)
    return pl.pallas_call(
        body, name="ssm_param_bwd", out_shape=[gp, gp, jax.ShapeDtypeStruct((g, 1), F32), gcp, gcp, gcp, gcp],
        compiler_params=_cparams(),
    )(a_re, a_im, log_dt, b_re, b_im, g_lr, g_li, g_wb, g_wct_re, g_wct_im)


SCAN_STRIP = 512


def _scan_chunk(inr_ref, ini_ref, outr_ref, outi_ref, cr_ref, ci_ref, tab_ref, tc, reverse):
    n_blocks = tc // 8

    def block(j, _):
        i = (n_blocks - 1 - j) if reverse else j
        rows = pl.ds(pl.multiple_of(i * 8, 8), 8)
        for s in range(GP // SCAN_STRIP):
            sl = pl.ds(s * SCAN_STRIP, SCAN_STRIP)
            xr = inr_ref[rows, sl]
            xi = ini_ref[rows, sl]
            for n, k in enumerate((1, 2, 4)):
                shift = (8 - k) if reverse else k
                sr = pltpu.roll(xr, shift, 0)
                si = pltpu.roll(xi, shift, 0)
                mr = tab_ref[2 * n, :, sl]
                mi = tab_ref[2 * n + 1, :, sl]
                xr, xi = xr + mr * sr - mi * si, xi + mr * si + mi * sr
            qr = tab_ref[6, :, sl]
            qi = tab_ref[7, :, sl]
            cr = cr_ref[:, sl]
            ci = ci_ref[:, sl]
            xr, xi = xr + qr * cr - qi * ci, xi + qr * ci + qi * cr
            outr_ref[rows, sl] = xr
            outi_ref[rows, sl] = xi
            edge = 0 if reverse else 7
            cr_ref[:, sl] = jnp.broadcast_to(xr[edge:edge + 1, :], (8, SCAN_STRIP))
            ci_ref[:, sl] = jnp.broadcast_to(xi[edge:edge + 1, :], (8, SCAN_STRIP))
        return 0

    lax.fori_loop(0, n_blocks, block, 0)


def _glu_pre(z, wg_ref):
    return sum(_mm(z[:, 128 * j:128 * (j + 1)], wg_ref[j]) for j in range(4))


def _ssm_fwd(u, wb, wc, tabs, dskip, grp_d, b_glu, grp_e, tc):
    l = u.shape[0]

    def body(u_ref, wb_ref, wc_ref, tab_ref, d_ref, wg_ref, bg_ref, wo_ref, xr_ref, xi_ref, y_ref, ys_ref,
             bur, bui, cr, ci):
        @pl.when(pl.program_id(0) == 0)
        def _():
            cr[...] = jnp.zeros_like(cr)
            ci[...] = jnp.zeros_like(ci)

        uv = u_ref[...]
        ub = uv.astype(BF16)
        bur[...] = _mm(ub, wb_ref[:, 0:GP])
        bui[...] = _mm(ub, wb_ref[:, GP:2 * GP])
        _scan_chunk(bur, bui, xr_ref, xi_ref, cr, ci, tab_ref, tc, False)
        y = _mm_nt(xr_ref[...], wc_ref[:, 0:GP]) + _mm_nt(xi_ref[...], wc_ref[:, GP:2 * GP]) + d_ref[...] * uv
        y_ref[...] = y
        z = _gelu(y)
        z2 = z * _sigmoid(_glu_pre(z, wg_ref) + bg_ref[...])
        for s in range(4):
            ys_ref[:, 256 * s:256 * (s + 1)] = _mm(z2, wo_ref[s])

    return pl.pallas_call(
        body, name="ssm_fwd", grid=(l // tc,),
        in_specs=[_rows(tc, 512), _resident((512, 2 * GP)), _resident((512, 2 * GP)), _resident((8, 8, GP)),
                  _resident((1, 512)), _member_block("w_glu"), _resident((1, 512)), _member_block("w_o_ssm")],
        out_specs=[_rows(tc, GP), _rows(tc, GP), _rows(tc, 512), _rows(tc, D_MODEL)],
        out_shape=[jax.ShapeDtypeStruct((l, GP), F32), jax.ShapeDtypeStruct((l, GP), F32),
                   jax.ShapeDtypeStruct((l, 512), F32), jax.ShapeDtypeStruct((l, D_MODEL), F32)],
        scratch_shapes=[pltpu.VMEM((tc, GP), F32), pltpu.VMEM((tc, GP), F32), pltpu.VMEM((8, GP), F32),
                        pltpu.VMEM((8, GP), F32)],
        compiler_params=_cparams("arbitrary"),
    )(u, wb, wc, tabs, dskip, grp_d, b_glu, grp_e)


def _ssm_bwd(dys, y, u, xr, xi, wb, wc, tabs_rev, dskip, grp_d, b_glu, grp_e, tc):
    l = u.shape[0]
    nc = l // tc

    def body(dys_ref, y_ref, u_ref, xr_ref, xi_ref, wb_ref, wc_ref, tab_ref, d_ref, wg_ref, bg_ref, wo_ref,
             du_ref, a_ref, dy_ref, z_ref, z2_ref, dpre_ref, gb_ref, gd_ref, glr_ref, gli_ref,
             dxr, dxi, ar, ai, cr, ci):
        first = pl.program_id(0) == 0

        @pl.when(first)
        def _():
            cr[...] = jnp.zeros_like(cr)
            ci[...] = jnp.zeros_like(ci)

        yv = y_ref[...]
        uv = u_ref[...]
        dz2 = sum(_mm_nt(dys_ref[:, 256 * j:256 * (j + 1)], wo_ref[j]) for j in range(4))
        z = _gelu(yv)
        s = _sigmoid(_glu_pre(z, wg_ref) + bg_ref[...])
        dpre = dz2 * z * s * (1.0 - s)
        dpreb = dpre.astype(BF16)
        dz = dz2 * s + jnp.concatenate([_mm_nt(dpreb, wg_ref[j]) for j in range(4)], axis=-1)
        dy = dz * _gelu_grad(yv)
        z_ref[...] = z.astype(BF16)
        z2_ref[...] = (z * s).astype(BF16)
        dpre_ref[...] = dpre.astype(BF16)
        dy_ref[...] = dy.astype(BF16)
        _accumulate(gb_ref, _colsum(dpre), first)
        _accumulate(gd_ref, _colsum(dy * uv), first)

        dyb = dy.astype(BF16)
        dxr[...] = _mm(dyb, wc_ref[:, 0:GP])
        dxi[...] = _mm(dyb, wc_ref[:, GP:2 * GP])
        ar[pl.ds(tc, 8), :] = cr[...]
        ai[pl.ds(tc, 8), :] = ci[...]
        _scan_chunk(dxr, dxi, ar, ai, cr, ci, tab_ref, tc, True)
        a_ref[:, 0:GP] = ar[pl.ds(0, tc), :].astype(BF16)
        a_ref[:, GP:2 * GP] = ai[pl.ds(0, tc), :].astype(BF16)
        du_ref[...] = (dy * d_ref[...] + _mm_nt(a_ref[...], wb_ref[...])).astype(BF16)
        anr = ar[pl.ds(1, tc), :]
        ani = ai[pl.ds(1, tc), :]
        xrv = xr_ref[...]
        xiv = xi_ref[...]
        _accumulate(glr_ref, _colsum(anr * xrv + ani * xiv), first)
        _accumulate(gli_ref, _colsum(ani * xrv - anr * xiv), first)

    rev = lambda w: pl.BlockSpec((tc, w), lambda i: (nc - 1 - i, 0))
    acc = lambda w: pl.BlockSpec((1, w), lambda i: (0, 0))
    return pl.pallas_call(
        body, name="ssm_bwd", grid=(nc,),
        in_specs=[rev(D_MODEL), rev(512), rev(512), rev(GP), rev(GP), _resident((512, 2 * GP)), _resident((512, 2 * GP)),
                  _resident((8, 8, GP)), _resident((1, 512)), _member_block("w_glu"), _resident((1, 512)),
                  _member_block("w_o_ssm")],
        out_specs=[rev(512), rev(2 * GP), rev(512), rev(512), rev(512), rev(512), acc(512), acc(512), acc(GP), acc(GP)],
        out_shape=[jax.ShapeDtypeStruct((l, 512), BF16), jax.ShapeDtypeStruct((l, 2 * GP), BF16),
                   jax.ShapeDtypeStruct((l, 512), BF16), jax.ShapeDtypeStruct((l, 512), BF16),
                   jax.ShapeDtypeStruct((l, 512), BF16), jax.ShapeDtypeStruct((l, 512), BF16),
                   jax.ShapeDtypeStruct((1, 512), F32), jax.ShapeDtypeStruct((1, 512), F32),
                   jax.ShapeDtypeStruct((1, GP), F32), jax.ShapeDtypeStruct((1, GP), F32)],
        scratch_shapes=[pltpu.VMEM((tc, GP), F32), pltpu.VMEM((tc, GP), F32), pltpu.VMEM((tc + 8, GP), F32),
                        pltpu.VMEM((tc + 8, GP), F32), pltpu.VMEM((8, GP), F32), pltpu.VMEM((8, GP), F32)],
        compiler_params=_cparams("arbitrary"),
    )(dys, y, u, xr, xi, wb, wc, tabs_rev, dskip, grp_d, b_glu, grp_e)


def _swap_halves(b):
    lane = lax.broadcasted_iota(jnp.int32, b.shape, 1)
    return jnp.where(lane < 32, pltpu.roll(b, 96, 1), pltpu.roll(b, 32, 1))


def _rope_tables(pos_ref, invf_ref, sgn_ref):
    ang = pos_ref[...].astype(F32) * invf_ref[...]
    return jnp.cos(ang), jnp.sin(ang) * sgn_ref[...]


def _mla_pre_fwd(lat, pos, invf, sgn, gqa, gkva, gq, gk, w_qb_p, w_kvb, t):
    l = lat.shape[0]

    def body(lat_ref, pos_ref, invf_ref, sgn_ref, gqa_ref, gkva_ref, gq_ref, gk_ref, wq_ref, wkv_ref, q_ref, k_ref, v_ref):
        cs, sn = _rope_tables(pos_ref, invf_ref, sgn_ref)
        ql = _rms_fwd(lat_ref[:, 0:Q_LORA], gqa_ref[...], Q_LORA)
        ckn = _rms_fwd(lat_ref[:, Q_LORA:Q_LORA + KV_LORA], gkva_ref[...], KV_LORA)
        kpe = lat_ref[:, 640:768]
        q0 = _mm(ql, wq_ref[...])
        cknb = ckn.astype(BF16)
        kv = jnp.concatenate([_mm(cknb, wkv_ref[s]) for s in range(4)], axis=-1)
        for h in range(N_HEADS):
            q1 = _rms_fwd(q0[:, HEAD_PAD * h:HEAD_PAD * (h + 1)], gq_ref[...], QK_HEAD)
            b = q1[:, 128:256]
            q_ref[h, :, 0:128] = (q1[:, 0:128] * ATT_SCALE).astype(BF16)
            q_ref[h, :, 128:256] = ((b * cs + _swap_halves(b) * sn) * ATT_SCALE).astype(BF16)
            k0 = jnp.concatenate([kv[:, 256 * h:256 * h + 128], kpe], axis=-1)
            k1 = _rms_fwd(k0, gk_ref[...], QK_HEAD)
            b = k1[:, 128:256]
            k_ref[h, :, 0:128] = k1[:, 0:128].astype(BF16)
            k_ref[h, :, 128:256] = (b * cs + _swap_halves(b) * sn).astype(BF16)
            v_ref[h] = kv[:, 256 * h + 128:256 * h + 256].astype(BF16)

    heads = lambda w: pl.BlockSpec((N_HEADS, t, w), lambda i: (0, i, 0))
    return pl.pallas_call(
        body, name="mla_pre_fwd", grid=(l // t,),
        in_specs=[_rows(t, LAT_W), _rows(t, 1), _resident((1, 128)), _resident((1, 128)), _resident((1, Q_LORA)),
                  _resident((1, KV_LORA)), _resident((1, HEAD_PAD)), _resident((1, HEAD_PAD)),
                  _resident((Q_LORA, N_HEADS * HEAD_PAD)), _member_block("w_kv_b")],
        out_specs=[heads(HEAD_PAD), heads(HEAD_PAD), heads(V_HEAD)],
        out_shape=[jax.ShapeDtypeStruct((N_HEADS, l, HEAD_PAD), BF16), jax.ShapeDtypeStruct((N_HEADS, l, HEAD_PAD), BF16),
                   jax.ShapeDtypeStruct((N_HEADS, l, V_HEAD), BF16)],
        compiler_params=_cparams("parallel"),
    )(lat, pos, invf, sgn, gqa, gkva, gq, gk, w_qb_p, w_kvb)


def _mla_pre_bwd(lat, pos, invf, sgn, gqa, gkva, gq, gk, w_qb_p, w_kvb, dq, dk, dv, t):
    l = lat.shape[0]

    def body(lat_ref, pos_ref, invf_ref, sgn_ref, gqa_ref, gkva_ref, gq_ref, gk_ref, wq_ref, wkv_ref, dq_ref, dk_ref, dv_ref,
             dlat_ref, ql_ref, dq0_ref, ckn_ref, dkv_ref, ggqa_ref, ggkva_ref, ggq_ref, ggk_ref):
        first = pl.program_id(0) == 0
        cs, sn = _rope_tables(pos_ref, invf_ref, sgn_ref)
        q_lat = lat_ref[:, 0:Q_LORA]
        c_kv = lat_ref[:, Q_LORA:Q_LORA + KV_LORA]
        kpe = lat_ref[:, 640:768]
        ql = _rms_fwd(q_lat, gqa_ref[...], Q_LORA)
        ckn = _rms_fwd(c_kv, gkva_ref[...], KV_LORA)
        ql_ref[...] = ql.astype(BF16)
        ckn_ref[...] = ckn.astype(BF16)
        q0 = _mm(ql, wq_ref[...])
        cknb = ckn.astype(BF16)
        kv = jnp.concatenate([_mm(cknb, wkv_ref[s]) for s in range(4)], axis=-1)
        dkpe = jnp.zeros_like(kpe)
        ggq = jnp.zeros((1, HEAD_PAD), F32)
        ggk = jnp.zeros((1, HEAD_PAD), F32)

        def unrope(d):
            b = d[:, 128:256]
            return jnp.concatenate([d[:, 0:128], b * cs + _swap_halves(b * sn)], axis=-1)

        for h in range(N_HEADS):
            dq1 = unrope(dq_ref[h] * ATT_SCALE)
            dq0h, gq_rows = _rms_bwd(q0[:, HEAD_PAD * h:HEAD_PAD * (h + 1)], gq_ref[...], dq1, QK_HEAD)
            ggq = ggq + _colsum(gq_rows)
            dq0_ref[:, HEAD_PAD * h:HEAD_PAD * (h + 1)] = dq0h.astype(BF16)
            k0 = jnp.concatenate([kv[:, 256 * h:256 * h + 128], kpe], axis=-1)
            dk0, gk_rows = _rms_bwd(k0, gk_ref[...], unrope(dk_ref[h]), QK_HEAD)
            ggk = ggk + _colsum(gk_rows)
            dkpe = dkpe + dk0[:, 128:256]
            dkv_ref[:, 256 * h:256 * h + 128] = dk0[:, 0:128].astype(BF16)
            dkv_ref[:, 256 * h + 128:256 * h + 256] = dv_ref[h].astype(BF16)
        dql = _mm_nt(dq0_ref[...], wq_ref[...])
        dckn = sum(_mm_nt(dkv_ref[:, 512 * s:512 * (s + 1)], wkv_ref[s]) for s in range(4))
        dq_lat, gqa_rows = _rms_bwd(q_lat, gqa_ref[...], dql, Q_LORA)
        dc_kv, gkva_rows = _rms_bwd(c_kv, gkva_ref[...], dckn, KV_LORA)
        dlat_ref[:, 0:Q_LORA] = dq_lat.astype(BF16)
        dlat_ref[:, Q_LORA:Q_LORA + KV_LORA] = dc_kv.astype(BF16)
        dlat_ref[:, 640:768] = dkpe.astype(BF16)
        _accumulate(ggqa_ref, _colsum(gqa_rows), first)
        _accumulate(ggkva_ref, _colsum(gkva_rows), first)
        _accumulate(ggq_ref, ggq, first)
        _accumulate(ggk_ref, ggk, first)

    heads = lambda w: pl.BlockSpec((N_HEADS, t, w), lambda i: (0, i, 0))
    acc = lambda w: pl.BlockSpec((1, w), lambda i: (0, 0))
    return pl.pallas_call(
        body, name="mla_pre_bwd", grid=(l // t,),
        in_specs=[_rows(t, LAT_W), _rows(t, 1), _resident((1, 128)), _resident((1, 128)), _resident((1, Q_LORA)),
                  _resident((1, KV_LORA)), _resident((1, HEAD_PAD)), _resident((1, HEAD_PAD)),
                  _resident((Q_LORA, N_HEADS * HEAD_PAD)), _member_block("w_kv_b"),
                  heads(HEAD_PAD), heads(HEAD_PAD), heads(V_HEAD)],
        out_specs=[_rows(t, LAT_W), _rows(t, Q_LORA), _rows(t, N_HEADS * HEAD_PAD), _rows(t, KV_LORA), _rows(t, N_HEADS * 256),
                   acc(Q_LORA), acc(KV_LORA), acc(HEAD_PAD), acc(HEAD_PAD)],
        out_shape=[jax.ShapeDtypeStruct((l, LAT_W), BF16), jax.ShapeDtypeStruct((l, Q_LORA), BF16),
                   jax.ShapeDtypeStruct((l, N_HEADS * HEAD_PAD), BF16), jax.ShapeDtypeStruct((l, KV_LORA), BF16),
                   jax.ShapeDtypeStruct((l, N_HEADS * 256), BF16), jax.ShapeDtypeStruct((1, Q_LORA), F32),
                   jax.ShapeDtypeStruct((1, KV_LORA), F32), jax.ShapeDtypeStruct((1, HEAD_PAD), F32),
                   jax.ShapeDtypeStruct((1, HEAD_PAD), F32)],
        compiler_params=_cparams("arbitrary"),
    )(lat, pos, invf, sgn, gqa, gkva, gq, gk, w_qb_p, w_kvb, dq, dk, dv)


def _causal(s, transposed):
    row = lax.broadcasted_iota(jnp.int32, s.shape, 0)
    col = lax.broadcasted_iota(jnp.int32, s.shape, 1)
    keep = (row <= col) if transposed else (col <= row)
    return jnp.where(keep, s, -jnp.inf)


def _attn_fwd(q, k, v, tq):
    l = q.shape[1]

    def body(q_ref, k_ref, v_ref, o_ref, lse_ref):
        qi = pl.program_id(1)
        qv = q_ref[0]

        def step(kb, carry, masked):
            m, den, acc = carry
            rows = pl.ds(pl.multiple_of(kb * tq, tq), tq)
            s = _mm_nt(qv, k_ref[0, rows, :])
            if masked:
                s = _causal(s, False)
            m_new = jnp.maximum(m, jnp.max(s, axis=-1, keepdims=True))
            alpha = jnp.exp(m - m_new)
            p = jnp.exp(s - m_new)
            den = alpha * den + jnp.sum(p, axis=-1, keepdims=True)
            acc = alpha * acc + _mm(p, v_ref[0, rows, :])
            return m_new, den, acc

        init = (jnp.full((tq, 1), -jnp.inf, F32), jnp.zeros((tq, 1), F32), jnp.zeros((tq, V_HEAD), F32))
        carry = lax.fori_loop(0, qi, lambda kb, c: step(kb, c, False), init)
        m, den, acc = step(qi, carry, True)
        o_ref[...] = acc / den
        lse_ref[0] = m + jnp.log(den)

    return pl.pallas_call(
        body, name="attn_fwd", grid=(N_HEADS, l // tq),
        in_specs=[pl.BlockSpec((1, tq, HEAD_PAD), lambda h, i: (h, i, 0)), pl.BlockSpec((1, l, HEAD_PAD), lambda h, i: (h, 0, 0)),
                  pl.BlockSpec((1, l, V_HEAD), lambda h, i: (h, 0, 0))],
        out_specs=[pl.BlockSpec((tq, V_HEAD), lambda h, i: (i, h)), pl.BlockSpec((1, tq, 1), lambda h, i: (h, i, 0))],
        out_shape=[jax.ShapeDtypeStruct((l, N_HEADS * V_HEAD), F32), jax.ShapeDtypeStruct((N_HEADS, l, 1), F32)],
        compiler_params=_cparams("parallel", "arbitrary"),
    )(q, k, v)


def _attn_bwd_dq(q, k, v, o, do, lse, tq):
    l = q.shape[1]

    def body(q_ref, k_ref, v_ref, o_ref, do_ref, lse_ref, dq_ref, delta_ref):
        qi = pl.program_id(1)
        qv = q_ref[0]
        dov = do_ref[...]
        delta = jnp.sum(dov * o_ref[...], axis=-1, keepdims=True)
        delta_ref[0] = delta
        dob = dov.astype(BF16)
        lse = lse_ref[0]

        def step(kb, dq, masked):
            rows = pl.ds(pl.multiple_of(kb * tq, tq), tq)
            kblk = k_ref[0, rows, :]
            s = _mm_nt(qv, kblk)
            if masked:
                s = _causal(s, False)
            p = jnp.exp(s - lse)
            dp = _mm_nt(dob, v_ref[0, rows, :])
            return dq + _mm(p * (dp - delta), kblk)

        dq = lax.fori_loop(0, qi, lambda kb, c: step(kb, c, False), jnp.zeros((tq, HEAD_PAD), F32))
        dq_ref[0] = step(qi, dq, True)

    return pl.pallas_call(
        body, name="attn_bwd_dq", grid=(N_HEADS, l // tq),
        in_specs=[pl.BlockSpec((1, tq, HEAD_PAD), lambda h, i: (h, i, 0)), pl.BlockSpec((1, l, HEAD_PAD), lambda h, i: (h, 0, 0)),
                  pl.BlockSpec((1, l, V_HEAD), lambda h, i: (h, 0, 0)), pl.BlockSpec((tq, V_HEAD), lambda h, i: (i, h)),
                  pl.BlockSpec((tq, V_HEAD), lambda h, i: (i, h)), pl.BlockSpec((1, tq, 1), lambda h, i: (h, i, 0))],
        out_specs=[pl.BlockSpec((1, tq, HEAD_PAD), lambda h, i: (h, i, 0)), pl.BlockSpec((1, tq, 1), lambda h, i: (h, i, 0))],
        out_shape=[jax.ShapeDtypeStruct((N_HEADS, l, HEAD_PAD), F32), jax.ShapeDtypeStruct((N_HEADS, l, 1), F32)],
        compiler_params=_cparams("parallel", "arbitrary"),
    )(q, k, v, o, do, lse)


def _attn_bwd_dkv(q, k, v, do, lse_t, delta_t, tq):
    l = q.shape[1]
    nq = l // tq

    def body(q_ref, k_ref, v_ref, do_ref, lse_ref, delta_ref, dk_ref, dv_ref):
        ki = pl.program_id(1)
        kblk = k_ref[0]
        vblk = v_ref[0]

        def step(qb, carry, masked):
            dk, dv = carry
            rows = pl.ds(pl.multiple_of(qb * tq, tq), tq)
            qblk = q_ref[0, rows, :]
            dob = do_ref[rows, :].astype(BF16)
            st = _mm_nt(kblk, qblk)
            if masked:
                st = _causal(st, True)
            pt = jnp.exp(st - lse_ref[0, qb])
            dv = dv + _mm(pt, dob)
            dpt = _mm_nt(vblk, dob)
            dk = dk + _mm(pt * (dpt - delta_ref[0, qb]), qblk)
            return dk, dv

        carry = step(ki, (jnp.zeros((tq, HEAD_PAD), F32), jnp.zeros((tq, V_HEAD), F32)), True)
        dk, dv = lax.fori_loop(ki + 1, nq, lambda qb, c: step(qb, c, False), carry)
        dk_ref[0] = dk
        dv_ref[0] = dv

    return pl.pallas_call(
        body, name="attn_bwd_dkv", grid=(N_HEADS, nq),
        in_specs=[pl.BlockSpec((1, l, HEAD_PAD), lambda h, i: (h, 0, 0)), pl.BlockSpec((1, tq, HEAD_PAD), lambda h, i: (h, i, 0)),
                  pl.BlockSpec((1, tq, V_HEAD), lambda h, i: (h, i, 0)), pl.BlockSpec((l, V_HEAD), lambda h, i: (0, h)),
                  pl.BlockSpec((1, nq, 1, tq), lambda h, i: (h, 0, 0, 0)), pl.BlockSpec((1, nq, 1, tq), lambda h, i: (h, 0, 0, 0))],
        out_specs=[pl.BlockSpec((1, tq, HEAD_PAD), lambda h, i: (h, i, 0)), pl.BlockSpec((1, tq, V_HEAD), lambda h, i: (h, i, 0))],
        out_shape=[jax.ShapeDtypeStruct((N_HEADS, l, HEAD_PAD), F32), jax.ShapeDtypeStruct((N_HEADS, l, V_HEAD), F32)],
        compiler_params=_cparams("parallel", "arbitrary"),
    )(q, k, v, do, lse_t, delta_t)


def _row_shards_mm(a, w_ref):
    a = a.astype(BF16)
    return sum(_mm(a[:, 256 * j:256 * (j + 1)], w_ref[j]) for j in range(4))


def _row_shards_mm_nt(a, w_ref):
    a = a.astype(BF16)
    return jnp.concatenate([_mm_nt(a, w_ref[j]) for j in range(4)], axis=-1)


def _merge_fwd(attn, y_ssm, gs, gm, x, grp_a, t):
    l = x.shape[0]

    def body(attn_ref, ys_ref, gs_ref, gm_ref, x_ref, wo_ref, wout_ref, ym_ref, mixed_ref, h_ref):
        y_mla = _row_shards_mm(attn_ref[...], wo_ref)
        ym_ref[...] = y_mla
        mixed = (_sigmoid(gs_ref[...]) * ys_ref[...] + _sigmoid(gm_ref[...]) * y_mla).astype(BF16)
        mixed_ref[...] = mixed
        h_ref[...] = x_ref[...] + _row_shards_mm(mixed, wout_ref)

    r = lambda: _rows(t, D_MODEL)
    return pl.pallas_call(
        body, name="merge_fwd", grid=(l // t,),
        in_specs=[r(), r(), r(), r(), r(), _member_block("w_o_mla"), _member_block("w_out")],
        out_specs=[r(), r(), r()],
        out_shape=[jax.ShapeDtypeStruct((l, D_MODEL), F32), jax.ShapeDtypeStruct((l, D_MODEL), BF16),
                   jax.ShapeDtypeStruct((l, D_MODEL), F32)],
        compiler_params=_cparams("parallel"),
    )(attn, y_ssm, gs, gm, x, grp_a, grp_a)


def _merge_bwd(dh, y_ssm, y_mla, gs, gm, grp_a, t):
    l = dh.shape[0]

    def body(dh_ref, ys_ref, ym_ref, gs_ref, gm_ref, wo_ref, wout_ref, dys_ref, dym_ref, dgs_ref, dgm_ref, dattn_ref):
        dmixed = _row_shards_mm_nt(dh_ref[...], wout_ref)
        sg = _sigmoid(gs_ref[...])
        sm = _sigmoid(gm_ref[...])
        dys_ref[...] = (dmixed * sg).astype(BF16)
        dgs_ref[...] = (dmixed * ys_ref[...] * sg * (1.0 - sg)).astype(BF16)
        dym = (dmixed * sm).astype(BF16)
        dym_ref[...] = dym
        dgm_ref[...] = (dmixed * ym_ref[...] * sm * (1.0 - sm)).astype(BF16)
        dattn_ref[...] = _row_shards_mm_nt(dym, wo_ref)

    r = lambda: _rows(t, D_MODEL)
    bf = jax.ShapeDtypeStruct((l, D_MODEL), BF16)
    return pl.pallas_call(
        body, name="merge_bwd", grid=(l // t,),
        in_specs=[r(), r(), r(), r(), r(), _member_block("w_o_mla"), _member_block("w_out")],
        out_specs=[r(), r(), r(), r(), r()],
        out_shape=[bf, bf, bf, bf, jax.ShapeDtypeStruct((l, D_MODEL), F32)],
        compiler_params=_cparams("parallel"),
    )(dh, y_ssm, y_mla, gs, gm, grp_a, grp_a)


def _mlp_fwd_bwd(h, tgt, g2, grp_a, t):
    l = h.shape[0]

    def body(h_ref, tgt_ref, g_ref, wu_ref, wd_ref, dh_ref, hn_ref, da_ref, hid_ref, dout_ref, loss_ref, dg_ref):
        first = pl.program_id(0) == 0
        hv = h_ref[...]
        g = g_ref[...]
        hn = _rms_fwd(hv, g, D_MODEL).astype(BF16)
        hn_ref[...] = hn
        out = hv
        relus = []
        for s in range(4):
            cols = slice(1024 * s, 1024 * (s + 1))
            relu = jnp.maximum(_mm(hn, wu_ref[s]), 0.0)
            relus.append(relu)
            hid = (relu * relu).astype(BF16)
            hid_ref[:, cols] = hid
            out = out + _mm(hid, wd_ref[s])
        err = out - tgt_ref[...]
        _accumulate(loss_ref, jnp.full((8, 128), jnp.sum(err * err) * (0.5 / D_MODEL), F32), first)
        dout = err * (1.0 / D_MODEL)
        doutb = dout.astype(BF16)
        dout_ref[...] = doutb
        dhn = jnp.zeros_like(hv)
        for s in range(4):
            da = (_mm_nt(doutb, wd_ref[s]) * (2.0 * relus[s])).astype(BF16)
            da_ref[:, 1024 * s:1024 * (s + 1)] = da
            dhn = dhn + _mm_nt(da, wu_ref[s])
        dx, dg_rows = _rms_bwd(hv, g, dhn, D_MODEL)
        dh_ref[...] = dout + dx
        _accumulate(dg_ref, _colsum(dg_rows), first)

    r = lambda w: _rows(t, w)
    return pl.pallas_call(
        body, name="mlp_fwd_bwd", grid=(l // t,),
        in_specs=[r(D_MODEL), r(D_MODEL), _resident((1, D_MODEL)), _member_block("w_up"), _member_block("w_down")],
        out_specs=[r(D_MODEL), r(D_MODEL), r(D_FF), r(D_FF), r(D_MODEL), pl.BlockSpec((8, 128), lambda i: (0, 0)),
                   pl.BlockSpec((1, D_MODEL), lambda i: (0, 0))],
        out_shape=[jax.ShapeDtypeStruct((l, D_MODEL), F32), jax.ShapeDtypeStruct((l, D_MODEL), BF16),
                   jax.ShapeDtypeStruct((l, D_FF), BF16), jax.ShapeDtypeStruct((l, D_FF), BF16),
                   jax.ShapeDtypeStruct((l, D_MODEL), BF16), jax.ShapeDtypeStruct((8, 128), F32),
                   jax.ShapeDtypeStruct((1, D_MODEL), F32)],
        compiler_params=_cparams("arbitrary"),
    )(h, tgt, g2, grp_a, grp_a)


def _wgrad(a, b, name):
    l, m = a.shape
    n = b.shape[1]
    bm = m if m <= 512 else 512
    bl = min(l, 2048 if n <= 1024 else 1024)

    def body(a_ref, b_ref, o_ref):
        _accumulate(o_ref, _mm_tn(a_ref[...], b_ref[...]), pl.program_id(1) == 0)

    return pl.pallas_call(
        body, name=name, grid=(m // bm, l // bl),
        in_specs=[pl.BlockSpec((bl, bm), lambda i, j: (j, i)), pl.BlockSpec((bl, n), lambda i, j: (j, 0))],
        out_specs=pl.BlockSpec((bm, n), lambda i, j: (i, 0)),
        out_shape=jax.ShapeDtypeStruct((m, n), F32),
        compiler_params=_cparams("parallel", "arbitrary"),
    )(a, b)


def _wgrad_into(a, b, member, cut, dest=None):
    group, off, rs, cs = _place_in_group(member)
    l = a.shape[0]
    bm = min(rs, 512)
    bl = min(l, 2048)
    nb = rs // bm
    if cut == "row":
        a_spec = pl.BlockSpec((bl, bm), lambda j, i, k: (k, j * nb + i))
        b_spec = pl.BlockSpec((bl, cs), lambda j, i, k: (k, 0))
    else:
        a_spec = pl.BlockSpec((bl, bm), lambda j, i, k: (k, i))
        b_spec = pl.BlockSpec((bl, cs), lambda j, i, k: (k, j))

    def body(a_ref, b_ref, *rest):
        o_ref = rest[-1]
        part = _mm_tn(a_ref[...], b_ref[...])

        @pl.when(pl.program_id(2) == 0)
        def _():
            o_ref[0] = part

        @pl.when(pl.program_id(2) != 0)
        def _():
            o_ref[0] += part

    operands, in_specs, aliases = [a, b], [a_spec, b_spec], {}
    if dest is not None:
        operands.append(dest)
        in_specs.append(ANY)
        aliases = {2: 0}
    return pl.pallas_call(
        body, name="wgrad_" + member, grid=(4, nb, l // bl), in_specs=in_specs,
        out_specs=pl.BlockSpec((1, bm, cs), lambda j, i, k: (j, off // bm + i, 0)),
        out_shape=jax.ShapeDtypeStruct((4, _group_rows(group), cs), F32), input_output_aliases=aliases,
        compiler_params=_cparams("parallel", "parallel", "arbitrary"),
    )(*operands)


def _adamw(w, g, m, v, name, g_off=0):
    r, c = w.shape
    br = r
    for cand in (256, 128, 64, 32, 16, 8):
        if r % cand == 0 and g_off % cand == 0:
            br = cand
            break

    def body(w_ref, g_ref, m_ref, v_ref, go_ref, d_ref, nm_ref, nv_ref):
        gv = g_ref[...]
        go_ref[...] = gv
        nm = ADAM_B1 * m_ref[...] + (1.0 - ADAM_B1) * gv
        nv = ADAM_B2 * v_ref[...] + (1.0 - ADAM_B2) * (gv * gv)
        m_hat = nm / (1.0 - ADAM_B1 ** ADAM_STEP)
        v_hat = nv / (1.0 - ADAM_B2 ** ADAM_STEP)
        d_ref[...] = -ADAM_LR * (m_hat / (jnp.sqrt(v_hat) + ADAM_EPS) + ADAM_WD * w_ref[...])
        nm_ref[...] = nm
        nv_ref[...] = nv

    spec = lambda: pl.BlockSpec((br, c), lambda i: (i, 0))
    g_spec = pl.BlockSpec((br, c), lambda i: (g_off // br + i, 0))
    shp = jax.ShapeDtypeStruct((r, c), F32)
    return pl.pallas_call(
        body, name=name, grid=(r // br,), in_specs=[spec(), g_spec, spec(), spec()],
        out_specs=[spec(), spec(), spec(), spec()], out_shape=[shp, shp, shp, shp], compiler_params=_cparams("parallel"),
    )(w, g, m, v)


def _place():
    return lax.axis_index("x"), lax.axis_index("y"), lax.axis_index("c")


def _other_chips(x, y):
    return [(1 - x, y), (x, 1 - y), (1 - x, 1 - y)]


ANY = pl.BlockSpec(memory_space=pl.ANY)


def _gather_weights(bufs):
    n = len(bufs)

    def body(*refs):
        outs, send_sems, recv_sems = refs[n:2 * n], refs[2 * n], refs[2 * n + 1]
        x, y, c = _place()
        chips = _other_chips(x, y)

        def part(g, px, py, pc):
            half = outs[g].shape[1] // 2
            return outs[g].at[2 * px + py, pl.ds(pl.multiple_of(pc * half, 16), half), :]

        def copy(k, src, dst, to):
            return pltpu.make_async_remote_copy(src_ref=src, dst_ref=dst, send_sem=send_sems.at[k], recv_sem=recv_sems.at[k],
                                                device_id=to, device_id_type=MESH)

        first = [copy(6 * g + j, part(g, x, y, c), part(g, x, y, c), (*chip, c)) for g in range(n) for j, chip in enumerate(chips)]
        for cp in first:
            cp.start()
        passed = []
        for g in range(n):
            for j, chip in enumerate(chips):
                landed = part(g, *chip, c)
                copy(6 * g + j, landed, landed, (x, y, c)).wait_recv()
                passed.append(copy(6 * g + 3 + j, landed, landed, (x, y, 1 - c)))
                passed[-1].start()
        for g in range(n):
            for j, chip in enumerate(chips):
                other = part(g, *chip, 1 - c)
                copy(6 * g + 3 + j, other, other, (x, y, c)).wait_recv()
        for cp in first + passed:
            cp.wait_send()

    return pl.pallas_call(
        body, name="gather_weights", in_specs=[ANY] * n, out_specs=[ANY] * n,
        out_shape=[jax.ShapeDtypeStruct(b.shape, b.dtype) for b in bufs], input_output_aliases={g: g for g in range(n)},
        scratch_shapes=[pltpu.SemaphoreType.DMA((6 * n,)), pltpu.SemaphoreType.DMA((6 * n,))],
    )(*bufs)


def _cast_shards(shards, group, place):
    width, members = GROUPS[group]
    rows = _group_rows(group)

    def body(place_ref, *refs):
        out = refs[-1]
        off = 0
        for ref, (_, r) in zip(refs[:-1], members):
            out[0, off:off + r, :] = ref[...].astype(BF16)
            off += r

    grid_spec = pltpu.PrefetchScalarGridSpec(
        num_scalar_prefetch=1, grid=(1,),
        in_specs=[pl.BlockSpec((r, width), lambda i, p: (0, 0)) for _, r in members],
        out_specs=pl.BlockSpec((1, rows, width), lambda i, p: (p[0], 0, 0)))
    return pl.pallas_call(
        body, name="cast_shards_" + group, grid_spec=grid_spec, out_shape=jax.ShapeDtypeStruct((4, rows, width), BF16),
        compiler_params=_cparams("arbitrary"),
    )(place, *[shards[name] for name, _ in members])


def _swap_gradient_halves(bufs):
    n = len(bufs)

    def body(*refs):
        ins, outs, send_sems, recv_sems = refs[:n], refs[n:2 * n], refs[2 * n], refs[2 * n + 1]
        x, y, c = _place()
        copies = []
        for g in range(n):
            half = ins[g].shape[1] // 2
            give = ins[g].at[:, pl.ds(pl.multiple_of((1 - c) * half, 8), half), :]
            copies.append(pltpu.make_async_remote_copy(src_ref=give, dst_ref=outs[g], send_sem=send_sems.at[g],
                                                       recv_sem=recv_sems.at[g], device_id=(x, y, 1 - c), device_id_type=MESH))
        for cp in copies:
            cp.start()
        for cp in copies:
            cp.wait()

    return pl.pallas_call(
        body, name="swap_gradient_halves", in_specs=[ANY] * n, out_specs=[ANY] * n,
        out_shape=[jax.ShapeDtypeStruct((4, b.shape[1] // 2, b.shape[2]), b.dtype) for b in bufs],
        scratch_shapes=[pltpu.SemaphoreType.DMA((n,)), pltpu.SemaphoreType.DMA((n,))],
    )(*bufs)


def _block_rows(h):
    return next(cand for cand in (256, 192, 128, 64, 32, 16) if h % cand == 0)


def _add_pair(buf, got, place, name):
    n, h, w = got.shape
    bh = _block_rows(h)
    nb = h // bh

    def body(place_ref, a_ref, b_ref, s_ref, sb_ref):
        s = a_ref[...] + b_ref[...]
        s_ref[...] = s
        sb_ref[...] = s.astype(BF16)

    spec = lambda: pl.BlockSpec((1, bh, w), lambda j, i, p: (j, i, 0))
    grid_spec = pltpu.PrefetchScalarGridSpec(
        num_scalar_prefetch=1, grid=(n, nb),
        in_specs=[pl.BlockSpec((1, bh, w), lambda j, i, p: (j, p[1] * nb + i, 0)), spec()], out_specs=[spec(), spec()])
    return pl.pallas_call(
        body, name=name, grid_spec=grid_spec,
        out_shape=[jax.ShapeDtypeStruct(got.shape, F32), jax.ShapeDtypeStruct(got.shape, BF16)],
        compiler_params=_cparams("parallel", "parallel"),
    )(place, buf, got)


def _scatter_to_chips(bufs):
    n = len(bufs)

    def body(*refs):
        ins, outs, send_sems, recv_sems = refs[:n], refs[n:2 * n], refs[2 * n], refs[2 * n + 1]
        x, y, c = _place()
        copies = [pltpu.make_async_remote_copy(src_ref=ins[g].at[2 * px + py], dst_ref=outs[g].at[j],
                                               send_sem=send_sems.at[3 * g + j], recv_sem=recv_sems.at[3 * g + j],
                                               device_id=(px, py, c), device_id_type=MESH)
                  for g in range(n) for j, (px, py) in enumerate(_other_chips(x, y))]
        for cp in copies:
            cp.start()
        for cp in copies:
            cp.wait()

    return pl.pallas_call(
        body, name="scatter_to_chips", in_specs=[ANY] * n, out_specs=[ANY] * n,
        out_shape=[jax.ShapeDtypeStruct((3,) + b.shape[1:], b.dtype) for b in bufs],
        scratch_shapes=[pltpu.SemaphoreType.DMA((3 * n,)), pltpu.SemaphoreType.DMA((3 * n,))],
    )(*bufs)


def _add_received(pair, got, place, name):
    _, h, w = pair.shape
    bh = _block_rows(h)
    nb = h // bh

    def body(place_ref, own_ref, got_ref, o_ref):
        o_ref[...] = ((own_ref[0] + got_ref[0].astype(F32)) + got_ref[1].astype(F32)) + got_ref[2].astype(F32)

    grid_spec = pltpu.PrefetchScalarGridSpec(
        num_scalar_prefetch=1, grid=(nb,),
        in_specs=[pl.BlockSpec((1, bh, w), lambda i, p: (p[0], i, 0)), pl.BlockSpec((3, bh, w), lambda i, p: (0, i, 0))],
        out_specs=pl.BlockSpec((bh, w), lambda i, p: (p[1] * nb + i, 0)))
    return pl.pallas_call(
        body, name=name, grid_spec=grid_spec, out_shape=jax.ShapeDtypeStruct((2 * h, w), F32),
        compiler_params=_cparams("parallel"),
    )(place, pair, got)


def _swap_reduced_halves(bufs):
    n = len(bufs)

    def body(*refs):
        outs, send_sems, recv_sems = refs[n:2 * n], refs[2 * n], refs[2 * n + 1]
        x, y, c = _place()
        copies = []
        for g in range(n):
            half = outs[g].shape[0] // 2
            own = outs[g].at[pl.ds(pl.multiple_of(c * half, 8), half), :]
            copies.append(pltpu.make_async_remote_copy(src_ref=own, dst_ref=own, send_sem=send_sems.at[g],
                                                       recv_sem=recv_sems.at[g], device_id=(x, y, 1 - c), device_id_type=MESH))
        for cp in copies:
            cp.start()
        for g in range(n):
            half = outs[g].shape[0] // 2
            other = outs[g].at[pl.ds(pl.multiple_of((1 - c) * half, 8), half), :]
            pltpu.make_async_remote_copy(src_ref=other, dst_ref=other, send_sem=send_sems.at[g], recv_sem=recv_sems.at[g],
                                         device_id=(x, y, 1 - c), device_id_type=MESH).wait_recv()
        for cp in copies:
            cp.wait_send()

    return pl.pallas_call(
        body, name="swap_reduced_halves", in_specs=[ANY] * n, out_specs=[ANY] * n,
        out_shape=[jax.ShapeDtypeStruct(b.shape, b.dtype) for b in bufs], input_output_aliases={g: g for g in range(n)},
        scratch_shapes=[pltpu.SemaphoreType.DMA((n,)), pltpu.SemaphoreType.DMA((n,))],
    )(*bufs)


def _all_sum_small(mine):
    rows, w = mine.shape

    def body(in_ref, out_ref, slots, send_sems, recv_sems):
        x, y, c = _place()
        me = 4 * x + 2 * y + c
        slots[me] = in_ref[...]
        copies = []
        for k in range(1, 8):
            peer = (1 - x if k & 4 else x, 1 - y if k & 2 else y, 1 - c if k & 1 else c)
            copies.append(pltpu.make_async_remote_copy(src_ref=in_ref, dst_ref=slots.at[me], send_sem=send_sems.at[k - 1],
                                                       recv_sem=recv_sems.at[k - 1], device_id=peer, device_id_type=MESH))
        for cp in copies:
            cp.start()
        for cp in copies:
            cp.wait()
        total = slots[0]
        for d in range(1, 8):
            total = total + slots[d]
        out_ref[...] = total

    return pl.pallas_call(
        body, name="all_sum_small", out_shape=jax.ShapeDtypeStruct((rows, w), F32),
        in_specs=[pl.BlockSpec(memory_space=pltpu.VMEM)], out_specs=pl.BlockSpec(memory_space=pltpu.VMEM),
        scratch_shapes=[pltpu.VMEM((8, rows, w), F32), pltpu.SemaphoreType.DMA((7,)), pltpu.SemaphoreType.DMA((7,))],
        compiler_params=pltpu.CompilerParams(vmem_limit_bytes=VMEM_LIMIT_V7X),
    )(mine)


def _join_column_shards(g):
    return jnp.transpose(g, (1, 0, 2)).reshape(g.shape[1], 4 * g.shape[2])


def _split_column_shards(w):
    r = w.shape[0]
    return jnp.transpose(w.reshape(r, 4, w.shape[1] // 4), (1, 0, 2))


def _small_rows(shape):
    return -(-int(np.prod(shape)) // 1024)


def _pack_small(vals):
    segs = []
    for name, shape in SMALL_WEIGHTS:
        flat = vals[name].reshape(-1)
        segs.append(jnp.pad(flat, (0, _small_rows(shape) * 1024 - flat.shape[0])))
    total = sum(s.shape[0] for s in segs) // 1024
    segs.append(jnp.zeros((-total % 8 * 1024,), F32))
    return jnp.concatenate(segs).reshape(-1, 1024)


def _unpack_small(packed):
    out, off = {}, 0
    for name, shape in SMALL_WEIGHTS:
        rows = _small_rows(shape)
        out[name] = packed[off:off + rows].reshape(-1)[:int(np.prod(shape))].reshape(shape)
        off += rows
    return out


def _pad_w_in(w):
    return jnp.concatenate([w[:, :1216], jnp.zeros((w.shape[0], 64), w.dtype), w[:, 1216:]], axis=1)


def _unpad_w_in(g):
    return jnp.concatenate([g[:, :1216], g[:, 1280:]], axis=1)


def _pad_heads(w):
    r = w.shape[0]
    return jnp.pad(w.reshape(r, N_HEADS, QK_HEAD), ((0, 0), (0, 0), (0, HEAD_PAD - QK_HEAD))).reshape(r, N_HEADS * HEAD_PAD)


def _unpad_heads(g):
    r = g.shape[0]
    return g.reshape(r, N_HEADS, HEAD_PAD)[:, :, :QK_HEAD].reshape(r, N_HEADS * QK_HEAD)


def _local_step(x, positions, tgt, grp, small):
    l = x.shape[0]
    t = min(l, 512)
    t_mlp = min(l, 256)
    tq = min(l, 512)
    tc = min(l, 256)
    row = lambda v: v.reshape(1, -1).astype(F32)

    w_in_p = _pad_w_in(_join_column_shards(grp["b"]))
    w_qb_p = _pad_heads(_join_column_shards(grp["c"]))
    g1, g2 = row(small["norm_mix"]), row(small["norm_mlp"])
    gqa, gkva = row(small["q_a_norm"]), row(small["kv_a_norm"])
    gq = jnp.pad(row(small["q_norm"]), ((0, 0), (0, HEAD_PAD - QK_HEAD)))
    gk = jnp.pad(row(small["k_norm"]), ((0, 0), (0, HEAD_PAD - QK_HEAD)))
    half = QK_ROPE // 2
    inv_freq = ROPE_THETA ** (-jnp.arange(half, dtype=F32) / half)
    invf = jnp.concatenate([inv_freq, inv_freq, jnp.zeros((64,), F32)]).reshape(1, 128)
    sgn = jnp.concatenate([-jnp.ones((half,), F32), jnp.ones((half,), F32), jnp.zeros((64,), F32)]).reshape(1, 128)
    pos = positions.reshape(l, 1)

    a_re, a_im = small["ssm_a_re"], small["ssm_a_im"]
    log_dt = small["ssm_log_dt"].reshape(SSM_GROUPS, 1)
    to_gcp = lambda b: jnp.transpose(b, (0, 2, 1)).reshape(SSM_WIDTH, SSM_STATE)
    from_gcp = lambda b: jnp.transpose(b.reshape(SSM_GROUPS, SSM_GROUP_CH, SSM_STATE), (0, 2, 1))
    b_re, b_im = to_gcp(small["ssm_b_re"]), to_gcp(small["ssm_b_im"])
    c_re, c_im = small["ssm_c_re"].reshape(SSM_WIDTH, SSM_STATE), small["ssm_c_im"].reshape(SSM_WIDTH, SSM_STATE)
    wb, wc, tabs_fwd, tabs_rev = _ssm_param_fwd(a_re, a_im, log_dt, b_re, b_im, c_re, c_im)
    dskip = row(small["ssm_d"])
    b_glu = row(small["b_glu"])

    u, lat, gs, gm = _in_proj_fwd(x, g1, w_in_p, t)
    xr, xi, y, y_ssm = _ssm_fwd(u, wb, wc, tabs_fwd, dskip, grp["d"], b_glu, grp["e"], tc)
    q, k, v = _mla_pre_fwd(lat, pos, invf, sgn, gqa, gkva, gq, gk, w_qb_p, grp["d"], t)
    attn, lse = _attn_fwd(q, k, v, tq)
    y_mla, mixed, h = _merge_fwd(attn, y_ssm, gs, gm, x, grp["a"], t)
    dh, hn, da, hid, dout, loss_blk, g_norm_mlp = _mlp_fwd_bwd(h, tgt, g2, grp["a"], t_mlp)

    grads = {}
    grads["a"] = _wgrad_into(hn, da, "w_up", "col", _wgrad_into(hid, dout, "w_down", "row"))
    dys, dym, dgs, dgm, dattn = _merge_bwd(dh, y_ssm, y_mla, gs, gm, grp["a"], t)
    grads["a"] = _wgrad_into(attn, dym, "w_o_mla", "row", _wgrad_into(mixed, dh, "w_out", "row", grads["a"]))

    dq, delta = _attn_bwd_dq(q, k, v, attn, dattn, lse, tq)
    lanes = lambda a: a.reshape(N_HEADS, l // tq, 1, tq)
    dk, dv = _attn_bwd_dkv(q, k, v, dattn, lanes(lse), lanes(delta), tq)
    d_lat, ql, dq0, ckn, dkv, g_qa, g_kva, g_q, g_k = _mla_pre_bwd(lat, pos, invf, sgn, gqa, gkva, gq, gk, w_qb_p,
                                                                    grp["d"], dq, dk, dv, t)
    grads["c"] = _split_column_shards(_unpad_heads(_wgrad(ql, dq0, "wgrad_q_b")))

    d_u, adj, dy, z, z2, dpre, g_b_glu, g_d, g_lr, g_li = _ssm_bwd(
        dys, y, u, xr, xi, wb, wc, tabs_rev, dskip, grp["d"], b_glu, grp["e"], tc)
    grads["d"] = _wgrad_into(z, dpre, "w_glu", "row", _wgrad_into(ckn, dkv, "w_kv_b", "col"))
    grads["e"] = _wgrad_into(z2, dys, "w_o_ssm", "col")
    g_ar, g_ai, g_ldt, g_br, g_bi, g_cr, g_ci = _ssm_param_bwd(
        a_re, a_im, log_dt, b_re, b_im, g_lr, g_li, _wgrad(u, adj, "wgrad_ssm_b"), _wgrad(dy, xr, "wgrad_ssm_c_re"),
        _wgrad(dy, xi, "wgrad_ssm_c_im"))

    grad_x, xn, dproj, g_norm_mix = _in_proj_bwd(x, g1, w_in_p, d_u, d_lat, dgs, dgm, dh, t)
    grads["b"] = _split_column_shards(_unpad_w_in(_wgrad(xn, dproj, "wgrad_in")))

    g_small = {
        "norm_mix": g_norm_mix.reshape(-1), "norm_mlp": g_norm_mlp.reshape(-1), "q_a_norm": g_qa.reshape(-1),
        "kv_a_norm": g_kva.reshape(-1), "q_norm": g_q.reshape(-1)[:QK_HEAD], "k_norm": g_k.reshape(-1)[:QK_HEAD],
        "ssm_a_re": g_ar, "ssm_a_im": g_ai, "ssm_log_dt": g_ldt.reshape(-1),
        "ssm_b_re": from_gcp(g_br), "ssm_b_im": from_gcp(g_bi),
        "ssm_c_re": g_cr.reshape(SSM_GROUPS, SSM_GROUP_CH, SSM_STATE), "ssm_c_im": g_ci.reshape(SSM_GROUPS, SSM_GROUP_CH, SSM_STATE),
        "ssm_d": g_d.reshape(SSM_GROUPS, SSM_GROUP_CH), "b_glu": g_b_glu.reshape(-1),
    }
    return loss_blk[0, 0], grad_x, grads, g_small


def kernel(x, positions, norm_mix, w_in, q_a_norm, kv_a_norm, w_q_b, w_kv_b, q_norm, k_norm, w_o_mla, ssm_a_re, ssm_a_im, ssm_log_dt, ssm_b_re, ssm_b_im, ssm_c_re, ssm_c_im, ssm_d, w_glu, b_glu, w_o_ssm, w_out, norm_mlp, w_up, w_down, loss_target, m_norm_mix, m_w_in, m_q_a_norm, m_kv_a_norm, m_w_q_b, m_w_kv_b, m_q_norm, m_k_norm, m_w_o_mla, m_ssm_a_re, m_ssm_a_im, m_ssm_log_dt, m_ssm_b_re, m_ssm_b_im, m_ssm_c_re, m_ssm_c_im, m_ssm_d, m_w_glu, m_b_glu, m_w_o_ssm, m_w_out, m_norm_mlp, m_w_up, m_w_down, v_norm_mix, v_w_in, v_q_a_norm, v_kv_a_norm, v_w_q_b, v_w_kv_b, v_q_norm, v_k_norm, v_w_o_mla, v_ssm_a_re, v_ssm_a_im, v_ssm_log_dt, v_ssm_b_re, v_ssm_b_im, v_ssm_c_re, v_ssm_c_im, v_ssm_d, v_w_glu, v_b_glu, v_w_o_ssm, v_w_out, v_norm_mlp, v_w_up, v_w_down):
    args = dict(locals())
    w = {n: args[n][0] for n in WEIGHT_ORDER}
    m = {n: args["m_" + n][0] for n in WEIGHT_ORDER}
    v = {n: args["v_" + n][0] for n in WEIGHT_ORDER}
    big_names = [n for n, *_ in BIG_WEIGHTS]
    small_names = [n for n, _ in SMALL_WEIGHTS]

    place = jnp.stack([2 * lax.axis_index("x") + lax.axis_index("y"), lax.axis_index("c")]).astype(jnp.int32)
    groups = sorted(GROUPS)

    gathered = _gather_weights([_cast_shards(w, g, place) for g in groups])
    grp = dict(zip(groups, gathered))
    small = {n: w[n] for n in small_names}

    loss_local, grad_x, grads, g_small = _local_step(x[0], positions[0], loss_target[0], grp, small)
    loss = lax.psum(loss_local, ("x", "y", "c"))

    bufs = [grads[g] for g in groups]
    pairs = [_add_pair(b, got, place, "add_pair_" + g) for g, b, got in zip(groups, bufs, _swap_gradient_halves(bufs))]
    landed = _scatter_to_chips([p[1] for p in pairs])
    halves = [_add_received(p[0], got, place, "add_received_" + g) for g, p, got in zip(groups, pairs, landed)]
    reduced = dict(zip(groups, _swap_reduced_halves(halves)))

    small_sum = _all_sum_small(_pack_small(g_small))

    grad_w, delta_w, new_m, new_v = {}, {}, {}, {}
    for n in big_names:
        g, off, _, _ = _place_in_group(n)
        grad_w[n], delta_w[n], new_m[n], new_v[n] = _adamw(w[n], reduced[g], m[n], v[n], "adamw_" + n, off)
    g_s, d_s, m_s, v_s = _adamw(_pack_small(small), small_sum, _pack_small({n: m[n] for n in small_names}),
                                _pack_small({n: v[n] for n in small_names}), "adamw_small")
    g_s, d_s, m_s, v_s = _unpack_small(g_s), _unpack_small(d_s), _unpack_small(m_s), _unpack_small(v_s)
    for n in small_names:
        grad_w[n], delta_w[n], new_m[n], new_v[n] = g_s[n], d_s[n], m_s[n], v_s[n]

    lead = lambda d: [d[n][None] for n in WEIGHT_ORDER]
    return (loss, grad_x[None], *lead(grad_w), *lead(delta_w), *lead(new_m), *lead(new_v))
```

```python
import functools
import math

import jax
import jax.numpy as jnp
import numpy as np
from jax import lax
from jax.experimental import pallas as pl
from jax.experimental.pallas import tpu as pltpu

F32 = jnp.float32
BF16 = jnp.bfloat16

D_MODEL = 1024
SSM_GROUPS = 32
SSM_GROUP_CH = 16
SSM_WIDTH = 512
SSM_STATE = 64
GP = SSM_GROUPS * SSM_STATE
N_HEADS = 8
QK_NOPE = 128
QK_ROPE = 64
QK_HEAD = 192
HEAD_PAD = 256
V_HEAD = 128
Q_LORA = 384
KV_LORA = 256
LAT_W = 768
D_IN = 3264
D_IN_PAD = 3328
D_FF = 4096
ROPE_THETA = 10000.0
EPS = 1e-6
ATT_SCALE = QK_HEAD ** -0.5

ADAM_LR = 0.001
ADAM_B1 = 0.9
ADAM_B2 = 0.999
ADAM_EPS = 1e-08
ADAM_WD = 0.01
ADAM_STEP = 10

VMEM_LIMIT_V7X = 56 * 1024 * 1024
MESH = pl.DeviceIdType.MESH

BIG_WEIGHTS = (
    ("w_in", 1024, 3264, "col"),
    ("w_q_b", 384, 1536, "col"),
    ("w_kv_b", 256, 2048, "col"),
    ("w_o_mla", 1024, 1024, "row"),
    ("w_glu", 512, 512, "row"),
    ("w_o_ssm", 512, 1024, "col"),
    ("w_out", 1024, 1024, "row"),
    ("w_up", 1024, 4096, "col"),
    ("w_down", 4096, 1024, "row"),
)
GROUPS = {
    "a": (1024, (("w_down", 1024), ("w_up", 1024), ("w_o_mla", 256), ("w_out", 256))),
    "b": (816, (("w_in", 1024),)),
    "c": (384, (("w_q_b", 384),)),
    "d": (512, (("w_kv_b", 256), ("w_glu", 128))),
    "e": (256, (("w_o_ssm", 512),)),
}


def _group_rows(group):
    return sum(r for _, r in GROUPS[group][1])


def _place_in_group(name):
    for group, (width, members) in GROUPS.items():
        off = 0
        for member, rows in members:
            if member == name:
                return group, off, rows, width
            off += rows
    raise KeyError(name)


SMALL_WEIGHTS = (
    ("norm_mix", (1024,)), ("q_a_norm", (384,)), ("kv_a_norm", (256,)), ("q_norm", (192,)), ("k_norm", (192,)),
    ("ssm_a_re", (32, 64)), ("ssm_a_im", (32, 64)), ("ssm_log_dt", (32,)),
    ("ssm_b_re", (32, 64, 16)), ("ssm_b_im", (32, 64, 16)), ("ssm_c_re", (32, 16, 64)), ("ssm_c_im", (32, 16, 64)),
    ("ssm_d", (32, 16)), ("b_glu", (512,)), ("norm_mlp", (1024,)),
)
WEIGHT_ORDER = ('norm_mix', 'w_in', 'q_a_norm', 'kv_a_norm', 'w_q_b', 'w_kv_b', 'q_norm', 'k_norm', 'w_o_mla', 'ssm_a_re',
                'ssm_a_im', 'ssm_log_dt', 'ssm_b_re', 'ssm_b_im', 'ssm_c_re', 'ssm_c_im', 'ssm_d', 'w_glu', 'b_glu',
                'w_o_ssm', 'w_out', 'norm_mlp', 'w_up', 'w_down')


def _cparams(*sem):
    return pltpu.CompilerParams(dimension_semantics=sem if sem else None, vmem_limit_bytes=VMEM_LIMIT_V7X)


def _resident(shape, index=None):
    index = (0,) * len(shape) if index is None else index
    return pl.BlockSpec(shape, lambda *_: index, pipeline_mode=pl.Buffered(1))


def _member_block(name):
    _, off, rows, width = _place_in_group(name)
    return _resident((4, rows, width), (0, off // rows, 0))


def _rows(t, width):
    return pl.BlockSpec((t, width), lambda i: (i, 0))


def _mm(a, b):
    return jnp.dot(a.astype(BF16), b.astype(BF16), preferred_element_type=F32)


def _mm_nt(a, b):
    return lax.dot_general(a.astype(BF16), b.astype(BF16), (((1,), (1,)), ((), ())), preferred_element_type=F32)


def _mm_tn(a, b):
    return lax.dot_general(a.astype(BF16), b.astype(BF16), (((0,), (0,)), ((), ())), preferred_element_type=F32)


def _rms_fwd(x, g, n):
    r = lax.rsqrt(jnp.sum(x * x, axis=-1, keepdims=True) * (1.0 / n) + EPS)
    return x * r * g


def _rms_bwd(x, g, dy, n):
    r = lax.rsqrt(jnp.sum(x * x, axis=-1, keepdims=True) * (1.0 / n) + EPS)
    xh = x * r
    dxh = dy * g
    dx = r * (dxh - xh * (jnp.sum(dxh * xh, axis=-1, keepdims=True) * (1.0 / n)))
    return dx, dy * xh


def _colsum(a):
    return jnp.sum(a, axis=0, keepdims=True)


def _accumulate(ref, value, first):
    @pl.when(first)
    def _():
        ref[...] = value

    @pl.when(jnp.logical_not(first))
    def _():
        ref[...] += value


def _sigmoid(a):
    return 1.0 / (1.0 + jnp.exp(-a))


GELU_C = math.sqrt(2.0 / math.pi)
GELU_A = 0.044715


def _gelu(y):
    return 0.5 * y * (1.0 + jnp.tanh(GELU_C * (y + GELU_A * y * y * y)))


def _gelu_grad(y):
    t = jnp.tanh(GELU_C * (y + GELU_A * y * y * y))
    return 0.5 * (1.0 + t) + 0.5 * y * (1.0 - t * t) * GELU_C * (1.0 + 3.0 * GELU_A * y * y)


def _in_proj_fwd(x, g1, w_in_p, t):
    l = x.shape[0]

    def body(x_ref, g_ref, w_ref, u_ref, lat_ref, gs_ref, gm_ref):
        xn = _rms_fwd(x_ref[...], g_ref[...], D_MODEL).astype(BF16)
        u_ref[...] = _mm(xn, w_ref[:, 0:512])
        lat_ref[...] = _mm(xn, w_ref[:, 512:1280])
        gs_ref[...] = _mm(xn, w_ref[:, 1280:2304])
        gm_ref[...] = _mm(xn, w_ref[:, 2304:3328])

    return pl.pallas_call(
        body, name="in_proj_fwd", grid=(l // t,),
        in_specs=[_rows(t, D_MODEL), _resident((1, D_MODEL)), _resident((D_MODEL, D_IN_PAD))],
        out_specs=[_rows(t, 512), _rows(t, LAT_W), _rows(t, D_MODEL), _rows(t, D_MODEL)],
        out_shape=[jax.ShapeDtypeStruct((l, 512), F32), jax.ShapeDtypeStruct((l, LAT_W), F32),
                   jax.ShapeDtypeStruct((l, D_MODEL), F32), jax.ShapeDtypeStruct((l, D_MODEL), F32)],
        compiler_params=_cparams("parallel"),
    )(x, g1, w_in_p)


def _in_proj_bwd(x, g1, w_in_p, d_u, d_lat, d_gs, d_gm, dh, t):
    l = x.shape[0]

    def body(x_ref, g_ref, w_ref, du_ref, dlat_ref, dgs_ref, dgm_ref, dh_ref, gx_ref, xn_ref, dproj_ref, dg_ref):
        xv = x_ref[...]
        g = g_ref[...]
        xn_ref[...] = _rms_fwd(xv, g, D_MODEL).astype(BF16)
        dproj_ref[:, 0:512] = du_ref[...]
        dproj_ref[:, 512:1280] = dlat_ref[...]
        dproj_ref[:, 1280:2304] = dgs_ref[...]
        dproj_ref[:, 2304:3328] = dgm_ref[...]
        dxn = _mm_nt(dproj_ref[...], w_ref[...])
        dx, dg_rows = _rms_bwd(xv, g, dxn, D_MODEL)
        gx_ref[...] = dh_ref[...] + dx
        _accumulate(dg_ref, _colsum(dg_rows), pl.program_id(0) == 0)

    return pl.pallas_call(
        body, name="in_proj_bwd", grid=(l // t,),
        in_specs=[_rows(t, D_MODEL), _resident((1, D_MODEL)), _resident((D_MODEL, D_IN_PAD)), _rows(t, 512),
                  _rows(t, LAT_W), _rows(t, D_MODEL), _rows(t, D_MODEL), _rows(t, D_MODEL)],
        out_specs=[_rows(t, D_MODEL), _rows(t, D_MODEL), _rows(t, D_IN_PAD), pl.BlockSpec((1, D_MODEL), lambda i: (0, 0))],
        out_shape=[jax.ShapeDtypeStruct((l, D_MODEL), F32), jax.ShapeDtypeStruct((l, D_MODEL), BF16),
                   jax.ShapeDtypeStruct((l, D_IN_PAD), BF16), jax.ShapeDtypeStruct((1, D_MODEL), F32)],
        compiler_params=_cparams("arbitrary"),
    )(x, g1, w_in_p, d_u, d_lat, d_gs, d_gm, dh)


def _ssm_param_fn(a_re, a_im, log_dt, b_re, b_im):
    dt = jnp.exp(log_dt)
    er = jnp.exp(a_re * dt)
    lr = er * jnp.cos(a_im * dt)
    li = er * jnp.sin(a_im * dt)
    den = a_re * a_re + a_im * a_im
    nr = lr - 1.0
    kr = (nr * a_re + li * a_im) / den
    ki = (li * a_re - nr * a_im) / den
    rows = lambda k: jnp.broadcast_to(k[:, None, :], (SSM_GROUPS, SSM_GROUP_CH, SSM_STATE)).reshape(SSM_WIDTH, SSM_STATE)
    krt, kit = rows(kr), rows(ki)
    return lr, li, krt * b_re - kit * b_im, krt * b_im + kit * b_re


def _state_selector():
    row = lax.broadcasted_iota(jnp.int32, (SSM_STATE, GP), 0)
    col = lax.broadcasted_iota(jnp.int32, (SSM_STATE, GP), 1)
    return jnp.where(jnp.bitwise_and(col, SSM_STATE - 1) == row, 1.0, 0.0).astype(BF16)


def _own_group(rows, rows_per_group_log2):
    row = lax.broadcasted_iota(jnp.int32, (rows, GP), 0)
    col = lax.broadcasted_iota(jnp.int32, (rows, GP), 1)
    return jnp.right_shift(row, rows_per_group_log2) == jnp.right_shift(col, 6)


def _three_bf16(x):
    hi = x.astype(BF16)
    rest = x - hi.astype(F32)
    mid = rest.astype(BF16)
    return hi, mid, (rest - mid.astype(F32)).astype(BF16)


def _spread(x, sel):
    return sum(jnp.dot(part, sel, preferred_element_type=F32) for part in _three_bf16(x))


def _collect(xw, sel):
    return sum(lax.dot_general(part, sel, (((1,), (1,)), ((), ())), preferred_element_type=F32) for part in _three_bf16(xw))


def _ssm_param_fwd(a_re, a_im, log_dt, b_re, b_im, c_re, c_im):
    def body(ar_ref, ai_ref, ldt_ref, br_ref, bi_ref, cr_ref, ci_ref, wb_ref, wct_ref, tf_ref, tr_ref):
        lr, li, bbr, bbi = _ssm_param_fn(ar_ref[...], ai_ref[...], ldt_ref[...], br_ref[...], bi_ref[...])
        sel = _state_selector()
        own16 = _own_group(SSM_WIDTH, 4)
        own1 = _own_group(SSM_GROUPS, 0)
        block = lambda m: jnp.where(own16, jnp.dot(m.astype(BF16), sel, preferred_element_type=F32), 0.0).astype(BF16)
        wb_ref[:, 0:GP] = block(bbr)
        wb_ref[:, GP:2 * GP] = block(bbi)
        wct_ref[:, 0:GP] = block(cr_ref[...])
        wct_ref[:, GP:2 * GP] = block(-ci_ref[...])
        flat = lambda m: _colsum(jnp.where(own1, _spread(m, sel), 0.0))
        pr, pi = [], []
        qr, qi = lr, li
        for _ in range(8):
            pr.append(flat(qr))
            pi.append(flat(qi))
            qr, qi = qr * lr - qi * li, qr * li + qi * lr
        row = lax.broadcasted_iota(jnp.int32, (8, GP), 0)
        for n, k in enumerate((1, 2, 4)):
            tf_ref[2 * n] = jnp.where(row >= k, pr[k - 1], 0.0)
            tf_ref[2 * n + 1] = jnp.where(row >= k, pi[k - 1], 0.0)
            tr_ref[2 * n] = jnp.where(row < 8 - k, pr[k - 1], 0.0)
            tr_ref[2 * n + 1] = jnp.where(row < 8 - k, -pi[k - 1], 0.0)
        pick = lambda vals: sum(jnp.where(row == j, v, 0.0) for j, v in enumerate(vals))
        tf_ref[6] = pick(pr)
        tf_ref[7] = pick(pi)
        tr_ref[6] = pick(pr[::-1])
        tr_ref[7] = pick([-v for v in pi[::-1]])

    return pl.pallas_call(
        body, name="ssm_param_fwd",
        out_shape=[jax.ShapeDtypeStruct((SSM_WIDTH, 2 * GP), BF16), jax.ShapeDtypeStruct((SSM_WIDTH, 2 * GP), BF16),
                   jax.ShapeDtypeStruct((8, 8, GP), F32), jax.ShapeDtypeStruct((8, 8, GP), F32)],
        compiler_params=_cparams(),
    )(a_re, a_im, log_dt, b_re, b_im, c_re, c_im)


def _ssm_param_bwd(a_re, a_im, log_dt, b_re, b_im, g_lr, g_li, g_wb, g_wct_re, g_wct_im):
    def body(ar_ref, ai_ref, ldt_ref, br_ref, bi_ref, glr_ref, gli_ref, gwb_ref, gcr_ref, gci_ref,
             o_ar, o_ai, o_ldt, o_br, o_bi, o_cr, o_ci):
        sel = _state_selector()
        own16 = _own_group(SSM_WIDTH, 4)
        own1 = _own_group(SSM_GROUPS, 0)
        blocks = lambda m: _collect(jnp.where(own16, m, 0.0), sel)
        unflat = lambda v: _collect(jnp.where(own1, v, 0.0), sel)
        _, vjp = jax.vjp(_ssm_param_fn, ar_ref[...], ai_ref[...], ldt_ref[...], br_ref[...], bi_ref[...])
        d_ar, d_ai, d_ldt, d_br, d_bi = vjp((unflat(glr_ref[...]), unflat(gli_ref[...]),
                                             blocks(gwb_ref[:, 0:GP]), blocks(gwb_ref[:, GP:2 * GP])))
        o_ar[...] = d_ar
        o_ai[...] = d_ai
        o_ldt[...] = d_ldt
        o_br[...] = d_br
        o_bi[...] = d_bi
        o_cr[...] = blocks(gcr_ref[...])
        o_ci[...] = -blocks(gci_ref[...])

    g, p = SSM_GROUPS, SSM_STATE
    gp = jax.ShapeDtypeStruct((g, p), F32)
    gcp = jax.ShapeDtypeStruct((SSM_WIDTH, p), F32)
    return pl.pallas_call(
        body, name="ssm_param_bwd", out_shape=[gp, gp, jax.ShapeDtypeStruct((g, 1), F32), gcp, gcp, gcp, gcp],
        compiler_params=_cparams(),
    )(a_re, a_im, log_dt, b_re, b_im, g_lr, g_li, g_wb, g_wct_re, g_wct_im)


SCAN_STRIP = 512


def _scan_chunk(inr_ref, ini_ref, outr_ref, outi_ref, cr_ref, ci_ref, tab_ref, tc, reverse):
    n_blocks = tc // 8

    def block(j, _):
        i = (n_blocks - 1 - j) if reverse else j
        rows = pl.ds(pl.multiple_of(i * 8, 8), 8)
        for s in range(GP // SCAN_STRIP):
            sl = pl.ds(s * SCAN_STRIP, SCAN_STRIP)
            xr = inr_ref[rows, sl]
            xi = ini_ref[rows, sl]
            for n, k in enumerate((1, 2, 4)):
                shift = (8 - k) if reverse else k
                sr = pltpu.roll(xr, shift, 0)
                si = pltpu.roll(xi, shift, 0)
                mr = tab_ref[2 * n, :, sl]
                mi = tab_ref[2 * n + 1, :, sl]
                xr, xi = xr + mr * sr - mi * si, xi + mr * si + mi * sr
            qr = tab_ref[6, :, sl]
            qi = tab_ref[7, :, sl]
            cr = cr_ref[:, sl]
            ci = ci_ref[:, sl]
            xr, xi = xr + qr * cr - qi * ci, xi + qr * ci + qi * cr
            outr_ref[rows, sl] = xr
            outi_ref[rows, sl] = xi
            edge = 0 if reverse else 7
            cr_ref[:, sl] = jnp.broadcast_to(xr[edge:edge + 1, :], (8, SCAN_STRIP))
            ci_ref[:, sl] = jnp.broadcast_to(xi[edge:edge + 1, :], (8, SCAN_STRIP))
        return 0

    lax.fori_loop(0, n_blocks, block, 0)


def _glu_pre(z, wg_ref):
    return sum(_mm(z[:, 128 * j:128 * (j + 1)], wg_ref[j]) for j in range(4))


def _ssm_fwd(u, wb, wc, tabs, dskip, grp_d, b_glu, grp_e, tc):
    l = u.shape[0]

    def body(u_ref, wb_ref, wc_ref, tab_ref, d_ref, wg_ref, bg_ref, wo_ref, xr_ref, xi_ref, y_ref, ys_ref,
             bur, bui, cr, ci):
        @pl.when(pl.program_id(0) == 0)
        def _():
            cr[...] = jnp.zeros_like(cr)
            ci[...] = jnp.zeros_like(ci)

        uv = u_ref[...]
        ub = uv.astype(BF16)
        bur[...] = _mm(ub, wb_ref[:, 0:GP])
        bui[...] = _mm(ub, wb_ref[:, GP:2 * GP])
        _scan_chunk(bur, bui, xr_ref, xi_ref, cr, ci, tab_ref, tc, False)
        y = _mm_nt(xr_ref[...], wc_ref[:, 0:GP]) + _mm_nt(xi_ref[...], wc_ref[:, GP:2 * GP]) + d_ref[...] * uv
        y_ref[...] = y
        z = _gelu(y)
        z2 = z * _sigmoid(_glu_pre(z, wg_ref) + bg_ref[...])
        for s in range(4):
            ys_ref[:, 256 * s:256 * (s + 1)] = _mm(z2, wo_ref[s])

    return pl.pallas_call(
        body, name="ssm_fwd", grid=(l // tc,),
        in_specs=[_rows(tc, 512), _resident((512, 2 * GP)), _resident((512, 2 * GP)), _resident((8, 8, GP)),
                  _resident((1, 512)), _member_block("w_glu"), _resident((1, 512)), _member_block("w_o_ssm")],
        out_specs=[_rows(tc, GP), _rows(tc, GP), _rows(tc, 512), _rows(tc, D_MODEL)],
        out_shape=[jax.ShapeDtypeStruct((l, GP), F32), jax.ShapeDtypeStruct((l, GP), F32),
                   jax.ShapeDtypeStruct((l, 512), F32), jax.ShapeDtypeStruct((l, D_MODEL), F32)],
        scratch_shapes=[pltpu.VMEM((tc, GP), F32), pltpu.VMEM((tc, GP), F32), pltpu.VMEM((8, GP), F32),
                        pltpu.VMEM((8, GP), F32)],
        compiler_params=_cparams("arbitrary"),
    )(u, wb, wc, tabs, dskip, grp_d, b_glu, grp_e)


def _ssm_bwd(dys, y, u, xr, xi, wb, wc, tabs_rev, dskip, grp_d, b_glu, grp_e, tc):
    l = u.shape[0]
    nc = l // tc

    def body(dys_ref, y_ref, u_ref, xr_ref, xi_ref, wb_ref, wc_ref, tab_ref, d_ref, wg_ref, bg_ref, wo_ref,
             du_ref, a_ref, dy_ref, z_ref, z2_ref, dpre_ref, gb_ref, gd_ref, glr_ref, gli_ref,
             dxr, dxi, ar, ai, cr, ci):
        first = pl.program_id(0) == 0

        @pl.when(first)
        def _():
            cr[...] = jnp.zeros_like(cr)
            ci[...] = jnp.zeros_like(ci)

        yv = y_ref[...]
        uv = u_ref[...]
        dz2 = sum(_mm_nt(dys_ref[:, 256 * j:256 * (j + 1)], wo_ref[j]) for j in range(4))
        z = _gelu(yv)
        s = _sigmoid(_glu_pre(z, wg_ref) + bg_ref[...])
        dpre = dz2 * z * s * (1.0 - s)
        dpreb = dpre.astype(BF16)
        dz = dz2 * s + jnp.concatenate([_mm_nt(dpreb, wg_ref[j]) for j in range(4)], axis=-1)
        dy = dz * _gelu_grad(yv)
        z_ref[...] = z.astype(BF16)
        z2_ref[...] = (z * s).astype(BF16)
        dpre_ref[...] = dpre.astype(BF16)
        dy_ref[...] = dy.astype(BF16)
        _accumulate(gb_ref, _colsum(dpre), first)
        _accumulate(gd_ref, _colsum(dy * uv), first)

        dyb = dy.astype(BF16)
        dxr[...] = _mm(dyb, wc_ref[:, 0:GP])
        dxi[...] = _mm(dyb, wc_ref[:, GP:2 * GP])
        ar[pl.ds(tc, 8), :] = cr[...]
        ai[pl.ds(tc, 8), :] = ci[...]
        _scan_chunk(dxr, dxi, ar, ai, cr, ci, tab_ref, tc, True)
        a_ref[:, 0:GP] = ar[pl.ds(0, tc), :].astype(BF16)
        a_ref[:, GP:2 * GP] = ai[pl.ds(0, tc), :].astype(BF16)
        du_ref[...] = (dy * d_ref[...] + _mm_nt(a_ref[...], wb_ref[...])).astype(BF16)
        anr = ar[pl.ds(1, tc), :]
        ani = ai[pl.ds(1, tc), :]
        xrv = xr_ref[...]
        xiv = xi_ref[...]
        _accumulate(glr_ref, _colsum(anr * xrv + ani * xiv), first)
        _accumulate(gli_ref, _colsum(ani * xrv - anr * xiv), first)

    rev = lambda w: pl.BlockSpec((tc, w), lambda i: (nc - 1 - i, 0))
    acc = lambda w: pl.BlockSpec((1, w), lambda i: (0, 0))
    return pl.pallas_call(
        body, name="ssm_bwd", grid=(nc,),
        in_specs=[rev(D_MODEL), rev(512), rev(512), rev(GP), rev(GP), _resident((512, 2 * GP)), _resident((512, 2 * GP)),
                  _resident((8, 8, GP)), _resident((1, 512)), _member_block("w_glu"), _resident((1, 512)),
                  _member_block("w_o_ssm")],
        out_specs=[rev(512), rev(2 * GP), rev(512), rev(512), rev(512), rev(512), acc(512), acc(512), acc(GP), acc(GP)],
        out_shape=[jax.ShapeDtypeStruct((l, 512), BF16), jax.ShapeDtypeStruct((l, 2 * GP), BF16),
                   jax.ShapeDtypeStruct((l, 512), BF16), jax.ShapeDtypeStruct((l, 512), BF16),
                   jax.ShapeDtypeStruct((l, 512), BF16), jax.ShapeDtypeStruct((l, 512), BF16),
                   jax.ShapeDtypeStruct((1, 512), F32), jax.ShapeDtypeStruct((1, 512), F32),
                   jax.ShapeDtypeStruct((1, GP), F32), jax.ShapeDtypeStruct((1, GP), F32)],
        scratch_shapes=[pltpu.VMEM((tc, GP), F32), pltpu.VMEM((tc, GP), F32), pltpu.VMEM((tc + 8, GP), F32),
                        pltpu.VMEM((tc + 8, GP), F32), pltpu.VMEM((8, GP), F32), pltpu.VMEM((8, GP), F32)],
        compiler_params=_cparams("arbitrary"),
    )(dys, y, u, xr, xi, wb, wc, tabs_rev, dskip, grp_d, b_glu, grp_e)


def _swap_halves(b):
    lane = lax.broadcasted_iota(jnp.int32, b.shape, 1)
    return jnp.where(lane < 32, pltpu.roll(b, 96, 1), pltpu.roll(b, 32, 1))


def _rope_tables(pos_ref, invf_ref, sgn_ref):
    ang = pos_ref[...].astype(F32) * invf_ref[...]
    return jnp.cos(ang), jnp.sin(ang) * sgn_ref[...]


def _mla_pre_fwd(lat, pos, invf, sgn, gqa, gkva, gq, gk, w_qb_p, w_kvb, t):
    l = lat.shape[0]

    def body(lat_ref, pos_ref, invf_ref, sgn_ref, gqa_ref, gkva_ref, gq_ref, gk_ref, wq_ref, wkv_ref, q_ref, k_ref, v_ref):
        cs, sn = _rope_tables(pos_ref, invf_ref, sgn_ref)
        ql = _rms_fwd(lat_ref[:, 0:Q_LORA], gqa_ref[...], Q_LORA)
        ckn = _rms_fwd(lat_ref[:, Q_LORA:Q_LORA + KV_LORA], gkva_ref[...], KV_LORA)
        kpe = lat_ref[:, 640:768]
        q0 = _mm(ql, wq_ref[...])
        cknb = ckn.astype(BF16)
        kv = jnp.concatenate([_mm(cknb, wkv_ref[s]) for s in range(4)], axis=-1)
        for h in range(N_HEADS):
            q1 = _rms_fwd(q0[:, HEAD_PAD * h:HEAD_PAD * (h + 1)], gq_ref[...], QK_HEAD)
            b = q1[:, 128:256]
            q_ref[h, :, 0:128] = (q1[:, 0:128] * ATT_SCALE).astype(BF16)
            q_ref[h, :, 128:256] = ((b * cs + _swap_halves(b) * sn) * ATT_SCALE).astype(BF16)
            k0 = jnp.concatenate([kv[:, 256 * h:256 * h + 128], kpe], axis=-1)
            k1 = _rms_fwd(k0, gk_ref[...], QK_HEAD)
            b = k1[:, 128:256]
            k_ref[h, :, 0:128] = k1[:, 0:128].astype(BF16)
            k_ref[h, :, 128:256] = (b * cs + _swap_halves(b) * sn).astype(BF16)
            v_ref[h] = kv[:, 256 * h + 128:256 * h + 256].astype(BF16)

    heads = lambda w: pl.BlockSpec((N_HEADS, t, w), lambda i: (0, i, 0))
    return pl.pallas_call(
        body, name="mla_pre_fwd", grid=(l // t,),
        in_specs=[_rows(t, LAT_W), _rows(t, 1), _resident((1, 128)), _resident((1, 128)), _resident((1, Q_LORA)),
                  _resident((1, KV_LORA)), _resident((1, HEAD_PAD)), _resident((1, HEAD_PAD)),
                  _resident((Q_LORA, N_HEADS * HEAD_PAD)), _member_block("w_kv_b")],
        out_specs=[heads(HEAD_PAD), heads(HEAD_PAD), heads(V_HEAD)],
        out_shape=[jax.ShapeDtypeStruct((N_HEADS, l, HEAD_PAD), BF16), jax.ShapeDtypeStruct((N_HEADS, l, HEAD_PAD), BF16),
                   jax.ShapeDtypeStruct((N_HEADS, l, V_HEAD), BF16)],
        compiler_params=_cparams("parallel"),
    )(lat, pos, invf, sgn, gqa, gkva, gq, gk, w_qb_p, w_kvb)


def _mla_pre_bwd(lat, pos, invf, sgn, gqa, gkva, gq, gk, w_qb_p, w_kvb, dq, dk, dv, t):
    l = lat.shape[0]

    def body(lat_ref, pos_ref, invf_ref, sgn_ref, gqa_ref, gkva_ref, gq_ref, gk_ref, wq_ref, wkv_ref, dq_ref, dk_ref, dv_ref,
             dlat_ref, ql_ref, dq0_ref, ckn_ref, dkv_ref, ggqa_ref, ggkva_ref, ggq_ref, ggk_ref):
        first = pl.program_id(0) == 0
        cs, sn = _rope_tables(pos_ref, invf_ref, sgn_ref)
        q_lat = lat_ref[:, 0:Q_LORA]
        c_kv = lat_ref[:, Q_LORA:Q_LORA + KV_LORA]
        kpe = lat_ref[:, 640:768]
        ql = _rms_fwd(q_lat, gqa_ref[...], Q_LORA)
        ckn = _rms_fwd(c_kv, gkva_ref[...], KV_LORA)
        ql_ref[...] = ql.astype(BF16)
        ckn_ref[...] = ckn.astype(BF16)
        q0 = _mm(ql, wq_ref[...])
        cknb = ckn.astype(BF16)
        kv = jnp.concatenate([_mm(cknb, wkv_ref[s]) for s in range(4)], axis=-1)
        dkpe = jnp.zeros_like(kpe)
        ggq = jnp.zeros((1, HEAD_PAD), F32)
        ggk = jnp.zeros((1, HEAD_PAD), F32)

        def unrope(d):
            b = d[:, 128:256]
            return jnp.concatenate([d[:, 0:128], b * cs + _swap_halves(b * sn)], axis=-1)

        for h in range(N_HEADS):
            dq1 = unrope(dq_ref[h] * ATT_SCALE)
            dq0h, gq_rows = _rms_bwd(q0[:, HEAD_PAD * h:HEAD_PAD * (h + 1)], gq_ref[...], dq1, QK_HEAD)
            ggq = ggq + _colsum(gq_rows)
            dq0_ref[:, HEAD_PAD * h:HEAD_PAD * (h + 1)] = dq0h.astype(BF16)
            k0 = jnp.concatenate([kv[:, 256 * h:256 * h + 128], kpe], axis=-1)
            dk0, gk_rows = _rms_bwd(k0, gk_ref[...], unrope(dk_ref[h]), QK_HEAD)
            ggk = ggk + _colsum(gk_rows)
            dkpe = dkpe + dk0[:, 128:256]
            dkv_ref[:, 256 * h:256 * h + 128] = dk0[:, 0:128].astype(BF16)
            dkv_ref[:, 256 * h + 128:256 * h + 256] = dv_ref[h].astype(BF16)
        dql = _mm_nt(dq0_ref[...], wq_ref[...])
        dckn = sum(_mm_nt(dkv_ref[:, 512 * s:512 * (s + 1)], wkv_ref[s]) for s in range(4))
        dq_lat, gqa_rows = _rms_bwd(q_lat, gqa_ref[...], dql, Q_LORA)
        dc_kv, gkva_rows = _rms_bwd(c_kv, gkva_ref[...], dckn, KV_LORA)
        dlat_ref[:, 0:Q_LORA] = dq_lat.astype(BF16)
        dlat_ref[:, Q_LORA:Q_LORA + KV_LORA] = dc_kv.astype(BF16)
        dlat_ref[:, 640:768] = dkpe.astype(BF16)
        _accumulate(ggqa_ref, _colsum(gqa_rows), first)
        _accumulate(ggkva_ref, _colsum(gkva_rows), first)
        _accumulate(ggq_ref, ggq, first)
        _accumulate(ggk_ref, ggk, first)

    heads = lambda w: pl.BlockSpec((N_HEADS, t, w), lambda i: (0, i, 0))
    acc = lambda w: pl.BlockSpec((1, w), lambda i: (0, 0))
    return pl.pallas_call(
        body, name="mla_pre_bwd", grid=(l // t,),
        in_specs=[_rows(t, LAT_W), _rows(t, 1), _resident((1, 128)), _resident((1, 128)), _resident((1, Q_LORA)),
                  _resident((1, KV_LORA)), _resident((1, HEAD_PAD)), _resident((1, HEAD_PAD)),
                  _resident((Q_LORA, N_HEADS * HEAD_PAD)), _member_block("w_kv_b"),
                  heads(HEAD_PAD), heads(HEAD_PAD), heads(V_HEAD)],
        out_specs=[_rows(t, LAT_W), _rows(t, Q_LORA), _rows(t, N_HEADS * HEAD_PAD), _rows(t, KV_LORA), _rows(t, N_HEADS * 256),
                   acc(Q_LORA), acc(KV_LORA), acc(HEAD_PAD), acc(HEAD_PAD)],
        out_shape=[jax.ShapeDtypeStruct((l, LAT_W), BF16), jax.ShapeDtypeStruct((l, Q_LORA), BF16),
                   jax.ShapeDtypeStruct((l, N_HEADS * HEAD_PAD), BF16), jax.ShapeDtypeStruct((l, KV_LORA), BF16),
                   jax.ShapeDtypeStruct((l, N_HEADS * 256), BF16), jax.ShapeDtypeStruct((1, Q_LORA), F32),
                   jax.ShapeDtypeStruct((1, KV_LORA), F32), jax.ShapeDtypeStruct((1, HEAD_PAD), F32),
                   jax.ShapeDtypeStruct((1, HEAD_PAD), F32)],
        compiler_params=_cparams("arbitrary"),
    )(lat, pos, invf, sgn, gqa, gkva, gq, gk, w_qb_p, w_kvb, dq, dk, dv)


def _causal(s, transposed):
    row = lax.broadcasted_iota(jnp.int32, s.shape, 0)
    col = lax.broadcasted_iota(jnp.int32, s.shape, 1)
    keep = (row <= col) if transposed else (col <= row)
    return jnp.where(keep, s, -jnp.inf)


def _attn_fwd(q, k, v, tq):
    l = q.shape[1]

    def body(q_ref, k_ref, v_ref, o_ref, lse_ref):
        qi = pl.program_id(1)
        qv = q_ref[0]

        def step(kb, carry, masked):
            m, den, acc = carry
            rows = pl.ds(pl.multiple_of(kb * tq, tq), tq)
            s = _mm_nt(qv, k_ref[0, rows, :])
            if masked:
                s = _causal(s, False)
            m_new = jnp.maximum(m, jnp.max(s, axis=-1, keepdims=True))
            alpha = jnp.exp(m - m_new)
            p = jnp.exp(s - m_new)
            den = alpha * den + jnp.sum(p, axis=-1, keepdims=True)
            acc = alpha * acc + _mm(p, v_ref[0, rows, :])
            return m_new, den, acc

        init = (jnp.full((tq, 1), -jnp.inf, F32), jnp.zeros((tq, 1), F32), jnp.zeros((tq, V_HEAD), F32))
        carry = lax.fori_loop(0, qi, lambda kb, c: step(kb, c, False), init)
        m, den, acc = step(qi, carry, True)
        o_ref[...] = acc / den
        lse_ref[0] = m + jnp.log(den)

    return pl.pallas_call(
        body, name="attn_fwd", grid=(N_HEADS, l // tq),
        in_specs=[pl.BlockSpec((1, tq, HEAD_PAD), lambda h, i: (h, i, 0)), pl.BlockSpec((1, l, HEAD_PAD), lambda h, i: (h, 0, 0)),
                  pl.BlockSpec((1, l, V_HEAD), lambda h, i: (h, 0, 0))],
        out_specs=[pl.BlockSpec((tq, V_HEAD), lambda h, i: (i, h)), pl.BlockSpec((1, tq, 1), lambda h, i: (h, i, 0))],
        out_shape=[jax.ShapeDtypeStruct((l, N_HEADS * V_HEAD), F32), jax.ShapeDtypeStruct((N_HEADS, l, 1), F32)],
        compiler_params=_cparams("parallel", "arbitrary"),
    )(q, k, v)


def _attn_bwd_dq(q, k, v, o, do, lse, tq):
    l = q.shape[1]

    def body(q_ref, k_ref, v_ref, o_ref, do_ref, lse_ref, dq_ref, delta_ref):
        qi = pl.program_id(1)
        qv = q_ref[0]
        dov = do_ref[...]
        delta = jnp.sum(dov * o_ref[...], axis=-1, keepdims=True)
        delta_ref[0] = delta
        dob = dov.astype(BF16)
        lse = lse_ref[0]

        def step(kb, dq, masked):
            rows = pl.ds(pl.multiple_of(kb * tq, tq), tq)
            kblk = k_ref[0, rows, :]
            s = _mm_nt(qv, kblk)
            if masked:
                s = _causal(s, False)
            p = jnp.exp(s - lse)
            dp = _mm_nt(dob, v_ref[0, rows, :])
            return dq + _mm(p * (dp - delta), kblk)

        dq = lax.fori_loop(0, qi, lambda kb, c: step(kb, c, False), jnp.zeros((tq, HEAD_PAD), F32))
        dq_ref[0] = step(qi, dq, True)

    return pl.pallas_call(
        body, name="attn_bwd_dq", grid=(N_HEADS, l // tq),
        in_specs=[pl.BlockSpec((1, tq, HEAD_PAD), lambda h, i: (h, i, 0)), pl.BlockSpec((1, l, HEAD_PAD), lambda h, i: (h, 0, 0)),
                  pl.BlockSpec((1, l, V_HEAD), lambda h, i: (h, 0, 0)), pl.BlockSpec((tq, V_HEAD), lambda h, i: (i, h)),
                  pl.BlockSpec((tq, V_HEAD), lambda h, i: (i, h)), pl.BlockSpec((1, tq, 1), lambda h, i: (h, i, 0))],
        out_specs=[pl.BlockSpec((1, tq, HEAD_PAD), lambda h, i: (h, i, 0)), pl.BlockSpec((1, tq, 1), lambda h, i: (h, i, 0))],
        out_shape=[jax.ShapeDtypeStruct((N_HEADS, l, HEAD_PAD), F32), jax.ShapeDtypeStruct((N_HEADS, l, 1), F32)],
        compiler_params=_cparams("parallel", "arbitrary"),
    )(q, k, v, o, do, lse)


def _attn_bwd_dkv(q, k, v, do, lse_t, delta_t, tq):
    l = q.shape[1]
    nq = l // tq

    def body(q_ref, k_ref, v_ref, do_ref, lse_ref, delta_ref, dk_ref, dv_ref):
        ki = pl.program_id(1)
        kblk = k_ref[0]
        vblk = v_ref[0]

        def step(qb, carry, masked):
            dk, dv = carry
            rows = pl.ds(pl.multiple_of(qb * tq, tq), tq)
            qblk = q_ref[0, rows, :]
            dob = do_ref[rows, :].astype(BF16)
            st = _mm_nt(kblk, qblk)
            if masked:
                st = _causal(st, True)
            pt = jnp.exp(st - lse_ref[0, qb])
            dv = dv + _mm(pt, dob)
            dpt = _mm_nt(vblk, dob)
            dk = dk + _mm(pt * (dpt - delta_ref[0, qb]), qblk)
            return dk, dv

        carry = step(ki, (jnp.zeros((tq, HEAD_PAD), F32), jnp.zeros((tq, V_HEAD), F32)), True)
        dk, dv = lax.fori_loop(ki + 1, nq, lambda qb, c: step(qb, c, False), carry)
        dk_ref[0] = dk
        dv_ref[0] = dv

    return pl.pallas_call(
        body, name="attn_bwd_dkv", grid=(N_HEADS, nq),
        in_specs=[pl.BlockSpec((1, l, HEAD_PAD), lambda h, i: (h, 0, 0)), pl.BlockSpec((1, tq, HEAD_PAD), lambda h, i: (h, i, 0)),
                  pl.BlockSpec((1, tq, V_HEAD), lambda h, i: (h, i, 0)), pl.BlockSpec((l, V_HEAD), lambda h, i: (0, h)),
                  pl.BlockSpec((1, nq, 1, tq), lambda h, i: (h, 0, 0, 0)), pl.BlockSpec((1, nq, 1, tq), lambda h, i: (h, 0, 0, 0))],
        out_specs=[pl.BlockSpec((1, tq, HEAD_PAD), lambda h, i: (h, i, 0)), pl.BlockSpec((1, tq, V_HEAD), lambda h, i: (h, i, 0))],
        out_shape=[jax.ShapeDtypeStruct((N_HEADS, l, HEAD_PAD), F32), jax.ShapeDtypeStruct((N_HEADS, l, V_HEAD), F32)],
        compiler_params=_cparams("parallel", "arbitrary"),
    )(q, k, v, do, lse_t, delta_t)


def _row_shards_mm(a, w_ref):
    a = a.astype(BF16)
    return sum(_mm(a[:, 256 * j:256 * (j + 1)], w_ref[j]) for j in range(4))


def _row_shards_mm_nt(a, w_ref):
    a = a.astype(BF16)
    return jnp.concatenate([_mm_nt(a, w_ref[j]) for j in range(4)], axis=-1)


def _merge_fwd(attn, y_ssm, gs, gm, x, grp_a, t):
    l = x.shape[0]

    def body(attn_ref, ys_ref, gs_ref, gm_ref, x_ref, wo_ref, wout_ref, ym_ref, mixed_ref, h_ref):
        y_mla = _row_shards_mm(attn_ref[...], wo_ref)
        ym_ref[...] = y_mla
        mixed = (_sigmoid(gs_ref[...]) * ys_ref[...] + _sigmoid(gm_ref[...]) * y_mla).astype(BF16)
        mixed_ref[...] = mixed
        h_ref[...] = x_ref[...] + _row_shards_mm(mixed, wout_ref)

    r = lambda: _rows(t, D_MODEL)
    return pl.pallas_call(
        body, name="merge_fwd", grid=(l // t,),
        in_specs=[r(), r(), r(), r(), r(), _member_block("w_o_mla"), _member_block("w_out")],
        out_specs=[r(), r(), r()],
        out_shape=[jax.ShapeDtypeStruct((l, D_MODEL), F32), jax.ShapeDtypeStruct((l, D_MODEL), BF16),
                   jax.ShapeDtypeStruct((l, D_MODEL), F32)],
        compiler_params=_cparams("parallel"),
    )(attn, y_ssm, gs, gm, x, grp_a, grp_a)


def _merge_bwd(dh, y_ssm, y_mla, gs, gm, grp_a, t):
    l = dh.shape[0]

    def body(dh_ref, ys_ref, ym_ref, gs_ref, gm_ref, wo_ref, wout_ref, dys_ref, dym_ref, dgs_ref, dgm_ref, dattn_ref):
        dmixed = _row_shards_mm_nt(dh_ref[...], wout_ref)
        sg = _sigmoid(gs_ref[...])
        sm = _sigmoid(gm_ref[...])
        dys_ref[...] = (dmixed * sg).astype(BF16)
        dgs_ref[...] = (dmixed * ys_ref[...] * sg * (1.0 - sg)).astype(BF16)
        dym = (dmixed * sm).astype(BF16)
        dym_ref[...] = dym
        dgm_ref[...] = (dmixed * ym_ref[...] * sm * (1.0 - sm)).astype(BF16)
        dattn_ref[...] = _row_shards_mm_nt(dym, wo_ref)

    r = lambda: _rows(t, D_MODEL)
    bf = jax.ShapeDtypeStruct((l, D_MODEL), BF16)
    return pl.pallas_call(
        body, name="merge_bwd", grid=(l // t,),
        in_specs=[r(), r(), r(), r(), r(), _member_block("w_o_mla"), _member_block("w_out")],
        out_specs=[r(), r(), r(), r(), r()],
        out_shape=[bf, bf, bf, bf, jax.ShapeDtypeStruct((l, D_MODEL), F32)],
        compiler_params=_cparams("parallel"),
    )(dh, y_ssm, y_mla, gs, gm, grp_a, grp_a)


def _mlp_fwd_bwd(h, tgt, g2, grp_a, t):
    l = h.shape[0]

    def body(h_ref, tgt_ref, g_ref, wu_ref, wd_ref, dh_ref, hn_ref, da_ref, hid_ref, dout_ref, loss_ref, dg_ref):
        first = pl.program_id(0) == 0
        hv = h_ref[...]
        g = g_ref[...]
        hn = _rms_fwd(hv, g, D_MODEL).astype(BF16)
        hn_ref[...] = hn
        out = hv
        relus = []
        for s in range(4):
            cols = slice(1024 * s, 1024 * (s + 1))
            relu = jnp.maximum(_mm(hn, wu_ref[s]), 0.0)
            relus.append(relu)
            hid = (relu * relu).astype(BF16)
            hid_ref[:, cols] = hid
            out = out + _mm(hid, wd_ref[s])
        err = out - tgt_ref[...]
        _accumulate(loss_ref, jnp.full((8, 128), jnp.sum(err * err) * (0.5 / D_MODEL), F32), first)
        dout = err * (1.0 / D_MODEL)
        doutb = dout.astype(BF16)
        dout_ref[...] = doutb
        dhn = jnp.zeros_like(hv)
        for s in range(4):
            da = (_mm_nt(doutb, wd_ref[s]) * (2.0 * relus[s])).astype(BF16)
            da_ref[:, 1024 * s:1024 * (s + 1)] = da
            dhn = dhn + _mm_nt(da, wu_ref[s])
        dx, dg_rows = _rms_bwd(hv, g, dhn, D_MODEL)
        dh_ref[...] = dout + dx
        _accumulate(dg_ref, _colsum(dg_rows), first)

    r = lambda w: _rows(t, w)
    return pl.pallas_call(
        body, name="mlp_fwd_bwd", grid=(l // t,),
        in_specs=[r(D_MODEL), r(D_MODEL), _resident((1, D_MODEL)), _member_block("w_up"), _member_block("w_down")],
        out_specs=[r(D_MODEL), r(D_MODEL), r(D_FF), r(D_FF), r(D_MODEL), pl.BlockSpec((8, 128), lambda i: (0, 0)),
                   pl.BlockSpec((1, D_MODEL), lambda i: (0, 0))],
        out_shape=[jax.ShapeDtypeStruct((l, D_MODEL), F32), jax.ShapeDtypeStruct((l, D_MODEL), BF16),
                   jax.ShapeDtypeStruct((l, D_FF), BF16), jax.ShapeDtypeStruct((l, D_FF), BF16),
                   jax.ShapeDtypeStruct((l, D_MODEL), BF16), jax.ShapeDtypeStruct((8, 128), F32),
                   jax.ShapeDtypeStruct((1, D_MODEL), F32)],
        compiler_params=_cparams("arbitrary"),
    )(h, tgt, g2, grp_a, grp_a)


def _wgrad(a, b, name):
    l, m = a.shape
    n = b.shape[1]
    bm = m if m <= 512 else 512
    bl = min(l, 2048 if n <= 1024 else 1024)

    def body(a_ref, b_ref, o_ref):
        _accumulate(o_ref, _mm_tn(a_ref[...], b_ref[...]), pl.program_id(1) == 0)

    return pl.pallas_call(
        body, name=name, grid=(m // bm, l // bl),
        in_specs=[pl.BlockSpec((bl, bm), lambda i, j: (j, i)), pl.BlockSpec((bl, n), lambda i, j: (j, 0))],
        out_specs=pl.BlockSpec((bm, n), lambda i, j: (i, 0)),
        out_shape=jax.ShapeDtypeStruct((m, n), F32),
        compiler_params=_cparams("parallel", "arbitrary"),
    )(a, b)


def _wgrad_into(a, b, member, cut, dest=None):
    group, off, rs, cs = _place_in_group(member)
    l = a.shape[0]
    bm = min(rs, 512)
    bl = min(l, 2048)
    nb = rs // bm
    if cut == "row":
        a_spec = pl.BlockSpec((bl, bm), lambda j, i, k: (k, j * nb + i))
        b_spec = pl.BlockSpec((bl, cs), lambda j, i, k: (k, 0))
    else:
        a_spec = pl.BlockSpec((bl, bm), lambda j, i, k: (k, i))
        b_spec = pl.BlockSpec((bl, cs), lambda j, i, k: (k, j))

    def body(a_ref, b_ref, *rest):
        o_ref = rest[-1]
        part = _mm_tn(a_ref[...], b_ref[...])

        @pl.when(pl.program_id(2) == 0)
        def _():
            o_ref[0] = part

        @pl.when(pl.program_id(2) != 0)
        def _():
            o_ref[0] += part

    operands, in_specs, aliases = [a, b], [a_spec, b_spec], {}
    if dest is not None:
        operands.append(dest)
        in_specs.append(ANY)
        aliases = {2: 0}
    return pl.pallas_call(
        body, name="wgrad_" + member, grid=(4, nb, l // bl), in_specs=in_specs,
        out_specs=pl.BlockSpec((1, bm, cs), lambda j, i, k: (j, off // bm + i, 0)),
        out_shape=jax.ShapeDtypeStruct((4, _group_rows(group), cs), F32), input_output_aliases=aliases,
        compiler_params=_cparams("parallel", "parallel", "arbitrary"),
    )(*operands)


def _adamw(w, g, m, v, name, g_off=0):
    r, c = w.shape
    br = r
    for cand in (256, 128, 64, 32, 16, 8):
        if r % cand == 0 and g_off % cand == 0:
            br = cand
            break

    def body(w_ref, g_ref, m_ref, v_ref, go_ref, d_ref, nm_ref, nv_ref):
        gv = g_ref[...]
        go_ref[...] = gv
        nm = ADAM_B1 * m_ref[...] + (1.0 - ADAM_B1) * gv
        nv = ADAM_B2 * v_ref[...] + (1.0 - ADAM_B2) * (gv * gv)
        m_hat = nm / (1.0 - ADAM_B1 ** ADAM_STEP)
        v_hat = nv / (1.0 - ADAM_B2 ** ADAM_STEP)
        d_ref[...] = -ADAM_LR * (m_hat / (jnp.sqrt(v_hat) + ADAM_EPS) + ADAM_WD * w_ref[...])
        nm_ref[...] = nm
        nv_ref[...] = nv

    spec = lambda: pl.BlockSpec((br, c), lambda i: (i, 0))
    g_spec = pl.BlockSpec((br, c), lambda i: (g_off // br + i, 0))
    shp = jax.ShapeDtypeStruct((r, c), F32)
    return pl.pallas_call(
        body, name=name, grid=(r // br,), in_specs=[spec(), g_spec, spec(), spec()],
        out_specs=[spec(), spec(), spec(), spec()], out_shape=[shp, shp, shp, shp], compiler_params=_cparams("parallel"),
    )(w, g, m, v)


def _place():
    return lax.axis_index("x"), lax.axis_index("y"), lax.axis_index("c")


def _other_chips(x, y):
    return [(1 - x, y), (x, 1 - y), (1 - x, 1 - y)]


ANY = pl.BlockSpec(memory_space=pl.ANY)


def _gather_weights(bufs):
    n = len(bufs)

    def body(*refs):
        outs, send_sems, recv_sems = refs[n:2 * n], refs[2 * n], refs[2 * n + 1]
        x, y, c = _place()
        chips = _other_chips(x, y)

        def part(g, px, py, pc):
            half = outs[g].shape[1] // 2
            return outs[g].at[2 * px + py, pl.ds(pl.multiple_of(pc * half, 16), half), :]

        def copy(k, src, dst, to):
            return pltpu.make_async_remote_copy(src_ref=src, dst_ref=dst, send_sem=send_sems.at[k], recv_sem=recv_sems.at[k],
                                                device_id=to, device_id_type=MESH)

        first = [copy(6 * g + j, part(g, x, y, c), part(g, x, y, c), (*chip, c)) for g in range(n) for j, chip in enumerate(chips)]
        for cp in first:
            cp.start()
        passed = []
        for g in range(n):
            for j, chip in enumerate(chips):
                landed = part(g, *chip, c)
                copy(6 * g + j, landed, landed, (x, y, c)).wait_recv()
                passed.append(copy(6 * g + 3 + j, landed, landed, (x, y, 1 - c)))
                passed[-1].start()
        for g in range(n):
            for j, chip in enumerate(chips):
                other = part(g, *chip, 1 - c)
                copy(6 * g + 3 + j, other, other, (x, y, c)).wait_recv()
        for cp in first + passed:
            cp.wait_send()

    return pl.pallas_call(
        body, name="gather_weights", in_specs=[ANY] * n, out_specs=[ANY] * n,
        out_shape=[jax.ShapeDtypeStruct(b.shape, b.dtype) for b in bufs], input_output_aliases={g: g for g in range(n)},
        scratch_shapes=[pltpu.SemaphoreType.DMA((6 * n,)), pltpu.SemaphoreType.DMA((6 * n,))],
    )(*bufs)


def _cast_shards(shards, group, place):
    width, members = GROUPS[group]
    rows = _group_rows(group)

    def body(place_ref, *refs):
        out = refs[-1]
        off = 0
        for ref, (_, r) in zip(refs[:-1], members):
            out[0, off:off + r, :] = ref[...].astype(BF16)
            off += r

    grid_spec = pltpu.PrefetchScalarGridSpec(
        num_scalar_prefetch=1, grid=(1,),
        in_specs=[pl.BlockSpec((r, width), lambda i, p: (0, 0)) for _, r in members],
        out_specs=pl.BlockSpec((1, rows, width), lambda i, p: (p[0], 0, 0)))
    return pl.pallas_call(
        body, name="cast_shards_" + group, grid_spec=grid_spec, out_shape=jax.ShapeDtypeStruct((4, rows, width), BF16),
        compiler_params=_cparams("arbitrary"),
    )(place, *[shards[name] for name, _ in members])


def _swap_gradient_halves(bufs):
    n = len(bufs)

    def body(*refs):
        ins, outs, send_sems, recv_sems = refs[:n], refs[n:2 * n], refs[2 * n], refs[2 * n + 1]
        x, y, c = _place()
        copies = []
        for g in range(n):
            half = ins[g].shape[1] // 2
            give = ins[g].at[:, pl.ds(pl.multiple_of((1 - c) * half, 8), half), :]
            copies.append(pltpu.make_async_remote_copy(src_ref=give, dst_ref=outs[g], send_sem=send_sems.at[g],
                                                       recv_sem=recv_sems.at[g], device_id=(x, y, 1 - c), device_id_type=MESH))
        for cp in copies:
            cp.start()
        for cp in copies:
            cp.wait()

    return pl.pallas_call(
        body, name="swap_gradient_halves", in_specs=[ANY] * n, out_specs=[ANY] * n,
        out_shape=[jax.ShapeDtypeStruct((4, b.shape[1] // 2, b.shape[2]), b.dtype) for b in bufs],
        scratch_shapes=[pltpu.SemaphoreType.DMA((n,)), pltpu.SemaphoreType.DMA((n,))],
    )(*bufs)


def _block_rows(h):
    return next(cand for cand in (256, 192, 128, 64, 32, 16) if h % cand == 0)


def _add_pair(buf, got, place, name):
    n, h, w = got.shape
    bh = _block_rows(h)
    nb = h // bh

    def body(place_ref, a_ref, b_ref, s_ref, sb_ref):
        s = a_ref[...] + b_ref[...]
        s_ref[...] = s
        sb_ref[...] = s.astype(BF16)

    spec = lambda: pl.BlockSpec((1, bh, w), lambda j, i, p: (j, i, 0))
    grid_spec = pltpu.PrefetchScalarGridSpec(
        num_scalar_prefetch=1, grid=(n, nb),
        in_specs=[pl.BlockSpec((1, bh, w), lambda j, i, p: (j, p[1] * nb + i, 0)), spec()], out_specs=[spec(), spec()])
    return pl.pallas_call(
        body, name=name, grid_spec=grid_spec,
        out_shape=[jax.ShapeDtypeStruct(got.shape, F32), jax.ShapeDtypeStruct(got.shape, BF16)],
        compiler_params=_cparams("parallel", "parallel"),
    )(place, buf, got)


def _scatter_to_chips(bufs):
    n = len(bufs)

    def body(*refs):
        ins, outs, send_sems, recv_sems = refs[:n], refs[n:2 * n], refs[2 * n], refs[2 * n + 1]
        x, y, c = _place()
        copies = [pltpu.make_async_remote_copy(src_ref=ins[g].at[2 * px + py], dst_ref=outs[g].at[j],
                                               send_sem=send_sems.at[3 * g + j], recv_sem=recv_sems.at[3 * g + j],
                                               device_id=(px, py, c), device_id_type=MESH)
                  for g in range(n) for j, (px, py) in enumerate(_other_chips(x, y))]
        for cp in copies:
            cp.start()
        for cp in copies:
            cp.wait()

    return pl.pallas_call(
        body, name="scatter_to_chips", in_specs=[ANY] * n, out_specs=[ANY] * n,
        out_shape=[jax.ShapeDtypeStruct((3,) + b.shape[1:], b.dtype) for b in bufs],
        scratch_shapes=[pltpu.SemaphoreType.DMA((3 * n,)), pltpu.SemaphoreType.DMA((3 * n,))],
    )(*bufs)


def _add_received(pair, got, place, name):
    _, h, w = pair.shape
    bh = _block_rows(h)
    nb = h // bh

    def body(place_ref, own_ref, got_ref, o_ref):
        o_ref[...] = ((own_ref[0] + got_ref[0].astype(F32)) + got_ref[1].astype(F32)) + got_ref[2].astype(F32)

    grid_spec = pltpu.PrefetchScalarGridSpec(
        num_scalar_prefetch=1, grid=(nb,),
        in_specs=[pl.BlockSpec((1, bh, w), lambda i, p: (p[0], i, 0)), pl.BlockSpec((3, bh, w), lambda i, p: (0, i, 0))],
        out_specs=pl.BlockSpec((bh, w), lambda i, p: (p[1] * nb + i, 0)))
    return pl.pallas_call(
        body, name=name, grid_spec=grid_spec, out_shape=jax.ShapeDtypeStruct((2 * h, w), F32),
        compiler_params=_cparams("parallel"),
    )(place, pair, got)


def _swap_reduced_halves(bufs):
    n = len(bufs)

    def body(*refs):
        outs, send_sems, recv_sems = refs[n:2 * n], refs[2 * n], refs[2 * n + 1]
        x, y, c = _place()
        copies = []
        for g in range(n):
            half = outs[g].shape[0] // 2
            own = outs[g].at[pl.ds(pl.multiple_of(c * half, 8), half), :]
            copies.append(pltpu.make_async_remote_copy(src_ref=own, dst_ref=own, send_sem=send_sems.at[g],
                                                       recv_sem=recv_sems.at[g], device_id=(x, y, 1 - c), device_id_type=MESH))
        for cp in copies:
            cp.start()
        for g in range(n):
            half = outs[g].shape[0] // 2
            other = outs[g].at[pl.ds(pl.multiple_of((1 - c) * half, 8), half), :]
            pltpu.make_async_remote_copy(src_ref=other, dst_ref=other, send_sem=send_sems.at[g], recv_sem=recv_sems.at[g],
                                         device_id=(x, y, 1 - c), device_id_type=MESH).wait_recv()
        for cp in copies:
            cp.wait_send()

    return pl.pallas_call(
        body, name="swap_reduced_halves", in_specs=[ANY] * n, out_specs=[ANY] * n,
        out_shape=[jax.ShapeDtypeStruct(b.shape, b.dtype) for b in bufs], input_output_aliases={g: g for g in range(n)},
        scratch_shapes=[pltpu.SemaphoreType.DMA((n,)), pltpu.SemaphoreType.DMA((n,))],
    )(*bufs)


HBM = pl.BlockSpec(memory_space=pltpu.HBM)
SEM = pl.BlockSpec(memory_space=pltpu.SEMAPHORE)


def _copies_start(name, bufs, n_copies, plan):
    n = len(bufs)

    def body(*refs):
        sems = refs[n:n + 2 * n_copies]
        x, y, c = _place()
        for i, (src, dst, dev) in enumerate(plan(refs[:n], x, y, c)):
            pltpu.make_async_remote_copy(src_ref=src, dst_ref=dst, send_sem=sems[i], recv_sem=sems[n_copies + i],
                                         device_id=dev, device_id_type=MESH).start()
        token = refs[-1]
        token[...] = jnp.zeros_like(token)

    out = pl.pallas_call(
        body, name=name,
        out_shape=[pltpu.SemaphoreType.DMA(())] * (2 * n_copies) + [pltpu.HBM(b.shape, b.dtype) for b in bufs]
        + [jax.ShapeDtypeStruct((8, 128), F32)],
        in_specs=[HBM] * n, out_specs=[SEM] * (2 * n_copies) + [HBM] * n + [pl.BlockSpec(memory_space=pltpu.VMEM)],
        input_output_aliases={i: 2 * n_copies + i for i in range(n)},
        compiler_params=pltpu.CompilerParams(has_side_effects=pltpu.SideEffectType.DATAFLOW_SIDE_EFFECTING),
    )(*[pltpu.with_memory_space_constraint(b, pltpu.HBM) for b in bufs])
    return list(out[:2 * n_copies]), list(out[2 * n_copies:-1]), out[-1]


def _copies_wait(name, bufs, sems, after, plan):
    n = len(bufs)
    k = len(sems) // 2

    def body(*refs):
        sem_refs = refs[n:n + 2 * k]
        x, y, c = _place()
        for i, (sent, landed, dev) in enumerate(plan(refs[:n], x, y, c)):
            cp = pltpu.make_async_remote_copy(src_ref=sent, dst_ref=landed, send_sem=sem_refs[i], recv_sem=sem_refs[k + i],
                                              device_id=dev, device_id_type=MESH)
            cp.wait_send()
            cp.wait_recv()

    return pl.pallas_call(
        body, name=name, out_shape=[pltpu.HBM(b.shape, b.dtype) for b in bufs],
        in_specs=[HBM] * n + [SEM] * (2 * k) + [ANY], out_specs=[HBM] * n, input_output_aliases={i: i for i in range(n)},
        compiler_params=pltpu.CompilerParams(has_side_effects=pltpu.SideEffectType.DATAFLOW_SIDE_EFFECTING),
    )(*bufs, *sems, after)


class _GroupAExchange:
    def __init__(self, own_a, place, after):
        self.place = place
        own_a = lax.optimization_barrier((own_a, after))[0]
        self.gather = _copies_start("gather_a_start", [own_a], 3, self._gather_plan)

    @staticmethod
    def _gather_plan(refs, x, y, c):
        (wa,) = refs
        return [(wa.at[2 * x + y], wa.at[2 * x + y], (px, py, c)) for px, py in _other_chips(x, y)]

    @staticmethod
    def _gather_landed(refs, x, y, c):
        (wa,) = refs
        return [(wa.at[2 * x + y], wa.at[2 * px + py], (px, py, c)) for px, py in _other_chips(x, y)]

    def tie(self, token, value):
        return lax.optimization_barrier((value, token))[0]

    def weights(self, after):
        sems, bufs, _ = self.gather
        return _copies_wait("gather_a_wait", bufs, sems, after, self._gather_landed)[0]

    def token_after_gather_start(self):
        return self.gather[2]


    def start_pair(self, ga):
        half = ga.shape[1] // 2
        land = lax.empty((4, half, ga.shape[2]), F32)

        def plan(refs, x, y, c):
            g, got = refs
            return [(g.at[:, pl.ds(pl.multiple_of((1 - c) * half, 8), half), :], got, (x, y, 1 - c))]

        self._pair_plan = plan
        self._pair = _copies_start("pair_a_start", [ga, land], 1, plan)
        return self._pair[2]

    def pair_done_start_scatter(self, after):
        sems, bufs, _ = self._pair
        ga, got = _copies_wait("pair_a_wait", bufs, sems, after, self._pair_plan)
        self._pair_f32, pair_bf16 = _add_pair(ga, got, self.place, "add_pair_a")
        land = lax.empty((3,) + pair_bf16.shape[1:], BF16)

        def plan(refs, x, y, c):
            mine, got = refs
            return [(mine.at[2 * px + py], got.at[j], (px, py, c)) for j, (px, py) in enumerate(_other_chips(x, y))]

        self._scatter_plan = plan
        self._scatter = _copies_start("scatter_a_start", [pair_bf16, land], 3, plan)
        return self._scatter[2]

    def scatter_done_start_join(self, after):
        sems, bufs, _ = self._scatter
        _, got = _copies_wait("scatter_a_wait", bufs, sems, after, self._scatter_plan)
        mine = _add_received(self._pair_f32, got, self.place, "add_received_a")
        half = mine.shape[0] // 2
        rows = lambda r, pc: r.at[pl.ds(pl.multiple_of(pc * half, 8), half), :]
        self._join_landed = lambda refs, x, y, c: [(rows(refs[0], c), rows(refs[0], 1 - c), (x, y, 1 - c))]
        self._join = _copies_start("join_a_start", [mine], 1,
                                   lambda refs, x, y, c: [(rows(refs[0], c), rows(refs[0], c), (x, y, 1 - c))])
        return self._join[2]

    def join_done(self, after):
        sems, bufs, _ = self._join
        self.reduced = _copies_wait("join_a_wait", bufs, sems, after, self._join_landed)[0]


def _all_sum_small(mine):
    rows, w = mine.shape

    def body(in_ref, out_ref, slots, send_sems, recv_sems):
        x, y, c = _place()
        me = 4 * x + 2 * y + c
        slots[me] = in_ref[...]
        copies = []
        for k in range(1, 8):
            peer = (1 - x if k & 4 else x, 1 - y if k & 2 else y, 1 - c if k & 1 else c)
            copies.append(pltpu.make_async_remote_copy(src_ref=in_ref, dst_ref=slots.at[me], send_sem=send_sems.at[k - 1],
                                                       recv_sem=recv_sems.at[k - 1], device_id=peer, device_id_type=MESH))
        for cp in copies:
            cp.start()
        for cp in copies:
            cp.wait()
        total = slots[0]
        for d in range(1, 8):
            total = total + slots[d]
        out_ref[...] = total

    return pl.pallas_call(
        body, name="all_sum_small", out_shape=jax.ShapeDtypeStruct((rows, w), F32),
        in_specs=[pl.BlockSpec(memory_space=pltpu.VMEM)], out_specs=pl.BlockSpec(memory_space=pltpu.VMEM),
        scratch_shapes=[pltpu.VMEM((8, rows, w), F32), pltpu.SemaphoreType.DMA((7,)), pltpu.SemaphoreType.DMA((7,))],
        compiler_params=pltpu.CompilerParams(vmem_limit_bytes=VMEM_LIMIT_V7X),
    )(mine)


def _join_column_shards(g):
    return jnp.transpose(g, (1, 0, 2)).reshape(g.shape[1], 4 * g.shape[2])


def _split_column_shards(w):
    r = w.shape[0]
    return jnp.transpose(w.reshape(r, 4, w.shape[1] // 4), (1, 0, 2))


def _small_rows(shape):
    return -(-int(np.prod(shape)) // 1024)


def _pack_small(vals):
    segs = []
    for name, shape in SMALL_WEIGHTS:
        flat = vals[name].reshape(-1)
        segs.append(jnp.pad(flat, (0, _small_rows(shape) * 1024 - flat.shape[0])))
    total = sum(s.shape[0] for s in segs) // 1024
    segs.append(jnp.zeros((-total % 8 * 1024,), F32))
    return jnp.concatenate(segs).reshape(-1, 1024)


def _unpack_small(packed):
    out, off = {}, 0
    for name, shape in SMALL_WEIGHTS:
        rows = _small_rows(shape)
        out[name] = packed[off:off + rows].reshape(-1)[:int(np.prod(shape))].reshape(shape)
        off += rows
    return out


def _pad_w_in(w):
    return jnp.concatenate([w[:, :1216], jnp.zeros((w.shape[0], 64), w.dtype), w[:, 1216:]], axis=1)


def _unpad_w_in(g):
    return jnp.concatenate([g[:, :1216], g[:, 1280:]], axis=1)


def _pad_heads(w):
    r = w.shape[0]
    return jnp.pad(w.reshape(r, N_HEADS, QK_HEAD), ((0, 0), (0, 0), (0, HEAD_PAD - QK_HEAD))).reshape(r, N_HEADS * HEAD_PAD)


def _unpad_heads(g):
    r = g.shape[0]
    return g.reshape(r, N_HEADS, HEAD_PAD)[:, :, :QK_HEAD].reshape(r, N_HEADS * QK_HEAD)


def _local_step(x, positions, tgt, grp, small, ex):
    l = x.shape[0]
    x = ex.tie(ex.token_after_gather_start(), x)
    t = min(l, 512)
    t_mlp = min(l, 256)
    tq = min(l, 512)
    tc = min(l, 256)
    row = lambda v: v.reshape(1, -1).astype(F32)

    w_in_p = _pad_w_in(_join_column_shards(grp["b"]))
    w_qb_p = _pad_heads(_join_column_shards(grp["c"]))
    g1, g2 = row(small["norm_mix"]), row(small["norm_mlp"])
    gqa, gkva = row(small["q_a_norm"]), row(small["kv_a_norm"])
    gq = jnp.pad(row(small["q_norm"]), ((0, 0), (0, HEAD_PAD - QK_HEAD)))
    gk = jnp.pad(row(small["k_norm"]), ((0, 0), (0, HEAD_PAD - QK_HEAD)))
    half = QK_ROPE // 2
    inv_freq = ROPE_THETA ** (-jnp.arange(half, dtype=F32) / half)
    invf = jnp.concatenate([inv_freq, inv_freq, jnp.zeros((64,), F32)]).reshape(1, 128)
    sgn = jnp.concatenate([-jnp.ones((half,), F32), jnp.ones((half,), F32), jnp.zeros((64,), F32)]).reshape(1, 128)
    pos = positions.reshape(l, 1)

    a_re, a_im = small["ssm_a_re"], small["ssm_a_im"]
    log_dt = small["ssm_log_dt"].reshape(SSM_GROUPS, 1)
    to_gcp = lambda b: jnp.transpose(b, (0, 2, 1)).reshape(SSM_WIDTH, SSM_STATE)
    from_gcp = lambda b: jnp.transpose(b.reshape(SSM_GROUPS, SSM_GROUP_CH, SSM_STATE), (0, 2, 1))
    b_re, b_im = to_gcp(small["ssm_b_re"]), to_gcp(small["ssm_b_im"])
    c_re, c_im = small["ssm_c_re"].reshape(SSM_WIDTH, SSM_STATE), small["ssm_c_im"].reshape(SSM_WIDTH, SSM_STATE)
    wb, wc, tabs_fwd, tabs_rev = _ssm_param_fwd(a_re, a_im, log_dt, b_re, b_im, c_re, c_im)
    dskip = row(small["ssm_d"])
    b_glu = row(small["b_glu"])

    u, lat, gs, gm = _in_proj_fwd(x, g1, w_in_p, t)
    xr, xi, y, y_ssm = _ssm_fwd(u, wb, wc, tabs_fwd, dskip, grp["d"], b_glu, grp["e"], tc)
    q, k, v = _mla_pre_fwd(lat, pos, invf, sgn, gqa, gkva, gq, gk, w_qb_p, grp["d"], t)
    attn, lse = _attn_fwd(q, k, v, tq)
    grp_a = ex.weights(attn)
    y_mla, mixed, h = _merge_fwd(attn, y_ssm, gs, gm, x, grp_a, t)
    dh, hn, da, hid, dout, loss_blk, g_norm_mlp = _mlp_fwd_bwd(h, tgt, g2, grp_a, t_mlp)

    grads = {}
    ga = _wgrad_into(hn, da, "w_up", "col", _wgrad_into(hid, dout, "w_down", "row"))
    dys, dym, dgs, dgm, dattn = _merge_bwd(dh, y_ssm, y_mla, gs, gm, grp_a, t)
    ga = _wgrad_into(attn, dym, "w_o_mla", "row", _wgrad_into(mixed, dh, "w_out", "row", ga))

    dq, delta = _attn_bwd_dq(ex.tie(ex.start_pair(ga), q), k, v, attn, dattn, lse, tq)
    lanes = lambda a: a.reshape(N_HEADS, l // tq, 1, tq)
    dk, dv = _attn_bwd_dkv(ex.tie(ex.pair_done_start_scatter(dq), q), k, v, dattn, lanes(lse), lanes(delta), tq)
    d_lat, ql, dq0, ckn, dkv, g_qa, g_kva, g_q, g_k = _mla_pre_bwd(ex.tie(ex.scatter_done_start_join(dk), lat), pos, invf,
                                                                    sgn, gqa, gkva, gq, gk, w_qb_p, grp["d"], dq, dk, dv, t)
    ex.join_done(d_lat)
    grads["c"] = _split_column_shards(_unpad_heads(_wgrad(ql, dq0, "wgrad_q_b")))

    d_u, adj, dy, z, z2, dpre, g_b_glu, g_d, g_lr, g_li = _ssm_bwd(
        dys, y, u, xr, xi, wb, wc, tabs_rev, dskip, grp["d"], b_glu, grp["e"], tc)
    grads["d"] = _wgrad_into(z, dpre, "w_glu", "row", _wgrad_into(ckn, dkv, "w_kv_b", "col"))
    grads["e"] = _wgrad_into(z2, dys, "w_o_ssm", "col")
    g_ar, g_ai, g_ldt, g_br, g_bi, g_cr, g_ci = _ssm_param_bwd(
        a_re, a_im, log_dt, b_re, b_im, g_lr, g_li, _wgrad(u, adj, "wgrad_ssm_b"), _wgrad(dy, xr, "wgrad_ssm_c_re"),
        _wgrad(dy, xi, "wgrad_ssm_c_im"))

    grad_x, xn, dproj, g_norm_mix = _in_proj_bwd(x, g1, w_in_p, d_u, d_lat, dgs, dgm, dh, t)
    grads["b"] = _split_column_shards(_unpad_w_in(_wgrad(xn, dproj, "wgrad_in")))

    g_small = {
        "norm_mix": g_norm_mix.reshape(-1), "norm_mlp": g_norm_mlp.reshape(-1), "q_a_norm": g_qa.reshape(-1),
        "kv_a_norm": g_kva.reshape(-1), "q_norm": g_q.reshape(-1)[:QK_HEAD], "k_norm": g_k.reshape(-1)[:QK_HEAD],
        "ssm_a_re": g_ar, "ssm_a_im": g_ai, "ssm_log_dt": g_ldt.reshape(-1),
        "ssm_b_re": from_gcp(g_br), "ssm_b_im": from_gcp(g_bi),
        "ssm_c_re": g_cr.reshape(SSM_GROUPS, SSM_GROUP_CH, SSM_STATE), "ssm_c_im": g_ci.reshape(SSM_GROUPS, SSM_GROUP_CH, SSM_STATE),
        "ssm_d": g_d.reshape(SSM_GROUPS, SSM_GROUP_CH), "b_glu": g_b_glu.reshape(-1),
    }
    return loss_blk[0, 0], grad_x, grads, g_small


def kernel(x, positions, norm_mix, w_in, q_a_norm, kv_a_norm, w_q_b, w_kv_b, q_norm, k_norm, w_o_mla, ssm_a_re, ssm_a_im, ssm_log_dt, ssm_b_re, ssm_b_im, ssm_c_re, ssm_c_im, ssm_d, w_glu, b_glu, w_o_ssm, w_out, norm_mlp, w_up, w_down, loss_target, m_norm_mix, m_w_in, m_q_a_norm, m_kv_a_norm, m_w_q_b, m_w_kv_b, m_q_norm, m_k_norm, m_w_o_mla, m_ssm_a_re, m_ssm_a_im, m_ssm_log_dt, m_ssm_b_re, m_ssm_b_im, m_ssm_c_re, m_ssm_c_im, m_ssm_d, m_w_glu, m_b_glu, m_w_o_ssm, m_w_out, m_norm_mlp, m_w_up, m_w_down, v_norm_mix, v_w_in, v_q_a_norm, v_kv_a_norm, v_w_q_b, v_w_kv_b, v_q_norm, v_k_norm, v_w_o_mla, v_ssm_a_re, v_ssm_a_im, v_ssm_log_dt, v_ssm_b_re, v_ssm_b_im, v_ssm_c_re, v_ssm_c_im, v_ssm_d, v_w_glu, v_b_glu, v_w_o_ssm, v_w_out, v_norm_mlp, v_w_up, v_w_down):
    args = dict(locals())
    w = {n: args[n][0] for n in WEIGHT_ORDER}
    m = {n: args["m_" + n][0] for n in WEIGHT_ORDER}
    v = {n: args["v_" + n][0] for n in WEIGHT_ORDER}
    big_names = [n for n, *_ in BIG_WEIGHTS]
    small_names = [n for n, _ in SMALL_WEIGHTS]

    place = jnp.stack([2 * lax.axis_index("x") + lax.axis_index("y"), lax.axis_index("c")]).astype(jnp.int32)
    groups = [g for g in sorted(GROUPS) if g != "a"]

    gathered = _gather_weights([_cast_shards(w, g, place) for g in groups])
    grp = dict(zip(groups, gathered))
    ex = _GroupAExchange(_cast_shards(w, "a", place), place, gathered[0])
    small = {n: w[n] for n in small_names}

    loss_local, grad_x, grads, g_small = _local_step(x[0], positions[0], loss_target[0], grp, small, ex)
    loss = lax.psum(loss_local, ("x", "y", "c"))

    bufs = [grads[g] for g in groups]
    pairs = [_add_pair(b, got, place, "add_pair_" + g) for g, b, got in zip(groups, bufs, _swap_gradient_halves(bufs))]
    landed = _scatter_to_chips([p[1] for p in pairs])
    halves = [_add_received(p[0], got, place, "add_received_" + g) for g, p, got in zip(groups, pairs, landed)]
    reduced = dict(zip(groups, _swap_reduced_halves(halves)))
    reduced["a"] = ex.reduced

    small_sum = _all_sum_small(_pack_small(g_small))

    grad_w, delta_w, new_m, new_v = {}, {}, {}, {}
    for n in big_names:
        g, off, _, _ = _place_in_group(n)
        grad_w[n], delta_w[n], new_m[n], new_v[n] = _adamw(w[n], reduced[g], m[n], v[n], "adamw_" + n, off)
    g_s, d_s, m_s, v_s = _adamw(_pack_small(small), small_sum, _pack_small({n: m[n] for n in small_names}),
                                _pack_small({n: v[n] for n in small_names}), "adamw_small")
    g_s, d_s, m_s, v_s = _unpack_small(g_s), _unpack_small(d_s), _unpack_small(m_s), _unpack_small(v_s)
    for n in small_names:
        grad_w[n], delta_w[n], new_m[n], new_v[n] = g_s[n], d_s[n], m_s[n], v_s[n]

    lead = lambda d: [d[n][None] for n in WEIGHT_ORDER]
    return (loss, grad_x[None], *lead(grad_w), *lead(delta_w), *lead(new_m), *lead(new_v))
```

```python
import functools
import math

import jax
import jax.numpy as jnp
import numpy as np
from jax import lax
from jax.experimental import pallas as pl
from jax.experimental.pallas import tpu as pltpu

F32 = jnp.float32
BF16 = jnp.bfloat16

D_MODEL = 1024
SSM_GROUPS = 32
SSM_GROUP_CH = 16
SSM_WIDTH = 512
SSM_STATE = 64
GP = SSM_GROUPS * SSM_STATE
N_HEADS = 8
QK_NOPE = 128
QK_ROPE = 64
QK_HEAD = 192
HEAD_PAD = 256
V_HEAD = 128
Q_LORA = 384
KV_LORA = 256
LAT_W = 768
D_IN = 3264
D_IN_PAD = 3328
D_FF = 4096
ROPE_THETA = 10000.0
EPS = 1e-6
ATT_SCALE = QK_HEAD ** -0.5

ADAM_LR = 0.001
ADAM_B1 = 0.9
ADAM_B2 = 0.999
ADAM_EPS = 1e-08
ADAM_WD = 0.01
ADAM_STEP = 10

VMEM_LIMIT_V7X = 56 * 1024 * 1024
MESH = pl.DeviceIdType.MESH

BIG_WEIGHTS = (
    ("w_in", 1024, 3264, "col"),
    ("w_q_b", 384, 1536, "col"),
    ("w_kv_b", 256, 2048, "col"),
    ("w_o_mla", 1024, 1024, "row"),
    ("w_glu", 512, 512, "row"),
    ("w_o_ssm", 512, 1024, "col"),
    ("w_out", 1024, 1024, "row"),
    ("w_up", 1024, 4096, "col"),
    ("w_down", 4096, 1024, "row"),
)
GROUPS = {
    "a": (1024, (("w_down", 1024), ("w_up", 1024), ("w_o_mla", 256), ("w_out", 256))),
    "b": (816, (("w_in", 1024),)),
    "c": (384, (("w_q_b", 384),)),
    "d": (512, (("w_kv_b", 256), ("w_glu", 128))),
    "e": (256, (("w_o_ssm", 512),)),
}


def _group_rows(group):
    return sum(r for _, r in GROUPS[group][1])


def _place_in_group(name):
    for group, (width, members) in GROUPS.items():
        off = 0
        for member, rows in members:
            if member == name:
                return group, off, rows, width
            off += rows
    raise KeyError(name)


SMALL_WEIGHTS = (
    ("norm_mix", (1024,)), ("q_a_norm", (384,)), ("kv_a_norm", (256,)), ("q_norm", (192,)), ("k_norm", (192,)),
    ("ssm_a_re", (32, 64)), ("ssm_a_im", (32, 64)), ("ssm_log_dt", (32,)),
    ("ssm_b_re", (32, 64, 16)), ("ssm_b_im", (32, 64, 16)), ("ssm_c_re", (32, 16, 64)), ("ssm_c_im", (32, 16, 64)),
    ("ssm_d", (32, 16)), ("b_glu", (512,)), ("norm_mlp", (1024,)),
)
WEIGHT_ORDER = ('norm_mix', 'w_in', 'q_a_norm', 'kv_a_norm', 'w_q_b', 'w_kv_b', 'q_norm', 'k_norm', 'w_o_mla', 'ssm_a_re',
                'ssm_a_im', 'ssm_log_dt', 'ssm_b_re', 'ssm_b_im', 'ssm_c_re', 'ssm_c_im', 'ssm_d', 'w_glu', 'b_glu',
                'w_o_ssm', 'w_out', 'norm_mlp', 'w_up', 'w_down')


def _cparams(*sem):
    return pltpu.CompilerParams(dimension_semantics=sem if sem else None, vmem_limit_bytes=VMEM_LIMIT_V7X)


def _resident(shape, index=None):
    index = (0,) * len(shape) if index is None else index
    return pl.BlockSpec(shape, lambda *_: index, pipeline_mode=pl.Buffered(1))


def _member_block(name):
    _, off, rows, width = _place_in_group(name)
    return _resident((4, rows, width), (0, off // rows, 0))


def _rows(t, width):
    return pl.BlockSpec((t, width), lambda i: (i, 0))


def _mm(a, b):
    return jnp.dot(a.astype(BF16), b.astype(BF16), preferred_element_type=F32)


def _mm_nt(a, b):
    return lax.dot_general(a.astype(BF16), b.astype(BF16), (((1,), (1,)), ((), ())), preferred_element_type=F32)


def _mm_tn(a, b):
    return lax.dot_general(a.astype(BF16), b.astype(BF16), (((0,), (0,)), ((), ())), preferred_element_type=F32)


def _rms_fwd(x, g, n):
    r = lax.rsqrt(jnp.sum(x * x, axis=-1, keepdims=True) * (1.0 / n) + EPS)
    return x * r * g


def _rms_bwd(x, g, dy, n):
    r = lax.rsqrt(jnp.sum(x * x, axis=-1, keepdims=True) * (1.0 / n) + EPS)
    xh = x * r
    dxh = dy * g
    dx = r * (dxh - xh * (jnp.sum(dxh * xh, axis=-1, keepdims=True) * (1.0 / n)))
    return dx, dy * xh


def _colsum(a):
    return jnp.sum(a, axis=0, keepdims=True)


def _accumulate(ref, value, first):
    @pl.when(first)
    def _():
        ref[...] = value

    @pl.when(jnp.logical_not(first))
    def _():
        ref[...] += value


def _sigmoid(a):
    return 1.0 / (1.0 + jnp.exp(-a))


GELU_C = math.sqrt(2.0 / math.pi)
GELU_A = 0.044715


def _gelu(y):
    return 0.5 * y * (1.0 + jnp.tanh(GELU_C * (y + GELU_A * y * y * y)))


def _gelu_grad(y):
    t = jnp.tanh(GELU_C * (y + GELU_A * y * y * y))
    return 0.5 * (1.0 + t) + 0.5 * y * (1.0 - t * t) * GELU_C * (1.0 + 3.0 * GELU_A * y * y)


def _in_proj_fwd(x, g1, w_in_p, t, token):
    l = x.shape[0]

    def body(x_ref, g_ref, w_ref, token_ref, u_ref, lat_ref, gs_ref, gm_ref):
        xn = _rms_fwd(x_ref[...], g_ref[...], D_MODEL).astype(BF16)
        u_ref[...] = _mm(xn, w_ref[:, 0:512])
        lat_ref[...] = _mm(xn, w_ref[:, 512:1280])
        gs_ref[...] = _mm(xn, w_ref[:, 1280:2304])
        gm_ref[...] = _mm(xn, w_ref[:, 2304:3328])

    return pl.pallas_call(
        body, name="in_proj_fwd", grid=(l // t,),
        in_specs=[_rows(t, D_MODEL), _resident((1, D_MODEL)), _resident((D_MODEL, D_IN_PAD)), ANY],
        out_specs=[_rows(t, 512), _rows(t, LAT_W), _rows(t, D_MODEL), _rows(t, D_MODEL)],
        out_shape=[jax.ShapeDtypeStruct((l, 512), F32), jax.ShapeDtypeStruct((l, LAT_W), F32),
                   jax.ShapeDtypeStruct((l, D_MODEL), F32), jax.ShapeDtypeStruct((l, D_MODEL), F32)],
        compiler_params=_cparams("parallel"),
    )(x, g1, w_in_p, token)


def _in_proj_bwd(x, g1, w_in_p, d_u, d_lat, d_gs, d_gm, dh, t):
    l = x.shape[0]

    def body(x_ref, g_ref, w_ref, du_ref, dlat_ref, dgs_ref, dgm_ref, dh_ref, gx_ref, xn_ref, dproj_ref, dg_ref):
        xv = x_ref[...]
        g = g_ref[...]
        xn_ref[...] = _rms_fwd(xv, g, D_MODEL).astype(BF16)
        dproj_ref[:, 0:512] = du_ref[...]
        dproj_ref[:, 512:1280] = dlat_ref[...]
        dproj_ref[:, 1280:2304] = dgs_ref[...]
        dproj_ref[:, 2304:3328] = dgm_ref[...]
        dxn = _mm_nt(dproj_ref[...], w_ref[...])
        dx, dg_rows = _rms_bwd(xv, g, dxn, D_MODEL)
        gx_ref[...] = dh_ref[...] + dx
        _accumulate(dg_ref, _colsum(dg_rows), pl.program_id(0) == 0)

    return pl.pallas_call(
        body, name="in_proj_bwd", grid=(l // t,),
        in_specs=[_rows(t, D_MODEL), _resident((1, D_MODEL)), _resident((D_MODEL, D_IN_PAD)), _rows(t, 512),
                  _rows(t, LAT_W), _rows(t, D_MODEL), _rows(t, D_MODEL), _rows(t, D_MODEL)],
        out_specs=[_rows(t, D_MODEL), _rows(t, D_MODEL), _rows(t, D_IN_PAD), pl.BlockSpec((1, D_MODEL), lambda i: (0, 0))],
        out_shape=[jax.ShapeDtypeStruct((l, D_MODEL), F32), jax.ShapeDtypeStruct((l, D_MODEL), BF16),
                   jax.ShapeDtypeStruct((l, D_IN_PAD), BF16), jax.ShapeDtypeStruct((1, D_MODEL), F32)],
        compiler_params=_cparams("arbitrary"),
    )(x, g1, w_in_p, d_u, d_lat, d_gs, d_gm, dh)


def _ssm_param_fn(a_re, a_im, log_dt, b_re, b_im):
    dt = jnp.exp(log_dt)
    er = jnp.exp(a_re * dt)
    lr = er * jnp.cos(a_im * dt)
    li = er * jnp.sin(a_im * dt)
    den = a_re * a_re + a_im * a_im
    nr = lr - 1.0
    kr = (nr * a_re + li * a_im) / den
    ki = (li * a_re - nr * a_im) / den
    rows = lambda k: jnp.broadcast_to(k[:, None, :], (SSM_GROUPS, SSM_GROUP_CH, SSM_STATE)).reshape(SSM_WIDTH, SSM_STATE)
    krt, kit = rows(kr), rows(ki)
    return lr, li, krt * b_re - kit * b_im, krt * b_im + kit * b_re


def _state_selector():
    row = lax.broadcasted_iota(jnp.int32, (SSM_STATE, GP), 0)
    col = lax.broadcasted_iota(jnp.int32, (SSM_STATE, GP), 1)
    return jnp.where(jnp.bitwise_and(col, SSM_STATE - 1) == row, 1.0, 0.0).astype(BF16)


def _own_group(rows, rows_per_group_log2):
    row = lax.broadcasted_iota(jnp.int32, (rows, GP), 0)
    col = lax.broadcasted_iota(jnp.int32, (rows, GP), 1)
    return jnp.right_shift(row, rows_per_group_log2) == jnp.right_shift(col, 6)


def _three_bf16(x):
    hi = x.astype(BF16)
    rest = x - hi.astype(F32)
    mid = rest.astype(BF16)
    return hi, mid, (rest - mid.astype(F32)).astype(BF16)


def _spread(x, sel):
    return sum(jnp.dot(part, sel, preferred_element_type=F32) for part in _three_bf16(x))


def _collect(xw, sel):
    return sum(lax.dot_general(part, sel, (((1,), (1,)), ((), ())), preferred_element_type=F32) for part in _three_bf16(xw))


def _ssm_param_fwd(a_re, a_im, log_dt, b_re, b_im, c_re, c_im):
    def body(ar_ref, ai_ref, ldt_ref, br_ref, bi_ref, cr_ref, ci_ref, wb_ref, wct_ref, tf_ref, tr_ref):
        lr, li, bbr, bbi = _ssm_param_fn(ar_ref[...], ai_ref[...], ldt_ref[...], br_ref[...], bi_ref[...])
        sel = _state_selector()
        own16 = _own_group(SSM_WIDTH, 4)
        own1 = _own_group(SSM_GROUPS, 0)
        block = lambda m: jnp.where(own16, jnp.dot(m.astype(BF16), sel, preferred_element_type=F32), 0.0).astype(BF16)
        wb_ref[:, 0:GP] = block(bbr)
        wb_ref[:, GP:2 * GP] = block(bbi)
        wct_ref[:, 0:GP] = block(cr_ref[...])
        wct_ref[:, GP:2 * GP] = block(-ci_ref[...])
        flat = lambda m: _colsum(jnp.where(own1, _spread(m, sel), 0.0))
        pr, pi = [], []
        qr, qi = lr, li
        for _ in range(8):
            pr.append(flat(qr))
            pi.append(flat(qi))
            qr, qi = qr * lr - qi * li, qr * li + qi * lr
        row = lax.broadcasted_iota(jnp.int32, (8, GP), 0)
        for n, k in enumerate((1, 2, 4)):
            tf_ref[2 * n] = jnp.where(row >= k, pr[k - 1], 0.0)
            tf_ref[2 * n + 1] = jnp.where(row >= k, pi[k - 1], 0.0)
            tr_ref[2 * n] = jnp.where(row < 8 - k, pr[k - 1], 0.0)
            tr_ref[2 * n + 1] = jnp.where(row < 8 - k, -pi[k - 1], 0.0)
        pick = lambda vals: sum(jnp.where(row == j, v, 0.0) for j, v in enumerate(vals))
        tf_ref[6] = pick(pr)
        tf_ref[7] = pick(pi)
        tr_ref[6] = pick(pr[::-1])
        tr_ref[7] = pick([-v for v in pi[::-1]])

    return pl.pallas_call(
        body, name="ssm_param_fwd",
        out_shape=[jax.ShapeDtypeStruct((SSM_WIDTH, 2 * GP), BF16), jax.ShapeDtypeStruct((SSM_WIDTH, 2 * GP), BF16),
                   jax.ShapeDtypeStruct((8, 8, GP), F32), jax.ShapeDtypeStruct((8, 8, GP), F32)],
        compiler_params=_cparams(),
    )(a_re, a_im, log_dt, b_re, b_im, c_re, c_im)


def _ssm_param_bwd(a_re, a_im, log_dt, b_re, b_im, g_lr, g_li, g_wb, g_wct_re, g_wct_im):
    def body(ar_ref, ai_ref, ldt_ref, br_ref, bi_ref, glr_ref, gli_ref, gwb_ref, gcr_ref, gci_ref,
             o_ar, o_ai, o_ldt, o_br, o_bi, o_cr, o_ci):
        sel = _state_selector()
        own16 = _own_group(SSM_WIDTH, 4)
        own1 = _own_group(SSM_GROUPS, 0)
        blocks = lambda m: _collect(jnp.where(own16, m, 0.0), sel)
        unflat = lambda v: _collect(jnp.where(own1, v, 0.0), sel)
        _, vjp = jax.vjp(_ssm_param_fn, ar_ref[...], ai_ref[...], ldt_ref[...], br_ref[...], bi_ref[...])
        d_ar, d_ai, d_ldt, d_br, d_bi = vjp((unflat(glr_ref[...]), unflat(gli_ref[...]),
                                             blocks(gwb_ref[:, 0:GP]), blocks(gwb_ref[:, GP:2 * GP])))
        o_ar[...] = d_ar
        o_ai[...] = d_ai
        o_ldt[...] = d_ldt
        o_br[...] = d_br
        o_bi[...] = d_bi
        o_cr[...] = blocks(gcr_ref[...])
        o_ci[...] = -blocks(gci_ref[...])

    g, p = SSM_GROUPS, SSM_STATE
    gp = jax.ShapeDtypeStruct((g, p), F32)
    gcp = jax.ShapeDtypeStruct((SSM_WIDTH, p), F32)
    return pl.pallas_call(
        body, name="ssm_param_bwd", out_shape=[gp, gp, jax.ShapeDtypeStruct((g, 1), F32), gcp, gcp, gcp, gcp],
        compiler_params=_cparams(),
    )(a_re, a_im, log_dt, b_re, b_im, g_lr, g_li, g_wb, g_wct_re, g_wct_im)


SCAN_STRIP = 512


def _scan_chunk(inr_ref, ini_ref, outr_ref, outi_ref, cr_ref, ci_ref, tab_ref, tc, reverse):
    n_blocks = tc // 8

    def block(j, _):
        i = (n_blocks - 1 - j) if reverse else j
        rows = pl.ds(pl.multiple_of(i * 8, 8), 8)
        for s in range(GP // SCAN_STRIP):
            sl = pl.ds(s * SCAN_STRIP, SCAN_STRIP)
            xr = inr_ref[rows, sl]
            xi = ini_ref[rows, sl]
            for n, k in enumerate((1, 2, 4)):
                shift = (8 - k) if reverse else k
                sr = pltpu.roll(xr, shift, 0)
                si = pltpu.roll(xi, shift, 0)
                mr = tab_ref[2 * n, :, sl]
                mi = tab_ref[2 * n + 1, :, sl]
                xr, xi = xr + mr * sr - mi * si, xi + mr * si + mi * sr
            qr = tab_ref[6, :, sl]
            qi = tab_ref[7, :, sl]
            cr = cr_ref[:, sl]
            ci = ci_ref[:, sl]
            xr, xi = xr + qr * cr - qi * ci, xi + qr * ci + qi * cr
            outr_ref[rows, sl] = xr
            outi_ref[rows, sl] = xi
            edge = 0 if reverse else 7
            cr_ref[:, sl] = jnp.broadcast_to(xr[edge:edge + 1, :], (8, SCAN_STRIP))
            ci_ref[:, sl] = jnp.broadcast_to(xi[edge:edge + 1, :], (8, SCAN_STRIP))
        return 0

    lax.fori_loop(0, n_blocks, block, 0)


def _glu_pre(z, wg_ref):
    return sum(_mm(z[:, 128 * j:128 * (j + 1)], wg_ref[j]) for j in range(4))


def _ssm_fwd(u, wb, wc, tabs, dskip, grp_d, b_glu, grp_e, tc):
    l = u.shape[0]

    def body(u_ref, wb_ref, wc_ref, tab_ref, d_ref, wg_ref, bg_ref, wo_ref, xr_ref, xi_ref, y_ref, ys_ref,
             bur, bui, cr, ci):
        @pl.when(pl.program_id(0) == 0)
        def _():
            cr[...] = jnp.zeros_like(cr)
            ci[...] = jnp.zeros_like(ci)

        uv = u_ref[...]
        ub = uv.astype(BF16)
        bur[...] = _mm(ub, wb_ref[:, 0:GP])
        bui[...] = _mm(ub, wb_ref[:, GP:2 * GP])
        _scan_chunk(bur, bui, xr_ref, xi_ref, cr, ci, tab_ref, tc, False)
        y = _mm_nt(xr_ref[...], wc_ref[:, 0:GP]) + _mm_nt(xi_ref[...], wc_ref[:, GP:2 * GP]) + d_ref[...] * uv
        y_ref[...] = y
        z = _gelu(y)
        z2 = z * _sigmoid(_glu_pre(z, wg_ref) + bg_ref[...])
        for s in range(4):
            ys_ref[:, 256 * s:256 * (s + 1)] = _mm(z2, wo_ref[s])

    return pl.pallas_call(
        body, name="ssm_fwd", grid=(l // tc,),
        in_specs=[_rows(tc, 512), _resident((512, 2 * GP)), _resident((512, 2 * GP)), _resident((8, 8, GP)),
                  _resident((1, 512)), _member_block("w_glu"), _resident((1, 512)), _member_block("w_o_ssm")],
        out_specs=[_rows(tc, GP), _rows(tc, GP), _rows(tc, 512), _rows(tc, D_MODEL)],
        out_shape=[jax.ShapeDtypeStruct((l, GP), F32), jax.ShapeDtypeStruct((l, GP), F32),
                   jax.ShapeDtypeStruct((l, 512), F32), jax.ShapeDtypeStruct((l, D_MODEL), F32)],
        scratch_shapes=[pltpu.VMEM((tc, GP), F32), pltpu.VMEM((tc, GP), F32), pltpu.VMEM((8, GP), F32),
                        pltpu.VMEM((8, GP), F32)],
        compiler_params=_cparams("arbitrary"),
    )(u, wb, wc, tabs, dskip, grp_d, b_glu, grp_e)


def _ssm_bwd(dys, y, u, xr, xi, wb, wc, tabs_rev, dskip, grp_d, b_glu, grp_e, tc):
    l = u.shape[0]
    nc = l // tc

    def body(dys_ref, y_ref, u_ref, xr_ref, xi_ref, wb_ref, wc_ref, tab_ref, d_ref, wg_ref, bg_ref, wo_ref,
             du_ref, a_ref, dy_ref, z_ref, z2_ref, dpre_ref, gb_ref, gd_ref, glr_ref, gli_ref,
             dxr, dxi, ar, ai, cr, ci):
        first = pl.program_id(0) == 0

        @pl.when(first)
        def _():
            cr[...] = jnp.zeros_like(cr)
            ci[...] = jnp.zeros_like(ci)

        yv = y_ref[...]
        uv = u_ref[...]
        dz2 = sum(_mm_nt(dys_ref[:, 256 * j:256 * (j + 1)], wo_ref[j]) for j in range(4))
        z = _gelu(yv)
        s = _sigmoid(_glu_pre(z, wg_ref) + bg_ref[...])
        dpre = dz2 * z * s * (1.0 - s)
        dpreb = dpre.astype(BF16)
        dz = dz2 * s + jnp.concatenate([_mm_nt(dpreb, wg_ref[j]) for j in range(4)], axis=-1)
        dy = dz * _gelu_grad(yv)
        z_ref[...] = z.astype(BF16)
        z2_ref[...] = (z * s).astype(BF16)
        dpre_ref[...] = dpre.astype(BF16)
        dy_ref[...] = dy.astype(BF16)
        _accumulate(gb_ref, _colsum(dpre), first)
        _accumulate(gd_ref, _colsum(dy * uv), first)

        dyb = dy.astype(BF16)
        dxr[...] = _mm(dyb, wc_ref[:, 0:GP])
        dxi[...] = _mm(dyb, wc_ref[:, GP:2 * GP])
        ar[pl.ds(tc, 8), :] = cr[...]
        ai[pl.ds(tc, 8), :] = ci[...]
        _scan_chunk(dxr, dxi, ar, ai, cr, ci, tab_ref, tc, True)
        a_ref[:, 0:GP] = ar[pl.ds(0, tc), :].astype(BF16)
        a_ref[:, GP:2 * GP] = ai[pl.ds(0, tc), :].astype(BF16)
        du_ref[...] = (dy * d_ref[...] + _mm_nt(a_ref[...], wb_ref[...])).astype(BF16)
        anr = ar[pl.ds(1, tc), :]
        ani = ai[pl.ds(1, tc), :]
        xrv = xr_ref[...]
        xiv = xi_ref[...]
        _accumulate(glr_ref, _colsum(anr * xrv + ani * xiv), first)
        _accumulate(gli_ref, _colsum(ani * xrv - anr * xiv), first)

    rev = lambda w: pl.BlockSpec((tc, w), lambda i: (nc - 1 - i, 0))
    acc = lambda w: pl.BlockSpec((1, w), lambda i: (0, 0))
    return pl.pallas_call(
        body, name="ssm_bwd", grid=(nc,),
        in_specs=[rev(D_MODEL), rev(512), rev(512), rev(GP), rev(GP), _resident((512, 2 * GP)), _resident((512, 2 * GP)),
                  _resident((8, 8, GP)), _resident((1, 512)), _member_block("w_glu"), _resident((1, 512)),
                  _member_block("w_o_ssm")],
        out_specs=[rev(512), rev(2 * GP), rev(512), rev(512), rev(512), rev(512), acc(512), acc(512), acc(GP), acc(GP)],
        out_shape=[jax.ShapeDtypeStruct((l, 512), BF16), jax.ShapeDtypeStruct((l, 2 * GP), BF16),
                   jax.ShapeDtypeStruct((l, 512), BF16), jax.ShapeDtypeStruct((l, 512), BF16),
                   jax.ShapeDtypeStruct((l, 512), BF16), jax.ShapeDtypeStruct((l, 512), BF16),
                   jax.ShapeDtypeStruct((1, 512), F32), jax.ShapeDtypeStruct((1, 512), F32),
                   jax.ShapeDtypeStruct((1, GP), F32), jax.ShapeDtypeStruct((1, GP), F32)],
        scratch_shapes=[pltpu.VMEM((tc, GP), F32), pltpu.VMEM((tc, GP), F32), pltpu.VMEM((tc + 8, GP), F32),
                        pltpu.VMEM((tc + 8, GP), F32), pltpu.VMEM((8, GP), F32), pltpu.VMEM((8, GP), F32)],
        compiler_params=_cparams("arbitrary"),
    )(dys, y, u, xr, xi, wb, wc, tabs_rev, dskip, grp_d, b_glu, grp_e)


def _swap_halves(b):
    lane = lax.broadcasted_iota(jnp.int32, b.shape, 1)
    return jnp.where(lane < 32, pltpu.roll(b, 96, 1), pltpu.roll(b, 32, 1))


def _rope_tables(pos_ref, invf_ref, sgn_ref):
    ang = pos_ref[...].astype(F32) * invf_ref[...]
    return jnp.cos(ang), jnp.sin(ang) * sgn_ref[...]


def _mla_pre_fwd(lat, pos, invf, sgn, gqa, gkva, gq, gk, w_qb_p, w_kvb, t):
    l = lat.shape[0]

    def body(lat_ref, pos_ref, invf_ref, sgn_ref, gqa_ref, gkva_ref, gq_ref, gk_ref, wq_ref, wkv_ref, q_ref, k_ref, v_ref):
        cs, sn = _rope_tables(pos_ref, invf_ref, sgn_ref)
        ql = _rms_fwd(lat_ref[:, 0:Q_LORA], gqa_ref[...], Q_LORA)
        ckn = _rms_fwd(lat_ref[:, Q_LORA:Q_LORA + KV_LORA], gkva_ref[...], KV_LORA)
        kpe = lat_ref[:, 640:768]
        q0 = _mm(ql, wq_ref[...])
        cknb = ckn.astype(BF16)
        kv = jnp.concatenate([_mm(cknb, wkv_ref[s]) for s in range(4)], axis=-1)
        for h in range(N_HEADS):
            q1 = _rms_fwd(q0[:, HEAD_PAD * h:HEAD_PAD * (h + 1)], gq_ref[...], QK_HEAD)
            b = q1[:, 128:256]
            q_ref[h, :, 0:128] = (q1[:, 0:128] * ATT_SCALE).astype(BF16)
            q_ref[h, :, 128:256] = ((b * cs + _swap_halves(b) * sn) * ATT_SCALE).astype(BF16)
            k0 = jnp.concatenate([kv[:, 256 * h:256 * h + 128], kpe], axis=-1)
            k1 = _rms_fwd(k0, gk_ref[...], QK_HEAD)
            b = k1[:, 128:256]
            k_ref[h, :, 0:128] = k1[:, 0:128].astype(BF16)
            k_ref[h, :, 128:256] = (b * cs + _swap_halves(b) * sn).astype(BF16)
            v_ref[h] = kv[:, 256 * h + 128:256 * h + 256].astype(BF16)

    heads = lambda w: pl.BlockSpec((N_HEADS, t, w), lambda i: (0, i, 0))
    return pl.pallas_call(
        body, name="mla_pre_fwd", grid=(l // t,),
        in_specs=[_rows(t, LAT_W), _rows(t, 1), _resident((1, 128)), _resident((1, 128)), _resident((1, Q_LORA)),
                  _resident((1, KV_LORA)), _resident((1, HEAD_PAD)), _resident((1, HEAD_PAD)),
                  _resident((Q_LORA, N_HEADS * HEAD_PAD)), _member_block("w_kv_b")],
        out_specs=[heads(HEAD_PAD), heads(HEAD_PAD), heads(V_HEAD)],
        out_shape=[jax.ShapeDtypeStruct((N_HEADS, l, HEAD_PAD), BF16), jax.ShapeDtypeStruct((N_HEADS, l, HEAD_PAD), BF16),
                   jax.ShapeDtypeStruct((N_HEADS, l, V_HEAD), BF16)],
        compiler_params=_cparams("parallel"),
    )(lat, pos, invf, sgn, gqa, gkva, gq, gk, w_qb_p, w_kvb)


def _mla_pre_bwd(lat, pos, invf, sgn, gqa, gkva, gq, gk, w_qb_p, w_kvb, dq, dk, dv, t, token):
    l = lat.shape[0]

    def body(lat_ref, pos_ref, invf_ref, sgn_ref, gqa_ref, gkva_ref, gq_ref, gk_ref, wq_ref, wkv_ref, dq_ref, dk_ref, dv_ref,
             token_ref, dlat_ref, ql_ref, dq0_ref, ckn_ref, dkv_ref, ggqa_ref, ggkva_ref, ggq_ref, ggk_ref):
        first = pl.program_id(0) == 0
        cs, sn = _rope_tables(pos_ref, invf_ref, sgn_ref)
        q_lat = lat_ref[:, 0:Q_LORA]
        c_kv = lat_ref[:, Q_LORA:Q_LORA + KV_LORA]
        kpe = lat_ref[:, 640:768]
        ql = _rms_fwd(q_lat, gqa_ref[...], Q_LORA)
        ckn = _rms_fwd(c_kv, gkva_ref[...], KV_LORA)
        ql_ref[...] = ql.astype(BF16)
        ckn_ref[...] = ckn.astype(BF16)
        q0 = _mm(ql, wq_ref[...])
        cknb = ckn.astype(BF16)
        kv = jnp.concatenate([_mm(cknb, wkv_ref[s]) for s in range(4)], axis=-1)
        dkpe = jnp.zeros_like(kpe)
        ggq = jnp.zeros((1, HEAD_PAD), F32)
        ggk = jnp.zeros((1, HEAD_PAD), F32)

        def unrope(d):
            b = d[:, 128:256]
            return jnp.concatenate([d[:, 0:128], b * cs + _swap_halves(b * sn)], axis=-1)

        for h in range(N_HEADS):
            dq1 = unrope(dq_ref[h] * ATT_SCALE)
            dq0h, gq_rows = _rms_bwd(q0[:, HEAD_PAD * h:HEAD_PAD * (h + 1)], gq_ref[...], dq1, QK_HEAD)
            ggq = ggq + _colsum(gq_rows)
            dq0_ref[:, HEAD_PAD * h:HEAD_PAD * (h + 1)] = dq0h.astype(BF16)
            k0 = jnp.concatenate([kv[:, 256 * h:256 * h + 128], kpe], axis=-1)
            dk0, gk_rows = _rms_bwd(k0, gk_ref[...], unrope(dk_ref[h]), QK_HEAD)
            ggk = ggk + _colsum(gk_rows)
            dkpe = dkpe + dk0[:, 128:256]
            dkv_ref[:, 256 * h:256 * h + 128] = dk0[:, 0:128].astype(BF16)
            dkv_ref[:, 256 * h + 128:256 * h + 256] = dv_ref[h].astype(BF16)
        dql = _mm_nt(dq0_ref[...], wq_ref[...])
        dckn = sum(_mm_nt(dkv_ref[:, 512 * s:512 * (s + 1)], wkv_ref[s]) for s in range(4))
        dq_lat, gqa_rows = _rms_bwd(q_lat, gqa_ref[...], dql, Q_LORA)
        dc_kv, gkva_rows = _rms_bwd(c_kv, gkva_ref[...], dckn, KV_LORA)
        dlat_ref[:, 0:Q_LORA] = dq_lat.astype(BF16)
        dlat_ref[:, Q_LORA:Q_LORA + KV_LORA] = dc_kv.astype(BF16)
        dlat_ref[:, 640:768] = dkpe.astype(BF16)
        _accumulate(ggqa_ref, _colsum(gqa_rows), first)
        _accumulate(ggkva_ref, _colsum(gkva_rows), first)
        _accumulate(ggq_ref, ggq, first)
        _accumulate(ggk_ref, ggk, first)

    heads = lambda w: pl.BlockSpec((N_HEADS, t, w), lambda i: (0, i, 0))
    acc = lambda w: pl.BlockSpec((1, w), lambda i: (0, 0))
    return pl.pallas_call(
        body, name="mla_pre_bwd", grid=(l // t,),
        in_specs=[_rows(t, LAT_W), _rows(t, 1), _resident((1, 128)), _resident((1, 128)), _resident((1, Q_LORA)),
                  _resident((1, KV_LORA)), _resident((1, HEAD_PAD)), _resident((1, HEAD_PAD)),
                  _resident((Q_LORA, N_HEADS * HEAD_PAD)), _member_block("w_kv_b"),
                  heads(HEAD_PAD), heads(HEAD_PAD), heads(V_HEAD), ANY],
        out_specs=[_rows(t, LAT_W), _rows(t, Q_LORA), _rows(t, N_HEADS * HEAD_PAD), _rows(t, KV_LORA), _rows(t, N_HEADS * 256),
                   acc(Q_LORA), acc(KV_LORA), acc(HEAD_PAD), acc(HEAD_PAD)],
        out_shape=[jax.ShapeDtypeStruct((l, LAT_W), BF16), jax.ShapeDtypeStruct((l, Q_LORA), BF16),
                   jax.ShapeDtypeStruct((l, N_HEADS * HEAD_PAD), BF16), jax.ShapeDtypeStruct((l, KV_LORA), BF16),
                   jax.ShapeDtypeStruct((l, N_HEADS * 256), BF16), jax.ShapeDtypeStruct((1, Q_LORA), F32),
                   jax.ShapeDtypeStruct((1, KV_LORA), F32), jax.ShapeDtypeStruct((1, HEAD_PAD), F32),
                   jax.ShapeDtypeStruct((1, HEAD_PAD), F32)],
        compiler_params=_cparams("arbitrary"),
    )(lat, pos, invf, sgn, gqa, gkva, gq, gk, w_qb_p, w_kvb, dq, dk, dv, token)


def _causal(s, transposed):
    row = lax.broadcasted_iota(jnp.int32, s.shape, 0)
    col = lax.broadcasted_iota(jnp.int32, s.shape, 1)
    keep = (row <= col) if transposed else (col <= row)
    return jnp.where(keep, s, -jnp.inf)


def _attn_fwd(q, k, v, tq):
    l = q.shape[1]

    def body(q_ref, k_ref, v_ref, o_ref, lse_ref):
        qi = pl.program_id(1)
        qv = q_ref[0]

        def step(kb, carry, masked):
            m, den, acc = carry
            rows = pl.ds(pl.multiple_of(kb * tq, tq), tq)
            s = _mm_nt(qv, k_ref[0, rows, :])
            if masked:
                s = _causal(s, False)
            m_new = jnp.maximum(m, jnp.max(s, axis=-1, keepdims=True))
            alpha = jnp.exp(m - m_new)
            p = jnp.exp(s - m_new)
            den = alpha * den + jnp.sum(p, axis=-1, keepdims=True)
            acc = alpha * acc + _mm(p, v_ref[0, rows, :])
            return m_new, den, acc

        init = (jnp.full((tq, 1), -jnp.inf, F32), jnp.zeros((tq, 1), F32), jnp.zeros((tq, V_HEAD), F32))
        carry = lax.fori_loop(0, qi, lambda kb, c: step(kb, c, False), init)
        m, den, acc = step(qi, carry, True)
        o_ref[...] = acc / den
        lse_ref[0] = m + jnp.log(den)

    return pl.pallas_call(
        body, name="attn_fwd", grid=(N_HEADS, l // tq),
        in_specs=[pl.BlockSpec((1, tq, HEAD_PAD), lambda h, i: (h, i, 0)), pl.BlockSpec((1, l, HEAD_PAD), lambda h, i: (h, 0, 0)),
                  pl.BlockSpec((1, l, V_HEAD), lambda h, i: (h, 0, 0))],
        out_specs=[pl.BlockSpec((tq, V_HEAD), lambda h, i: (i, h)), pl.BlockSpec((1, tq, 1), lambda h, i: (h, i, 0))],
        out_shape=[jax.ShapeDtypeStruct((l, N_HEADS * V_HEAD), F32), jax.ShapeDtypeStruct((N_HEADS, l, 1), F32)],
        compiler_params=_cparams("parallel", "arbitrary"),
    )(q, k, v)


def _attn_bwd_dq(q, k, v, o, do, lse, tq, token):
    l = q.shape[1]

    def body(q_ref, k_ref, v_ref, o_ref, do_ref, lse_ref, token_ref, dq_ref, delta_ref):
        qi = pl.program_id(1)
        qv = q_ref[0]
        dov = do_ref[...]
        delta = jnp.sum(dov * o_ref[...], axis=-1, keepdims=True)
        delta_ref[0] = delta
        dob = dov.astype(BF16)
        lse = lse_ref[0]

        def step(kb, dq, masked):
            rows = pl.ds(pl.multiple_of(kb * tq, tq), tq)
            kblk = k_ref[0, rows, :]
            s = _mm_nt(qv, kblk)
            if masked:
                s = _causal(s, False)
            p = jnp.exp(s - lse)
            dp = _mm_nt(dob, v_ref[0, rows, :])
            return dq + _mm(p * (dp - delta), kblk)

        dq = lax.fori_loop(0, qi, lambda kb, c: step(kb, c, False), jnp.zeros((tq, HEAD_PAD), F32))
        dq_ref[0] = step(qi, dq, True)

    return pl.pallas_call(
        body, name="attn_bwd_dq", grid=(N_HEADS, l // tq),
        in_specs=[pl.BlockSpec((1, tq, HEAD_PAD), lambda h, i: (h, i, 0)), pl.BlockSpec((1, l, HEAD_PAD), lambda h, i: (h, 0, 0)),
                  pl.BlockSpec((1, l, V_HEAD), lambda h, i: (h, 0, 0)), pl.BlockSpec((tq, V_HEAD), lambda h, i: (i, h)),
                  pl.BlockSpec((tq, V_HEAD), lambda h, i: (i, h)), pl.BlockSpec((1, tq, 1), lambda h, i: (h, i, 0)), ANY],
        out_specs=[pl.BlockSpec((1, tq, HEAD_PAD), lambda h, i: (h, i, 0)), pl.BlockSpec((1, tq, 1), lambda h, i: (h, i, 0))],
        out_shape=[jax.ShapeDtypeStruct((N_HEADS, l, HEAD_PAD), F32), jax.ShapeDtypeStruct((N_HEADS, l, 1), F32)],
        compiler_params=_cparams("parallel", "arbitrary"),
    )(q, k, v, o, do, lse, token)


def _attn_bwd_dkv(q, k, v, do, lse_t, delta_t, tq, token):
    l = q.shape[1]
    nq = l // tq

    def body(q_ref, k_ref, v_ref, do_ref, lse_ref, delta_ref, token_ref, dk_ref, dv_ref):
        ki = pl.program_id(1)
        kblk = k_ref[0]
        vblk = v_ref[0]

        def step(qb, carry, masked):
            dk, dv = carry
            rows = pl.ds(pl.multiple_of(qb * tq, tq), tq)
            qblk = q_ref[0, rows, :]
            dob = do_ref[rows, :].astype(BF16)
            st = _mm_nt(kblk, qblk)
            if masked:
                st = _causal(st, True)
            pt = jnp.exp(st - lse_ref[0, qb])
            dv = dv + _mm(pt, dob)
            dpt = _mm_nt(vblk, dob)
            dk = dk + _mm(pt * (dpt - delta_ref[0, qb]), qblk)
            return dk, dv

        carry = step(ki, (jnp.zeros((tq, HEAD_PAD), F32), jnp.zeros((tq, V_HEAD), F32)), True)
        dk, dv = lax.fori_loop(ki + 1, nq, lambda qb, c: step(qb, c, False), carry)
        dk_ref[0] = dk
        dv_ref[0] = dv

    return pl.pallas_call(
        body, name="attn_bwd_dkv", grid=(N_HEADS, nq),
        in_specs=[pl.BlockSpec((1, l, HEAD_PAD), lambda h, i: (h, 0, 0)), pl.BlockSpec((1, tq, HEAD_PAD), lambda h, i: (h, i, 0)),
                  pl.BlockSpec((1, tq, V_HEAD), lambda h, i: (h, i, 0)), pl.BlockSpec((l, V_HEAD), lambda h, i: (0, h)),
                  pl.BlockSpec((1, nq, 1, tq), lambda h, i: (h, 0, 0, 0)), pl.BlockSpec((1, nq, 1, tq), lambda h, i: (h, 0, 0, 0)),
                  ANY],
        out_specs=[pl.BlockSpec((1, tq, HEAD_PAD), lambda h, i: (h, i, 0)), pl.BlockSpec((1, tq, V_HEAD), lambda h, i: (h, i, 0))],
        out_shape=[jax.ShapeDtypeStruct((N_HEADS, l, HEAD_PAD), F32), jax.ShapeDtypeStruct((N_HEADS, l, V_HEAD), F32)],
        compiler_params=_cparams("parallel", "arbitrary"),
    )(q, k, v, do, lse_t, delta_t, token)


def _row_shards_mm(a, w_ref):
    a = a.astype(BF16)
    return sum(_mm(a[:, 256 * j:256 * (j + 1)], w_ref[j]) for j in range(4))


def _row_shards_mm_nt(a, w_ref):
    a = a.astype(BF16)
    return jnp.concatenate([_mm_nt(a, w_ref[j]) for j in range(4)], axis=-1)


def _merge_fwd(attn, y_ssm, gs, gm, x, grp_a, t):
    l = x.shape[0]

    def body(attn_ref, ys_ref, gs_ref, gm_ref, x_ref, wo_ref, wout_ref, ym_ref, mixed_ref, h_ref):
        y_mla = _row_shards_mm(attn_ref[...], wo_ref)
        ym_ref[...] = y_mla
        mixed = (_sigmoid(gs_ref[...]) * ys_ref[...] + _sigmoid(gm_ref[...]) * y_mla).astype(BF16)
        mixed_ref[...] = mixed
        h_ref[...] = x_ref[...] + _row_shards_mm(mixed, wout_ref)

    r = lambda: _rows(t, D_MODEL)
    return pl.pallas_call(
        body, name="merge_fwd", grid=(l // t,),
        in_specs=[r(), r(), r(), r(), r(), _member_block("w_o_mla"), _member_block("w_out")],
        out_specs=[r(), r(), r()],
        out_shape=[jax.ShapeDtypeStruct((l, D_MODEL), F32), jax.ShapeDtypeStruct((l, D_MODEL), BF16),
                   jax.ShapeDtypeStruct((l, D_MODEL), F32)],
        compiler_params=_cparams("parallel"),
    )(attn, y_ssm, gs, gm, x, grp_a, grp_a)


def _merge_bwd(dh, y_ssm, y_mla, gs, gm, grp_a, t):
    l = dh.shape[0]

    def body(dh_ref, ys_ref, ym_ref, gs_ref, gm_ref, wo_ref, wout_ref, dys_ref, dym_ref, dgs_ref, dgm_ref, dattn_ref):
        dmixed = _row_shards_mm_nt(dh_ref[...], wout_ref)
        sg = _sigmoid(gs_ref[...])
        sm = _sigmoid(gm_ref[...])
        dys_ref[...] = (dmixed * sg).astype(BF16)
        dgs_ref[...] = (dmixed * ys_ref[...] * sg * (1.0 - sg)).astype(BF16)
        dym = (dmixed * sm).astype(BF16)
        dym_ref[...] = dym
        dgm_ref[...] = (dmixed * ym_ref[...] * sm * (1.0 - sm)).astype(BF16)
        dattn_ref[...] = _row_shards_mm_nt(dym, wo_ref)

    r = lambda: _rows(t, D_MODEL)
    bf = jax.ShapeDtypeStruct((l, D_MODEL), BF16)
    return pl.pallas_call(
        body, name="merge_bwd", grid=(l // t,),
        in_specs=[r(), r(), r(), r(), r(), _member_block("w_o_mla"), _member_block("w_out")],
        out_specs=[r(), r(), r(), r(), r()],
        out_shape=[bf, bf, bf, bf, jax.ShapeDtypeStruct((l, D_MODEL), F32)],
        compiler_params=_cparams("parallel"),
    )(dh, y_ssm, y_mla, gs, gm, grp_a, grp_a)


def _mlp_fwd_bwd(h, tgt, g2, grp_a, t):
    l = h.shape[0]

    def body(h_ref, tgt_ref, g_ref, wu_ref, wd_ref, dh_ref, hn_ref, da_ref, hid_ref, dout_ref, loss_ref, dg_ref):
        first = pl.program_id(0) == 0
        hv = h_ref[...]
        g = g_ref[...]
        hn = _rms_fwd(hv, g, D_MODEL).astype(BF16)
        hn_ref[...] = hn
        out = hv
        relus = []
        for s in range(4):
            cols = slice(1024 * s, 1024 * (s + 1))
            relu = jnp.maximum(_mm(hn, wu_ref[s]), 0.0)
            relus.append(relu)
            hid = (relu * relu).astype(BF16)
            hid_ref[:, cols] = hid
            out = out + _mm(hid, wd_ref[s])
        err = out - tgt_ref[...]
        _accumulate(loss_ref, jnp.full((8, 128), jnp.sum(err * err) * (0.5 / D_MODEL), F32), first)
        dout = err * (1.0 / D_MODEL)
        doutb = dout.astype(BF16)
        dout_ref[...] = doutb
        dhn = jnp.zeros_like(hv)
        for s in range(4):
            da = (_mm_nt(doutb, wd_ref[s]) * (2.0 * relus[s])).astype(BF16)
            da_ref[:, 1024 * s:1024 * (s + 1)] = da
            dhn = dhn + _mm_nt(da, wu_ref[s])
        dx, dg_rows = _rms_bwd(hv, g, dhn, D_MODEL)
        dh_ref[...] = dout + dx
        _accumulate(dg_ref, _colsum(dg_rows), first)

    r = lambda w: _rows(t, w)
    return pl.pallas_call(
        body, name="mlp_fwd_bwd", grid=(l // t,),
        in_specs=[r(D_MODEL), r(D_MODEL), _resident((1, D_MODEL)), _member_block("w_up"), _member_block("w_down")],
        out_specs=[r(D_MODEL), r(D_MODEL), r(D_FF), r(D_FF), r(D_MODEL), pl.BlockSpec((8, 128), lambda i: (0, 0)),
                   pl.BlockSpec((1, D_MODEL), lambda i: (0, 0))],
        out_shape=[jax.ShapeDtypeStruct((l, D_MODEL), F32), jax.ShapeDtypeStruct((l, D_MODEL), BF16),
                   jax.ShapeDtypeStruct((l, D_FF), BF16), jax.ShapeDtypeStruct((l, D_FF), BF16),
                   jax.ShapeDtypeStruct((l, D_MODEL), BF16), jax.ShapeDtypeStruct((8, 128), F32),
                   jax.ShapeDtypeStruct((1, D_MODEL), F32)],
        compiler_params=_cparams("arbitrary"),
    )(h, tgt, g2, grp_a, grp_a)


def _wgrad(a, b, name):
    l, m = a.shape
    n = b.shape[1]
    bm = m if m <= 512 else 512
    bl = min(l, 2048 if n <= 1024 else 1024)

    def body(a_ref, b_ref, o_ref):
        _accumulate(o_ref, _mm_tn(a_ref[...], b_ref[...]), pl.program_id(1) == 0)

    return pl.pallas_call(
        body, name=name, grid=(m // bm, l // bl),
        in_specs=[pl.BlockSpec((bl, bm), lambda i, j: (j, i)), pl.BlockSpec((bl, n), lambda i, j: (j, 0))],
        out_specs=pl.BlockSpec((bm, n), lambda i, j: (i, 0)),
        out_shape=jax.ShapeDtypeStruct((m, n), F32),
        compiler_params=_cparams("parallel", "arbitrary"),
    )(a, b)


def _wgrad_into(a, b, member, cut, dest=None):
    group, off, rs, cs = _place_in_group(member)
    l = a.shape[0]
    bm = min(rs, 512)
    bl = min(l, 2048)
    nb = rs // bm
    if cut == "row":
        a_spec = pl.BlockSpec((bl, bm), lambda j, i, k: (k, j * nb + i))
        b_spec = pl.BlockSpec((bl, cs), lambda j, i, k: (k, 0))
    else:
        a_spec = pl.BlockSpec((bl, bm), lambda j, i, k: (k, i))
        b_spec = pl.BlockSpec((bl, cs), lambda j, i, k: (k, j))

    def body(a_ref, b_ref, *rest):
        o_ref = rest[-1]
        part = _mm_tn(a_ref[...], b_ref[...])

        @pl.when(pl.program_id(2) == 0)
        def _():
            o_ref[0] = part

        @pl.when(pl.program_id(2) != 0)
        def _():
            o_ref[0] += part

    operands, in_specs, aliases = [a, b], [a_spec, b_spec], {}
    if dest is not None:
        operands.append(dest)
        in_specs.append(ANY)
        aliases = {2: 0}
    return pl.pallas_call(
        body, name="wgrad_" + member, grid=(4, nb, l // bl), in_specs=in_specs,
        out_specs=pl.BlockSpec((1, bm, cs), lambda j, i, k: (j, off // bm + i, 0)),
        out_shape=jax.ShapeDtypeStruct((4, _group_rows(group), cs), F32), input_output_aliases=aliases,
        compiler_params=_cparams("parallel", "parallel", "arbitrary"),
    )(*operands)


def _adamw(w, g, m, v, name, g_off=0):
    r, c = w.shape
    br = r
    for cand in (256, 128, 64, 32, 16, 8):
        if r % cand == 0 and g_off % cand == 0:
            br = cand
            break

    def body(w_ref, g_ref, m_ref, v_ref, go_ref, d_ref, nm_ref, nv_ref):
        gv = g_ref[...]
        go_ref[...] = gv
        nm = ADAM_B1 * m_ref[...] + (1.0 - ADAM_B1) * gv
        nv = ADAM_B2 * v_ref[...] + (1.0 - ADAM_B2) * (gv * gv)
        m_hat = nm / (1.0 - ADAM_B1 ** ADAM_STEP)
        v_hat = nv / (1.0 - ADAM_B2 ** ADAM_STEP)
        d_ref[...] = -ADAM_LR * (m_hat / (jnp.sqrt(v_hat) + ADAM_EPS) + ADAM_WD * w_ref[...])
        nm_ref[...] = nm
        nv_ref[...] = nv

    spec = lambda: pl.BlockSpec((br, c), lambda i: (i, 0))
    g_spec = pl.BlockSpec((br, c), lambda i: (g_off // br + i, 0))
    shp = jax.ShapeDtypeStruct((r, c), F32)
    return pl.pallas_call(
        body, name=name, grid=(r // br,), in_specs=[spec(), g_spec, spec(), spec()],
        out_specs=[spec(), spec(), spec(), spec()], out_shape=[shp, shp, shp, shp], compiler_params=_cparams("parallel"),
    )(w, g, m, v)


def _place():
    return lax.axis_index("x"), lax.axis_index("y"), lax.axis_index("c")


def _other_chips(x, y):
    return [(1 - x, y), (x, 1 - y), (1 - x, 1 - y)]


ANY = pl.BlockSpec(memory_space=pl.ANY)


def _gather_weights(bufs):
    n = len(bufs)

    def body(*refs):
        outs, send_sems, recv_sems = refs[n:2 * n], refs[2 * n], refs[2 * n + 1]
        x, y, c = _place()
        chips = _other_chips(x, y)

        def part(g, px, py, pc):
            half = outs[g].shape[1] // 2
            return outs[g].at[2 * px + py, pl.ds(pl.multiple_of(pc * half, 16), half), :]

        def copy(k, src, dst, to):
            return pltpu.make_async_remote_copy(src_ref=src, dst_ref=dst, send_sem=send_sems.at[k], recv_sem=recv_sems.at[k],
                                                device_id=to, device_id_type=MESH)

        first = [copy(6 * g + j, part(g, x, y, c), part(g, x, y, c), (*chip, c)) for g in range(n) for j, chip in enumerate(chips)]
        for cp in first:
            cp.start()
        passed = []
        for g in range(n):
            for j, chip in enumerate(chips):
                landed = part(g, *chip, c)
                copy(6 * g + j, landed, landed, (x, y, c)).wait_recv()
                passed.append(copy(6 * g + 3 + j, landed, landed, (x, y, 1 - c)))
                passed[-1].start()
        for g in range(n):
            for j, chip in enumerate(chips):
                other = part(g, *chip, 1 - c)
                copy(6 * g + 3 + j, other, other, (x, y, c)).wait_recv()
        for cp in first + passed:
            cp.wait_send()

    return pl.pallas_call(
        body, name="gather_weights", in_specs=[ANY] * n, out_specs=[ANY] * n,
        out_shape=[jax.ShapeDtypeStruct(b.shape, b.dtype) for b in bufs], input_output_aliases={g: g for g in range(n)},
        scratch_shapes=[pltpu.SemaphoreType.DMA((6 * n,)), pltpu.SemaphoreType.DMA((6 * n,))],
    )(*bufs)


def _cast_shards(shards, group, place):
    width, members = GROUPS[group]
    rows = _group_rows(group)

    def body(place_ref, *refs):
        out = refs[-1]
        off = 0
        for ref, (_, r) in zip(refs[:-1], members):
            out[0, off:off + r, :] = ref[...].astype(BF16)
            off += r

    grid_spec = pltpu.PrefetchScalarGridSpec(
        num_scalar_prefetch=1, grid=(1,),
        in_specs=[pl.BlockSpec((r, width), lambda i, p: (0, 0)) for _, r in members],
        out_specs=pl.BlockSpec((1, rows, width), lambda i, p: (p[0], 0, 0)))
    return pl.pallas_call(
        body, name="cast_shards_" + group, grid_spec=grid_spec, out_shape=jax.ShapeDtypeStruct((4, rows, width), BF16),
        compiler_params=_cparams("arbitrary"),
    )(place, *[shards[name] for name, _ in members])


def _swap_gradient_halves(bufs):
    n = len(bufs)

    def body(*refs):
        ins, outs, send_sems, recv_sems = refs[:n], refs[n:2 * n], refs[2 * n], refs[2 * n + 1]
        x, y, c = _place()
        copies = []
        for g in range(n):
            half = ins[g].shape[1] // 2
            give = ins[g].at[:, pl.ds(pl.multiple_of((1 - c) * half, 8), half), :]
            copies.append(pltpu.make_async_remote_copy(src_ref=give, dst_ref=outs[g], send_sem=send_sems.at[g],
                                                       recv_sem=recv_sems.at[g], device_id=(x, y, 1 - c), device_id_type=MESH))
        for cp in copies:
            cp.start()
        for cp in copies:
            cp.wait()

    return pl.pallas_call(
        body, name="swap_gradient_halves", in_specs=[ANY] * n, out_specs=[ANY] * n,
        out_shape=[jax.ShapeDtypeStruct((4, b.shape[1] // 2, b.shape[2]), b.dtype) for b in bufs],
        scratch_shapes=[pltpu.SemaphoreType.DMA((n,)), pltpu.SemaphoreType.DMA((n,))],
    )(*bufs)


def _block_rows(h):
    return next(cand for cand in (256, 192, 128, 64, 32, 16) if h % cand == 0)


def _add_pair(buf, got, place, name):
    n, h, w = got.shape
    bh = _block_rows(h)
    nb = h // bh

    def body(place_ref, a_ref, b_ref, s_ref, sb_ref):
        s = a_ref[...] + b_ref[...]
        s_ref[...] = s
        sb_ref[...] = s.astype(BF16)

    spec = lambda: pl.BlockSpec((1, bh, w), lambda j, i, p: (j, i, 0))
    grid_spec = pltpu.PrefetchScalarGridSpec(
        num_scalar_prefetch=1, grid=(n, nb),
        in_specs=[pl.BlockSpec((1, bh, w), lambda j, i, p: (j, p[1] * nb + i, 0)), spec()], out_specs=[spec(), spec()])
    return pl.pallas_call(
        body, name=name, grid_spec=grid_spec,
        out_shape=[jax.ShapeDtypeStruct(got.shape, F32), jax.ShapeDtypeStruct(got.shape, BF16)],
        compiler_params=_cparams("parallel", "parallel"),
    )(place, buf, got)


def _scatter_to_chips(bufs):
    n = len(bufs)

    def body(*refs):
        ins, outs, send_sems, recv_sems = refs[:n], refs[n:2 * n], refs[2 * n], refs[2 * n + 1]
        x, y, c = _place()
        copies = [pltpu.make_async_remote_copy(src_ref=ins[g].at[2 * px + py], dst_ref=outs[g].at[j],
                                               send_sem=send_sems.at[3 * g + j], recv_sem=recv_sems.at[3 * g + j],
                                               device_id=(px, py, c), device_id_type=MESH)
                  for g in range(n) for j, (px, py) in enumerate(_other_chips(x, y))]
        for cp in copies:
            cp.start()
        for cp in copies:
            cp.wait()

    return pl.pallas_call(
        body, name="scatter_to_chips", in_specs=[ANY] * n, out_specs=[ANY] * n,
        out_shape=[jax.ShapeDtypeStruct((3,) + b.shape[1:], b.dtype) for b in bufs],
        scratch_shapes=[pltpu.SemaphoreType.DMA((3 * n,)), pltpu.SemaphoreType.DMA((3 * n,))],
    )(*bufs)


def _add_received(pair, got, place, name):
    _, h, w = pair.shape
    bh = _block_rows(h)
    nb = h // bh

    def body(place_ref, own_ref, got_ref, o_ref):
        o_ref[...] = ((own_ref[0] + got_ref[0].astype(F32)) + got_ref[1].astype(F32)) + got_ref[2].astype(F32)

    grid_spec = pltpu.PrefetchScalarGridSpec(
        num_scalar_prefetch=1, grid=(nb,),
        in_specs=[pl.BlockSpec((1, bh, w), lambda i, p: (p[0], i, 0)), pl.BlockSpec((3, bh, w), lambda i, p: (0, i, 0))],
        out_specs=pl.BlockSpec((bh, w), lambda i, p: (p[1] * nb + i, 0)))
    return pl.pallas_call(
        body, name=name, grid_spec=grid_spec, out_shape=jax.ShapeDtypeStruct((2 * h, w), F32),
        compiler_params=_cparams("parallel"),
    )(place, pair, got)


def _swap_reduced_halves(bufs):
    n = len(bufs)

    def body(*refs):
        outs, send_sems, recv_sems = refs[n:2 * n], refs[2 * n], refs[2 * n + 1]
        x, y, c = _place()
        copies = []
        for g in range(n):
            half = outs[g].shape[0] // 2
            own = outs[g].at[pl.ds(pl.multiple_of(c * half, 8), half), :]
            copies.append(pltpu.make_async_remote_copy(src_ref=own, dst_ref=own, send_sem=send_sems.at[g],
                                                       recv_sem=recv_sems.at[g], device_id=(x, y, 1 - c), device_id_type=MESH))
        for cp in copies:
            cp.start()
        for g in range(n):
            half = outs[g].shape[0] // 2
            other = outs[g].at[pl.ds(pl.multiple_of((1 - c) * half, 8), half), :]
            pltpu.make_async_remote_copy(src_ref=other, dst_ref=other, send_sem=send_sems.at[g], recv_sem=recv_sems.at[g],
                                         device_id=(x, y, 1 - c), device_id_type=MESH).wait_recv()
        for cp in copies:
            cp.wait_send()

    return pl.pallas_call(
        body, name="swap_reduced_halves", in_specs=[ANY] * n, out_specs=[ANY] * n,
        out_shape=[jax.ShapeDtypeStruct(b.shape, b.dtype) for b in bufs], input_output_aliases={g: g for g in range(n)},
        scratch_shapes=[pltpu.SemaphoreType.DMA((n,)), pltpu.SemaphoreType.DMA((n,))],
    )(*bufs)


HBM = pl.BlockSpec(memory_space=pltpu.HBM)
SEM = pl.BlockSpec(memory_space=pltpu.SEMAPHORE)


def _copies_start(name, bufs, n_copies, plan, after=None):
    n = len(bufs)
    extra = [] if after is None else [after]

    def body(*refs):
        sems = refs[n + len(extra):n + len(extra) + 2 * n_copies]
        x, y, c = _place()
        for i, (src, dst, dev) in enumerate(plan(refs[:n], x, y, c)):
            pltpu.make_async_remote_copy(src_ref=src, dst_ref=dst, send_sem=sems[i], recv_sem=sems[n_copies + i],
                                         device_id=dev, device_id_type=MESH).start()
        token = refs[-1]
        token[...] = jnp.zeros_like(token)

    out = pl.pallas_call(
        body, name=name,
        out_shape=[pltpu.SemaphoreType.DMA(())] * (2 * n_copies) + [pltpu.HBM(b.shape, b.dtype) for b in bufs]
        + [jax.ShapeDtypeStruct((8, 128), F32)],
        in_specs=[HBM] * n + [ANY] * len(extra),
        out_specs=[SEM] * (2 * n_copies) + [HBM] * n + [pl.BlockSpec(memory_space=pltpu.VMEM)],
        input_output_aliases={i: 2 * n_copies + i for i in range(n)},
        compiler_params=pltpu.CompilerParams(has_side_effects=pltpu.SideEffectType.DATAFLOW_SIDE_EFFECTING),
    )(*[pltpu.with_memory_space_constraint(b, pltpu.HBM) for b in bufs], *extra)
    return list(out[:2 * n_copies]), list(out[2 * n_copies:-1]), out[-1]


def _copies_wait(name, bufs, sems, after, plan):
    n = len(bufs)
    k = len(sems) // 2

    def body(*refs):
        sem_refs = refs[n:n + 2 * k]
        x, y, c = _place()
        for i, (sent, landed, dev) in enumerate(plan(refs[:n], x, y, c)):
            cp = pltpu.make_async_remote_copy(src_ref=sent, dst_ref=landed, send_sem=sem_refs[i], recv_sem=sem_refs[k + i],
                                              device_id=dev, device_id_type=MESH)
            cp.wait_send()
            cp.wait_recv()

    return pl.pallas_call(
        body, name=name, out_shape=[pltpu.HBM(b.shape, b.dtype) for b in bufs],
        in_specs=[HBM] * n + [SEM] * (2 * k) + [ANY], out_specs=[HBM] * n, input_output_aliases={i: i for i in range(n)},
        compiler_params=pltpu.CompilerParams(has_side_effects=pltpu.SideEffectType.DATAFLOW_SIDE_EFFECTING),
    )(*bufs, *sems, after)


class _GroupAExchange:
    def __init__(self, own_a, place, after):
        self.place = place
        self.gather = _copies_start("gather_a_start", [own_a], 3, self._gather_plan, after)

    @staticmethod
    def _gather_plan(refs, x, y, c):
        (wa,) = refs
        return [(wa.at[2 * x + y], wa.at[2 * x + y], (px, py, c)) for px, py in _other_chips(x, y)]

    @staticmethod
    def _gather_landed(refs, x, y, c):
        (wa,) = refs
        return [(wa.at[2 * x + y], wa.at[2 * px + py], (px, py, c)) for px, py in _other_chips(x, y)]

    def weights(self, after):
        sems, bufs, _ = self.gather
        return _copies_wait("gather_a_wait", bufs, sems, after, self._gather_landed)[0]

    def token_after_gather_start(self):
        return self.gather[2]


    def start_pair(self, ga):
        half = ga.shape[1] // 2
        land = lax.empty((4, half, ga.shape[2]), F32)

        def plan(refs, x, y, c):
            g, got = refs
            return [(g.at[:, pl.ds(pl.multiple_of((1 - c) * half, 8), half), :], got, (x, y, 1 - c))]

        self._pair_plan = plan
        self._pair = _copies_start("pair_a_start", [ga, land], 1, plan)
        return self._pair[2]

    def pair_done_start_scatter(self, after):
        sems, bufs, _ = self._pair
        ga, got = _copies_wait("pair_a_wait", bufs, sems, after, self._pair_plan)
        self._pair_f32, pair_bf16 = _add_pair(ga, got, self.place, "add_pair_a")
        land = lax.empty((3,) + pair_bf16.shape[1:], BF16)

        def plan(refs, x, y, c):
            mine, got = refs
            return [(mine.at[2 * px + py], got.at[j], (px, py, c)) for j, (px, py) in enumerate(_other_chips(x, y))]

        self._scatter_plan = plan
        self._scatter = _copies_start("scatter_a_start", [pair_bf16, land], 3, plan)
        return self._scatter[2]

    def scatter_done_start_join(self, after):
        sems, bufs, _ = self._scatter
        _, got = _copies_wait("scatter_a_wait", bufs, sems, after, self._scatter_plan)
        mine = _add_received(self._pair_f32, got, self.place, "add_received_a")
        half = mine.shape[0] // 2
        rows = lambda r, pc: r.at[pl.ds(pl.multiple_of(pc * half, 8), half), :]
        self._join_landed = lambda refs, x, y, c: [(rows(refs[0], c), rows(refs[0], 1 - c), (x, y, 1 - c))]
        self._join = _copies_start("join_a_start", [mine], 1,
                                   lambda refs, x, y, c: [(rows(refs[0], c), rows(refs[0], c), (x, y, 1 - c))])
        return self._join[2]

    def join_done(self, after):
        sems, bufs, _ = self._join
        self.reduced = _copies_wait("join_a_wait", bufs, sems, after, self._join_landed)[0]


def _all_sum_small(mine):
    rows, w = mine.shape

    def body(in_ref, out_ref, slots, send_sems, recv_sems):
        x, y, c = _place()
        me = 4 * x + 2 * y + c
        slots[me] = in_ref[...]
        copies = []
        for k in range(1, 8):
            peer = (1 - x if k & 4 else x, 1 - y if k & 2 else y, 1 - c if k & 1 else c)
            copies.append(pltpu.make_async_remote_copy(src_ref=in_ref, dst_ref=slots.at[me], send_sem=send_sems.at[k - 1],
                                                       recv_sem=recv_sems.at[k - 1], device_id=peer, device_id_type=MESH))
        for cp in copies:
            cp.start()
        for cp in copies:
            cp.wait()
        total = slots[0]
        for d in range(1, 8):
            total = total + slots[d]
        out_ref[...] = total

    return pl.pallas_call(
        body, name="all_sum_small", out_shape=jax.ShapeDtypeStruct((rows, w), F32),
        in_specs=[pl.BlockSpec(memory_space=pltpu.VMEM)], out_specs=pl.BlockSpec(memory_space=pltpu.VMEM),
        scratch_shapes=[pltpu.VMEM((8, rows, w), F32), pltpu.SemaphoreType.DMA((7,)), pltpu.SemaphoreType.DMA((7,))],
        compiler_params=pltpu.CompilerParams(vmem_limit_bytes=VMEM_LIMIT_V7X),
    )(mine)


def _join_column_shards(g):
    return jnp.transpose(g, (1, 0, 2)).reshape(g.shape[1], 4 * g.shape[2])


def _split_column_shards(w):
    r = w.shape[0]
    return jnp.transpose(w.reshape(r, 4, w.shape[1] // 4), (1, 0, 2))


def _small_rows(shape):
    return -(-int(np.prod(shape)) // 1024)


def _pack_small(vals):
    segs = []
    for name, shape in SMALL_WEIGHTS:
        flat = vals[name].reshape(-1)
        segs.append(jnp.pad(flat, (0, _small_rows(shape) * 1024 - flat.shape[0])))
    total = sum(s.shape[0] for s in segs) // 1024
    segs.append(jnp.zeros((-total % 8 * 1024,), F32))
    return jnp.concatenate(segs).reshape(-1, 1024)


def _unpack_small(packed):
    out, off = {}, 0
    for name, shape in SMALL_WEIGHTS:
        rows = _small_rows(shape)
        out[name] = packed[off:off + rows].reshape(-1)[:int(np.prod(shape))].reshape(shape)
        off += rows
    return out


def _pad_w_in(w):
    return jnp.concatenate([w[:, :1216], jnp.zeros((w.shape[0], 64), w.dtype), w[:, 1216:]], axis=1)


def _unpad_w_in(g):
    return jnp.concatenate([g[:, :1216], g[:, 1280:]], axis=1)


def _pad_heads(w):
    r = w.shape[0]
    return jnp.pad(w.reshape(r, N_HEADS, QK_HEAD), ((0, 0), (0, 0), (0, HEAD_PAD - QK_HEAD))).reshape(r, N_HEADS * HEAD_PAD)


def _unpad_heads(g):
    r = g.shape[0]
    return g.reshape(r, N_HEADS, HEAD_PAD)[:, :, :QK_HEAD].reshape(r, N_HEADS * QK_HEAD)


def _local_step(x, positions, tgt, grp, small, ex):
    l = x.shape[0]
    t = min(l, 512)
    t_mlp = min(l, 256)
    tq = min(l, 512)
    tc = min(l, 256)
    row = lambda v: v.reshape(1, -1).astype(F32)

    w_in_p = _pad_w_in(_join_column_shards(grp["b"]))
    w_qb_p = _pad_heads(_join_column_shards(grp["c"]))
    g1, g2 = row(small["norm_mix"]), row(small["norm_mlp"])
    gqa, gkva = row(small["q_a_norm"]), row(small["kv_a_norm"])
    gq = jnp.pad(row(small["q_norm"]), ((0, 0), (0, HEAD_PAD - QK_HEAD)))
    gk = jnp.pad(row(small["k_norm"]), ((0, 0), (0, HEAD_PAD - QK_HEAD)))
    half = QK_ROPE // 2
    inv_freq = ROPE_THETA ** (-jnp.arange(half, dtype=F32) / half)
    invf = jnp.concatenate([inv_freq, inv_freq, jnp.zeros((64,), F32)]).reshape(1, 128)
    sgn = jnp.concatenate([-jnp.ones((half,), F32), jnp.ones((half,), F32), jnp.zeros((64,), F32)]).reshape(1, 128)
    pos = positions.reshape(l, 1)

    a_re, a_im = small["ssm_a_re"], small["ssm_a_im"]
    log_dt = small["ssm_log_dt"].reshape(SSM_GROUPS, 1)
    to_gcp = lambda b: jnp.transpose(b, (0, 2, 1)).reshape(SSM_WIDTH, SSM_STATE)
    from_gcp = lambda b: jnp.transpose(b.reshape(SSM_GROUPS, SSM_GROUP_CH, SSM_STATE), (0, 2, 1))
    b_re, b_im = to_gcp(small["ssm_b_re"]), to_gcp(small["ssm_b_im"])
    c_re, c_im = small["ssm_c_re"].reshape(SSM_WIDTH, SSM_STATE), small["ssm_c_im"].reshape(SSM_WIDTH, SSM_STATE)
    wb, wc, tabs_fwd, tabs_rev = _ssm_param_fwd(a_re, a_im, log_dt, b_re, b_im, c_re, c_im)
    dskip = row(small["ssm_d"])
    b_glu = row(small["b_glu"])

    u, lat, gs, gm = _in_proj_fwd(x, g1, w_in_p, t, ex.token_after_gather_start())
    xr, xi, y, y_ssm = _ssm_fwd(u, wb, wc, tabs_fwd, dskip, grp["d"], b_glu, grp["e"], tc)
    q, k, v = _mla_pre_fwd(lat, pos, invf, sgn, gqa, gkva, gq, gk, w_qb_p, grp["d"], t)
    attn, lse = _attn_fwd(q, k, v, tq)
    grp_a = ex.weights(attn)
    y_mla, mixed, h = _merge_fwd(attn, y_ssm, gs, gm, x, grp_a, t)
    dh, hn, da, hid, dout, loss_blk, g_norm_mlp = _mlp_fwd_bwd(h, tgt, g2, grp_a, t_mlp)

    grads = {}
    ga = _wgrad_into(hn, da, "w_up", "col", _wgrad_into(hid, dout, "w_down", "row"))
    dys, dym, dgs, dgm, dattn = _merge_bwd(dh, y_ssm, y_mla, gs, gm, grp_a, t)
    ga = _wgrad_into(attn, dym, "w_o_mla", "row", _wgrad_into(mixed, dh, "w_out", "row", ga))

    dq, delta = _attn_bwd_dq(q, k, v, attn, dattn, lse, tq, ex.start_pair(ga))
    lanes = lambda a: a.reshape(N_HEADS, l // tq, 1, tq)
    dk, dv = _attn_bwd_dkv(q, k, v, dattn, lanes(lse), lanes(delta), tq, ex.pair_done_start_scatter(dq))
    d_lat, ql, dq0, ckn, dkv, g_qa, g_kva, g_q, g_k = _mla_pre_bwd(lat, pos, invf, sgn, gqa, gkva, gq, gk, w_qb_p, grp["d"],
                                                                    dq, dk, dv, t, ex.scatter_done_start_join(dk))
    ex.join_done(d_lat)
    grads["c"] = _split_column_shards(_unpad_heads(_wgrad(ql, dq0, "wgrad_q_b")))

    d_u, adj, dy, z, z2, dpre, g_b_glu, g_d, g_lr, g_li = _ssm_bwd(
        dys, y, u, xr, xi, wb, wc, tabs_rev, dskip, grp["d"], b_glu, grp["e"], tc)
    grads["d"] = _wgrad_into(z, dpre, "w_glu", "row", _wgrad_into(ckn, dkv, "w_kv_b", "col"))
    grads["e"] = _wgrad_into(z2, dys, "w_o_ssm", "col")
    g_ar, g_ai, g_ldt, g_br, g_bi, g_cr, g_ci = _ssm_param_bwd(
        a_re, a_im, log_dt, b_re, b_im, g_lr, g_li, _wgrad(u, adj, "wgrad_ssm_b"), _wgrad(dy, xr, "wgrad_ssm_c_re"),
        _wgrad(dy, xi, "wgrad_ssm_c_im"))

    grad_x, xn, dproj, g_norm_mix = _in_proj_bwd(x, g1, w_in_p, d_u, d_lat, dgs, dgm, dh, t)
    grads["b"] = _split_column_shards(_unpad_w_in(_wgrad(xn, dproj, "wgrad_in")))

    g_small = {
        "norm_mix": g_norm_mix.reshape(-1), "norm_mlp": g_norm_mlp.reshape(-1), "q_a_norm": g_qa.reshape(-1),
        "kv_a_norm": g_kva.reshape(-1), "q_norm": g_q.reshape(-1)[:QK_HEAD], "k_norm": g_k.reshape(-1)[:QK_HEAD],
        "ssm_a_re": g_ar, "ssm_a_im": g_ai, "ssm_log_dt": g_ldt.reshape(-1),
        "ssm_b_re": from_gcp(g_br), "ssm_b_im": from_gcp(g_bi),
        "ssm_c_re": g_cr.reshape(SSM_GROUPS, SSM_GROUP_CH, SSM_STATE), "ssm_c_im": g_ci.reshape(SSM_GROUPS, SSM_GROUP_CH, SSM_STATE),
        "ssm_d": g_d.reshape(SSM_GROUPS, SSM_GROUP_CH), "b_glu": g_b_glu.reshape(-1),
    }
    return loss_blk[0, 0], grad_x, grads, g_small


def kernel(x, positions, norm_mix, w_in, q_a_norm, kv_a_norm, w_q_b, w_kv_b, q_norm, k_norm, w_o_mla, ssm_a_re, ssm_a_im, ssm_log_dt, ssm_b_re, ssm_b_im, ssm_c_re, ssm_c_im, ssm_d, w_glu, b_glu, w_o_ssm, w_out, norm_mlp, w_up, w_down, loss_target, m_norm_mix, m_w_in, m_q_a_norm, m_kv_a_norm, m_w_q_b, m_w_kv_b, m_q_norm, m_k_norm, m_w_o_mla, m_ssm_a_re, m_ssm_a_im, m_ssm_log_dt, m_ssm_b_re, m_ssm_b_im, m_ssm_c_re, m_ssm_c_im, m_ssm_d, m_w_glu, m_b_glu, m_w_o_ssm, m_w_out, m_norm_mlp, m_w_up, m_w_down, v_norm_mix, v_w_in, v_q_a_norm, v_kv_a_norm, v_w_q_b, v_w_kv_b, v_q_norm, v_k_norm, v_w_o_mla, v_ssm_a_re, v_ssm_a_im, v_ssm_log_dt, v_ssm_b_re, v_ssm_b_im, v_ssm_c_re, v_ssm_c_im, v_ssm_d, v_w_glu, v_b_glu, v_w_o_ssm, v_w_out, v_norm_mlp, v_w_up, v_w_down):
    args = dict(locals())
    w = {n: args[n][0] for n in WEIGHT_ORDER}
    m = {n: args["m_" + n][0] for n in WEIGHT_ORDER}
    v = {n: args["v_" + n][0] for n in WEIGHT_ORDER}
    big_names = [n for n, *_ in BIG_WEIGHTS]
    small_names = [n for n, _ in SMALL_WEIGHTS]

    place = jnp.stack([2 * lax.axis_index("x") + lax.axis_index("y"), lax.axis_index("c")]).astype(jnp.int32)
    groups = [g for g in sorted(GROUPS) if g != "a"]

    gathered = _gather_weights([_cast_shards(w, g, place) for g in groups])
    grp = dict(zip(groups, gathered))
    ex = _GroupAExchange(_cast_shards(w, "a", place), place, gathered[0])
    small = {n: w[n] for n in small_names}

    loss_local, grad_x, grads, g_small = _local_step(x[0], positions[0], loss_target[0], grp, small, ex)
    loss = lax.psum(loss_local, ("x", "y", "c"))

    bufs = [grads[g] for g in groups]
    pairs = [_add_pair(b, got, place, "add_pair_" + g) for g, b, got in zip(groups, bufs, _swap_gradient_halves(bufs))]
    landed = _scatter_to_chips([p[1] for p in pairs])
    halves = [_add_received(p[0], got, place, "add_received_" + g) for g, p, got in zip(groups, pairs, landed)]
    reduced = dict(zip(groups, _swap_reduced_halves(halves)))
    reduced["a"] = ex.reduced

    small_sum = _all_sum_small(_pack_small(g_small))

    grad_w, delta_w, new_m, new_v = {}, {}, {}, {}
    for n in big_names:
        g, off, _, _ = _place_in_group(n)
        grad_w[n], delta_w[n], new_m[n], new_v[n] = _adamw(w[n], reduced[g], m[n], v[n], "adamw_" + n, off)
    g_s, d_s, m_s, v_s = _adamw(_pack_small(small), small_sum, _pack_small({n: m[n] for n in small_names}),
                                _pack_small({n: v[n] for n in small_names}), "adamw_small")
    g_s, d_s, m_s, v_s = _unpack_small(g_s), _unpack_small(d_s), _unpack_small(m_s), _unpack_small(v_s)
    for n in small_names:
        grad_w[n], delta_w[n], new_m[n], new_v[n] = g_s[n], d_s[n], m_s[n], v_s[n]

    lead = lambda d: [d[n][None] for n in WEIGHT_ORDER]
    return (loss, grad_x[None], *lead(grad_w), *lead(delta_w), *lead(new_m), *lead(new_v))
```

```python
import functools
import math

import jax
import jax.numpy as jnp
import numpy as np
from jax import lax
from jax.experimental import pallas as pl
from jax.experimental.pallas import tpu as pltpu

F32 = jnp.float32
BF16 = jnp.bfloat16

D_MODEL = 1024
SSM_GROUPS = 32
SSM_GROUP_CH = 16
SSM_WIDTH = 512
SSM_STATE = 64
GP = SSM_GROUPS * SSM_STATE
N_HEADS = 8
QK_NOPE = 128
QK_ROPE = 64
QK_HEAD = 192
HEAD_PAD = 256
V_HEAD = 128
Q_LORA = 384
KV_LORA = 256
LAT_W = 768
D_IN = 3264
D_IN_PAD = 3328
D_FF = 4096
ROPE_THETA = 10000.0
EPS = 1e-6
ATT_SCALE = QK_HEAD ** -0.5

ADAM_LR = 0.001
ADAM_B1 = 0.9
ADAM_B2 = 0.999
ADAM_EPS = 1e-08
ADAM_WD = 0.01
ADAM_STEP = 10

VMEM_LIMIT_V7X = 56 * 1024 * 1024
MESH = pl.DeviceIdType.MESH

BIG_WEIGHTS = (
    ("w_in", 1024, 3264, "col"),
    ("w_q_b", 384, 1536, "col"),
    ("w_kv_b", 256, 2048, "col"),
    ("w_o_mla", 1024, 1024, "row"),
    ("w_glu", 512, 512, "row"),
    ("w_o_ssm", 512, 1024, "col"),
    ("w_out", 1024, 1024, "row"),
    ("w_up", 1024, 4096, "col"),
    ("w_down", 4096, 1024, "row"),
)
GROUPS = {
    "a": (1024, (("w_down", 1024), ("w_up", 1024), ("w_o_mla", 256), ("w_out", 256))),
    "b": (816, (("w_in", 1024),)),
    "c": (384, (("w_q_b", 384),)),
    "d": (512, (("w_kv_b", 256), ("w_glu", 128))),
    "e": (256, (("w_o_ssm", 512),)),
}


def _group_rows(group):
    return sum(r for _, r in GROUPS[group][1])


def _place_in_group(name):
    for group, (width, members) in GROUPS.items():
        off = 0
        for member, rows in members:
            if member == name:
                return group, off, rows, width
            off += rows
    raise KeyError(name)


SMALL_WEIGHTS = (
    ("norm_mix", (1024,)), ("q_a_norm", (384,)), ("kv_a_norm", (256,)), ("q_norm", (192,)), ("k_norm", (192,)),
    ("ssm_a_re", (32, 64)), ("ssm_a_im", (32, 64)), ("ssm_log_dt", (32,)),
    ("ssm_b_re", (32, 64, 16)), ("ssm_b_im", (32, 64, 16)), ("ssm_c_re", (32, 16, 64)), ("ssm_c_im", (32, 16, 64)),
    ("ssm_d", (32, 16)), ("b_glu", (512,)), ("norm_mlp", (1024,)),
)
WEIGHT_ORDER = ('norm_mix', 'w_in', 'q_a_norm', 'kv_a_norm', 'w_q_b', 'w_kv_b', 'q_norm', 'k_norm', 'w_o_mla', 'ssm_a_re',
                'ssm_a_im', 'ssm_log_dt', 'ssm_b_re', 'ssm_b_im', 'ssm_c_re', 'ssm_c_im', 'ssm_d', 'w_glu', 'b_glu',
                'w_o_ssm', 'w_out', 'norm_mlp', 'w_up', 'w_down')


def _cparams(*sem):
    return pltpu.CompilerParams(dimension_semantics=sem if sem else None, vmem_limit_bytes=VMEM_LIMIT_V7X)


def _resident(shape, index=None):
    index = (0,) * len(shape) if index is None else index
    return pl.BlockSpec(shape, lambda *_: index, pipeline_mode=pl.Buffered(1))


def _member_block(name):
    _, off, rows, width = _place_in_group(name)
    return _resident((4, rows, width), (0, off // rows, 0))


def _rows(t, width):
    return pl.BlockSpec((t, width), lambda i: (i, 0))


def _mm(a, b):
    return jnp.dot(a.astype(BF16), b.astype(BF16), preferred_element_type=F32)


def _mm_nt(a, b):
    return lax.dot_general(a.astype(BF16), b.astype(BF16), (((1,), (1,)), ((), ())), preferred_element_type=F32)


def _mm_tn(a, b):
    return lax.dot_general(a.astype(BF16), b.astype(BF16), (((0,), (0,)), ((), ())), preferred_element_type=F32)


def _rms_fwd(x, g, n):
    r = lax.rsqrt(jnp.sum(x * x, axis=-1, keepdims=True) * (1.0 / n) + EPS)
    return x * r * g


def _rms_bwd(x, g, dy, n):
    r = lax.rsqrt(jnp.sum(x * x, axis=-1, keepdims=True) * (1.0 / n) + EPS)
    xh = x * r
    dxh = dy * g
    dx = r * (dxh - xh * (jnp.sum(dxh * xh, axis=-1, keepdims=True) * (1.0 / n)))
    return dx, dy * xh


def _colsum(a):
    return jnp.sum(a, axis=0, keepdims=True)


def _accumulate(ref, value, first):
    @pl.when(first)
    def _():
        ref[...] = value

    @pl.when(jnp.logical_not(first))
    def _():
        ref[...] += value


def _sigmoid(a):
    return 1.0 / (1.0 + jnp.exp(-a))


GELU_C = math.sqrt(2.0 / math.pi)
GELU_A = 0.044715


def _gelu(y):
    return 0.5 * y * (1.0 + jnp.tanh(GELU_C * (y + GELU_A * y * y * y)))


def _gelu_grad(y):
    t = jnp.tanh(GELU_C * (y + GELU_A * y * y * y))
    return 0.5 * (1.0 + t) + 0.5 * y * (1.0 - t * t) * GELU_C * (1.0 + 3.0 * GELU_A * y * y)


def _in_proj_fwd(x, g1, w_in_p, t, token):
    l = x.shape[0]

    def body(x_ref, g_ref, w_ref, token_ref, u_ref, lat_ref, gs_ref, gm_ref):
        xn = _rms_fwd(x_ref[...], g_ref[...], D_MODEL).astype(BF16)
        u_ref[...] = _mm(xn, w_ref[:, 0:512])
        lat_ref[...] = _mm(xn, w_ref[:, 512:1280])
        gs_ref[...] = _mm(xn, w_ref[:, 1280:2304])
        gm_ref[...] = _mm(xn, w_ref[:, 2304:3328])

    return pl.pallas_call(
        body, name="in_proj_fwd", grid=(l // t,),
        in_specs=[_rows(t, D_MODEL), _resident((1, D_MODEL)), _resident((D_MODEL, D_IN_PAD)), ANY],
        out_specs=[_rows(t, 512), _rows(t, LAT_W), _rows(t, D_MODEL), _rows(t, D_MODEL)],
        out_shape=[jax.ShapeDtypeStruct((l, 512), F32), jax.ShapeDtypeStruct((l, LAT_W), F32),
                   jax.ShapeDtypeStruct((l, D_MODEL), F32), jax.ShapeDtypeStruct((l, D_MODEL), F32)],
        compiler_params=_cparams("parallel"),
    )(x, g1, w_in_p, token)


def _in_proj_bwd(x, g1, w_in_p, d_u, d_lat, d_gs, d_gm, dh, t):
    l = x.shape[0]

    def body(x_ref, g_ref, w_ref, du_ref, dlat_ref, dgs_ref, dgm_ref, dh_ref, gx_ref, xn_ref, dproj_ref, dg_ref):
        xv = x_ref[...]
        g = g_ref[...]
        xn_ref[...] = _rms_fwd(xv, g, D_MODEL).astype(BF16)
        dproj_ref[:, 0:512] = du_ref[...]
        dproj_ref[:, 512:1280] = dlat_ref[...]
        dproj_ref[:, 1280:2304] = dgs_ref[...]
        dproj_ref[:, 2304:3328] = dgm_ref[...]
        dxn = _mm_nt(dproj_ref[...], w_ref[...])
        dx, dg_rows = _rms_bwd(xv, g, dxn, D_MODEL)
        gx_ref[...] = dh_ref[...] + dx
        _accumulate(dg_ref, _colsum(dg_rows), pl.program_id(0) == 0)

    return pl.pallas_call(
        body, name="in_proj_bwd", grid=(l // t,),
        in_specs=[_rows(t, D_MODEL), _resident((1, D_MODEL)), _resident((D_MODEL, D_IN_PAD)), _rows(t, 512),
                  _rows(t, LAT_W), _rows(t, D_MODEL), _rows(t, D_MODEL), _rows(t, D_MODEL)],
        out_specs=[_rows(t, D_MODEL), _rows(t, D_MODEL), _rows(t, D_IN_PAD), pl.BlockSpec((1, D_MODEL), lambda i: (0, 0))],
        out_shape=[jax.ShapeDtypeStruct((l, D_MODEL), F32), jax.ShapeDtypeStruct((l, D_MODEL), BF16),
                   jax.ShapeDtypeStruct((l, D_IN_PAD), BF16), jax.ShapeDtypeStruct((1, D_MODEL), F32)],
        compiler_params=_cparams("arbitrary"),
    )(x, g1, w_in_p, d_u, d_lat, d_gs, d_gm, dh)


def _ssm_param_fn(a_re, a_im, log_dt, b_re, b_im):
    dt = jnp.exp(log_dt)
    er = jnp.exp(a_re * dt)
    lr = er * jnp.cos(a_im * dt)
    li = er * jnp.sin(a_im * dt)
    den = a_re * a_re + a_im * a_im
    nr = lr - 1.0
    kr = (nr * a_re + li * a_im) / den
    ki = (li * a_re - nr * a_im) / den
    rows = lambda k: jnp.broadcast_to(k[:, None, :], (SSM_GROUPS, SSM_GROUP_CH, SSM_STATE)).reshape(SSM_WIDTH, SSM_STATE)
    krt, kit = rows(kr), rows(ki)
    return lr, li, krt * b_re - kit * b_im, krt * b_im + kit * b_re


def _state_selector():
    row = lax.broadcasted_iota(jnp.int32, (SSM_STATE, GP), 0)
    col = lax.broadcasted_iota(jnp.int32, (SSM_STATE, GP), 1)
    return jnp.where(jnp.bitwise_and(col, SSM_STATE - 1) == row, 1.0, 0.0).astype(BF16)


def _own_group(rows, rows_per_group_log2):
    row = lax.broadcasted_iota(jnp.int32, (rows, GP), 0)
    col = lax.broadcasted_iota(jnp.int32, (rows, GP), 1)
    return jnp.right_shift(row, rows_per_group_log2) == jnp.right_shift(col, 6)


def _three_bf16(x):
    hi = x.astype(BF16)
    rest = x - hi.astype(F32)
    mid = rest.astype(BF16)
    return hi, mid, (rest - mid.astype(F32)).astype(BF16)


def _spread(x, sel):
    return sum(jnp.dot(part, sel, preferred_element_type=F32) for part in _three_bf16(x))


def _collect(xw, sel):
    return sum(lax.dot_general(part, sel, (((1,), (1,)), ((), ())), preferred_element_type=F32) for part in _three_bf16(xw))


def _ssm_param_fwd(a_re, a_im, log_dt, b_re, b_im, c_re, c_im):
    def body(ar_ref, ai_ref, ldt_ref, br_ref, bi_ref, cr_ref, ci_ref, wb_ref, wct_ref, tf_ref, tr_ref):
        lr, li, bbr, bbi = _ssm_param_fn(ar_ref[...], ai_ref[...], ldt_ref[...], br_ref[...], bi_ref[...])
        sel = _state_selector()
        own16 = _own_group(SSM_WIDTH, 4)
        own1 = _own_group(SSM_GROUPS, 0)
        block = lambda m: jnp.where(own16, jnp.dot(m.astype(BF16), sel, preferred_element_type=F32), 0.0).astype(BF16)
        wb_ref[:, 0:GP] = block(bbr)
        wb_ref[:, GP:2 * GP] = block(bbi)
        wct_ref[:, 0:GP] = block(cr_ref[...])
        wct_ref[:, GP:2 * GP] = block(-ci_ref[...])
        flat = lambda m: _colsum(jnp.where(own1, _spread(m, sel), 0.0))
        pr, pi = [], []
        qr, qi = lr, li
        for _ in range(8):
            pr.append(flat(qr))
            pi.append(flat(qi))
            qr, qi = qr * lr - qi * li, qr * li + qi * lr
        row = lax.broadcasted_iota(jnp.int32, (8, GP), 0)
        for n, k in enumerate((1, 2, 4)):
            tf_ref[2 * n] = jnp.where(row >= k, pr[k - 1], 0.0)
            tf_ref[2 * n + 1] = jnp.where(row >= k, pi[k - 1], 0.0)
            tr_ref[2 * n] = jnp.where(row < 8 - k, pr[k - 1], 0.0)
            tr_ref[2 * n + 1] = jnp.where(row < 8 - k, -pi[k - 1], 0.0)
        pick = lambda vals: sum(jnp.where(row == j, v, 0.0) for j, v in enumerate(vals))
        tf_ref[6] = pick(pr)
        tf_ref[7] = pick(pi)
        tr_ref[6] = pick(pr[::-1])
        tr_ref[7] = pick([-v for v in pi[::-1]])

    return pl.pallas_call(
        body, name="ssm_param_fwd",
        out_shape=[jax.ShapeDtypeStruct((SSM_WIDTH, 2 * GP), BF16), jax.ShapeDtypeStruct((SSM_WIDTH, 2 * GP), BF16),
                   jax.ShapeDtypeStruct((8, 8, GP), F32), jax.ShapeDtypeStruct((8, 8, GP), F32)],
        compiler_params=_cparams(),
    )(a_re, a_im, log_dt, b_re, b_im, c_re, c_im)


def _ssm_param_bwd(a_re, a_im, log_dt, b_re, b_im, g_lr, g_li, g_wb, g_wct_re, g_wct_im):
    def body(ar_ref, ai_ref, ldt_ref, br_ref, bi_ref, glr_ref, gli_ref, gwb_ref, gcr_ref, gci_ref,
             o_ar, o_ai, o_ldt, o_br, o_bi, o_cr, o_ci):
        sel = _state_selector()
        own16 = _own_group(SSM_WIDTH, 4)
        own1 = _own_group(SSM_GROUPS, 0)
        blocks = lambda m: _collect(jnp.where(own16, m, 0.0), sel)
        unflat = lambda v: _collect(jnp.where(own1, v, 0.0), sel)
        _, vjp = jax.vjp(_ssm_param_fn, ar_ref[...], ai_ref[...], ldt_ref[...], br_ref[...], bi_ref[...])
        d_ar, d_ai, d_ldt, d_br, d_bi = vjp((unflat(glr_ref[...]), unflat(gli_ref[...]),
                                             blocks(gwb_ref[:, 0:GP]), blocks(gwb_ref[:, GP:2 * GP])))
        o_ar[...] = d_ar
        o_ai[...] = d_ai
        o_ldt[...] = d_ldt
        o_br[...] = d_br
        o_bi[...] = d_bi
        o_cr[...] = blocks(gcr_ref[...])
        o_ci[...] = -blocks(gci_ref[...])

    g, p = SSM_GROUPS, SSM_STATE
    gp = jax.ShapeDtypeStruct((g, p), F32)
    gcp = jax.ShapeDtypeStruct((SSM_WIDTH, p), F32)
    return pl.pallas_call(
        body, name="ssm_param_bwd", out_shape=[gp, gp, jax.ShapeDtypeStruct((g, 1), F32), gcp, gcp, gcp, gcp],
        compiler_params=_cparams(),
    )(a_re, a_im, log_dt, b_re, b_im, g_lr, g_li, g_wb, g_wct_re, g_wct_im)


SCAN_STRIP = 512


def _scan_chunk(inr_ref, ini_ref, outr_ref, outi_ref, cr_ref, ci_ref, tab_ref, tc, reverse):
    n_blocks = tc // 8

    def block(j, _):
        i = (n_blocks - 1 - j) if reverse else j
        rows = pl.ds(pl.multiple_of(i * 8, 8), 8)
        for s in range(GP // SCAN_STRIP):
            sl = pl.ds(s * SCAN_STRIP, SCAN_STRIP)
            xr = inr_ref[rows, sl]
            xi = ini_ref[rows, sl]
            for n, k in enumerate((1, 2, 4)):
                shift = (8 - k) if reverse else k
                sr = pltpu.roll(xr, shift, 0)
                si = pltpu.roll(xi, shift, 0)
                mr = tab_ref[2 * n, :, sl]
                mi = tab_ref[2 * n + 1, :, sl]
                xr, xi = xr + mr * sr - mi * si, xi + mr * si + mi * sr
            qr = tab_ref[6, :, sl]
            qi = tab_ref[7, :, sl]
            cr = cr_ref[:, sl]
            ci = ci_ref[:, sl]
            xr, xi = xr + qr * cr - qi * ci, xi + qr * ci + qi * cr
            outr_ref[rows, sl] = xr
            outi_ref[rows, sl] = xi
            edge = 0 if reverse else 7
            cr_ref[:, sl] = jnp.broadcast_to(xr[edge:edge + 1, :], (8, SCAN_STRIP))
            ci_ref[:, sl] = jnp.broadcast_to(xi[edge:edge + 1, :], (8, SCAN_STRIP))
        return 0

    lax.fori_loop(0, n_blocks, block, 0)


def _glu_pre(z, wg_ref):
    return sum(_mm(z[:, 128 * j:128 * (j + 1)], wg_ref[j]) for j in range(4))


def _ssm_fwd(u, wb, wc, tabs, dskip, grp_d, b_glu, grp_e, tc):
    l = u.shape[0]

    def body(u_ref, wb_ref, wc_ref, tab_ref, d_ref, wg_ref, bg_ref, wo_ref, xr_ref, xi_ref, y_ref, ys_ref,
             bur, bui, cr, ci):
        @pl.when(pl.program_id(0) == 0)
        def _():
            cr[...] = jnp.zeros_like(cr)
            ci[...] = jnp.zeros_like(ci)

        uv = u_ref[...]
        ub = uv.astype(BF16)
        bur[...] = _mm(ub, wb_ref[:, 0:GP])
        bui[...] = _mm(ub, wb_ref[:, GP:2 * GP])
        _scan_chunk(bur, bui, xr_ref, xi_ref, cr, ci, tab_ref, tc, False)
        y = _mm_nt(xr_ref[...], wc_ref[:, 0:GP]) + _mm_nt(xi_ref[...], wc_ref[:, GP:2 * GP]) + d_ref[...] * uv
        y_ref[...] = y
        z = _gelu(y)
        z2 = z * _sigmoid(_glu_pre(z, wg_ref) + bg_ref[...])
        for s in range(4):
            ys_ref[:, 256 * s:256 * (s + 1)] = _mm(z2, wo_ref[s])

    return pl.pallas_call(
        body, name="ssm_fwd", grid=(l // tc,),
        in_specs=[_rows(tc, 512), _resident((512, 2 * GP)), _resident((512, 2 * GP)), _resident((8, 8, GP)),
                  _resident((1, 512)), _member_block("w_glu"), _resident((1, 512)), _member_block("w_o_ssm")],
        out_specs=[_rows(tc, GP), _rows(tc, GP), _rows(tc, 512), _rows(tc, D_MODEL)],
        out_shape=[jax.ShapeDtypeStruct((l, GP), F32), jax.ShapeDtypeStruct((l, GP), F32),
                   jax.ShapeDtypeStruct((l, 512), F32), jax.ShapeDtypeStruct((l, D_MODEL), F32)],
        scratch_shapes=[pltpu.VMEM((tc, GP), F32), pltpu.VMEM((tc, GP), F32), pltpu.VMEM((8, GP), F32),
                        pltpu.VMEM((8, GP), F32)],
        compiler_params=_cparams("arbitrary"),
    )(u, wb, wc, tabs, dskip, grp_d, b_glu, grp_e)


def _ssm_bwd(dys, y, u, xr, xi, wb, wc, tabs_rev, dskip, grp_d, b_glu, grp_e, tc):
    l = u.shape[0]
    nc = l // tc

    def body(dys_ref, y_ref, u_ref, xr_ref, xi_ref, wb_ref, wc_ref, tab_ref, d_ref, wg_ref, bg_ref, wo_ref,
             du_ref, a_ref, dy_ref, z_ref, z2_ref, dpre_ref, gb_ref, gd_ref, glr_ref, gli_ref,
             dxr, dxi, ar, ai, cr, ci):
        first = pl.program_id(0) == 0

        @pl.when(first)
        def _():
            cr[...] = jnp.zeros_like(cr)
            ci[...] = jnp.zeros_like(ci)

        yv = y_ref[...]
        uv = u_ref[...]
        dz2 = sum(_mm_nt(dys_ref[:, 256 * j:256 * (j + 1)], wo_ref[j]) for j in range(4))
        z = _gelu(yv)
        s = _sigmoid(_glu_pre(z, wg_ref) + bg_ref[...])
        dpre = dz2 * z * s * (1.0 - s)
        dpreb = dpre.astype(BF16)
        dz = dz2 * s + jnp.concatenate([_mm_nt(dpreb, wg_ref[j]) for j in range(4)], axis=-1)
        dy = dz * _gelu_grad(yv)
        z_ref[...] = z.astype(BF16)
        z2_ref[...] = (z * s).astype(BF16)
        dpre_ref[...] = dpre.astype(BF16)
        dy_ref[...] = dy.astype(BF16)
        _accumulate(gb_ref, _colsum(dpre), first)
        _accumulate(gd_ref, _colsum(dy * uv), first)

        dyb = dy.astype(BF16)
        dxr[...] = _mm(dyb, wc_ref[:, 0:GP])
        dxi[...] = _mm(dyb, wc_ref[:, GP:2 * GP])
        ar[pl.ds(tc, 8), :] = cr[...]
        ai[pl.ds(tc, 8), :] = ci[...]
        _scan_chunk(dxr, dxi, ar, ai, cr, ci, tab_ref, tc, True)
        a_ref[:, 0:GP] = ar[pl.ds(0, tc), :].astype(BF16)
        a_ref[:, GP:2 * GP] = ai[pl.ds(0, tc), :].astype(BF16)
        du_ref[...] = (dy * d_ref[...] + _mm_nt(a_ref[...], wb_ref[...])).astype(BF16)
        anr = ar[pl.ds(1, tc), :]
        ani = ai[pl.ds(1, tc), :]
        xrv = xr_ref[...]
        xiv = xi_ref[...]
        _accumulate(glr_ref, _colsum(anr * xrv + ani * xiv), first)
        _accumulate(gli_ref, _colsum(ani * xrv - anr * xiv), first)

    rev = lambda w: pl.BlockSpec((tc, w), lambda i: (nc - 1 - i, 0))
    acc = lambda w: pl.BlockSpec((1, w), lambda i: (0, 0))
    return pl.pallas_call(
        body, name="ssm_bwd", grid=(nc,),
        in_specs=[rev(D_MODEL), rev(512), rev(512), rev(GP), rev(GP), _resident((512, 2 * GP)), _resident((512, 2 * GP)),
                  _resident((8, 8, GP)), _resident((1, 512)), _member_block("w_glu"), _resident((1, 512)),
                  _member_block("w_o_ssm")],
        out_specs=[rev(512), rev(2 * GP), rev(512), rev(512), rev(512), rev(512), acc(512), acc(512), acc(GP), acc(GP)],
        out_shape=[jax.ShapeDtypeStruct((l, 512), BF16), jax.ShapeDtypeStruct((l, 2 * GP), BF16),
                   jax.ShapeDtypeStruct((l, 512), BF16), jax.ShapeDtypeStruct((l, 512), BF16),
                   jax.ShapeDtypeStruct((l, 512), BF16), jax.ShapeDtypeStruct((l, 512), BF16),
                   jax.ShapeDtypeStruct((1, 512), F32), jax.ShapeDtypeStruct((1, 512), F32),
                   jax.ShapeDtypeStruct((1, GP), F32), jax.ShapeDtypeStruct((1, GP), F32)],
        scratch_shapes=[pltpu.VMEM((tc, GP), F32), pltpu.VMEM((tc, GP), F32), pltpu.VMEM((tc + 8, GP), F32),
                        pltpu.VMEM((tc + 8, GP), F32), pltpu.VMEM((8, GP), F32), pltpu.VMEM((8, GP), F32)],
        compiler_params=_cparams("arbitrary"),
    )(dys, y, u, xr, xi, wb, wc, tabs_rev, dskip, grp_d, b_glu, grp_e)


def _swap_halves(b):
    lane = lax.broadcasted_iota(jnp.int32, b.shape, 1)
    return jnp.where(lane < 32, pltpu.roll(b, 96, 1), pltpu.roll(b, 32, 1))


def _rope_tables(pos_ref, invf_ref, sgn_ref):
    ang = pos_ref[...].astype(F32) * invf_ref[...]
    return jnp.cos(ang), jnp.sin(ang) * sgn_ref[...]


def _mla_pre_fwd(lat, pos, invf, sgn, gqa, gkva, gq, gk, w_qb_p, w_kvb, t):
    l = lat.shape[0]

    def body(lat_ref, pos_ref, invf_ref, sgn_ref, gqa_ref, gkva_ref, gq_ref, gk_ref, wq_ref, wkv_ref, q_ref, k_ref, v_ref):
        cs, sn = _rope_tables(pos_ref, invf_ref, sgn_ref)
        ql = _rms_fwd(lat_ref[:, 0:Q_LORA], gqa_ref[...], Q_LORA)
        ckn = _rms_fwd(lat_ref[:, Q_LORA:Q_LORA + KV_LORA], gkva_ref[...], KV_LORA)
        kpe = lat_ref[:, 640:768]
        q0 = _mm(ql, wq_ref[...])
        cknb = ckn.astype(BF16)
        kv = jnp.concatenate([_mm(cknb, wkv_ref[s]) for s in range(4)], axis=-1)
        for h in range(N_HEADS):
            q1 = _rms_fwd(q0[:, HEAD_PAD * h:HEAD_PAD * (h + 1)], gq_ref[...], QK_HEAD)
            b = q1[:, 128:256]
            q_ref[h, :, 0:128] = (q1[:, 0:128] * ATT_SCALE).astype(BF16)
            q_ref[h, :, 128:256] = ((b * cs + _swap_halves(b) * sn) * ATT_SCALE).astype(BF16)
            k0 = jnp.concatenate([kv[:, 256 * h:256 * h + 128], kpe], axis=-1)
            k1 = _rms_fwd(k0, gk_ref[...], QK_HEAD)
            b = k1[:, 128:256]
            k_ref[h, :, 0:128] = k1[:, 0:128].astype(BF16)
            k_ref[h, :, 128:256] = (b * cs + _swap_halves(b) * sn).astype(BF16)
            v_ref[h] = kv[:, 256 * h + 128:256 * h + 256].astype(BF16)

    heads = lambda w: pl.BlockSpec((N_HEADS, t, w), lambda i: (0, i, 0))
    return pl.pallas_call(
        body, name="mla_pre_fwd", grid=(l // t,),
        in_specs=[_rows(t, LAT_W), _rows(t, 1), _resident((1, 128)), _resident((1, 128)), _resident((1, Q_LORA)),
                  _resident((1, KV_LORA)), _resident((1, HEAD_PAD)), _resident((1, HEAD_PAD)),
                  _resident((Q_LORA, N_HEADS * HEAD_PAD)), _member_block("w_kv_b")],
        out_specs=[heads(HEAD_PAD), heads(HEAD_PAD), heads(V_HEAD)],
        out_shape=[jax.ShapeDtypeStruct((N_HEADS, l, HEAD_PAD), BF16), jax.ShapeDtypeStruct((N_HEADS, l, HEAD_PAD), BF16),
                   jax.ShapeDtypeStruct((N_HEADS, l, V_HEAD), BF16)],
        compiler_params=_cparams("parallel"),
    )(lat, pos, invf, sgn, gqa, gkva, gq, gk, w_qb_p, w_kvb)


def _mla_pre_bwd(lat, pos, invf, sgn, gqa, gkva, gq, gk, w_qb_p, w_kvb, dq, dk, dv, t, token):
    l = lat.shape[0]

    def body(lat_ref, pos_ref, invf_ref, sgn_ref, gqa_ref, gkva_ref, gq_ref, gk_ref, wq_ref, wkv_ref, dq_ref, dk_ref, dv_ref,
             token_ref, dlat_ref, ql_ref, dq0_ref, ckn_ref, dkv_ref, ggqa_ref, ggkva_ref, ggq_ref, ggk_ref):
        first = pl.program_id(0) == 0
        cs, sn = _rope_tables(pos_ref, invf_ref, sgn_ref)
        q_lat = lat_ref[:, 0:Q_LORA]
        c_kv = lat_ref[:, Q_LORA:Q_LORA + KV_LORA]
        kpe = lat_ref[:, 640:768]
        ql = _rms_fwd(q_lat, gqa_ref[...], Q_LORA)
        ckn = _rms_fwd(c_kv, gkva_ref[...], KV_LORA)
        ql_ref[...] = ql.astype(BF16)
        ckn_ref[...] = ckn.astype(BF16)
        q0 = _mm(ql, wq_ref[...])
        cknb = ckn.astype(BF16)
        kv = jnp.concatenate([_mm(cknb, wkv_ref[s]) for s in range(4)], axis=-1)
        dkpe = jnp.zeros_like(kpe)
        ggq = jnp.zeros((1, HEAD_PAD), F32)
        ggk = jnp.zeros((1, HEAD_PAD), F32)

        def unrope(d):
            b = d[:, 128:256]
            return jnp.concatenate([d[:, 0:128], b * cs + _swap_halves(b * sn)], axis=-1)

        for h in range(N_HEADS):
            dq1 = unrope(dq_ref[h] * ATT_SCALE)
            dq0h, gq_rows = _rms_bwd(q0[:, HEAD_PAD * h:HEAD_PAD * (h + 1)], gq_ref[...], dq1, QK_HEAD)
            ggq = ggq + _colsum(gq_rows)
            dq0_ref[:, HEAD_PAD * h:HEAD_PAD * (h + 1)] = dq0h.astype(BF16)
            k0 = jnp.concatenate([kv[:, 256 * h:256 * h + 128], kpe], axis=-1)
            dk0, gk_rows = _rms_bwd(k0, gk_ref[...], unrope(dk_ref[h]), QK_HEAD)
            ggk = ggk + _colsum(gk_rows)
            dkpe = dkpe + dk0[:, 128:256]
            dkv_ref[:, 256 * h:256 * h + 128] = dk0[:, 0:128].astype(BF16)
            dkv_ref[:, 256 * h + 128:256 * h + 256] = dv_ref[h].astype(BF16)
        dql = _mm_nt(dq0_ref[...], wq_ref[...])
        dckn = sum(_mm_nt(dkv_ref[:, 512 * s:512 * (s + 1)], wkv_ref[s]) for s in range(4))
        dq_lat, gqa_rows = _rms_bwd(q_lat, gqa_ref[...], dql, Q_LORA)
        dc_kv, gkva_rows = _rms_bwd(c_kv, gkva_ref[...], dckn, KV_LORA)
        dlat_ref[:, 0:Q_LORA] = dq_lat.astype(BF16)
        dlat_ref[:, Q_LORA:Q_LORA + KV_LORA] = dc_kv.astype(BF16)
        dlat_ref[:, 640:768] = dkpe.astype(BF16)
        _accumulate(ggqa_ref, _colsum(gqa_rows), first)
        _accumulate(ggkva_ref, _colsum(gkva_rows), first)
        _accumulate(ggq_ref, ggq, first)
        _accumulate(ggk_ref, ggk, first)

    heads = lambda w: pl.BlockSpec((N_HEADS, t, w), lambda i: (0, i, 0))
    acc = lambda w: pl.BlockSpec((1, w), lambda i: (0, 0))
    return pl.pallas_call(
        body, name="mla_pre_bwd", grid=(l // t,),
        in_specs=[_rows(t, LAT_W), _rows(t, 1), _resident((1, 128)), _resident((1, 128)), _resident((1, Q_LORA)),
                  _resident((1, KV_LORA)), _resident((1, HEAD_PAD)), _resident((1, HEAD_PAD)),
                  _resident((Q_LORA, N_HEADS * HEAD_PAD)), _member_block("w_kv_b"),
                  heads(HEAD_PAD), heads(HEAD_PAD), heads(V_HEAD), ANY],
        out_specs=[_rows(t, LAT_W), _rows(t, Q_LORA), _rows(t, N_HEADS * HEAD_PAD), _rows(t, KV_LORA), _rows(t, N_HEADS * 256),
                   acc(Q_LORA), acc(KV_LORA), acc(HEAD_PAD), acc(HEAD_PAD)],
        out_shape=[jax.ShapeDtypeStruct((l, LAT_W), BF16), jax.ShapeDtypeStruct((l, Q_LORA), BF16),
                   jax.ShapeDtypeStruct((l, N_HEADS * HEAD_PAD), BF16), jax.ShapeDtypeStruct((l, KV_LORA), BF16),
                   jax.ShapeDtypeStruct((l, N_HEADS * 256), BF16), jax.ShapeDtypeStruct((1, Q_LORA), F32),
                   jax.ShapeDtypeStruct((1, KV_LORA), F32), jax.ShapeDtypeStruct((1, HEAD_PAD), F32),
                   jax.ShapeDtypeStruct((1, HEAD_PAD), F32)],
        compiler_params=_cparams("arbitrary"),
    )(lat, pos, invf, sgn, gqa, gkva, gq, gk, w_qb_p, w_kvb, dq, dk, dv, token)


def _causal(s, transposed):
    row = lax.broadcasted_iota(jnp.int32, s.shape, 0)
    col = lax.broadcasted_iota(jnp.int32, s.shape, 1)
    keep = (row <= col) if transposed else (col <= row)
    return jnp.where(keep, s, -jnp.inf)


def _as_row(col):
    n = col.shape[0]
    row = lax.broadcasted_iota(jnp.int32, (n, n), 0)
    lane = lax.broadcasted_iota(jnp.int32, (n, n), 1)
    return jnp.sum(jnp.where(row == lane, col, 0.0), axis=0, keepdims=True)


def _attn_fwd(q, k, v, tq):
    l = q.shape[1]

    def body(q_ref, k_ref, v_ref, o_ref, lse_ref):
        qi = pl.program_id(1)
        qv = q_ref[0]

        def step(kb, carry, masked):
            m, den, acc = carry
            rows = pl.ds(pl.multiple_of(kb * tq, tq), tq)
            s = _mm_nt(qv, k_ref[0, rows, :])
            if masked:
                s = _causal(s, False)
            m_new = jnp.maximum(m, jnp.max(s, axis=-1, keepdims=True))
            alpha = jnp.exp(m - m_new)
            p = jnp.exp(s - m_new)
            den = alpha * den + jnp.sum(p, axis=-1, keepdims=True)
            acc = alpha * acc + _mm(p, v_ref[0, rows, :])
            return m_new, den, acc

        init = (jnp.full((tq, 1), -jnp.inf, F32), jnp.zeros((tq, 1), F32), jnp.zeros((tq, V_HEAD), F32))
        carry = lax.fori_loop(0, qi, lambda kb, c: step(kb, c, False), init)
        m, den, acc = step(qi, carry, True)
        o_ref[...] = acc / den
        lse_ref[0, 0] = _as_row(m + jnp.log(den))

    return pl.pallas_call(
        body, name="attn_fwd", grid=(N_HEADS, l // tq),
        in_specs=[pl.BlockSpec((1, tq, HEAD_PAD), lambda h, i: (h, i, 0)), pl.BlockSpec((1, l, HEAD_PAD), lambda h, i: (h, 0, 0)),
                  pl.BlockSpec((1, l, V_HEAD), lambda h, i: (h, 0, 0))],
        out_specs=[pl.BlockSpec((tq, V_HEAD), lambda h, i: (i, h)), pl.BlockSpec((1, 1, 1, tq), lambda h, i: (h, i, 0, 0))],
        out_shape=[jax.ShapeDtypeStruct((l, N_HEADS * V_HEAD), F32), jax.ShapeDtypeStruct((N_HEADS, l // tq, 1, tq), F32)],
        compiler_params=_cparams("parallel", "arbitrary"),
    )(q, k, v)


def _attn_bwd(q, k, v, o, do, lse_t, tq, token):
    l = q.shape[1]
    nq = l // tq

    def body(q_ref, k_ref, v_ref, o_ref, do_ref, lse_ref, token_ref, dq_ref, dk_ref, dv_ref):
        ki = pl.program_id(1)

        @pl.when(ki == 0)
        def _():
            dq_ref[...] = jnp.zeros_like(dq_ref)

        kblk = k_ref[0]
        vblk = v_ref[0]
        ones = jnp.ones((8, V_HEAD), BF16)

        def step(qb, carry, masked):
            dk, dv = carry
            rows = pl.ds(pl.multiple_of(qb * tq, tq), tq)
            qblk = q_ref[0, rows, :]
            dov = do_ref[rows, :]
            dob = dov.astype(BF16)
            delta = sum(_mm_nt(ones, part) for part in _three_bf16(dov * o_ref[rows, :]))[0:1, :]
            st = _mm_nt(kblk, qblk)
            if masked:
                st = _causal(st, True)
            pt = jnp.exp(st - lse_ref[0, qb])
            dv = dv + _mm(pt, dob)
            dst = (pt * (_mm_nt(vblk, dob) - delta)).astype(BF16)
            dk = dk + _mm(dst, qblk)
            dq_ref[0, rows, :] += _mm_tn(dst, kblk)
            return dk, dv

        carry = step(ki, (jnp.zeros((tq, HEAD_PAD), F32), jnp.zeros((tq, V_HEAD), F32)), True)
        dk, dv = lax.fori_loop(ki + 1, nq, lambda qb, c: step(qb, c, False), carry)
        dk_ref[0] = dk
        dv_ref[0] = dv

    return pl.pallas_call(
        body, name="attn_bwd", grid=(N_HEADS, nq),
        in_specs=[pl.BlockSpec((1, l, HEAD_PAD), lambda h, i: (h, 0, 0)), pl.BlockSpec((1, tq, HEAD_PAD), lambda h, i: (h, i, 0)),
                  pl.BlockSpec((1, tq, V_HEAD), lambda h, i: (h, i, 0)), pl.BlockSpec((l, V_HEAD), lambda h, i: (0, h)),
                  pl.BlockSpec((l, V_HEAD), lambda h, i: (0, h)), pl.BlockSpec((1, nq, 1, tq), lambda h, i: (h, 0, 0, 0)), ANY],
        out_specs=[pl.BlockSpec((1, l, HEAD_PAD), lambda h, i: (h, 0, 0)), pl.BlockSpec((1, tq, HEAD_PAD), lambda h, i: (h, i, 0)),
                   pl.BlockSpec((1, tq, V_HEAD), lambda h, i: (h, i, 0))],
        out_shape=[jax.ShapeDtypeStruct((N_HEADS, l, HEAD_PAD), F32), jax.ShapeDtypeStruct((N_HEADS, l, HEAD_PAD), F32),
                   jax.ShapeDtypeStruct((N_HEADS, l, V_HEAD), F32)],
        compiler_params=_cparams("parallel", "arbitrary"),
    )(q, k, v, o, do, lse_t, token)


def _row_shards_mm(a, w_ref):
    a = a.astype(BF16)
    return sum(_mm(a[:, 256 * j:256 * (j + 1)], w_ref[j]) for j in range(4))


def _row_shards_mm_nt(a, w_ref):
    a = a.astype(BF16)
    return jnp.concatenate([_mm_nt(a, w_ref[j]) for j in range(4)], axis=-1)


def _merge_fwd(attn, y_ssm, gs, gm, x, grp_a, t):
    l = x.shape[0]

    def body(attn_ref, ys_ref, gs_ref, gm_ref, x_ref, wo_ref, wout_ref, ym_ref, mixed_ref, h_ref):
        y_mla = _row_shards_mm(attn_ref[...], wo_ref)
        ym_ref[...] = y_mla
        mixed = (_sigmoid(gs_ref[...]) * ys_ref[...] + _sigmoid(gm_ref[...]) * y_mla).astype(BF16)
        mixed_ref[...] = mixed
        h_ref[...] = x_ref[...] + _row_shards_mm(mixed, wout_ref)

    r = lambda: _rows(t, D_MODEL)
    return pl.pallas_call(
        body, name="merge_fwd", grid=(l // t,),
        in_specs=[r(), r(), r(), r(), r(), _member_block("w_o_mla"), _member_block("w_out")],
        out_specs=[r(), r(), r()],
        out_shape=[jax.ShapeDtypeStruct((l, D_MODEL), F32), jax.ShapeDtypeStruct((l, D_MODEL), BF16),
                   jax.ShapeDtypeStruct((l, D_MODEL), F32)],
        compiler_params=_cparams("parallel"),
    )(attn, y_ssm, gs, gm, x, grp_a, grp_a)


def _merge_bwd(dh, y_ssm, y_mla, gs, gm, grp_a, t):
    l = dh.shape[0]

    def body(dh_ref, ys_ref, ym_ref, gs_ref, gm_ref, wo_ref, wout_ref, dys_ref, dym_ref, dgs_ref, dgm_ref, dattn_ref):
        dmixed = _row_shards_mm_nt(dh_ref[...], wout_ref)
        sg = _sigmoid(gs_ref[...])
        sm = _sigmoid(gm_ref[...])
        dys_ref[...] = (dmixed * sg).astype(BF16)
        dgs_ref[...] = (dmixed * ys_ref[...] * sg * (1.0 - sg)).astype(BF16)
        dym = (dmixed * sm).astype(BF16)
        dym_ref[...] = dym
        dgm_ref[...] = (dmixed * ym_ref[...] * sm * (1.0 - sm)).astype(BF16)
        dattn_ref[...] = _row_shards_mm_nt(dym, wo_ref)

    r = lambda: _rows(t, D_MODEL)
    bf = jax.ShapeDtypeStruct((l, D_MODEL), BF16)
    return pl.pallas_call(
        body, name="merge_bwd", grid=(l // t,),
        in_specs=[r(), r(), r(), r(), r(), _member_block("w_o_mla"), _member_block("w_out")],
        out_specs=[r(), r(), r(), r(), r()],
        out_shape=[bf, bf, bf, bf, jax.ShapeDtypeStruct((l, D_MODEL), F32)],
        compiler_params=_cparams("parallel"),
    )(dh, y_ssm, y_mla, gs, gm, grp_a, grp_a)


def _mlp_fwd_bwd(h, tgt, g2, grp_a, t):
    l = h.shape[0]

    def body(h_ref, tgt_ref, g_ref, wu_ref, wd_ref, dh_ref, hn_ref, da_ref, hid_ref, dout_ref, loss_ref, dg_ref):
        first = pl.program_id(0) == 0
        hv = h_ref[...]
        g = g_ref[...]
        hn = _rms_fwd(hv, g, D_MODEL).astype(BF16)
        hn_ref[...] = hn
        out = hv
        relus = []
        for s in range(4):
            cols = slice(1024 * s, 1024 * (s + 1))
            relu = jnp.maximum(_mm(hn, wu_ref[s]), 0.0)
            relus.append(relu)
            hid = (relu * relu).astype(BF16)
            hid_ref[:, cols] = hid
            out = out + _mm(hid, wd_ref[s])
        err = out - tgt_ref[...]
        _accumulate(loss_ref, jnp.full((8, 128), jnp.sum(err * err) * (0.5 / D_MODEL), F32), first)
        dout = err * (1.0 / D_MODEL)
        doutb = dout.astype(BF16)
        dout_ref[...] = doutb
        dhn = jnp.zeros_like(hv)
        for s in range(4):
            da = (_mm_nt(doutb, wd_ref[s]) * (2.0 * relus[s])).astype(BF16)
            da_ref[:, 1024 * s:1024 * (s + 1)] = da
            dhn = dhn + _mm_nt(da, wu_ref[s])
        dx, dg_rows = _rms_bwd(hv, g, dhn, D_MODEL)
        dh_ref[...] = dout + dx
        _accumulate(dg_ref, _colsum(dg_rows), first)

    r = lambda w: _rows(t, w)
    return pl.pallas_call(
        body, name="mlp_fwd_bwd", grid=(l // t,),
        in_specs=[r(D_MODEL), r(D_MODEL), _resident((1, D_MODEL)), _member_block("w_up"), _member_block("w_down")],
        out_specs=[r(D_MODEL), r(D_MODEL), r(D_FF), r(D_FF), r(D_MODEL), pl.BlockSpec((8, 128), lambda i: (0, 0)),
                   pl.BlockSpec((1, D_MODEL), lambda i: (0, 0))],
        out_shape=[jax.ShapeDtypeStruct((l, D_MODEL), F32), jax.ShapeDtypeStruct((l, D_MODEL), BF16),
                   jax.ShapeDtypeStruct((l, D_FF), BF16), jax.ShapeDtypeStruct((l, D_FF), BF16),
                   jax.ShapeDtypeStruct((l, D_MODEL), BF16), jax.ShapeDtypeStruct((8, 128), F32),
                   jax.ShapeDtypeStruct((1, D_MODEL), F32)],
        compiler_params=_cparams("arbitrary"),
    )(h, tgt, g2, grp_a, grp_a)


def _wgrad(a, b, name):
    l, m = a.shape
    n = b.shape[1]
    bm = m if m <= 512 else 512
    bl = min(l, 2048 if n <= 1024 else 1024)

    def body(a_ref, b_ref, o_ref):
        _accumulate(o_ref, _mm_tn(a_ref[...], b_ref[...]), pl.program_id(1) == 0)

    return pl.pallas_call(
        body, name=name, grid=(m // bm, l // bl),
        in_specs=[pl.BlockSpec((bl, bm), lambda i, j: (j, i)), pl.BlockSpec((bl, n), lambda i, j: (j, 0))],
        out_specs=pl.BlockSpec((bm, n), lambda i, j: (i, 0)),
        out_shape=jax.ShapeDtypeStruct((m, n), F32),
        compiler_params=_cparams("parallel", "arbitrary"),
    )(a, b)


def _wgrad_into(a, b, member, cut, dest=None):
    group, off, rs, cs = _place_in_group(member)
    l = a.shape[0]
    bm = min(rs, 512)
    bl = min(l, 2048)
    nb = rs // bm
    if cut == "row":
        a_spec = pl.BlockSpec((bl, bm), lambda j, i, k: (k, j * nb + i))
        b_spec = pl.BlockSpec((bl, cs), lambda j, i, k: (k, 0))
    else:
        a_spec = pl.BlockSpec((bl, bm), lambda j, i, k: (k, i))
        b_spec = pl.BlockSpec((bl, cs), lambda j, i, k: (k, j))

    def body(a_ref, b_ref, *rest):
        o_ref = rest[-1]
        part = _mm_tn(a_ref[...], b_ref[...])

        @pl.when(pl.program_id(2) == 0)
        def _():
            o_ref[0] = part

        @pl.when(pl.program_id(2) != 0)
        def _():
            o_ref[0] += part

    operands, in_specs, aliases = [a, b], [a_spec, b_spec], {}
    if dest is not None:
        operands.append(dest)
        in_specs.append(ANY)
        aliases = {2: 0}
    return pl.pallas_call(
        body, name="wgrad_" + member, grid=(4, nb, l // bl), in_specs=in_specs,
        out_specs=pl.BlockSpec((1, bm, cs), lambda j, i, k: (j, off // bm + i, 0)),
        out_shape=jax.ShapeDtypeStruct((4, _group_rows(group), cs), F32), input_output_aliases=aliases,
        compiler_params=_cparams("parallel", "parallel", "arbitrary"),
    )(*operands)


def _adamw(w, g, m, v, name, g_off=0):
    r, c = w.shape
    br = r
    for cand in (256, 128, 64, 32, 16, 8):
        if r % cand == 0 and g_off % cand == 0:
            br = cand
            break

    def body(w_ref, g_ref, m_ref, v_ref, go_ref, d_ref, nm_ref, nv_ref):
        gv = g_ref[...]
        go_ref[...] = gv
        nm = ADAM_B1 * m_ref[...] + (1.0 - ADAM_B1) * gv
        nv = ADAM_B2 * v_ref[...] + (1.0 - ADAM_B2) * (gv * gv)
        m_hat = nm / (1.0 - ADAM_B1 ** ADAM_STEP)
        v_hat = nv / (1.0 - ADAM_B2 ** ADAM_STEP)
        d_ref[...] = -ADAM_LR * (m_hat / (jnp.sqrt(v_hat) + ADAM_EPS) + ADAM_WD * w_ref[...])
        nm_ref[...] = nm
        nv_ref[...] = nv

    spec = lambda: pl.BlockSpec((br, c), lambda i: (i, 0))
    g_spec = pl.BlockSpec((br, c), lambda i: (g_off // br + i, 0))
    shp = jax.ShapeDtypeStruct((r, c), F32)
    return pl.pallas_call(
        body, name=name, grid=(r // br,), in_specs=[spec(), g_spec, spec(), spec()],
        out_specs=[spec(), spec(), spec(), spec()], out_shape=[shp, shp, shp, shp], compiler_params=_cparams("parallel"),
    )(w, g, m, v)


def _place():
    return lax.axis_index("x"), lax.axis_index("y"), lax.axis_index("c")


def _other_chips(x, y):
    return [(1 - x, y), (x, 1 - y), (1 - x, 1 - y)]


ANY = pl.BlockSpec(memory_space=pl.ANY)


def _gather_weights(bufs):
    n = len(bufs)

    def body(*refs):
        outs, send_sems, recv_sems = refs[n:2 * n], refs[2 * n], refs[2 * n + 1]
        x, y, c = _place()
        chips = _other_chips(x, y)

        def part(g, px, py, pc):
            half = outs[g].shape[1] // 2
            return outs[g].at[2 * px + py, pl.ds(pl.multiple_of(pc * half, 16), half), :]

        def copy(k, src, dst, to):
            return pltpu.make_async_remote_copy(src_ref=src, dst_ref=dst, send_sem=send_sems.at[k], recv_sem=recv_sems.at[k],
                                                device_id=to, device_id_type=MESH)

        first = [copy(6 * g + j, part(g, x, y, c), part(g, x, y, c), (*chip, c)) for g in range(n) for j, chip in enumerate(chips)]
        for cp in first:
            cp.start()
        passed = []
        for g in range(n):
            for j, chip in enumerate(chips):
                landed = part(g, *chip, c)
                copy(6 * g + j, landed, landed, (x, y, c)).wait_recv()
                passed.append(copy(6 * g + 3 + j, landed, landed, (x, y, 1 - c)))
                passed[-1].start()
        for g in range(n):
            for j, chip in enumerate(chips):
                other = part(g, *chip, 1 - c)
                copy(6 * g + 3 + j, other, other, (x, y, c)).wait_recv()
        for cp in first + passed:
            cp.wait_send()

    return pl.pallas_call(
        body, name="gather_weights", in_specs=[ANY] * n, out_specs=[ANY] * n,
        out_shape=[jax.ShapeDtypeStruct(b.shape, b.dtype) for b in bufs], input_output_aliases={g: g for g in range(n)},
        scratch_shapes=[pltpu.SemaphoreType.DMA((6 * n,)), pltpu.SemaphoreType.DMA((6 * n,))],
    )(*bufs)


def _cast_shards(shards, group, place):
    width, members = GROUPS[group]
    rows = _group_rows(group)

    def body(place_ref, *refs):
        out = refs[-1]
        off = 0
        for ref, (_, r) in zip(refs[:-1], members):
            out[0, off:off + r, :] = ref[...].astype(BF16)
            off += r

    grid_spec = pltpu.PrefetchScalarGridSpec(
        num_scalar_prefetch=1, grid=(1,),
        in_specs=[pl.BlockSpec((r, width), lambda i, p: (0, 0)) for _, r in members],
        out_specs=pl.BlockSpec((1, rows, width), lambda i, p: (p[0], 0, 0)))
    return pl.pallas_call(
        body, name="cast_shards_" + group, grid_spec=grid_spec, out_shape=jax.ShapeDtypeStruct((4, rows, width), BF16),
        compiler_params=_cparams("arbitrary"),
    )(place, *[shards[name] for name, _ in members])


def _swap_gradient_halves(bufs):
    n = len(bufs)

    def body(*refs):
        ins, outs, send_sems, recv_sems = refs[:n], refs[n:2 * n], refs[2 * n], refs[2 * n + 1]
        x, y, c = _place()
        copies = []
        for g in range(n):
            half = ins[g].shape[1] // 2
            give = ins[g].at[:, pl.ds(pl.multiple_of((1 - c) * half, 8), half), :]
            copies.append(pltpu.make_async_remote_copy(src_ref=give, dst_ref=outs[g], send_sem=send_sems.at[g],
                                                       recv_sem=recv_sems.at[g], device_id=(x, y, 1 - c), device_id_type=MESH))
        for cp in copies:
            cp.start()
        for cp in copies:
            cp.wait()

    return pl.pallas_call(
        body, name="swap_gradient_halves", in_specs=[ANY] * n, out_specs=[ANY] * n,
        out_shape=[jax.ShapeDtypeStruct((4, b.shape[1] // 2, b.shape[2]), b.dtype) for b in bufs],
        scratch_shapes=[pltpu.SemaphoreType.DMA((n,)), pltpu.SemaphoreType.DMA((n,))],
    )(*bufs)


def _block_rows(h):
    return next(cand for cand in (256, 192, 128, 64, 32, 16) if h % cand == 0)


def _add_pair(buf, got, place, name):
    n, h, w = got.shape
    bh = _block_rows(h)
    nb = h // bh

    def body(place_ref, a_ref, b_ref, s_ref, sb_ref):
        s = a_ref[...] + b_ref[...]
        s_ref[...] = s
        sb_ref[...] = s.astype(BF16)

    spec = lambda: pl.BlockSpec((1, bh, w), lambda j, i, p: (j, i, 0))
    grid_spec = pltpu.PrefetchScalarGridSpec(
        num_scalar_prefetch=1, grid=(n, nb),
        in_specs=[pl.BlockSpec((1, bh, w), lambda j, i, p: (j, p[1] * nb + i, 0)), spec()], out_specs=[spec(), spec()])
    return pl.pallas_call(
        body, name=name, grid_spec=grid_spec,
        out_shape=[jax.ShapeDtypeStruct(got.shape, F32), jax.ShapeDtypeStruct(got.shape, BF16)],
        compiler_params=_cparams("parallel", "parallel"),
    )(place, buf, got)


def _scatter_to_chips(bufs):
    n = len(bufs)

    def body(*refs):
        ins, outs, send_sems, recv_sems = refs[:n], refs[n:2 * n], refs[2 * n], refs[2 * n + 1]
        x, y, c = _place()
        copies = [pltpu.make_async_remote_copy(src_ref=ins[g].at[2 * px + py], dst_ref=outs[g].at[j],
                                               send_sem=send_sems.at[3 * g + j], recv_sem=recv_sems.at[3 * g + j],
                                               device_id=(px, py, c), device_id_type=MESH)
                  for g in range(n) for j, (px, py) in enumerate(_other_chips(x, y))]
        for cp in copies:
            cp.start()
        for cp in copies:
            cp.wait()

    return pl.pallas_call(
        body, name="scatter_to_chips", in_specs=[ANY] * n, out_specs=[ANY] * n,
        out_shape=[jax.ShapeDtypeStruct((3,) + b.shape[1:], b.dtype) for b in bufs],
        scratch_shapes=[pltpu.SemaphoreType.DMA((3 * n,)), pltpu.SemaphoreType.DMA((3 * n,))],
    )(*bufs)


def _add_received(pair, got, place, name):
    _, h, w = pair.shape
    bh = _block_rows(h)
    nb = h // bh

    def body(place_ref, own_ref, got_ref, o_ref):
        o_ref[...] = ((own_ref[0] + got_ref[0].astype(F32)) + got_ref[1].astype(F32)) + got_ref[2].astype(F32)

    grid_spec = pltpu.PrefetchScalarGridSpec(
        num_scalar_prefetch=1, grid=(nb,),
        in_specs=[pl.BlockSpec((1, bh, w), lambda i, p: (p[0], i, 0)), pl.BlockSpec((3, bh, w), lambda i, p: (0, i, 0))],
        out_specs=pl.BlockSpec((bh, w), lambda i, p: (p[1] * nb + i, 0)))
    return pl.pallas_call(
        body, name=name, grid_spec=grid_spec, out_shape=jax.ShapeDtypeStruct((2 * h, w), F32),
        compiler_params=_cparams("parallel"),
    )(place, pair, got)


def _swap_reduced_halves(bufs):
    n = len(bufs)

    def body(*refs):
        outs, send_sems, recv_sems = refs[n:2 * n], refs[2 * n], refs[2 * n + 1]
        x, y, c = _place()
        copies = []
        for g in range(n):
            half = outs[g].shape[0] // 2
            own = outs[g].at[pl.ds(pl.multiple_of(c * half, 8), half), :]
            copies.append(pltpu.make_async_remote_copy(src_ref=own, dst_ref=own, send_sem=send_sems.at[g],
                                                       recv_sem=recv_sems.at[g], device_id=(x, y, 1 - c), device_id_type=MESH))
        for cp in copies:
            cp.start()
        for g in range(n):
            half = outs[g].shape[0] // 2
            other = outs[g].at[pl.ds(pl.multiple_of((1 - c) * half, 8), half), :]
            pltpu.make_async_remote_copy(src_ref=other, dst_ref=other, send_sem=send_sems.at[g], recv_sem=recv_sems.at[g],
                                         device_id=(x, y, 1 - c), device_id_type=MESH).wait_recv()
        for cp in copies:
            cp.wait_send()

    return pl.pallas_call(
        body, name="swap_reduced_halves", in_specs=[ANY] * n, out_specs=[ANY] * n,
        out_shape=[jax.ShapeDtypeStruct(b.shape, b.dtype) for b in bufs], input_output_aliases={g: g for g in range(n)},
        scratch_shapes=[pltpu.SemaphoreType.DMA((n,)), pltpu.SemaphoreType.DMA((n,))],
    )(*bufs)


HBM = pl.BlockSpec(memory_space=pltpu.HBM)
SEM = pl.BlockSpec(memory_space=pltpu.SEMAPHORE)


def _copies_start(name, bufs, n_copies, plan, after=None):
    n = len(bufs)
    extra = [] if after is None else [after]

    def body(*refs):
        sems = refs[n + len(extra):n + len(extra) + 2 * n_copies]
        x, y, c = _place()
        for i, (src, dst, dev) in enumerate(plan(refs[:n], x, y, c)):
            pltpu.make_async_remote_copy(src_ref=src, dst_ref=dst, send_sem=sems[i], recv_sem=sems[n_copies + i],
                                         device_id=dev, device_id_type=MESH).start()
        token = refs[-1]
        token[...] = jnp.zeros_like(token)

    out = pl.pallas_call(
        body, name=name,
        out_shape=[pltpu.SemaphoreType.DMA(())] * (2 * n_copies) + [pltpu.HBM(b.shape, b.dtype) for b in bufs]
        + [jax.ShapeDtypeStruct((8, 128), F32)],
        in_specs=[HBM] * n + [ANY] * len(extra),
        out_specs=[SEM] * (2 * n_copies) + [HBM] * n + [pl.BlockSpec(memory_space=pltpu.VMEM)],
        input_output_aliases={i: 2 * n_copies + i for i in range(n)},
        compiler_params=pltpu.CompilerParams(has_side_effects=pltpu.SideEffectType.DATAFLOW_SIDE_EFFECTING),
    )(*[pltpu.with_memory_space_constraint(b, pltpu.HBM) for b in bufs], *extra)
    return list(out[:2 * n_copies]), list(out[2 * n_copies:-1]), out[-1]


def _copies_wait(name, bufs, sems, after, plan):
    n = len(bufs)
    k = len(sems) // 2

    def body(*refs):
        sem_refs = refs[n:n + 2 * k]
        x, y, c = _place()
        for i, (sent, landed, dev) in enumerate(plan(refs[:n], x, y, c)):
            cp = pltpu.make_async_remote_copy(src_ref=sent, dst_ref=landed, send_sem=sem_refs[i], recv_sem=sem_refs[k + i],
                                              device_id=dev, device_id_type=MESH)
            cp.wait_send()
            cp.wait_recv()

    return pl.pallas_call(
        body, name=name, out_shape=[pltpu.HBM(b.shape, b.dtype) for b in bufs],
        in_specs=[HBM] * n + [SEM] * (2 * k) + [ANY], out_specs=[HBM] * n, input_output_aliases={i: i for i in range(n)},
        compiler_params=pltpu.CompilerParams(has_side_effects=pltpu.SideEffectType.DATAFLOW_SIDE_EFFECTING),
    )(*bufs, *sems, after)


class _GroupAExchange:
    def __init__(self, own_a, place, after):
        self.place = place
        self.gather = _copies_start("gather_a_start", [own_a], 3, self._gather_plan, after)

    @staticmethod
    def _gather_plan(refs, x, y, c):
        (wa,) = refs
        return [(wa.at[2 * x + y], wa.at[2 * x + y], (px, py, c)) for px, py in _other_chips(x, y)]

    @staticmethod
    def _gather_landed(refs, x, y, c):
        (wa,) = refs
        return [(wa.at[2 * x + y], wa.at[2 * px + py], (px, py, c)) for px, py in _other_chips(x, y)]

    def weights(self, after):
        sems, bufs, _ = self.gather
        return _copies_wait("gather_a_wait", bufs, sems, after, self._gather_landed)[0]

    def token_after_gather_start(self):
        return self.gather[2]


    def start_pair(self, ga):
        half = ga.shape[1] // 2
        land = lax.empty((4, half, ga.shape[2]), F32)

        def plan(refs, x, y, c):
            g, got = refs
            return [(g.at[:, pl.ds(pl.multiple_of((1 - c) * half, 8), half), :], got, (x, y, 1 - c))]

        self._pair_plan = plan
        self._pair = _copies_start("pair_a_start", [ga, land], 1, plan)
        return self._pair[2]

    def pair_done_start_scatter(self, after):
        sems, bufs, _ = self._pair
        ga, got = _copies_wait("pair_a_wait", bufs, sems, after, self._pair_plan)
        self._pair_f32, pair_bf16 = _add_pair(ga, got, self.place, "add_pair_a")
        land = lax.empty((3,) + pair_bf16.shape[1:], BF16)

        def plan(refs, x, y, c):
            mine, got = refs
            return [(mine.at[2 * px + py], got.at[j], (px, py, c)) for j, (px, py) in enumerate(_other_chips(x, y))]

        self._scatter_plan = plan
        self._scatter = _copies_start("scatter_a_start", [pair_bf16, land], 3, plan)
        return self._scatter[2]

    def scatter_done_start_join(self, after):
        sems, bufs, _ = self._scatter
        _, got = _copies_wait("scatter_a_wait", bufs, sems, after, self._scatter_plan)
        mine = _add_received(self._pair_f32, got, self.place, "add_received_a")
        half = mine.shape[0] // 2
        rows = lambda r, pc: r.at[pl.ds(pl.multiple_of(pc * half, 8), half), :]
        self._join_landed = lambda refs, x, y, c: [(rows(refs[0], c), rows(refs[0], 1 - c), (x, y, 1 - c))]
        self._join = _copies_start("join_a_start", [mine], 1,
                                   lambda refs, x, y, c: [(rows(refs[0], c), rows(refs[0], c), (x, y, 1 - c))])
        return self._join[2]

    def join_done(self, after):
        sems, bufs, _ = self._join
        self.reduced = _copies_wait("join_a_wait", bufs, sems, after, self._join_landed)[0]


def _all_sum_small(mine):
    rows, w = mine.shape

    def body(in_ref, out_ref, slots, send_sems, recv_sems):
        x, y, c = _place()
        me = 4 * x + 2 * y + c
        slots[me] = in_ref[...]
        copies = []
        for k in range(1, 8):
            peer = (1 - x if k & 4 else x, 1 - y if k & 2 else y, 1 - c if k & 1 else c)
            copies.append(pltpu.make_async_remote_copy(src_ref=in_ref, dst_ref=slots.at[me], send_sem=send_sems.at[k - 1],
                                                       recv_sem=recv_sems.at[k - 1], device_id=peer, device_id_type=MESH))
        for cp in copies:
            cp.start()
        for cp in copies:
            cp.wait()
        total = slots[0]
        for d in range(1, 8):
            total = total + slots[d]
        out_ref[...] = total

    return pl.pallas_call(
        body, name="all_sum_small", out_shape=jax.ShapeDtypeStruct((rows, w), F32),
        in_specs=[pl.BlockSpec(memory_space=pltpu.VMEM)], out_specs=pl.BlockSpec(memory_space=pltpu.VMEM),
        scratch_shapes=[pltpu.VMEM((8, rows, w), F32), pltpu.SemaphoreType.DMA((7,)), pltpu.SemaphoreType.DMA((7,))],
        compiler_params=pltpu.CompilerParams(vmem_limit_bytes=VMEM_LIMIT_V7X),
    )(mine)


def _join_column_shards(g):
    return jnp.transpose(g, (1, 0, 2)).reshape(g.shape[1], 4 * g.shape[2])


def _split_column_shards(w):
    r = w.shape[0]
    return jnp.transpose(w.reshape(r, 4, w.shape[1] // 4), (1, 0, 2))


def _small_rows(shape):
    return -(-int(np.prod(shape)) // 1024)


def _pack_small(vals):
    segs = []
    for name, shape in SMALL_WEIGHTS:
        flat = vals[name].reshape(-1)
        segs.append(jnp.pad(flat, (0, _small_rows(shape) * 1024 - flat.shape[0])))
    total = sum(s.shape[0] for s in segs) // 1024
    segs.append(jnp.zeros((-total % 8 * 1024,), F32))
    return jnp.concatenate(segs).reshape(-1, 1024)


def _unpack_small(packed):
    out, off = {}, 0
    for name, shape in SMALL_WEIGHTS:
        rows = _small_rows(shape)
        out[name] = packed[off:off + rows].reshape(-1)[:int(np.prod(shape))].reshape(shape)
        off += rows
    return out


def _pad_w_in(w):
    return jnp.concatenate([w[:, :1216], jnp.zeros((w.shape[0], 64), w.dtype), w[:, 1216:]], axis=1)


def _unpad_w_in(g):
    return jnp.concatenate([g[:, :1216], g[:, 1280:]], axis=1)


def _pad_heads(w):
    r = w.shape[0]
    return jnp.pad(w.reshape(r, N_HEADS, QK_HEAD), ((0, 0), (0, 0), (0, HEAD_PAD - QK_HEAD))).reshape(r, N_HEADS * HEAD_PAD)


def _unpad_heads(g):
    r = g.shape[0]
    return g.reshape(r, N_HEADS, HEAD_PAD)[:, :, :QK_HEAD].reshape(r, N_HEADS * QK_HEAD)


def _local_step(x, positions, tgt, grp, small, ex):
    l = x.shape[0]
    t = min(l, 512)
    t_mlp = min(l, 256)
    tq = min(l, 512)
    tc = min(l, 256)
    row = lambda v: v.reshape(1, -1).astype(F32)

    w_in_p = _pad_w_in(_join_column_shards(grp["b"]))
    w_qb_p = _pad_heads(_join_column_shards(grp["c"]))
    g1, g2 = row(small["norm_mix"]), row(small["norm_mlp"])
    gqa, gkva = row(small["q_a_norm"]), row(small["kv_a_norm"])
    gq = jnp.pad(row(small["q_norm"]), ((0, 0), (0, HEAD_PAD - QK_HEAD)))
    gk = jnp.pad(row(small["k_norm"]), ((0, 0), (0, HEAD_PAD - QK_HEAD)))
    half = QK_ROPE // 2
    inv_freq = ROPE_THETA ** (-jnp.arange(half, dtype=F32) / half)
    invf = jnp.concatenate([inv_freq, inv_freq, jnp.zeros((64,), F32)]).reshape(1, 128)
    sgn = jnp.concatenate([-jnp.ones((half,), F32), jnp.ones((half,), F32), jnp.zeros((64,), F32)]).reshape(1, 128)
    pos = positions.reshape(l, 1)

    a_re, a_im = small["ssm_a_re"], small["ssm_a_im"]
    log_dt = small["ssm_log_dt"].reshape(SSM_GROUPS, 1)
    to_gcp = lambda b: jnp.transpose(b, (0, 2, 1)).reshape(SSM_WIDTH, SSM_STATE)
    from_gcp = lambda b: jnp.transpose(b.reshape(SSM_GROUPS, SSM_GROUP_CH, SSM_STATE), (0, 2, 1))
    b_re, b_im = to_gcp(small["ssm_b_re"]), to_gcp(small["ssm_b_im"])
    c_re, c_im = small["ssm_c_re"].reshape(SSM_WIDTH, SSM_STATE), small["ssm_c_im"].reshape(SSM_WIDTH, SSM_STATE)
    wb, wc, tabs_fwd, tabs_rev = _ssm_param_fwd(a_re, a_im, log_dt, b_re, b_im, c_re, c_im)
    dskip = row(small["ssm_d"])
    b_glu = row(small["b_glu"])

    u, lat, gs, gm = _in_proj_fwd(x, g1, w_in_p, t, ex.token_after_gather_start())
    xr, xi, y, y_ssm = _ssm_fwd(u, wb, wc, tabs_fwd, dskip, grp["d"], b_glu, grp["e"], tc)
    q, k, v = _mla_pre_fwd(lat, pos, invf, sgn, gqa, gkva, gq, gk, w_qb_p, grp["d"], t)
    attn, lse = _attn_fwd(q, k, v, tq)
    grp_a = ex.weights(attn)
    y_mla, mixed, h = _merge_fwd(attn, y_ssm, gs, gm, x, grp_a, t)
    dh, hn, da, hid, dout, loss_blk, g_norm_mlp = _mlp_fwd_bwd(h, tgt, g2, grp_a, t_mlp)

    grads = {}
    ga = _wgrad_into(hn, da, "w_up", "col", _wgrad_into(hid, dout, "w_down", "row"))
    dys, dym, dgs, dgm, dattn = _merge_bwd(dh, y_ssm, y_mla, gs, gm, grp_a, t)
    ga = _wgrad_into(attn, dym, "w_o_mla", "row", _wgrad_into(mixed, dh, "w_out", "row", ga))

    dq, dk, dv = _attn_bwd(q, k, v, attn, dattn, lse, tq, ex.start_pair(ga))
    d_lat, ql, dq0, ckn, dkv, g_qa, g_kva, g_q, g_k = _mla_pre_bwd(lat, pos, invf, sgn, gqa, gkva, gq, gk, w_qb_p, grp["d"],
                                                                    dq, dk, dv, t, ex.pair_done_start_scatter(dk))
    grads["c"] = _split_column_shards(_unpad_heads(_wgrad(ql, dq0, "wgrad_q_b")))

    d_u, adj, dy, z, z2, dpre, g_b_glu, g_d, g_lr, g_li = _ssm_bwd(
        dys, y, u, xr, xi, wb, wc, tabs_rev, dskip, grp["d"], b_glu, grp["e"], tc)
    grads["d"] = _wgrad_into(z, dpre, "w_glu", "row", _wgrad_into(ckn, dkv, "w_kv_b", "col"))
    grads["e"] = _wgrad_into(z2, dys, "w_o_ssm", "col")
    g_ar, g_ai, g_ldt, g_br, g_bi, g_cr, g_ci = _ssm_param_bwd(
        a_re, a_im, log_dt, b_re, b_im, g_lr, g_li, _wgrad(u, adj, "wgrad_ssm_b"), _wgrad(dy, xr, "wgrad_ssm_c_re"),
        _wgrad(dy, xi, "wgrad_ssm_c_im"))

    grad_x, xn, dproj, g_norm_mix = _in_proj_bwd(x, g1, w_in_p, d_u, d_lat, dgs, dgm, dh, t)
    grads["b"] = _split_column_shards(_unpad_w_in(_wgrad(xn, dproj, "wgrad_in")))
    ex.scatter_done_start_join(grads["b"])

    g_small = {
        "norm_mix": g_norm_mix.reshape(-1), "norm_mlp": g_norm_mlp.reshape(-1), "q_a_norm": g_qa.reshape(-1),
        "kv_a_norm": g_kva.reshape(-1), "q_norm": g_q.reshape(-1)[:QK_HEAD], "k_norm": g_k.reshape(-1)[:QK_HEAD],
        "ssm_a_re": g_ar, "ssm_a_im": g_ai, "ssm_log_dt": g_ldt.reshape(-1),
        "ssm_b_re": from_gcp(g_br), "ssm_b_im": from_gcp(g_bi),
        "ssm_c_re": g_cr.reshape(SSM_GROUPS, SSM_GROUP_CH, SSM_STATE), "ssm_c_im": g_ci.reshape(SSM_GROUPS, SSM_GROUP_CH, SSM_STATE),
        "ssm_d": g_d.reshape(SSM_GROUPS, SSM_GROUP_CH), "b_glu": g_b_glu.reshape(-1),
    }
    return loss_blk[0, 0], grad_x, grads, g_small


def kernel(x, positions, norm_mix, w_in, q_a_norm, kv_a_norm, w_q_b, w_kv_b, q_norm, k_norm, w_o_mla, ssm_a_re, ssm_a_im, ssm_log_dt, ssm_b_re, ssm_b_im, ssm_c_re, ssm_c_im, ssm_d, w_glu, b_glu, w_o_ssm, w_out, norm_mlp, w_up, w_down, loss_target, m_norm_mix, m_w_in, m_q_a_norm, m_kv_a_norm, m_w_q_b, m_w_kv_b, m_q_norm, m_k_norm, m_w_o_mla, m_ssm_a_re, m_ssm_a_im, m_ssm_log_dt, m_ssm_b_re, m_ssm_b_im, m_ssm_c_re, m_ssm_c_im, m_ssm_d, m_w_glu, m_b_glu, m_w_o_ssm, m_w_out, m_norm_mlp, m_w_up, m_w_down, v_norm_mix, v_w_in, v_q_a_norm, v_kv_a_norm, v_w_q_b, v_w_kv_b, v_q_norm, v_k_norm, v_w_o_mla, v_ssm_a_re, v_ssm_a_im, v_ssm_log_dt, v_ssm_b_re, v_ssm_b_im, v_ssm_c_re, v_ssm_c_im, v_ssm_d, v_w_glu, v_b_glu, v_w_o_ssm, v_w_out, v_norm_mlp, v_w_up, v_w_down):
    args = dict(locals())
    w = {n: args[n][0] for n in WEIGHT_ORDER}
    m = {n: args["m_" + n][0] for n in WEIGHT_ORDER}
    v = {n: args["v_" + n][0] for n in WEIGHT_ORDER}
    big_names = [n for n, *_ in BIG_WEIGHTS]
    small_names = [n for n, _ in SMALL_WEIGHTS]

    place = jnp.stack([2 * lax.axis_index("x") + lax.axis_index("y"), lax.axis_index("c")]).astype(jnp.int32)
    groups = [g for g in sorted(GROUPS) if g != "a"]

    gathered = _gather_weights([_cast_shards(w, g, place) for g in groups])
    grp = dict(zip(groups, gathered))
    ex = _GroupAExchange(_cast_shards(w, "a", place), place, gathered[0])
    small = {n: w[n] for n in small_names}

    loss_local, grad_x, grads, g_small = _local_step(x[0], positions[0], loss_target[0], grp, small, ex)
    loss = lax.psum(loss_local, ("x", "y", "c"))

    bufs = [grads[g] for g in groups]
    pairs = [_add_pair(b, got, place, "add_pair_" + g) for g, b, got in zip(groups, bufs, _swap_gradient_halves(bufs))]
    landed = _scatter_to_chips([p[1] for p in pairs])
    halves = [_add_received(p[0], got, place, "add_received_" + g) for g, p, got in zip(groups, pairs, landed)]
    reduced = dict(zip(groups, _swap_reduced_halves(halves)))
    ex.join_done(reduced[groups[0]])
    reduced["a"] = ex.reduced

    small_sum = _all_sum_small(_pack_small(g_small))

    grad_w, delta_w, new_m, new_v = {}, {}, {}, {}
    for n in big_names:
        g, off, _, _ = _place_in_group(n)
        grad_w[n], delta_w[n], new_m[n], new_v[n] = _adamw(w[n], reduced[g], m[n], v[n], "adamw_" + n, off)
    g_s, d_s, m_s, v_s = _adamw(_pack_small(small), small_sum, _pack_small({n: m[n] for n in small_names}),
                                _pack_small({n: v[n] for n in small_names}), "adamw_small")
    g_s, d_s, m_s, v_s = _unpack_small(g_s), _unpack_small(d_s), _unpack_small(m_s), _unpack_small(v_s)
    for n in small_names:
        grad_w[n], delta_w[n], new_m[n], new_v[n] = g_s[n], d_s[n], m_s[n], v_s[n]

    lead = lambda d: [d[n][None] for n in WEIGHT_ORDER]
    return (loss, grad_x[None], *lead(grad_w), *lead(delta_w), *lead(new_m), *lead(new_v))
```

```python
import functools
import math

import jax
import jax.numpy as jnp
import numpy as np
from jax import lax
from jax.experimental import pallas as pl
from jax.experimental.pallas import tpu as pltpu

F32 = jnp.float32
BF16 = jnp.bfloat16

D_MODEL = 1024
SSM_GROUPS = 32
SSM_GROUP_CH = 16
SSM_WIDTH = 512
SSM_STATE = 64
GP = SSM_GROUPS * SSM_STATE
N_HEADS = 8
QK_NOPE = 128
QK_ROPE = 64
QK_HEAD = 192
HEAD_PAD = 256
V_HEAD = 128
Q_LORA = 384
KV_LORA = 256
LAT_W = 768
D_IN = 3264
D_IN_PAD = 3328
D_FF = 4096
ROPE_THETA = 10000.0
EPS = 1e-6
ATT_SCALE = QK_HEAD ** -0.5

ADAM_LR = 0.001
ADAM_B1 = 0.9
ADAM_B2 = 0.999
ADAM_EPS = 1e-08
ADAM_WD = 0.01
ADAM_STEP = 10

VMEM_LIMIT_V7X = 56 * 1024 * 1024
MESH = pl.DeviceIdType.MESH

BIG_WEIGHTS = (
    ("w_in", 1024, 3264, "col"),
    ("w_q_b", 384, 1536, "col"),
    ("w_kv_b", 256, 2048, "col"),
    ("w_o_mla", 1024, 1024, "row"),
    ("w_glu", 512, 512, "row"),
    ("w_o_ssm", 512, 1024, "col"),
    ("w_out", 1024, 1024, "row"),
    ("w_up", 1024, 4096, "col"),
    ("w_down", 4096, 1024, "row"),
)
GROUPS = {
    "a": (1024, (("w_down", 1024), ("w_up", 1024), ("w_o_mla", 256), ("w_out", 256))),
    "b": (816, (("w_in", 1024),)),
    "c": (384, (("w_q_b", 384),)),
    "d": (512, (("w_kv_b", 256), ("w_glu", 128))),
    "e": (256, (("w_o_ssm", 512),)),
}


def _group_rows(group):
    return sum(r for _, r in GROUPS[group][1])


def _place_in_group(name):
    for group, (width, members) in GROUPS.items():
        off = 0
        for member, rows in members:
            if member == name:
                return group, off, rows, width
            off += rows
    raise KeyError(name)


SMALL_WEIGHTS = (
    ("norm_mix", (1024,)), ("q_a_norm", (384,)), ("kv_a_norm", (256,)), ("q_norm", (192,)), ("k_norm", (192,)),
    ("ssm_a_re", (32, 64)), ("ssm_a_im", (32, 64)), ("ssm_log_dt", (32,)),
    ("ssm_b_re", (32, 64, 16)), ("ssm_b_im", (32, 64, 16)), ("ssm_c_re", (32, 16, 64)), ("ssm_c_im", (32, 16, 64)),
    ("ssm_d", (32, 16)), ("b_glu", (512,)), ("norm_mlp", (1024,)),
)
WEIGHT_ORDER = ('norm_mix', 'w_in', 'q_a_norm', 'kv_a_norm', 'w_q_b', 'w_kv_b', 'q_norm', 'k_norm', 'w_o_mla', 'ssm_a_re',
                'ssm_a_im', 'ssm_log_dt', 'ssm_b_re', 'ssm_b_im', 'ssm_c_re', 'ssm_c_im', 'ssm_d', 'w_glu', 'b_glu',
                'w_o_ssm', 'w_out', 'norm_mlp', 'w_up', 'w_down')


def _cparams(*sem):
    return pltpu.CompilerParams(dimension_semantics=sem if sem else None, vmem_limit_bytes=VMEM_LIMIT_V7X)


def _resident(shape, index=None):
    index = (0,) * len(shape) if index is None else index
    return pl.BlockSpec(shape, lambda *_: index, pipeline_mode=pl.Buffered(1))


def _member_block(name):
    _, off, rows, width = _place_in_group(name)
    return _resident((4, rows, width), (0, off // rows, 0))


def _rows(t, width):
    return pl.BlockSpec((t, width), lambda i: (i, 0))


def _mm(a, b):
    return jnp.dot(a.astype(BF16), b.astype(BF16), preferred_element_type=F32)


def _mm_nt(a, b):
    return lax.dot_general(a.astype(BF16), b.astype(BF16), (((1,), (1,)), ((), ())), preferred_element_type=F32)


def _mm_tn(a, b):
    return lax.dot_general(a.astype(BF16), b.astype(BF16), (((0,), (0,)), ((), ())), preferred_element_type=F32)


def _rms_fwd(x, g, n):
    r = lax.rsqrt(jnp.sum(x * x, axis=-1, keepdims=True) * (1.0 / n) + EPS)
    return x * r * g


def _rms_bwd(x, g, dy, n):
    r = lax.rsqrt(jnp.sum(x * x, axis=-1, keepdims=True) * (1.0 / n) + EPS)
    xh = x * r
    dxh = dy * g
    dx = r * (dxh - xh * (jnp.sum(dxh * xh, axis=-1, keepdims=True) * (1.0 / n)))
    return dx, dy * xh


def _colsum(a):
    return jnp.sum(a, axis=0, keepdims=True)


def _accumulate(ref, value, first):
    @pl.when(first)
    def _():
        ref[...] = value

    @pl.when(jnp.logical_not(first))
    def _():
        ref[...] += value


def _sigmoid(a):
    return 1.0 / (1.0 + jnp.exp(-a))


GELU_C = math.sqrt(2.0 / math.pi)
GELU_A = 0.044715


def _gelu(y):
    return 0.5 * y * (1.0 + jnp.tanh(GELU_C * (y + GELU_A * y * y * y)))


def _gelu_grad(y):
    t = jnp.tanh(GELU_C * (y + GELU_A * y * y * y))
    return 0.5 * (1.0 + t) + 0.5 * y * (1.0 - t * t) * GELU_C * (1.0 + 3.0 * GELU_A * y * y)


def _in_proj_fwd(x, g1, w_in_p, t, token):
    l = x.shape[0]

    def body(x_ref, g_ref, w_ref, token_ref, u_ref, lat_ref, gs_ref, gm_ref):
        xn = _rms_fwd(x_ref[...], g_ref[...], D_MODEL).astype(BF16)
        u_ref[...] = _mm(xn, w_ref[:, 0:512])
        lat_ref[...] = _mm(xn, w_ref[:, 512:1280])
        gs_ref[...] = _mm(xn, w_ref[:, 1280:2304])
        gm_ref[...] = _mm(xn, w_ref[:, 2304:3328])

    return pl.pallas_call(
        body, name="in_proj_fwd", grid=(l // t,),
        in_specs=[_rows(t, D_MODEL), _resident((1, D_MODEL)), _resident((D_MODEL, D_IN_PAD)), ANY],
        out_specs=[_rows(t, 512), _rows(t, LAT_W), _rows(t, D_MODEL), _rows(t, D_MODEL)],
        out_shape=[jax.ShapeDtypeStruct((l, 512), F32), jax.ShapeDtypeStruct((l, LAT_W), F32),
                   jax.ShapeDtypeStruct((l, D_MODEL), F32), jax.ShapeDtypeStruct((l, D_MODEL), F32)],
        compiler_params=_cparams("parallel"),
    )(x, g1, w_in_p, token)


def _in_proj_bwd(x, g1, w_in_p, d_u, d_lat, d_gs, d_gm, dh, t):
    l = x.shape[0]

    def body(x_ref, g_ref, w_ref, du_ref, dlat_ref, dgs_ref, dgm_ref, dh_ref, gx_ref, xn_ref, dproj_ref, dg_ref):
        xv = x_ref[...]
        g = g_ref[...]
        xn_ref[...] = _rms_fwd(xv, g, D_MODEL).astype(BF16)
        dproj_ref[:, 0:512] = du_ref[...]
        dproj_ref[:, 512:1280] = dlat_ref[...]
        dproj_ref[:, 1280:2304] = dgs_ref[...]
        dproj_ref[:, 2304:3328] = dgm_ref[...]
        dxn = _mm_nt(dproj_ref[...], w_ref[...])
        dx, dg_rows = _rms_bwd(xv, g, dxn, D_MODEL)
        gx_ref[...] = dh_ref[...] + dx
        _accumulate(dg_ref, _colsum(dg_rows), pl.program_id(0) == 0)

    return pl.pallas_call(
        body, name="in_proj_bwd", grid=(l // t,),
        in_specs=[_rows(t, D_MODEL), _resident((1, D_MODEL)), _resident((D_MODEL, D_IN_PAD)), _rows(t, 512),
                  _rows(t, LAT_W), _rows(t, D_MODEL), _rows(t, D_MODEL), _rows(t, D_MODEL)],
        out_specs=[_rows(t, D_MODEL), _rows(t, D_MODEL), _rows(t, D_IN_PAD), pl.BlockSpec((1, D_MODEL), lambda i: (0, 0))],
        out_shape=[jax.ShapeDtypeStruct((l, D_MODEL), F32), jax.ShapeDtypeStruct((l, D_MODEL), BF16),
                   jax.ShapeDtypeStruct((l, D_IN_PAD), BF16), jax.ShapeDtypeStruct((1, D_MODEL), F32)],
        compiler_params=_cparams("arbitrary"),
    )(x, g1, w_in_p, d_u, d_lat, d_gs, d_gm, dh)


def _ssm_param_fn(a_re, a_im, log_dt, b_re, b_im):
    dt = jnp.exp(log_dt)
    er = jnp.exp(a_re * dt)
    lr = er * jnp.cos(a_im * dt)
    li = er * jnp.sin(a_im * dt)
    den = a_re * a_re + a_im * a_im
    nr = lr - 1.0
    kr = (nr * a_re + li * a_im) / den
    ki = (li * a_re - nr * a_im) / den
    rows = lambda k: jnp.broadcast_to(k[:, None, :], (SSM_GROUPS, SSM_GROUP_CH, SSM_STATE)).reshape(SSM_WIDTH, SSM_STATE)
    krt, kit = rows(kr), rows(ki)
    return lr, li, krt * b_re - kit * b_im, krt * b_im + kit * b_re


def _state_selector():
    row = lax.broadcasted_iota(jnp.int32, (SSM_STATE, GP), 0)
    col = lax.broadcasted_iota(jnp.int32, (SSM_STATE, GP), 1)
    return jnp.where(jnp.bitwise_and(col, SSM_STATE - 1) == row, 1.0, 0.0).astype(BF16)


def _own_group(rows, rows_per_group_log2):
    row = lax.broadcasted_iota(jnp.int32, (rows, GP), 0)
    col = lax.broadcasted_iota(jnp.int32, (rows, GP), 1)
    return jnp.right_shift(row, rows_per_group_log2) == jnp.right_shift(col, 6)


def _three_bf16(x):
    hi = x.astype(BF16)
    rest = x - hi.astype(F32)
    mid = rest.astype(BF16)
    return hi, mid, (rest - mid.astype(F32)).astype(BF16)


def _spread(x, sel):
    return sum(jnp.dot(part, sel, preferred_element_type=F32) for part in _three_bf16(x))


def _collect(xw, sel):
    return sum(lax.dot_general(part, sel, (((1,), (1,)), ((), ())), preferred_element_type=F32) for part in _three_bf16(xw))


def _ssm_param_fwd(a_re, a_im, log_dt, b_re, b_im, c_re, c_im):
    def body(ar_ref, ai_ref, ldt_ref, br_ref, bi_ref, cr_ref, ci_ref, wb_ref, wct_ref, tf_ref, tr_ref):
        lr, li, bbr, bbi = _ssm_param_fn(ar_ref[...], ai_ref[...], ldt_ref[...], br_ref[...], bi_ref[...])
        sel = _state_selector()
        own16 = _own_group(SSM_WIDTH, 4)
        own1 = _own_group(SSM_GROUPS, 0)
        block = lambda m: jnp.where(own16, jnp.dot(m.astype(BF16), sel, preferred_element_type=F32), 0.0).astype(BF16)
        wb_ref[:, 0:GP] = block(bbr)
        wb_ref[:, GP:2 * GP] = block(bbi)
        wct_ref[:, 0:GP] = block(cr_ref[...])
        wct_ref[:, GP:2 * GP] = block(-ci_ref[...])
        flat = lambda m: _colsum(jnp.where(own1, _spread(m, sel), 0.0))
        pr, pi = [], []
        qr, qi = lr, li
        for _ in range(8):
            pr.append(flat(qr))
            pi.append(flat(qi))
            qr, qi = qr * lr - qi * li, qr * li + qi * lr
        row = lax.broadcasted_iota(jnp.int32, (8, GP), 0)
        for n, k in enumerate((1, 2, 4)):
            tf_ref[2 * n] = jnp.where(row >= k, pr[k - 1], 0.0)
            tf_ref[2 * n + 1] = jnp.where(row >= k, pi[k - 1], 0.0)
            tr_ref[2 * n] = jnp.where(row < 8 - k, pr[k - 1], 0.0)
            tr_ref[2 * n + 1] = jnp.where(row < 8 - k, -pi[k - 1], 0.0)
        pick = lambda vals: sum(jnp.where(row == j, v, 0.0) for j, v in enumerate(vals))
        tf_ref[6] = pick(pr)
        tf_ref[7] = pick(pi)
        tr_ref[6] = pick(pr[::-1])
        tr_ref[7] = pick([-v for v in pi[::-1]])

    return pl.pallas_call(
        body, name="ssm_param_fwd",
        out_shape=[jax.ShapeDtypeStruct((SSM_WIDTH, 2 * GP), BF16), jax.ShapeDtypeStruct((SSM_WIDTH, 2 * GP), BF16),
                   jax.ShapeDtypeStruct((8, 8, GP), F32), jax.ShapeDtypeStruct((8, 8, GP), F32)],
        compiler_params=_cparams(),
    )(a_re, a_im, log_dt, b_re, b_im, c_re, c_im)


def _ssm_param_bwd(a_re, a_im, log_dt, b_re, b_im, g_lr, g_li, g_wb, g_wct_re, g_wct_im):
    def body(ar_ref, ai_ref, ldt_ref, br_ref, bi_ref, glr_ref, gli_ref, gwb_ref, gcr_ref, gci_ref,
             o_ar, o_ai, o_ldt, o_br, o_bi, o_cr, o_ci):
        sel = _state_selector()
        own16 = _own_group(SSM_WIDTH, 4)
        own1 = _own_group(SSM_GROUPS, 0)
        blocks = lambda m: _collect(jnp.where(own16, m, 0.0), sel)
        unflat = lambda v: _collect(jnp.where(own1, v, 0.0), sel)
        _, vjp = jax.vjp(_ssm_param_fn, ar_ref[...], ai_ref[...], ldt_ref[...], br_ref[...], bi_ref[...])
        d_ar, d_ai, d_ldt, d_br, d_bi = vjp((unflat(glr_ref[...]), unflat(gli_ref[...]),
                                             blocks(gwb_ref[:, 0:GP]), blocks(gwb_ref[:, GP:2 * GP])))
        o_ar[...] = d_ar
        o_ai[...] = d_ai
        o_ldt[...] = d_ldt
        o_br[...] = d_br
        o_bi[...] = d_bi
        o_cr[...] = blocks(gcr_ref[...])
        o_ci[...] = -blocks(gci_ref[...])

    g, p = SSM_GROUPS, SSM_STATE
    gp = jax.ShapeDtypeStruct((g, p), F32)
    gcp = jax.ShapeDtypeStruct((SSM_WIDTH, p), F32)
    return pl.pallas_call(
        body, name="ssm_param_bwd", out_shape=[gp, gp, jax.ShapeDtypeStruct((g, 1), F32), gcp, gcp, gcp, gcp],
        compiler_params=_cparams(),
    )(a_re, a_im, log_dt, b_re, b_im, g_lr, g_li, g_wb, g_wct_re, g_wct_im)


SCAN_STRIP = 512


def _scan_chunk(inr_ref, ini_ref, outr_ref, outi_ref, cr_ref, ci_ref, tab_ref, tc, reverse):
    n_blocks = tc // 8

    def block(j, _):
        i = (n_blocks - 1 - j) if reverse else j
        rows = pl.ds(pl.multiple_of(i * 8, 8), 8)
        for s in range(GP // SCAN_STRIP):
            sl = pl.ds(s * SCAN_STRIP, SCAN_STRIP)
            xr = inr_ref[rows, sl]
            xi = ini_ref[rows, sl]
            for n, k in enumerate((1, 2, 4)):
                shift = (8 - k) if reverse else k
                sr = pltpu.roll(xr, shift, 0)
                si = pltpu.roll(xi, shift, 0)
                mr = tab_ref[2 * n, :, sl]
                mi = tab_ref[2 * n + 1, :, sl]
                xr, xi = xr + mr * sr - mi * si, xi + mr * si + mi * sr
            qr = tab_ref[6, :, sl]
            qi = tab_ref[7, :, sl]
            cr = cr_ref[:, sl]
            ci = ci_ref[:, sl]
            xr, xi = xr + qr * cr - qi * ci, xi + qr * ci + qi * cr
            outr_ref[rows, sl] = xr
            outi_ref[rows, sl] = xi
            edge = 0 if reverse else 7
            cr_ref[:, sl] = jnp.broadcast_to(xr[edge:edge + 1, :], (8, SCAN_STRIP))
            ci_ref[:, sl] = jnp.broadcast_to(xi[edge:edge + 1, :], (8, SCAN_STRIP))
        return 0

    lax.fori_loop(0, n_blocks, block, 0)


def _glu_pre(z, wg_ref):
    return sum(_mm(z[:, 128 * j:128 * (j + 1)], wg_ref[j]) for j in range(4))


def _ssm_fwd(u, wb, wc, tabs, dskip, grp_d, b_glu, grp_e, tc):
    l = u.shape[0]

    def body(u_ref, wb_ref, wc_ref, tab_ref, d_ref, wg_ref, bg_ref, wo_ref, xr_ref, xi_ref, y_ref, ys_ref,
             bur, bui, cr, ci):
        @pl.when(pl.program_id(0) == 0)
        def _():
            cr[...] = jnp.zeros_like(cr)
            ci[...] = jnp.zeros_like(ci)

        uv = u_ref[...]
        ub = uv.astype(BF16)
        bur[...] = _mm(ub, wb_ref[:, 0:GP])
        bui[...] = _mm(ub, wb_ref[:, GP:2 * GP])
        _scan_chunk(bur, bui, xr_ref, xi_ref, cr, ci, tab_ref, tc, False)
        y = _mm_nt(xr_ref[...], wc_ref[:, 0:GP]) + _mm_nt(xi_ref[...], wc_ref[:, GP:2 * GP]) + d_ref[...] * uv
        y_ref[...] = y
        z = _gelu(y)
        z2 = z * _sigmoid(_glu_pre(z, wg_ref) + bg_ref[...])
        for s in range(4):
            ys_ref[:, 256 * s:256 * (s + 1)] = _mm(z2, wo_ref[s])

    return pl.pallas_call(
        body, name="ssm_fwd", grid=(l // tc,),
        in_specs=[_rows(tc, 512), _resident((512, 2 * GP)), _resident((512, 2 * GP)), _resident((8, 8, GP)),
                  _resident((1, 512)), _member_block("w_glu"), _resident((1, 512)), _member_block("w_o_ssm")],
        out_specs=[_rows(tc, GP), _rows(tc, GP), _rows(tc, 512), _rows(tc, D_MODEL)],
        out_shape=[jax.ShapeDtypeStruct((l, GP), F32), jax.ShapeDtypeStruct((l, GP), F32),
                   jax.ShapeDtypeStruct((l, 512), F32), jax.ShapeDtypeStruct((l, D_MODEL), F32)],
        scratch_shapes=[pltpu.VMEM((tc, GP), F32), pltpu.VMEM((tc, GP), F32), pltpu.VMEM((8, GP), F32),
                        pltpu.VMEM((8, GP), F32)],
        compiler_params=_cparams("arbitrary"),
    )(u, wb, wc, tabs, dskip, grp_d, b_glu, grp_e)


def _ssm_bwd(dys, y, u, xr, xi, wb, wc, tabs_rev, dskip, grp_d, b_glu, grp_e, tc):
    l = u.shape[0]
    nc = l // tc

    def body(dys_ref, y_ref, u_ref, xr_ref, xi_ref, wb_ref, wc_ref, tab_ref, d_ref, wg_ref, bg_ref, wo_ref,
             du_ref, a_ref, dy_ref, z_ref, z2_ref, dpre_ref, gb_ref, gd_ref, glr_ref, gli_ref,
             dxr, dxi, ar, ai, cr, ci):
        first = pl.program_id(0) == 0

        @pl.when(first)
        def _():
            cr[...] = jnp.zeros_like(cr)
            ci[...] = jnp.zeros_like(ci)

        yv = y_ref[...]
        uv = u_ref[...]
        dz2 = sum(_mm_nt(dys_ref[:, 256 * j:256 * (j + 1)], wo_ref[j]) for j in range(4))
        z = _gelu(yv)
        s = _sigmoid(_glu_pre(z, wg_ref) + bg_ref[...])
        dpre = dz2 * z * s * (1.0 - s)
        dpreb = dpre.astype(BF16)
        dz = dz2 * s + jnp.concatenate([_mm_nt(dpreb, wg_ref[j]) for j in range(4)], axis=-1)
        dy = dz * _gelu_grad(yv)
        z_ref[...] = z.astype(BF16)
        z2_ref[...] = (z * s).astype(BF16)
        dpre_ref[...] = dpre.astype(BF16)
        dy_ref[...] = dy.astype(BF16)
        _accumulate(gb_ref, _colsum(dpre), first)
        _accumulate(gd_ref, _colsum(dy * uv), first)

        dyb = dy.astype(BF16)
        dxr[...] = _mm(dyb, wc_ref[:, 0:GP])
        dxi[...] = _mm(dyb, wc_ref[:, GP:2 * GP])
        ar[pl.ds(tc, 8), :] = cr[...]
        ai[pl.ds(tc, 8), :] = ci[...]
        _scan_chunk(dxr, dxi, ar, ai, cr, ci, tab_ref, tc, True)
        a_ref[:, 0:GP] = ar[pl.ds(0, tc), :].astype(BF16)
        a_ref[:, GP:2 * GP] = ai[pl.ds(0, tc), :].astype(BF16)
        du_ref[...] = (dy * d_ref[...] + _mm_nt(a_ref[...], wb_ref[...])).astype(BF16)
        anr = ar[pl.ds(1, tc), :]
        ani = ai[pl.ds(1, tc), :]
        xrv = xr_ref[...]
        xiv = xi_ref[...]
        _accumulate(glr_ref, _colsum(anr * xrv + ani * xiv), first)
        _accumulate(gli_ref, _colsum(ani * xrv - anr * xiv), first)

    rev = lambda w: pl.BlockSpec((tc, w), lambda i: (nc - 1 - i, 0))
    acc = lambda w: pl.BlockSpec((1, w), lambda i: (0, 0))
    return pl.pallas_call(
        body, name="ssm_bwd", grid=(nc,),
        in_specs=[rev(D_MODEL), rev(512), rev(512), rev(GP), rev(GP), _resident((512, 2 * GP)), _resident((512, 2 * GP)),
                  _resident((8, 8, GP)), _resident((1, 512)), _member_block("w_glu"), _resident((1, 512)),
                  _member_block("w_o_ssm")],
        out_specs=[rev(512), rev(2 * GP), rev(512), rev(512), rev(512), rev(512), acc(512), acc(512), acc(GP), acc(GP)],
        out_shape=[jax.ShapeDtypeStruct((l, 512), BF16), jax.ShapeDtypeStruct((l, 2 * GP), BF16),
                   jax.ShapeDtypeStruct((l, 512), BF16), jax.ShapeDtypeStruct((l, 512), BF16),
                   jax.ShapeDtypeStruct((l, 512), BF16), jax.ShapeDtypeStruct((l, 512), BF16),
                   jax.ShapeDtypeStruct((1, 512), F32), jax.ShapeDtypeStruct((1, 512), F32),
                   jax.ShapeDtypeStruct((1, GP), F32), jax.ShapeDtypeStruct((1, GP), F32)],
        scratch_shapes=[pltpu.VMEM((tc, GP), F32), pltpu.VMEM((tc, GP), F32), pltpu.VMEM((tc + 8, GP), F32),
                        pltpu.VMEM((tc + 8, GP), F32), pltpu.VMEM((8, GP), F32), pltpu.VMEM((8, GP), F32)],
        compiler_params=_cparams("arbitrary"),
    )(dys, y, u, xr, xi, wb, wc, tabs_rev, dskip, grp_d, b_glu, grp_e)


def _swap_halves(b):
    lane = lax.broadcasted_iota(jnp.int32, b.shape, 1)
    return jnp.where(lane < 32, pltpu.roll(b, 96, 1), pltpu.roll(b, 32, 1))


def _rope_tables(pos_ref, invf_ref, sgn_ref):
    ang = pos_ref[...].astype(F32) * invf_ref[...]
    return jnp.cos(ang), jnp.sin(ang) * sgn_ref[...]


def _mla_pre_fwd(lat, pos, invf, sgn, gqa, gkva, gq, gk, w_qb_p, w_kvb, t):
    l = lat.shape[0]

    def body(lat_ref, pos_ref, invf_ref, sgn_ref, gqa_ref, gkva_ref, gq_ref, gk_ref, wq_ref, wkv_ref, q_ref, k_ref, v_ref):
        cs, sn = _rope_tables(pos_ref, invf_ref, sgn_ref)
        ql = _rms_fwd(lat_ref[:, 0:Q_LORA], gqa_ref[...], Q_LORA)
        ckn = _rms_fwd(lat_ref[:, Q_LORA:Q_LORA + KV_LORA], gkva_ref[...], KV_LORA)
        kpe = lat_ref[:, 640:768]
        q0 = _mm(ql, wq_ref[...])
        cknb = ckn.astype(BF16)
        kv = jnp.concatenate([_mm(cknb, wkv_ref[s]) for s in range(4)], axis=-1)
        for h in range(N_HEADS):
            q1 = _rms_fwd(q0[:, HEAD_PAD * h:HEAD_PAD * (h + 1)], gq_ref[...], QK_HEAD)
            b = q1[:, 128:256]
            q_ref[h, :, 0:128] = (q1[:, 0:128] * ATT_SCALE).astype(BF16)
            q_ref[h, :, 128:256] = ((b * cs + _swap_halves(b) * sn) * ATT_SCALE).astype(BF16)
            k0 = jnp.concatenate([kv[:, 256 * h:256 * h + 128], kpe], axis=-1)
            k1 = _rms_fwd(k0, gk_ref[...], QK_HEAD)
            b = k1[:, 128:256]
            k_ref[h, :, 0:128] = k1[:, 0:128].astype(BF16)
            k_ref[h, :, 128:256] = (b * cs + _swap_halves(b) * sn).astype(BF16)
            v_ref[h] = kv[:, 256 * h + 128:256 * h + 256].astype(BF16)

    heads = lambda w: pl.BlockSpec((N_HEADS, t, w), lambda i: (0, i, 0))
    return pl.pallas_call(
        body, name="mla_pre_fwd", grid=(l // t,),
        in_specs=[_rows(t, LAT_W), _rows(t, 1), _resident((1, 128)), _resident((1, 128)), _resident((1, Q_LORA)),
                  _resident((1, KV_LORA)), _resident((1, HEAD_PAD)), _resident((1, HEAD_PAD)),
                  _resident((Q_LORA, N_HEADS * HEAD_PAD)), _member_block("w_kv_b")],
        out_specs=[heads(HEAD_PAD), heads(HEAD_PAD), heads(V_HEAD)],
        out_shape=[jax.ShapeDtypeStruct((N_HEADS, l, HEAD_PAD), BF16), jax.ShapeDtypeStruct((N_HEADS, l, HEAD_PAD), BF16),
                   jax.ShapeDtypeStruct((N_HEADS, l, V_HEAD), BF16)],
        compiler_params=_cparams("parallel"),
    )(lat, pos, invf, sgn, gqa, gkva, gq, gk, w_qb_p, w_kvb)


def _mla_pre_bwd(lat, pos, invf, sgn, gqa, gkva, gq, gk, w_qb_p, w_kvb, dq, dk, dv, t, token):
    l = lat.shape[0]

    def body(lat_ref, pos_ref, invf_ref, sgn_ref, gqa_ref, gkva_ref, gq_ref, gk_ref, wq_ref, wkv_ref, dq_ref, dk_ref, dv_ref,
             token_ref, dlat_ref, ql_ref, dq0_ref, ckn_ref, dkv_ref, ggqa_ref, ggkva_ref, ggq_ref, ggk_ref):
        first = pl.program_id(0) == 0
        cs, sn = _rope_tables(pos_ref, invf_ref, sgn_ref)
        q_lat = lat_ref[:, 0:Q_LORA]
        c_kv = lat_ref[:, Q_LORA:Q_LORA + KV_LORA]
        kpe = lat_ref[:, 640:768]
        ql = _rms_fwd(q_lat, gqa_ref[...], Q_LORA)
        ckn = _rms_fwd(c_kv, gkva_ref[...], KV_LORA)
        ql_ref[...] = ql.astype(BF16)
        ckn_ref[...] = ckn.astype(BF16)
        q0 = _mm(ql, wq_ref[...])
        cknb = ckn.astype(BF16)
        kv = jnp.concatenate([_mm(cknb, wkv_ref[s]) for s in range(4)], axis=-1)
        dkpe = jnp.zeros_like(kpe)
        ggq = jnp.zeros((1, HEAD_PAD), F32)
        ggk = jnp.zeros((1, HEAD_PAD), F32)

        def unrope(d):
            b = d[:, 128:256]
            return jnp.concatenate([d[:, 0:128], b * cs + _swap_halves(b * sn)], axis=-1)

        for h in range(N_HEADS):
            dq1 = unrope(dq_ref[h] * ATT_SCALE)
            dq0h, gq_rows = _rms_bwd(q0[:, HEAD_PAD * h:HEAD_PAD * (h + 1)], gq_ref[...], dq1, QK_HEAD)
            ggq = ggq + _colsum(gq_rows)
            dq0_ref[:, HEAD_PAD * h:HEAD_PAD * (h + 1)] = dq0h.astype(BF16)
            k0 = jnp.concatenate([kv[:, 256 * h:256 * h + 128], kpe], axis=-1)
            dk0, gk_rows = _rms_bwd(k0, gk_ref[...], unrope(dk_ref[h]), QK_HEAD)
            ggk = ggk + _colsum(gk_rows)
            dkpe = dkpe + dk0[:, 128:256]
            dkv_ref[:, 256 * h:256 * h + 128] = dk0[:, 0:128].astype(BF16)
            dkv_ref[:, 256 * h + 128:256 * h + 256] = dv_ref[h].astype(BF16)
        dql = _mm_nt(dq0_ref[...], wq_ref[...])
        dckn = sum(_mm_nt(dkv_ref[:, 512 * s:512 * (s + 1)], wkv_ref[s]) for s in range(4))
        dq_lat, gqa_rows = _rms_bwd(q_lat, gqa_ref[...], dql, Q_LORA)
        dc_kv, gkva_rows = _rms_bwd(c_kv, gkva_ref[...], dckn, KV_LORA)
        dlat_ref[:, 0:Q_LORA] = dq_lat.astype(BF16)
        dlat_ref[:, Q_LORA:Q_LORA + KV_LORA] = dc_kv.astype(BF16)
        dlat_ref[:, 640:768] = dkpe.astype(BF16)
        _accumulate(ggqa_ref, _colsum(gqa_rows), first)
        _accumulate(ggkva_ref, _colsum(gkva_rows), first)
        _accumulate(ggq_ref, ggq, first)
        _accumulate(ggk_ref, ggk, first)

    heads = lambda w: pl.BlockSpec((N_HEADS, t, w), lambda i: (0, i, 0))
    acc = lambda w: pl.BlockSpec((1, w), lambda i: (0, 0))
    return pl.pallas_call(
        body, name="mla_pre_bwd", grid=(l // t,),
        in_specs=[_rows(t, LAT_W), _rows(t, 1), _resident((1, 128)), _resident((1, 128)), _resident((1, Q_LORA)),
                  _resident((1, KV_LORA)), _resident((1, HEAD_PAD)), _resident((1, HEAD_PAD)),
                  _resident((Q_LORA, N_HEADS * HEAD_PAD)), _member_block("w_kv_b"),
                  heads(HEAD_PAD), heads(HEAD_PAD), heads(V_HEAD), ANY],
        out_specs=[_rows(t, LAT_W), _rows(t, Q_LORA), _rows(t, N_HEADS * HEAD_PAD), _rows(t, KV_LORA), _rows(t, N_HEADS * 256),
                   acc(Q_LORA), acc(KV_LORA), acc(HEAD_PAD), acc(HEAD_PAD)],
        out_shape=[jax.ShapeDtypeStruct((l, LAT_W), BF16), jax.ShapeDtypeStruct((l, Q_LORA), BF16),
                   jax.ShapeDtypeStruct((l, N_HEADS * HEAD_PAD), BF16), jax.ShapeDtypeStruct((l, KV_LORA), BF16),
                   jax.ShapeDtypeStruct((l, N_HEADS * 256), BF16), jax.ShapeDtypeStruct((1, Q_LORA), F32),
                   jax.ShapeDtypeStruct((1, KV_LORA), F32), jax.ShapeDtypeStruct((1, HEAD_PAD), F32),
                   jax.ShapeDtypeStruct((1, HEAD_PAD), F32)],
        compiler_params=_cparams("arbitrary"),
    )(lat, pos, invf, sgn, gqa, gkva, gq, gk, w_qb_p, w_kvb, dq, dk, dv, token)


def _causal(s, transposed):
    row = lax.broadcasted_iota(jnp.int32, s.shape, 0)
    col = lax.broadcasted_iota(jnp.int32, s.shape, 1)
    keep = (row <= col) if transposed else (col <= row)
    return jnp.where(keep, s, -jnp.inf)


def _as_row(col):
    n = col.shape[0]
    row = lax.broadcasted_iota(jnp.int32, (n, n), 0)
    lane = lax.broadcasted_iota(jnp.int32, (n, n), 1)
    return jnp.sum(jnp.where(row == lane, col, 0.0), axis=0, keepdims=True)


def _attn_fwd(q, k, v, tq):
    l = q.shape[1]

    hb = 2

    def body(q_ref, k_ref, v_ref, o_ref, lse_ref):
        qi = pl.program_id(1)
        qs = [q_ref[a] for a in range(hb)]

        def step(kb, carry, masked):
            rows = pl.ds(pl.multiple_of(kb * tq, tq), tq)
            out = []
            for a, (m, den, acc) in enumerate(carry):
                s = _mm_nt(qs[a], k_ref[a, rows, :])
                if masked:
                    s = _causal(s, False)
                m_new = jnp.maximum(m, jnp.max(s, axis=-1, keepdims=True))
                alpha = jnp.exp(m - m_new)
                p = jnp.exp(s - m_new)
                den = alpha * den + jnp.sum(p, axis=-1, keepdims=True)
                acc = alpha * acc + _mm(p, v_ref[a, rows, :])
                out.append((m_new, den, acc))
            return tuple(out)

        init = tuple((jnp.full((tq, 1), -jnp.inf, F32), jnp.zeros((tq, 1), F32), jnp.zeros((tq, V_HEAD), F32))
                     for _ in range(hb))
        carry = lax.fori_loop(0, qi, lambda kb, c: step(kb, c, False), init)
        for a, (m, den, acc) in enumerate(step(qi, carry, True)):
            o_ref[:, V_HEAD * a:V_HEAD * (a + 1)] = acc / den
            lse_ref[a, 0] = _as_row(m + jnp.log(den))

    return pl.pallas_call(
        body, name="attn_fwd", grid=(N_HEADS // hb, l // tq),
        in_specs=[pl.BlockSpec((hb, tq, HEAD_PAD), lambda h, i: (h, i, 0)), pl.BlockSpec((hb, l, HEAD_PAD), lambda h, i: (h, 0, 0)),
                  pl.BlockSpec((hb, l, V_HEAD), lambda h, i: (h, 0, 0))],
        out_specs=[pl.BlockSpec((tq, hb * V_HEAD), lambda h, i: (i, h)), pl.BlockSpec((hb, 1, 1, tq), lambda h, i: (h, i, 0, 0))],
        out_shape=[jax.ShapeDtypeStruct((l, N_HEADS * V_HEAD), F32), jax.ShapeDtypeStruct((N_HEADS, l // tq, 1, tq), F32)],
        compiler_params=_cparams("parallel", "arbitrary"),
    )(q, k, v)


def _attn_bwd(q, k, v, o, do, lse_t, tq, token):
    l = q.shape[1]
    nq = l // tq

    def body(q_ref, k_ref, v_ref, o_ref, do_ref, lse_ref, token_ref, dq_ref, dk_ref, dv_ref):
        ki = pl.program_id(1)

        @pl.when(ki == 0)
        def _():
            dq_ref[...] = jnp.zeros_like(dq_ref)

        kblk = k_ref[0]
        vblk = v_ref[0]
        ones = jnp.ones((8, V_HEAD), BF16)

        def step(qb, carry, masked):
            dk, dv = carry
            rows = pl.ds(pl.multiple_of(qb * tq, tq), tq)
            qblk = q_ref[0, rows, :]
            dov = do_ref[rows, :]
            dob = dov.astype(BF16)
            delta = sum(_mm_nt(ones, part) for part in _three_bf16(dov * o_ref[rows, :]))[0:1, :]
            st = _mm_nt(kblk, qblk)
            if masked:
                st = _causal(st, True)
            pt = jnp.exp(st - lse_ref[0, qb])
            dv = dv + _mm(pt, dob)
            dst = (pt * (_mm_nt(vblk, dob) - delta)).astype(BF16)
            dk = dk + _mm(dst, qblk)
            dq_ref[0, rows, :] += _mm_tn(dst, kblk)
            return dk, dv

        carry = step(ki, (jnp.zeros((tq, HEAD_PAD), F32), jnp.zeros((tq, V_HEAD), F32)), True)
        dk, dv = lax.fori_loop(ki + 1, nq, lambda qb, c: step(qb, c, False), carry)
        dk_ref[0] = dk
        dv_ref[0] = dv

    return pl.pallas_call(
        body, name="attn_bwd", grid=(N_HEADS, nq),
        in_specs=[pl.BlockSpec((1, l, HEAD_PAD), lambda h, i: (h, 0, 0)), pl.BlockSpec((1, tq, HEAD_PAD), lambda h, i: (h, i, 0)),
                  pl.BlockSpec((1, tq, V_HEAD), lambda h, i: (h, i, 0)), pl.BlockSpec((l, V_HEAD), lambda h, i: (0, h)),
                  pl.BlockSpec((l, V_HEAD), lambda h, i: (0, h)), pl.BlockSpec((1, nq, 1, tq), lambda h, i: (h, 0, 0, 0)), ANY],
        out_specs=[pl.BlockSpec((1, l, HEAD_PAD), lambda h, i: (h, 0, 0)), pl.BlockSpec((1, tq, HEAD_PAD), lambda h, i: (h, i, 0)),
                   pl.BlockSpec((1, tq, V_HEAD), lambda h, i: (h, i, 0))],
        out_shape=[jax.ShapeDtypeStruct((N_HEADS, l, HEAD_PAD), F32), jax.ShapeDtypeStruct((N_HEADS, l, HEAD_PAD), F32),
                   jax.ShapeDtypeStruct((N_HEADS, l, V_HEAD), F32)],
        compiler_params=_cparams("parallel", "arbitrary"),
    )(q, k, v, o, do, lse_t, token)


def _row_shards_mm(a, w_ref):
    a = a.astype(BF16)
    return sum(_mm(a[:, 256 * j:256 * (j + 1)], w_ref[j]) for j in range(4))


def _row_shards_mm_nt(a, w_ref):
    a = a.astype(BF16)
    return jnp.concatenate([_mm_nt(a, w_ref[j]) for j in range(4)], axis=-1)


def _merge_fwd(attn, y_ssm, gs, gm, x, grp_a, t):
    l = x.shape[0]

    def body(attn_ref, ys_ref, gs_ref, gm_ref, x_ref, wo_ref, wout_ref, ym_ref, mixed_ref, h_ref):
        y_mla = _row_shards_mm(attn_ref[...], wo_ref)
        ym_ref[...] = y_mla
        mixed = (_sigmoid(gs_ref[...]) * ys_ref[...] + _sigmoid(gm_ref[...]) * y_mla).astype(BF16)
        mixed_ref[...] = mixed
        h_ref[...] = x_ref[...] + _row_shards_mm(mixed, wout_ref)

    r = lambda: _rows(t, D_MODEL)
    return pl.pallas_call(
        body, name="merge_fwd", grid=(l // t,),
        in_specs=[r(), r(), r(), r(), r(), _member_block("w_o_mla"), _member_block("w_out")],
        out_specs=[r(), r(), r()],
        out_shape=[jax.ShapeDtypeStruct((l, D_MODEL), F32), jax.ShapeDtypeStruct((l, D_MODEL), BF16),
                   jax.ShapeDtypeStruct((l, D_MODEL), F32)],
        compiler_params=_cparams("parallel"),
    )(attn, y_ssm, gs, gm, x, grp_a, grp_a)


def _merge_bwd(dh, y_ssm, y_mla, gs, gm, grp_a, t):
    l = dh.shape[0]

    def body(dh_ref, ys_ref, ym_ref, gs_ref, gm_ref, wo_ref, wout_ref, dys_ref, dym_ref, dgs_ref, dgm_ref, dattn_ref):
        dmixed = _row_shards_mm_nt(dh_ref[...], wout_ref)
        sg = _sigmoid(gs_ref[...])
        sm = _sigmoid(gm_ref[...])
        dys_ref[...] = (dmixed * sg).astype(BF16)
        dgs_ref[...] = (dmixed * ys_ref[...] * sg * (1.0 - sg)).astype(BF16)
        dym = (dmixed * sm).astype(BF16)
        dym_ref[...] = dym
        dgm_ref[...] = (dmixed * ym_ref[...] * sm * (1.0 - sm)).astype(BF16)
        dattn_ref[...] = _row_shards_mm_nt(dym, wo_ref)

    r = lambda: _rows(t, D_MODEL)
    bf = jax.ShapeDtypeStruct((l, D_MODEL), BF16)
    return pl.pallas_call(
        body, name="merge_bwd", grid=(l // t,),
        in_specs=[r(), r(), r(), r(), r(), _member_block("w_o_mla"), _member_block("w_out")],
        out_specs=[r(), r(), r(), r(), r()],
        out_shape=[bf, bf, bf, bf, jax.ShapeDtypeStruct((l, D_MODEL), F32)],
        compiler_params=_cparams("parallel"),
    )(dh, y_ssm, y_mla, gs, gm, grp_a, grp_a)


def _mlp_fwd_bwd(h, tgt, g2, grp_a, t):
    l = h.shape[0]

    def body(h_ref, tgt_ref, g_ref, wu_ref, wd_ref, dh_ref, hn_ref, da_ref, hid_ref, dout_ref, loss_ref, dg_ref):
        first = pl.program_id(0) == 0
        hv = h_ref[...]
        g = g_ref[...]
        hn = _rms_fwd(hv, g, D_MODEL).astype(BF16)
        hn_ref[...] = hn
        out = hv
        relus = []
        for s in range(4):
            cols = slice(1024 * s, 1024 * (s + 1))
            relu = jnp.maximum(_mm(hn, wu_ref[s]), 0.0)
            relus.append(relu)
            hid = (relu * relu).astype(BF16)
            hid_ref[:, cols] = hid
            out = out + _mm(hid, wd_ref[s])
        err = out - tgt_ref[...]
        _accumulate(loss_ref, jnp.full((8, 128), jnp.sum(err * err) * (0.5 / D_MODEL), F32), first)
        dout = err * (1.0 / D_MODEL)
        doutb = dout.astype(BF16)
        dout_ref[...] = doutb
        dhn = jnp.zeros_like(hv)
        for s in range(4):
            da = (_mm_nt(doutb, wd_ref[s]) * (2.0 * relus[s])).astype(BF16)
            da_ref[:, 1024 * s:1024 * (s + 1)] = da
            dhn = dhn + _mm_nt(da, wu_ref[s])
        dx, dg_rows = _rms_bwd(hv, g, dhn, D_MODEL)
        dh_ref[...] = dout + dx
        _accumulate(dg_ref, _colsum(dg_rows), first)

    r = lambda w: _rows(t, w)
    return pl.pallas_call(
        body, name="mlp_fwd_bwd", grid=(l // t,),
        in_specs=[r(D_MODEL), r(D_MODEL), _resident((1, D_MODEL)), _member_block("w_up"), _member_block("w_down")],
        out_specs=[r(D_MODEL), r(D_MODEL), r(D_FF), r(D_FF), r(D_MODEL), pl.BlockSpec((8, 128), lambda i: (0, 0)),
                   pl.BlockSpec((1, D_MODEL), lambda i: (0, 0))],
        out_shape=[jax.ShapeDtypeStruct((l, D_MODEL), F32), jax.ShapeDtypeStruct((l, D_MODEL), BF16),
                   jax.ShapeDtypeStruct((l, D_FF), BF16), jax.ShapeDtypeStruct((l, D_FF), BF16),
                   jax.ShapeDtypeStruct((l, D_MODEL), BF16), jax.ShapeDtypeStruct((8, 128), F32),
                   jax.ShapeDtypeStruct((1, D_MODEL), F32)],
        compiler_params=_cparams("arbitrary"),
    )(h, tgt, g2, grp_a, grp_a)


def _wgrad(a, b, name):
    l, m = a.shape
    n = b.shape[1]
    bm = m if m <= 512 else 512
    bl = min(l, 2048 if n <= 1024 else 1024)

    def body(a_ref, b_ref, o_ref):
        _accumulate(o_ref, _mm_tn(a_ref[...], b_ref[...]), pl.program_id(1) == 0)

    return pl.pallas_call(
        body, name=name, grid=(m // bm, l // bl),
        in_specs=[pl.BlockSpec((bl, bm), lambda i, j: (j, i)), pl.BlockSpec((bl, n), lambda i, j: (j, 0))],
        out_specs=pl.BlockSpec((bm, n), lambda i, j: (i, 0)),
        out_shape=jax.ShapeDtypeStruct((m, n), F32),
        compiler_params=_cparams("parallel", "arbitrary"),
    )(a, b)


def _wgrad_into(a, b, member, cut, dest=None):
    group, off, rs, cs = _place_in_group(member)
    l = a.shape[0]
    bm = min(rs, 512)
    bl = min(l, 2048)
    nb = rs // bm
    if cut == "row":
        a_spec = pl.BlockSpec((bl, bm), lambda j, i, k: (k, j * nb + i))
        b_spec = pl.BlockSpec((bl, cs), lambda j, i, k: (k, 0))
    else:
        a_spec = pl.BlockSpec((bl, bm), lambda j, i, k: (k, i))
        b_spec = pl.BlockSpec((bl, cs), lambda j, i, k: (k, j))

    def body(a_ref, b_ref, *rest):
        o_ref = rest[-1]
        part = _mm_tn(a_ref[...], b_ref[...])

        @pl.when(pl.program_id(2) == 0)
        def _():
            o_ref[0] = part

        @pl.when(pl.program_id(2) != 0)
        def _():
            o_ref[0] += part

    operands, in_specs, aliases = [a, b], [a_spec, b_spec], {}
    if dest is not None:
        operands.append(dest)
        in_specs.append(ANY)
        aliases = {2: 0}
    return pl.pallas_call(
        body, name="wgrad_" + member, grid=(4, nb, l // bl), in_specs=in_specs,
        out_specs=pl.BlockSpec((1, bm, cs), lambda j, i, k: (j, off // bm + i, 0)),
        out_shape=jax.ShapeDtypeStruct((4, _group_rows(group), cs), F32), input_output_aliases=aliases,
        compiler_params=_cparams("parallel", "parallel", "arbitrary"),
    )(*operands)


def _adamw(w, g, m, v, name, g_off=0):
    r, c = w.shape
    br = r
    for cand in (256, 128, 64, 32, 16, 8):
        if r % cand == 0 and g_off % cand == 0:
            br = cand
            break

    def body(w_ref, g_ref, m_ref, v_ref, go_ref, d_ref, nm_ref, nv_ref):
        gv = g_ref[...]
        go_ref[...] = gv
        nm = ADAM_B1 * m_ref[...] + (1.0 - ADAM_B1) * gv
        nv = ADAM_B2 * v_ref[...] + (1.0 - ADAM_B2) * (gv * gv)
        m_hat = nm / (1.0 - ADAM_B1 ** ADAM_STEP)
        v_hat = nv / (1.0 - ADAM_B2 ** ADAM_STEP)
        d_ref[...] = -ADAM_LR * (m_hat / (jnp.sqrt(v_hat) + ADAM_EPS) + ADAM_WD * w_ref[...])
        nm_ref[...] = nm
        nv_ref[...] = nv

    spec = lambda: pl.BlockSpec((br, c), lambda i: (i, 0))
    g_spec = pl.BlockSpec((br, c), lambda i: (g_off // br + i, 0))
    shp = jax.ShapeDtypeStruct((r, c), F32)
    return pl.pallas_call(
        body, name=name, grid=(r // br,), in_specs=[spec(), g_spec, spec(), spec()],
        out_specs=[spec(), spec(), spec(), spec()], out_shape=[shp, shp, shp, shp], compiler_params=_cparams("parallel"),
    )(w, g, m, v)


def _place():
    return lax.axis_index("x"), lax.axis_index("y"), lax.axis_index("c")


def _other_chips(x, y):
    return [(1 - x, y), (x, 1 - y), (1 - x, 1 - y)]


ANY = pl.BlockSpec(memory_space=pl.ANY)


def _gather_weights(bufs):
    n = len(bufs)

    def body(*refs):
        outs, send_sems, recv_sems = refs[n:2 * n], refs[2 * n], refs[2 * n + 1]
        x, y, c = _place()
        chips = _other_chips(x, y)

        def part(g, px, py, pc):
            half = outs[g].shape[1] // 2
            return outs[g].at[2 * px + py, pl.ds(pl.multiple_of(pc * half, 16), half), :]

        def copy(k, src, dst, to):
            return pltpu.make_async_remote_copy(src_ref=src, dst_ref=dst, send_sem=send_sems.at[k], recv_sem=recv_sems.at[k],
                                                device_id=to, device_id_type=MESH)

        first = [copy(6 * g + j, part(g, x, y, c), part(g, x, y, c), (*chip, c)) for g in range(n) for j, chip in enumerate(chips)]
        for cp in first:
            cp.start()
        passed = []
        for g in range(n):
            for j, chip in enumerate(chips):
                landed = part(g, *chip, c)
                copy(6 * g + j, landed, landed, (x, y, c)).wait_recv()
                passed.append(copy(6 * g + 3 + j, landed, landed, (x, y, 1 - c)))
                passed[-1].start()
        for g in range(n):
            for j, chip in enumerate(chips):
                other = part(g, *chip, 1 - c)
                copy(6 * g + 3 + j, other, other, (x, y, c)).wait_recv()
        for cp in first + passed:
            cp.wait_send()

    return pl.pallas_call(
        body, name="gather_weights", in_specs=[ANY] * n, out_specs=[ANY] * n,
        out_shape=[jax.ShapeDtypeStruct(b.shape, b.dtype) for b in bufs], input_output_aliases={g: g for g in range(n)},
        scratch_shapes=[pltpu.SemaphoreType.DMA((6 * n,)), pltpu.SemaphoreType.DMA((6 * n,))],
    )(*bufs)


def _cast_shards(shards, group, place):
    width, members = GROUPS[group]
    rows = _group_rows(group)

    def body(place_ref, *refs):
        out = refs[-1]
        off = 0
        for ref, (_, r) in zip(refs[:-1], members):
            out[0, off:off + r, :] = ref[...].astype(BF16)
            off += r

    grid_spec = pltpu.PrefetchScalarGridSpec(
        num_scalar_prefetch=1, grid=(1,),
        in_specs=[pl.BlockSpec((r, width), lambda i, p: (0, 0)) for _, r in members],
        out_specs=pl.BlockSpec((1, rows, width), lambda i, p: (p[0], 0, 0)))
    return pl.pallas_call(
        body, name="cast_shards_" + group, grid_spec=grid_spec, out_shape=jax.ShapeDtypeStruct((4, rows, width), BF16),
        compiler_params=_cparams("arbitrary"),
    )(place, *[shards[name] for name, _ in members])


def _swap_gradient_halves(bufs):
    n = len(bufs)

    def body(*refs):
        ins, outs, send_sems, recv_sems = refs[:n], refs[n:2 * n], refs[2 * n], refs[2 * n + 1]
        x, y, c = _place()
        copies = []
        for g in range(n):
            half = ins[g].shape[1] // 2
            give = ins[g].at[:, pl.ds(pl.multiple_of((1 - c) * half, 8), half), :]
            copies.append(pltpu.make_async_remote_copy(src_ref=give, dst_ref=outs[g], send_sem=send_sems.at[g],
                                                       recv_sem=recv_sems.at[g], device_id=(x, y, 1 - c), device_id_type=MESH))
        for cp in copies:
            cp.start()
        for cp in copies:
            cp.wait()

    return pl.pallas_call(
        body, name="swap_gradient_halves", in_specs=[ANY] * n, out_specs=[ANY] * n,
        out_shape=[jax.ShapeDtypeStruct((4, b.shape[1] // 2, b.shape[2]), b.dtype) for b in bufs],
        scratch_shapes=[pltpu.SemaphoreType.DMA((n,)), pltpu.SemaphoreType.DMA((n,))],
    )(*bufs)


def _block_rows(h):
    return next(cand for cand in (256, 192, 128, 64, 32, 16) if h % cand == 0)


def _add_pair(buf, got, place, name):
    n, h, w = got.shape
    bh = _block_rows(h)
    nb = h // bh

    def body(place_ref, a_ref, b_ref, s_ref, sb_ref):
        s = a_ref[...] + b_ref[...]
        s_ref[...] = s
        sb_ref[...] = s.astype(BF16)

    spec = lambda: pl.BlockSpec((1, bh, w), lambda j, i, p: (j, i, 0))
    grid_spec = pltpu.PrefetchScalarGridSpec(
        num_scalar_prefetch=1, grid=(n, nb),
        in_specs=[pl.BlockSpec((1, bh, w), lambda j, i, p: (j, p[1] * nb + i, 0)), spec()], out_specs=[spec(), spec()])
    return pl.pallas_call(
        body, name=name, grid_spec=grid_spec,
        out_shape=[jax.ShapeDtypeStruct(got.shape, F32), jax.ShapeDtypeStruct(got.shape, BF16)],
        compiler_params=_cparams("parallel", "parallel"),
    )(place, buf, got)


def _scatter_to_chips(bufs):
    n = len(bufs)

    def body(*refs):
        ins, outs, send_sems, recv_sems = refs[:n], refs[n:2 * n], refs[2 * n], refs[2 * n + 1]
        x, y, c = _place()
        copies = [pltpu.make_async_remote_copy(src_ref=ins[g].at[2 * px + py], dst_ref=outs[g].at[j],
                                               send_sem=send_sems.at[3 * g + j], recv_sem=recv_sems.at[3 * g + j],
                                               device_id=(px, py, c), device_id_type=MESH)
                  for g in range(n) for j, (px, py) in enumerate(_other_chips(x, y))]
        for cp in copies:
            cp.start()
        for cp in copies:
            cp.wait()

    return pl.pallas_call(
        body, name="scatter_to_chips", in_specs=[ANY] * n, out_specs=[ANY] * n,
        out_shape=[jax.ShapeDtypeStruct((3,) + b.shape[1:], b.dtype) for b in bufs],
        scratch_shapes=[pltpu.SemaphoreType.DMA((3 * n,)), pltpu.SemaphoreType.DMA((3 * n,))],
    )(*bufs)


def _add_received(pair, got, place, name):
    _, h, w = pair.shape
    bh = _block_rows(h)
    nb = h // bh

    def body(place_ref, own_ref, got_ref, o_ref):
        o_ref[...] = ((own_ref[0] + got_ref[0].astype(F32)) + got_ref[1].astype(F32)) + got_ref[2].astype(F32)

    grid_spec = pltpu.PrefetchScalarGridSpec(
        num_scalar_prefetch=1, grid=(nb,),
        in_specs=[pl.BlockSpec((1, bh, w), lambda i, p: (p[0], i, 0)), pl.BlockSpec((3, bh, w), lambda i, p: (0, i, 0))],
        out_specs=pl.BlockSpec((bh, w), lambda i, p: (p[1] * nb + i, 0)))
    return pl.pallas_call(
        body, name=name, grid_spec=grid_spec, out_shape=jax.ShapeDtypeStruct((2 * h, w), F32),
        compiler_params=_cparams("parallel"),
    )(place, pair, got)


def _swap_reduced_halves(bufs):
    n = len(bufs)

    def body(*refs):
        outs, send_sems, recv_sems = refs[n:2 * n], refs[2 * n], refs[2 * n + 1]
        x, y, c = _place()
        copies = []
        for g in range(n):
            half = outs[g].shape[0] // 2
            own = outs[g].at[pl.ds(pl.multiple_of(c * half, 8), half), :]
            copies.append(pltpu.make_async_remote_copy(src_ref=own, dst_ref=own, send_sem=send_sems.at[g],
                                                       recv_sem=recv_sems.at[g], device_id=(x, y, 1 - c), device_id_type=MESH))
        for cp in copies:
            cp.start()
        for g in range(n):
            half = outs[g].shape[0] // 2
            other = outs[g].at[pl.ds(pl.multiple_of((1 - c) * half, 8), half), :]
            pltpu.make_async_remote_copy(src_ref=other, dst_ref=other, send_sem=send_sems.at[g], recv_sem=recv_sems.at[g],
                                         device_id=(x, y, 1 - c), device_id_type=MESH).wait_recv()
        for cp in copies:
            cp.wait_send()

    return pl.pallas_call(
        body, name="swap_reduced_halves", in_specs=[ANY] * n, out_specs=[ANY] * n,
        out_shape=[jax.ShapeDtypeStruct(b.shape, b.dtype) for b in bufs], input_output_aliases={g: g for g in range(n)},
        scratch_shapes=[pltpu.SemaphoreType.DMA((n,)), pltpu.SemaphoreType.DMA((n,))],
    )(*bufs)


HBM = pl.BlockSpec(memory_space=pltpu.HBM)
SEM = pl.BlockSpec(memory_space=pltpu.SEMAPHORE)


def _copies_start(name, bufs, n_copies, plan, after=None):
    n = len(bufs)
    extra = [] if after is None else [after]

    def body(*refs):
        sems = refs[n + len(extra):n + len(extra) + 2 * n_copies]
        x, y, c = _place()
        for i, (src, dst, dev) in enumerate(plan(refs[:n], x, y, c)):
            pltpu.make_async_remote_copy(src_ref=src, dst_ref=dst, send_sem=sems[i], recv_sem=sems[n_copies + i],
                                         device_id=dev, device_id_type=MESH).start()
        token = refs[-1]
        token[...] = jnp.zeros_like(token)

    out = pl.pallas_call(
        body, name=name,
        out_shape=[pltpu.SemaphoreType.DMA(())] * (2 * n_copies) + [pltpu.HBM(b.shape, b.dtype) for b in bufs]
        + [jax.ShapeDtypeStruct((8, 128), F32)],
        in_specs=[HBM] * n + [ANY] * len(extra),
        out_specs=[SEM] * (2 * n_copies) + [HBM] * n + [pl.BlockSpec(memory_space=pltpu.VMEM)],
        input_output_aliases={i: 2 * n_copies + i for i in range(n)},
        compiler_params=pltpu.CompilerParams(has_side_effects=pltpu.SideEffectType.DATAFLOW_SIDE_EFFECTING),
    )(*[pltpu.with_memory_space_constraint(b, pltpu.HBM) for b in bufs], *extra)
    return list(out[:2 * n_copies]), list(out[2 * n_copies:-1]), out[-1]


def _copies_wait(name, bufs, sems, after, plan):
    n = len(bufs)
    k = len(sems) // 2

    def body(*refs):
        sem_refs = refs[n:n + 2 * k]
        x, y, c = _place()
        for i, (sent, landed, dev) in enumerate(plan(refs[:n], x, y, c)):
            cp = pltpu.make_async_remote_copy(src_ref=sent, dst_ref=landed, send_sem=sem_refs[i], recv_sem=sem_refs[k + i],
                                              device_id=dev, device_id_type=MESH)
            cp.wait_send()
            cp.wait_recv()

    return pl.pallas_call(
        body, name=name, out_shape=[pltpu.HBM(b.shape, b.dtype) for b in bufs],
        in_specs=[HBM] * n + [SEM] * (2 * k) + [ANY], out_specs=[HBM] * n, input_output_aliases={i: i for i in range(n)},
        compiler_params=pltpu.CompilerParams(has_side_effects=pltpu.SideEffectType.DATAFLOW_SIDE_EFFECTING),
    )(*bufs, *sems, after)


class _GroupAExchange:
    def __init__(self, own_a, place, after):
        self.place = place
        self.gather = _copies_start("gather_a_start", [own_a], 3, self._gather_plan, after)

    @staticmethod
    def _gather_plan(refs, x, y, c):
        (wa,) = refs
        return [(wa.at[2 * x + y], wa.at[2 * x + y], (px, py, c)) for px, py in _other_chips(x, y)]

    @staticmethod
    def _gather_landed(refs, x, y, c):
        (wa,) = refs
        return [(wa.at[2 * x + y], wa.at[2 * px + py], (px, py, c)) for px, py in _other_chips(x, y)]

    def weights(self, after):
        sems, bufs, _ = self.gather
        return _copies_wait("gather_a_wait", bufs, sems, after, self._gather_landed)[0]

    def token_after_gather_start(self):
        return self.gather[2]


    def start_pair(self, ga):
        half = ga.shape[1] // 2
        land = lax.empty((4, half, ga.shape[2]), F32)

        def plan(refs, x, y, c):
            g, got = refs
            return [(g.at[:, pl.ds(pl.multiple_of((1 - c) * half, 8), half), :], got, (x, y, 1 - c))]

        self._pair_plan = plan
        self._pair = _copies_start("pair_a_start", [ga, land], 1, plan)
        return self._pair[2]

    def pair_done_start_scatter(self, after):
        sems, bufs, _ = self._pair
        ga, got = _copies_wait("pair_a_wait", bufs, sems, after, self._pair_plan)
        self._pair_f32, pair_bf16 = _add_pair(ga, got, self.place, "add_pair_a")
        land = lax.empty((3,) + pair_bf16.shape[1:], BF16)

        def plan(refs, x, y, c):
            mine, got = refs
            return [(mine.at[2 * px + py], got.at[j], (px, py, c)) for j, (px, py) in enumerate(_other_chips(x, y))]

        self._scatter_plan = plan
        self._scatter = _copies_start("scatter_a_start", [pair_bf16, land], 3, plan)
        return self._scatter[2]

    def scatter_done_start_join(self, after):
        sems, bufs, _ = self._scatter
        _, got = _copies_wait("scatter_a_wait", bufs, sems, after, self._scatter_plan)
        mine = _add_received(self._pair_f32, got, self.place, "add_received_a")
        half = mine.shape[0] // 2
        rows = lambda r, pc: r.at[pl.ds(pl.multiple_of(pc * half, 8), half), :]
        self._join_landed = lambda refs, x, y, c: [(rows(refs[0], c), rows(refs[0], 1 - c), (x, y, 1 - c))]
        self._join = _copies_start("join_a_start", [mine], 1,
                                   lambda refs, x, y, c: [(rows(refs[0], c), rows(refs[0], c), (x, y, 1 - c))])
        return self._join[2]

    def join_done(self, after):
        sems, bufs, _ = self._join
        self.reduced = _copies_wait("join_a_wait", bufs, sems, after, self._join_landed)[0]


def _all_sum_small(mine):
    rows, w = mine.shape

    def body(in_ref, out_ref, sibling, pair, chips, send_sems, recv_sems):
        x, y, c = _place()
        swap = pltpu.make_async_remote_copy(src_ref=in_ref, dst_ref=sibling, send_sem=send_sems.at[0], recv_sem=recv_sems.at[0],
                                            device_id=(x, y, 1 - c), device_id_type=MESH)
        swap.start()
        swap.wait()
        pair[...] = in_ref[...] + sibling[...]
        chip = 2 * x + y
        chips[chip] = pair[...]
        copies = [pltpu.make_async_remote_copy(src_ref=pair, dst_ref=chips.at[chip], send_sem=send_sems.at[1 + j],
                                               recv_sem=recv_sems.at[1 + j], device_id=(px, py, c), device_id_type=MESH)
                  for j, (px, py) in enumerate(_other_chips(x, y))]
        for cp in copies:
            cp.start()
        for cp in copies:
            cp.wait()
        out_ref[...] = ((chips[0] + chips[1]) + chips[2]) + chips[3]

    return pl.pallas_call(
        body, name="all_sum_small", out_shape=jax.ShapeDtypeStruct((rows, w), F32),
        in_specs=[pl.BlockSpec(memory_space=pltpu.VMEM)], out_specs=pl.BlockSpec(memory_space=pltpu.VMEM),
        scratch_shapes=[pltpu.VMEM((rows, w), F32), pltpu.VMEM((rows, w), F32), pltpu.VMEM((4, rows, w), F32),
                        pltpu.SemaphoreType.DMA((4,)), pltpu.SemaphoreType.DMA((4,))],
        compiler_params=pltpu.CompilerParams(vmem_limit_bytes=VMEM_LIMIT_V7X),
    )(mine)


def _join_column_shards(g):
    return jnp.transpose(g, (1, 0, 2)).reshape(g.shape[1], 4 * g.shape[2])


def _split_column_shards(w):
    r = w.shape[0]
    return jnp.transpose(w.reshape(r, 4, w.shape[1] // 4), (1, 0, 2))


def _small_rows(shape):
    return -(-int(np.prod(shape)) // 1024)


def _pack_small(vals):
    segs = []
    for name, shape in SMALL_WEIGHTS:
        flat = vals[name].reshape(-1)
        segs.append(jnp.pad(flat, (0, _small_rows(shape) * 1024 - flat.shape[0])))
    total = sum(s.shape[0] for s in segs) // 1024
    segs.append(jnp.zeros((-total % 8 * 1024,), F32))
    return jnp.concatenate(segs).reshape(-1, 1024)


def _unpack_small(packed):
    out, off = {}, 0
    for name, shape in SMALL_WEIGHTS:
        rows = _small_rows(shape)
        out[name] = packed[off:off + rows].reshape(-1)[:int(np.prod(shape))].reshape(shape)
        off += rows
    return out


W_IN_SHARD = D_IN // 4
W_IN_GAP = 1216


def _pad_w_in(g):
    cut = W_IN_GAP - W_IN_SHARD
    return jnp.concatenate([g[0], g[1][:, :cut], jnp.zeros((g.shape[1], D_IN_PAD - D_IN), g.dtype), g[1][:, cut:], g[2], g[3]],
                           axis=1)


def _unpad_w_in(g):
    skip = D_IN_PAD - D_IN
    second = jnp.concatenate([g[:, W_IN_SHARD:W_IN_GAP], g[:, W_IN_GAP + skip:2 * W_IN_SHARD + skip]], axis=1)
    return jnp.stack([g[:, :W_IN_SHARD], second, g[:, 2 * W_IN_SHARD + skip:3 * W_IN_SHARD + skip],
                      g[:, 3 * W_IN_SHARD + skip:]])


def _pad_heads(w):
    r = w.shape[0]
    return jnp.pad(w.reshape(r, N_HEADS, QK_HEAD), ((0, 0), (0, 0), (0, HEAD_PAD - QK_HEAD))).reshape(r, N_HEADS * HEAD_PAD)


def _unpad_heads(g):
    r = g.shape[0]
    return g.reshape(r, N_HEADS, HEAD_PAD)[:, :, :QK_HEAD].reshape(r, N_HEADS * QK_HEAD)


def _local_step(x, positions, tgt, grp, small, ex):
    l = x.shape[0]
    t = min(l, 512)
    t_mlp = min(l, 256)
    tq = min(l, 512)
    tc = min(l, 256)
    row = lambda v: v.reshape(1, -1).astype(F32)

    w_in_p = _pad_w_in(grp["b"])
    w_qb_p = _pad_heads(_join_column_shards(grp["c"]))
    g1, g2 = row(small["norm_mix"]), row(small["norm_mlp"])
    gqa, gkva = row(small["q_a_norm"]), row(small["kv_a_norm"])
    gq = jnp.pad(row(small["q_norm"]), ((0, 0), (0, HEAD_PAD - QK_HEAD)))
    gk = jnp.pad(row(small["k_norm"]), ((0, 0), (0, HEAD_PAD - QK_HEAD)))
    half = QK_ROPE // 2
    inv_freq = ROPE_THETA ** (-jnp.arange(half, dtype=F32) / half)
    invf = jnp.concatenate([inv_freq, inv_freq, jnp.zeros((64,), F32)]).reshape(1, 128)
    sgn = jnp.concatenate([-jnp.ones((half,), F32), jnp.ones((half,), F32), jnp.zeros((64,), F32)]).reshape(1, 128)
    pos = positions.reshape(l, 1)

    a_re, a_im = small["ssm_a_re"], small["ssm_a_im"]
    log_dt = small["ssm_log_dt"].reshape(SSM_GROUPS, 1)
    to_gcp = lambda b: jnp.transpose(b, (0, 2, 1)).reshape(SSM_WIDTH, SSM_STATE)
    from_gcp = lambda b: jnp.transpose(b.reshape(SSM_GROUPS, SSM_GROUP_CH, SSM_STATE), (0, 2, 1))
    b_re, b_im = to_gcp(small["ssm_b_re"]), to_gcp(small["ssm_b_im"])
    c_re, c_im = small["ssm_c_re"].reshape(SSM_WIDTH, SSM_STATE), small["ssm_c_im"].reshape(SSM_WIDTH, SSM_STATE)
    wb, wc, tabs_fwd, tabs_rev = _ssm_param_fwd(a_re, a_im, log_dt, b_re, b_im, c_re, c_im)
    dskip = row(small["ssm_d"])
    b_glu = row(small["b_glu"])

    u, lat, gs, gm = _in_proj_fwd(x, g1, w_in_p, t, ex.token_after_gather_start())
    xr, xi, y, y_ssm = _ssm_fwd(u, wb, wc, tabs_fwd, dskip, grp["d"], b_glu, grp["e"], tc)
    q, k, v = _mla_pre_fwd(lat, pos, invf, sgn, gqa, gkva, gq, gk, w_qb_p, grp["d"], t)
    attn, lse = _attn_fwd(q, k, v, tq)
    grp_a = ex.weights(attn)
    y_mla, mixed, h = _merge_fwd(attn, y_ssm, gs, gm, x, grp_a, t)
    dh, hn, da, hid, dout, loss_blk, g_norm_mlp = _mlp_fwd_bwd(h, tgt, g2, grp_a, t_mlp)

    grads = {}
    ga = _wgrad_into(hn, da, "w_up", "col", _wgrad_into(hid, dout, "w_down", "row"))
    dys, dym, dgs, dgm, dattn = _merge_bwd(dh, y_ssm, y_mla, gs, gm, grp_a, t)
    ga = _wgrad_into(attn, dym, "w_o_mla", "row", _wgrad_into(mixed, dh, "w_out", "row", ga))

    dq, dk, dv = _attn_bwd(q, k, v, attn, dattn, lse, tq, ex.start_pair(ga))
    d_lat, ql, dq0, ckn, dkv, g_qa, g_kva, g_q, g_k = _mla_pre_bwd(lat, pos, invf, sgn, gqa, gkva, gq, gk, w_qb_p, grp["d"],
                                                                    dq, dk, dv, t, ex.pair_done_start_scatter(dk))
    grads["c"] = _split_column_shards(_unpad_heads(_wgrad(ql, dq0, "wgrad_q_b")))

    d_u, adj, dy, z, z2, dpre, g_b_glu, g_d, g_lr, g_li = _ssm_bwd(
        dys, y, u, xr, xi, wb, wc, tabs_rev, dskip, grp["d"], b_glu, grp["e"], tc)
    grads["d"] = _wgrad_into(z, dpre, "w_glu", "row", _wgrad_into(ckn, dkv, "w_kv_b", "col"))
    grads["e"] = _wgrad_into(z2, dys, "w_o_ssm", "col")
    g_ar, g_ai, g_ldt, g_br, g_bi, g_cr, g_ci = _ssm_param_bwd(
        a_re, a_im, log_dt, b_re, b_im, g_lr, g_li, _wgrad(u, adj, "wgrad_ssm_b"), _wgrad(dy, xr, "wgrad_ssm_c_re"),
        _wgrad(dy, xi, "wgrad_ssm_c_im"))

    grad_x, xn, dproj, g_norm_mix = _in_proj_bwd(x, g1, w_in_p, d_u, d_lat, dgs, dgm, dh, t)
    grads["b"] = _unpad_w_in(_wgrad(xn, dproj, "wgrad_in"))
    ex.scatter_done_start_join(grads["b"])

    g_small = {
        "norm_mix": g_norm_mix.reshape(-1), "norm_mlp": g_norm_mlp.reshape(-1), "q_a_norm": g_qa.reshape(-1),
        "kv_a_norm": g_kva.reshape(-1), "q_norm": g_q.reshape(-1)[:QK_HEAD], "k_norm": g_k.reshape(-1)[:QK_HEAD],
        "ssm_a_re": g_ar, "ssm_a_im": g_ai, "ssm_log_dt": g_ldt.reshape(-1),
        "ssm_b_re": from_gcp(g_br), "ssm_b_im": from_gcp(g_bi),
        "ssm_c_re": g_cr.reshape(SSM_GROUPS, SSM_GROUP_CH, SSM_STATE), "ssm_c_im": g_ci.reshape(SSM_GROUPS, SSM_GROUP_CH, SSM_STATE),
        "ssm_d": g_d.reshape(SSM_GROUPS, SSM_GROUP_CH), "b_glu": g_b_glu.reshape(-1),
    }
    return loss_blk[0, 0], grad_x, grads, g_small


def kernel(x, positions, norm_mix, w_in, q_a_norm, kv_a_norm, w_q_b, w_kv_b, q_norm, k_norm, w_o_mla, ssm_a_re, ssm_a_im, ssm_log_dt, ssm_b_re, ssm_b_im, ssm_c_re, ssm_c_im, ssm_d, w_glu, b_glu, w_o_ssm, w_out, norm_mlp, w_up, w_down, loss_target, m_norm_mix, m_w_in, m_q_a_norm, m_kv_a_norm, m_w_q_b, m_w_kv_b, m_q_norm, m_k_norm, m_w_o_mla, m_ssm_a_re, m_ssm_a_im, m_ssm_log_dt, m_ssm_b_re, m_ssm_b_im, m_ssm_c_re, m_ssm_c_im, m_ssm_d, m_w_glu, m_b_glu, m_w_o_ssm, m_w_out, m_norm_mlp, m_w_up, m_w_down, v_norm_mix, v_w_in, v_q_a_norm, v_kv_a_norm, v_w_q_b, v_w_kv_b, v_q_norm, v_k_norm, v_w_o_mla, v_ssm_a_re, v_ssm_a_im, v_ssm_log_dt, v_ssm_b_re, v_ssm_b_im, v_ssm_c_re, v_ssm_c_im, v_ssm_d, v_w_glu, v_b_glu, v_w_o_ssm, v_w_out, v_norm_mlp, v_w_up, v_w_down):
    args = dict(locals())
    w = {n: args[n][0] for n in WEIGHT_ORDER}
    m = {n: args["m_" + n][0] for n in WEIGHT_ORDER}
    v = {n: args["v_" + n][0] for n in WEIGHT_ORDER}
    big_names = [n for n, *_ in BIG_WEIGHTS]
    small_names = [n for n, _ in SMALL_WEIGHTS]

    place = jnp.stack([2 * lax.axis_index("x") + lax.axis_index("y"), lax.axis_index("c")]).astype(jnp.int32)
    groups = [g for g in sorted(GROUPS) if g != "a"]

    gathered = _gather_weights([_cast_shards(w, g, place) for g in groups])
    grp = dict(zip(groups, gathered))
    ex = _GroupAExchange(_cast_shards(w, "a", place), place, gathered[0])
    small = {n: w[n] for n in small_names}

    loss_local, grad_x, grads, g_small = _local_step(x[0], positions[0], loss_target[0], grp, small, ex)
    loss = lax.psum(loss_local, ("x", "y", "c"))

    bufs = [grads[g] for g in groups]
    pairs = [_add_pair(b, got, place, "add_pair_" + g) for g, b, got in zip(groups, bufs, _swap_gradient_halves(bufs))]
    landed = _scatter_to_chips([p[1] for p in pairs])
    halves = [_add_received(p[0], got, place, "add_received_" + g) for g, p, got in zip(groups, pairs, landed)]
    reduced = dict(zip(groups, _swap_reduced_halves(halves)))
    ex.join_done(reduced[groups[0]])
    reduced["a"] = ex.reduced

    small_sum = _all_sum_small(_pack_small(g_small))

    grad_w, delta_w, new_m, new_v = {}, {}, {}, {}
    for n in big_names:
        g, off, _, _ = _place_in_group(n)
        grad_w[n], delta_w[n], new_m[n], new_v[n] = _adamw(w[n], reduced[g], m[n], v[n], "adamw_" + n, off)
    g_s, d_s, m_s, v_s = _adamw(_pack_small(small), small_sum, _pack_small({n: m[n] for n in small_names}),
                                _pack_small({n: v[n] for n in small_names}), "adamw_small")
    g_s, d_s, m_s, v_s = _unpack_small(g_s), _unpack_small(d_s), _unpack_small(m_s), _unpack_small(v_s)
    for n in small_names:
        grad_w[n], delta_w[n], new_m[n], new_v[n] = g_s[n], d_s[n], m_s[n], v_s[n]

    lead = lambda d: [d[n][None] for n in WEIGHT_ORDER]
    return (loss, grad_x[None], *lead(grad_w), *lead(delta_w), *lead(new_m), *lead(new_v))
```

```python
import math

import jax
import jax.numpy as jnp
import numpy as np
from jax import lax
from jax.experimental import pallas as pl
from jax.experimental.pallas import tpu as pltpu

F32 = jnp.float32
BF16 = jnp.bfloat16

D_MODEL = 1024
SSM_GROUPS = 32
SSM_GROUP_CH = 16
SSM_WIDTH = 512
SSM_STATE = 64
GP = SSM_GROUPS * SSM_STATE
N_HEADS = 8
QK_NOPE = 128
QK_ROPE = 64
QK_HEAD = 192
HEAD_PAD = 256
V_HEAD = 128
Q_LORA = 384
KV_LORA = 256
LAT_W = 768
D_IN = 3264
D_IN_PAD = 3328
D_FF = 4096
ROPE_THETA = 10000.0
EPS = 1e-6
ATT_SCALE = QK_HEAD ** -0.5

ADAM_LR = 0.001
ADAM_B1 = 0.9
ADAM_B2 = 0.999
ADAM_EPS = 1e-08
ADAM_WD = 0.01
ADAM_STEP = 10

VMEM_LIMIT_V7X = 56 * 1024 * 1024
MESH = pl.DeviceIdType.MESH

BIG_WEIGHTS = (
    ("w_in", 1024, 3264, "col"),
    ("w_q_b", 384, 1536, "col"),
    ("w_kv_b", 256, 2048, "col"),
    ("w_o_mla", 1024, 1024, "row"),
    ("w_glu", 512, 512, "row"),
    ("w_o_ssm", 512, 1024, "col"),
    ("w_out", 1024, 1024, "row"),
    ("w_up", 1024, 4096, "col"),
    ("w_down", 4096, 1024, "row"),
)
GROUPS = {
    "a": (1024, (("w_down", 1024), ("w_up", 1024), ("w_o_mla", 256), ("w_out", 256))),
    "b": (816, (("w_in", 1024),)),
    "c": (384, (("w_q_b", 384),)),
    "d": (512, (("w_kv_b", 256), ("w_glu", 128))),
    "e": (256, (("w_o_ssm", 512),)),
}


def _group_rows(group):
    return sum(r for _, r in GROUPS[group][1])


def _place_in_group(name):
    for group, (width, members) in GROUPS.items():
        off = 0
        for member, rows in members:
            if member == name:
                return group, off, rows, width
            off += rows
    raise KeyError(name)


SMALL_WEIGHTS = (
    ("norm_mix", (1024,)), ("q_a_norm", (384,)), ("kv_a_norm", (256,)), ("q_norm", (192,)), ("k_norm", (192,)),
    ("ssm_a_re", (32, 64)), ("ssm_a_im", (32, 64)), ("ssm_log_dt", (32,)),
    ("ssm_b_re", (32, 64, 16)), ("ssm_b_im", (32, 64, 16)), ("ssm_c_re", (32, 16, 64)), ("ssm_c_im", (32, 16, 64)),
    ("ssm_d", (32, 16)), ("b_glu", (512,)), ("norm_mlp", (1024,)),
)
WEIGHT_ORDER = ('norm_mix', 'w_in', 'q_a_norm', 'kv_a_norm', 'w_q_b', 'w_kv_b', 'q_norm', 'k_norm', 'w_o_mla', 'ssm_a_re',
                'ssm_a_im', 'ssm_log_dt', 'ssm_b_re', 'ssm_b_im', 'ssm_c_re', 'ssm_c_im', 'ssm_d', 'w_glu', 'b_glu',
                'w_o_ssm', 'w_out', 'norm_mlp', 'w_up', 'w_down')


def _cparams(*sem):
    return pltpu.CompilerParams(dimension_semantics=sem if sem else None, vmem_limit_bytes=VMEM_LIMIT_V7X)


def _resident(shape, index=None):
    index = (0,) * len(shape) if index is None else index
    return pl.BlockSpec(shape, lambda *_: index, pipeline_mode=pl.Buffered(1))


def _member_block(name):
    _, off, rows, width = _place_in_group(name)
    return _resident((4, rows, width), (0, off // rows, 0))


def _rows(t, width):
    return pl.BlockSpec((t, width), lambda i: (i, 0))


def _mm(a, b):
    return jnp.dot(a.astype(BF16), b.astype(BF16), preferred_element_type=F32)


def _mm_nt(a, b):
    return lax.dot_general(a.astype(BF16), b.astype(BF16), (((1,), (1,)), ((), ())), preferred_element_type=F32)


def _mm_tn(a, b):
    return lax.dot_general(a.astype(BF16), b.astype(BF16), (((0,), (0,)), ((), ())), preferred_element_type=F32)


def _rms_fwd(x, g, n):
    r = lax.rsqrt(jnp.sum(x * x, axis=-1, keepdims=True) * (1.0 / n) + EPS)
    return x * r * g


def _rms_bwd(x, g, dy, n):
    r = lax.rsqrt(jnp.sum(x * x, axis=-1, keepdims=True) * (1.0 / n) + EPS)
    xh = x * r
    dxh = dy * g
    dx = r * (dxh - xh * (jnp.sum(dxh * xh, axis=-1, keepdims=True) * (1.0 / n)))
    return dx, dy * xh


def _colsum(a):
    return jnp.sum(a, axis=0, keepdims=True)


def _accumulate(ref, value, first):
    @pl.when(first)
    def _():
        ref[...] = value

    @pl.when(jnp.logical_not(first))
    def _():
        ref[...] += value


def _sigmoid(a):
    return 1.0 / (1.0 + jnp.exp(-a))


GELU_C = math.sqrt(2.0 / math.pi)
GELU_A = 0.044715


def _gelu(y):
    return 0.5 * y * (1.0 + jnp.tanh(GELU_C * (y + GELU_A * y * y * y)))


def _gelu_grad(y):
    t = jnp.tanh(GELU_C * (y + GELU_A * y * y * y))
    return 0.5 * (1.0 + t) + 0.5 * y * (1.0 - t * t) * GELU_C * (1.0 + 3.0 * GELU_A * y * y)


def _in_proj_fwd(x, g1, w_in_p, t, token):
    l = x.shape[0]

    def body(x_ref, g_ref, w_ref, token_ref, u_ref, lat_ref, gs_ref, gm_ref):
        xn = _rms_fwd(x_ref[...], g_ref[...], D_MODEL).astype(BF16)
        u_ref[...] = _mm(xn, w_ref[:, 0:512])
        lat_ref[...] = _mm(xn, w_ref[:, 512:1280])
        gs_ref[...] = _mm(xn, w_ref[:, 1280:2304])
        gm_ref[...] = _mm(xn, w_ref[:, 2304:3328])

    return pl.pallas_call(
        body, name="in_proj_fwd", grid=(l // t,),
        in_specs=[_rows(t, D_MODEL), _resident((1, D_MODEL)), _resident((D_MODEL, D_IN_PAD)), ANY],
        out_specs=[_rows(t, 512), _rows(t, LAT_W), _rows(t, D_MODEL), _rows(t, D_MODEL)],
        out_shape=[jax.ShapeDtypeStruct((l, 512), F32), jax.ShapeDtypeStruct((l, LAT_W), F32),
                   jax.ShapeDtypeStruct((l, D_MODEL), F32), jax.ShapeDtypeStruct((l, D_MODEL), F32)],
        compiler_params=_cparams("parallel"),
    )(x, g1, w_in_p, token)


def _in_proj_bwd(x, g1, w_in_p, d_u, d_lat, d_gs, d_gm, dh, t):
    l = x.shape[0]

    def body(x_ref, g_ref, w_ref, du_ref, dlat_ref, dgs_ref, dgm_ref, dh_ref, gx_ref, xn_ref, dproj_ref, dg_ref):
        xv = x_ref[...]
        g = g_ref[...]
        xn_ref[...] = _rms_fwd(xv, g, D_MODEL).astype(BF16)
        dproj_ref[:, 0:512] = du_ref[...]
        dproj_ref[:, 512:1280] = dlat_ref[...]
        dproj_ref[:, 1280:2304] = dgs_ref[...]
        dproj_ref[:, 2304:3328] = dgm_ref[...]
        dxn = _mm_nt(dproj_ref[...], w_ref[...])
        dx, dg_rows = _rms_bwd(xv, g, dxn, D_MODEL)
        gx_ref[...] = dh_ref[...] + dx
        _accumulate(dg_ref, _colsum(dg_rows), pl.program_id(0) == 0)

    return pl.pallas_call(
        body, name="in_proj_bwd", grid=(l // t,),
        in_specs=[_rows(t, D_MODEL), _resident((1, D_MODEL)), _resident((D_MODEL, D_IN_PAD)), _rows(t, 512),
                  _rows(t, LAT_W), _rows(t, D_MODEL), _rows(t, D_MODEL), _rows(t, D_MODEL)],
        out_specs=[_rows(t, D_MODEL), _rows(t, D_MODEL), _rows(t, D_IN_PAD), pl.BlockSpec((1, D_MODEL), lambda i: (0, 0))],
        out_shape=[jax.ShapeDtypeStruct((l, D_MODEL), F32), jax.ShapeDtypeStruct((l, D_MODEL), BF16),
                   jax.ShapeDtypeStruct((l, D_IN_PAD), BF16), jax.ShapeDtypeStruct((1, D_MODEL), F32)],
        compiler_params=_cparams("arbitrary"),
    )(x, g1, w_in_p, d_u, d_lat, d_gs, d_gm, dh)


def _ssm_param_fn(a_re, a_im, log_dt, b_re, b_im):
    dt = jnp.exp(log_dt)
    er = jnp.exp(a_re * dt)
    lr = er * jnp.cos(a_im * dt)
    li = er * jnp.sin(a_im * dt)
    den = a_re * a_re + a_im * a_im
    nr = lr - 1.0
    kr = (nr * a_re + li * a_im) / den
    ki = (li * a_re - nr * a_im) / den
    rows = lambda k: jnp.broadcast_to(k[:, None, :], (SSM_GROUPS, SSM_GROUP_CH, SSM_STATE)).reshape(SSM_WIDTH, SSM_STATE)
    krt, kit = rows(kr), rows(ki)
    return lr, li, krt * b_re - kit * b_im, krt * b_im + kit * b_re


def _state_selector():
    row = lax.broadcasted_iota(jnp.int32, (SSM_STATE, GP), 0)
    col = lax.broadcasted_iota(jnp.int32, (SSM_STATE, GP), 1)
    return jnp.where(jnp.bitwise_and(col, SSM_STATE - 1) == row, 1.0, 0.0).astype(BF16)


def _own_group(rows, rows_per_group_log2):
    row = lax.broadcasted_iota(jnp.int32, (rows, GP), 0)
    col = lax.broadcasted_iota(jnp.int32, (rows, GP), 1)
    return jnp.right_shift(row, rows_per_group_log2) == jnp.right_shift(col, 6)


def _three_bf16(x):
    hi = x.astype(BF16)
    rest = x - hi.astype(F32)
    mid = rest.astype(BF16)
    return hi, mid, (rest - mid.astype(F32)).astype(BF16)


def _spread(x, sel):
    return sum(jnp.dot(part, sel, preferred_element_type=F32) for part in _three_bf16(x))


def _collect(xw, sel):
    return sum(lax.dot_general(part, sel, (((1,), (1,)), ((), ())), preferred_element_type=F32) for part in _three_bf16(xw))


def _ssm_param_fwd(a_re, a_im, log_dt, b_re, b_im, c_re, c_im):
    def body(ar_ref, ai_ref, ldt_ref, br_ref, bi_ref, cr_ref, ci_ref, wb_ref, wct_ref, tf_ref, tr_ref):
        lr, li, bbr, bbi = _ssm_param_fn(ar_ref[...], ai_ref[...], ldt_ref[...], br_ref[...], bi_ref[...])
        sel = _state_selector()
        own16 = _own_group(SSM_WIDTH, 4)
        own1 = _own_group(SSM_GROUPS, 0)
        block = lambda m: jnp.where(own16, jnp.dot(m.astype(BF16), sel, preferred_element_type=F32), 0.0).astype(BF16)
        wb_ref[:, 0:GP] = block(bbr)
        wb_ref[:, GP:2 * GP] = block(bbi)
        wct_ref[:, 0:GP] = block(cr_ref[...])
        wct_ref[:, GP:2 * GP] = block(-ci_ref[...])
        flat = lambda m: _colsum(jnp.where(own1, _spread(m, sel), 0.0))
        pr, pi = [], []
        qr, qi = lr, li
        for _ in range(8):
            pr.append(flat(qr))
            pi.append(flat(qi))
            qr, qi = qr * lr - qi * li, qr * li + qi * lr
        row = lax.broadcasted_iota(jnp.int32, (8, GP), 0)
        for n, k in enumerate((1, 2, 4)):
            tf_ref[2 * n] = jnp.where(row >= k, pr[k - 1], 0.0)
            tf_ref[2 * n + 1] = jnp.where(row >= k, pi[k - 1], 0.0)
            tr_ref[2 * n] = jnp.where(row < 8 - k, pr[k - 1], 0.0)
            tr_ref[2 * n + 1] = jnp.where(row < 8 - k, -pi[k - 1], 0.0)
        pick = lambda vals: sum(jnp.where(row == j, v, 0.0) for j, v in enumerate(vals))
        tf_ref[6] = pick(pr)
        tf_ref[7] = pick(pi)
        tr_ref[6] = pick(pr[::-1])
        tr_ref[7] = pick([-v for v in pi[::-1]])

    return pl.pallas_call(
        body, name="ssm_param_fwd",
        out_shape=[jax.ShapeDtypeStruct((SSM_WIDTH, 2 * GP), BF16), jax.ShapeDtypeStruct((SSM_WIDTH, 2 * GP), BF16),
                   jax.ShapeDtypeStruct((8, 8, GP), F32), jax.ShapeDtypeStruct((8, 8, GP), F32)],
        compiler_params=_cparams(),
    )(a_re, a_im, log_dt, b_re, b_im, c_re, c_im)


STRIP_CH = 128
STRIP_ST = 512
N_STRIPS = SSM_WIDTH // STRIP_CH


def _ssm_param_bwd(a_re, a_im, log_dt, b_re, b_im, g_lr, g_li, g_wb, g_wct):
    def body(ar_ref, ai_ref, ldt_ref, br_ref, bi_ref, glr_ref, gli_ref, gwb_ref, gwc_ref,
             o_ar, o_ai, o_ldt, o_br, o_bi, o_cr, o_ci):
        sel = _state_selector()
        own1 = _own_group(SSM_GROUPS, 0)
        row = lax.broadcasted_iota(jnp.int32, (SSM_WIDTH, STRIP_ST), 0)
        col = lax.broadcasted_iota(jnp.int32, (SSM_WIDTH, STRIP_ST), 1)
        own = jnp.bitwise_and(jnp.right_shift(row, 4), 7) == jnp.right_shift(col, 6)
        blocks = lambda m: _collect(jnp.where(own, m, 0.0), sel[:, 0:STRIP_ST])
        unflat = lambda v: _collect(jnp.where(own1, v, 0.0), sel)
        _, vjp = jax.vjp(_ssm_param_fn, ar_ref[...], ai_ref[...], ldt_ref[...], br_ref[...], bi_ref[...])
        d_ar, d_ai, d_ldt, d_br, d_bi = vjp((unflat(glr_ref[...]), unflat(gli_ref[...]),
                                             blocks(gwb_ref[:, 0:STRIP_ST]), blocks(gwb_ref[:, STRIP_ST:2 * STRIP_ST])))
        o_ar[...] = d_ar
        o_ai[...] = d_ai
        o_ldt[...] = d_ldt
        o_br[...] = d_br
        o_bi[...] = d_bi
        o_cr[...] = blocks(gwc_ref[:, 0:STRIP_ST])
        o_ci[...] = -blocks(gwc_ref[:, STRIP_ST:2 * STRIP_ST])

    g, p = SSM_GROUPS, SSM_STATE
    gp = jax.ShapeDtypeStruct((g, p), F32)
    gcp = jax.ShapeDtypeStruct((SSM_WIDTH, p), F32)
    return pl.pallas_call(
        body, name="ssm_param_bwd", out_shape=[gp, gp, jax.ShapeDtypeStruct((g, 1), F32), gcp, gcp, gcp, gcp],
        compiler_params=_cparams(),
    )(a_re, a_im, log_dt, b_re, b_im, g_lr, g_li, g_wb, g_wct)


def _strip(ref, j, im):
    return ref[STRIP_CH * j:STRIP_CH * (j + 1), im * GP + STRIP_ST * j:im * GP + STRIP_ST * (j + 1)]


def _wgrad_strips(a, b_re, b_im, name, im_block=0):
    l = a.shape[0]
    bl = min(l, 512)

    def body(a_ref, bre_ref, bim_ref, o_ref):
        first = pl.program_id(0) == 0
        for j in range(N_STRIPS):
            aj = a_ref[:, STRIP_CH * j:STRIP_CH * (j + 1)]
            states = slice(STRIP_ST * j, STRIP_ST * (j + 1))
            _accumulate(o_ref.at[STRIP_CH * j:STRIP_CH * (j + 1), 0:STRIP_ST], _mm_tn(aj, bre_ref[:, states]), first)
            _accumulate(o_ref.at[STRIP_CH * j:STRIP_CH * (j + 1), STRIP_ST:2 * STRIP_ST], _mm_tn(aj, bim_ref[:, states]), first)

    return pl.pallas_call(
        body, name=name, grid=(l // bl,),
        in_specs=[pl.BlockSpec((bl, SSM_WIDTH), lambda k: (k, 0)), pl.BlockSpec((bl, GP), lambda k: (k, 0)),
                  pl.BlockSpec((bl, GP), lambda k: (k, im_block))],
        out_specs=pl.BlockSpec((SSM_WIDTH, 2 * STRIP_ST), lambda k: (0, 0)),
        out_shape=jax.ShapeDtypeStruct((SSM_WIDTH, 2 * STRIP_ST), F32),
        compiler_params=_cparams("arbitrary"),
    )(a, b_re, b_im)


SCAN_STRIP = 512


def _scan_chunk(inr_ref, ini_ref, outr_ref, outi_ref, cr_ref, ci_ref, tab_ref, tc, reverse):
    n_blocks = tc // 8

    def block(j, _):
        i = (n_blocks - 1 - j) if reverse else j
        rows = pl.ds(pl.multiple_of(i * 8, 8), 8)
        for s in range(GP // SCAN_STRIP):
            sl = pl.ds(s * SCAN_STRIP, SCAN_STRIP)
            xr = inr_ref[rows, sl]
            xi = ini_ref[rows, sl]
            for n, k in enumerate((1, 2, 4)):
                shift = (8 - k) if reverse else k
                sr = pltpu.roll(xr, shift, 0)
                si = pltpu.roll(xi, shift, 0)
                mr = tab_ref[2 * n, :, sl]
                mi = tab_ref[2 * n + 1, :, sl]
                xr, xi = xr + mr * sr - mi * si, xi + mr * si + mi * sr
            qr = tab_ref[6, :, sl]
            qi = tab_ref[7, :, sl]
            cr = cr_ref[:, sl]
            ci = ci_ref[:, sl]
            xr, xi = xr + qr * cr - qi * ci, xi + qr * ci + qi * cr
            outr_ref[rows, sl] = xr
            outi_ref[rows, sl] = xi
            edge = 0 if reverse else 7
            cr_ref[:, sl] = jnp.broadcast_to(xr[edge:edge + 1, :], (8, SCAN_STRIP))
            ci_ref[:, sl] = jnp.broadcast_to(xi[edge:edge + 1, :], (8, SCAN_STRIP))
        return 0

    lax.fori_loop(0, n_blocks, block, 0)


def _glu_pre(z, wg_ref):
    return sum(_mm(z[:, 128 * j:128 * (j + 1)], wg_ref[j]) for j in range(4))


def _ssm_fwd(u, wb, wc, tabs, dskip, grp_d, b_glu, grp_e, tc):
    l = u.shape[0]

    def body(u_ref, wb_ref, wc_ref, tab_ref, d_ref, wg_ref, bg_ref, wo_ref, xr_ref, xi_ref, y_ref, ys_ref,
             bur, bui, cr, ci):
        @pl.when(pl.program_id(0) == 0)
        def _():
            cr[...] = jnp.zeros_like(cr)
            ci[...] = jnp.zeros_like(ci)

        uv = u_ref[...]
        ub = uv.astype(BF16)
        for j in range(N_STRIPS):
            uj = ub[:, STRIP_CH * j:STRIP_CH * (j + 1)]
            states = slice(STRIP_ST * j, STRIP_ST * (j + 1))
            bur[:, states] = _mm(uj, _strip(wb_ref, j, 0))
            bui[:, states] = _mm(uj, _strip(wb_ref, j, 1))
        _scan_chunk(bur, bui, xr_ref, xi_ref, cr, ci, tab_ref, tc, False)
        y = jnp.concatenate(
            [_mm_nt(xr_ref[:, STRIP_ST * j:STRIP_ST * (j + 1)], _strip(wc_ref, j, 0))
             + _mm_nt(xi_ref[:, STRIP_ST * j:STRIP_ST * (j + 1)], _strip(wc_ref, j, 1)) for j in range(N_STRIPS)],
            axis=-1) + d_ref[...] * uv
        y_ref[...] = y
        z = _gelu(y)
        z2 = z * _sigmoid(_glu_pre(z, wg_ref) + bg_ref[...])
        for s in range(4):
            ys_ref[:, 256 * s:256 * (s + 1)] = _mm(z2, wo_ref[s])

    return pl.pallas_call(
        body, name="ssm_fwd", grid=(l // tc,),
        in_specs=[_rows(tc, 512), _resident((512, 2 * GP)), _resident((512, 2 * GP)), _resident((8, 8, GP)),
                  _resident((1, 512)), _member_block("w_glu"), _resident((1, 512)), _member_block("w_o_ssm")],
        out_specs=[_rows(tc, GP), _rows(tc, GP), _rows(tc, 512), _rows(tc, D_MODEL)],
        out_shape=[jax.ShapeDtypeStruct((l, GP), F32), jax.ShapeDtypeStruct((l, GP), F32),
                   jax.ShapeDtypeStruct((l, 512), F32), jax.ShapeDtypeStruct((l, D_MODEL), F32)],
        scratch_shapes=[pltpu.VMEM((tc, GP), F32), pltpu.VMEM((tc, GP), F32), pltpu.VMEM((8, GP), F32),
                        pltpu.VMEM((8, GP), F32)],
        compiler_params=_cparams("arbitrary"),
    )(u, wb, wc, tabs, dskip, grp_d, b_glu, grp_e)


def _ssm_bwd(dys, y, u, xr, xi, wb, wc, tabs_rev, dskip, grp_d, b_glu, grp_e, tc):
    l = u.shape[0]
    nc = l // tc

    def body(dys_ref, y_ref, u_ref, xr_ref, xi_ref, wb_ref, wc_ref, tab_ref, d_ref, wg_ref, bg_ref, wo_ref,
             du_ref, a_ref, dy_ref, z_ref, z2_ref, dpre_ref, gb_ref, gd_ref, glr_ref, gli_ref,
             dxr, dxi, ar, ai, cr, ci):
        first = pl.program_id(0) == 0

        @pl.when(first)
        def _():
            cr[...] = jnp.zeros_like(cr)
            ci[...] = jnp.zeros_like(ci)

        yv = y_ref[...]
        uv = u_ref[...]
        dz2 = sum(_mm_nt(dys_ref[:, 256 * j:256 * (j + 1)], wo_ref[j]) for j in range(4))
        z = _gelu(yv)
        s = _sigmoid(_glu_pre(z, wg_ref) + bg_ref[...])
        dpre = dz2 * z * s * (1.0 - s)
        dpreb = dpre.astype(BF16)
        dz = dz2 * s + jnp.concatenate([_mm_nt(dpreb, wg_ref[j]) for j in range(4)], axis=-1)
        dy = dz * _gelu_grad(yv)
        z_ref[...] = z.astype(BF16)
        z2_ref[...] = (z * s).astype(BF16)
        dpre_ref[...] = dpre.astype(BF16)
        dy_ref[...] = dy.astype(BF16)
        _accumulate(gb_ref, _colsum(dpre), first)
        _accumulate(gd_ref, _colsum(dy * uv), first)

        dyb = dy.astype(BF16)
        for j in range(N_STRIPS):
            dyj = dyb[:, STRIP_CH * j:STRIP_CH * (j + 1)]
            dxr[:, STRIP_ST * j:STRIP_ST * (j + 1)] = _mm(dyj, _strip(wc_ref, j, 0))
            dxi[:, STRIP_ST * j:STRIP_ST * (j + 1)] = _mm(dyj, _strip(wc_ref, j, 1))
        ar[pl.ds(tc, 8), :] = cr[...]
        ai[pl.ds(tc, 8), :] = ci[...]
        _scan_chunk(dxr, dxi, ar, ai, cr, ci, tab_ref, tc, True)
        a_ref[:, 0:GP] = ar[pl.ds(0, tc), :].astype(BF16)
        a_ref[:, GP:2 * GP] = ai[pl.ds(0, tc), :].astype(BF16)
        du_states = jnp.concatenate(
            [_mm_nt(a_ref[:, STRIP_ST * j:STRIP_ST * (j + 1)], _strip(wb_ref, j, 0))
             + _mm_nt(a_ref[:, GP + STRIP_ST * j:GP + STRIP_ST * (j + 1)], _strip(wb_ref, j, 1)) for j in range(N_STRIPS)],
            axis=-1)
        du_ref[...] = (dy * d_ref[...] + du_states).astype(BF16)
        anr = ar[pl.ds(1, tc), :]
        ani = ai[pl.ds(1, tc), :]
        xrv = xr_ref[...]
        xiv = xi_ref[...]
        _accumulate(glr_ref, _colsum(anr * xrv + ani * xiv), first)
        _accumulate(gli_ref, _colsum(ani * xrv - anr * xiv), first)

    rev = lambda w: pl.BlockSpec((tc, w), lambda i: (nc - 1 - i, 0))
    acc = lambda w: pl.BlockSpec((1, w), lambda i: (0, 0))
    return pl.pallas_call(
        body, name="ssm_bwd", grid=(nc,),
        in_specs=[rev(D_MODEL), rev(512), rev(512), rev(GP), rev(GP), _resident((512, 2 * GP)), _resident((512, 2 * GP)),
                  _resident((8, 8, GP)), _resident((1, 512)), _member_block("w_glu"), _resident((1, 512)),
                  _member_block("w_o_ssm")],
        out_specs=[rev(512), rev(2 * GP), rev(512), rev(512), rev(512), rev(512), acc(512), acc(512), acc(GP), acc(GP)],
        out_shape=[jax.ShapeDtypeStruct((l, 512), BF16), jax.ShapeDtypeStruct((l, 2 * GP), BF16),
                   jax.ShapeDtypeStruct((l, 512), BF16), jax.ShapeDtypeStruct((l, 512), BF16),
                   jax.ShapeDtypeStruct((l, 512), BF16), jax.ShapeDtypeStruct((l, 512), BF16),
                   jax.ShapeDtypeStruct((1, 512), F32), jax.ShapeDtypeStruct((1, 512), F32),
                   jax.ShapeDtypeStruct((1, GP), F32), jax.ShapeDtypeStruct((1, GP), F32)],
        scratch_shapes=[pltpu.VMEM((tc, GP), F32), pltpu.VMEM((tc, GP), F32), pltpu.VMEM((tc + 8, GP), F32),
                        pltpu.VMEM((tc + 8, GP), F32), pltpu.VMEM((8, GP), F32), pltpu.VMEM((8, GP), F32)],
        compiler_params=_cparams("arbitrary"),
    )(dys, y, u, xr, xi, wb, wc, tabs_rev, dskip, grp_d, b_glu, grp_e)


def _swap_halves(b):
    lane = lax.broadcasted_iota(jnp.int32, b.shape, 1)
    return jnp.where(lane < 32, pltpu.roll(b, 96, 1), pltpu.roll(b, 32, 1))


def _rope_tables(pos_ref, invf_ref, sgn_ref):
    ang = pos_ref[...].astype(F32) * invf_ref[...]
    return jnp.cos(ang), jnp.sin(ang) * sgn_ref[...]


def _mla_pre_fwd(lat, pos, invf, sgn, gqa, gkva, gq, gk, w_qb_p, w_kvb, t):
    l = lat.shape[0]

    def body(lat_ref, pos_ref, invf_ref, sgn_ref, gqa_ref, gkva_ref, gq_ref, gk_ref, wq_ref, wkv_ref, q_ref, k_ref, v_ref):
        cs, sn = _rope_tables(pos_ref, invf_ref, sgn_ref)
        ql = _rms_fwd(lat_ref[:, 0:Q_LORA], gqa_ref[...], Q_LORA)
        ckn = _rms_fwd(lat_ref[:, Q_LORA:Q_LORA + KV_LORA], gkva_ref[...], KV_LORA)
        kpe = lat_ref[:, 640:768]
        q0 = _mm(ql, wq_ref[...])
        cknb = ckn.astype(BF16)
        kv = jnp.concatenate([_mm(cknb, wkv_ref[s]) for s in range(4)], axis=-1)
        for h in range(N_HEADS):
            q1 = _rms_fwd(q0[:, HEAD_PAD * h:HEAD_PAD * (h + 1)], gq_ref[...], QK_HEAD)
            b = q1[:, 128:256]
            q_ref[h, :, 0:128] = (q1[:, 0:128] * ATT_SCALE).astype(BF16)
            q_ref[h, :, 128:256] = ((b * cs + _swap_halves(b) * sn) * ATT_SCALE).astype(BF16)
            k0 = jnp.concatenate([kv[:, 256 * h:256 * h + 128], kpe], axis=-1)
            k1 = _rms_fwd(k0, gk_ref[...], QK_HEAD)
            b = k1[:, 128:256]
            k_ref[h, :, 0:128] = k1[:, 0:128].astype(BF16)
            k_ref[h, :, 128:256] = (b * cs + _swap_halves(b) * sn).astype(BF16)
            v_ref[h] = kv[:, 256 * h + 128:256 * h + 256].astype(BF16)

    heads = lambda w: pl.BlockSpec((N_HEADS, t, w), lambda i: (0, i, 0))
    return pl.pallas_call(
        body, name="mla_pre_fwd", grid=(l // t,),
        in_specs=[_rows(t, LAT_W), _rows(t, 1), _resident((1, 128)), _resident((1, 128)), _resident((1, Q_LORA)),
                  _resident((1, KV_LORA)), _resident((1, HEAD_PAD)), _resident((1, HEAD_PAD)),
                  _resident((Q_LORA, N_HEADS * HEAD_PAD)), _member_block("w_kv_b")],
        out_specs=[heads(HEAD_PAD), heads(HEAD_PAD), heads(V_HEAD)],
        out_shape=[jax.ShapeDtypeStruct((N_HEADS, l, HEAD_PAD), BF16), jax.ShapeDtypeStruct((N_HEADS, l, HEAD_PAD), BF16),
                   jax.ShapeDtypeStruct((N_HEADS, l, V_HEAD), BF16)],
        compiler_params=_cparams("parallel"),
    )(lat, pos, invf, sgn, gqa, gkva, gq, gk, w_qb_p, w_kvb)


def _mla_pre_bwd(lat, pos, invf, sgn, gqa, gkva, gq, gk, w_qb_p, w_kvb, dq, dk, dv, t, token):
    l = lat.shape[0]

    def body(lat_ref, pos_ref, invf_ref, sgn_ref, gqa_ref, gkva_ref, gq_ref, gk_ref, wq_ref, wkv_ref, dq_ref, dk_ref, dv_ref,
             token_ref, dlat_ref, ql_ref, dq0_ref, ckn_ref, dkv_ref, ggqa_ref, ggkva_ref, ggq_ref, ggk_ref):
        first = pl.program_id(0) == 0
        cs, sn = _rope_tables(pos_ref, invf_ref, sgn_ref)
        q_lat = lat_ref[:, 0:Q_LORA]
        c_kv = lat_ref[:, Q_LORA:Q_LORA + KV_LORA]
        kpe = lat_ref[:, 640:768]
        ql = _rms_fwd(q_lat, gqa_ref[...], Q_LORA)
        ckn = _rms_fwd(c_kv, gkva_ref[...], KV_LORA)
        ql_ref[...] = ql.astype(BF16)
        ckn_ref[...] = ckn.astype(BF16)
        q0 = _mm(ql, wq_ref[...])
        cknb = ckn.astype(BF16)
        kv = jnp.concatenate([_mm(cknb, wkv_ref[s]) for s in range(4)], axis=-1)
        dkpe = jnp.zeros_like(kpe)
        ggq = jnp.zeros((1, HEAD_PAD), F32)
        ggk = jnp.zeros((1, HEAD_PAD), F32)

        def unrope(d):
            b = d[:, 128:256]
            return jnp.concatenate([d[:, 0:128], b * cs + _swap_halves(b * sn)], axis=-1)

        for h in range(N_HEADS):
            dq1 = unrope(dq_ref[h] * ATT_SCALE)
            dq0h, gq_rows = _rms_bwd(q0[:, HEAD_PAD * h:HEAD_PAD * (h + 1)], gq_ref[...], dq1, QK_HEAD)
            ggq = ggq + _colsum(gq_rows)
            dq0_ref[:, HEAD_PAD * h:HEAD_PAD * (h + 1)] = dq0h.astype(BF16)
            k0 = jnp.concatenate([kv[:, 256 * h:256 * h + 128], kpe], axis=-1)
            dk0, gk_rows = _rms_bwd(k0, gk_ref[...], unrope(dk_ref[h]), QK_HEAD)
            ggk = ggk + _colsum(gk_rows)
            dkpe = dkpe + dk0[:, 128:256]
            dkv_ref[:, 256 * h:256 * h + 128] = dk0[:, 0:128].astype(BF16)
            dkv_ref[:, 256 * h + 128:256 * h + 256] = dv_ref[h].astype(BF16)
        dql = _mm_nt(dq0_ref[...], wq_ref[...])
        dckn = sum(_mm_nt(dkv_ref[:, 512 * s:512 * (s + 1)], wkv_ref[s]) for s in range(4))
        dq_lat, gqa_rows = _rms_bwd(q_lat, gqa_ref[...], dql, Q_LORA)
        dc_kv, gkva_rows = _rms_bwd(c_kv, gkva_ref[...], dckn, KV_LORA)
        dlat_ref[:, 0:Q_LORA] = dq_lat.astype(BF16)
        dlat_ref[:, Q_LORA:Q_LORA + KV_LORA] = dc_kv.astype(BF16)
        dlat_ref[:, 640:768] = dkpe.astype(BF16)
        _accumulate(ggqa_ref, _colsum(gqa_rows), first)
        _accumulate(ggkva_ref, _colsum(gkva_rows), first)
        _accumulate(ggq_ref, ggq, first)
        _accumulate(ggk_ref, ggk, first)

    heads = lambda w: pl.BlockSpec((N_HEADS, t, w), lambda i: (0, i, 0))
    acc = lambda w: pl.BlockSpec((1, w), lambda i: (0, 0))
    return pl.pallas_call(
        body, name="mla_pre_bwd", grid=(l // t,),
        in_specs=[_rows(t, LAT_W), _rows(t, 1), _resident((1, 128)), _resident((1, 128)), _resident((1, Q_LORA)),
                  _resident((1, KV_LORA)), _resident((1, HEAD_PAD)), _resident((1, HEAD_PAD)),
                  _resident((Q_LORA, N_HEADS * HEAD_PAD)), _member_block("w_kv_b"),
                  heads(HEAD_PAD), heads(HEAD_PAD), heads(V_HEAD), ANY],
        out_specs=[_rows(t, LAT_W), _rows(t, Q_LORA), _rows(t, N_HEADS * HEAD_PAD), _rows(t, KV_LORA), _rows(t, N_HEADS * 256),
                   acc(Q_LORA), acc(KV_LORA), acc(HEAD_PAD), acc(HEAD_PAD)],
        out_shape=[jax.ShapeDtypeStruct((l, LAT_W), BF16), jax.ShapeDtypeStruct((l, Q_LORA), BF16),
                   jax.ShapeDtypeStruct((l, N_HEADS * HEAD_PAD), BF16), jax.ShapeDtypeStruct((l, KV_LORA), BF16),
                   jax.ShapeDtypeStruct((l, N_HEADS * 256), BF16), jax.ShapeDtypeStruct((1, Q_LORA), F32),
                   jax.ShapeDtypeStruct((1, KV_LORA), F32), jax.ShapeDtypeStruct((1, HEAD_PAD), F32),
                   jax.ShapeDtypeStruct((1, HEAD_PAD), F32)],
        compiler_params=_cparams("arbitrary"),
    )(lat, pos, invf, sgn, gqa, gkva, gq, gk, w_qb_p, w_kvb, dq, dk, dv, token)


def _causal(s, transposed):
    row = lax.broadcasted_iota(jnp.int32, s.shape, 0)
    col = lax.broadcasted_iota(jnp.int32, s.shape, 1)
    keep = (row <= col) if transposed else (col <= row)
    return jnp.where(keep, s, -jnp.inf)


def _as_row(col):
    n = col.shape[0]
    row = lax.broadcasted_iota(jnp.int32, (n, n), 0)
    lane = lax.broadcasted_iota(jnp.int32, (n, n), 1)
    return jnp.sum(jnp.where(row == lane, col, 0.0), axis=0, keepdims=True)


def _attn_fwd(q, k, v, tq):
    l = q.shape[1]

    hb = 2

    def body(q_ref, k_ref, v_ref, o_ref, lse_ref):
        qi = pl.program_id(1)
        qs = [q_ref[a] for a in range(hb)]

        def step(kb, carry, masked):
            rows = pl.ds(pl.multiple_of(kb * tq, tq), tq)
            out = []
            for a, (m, den, acc) in enumerate(carry):
                s = _mm_nt(qs[a], k_ref[a, rows, :])
                if masked:
                    s = _causal(s, False)
                m_new = jnp.maximum(m, jnp.max(s, axis=-1, keepdims=True))
                alpha = jnp.exp(m - m_new)
                p = jnp.exp(s - m_new)
                den = alpha * den + jnp.sum(p, axis=-1, keepdims=True)
                acc = alpha * acc + _mm(p, v_ref[a, rows, :])
                out.append((m_new, den, acc))
            return tuple(out)

        init = tuple((jnp.full((tq, 1), -jnp.inf, F32), jnp.zeros((tq, 1), F32), jnp.zeros((tq, V_HEAD), F32))
                     for _ in range(hb))
        carry = lax.fori_loop(0, qi, lambda kb, c: step(kb, c, False), init)
        for a, (m, den, acc) in enumerate(step(qi, carry, True)):
            o_ref[:, V_HEAD * a:V_HEAD * (a + 1)] = acc / den
            lse_ref[a, 0] = _as_row(m + jnp.log(den))

    return pl.pallas_call(
        body, name="attn_fwd", grid=(N_HEADS // hb, l // tq),
        in_specs=[pl.BlockSpec((hb, tq, HEAD_PAD), lambda h, i: (h, i, 0)), pl.BlockSpec((hb, l, HEAD_PAD), lambda h, i: (h, 0, 0)),
                  pl.BlockSpec((hb, l, V_HEAD), lambda h, i: (h, 0, 0))],
        out_specs=[pl.BlockSpec((tq, hb * V_HEAD), lambda h, i: (i, h)), pl.BlockSpec((hb, 1, 1, tq), lambda h, i: (h, i, 0, 0))],
        out_shape=[jax.ShapeDtypeStruct((l, N_HEADS * V_HEAD), F32), jax.ShapeDtypeStruct((N_HEADS, l // tq, 1, tq), F32)],
        compiler_params=_cparams("parallel", "arbitrary"),
    )(q, k, v)


def _attn_bwd(q, k, v, o, do, lse_t, tq, token):
    l = q.shape[1]
    nq = l // tq

    hb = 2

    def body(q_ref, k_ref, v_ref, o_ref, do_ref, lse_ref, token_ref, dq_ref, dk_ref, dv_ref):
        ki = pl.program_id(1)

        @pl.when(ki == 0)
        def _():
            dq_ref[...] = jnp.zeros_like(dq_ref)

        kblks = [k_ref[a] for a in range(hb)]
        vblks = [v_ref[a] for a in range(hb)]
        ones = jnp.ones((8, V_HEAD), BF16)

        def step(qb, carry, masked):
            rows = pl.ds(pl.multiple_of(qb * tq, tq), tq)
            out = []
            for a, (dk, dv) in enumerate(carry):
                cols = slice(V_HEAD * a, V_HEAD * (a + 1))
                qblk = q_ref[a, rows, :]
                dov = do_ref[rows, cols]
                dob = dov.astype(BF16)
                delta = sum(_mm_nt(ones, part) for part in _three_bf16(dov * o_ref[rows, cols]))[0:1, :]
                st = _mm_nt(kblks[a], qblk)
                if masked:
                    st = _causal(st, True)
                pt = jnp.exp(st - lse_ref[a, qb])
                dv = dv + _mm(pt, dob)
                dst = (pt * (_mm_nt(vblks[a], dob) - delta)).astype(BF16)
                dk = dk + _mm(dst, qblk)
                dq_ref[a, rows, :] += _mm_tn(dst, kblks[a])
                out.append((dk, dv))
            return tuple(out)

        init = tuple((jnp.zeros((tq, HEAD_PAD), F32), jnp.zeros((tq, V_HEAD), F32)) for _ in range(hb))
        carry = lax.fori_loop(ki + 1, nq, lambda qb, c: step(qb, c, False), step(ki, init, True))
        for a, (dk, dv) in enumerate(carry):
            dk_ref[a] = dk
            dv_ref[a] = dv

    return pl.pallas_call(
        body, name="attn_bwd", grid=(N_HEADS // hb, nq),
        in_specs=[pl.BlockSpec((hb, l, HEAD_PAD), lambda h, i: (h, 0, 0)), pl.BlockSpec((hb, tq, HEAD_PAD), lambda h, i: (h, i, 0)),
                  pl.BlockSpec((hb, tq, V_HEAD), lambda h, i: (h, i, 0)), pl.BlockSpec((l, hb * V_HEAD), lambda h, i: (0, h)),
                  pl.BlockSpec((l, hb * V_HEAD), lambda h, i: (0, h)), pl.BlockSpec((hb, nq, 1, tq), lambda h, i: (h, 0, 0, 0)), ANY],
        out_specs=[pl.BlockSpec((hb, l, HEAD_PAD), lambda h, i: (h, 0, 0)), pl.BlockSpec((hb, tq, HEAD_PAD), lambda h, i: (h, i, 0)),
                   pl.BlockSpec((hb, tq, V_HEAD), lambda h, i: (h, i, 0))],
        out_shape=[jax.ShapeDtypeStruct((N_HEADS, l, HEAD_PAD), F32), jax.ShapeDtypeStruct((N_HEADS, l, HEAD_PAD), F32),
                   jax.ShapeDtypeStruct((N_HEADS, l, V_HEAD), F32)],
        compiler_params=_cparams("parallel", "arbitrary"),
    )(q, k, v, o, do, lse_t, token)


def _row_shards_mm(a, w_ref):
    a = a.astype(BF16)
    return sum(_mm(a[:, 256 * j:256 * (j + 1)], w_ref[j]) for j in range(4))


def _row_shards_mm_nt(a, w_ref):
    a = a.astype(BF16)
    return jnp.concatenate([_mm_nt(a, w_ref[j]) for j in range(4)], axis=-1)


def _merge_fwd(attn, y_ssm, gs, gm, x, grp_a, t):
    l = x.shape[0]

    def body(attn_ref, ys_ref, gs_ref, gm_ref, x_ref, wo_ref, wout_ref, ym_ref, mixed_ref, h_ref):
        y_mla = _row_shards_mm(attn_ref[...], wo_ref)
        ym_ref[...] = y_mla
        mixed = (_sigmoid(gs_ref[...]) * ys_ref[...] + _sigmoid(gm_ref[...]) * y_mla).astype(BF16)
        mixed_ref[...] = mixed
        h_ref[...] = x_ref[...] + _row_shards_mm(mixed, wout_ref)

    r = lambda: _rows(t, D_MODEL)
    return pl.pallas_call(
        body, name="merge_fwd", grid=(l // t,),
        in_specs=[r(), r(), r(), r(), r(), _member_block("w_o_mla"), _member_block("w_out")],
        out_specs=[r(), r(), r()],
        out_shape=[jax.ShapeDtypeStruct((l, D_MODEL), F32), jax.ShapeDtypeStruct((l, D_MODEL), BF16),
                   jax.ShapeDtypeStruct((l, D_MODEL), F32)],
        compiler_params=_cparams("parallel"),
    )(attn, y_ssm, gs, gm, x, grp_a, grp_a)


def _merge_bwd(dh, y_ssm, y_mla, gs, gm, grp_a, t):
    l = dh.shape[0]

    def body(dh_ref, ys_ref, ym_ref, gs_ref, gm_ref, wo_ref, wout_ref, dys_ref, dym_ref, dgs_ref, dgm_ref, dattn_ref):
        dmixed = _row_shards_mm_nt(dh_ref[...], wout_ref)
        sg = _sigmoid(gs_ref[...])
        sm = _sigmoid(gm_ref[...])
        dys_ref[...] = (dmixed * sg).astype(BF16)
        dgs_ref[...] = (dmixed * ys_ref[...] * sg * (1.0 - sg)).astype(BF16)
        dym = (dmixed * sm).astype(BF16)
        dym_ref[...] = dym
        dgm_ref[...] = (dmixed * ym_ref[...] * sm * (1.0 - sm)).astype(BF16)
        dattn_ref[...] = _row_shards_mm_nt(dym, wo_ref)

    r = lambda: _rows(t, D_MODEL)
    bf = jax.ShapeDtypeStruct((l, D_MODEL), BF16)
    return pl.pallas_call(
        body, name="merge_bwd", grid=(l // t,),
        in_specs=[r(), r(), r(), r(), r(), _member_block("w_o_mla"), _member_block("w_out")],
        out_specs=[r(), r(), r(), r(), r()],
        out_shape=[bf, bf, bf, bf, jax.ShapeDtypeStruct((l, D_MODEL), F32)],
        compiler_params=_cparams("parallel"),
    )(dh, y_ssm, y_mla, gs, gm, grp_a, grp_a)


def _mlp_fwd_bwd(h, tgt, g2, grp_a, t):
    l = h.shape[0]

    def body(h_ref, tgt_ref, g_ref, wu_ref, wd_ref, dh_ref, hn_ref, da_ref, hid_ref, dout_ref, loss_ref, dg_ref):
        first = pl.program_id(0) == 0
        hv = h_ref[...]
        g = g_ref[...]
        hn = _rms_fwd(hv, g, D_MODEL).astype(BF16)
        hn_ref[...] = hn
        out = hv
        relus = []
        for s in range(4):
            cols = slice(1024 * s, 1024 * (s + 1))
            relu = jnp.maximum(_mm(hn, wu_ref[s]), 0.0)
            relus.append(relu)
            hid = (relu * relu).astype(BF16)
            hid_ref[:, cols] = hid
            out = out + _mm(hid, wd_ref[s])
        err = out - tgt_ref[...]
        _accumulate(loss_ref, jnp.full((8, 128), jnp.sum(err * err) * (0.5 / D_MODEL), F32), first)
        dout = err * (1.0 / D_MODEL)
        doutb = dout.astype(BF16)
        dout_ref[...] = doutb
        dhn = jnp.zeros_like(hv)
        for s in range(4):
            da = (_mm_nt(doutb, wd_ref[s]) * (2.0 * relus[s])).astype(BF16)
            da_ref[:, 1024 * s:1024 * (s + 1)] = da
            dhn = dhn + _mm_nt(da, wu_ref[s])
        dx, dg_rows = _rms_bwd(hv, g, dhn, D_MODEL)
        dh_ref[...] = dout + dx
        _accumulate(dg_ref, _colsum(dg_rows), first)

    r = lambda w: _rows(t, w)
    return pl.pallas_call(
        body, name="mlp_fwd_bwd", grid=(l // t,),
        in_specs=[r(D_MODEL), r(D_MODEL), _resident((1, D_MODEL)), _member_block("w_up"), _member_block("w_down")],
        out_specs=[r(D_MODEL), r(D_MODEL), r(D_FF), r(D_FF), r(D_MODEL), pl.BlockSpec((8, 128), lambda i: (0, 0)),
                   pl.BlockSpec((1, D_MODEL), lambda i: (0, 0))],
        out_shape=[jax.ShapeDtypeStruct((l, D_MODEL), F32), jax.ShapeDtypeStruct((l, D_MODEL), BF16),
                   jax.ShapeDtypeStruct((l, D_FF), BF16), jax.ShapeDtypeStruct((l, D_FF), BF16),
                   jax.ShapeDtypeStruct((l, D_MODEL), BF16), jax.ShapeDtypeStruct((8, 128), F32),
                   jax.ShapeDtypeStruct((1, D_MODEL), F32)],
        compiler_params=_cparams("arbitrary"),
    )(h, tgt, g2, grp_a, grp_a)


def _wgrad(a, b, name):
    l, m = a.shape
    n = b.shape[1]
    bm = m if m <= 512 else 512
    bl = min(l, 2048 if n <= 1024 else 1024)

    def body(a_ref, b_ref, o_ref):
        _accumulate(o_ref, _mm_tn(a_ref[...], b_ref[...]), pl.program_id(1) == 0)

    return pl.pallas_call(
        body, name=name, grid=(m // bm, l // bl),
        in_specs=[pl.BlockSpec((bl, bm), lambda i, j: (j, i)), pl.BlockSpec((bl, n), lambda i, j: (j, 0))],
        out_specs=pl.BlockSpec((bm, n), lambda i, j: (i, 0)),
        out_shape=jax.ShapeDtypeStruct((m, n), F32),
        compiler_params=_cparams("parallel", "arbitrary"),
    )(a, b)


def _wgrad_into(a, b, member, cut, dest=None):
    group, off, rs, cs = _place_in_group(member)
    l = a.shape[0]
    bm = min(rs, 512)
    bl = min(l, 2048)
    nb = rs // bm
    if cut == "row":
        a_spec = pl.BlockSpec((bl, bm), lambda j, i, k: (k, j * nb + i))
        b_spec = pl.BlockSpec((bl, cs), lambda j, i, k: (k, 0))
    else:
        a_spec = pl.BlockSpec((bl, bm), lambda j, i, k: (k, i))
        b_spec = pl.BlockSpec((bl, cs), lambda j, i, k: (k, j))

    def body(a_ref, b_ref, *rest):
        o_ref = rest[-1]
        part = _mm_tn(a_ref[...], b_ref[...])

        @pl.when(pl.program_id(2) == 0)
        def _():
            o_ref[0] = part

        @pl.when(pl.program_id(2) != 0)
        def _():
            o_ref[0] += part

    operands, in_specs, aliases = [a, b], [a_spec, b_spec], {}
    if dest is not None:
        operands.append(dest)
        in_specs.append(ANY)
        aliases = {2: 0}
    return pl.pallas_call(
        body, name="wgrad_" + member, grid=(4, nb, l // bl), in_specs=in_specs,
        out_specs=pl.BlockSpec((1, bm, cs), lambda j, i, k: (j, off // bm + i, 0)),
        out_shape=jax.ShapeDtypeStruct((4, _group_rows(group), cs), F32), input_output_aliases=aliases,
        compiler_params=_cparams("parallel", "parallel", "arbitrary"),
    )(*operands)


def _adamw(w, g, m, v, name, g_off=0):
    r, c = w.shape
    br = r
    for cand in (256, 128, 64, 32, 16, 8):
        if r % cand == 0 and g_off % cand == 0:
            br = cand
            break

    def body(w_ref, g_ref, m_ref, v_ref, go_ref, d_ref, nm_ref, nv_ref):
        gv = g_ref[...]
        go_ref[...] = gv
        nm = ADAM_B1 * m_ref[...] + (1.0 - ADAM_B1) * gv
        nv = ADAM_B2 * v_ref[...] + (1.0 - ADAM_B2) * (gv * gv)
        m_hat = nm / (1.0 - ADAM_B1 ** ADAM_STEP)
        v_hat = nv / (1.0 - ADAM_B2 ** ADAM_STEP)
        d_ref[...] = -ADAM_LR * (m_hat / (jnp.sqrt(v_hat) + ADAM_EPS) + ADAM_WD * w_ref[...])
        nm_ref[...] = nm
        nv_ref[...] = nv

    spec = lambda: pl.BlockSpec((br, c), lambda i: (i, 0))
    g_spec = pl.BlockSpec((br, c), lambda i: (g_off // br + i, 0))
    shp = jax.ShapeDtypeStruct((r, c), F32)
    return pl.pallas_call(
        body, name=name, grid=(r // br,), in_specs=[spec(), g_spec, spec(), spec()],
        out_specs=[spec(), spec(), spec(), spec()], out_shape=[shp, shp, shp, shp], compiler_params=_cparams("parallel"),
    )(w, g, m, v)


def _place():
    return lax.axis_index("x"), lax.axis_index("y"), lax.axis_index("c")


def _other_chips(x, y):
    return [(1 - x, y), (x, 1 - y), (1 - x, 1 - y)]


ANY = pl.BlockSpec(memory_space=pl.ANY)


def _gather_weights(bufs):
    n = len(bufs)

    def body(*refs):
        outs, send_sems, recv_sems = refs[n:2 * n], refs[2 * n], refs[2 * n + 1]
        x, y, c = _place()
        chips = _other_chips(x, y)

        def part(g, px, py, pc):
            half = outs[g].shape[1] // 2
            return outs[g].at[2 * px + py, pl.ds(pl.multiple_of(pc * half, 16), half), :]

        def copy(k, src, dst, to):
            return pltpu.make_async_remote_copy(src_ref=src, dst_ref=dst, send_sem=send_sems.at[k], recv_sem=recv_sems.at[k],
                                                device_id=to, device_id_type=MESH)

        first = [copy(6 * g + j, part(g, x, y, c), part(g, x, y, c), (*chip, c)) for g in range(n) for j, chip in enumerate(chips)]
        for cp in first:
            cp.start()
        passed = []
        for g in range(n):
            for j, chip in enumerate(chips):
                landed = part(g, *chip, c)
                copy(6 * g + j, landed, landed, (x, y, c)).wait_recv()
                passed.append(copy(6 * g + 3 + j, landed, landed, (x, y, 1 - c)))
                passed[-1].start()
        for g in range(n):
            for j, chip in enumerate(chips):
                other = part(g, *chip, 1 - c)
                copy(6 * g + 3 + j, other, other, (x, y, c)).wait_recv()
        for cp in first + passed:
            cp.wait_send()

    return pl.pallas_call(
        body, name="gather_weights", in_specs=[ANY] * n, out_specs=[ANY] * n,
        out_shape=[jax.ShapeDtypeStruct(b.shape, b.dtype) for b in bufs], input_output_aliases={g: g for g in range(n)},
        scratch_shapes=[pltpu.SemaphoreType.DMA((6 * n,)), pltpu.SemaphoreType.DMA((6 * n,))],
    )(*bufs)


def _cast_shards(shards, group, place):
    width, members = GROUPS[group]
    rows = _group_rows(group)

    def body(place_ref, *refs):
        out = refs[-1]
        off = 0
        for ref, (_, r) in zip(refs[:-1], members):
            out[0, off:off + r, :] = ref[...].astype(BF16)
            off += r

    grid_spec = pltpu.PrefetchScalarGridSpec(
        num_scalar_prefetch=1, grid=(1,),
        in_specs=[pl.BlockSpec((r, width), lambda i, p: (0, 0)) for _, r in members],
        out_specs=pl.BlockSpec((1, rows, width), lambda i, p: (p[0], 0, 0)))
    return pl.pallas_call(
        body, name="cast_shards_" + group, grid_spec=grid_spec, out_shape=jax.ShapeDtypeStruct((4, rows, width), BF16),
        compiler_params=_cparams("arbitrary"),
    )(place, *[shards[name] for name, _ in members])


def _swap_gradient_halves(bufs):
    n = len(bufs)

    def body(*refs):
        ins, outs, send_sems, recv_sems = refs[:n], refs[n:2 * n], refs[2 * n], refs[2 * n + 1]
        x, y, c = _place()
        copies = []
        for g in range(n):
            half = ins[g].shape[1] // 2
            give = ins[g].at[:, pl.ds(pl.multiple_of((1 - c) * half, 8), half), :]
            copies.append(pltpu.make_async_remote_copy(src_ref=give, dst_ref=outs[g], send_sem=send_sems.at[g],
                                                       recv_sem=recv_sems.at[g], device_id=(x, y, 1 - c), device_id_type=MESH))
        for cp in copies:
            cp.start()
        for cp in copies:
            cp.wait()

    return pl.pallas_call(
        body, name="swap_gradient_halves", in_specs=[ANY] * n, out_specs=[ANY] * n,
        out_shape=[jax.ShapeDtypeStruct((4, b.shape[1] // 2, b.shape[2]), b.dtype) for b in bufs],
        scratch_shapes=[pltpu.SemaphoreType.DMA((n,)), pltpu.SemaphoreType.DMA((n,))],
    )(*bufs)


def _block_rows(h):
    return next(cand for cand in (256, 192, 128, 64, 32, 16) if h % cand == 0)


def _add_pair(buf, got, place, name):
    n, h, w = got.shape
    bh = _block_rows(h)
    nb = h // bh

    def body(place_ref, a_ref, b_ref, s_ref, sb_ref):
        s = a_ref[...] + b_ref[...]
        s_ref[...] = s
        sb_ref[...] = s.astype(BF16)

    spec = lambda: pl.BlockSpec((1, bh, w), lambda j, i, p: (j, i, 0))
    grid_spec = pltpu.PrefetchScalarGridSpec(
        num_scalar_prefetch=1, grid=(n, nb),
        in_specs=[pl.BlockSpec((1, bh, w), lambda j, i, p: (j, p[1] * nb + i, 0)), spec()], out_specs=[spec(), spec()])
    return pl.pallas_call(
        body, name=name, grid_spec=grid_spec,
        out_shape=[jax.ShapeDtypeStruct(got.shape, F32), jax.ShapeDtypeStruct(got.shape, BF16)],
        compiler_params=_cparams("parallel", "parallel"),
    )(place, buf, got)


def _scatter_to_chips(bufs):
    n = len(bufs)

    def body(*refs):
        ins, outs, send_sems, recv_sems = refs[:n], refs[n:2 * n], refs[2 * n], refs[2 * n + 1]
        x, y, c = _place()
        copies = [pltpu.make_async_remote_copy(src_ref=ins[g].at[2 * px + py], dst_ref=outs[g].at[j],
                                               send_sem=send_sems.at[3 * g + j], recv_sem=recv_sems.at[3 * g + j],
                                               device_id=(px, py, c), device_id_type=MESH)
                  for g in range(n) for j, (px, py) in enumerate(_other_chips(x, y))]
        for cp in copies:
            cp.start()
        for cp in copies:
            cp.wait()

    return pl.pallas_call(
        body, name="scatter_to_chips", in_specs=[ANY] * n, out_specs=[ANY] * n,
        out_shape=[jax.ShapeDtypeStruct((3,) + b.shape[1:], b.dtype) for b in bufs],
        scratch_shapes=[pltpu.SemaphoreType.DMA((3 * n,)), pltpu.SemaphoreType.DMA((3 * n,))],
    )(*bufs)


def _add_received(pair, got, place, name):
    _, h, w = pair.shape
    bh = _block_rows(h)
    nb = h // bh

    def body(place_ref, own_ref, got_ref, o_ref):
        o_ref[...] = ((own_ref[0] + got_ref[0].astype(F32)) + got_ref[1].astype(F32)) + got_ref[2].astype(F32)

    grid_spec = pltpu.PrefetchScalarGridSpec(
        num_scalar_prefetch=1, grid=(nb,),
        in_specs=[pl.BlockSpec((1, bh, w), lambda i, p: (p[0], i, 0)), pl.BlockSpec((3, bh, w), lambda i, p: (0, i, 0))],
        out_specs=pl.BlockSpec((bh, w), lambda i, p: (p[1] * nb + i, 0)))
    return pl.pallas_call(
        body, name=name, grid_spec=grid_spec, out_shape=jax.ShapeDtypeStruct((2 * h, w), F32),
        compiler_params=_cparams("parallel"),
    )(place, pair, got)


def _swap_reduced_halves(bufs):
    n = len(bufs)

    def body(*refs):
        outs, send_sems, recv_sems = refs[n:2 * n], refs[2 * n], refs[2 * n + 1]
        x, y, c = _place()
        copies = []
        for g in range(n):
            half = outs[g].shape[0] // 2
            own = outs[g].at[pl.ds(pl.multiple_of(c * half, 8), half), :]
            copies.append(pltpu.make_async_remote_copy(src_ref=own, dst_ref=own, send_sem=send_sems.at[g],
                                                       recv_sem=recv_sems.at[g], device_id=(x, y, 1 - c), device_id_type=MESH))
        for cp in copies:
            cp.start()
        for g in range(n):
            half = outs[g].shape[0] // 2
            other = outs[g].at[pl.ds(pl.multiple_of((1 - c) * half, 8), half), :]
            pltpu.make_async_remote_copy(src_ref=other, dst_ref=other, send_sem=send_sems.at[g], recv_sem=recv_sems.at[g],
                                         device_id=(x, y, 1 - c), device_id_type=MESH).wait_recv()
        for cp in copies:
            cp.wait_send()

    return pl.pallas_call(
        body, name="swap_reduced_halves", in_specs=[ANY] * n, out_specs=[ANY] * n,
        out_shape=[jax.ShapeDtypeStruct(b.shape, b.dtype) for b in bufs], input_output_aliases={g: g for g in range(n)},
        scratch_shapes=[pltpu.SemaphoreType.DMA((n,)), pltpu.SemaphoreType.DMA((n,))],
    )(*bufs)


HBM = pl.BlockSpec(memory_space=pltpu.HBM)
SEM = pl.BlockSpec(memory_space=pltpu.SEMAPHORE)


def _copies_start(name, bufs, n_copies, plan, after=None):
    n = len(bufs)
    extra = [] if after is None else [after]

    def body(*refs):
        sems = refs[n + len(extra):n + len(extra) + 2 * n_copies]
        x, y, c = _place()
        for i, (src, dst, dev) in enumerate(plan(refs[:n], x, y, c)):
            pltpu.make_async_remote_copy(src_ref=src, dst_ref=dst, send_sem=sems[i], recv_sem=sems[n_copies + i],
                                         device_id=dev, device_id_type=MESH).start()
        token = refs[-1]
        token[...] = jnp.zeros_like(token)

    out = pl.pallas_call(
        body, name=name,
        out_shape=[pltpu.SemaphoreType.DMA(())] * (2 * n_copies) + [pltpu.HBM(b.shape, b.dtype) for b in bufs]
        + [jax.ShapeDtypeStruct((8, 128), F32)],
        in_specs=[HBM] * n + [ANY] * len(extra),
        out_specs=[SEM] * (2 * n_copies) + [HBM] * n + [pl.BlockSpec(memory_space=pltpu.VMEM)],
        input_output_aliases={i: 2 * n_copies + i for i in range(n)},
        compiler_params=pltpu.CompilerParams(has_side_effects=pltpu.SideEffectType.DATAFLOW_SIDE_EFFECTING),
    )(*[pltpu.with_memory_space_constraint(b, pltpu.HBM) for b in bufs], *extra)
    return list(out[:2 * n_copies]), list(out[2 * n_copies:-1]), out[-1]


def _copies_wait(name, bufs, sems, after, plan):
    n = len(bufs)
    k = len(sems) // 2

    def body(*refs):
        sem_refs = refs[n:n + 2 * k]
        x, y, c = _place()
        for i, (sent, landed, dev) in enumerate(plan(refs[:n], x, y, c)):
            cp = pltpu.make_async_remote_copy(src_ref=sent, dst_ref=landed, send_sem=sem_refs[i], recv_sem=sem_refs[k + i],
                                              device_id=dev, device_id_type=MESH)
            cp.wait_send()
            cp.wait_recv()

    return pl.pallas_call(
        body, name=name, out_shape=[pltpu.HBM(b.shape, b.dtype) for b in bufs],
        in_specs=[HBM] * n + [SEM] * (2 * k) + [ANY], out_specs=[HBM] * n, input_output_aliases={i: i for i in range(n)},
        compiler_params=pltpu.CompilerParams(has_side_effects=pltpu.SideEffectType.DATAFLOW_SIDE_EFFECTING),
    )(*bufs, *sems, after)


class _GroupAExchange:
    def __init__(self, own_a, place, after):
        self.place = place
        self.gather = _copies_start("gather_a_start", [own_a], 3, self._gather_plan, after)

    @staticmethod
    def _gather_plan(refs, x, y, c):
        (wa,) = refs
        return [(wa.at[2 * x + y], wa.at[2 * x + y], (px, py, c)) for px, py in _other_chips(x, y)]

    @staticmethod
    def _gather_landed(refs, x, y, c):
        (wa,) = refs
        return [(wa.at[2 * x + y], wa.at[2 * px + py], (px, py, c)) for px, py in _other_chips(x, y)]

    def weights(self, after):
        sems, bufs, _ = self.gather
        return _copies_wait("gather_a_wait", bufs, sems, after, self._gather_landed)[0]

    def token_after_gather_start(self):
        return self.gather[2]


    def start_pair(self, ga):
        half = ga.shape[1] // 2
        land = lax.empty((4, half, ga.shape[2]), F32)

        def plan(refs, x, y, c):
            g, got = refs
            return [(g.at[:, pl.ds(pl.multiple_of((1 - c) * half, 8), half), :], got, (x, y, 1 - c))]

        self._pair_plan = plan
        self._pair = _copies_start("pair_a_start", [ga, land], 1, plan)
        return self._pair[2]

    def pair_done_start_scatter(self, after):
        sems, bufs, _ = self._pair
        ga, got = _copies_wait("pair_a_wait", bufs, sems, after, self._pair_plan)
        self._pair_f32, pair_bf16 = _add_pair(ga, got, self.place, "add_pair_a")
        land = lax.empty((3,) + pair_bf16.shape[1:], BF16)

        def plan(refs, x, y, c):
            mine, got = refs
            return [(mine.at[2 * px + py], got.at[j], (px, py, c)) for j, (px, py) in enumerate(_other_chips(x, y))]

        self._scatter_plan = plan
        self._scatter = _copies_start("scatter_a_start", [pair_bf16, land], 3, plan)
        return self._scatter[2]

    def scatter_done_start_join(self, after):
        sems, bufs, _ = self._scatter
        _, got = _copies_wait("scatter_a_wait", bufs, sems, after, self._scatter_plan)
        mine = _add_received(self._pair_f32, got, self.place, "add_received_a")
        half = mine.shape[0] // 2
        rows = lambda r, pc: r.at[pl.ds(pl.multiple_of(pc * half, 8), half), :]
        self._join_landed = lambda refs, x, y, c: [(rows(refs[0], c), rows(refs[0], 1 - c), (x, y, 1 - c))]
        self._join = _copies_start("join_a_start", [mine], 1,
                                   lambda refs, x, y, c: [(rows(refs[0], c), rows(refs[0], c), (x, y, 1 - c))])
        return self._join[2]

    def join_done(self, after):
        sems, bufs, _ = self._join
        self.reduced = _copies_wait("join_a_wait", bufs, sems, after, self._join_landed)[0]


def _all_sum_small(mine):
    rows, w = mine.shape

    def body(in_ref, out_ref, sibling, pair, chips, send_sems, recv_sems):
        x, y, c = _place()
        swap = pltpu.make_async_remote_copy(src_ref=in_ref, dst_ref=sibling, send_sem=send_sems.at[0], recv_sem=recv_sems.at[0],
                                            device_id=(x, y, 1 - c), device_id_type=MESH)
        swap.start()
        swap.wait()
        pair[...] = in_ref[...] + sibling[...]
        chip = 2 * x + y
        chips[chip] = pair[...]
        copies = [pltpu.make_async_remote_copy(src_ref=pair, dst_ref=chips.at[chip], send_sem=send_sems.at[1 + j],
                                               recv_sem=recv_sems.at[1 + j], device_id=(px, py, c), device_id_type=MESH)
                  for j, (px, py) in enumerate(_other_chips(x, y))]
        for cp in copies:
            cp.start()
        for cp in copies:
            cp.wait()
        out_ref[...] = ((chips[0] + chips[1]) + chips[2]) + chips[3]

    return pl.pallas_call(
        body, name="all_sum_small", out_shape=jax.ShapeDtypeStruct((rows, w), F32),
        in_specs=[pl.BlockSpec(memory_space=pltpu.VMEM)], out_specs=pl.BlockSpec(memory_space=pltpu.VMEM),
        scratch_shapes=[pltpu.VMEM((rows, w), F32), pltpu.VMEM((rows, w), F32), pltpu.VMEM((4, rows, w), F32),
                        pltpu.SemaphoreType.DMA((4,)), pltpu.SemaphoreType.DMA((4,))],
        compiler_params=pltpu.CompilerParams(vmem_limit_bytes=VMEM_LIMIT_V7X),
    )(mine)


def _join_column_shards(g):
    return jnp.transpose(g, (1, 0, 2)).reshape(g.shape[1], 4 * g.shape[2])


def _split_column_shards(w):
    r = w.shape[0]
    return jnp.transpose(w.reshape(r, 4, w.shape[1] // 4), (1, 0, 2))


def _small_rows(shape):
    return -(-int(np.prod(shape)) // 1024)


def _pack_small(vals):
    segs = []
    for name, shape in SMALL_WEIGHTS:
        flat = vals[name].reshape(-1)
        segs.append(jnp.pad(flat, (0, _small_rows(shape) * 1024 - flat.shape[0])))
    total = sum(s.shape[0] for s in segs) // 1024
    segs.append(jnp.zeros((-total % 8 * 1024,), F32))
    return jnp.concatenate(segs).reshape(-1, 1024)


def _unpack_small(packed):
    out, off = {}, 0
    for name, shape in SMALL_WEIGHTS:
        rows = _small_rows(shape)
        out[name] = packed[off:off + rows].reshape(-1)[:int(np.prod(shape))].reshape(shape)
        off += rows
    return out


W_IN_SHARD = D_IN // 4
W_IN_GAP = 1216


def _pad_w_in(g):
    cut = W_IN_GAP - W_IN_SHARD
    return jnp.concatenate([g[0], g[1][:, :cut], jnp.zeros((g.shape[1], D_IN_PAD - D_IN), g.dtype), g[1][:, cut:], g[2], g[3]],
                           axis=1)


def _unpad_w_in(g):
    skip = D_IN_PAD - D_IN
    second = jnp.concatenate([g[:, W_IN_SHARD:W_IN_GAP], g[:, W_IN_GAP + skip:2 * W_IN_SHARD + skip]], axis=1)
    return jnp.stack([g[:, :W_IN_SHARD], second, g[:, 2 * W_IN_SHARD + skip:3 * W_IN_SHARD + skip],
                      g[:, 3 * W_IN_SHARD + skip:]])


def _pad_heads(w):
    r = w.shape[0]
    return jnp.pad(w.reshape(r, N_HEADS, QK_HEAD), ((0, 0), (0, 0), (0, HEAD_PAD - QK_HEAD))).reshape(r, N_HEADS * HEAD_PAD)


def _unpad_heads(g):
    r = g.shape[0]
    return g.reshape(r, N_HEADS, HEAD_PAD)[:, :, :QK_HEAD].reshape(r, N_HEADS * QK_HEAD)


def _local_step(x, positions, tgt, grp, small, ex):
    l = x.shape[0]
    t = min(l, 512)
    t_mlp = min(l, 256)
    tq = min(l, 512)
    tc = min(l, 256)
    row = lambda v: v.reshape(1, -1).astype(F32)

    w_in_p = _pad_w_in(grp["b"])
    w_qb_p = _pad_heads(_join_column_shards(grp["c"]))
    g1, g2 = row(small["norm_mix"]), row(small["norm_mlp"])
    gqa, gkva = row(small["q_a_norm"]), row(small["kv_a_norm"])
    gq = jnp.pad(row(small["q_norm"]), ((0, 0), (0, HEAD_PAD - QK_HEAD)))
    gk = jnp.pad(row(small["k_norm"]), ((0, 0), (0, HEAD_PAD - QK_HEAD)))
    half = QK_ROPE // 2
    inv_freq = ROPE_THETA ** (-jnp.arange(half, dtype=F32) / half)
    invf = jnp.concatenate([inv_freq, inv_freq, jnp.zeros((64,), F32)]).reshape(1, 128)
    sgn = jnp.concatenate([-jnp.ones((half,), F32), jnp.ones((half,), F32), jnp.zeros((64,), F32)]).reshape(1, 128)
    pos = positions.reshape(l, 1)

    a_re, a_im = small["ssm_a_re"], small["ssm_a_im"]
    log_dt = small["ssm_log_dt"].reshape(SSM_GROUPS, 1)
    to_gcp = lambda b: jnp.transpose(b, (0, 2, 1)).reshape(SSM_WIDTH, SSM_STATE)
    from_gcp = lambda b: jnp.transpose(b.reshape(SSM_GROUPS, SSM_GROUP_CH, SSM_STATE), (0, 2, 1))
    b_re, b_im = to_gcp(small["ssm_b_re"]), to_gcp(small["ssm_b_im"])
    c_re, c_im = small["ssm_c_re"].reshape(SSM_WIDTH, SSM_STATE), small["ssm_c_im"].reshape(SSM_WIDTH, SSM_STATE)
    wb, wc, tabs_fwd, tabs_rev = _ssm_param_fwd(a_re, a_im, log_dt, b_re, b_im, c_re, c_im)
    dskip = row(small["ssm_d"])
    b_glu = row(small["b_glu"])

    u, lat, gs, gm = _in_proj_fwd(x, g1, w_in_p, t, ex.token_after_gather_start())
    xr, xi, y, y_ssm = _ssm_fwd(u, wb, wc, tabs_fwd, dskip, grp["d"], b_glu, grp["e"], tc)
    q, k, v = _mla_pre_fwd(lat, pos, invf, sgn, gqa, gkva, gq, gk, w_qb_p, grp["d"], t)
    attn, lse = _attn_fwd(q, k, v, tq)
    grp_a = ex.weights(attn)
    y_mla, mixed, h = _merge_fwd(attn, y_ssm, gs, gm, x, grp_a, t)
    dh, hn, da, hid, dout, loss_blk, g_norm_mlp = _mlp_fwd_bwd(h, tgt, g2, grp_a, t_mlp)

    grads = {}
    ga = _wgrad_into(hn, da, "w_up", "col", _wgrad_into(hid, dout, "w_down", "row"))
    dys, dym, dgs, dgm, dattn = _merge_bwd(dh, y_ssm, y_mla, gs, gm, grp_a, t)
    ga = _wgrad_into(attn, dym, "w_o_mla", "row", _wgrad_into(mixed, dh, "w_out", "row", ga))

    dq, dk, dv = _attn_bwd(q, k, v, attn, dattn, lse, tq, ex.start_pair(ga))
    d_lat, ql, dq0, ckn, dkv, g_qa, g_kva, g_q, g_k = _mla_pre_bwd(lat, pos, invf, sgn, gqa, gkva, gq, gk, w_qb_p, grp["d"],
                                                                    dq, dk, dv, t, ex.pair_done_start_scatter(dk))
    grads["c"] = _split_column_shards(_unpad_heads(_wgrad(ql, dq0, "wgrad_q_b")))

    d_u, adj, dy, z, z2, dpre, g_b_glu, g_d, g_lr, g_li = _ssm_bwd(
        dys, y, u, xr, xi, wb, wc, tabs_rev, dskip, grp["d"], b_glu, grp["e"], tc)
    grads["d"] = _wgrad_into(z, dpre, "w_glu", "row", _wgrad_into(ckn, dkv, "w_kv_b", "col"))
    grads["e"] = _wgrad_into(z2, dys, "w_o_ssm", "col")
    g_ar, g_ai, g_ldt, g_br, g_bi, g_cr, g_ci = _ssm_param_bwd(
        a_re, a_im, log_dt, b_re, b_im, g_lr, g_li, _wgrad_strips(u, adj, adj, "wgrad_ssm_b", 1),
        _wgrad_strips(dy, xr, xi, "wgrad_ssm_c"))

    grad_x, xn, dproj, g_norm_mix = _in_proj_bwd(x, g1, w_in_p, d_u, d_lat, dgs, dgm, dh, t)
    grads["b"] = _unpad_w_in(_wgrad(xn, dproj, "wgrad_in"))
    ex.scatter_done_start_join(grads["b"])

    g_small = {
        "norm_mix": g_norm_mix.reshape(-1), "norm_mlp": g_norm_mlp.reshape(-1), "q_a_norm": g_qa.reshape(-1),
        "kv_a_norm": g_kva.reshape(-1), "q_norm": g_q.reshape(-1)[:QK_HEAD], "k_norm": g_k.reshape(-1)[:QK_HEAD],
        "ssm_a_re": g_ar, "ssm_a_im": g_ai, "ssm_log_dt": g_ldt.reshape(-1),
        "ssm_b_re": from_gcp(g_br), "ssm_b_im": from_gcp(g_bi),
        "ssm_c_re": g_cr.reshape(SSM_GROUPS, SSM_GROUP_CH, SSM_STATE), "ssm_c_im": g_ci.reshape(SSM_GROUPS, SSM_GROUP_CH, SSM_STATE),
        "ssm_d": g_d.reshape(SSM_GROUPS, SSM_GROUP_CH), "b_glu": g_b_glu.reshape(-1),
    }
    return loss_blk[0, 0], grad_x, grads, g_small


def kernel(x, positions, norm_mix, w_in, q_a_norm, kv_a_norm, w_q_b, w_kv_b, q_norm, k_norm, w_o_mla, ssm_a_re, ssm_a_im, ssm_log_dt, ssm_b_re, ssm_b_im, ssm_c_re, ssm_c_im, ssm_d, w_glu, b_glu, w_o_ssm, w_out, norm_mlp, w_up, w_down, loss_target, m_norm_mix, m_w_in, m_q_a_norm, m_kv_a_norm, m_w_q_b, m_w_kv_b, m_q_norm, m_k_norm, m_w_o_mla, m_ssm_a_re, m_ssm_a_im, m_ssm_log_dt, m_ssm_b_re, m_ssm_b_im, m_ssm_c_re, m_ssm_c_im, m_ssm_d, m_w_glu, m_b_glu, m_w_o_ssm, m_w_out, m_norm_mlp, m_w_up, m_w_down, v_norm_mix, v_w_in, v_q_a_norm, v_kv_a_norm, v_w_q_b, v_w_kv_b, v_q_norm, v_k_norm, v_w_o_mla, v_ssm_a_re, v_ssm_a_im, v_ssm_log_dt, v_ssm_b_re, v_ssm_b_im, v_ssm_c_re, v_ssm_c_im, v_ssm_d, v_w_glu, v_b_glu, v_w_o_ssm, v_w_out, v_norm_mlp, v_w_up, v_w_down):
    args = dict(locals())
    w = {n: args[n][0] for n in WEIGHT_ORDER}
    m = {n: args["m_" + n][0] for n in WEIGHT_ORDER}
    v = {n: args["v_" + n][0] for n in WEIGHT_ORDER}
    big_names = [n for n, *_ in BIG_WEIGHTS]
    small_names = [n for n, _ in SMALL_WEIGHTS]

    place = jnp.stack([2 * lax.axis_index("x") + lax.axis_index("y"), lax.axis_index("c")]).astype(jnp.int32)
    groups = [g for g in sorted(GROUPS) if g != "a"]

    gathered = _gather_weights([_cast_shards(w, g, place) for g in groups])
    grp = dict(zip(groups, gathered))
    ex = _GroupAExchange(_cast_shards(w, "a", place), place, gathered[0])
    small = {n: w[n] for n in small_names}

    loss_local, grad_x, grads, g_small = _local_step(x[0], positions[0], loss_target[0], grp, small, ex)
    loss = lax.psum(loss_local, ("x", "y", "c"))

    bufs = [grads[g] for g in groups]
    pairs = [_add_pair(b, got, place, "add_pair_" + g) for g, b, got in zip(groups, bufs, _swap_gradient_halves(bufs))]
    landed = _scatter_to_chips([p[1] for p in pairs])
    halves = [_add_received(p[0], got, place, "add_received_" + g) for g, p, got in zip(groups, pairs, landed)]
    reduced = dict(zip(groups, _swap_reduced_halves(halves)))
    ex.join_done(reduced[groups[0]])
    reduced["a"] = ex.reduced

    small_sum = _all_sum_small(_pack_small(g_small))

    grad_w, delta_w, new_m, new_v = {}, {}, {}, {}
    for n in big_names:
        g, off, _, _ = _place_in_group(n)
        grad_w[n], delta_w[n], new_m[n], new_v[n] = _adamw(w[n], reduced[g], m[n], v[n], "adamw_" + n, off)
    g_s, d_s, m_s, v_s = _adamw(_pack_small(small), small_sum, _pack_small({n: m[n] for n in small_names}),
                                _pack_small({n: v[n] for n in small_names}), "adamw_small")
    g_s, d_s, m_s, v_s = _unpack_small(g_s), _unpack_small(d_s), _unpack_small(m_s), _unpack_small(v_s)
    for n in small_names:
        grad_w[n], delta_w[n], new_m[n], new_v[n] = g_s[n], d_s[n], m_s[n], v_s[n]

    lead = lambda d: [d[n][None] for n in WEIGHT_ORDER]
    return (loss, grad_x[None], *lead(grad_w), *lead(delta_w), *lead(new_m), *lead(new_v))
```

```python
import math

import jax
import jax.numpy as jnp
import numpy as np
from jax import lax
from jax.experimental import pallas as pl
from jax.experimental.pallas import tpu as pltpu

F32 = jnp.float32
BF16 = jnp.bfloat16

D_MODEL = 1024
SSM_GROUPS = 32
SSM_GROUP_CH = 16
SSM_WIDTH = 512
SSM_STATE = 64
GP = SSM_GROUPS * SSM_STATE
N_HEADS = 8
QK_NOPE = 128
QK_ROPE = 64
QK_HEAD = 192
HEAD_PAD = 256
V_HEAD = 128
Q_LORA = 384
KV_LORA = 256
LAT_W = 768
D_IN = 3264
D_IN_PAD = 3328
D_FF = 4096
ROPE_THETA = 10000.0
EPS = 1e-6
ATT_SCALE = QK_HEAD ** -0.5

ADAM_LR = 0.001
ADAM_B1 = 0.9
ADAM_B2 = 0.999
ADAM_EPS = 1e-08
ADAM_WD = 0.01
ADAM_STEP = 10

VMEM_LIMIT_V7X = 56 * 1024 * 1024
MESH = pl.DeviceIdType.MESH

BIG_WEIGHTS = (
    ("w_in", 1024, 3264, "col"),
    ("w_q_b", 384, 1536, "col"),
    ("w_kv_b", 256, 2048, "col"),
    ("w_o_mla", 1024, 1024, "row"),
    ("w_glu", 512, 512, "row"),
    ("w_o_ssm", 512, 1024, "col"),
    ("w_out", 1024, 1024, "row"),
    ("w_up", 1024, 4096, "col"),
    ("w_down", 4096, 1024, "row"),
)
GROUPS = {
    "a": (1024, (("w_down", 1024), ("w_up", 1024), ("w_o_mla", 256), ("w_out", 256))),
    "b": (816, (("w_in", 1024),)),
    "c": (384, (("w_q_b", 384),)),
    "d": (512, (("w_kv_b", 256), ("w_glu", 128))),
    "e": (256, (("w_o_ssm", 512),)),
}


def _group_rows(group):
    return sum(r for _, r in GROUPS[group][1])


def _place_in_group(name):
    for group, (width, members) in GROUPS.items():
        off = 0
        for member, rows in members:
            if member == name:
                return group, off, rows, width
            off += rows
    raise KeyError(name)


SMALL_WEIGHTS = (
    ("norm_mix", (1024,)), ("q_a_norm", (384,)), ("kv_a_norm", (256,)), ("q_norm", (192,)), ("k_norm", (192,)),
    ("ssm_a_re", (32, 64)), ("ssm_a_im", (32, 64)), ("ssm_log_dt", (32,)),
    ("ssm_b_re", (32, 64, 16)), ("ssm_b_im", (32, 64, 16)), ("ssm_c_re", (32, 16, 64)), ("ssm_c_im", (32, 16, 64)),
    ("ssm_d", (32, 16)), ("b_glu", (512,)), ("norm_mlp", (1024,)),
)
WEIGHT_ORDER = ('norm_mix', 'w_in', 'q_a_norm', 'kv_a_norm', 'w_q_b', 'w_kv_b', 'q_norm', 'k_norm', 'w_o_mla', 'ssm_a_re',
                'ssm_a_im', 'ssm_log_dt', 'ssm_b_re', 'ssm_b_im', 'ssm_c_re', 'ssm_c_im', 'ssm_d', 'w_glu', 'b_glu',
                'w_o_ssm', 'w_out', 'norm_mlp', 'w_up', 'w_down')


def _cparams(*sem):
    return pltpu.CompilerParams(dimension_semantics=sem if sem else None, vmem_limit_bytes=VMEM_LIMIT_V7X)


def _resident(shape, index=None):
    index = (0,) * len(shape) if index is None else index
    return pl.BlockSpec(shape, lambda *_: index, pipeline_mode=pl.Buffered(1))


def _member_block(name):
    _, off, rows, width = _place_in_group(name)
    return _resident((4, rows, width), (0, off // rows, 0))


def _rows(t, width):
    return pl.BlockSpec((t, width), lambda i: (i, 0))


def _mm(a, b):
    return jnp.dot(a.astype(BF16), b.astype(BF16), preferred_element_type=F32)


def _mm_nt(a, b):
    return lax.dot_general(a.astype(BF16), b.astype(BF16), (((1,), (1,)), ((), ())), preferred_element_type=F32)


def _mm_tn(a, b):
    return lax.dot_general(a.astype(BF16), b.astype(BF16), (((0,), (0,)), ((), ())), preferred_element_type=F32)


def _rms_fwd(x, g, n):
    r = lax.rsqrt(jnp.sum(x * x, axis=-1, keepdims=True) * (1.0 / n) + EPS)
    return x * r * g


def _rms_bwd(x, g, dy, n):
    r = lax.rsqrt(jnp.sum(x * x, axis=-1, keepdims=True) * (1.0 / n) + EPS)
    xh = x * r
    dxh = dy * g
    dx = r * (dxh - xh * (jnp.sum(dxh * xh, axis=-1, keepdims=True) * (1.0 / n)))
    return dx, dy * xh


def _colsum(a):
    return jnp.sum(a, axis=0, keepdims=True)


def _accumulate(ref, value, first):
    @pl.when(first)
    def _():
        ref[...] = value

    @pl.when(jnp.logical_not(first))
    def _():
        ref[...] += value


def _sigmoid(a):
    return 1.0 / (1.0 + jnp.exp(-a))


GELU_C = math.sqrt(2.0 / math.pi)
GELU_A = 0.044715


def _gelu(y):
    return 0.5 * y * (1.0 + jnp.tanh(GELU_C * (y + GELU_A * y * y * y)))


def _gelu_grad(y):
    t = jnp.tanh(GELU_C * (y + GELU_A * y * y * y))
    return 0.5 * (1.0 + t) + 0.5 * y * (1.0 - t * t) * GELU_C * (1.0 + 3.0 * GELU_A * y * y)


def _in_proj_fwd(x, g1, w_in_p, t, token):
    l = x.shape[0]

    def body(x_ref, g_ref, w_ref, token_ref, u_ref, lat_ref, gs_ref, gm_ref):
        xn = _rms_fwd(x_ref[...], g_ref[...], D_MODEL).astype(BF16)
        u_ref[...] = _mm(xn, w_ref[:, 0:512])
        lat_ref[...] = _mm(xn, w_ref[:, 512:1280])
        gs_ref[...] = _mm(xn, w_ref[:, 1280:2304])
        gm_ref[...] = _mm(xn, w_ref[:, 2304:3328])

    return pl.pallas_call(
        body, name="in_proj_fwd", grid=(l // t,),
        in_specs=[_rows(t, D_MODEL), _resident((1, D_MODEL)), _resident((D_MODEL, D_IN_PAD)), ANY],
        out_specs=[_rows(t, 512), _rows(t, LAT_W), _rows(t, D_MODEL), _rows(t, D_MODEL)],
        out_shape=[jax.ShapeDtypeStruct((l, 512), F32), jax.ShapeDtypeStruct((l, LAT_W), F32),
                   jax.ShapeDtypeStruct((l, D_MODEL), F32), jax.ShapeDtypeStruct((l, D_MODEL), F32)],
        compiler_params=_cparams("parallel"),
    )(x, g1, w_in_p, token)


def _in_proj_bwd(x, g1, w_in_p, d_u, d_lat, d_gs, d_gm, dh, t):
    l = x.shape[0]

    def body(x_ref, g_ref, w_ref, du_ref, dlat_ref, dgs_ref, dgm_ref, dh_ref, gx_ref, xn_ref, dproj_ref, dg_ref):
        xv = x_ref[...]
        g = g_ref[...]
        xn_ref[...] = _rms_fwd(xv, g, D_MODEL).astype(BF16)
        dproj_ref[:, 0:512] = du_ref[...]
        dproj_ref[:, 512:1280] = dlat_ref[...]
        dproj_ref[:, 1280:2304] = dgs_ref[...]
        dproj_ref[:, 2304:3328] = dgm_ref[...]
        dxn = _mm_nt(dproj_ref[...], w_ref[...])
        dx, dg_rows = _rms_bwd(xv, g, dxn, D_MODEL)
        gx_ref[...] = dh_ref[...] + dx
        _accumulate(dg_ref, _colsum(dg_rows), pl.program_id(0) == 0)

    return pl.pallas_call(
        body, name="in_proj_bwd", grid=(l // t,),
        in_specs=[_rows(t, D_MODEL), _resident((1, D_MODEL)), _resident((D_MODEL, D_IN_PAD)), _rows(t, 512),
                  _rows(t, LAT_W), _rows(t, D_MODEL), _rows(t, D_MODEL), _rows(t, D_MODEL)],
        out_specs=[_rows(t, D_MODEL), _rows(t, D_MODEL), _rows(t, D_IN_PAD), pl.BlockSpec((1, D_MODEL), lambda i: (0, 0))],
        out_shape=[jax.ShapeDtypeStruct((l, D_MODEL), F32), jax.ShapeDtypeStruct((l, D_MODEL), BF16),
                   jax.ShapeDtypeStruct((l, D_IN_PAD), BF16), jax.ShapeDtypeStruct((1, D_MODEL), F32)],
        compiler_params=_cparams("arbitrary"),
    )(x, g1, w_in_p, d_u, d_lat, d_gs, d_gm, dh)


def _ssm_param_fn(a_re, a_im, log_dt, b_re, b_im):
    dt = jnp.exp(log_dt)
    er = jnp.exp(a_re * dt)
    lr = er * jnp.cos(a_im * dt)
    li = er * jnp.sin(a_im * dt)
    den = a_re * a_re + a_im * a_im
    nr = lr - 1.0
    kr = (nr * a_re + li * a_im) / den
    ki = (li * a_re - nr * a_im) / den
    rows = lambda k: jnp.broadcast_to(k[:, None, :], (SSM_GROUPS, SSM_GROUP_CH, SSM_STATE)).reshape(SSM_WIDTH, SSM_STATE)
    krt, kit = rows(kr), rows(ki)
    return lr, li, krt * b_re - kit * b_im, krt * b_im + kit * b_re


def _state_selector():
    row = lax.broadcasted_iota(jnp.int32, (SSM_STATE, GP), 0)
    col = lax.broadcasted_iota(jnp.int32, (SSM_STATE, GP), 1)
    return jnp.where(jnp.bitwise_and(col, SSM_STATE - 1) == row, 1.0, 0.0).astype(BF16)


def _own_group(rows, rows_per_group_log2):
    row = lax.broadcasted_iota(jnp.int32, (rows, GP), 0)
    col = lax.broadcasted_iota(jnp.int32, (rows, GP), 1)
    return jnp.right_shift(row, rows_per_group_log2) == jnp.right_shift(col, 6)


def _three_bf16(x):
    hi = x.astype(BF16)
    rest = x - hi.astype(F32)
    mid = rest.astype(BF16)
    return hi, mid, (rest - mid.astype(F32)).astype(BF16)


def _spread(x, sel):
    return sum(jnp.dot(part, sel, preferred_element_type=F32) for part in _three_bf16(x))


def _collect(xw, sel):
    return sum(lax.dot_general(part, sel, (((1,), (1,)), ((), ())), preferred_element_type=F32) for part in _three_bf16(xw))


def _ssm_param_fwd(a_re, a_im, log_dt, b_re, b_im, c_re, c_im):
    def body(ar_ref, ai_ref, ldt_ref, br_ref, bi_ref, cr_ref, ci_ref, wb_ref, wct_ref, tf_ref, tr_ref):
        lr, li, bbr, bbi = _ssm_param_fn(ar_ref[...], ai_ref[...], ldt_ref[...], br_ref[...], bi_ref[...])
        sel = _state_selector()
        own16 = _own_group(SSM_WIDTH, 4)
        own1 = _own_group(SSM_GROUPS, 0)
        block = lambda m: jnp.where(own16, jnp.dot(m.astype(BF16), sel, preferred_element_type=F32), 0.0).astype(BF16)
        wb_ref[:, 0:GP] = block(bbr)
        wb_ref[:, GP:2 * GP] = block(bbi)
        wct_ref[:, 0:GP] = block(cr_ref[...])
        wct_ref[:, GP:2 * GP] = block(-ci_ref[...])
        flat = lambda m: _colsum(jnp.where(own1, _spread(m, sel), 0.0))
        pr, pi = [], []
        qr, qi = lr, li
        for _ in range(8):
            pr.append(flat(qr))
            pi.append(flat(qi))
            qr, qi = qr * lr - qi * li, qr * li + qi * lr
        row = lax.broadcasted_iota(jnp.int32, (8, GP), 0)
        for n, k in enumerate((1, 2, 4)):
            tf_ref[2 * n] = jnp.where(row >= k, pr[k - 1], 0.0)
            tf_ref[2 * n + 1] = jnp.where(row >= k, pi[k - 1], 0.0)
            tr_ref[2 * n] = jnp.where(row < 8 - k, pr[k - 1], 0.0)
            tr_ref[2 * n + 1] = jnp.where(row < 8 - k, -pi[k - 1], 0.0)
        pick = lambda vals: sum(jnp.where(row == j, v, 0.0) for j, v in enumerate(vals))
        tf_ref[6] = pick(pr)
        tf_ref[7] = pick(pi)
        tr_ref[6] = pick(pr[::-1])
        tr_ref[7] = pick([-v for v in pi[::-1]])

    return pl.pallas_call(
        body, name="ssm_param_fwd",
        out_shape=[jax.ShapeDtypeStruct((SSM_WIDTH, 2 * GP), BF16), jax.ShapeDtypeStruct((SSM_WIDTH, 2 * GP), BF16),
                   jax.ShapeDtypeStruct((8, 8, GP), F32), jax.ShapeDtypeStruct((8, 8, GP), F32)],
        compiler_params=_cparams(),
    )(a_re, a_im, log_dt, b_re, b_im, c_re, c_im)


STRIP_CH = 128
STRIP_ST = 512
N_STRIPS = SSM_WIDTH // STRIP_CH


def _ssm_param_bwd(a_re, a_im, log_dt, b_re, b_im, g_lr, g_li, g_wb, g_wct):
    def body(ar_ref, ai_ref, ldt_ref, br_ref, bi_ref, glr_ref, gli_ref, gwb_ref, gwc_ref,
             o_ar, o_ai, o_ldt, o_br, o_bi, o_cr, o_ci):
        sel = _state_selector()
        own1 = _own_group(SSM_GROUPS, 0)
        row = lax.broadcasted_iota(jnp.int32, (SSM_WIDTH, STRIP_ST), 0)
        col = lax.broadcasted_iota(jnp.int32, (SSM_WIDTH, STRIP_ST), 1)
        own = jnp.bitwise_and(jnp.right_shift(row, 4), 7) == jnp.right_shift(col, 6)
        blocks = lambda m: _collect(jnp.where(own, m, 0.0), sel[:, 0:STRIP_ST])
        unflat = lambda v: _collect(jnp.where(own1, v, 0.0), sel)
        _, vjp = jax.vjp(_ssm_param_fn, ar_ref[...], ai_ref[...], ldt_ref[...], br_ref[...], bi_ref[...])
        d_ar, d_ai, d_ldt, d_br, d_bi = vjp((unflat(glr_ref[...]), unflat(gli_ref[...]),
                                             blocks(gwb_ref[:, 0:STRIP_ST]), blocks(gwb_ref[:, STRIP_ST:2 * STRIP_ST])))
        o_ar[...] = d_ar
        o_ai[...] = d_ai
        o_ldt[...] = d_ldt
        o_br[...] = d_br
        o_bi[...] = d_bi
        o_cr[...] = blocks(gwc_ref[:, 0:STRIP_ST])
        o_ci[...] = -blocks(gwc_ref[:, STRIP_ST:2 * STRIP_ST])

    g, p = SSM_GROUPS, SSM_STATE
    gp = jax.ShapeDtypeStruct((g, p), F32)
    gcp = jax.ShapeDtypeStruct((SSM_WIDTH, p), F32)
    return pl.pallas_call(
        body, name="ssm_param_bwd", out_shape=[gp, gp, jax.ShapeDtypeStruct((g, 1), F32), gcp, gcp, gcp, gcp],
        compiler_params=_cparams(),
    )(a_re, a_im, log_dt, b_re, b_im, g_lr, g_li, g_wb, g_wct)


def _strip(ref, j, im):
    return ref[STRIP_CH * j:STRIP_CH * (j + 1), im * GP + STRIP_ST * j:im * GP + STRIP_ST * (j + 1)]


def _wgrad_strips(a, b_re, b_im, name, im_block, token):
    l = a.shape[0]
    bl = min(l, 512)

    def body(a_ref, bre_ref, bim_ref, token_ref, o_ref):
        first = pl.program_id(0) == 0
        for j in range(N_STRIPS):
            aj = a_ref[:, STRIP_CH * j:STRIP_CH * (j + 1)]
            states = slice(STRIP_ST * j, STRIP_ST * (j + 1))
            _accumulate(o_ref.at[STRIP_CH * j:STRIP_CH * (j + 1), 0:STRIP_ST], _mm_tn(aj, bre_ref[:, states]), first)
            _accumulate(o_ref.at[STRIP_CH * j:STRIP_CH * (j + 1), STRIP_ST:2 * STRIP_ST], _mm_tn(aj, bim_ref[:, states]), first)

    return pl.pallas_call(
        body, name=name, grid=(l // bl,),
        in_specs=[pl.BlockSpec((bl, SSM_WIDTH), lambda k: (k, 0)), pl.BlockSpec((bl, GP), lambda k: (k, 0)),
                  pl.BlockSpec((bl, GP), lambda k: (k, im_block)), ANY],
        out_specs=pl.BlockSpec((SSM_WIDTH, 2 * STRIP_ST), lambda k: (0, 0)),
        out_shape=jax.ShapeDtypeStruct((SSM_WIDTH, 2 * STRIP_ST), F32),
        compiler_params=_cparams("arbitrary"),
    )(a, b_re, b_im, token)


SCAN_STRIP = 512


def _scan_chunk(inr_ref, ini_ref, outr_ref, outi_ref, cr_ref, ci_ref, tab_ref, tc, reverse):
    n_blocks = tc // 8

    def block(j, _):
        i = (n_blocks - 1 - j) if reverse else j
        rows = pl.ds(pl.multiple_of(i * 8, 8), 8)
        for s in range(GP // SCAN_STRIP):
            sl = pl.ds(s * SCAN_STRIP, SCAN_STRIP)
            xr = inr_ref[rows, sl]
            xi = ini_ref[rows, sl]
            for n, k in enumerate((1, 2, 4)):
                shift = (8 - k) if reverse else k
                sr = pltpu.roll(xr, shift, 0)
                si = pltpu.roll(xi, shift, 0)
                mr = tab_ref[2 * n, :, sl]
                mi = tab_ref[2 * n + 1, :, sl]
                xr, xi = xr + mr * sr - mi * si, xi + mr * si + mi * sr
            qr = tab_ref[6, :, sl]
            qi = tab_ref[7, :, sl]
            cr = cr_ref[:, sl]
            ci = ci_ref[:, sl]
            xr, xi = xr + qr * cr - qi * ci, xi + qr * ci + qi * cr
            outr_ref[rows, sl] = xr
            outi_ref[rows, sl] = xi
            edge = 0 if reverse else 7
            cr_ref[:, sl] = jnp.broadcast_to(xr[edge:edge + 1, :], (8, SCAN_STRIP))
            ci_ref[:, sl] = jnp.broadcast_to(xi[edge:edge + 1, :], (8, SCAN_STRIP))
        return 0

    lax.fori_loop(0, n_blocks, block, 0)


def _glu_pre(z, wg_ref):
    return sum(_mm(z[:, 128 * j:128 * (j + 1)], wg_ref[j]) for j in range(4))


def _ssm_fwd(u, wb, wc, tabs, dskip, grp_d, b_glu, grp_e, tc):
    l = u.shape[0]

    def body(u_ref, wb_ref, wc_ref, tab_ref, d_ref, wg_ref, bg_ref, wo_ref, xr_ref, xi_ref, y_ref, ys_ref,
             bur, bui, cr, ci):
        @pl.when(pl.program_id(0) == 0)
        def _():
            cr[...] = jnp.zeros_like(cr)
            ci[...] = jnp.zeros_like(ci)

        uv = u_ref[...]
        ub = uv.astype(BF16)
        for j in range(N_STRIPS):
            uj = ub[:, STRIP_CH * j:STRIP_CH * (j + 1)]
            states = slice(STRIP_ST * j, STRIP_ST * (j + 1))
            bur[:, states] = _mm(uj, _strip(wb_ref, j, 0))
            bui[:, states] = _mm(uj, _strip(wb_ref, j, 1))
        _scan_chunk(bur, bui, xr_ref, xi_ref, cr, ci, tab_ref, tc, False)
        y = jnp.concatenate(
            [_mm_nt(xr_ref[:, STRIP_ST * j:STRIP_ST * (j + 1)], _strip(wc_ref, j, 0))
             + _mm_nt(xi_ref[:, STRIP_ST * j:STRIP_ST * (j + 1)], _strip(wc_ref, j, 1)) for j in range(N_STRIPS)],
            axis=-1) + d_ref[...] * uv
        y_ref[...] = y
        z = _gelu(y)
        z2 = z * _sigmoid(_glu_pre(z, wg_ref) + bg_ref[...])
        for s in range(4):
            ys_ref[:, 256 * s:256 * (s + 1)] = _mm(z2, wo_ref[s])

    return pl.pallas_call(
        body, name="ssm_fwd", grid=(l // tc,),
        in_specs=[_rows(tc, 512), _resident((512, 2 * GP)), _resident((512, 2 * GP)), _resident((8, 8, GP)),
                  _resident((1, 512)), _member_block("w_glu"), _resident((1, 512)), _member_block("w_o_ssm")],
        out_specs=[_rows(tc, GP), _rows(tc, GP), _rows(tc, 512), _rows(tc, D_MODEL)],
        out_shape=[jax.ShapeDtypeStruct((l, GP), F32), jax.ShapeDtypeStruct((l, GP), F32),
                   jax.ShapeDtypeStruct((l, 512), F32), jax.ShapeDtypeStruct((l, D_MODEL), F32)],
        scratch_shapes=[pltpu.VMEM((tc, GP), F32), pltpu.VMEM((tc, GP), F32), pltpu.VMEM((8, GP), F32),
                        pltpu.VMEM((8, GP), F32)],
        compiler_params=_cparams("arbitrary"),
    )(u, wb, wc, tabs, dskip, grp_d, b_glu, grp_e)


def _ssm_bwd(dys, y, u, xr, xi, wb, wc, tabs_rev, dskip, grp_d, b_glu, grp_e, tc):
    l = u.shape[0]
    nc = l // tc

    def body(dys_ref, y_ref, u_ref, xr_ref, xi_ref, wb_ref, wc_ref, tab_ref, d_ref, wg_ref, bg_ref, wo_ref,
             du_ref, a_ref, dy_ref, z_ref, z2_ref, dpre_ref, gb_ref, gd_ref, glr_ref, gli_ref,
             dxr, dxi, ar, ai, cr, ci):
        first = pl.program_id(0) == 0

        @pl.when(first)
        def _():
            cr[...] = jnp.zeros_like(cr)
            ci[...] = jnp.zeros_like(ci)

        yv = y_ref[...]
        uv = u_ref[...]
        dz2 = sum(_mm_nt(dys_ref[:, 256 * j:256 * (j + 1)], wo_ref[j]) for j in range(4))
        z = _gelu(yv)
        s = _sigmoid(_glu_pre(z, wg_ref) + bg_ref[...])
        dpre = dz2 * z * s * (1.0 - s)
        dpreb = dpre.astype(BF16)
        dz = dz2 * s + jnp.concatenate([_mm_nt(dpreb, wg_ref[j]) for j in range(4)], axis=-1)
        dy = dz * _gelu_grad(yv)
        z_ref[...] = z.astype(BF16)
        z2_ref[...] = (z * s).astype(BF16)
        dpre_ref[...] = dpre.astype(BF16)
        dy_ref[...] = dy.astype(BF16)
        _accumulate(gb_ref, _colsum(dpre), first)
        _accumulate(gd_ref, _colsum(dy * uv), first)

        dyb = dy.astype(BF16)
        for j in range(N_STRIPS):
            dyj = dyb[:, STRIP_CH * j:STRIP_CH * (j + 1)]
            dxr[:, STRIP_ST * j:STRIP_ST * (j + 1)] = _mm(dyj, _strip(wc_ref, j, 0))
            dxi[:, STRIP_ST * j:STRIP_ST * (j + 1)] = _mm(dyj, _strip(wc_ref, j, 1))
        ar[pl.ds(tc, 8), :] = cr[...]
        ai[pl.ds(tc, 8), :] = ci[...]
        _scan_chunk(dxr, dxi, ar, ai, cr, ci, tab_ref, tc, True)
        a_ref[:, 0:GP] = ar[pl.ds(0, tc), :].astype(BF16)
        a_ref[:, GP:2 * GP] = ai[pl.ds(0, tc), :].astype(BF16)
        du_states = jnp.concatenate(
            [_mm_nt(a_ref[:, STRIP_ST * j:STRIP_ST * (j + 1)], _strip(wb_ref, j, 0))
             + _mm_nt(a_ref[:, GP + STRIP_ST * j:GP + STRIP_ST * (j + 1)], _strip(wb_ref, j, 1)) for j in range(N_STRIPS)],
            axis=-1)
        du_ref[...] = (dy * d_ref[...] + du_states).astype(BF16)
        anr = ar[pl.ds(1, tc), :]
        ani = ai[pl.ds(1, tc), :]
        xrv = xr_ref[...]
        xiv = xi_ref[...]
        _accumulate(glr_ref, _colsum(anr * xrv + ani * xiv), first)
        _accumulate(gli_ref, _colsum(ani * xrv - anr * xiv), first)

    rev = lambda w: pl.BlockSpec((tc, w), lambda i: (nc - 1 - i, 0))
    acc = lambda w: pl.BlockSpec((1, w), lambda i: (0, 0))
    return pl.pallas_call(
        body, name="ssm_bwd", grid=(nc,),
        in_specs=[rev(D_MODEL), rev(512), rev(512), rev(GP), rev(GP), _resident((512, 2 * GP)), _resident((512, 2 * GP)),
                  _resident((8, 8, GP)), _resident((1, 512)), _member_block("w_glu"), _resident((1, 512)),
                  _member_block("w_o_ssm")],
        out_specs=[rev(512), rev(2 * GP), rev(512), rev(512), rev(512), rev(512), acc(512), acc(512), acc(GP), acc(GP)],
        out_shape=[jax.ShapeDtypeStruct((l, 512), BF16), jax.ShapeDtypeStruct((l, 2 * GP), BF16),
                   jax.ShapeDtypeStruct((l, 512), BF16), jax.ShapeDtypeStruct((l, 512), BF16),
                   jax.ShapeDtypeStruct((l, 512), BF16), jax.ShapeDtypeStruct((l, 512), BF16),
                   jax.ShapeDtypeStruct((1, 512), F32), jax.ShapeDtypeStruct((1, 512), F32),
                   jax.ShapeDtypeStruct((1, GP), F32), jax.ShapeDtypeStruct((1, GP), F32)],
        scratch_shapes=[pltpu.VMEM((tc, GP), F32), pltpu.VMEM((tc, GP), F32), pltpu.VMEM((tc + 8, GP), F32),
                        pltpu.VMEM((tc + 8, GP), F32), pltpu.VMEM((8, GP), F32), pltpu.VMEM((8, GP), F32)],
        compiler_params=_cparams("arbitrary"),
    )(dys, y, u, xr, xi, wb, wc, tabs_rev, dskip, grp_d, b_glu, grp_e)


def _swap_halves(b):
    lane = lax.broadcasted_iota(jnp.int32, b.shape, 1)
    return jnp.where(lane < 32, pltpu.roll(b, 96, 1), pltpu.roll(b, 32, 1))


def _rope_tables(pos_ref, invf_ref, sgn_ref):
    ang = pos_ref[...].astype(F32) * invf_ref[...]
    return jnp.cos(ang), jnp.sin(ang) * sgn_ref[...]


def _mla_pre_fwd(lat, pos, invf, sgn, gqa, gkva, gq, gk, w_qb_p, w_kvb, t):
    l = lat.shape[0]

    def body(lat_ref, pos_ref, invf_ref, sgn_ref, gqa_ref, gkva_ref, gq_ref, gk_ref, wq_ref, wkv_ref, q_ref, k_ref, v_ref):
        cs, sn = _rope_tables(pos_ref, invf_ref, sgn_ref)
        ql = _rms_fwd(lat_ref[:, 0:Q_LORA], gqa_ref[...], Q_LORA)
        ckn = _rms_fwd(lat_ref[:, Q_LORA:Q_LORA + KV_LORA], gkva_ref[...], KV_LORA)
        kpe = lat_ref[:, 640:768]
        q0 = _mm(ql, wq_ref[...])
        cknb = ckn.astype(BF16)
        kv = jnp.concatenate([_mm(cknb, wkv_ref[s]) for s in range(4)], axis=-1)
        for h in range(N_HEADS):
            q1 = _rms_fwd(q0[:, HEAD_PAD * h:HEAD_PAD * (h + 1)], gq_ref[...], QK_HEAD)
            b = q1[:, 128:256]
            q_ref[h, :, 0:128] = (q1[:, 0:128] * ATT_SCALE).astype(BF16)
            q_ref[h, :, 128:256] = ((b * cs + _swap_halves(b) * sn) * ATT_SCALE).astype(BF16)
            k0 = jnp.concatenate([kv[:, 256 * h:256 * h + 128], kpe], axis=-1)
            k1 = _rms_fwd(k0, gk_ref[...], QK_HEAD)
            b = k1[:, 128:256]
            k_ref[h, :, 0:128] = k1[:, 0:128].astype(BF16)
            k_ref[h, :, 128:256] = (b * cs + _swap_halves(b) * sn).astype(BF16)
            v_ref[h] = kv[:, 256 * h + 128:256 * h + 256].astype(BF16)

    heads = lambda w: pl.BlockSpec((N_HEADS, t, w), lambda i: (0, i, 0))
    return pl.pallas_call(
        body, name="mla_pre_fwd", grid=(l // t,),
        in_specs=[_rows(t, LAT_W), _rows(t, 1), _resident((1, 128)), _resident((1, 128)), _resident((1, Q_LORA)),
                  _resident((1, KV_LORA)), _resident((1, HEAD_PAD)), _resident((1, HEAD_PAD)),
                  _resident((Q_LORA, N_HEADS * HEAD_PAD)), _member_block("w_kv_b")],
        out_specs=[heads(HEAD_PAD), heads(HEAD_PAD), heads(V_HEAD)],
        out_shape=[jax.ShapeDtypeStruct((N_HEADS, l, HEAD_PAD), BF16), jax.ShapeDtypeStruct((N_HEADS, l, HEAD_PAD), BF16),
                   jax.ShapeDtypeStruct((N_HEADS, l, V_HEAD), BF16)],
        compiler_params=_cparams("parallel"),
    )(lat, pos, invf, sgn, gqa, gkva, gq, gk, w_qb_p, w_kvb)


def _mla_pre_bwd(lat, pos, invf, sgn, gqa, gkva, gq, gk, w_qb_p, w_kvb, dq, dk, dv, t, token):
    l = lat.shape[0]

    def body(lat_ref, pos_ref, invf_ref, sgn_ref, gqa_ref, gkva_ref, gq_ref, gk_ref, wq_ref, wkv_ref, dq_ref, dk_ref, dv_ref,
             token_ref, dlat_ref, ql_ref, dq0_ref, ckn_ref, dkv_ref, ggqa_ref, ggkva_ref, ggq_ref, ggk_ref):
        first = pl.program_id(0) == 0
        cs, sn = _rope_tables(pos_ref, invf_ref, sgn_ref)
        q_lat = lat_ref[:, 0:Q_LORA]
        c_kv = lat_ref[:, Q_LORA:Q_LORA + KV_LORA]
        kpe = lat_ref[:, 640:768]
        ql = _rms_fwd(q_lat, gqa_ref[...], Q_LORA)
        ckn = _rms_fwd(c_kv, gkva_ref[...], KV_LORA)
        ql_ref[...] = ql.astype(BF16)
        ckn_ref[...] = ckn.astype(BF16)
        q0 = _mm(ql, wq_ref[...])
        cknb = ckn.astype(BF16)
        kv = jnp.concatenate([_mm(cknb, wkv_ref[s]) for s in range(4)], axis=-1)
        dkpe = jnp.zeros_like(kpe)
        ggq = jnp.zeros((1, HEAD_PAD), F32)
        ggk = jnp.zeros((1, HEAD_PAD), F32)

        def unrope(d):
            b = d[:, 128:256]
            return jnp.concatenate([d[:, 0:128], b * cs + _swap_halves(b * sn)], axis=-1)

        for h in range(N_HEADS):
            dq1 = unrope(dq_ref[h] * ATT_SCALE)
            dq0h, gq_rows = _rms_bwd(q0[:, HEAD_PAD * h:HEAD_PAD * (h + 1)], gq_ref[...], dq1, QK_HEAD)
            ggq = ggq + _colsum(gq_rows)
            dq0_ref[:, HEAD_PAD * h:HEAD_PAD * (h + 1)] = dq0h.astype(BF16)
            k0 = jnp.concatenate([kv[:, 256 * h:256 * h + 128], kpe], axis=-1)
            dk0, gk_rows = _rms_bwd(k0, gk_ref[...], unrope(dk_ref[h]), QK_HEAD)
            ggk = ggk + _colsum(gk_rows)
            dkpe = dkpe + dk0[:, 128:256]
            dkv_ref[:, 256 * h:256 * h + 128] = dk0[:, 0:128].astype(BF16)
            dkv_ref[:, 256 * h + 128:256 * h + 256] = dv_ref[h].astype(BF16)
        dql = _mm_nt(dq0_ref[...], wq_ref[...])
        dckn = sum(_mm_nt(dkv_ref[:, 512 * s:512 * (s + 1)], wkv_ref[s]) for s in range(4))
        dq_lat, gqa_rows = _rms_bwd(q_lat, gqa_ref[...], dql, Q_LORA)
        dc_kv, gkva_rows = _rms_bwd(c_kv, gkva_ref[...], dckn, KV_LORA)
        dlat_ref[:, 0:Q_LORA] = dq_lat.astype(BF16)
        dlat_ref[:, Q_LORA:Q_LORA + KV_LORA] = dc_kv.astype(BF16)
        dlat_ref[:, 640:768] = dkpe.astype(BF16)
        _accumulate(ggqa_ref, _colsum(gqa_rows), first)
        _accumulate(ggkva_ref, _colsum(gkva_rows), first)
        _accumulate(ggq_ref, ggq, first)
        _accumulate(ggk_ref, ggk, first)

    heads = lambda w: pl.BlockSpec((N_HEADS, t, w), lambda i: (0, i, 0))
    acc = lambda w: pl.BlockSpec((1, w), lambda i: (0, 0))
    return pl.pallas_call(
        body, name="mla_pre_bwd", grid=(l // t,),
        in_specs=[_rows(t, LAT_W), _rows(t, 1), _resident((1, 128)), _resident((1, 128)), _resident((1, Q_LORA)),
                  _resident((1, KV_LORA)), _resident((1, HEAD_PAD)), _resident((1, HEAD_PAD)),
                  _resident((Q_LORA, N_HEADS * HEAD_PAD)), _member_block("w_kv_b"),
                  heads(HEAD_PAD), heads(HEAD_PAD), heads(V_HEAD), ANY],
        out_specs=[_rows(t, LAT_W), _rows(t, Q_LORA), _rows(t, N_HEADS * HEAD_PAD), _rows(t, KV_LORA), _rows(t, N_HEADS * 256),
                   acc(Q_LORA), acc(KV_LORA), acc(HEAD_PAD), acc(HEAD_PAD)],
        out_shape=[jax.ShapeDtypeStruct((l, LAT_W), BF16), jax.ShapeDtypeStruct((l, Q_LORA), BF16),
                   jax.ShapeDtypeStruct((l, N_HEADS * HEAD_PAD), BF16), jax.ShapeDtypeStruct((l, KV_LORA), BF16),
                   jax.ShapeDtypeStruct((l, N_HEADS * 256), BF16), jax.ShapeDtypeStruct((1, Q_LORA), F32),
                   jax.ShapeDtypeStruct((1, KV_LORA), F32), jax.ShapeDtypeStruct((1, HEAD_PAD), F32),
                   jax.ShapeDtypeStruct((1, HEAD_PAD), F32)],
        compiler_params=_cparams("arbitrary"),
    )(lat, pos, invf, sgn, gqa, gkva, gq, gk, w_qb_p, w_kvb, dq, dk, dv, token)


def _causal(s, transposed):
    row = lax.broadcasted_iota(jnp.int32, s.shape, 0)
    col = lax.broadcasted_iota(jnp.int32, s.shape, 1)
    keep = (row <= col) if transposed else (col <= row)
    return jnp.where(keep, s, -jnp.inf)


def _as_row(col):
    n = col.shape[0]
    row = lax.broadcasted_iota(jnp.int32, (n, n), 0)
    lane = lax.broadcasted_iota(jnp.int32, (n, n), 1)
    return jnp.sum(jnp.where(row == lane, col, 0.0), axis=0, keepdims=True)


def _attn_fwd(q, k, v, tq):
    l = q.shape[1]

    hb = 4

    def body(q_ref, k_ref, v_ref, o_ref, lse_ref):
        qi = pl.program_id(1)
        qs = [q_ref[a] for a in range(hb)]

        def step(kb, carry, masked):
            rows = pl.ds(pl.multiple_of(kb * tq, tq), tq)
            out = []
            for a, (m, den, acc) in enumerate(carry):
                s = _mm_nt(qs[a], k_ref[a, rows, :])
                if masked:
                    s = _causal(s, False)
                m_new = jnp.maximum(m, jnp.max(s, axis=-1, keepdims=True))
                alpha = jnp.exp(m - m_new)
                p = jnp.exp(s - m_new)
                den = alpha * den + jnp.sum(p, axis=-1, keepdims=True)
                acc = alpha * acc + _mm(p, v_ref[a, rows, :])
                out.append((m_new, den, acc))
            return tuple(out)

        init = tuple((jnp.full((tq, 1), -jnp.inf, F32), jnp.zeros((tq, 1), F32), jnp.zeros((tq, V_HEAD), F32))
                     for _ in range(hb))
        carry = lax.fori_loop(0, qi, lambda kb, c: step(kb, c, False), init)
        for a, (m, den, acc) in enumerate(step(qi, carry, True)):
            o_ref[:, V_HEAD * a:V_HEAD * (a + 1)] = acc / den
            lse_ref[a, 0] = _as_row(m + jnp.log(den))

    return pl.pallas_call(
        body, name="attn_fwd", grid=(N_HEADS // hb, l // tq),
        in_specs=[pl.BlockSpec((hb, tq, HEAD_PAD), lambda h, i: (h, i, 0)), pl.BlockSpec((hb, l, HEAD_PAD), lambda h, i: (h, 0, 0)),
                  pl.BlockSpec((hb, l, V_HEAD), lambda h, i: (h, 0, 0))],
        out_specs=[pl.BlockSpec((tq, hb * V_HEAD), lambda h, i: (i, h)), pl.BlockSpec((hb, 1, 1, tq), lambda h, i: (h, i, 0, 0))],
        out_shape=[jax.ShapeDtypeStruct((l, N_HEADS * V_HEAD), F32), jax.ShapeDtypeStruct((N_HEADS, l // tq, 1, tq), F32)],
        compiler_params=_cparams("parallel", "arbitrary"),
    )(q, k, v)


def _attn_bwd(q, k, v, o, do, lse_t, tq, token):
    l = q.shape[1]
    nq = l // tq

    hb = 2

    def body(q_ref, k_ref, v_ref, o_ref, do_ref, lse_ref, token_ref, dq_ref, dk_ref, dv_ref):
        ki = pl.program_id(1)

        @pl.when(ki == 0)
        def _():
            dq_ref[...] = jnp.zeros_like(dq_ref)

        kblks = [k_ref[a] for a in range(hb)]
        vblks = [v_ref[a] for a in range(hb)]
        ones = jnp.ones((8, V_HEAD), BF16)

        def step(qb, carry, masked):
            rows = pl.ds(pl.multiple_of(qb * tq, tq), tq)
            out = []
            for a, (dk, dv) in enumerate(carry):
                cols = slice(V_HEAD * a, V_HEAD * (a + 1))
                qblk = q_ref[a, rows, :]
                dov = do_ref[rows, cols]
                dob = dov.astype(BF16)
                delta = sum(_mm_nt(ones, part) for part in _three_bf16(dov * o_ref[rows, cols]))[0:1, :]
                st = _mm_nt(kblks[a], qblk)
                if masked:
                    st = _causal(st, True)
                pt = jnp.exp(st - lse_ref[a, qb])
                dv = dv + _mm(pt, dob)
                dst = (pt * (_mm_nt(vblks[a], dob) - delta)).astype(BF16)
                dk = dk + _mm(dst, qblk)
                dq_ref[a, rows, :] += _mm_tn(dst, kblks[a])
                out.append((dk, dv))
            return tuple(out)

        init = tuple((jnp.zeros((tq, HEAD_PAD), F32), jnp.zeros((tq, V_HEAD), F32)) for _ in range(hb))
        carry = lax.fori_loop(ki + 1, nq, lambda qb, c: step(qb, c, False), step(ki, init, True))
        for a, (dk, dv) in enumerate(carry):
            dk_ref[a] = dk
            dv_ref[a] = dv

    return pl.pallas_call(
        body, name="attn_bwd", grid=(N_HEADS // hb, nq),
        in_specs=[pl.BlockSpec((hb, l, HEAD_PAD), lambda h, i: (h, 0, 0)), pl.BlockSpec((hb, tq, HEAD_PAD), lambda h, i: (h, i, 0)),
                  pl.BlockSpec((hb, tq, V_HEAD), lambda h, i: (h, i, 0)), pl.BlockSpec((l, hb * V_HEAD), lambda h, i: (0, h)),
                  pl.BlockSpec((l, hb * V_HEAD), lambda h, i: (0, h)), pl.BlockSpec((hb, nq, 1, tq), lambda h, i: (h, 0, 0, 0)), ANY],
        out_specs=[pl.BlockSpec((hb, l, HEAD_PAD), lambda h, i: (h, 0, 0)), pl.BlockSpec((hb, tq, HEAD_PAD), lambda h, i: (h, i, 0)),
                   pl.BlockSpec((hb, tq, V_HEAD), lambda h, i: (h, i, 0))],
        out_shape=[jax.ShapeDtypeStruct((N_HEADS, l, HEAD_PAD), F32), jax.ShapeDtypeStruct((N_HEADS, l, HEAD_PAD), F32),
                   jax.ShapeDtypeStruct((N_HEADS, l, V_HEAD), F32)],
        compiler_params=_cparams("parallel", "arbitrary"),
    )(q, k, v, o, do, lse_t, token)


def _row_shards_mm(a, w_ref):
    a = a.astype(BF16)
    return sum(_mm(a[:, 256 * j:256 * (j + 1)], w_ref[j]) for j in range(4))


def _row_shards_mm_nt(a, w_ref):
    a = a.astype(BF16)
    return jnp.concatenate([_mm_nt(a, w_ref[j]) for j in range(4)], axis=-1)


def _merge_fwd(attn, y_ssm, gs, gm, x, grp_a, t):
    l = x.shape[0]

    def body(attn_ref, ys_ref, gs_ref, gm_ref, x_ref, wo_ref, wout_ref, ym_ref, mixed_ref, h_ref):
        y_mla = _row_shards_mm(attn_ref[...], wo_ref)
        ym_ref[...] = y_mla
        mixed = (_sigmoid(gs_ref[...]) * ys_ref[...] + _sigmoid(gm_ref[...]) * y_mla).astype(BF16)
        mixed_ref[...] = mixed
        h_ref[...] = x_ref[...] + _row_shards_mm(mixed, wout_ref)

    r = lambda: _rows(t, D_MODEL)
    return pl.pallas_call(
        body, name="merge_fwd", grid=(l // t,),
        in_specs=[r(), r(), r(), r(), r(), _member_block("w_o_mla"), _member_block("w_out")],
        out_specs=[r(), r(), r()],
        out_shape=[jax.ShapeDtypeStruct((l, D_MODEL), F32), jax.ShapeDtypeStruct((l, D_MODEL), BF16),
                   jax.ShapeDtypeStruct((l, D_MODEL), F32)],
        compiler_params=_cparams("parallel"),
    )(attn, y_ssm, gs, gm, x, grp_a, grp_a)


def _merge_bwd(dh, y_ssm, y_mla, gs, gm, grp_a, t):
    l = dh.shape[0]

    def body(dh_ref, ys_ref, ym_ref, gs_ref, gm_ref, wo_ref, wout_ref, dys_ref, dym_ref, dgs_ref, dgm_ref, dattn_ref):
        dmixed = _row_shards_mm_nt(dh_ref[...], wout_ref)
        sg = _sigmoid(gs_ref[...])
        sm = _sigmoid(gm_ref[...])
        dys_ref[...] = (dmixed * sg).astype(BF16)
        dgs_ref[...] = (dmixed * ys_ref[...] * sg * (1.0 - sg)).astype(BF16)
        dym = (dmixed * sm).astype(BF16)
        dym_ref[...] = dym
        dgm_ref[...] = (dmixed * ym_ref[...] * sm * (1.0 - sm)).astype(BF16)
        dattn_ref[...] = _row_shards_mm_nt(dym, wo_ref)

    r = lambda: _rows(t, D_MODEL)
    bf = jax.ShapeDtypeStruct((l, D_MODEL), BF16)
    return pl.pallas_call(
        body, name="merge_bwd", grid=(l // t,),
        in_specs=[r(), r(), r(), r(), r(), _member_block("w_o_mla"), _member_block("w_out")],
        out_specs=[r(), r(), r(), r(), r()],
        out_shape=[bf, bf, bf, bf, jax.ShapeDtypeStruct((l, D_MODEL), F32)],
        compiler_params=_cparams("parallel"),
    )(dh, y_ssm, y_mla, gs, gm, grp_a, grp_a)


def _mlp_fwd_bwd(h, tgt, g2, grp_a, t):
    l = h.shape[0]

    def body(h_ref, tgt_ref, g_ref, wu_ref, wd_ref, dh_ref, hn_ref, da_ref, hid_ref, dout_ref, loss_ref, dg_ref):
        first = pl.program_id(0) == 0
        hv = h_ref[...]
        g = g_ref[...]
        hn = _rms_fwd(hv, g, D_MODEL).astype(BF16)
        hn_ref[...] = hn
        out = hv
        relus = []
        for s in range(4):
            cols = slice(1024 * s, 1024 * (s + 1))
            relu = jnp.maximum(_mm(hn, wu_ref[s]), 0.0)
            relus.append(relu)
            hid = (relu * relu).astype(BF16)
            hid_ref[:, cols] = hid
            out = out + _mm(hid, wd_ref[s])
        err = out - tgt_ref[...]
        _accumulate(loss_ref, jnp.full((8, 128), jnp.sum(err * err) * (0.5 / D_MODEL), F32), first)
        dout = err * (1.0 / D_MODEL)
        doutb = dout.astype(BF16)
        dout_ref[...] = doutb
        dhn = jnp.zeros_like(hv)
        for s in range(4):
            da = (_mm_nt(doutb, wd_ref[s]) * (2.0 * relus[s])).astype(BF16)
            da_ref[:, 1024 * s:1024 * (s + 1)] = da
            dhn = dhn + _mm_nt(da, wu_ref[s])
        dx, dg_rows = _rms_bwd(hv, g, dhn, D_MODEL)
        dh_ref[...] = dout + dx
        _accumulate(dg_ref, _colsum(dg_rows), first)

    r = lambda w: _rows(t, w)
    return pl.pallas_call(
        body, name="mlp_fwd_bwd", grid=(l // t,),
        in_specs=[r(D_MODEL), r(D_MODEL), _resident((1, D_MODEL)), _member_block("w_up"), _member_block("w_down")],
        out_specs=[r(D_MODEL), r(D_MODEL), r(D_FF), r(D_FF), r(D_MODEL), pl.BlockSpec((8, 128), lambda i: (0, 0)),
                   pl.BlockSpec((1, D_MODEL), lambda i: (0, 0))],
        out_shape=[jax.ShapeDtypeStruct((l, D_MODEL), F32), jax.ShapeDtypeStruct((l, D_MODEL), BF16),
                   jax.ShapeDtypeStruct((l, D_FF), BF16), jax.ShapeDtypeStruct((l, D_FF), BF16),
                   jax.ShapeDtypeStruct((l, D_MODEL), BF16), jax.ShapeDtypeStruct((8, 128), F32),
                   jax.ShapeDtypeStruct((1, D_MODEL), F32)],
        compiler_params=_cparams("arbitrary"),
    )(h, tgt, g2, grp_a, grp_a)


def _wgrad(a, b, name):
    l, m = a.shape
    n = b.shape[1]
    bm = m if m <= 512 else 512
    bl = min(l, 2048 if n <= 1024 else 1024)

    def body(a_ref, b_ref, o_ref):
        _accumulate(o_ref, _mm_tn(a_ref[...], b_ref[...]), pl.program_id(1) == 0)

    return pl.pallas_call(
        body, name=name, grid=(m // bm, l // bl),
        in_specs=[pl.BlockSpec((bl, bm), lambda i, j: (j, i)), pl.BlockSpec((bl, n), lambda i, j: (j, 0))],
        out_specs=pl.BlockSpec((bm, n), lambda i, j: (i, 0)),
        out_shape=jax.ShapeDtypeStruct((m, n), F32),
        compiler_params=_cparams("parallel", "arbitrary"),
    )(a, b)


def _wgrad_into(a, b, member, cut, dest=None):
    group, off, rs, cs = _place_in_group(member)
    l = a.shape[0]
    bm = min(rs, 512)
    bl = min(l, 2048)
    nb = rs // bm
    if cut == "row":
        a_spec = pl.BlockSpec((bl, bm), lambda j, i, k: (k, j * nb + i))
        b_spec = pl.BlockSpec((bl, cs), lambda j, i, k: (k, 0))
    else:
        a_spec = pl.BlockSpec((bl, bm), lambda j, i, k: (k, i))
        b_spec = pl.BlockSpec((bl, cs), lambda j, i, k: (k, j))

    def body(a_ref, b_ref, *rest):
        o_ref = rest[-1]
        part = _mm_tn(a_ref[...], b_ref[...])

        @pl.when(pl.program_id(2) == 0)
        def _():
            o_ref[0] = part

        @pl.when(pl.program_id(2) != 0)
        def _():
            o_ref[0] += part

    operands, in_specs, aliases = [a, b], [a_spec, b_spec], {}
    if dest is not None:
        operands.append(dest)
        in_specs.append(ANY)
        aliases = {2: 0}
    return pl.pallas_call(
        body, name="wgrad_" + member, grid=(4, nb, l // bl), in_specs=in_specs,
        out_specs=pl.BlockSpec((1, bm, cs), lambda j, i, k: (j, off // bm + i, 0)),
        out_shape=jax.ShapeDtypeStruct((4, _group_rows(group), cs), F32), input_output_aliases=aliases,
        compiler_params=_cparams("parallel", "parallel", "arbitrary"),
    )(*operands)


def _adamw(w, g, m, v, name, g_off=0):
    r, c = w.shape
    br = r
    for cand in (256, 128, 64, 32, 16, 8):
        if r % cand == 0 and g_off % cand == 0:
            br = cand
            break

    def body(w_ref, g_ref, m_ref, v_ref, go_ref, d_ref, nm_ref, nv_ref):
        gv = g_ref[...]
        go_ref[...] = gv
        nm = ADAM_B1 * m_ref[...] + (1.0 - ADAM_B1) * gv
        nv = ADAM_B2 * v_ref[...] + (1.0 - ADAM_B2) * (gv * gv)
        m_hat = nm / (1.0 - ADAM_B1 ** ADAM_STEP)
        v_hat = nv / (1.0 - ADAM_B2 ** ADAM_STEP)
        d_ref[...] = -ADAM_LR * (m_hat / (jnp.sqrt(v_hat) + ADAM_EPS) + ADAM_WD * w_ref[...])
        nm_ref[...] = nm
        nv_ref[...] = nv

    spec = lambda: pl.BlockSpec((br, c), lambda i: (i, 0))
    g_spec = pl.BlockSpec((br, c), lambda i: (g_off // br + i, 0))
    shp = jax.ShapeDtypeStruct((r, c), F32)
    return pl.pallas_call(
        body, name=name, grid=(r // br,), in_specs=[spec(), g_spec, spec(), spec()],
        out_specs=[spec(), spec(), spec(), spec()], out_shape=[shp, shp, shp, shp], compiler_params=_cparams("parallel"),
    )(w, g, m, v)


def _place():
    return lax.axis_index("x"), lax.axis_index("y"), lax.axis_index("c")


def _other_chips(x, y):
    return [(1 - x, y), (x, 1 - y), (1 - x, 1 - y)]


ANY = pl.BlockSpec(memory_space=pl.ANY)


def _gather_weights(bufs):
    n = len(bufs)

    def body(*refs):
        outs, send_sems, recv_sems = refs[n:2 * n], refs[2 * n], refs[2 * n + 1]
        x, y, c = _place()
        chips = _other_chips(x, y)

        def part(g, px, py, pc):
            half = outs[g].shape[1] // 2
            return outs[g].at[2 * px + py, pl.ds(pl.multiple_of(pc * half, 16), half), :]

        def copy(k, src, dst, to):
            return pltpu.make_async_remote_copy(src_ref=src, dst_ref=dst, send_sem=send_sems.at[k], recv_sem=recv_sems.at[k],
                                                device_id=to, device_id_type=MESH)

        first = [copy(6 * g + j, part(g, x, y, c), part(g, x, y, c), (*chip, c)) for g in range(n) for j, chip in enumerate(chips)]
        for cp in first:
            cp.start()
        passed = []
        for g in range(n):
            for j, chip in enumerate(chips):
                landed = part(g, *chip, c)
                copy(6 * g + j, landed, landed, (x, y, c)).wait_recv()
                passed.append(copy(6 * g + 3 + j, landed, landed, (x, y, 1 - c)))
                passed[-1].start()
        for g in range(n):
            for j, chip in enumerate(chips):
                other = part(g, *chip, 1 - c)
                copy(6 * g + 3 + j, other, other, (x, y, c)).wait_recv()
        for cp in first + passed:
            cp.wait_send()

    return pl.pallas_call(
        body, name="gather_weights", in_specs=[ANY] * n, out_specs=[ANY] * n,
        out_shape=[jax.ShapeDtypeStruct(b.shape, b.dtype) for b in bufs], input_output_aliases={g: g for g in range(n)},
        scratch_shapes=[pltpu.SemaphoreType.DMA((6 * n,)), pltpu.SemaphoreType.DMA((6 * n,))],
    )(*bufs)


def _cast_shards(shards, group, place):
    width, members = GROUPS[group]
    rows = _group_rows(group)

    def body(place_ref, *refs):
        out = refs[-1]
        off = 0
        for ref, (_, r) in zip(refs[:-1], members):
            out[0, off:off + r, :] = ref[...].astype(BF16)
            off += r

    grid_spec = pltpu.PrefetchScalarGridSpec(
        num_scalar_prefetch=1, grid=(1,),
        in_specs=[pl.BlockSpec((r, width), lambda i, p: (0, 0)) for _, r in members],
        out_specs=pl.BlockSpec((1, rows, width), lambda i, p: (p[0], 0, 0)))
    return pl.pallas_call(
        body, name="cast_shards_" + group, grid_spec=grid_spec, out_shape=jax.ShapeDtypeStruct((4, rows, width), BF16),
        compiler_params=_cparams("arbitrary"),
    )(place, *[shards[name] for name, _ in members])


def _block_rows(h):
    return next(cand for cand in (256, 192, 128, 64, 32, 16) if h % cand == 0)


def _add_pair(buf, got, place, name):
    n, h, w = got.shape
    bh = _block_rows(h)
    nb = h // bh

    def body(place_ref, a_ref, b_ref, s_ref, sb_ref):
        s = a_ref[...] + b_ref[...]
        s_ref[...] = s
        sb_ref[...] = s.astype(BF16)

    spec = lambda: pl.BlockSpec((1, bh, w), lambda j, i, p: (j, i, 0))
    grid_spec = pltpu.PrefetchScalarGridSpec(
        num_scalar_prefetch=1, grid=(n, nb),
        in_specs=[pl.BlockSpec((1, bh, w), lambda j, i, p: (j, p[1] * nb + i, 0)), spec()], out_specs=[spec(), spec()])
    return pl.pallas_call(
        body, name=name, grid_spec=grid_spec,
        out_shape=[jax.ShapeDtypeStruct(got.shape, F32), jax.ShapeDtypeStruct(got.shape, BF16)],
        compiler_params=_cparams("parallel", "parallel"),
    )(place, buf, got)


def _add_received(pair, got, place, name):
    _, h, w = pair.shape
    bh = _block_rows(h)
    nb = h // bh

    def body(place_ref, own_ref, got_ref, o_ref):
        o_ref[...] = ((own_ref[0] + got_ref[0].astype(F32)) + got_ref[1].astype(F32)) + got_ref[2].astype(F32)

    grid_spec = pltpu.PrefetchScalarGridSpec(
        num_scalar_prefetch=1, grid=(nb,),
        in_specs=[pl.BlockSpec((1, bh, w), lambda i, p: (p[0], i, 0)), pl.BlockSpec((3, bh, w), lambda i, p: (0, i, 0))],
        out_specs=pl.BlockSpec((bh, w), lambda i, p: (p[1] * nb + i, 0)))
    return pl.pallas_call(
        body, name=name, grid_spec=grid_spec, out_shape=jax.ShapeDtypeStruct((2 * h, w), F32),
        compiler_params=_cparams("parallel"),
    )(place, pair, got)


def _swap_reduced_halves(bufs):
    n = len(bufs)

    def body(*refs):
        outs, send_sems, recv_sems = refs[n:2 * n], refs[2 * n], refs[2 * n + 1]
        x, y, c = _place()
        copies = []
        for g in range(n):
            half = outs[g].shape[0] // 2
            own = outs[g].at[pl.ds(pl.multiple_of(c * half, 8), half), :]
            copies.append(pltpu.make_async_remote_copy(src_ref=own, dst_ref=own, send_sem=send_sems.at[g],
                                                       recv_sem=recv_sems.at[g], device_id=(x, y, 1 - c), device_id_type=MESH))
        for cp in copies:
            cp.start()
        for g in range(n):
            half = outs[g].shape[0] // 2
            other = outs[g].at[pl.ds(pl.multiple_of((1 - c) * half, 8), half), :]
            pltpu.make_async_remote_copy(src_ref=other, dst_ref=other, send_sem=send_sems.at[g], recv_sem=recv_sems.at[g],
                                         device_id=(x, y, 1 - c), device_id_type=MESH).wait_recv()
        for cp in copies:
            cp.wait_send()

    return pl.pallas_call(
        body, name="swap_reduced_halves", in_specs=[ANY] * n, out_specs=[ANY] * n,
        out_shape=[jax.ShapeDtypeStruct(b.shape, b.dtype) for b in bufs], input_output_aliases={g: g for g in range(n)},
        scratch_shapes=[pltpu.SemaphoreType.DMA((n,)), pltpu.SemaphoreType.DMA((n,))],
    )(*bufs)


HBM = pl.BlockSpec(memory_space=pltpu.HBM)
SEM = pl.BlockSpec(memory_space=pltpu.SEMAPHORE)


def _copies_start(name, bufs, n_copies, plan, after=None):
    n = len(bufs)
    extra = [] if after is None else [after]

    def body(*refs):
        sems = refs[n + len(extra):n + len(extra) + 2 * n_copies]
        x, y, c = _place()
        for i, (src, dst, dev) in enumerate(plan(refs[:n], x, y, c)):
            pltpu.make_async_remote_copy(src_ref=src, dst_ref=dst, send_sem=sems[i], recv_sem=sems[n_copies + i],
                                         device_id=dev, device_id_type=MESH).start()
        token = refs[-1]
        token[...] = jnp.zeros_like(token)

    out = pl.pallas_call(
        body, name=name,
        out_shape=[pltpu.SemaphoreType.DMA(())] * (2 * n_copies) + [pltpu.HBM(b.shape, b.dtype) for b in bufs]
        + [jax.ShapeDtypeStruct((8, 128), F32)],
        in_specs=[HBM] * n + [ANY] * len(extra),
        out_specs=[SEM] * (2 * n_copies) + [HBM] * n + [pl.BlockSpec(memory_space=pltpu.VMEM)],
        input_output_aliases={i: 2 * n_copies + i for i in range(n)},
        compiler_params=pltpu.CompilerParams(has_side_effects=pltpu.SideEffectType.DATAFLOW_SIDE_EFFECTING),
    )(*[pltpu.with_memory_space_constraint(b, pltpu.HBM) for b in bufs], *extra)
    return list(out[:2 * n_copies]), list(out[2 * n_copies:-1]), out[-1]


def _copies_wait(name, bufs, sems, after, plan):
    n = len(bufs)
    k = len(sems) // 2

    def body(*refs):
        sem_refs = refs[n:n + 2 * k]
        x, y, c = _place()
        for i, (sent, landed, dev) in enumerate(plan(refs[:n], x, y, c)):
            cp = pltpu.make_async_remote_copy(src_ref=sent, dst_ref=landed, send_sem=sem_refs[i], recv_sem=sem_refs[k + i],
                                              device_id=dev, device_id_type=MESH)
            cp.wait_send()
            cp.wait_recv()

    return pl.pallas_call(
        body, name=name, out_shape=[pltpu.HBM(b.shape, b.dtype) for b in bufs],
        in_specs=[HBM] * n + [SEM] * (2 * k) + [ANY], out_specs=[HBM] * n, input_output_aliases={i: i for i in range(n)},
        compiler_params=pltpu.CompilerParams(has_side_effects=pltpu.SideEffectType.DATAFLOW_SIDE_EFFECTING),
    )(*bufs, *sems, after)


def _row_half(ref, which, axis):
    half = ref.shape[axis] // 2
    rows = pl.ds(pl.multiple_of(which * half, 8), half)
    return ref.at[rows, :] if axis == 0 else ref.at[:, rows, :]


class _SplitGather:
    def __init__(self, own, after):
        self.n = len(own)
        self.state = _copies_start("gather_start", own, 3 * self.n, self._sent, after)

    @staticmethod
    def _sent(refs, x, y, c):
        return [(w.at[2 * x + y], w.at[2 * x + y], (px, py, c)) for w in refs for px, py in _other_chips(x, y)]

    @staticmethod
    def _landed(refs, x, y, c):
        return [(w.at[2 * x + y], w.at[2 * px + py], (px, py, c)) for w in refs for px, py in _other_chips(x, y)]

    def token(self):
        return self.state[2]

    def wait(self, which, name, after):
        sems, bufs, _ = self.state
        k = 3 * self.n
        mine = [sems[3 * i + j] for i in which for j in range(3)] + [sems[k + 3 * i + j] for i in which for j in range(3)]
        return _copies_wait(name, [bufs[i] for i in which], mine, after, self._landed)


class _SplitReduction:
    def __init__(self, tag, groups, place):
        self.tag, self.groups, self.place = tag, groups, place

    def start_pair(self, bufs):
        n = len(bufs)
        lands = [lax.empty((4, b.shape[1] // 2, b.shape[2]), F32) for b in bufs]
        plan = lambda refs, x, y, c: [(_row_half(refs[i], 1 - c, 1), refs[n + i], (x, y, 1 - c)) for i in range(n)]
        self._pair = (_copies_start("pair_%s_start" % self.tag, bufs + lands, n, plan), plan, n)
        return self._pair[0][2]

    def pair_done_start_scatter(self, after):
        (sems, bufs, _), plan, n = self._pair
        out = _copies_wait("pair_%s_wait" % self.tag, bufs, sems, after, plan)
        pairs = [_add_pair(out[i], out[n + i], self.place, "add_pair_" + g) for i, g in enumerate(self.groups)]
        self._pair_f32 = [p[0] for p in pairs]
        lands = [lax.empty((3,) + p[1].shape[1:], BF16) for p in pairs]
        plan = lambda refs, x, y, c: [(refs[i].at[2 * px + py], refs[n + i].at[j], (px, py, c))
                                      for i in range(n) for j, (px, py) in enumerate(_other_chips(x, y))]
        self._scatter = (_copies_start("scatter_%s_start" % self.tag, [p[1] for p in pairs] + lands, 3 * n, plan), plan, n)
        return self._scatter[0][2]

    def scatter_done(self, after):
        (sems, bufs, _), plan, n = self._scatter
        out = _copies_wait("scatter_%s_wait" % self.tag, bufs, sems, after, plan)
        return [_add_received(self._pair_f32[i], out[n + i], self.place, "add_received_" + g)
                for i, g in enumerate(self.groups)]

    def start_join(self, halves):
        n = len(halves)
        sent = lambda refs, x, y, c: [(_row_half(r, c, 0), _row_half(r, c, 0), (x, y, 1 - c)) for r in refs]
        landed = lambda refs, x, y, c: [(_row_half(r, c, 0), _row_half(r, 1 - c, 0), (x, y, 1 - c)) for r in refs]
        self._join = (_copies_start("join_%s_start" % self.tag, halves, n, sent), landed)
        return self._join[0][2]

    def join_done(self, after):
        (sems, bufs, _), landed = self._join
        return _copies_wait("join_%s_wait" % self.tag, bufs, sems, after, landed)


def _all_sum_small(mine):
    rows, w = mine.shape

    def body(in_ref, out_ref, sibling, pair, chips, send_sems, recv_sems):
        x, y, c = _place()
        swap = pltpu.make_async_remote_copy(src_ref=in_ref, dst_ref=sibling, send_sem=send_sems.at[0], recv_sem=recv_sems.at[0],
                                            device_id=(x, y, 1 - c), device_id_type=MESH)
        swap.start()
        swap.wait()
        pair[...] = in_ref[...] + sibling[...]
        chip = 2 * x + y
        chips[chip] = pair[...]
        copies = [pltpu.make_async_remote_copy(src_ref=pair, dst_ref=chips.at[chip], send_sem=send_sems.at[1 + j],
                                               recv_sem=recv_sems.at[1 + j], device_id=(px, py, c), device_id_type=MESH)
                  for j, (px, py) in enumerate(_other_chips(x, y))]
        for cp in copies:
            cp.start()
        for cp in copies:
            cp.wait()
        out_ref[...] = ((chips[0] + chips[1]) + chips[2]) + chips[3]

    return pl.pallas_call(
        body, name="all_sum_small", out_shape=jax.ShapeDtypeStruct((rows, w), F32),
        in_specs=[pl.BlockSpec(memory_space=pltpu.VMEM)], out_specs=pl.BlockSpec(memory_space=pltpu.VMEM),
        scratch_shapes=[pltpu.VMEM((rows, w), F32), pltpu.VMEM((rows, w), F32), pltpu.VMEM((4, rows, w), F32),
                        pltpu.SemaphoreType.DMA((4,)), pltpu.SemaphoreType.DMA((4,))],
        compiler_params=pltpu.CompilerParams(vmem_limit_bytes=VMEM_LIMIT_V7X),
    )(mine)


def _join_column_shards(g):
    return jnp.transpose(g, (1, 0, 2)).reshape(g.shape[1], 4 * g.shape[2])


def _split_column_shards(w):
    r = w.shape[0]
    return jnp.transpose(w.reshape(r, 4, w.shape[1] // 4), (1, 0, 2))


def _small_rows(shape):
    return -(-int(np.prod(shape)) // 1024)


def _pack_small(vals):
    segs = []
    for name, shape in SMALL_WEIGHTS:
        flat = vals[name].reshape(-1)
        segs.append(jnp.pad(flat, (0, _small_rows(shape) * 1024 - flat.shape[0])))
    total = sum(s.shape[0] for s in segs) // 1024
    segs.append(jnp.zeros((-total % 8 * 1024,), F32))
    return jnp.concatenate(segs).reshape(-1, 1024)


def _unpack_small(packed):
    out, off = {}, 0
    for name, shape in SMALL_WEIGHTS:
        rows = _small_rows(shape)
        out[name] = packed[off:off + rows].reshape(-1)[:int(np.prod(shape))].reshape(shape)
        off += rows
    return out


W_IN_SHARD = D_IN // 4
W_IN_GAP = 1216


def _pad_w_in(g):
    cut = W_IN_GAP - W_IN_SHARD
    return jnp.concatenate([g[0], g[1][:, :cut], jnp.zeros((g.shape[1], D_IN_PAD - D_IN), g.dtype), g[1][:, cut:], g[2], g[3]],
                           axis=1)


def _unpad_w_in(g):
    skip = D_IN_PAD - D_IN
    second = jnp.concatenate([g[:, W_IN_SHARD:W_IN_GAP], g[:, W_IN_GAP + skip:2 * W_IN_SHARD + skip]], axis=1)
    return jnp.stack([g[:, :W_IN_SHARD], second, g[:, 2 * W_IN_SHARD + skip:3 * W_IN_SHARD + skip],
                      g[:, 3 * W_IN_SHARD + skip:]])


def _pad_heads(w):
    r = w.shape[0]
    return jnp.pad(w.reshape(r, N_HEADS, QK_HEAD), ((0, 0), (0, 0), (0, HEAD_PAD - QK_HEAD))).reshape(r, N_HEADS * HEAD_PAD)


def _unpad_heads(g):
    r = g.shape[0]
    return g.reshape(r, N_HEADS, HEAD_PAD)[:, :, :QK_HEAD].reshape(r, N_HEADS * QK_HEAD)


def _local_step(x, positions, tgt, grp_b, small, gather, red_a, red_rest):
    l = x.shape[0]
    t = min(l, 512)
    t_mlp = min(l, 256)
    tq = min(l, 512)
    tc = min(l, 256)
    row = lambda v: v.reshape(1, -1).astype(F32)

    w_in_p = _pad_w_in(grp_b)
    g1, g2 = row(small["norm_mix"]), row(small["norm_mlp"])
    gqa, gkva = row(small["q_a_norm"]), row(small["kv_a_norm"])
    gq = jnp.pad(row(small["q_norm"]), ((0, 0), (0, HEAD_PAD - QK_HEAD)))
    gk = jnp.pad(row(small["k_norm"]), ((0, 0), (0, HEAD_PAD - QK_HEAD)))
    half = QK_ROPE // 2
    inv_freq = ROPE_THETA ** (-jnp.arange(half, dtype=F32) / half)
    invf = jnp.concatenate([inv_freq, inv_freq, jnp.zeros((64,), F32)]).reshape(1, 128)
    sgn = jnp.concatenate([-jnp.ones((half,), F32), jnp.ones((half,), F32), jnp.zeros((64,), F32)]).reshape(1, 128)
    pos = positions.reshape(l, 1)

    a_re, a_im = small["ssm_a_re"], small["ssm_a_im"]
    log_dt = small["ssm_log_dt"].reshape(SSM_GROUPS, 1)
    to_gcp = lambda b: jnp.transpose(b, (0, 2, 1)).reshape(SSM_WIDTH, SSM_STATE)
    from_gcp = lambda b: jnp.transpose(b.reshape(SSM_GROUPS, SSM_GROUP_CH, SSM_STATE), (0, 2, 1))
    b_re, b_im = to_gcp(small["ssm_b_re"]), to_gcp(small["ssm_b_im"])
    c_re, c_im = small["ssm_c_re"].reshape(SSM_WIDTH, SSM_STATE), small["ssm_c_im"].reshape(SSM_WIDTH, SSM_STATE)
    wb, wc, tabs_fwd, tabs_rev = _ssm_param_fwd(a_re, a_im, log_dt, b_re, b_im, c_re, c_im)
    dskip = row(small["ssm_d"])
    b_glu = row(small["b_glu"])

    u, lat, gs, gm = _in_proj_fwd(x, g1, w_in_p, t, gather.token())
    grp_c, grp_d, grp_e = gather.wait([0, 1, 2], "gather_cde_wait", u)
    w_qb_p = _pad_heads(_join_column_shards(grp_c))
    xr, xi, y, y_ssm = _ssm_fwd(u, wb, wc, tabs_fwd, dskip, grp_d, b_glu, grp_e, tc)
    q, k, v = _mla_pre_fwd(lat, pos, invf, sgn, gqa, gkva, gq, gk, w_qb_p, grp_d, t)
    attn, lse = _attn_fwd(q, k, v, tq)
    (grp_a,) = gather.wait([3], "gather_a_wait", attn)
    y_mla, mixed, h = _merge_fwd(attn, y_ssm, gs, gm, x, grp_a, t)
    dh, hn, da, hid, dout, loss_blk, g_norm_mlp = _mlp_fwd_bwd(h, tgt, g2, grp_a, t_mlp)

    ga = _wgrad_into(hn, da, "w_up", "col", _wgrad_into(hid, dout, "w_down", "row"))
    dys, dym, dgs, dgm, dattn = _merge_bwd(dh, y_ssm, y_mla, gs, gm, grp_a, t)
    ga = _wgrad_into(attn, dym, "w_o_mla", "row", _wgrad_into(mixed, dh, "w_out", "row", ga))

    dq, dk, dv = _attn_bwd(q, k, v, attn, dattn, lse, tq, red_a.start_pair([ga]))
    d_lat, ql, dq0, ckn, dkv, g_qa, g_kva, g_q, g_k = _mla_pre_bwd(lat, pos, invf, sgn, gqa, gkva, gq, gk, w_qb_p, grp_d,
                                                                    dq, dk, dv, t, red_a.pair_done_start_scatter(dk))
    gc = _split_column_shards(_unpad_heads(_wgrad(ql, dq0, "wgrad_q_b")))

    d_u, adj, dy, z, z2, dpre, g_b_glu, g_d, g_lr, g_li = _ssm_bwd(
        dys, y, u, xr, xi, wb, wc, tabs_rev, dskip, grp_d, b_glu, grp_e, tc)
    gd = _wgrad_into(z, dpre, "w_glu", "row", _wgrad_into(ckn, dkv, "w_kv_b", "col"))
    ge = _wgrad_into(z2, dys, "w_o_ssm", "col")
    grad_x, xn, dproj, g_norm_mix = _in_proj_bwd(x, g1, w_in_p, d_u, d_lat, dgs, dgm, dh, t)
    gb = _unpad_w_in(_wgrad(xn, dproj, "wgrad_in"))

    red_a.start_join(red_a.scatter_done(gb))
    g_wb = _wgrad_strips(u, adj, adj, "wgrad_ssm_b", 1, red_rest.start_pair([gb, gc, gd, ge]))
    g_wct = _wgrad_strips(dy, xr, xi, "wgrad_ssm_c", 0, red_rest.pair_done_start_scatter(g_wb))
    g_ar, g_ai, g_ldt, g_br, g_bi, g_cr, g_ci = _ssm_param_bwd(a_re, a_im, log_dt, b_re, b_im, g_lr, g_li, g_wb, g_wct)

    g_small = {
        "norm_mix": g_norm_mix.reshape(-1), "norm_mlp": g_norm_mlp.reshape(-1), "q_a_norm": g_qa.reshape(-1),
        "kv_a_norm": g_kva.reshape(-1), "q_norm": g_q.reshape(-1)[:QK_HEAD], "k_norm": g_k.reshape(-1)[:QK_HEAD],
        "ssm_a_re": g_ar, "ssm_a_im": g_ai, "ssm_log_dt": g_ldt.reshape(-1),
        "ssm_b_re": from_gcp(g_br), "ssm_b_im": from_gcp(g_bi),
        "ssm_c_re": g_cr.reshape(SSM_GROUPS, SSM_GROUP_CH, SSM_STATE), "ssm_c_im": g_ci.reshape(SSM_GROUPS, SSM_GROUP_CH, SSM_STATE),
        "ssm_d": g_d.reshape(SSM_GROUPS, SSM_GROUP_CH), "b_glu": g_b_glu.reshape(-1),
    }
    return loss_blk[0, 0], grad_x, g_small


def kernel(x, positions, norm_mix, w_in, q_a_norm, kv_a_norm, w_q_b, w_kv_b, q_norm, k_norm, w_o_mla, ssm_a_re, ssm_a_im, ssm_log_dt, ssm_b_re, ssm_b_im, ssm_c_re, ssm_c_im, ssm_d, w_glu, b_glu, w_o_ssm, w_out, norm_mlp, w_up, w_down, loss_target, m_norm_mix, m_w_in, m_q_a_norm, m_kv_a_norm, m_w_q_b, m_w_kv_b, m_q_norm, m_k_norm, m_w_o_mla, m_ssm_a_re, m_ssm_a_im, m_ssm_log_dt, m_ssm_b_re, m_ssm_b_im, m_ssm_c_re, m_ssm_c_im, m_ssm_d, m_w_glu, m_b_glu, m_w_o_ssm, m_w_out, m_norm_mlp, m_w_up, m_w_down, v_norm_mix, v_w_in, v_q_a_norm, v_kv_a_norm, v_w_q_b, v_w_kv_b, v_q_norm, v_k_norm, v_w_o_mla, v_ssm_a_re, v_ssm_a_im, v_ssm_log_dt, v_ssm_b_re, v_ssm_b_im, v_ssm_c_re, v_ssm_c_im, v_ssm_d, v_w_glu, v_b_glu, v_w_o_ssm, v_w_out, v_norm_mlp, v_w_up, v_w_down):
    args = dict(locals())
    w = {n: args[n][0] for n in WEIGHT_ORDER}
    m = {n: args["m_" + n][0] for n in WEIGHT_ORDER}
    v = {n: args["v_" + n][0] for n in WEIGHT_ORDER}
    big_names = [n for n, *_ in BIG_WEIGHTS]
    small_names = [n for n, _ in SMALL_WEIGHTS]

    place = jnp.stack([2 * lax.axis_index("x") + lax.axis_index("y"), lax.axis_index("c")]).astype(jnp.int32)
    rest = ["b", "c", "d", "e"]

    (grp_b,) = _gather_weights([_cast_shards(w, "b", place)])
    gather = _SplitGather([_cast_shards(w, g, place) for g in ("c", "d", "e", "a")], grp_b)
    red_a = _SplitReduction("a", ["a"], place)
    red_rest = _SplitReduction("rest", rest, place)
    small = {n: w[n] for n in small_names}

    loss_local, grad_x, g_small = _local_step(x[0], positions[0], loss_target[0], grp_b, small, gather, red_a, red_rest)
    loss = lax.psum(loss_local, ("x", "y", "c"))

    grad_w, delta_w, new_m, new_v = {}, {}, {}, {}

    def update(names, reduced):
        for n in names:
            g, off, _, _ = _place_in_group(n)
            grad_w[n], delta_w[n], new_m[n], new_v[n] = _adamw(w[n], reduced[g], m[n], v[n], "adamw_" + n, off)

    small_sum = _all_sum_small(_pack_small(g_small))
    g_s, d_s, m_s, v_s = _adamw(_pack_small(small), small_sum, _pack_small({n: m[n] for n in small_names}),
                                _pack_small({n: v[n] for n in small_names}), "adamw_small")
    g_s, d_s, m_s, v_s = _unpack_small(g_s), _unpack_small(d_s), _unpack_small(m_s), _unpack_small(v_s)
    for n in small_names:
        grad_w[n], delta_w[n], new_m[n], new_v[n] = g_s[n], d_s[n], m_s[n], v_s[n]
    in_a = [n for n, _ in GROUPS["a"][1]]
    update(in_a, {"a": red_a.join_done(small_sum)[0]})
    halves = red_rest.scatter_done(new_v[in_a[-1]])
    update([n for n in big_names if n not in in_a], dict(zip(rest, _swap_reduced_halves(halves))))

    lead = lambda d: [d[n][None] for n in WEIGHT_ORDER]
    return (loss, grad_x[None], *lead(grad_w), *lead(delta_w), *lead(new_m), *lead(new_v))
```

```python
import math

import jax
import jax.numpy as jnp
import numpy as np
from jax import lax
from jax.experimental import pallas as pl
from jax.experimental.pallas import tpu as pltpu

F32 = jnp.float32
BF16 = jnp.bfloat16

D_MODEL = 1024
SSM_GROUPS = 32
SSM_GROUP_CH = 16
SSM_WIDTH = 512
SSM_STATE = 64
GP = SSM_GROUPS * SSM_STATE
N_HEADS = 8
QK_NOPE = 128
QK_ROPE = 64
QK_HEAD = 192
HEAD_PAD = 256
V_HEAD = 128
Q_LORA = 384
KV_LORA = 256
LAT_W = 768
D_IN = 3264
D_IN_PAD = 3328
D_FF = 4096
ROPE_THETA = 10000.0
EPS = 1e-6
ATT_SCALE = QK_HEAD ** -0.5

ADAM_LR = 0.001
ADAM_B1 = 0.9
ADAM_B2 = 0.999
ADAM_EPS = 1e-08
ADAM_WD = 0.01
ADAM_STEP = 10

VMEM_LIMIT_V7X = 56 * 1024 * 1024
MESH = pl.DeviceIdType.MESH

BIG_WEIGHTS = (
    ("w_in", 1024, 3264, "col"),
    ("w_q_b", 384, 1536, "col"),
    ("w_kv_b", 256, 2048, "col"),
    ("w_o_mla", 1024, 1024, "row"),
    ("w_glu", 512, 512, "row"),
    ("w_o_ssm", 512, 1024, "col"),
    ("w_out", 1024, 1024, "row"),
    ("w_up", 1024, 4096, "col"),
    ("w_down", 4096, 1024, "row"),
)
GROUPS = {
    "a": (1024, (("w_down", 1024), ("w_up", 1024), ("w_o_mla", 256), ("w_out", 256))),
    "b": (816, (("w_in", 1024),)),
    "c": (384, (("w_q_b", 384),)),
    "d": (512, (("w_kv_b", 256), ("w_glu", 128))),
    "e": (256, (("w_o_ssm", 512),)),
}


def _group_rows(group):
    return sum(r for _, r in GROUPS[group][1])


def _place_in_group(name):
    for group, (width, members) in GROUPS.items():
        off = 0
        for member, rows in members:
            if member == name:
                return group, off, rows, width
            off += rows
    raise KeyError(name)


SMALL_WEIGHTS = (
    ("norm_mix", (1024,)), ("q_a_norm", (384,)), ("kv_a_norm", (256,)), ("q_norm", (192,)), ("k_norm", (192,)),
    ("ssm_a_re", (32, 64)), ("ssm_a_im", (32, 64)), ("ssm_log_dt", (32,)),
    ("ssm_b_re", (32, 64, 16)), ("ssm_b_im", (32, 64, 16)), ("ssm_c_re", (32, 16, 64)), ("ssm_c_im", (32, 16, 64)),
    ("ssm_d", (32, 16)), ("b_glu", (512,)), ("norm_mlp", (1024,)),
)
WEIGHT_ORDER = ('norm_mix', 'w_in', 'q_a_norm', 'kv_a_norm', 'w_q_b', 'w_kv_b', 'q_norm', 'k_norm', 'w_o_mla', 'ssm_a_re',
                'ssm_a_im', 'ssm_log_dt', 'ssm_b_re', 'ssm_b_im', 'ssm_c_re', 'ssm_c_im', 'ssm_d', 'w_glu', 'b_glu',
                'w_o_ssm', 'w_out', 'norm_mlp', 'w_up', 'w_down')


def _cparams(*sem):
    return pltpu.CompilerParams(dimension_semantics=sem if sem else None, vmem_limit_bytes=VMEM_LIMIT_V7X)


def _resident(shape, index=None):
    index = (0,) * len(shape) if index is None else index
    return pl.BlockSpec(shape, lambda *_: index, pipeline_mode=pl.Buffered(1))


def _member_block(name):
    _, off, rows, width = _place_in_group(name)
    return _resident((4, rows, width), (0, off // rows, 0))


def _rows(t, width):
    return pl.BlockSpec((t, width), lambda i: (i, 0))


def _mm(a, b):
    return jnp.dot(a.astype(BF16), b.astype(BF16), preferred_element_type=F32)


def _mm_nt(a, b):
    return lax.dot_general(a.astype(BF16), b.astype(BF16), (((1,), (1,)), ((), ())), preferred_element_type=F32)


def _mm_tn(a, b):
    return lax.dot_general(a.astype(BF16), b.astype(BF16), (((0,), (0,)), ((), ())), preferred_element_type=F32)


def _rms_fwd(x, g, n):
    r = lax.rsqrt(jnp.sum(x * x, axis=-1, keepdims=True) * (1.0 / n) + EPS)
    return x * r * g


def _rms_bwd(x, g, dy, n):
    r = lax.rsqrt(jnp.sum(x * x, axis=-1, keepdims=True) * (1.0 / n) + EPS)
    xh = x * r
    dxh = dy * g
    dx = r * (dxh - xh * (jnp.sum(dxh * xh, axis=-1, keepdims=True) * (1.0 / n)))
    return dx, dy * xh


def _colsum(a):
    return jnp.sum(a, axis=0, keepdims=True)


def _accumulate(ref, value, first):
    @pl.when(first)
    def _():
        ref[...] = value

    @pl.when(jnp.logical_not(first))
    def _():
        ref[...] += value


def _sigmoid(a):
    return 1.0 / (1.0 + jnp.exp(-a))


GELU_C = math.sqrt(2.0 / math.pi)
GELU_A = 0.044715


def _gelu(y):
    return 0.5 * y * (1.0 + jnp.tanh(GELU_C * (y + GELU_A * y * y * y)))


def _gelu_grad(y):
    t = jnp.tanh(GELU_C * (y + GELU_A * y * y * y))
    return 0.5 * (1.0 + t) + 0.5 * y * (1.0 - t * t) * GELU_C * (1.0 + 3.0 * GELU_A * y * y)


def _in_proj_fwd(x, g1, w_in_p, t, token):
    l = x.shape[0]

    def body(x_ref, g_ref, w_ref, token_ref, u_ref, lat_ref, gs_ref, gm_ref):
        xn = _rms_fwd(x_ref[...], g_ref[...], D_MODEL).astype(BF16)
        u_ref[...] = _mm(xn, w_ref[:, 0:512])
        lat_ref[...] = _mm(xn, w_ref[:, 512:1280])
        gs_ref[...] = _mm(xn, w_ref[:, 1280:2304])
        gm_ref[...] = _mm(xn, w_ref[:, 2304:3328])

    return pl.pallas_call(
        body, name="in_proj_fwd", grid=(l // t,),
        in_specs=[_rows(t, D_MODEL), _resident((1, D_MODEL)), _resident((D_MODEL, D_IN_PAD)), ANY],
        out_specs=[_rows(t, 512), _rows(t, LAT_W), _rows(t, D_MODEL), _rows(t, D_MODEL)],
        out_shape=[jax.ShapeDtypeStruct((l, 512), F32), jax.ShapeDtypeStruct((l, LAT_W), F32),
                   jax.ShapeDtypeStruct((l, D_MODEL), F32), jax.ShapeDtypeStruct((l, D_MODEL), F32)],
        compiler_params=_cparams("parallel"),
    )(x, g1, w_in_p, token)


def _in_proj_bwd(x, g1, w_in_p, d_u, d_lat, d_gs, d_gm, dh, t):
    l = x.shape[0]

    def body(x_ref, g_ref, w_ref, du_ref, dlat_ref, dgs_ref, dgm_ref, dh_ref, gx_ref, xn_ref, dproj_ref, dg_ref):
        xv = x_ref[...]
        g = g_ref[...]
        xn_ref[...] = _rms_fwd(xv, g, D_MODEL).astype(BF16)
        dproj_ref[:, 0:512] = du_ref[...]
        dproj_ref[:, 512:1280] = dlat_ref[...]
        dproj_ref[:, 1280:2304] = dgs_ref[...]
        dproj_ref[:, 2304:3328] = dgm_ref[...]
        dxn = _mm_nt(dproj_ref[...], w_ref[...])
        dx, dg_rows = _rms_bwd(xv, g, dxn, D_MODEL)
        gx_ref[...] = dh_ref[...] + dx
        _accumulate(dg_ref, _colsum(dg_rows), pl.program_id(0) == 0)

    return pl.pallas_call(
        body, name="in_proj_bwd", grid=(l // t,),
        in_specs=[_rows(t, D_MODEL), _resident((1, D_MODEL)), _resident((D_MODEL, D_IN_PAD)), _rows(t, 512),
                  _rows(t, LAT_W), _rows(t, D_MODEL), _rows(t, D_MODEL), _rows(t, D_MODEL)],
        out_specs=[_rows(t, D_MODEL), _rows(t, D_MODEL), _rows(t, D_IN_PAD), pl.BlockSpec((1, D_MODEL), lambda i: (0, 0))],
        out_shape=[jax.ShapeDtypeStruct((l, D_MODEL), F32), jax.ShapeDtypeStruct((l, D_MODEL), BF16),
                   jax.ShapeDtypeStruct((l, D_IN_PAD), BF16), jax.ShapeDtypeStruct((1, D_MODEL), F32)],
        compiler_params=_cparams("arbitrary"),
    )(x, g1, w_in_p, d_u, d_lat, d_gs, d_gm, dh)


def _ssm_param_fn(a_re, a_im, log_dt, b_re, b_im):
    dt = jnp.exp(log_dt)
    er = jnp.exp(a_re * dt)
    lr = er * jnp.cos(a_im * dt)
    li = er * jnp.sin(a_im * dt)
    den = a_re * a_re + a_im * a_im
    nr = lr - 1.0
    kr = (nr * a_re + li * a_im) / den
    ki = (li * a_re - nr * a_im) / den
    rows = lambda k: jnp.broadcast_to(k[:, None, :], (SSM_GROUPS, SSM_GROUP_CH, SSM_STATE)).reshape(SSM_WIDTH, SSM_STATE)
    krt, kit = rows(kr), rows(ki)
    return lr, li, krt * b_re - kit * b_im, krt * b_im + kit * b_re


def _state_selector():
    row = lax.broadcasted_iota(jnp.int32, (SSM_STATE, GP), 0)
    col = lax.broadcasted_iota(jnp.int32, (SSM_STATE, GP), 1)
    return jnp.where(jnp.bitwise_and(col, SSM_STATE - 1) == row, 1.0, 0.0).astype(BF16)


def _own_group(rows, rows_per_group_log2):
    row = lax.broadcasted_iota(jnp.int32, (rows, GP), 0)
    col = lax.broadcasted_iota(jnp.int32, (rows, GP), 1)
    return jnp.right_shift(row, rows_per_group_log2) == jnp.right_shift(col, 6)


def _three_bf16(x):
    hi = x.astype(BF16)
    rest = x - hi.astype(F32)
    mid = rest.astype(BF16)
    return hi, mid, (rest - mid.astype(F32)).astype(BF16)


def _spread(x, sel):
    return sum(jnp.dot(part, sel, preferred_element_type=F32) for part in _three_bf16(x))


def _collect(xw, sel):
    return sum(lax.dot_general(part, sel, (((1,), (1,)), ((), ())), preferred_element_type=F32) for part in _three_bf16(xw))


def _ssm_param_fwd(a_re, a_im, log_dt, b_re, b_im, c_re, c_im):
    def body(ar_ref, ai_ref, ldt_ref, br_ref, bi_ref, cr_ref, ci_ref, wb_ref, wct_ref, tf_ref, tr_ref):
        lr, li, bbr, bbi = _ssm_param_fn(ar_ref[...], ai_ref[...], ldt_ref[...], br_ref[...], bi_ref[...])
        sel = _state_selector()
        own16 = _own_group(SSM_WIDTH, 4)
        own1 = _own_group(SSM_GROUPS, 0)
        block = lambda m: jnp.where(own16, jnp.dot(m.astype(BF16), sel, preferred_element_type=F32), 0.0).astype(BF16)
        wb_ref[:, 0:GP] = block(bbr)
        wb_ref[:, GP:2 * GP] = block(bbi)
        wct_ref[:, 0:GP] = block(cr_ref[...])
        wct_ref[:, GP:2 * GP] = block(-ci_ref[...])
        flat = lambda m: _colsum(jnp.where(own1, _spread(m, sel), 0.0))
        pr, pi = [], []
        qr, qi = lr, li
        for _ in range(8):
            pr.append(flat(qr))
            pi.append(flat(qi))
            qr, qi = qr * lr - qi * li, qr * li + qi * lr
        row = lax.broadcasted_iota(jnp.int32, (8, GP), 0)
        for n, k in enumerate((1, 2, 4)):
            tf_ref[2 * n] = jnp.where(row >= k, pr[k - 1], 0.0)
            tf_ref[2 * n + 1] = jnp.where(row >= k, pi[k - 1], 0.0)
            tr_ref[2 * n] = jnp.where(row < 8 - k, pr[k - 1], 0.0)
            tr_ref[2 * n + 1] = jnp.where(row < 8 - k, -pi[k - 1], 0.0)
        pick = lambda vals: sum(jnp.where(row == j, v, 0.0) for j, v in enumerate(vals))
        tf_ref[6] = pick(pr)
        tf_ref[7] = pick(pi)
        tr_ref[6] = pick(pr[::-1])
        tr_ref[7] = pick([-v for v in pi[::-1]])

    return pl.pallas_call(
        body, name="ssm_param_fwd",
        out_shape=[jax.ShapeDtypeStruct((SSM_WIDTH, 2 * GP), BF16), jax.ShapeDtypeStruct((SSM_WIDTH, 2 * GP), BF16),
                   jax.ShapeDtypeStruct((8, 8, GP), F32), jax.ShapeDtypeStruct((8, 8, GP), F32)],
        compiler_params=_cparams(),
    )(a_re, a_im, log_dt, b_re, b_im, c_re, c_im)


STRIP_CH = 128
STRIP_ST = 512
N_STRIPS = SSM_WIDTH // STRIP_CH


def _ssm_param_bwd(a_re, a_im, log_dt, b_re, b_im, g_lr, g_li, g_wb, g_wct, token):
    def body(ar_ref, ai_ref, ldt_ref, br_ref, bi_ref, glr_ref, gli_ref, gwb_ref, gwc_ref, token_ref,
             o_ar, o_ai, o_ldt, o_br, o_bi, o_cr, o_ci):
        sel = _state_selector()
        own1 = _own_group(SSM_GROUPS, 0)
        row = lax.broadcasted_iota(jnp.int32, (SSM_WIDTH, STRIP_ST), 0)
        col = lax.broadcasted_iota(jnp.int32, (SSM_WIDTH, STRIP_ST), 1)
        own = jnp.bitwise_and(jnp.right_shift(row, 4), 7) == jnp.right_shift(col, 6)
        blocks = lambda m: _collect(jnp.where(own, m, 0.0), sel[:, 0:STRIP_ST])
        unflat = lambda v: _collect(jnp.where(own1, v, 0.0), sel)
        _, vjp = jax.vjp(_ssm_param_fn, ar_ref[...], ai_ref[...], ldt_ref[...], br_ref[...], bi_ref[...])
        d_ar, d_ai, d_ldt, d_br, d_bi = vjp((unflat(glr_ref[...]), unflat(gli_ref[...]),
                                             blocks(gwb_ref[:, 0:STRIP_ST]), blocks(gwb_ref[:, STRIP_ST:2 * STRIP_ST])))
        o_ar[...] = d_ar
        o_ai[...] = d_ai
        o_ldt[...] = d_ldt
        o_br[...] = d_br
        o_bi[...] = d_bi
        o_cr[...] = blocks(gwc_ref[:, 0:STRIP_ST])
        o_ci[...] = -blocks(gwc_ref[:, STRIP_ST:2 * STRIP_ST])

    g, p = SSM_GROUPS, SSM_STATE
    gp = jax.ShapeDtypeStruct((g, p), F32)
    gcp = jax.ShapeDtypeStruct((SSM_WIDTH, p), F32)
    return pl.pallas_call(
        body, name="ssm_param_bwd", out_shape=[gp, gp, jax.ShapeDtypeStruct((g, 1), F32), gcp, gcp, gcp, gcp],
        in_specs=[pl.BlockSpec(memory_space=pltpu.VMEM)] * 9 + [ANY], compiler_params=_cparams(),
    )(a_re, a_im, log_dt, b_re, b_im, g_lr, g_li, g_wb, g_wct, token)


def _strip(ref, j, im):
    return ref[STRIP_CH * j:STRIP_CH * (j + 1), im * GP + STRIP_ST * j:im * GP + STRIP_ST * (j + 1)]


SCAN_STRIP = 512


def _scan_chunk(inr_ref, ini_ref, outr_ref, outi_ref, cr_ref, ci_ref, tab_ref, tc, reverse):
    n_blocks = tc // 8

    def block(j, _):
        i = (n_blocks - 1 - j) if reverse else j
        rows = pl.ds(pl.multiple_of(i * 8, 8), 8)
        for s in range(GP // SCAN_STRIP):
            sl = pl.ds(s * SCAN_STRIP, SCAN_STRIP)
            xr = inr_ref[rows, sl]
            xi = ini_ref[rows, sl]
            for n, k in enumerate((1, 2, 4)):
                shift = (8 - k) if reverse else k
                sr = pltpu.roll(xr, shift, 0)
                si = pltpu.roll(xi, shift, 0)
                mr = tab_ref[2 * n, :, sl]
                mi = tab_ref[2 * n + 1, :, sl]
                xr, xi = xr + mr * sr - mi * si, xi + mr * si + mi * sr
            qr = tab_ref[6, :, sl]
            qi = tab_ref[7, :, sl]
            cr = cr_ref[:, sl]
            ci = ci_ref[:, sl]
            xr, xi = xr + qr * cr - qi * ci, xi + qr * ci + qi * cr
            outr_ref[rows, sl] = xr
            outi_ref[rows, sl] = xi
            edge = 0 if reverse else 7
            cr_ref[:, sl] = jnp.broadcast_to(xr[edge:edge + 1, :], (8, SCAN_STRIP))
            ci_ref[:, sl] = jnp.broadcast_to(xi[edge:edge + 1, :], (8, SCAN_STRIP))
        return 0

    lax.fori_loop(0, n_blocks, block, 0)


def _glu_pre(z, wg_ref):
    return sum(_mm(z[:, 128 * j:128 * (j + 1)], wg_ref[j]) for j in range(4))


def _ssm_fwd(u, wb, wc, tabs, dskip, grp_d, b_glu, grp_e, tc):
    l = u.shape[0]

    def body(u_ref, wb_ref, wc_ref, tab_ref, d_ref, wg_ref, bg_ref, wo_ref, xr_ref, xi_ref, y_ref, ys_ref,
             bur, bui, cr, ci):
        @pl.when(pl.program_id(0) == 0)
        def _():
            cr[...] = jnp.zeros_like(cr)
            ci[...] = jnp.zeros_like(ci)

        uv = u_ref[...]
        ub = uv.astype(BF16)
        for j in range(N_STRIPS):
            uj = ub[:, STRIP_CH * j:STRIP_CH * (j + 1)]
            states = slice(STRIP_ST * j, STRIP_ST * (j + 1))
            bur[:, states] = _mm(uj, _strip(wb_ref, j, 0))
            bui[:, states] = _mm(uj, _strip(wb_ref, j, 1))
        _scan_chunk(bur, bui, xr_ref, xi_ref, cr, ci, tab_ref, tc, False)
        y = jnp.concatenate(
            [_mm_nt(xr_ref[:, STRIP_ST * j:STRIP_ST * (j + 1)], _strip(wc_ref, j, 0))
             + _mm_nt(xi_ref[:, STRIP_ST * j:STRIP_ST * (j + 1)], _strip(wc_ref, j, 1)) for j in range(N_STRIPS)],
            axis=-1) + d_ref[...] * uv
        y_ref[...] = y
        z = _gelu(y)
        z2 = z * _sigmoid(_glu_pre(z, wg_ref) + bg_ref[...])
        for s in range(4):
            ys_ref[:, 256 * s:256 * (s + 1)] = _mm(z2, wo_ref[s])

    return pl.pallas_call(
        body, name="ssm_fwd", grid=(l // tc,),
        in_specs=[_rows(tc, 512), _resident((512, 2 * GP)), _resident((512, 2 * GP)), _resident((8, 8, GP)),
                  _resident((1, 512)), _member_block("w_glu"), _resident((1, 512)), _member_block("w_o_ssm")],
        out_specs=[_rows(tc, GP), _rows(tc, GP), _rows(tc, 512), _rows(tc, D_MODEL)],
        out_shape=[jax.ShapeDtypeStruct((l, GP), F32), jax.ShapeDtypeStruct((l, GP), F32),
                   jax.ShapeDtypeStruct((l, 512), F32), jax.ShapeDtypeStruct((l, D_MODEL), F32)],
        scratch_shapes=[pltpu.VMEM((tc, GP), F32), pltpu.VMEM((tc, GP), F32), pltpu.VMEM((8, GP), F32),
                        pltpu.VMEM((8, GP), F32)],
        compiler_params=_cparams("arbitrary"),
    )(u, wb, wc, tabs, dskip, grp_d, b_glu, grp_e)


def _ssm_bwd(dys, y, u, xr, xi, wb, wc, tabs_rev, dskip, grp_d, b_glu, grp_e, tc):
    l = u.shape[0]
    nc = l // tc

    def body(dys_ref, y_ref, u_ref, xr_ref, xi_ref, wb_ref, wc_ref, tab_ref, d_ref, wg_ref, bg_ref, wo_ref,
             du_ref, z_ref, z2_ref, dpre_ref, gb_ref, gd_ref, glr_ref, gli_ref, gwb_ref, gwc_ref,
             dxr, dxi, ar, ai, cr, ci, a_ref):
        first = pl.program_id(0) == 0

        @pl.when(first)
        def _():
            cr[...] = jnp.zeros_like(cr)
            ci[...] = jnp.zeros_like(ci)

        yv = y_ref[...]
        uv = u_ref[...]
        dz2 = sum(_mm_nt(dys_ref[:, 256 * j:256 * (j + 1)], wo_ref[j]) for j in range(4))
        z = _gelu(yv)
        s = _sigmoid(_glu_pre(z, wg_ref) + bg_ref[...])
        dpre = dz2 * z * s * (1.0 - s)
        dpreb = dpre.astype(BF16)
        dz = dz2 * s + jnp.concatenate([_mm_nt(dpreb, wg_ref[j]) for j in range(4)], axis=-1)
        dy = dz * _gelu_grad(yv)
        z_ref[...] = z.astype(BF16)
        z2_ref[...] = (z * s).astype(BF16)
        dpre_ref[...] = dpre.astype(BF16)
        _accumulate(gb_ref, _colsum(dpre), first)
        _accumulate(gd_ref, _colsum(dy * uv), first)

        dyb = dy.astype(BF16)
        for j in range(N_STRIPS):
            dyj = dyb[:, STRIP_CH * j:STRIP_CH * (j + 1)]
            dxr[:, STRIP_ST * j:STRIP_ST * (j + 1)] = _mm(dyj, _strip(wc_ref, j, 0))
            dxi[:, STRIP_ST * j:STRIP_ST * (j + 1)] = _mm(dyj, _strip(wc_ref, j, 1))
        ar[pl.ds(tc, 8), :] = cr[...]
        ai[pl.ds(tc, 8), :] = ci[...]
        _scan_chunk(dxr, dxi, ar, ai, cr, ci, tab_ref, tc, True)
        a_ref[:, 0:GP] = ar[pl.ds(0, tc), :].astype(BF16)
        a_ref[:, GP:2 * GP] = ai[pl.ds(0, tc), :].astype(BF16)
        du_states = jnp.concatenate(
            [_mm_nt(a_ref[:, STRIP_ST * j:STRIP_ST * (j + 1)], _strip(wb_ref, j, 0))
             + _mm_nt(a_ref[:, GP + STRIP_ST * j:GP + STRIP_ST * (j + 1)], _strip(wb_ref, j, 1)) for j in range(N_STRIPS)],
            axis=-1)
        du_ref[...] = (dy * d_ref[...] + du_states).astype(BF16)
        anr = ar[pl.ds(1, tc), :]
        ani = ai[pl.ds(1, tc), :]
        xrv = xr_ref[...]
        xiv = xi_ref[...]
        _accumulate(glr_ref, _colsum(anr * xrv + ani * xiv), first)
        _accumulate(gli_ref, _colsum(ani * xrv - anr * xiv), first)
        for j in range(N_STRIPS):
            ch = slice(STRIP_CH * j, STRIP_CH * (j + 1))
            st = slice(STRIP_ST * j, STRIP_ST * (j + 1))
            uj = uv[:, ch].astype(BF16)
            _accumulate(gwb_ref.at[ch, 0:STRIP_ST], _mm_tn(uj, a_ref[:, st]), first)
            _accumulate(gwb_ref.at[ch, STRIP_ST:2 * STRIP_ST], _mm_tn(uj, a_ref[:, GP + STRIP_ST * j:GP + STRIP_ST * (j + 1)]), first)
            _accumulate(gwc_ref.at[ch, 0:STRIP_ST], _mm_tn(dyb[:, ch], xr_ref[:, st]), first)
            _accumulate(gwc_ref.at[ch, STRIP_ST:2 * STRIP_ST], _mm_tn(dyb[:, ch], xi_ref[:, st]), first)

    rev = lambda w: pl.BlockSpec((tc, w), lambda i: (nc - 1 - i, 0))
    acc = lambda w, r=1: pl.BlockSpec((r, w), lambda i: (0, 0))
    bf = jax.ShapeDtypeStruct((l, 512), BF16)
    strips = jax.ShapeDtypeStruct((SSM_WIDTH, 2 * STRIP_ST), F32)
    return pl.pallas_call(
        body, name="ssm_bwd", grid=(nc,),
        in_specs=[rev(D_MODEL), rev(512), rev(512), rev(GP), rev(GP), _resident((512, 2 * GP)), _resident((512, 2 * GP)),
                  _resident((8, 8, GP)), _resident((1, 512)), _member_block("w_glu"), _resident((1, 512)),
                  _member_block("w_o_ssm")],
        out_specs=[rev(512), rev(512), rev(512), rev(512), acc(512), acc(512), acc(GP), acc(GP),
                   acc(2 * STRIP_ST, SSM_WIDTH), acc(2 * STRIP_ST, SSM_WIDTH)],
        out_shape=[bf, bf, bf, bf, jax.ShapeDtypeStruct((1, 512), F32), jax.ShapeDtypeStruct((1, 512), F32),
                   jax.ShapeDtypeStruct((1, GP), F32), jax.ShapeDtypeStruct((1, GP), F32), strips, strips],
        scratch_shapes=[pltpu.VMEM((tc, GP), F32), pltpu.VMEM((tc, GP), F32), pltpu.VMEM((tc + 8, GP), F32),
                        pltpu.VMEM((tc + 8, GP), F32), pltpu.VMEM((8, GP), F32), pltpu.VMEM((8, GP), F32),
                        pltpu.VMEM((tc, 2 * GP), BF16)],
        compiler_params=_cparams("arbitrary"),
    )(dys, y, u, xr, xi, wb, wc, tabs_rev, dskip, grp_d, b_glu, grp_e)


def _swap_halves(b):
    lane = lax.broadcasted_iota(jnp.int32, b.shape, 1)
    return jnp.where(lane < 32, pltpu.roll(b, 96, 1), pltpu.roll(b, 32, 1))


def _rope_tables(pos_ref, invf_ref, sgn_ref):
    ang = pos_ref[...].astype(F32) * invf_ref[...]
    return jnp.cos(ang), jnp.sin(ang) * sgn_ref[...]


def _mla_pre_fwd(lat, pos, invf, sgn, gqa, gkva, gq, gk, w_qb_p, w_kvb, t):
    l = lat.shape[0]

    def body(lat_ref, pos_ref, invf_ref, sgn_ref, gqa_ref, gkva_ref, gq_ref, gk_ref, wq_ref, wkv_ref, q_ref, k_ref, v_ref):
        cs, sn = _rope_tables(pos_ref, invf_ref, sgn_ref)
        ql = _rms_fwd(lat_ref[:, 0:Q_LORA], gqa_ref[...], Q_LORA)
        ckn = _rms_fwd(lat_ref[:, Q_LORA:Q_LORA + KV_LORA], gkva_ref[...], KV_LORA)
        kpe = lat_ref[:, 640:768]
        q0 = _mm(ql, wq_ref[...])
        cknb = ckn.astype(BF16)
        kv = jnp.concatenate([_mm(cknb, wkv_ref[s]) for s in range(4)], axis=-1)
        for h in range(N_HEADS):
            q1 = _rms_fwd(q0[:, HEAD_PAD * h:HEAD_PAD * (h + 1)], gq_ref[...], QK_HEAD)
            b = q1[:, 128:256]
            q_ref[h, :, 0:128] = (q1[:, 0:128] * ATT_SCALE).astype(BF16)
            q_ref[h, :, 128:256] = ((b * cs + _swap_halves(b) * sn) * ATT_SCALE).astype(BF16)
            k0 = jnp.concatenate([kv[:, 256 * h:256 * h + 128], kpe], axis=-1)
            k1 = _rms_fwd(k0, gk_ref[...], QK_HEAD)
            b = k1[:, 128:256]
            k_ref[h, :, 0:128] = k1[:, 0:128].astype(BF16)
            k_ref[h, :, 128:256] = (b * cs + _swap_halves(b) * sn).astype(BF16)
            v_ref[h] = kv[:, 256 * h + 128:256 * h + 256].astype(BF16)

    heads = lambda w: pl.BlockSpec((N_HEADS, t, w), lambda i: (0, i, 0))
    return pl.pallas_call(
        body, name="mla_pre_fwd", grid=(l // t,),
        in_specs=[_rows(t, LAT_W), _rows(t, 1), _resident((1, 128)), _resident((1, 128)), _resident((1, Q_LORA)),
                  _resident((1, KV_LORA)), _resident((1, HEAD_PAD)), _resident((1, HEAD_PAD)),
                  _resident((Q_LORA, N_HEADS * HEAD_PAD)), _member_block("w_kv_b")],
        out_specs=[heads(HEAD_PAD), heads(HEAD_PAD), heads(V_HEAD)],
        out_shape=[jax.ShapeDtypeStruct((N_HEADS, l, HEAD_PAD), BF16), jax.ShapeDtypeStruct((N_HEADS, l, HEAD_PAD), BF16),
                   jax.ShapeDtypeStruct((N_HEADS, l, V_HEAD), BF16)],
        compiler_params=_cparams("parallel"),
    )(lat, pos, invf, sgn, gqa, gkva, gq, gk, w_qb_p, w_kvb)


def _mla_pre_bwd(lat, pos, invf, sgn, gqa, gkva, gq, gk, w_qb_p, w_kvb, dq, dk, dv, t, token):
    l = lat.shape[0]

    def body(lat_ref, pos_ref, invf_ref, sgn_ref, gqa_ref, gkva_ref, gq_ref, gk_ref, wq_ref, wkv_ref, dq_ref, dk_ref, dv_ref,
             token_ref, dlat_ref, ql_ref, dq0_ref, ckn_ref, dkv_ref, ggqa_ref, ggkva_ref, ggq_ref, ggk_ref):
        first = pl.program_id(0) == 0
        cs, sn = _rope_tables(pos_ref, invf_ref, sgn_ref)
        q_lat = lat_ref[:, 0:Q_LORA]
        c_kv = lat_ref[:, Q_LORA:Q_LORA + KV_LORA]
        kpe = lat_ref[:, 640:768]
        ql = _rms_fwd(q_lat, gqa_ref[...], Q_LORA)
        ckn = _rms_fwd(c_kv, gkva_ref[...], KV_LORA)
        ql_ref[...] = ql.astype(BF16)
        ckn_ref[...] = ckn.astype(BF16)
        q0 = _mm(ql, wq_ref[...])
        cknb = ckn.astype(BF16)
        kv = jnp.concatenate([_mm(cknb, wkv_ref[s]) for s in range(4)], axis=-1)
        dkpe = jnp.zeros_like(kpe)
        ggq = jnp.zeros((1, HEAD_PAD), F32)
        ggk = jnp.zeros((1, HEAD_PAD), F32)

        def unrope(d):
            b = d[:, 128:256]
            return jnp.concatenate([d[:, 0:128], b * cs + _swap_halves(b * sn)], axis=-1)

        for h in range(N_HEADS):
            dq1 = unrope(dq_ref[h].astype(F32) * ATT_SCALE)
            dq0h, gq_rows = _rms_bwd(q0[:, HEAD_PAD * h:HEAD_PAD * (h + 1)], gq_ref[...], dq1, QK_HEAD)
            ggq = ggq + _colsum(gq_rows)
            dq0_ref[:, HEAD_PAD * h:HEAD_PAD * (h + 1)] = dq0h.astype(BF16)
            k0 = jnp.concatenate([kv[:, 256 * h:256 * h + 128], kpe], axis=-1)
            dk0, gk_rows = _rms_bwd(k0, gk_ref[...], unrope(dk_ref[h].astype(F32)), QK_HEAD)
            ggk = ggk + _colsum(gk_rows)
            dkpe = dkpe + dk0[:, 128:256]
            dkv_ref[:, 256 * h:256 * h + 128] = dk0[:, 0:128].astype(BF16)
            dkv_ref[:, 256 * h + 128:256 * h + 256] = dv_ref[h].astype(BF16)
        dql = _mm_nt(dq0_ref[...], wq_ref[...])
        dckn = sum(_mm_nt(dkv_ref[:, 512 * s:512 * (s + 1)], wkv_ref[s]) for s in range(4))
        dq_lat, gqa_rows = _rms_bwd(q_lat, gqa_ref[...], dql, Q_LORA)
        dc_kv, gkva_rows = _rms_bwd(c_kv, gkva_ref[...], dckn, KV_LORA)
        dlat_ref[:, 0:Q_LORA] = dq_lat.astype(BF16)
        dlat_ref[:, Q_LORA:Q_LORA + KV_LORA] = dc_kv.astype(BF16)
        dlat_ref[:, 640:768] = dkpe.astype(BF16)
        _accumulate(ggqa_ref, _colsum(gqa_rows), first)
        _accumulate(ggkva_ref, _colsum(gkva_rows), first)
        _accumulate(ggq_ref, ggq, first)
        _accumulate(ggk_ref, ggk, first)

    heads = lambda w: pl.BlockSpec((N_HEADS, t, w), lambda i: (0, i, 0))
    acc = lambda w: pl.BlockSpec((1, w), lambda i: (0, 0))
    return pl.pallas_call(
        body, name="mla_pre_bwd", grid=(l // t,),
        in_specs=[_rows(t, LAT_W), _rows(t, 1), _resident((1, 128)), _resident((1, 128)), _resident((1, Q_LORA)),
                  _resident((1, KV_LORA)), _resident((1, HEAD_PAD)), _resident((1, HEAD_PAD)),
                  _resident((Q_LORA, N_HEADS * HEAD_PAD)), _member_block("w_kv_b"),
                  heads(HEAD_PAD), heads(HEAD_PAD), heads(V_HEAD), ANY],
        out_specs=[_rows(t, LAT_W), _rows(t, Q_LORA), _rows(t, N_HEADS * HEAD_PAD), _rows(t, KV_LORA), _rows(t, N_HEADS * 256),
                   acc(Q_LORA), acc(KV_LORA), acc(HEAD_PAD), acc(HEAD_PAD)],
        out_shape=[jax.ShapeDtypeStruct((l, LAT_W), BF16), jax.ShapeDtypeStruct((l, Q_LORA), BF16),
                   jax.ShapeDtypeStruct((l, N_HEADS * HEAD_PAD), BF16), jax.ShapeDtypeStruct((l, KV_LORA), BF16),
                   jax.ShapeDtypeStruct((l, N_HEADS * 256), BF16), jax.ShapeDtypeStruct((1, Q_LORA), F32),
                   jax.ShapeDtypeStruct((1, KV_LORA), F32), jax.ShapeDtypeStruct((1, HEAD_PAD), F32),
                   jax.ShapeDtypeStruct((1, HEAD_PAD), F32)],
        compiler_params=_cparams("arbitrary"),
    )(lat, pos, invf, sgn, gqa, gkva, gq, gk, w_qb_p, w_kvb, dq, dk, dv, token)


def _causal(s, transposed):
    row = lax.broadcasted_iota(jnp.int32, s.shape, 0)
    col = lax.broadcasted_iota(jnp.int32, s.shape, 1)
    keep = (row <= col) if transposed else (col <= row)
    return jnp.where(keep, s, -jnp.inf)


def _as_row(col):
    n = col.shape[0]
    row = lax.broadcasted_iota(jnp.int32, (n, n), 0)
    lane = lax.broadcasted_iota(jnp.int32, (n, n), 1)
    return jnp.sum(jnp.where(row == lane, col, 0.0), axis=0, keepdims=True)


def _attn_fwd(q, k, v, tq):
    l = q.shape[1]

    hb = 4

    def body(q_ref, k_ref, v_ref, o_ref, lse_ref):
        qi = pl.program_id(1)
        qs = [q_ref[a] for a in range(hb)]

        def step(kb, carry, masked):
            rows = pl.ds(pl.multiple_of(kb * tq, tq), tq)
            out = []
            for a, (m, den, acc) in enumerate(carry):
                s = _mm_nt(qs[a], k_ref[a, rows, :])
                if masked:
                    s = _causal(s, False)
                m_new = jnp.maximum(m, jnp.max(s, axis=-1, keepdims=True))
                alpha = jnp.exp(m - m_new)
                p = jnp.exp(s - m_new)
                den = alpha * den + jnp.sum(p, axis=-1, keepdims=True)
                acc = alpha * acc + _mm(p, v_ref[a, rows, :])
                out.append((m_new, den, acc))
            return tuple(out)

        init = tuple((jnp.full((tq, 1), -jnp.inf, F32), jnp.zeros((tq, 1), F32), jnp.zeros((tq, V_HEAD), F32))
                     for _ in range(hb))
        carry = lax.fori_loop(0, qi, lambda kb, c: step(kb, c, False), init)
        for a, (m, den, acc) in enumerate(step(qi, carry, True)):
            o_ref[:, V_HEAD * a:V_HEAD * (a + 1)] = acc / den
            lse_ref[a, 0] = _as_row(m + jnp.log(den))

    return pl.pallas_call(
        body, name="attn_fwd", grid=(N_HEADS // hb, l // tq),
        in_specs=[pl.BlockSpec((hb, tq, HEAD_PAD), lambda h, i: (h, i, 0)), pl.BlockSpec((hb, l, HEAD_PAD), lambda h, i: (h, 0, 0)),
                  pl.BlockSpec((hb, l, V_HEAD), lambda h, i: (h, 0, 0))],
        out_specs=[pl.BlockSpec((tq, hb * V_HEAD), lambda h, i: (i, h)), pl.BlockSpec((hb, 1, 1, tq), lambda h, i: (h, i, 0, 0))],
        out_shape=[jax.ShapeDtypeStruct((l, N_HEADS * V_HEAD), F32), jax.ShapeDtypeStruct((N_HEADS, l // tq, 1, tq), F32)],
        compiler_params=_cparams("parallel", "arbitrary"),
    )(q, k, v)


def _attn_bwd(q, k, v, o, do, lse_t, tq, token):
    l = q.shape[1]
    nq = l // tq

    hb = 2

    def body(q_ref, k_ref, v_ref, o_ref, do_ref, lse_ref, token_ref, dq_ref, dk_ref, dv_ref, dq_acc):
        ki = pl.program_id(1)

        @pl.when(ki == 0)
        def _():
            dq_acc[...] = jnp.zeros_like(dq_acc)

        kblks = [k_ref[a] for a in range(hb)]
        vblks = [v_ref[a] for a in range(hb)]
        ones = jnp.ones((8, V_HEAD), BF16)

        def step(qb, carry, masked):
            rows = pl.ds(pl.multiple_of(qb * tq, tq), tq)
            out = []
            for a, (dk, dv) in enumerate(carry):
                cols = slice(V_HEAD * a, V_HEAD * (a + 1))
                qblk = q_ref[a, rows, :]
                dov = do_ref[rows, cols]
                dob = dov.astype(BF16)
                delta = sum(_mm_nt(ones, part) for part in _three_bf16(dov * o_ref[rows, cols]))[0:1, :]
                st = _mm_nt(kblks[a], qblk)
                if masked:
                    st = _causal(st, True)
                pt = jnp.exp(st - lse_ref[a, qb])
                dv = dv + _mm(pt, dob)
                dst = (pt * (_mm_nt(vblks[a], dob) - delta)).astype(BF16)
                dk = dk + _mm(dst, qblk)
                dq_acc[a, rows, :] += _mm_tn(dst, kblks[a])
                out.append((dk, dv))
            return tuple(out)

        init = tuple((jnp.zeros((tq, HEAD_PAD), F32), jnp.zeros((tq, V_HEAD), F32)) for _ in range(hb))
        carry = lax.fori_loop(ki + 1, nq, lambda qb, c: step(qb, c, False), step(ki, init, True))
        for a, (dk, dv) in enumerate(carry):
            dk_ref[a] = dk.astype(BF16)
            dv_ref[a] = dv.astype(BF16)

        @pl.when(ki == nq - 1)
        def _():
            dq_ref[...] = dq_acc[...].astype(BF16)

    return pl.pallas_call(
        body, name="attn_bwd", grid=(N_HEADS // hb, nq),
        in_specs=[pl.BlockSpec((hb, l, HEAD_PAD), lambda h, i: (h, 0, 0)), pl.BlockSpec((hb, tq, HEAD_PAD), lambda h, i: (h, i, 0)),
                  pl.BlockSpec((hb, tq, V_HEAD), lambda h, i: (h, i, 0)), pl.BlockSpec((l, hb * V_HEAD), lambda h, i: (0, h)),
                  pl.BlockSpec((l, hb * V_HEAD), lambda h, i: (0, h)), pl.BlockSpec((hb, nq, 1, tq), lambda h, i: (h, 0, 0, 0)), ANY],
        out_specs=[pl.BlockSpec((hb, l, HEAD_PAD), lambda h, i: (h, 0, 0)), pl.BlockSpec((hb, tq, HEAD_PAD), lambda h, i: (h, i, 0)),
                   pl.BlockSpec((hb, tq, V_HEAD), lambda h, i: (h, i, 0))],
        out_shape=[jax.ShapeDtypeStruct((N_HEADS, l, HEAD_PAD), BF16), jax.ShapeDtypeStruct((N_HEADS, l, HEAD_PAD), BF16),
                   jax.ShapeDtypeStruct((N_HEADS, l, V_HEAD), BF16)],
        scratch_shapes=[pltpu.VMEM((hb, l, HEAD_PAD), F32)],
        compiler_params=_cparams("parallel", "arbitrary"),
    )(q, k, v, o, do, lse_t, token)


def _row_shards_mm(a, w_ref):
    a = a.astype(BF16)
    return sum(_mm(a[:, 256 * j:256 * (j + 1)], w_ref[j]) for j in range(4))


def _row_shards_mm_nt(a, w_ref):
    a = a.astype(BF16)
    return jnp.concatenate([_mm_nt(a, w_ref[j]) for j in range(4)], axis=-1)


def _merge_fwd(attn, y_ssm, gs, gm, x, grp_a, t):
    l = x.shape[0]

    def body(attn_ref, ys_ref, gs_ref, gm_ref, x_ref, wo_ref, wout_ref, ym_ref, mixed_ref, h_ref):
        y_mla = _row_shards_mm(attn_ref[...], wo_ref)
        ym_ref[...] = y_mla
        mixed = (_sigmoid(gs_ref[...]) * ys_ref[...] + _sigmoid(gm_ref[...]) * y_mla).astype(BF16)
        mixed_ref[...] = mixed
        h_ref[...] = x_ref[...] + _row_shards_mm(mixed, wout_ref)

    r = lambda: _rows(t, D_MODEL)
    return pl.pallas_call(
        body, name="merge_fwd", grid=(l // t,),
        in_specs=[r(), r(), r(), r(), r(), _member_block("w_o_mla"), _member_block("w_out")],
        out_specs=[r(), r(), r()],
        out_shape=[jax.ShapeDtypeStruct((l, D_MODEL), F32), jax.ShapeDtypeStruct((l, D_MODEL), BF16),
                   jax.ShapeDtypeStruct((l, D_MODEL), F32)],
        compiler_params=_cparams("parallel"),
    )(attn, y_ssm, gs, gm, x, grp_a, grp_a)


def _merge_bwd(dh, y_ssm, y_mla, gs, gm, grp_a, t):
    l = dh.shape[0]

    def body(dh_ref, ys_ref, ym_ref, gs_ref, gm_ref, wo_ref, wout_ref, dys_ref, dym_ref, dgs_ref, dgm_ref, dattn_ref):
        dmixed = _row_shards_mm_nt(dh_ref[...], wout_ref)
        sg = _sigmoid(gs_ref[...])
        sm = _sigmoid(gm_ref[...])
        dys_ref[...] = (dmixed * sg).astype(BF16)
        dgs_ref[...] = (dmixed * ys_ref[...] * sg * (1.0 - sg)).astype(BF16)
        dym = (dmixed * sm).astype(BF16)
        dym_ref[...] = dym
        dgm_ref[...] = (dmixed * ym_ref[...] * sm * (1.0 - sm)).astype(BF16)
        dattn_ref[...] = _row_shards_mm_nt(dym, wo_ref)

    r = lambda: _rows(t, D_MODEL)
    bf = jax.ShapeDtypeStruct((l, D_MODEL), BF16)
    return pl.pallas_call(
        body, name="merge_bwd", grid=(l // t,),
        in_specs=[r(), r(), r(), r(), r(), _member_block("w_o_mla"), _member_block("w_out")],
        out_specs=[r(), r(), r(), r(), r()],
        out_shape=[bf, bf, bf, bf, jax.ShapeDtypeStruct((l, D_MODEL), F32)],
        compiler_params=_cparams("parallel"),
    )(dh, y_ssm, y_mla, gs, gm, grp_a, grp_a)


def _mlp_fwd_bwd(h, tgt, g2, grp_a, t):
    l = h.shape[0]

    def body(h_ref, tgt_ref, g_ref, wu_ref, wd_ref, dh_ref, hn_ref, da_ref, hid_ref, dout_ref, loss_ref, dg_ref):
        first = pl.program_id(0) == 0
        hv = h_ref[...]
        g = g_ref[...]
        hn = _rms_fwd(hv, g, D_MODEL).astype(BF16)
        hn_ref[...] = hn
        out = hv
        relus = []
        for s in range(4):
            cols = slice(1024 * s, 1024 * (s + 1))
            relu = jnp.maximum(_mm(hn, wu_ref[s]), 0.0)
            relus.append(relu)
            hid = (relu * relu).astype(BF16)
            hid_ref[:, cols] = hid
            out = out + _mm(hid, wd_ref[s])
        err = out - tgt_ref[...]
        _accumulate(loss_ref, jnp.full((8, 128), jnp.sum(err * err) * (0.5 / D_MODEL), F32), first)
        dout = err * (1.0 / D_MODEL)
        doutb = dout.astype(BF16)
        dout_ref[...] = doutb
        dhn = jnp.zeros_like(hv)
        for s in range(4):
            da = (_mm_nt(doutb, wd_ref[s]) * (2.0 * relus[s])).astype(BF16)
            da_ref[:, 1024 * s:1024 * (s + 1)] = da
            dhn = dhn + _mm_nt(da, wu_ref[s])
        dx, dg_rows = _rms_bwd(hv, g, dhn, D_MODEL)
        dh_ref[...] = dout + dx
        _accumulate(dg_ref, _colsum(dg_rows), first)

    r = lambda w: _rows(t, w)
    return pl.pallas_call(
        body, name="mlp_fwd_bwd", grid=(l // t,),
        in_specs=[r(D_MODEL), r(D_MODEL), _resident((1, D_MODEL)), _member_block("w_up"), _member_block("w_down")],
        out_specs=[r(D_MODEL), r(D_MODEL), r(D_FF), r(D_FF), r(D_MODEL), pl.BlockSpec((8, 128), lambda i: (0, 0)),
                   pl.BlockSpec((1, D_MODEL), lambda i: (0, 0))],
        out_shape=[jax.ShapeDtypeStruct((l, D_MODEL), F32), jax.ShapeDtypeStruct((l, D_MODEL), BF16),
                   jax.ShapeDtypeStruct((l, D_FF), BF16), jax.ShapeDtypeStruct((l, D_FF), BF16),
                   jax.ShapeDtypeStruct((l, D_MODEL), BF16), jax.ShapeDtypeStruct((8, 128), F32),
                   jax.ShapeDtypeStruct((1, D_MODEL), F32)],
        compiler_params=_cparams("arbitrary"),
    )(h, tgt, g2, grp_a, grp_a)


def _wgrad(a, b, name):
    l, m = a.shape
    n = b.shape[1]
    bm = m if m <= 512 else 512
    bl = min(l, 2048 if n <= 1024 else 1024)

    def body(a_ref, b_ref, o_ref):
        _accumulate(o_ref, _mm_tn(a_ref[...], b_ref[...]), pl.program_id(1) == 0)

    return pl.pallas_call(
        body, name=name, grid=(m // bm, l // bl),
        in_specs=[pl.BlockSpec((bl, bm), lambda i, j: (j, i)), pl.BlockSpec((bl, n), lambda i, j: (j, 0))],
        out_specs=pl.BlockSpec((bm, n), lambda i, j: (i, 0)),
        out_shape=jax.ShapeDtypeStruct((m, n), F32),
        compiler_params=_cparams("parallel", "arbitrary"),
    )(a, b)


def _wgrad_into(a, b, member, cut, dest=None):
    group, off, rs, cs = _place_in_group(member)
    l = a.shape[0]
    bm = min(rs, 512)
    bl = min(l, 2048)
    nb = rs // bm
    if cut == "row":
        a_spec = pl.BlockSpec((bl, bm), lambda j, i, k: (k, j * nb + i))
        b_spec = pl.BlockSpec((bl, cs), lambda j, i, k: (k, 0))
    else:
        a_spec = pl.BlockSpec((bl, bm), lambda j, i, k: (k, i))
        b_spec = pl.BlockSpec((bl, cs), lambda j, i, k: (k, j))

    def body(a_ref, b_ref, *rest):
        o_ref = rest[-1]
        part = _mm_tn(a_ref[...], b_ref[...])

        @pl.when(pl.program_id(2) == 0)
        def _():
            o_ref[0] = part

        @pl.when(pl.program_id(2) != 0)
        def _():
            o_ref[0] += part

    operands, in_specs, aliases = [a, b], [a_spec, b_spec], {}
    if dest is not None:
        operands.append(dest)
        in_specs.append(ANY)
        aliases = {2: 0}
    return pl.pallas_call(
        body, name="wgrad_" + member, grid=(4, nb, l // bl), in_specs=in_specs,
        out_specs=pl.BlockSpec((1, bm, cs), lambda j, i, k: (j, off // bm + i, 0)),
        out_shape=jax.ShapeDtypeStruct((4, _group_rows(group), cs), F32), input_output_aliases=aliases,
        compiler_params=_cparams("parallel", "parallel", "arbitrary"),
    )(*operands)


def _adamw(w, g, m, v, name, g_off=0):
    r, c = w.shape
    br = r
    for cand in (256, 128, 64, 32, 16, 8):
        if r % cand == 0 and g_off % cand == 0:
            br = cand
            break

    def body(w_ref, g_ref, m_ref, v_ref, go_ref, d_ref, nm_ref, nv_ref):
        gv = g_ref[...]
        go_ref[...] = gv
        nm = ADAM_B1 * m_ref[...] + (1.0 - ADAM_B1) * gv
        nv = ADAM_B2 * v_ref[...] + (1.0 - ADAM_B2) * (gv * gv)
        m_hat = nm / (1.0 - ADAM_B1 ** ADAM_STEP)
        v_hat = nv / (1.0 - ADAM_B2 ** ADAM_STEP)
        d_ref[...] = -ADAM_LR * (m_hat / (jnp.sqrt(v_hat) + ADAM_EPS) + ADAM_WD * w_ref[...])
        nm_ref[...] = nm
        nv_ref[...] = nv

    spec = lambda: pl.BlockSpec((br, c), lambda i: (i, 0))
    g_spec = pl.BlockSpec((br, c), lambda i: (g_off // br + i, 0))
    shp = jax.ShapeDtypeStruct((r, c), F32)
    return pl.pallas_call(
        body, name=name, grid=(r // br,), in_specs=[spec(), g_spec, spec(), spec()],
        out_specs=[spec(), spec(), spec(), spec()], out_shape=[shp, shp, shp, shp], compiler_params=_cparams("parallel"),
    )(w, g, m, v)


def _place():
    return lax.axis_index("x"), lax.axis_index("y"), lax.axis_index("c")


def _other_chips(x, y):
    return [(1 - x, y), (x, 1 - y), (1 - x, 1 - y)]


ANY = pl.BlockSpec(memory_space=pl.ANY)


def _gather_weights(bufs):
    n = len(bufs)

    def body(*refs):
        outs, send_sems, recv_sems = refs[n:2 * n], refs[2 * n], refs[2 * n + 1]
        x, y, c = _place()
        chips = _other_chips(x, y)

        def part(g, px, py, pc):
            half = outs[g].shape[1] // 2
            return outs[g].at[2 * px + py, pl.ds(pl.multiple_of(pc * half, 16), half), :]

        def copy(k, src, dst, to):
            return pltpu.make_async_remote_copy(src_ref=src, dst_ref=dst, send_sem=send_sems.at[k], recv_sem=recv_sems.at[k],
                                                device_id=to, device_id_type=MESH)

        first = [copy(6 * g + j, part(g, x, y, c), part(g, x, y, c), (*chip, c)) for g in range(n) for j, chip in enumerate(chips)]
        for cp in first:
            cp.start()
        passed = []
        for g in range(n):
            for j, chip in enumerate(chips):
                landed = part(g, *chip, c)
                copy(6 * g + j, landed, landed, (x, y, c)).wait_recv()
                passed.append(copy(6 * g + 3 + j, landed, landed, (x, y, 1 - c)))
                passed[-1].start()
        for g in range(n):
            for j, chip in enumerate(chips):
                other = part(g, *chip, 1 - c)
                copy(6 * g + 3 + j, other, other, (x, y, c)).wait_recv()
        for cp in first + passed:
            cp.wait_send()

    return pl.pallas_call(
        body, name="gather_weights", in_specs=[ANY] * n, out_specs=[ANY] * n,
        out_shape=[jax.ShapeDtypeStruct(b.shape, b.dtype) for b in bufs], input_output_aliases={g: g for g in range(n)},
        scratch_shapes=[pltpu.SemaphoreType.DMA((6 * n,)), pltpu.SemaphoreType.DMA((6 * n,))],
    )(*bufs)


def _cast_shards(shards, group, place):
    width, members = GROUPS[group]
    rows = _group_rows(group)

    def body(place_ref, *refs):
        out = refs[-1]
        off = 0
        for ref, (_, r) in zip(refs[:-1], members):
            out[0, off:off + r, :] = ref[...].astype(BF16)
            off += r

    grid_spec = pltpu.PrefetchScalarGridSpec(
        num_scalar_prefetch=1, grid=(1,),
        in_specs=[pl.BlockSpec((r, width), lambda i, p: (0, 0)) for _, r in members],
        out_specs=pl.BlockSpec((1, rows, width), lambda i, p: (p[0], 0, 0)))
    return pl.pallas_call(
        body, name="cast_shards_" + group, grid_spec=grid_spec, out_shape=jax.ShapeDtypeStruct((4, rows, width), BF16),
        compiler_params=_cparams("arbitrary"),
    )(place, *[shards[name] for name, _ in members])


def _block_rows(h):
    return next(cand for cand in (256, 192, 128, 64, 32, 16) if h % cand == 0)


def _add_pair(buf, got, place, name):
    n, h, w = got.shape
    bh = _block_rows(h)
    nb = h // bh

    def body(place_ref, a_ref, b_ref, s_ref, sb_ref):
        s = a_ref[...] + b_ref[...]
        s_ref[...] = s
        sb_ref[...] = s.astype(BF16)

    spec = lambda: pl.BlockSpec((1, bh, w), lambda j, i, p: (j, i, 0))
    grid_spec = pltpu.PrefetchScalarGridSpec(
        num_scalar_prefetch=1, grid=(n, nb),
        in_specs=[pl.BlockSpec((1, bh, w), lambda j, i, p: (j, p[1] * nb + i, 0)), spec()], out_specs=[spec(), spec()])
    return pl.pallas_call(
        body, name=name, grid_spec=grid_spec,
        out_shape=[jax.ShapeDtypeStruct(got.shape, F32), jax.ShapeDtypeStruct(got.shape, BF16)],
        compiler_params=_cparams("parallel", "parallel"),
    )(place, buf, got)


def _add_received(pair, got, place, name):
    _, h, w = pair.shape
    bh = _block_rows(h)
    nb = h // bh

    def body(place_ref, own_ref, got_ref, o_ref):
        o_ref[...] = ((own_ref[0] + got_ref[0].astype(F32)) + got_ref[1].astype(F32)) + got_ref[2].astype(F32)

    grid_spec = pltpu.PrefetchScalarGridSpec(
        num_scalar_prefetch=1, grid=(nb,),
        in_specs=[pl.BlockSpec((1, bh, w), lambda i, p: (p[0], i, 0)), pl.BlockSpec((3, bh, w), lambda i, p: (0, i, 0))],
        out_specs=pl.BlockSpec((bh, w), lambda i, p: (p[1] * nb + i, 0)))
    return pl.pallas_call(
        body, name=name, grid_spec=grid_spec, out_shape=jax.ShapeDtypeStruct((2 * h, w), F32),
        compiler_params=_cparams("parallel"),
    )(place, pair, got)


def _swap_reduced_halves(bufs):
    n = len(bufs)

    def body(*refs):
        outs, send_sems, recv_sems = refs[n:2 * n], refs[2 * n], refs[2 * n + 1]
        x, y, c = _place()
        copies = []
        for g in range(n):
            half = outs[g].shape[0] // 2
            own = outs[g].at[pl.ds(pl.multiple_of(c * half, 8), half), :]
            copies.append(pltpu.make_async_remote_copy(src_ref=own, dst_ref=own, send_sem=send_sems.at[g],
                                                       recv_sem=recv_sems.at[g], device_id=(x, y, 1 - c), device_id_type=MESH))
        for cp in copies:
            cp.start()
        for g in range(n):
            half = outs[g].shape[0] // 2
            other = outs[g].at[pl.ds(pl.multiple_of((1 - c) * half, 8), half), :]
            pltpu.make_async_remote_copy(src_ref=other, dst_ref=other, send_sem=send_sems.at[g], recv_sem=recv_sems.at[g],
                                         device_id=(x, y, 1 - c), device_id_type=MESH).wait_recv()
        for cp in copies:
            cp.wait_send()

    return pl.pallas_call(
        body, name="swap_reduced_halves", in_specs=[ANY] * n, out_specs=[ANY] * n,
        out_shape=[jax.ShapeDtypeStruct(b.shape, b.dtype) for b in bufs], input_output_aliases={g: g for g in range(n)},
        scratch_shapes=[pltpu.SemaphoreType.DMA((n,)), pltpu.SemaphoreType.DMA((n,))],
    )(*bufs)


HBM = pl.BlockSpec(memory_space=pltpu.HBM)
SEM = pl.BlockSpec(memory_space=pltpu.SEMAPHORE)


def _copies_start(name, bufs, n_copies, plan, after=None):
    n = len(bufs)
    extra = [] if after is None else [after]

    def body(*refs):
        sems = refs[n + len(extra):n + len(extra) + 2 * n_copies]
        x, y, c = _place()
        for i, (src, dst, dev) in enumerate(plan(refs[:n], x, y, c)):
            pltpu.make_async_remote_copy(src_ref=src, dst_ref=dst, send_sem=sems[i], recv_sem=sems[n_copies + i],
                                         device_id=dev, device_id_type=MESH).start()
        token = refs[-1]
        token[...] = jnp.zeros_like(token)

    out = pl.pallas_call(
        body, name=name,
        out_shape=[pltpu.SemaphoreType.DMA(())] * (2 * n_copies) + [pltpu.HBM(b.shape, b.dtype) for b in bufs]
        + [jax.ShapeDtypeStruct((8, 128), F32)],
        in_specs=[HBM] * n + [ANY] * len(extra),
        out_specs=[SEM] * (2 * n_copies) + [HBM] * n + [pl.BlockSpec(memory_space=pltpu.VMEM)],
        input_output_aliases={i: 2 * n_copies + i for i in range(n)},
        compiler_params=pltpu.CompilerParams(has_side_effects=pltpu.SideEffectType.DATAFLOW_SIDE_EFFECTING),
    )(*[pltpu.with_memory_space_constraint(b, pltpu.HBM) for b in bufs], *extra)
    return list(out[:2 * n_copies]), list(out[2 * n_copies:-1]), out[-1]


def _copies_wait(name, bufs, sems, after, plan):
    n = len(bufs)
    k = len(sems) // 2

    def body(*refs):
        sem_refs = refs[n:n + 2 * k]
        x, y, c = _place()
        for i, (sent, landed, dev) in enumerate(plan(refs[:n], x, y, c)):
            cp = pltpu.make_async_remote_copy(src_ref=sent, dst_ref=landed, send_sem=sem_refs[i], recv_sem=sem_refs[k + i],
                                              device_id=dev, device_id_type=MESH)
            cp.wait_send()
            cp.wait_recv()

    return pl.pallas_call(
        body, name=name, out_shape=[pltpu.HBM(b.shape, b.dtype) for b in bufs],
        in_specs=[HBM] * n + [SEM] * (2 * k) + [ANY], out_specs=[HBM] * n, input_output_aliases={i: i for i in range(n)},
        compiler_params=pltpu.CompilerParams(has_side_effects=pltpu.SideEffectType.DATAFLOW_SIDE_EFFECTING),
    )(*bufs, *sems, after)


def _row_half(ref, which, axis):
    half = ref.shape[axis] // 2
    rows = pl.ds(pl.multiple_of(which * half, 8), half)
    return ref.at[rows, :] if axis == 0 else ref.at[:, rows, :]


class _SplitGather:
    def __init__(self, own, after):
        self.n = len(own)
        self.state = _copies_start("gather_start", own, 3 * self.n, self._sent, after)

    @staticmethod
    def _sent(refs, x, y, c):
        return [(w.at[2 * x + y], w.at[2 * x + y], (px, py, c)) for w in refs for px, py in _other_chips(x, y)]

    @staticmethod
    def _landed(refs, x, y, c):
        return [(w.at[2 * x + y], w.at[2 * px + py], (px, py, c)) for w in refs for px, py in _other_chips(x, y)]

    def token(self):
        return self.state[2]

    def wait(self, which, name, after):
        sems, bufs, _ = self.state
        k = 3 * self.n
        mine = [sems[3 * i + j] for i in which for j in range(3)] + [sems[k + 3 * i + j] for i in which for j in range(3)]
        return _copies_wait(name, [bufs[i] for i in which], mine, after, self._landed)


class _SplitReduction:
    def __init__(self, tag, groups, place):
        self.tag, self.groups, self.place = tag, groups, place

    def start_pair(self, bufs):
        n = len(bufs)
        lands = [lax.empty((4, b.shape[1] // 2, b.shape[2]), F32) for b in bufs]
        plan = lambda refs, x, y, c: [(_row_half(refs[i], 1 - c, 1), refs[n + i], (x, y, 1 - c)) for i in range(n)]
        self._pair = (_copies_start("pair_%s_start" % self.tag, bufs + lands, n, plan), plan, n)
        return self._pair[0][2]

    def pair_done_start_scatter(self, after):
        (sems, bufs, _), plan, n = self._pair
        out = _copies_wait("pair_%s_wait" % self.tag, bufs, sems, after, plan)
        pairs = [_add_pair(out[i], out[n + i], self.place, "add_pair_" + g) for i, g in enumerate(self.groups)]
        self._pair_f32 = [p[0] for p in pairs]
        lands = [lax.empty((3,) + p[1].shape[1:], BF16) for p in pairs]
        plan = lambda refs, x, y, c: [(refs[i].at[2 * px + py], refs[n + i].at[j], (px, py, c))
                                      for i in range(n) for j, (px, py) in enumerate(_other_chips(x, y))]
        self._scatter = (_copies_start("scatter_%s_start" % self.tag, [p[1] for p in pairs] + lands, 3 * n, plan), plan, n)
        return self._scatter[0][2]

    def scatter_done(self, after):
        (sems, bufs, _), plan, n = self._scatter
        out = _copies_wait("scatter_%s_wait" % self.tag, bufs, sems, after, plan)
        return [_add_received(self._pair_f32[i], out[n + i], self.place, "add_received_" + g)
                for i, g in enumerate(self.groups)]

    def start_join(self, halves):
        n = len(halves)
        sent = lambda refs, x, y, c: [(_row_half(r, c, 0), _row_half(r, c, 0), (x, y, 1 - c)) for r in refs]
        landed = lambda refs, x, y, c: [(_row_half(r, c, 0), _row_half(r, 1 - c, 0), (x, y, 1 - c)) for r in refs]
        self._join = (_copies_start("join_%s_start" % self.tag, halves, n, sent), landed)
        return self._join[0][2]

    def join_done(self, after):
        (sems, bufs, _), landed = self._join
        return _copies_wait("join_%s_wait" % self.tag, bufs, sems, after, landed)


def _all_sum_small(mine, token):
    rows, w = mine.shape

    def body(in_ref, token_ref, out_ref, sibling, pair, chips, send_sems, recv_sems):
        x, y, c = _place()
        swap = pltpu.make_async_remote_copy(src_ref=in_ref, dst_ref=sibling, send_sem=send_sems.at[0], recv_sem=recv_sems.at[0],
                                            device_id=(x, y, 1 - c), device_id_type=MESH)
        swap.start()
        swap.wait()
        pair[...] = in_ref[...] + sibling[...]
        chip = 2 * x + y
        chips[chip] = pair[...]
        copies = [pltpu.make_async_remote_copy(src_ref=pair, dst_ref=chips.at[chip], send_sem=send_sems.at[1 + j],
                                               recv_sem=recv_sems.at[1 + j], device_id=(px, py, c), device_id_type=MESH)
                  for j, (px, py) in enumerate(_other_chips(x, y))]
        for cp in copies:
            cp.start()
        for cp in copies:
            cp.wait()
        out_ref[...] = ((chips[0] + chips[1]) + chips[2]) + chips[3]

    return pl.pallas_call(
        body, name="all_sum_small", out_shape=jax.ShapeDtypeStruct((rows, w), F32),
        in_specs=[pl.BlockSpec(memory_space=pltpu.VMEM), ANY], out_specs=pl.BlockSpec(memory_space=pltpu.VMEM),
        scratch_shapes=[pltpu.VMEM((rows, w), F32), pltpu.VMEM((rows, w), F32), pltpu.VMEM((4, rows, w), F32),
                        pltpu.SemaphoreType.DMA((4,)), pltpu.SemaphoreType.DMA((4,))],
        compiler_params=pltpu.CompilerParams(vmem_limit_bytes=VMEM_LIMIT_V7X),
    )(mine, token)


def _join_column_shards(g):
    return jnp.transpose(g, (1, 0, 2)).reshape(g.shape[1], 4 * g.shape[2])


def _split_column_shards(w):
    r = w.shape[0]
    return jnp.transpose(w.reshape(r, 4, w.shape[1] // 4), (1, 0, 2))


def _small_rows(shape):
    return -(-int(np.prod(shape)) // 1024)


def _pack_small(vals):
    segs = []
    for name, shape in SMALL_WEIGHTS:
        flat = vals[name].reshape(-1)
        segs.append(jnp.pad(flat, (0, _small_rows(shape) * 1024 - flat.shape[0])))
    total = sum(s.shape[0] for s in segs) // 1024
    segs.append(jnp.zeros((-total % 8 * 1024,), F32))
    return jnp.concatenate(segs).reshape(-1, 1024)


def _unpack_small(packed):
    out, off = {}, 0
    for name, shape in SMALL_WEIGHTS:
        rows = _small_rows(shape)
        out[name] = packed[off:off + rows].reshape(-1)[:int(np.prod(shape))].reshape(shape)
        off += rows
    return out


W_IN_SHARD = D_IN // 4
W_IN_GAP = 1216


def _pad_w_in(g):
    cut = W_IN_GAP - W_IN_SHARD
    return jnp.concatenate([g[0], g[1][:, :cut], jnp.zeros((g.shape[1], D_IN_PAD - D_IN), g.dtype), g[1][:, cut:], g[2], g[3]],
                           axis=1)


def _unpad_w_in(g):
    skip = D_IN_PAD - D_IN
    second = jnp.concatenate([g[:, W_IN_SHARD:W_IN_GAP], g[:, W_IN_GAP + skip:2 * W_IN_SHARD + skip]], axis=1)
    return jnp.stack([g[:, :W_IN_SHARD], second, g[:, 2 * W_IN_SHARD + skip:3 * W_IN_SHARD + skip],
                      g[:, 3 * W_IN_SHARD + skip:]])


def _pad_heads(w):
    r = w.shape[0]
    return jnp.pad(w.reshape(r, N_HEADS, QK_HEAD), ((0, 0), (0, 0), (0, HEAD_PAD - QK_HEAD))).reshape(r, N_HEADS * HEAD_PAD)


def _unpad_heads(g):
    r = g.shape[0]
    return g.reshape(r, N_HEADS, HEAD_PAD)[:, :, :QK_HEAD].reshape(r, N_HEADS * QK_HEAD)


def _local_step(x, positions, tgt, grp_b, small, gather, red_a, red_rest):
    l = x.shape[0]
    t = min(l, 512)
    t_mlp = min(l, 256)
    tq = min(l, 512)
    tc = min(l, 256)
    row = lambda v: v.reshape(1, -1).astype(F32)

    w_in_p = _pad_w_in(grp_b)
    g1, g2 = row(small["norm_mix"]), row(small["norm_mlp"])
    gqa, gkva = row(small["q_a_norm"]), row(small["kv_a_norm"])
    gq = jnp.pad(row(small["q_norm"]), ((0, 0), (0, HEAD_PAD - QK_HEAD)))
    gk = jnp.pad(row(small["k_norm"]), ((0, 0), (0, HEAD_PAD - QK_HEAD)))
    half = QK_ROPE // 2
    inv_freq = ROPE_THETA ** (-jnp.arange(half, dtype=F32) / half)
    invf = jnp.concatenate([inv_freq, inv_freq, jnp.zeros((64,), F32)]).reshape(1, 128)
    sgn = jnp.concatenate([-jnp.ones((half,), F32), jnp.ones((half,), F32), jnp.zeros((64,), F32)]).reshape(1, 128)
    pos = positions.reshape(l, 1)

    a_re, a_im = small["ssm_a_re"], small["ssm_a_im"]
    log_dt = small["ssm_log_dt"].reshape(SSM_GROUPS, 1)
    to_gcp = lambda b: jnp.transpose(b, (0, 2, 1)).reshape(SSM_WIDTH, SSM_STATE)
    from_gcp = lambda b: jnp.transpose(b.reshape(SSM_GROUPS, SSM_GROUP_CH, SSM_STATE), (0, 2, 1))
    b_re, b_im = to_gcp(small["ssm_b_re"]), to_gcp(small["ssm_b_im"])
    c_re, c_im = small["ssm_c_re"].reshape(SSM_WIDTH, SSM_STATE), small["ssm_c_im"].reshape(SSM_WIDTH, SSM_STATE)
    wb, wc, tabs_fwd, tabs_rev = _ssm_param_fwd(a_re, a_im, log_dt, b_re, b_im, c_re, c_im)
    dskip = row(small["ssm_d"])
    b_glu = row(small["b_glu"])

    u, lat, gs, gm = _in_proj_fwd(x, g1, w_in_p, t, gather.token())
    grp_c, grp_d, grp_e = gather.wait([0, 1, 2], "gather_cde_wait", u)
    w_qb_p = _pad_heads(_join_column_shards(grp_c))
    xr, xi, y, y_ssm = _ssm_fwd(u, wb, wc, tabs_fwd, dskip, grp_d, b_glu, grp_e, tc)
    q, k, v = _mla_pre_fwd(lat, pos, invf, sgn, gqa, gkva, gq, gk, w_qb_p, grp_d, t)
    attn, lse = _attn_fwd(q, k, v, tq)
    (grp_a,) = gather.wait([3], "gather_a_wait", attn)
    y_mla, mixed, h = _merge_fwd(attn, y_ssm, gs, gm, x, grp_a, t)
    dh, hn, da, hid, dout, loss_blk, g_norm_mlp = _mlp_fwd_bwd(h, tgt, g2, grp_a, t_mlp)

    ga = _wgrad_into(hn, da, "w_up", "col", _wgrad_into(hid, dout, "w_down", "row"))
    dys, dym, dgs, dgm, dattn = _merge_bwd(dh, y_ssm, y_mla, gs, gm, grp_a, t)
    ga = _wgrad_into(attn, dym, "w_o_mla", "row", _wgrad_into(mixed, dh, "w_out", "row", ga))

    dq, dk, dv = _attn_bwd(q, k, v, attn, dattn, lse, tq, red_a.start_pair([ga]))
    d_lat, ql, dq0, ckn, dkv, g_qa, g_kva, g_q, g_k = _mla_pre_bwd(lat, pos, invf, sgn, gqa, gkva, gq, gk, w_qb_p, grp_d,
                                                                    dq, dk, dv, t, red_a.pair_done_start_scatter(dk))
    gc = _split_column_shards(_unpad_heads(_wgrad(ql, dq0, "wgrad_q_b")))

    d_u, z, z2, dpre, g_b_glu, g_d, g_lr, g_li, g_wb, g_wct = _ssm_bwd(
        dys, y, u, xr, xi, wb, wc, tabs_rev, dskip, grp_d, b_glu, grp_e, tc)
    gd = _wgrad_into(z, dpre, "w_glu", "row", _wgrad_into(ckn, dkv, "w_kv_b", "col"))
    ge = _wgrad_into(z2, dys, "w_o_ssm", "col")
    grad_x, xn, dproj, g_norm_mix = _in_proj_bwd(x, g1, w_in_p, d_u, d_lat, dgs, dgm, dh, t)
    gb = _unpad_w_in(_wgrad(xn, dproj, "wgrad_in"))

    red_a.start_join(red_a.scatter_done(gb))
    g_ar, g_ai, g_ldt, g_br, g_bi, g_cr, g_ci = _ssm_param_bwd(a_re, a_im, log_dt, b_re, b_im, g_lr, g_li, g_wb, g_wct,
                                                                red_rest.start_pair([gb, gc, gd, ge]))
    scatter_started = red_rest.pair_done_start_scatter(g_ar)

    g_small = {
        "norm_mix": g_norm_mix.reshape(-1), "norm_mlp": g_norm_mlp.reshape(-1), "q_a_norm": g_qa.reshape(-1),
        "kv_a_norm": g_kva.reshape(-1), "q_norm": g_q.reshape(-1)[:QK_HEAD], "k_norm": g_k.reshape(-1)[:QK_HEAD],
        "ssm_a_re": g_ar, "ssm_a_im": g_ai, "ssm_log_dt": g_ldt.reshape(-1),
        "ssm_b_re": from_gcp(g_br), "ssm_b_im": from_gcp(g_bi),
        "ssm_c_re": g_cr.reshape(SSM_GROUPS, SSM_GROUP_CH, SSM_STATE), "ssm_c_im": g_ci.reshape(SSM_GROUPS, SSM_GROUP_CH, SSM_STATE),
        "ssm_d": g_d.reshape(SSM_GROUPS, SSM_GROUP_CH), "b_glu": g_b_glu.reshape(-1),
    }
    return loss_blk[0, 0], grad_x, g_small, scatter_started


def kernel(x, positions, norm_mix, w_in, q_a_norm, kv_a_norm, w_q_b, w_kv_b, q_norm, k_norm, w_o_mla, ssm_a_re, ssm_a_im, ssm_log_dt, ssm_b_re, ssm_b_im, ssm_c_re, ssm_c_im, ssm_d, w_glu, b_glu, w_o_ssm, w_out, norm_mlp, w_up, w_down, loss_target, m_norm_mix, m_w_in, m_q_a_norm, m_kv_a_norm, m_w_q_b, m_w_kv_b, m_q_norm, m_k_norm, m_w_o_mla, m_ssm_a_re, m_ssm_a_im, m_ssm_log_dt, m_ssm_b_re, m_ssm_b_im, m_ssm_c_re, m_ssm_c_im, m_ssm_d, m_w_glu, m_b_glu, m_w_o_ssm, m_w_out, m_norm_mlp, m_w_up, m_w_down, v_norm_mix, v_w_in, v_q_a_norm, v_kv_a_norm, v_w_q_b, v_w_kv_b, v_q_norm, v_k_norm, v_w_o_mla, v_ssm_a_re, v_ssm_a_im, v_ssm_log_dt, v_ssm_b_re, v_ssm_b_im, v_ssm_c_re, v_ssm_c_im, v_ssm_d, v_w_glu, v_b_glu, v_w_o_ssm, v_w_out, v_norm_mlp, v_w_up, v_w_down):
    args = dict(locals())
    w = {n: args[n][0] for n in WEIGHT_ORDER}
    m = {n: args["m_" + n][0] for n in WEIGHT_ORDER}
    v = {n: args["v_" + n][0] for n in WEIGHT_ORDER}
    big_names = [n for n, *_ in BIG_WEIGHTS]
    small_names = [n for n, _ in SMALL_WEIGHTS]

    place = jnp.stack([2 * lax.axis_index("x") + lax.axis_index("y"), lax.axis_index("c")]).astype(jnp.int32)
    rest = ["b", "c", "d", "e"]

    (grp_b,) = _gather_weights([_cast_shards(w, "b", place)])
    gather = _SplitGather([_cast_shards(w, g, place) for g in ("c", "d", "e", "a")], grp_b)
    red_a = _SplitReduction("a", ["a"], place)
    red_rest = _SplitReduction("rest", rest, place)
    small = {n: w[n] for n in small_names}

    loss_local, grad_x, g_small, scatter_started = _local_step(x[0], positions[0], loss_target[0], grp_b, small, gather,
                                                               red_a, red_rest)
    loss = lax.psum(loss_local, ("x", "y", "c"))

    grad_w, delta_w, new_m, new_v = {}, {}, {}, {}

    def update(names, reduced):
        for n in names:
            g, off, _, _ = _place_in_group(n)
            grad_w[n], delta_w[n], new_m[n], new_v[n] = _adamw(w[n], reduced[g], m[n], v[n], "adamw_" + n, off)

    small_sum = _all_sum_small(_pack_small(g_small), scatter_started)
    g_s, d_s, m_s, v_s = _adamw(_pack_small(small), small_sum, _pack_small({n: m[n] for n in small_names}),
                                _pack_small({n: v[n] for n in small_names}), "adamw_small")
    g_s, d_s, m_s, v_s = _unpack_small(g_s), _unpack_small(d_s), _unpack_small(m_s), _unpack_small(v_s)
    for n in small_names:
        grad_w[n], delta_w[n], new_m[n], new_v[n] = g_s[n], d_s[n], m_s[n], v_s[n]
    in_a = [n for n, _ in GROUPS["a"][1]]
    update(in_a, {"a": red_a.join_done(small_sum)[0]})
    halves = red_rest.scatter_done(new_v[in_a[-1]])
    update([n for n in big_names if n not in in_a], dict(zip(rest, _swap_reduced_halves(halves))))

    lead = lambda d: [d[n][None] for n in WEIGHT_ORDER]
    return (loss, grad_x[None], *lead(grad_w), *lead(delta_w), *lead(new_m), *lead(new_v))
```

```python
import math

import jax
import jax.numpy as jnp
import numpy as np
from jax import lax
from jax.experimental import pallas as pl
from jax.experimental.pallas import tpu as pltpu

F32 = jnp.float32
BF16 = jnp.bfloat16

D_MODEL = 1024
SSM_GROUPS = 32
SSM_GROUP_CH = 16
SSM_WIDTH = 512
SSM_STATE = 64
GP = SSM_GROUPS * SSM_STATE
N_HEADS = 8
QK_NOPE = 128
QK_ROPE = 64
QK_HEAD = 192
HEAD_PAD = 256
V_HEAD = 128
Q_LORA = 384
KV_LORA = 256
LAT_W = 768
D_IN = 3264
D_IN_PAD = 3328
D_FF = 4096
ROPE_THETA = 10000.0
EPS = 1e-6
ATT_SCALE = QK_HEAD ** -0.5

ADAM_LR = 0.001
ADAM_B1 = 0.9
ADAM_B2 = 0.999
ADAM_EPS = 1e-08
ADAM_WD = 0.01
ADAM_STEP = 10

VMEM_LIMIT_V7X = 56 * 1024 * 1024
MESH = pl.DeviceIdType.MESH

BIG_WEIGHTS = (
    ("w_in", 1024, 3264, "col"),
    ("w_q_b", 384, 1536, "col"),
    ("w_kv_b", 256, 2048, "col"),
    ("w_o_mla", 1024, 1024, "row"),
    ("w_glu", 512, 512, "row"),
    ("w_o_ssm", 512, 1024, "col"),
    ("w_out", 1024, 1024, "row"),
    ("w_up", 1024, 4096, "col"),
    ("w_down", 4096, 1024, "row"),
)
GROUPS = {
    "a": (1024, (("w_down", 1024), ("w_up", 1024), ("w_o_mla", 256), ("w_out", 256))),
    "b": (816, (("w_in", 1024),)),
    "c": (384, (("w_q_b", 384),)),
    "d": (512, (("w_kv_b", 256), ("w_glu", 128))),
    "e": (256, (("w_o_ssm", 512),)),
}


def _group_rows(group):
    return sum(r for _, r in GROUPS[group][1])


def _place_in_group(name):
    for group, (width, members) in GROUPS.items():
        off = 0
        for member, rows in members:
            if member == name:
                return group, off, rows, width
            off += rows
    raise KeyError(name)


SMALL_WEIGHTS = (
    ("norm_mix", (1024,)), ("q_a_norm", (384,)), ("kv_a_norm", (256,)), ("q_norm", (192,)), ("k_norm", (192,)),
    ("ssm_a_re", (32, 64)), ("ssm_a_im", (32, 64)), ("ssm_log_dt", (32,)),
    ("ssm_b_re", (32, 64, 16)), ("ssm_b_im", (32, 64, 16)), ("ssm_c_re", (32, 16, 64)), ("ssm_c_im", (32, 16, 64)),
    ("ssm_d", (32, 16)), ("b_glu", (512,)), ("norm_mlp", (1024,)),
)
WEIGHT_ORDER = ('norm_mix', 'w_in', 'q_a_norm', 'kv_a_norm', 'w_q_b', 'w_kv_b', 'q_norm', 'k_norm', 'w_o_mla', 'ssm_a_re',
                'ssm_a_im', 'ssm_log_dt', 'ssm_b_re', 'ssm_b_im', 'ssm_c_re', 'ssm_c_im', 'ssm_d', 'w_glu', 'b_glu',
                'w_o_ssm', 'w_out', 'norm_mlp', 'w_up', 'w_down')


def _cparams(*sem):
    return pltpu.CompilerParams(dimension_semantics=sem if sem else None, vmem_limit_bytes=VMEM_LIMIT_V7X)


def _resident(shape, index=None):
    index = (0,) * len(shape) if index is None else index
    return pl.BlockSpec(shape, lambda *_: index, pipeline_mode=pl.Buffered(1))


def _member_block(name):
    _, off, rows, width = _place_in_group(name)
    return _resident((4, rows, width), (0, off // rows, 0))


def _rows(t, width):
    return pl.BlockSpec((t, width), lambda i: (i, 0))


def _mm(a, b):
    return jnp.dot(a.astype(BF16), b.astype(BF16), preferred_element_type=F32)


def _mm_nt(a, b):
    return lax.dot_general(a.astype(BF16), b.astype(BF16), (((1,), (1,)), ((), ())), preferred_element_type=F32)


def _mm_tn(a, b):
    return lax.dot_general(a.astype(BF16), b.astype(BF16), (((0,), (0,)), ((), ())), preferred_element_type=F32)


def _rms_fwd(x, g, n):
    r = lax.rsqrt(jnp.sum(x * x, axis=-1, keepdims=True) * (1.0 / n) + EPS)
    return x * r * g


def _rms_bwd(x, g, dy, n):
    r = lax.rsqrt(jnp.sum(x * x, axis=-1, keepdims=True) * (1.0 / n) + EPS)
    xh = x * r
    dxh = dy * g
    dx = r * (dxh - xh * (jnp.sum(dxh * xh, axis=-1, keepdims=True) * (1.0 / n)))
    return dx, dy * xh


def _colsum(a):
    return jnp.sum(a, axis=0, keepdims=True)


def _accumulate(ref, value, first):
    @pl.when(first)
    def _():
        ref[...] = value

    @pl.when(jnp.logical_not(first))
    def _():
        ref[...] += value


def _sigmoid(a):
    return 1.0 / (1.0 + jnp.exp(-a))


GELU_C = math.sqrt(2.0 / math.pi)
GELU_A = 0.044715


def _gelu(y):
    return 0.5 * y * (1.0 + jnp.tanh(GELU_C * (y + GELU_A * y * y * y)))


def _gelu_grad(y):
    t = jnp.tanh(GELU_C * (y + GELU_A * y * y * y))
    return 0.5 * (1.0 + t) + 0.5 * y * (1.0 - t * t) * GELU_C * (1.0 + 3.0 * GELU_A * y * y)


def _in_proj_fwd(x, g1, w_in_p, t, token):
    l = x.shape[0]

    def body(x_ref, g_ref, w_ref, token_ref, u_ref, lat_ref, gs_ref, gm_ref):
        xn = _rms_fwd(x_ref[...], g_ref[...], D_MODEL).astype(BF16)
        u_ref[...] = _mm(xn, w_ref[:, 0:512])
        lat_ref[...] = _mm(xn, w_ref[:, 512:1280])
        gs_ref[...] = _mm(xn, w_ref[:, 1280:2304])
        gm_ref[...] = _mm(xn, w_ref[:, 2304:3328])

    return pl.pallas_call(
        body, name="in_proj_fwd", grid=(l // t,),
        in_specs=[_rows(t, D_MODEL), _resident((1, D_MODEL)), _resident((D_MODEL, D_IN_PAD)), ANY],
        out_specs=[_rows(t, 512), _rows(t, LAT_W), _rows(t, D_MODEL), _rows(t, D_MODEL)],
        out_shape=[jax.ShapeDtypeStruct((l, 512), F32), jax.ShapeDtypeStruct((l, LAT_W), F32),
                   jax.ShapeDtypeStruct((l, D_MODEL), F32), jax.ShapeDtypeStruct((l, D_MODEL), F32)],
        compiler_params=_cparams("parallel"),
    )(x, g1, w_in_p, token)


def _in_proj_bwd(x, g1, w_in_p, d_u, d_lat, d_gs, d_gm, dh, t):
    l = x.shape[0]

    def body(x_ref, g_ref, w_ref, du_ref, dlat_ref, dgs_ref, dgm_ref, dh_ref, gx_ref, xn_ref, dproj_ref, dg_ref):
        xv = x_ref[...]
        g = g_ref[...]
        xn_ref[...] = _rms_fwd(xv, g, D_MODEL).astype(BF16)
        dproj_ref[:, 0:512] = du_ref[...]
        dproj_ref[:, 512:1280] = dlat_ref[...]
        dproj_ref[:, 1280:2304] = dgs_ref[...]
        dproj_ref[:, 2304:3328] = dgm_ref[...]
        dxn = _mm_nt(dproj_ref[...], w_ref[...])
        dx, dg_rows = _rms_bwd(xv, g, dxn, D_MODEL)
        gx_ref[...] = dh_ref[...] + dx
        _accumulate(dg_ref, _colsum(dg_rows), pl.program_id(0) == 0)

    return pl.pallas_call(
        body, name="in_proj_bwd", grid=(l // t,),
        in_specs=[_rows(t, D_MODEL), _resident((1, D_MODEL)), _resident((D_MODEL, D_IN_PAD)), _rows(t, 512),
                  _rows(t, LAT_W), _rows(t, D_MODEL), _rows(t, D_MODEL), _rows(t, D_MODEL)],
        out_specs=[_rows(t, D_MODEL), _rows(t, D_MODEL), _rows(t, D_IN_PAD), pl.BlockSpec((1, D_MODEL), lambda i: (0, 0))],
        out_shape=[jax.ShapeDtypeStruct((l, D_MODEL), F32), jax.ShapeDtypeStruct((l, D_MODEL), BF16),
                   jax.ShapeDtypeStruct((l, D_IN_PAD), BF16), jax.ShapeDtypeStruct((1, D_MODEL), F32)],
        compiler_params=_cparams("arbitrary"),
    )(x, g1, w_in_p, d_u, d_lat, d_gs, d_gm, dh)


def _ssm_param_fn(a_re, a_im, log_dt, b_re, b_im):
    dt = jnp.exp(log_dt)
    er = jnp.exp(a_re * dt)
    lr = er * jnp.cos(a_im * dt)
    li = er * jnp.sin(a_im * dt)
    den = a_re * a_re + a_im * a_im
    nr = lr - 1.0
    kr = (nr * a_re + li * a_im) / den
    ki = (li * a_re - nr * a_im) / den
    rows = lambda k: jnp.broadcast_to(k[:, None, :], (SSM_GROUPS, SSM_GROUP_CH, SSM_STATE)).reshape(SSM_WIDTH, SSM_STATE)
    krt, kit = rows(kr), rows(ki)
    return lr, li, krt * b_re - kit * b_im, krt * b_im + kit * b_re


def _state_selector():
    row = lax.broadcasted_iota(jnp.int32, (SSM_STATE, GP), 0)
    col = lax.broadcasted_iota(jnp.int32, (SSM_STATE, GP), 1)
    return jnp.where(jnp.bitwise_and(col, SSM_STATE - 1) == row, 1.0, 0.0).astype(BF16)


def _own_group(rows, rows_per_group_log2):
    row = lax.broadcasted_iota(jnp.int32, (rows, GP), 0)
    col = lax.broadcasted_iota(jnp.int32, (rows, GP), 1)
    return jnp.right_shift(row, rows_per_group_log2) == jnp.right_shift(col, 6)


def _three_bf16(x):
    hi = x.astype(BF16)
    rest = x - hi.astype(F32)
    mid = rest.astype(BF16)
    return hi, mid, (rest - mid.astype(F32)).astype(BF16)


def _spread(x, sel):
    return sum(jnp.dot(part, sel, preferred_element_type=F32) for part in _three_bf16(x))


def _collect(xw, sel):
    return sum(lax.dot_general(part, sel, (((1,), (1,)), ((), ())), preferred_element_type=F32) for part in _three_bf16(xw))


def _ssm_param_fwd(a_re, a_im, log_dt, b_re, b_im, c_re, c_im):
    def body(ar_ref, ai_ref, ldt_ref, br_ref, bi_ref, cr_ref, ci_ref, wb_ref, wct_ref, tf_ref, tr_ref):
        lr, li, bbr, bbi = _ssm_param_fn(ar_ref[...], ai_ref[...], ldt_ref[...], br_ref[...], bi_ref[...])
        sel = _state_selector()
        own16 = _own_group(SSM_WIDTH, 4)
        own1 = _own_group(SSM_GROUPS, 0)
        block = lambda m: jnp.where(own16, jnp.dot(m.astype(BF16), sel, preferred_element_type=F32), 0.0).astype(BF16)
        wb_ref[:, 0:GP] = block(bbr)
        wb_ref[:, GP:2 * GP] = block(bbi)
        wct_ref[:, 0:GP] = block(cr_ref[...])
        wct_ref[:, GP:2 * GP] = block(-ci_ref[...])
        flat = lambda m: _colsum(jnp.where(own1, _spread(m, sel), 0.0))
        pr, pi = [], []
        qr, qi = lr, li
        for _ in range(8):
            pr.append(flat(qr))
            pi.append(flat(qi))
            qr, qi = qr * lr - qi * li, qr * li + qi * lr
        row = lax.broadcasted_iota(jnp.int32, (8, GP), 0)
        for n, k in enumerate((1, 2, 4)):
            tf_ref[2 * n] = jnp.where(row >= k, pr[k - 1], 0.0)
            tf_ref[2 * n + 1] = jnp.where(row >= k, pi[k - 1], 0.0)
            tr_ref[2 * n] = jnp.where(row < 8 - k, pr[k - 1], 0.0)
            tr_ref[2 * n + 1] = jnp.where(row < 8 - k, -pi[k - 1], 0.0)
        pick = lambda vals: sum(jnp.where(row == j, v, 0.0) for j, v in enumerate(vals))
        tf_ref[6] = pick(pr)
        tf_ref[7] = pick(pi)
        tr_ref[6] = pick(pr[::-1])
        tr_ref[7] = pick([-v for v in pi[::-1]])

    return pl.pallas_call(
        body, name="ssm_param_fwd",
        out_shape=[jax.ShapeDtypeStruct((SSM_WIDTH, 2 * GP), BF16), jax.ShapeDtypeStruct((SSM_WIDTH, 2 * GP), BF16),
                   jax.ShapeDtypeStruct((8, 8, GP), F32), jax.ShapeDtypeStruct((8, 8, GP), F32)],
        compiler_params=_cparams(),
    )(a_re, a_im, log_dt, b_re, b_im, c_re, c_im)


STRIP_CH = 128
STRIP_ST = 512
N_STRIPS = SSM_WIDTH // STRIP_CH


def _ssm_param_bwd(a_re, a_im, log_dt, b_re, b_im, g_lr, g_li, g_wb, g_wct):
    def body(ar_ref, ai_ref, ldt_ref, br_ref, bi_ref, glr_ref, gli_ref, gwb_ref, gwc_ref,
             o_ar, o_ai, o_ldt, o_br, o_bi, o_cr, o_ci):
        sel = _state_selector()
        own1 = _own_group(SSM_GROUPS, 0)
        row = lax.broadcasted_iota(jnp.int32, (SSM_WIDTH, STRIP_ST), 0)
        col = lax.broadcasted_iota(jnp.int32, (SSM_WIDTH, STRIP_ST), 1)
        own = jnp.bitwise_and(jnp.right_shift(row, 4), 7) == jnp.right_shift(col, 6)
        blocks = lambda m: _collect(jnp.where(own, m, 0.0), sel[:, 0:STRIP_ST])
        unflat = lambda v: _collect(jnp.where(own1, v, 0.0), sel)
        _, vjp = jax.vjp(_ssm_param_fn, ar_ref[...], ai_ref[...], ldt_ref[...], br_ref[...], bi_ref[...])
        d_ar, d_ai, d_ldt, d_br, d_bi = vjp((unflat(glr_ref[...]), unflat(gli_ref[...]),
                                             blocks(gwb_ref[:, 0:STRIP_ST]), blocks(gwb_ref[:, STRIP_ST:2 * STRIP_ST])))
        o_ar[...] = d_ar
        o_ai[...] = d_ai
        o_ldt[...] = d_ldt
        o_br[...] = d_br
        o_bi[...] = d_bi
        o_cr[...] = blocks(gwc_ref[:, 0:STRIP_ST])
        o_ci[...] = -blocks(gwc_ref[:, STRIP_ST:2 * STRIP_ST])

    g, p = SSM_GROUPS, SSM_STATE
    gp = jax.ShapeDtypeStruct((g, p), F32)
    gcp = jax.ShapeDtypeStruct((SSM_WIDTH, p), F32)
    return pl.pallas_call(
        body, name="ssm_param_bwd", out_shape=[gp, gp, jax.ShapeDtypeStruct((g, 1), F32), gcp, gcp, gcp, gcp],
        compiler_params=_cparams(),
    )(a_re, a_im, log_dt, b_re, b_im, g_lr, g_li, g_wb, g_wct)


def _strip(ref, j, im):
    return ref[STRIP_CH * j:STRIP_CH * (j + 1), im * GP + STRIP_ST * j:im * GP + STRIP_ST * (j + 1)]


def _wgrad_strips(a, b_re, b_im, name, im_block, token):
    l = a.shape[0]
    bl = min(l, 512)

    def body(a_ref, bre_ref, bim_ref, token_ref, o_ref):
        first = pl.program_id(0) == 0
        for j in range(N_STRIPS):
            aj = a_ref[:, STRIP_CH * j:STRIP_CH * (j + 1)]
            states = slice(STRIP_ST * j, STRIP_ST * (j + 1))
            _accumulate(o_ref.at[STRIP_CH * j:STRIP_CH * (j + 1), 0:STRIP_ST], _mm_tn(aj, bre_ref[:, states]), first)
            _accumulate(o_ref.at[STRIP_CH * j:STRIP_CH * (j + 1), STRIP_ST:2 * STRIP_ST], _mm_tn(aj, bim_ref[:, states]), first)

    return pl.pallas_call(
        body, name=name, grid=(l // bl,),
        in_specs=[pl.BlockSpec((bl, SSM_WIDTH), lambda k: (k, 0)), pl.BlockSpec((bl, GP), lambda k: (k, 0)),
                  pl.BlockSpec((bl, GP), lambda k: (k, im_block)), ANY],
        out_specs=pl.BlockSpec((SSM_WIDTH, 2 * STRIP_ST), lambda k: (0, 0)),
        out_shape=jax.ShapeDtypeStruct((SSM_WIDTH, 2 * STRIP_ST), F32),
        compiler_params=_cparams("arbitrary"),
    )(a, b_re, b_im, token)


SCAN_STRIP = 512


def _scan_chunk(inr_ref, ini_ref, outr_ref, outi_ref, cr_ref, ci_ref, tab_ref, tc, reverse):
    n_blocks = tc // 8

    def block(j, _):
        i = (n_blocks - 1 - j) if reverse else j
        rows = pl.ds(pl.multiple_of(i * 8, 8), 8)
        for s in range(GP // SCAN_STRIP):
            sl = pl.ds(s * SCAN_STRIP, SCAN_STRIP)
            xr = inr_ref[rows, sl]
            xi = ini_ref[rows, sl]
            for n, k in enumerate((1, 2, 4)):
                shift = (8 - k) if reverse else k
                sr = pltpu.roll(xr, shift, 0)
                si = pltpu.roll(xi, shift, 0)
                mr = tab_ref[2 * n, :, sl]
                mi = tab_ref[2 * n + 1, :, sl]
                xr, xi = xr + mr * sr - mi * si, xi + mr * si + mi * sr
            qr = tab_ref[6, :, sl]
            qi = tab_ref[7, :, sl]
            cr = cr_ref[:, sl]
            ci = ci_ref[:, sl]
            xr, xi = xr + qr * cr - qi * ci, xi + qr * ci + qi * cr
            outr_ref[rows, sl] = xr
            outi_ref[rows, sl] = xi
            edge = 0 if reverse else 7
            cr_ref[:, sl] = jnp.broadcast_to(xr[edge:edge + 1, :], (8, SCAN_STRIP))
            ci_ref[:, sl] = jnp.broadcast_to(xi[edge:edge + 1, :], (8, SCAN_STRIP))
        return 0

    lax.fori_loop(0, n_blocks, block, 0)


def _glu_pre(z, wg_ref):
    return sum(_mm(z[:, 128 * j:128 * (j + 1)], wg_ref[j]) for j in range(4))


def _ssm_fwd(u, wb, wc, tabs, dskip, grp_d, b_glu, grp_e, tc):
    l = u.shape[0]

    def body(u_ref, wb_ref, wc_ref, tab_ref, d_ref, wg_ref, bg_ref, wo_ref, xr_ref, xi_ref, y_ref, ys_ref,
             bur, bui, cr, ci):
        @pl.when(pl.program_id(0) == 0)
        def _():
            cr[...] = jnp.zeros_like(cr)
            ci[...] = jnp.zeros_like(ci)

        uv = u_ref[...]
        ub = uv.astype(BF16)
        for j in range(N_STRIPS):
            uj = ub[:, STRIP_CH * j:STRIP_CH * (j + 1)]
            states = slice(STRIP_ST * j, STRIP_ST * (j + 1))
            bur[:, states] = _mm(uj, _strip(wb_ref, j, 0))
            bui[:, states] = _mm(uj, _strip(wb_ref, j, 1))
        _scan_chunk(bur, bui, xr_ref, xi_ref, cr, ci, tab_ref, tc, False)
        y = jnp.concatenate(
            [_mm_nt(xr_ref[:, STRIP_ST * j:STRIP_ST * (j + 1)], _strip(wc_ref, j, 0))
             + _mm_nt(xi_ref[:, STRIP_ST * j:STRIP_ST * (j + 1)], _strip(wc_ref, j, 1)) for j in range(N_STRIPS)],
            axis=-1) + d_ref[...] * uv
        y_ref[...] = y
        z = _gelu(y)
        z2 = z * _sigmoid(_glu_pre(z, wg_ref) + bg_ref[...])
        for s in range(4):
            ys_ref[:, 256 * s:256 * (s + 1)] = _mm(z2, wo_ref[s])

    return pl.pallas_call(
        body, name="ssm_fwd", grid=(l // tc,),
        in_specs=[_rows(tc, 512), _resident((512, 2 * GP)), _resident((512, 2 * GP)), _resident((8, 8, GP)),
                  _resident((1, 512)), _member_block("w_glu"), _resident((1, 512)), _member_block("w_o_ssm")],
        out_specs=[_rows(tc, GP), _rows(tc, GP), _rows(tc, 512), _rows(tc, D_MODEL)],
        out_shape=[jax.ShapeDtypeStruct((l, GP), F32), jax.ShapeDtypeStruct((l, GP), F32),
                   jax.ShapeDtypeStruct((l, 512), F32), jax.ShapeDtypeStruct((l, D_MODEL), F32)],
        scratch_shapes=[pltpu.VMEM((tc, GP), F32), pltpu.VMEM((tc, GP), F32), pltpu.VMEM((8, GP), F32),
                        pltpu.VMEM((8, GP), F32)],
        compiler_params=_cparams("arbitrary"),
    )(u, wb, wc, tabs, dskip, grp_d, b_glu, grp_e)


def _ssm_bwd(dys, y, u, xr, xi, wb, wc, tabs_rev, dskip, grp_d, b_glu, grp_e, tc):
    l = u.shape[0]
    nc = l // tc

    def body(dys_ref, y_ref, u_ref, xr_ref, xi_ref, wb_ref, wc_ref, tab_ref, d_ref, wg_ref, bg_ref, wo_ref,
             du_ref, a_ref, dy_ref, z_ref, z2_ref, dpre_ref, gb_ref, gd_ref, glr_ref, gli_ref,
             dxr, dxi, ar, ai, cr, ci):
        first = pl.program_id(0) == 0

        @pl.when(first)
        def _():
            cr[...] = jnp.zeros_like(cr)
            ci[...] = jnp.zeros_like(ci)

        yv = y_ref[...]
        uv = u_ref[...]
        dz2 = sum(_mm_nt(dys_ref[:, 256 * j:256 * (j + 1)], wo_ref[j]) for j in range(4))
        z = _gelu(yv)
        s = _sigmoid(_glu_pre(z, wg_ref) + bg_ref[...])
        dpre = dz2 * z * s * (1.0 - s)
        dpreb = dpre.astype(BF16)
        dz = dz2 * s + jnp.concatenate([_mm_nt(dpreb, wg_ref[j]) for j in range(4)], axis=-1)
        dy = dz * _gelu_grad(yv)
        z_ref[...] = z.astype(BF16)
        z2_ref[...] = (z * s).astype(BF16)
        dpre_ref[...] = dpre.astype(BF16)
        dy_ref[...] = dy.astype(BF16)
        _accumulate(gb_ref, _colsum(dpre), first)
        _accumulate(gd_ref, _colsum(dy * uv), first)

        dyb = dy.astype(BF16)
        for j in range(N_STRIPS):
            dyj = dyb[:, STRIP_CH * j:STRIP_CH * (j + 1)]
            dxr[:, STRIP_ST * j:STRIP_ST * (j + 1)] = _mm(dyj, _strip(wc_ref, j, 0))
            dxi[:, STRIP_ST * j:STRIP_ST * (j + 1)] = _mm(dyj, _strip(wc_ref, j, 1))
        ar[pl.ds(tc, 8), :] = cr[...]
        ai[pl.ds(tc, 8), :] = ci[...]
        _scan_chunk(dxr, dxi, ar, ai, cr, ci, tab_ref, tc, True)
        a_ref[:, 0:GP] = ar[pl.ds(0, tc), :].astype(BF16)
        a_ref[:, GP:2 * GP] = ai[pl.ds(0, tc), :].astype(BF16)
        du_states = jnp.concatenate(
            [_mm_nt(a_ref[:, STRIP_ST * j:STRIP_ST * (j + 1)], _strip(wb_ref, j, 0))
             + _mm_nt(a_ref[:, GP + STRIP_ST * j:GP + STRIP_ST * (j + 1)], _strip(wb_ref, j, 1)) for j in range(N_STRIPS)],
            axis=-1)
        du_ref[...] = (dy * d_ref[...] + du_states).astype(BF16)
        anr = ar[pl.ds(1, tc), :]
        ani = ai[pl.ds(1, tc), :]
        xrv = xr_ref[...]
        xiv = xi_ref[...]
        _accumulate(glr_ref, _colsum(anr * xrv + ani * xiv), first)
        _accumulate(gli_ref, _colsum(ani * xrv - anr * xiv), first)

    rev = lambda w: pl.BlockSpec((tc, w), lambda i: (nc - 1 - i, 0))
    acc = lambda w: pl.BlockSpec((1, w), lambda i: (0, 0))
    bf = jax.ShapeDtypeStruct((l, 512), BF16)
    return pl.pallas_call(
        body, name="ssm_bwd", grid=(nc,),
        in_specs=[rev(D_MODEL), rev(512), rev(512), rev(GP), rev(GP), _resident((512, 2 * GP)), _resident((512, 2 * GP)),
                  _resident((8, 8, GP)), _resident((1, 512)), _member_block("w_glu"), _resident((1, 512)),
                  _member_block("w_o_ssm")],
        out_specs=[rev(512), rev(2 * GP), rev(512), rev(512), rev(512), rev(512), acc(512), acc(512), acc(GP), acc(GP)],
        out_shape=[bf, jax.ShapeDtypeStruct((l, 2 * GP), BF16), bf, bf, bf, bf,
                   jax.ShapeDtypeStruct((1, 512), F32), jax.ShapeDtypeStruct((1, 512), F32),
                   jax.ShapeDtypeStruct((1, GP), F32), jax.ShapeDtypeStruct((1, GP), F32)],
        scratch_shapes=[pltpu.VMEM((tc, GP), F32), pltpu.VMEM((tc, GP), F32), pltpu.VMEM((tc + 8, GP), F32),
                        pltpu.VMEM((tc + 8, GP), F32), pltpu.VMEM((8, GP), F32), pltpu.VMEM((8, GP), F32)],
        compiler_params=_cparams("arbitrary"),
    )(dys, y, u, xr, xi, wb, wc, tabs_rev, dskip, grp_d, b_glu, grp_e)


def _swap_halves(b):
    lane = lax.broadcasted_iota(jnp.int32, b.shape, 1)
    return jnp.where(lane < 32, pltpu.roll(b, 96, 1), pltpu.roll(b, 32, 1))


def _rope_tables(pos_ref, invf_ref, sgn_ref):
    ang = pos_ref[...].astype(F32) * invf_ref[...]
    return jnp.cos(ang), jnp.sin(ang) * sgn_ref[...]


def _mla_pre_fwd(lat, pos, invf, sgn, gqa, gkva, gq, gk, w_qb_p, w_kvb, t):
    l = lat.shape[0]

    def body(lat_ref, pos_ref, invf_ref, sgn_ref, gqa_ref, gkva_ref, gq_ref, gk_ref, wq_ref, wkv_ref, q_ref, k_ref, v_ref):
        cs, sn = _rope_tables(pos_ref, invf_ref, sgn_ref)
        ql = _rms_fwd(lat_ref[:, 0:Q_LORA], gqa_ref[...], Q_LORA)
        ckn = _rms_fwd(lat_ref[:, Q_LORA:Q_LORA + KV_LORA], gkva_ref[...], KV_LORA)
        kpe = lat_ref[:, 640:768]
        q0 = _mm(ql, wq_ref[...])
        cknb = ckn.astype(BF16)
        kv = jnp.concatenate([_mm(cknb, wkv_ref[s]) for s in range(4)], axis=-1)
        for h in range(N_HEADS):
            q1 = _rms_fwd(q0[:, HEAD_PAD * h:HEAD_PAD * (h + 1)], gq_ref[...], QK_HEAD)
            b = q1[:, 128:256]
            q_ref[h, :, 0:128] = (q1[:, 0:128] * ATT_SCALE).astype(BF16)
            q_ref[h, :, 128:256] = ((b * cs + _swap_halves(b) * sn) * ATT_SCALE).astype(BF16)
            k0 = jnp.concatenate([kv[:, 256 * h:256 * h + 128], kpe], axis=-1)
            k1 = _rms_fwd(k0, gk_ref[...], QK_HEAD)
            b = k1[:, 128:256]
            k_ref[h, :, 0:128] = k1[:, 0:128].astype(BF16)
            k_ref[h, :, 128:256] = (b * cs + _swap_halves(b) * sn).astype(BF16)
            v_ref[h] = kv[:, 256 * h + 128:256 * h + 256].astype(BF16)

    heads = lambda w: pl.BlockSpec((N_HEADS, t, w), lambda i: (0, i, 0))
    return pl.pallas_call(
        body, name="mla_pre_fwd", grid=(l // t,),
        in_specs=[_rows(t, LAT_W), _rows(t, 1), _resident((1, 128)), _resident((1, 128)), _resident((1, Q_LORA)),
                  _resident((1, KV_LORA)), _resident((1, HEAD_PAD)), _resident((1, HEAD_PAD)),
                  _resident((Q_LORA, N_HEADS * HEAD_PAD)), _member_block("w_kv_b")],
        out_specs=[heads(HEAD_PAD), heads(HEAD_PAD), heads(V_HEAD)],
        out_shape=[jax.ShapeDtypeStruct((N_HEADS, l, HEAD_PAD), BF16), jax.ShapeDtypeStruct((N_HEADS, l, HEAD_PAD), BF16),
                   jax.ShapeDtypeStruct((N_HEADS, l, V_HEAD), BF16)],
        compiler_params=_cparams("parallel"),
    )(lat, pos, invf, sgn, gqa, gkva, gq, gk, w_qb_p, w_kvb)


def _mla_pre_bwd(lat, pos, invf, sgn, gqa, gkva, gq, gk, w_qb_p, w_kvb, dq, dk, dv, t, token):
    l = lat.shape[0]

    def body(lat_ref, pos_ref, invf_ref, sgn_ref, gqa_ref, gkva_ref, gq_ref, gk_ref, wq_ref, wkv_ref, dq_ref, dk_ref, dv_ref,
             token_ref, dlat_ref, ql_ref, dq0_ref, ckn_ref, dkv_ref, ggqa_ref, ggkva_ref, ggq_ref, ggk_ref):
        first = pl.program_id(0) == 0
        cs, sn = _rope_tables(pos_ref, invf_ref, sgn_ref)
        q_lat = lat_ref[:, 0:Q_LORA]
        c_kv = lat_ref[:, Q_LORA:Q_LORA + KV_LORA]
        kpe = lat_ref[:, 640:768]
        ql = _rms_fwd(q_lat, gqa_ref[...], Q_LORA)
        ckn = _rms_fwd(c_kv, gkva_ref[...], KV_LORA)
        ql_ref[...] = ql.astype(BF16)
        ckn_ref[...] = ckn.astype(BF16)
        q0 = _mm(ql, wq_ref[...])
        cknb = ckn.astype(BF16)
        kv = jnp.concatenate([_mm(cknb, wkv_ref[s]) for s in range(4)], axis=-1)
        dkpe = jnp.zeros_like(kpe)
        ggq = jnp.zeros((1, HEAD_PAD), F32)
        ggk = jnp.zeros((1, HEAD_PAD), F32)

        def unrope(d):
            b = d[:, 128:256]
            return jnp.concatenate([d[:, 0:128], b * cs + _swap_halves(b * sn)], axis=-1)

        for h in range(N_HEADS):
            dq1 = unrope(dq_ref[h] * ATT_SCALE)
            dq0h, gq_rows = _rms_bwd(q0[:, HEAD_PAD * h:HEAD_PAD * (h + 1)], gq_ref[...], dq1, QK_HEAD)
            ggq = ggq + _colsum(gq_rows)
            dq0_ref[:, HEAD_PAD * h:HEAD_PAD * (h + 1)] = dq0h.astype(BF16)
            k0 = jnp.concatenate([kv[:, 256 * h:256 * h + 128], kpe], axis=-1)
            dk0, gk_rows = _rms_bwd(k0, gk_ref[...], unrope(dk_ref[h]), QK_HEAD)
            ggk = ggk + _colsum(gk_rows)
            dkpe = dkpe + dk0[:, 128:256]
            dkv_ref[:, 256 * h:256 * h + 128] = dk0[:, 0:128].astype(BF16)
            dkv_ref[:, 256 * h + 128:256 * h + 256] = dv_ref[h].astype(BF16)
        dql = _mm_nt(dq0_ref[...], wq_ref[...])
        dckn = sum(_mm_nt(dkv_ref[:, 512 * s:512 * (s + 1)], wkv_ref[s]) for s in range(4))
        dq_lat, gqa_rows = _rms_bwd(q_lat, gqa_ref[...], dql, Q_LORA)
        dc_kv, gkva_rows = _rms_bwd(c_kv, gkva_ref[...], dckn, KV_LORA)
        dlat_ref[:, 0:Q_LORA] = dq_lat.astype(BF16)
        dlat_ref[:, Q_LORA:Q_LORA + KV_LORA] = dc_kv.astype(BF16)
        dlat_ref[:, 640:768] = dkpe.astype(BF16)
        _accumulate(ggqa_ref, _colsum(gqa_rows), first)
        _accumulate(ggkva_ref, _colsum(gkva_rows), first)
        _accumulate(ggq_ref, ggq, first)
        _accumulate(ggk_ref, ggk, first)

    heads = lambda w: pl.BlockSpec((N_HEADS, t, w), lambda i: (0, i, 0))
    acc = lambda w: pl.BlockSpec((1, w), lambda i: (0, 0))
    return pl.pallas_call(
        body, name="mla_pre_bwd", grid=(l // t,),
        in_specs=[_rows(t, LAT_W), _rows(t, 1), _resident((1, 128)), _resident((1, 128)), _resident((1, Q_LORA)),
                  _resident((1, KV_LORA)), _resident((1, HEAD_PAD)), _resident((1, HEAD_PAD)),
                  _resident((Q_LORA, N_HEADS * HEAD_PAD)), _member_block("w_kv_b"),
                  heads(HEAD_PAD), heads(HEAD_PAD), heads(V_HEAD), ANY],
        out_specs=[_rows(t, LAT_W), _rows(t, Q_LORA), _rows(t, N_HEADS * HEAD_PAD), _rows(t, KV_LORA), _rows(t, N_HEADS * 256),
                   acc(Q_LORA), acc(KV_LORA), acc(HEAD_PAD), acc(HEAD_PAD)],
        out_shape=[jax.ShapeDtypeStruct((l, LAT_W), BF16), jax.ShapeDtypeStruct((l, Q_LORA), BF16),
                   jax.ShapeDtypeStruct((l, N_HEADS * HEAD_PAD), BF16), jax.ShapeDtypeStruct((l, KV_LORA), BF16),
                   jax.ShapeDtypeStruct((l, N_HEADS * 256), BF16), jax.ShapeDtypeStruct((1, Q_LORA), F32),
                   jax.ShapeDtypeStruct((1, KV_LORA), F32), jax.ShapeDtypeStruct((1, HEAD_PAD), F32),
                   jax.ShapeDtypeStruct((1, HEAD_PAD), F32)],
        compiler_params=_cparams("arbitrary"),
    )(lat, pos, invf, sgn, gqa, gkva, gq, gk, w_qb_p, w_kvb, dq, dk, dv, token)


def _causal(s, transposed):
    row = lax.broadcasted_iota(jnp.int32, s.shape, 0)
    col = lax.broadcasted_iota(jnp.int32, s.shape, 1)
    keep = (row <= col) if transposed else (col <= row)
    return jnp.where(keep, s, -jnp.inf)


def _as_row(col):
    n = col.shape[0]
    row = lax.broadcasted_iota(jnp.int32, (n, n), 0)
    lane = lax.broadcasted_iota(jnp.int32, (n, n), 1)
    return jnp.sum(jnp.where(row == lane, col, 0.0), axis=0, keepdims=True)


def _attn_fwd(q, k, v, tq):
    l = q.shape[1]

    hb = 2

    def body(q_ref, k_ref, v_ref, o_ref, lse_ref):
        qi = pl.program_id(1)
        qs = [q_ref[a] for a in range(hb)]

        def step(kb, carry, masked):
            rows = pl.ds(pl.multiple_of(kb * tq, tq), tq)
            out = []
            for a, (m, den, acc) in enumerate(carry):
                s = _mm_nt(qs[a], k_ref[a, rows, :])
                if masked:
                    s = _causal(s, False)
                m_new = jnp.maximum(m, jnp.max(s, axis=-1, keepdims=True))
                alpha = jnp.exp(m - m_new)
                p = jnp.exp(s - m_new)
                den = alpha * den + jnp.sum(p, axis=-1, keepdims=True)
                acc = alpha * acc + _mm(p, v_ref[a, rows, :])
                out.append((m_new, den, acc))
            return tuple(out)

        init = tuple((jnp.full((tq, 1), -jnp.inf, F32), jnp.zeros((tq, 1), F32), jnp.zeros((tq, V_HEAD), F32))
                     for _ in range(hb))
        carry = lax.fori_loop(0, qi, lambda kb, c: step(kb, c, False), init)
        for a, (m, den, acc) in enumerate(step(qi, carry, True)):
            o_ref[:, V_HEAD * a:V_HEAD * (a + 1)] = acc / den
            lse_ref[a, 0] = _as_row(m + jnp.log(den))

    return pl.pallas_call(
        body, name="attn_fwd", grid=(N_HEADS // hb, l // tq),
        in_specs=[pl.BlockSpec((hb, tq, HEAD_PAD), lambda h, i: (h, i, 0)), pl.BlockSpec((hb, l, HEAD_PAD), lambda h, i: (h, 0, 0)),
                  pl.BlockSpec((hb, l, V_HEAD), lambda h, i: (h, 0, 0))],
        out_specs=[pl.BlockSpec((tq, hb * V_HEAD), lambda h, i: (i, h)), pl.BlockSpec((hb, 1, 1, tq), lambda h, i: (h, i, 0, 0))],
        out_shape=[jax.ShapeDtypeStruct((l, N_HEADS * V_HEAD), F32), jax.ShapeDtypeStruct((N_HEADS, l // tq, 1, tq), F32)],
        compiler_params=_cparams("parallel", "arbitrary"),
    )(q, k, v)


def _attn_bwd(q, k, v, o, do, lse_t, tq, token):
    l = q.shape[1]
    nq = l // tq

    hb = 1

    def body(q_ref, k_ref, v_ref, o_ref, do_ref, lse_ref, token_ref, dq_ref, dk_ref, dv_ref):
        ki = pl.program_id(1)

        @pl.when(ki == 0)
        def _():
            dq_ref[...] = jnp.zeros_like(dq_ref)

        kblks = [k_ref[a] for a in range(hb)]
        vblks = [v_ref[a] for a in range(hb)]
        ones = jnp.ones((8, V_HEAD), BF16)

        def step(qb, carry, masked):
            rows = pl.ds(pl.multiple_of(qb * tq, tq), tq)
            out = []
            for a, (dk, dv) in enumerate(carry):
                cols = slice(V_HEAD * a, V_HEAD * (a + 1))
                qblk = q_ref[a, rows, :]
                dov = do_ref[rows, cols]
                dob = dov.astype(BF16)
                delta = sum(_mm_nt(ones, part) for part in _three_bf16(dov * o_ref[rows, cols]))[0:1, :]
                st = _mm_nt(kblks[a], qblk)
                if masked:
                    st = _causal(st, True)
                pt = jnp.exp(st - lse_ref[a, qb])
                dv = dv + _mm(pt, dob)
                dst = (pt * (_mm_nt(vblks[a], dob) - delta)).astype(BF16)
                dk = dk + _mm(dst, qblk)
                dq_ref[a, rows, :] += _mm_tn(dst, kblks[a])
                out.append((dk, dv))
            return tuple(out)

        init = tuple((jnp.zeros((tq, HEAD_PAD), F32), jnp.zeros((tq, V_HEAD), F32)) for _ in range(hb))
        carry = lax.fori_loop(ki + 1, nq, lambda qb, c: step(qb, c, False), step(ki, init, True))
        for a, (dk, dv) in enumerate(carry):
            dk_ref[a] = dk
            dv_ref[a] = dv

    return pl.pallas_call(
        body, name="attn_bwd", grid=(N_HEADS // hb, nq),
        in_specs=[pl.BlockSpec((hb, l, HEAD_PAD), lambda h, i: (h, 0, 0)), pl.BlockSpec((hb, tq, HEAD_PAD), lambda h, i: (h, i, 0)),
                  pl.BlockSpec((hb, tq, V_HEAD), lambda h, i: (h, i, 0)), pl.BlockSpec((l, hb * V_HEAD), lambda h, i: (0, h)),
                  pl.BlockSpec((l, hb * V_HEAD), lambda h, i: (0, h)), pl.BlockSpec((hb, nq, 1, tq), lambda h, i: (h, 0, 0, 0)), ANY],
        out_specs=[pl.BlockSpec((hb, l, HEAD_PAD), lambda h, i: (h, 0, 0)), pl.BlockSpec((hb, tq, HEAD_PAD), lambda h, i: (h, i, 0)),
                   pl.BlockSpec((hb, tq, V_HEAD), lambda h, i: (h, i, 0))],
        out_shape=[jax.ShapeDtypeStruct((N_HEADS, l, HEAD_PAD), F32), jax.ShapeDtypeStruct((N_HEADS, l, HEAD_PAD), F32),
                   jax.ShapeDtypeStruct((N_HEADS, l, V_HEAD), F32)],
        compiler_params=_cparams("parallel", "arbitrary"),
    )(q, k, v, o, do, lse_t, token)


def _row_shards_mm(a, w_ref):
    a = a.astype(BF16)
    return sum(_mm(a[:, 256 * j:256 * (j + 1)], w_ref[j]) for j in range(4))


def _row_shards_mm_nt(a, w_ref):
    a = a.astype(BF16)
    return jnp.concatenate([_mm_nt(a, w_ref[j]) for j in range(4)], axis=-1)


def _merge_fwd(attn, y_ssm, gs, gm, x, grp_a, t):
    l = x.shape[0]

    def body(attn_ref, ys_ref, gs_ref, gm_ref, x_ref, wo_ref, wout_ref, ym_ref, mixed_ref, h_ref):
        y_mla = _row_shards_mm(attn_ref[...], wo_ref)
        ym_ref[...] = y_mla
        mixed = (_sigmoid(gs_ref[...]) * ys_ref[...] + _sigmoid(gm_ref[...]) * y_mla).astype(BF16)
        mixed_ref[...] = mixed
        h_ref[...] = x_ref[...] + _row_shards_mm(mixed, wout_ref)

    r = lambda: _rows(t, D_MODEL)
    return pl.pallas_call(
        body, name="merge_fwd", grid=(l // t,),
        in_specs=[r(), r(), r(), r(), r(), _member_block("w_o_mla"), _member_block("w_out")],
        out_specs=[r(), r(), r()],
        out_shape=[jax.ShapeDtypeStruct((l, D_MODEL), F32), jax.ShapeDtypeStruct((l, D_MODEL), BF16),
                   jax.ShapeDtypeStruct((l, D_MODEL), F32)],
        compiler_params=_cparams("parallel"),
    )(attn, y_ssm, gs, gm, x, grp_a, grp_a)


def _merge_bwd(dh, y_ssm, y_mla, gs, gm, grp_a, t):
    l = dh.shape[0]

    def body(dh_ref, ys_ref, ym_ref, gs_ref, gm_ref, wo_ref, wout_ref, dys_ref, dym_ref, dgs_ref, dgm_ref, dattn_ref):
        dmixed = _row_shards_mm_nt(dh_ref[...], wout_ref)
        sg = _sigmoid(gs_ref[...])
        sm = _sigmoid(gm_ref[...])
        dys_ref[...] = (dmixed * sg).astype(BF16)
        dgs_ref[...] = (dmixed * ys_ref[...] * sg * (1.0 - sg)).astype(BF16)
        dym = (dmixed * sm).astype(BF16)
        dym_ref[...] = dym
        dgm_ref[...] = (dmixed * ym_ref[...] * sm * (1.0 - sm)).astype(BF16)
        dattn_ref[...] = _row_shards_mm_nt(dym, wo_ref)

    r = lambda: _rows(t, D_MODEL)
    bf = jax.ShapeDtypeStruct((l, D_MODEL), BF16)
    return pl.pallas_call(
        body, name="merge_bwd", grid=(l // t,),
        in_specs=[r(), r(), r(), r(), r(), _member_block("w_o_mla"), _member_block("w_out")],
        out_specs=[r(), r(), r(), r(), r()],
        out_shape=[bf, bf, bf, bf, jax.ShapeDtypeStruct((l, D_MODEL), F32)],
        compiler_params=_cparams("parallel"),
    )(dh, y_ssm, y_mla, gs, gm, grp_a, grp_a)


def _mlp_fwd_bwd(h, tgt, g2, grp_a, t):
    l = h.shape[0]

    def body(h_ref, tgt_ref, g_ref, wu_ref, wd_ref, dh_ref, hn_ref, da_ref, hid_ref, dout_ref, loss_ref, dg_ref):
        first = pl.program_id(0) == 0
        hv = h_ref[...]
        g = g_ref[...]
        hn = _rms_fwd(hv, g, D_MODEL).astype(BF16)
        hn_ref[...] = hn
        out = hv
        relus = []
        for s in range(4):
            cols = slice(1024 * s, 1024 * (s + 1))
            relu = jnp.maximum(_mm(hn, wu_ref[s]), 0.0)
            relus.append(relu)
            hid = (relu * relu).astype(BF16)
            hid_ref[:, cols] = hid
            out = out + _mm(hid, wd_ref[s])
        err = out - tgt_ref[...]
        _accumulate(loss_ref, jnp.full((8, 128), jnp.sum(err * err) * (0.5 / D_MODEL), F32), first)
        dout = err * (1.0 / D_MODEL)
        doutb = dout.astype(BF16)
        dout_ref[...] = doutb
        dhn = jnp.zeros_like(hv)
        for s in range(4):
            da = (_mm_nt(doutb, wd_ref[s]) * (2.0 * relus[s])).astype(BF16)
            da_ref[:, 1024 * s:1024 * (s + 1)] = da
            dhn = dhn + _mm_nt(da, wu_ref[s])
        dx, dg_rows = _rms_bwd(hv, g, dhn, D_MODEL)
        dh_ref[...] = dout + dx
        _accumulate(dg_ref, _colsum(dg_rows), first)

    r = lambda w: _rows(t, w)
    return pl.pallas_call(
        body, name="mlp_fwd_bwd", grid=(l // t,),
        in_specs=[r(D_MODEL), r(D_MODEL), _resident((1, D_MODEL)), _member_block("w_up"), _member_block("w_down")],
        out_specs=[r(D_MODEL), r(D_MODEL), r(D_FF), r(D_FF), r(D_MODEL), pl.BlockSpec((8, 128), lambda i: (0, 0)),
                   pl.BlockSpec((1, D_MODEL), lambda i: (0, 0))],
        out_shape=[jax.ShapeDtypeStruct((l, D_MODEL), F32), jax.ShapeDtypeStruct((l, D_MODEL), BF16),
                   jax.ShapeDtypeStruct((l, D_FF), BF16), jax.ShapeDtypeStruct((l, D_FF), BF16),
                   jax.ShapeDtypeStruct((l, D_MODEL), BF16), jax.ShapeDtypeStruct((8, 128), F32),
                   jax.ShapeDtypeStruct((1, D_MODEL), F32)],
        compiler_params=_cparams("arbitrary"),
    )(h, tgt, g2, grp_a, grp_a)


def _wgrad(a, b, name):
    l, m = a.shape
    n = b.shape[1]
    bm = m if m <= 512 else 512
    bl = min(l, 2048 if n <= 1024 else 1024)

    def body(a_ref, b_ref, o_ref):
        _accumulate(o_ref, _mm_tn(a_ref[...], b_ref[...]), pl.program_id(1) == 0)

    return pl.pallas_call(
        body, name=name, grid=(m // bm, l // bl),
        in_specs=[pl.BlockSpec((bl, bm), lambda i, j: (j, i)), pl.BlockSpec((bl, n), lambda i, j: (j, 0))],
        out_specs=pl.BlockSpec((bm, n), lambda i, j: (i, 0)),
        out_shape=jax.ShapeDtypeStruct((m, n), F32),
        compiler_params=_cparams("parallel", "arbitrary"),
    )(a, b)


def _wgrad_into(a, b, member, cut, dest=None):
    group, off, rs, cs = _place_in_group(member)
    l = a.shape[0]
    bm = min(rs, 512)
    bl = min(l, 2048)
    nb = rs // bm
    if cut == "row":
        a_spec = pl.BlockSpec((bl, bm), lambda j, i, k: (k, j * nb + i))
        b_spec = pl.BlockSpec((bl, cs), lambda j, i, k: (k, 0))
    else:
        a_spec = pl.BlockSpec((bl, bm), lambda j, i, k: (k, i))
        b_spec = pl.BlockSpec((bl, cs), lambda j, i, k: (k, j))

    def body(a_ref, b_ref, *rest):
        o_ref = rest[-1]
        part = _mm_tn(a_ref[...], b_ref[...])

        @pl.when(pl.program_id(2) == 0)
        def _():
            o_ref[0] = part

        @pl.when(pl.program_id(2) != 0)
        def _():
            o_ref[0] += part

    operands, in_specs, aliases = [a, b], [a_spec, b_spec], {}
    if dest is not None:
        operands.append(dest)
        in_specs.append(ANY)
        aliases = {2: 0}
    return pl.pallas_call(
        body, name="wgrad_" + member, grid=(4, nb, l // bl), in_specs=in_specs,
        out_specs=pl.BlockSpec((1, bm, cs), lambda j, i, k: (j, off // bm + i, 0)),
        out_shape=jax.ShapeDtypeStruct((4, _group_rows(group), cs), F32), input_output_aliases=aliases,
        compiler_params=_cparams("parallel", "parallel", "arbitrary"),
    )(*operands)


def _adamw(w, g, m, v, name, g_off=0):
    r, c = w.shape
    br = r
    for cand in (256, 128, 64, 32, 16, 8):
        if r % cand == 0 and g_off % cand == 0:
            br = cand
            break

    def body(w_ref, g_ref, m_ref, v_ref, go_ref, d_ref, nm_ref, nv_ref):
        gv = g_ref[...]
        go_ref[...] = gv
        nm = ADAM_B1 * m_ref[...] + (1.0 - ADAM_B1) * gv
        nv = ADAM_B2 * v_ref[...] + (1.0 - ADAM_B2) * (gv * gv)
        m_hat = nm / (1.0 - ADAM_B1 ** ADAM_STEP)
        v_hat = nv / (1.0 - ADAM_B2 ** ADAM_STEP)
        d_ref[...] = -ADAM_LR * (m_hat / (jnp.sqrt(v_hat) + ADAM_EPS) + ADAM_WD * w_ref[...])
        nm_ref[...] = nm
        nv_ref[...] = nv

    spec = lambda: pl.BlockSpec((br, c), lambda i: (i, 0))
    g_spec = pl.BlockSpec((br, c), lambda i: (g_off // br + i, 0))
    shp = jax.ShapeDtypeStruct((r, c), F32)
    return pl.pallas_call(
        body, name=name, grid=(r // br,), in_specs=[spec(), g_spec, spec(), spec()],
        out_specs=[spec(), spec(), spec(), spec()], out_shape=[shp, shp, shp, shp], compiler_params=_cparams("parallel"),
    )(w, g, m, v)


def _place():
    return lax.axis_index("x"), lax.axis_index("y"), lax.axis_index("c")


def _other_chips(x, y):
    return [(1 - x, y), (x, 1 - y), (1 - x, 1 - y)]


ANY = pl.BlockSpec(memory_space=pl.ANY)


def _gather_weights(bufs):
    n = len(bufs)

    def body(*refs):
        outs, send_sems, recv_sems = refs[n:2 * n], refs[2 * n], refs[2 * n + 1]
        x, y, c = _place()
        chips = _other_chips(x, y)

        def part(g, px, py, pc):
            half = outs[g].shape[1] // 2
            return outs[g].at[2 * px + py, pl.ds(pl.multiple_of(pc * half, 16), half), :]

        def copy(k, src, dst, to):
            return pltpu.make_async_remote_copy(src_ref=src, dst_ref=dst, send_sem=send_sems.at[k], recv_sem=recv_sems.at[k],
                                                device_id=to, device_id_type=MESH)

        first = [copy(6 * g + j, part(g, x, y, c), part(g, x, y, c), (*chip, c)) for g in range(n) for j, chip in enumerate(chips)]
        for cp in first:
            cp.start()
        passed = []
        for g in range(n):
            for j, chip in enumerate(chips):
                landed = part(g, *chip, c)
                copy(6 * g + j, landed, landed, (x, y, c)).wait_recv()
                passed.append(copy(6 * g + 3 + j, landed, landed, (x, y, 1 - c)))
                passed[-1].start()
        for g in range(n):
            for j, chip in enumerate(chips):
                other = part(g, *chip, 1 - c)
                copy(6 * g + 3 + j, other, other, (x, y, c)).wait_recv()
        for cp in first + passed:
            cp.wait_send()

    return pl.pallas_call(
        body, name="gather_weights", in_specs=[ANY] * n, out_specs=[ANY] * n,
        out_shape=[jax.ShapeDtypeStruct(b.shape, b.dtype) for b in bufs], input_output_aliases={g: g for g in range(n)},
        scratch_shapes=[pltpu.SemaphoreType.DMA((6 * n,)), pltpu.SemaphoreType.DMA((6 * n,))],
    )(*bufs)


def _cast_shards(shards, group, place):
    width, members = GROUPS[group]
    rows = _group_rows(group)

    def body(place_ref, *refs):
        out = refs[-1]
        off = 0
        for ref, (_, r) in zip(refs[:-1], members):
            out[0, off:off + r, :] = ref[...].astype(BF16)
            off += r

    grid_spec = pltpu.PrefetchScalarGridSpec(
        num_scalar_prefetch=1, grid=(1,),
        in_specs=[pl.BlockSpec((r, width), lambda i, p: (0, 0)) for _, r in members],
        out_specs=pl.BlockSpec((1, rows, width), lambda i, p: (p[0], 0, 0)))
    return pl.pallas_call(
        body, name="cast_shards_" + group, grid_spec=grid_spec, out_shape=jax.ShapeDtypeStruct((4, rows, width), BF16),
        compiler_params=_cparams("arbitrary"),
    )(place, *[shards[name] for name, _ in members])


def _block_rows(h):
    return next(cand for cand in (256, 192, 128, 64, 32, 16) if h % cand == 0)


def _add_pair(buf, got, place, name):
    n, h, w = got.shape
    bh = _block_rows(h)
    nb = h // bh

    def body(place_ref, a_ref, b_ref, s_ref, sb_ref):
        s = a_ref[...] + b_ref[...]
        s_ref[...] = s
        sb_ref[...] = s.astype(BF16)

    spec = lambda: pl.BlockSpec((1, bh, w), lambda j, i, p: (j, i, 0))
    grid_spec = pltpu.PrefetchScalarGridSpec(
        num_scalar_prefetch=1, grid=(n, nb),
        in_specs=[pl.BlockSpec((1, bh, w), lambda j, i, p: (j, p[1] * nb + i, 0)), spec()], out_specs=[spec(), spec()])
    return pl.pallas_call(
        body, name=name, grid_spec=grid_spec,
        out_shape=[jax.ShapeDtypeStruct(got.shape, F32), jax.ShapeDtypeStruct(got.shape, BF16)],
        compiler_params=_cparams("parallel", "parallel"),
    )(place, buf, got)


def _add_received(pair, got, place, name):
    _, h, w = pair.shape
    bh = _block_rows(h)
    nb = h // bh

    def body(place_ref, own_ref, got_ref, o_ref):
        o_ref[...] = ((own_ref[0] + got_ref[0].astype(F32)) + got_ref[1].astype(F32)) + got_ref[2].astype(F32)

    grid_spec = pltpu.PrefetchScalarGridSpec(
        num_scalar_prefetch=1, grid=(nb,),
        in_specs=[pl.BlockSpec((1, bh, w), lambda i, p: (p[0], i, 0)), pl.BlockSpec((3, bh, w), lambda i, p: (0, i, 0))],
        out_specs=pl.BlockSpec((bh, w), lambda i, p: (p[1] * nb + i, 0)))
    return pl.pallas_call(
        body, name=name, grid_spec=grid_spec, out_shape=jax.ShapeDtypeStruct((2 * h, w), F32),
        compiler_params=_cparams("parallel"),
    )(place, pair, got)


def _swap_reduced_halves(bufs):
    n = len(bufs)

    def body(*refs):
        outs, send_sems, recv_sems = refs[n:2 * n], refs[2 * n], refs[2 * n + 1]
        x, y, c = _place()
        copies = []
        for g in range(n):
            half = outs[g].shape[0] // 2
            own = outs[g].at[pl.ds(pl.multiple_of(c * half, 8), half), :]
            copies.append(pltpu.make_async_remote_copy(src_ref=own, dst_ref=own, send_sem=send_sems.at[g],
                                                       recv_sem=recv_sems.at[g], device_id=(x, y, 1 - c), device_id_type=MESH))
        for cp in copies:
            cp.start()
        for g in range(n):
            half = outs[g].shape[0] // 2
            other = outs[g].at[pl.ds(pl.multiple_of((1 - c) * half, 8), half), :]
            pltpu.make_async_remote_copy(src_ref=other, dst_ref=other, send_sem=send_sems.at[g], recv_sem=recv_sems.at[g],
                                         device_id=(x, y, 1 - c), device_id_type=MESH).wait_recv()
        for cp in copies:
            cp.wait_send()

    return pl.pallas_call(
        body, name="swap_reduced_halves", in_specs=[ANY] * n, out_specs=[ANY] * n,
        out_shape=[jax.ShapeDtypeStruct(b.shape, b.dtype) for b in bufs], input_output_aliases={g: g for g in range(n)},
        scratch_shapes=[pltpu.SemaphoreType.DMA((n,)), pltpu.SemaphoreType.DMA((n,))],
    )(*bufs)


HBM = pl.BlockSpec(memory_space=pltpu.HBM)
SEM = pl.BlockSpec(memory_space=pltpu.SEMAPHORE)


def _copies_start(name, bufs, n_copies, plan, after=None):
    n = len(bufs)
    extra = [] if after is None else [after]

    def body(*refs):
        sems = refs[n + len(extra):n + len(extra) + 2 * n_copies]
        x, y, c = _place()
        for i, (src, dst, dev) in enumerate(plan(refs[:n], x, y, c)):
            pltpu.make_async_remote_copy(src_ref=src, dst_ref=dst, send_sem=sems[i], recv_sem=sems[n_copies + i],
                                         device_id=dev, device_id_type=MESH).start()
        token = refs[-1]
        token[...] = jnp.zeros_like(token)

    out = pl.pallas_call(
        body, name=name,
        out_shape=[pltpu.SemaphoreType.DMA(())] * (2 * n_copies) + [pltpu.HBM(b.shape, b.dtype) for b in bufs]
        + [jax.ShapeDtypeStruct((8, 128), F32)],
        in_specs=[HBM] * n + [ANY] * len(extra),
        out_specs=[SEM] * (2 * n_copies) + [HBM] * n + [pl.BlockSpec(memory_space=pltpu.VMEM)],
        input_output_aliases={i: 2 * n_copies + i for i in range(n)},
        compiler_params=pltpu.CompilerParams(has_side_effects=pltpu.SideEffectType.DATAFLOW_SIDE_EFFECTING),
    )(*[pltpu.with_memory_space_constraint(b, pltpu.HBM) for b in bufs], *extra)
    return list(out[:2 * n_copies]), list(out[2 * n_copies:-1]), out[-1]


def _copies_wait(name, bufs, sems, after, plan):
    n = len(bufs)
    k = len(sems) // 2

    def body(*refs):
        sem_refs = refs[n:n + 2 * k]
        x, y, c = _place()
        for i, (sent, landed, dev) in enumerate(plan(refs[:n], x, y, c)):
            cp = pltpu.make_async_remote_copy(src_ref=sent, dst_ref=landed, send_sem=sem_refs[i], recv_sem=sem_refs[k + i],
                                              device_id=dev, device_id_type=MESH)
            cp.wait_send()
            cp.wait_recv()

    return pl.pallas_call(
        body, name=name, out_shape=[pltpu.HBM(b.shape, b.dtype) for b in bufs],
        in_specs=[HBM] * n + [SEM] * (2 * k) + [ANY], out_specs=[HBM] * n, input_output_aliases={i: i for i in range(n)},
        compiler_params=pltpu.CompilerParams(has_side_effects=pltpu.SideEffectType.DATAFLOW_SIDE_EFFECTING),
    )(*bufs, *sems, after)


def _row_half(ref, which, axis):
    half = ref.shape[axis] // 2
    rows = pl.ds(pl.multiple_of(which * half, 8), half)
    return ref.at[rows, :] if axis == 0 else ref.at[:, rows, :]


class _SplitGather:
    def __init__(self, own, after):
        self.n = len(own)
        self.state = _copies_start("gather_start", own, 3 * self.n, self._sent, after)

    @staticmethod
    def _sent(refs, x, y, c):
        return [(w.at[2 * x + y], w.at[2 * x + y], (px, py, c)) for w in refs for px, py in _other_chips(x, y)]

    @staticmethod
    def _landed(refs, x, y, c):
        return [(w.at[2 * x + y], w.at[2 * px + py], (px, py, c)) for w in refs for px, py in _other_chips(x, y)]

    def token(self):
        return self.state[2]

    def wait(self, which, name, after):
        sems, bufs, _ = self.state
        k = 3 * self.n
        mine = [sems[3 * i + j] for i in which for j in range(3)] + [sems[k + 3 * i + j] for i in which for j in range(3)]
        return _copies_wait(name, [bufs[i] for i in which], mine, after, self._landed)


class _SplitReduction:
    def __init__(self, tag, groups, place):
        self.tag, self.groups, self.place = tag, groups, place

    def start_pair(self, bufs):
        n = len(bufs)
        lands = [lax.empty((4, b.shape[1] // 2, b.shape[2]), F32) for b in bufs]
        plan = lambda refs, x, y, c: [(_row_half(refs[i], 1 - c, 1), refs[n + i], (x, y, 1 - c)) for i in range(n)]
        self._pair = (_copies_start("pair_%s_start" % self.tag, bufs + lands, n, plan), plan, n)
        return self._pair[0][2]

    def pair_done_start_scatter(self, after):
        (sems, bufs, _), plan, n = self._pair
        out = _copies_wait("pair_%s_wait" % self.tag, bufs, sems, after, plan)
        pairs = [_add_pair(out[i], out[n + i], self.place, "add_pair_" + g) for i, g in enumerate(self.groups)]
        self._pair_f32 = [p[0] for p in pairs]
        lands = [lax.empty((3,) + p[1].shape[1:], BF16) for p in pairs]
        plan = lambda refs, x, y, c: [(refs[i].at[2 * px + py], refs[n + i].at[j], (px, py, c))
                                      for i in range(n) for j, (px, py) in enumerate(_other_chips(x, y))]
        self._scatter = (_copies_start("scatter_%s_start" % self.tag, [p[1] for p in pairs] + lands, 3 * n, plan), plan, n)
        return self._scatter[0][2]

    def scatter_done(self, after):
        (sems, bufs, _), plan, n = self._scatter
        out = _copies_wait("scatter_%s_wait" % self.tag, bufs, sems, after, plan)
        return [_add_received(self._pair_f32[i], out[n + i], self.place, "add_received_" + g)
                for i, g in enumerate(self.groups)]

    def start_join(self, halves):
        n = len(halves)
        sent = lambda refs, x, y, c: [(_row_half(r, c, 0), _row_half(r, c, 0), (x, y, 1 - c)) for r in refs]
        landed = lambda refs, x, y, c: [(_row_half(r, c, 0), _row_half(r, 1 - c, 0), (x, y, 1 - c)) for r in refs]
        self._join = (_copies_start("join_%s_start" % self.tag, halves, n, sent), landed)
        return self._join[0][2]

    def join_done(self, after):
        (sems, bufs, _), landed = self._join
        return _copies_wait("join_%s_wait" % self.tag, bufs, sems, after, landed)


def _all_sum_small(mine):
    rows, w = mine.shape

    def body(in_ref, out_ref, sibling, pair, chips, send_sems, recv_sems):
        x, y, c = _place()
        swap = pltpu.make_async_remote_copy(src_ref=in_ref, dst_ref=sibling, send_sem=send_sems.at[0], recv_sem=recv_sems.at[0],
                                            device_id=(x, y, 1 - c), device_id_type=MESH)
        swap.start()
        swap.wait()
        pair[...] = in_ref[...] + sibling[...]
        chip = 2 * x + y
        chips[chip] = pair[...]
        copies = [pltpu.make_async_remote_copy(src_ref=pair, dst_ref=chips.at[chip], send_sem=send_sems.at[1 + j],
                                               recv_sem=recv_sems.at[1 + j], device_id=(px, py, c), device_id_type=MESH)
                  for j, (px, py) in enumerate(_other_chips(x, y))]
        for cp in copies:
            cp.start()
        for cp in copies:
            cp.wait()
        out_ref[...] = ((chips[0] + chips[1]) + chips[2]) + chips[3]

    return pl.pallas_call(
        body, name="all_sum_small", out_shape=jax.ShapeDtypeStruct((rows, w), F32),
        in_specs=[pl.BlockSpec(memory_space=pltpu.VMEM)], out_specs=pl.BlockSpec(memory_space=pltpu.VMEM),
        scratch_shapes=[pltpu.VMEM((rows, w), F32), pltpu.VMEM((rows, w), F32), pltpu.VMEM((4, rows, w), F32),
                        pltpu.SemaphoreType.DMA((4,)), pltpu.SemaphoreType.DMA((4,))],
        compiler_params=pltpu.CompilerParams(vmem_limit_bytes=VMEM_LIMIT_V7X),
    )(mine)


def _join_column_shards(g):
    return jnp.transpose(g, (1, 0, 2)).reshape(g.shape[1], 4 * g.shape[2])


def _split_column_shards(w):
    r = w.shape[0]
    return jnp.transpose(w.reshape(r, 4, w.shape[1] // 4), (1, 0, 2))


def _small_rows(shape):
    return -(-int(np.prod(shape)) // 1024)


def _pack_small(vals):
    segs = []
    for name, shape in SMALL_WEIGHTS:
        flat = vals[name].reshape(-1)
        segs.append(jnp.pad(flat, (0, _small_rows(shape) * 1024 - flat.shape[0])))
    total = sum(s.shape[0] for s in segs) // 1024
    segs.append(jnp.zeros((-total % 8 * 1024,), F32))
    return jnp.concatenate(segs).reshape(-1, 1024)


def _unpack_small(packed):
    out, off = {}, 0
    for name, shape in SMALL_WEIGHTS:
        rows = _small_rows(shape)
        out[name] = packed[off:off + rows].reshape(-1)[:int(np.prod(shape))].reshape(shape)
        off += rows
    return out


W_IN_SHARD = D_IN // 4
W_IN_GAP = 1216


def _pad_w_in(g):
    cut = W_IN_GAP - W_IN_SHARD
    return jnp.concatenate([g[0], g[1][:, :cut], jnp.zeros((g.shape[1], D_IN_PAD - D_IN), g.dtype), g[1][:, cut:], g[2], g[3]],
                           axis=1)


def _unpad_w_in(g):
    skip = D_IN_PAD - D_IN
    second = jnp.concatenate([g[:, W_IN_SHARD:W_IN_GAP], g[:, W_IN_GAP + skip:2 * W_IN_SHARD + skip]], axis=1)
    return jnp.stack([g[:, :W_IN_SHARD], second, g[:, 2 * W_IN_SHARD + skip:3 * W_IN_SHARD + skip],
                      g[:, 3 * W_IN_SHARD + skip:]])


def _pad_heads(w):
    r = w.shape[0]
    return jnp.pad(w.reshape(r, N_HEADS, QK_HEAD), ((0, 0), (0, 0), (0, HEAD_PAD - QK_HEAD))).reshape(r, N_HEADS * HEAD_PAD)


def _unpad_heads(g):
    r = g.shape[0]
    return g.reshape(r, N_HEADS, HEAD_PAD)[:, :, :QK_HEAD].reshape(r, N_HEADS * QK_HEAD)


def _local_step(x, positions, tgt, grp_b, small, gather, red_a, red_rest):
    l = x.shape[0]
    t = min(l, 512)
    t_mlp = min(l, 256)
    tq = min(l, 1024)
    tc = min(l, 256)
    row = lambda v: v.reshape(1, -1).astype(F32)

    w_in_p = _pad_w_in(grp_b)
    g1, g2 = row(small["norm_mix"]), row(small["norm_mlp"])
    gqa, gkva = row(small["q_a_norm"]), row(small["kv_a_norm"])
    gq = jnp.pad(row(small["q_norm"]), ((0, 0), (0, HEAD_PAD - QK_HEAD)))
    gk = jnp.pad(row(small["k_norm"]), ((0, 0), (0, HEAD_PAD - QK_HEAD)))
    half = QK_ROPE // 2
    inv_freq = ROPE_THETA ** (-jnp.arange(half, dtype=F32) / half)
    invf = jnp.concatenate([inv_freq, inv_freq, jnp.zeros((64,), F32)]).reshape(1, 128)
    sgn = jnp.concatenate([-jnp.ones((half,), F32), jnp.ones((half,), F32), jnp.zeros((64,), F32)]).reshape(1, 128)
    pos = positions.reshape(l, 1)

    a_re, a_im = small["ssm_a_re"], small["ssm_a_im"]
    log_dt = small["ssm_log_dt"].reshape(SSM_GROUPS, 1)
    to_gcp = lambda b: jnp.transpose(b, (0, 2, 1)).reshape(SSM_WIDTH, SSM_STATE)
    from_gcp = lambda b: jnp.transpose(b.reshape(SSM_GROUPS, SSM_GROUP_CH, SSM_STATE), (0, 2, 1))
    b_re, b_im = to_gcp(small["ssm_b_re"]), to_gcp(small["ssm_b_im"])
    c_re, c_im = small["ssm_c_re"].reshape(SSM_WIDTH, SSM_STATE), small["ssm_c_im"].reshape(SSM_WIDTH, SSM_STATE)
    wb, wc, tabs_fwd, tabs_rev = _ssm_param_fwd(a_re, a_im, log_dt, b_re, b_im, c_re, c_im)
    dskip = row(small["ssm_d"])
    b_glu = row(small["b_glu"])

    u, lat, gs, gm = _in_proj_fwd(x, g1, w_in_p, t, gather.token())
    grp_c, grp_d, grp_e = gather.wait([0, 1, 2], "gather_cde_wait", u)
    w_qb_p = _pad_heads(_join_column_shards(grp_c))
    xr, xi, y, y_ssm = _ssm_fwd(u, wb, wc, tabs_fwd, dskip, grp_d, b_glu, grp_e, tc)
    q, k, v = _mla_pre_fwd(lat, pos, invf, sgn, gqa, gkva, gq, gk, w_qb_p, grp_d, t)
    attn, lse = _attn_fwd(q, k, v, tq)
    (grp_a,) = gather.wait([3], "gather_a_wait", attn)
    y_mla, mixed, h = _merge_fwd(attn, y_ssm, gs, gm, x, grp_a, t)
    dh, hn, da, hid, dout, loss_blk, g_norm_mlp = _mlp_fwd_bwd(h, tgt, g2, grp_a, t_mlp)

    ga = _wgrad_into(hn, da, "w_up", "col", _wgrad_into(hid, dout, "w_down", "row"))
    dys, dym, dgs, dgm, dattn = _merge_bwd(dh, y_ssm, y_mla, gs, gm, grp_a, t)
    ga = _wgrad_into(attn, dym, "w_o_mla", "row", _wgrad_into(mixed, dh, "w_out", "row", ga))

    dq, dk, dv = _attn_bwd(q, k, v, attn, dattn, lse, tq, red_a.start_pair([ga]))
    d_lat, ql, dq0, ckn, dkv, g_qa, g_kva, g_q, g_k = _mla_pre_bwd(lat, pos, invf, sgn, gqa, gkva, gq, gk, w_qb_p, grp_d,
                                                                    dq, dk, dv, t, red_a.pair_done_start_scatter(dk))
    gc = _split_column_shards(_unpad_heads(_wgrad(ql, dq0, "wgrad_q_b")))

    d_u, adj, dy, z, z2, dpre, g_b_glu, g_d, g_lr, g_li = _ssm_bwd(
        dys, y, u, xr, xi, wb, wc, tabs_rev, dskip, grp_d, b_glu, grp_e, tc)
    gd = _wgrad_into(z, dpre, "w_glu", "row", _wgrad_into(ckn, dkv, "w_kv_b", "col"))
    ge = _wgrad_into(z2, dys, "w_o_ssm", "col")
    grad_x, xn, dproj, g_norm_mix = _in_proj_bwd(x, g1, w_in_p, d_u, d_lat, dgs, dgm, dh, t)
    gb = _unpad_w_in(_wgrad(xn, dproj, "wgrad_in"))

    red_a.start_join(red_a.scatter_done(gb))
    g_wb = _wgrad_strips(u, adj, adj, "wgrad_ssm_b", 1, red_rest.start_pair([gb, gc, gd, ge]))
    g_wct = _wgrad_strips(dy, xr, xi, "wgrad_ssm_c", 0, red_rest.pair_done_start_scatter(g_wb))
    g_ar, g_ai, g_ldt, g_br, g_bi, g_cr, g_ci = _ssm_param_bwd(a_re, a_im, log_dt, b_re, b_im, g_lr, g_li, g_wb, g_wct)

    g_small = {
        "norm_mix": g_norm_mix.reshape(-1), "norm_mlp": g_norm_mlp.reshape(-1), "q_a_norm": g_qa.reshape(-1),
        "kv_a_norm": g_kva.reshape(-1), "q_norm": g_q.reshape(-1)[:QK_HEAD], "k_norm": g_k.reshape(-1)[:QK_HEAD],
        "ssm_a_re": g_ar, "ssm_a_im": g_ai, "ssm_log_dt": g_ldt.reshape(-1),
        "ssm_b_re": from_gcp(g_br), "ssm_b_im": from_gcp(g_bi),
        "ssm_c_re": g_cr.reshape(SSM_GROUPS, SSM_GROUP_CH, SSM_STATE), "ssm_c_im": g_ci.reshape(SSM_GROUPS, SSM_GROUP_CH, SSM_STATE),
        "ssm_d": g_d.reshape(SSM_GROUPS, SSM_GROUP_CH), "b_glu": g_b_glu.reshape(-1),
    }
    return loss_blk[0, 0], grad_x, g_small


def kernel(x, positions, norm_mix, w_in, q_a_norm, kv_a_norm, w_q_b, w_kv_b, q_norm, k_norm, w_o_mla, ssm_a_re, ssm_a_im, ssm_log_dt, ssm_b_re, ssm_b_im, ssm_c_re, ssm_c_im, ssm_d, w_glu, b_glu, w_o_ssm, w_out, norm_mlp, w_up, w_down, loss_target, m_norm_mix, m_w_in, m_q_a_norm, m_kv_a_norm, m_w_q_b, m_w_kv_b, m_q_norm, m_k_norm, m_w_o_mla, m_ssm_a_re, m_ssm_a_im, m_ssm_log_dt, m_ssm_b_re, m_ssm_b_im, m_ssm_c_re, m_ssm_c_im, m_ssm_d, m_w_glu, m_b_glu, m_w_o_ssm, m_w_out, m_norm_mlp, m_w_up, m_w_down, v_norm_mix, v_w_in, v_q_a_norm, v_kv_a_norm, v_w_q_b, v_w_kv_b, v_q_norm, v_k_norm, v_w_o_mla, v_ssm_a_re, v_ssm_a_im, v_ssm_log_dt, v_ssm_b_re, v_ssm_b_im, v_ssm_c_re, v_ssm_c_im, v_ssm_d, v_w_glu, v_b_glu, v_w_o_ssm, v_w_out, v_norm_mlp, v_w_up, v_w_down):
    args = dict(locals())
    w = {n: args[n][0] for n in WEIGHT_ORDER}
    m = {n: args["m_" + n][0] for n in WEIGHT_ORDER}
    v = {n: args["v_" + n][0] for n in WEIGHT_ORDER}
    big_names = [n for n, *_ in BIG_WEIGHTS]
    small_names = [n for n, _ in SMALL_WEIGHTS]

    place = jnp.stack([2 * lax.axis_index("x") + lax.axis_index("y"), lax.axis_index("c")]).astype(jnp.int32)
    rest = ["b", "c", "d", "e"]

    (grp_b,) = _gather_weights([_cast_shards(w, "b", place)])
    gather = _SplitGather([_cast_shards(w, g, place) for g in ("c", "d", "e", "a")], grp_b)
    red_a = _SplitReduction("a", ["a"], place)
    red_rest = _SplitReduction("rest", rest, place)
    small = {n: w[n] for n in small_names}

    loss_local, grad_x, g_small = _local_step(x[0], positions[0], loss_target[0], grp_b, small, gather, red_a, red_rest)
    loss = lax.psum(loss_local, ("x", "y", "c"))

    grad_w, delta_w, new_m, new_v = {}, {}, {}, {}

    def update(names, reduced):
        for n in names:
            g, off, _, _ = _place_in_group(n)
            grad_w[n], delta_w[n], new_m[n], new_v[n] = _adamw(w[n], reduced[g], m[n], v[n], "adamw_" + n, off)

    small_sum = _all_sum_small(_pack_small(g_small))
    g_s, d_s, m_s, v_s = _adamw(_pack_small(small), small_sum, _pack_small({n: m[n] for n in small_names}),
                                _pack_small({n: v[n] for n in small_names}), "adamw_small")
    g_s, d_s, m_s, v_s = _unpack_small(g_s), _unpack_small(d_s), _unpack_small(m_s), _unpack_small(v_s)
    for n in small_names:
        grad_w[n], delta_w[n], new_m[n], new_v[n] = g_s[n], d_s[n], m_s[n], v_s[n]
    in_a = [n for n, _ in GROUPS["a"][1]]
    update(in_a, {"a": red_a.join_done(small_sum)[0]})
    halves = red_rest.scatter_done(new_v[in_a[-1]])
    update([n for n in big_names if n not in in_a], dict(zip(rest, _swap_reduced_halves(halves))))

    lead = lambda d: [d[n][None] for n in WEIGHT_ORDER]
    return (loss, grad_x[None], *lead(grad_w), *lead(delta_w), *lead(new_m), *lead(new_v))
```

```python
import math

import jax
import jax.numpy as jnp
import numpy as np
from jax import lax
from jax.experimental import pallas as pl
from jax.experimental.pallas import tpu as pltpu

F32 = jnp.float32
BF16 = jnp.bfloat16

D_MODEL = 1024
SSM_GROUPS = 32
SSM_GROUP_CH = 16
SSM_WIDTH = 512
SSM_STATE = 64
GP = SSM_GROUPS * SSM_STATE
N_HEADS = 8
QK_NOPE = 128
QK_ROPE = 64
QK_HEAD = 192
HEAD_PAD = 256
V_HEAD = 128
Q_LORA = 384
KV_LORA = 256
LAT_W = 768
D_IN = 3264
D_IN_PAD = 3328
D_FF = 4096
ROPE_THETA = 10000.0
EPS = 1e-6
ATT_SCALE = QK_HEAD ** -0.5

ADAM_LR = 0.001
ADAM_B1 = 0.9
ADAM_B2 = 0.999
ADAM_EPS = 1e-08
ADAM_WD = 0.01
ADAM_STEP = 10

VMEM_LIMIT_V7X = 56 * 1024 * 1024
MESH = pl.DeviceIdType.MESH

BIG_WEIGHTS = (
    ("w_in", 1024, 3264, "col"),
    ("w_q_b", 384, 1536, "col"),
    ("w_kv_b", 256, 2048, "col"),
    ("w_o_mla", 1024, 1024, "row"),
    ("w_glu", 512, 512, "row"),
    ("w_o_ssm", 512, 1024, "col"),
    ("w_out", 1024, 1024, "row"),
    ("w_up", 1024, 4096, "col"),
    ("w_down", 4096, 1024, "row"),
)
GROUPS = {
    "a": (1024, (("w_down", 1024), ("w_up", 1024), ("w_o_mla", 256), ("w_out", 256))),
    "b": (816, (("w_in", 1024),)),
    "c": (384, (("w_q_b", 384),)),
    "d": (512, (("w_kv_b", 256), ("w_glu", 128))),
    "e": (256, (("w_o_ssm", 512),)),
}


def _group_rows(group):
    return sum(r for _, r in GROUPS[group][1])


def _place_in_group(name):
    for group, (width, members) in GROUPS.items():
        off = 0
        for member, rows in members:
            if member == name:
                return group, off, rows, width
            off += rows
    raise KeyError(name)


SMALL_WEIGHTS = (
    ("norm_mix", (1024,)), ("q_a_norm", (384,)), ("kv_a_norm", (256,)), ("q_norm", (192,)), ("k_norm", (192,)),
    ("ssm_a_re", (32, 64)), ("ssm_a_im", (32, 64)), ("ssm_log_dt", (32,)),
    ("ssm_b_re", (32, 64, 16)), ("ssm_b_im", (32, 64, 16)), ("ssm_c_re", (32, 16, 64)), ("ssm_c_im", (32, 16, 64)),
    ("ssm_d", (32, 16)), ("b_glu", (512,)), ("norm_mlp", (1024,)),
)
WEIGHT_ORDER = ('norm_mix', 'w_in', 'q_a_norm', 'kv_a_norm', 'w_q_b', 'w_kv_b', 'q_norm', 'k_norm', 'w_o_mla', 'ssm_a_re',
                'ssm_a_im', 'ssm_log_dt', 'ssm_b_re', 'ssm_b_im', 'ssm_c_re', 'ssm_c_im', 'ssm_d', 'w_glu', 'b_glu',
                'w_o_ssm', 'w_out', 'norm_mlp', 'w_up', 'w_down')


def _cparams(*sem):
    return pltpu.CompilerParams(dimension_semantics=sem if sem else None, vmem_limit_bytes=VMEM_LIMIT_V7X)


def _resident(shape, index=None):
    index = (0,) * len(shape) if index is None else index
    return pl.BlockSpec(shape, lambda *_: index, pipeline_mode=pl.Buffered(1))


def _member_block(name):
    _, off, rows, width = _place_in_group(name)
    return _resident((4, rows, width), (0, off // rows, 0))


def _rows(t, width):
    return pl.BlockSpec((t, width), lambda i: (i, 0))


def _mm(a, b):
    return jnp.dot(a.astype(BF16), b.astype(BF16), preferred_element_type=F32)


def _mm_nt(a, b):
    return lax.dot_general(a.astype(BF16), b.astype(BF16), (((1,), (1,)), ((), ())), preferred_element_type=F32)


def _mm_tn(a, b):
    return lax.dot_general(a.astype(BF16), b.astype(BF16), (((0,), (0,)), ((), ())), preferred_element_type=F32)


def _rms_fwd(x, g, n):
    r = lax.rsqrt(jnp.sum(x * x, axis=-1, keepdims=True) * (1.0 / n) + EPS)
    return x * r * g


def _rms_bwd(x, g, dy, n):
    r = lax.rsqrt(jnp.sum(x * x, axis=-1, keepdims=True) * (1.0 / n) + EPS)
    xh = x * r
    dxh = dy * g
    dx = r * (dxh - xh * (jnp.sum(dxh * xh, axis=-1, keepdims=True) * (1.0 / n)))
    return dx, dy * xh


def _colsum(a):
    return jnp.sum(a, axis=0, keepdims=True)


def _accumulate(ref, value, first):
    @pl.when(first)
    def _():
        ref[...] = value

    @pl.when(jnp.logical_not(first))
    def _():
        ref[...] += value


def _sigmoid(a):
    return 1.0 / (1.0 + jnp.exp(-a))


GELU_C = math.sqrt(2.0 / math.pi)
GELU_A = 0.044715


def _gelu(y):
    return 0.5 * y * (1.0 + jnp.tanh(GELU_C * (y + GELU_A * y * y * y)))


def _gelu_grad(y):
    t = jnp.tanh(GELU_C * (y + GELU_A * y * y * y))
    return 0.5 * (1.0 + t) + 0.5 * y * (1.0 - t * t) * GELU_C * (1.0 + 3.0 * GELU_A * y * y)


def _in_proj_fwd(x, g1, w_in_p, t, token):
    l = x.shape[0]

    def body(x_ref, g_ref, w_ref, token_ref, u_ref, lat_ref, gs_ref, gm_ref):
        xn = _rms_fwd(x_ref[...], g_ref[...], D_MODEL).astype(BF16)
        u_ref[...] = _mm(xn, w_ref[:, 0:512])
        lat_ref[...] = _mm(xn, w_ref[:, 512:1280])
        gs_ref[...] = _mm(xn, w_ref[:, 1280:2304])
        gm_ref[...] = _mm(xn, w_ref[:, 2304:3328])

    return pl.pallas_call(
        body, name="in_proj_fwd", grid=(l // t,),
        in_specs=[_rows(t, D_MODEL), _resident((1, D_MODEL)), _resident((D_MODEL, D_IN_PAD)), ANY],
        out_specs=[_rows(t, 512), _rows(t, LAT_W), _rows(t, D_MODEL), _rows(t, D_MODEL)],
        out_shape=[jax.ShapeDtypeStruct((l, 512), F32), jax.ShapeDtypeStruct((l, LAT_W), F32),
                   jax.ShapeDtypeStruct((l, D_MODEL), F32), jax.ShapeDtypeStruct((l, D_MODEL), F32)],
        compiler_params=_cparams("parallel"),
    )(x, g1, w_in_p, token)


def _in_proj_bwd(x, g1, w_in_p, d_u, d_lat, d_gs, d_gm, dh, t):
    l = x.shape[0]

    def body(x_ref, g_ref, w_ref, du_ref, dlat_ref, dgs_ref, dgm_ref, dh_ref, gx_ref, xn_ref, dproj_ref, dg_ref):
        xv = x_ref[...]
        g = g_ref[...]
        xn_ref[...] = _rms_fwd(xv, g, D_MODEL).astype(BF16)
        dproj_ref[:, 0:512] = du_ref[...]
        dproj_ref[:, 512:1280] = dlat_ref[...]
        dproj_ref[:, 1280:2304] = dgs_ref[...]
        dproj_ref[:, 2304:3328] = dgm_ref[...]
        dxn = _mm_nt(dproj_ref[...], w_ref[...])
        dx, dg_rows = _rms_bwd(xv, g, dxn, D_MODEL)
        gx_ref[...] = dh_ref[...] + dx
        _accumulate(dg_ref, _colsum(dg_rows), pl.program_id(0) == 0)

    return pl.pallas_call(
        body, name="in_proj_bwd", grid=(l // t,),
        in_specs=[_rows(t, D_MODEL), _resident((1, D_MODEL)), _resident((D_MODEL, D_IN_PAD)), _rows(t, 512),
                  _rows(t, LAT_W), _rows(t, D_MODEL), _rows(t, D_MODEL), _rows(t, D_MODEL)],
        out_specs=[_rows(t, D_MODEL), _rows(t, D_MODEL), _rows(t, D_IN_PAD), pl.BlockSpec((1, D_MODEL), lambda i: (0, 0))],
        out_shape=[jax.ShapeDtypeStruct((l, D_MODEL), F32), jax.ShapeDtypeStruct((l, D_MODEL), BF16),
                   jax.ShapeDtypeStruct((l, D_IN_PAD), BF16), jax.ShapeDtypeStruct((1, D_MODEL), F32)],
        compiler_params=_cparams("arbitrary"),
    )(x, g1, w_in_p, d_u, d_lat, d_gs, d_gm, dh)


def _ssm_param_fn(a_re, a_im, log_dt, b_re, b_im):
    dt = jnp.exp(log_dt)
    er = jnp.exp(a_re * dt)
    lr = er * jnp.cos(a_im * dt)
    li = er * jnp.sin(a_im * dt)
    den = a_re * a_re + a_im * a_im
    nr = lr - 1.0
    kr = (nr * a_re + li * a_im) / den
    ki = (li * a_re - nr * a_im) / den
    rows = lambda k: jnp.broadcast_to(k[:, None, :], (SSM_GROUPS, SSM_GROUP_CH, SSM_STATE)).reshape(SSM_WIDTH, SSM_STATE)
    krt, kit = rows(kr), rows(ki)
    return lr, li, krt * b_re - kit * b_im, krt * b_im + kit * b_re


def _state_selector():
    row = lax.broadcasted_iota(jnp.int32, (SSM_STATE, GP), 0)
    col = lax.broadcasted_iota(jnp.int32, (SSM_STATE, GP), 1)
    return jnp.where(jnp.bitwise_and(col, SSM_STATE - 1) == row, 1.0, 0.0).astype(BF16)


def _own_group(rows, rows_per_group_log2):
    row = lax.broadcasted_iota(jnp.int32, (rows, GP), 0)
    col = lax.broadcasted_iota(jnp.int32, (rows, GP), 1)
    return jnp.right_shift(row, rows_per_group_log2) == jnp.right_shift(col, 6)


def _three_bf16(x):
    hi = x.astype(BF16)
    rest = x - hi.astype(F32)
    mid = rest.astype(BF16)
    return hi, mid, (rest - mid.astype(F32)).astype(BF16)


def _spread(x, sel):
    return sum(jnp.dot(part, sel, preferred_element_type=F32) for part in _three_bf16(x))


def _collect(xw, sel):
    return sum(lax.dot_general(part, sel, (((1,), (1,)), ((), ())), preferred_element_type=F32) for part in _three_bf16(xw))


def _ssm_param_fwd(a_re, a_im, log_dt, b_re, b_im, c_re, c_im):
    def body(ar_ref, ai_ref, ldt_ref, br_ref, bi_ref, cr_ref, ci_ref, wb_ref, wct_ref, tf_ref, tr_ref):
        lr, li, bbr, bbi = _ssm_param_fn(ar_ref[...], ai_ref[...], ldt_ref[...], br_ref[...], bi_ref[...])
        sel = _state_selector()
        own16 = _own_group(SSM_WIDTH, 4)
        own1 = _own_group(SSM_GROUPS, 0)
        block = lambda m: jnp.where(own16, jnp.dot(m.astype(BF16), sel, preferred_element_type=F32), 0.0).astype(BF16)
        wb_ref[:, 0:GP] = block(bbr)
        wb_ref[:, GP:2 * GP] = block(bbi)
        wct_ref[:, 0:GP] = block(cr_ref[...])
        wct_ref[:, GP:2 * GP] = block(-ci_ref[...])
        flat = lambda m: _colsum(jnp.where(own1, _spread(m, sel), 0.0))
        pr, pi = [], []
        qr, qi = lr, li
        for _ in range(8):
            pr.append(flat(qr))
            pi.append(flat(qi))
            qr, qi = qr * lr - qi * li, qr * li + qi * lr
        row = lax.broadcasted_iota(jnp.int32, (8, GP), 0)
        for n, k in enumerate((1, 2, 4)):
            tf_ref[2 * n] = jnp.where(row >= k, pr[k - 1], 0.0)
            tf_ref[2 * n + 1] = jnp.where(row >= k, pi[k - 1], 0.0)
            tr_ref[2 * n] = jnp.where(row < 8 - k, pr[k - 1], 0.0)
            tr_ref[2 * n + 1] = jnp.where(row < 8 - k, -pi[k - 1], 0.0)
        pick = lambda vals: sum(jnp.where(row == j, v, 0.0) for j, v in enumerate(vals))
        tf_ref[6] = pick(pr)
        tf_ref[7] = pick(pi)
        tr_ref[6] = pick(pr[::-1])
        tr_ref[7] = pick([-v for v in pi[::-1]])

    return pl.pallas_call(
        body, name="ssm_param_fwd",
        out_shape=[jax.ShapeDtypeStruct((SSM_WIDTH, 2 * GP), BF16), jax.ShapeDtypeStruct((SSM_WIDTH, 2 * GP), BF16),
                   jax.ShapeDtypeStruct((8, 8, GP), F32), jax.ShapeDtypeStruct((8, 8, GP), F32)],
        compiler_params=_cparams(),
    )(a_re, a_im, log_dt, b_re, b_im, c_re, c_im)


STRIP_CH = 128
STRIP_ST = 512
N_STRIPS = SSM_WIDTH // STRIP_CH


def _ssm_param_bwd(a_re, a_im, log_dt, b_re, b_im, g_lr, g_li, g_wb, g_wct):
    def body(ar_ref, ai_ref, ldt_ref, br_ref, bi_ref, glr_ref, gli_ref, gwb_ref, gwc_ref,
             o_ar, o_ai, o_ldt, o_br, o_bi, o_cr, o_ci):
        sel = _state_selector()
        own1 = _own_group(SSM_GROUPS, 0)
        row = lax.broadcasted_iota(jnp.int32, (SSM_WIDTH, STRIP_ST), 0)
        col = lax.broadcasted_iota(jnp.int32, (SSM_WIDTH, STRIP_ST), 1)
        own = jnp.bitwise_and(jnp.right_shift(row, 4), 7) == jnp.right_shift(col, 6)
        blocks = lambda m: _collect(jnp.where(own, m, 0.0), sel[:, 0:STRIP_ST])
        unflat = lambda v: _collect(jnp.where(own1, v, 0.0), sel)
        _, vjp = jax.vjp(_ssm_param_fn, ar_ref[...], ai_ref[...], ldt_ref[...], br_ref[...], bi_ref[...])
        d_ar, d_ai, d_ldt, d_br, d_bi = vjp((unflat(glr_ref[...]), unflat(gli_ref[...]),
                                             blocks(gwb_ref[:, 0:STRIP_ST]), blocks(gwb_ref[:, STRIP_ST:2 * STRIP_ST])))
        o_ar[...] = d_ar
        o_ai[...] = d_ai
        o_ldt[...] = d_ldt
        o_br[...] = d_br
        o_bi[...] = d_bi
        o_cr[...] = blocks(gwc_ref[:, 0:STRIP_ST])
        o_ci[...] = -blocks(gwc_ref[:, STRIP_ST:2 * STRIP_ST])

    g, p = SSM_GROUPS, SSM_STATE
    gp = jax.ShapeDtypeStruct((g, p), F32)
    gcp = jax.ShapeDtypeStruct((SSM_WIDTH, p), F32)
    return pl.pallas_call(
        body, name="ssm_param_bwd", out_shape=[gp, gp, jax.ShapeDtypeStruct((g, 1), F32), gcp, gcp, gcp, gcp],
        compiler_params=_cparams(),
    )(a_re, a_im, log_dt, b_re, b_im, g_lr, g_li, g_wb, g_wct)


def _strip(ref, j, im):
    return ref[STRIP_CH * j:STRIP_CH * (j + 1), im * GP + STRIP_ST * j:im * GP + STRIP_ST * (j + 1)]


def _wgrad_strips(a, b_re, b_im, name, im_block, token):
    l = a.shape[0]
    bl = min(l, 512)

    def body(a_ref, bre_ref, bim_ref, token_ref, o_ref):
        first = pl.program_id(0) == 0
        for j in range(N_STRIPS):
            aj = a_ref[:, STRIP_CH * j:STRIP_CH * (j + 1)]
            states = slice(STRIP_ST * j, STRIP_ST * (j + 1))
            _accumulate(o_ref.at[STRIP_CH * j:STRIP_CH * (j + 1), 0:STRIP_ST], _mm_tn(aj, bre_ref[:, states]), first)
            _accumulate(o_ref.at[STRIP_CH * j:STRIP_CH * (j + 1), STRIP_ST:2 * STRIP_ST], _mm_tn(aj, bim_ref[:, states]), first)

    return pl.pallas_call(
        body, name=name, grid=(l // bl,),
        in_specs=[pl.BlockSpec((bl, SSM_WIDTH), lambda k: (k, 0)), pl.BlockSpec((bl, GP), lambda k: (k, 0)),
                  pl.BlockSpec((bl, GP), lambda k: (k, im_block)), ANY],
        out_specs=pl.BlockSpec((SSM_WIDTH, 2 * STRIP_ST), lambda k: (0, 0)),
        out_shape=jax.ShapeDtypeStruct((SSM_WIDTH, 2 * STRIP_ST), F32),
        compiler_params=_cparams("arbitrary"),
    )(a, b_re, b_im, token)


SCAN_STRIP = 512


def _scan_chunk(inr_ref, ini_ref, outr_ref, outi_ref, cr_ref, ci_ref, tab_ref, tc, reverse):
    n_blocks = tc // 8

    def block(j, _):
        i = (n_blocks - 1 - j) if reverse else j
        rows = pl.ds(pl.multiple_of(i * 8, 8), 8)
        for s in range(GP // SCAN_STRIP):
            sl = pl.ds(s * SCAN_STRIP, SCAN_STRIP)
            xr = inr_ref[rows, sl]
            xi = ini_ref[rows, sl]
            for n, k in enumerate((1, 2, 4)):
                shift = (8 - k) if reverse else k
                sr = pltpu.roll(xr, shift, 0)
                si = pltpu.roll(xi, shift, 0)
                mr = tab_ref[2 * n, :, sl]
                mi = tab_ref[2 * n + 1, :, sl]
                xr, xi = xr + mr * sr - mi * si, xi + mr * si + mi * sr
            qr = tab_ref[6, :, sl]
            qi = tab_ref[7, :, sl]
            cr = cr_ref[:, sl]
            ci = ci_ref[:, sl]
            xr, xi = xr + qr * cr - qi * ci, xi + qr * ci + qi * cr
            outr_ref[rows, sl] = xr
            outi_ref[rows, sl] = xi
            edge = 0 if reverse else 7
            cr_ref[:, sl] = jnp.broadcast_to(xr[edge:edge + 1, :], (8, SCAN_STRIP))
            ci_ref[:, sl] = jnp.broadcast_to(xi[edge:edge + 1, :], (8, SCAN_STRIP))
        return 0

    lax.fori_loop(0, n_blocks, block, 0)


def _glu_pre(z, wg_ref):
    return sum(_mm(z[:, 128 * j:128 * (j + 1)], wg_ref[j]) for j in range(4))


def _ssm_fwd(u, wb, wc, tabs, dskip, grp_d, b_glu, grp_e, tc):
    l = u.shape[0]

    def body(u_ref, wb_ref, wc_ref, tab_ref, d_ref, wg_ref, bg_ref, wo_ref, xr_ref, xi_ref, y_ref, ys_ref,
             bur, bui, cr, ci):
        @pl.when(pl.program_id(0) == 0)
        def _():
            cr[...] = jnp.zeros_like(cr)
            ci[...] = jnp.zeros_like(ci)

        uv = u_ref[...]
        ub = uv.astype(BF16)
        for j in range(N_STRIPS):
            uj = ub[:, STRIP_CH * j:STRIP_CH * (j + 1)]
            states = slice(STRIP_ST * j, STRIP_ST * (j + 1))
            bur[:, states] = _mm(uj, _strip(wb_ref, j, 0))
            bui[:, states] = _mm(uj, _strip(wb_ref, j, 1))
        _scan_chunk(bur, bui, xr_ref, xi_ref, cr, ci, tab_ref, tc, False)
        y = jnp.concatenate(
            [_mm_nt(xr_ref[:, STRIP_ST * j:STRIP_ST * (j + 1)], _strip(wc_ref, j, 0))
             + _mm_nt(xi_ref[:, STRIP_ST * j:STRIP_ST * (j + 1)], _strip(wc_ref, j, 1)) for j in range(N_STRIPS)],
            axis=-1) + d_ref[...] * uv
        y_ref[...] = y
        z = _gelu(y)
        z2 = z * _sigmoid(_glu_pre(z, wg_ref) + bg_ref[...])
        for s in range(4):
            ys_ref[:, 256 * s:256 * (s + 1)] = _mm(z2, wo_ref[s])

    return pl.pallas_call(
        body, name="ssm_fwd", grid=(l // tc,),
        in_specs=[_rows(tc, 512), _resident((512, 2 * GP)), _resident((512, 2 * GP)), _resident((8, 8, GP)),
                  _resident((1, 512)), _member_block("w_glu"), _resident((1, 512)), _member_block("w_o_ssm")],
        out_specs=[_rows(tc, GP), _rows(tc, GP), _rows(tc, 512), _rows(tc, D_MODEL)],
        out_shape=[jax.ShapeDtypeStruct((l, GP), F32), jax.ShapeDtypeStruct((l, GP), F32),
                   jax.ShapeDtypeStruct((l, 512), F32), jax.ShapeDtypeStruct((l, D_MODEL), F32)],
        scratch_shapes=[pltpu.VMEM((tc, GP), F32), pltpu.VMEM((tc, GP), F32), pltpu.VMEM((8, GP), F32),
                        pltpu.VMEM((8, GP), F32)],
        compiler_params=_cparams("arbitrary"),
    )(u, wb, wc, tabs, dskip, grp_d, b_glu, grp_e)


def _ssm_bwd(dys, y, u, xr, xi, wb, wc, tabs_rev, dskip, grp_d, b_glu, grp_e, tc):
    l = u.shape[0]
    nc = l // tc

    def body(dys_ref, y_ref, u_ref, xr_ref, xi_ref, wb_ref, wc_ref, tab_ref, d_ref, wg_ref, bg_ref, wo_ref,
             du_ref, a_ref, dy_ref, z_ref, z2_ref, dpre_ref, gb_ref, gd_ref, glr_ref, gli_ref,
             dxr, dxi, ar, ai, cr, ci):
        first = pl.program_id(0) == 0

        @pl.when(first)
        def _():
            cr[...] = jnp.zeros_like(cr)
            ci[...] = jnp.zeros_like(ci)

        yv = y_ref[...]
        uv = u_ref[...]
        dz2 = sum(_mm_nt(dys_ref[:, 256 * j:256 * (j + 1)], wo_ref[j]) for j in range(4))
        z = _gelu(yv)
        s = _sigmoid(_glu_pre(z, wg_ref) + bg_ref[...])
        dpre = dz2 * z * s * (1.0 - s)
        dpreb = dpre.astype(BF16)
        dz = dz2 * s + jnp.concatenate([_mm_nt(dpreb, wg_ref[j]) for j in range(4)], axis=-1)
        dy = dz * _gelu_grad(yv)
        z_ref[...] = z.astype(BF16)
        z2_ref[...] = (z * s).astype(BF16)
        dpre_ref[...] = dpre.astype(BF16)
        dy_ref[...] = dy.astype(BF16)
        _accumulate(gb_ref, _colsum(dpre), first)
        _accumulate(gd_ref, _colsum(dy * uv), first)

        dyb = dy.astype(BF16)
        for j in range(N_STRIPS):
            dyj = dyb[:, STRIP_CH * j:STRIP_CH * (j + 1)]
            dxr[:, STRIP_ST * j:STRIP_ST * (j + 1)] = _mm(dyj, _strip(wc_ref, j, 0))
            dxi[:, STRIP_ST * j:STRIP_ST * (j + 1)] = _mm(dyj, _strip(wc_ref, j, 1))
        ar[pl.ds(tc, 8), :] = cr[...]
        ai[pl.ds(tc, 8), :] = ci[...]
        _scan_chunk(dxr, dxi, ar, ai, cr, ci, tab_ref, tc, True)
        a_ref[:, 0:GP] = ar[pl.ds(0, tc), :].astype(BF16)
        a_ref[:, GP:2 * GP] = ai[pl.ds(0, tc), :].astype(BF16)
        du_states = jnp.concatenate(
            [_mm_nt(a_ref[:, STRIP_ST * j:STRIP_ST * (j + 1)], _strip(wb_ref, j, 0))
             + _mm_nt(a_ref[:, GP + STRIP_ST * j:GP + STRIP_ST * (j + 1)], _strip(wb_ref, j, 1)) for j in range(N_STRIPS)],
            axis=-1)
        du_ref[...] = (dy * d_ref[...] + du_states).astype(BF16)
        anr = ar[pl.ds(1, tc), :]
        ani = ai[pl.ds(1, tc), :]
        xrv = xr_ref[...]
        xiv = xi_ref[...]
        _accumulate(glr_ref, _colsum(anr * xrv + ani * xiv), first)
        _accumulate(gli_ref, _colsum(ani * xrv - anr * xiv), first)

    rev = lambda w: pl.BlockSpec((tc, w), lambda i: (nc - 1 - i, 0))
    acc = lambda w: pl.BlockSpec((1, w), lambda i: (0, 0))
    bf = jax.ShapeDtypeStruct((l, 512), BF16)
    return pl.pallas_call(
        body, name="ssm_bwd", grid=(nc,),
        in_specs=[rev(D_MODEL), rev(512), rev(512), rev(GP), rev(GP), _resident((512, 2 * GP)), _resident((512, 2 * GP)),
                  _resident((8, 8, GP)), _resident((1, 512)), _member_block("w_glu"), _resident((1, 512)),
                  _member_block("w_o_ssm")],
        out_specs=[rev(512), rev(2 * GP), rev(512), rev(512), rev(512), rev(512), acc(512), acc(512), acc(GP), acc(GP)],
        out_shape=[bf, jax.ShapeDtypeStruct((l, 2 * GP), BF16), bf, bf, bf, bf,
                   jax.ShapeDtypeStruct((1, 512), F32), jax.ShapeDtypeStruct((1, 512), F32),
                   jax.ShapeDtypeStruct((1, GP), F32), jax.ShapeDtypeStruct((1, GP), F32)],
        scratch_shapes=[pltpu.VMEM((tc, GP), F32), pltpu.VMEM((tc, GP), F32), pltpu.VMEM((tc + 8, GP), F32),
                        pltpu.VMEM((tc + 8, GP), F32), pltpu.VMEM((8, GP), F32), pltpu.VMEM((8, GP), F32)],
        compiler_params=_cparams("arbitrary"),
    )(dys, y, u, xr, xi, wb, wc, tabs_rev, dskip, grp_d, b_glu, grp_e)


def _swap_halves(b):
    lane = lax.broadcasted_iota(jnp.int32, b.shape, 1)
    return jnp.where(lane < 32, pltpu.roll(b, 96, 1), pltpu.roll(b, 32, 1))


def _rope_tables(pos_ref, invf_ref, sgn_ref):
    ang = pos_ref[...].astype(F32) * invf_ref[...]
    return jnp.cos(ang), jnp.sin(ang) * sgn_ref[...]


def _mla_pre_fwd(lat, pos, invf, sgn, gqa, gkva, gq, gk, w_qb_p, w_kvb, t):
    l = lat.shape[0]

    def body(lat_ref, pos_ref, invf_ref, sgn_ref, gqa_ref, gkva_ref, gq_ref, gk_ref, wq_ref, wkv_ref, q_ref, k_ref, v_ref):
        cs, sn = _rope_tables(pos_ref, invf_ref, sgn_ref)
        ql = _rms_fwd(lat_ref[:, 0:Q_LORA], gqa_ref[...], Q_LORA)
        ckn = _rms_fwd(lat_ref[:, Q_LORA:Q_LORA + KV_LORA], gkva_ref[...], KV_LORA)
        kpe = lat_ref[:, 640:768]
        q0 = _mm(ql, wq_ref[...])
        cknb = ckn.astype(BF16)
        kv = jnp.concatenate([_mm(cknb, wkv_ref[s]) for s in range(4)], axis=-1)
        for h in range(N_HEADS):
            q1 = _rms_fwd(q0[:, HEAD_PAD * h:HEAD_PAD * (h + 1)], gq_ref[...], QK_HEAD)
            b = q1[:, 128:256]
            q_ref[h, :, 0:128] = (q1[:, 0:128] * ATT_SCALE).astype(BF16)
            q_ref[h, :, 128:256] = ((b * cs + _swap_halves(b) * sn) * ATT_SCALE).astype(BF16)
            k0 = jnp.concatenate([kv[:, 256 * h:256 * h + 128], kpe], axis=-1)
            k1 = _rms_fwd(k0, gk_ref[...], QK_HEAD)
            b = k1[:, 128:256]
            k_ref[h, :, 0:128] = k1[:, 0:128].astype(BF16)
            k_ref[h, :, 128:256] = (b * cs + _swap_halves(b) * sn).astype(BF16)
            v_ref[h] = kv[:, 256 * h + 128:256 * h + 256].astype(BF16)

    heads = lambda w: pl.BlockSpec((N_HEADS, t, w), lambda i: (0, i, 0))
    return pl.pallas_call(
        body, name="mla_pre_fwd", grid=(l // t,),
        in_specs=[_rows(t, LAT_W), _rows(t, 1), _resident((1, 128)), _resident((1, 128)), _resident((1, Q_LORA)),
                  _resident((1, KV_LORA)), _resident((1, HEAD_PAD)), _resident((1, HEAD_PAD)),
                  _resident((Q_LORA, N_HEADS * HEAD_PAD)), _member_block("w_kv_b")],
        out_specs=[heads(HEAD_PAD), heads(HEAD_PAD), heads(V_HEAD)],
        out_shape=[jax.ShapeDtypeStruct((N_HEADS, l, HEAD_PAD), BF16), jax.ShapeDtypeStruct((N_HEADS, l, HEAD_PAD), BF16),
                   jax.ShapeDtypeStruct((N_HEADS, l, V_HEAD), BF16)],
        compiler_params=_cparams("parallel"),
    )(lat, pos, invf, sgn, gqa, gkva, gq, gk, w_qb_p, w_kvb)


def _mla_pre_bwd(lat, pos, invf, sgn, gqa, gkva, gq, gk, w_qb_p, w_kvb, dq, dk, dv, t, token):
    l = lat.shape[0]

    def body(lat_ref, pos_ref, invf_ref, sgn_ref, gqa_ref, gkva_ref, gq_ref, gk_ref, wq_ref, wkv_ref, dq_ref, dk_ref, dv_ref,
             token_ref, dlat_ref, ql_ref, dq0_ref, ckn_ref, dkv_ref, ggqa_ref, ggkva_ref, ggq_ref, ggk_ref):
        first = pl.program_id(0) == 0
        cs, sn = _rope_tables(pos_ref, invf_ref, sgn_ref)
        q_lat = lat_ref[:, 0:Q_LORA]
        c_kv = lat_ref[:, Q_LORA:Q_LORA + KV_LORA]
        kpe = lat_ref[:, 640:768]
        ql = _rms_fwd(q_lat, gqa_ref[...], Q_LORA)
        ckn = _rms_fwd(c_kv, gkva_ref[...], KV_LORA)
        ql_ref[...] = ql.astype(BF16)
        ckn_ref[...] = ckn.astype(BF16)
        q0 = _mm(ql, wq_ref[...])
        cknb = ckn.astype(BF16)
        kv = jnp.concatenate([_mm(cknb, wkv_ref[s]) for s in range(4)], axis=-1)
        dkpe = jnp.zeros_like(kpe)
        ggq = jnp.zeros((1, HEAD_PAD), F32)
        ggk = jnp.zeros((1, HEAD_PAD), F32)

        def unrope(d):
            b = d[:, 128:256]
            return jnp.concatenate([d[:, 0:128], b * cs + _swap_halves(b * sn)], axis=-1)

        for h in range(N_HEADS):
            dq1 = unrope(dq_ref[h] * ATT_SCALE)
            dq0h, gq_rows = _rms_bwd(q0[:, HEAD_PAD * h:HEAD_PAD * (h + 1)], gq_ref[...], dq1, QK_HEAD)
            ggq = ggq + _colsum(gq_rows)
            dq0_ref[:, HEAD_PAD * h:HEAD_PAD * (h + 1)] = dq0h.astype(BF16)
            k0 = jnp.concatenate([kv[:, 256 * h:256 * h + 128], kpe], axis=-1)
            dk0, gk_rows = _rms_bwd(k0, gk_ref[...], unrope(dk_ref[h]), QK_HEAD)
            ggk = ggk + _colsum(gk_rows)
            dkpe = dkpe + dk0[:, 128:256]
            dkv_ref[:, 256 * h:256 * h + 128] = dk0[:, 0:128].astype(BF16)
            dkv_ref[:, 256 * h + 128:256 * h + 256] = dv_ref[h].astype(BF16)
        dql = _mm_nt(dq0_ref[...], wq_ref[...])
        dckn = sum(_mm_nt(dkv_ref[:, 512 * s:512 * (s + 1)], wkv_ref[s]) for s in range(4))
        dq_lat, gqa_rows = _rms_bwd(q_lat, gqa_ref[...], dql, Q_LORA)
        dc_kv, gkva_rows = _rms_bwd(c_kv, gkva_ref[...], dckn, KV_LORA)
        dlat_ref[:, 0:Q_LORA] = dq_lat.astype(BF16)
        dlat_ref[:, Q_LORA:Q_LORA + KV_LORA] = dc_kv.astype(BF16)
        dlat_ref[:, 640:768] = dkpe.astype(BF16)
        _accumulate(ggqa_ref, _colsum(gqa_rows), first)
        _accumulate(ggkva_ref, _colsum(gkva_rows), first)
        _accumulate(ggq_ref, ggq, first)
        _accumulate(ggk_ref, ggk, first)

    heads = lambda w: pl.BlockSpec((N_HEADS, t, w), lambda i: (0, i, 0))
    acc = lambda w: pl.BlockSpec((1, w), lambda i: (0, 0))
    return pl.pallas_call(
        body, name="mla_pre_bwd", grid=(l // t,),
        in_specs=[_rows(t, LAT_W), _rows(t, 1), _resident((1, 128)), _resident((1, 128)), _resident((1, Q_LORA)),
                  _resident((1, KV_LORA)), _resident((1, HEAD_PAD)), _resident((1, HEAD_PAD)),
                  _resident((Q_LORA, N_HEADS * HEAD_PAD)), _member_block("w_kv_b"),
                  heads(HEAD_PAD), heads(HEAD_PAD), heads(V_HEAD), ANY],
        out_specs=[_rows(t, LAT_W), _rows(t, Q_LORA), _rows(t, N_HEADS * HEAD_PAD), _rows(t, KV_LORA), _rows(t, N_HEADS * 256),
                   acc(Q_LORA), acc(KV_LORA), acc(HEAD_PAD), acc(HEAD_PAD)],
        out_shape=[jax.ShapeDtypeStruct((l, LAT_W), BF16), jax.ShapeDtypeStruct((l, Q_LORA), BF16),
                   jax.ShapeDtypeStruct((l, N_HEADS * HEAD_PAD), BF16), jax.ShapeDtypeStruct((l, KV_LORA), BF16),
                   jax.ShapeDtypeStruct((l, N_HEADS * 256), BF16), jax.ShapeDtypeStruct((1, Q_LORA), F32),
                   jax.ShapeDtypeStruct((1, KV_LORA), F32), jax.ShapeDtypeStruct((1, HEAD_PAD), F32),
                   jax.ShapeDtypeStruct((1, HEAD_PAD), F32)],
        compiler_params=_cparams("arbitrary"),
    )(lat, pos, invf, sgn, gqa, gkva, gq, gk, w_qb_p, w_kvb, dq, dk, dv, token)


def _causal(s, transposed):
    row = lax.broadcasted_iota(jnp.int32, s.shape, 0)
    col = lax.broadcasted_iota(jnp.int32, s.shape, 1)
    keep = (row <= col) if transposed else (col <= row)
    return jnp.where(keep, s, -jnp.inf)


def _as_row(col):
    n = col.shape[0]
    row = lax.broadcasted_iota(jnp.int32, (n, n), 0)
    lane = lax.broadcasted_iota(jnp.int32, (n, n), 1)
    return jnp.sum(jnp.where(row == lane, col, 0.0), axis=0, keepdims=True)


def _attn_fwd(q, k, v, tq):
    l = q.shape[1]

    hb = 2

    def body(q_ref, k_ref, v_ref, o_ref, lse_ref):
        qi = pl.program_id(1)
        qs = [q_ref[a] for a in range(hb)]

        def step(kb, carry, masked):
            rows = pl.ds(pl.multiple_of(kb * tq, tq), tq)
            out = []
            for a, (m, den, acc) in enumerate(carry):
                s = _mm_nt(qs[a], k_ref[a, rows, :])
                if masked:
                    s = _causal(s, False)
                m_new = jnp.maximum(m, jnp.max(s, axis=-1, keepdims=True))
                alpha = jnp.exp(m - m_new)
                p = jnp.exp(s - m_new)
                den = alpha * den + jnp.sum(p, axis=-1, keepdims=True)
                acc = alpha * acc + _mm(p, v_ref[a, rows, :])
                out.append((m_new, den, acc))
            return tuple(out)

        init = tuple((jnp.full((tq, 1), -jnp.inf, F32), jnp.zeros((tq, 1), F32), jnp.zeros((tq, V_HEAD), F32))
                     for _ in range(hb))
        carry = lax.fori_loop(0, qi, lambda kb, c: step(kb, c, False), init)
        for a, (m, den, acc) in enumerate(step(qi, carry, True)):
            o_ref[:, V_HEAD * a:V_HEAD * (a + 1)] = acc / den
            lse_ref[a, 0] = _as_row(m + jnp.log(den))

    return pl.pallas_call(
        body, name="attn_fwd", grid=(N_HEADS // hb, l // tq),
        in_specs=[pl.BlockSpec((hb, tq, HEAD_PAD), lambda h, i: (h, i, 0)), pl.BlockSpec((hb, l, HEAD_PAD), lambda h, i: (h, 0, 0)),
                  pl.BlockSpec((hb, l, V_HEAD), lambda h, i: (h, 0, 0))],
        out_specs=[pl.BlockSpec((tq, hb * V_HEAD), lambda h, i: (i, h)), pl.BlockSpec((hb, 1, 1, tq), lambda h, i: (h, i, 0, 0))],
        out_shape=[jax.ShapeDtypeStruct((l, N_HEADS * V_HEAD), F32), jax.ShapeDtypeStruct((N_HEADS, l // tq, 1, tq), F32)],
        compiler_params=_cparams("parallel", "arbitrary"),
    )(q, k, v)


def _attn_bwd(q, k, v, o, do, lse_t, tq, token):
    l = q.shape[1]
    nq = l // tq

    hb = 1

    def body(q_ref, k_ref, v_ref, o_ref, do_ref, lse_ref, token_ref, dq_ref, dk_ref, dv_ref):
        ki = pl.program_id(1)

        @pl.when(ki == 0)
        def _():
            dq_ref[...] = jnp.zeros_like(dq_ref)

        kblks = [k_ref[a] for a in range(hb)]
        vblks = [v_ref[a] for a in range(hb)]
        ones = jnp.ones((8, V_HEAD), BF16)

        def step(qb, carry, masked):
            rows = pl.ds(pl.multiple_of(qb * tq, tq), tq)
            out = []
            for a, (dk, dv) in enumerate(carry):
                cols = slice(V_HEAD * a, V_HEAD * (a + 1))
                qblk = q_ref[a, rows, :]
                dov = do_ref[rows, cols]
                dob = dov.astype(BF16)
                delta = sum(_mm_nt(ones, part) for part in _three_bf16(dov * o_ref[rows, cols]))[0:1, :]
                st = _mm_nt(kblks[a], qblk)
                if masked:
                    st = _causal(st, True)
                pt = jnp.exp(st - lse_ref[a, qb])
                dv = dv + _mm(pt, dob)
                dst = (pt * (_mm_nt(vblks[a], dob) - delta)).astype(BF16)
                dk = dk + _mm(dst, qblk)
                dq_ref[a, rows, :] += _mm_tn(dst, kblks[a])
                out.append((dk, dv))
            return tuple(out)

        init = tuple((jnp.zeros((tq, HEAD_PAD), F32), jnp.zeros((tq, V_HEAD), F32)) for _ in range(hb))
        carry = lax.fori_loop(ki + 1, nq, lambda qb, c: step(qb, c, False), step(ki, init, True))
        for a, (dk, dv) in enumerate(carry):
            dk_ref[a] = dk
            dv_ref[a] = dv

    return pl.pallas_call(
        body, name="attn_bwd", grid=(N_HEADS // hb, nq),
        in_specs=[pl.BlockSpec((hb, l, HEAD_PAD), lambda h, i: (h, 0, 0)), pl.BlockSpec((hb, tq, HEAD_PAD), lambda h, i: (h, i, 0)),
                  pl.BlockSpec((hb, tq, V_HEAD), lambda h, i: (h, i, 0)), pl.BlockSpec((l, hb * V_HEAD), lambda h, i: (0, h)),
                  pl.BlockSpec((l, hb * V_HEAD), lambda h, i: (0, h)), pl.BlockSpec((hb, nq, 1, tq), lambda h, i: (h, 0, 0, 0)), ANY],
        out_specs=[pl.BlockSpec((hb, l, HEAD_PAD), lambda h, i: (h, 0, 0)), pl.BlockSpec((hb, tq, HEAD_PAD), lambda h, i: (h, i, 0)),
                   pl.BlockSpec((hb, tq, V_HEAD), lambda h, i: (h, i, 0))],
        out_shape=[jax.ShapeDtypeStruct((N_HEADS, l, HEAD_PAD), F32), jax.ShapeDtypeStruct((N_HEADS, l, HEAD_PAD), F32),
                   jax.ShapeDtypeStruct((N_HEADS, l, V_HEAD), F32)],
        compiler_params=_cparams("parallel", "arbitrary"),
    )(q, k, v, o, do, lse_t, token)


def _row_shards_mm(a, w_ref):
    a = a.astype(BF16)
    return sum(_mm(a[:, 256 * j:256 * (j + 1)], w_ref[j]) for j in range(4))


def _row_shards_mm_nt(a, w_ref):
    a = a.astype(BF16)
    return jnp.concatenate([_mm_nt(a, w_ref[j]) for j in range(4)], axis=-1)


def _merge_fwd(attn, y_ssm, gs, gm, x, grp_a, t):
    l = x.shape[0]

    def body(attn_ref, ys_ref, gs_ref, gm_ref, x_ref, wo_ref, wout_ref, ym_ref, mixed_ref, h_ref):
        y_mla = _row_shards_mm(attn_ref[...], wo_ref)
        ym_ref[...] = y_mla
        mixed = (_sigmoid(gs_ref[...]) * ys_ref[...] + _sigmoid(gm_ref[...]) * y_mla).astype(BF16)
        mixed_ref[...] = mixed
        h_ref[...] = x_ref[...] + _row_shards_mm(mixed, wout_ref)

    r = lambda: _rows(t, D_MODEL)
    return pl.pallas_call(
        body, name="merge_fwd", grid=(l // t,),
        in_specs=[r(), r(), r(), r(), r(), _member_block("w_o_mla"), _member_block("w_out")],
        out_specs=[r(), r(), r()],
        out_shape=[jax.ShapeDtypeStruct((l, D_MODEL), F32), jax.ShapeDtypeStruct((l, D_MODEL), BF16),
                   jax.ShapeDtypeStruct((l, D_MODEL), F32)],
        compiler_params=_cparams("parallel"),
    )(attn, y_ssm, gs, gm, x, grp_a, grp_a)


def _merge_bwd(dh, y_ssm, y_mla, gs, gm, grp_a, t):
    l = dh.shape[0]

    def body(dh_ref, ys_ref, ym_ref, gs_ref, gm_ref, wo_ref, wout_ref, dys_ref, dym_ref, dgs_ref, dgm_ref, dattn_ref):
        dmixed = _row_shards_mm_nt(dh_ref[...], wout_ref)
        sg = _sigmoid(gs_ref[...])
        sm = _sigmoid(gm_ref[...])
        dys_ref[...] = (dmixed * sg).astype(BF16)
        dgs_ref[...] = (dmixed * ys_ref[...] * sg * (1.0 - sg)).astype(BF16)
        dym = (dmixed * sm).astype(BF16)
        dym_ref[...] = dym
        dgm_ref[...] = (dmixed * ym_ref[...] * sm * (1.0 - sm)).astype(BF16)
        dattn_ref[...] = _row_shards_mm_nt(dym, wo_ref)

    r = lambda: _rows(t, D_MODEL)
    bf = jax.ShapeDtypeStruct((l, D_MODEL), BF16)
    return pl.pallas_call(
        body, name="merge_bwd", grid=(l // t,),
        in_specs=[r(), r(), r(), r(), r(), _member_block("w_o_mla"), _member_block("w_out")],
        out_specs=[r(), r(), r(), r(), r()],
        out_shape=[bf, bf, bf, bf, jax.ShapeDtypeStruct((l, D_MODEL), F32)],
        compiler_params=_cparams("parallel"),
    )(dh, y_ssm, y_mla, gs, gm, grp_a, grp_a)


def _mlp_fwd_bwd(h, tgt, g2, grp_a, t):
    l = h.shape[0]

    def body(h_ref, tgt_ref, g_ref, wu_ref, wd_ref, dh_ref, hn_ref, da_ref, hid_ref, dout_ref, loss_ref, dg_ref):
        first = pl.program_id(0) == 0
        hv = h_ref[...]
        g = g_ref[...]
        hn = _rms_fwd(hv, g, D_MODEL).astype(BF16)
        hn_ref[...] = hn
        out = hv
        relus = []
        for s in range(4):
            cols = slice(1024 * s, 1024 * (s + 1))
            relu = jnp.maximum(_mm(hn, wu_ref[s]), 0.0)
            relus.append(relu)
            hid = (relu * relu).astype(BF16)
            hid_ref[:, cols] = hid
            out = out + _mm(hid, wd_ref[s])
        err = out - tgt_ref[...]
        _accumulate(loss_ref, jnp.full((8, 128), jnp.sum(err * err) * (0.5 / D_MODEL), F32), first)
        dout = err * (1.0 / D_MODEL)
        doutb = dout.astype(BF16)
        dout_ref[...] = doutb
        dhn = jnp.zeros_like(hv)
        for s in range(4):
            da = (_mm_nt(doutb, wd_ref[s]) * (2.0 * relus[s])).astype(BF16)
            da_ref[:, 1024 * s:1024 * (s + 1)] = da
            dhn = dhn + _mm_nt(da, wu_ref[s])
        dx, dg_rows = _rms_bwd(hv, g, dhn, D_MODEL)
        dh_ref[...] = dout + dx
        _accumulate(dg_ref, _colsum(dg_rows), first)

    r = lambda w: _rows(t, w)
    return pl.pallas_call(
        body, name="mlp_fwd_bwd", grid=(l // t,),
        in_specs=[r(D_MODEL), r(D_MODEL), _resident((1, D_MODEL)), _member_block("w_up"), _member_block("w_down")],
        out_specs=[r(D_MODEL), r(D_MODEL), r(D_FF), r(D_FF), r(D_MODEL), pl.BlockSpec((8, 128), lambda i: (0, 0)),
                   pl.BlockSpec((1, D_MODEL), lambda i: (0, 0))],
        out_shape=[jax.ShapeDtypeStruct((l, D_MODEL), F32), jax.ShapeDtypeStruct((l, D_MODEL), BF16),
                   jax.ShapeDtypeStruct((l, D_FF), BF16), jax.ShapeDtypeStruct((l, D_FF), BF16),
                   jax.ShapeDtypeStruct((l, D_MODEL), BF16), jax.ShapeDtypeStruct((8, 128), F32),
                   jax.ShapeDtypeStruct((1, D_MODEL), F32)],
        compiler_params=_cparams("arbitrary"),
    )(h, tgt, g2, grp_a, grp_a)


def _wgrad(a, b, name):
    l, m = a.shape
    n = b.shape[1]
    bm = m if m <= 512 else 512
    bl = min(l, 2048 if n <= 1024 else 1024)

    def body(a_ref, b_ref, o_ref):
        _accumulate(o_ref, _mm_tn(a_ref[...], b_ref[...]), pl.program_id(1) == 0)

    return pl.pallas_call(
        body, name=name, grid=(m // bm, l // bl),
        in_specs=[pl.BlockSpec((bl, bm), lambda i, j: (j, i)), pl.BlockSpec((bl, n), lambda i, j: (j, 0))],
        out_specs=pl.BlockSpec((bm, n), lambda i, j: (i, 0)),
        out_shape=jax.ShapeDtypeStruct((m, n), F32),
        compiler_params=_cparams("parallel", "arbitrary"),
    )(a, b)


def _wgrad_into(a, b, member, cut, dest=None):
    group, off, rs, cs = _place_in_group(member)
    l = a.shape[0]
    bm = min(rs, 512)
    bl = min(l, 2048)
    nb = rs // bm
    if cut == "row":
        a_spec = pl.BlockSpec((bl, bm), lambda j, i, k: (k, j * nb + i))
        b_spec = pl.BlockSpec((bl, cs), lambda j, i, k: (k, 0))
    else:
        a_spec = pl.BlockSpec((bl, bm), lambda j, i, k: (k, i))
        b_spec = pl.BlockSpec((bl, cs), lambda j, i, k: (k, j))

    def body(a_ref, b_ref, *rest):
        o_ref = rest[-1]
        part = _mm_tn(a_ref[...], b_ref[...])

        @pl.when(pl.program_id(2) == 0)
        def _():
            o_ref[0] = part

        @pl.when(pl.program_id(2) != 0)
        def _():
            o_ref[0] += part

    operands, in_specs, aliases = [a, b], [a_spec, b_spec], {}
    if dest is not None:
        operands.append(dest)
        in_specs.append(ANY)
        aliases = {2: 0}
    return pl.pallas_call(
        body, name="wgrad_" + member, grid=(4, nb, l // bl), in_specs=in_specs,
        out_specs=pl.BlockSpec((1, bm, cs), lambda j, i, k: (j, off // bm + i, 0)),
        out_shape=jax.ShapeDtypeStruct((4, _group_rows(group), cs), F32), input_output_aliases=aliases,
        compiler_params=_cparams("parallel", "parallel", "arbitrary"),
    )(*operands)


def _adamw(w, g, m, v, name, g_off, token):
    r, c = w.shape
    br = r
    for cand in (256, 128, 64, 32, 16, 8):
        if r % cand == 0 and g_off % cand == 0:
            br = cand
            break

    def body(w_ref, g_ref, m_ref, v_ref, token_ref, go_ref, d_ref, nm_ref, nv_ref):
        gv = g_ref[...]
        go_ref[...] = gv
        nm = ADAM_B1 * m_ref[...] + (1.0 - ADAM_B1) * gv
        nv = ADAM_B2 * v_ref[...] + (1.0 - ADAM_B2) * (gv * gv)
        m_hat = nm / (1.0 - ADAM_B1 ** ADAM_STEP)
        v_hat = nv / (1.0 - ADAM_B2 ** ADAM_STEP)
        d_ref[...] = -ADAM_LR * (m_hat / (jnp.sqrt(v_hat) + ADAM_EPS) + ADAM_WD * w_ref[...])
        nm_ref[...] = nm
        nv_ref[...] = nv

    spec = lambda: pl.BlockSpec((br, c), lambda i: (i, 0))
    g_spec = pl.BlockSpec((br, c), lambda i: (g_off // br + i, 0))
    shp = jax.ShapeDtypeStruct((r, c), F32)
    return pl.pallas_call(
        body, name=name, grid=(r // br,), in_specs=[spec(), g_spec, spec(), spec(), ANY],
        out_specs=[spec(), spec(), spec(), spec()], out_shape=[shp, shp, shp, shp], compiler_params=_cparams("parallel"),
    )(w, g, m, v, token)


def _place():
    return lax.axis_index("x"), lax.axis_index("y"), lax.axis_index("c")


def _other_chips(x, y):
    return [(1 - x, y), (x, 1 - y), (1 - x, 1 - y)]


ANY = pl.BlockSpec(memory_space=pl.ANY)


def _gather_weights(bufs):
    n = len(bufs)

    def body(*refs):
        outs, send_sems, recv_sems = refs[n:2 * n], refs[2 * n], refs[2 * n + 1]
        x, y, c = _place()
        chips = _other_chips(x, y)

        def part(g, px, py, pc):
            half = outs[g].shape[1] // 2
            return outs[g].at[2 * px + py, pl.ds(pl.multiple_of(pc * half, 16), half), :]

        def copy(k, src, dst, to):
            return pltpu.make_async_remote_copy(src_ref=src, dst_ref=dst, send_sem=send_sems.at[k], recv_sem=recv_sems.at[k],
                                                device_id=to, device_id_type=MESH)

        first = [copy(6 * g + j, part(g, x, y, c), part(g, x, y, c), (*chip, c)) for g in range(n) for j, chip in enumerate(chips)]
        for cp in first:
            cp.start()
        passed = []
        for g in range(n):
            for j, chip in enumerate(chips):
                landed = part(g, *chip, c)
                copy(6 * g + j, landed, landed, (x, y, c)).wait_recv()
                passed.append(copy(6 * g + 3 + j, landed, landed, (x, y, 1 - c)))
                passed[-1].start()
        for g in range(n):
            for j, chip in enumerate(chips):
                other = part(g, *chip, 1 - c)
                copy(6 * g + 3 + j, other, other, (x, y, c)).wait_recv()
        for cp in first + passed:
            cp.wait_send()

    return pl.pallas_call(
        body, name="gather_weights", in_specs=[ANY] * n, out_specs=[ANY] * n,
        out_shape=[jax.ShapeDtypeStruct(b.shape, b.dtype) for b in bufs], input_output_aliases={g: g for g in range(n)},
        scratch_shapes=[pltpu.SemaphoreType.DMA((6 * n,)), pltpu.SemaphoreType.DMA((6 * n,))],
    )(*bufs)


def _cast_shards(shards, group, place):
    width, members = GROUPS[group]
    rows = _group_rows(group)

    def body(place_ref, *refs):
        out = refs[-1]
        off = 0
        for ref, (_, r) in zip(refs[:-1], members):
            out[0, off:off + r, :] = ref[...].astype(BF16)
            off += r

    grid_spec = pltpu.PrefetchScalarGridSpec(
        num_scalar_prefetch=1, grid=(1,),
        in_specs=[pl.BlockSpec((r, width), lambda i, p: (0, 0)) for _, r in members],
        out_specs=pl.BlockSpec((1, rows, width), lambda i, p: (p[0], 0, 0)))
    return pl.pallas_call(
        body, name="cast_shards_" + group, grid_spec=grid_spec, out_shape=jax.ShapeDtypeStruct((4, rows, width), BF16),
        compiler_params=_cparams("arbitrary"),
    )(place, *[shards[name] for name, _ in members])


def _block_rows(h):
    return next(cand for cand in (256, 192, 128, 64, 32, 16) if h % cand == 0)


def _add_pair(buf, got, place, name):
    n, h, w = got.shape
    bh = _block_rows(h)
    nb = h // bh

    def body(place_ref, a_ref, b_ref, s_ref, sb_ref):
        s = a_ref[...] + b_ref[...]
        s_ref[...] = s
        sb_ref[...] = s.astype(BF16)

    spec = lambda: pl.BlockSpec((1, bh, w), lambda j, i, p: (j, i, 0))
    grid_spec = pltpu.PrefetchScalarGridSpec(
        num_scalar_prefetch=1, grid=(n, nb),
        in_specs=[pl.BlockSpec((1, bh, w), lambda j, i, p: (j, p[1] * nb + i, 0)), spec()], out_specs=[spec(), spec()])
    return pl.pallas_call(
        body, name=name, grid_spec=grid_spec,
        out_shape=[jax.ShapeDtypeStruct(got.shape, F32), jax.ShapeDtypeStruct(got.shape, BF16)],
        compiler_params=_cparams("parallel", "parallel"),
    )(place, buf, got)


def _add_received(pair, got, place, name):
    _, h, w = pair.shape
    bh = _block_rows(h)
    nb = h // bh

    def body(place_ref, own_ref, got_ref, o_ref):
        o_ref[...] = ((own_ref[0] + got_ref[0].astype(F32)) + got_ref[1].astype(F32)) + got_ref[2].astype(F32)

    grid_spec = pltpu.PrefetchScalarGridSpec(
        num_scalar_prefetch=1, grid=(nb,),
        in_specs=[pl.BlockSpec((1, bh, w), lambda i, p: (p[0], i, 0)), pl.BlockSpec((3, bh, w), lambda i, p: (0, i, 0))],
        out_specs=pl.BlockSpec((bh, w), lambda i, p: (p[1] * nb + i, 0)))
    return pl.pallas_call(
        body, name=name, grid_spec=grid_spec, out_shape=jax.ShapeDtypeStruct((2 * h, w), F32),
        compiler_params=_cparams("parallel"),
    )(place, pair, got)


HBM = pl.BlockSpec(memory_space=pltpu.HBM)
SEM = pl.BlockSpec(memory_space=pltpu.SEMAPHORE)


def _copies_start(name, bufs, n_copies, plan, after=None):
    n = len(bufs)
    extra = [] if after is None else [after]

    def body(*refs):
        sems = refs[n + len(extra):n + len(extra) + 2 * n_copies]
        x, y, c = _place()
        for i, (src, dst, dev) in enumerate(plan(refs[:n], x, y, c)):
            pltpu.make_async_remote_copy(src_ref=src, dst_ref=dst, send_sem=sems[i], recv_sem=sems[n_copies + i],
                                         device_id=dev, device_id_type=MESH).start()
        token = refs[-1]
        token[...] = jnp.zeros_like(token)

    out = pl.pallas_call(
        body, name=name,
        out_shape=[pltpu.SemaphoreType.DMA(())] * (2 * n_copies) + [pltpu.HBM(b.shape, b.dtype) for b in bufs]
        + [jax.ShapeDtypeStruct((8, 128), F32)],
        in_specs=[HBM] * n + [ANY] * len(extra),
        out_specs=[SEM] * (2 * n_copies) + [HBM] * n + [pl.BlockSpec(memory_space=pltpu.VMEM)],
        input_output_aliases={i: 2 * n_copies + i for i in range(n)},
        compiler_params=pltpu.CompilerParams(has_side_effects=pltpu.SideEffectType.DATAFLOW_SIDE_EFFECTING),
    )(*[pltpu.with_memory_space_constraint(b, pltpu.HBM) for b in bufs], *extra)
    return list(out[:2 * n_copies]), list(out[2 * n_copies:-1]), out[-1]


def _copies_wait(name, bufs, sems, after, plan):
    n = len(bufs)
    k = len(sems) // 2

    def body(*refs):
        sem_refs = refs[n:n + 2 * k]
        x, y, c = _place()
        for i, (sent, landed, dev) in enumerate(plan(refs[:n], x, y, c)):
            cp = pltpu.make_async_remote_copy(src_ref=sent, dst_ref=landed, send_sem=sem_refs[i], recv_sem=sem_refs[k + i],
                                              device_id=dev, device_id_type=MESH)
            cp.wait_send()
            cp.wait_recv()

    return pl.pallas_call(
        body, name=name, out_shape=[pltpu.HBM(b.shape, b.dtype) for b in bufs],
        in_specs=[HBM] * n + [SEM] * (2 * k) + [ANY], out_specs=[HBM] * n, input_output_aliases={i: i for i in range(n)},
        compiler_params=pltpu.CompilerParams(has_side_effects=pltpu.SideEffectType.DATAFLOW_SIDE_EFFECTING),
    )(*bufs, *sems, after)


def _row_half(ref, which, axis):
    half = ref.shape[axis] // 2
    rows = pl.ds(pl.multiple_of(which * half, 8), half)
    return ref.at[rows, :] if axis == 0 else ref.at[:, rows, :]


class _SplitGather:
    def __init__(self, own, after):
        self.n = len(own)
        self.state = _copies_start("gather_start", own, 3 * self.n, self._sent, after)

    @staticmethod
    def _sent(refs, x, y, c):
        return [(w.at[2 * x + y], w.at[2 * x + y], (px, py, c)) for w in refs for px, py in _other_chips(x, y)]

    @staticmethod
    def _landed(refs, x, y, c):
        return [(w.at[2 * x + y], w.at[2 * px + py], (px, py, c)) for w in refs for px, py in _other_chips(x, y)]

    def token(self):
        return self.state[2]

    def wait(self, which, name, after):
        sems, bufs, _ = self.state
        k = 3 * self.n
        mine = [sems[3 * i + j] for i in which for j in range(3)] + [sems[k + 3 * i + j] for i in which for j in range(3)]
        return _copies_wait(name, [bufs[i] for i in which], mine, after, self._landed)


class _SplitReduction:
    def __init__(self, tag, groups, place):
        self.tag, self.groups, self.place = tag, groups, place

    def start_pair(self, bufs):
        n = len(bufs)
        lands = [lax.empty((4, b.shape[1] // 2, b.shape[2]), F32) for b in bufs]
        plan = lambda refs, x, y, c: [(_row_half(refs[i], 1 - c, 1), refs[n + i], (x, y, 1 - c)) for i in range(n)]
        self._pair = (_copies_start("pair_%s_start" % self.tag, bufs + lands, n, plan), plan, n)
        return self._pair[0][2]

    def pair_done_start_scatter(self, after):
        (sems, bufs, _), plan, n = self._pair
        out = _copies_wait("pair_%s_wait" % self.tag, bufs, sems, after, plan)
        pairs = [_add_pair(out[i], out[n + i], self.place, "add_pair_" + g) for i, g in enumerate(self.groups)]
        self._pair_f32 = [p[0] for p in pairs]
        lands = [lax.empty((3,) + p[1].shape[1:], BF16) for p in pairs]
        plan = lambda refs, x, y, c: [(refs[i].at[2 * px + py], refs[n + i].at[j], (px, py, c))
                                      for i in range(n) for j, (px, py) in enumerate(_other_chips(x, y))]
        self._scatter = (_copies_start("scatter_%s_start" % self.tag, [p[1] for p in pairs] + lands, 3 * n, plan), plan, n)
        return self._scatter[0][2]

    def scatter_done(self, after):
        (sems, bufs, _), plan, n = self._scatter
        out = _copies_wait("scatter_%s_wait" % self.tag, bufs, sems, after, plan)
        return [_add_received(self._pair_f32[i], out[n + i], self.place, "add_received_" + g)
                for i, g in enumerate(self.groups)]

    def start_join(self, halves):
        n = len(halves)
        sent = lambda refs, x, y, c: [(_row_half(r, c, 0), _row_half(r, c, 0), (x, y, 1 - c)) for r in refs]
        landed = lambda refs, x, y, c: [(_row_half(r, c, 0), _row_half(r, 1 - c, 0), (x, y, 1 - c)) for r in refs]
        self._join = (_copies_start("join_%s_start" % self.tag, halves, n, sent), landed)
        return self._join[0][2]

    def join_done(self, after):
        (sems, bufs, _), landed = self._join
        return _copies_wait("join_%s_wait" % self.tag, bufs, sems, after, landed)


def _all_sum_small(mine):
    rows, w = mine.shape

    def body(in_ref, out_ref, sibling, pair, chips, send_sems, recv_sems):
        x, y, c = _place()
        swap = pltpu.make_async_remote_copy(src_ref=in_ref, dst_ref=sibling, send_sem=send_sems.at[0], recv_sem=recv_sems.at[0],
                                            device_id=(x, y, 1 - c), device_id_type=MESH)
        swap.start()
        swap.wait()
        pair[...] = in_ref[...] + sibling[...]
        chip = 2 * x + y
        chips[chip] = pair[...]
        copies = [pltpu.make_async_remote_copy(src_ref=pair, dst_ref=chips.at[chip], send_sem=send_sems.at[1 + j],
                                               recv_sem=recv_sems.at[1 + j], device_id=(px, py, c), device_id_type=MESH)
                  for j, (px, py) in enumerate(_other_chips(x, y))]
        for cp in copies:
            cp.start()
        for cp in copies:
            cp.wait()
        out_ref[...] = ((chips[0] + chips[1]) + chips[2]) + chips[3]

    return pl.pallas_call(
        body, name="all_sum_small", out_shape=jax.ShapeDtypeStruct((rows, w), F32),
        in_specs=[pl.BlockSpec(memory_space=pltpu.VMEM)], out_specs=pl.BlockSpec(memory_space=pltpu.VMEM),
        scratch_shapes=[pltpu.VMEM((rows, w), F32), pltpu.VMEM((rows, w), F32), pltpu.VMEM((4, rows, w), F32),
                        pltpu.SemaphoreType.DMA((4,)), pltpu.SemaphoreType.DMA((4,))],
        compiler_params=pltpu.CompilerParams(vmem_limit_bytes=VMEM_LIMIT_V7X),
    )(mine)


def _join_column_shards(g):
    return jnp.transpose(g, (1, 0, 2)).reshape(g.shape[1], 4 * g.shape[2])


def _split_column_shards(w):
    r = w.shape[0]
    return jnp.transpose(w.reshape(r, 4, w.shape[1] // 4), (1, 0, 2))


def _small_rows(shape):
    return -(-int(np.prod(shape)) // 1024)


def _pack_small(vals):
    segs = []
    for name, shape in SMALL_WEIGHTS:
        flat = vals[name].reshape(-1)
        segs.append(jnp.pad(flat, (0, _small_rows(shape) * 1024 - flat.shape[0])))
    total = sum(s.shape[0] for s in segs) // 1024
    segs.append(jnp.zeros((-total % 8 * 1024,), F32))
    return jnp.concatenate(segs).reshape(-1, 1024)


def _unpack_small(packed):
    out, off = {}, 0
    for name, shape in SMALL_WEIGHTS:
        rows = _small_rows(shape)
        out[name] = packed[off:off + rows].reshape(-1)[:int(np.prod(shape))].reshape(shape)
        off += rows
    return out


W_IN_SHARD = D_IN // 4
W_IN_GAP = 1216


def _pad_w_in(g):
    cut = W_IN_GAP - W_IN_SHARD
    return jnp.concatenate([g[0], g[1][:, :cut], jnp.zeros((g.shape[1], D_IN_PAD - D_IN), g.dtype), g[1][:, cut:], g[2], g[3]],
                           axis=1)


def _unpad_w_in(g):
    skip = D_IN_PAD - D_IN
    second = jnp.concatenate([g[:, W_IN_SHARD:W_IN_GAP], g[:, W_IN_GAP + skip:2 * W_IN_SHARD + skip]], axis=1)
    return jnp.stack([g[:, :W_IN_SHARD], second, g[:, 2 * W_IN_SHARD + skip:3 * W_IN_SHARD + skip],
                      g[:, 3 * W_IN_SHARD + skip:]])


def _pad_heads(w):
    r = w.shape[0]
    return jnp.pad(w.reshape(r, N_HEADS, QK_HEAD), ((0, 0), (0, 0), (0, HEAD_PAD - QK_HEAD))).reshape(r, N_HEADS * HEAD_PAD)


def _unpad_heads(g):
    r = g.shape[0]
    return g.reshape(r, N_HEADS, HEAD_PAD)[:, :, :QK_HEAD].reshape(r, N_HEADS * QK_HEAD)


def _local_step(x, positions, tgt, grp_b, small, gather, red_a, red_rest):
    l = x.shape[0]
    t = min(l, 512)
    t_mlp = min(l, 256)
    tq = min(l, 1024)
    tc = min(l, 256)
    row = lambda v: v.reshape(1, -1).astype(F32)

    w_in_p = _pad_w_in(grp_b)
    g1, g2 = row(small["norm_mix"]), row(small["norm_mlp"])
    gqa, gkva = row(small["q_a_norm"]), row(small["kv_a_norm"])
    gq = jnp.pad(row(small["q_norm"]), ((0, 0), (0, HEAD_PAD - QK_HEAD)))
    gk = jnp.pad(row(small["k_norm"]), ((0, 0), (0, HEAD_PAD - QK_HEAD)))
    half = QK_ROPE // 2
    inv_freq = ROPE_THETA ** (-jnp.arange(half, dtype=F32) / half)
    invf = jnp.concatenate([inv_freq, inv_freq, jnp.zeros((64,), F32)]).reshape(1, 128)
    sgn = jnp.concatenate([-jnp.ones((half,), F32), jnp.ones((half,), F32), jnp.zeros((64,), F32)]).reshape(1, 128)
    pos = positions.reshape(l, 1)

    a_re, a_im = small["ssm_a_re"], small["ssm_a_im"]
    log_dt = small["ssm_log_dt"].reshape(SSM_GROUPS, 1)
    to_gcp = lambda b: jnp.transpose(b, (0, 2, 1)).reshape(SSM_WIDTH, SSM_STATE)
    from_gcp = lambda b: jnp.transpose(b.reshape(SSM_GROUPS, SSM_GROUP_CH, SSM_STATE), (0, 2, 1))
    b_re, b_im = to_gcp(small["ssm_b_re"]), to_gcp(small["ssm_b_im"])
    c_re, c_im = small["ssm_c_re"].reshape(SSM_WIDTH, SSM_STATE), small["ssm_c_im"].reshape(SSM_WIDTH, SSM_STATE)
    wb, wc, tabs_fwd, tabs_rev = _ssm_param_fwd(a_re, a_im, log_dt, b_re, b_im, c_re, c_im)
    dskip = row(small["ssm_d"])
    b_glu = row(small["b_glu"])

    u, lat, gs, gm = _in_proj_fwd(x, g1, w_in_p, t, gather.token())
    grp_c, grp_d, grp_e = gather.wait([0, 1, 2], "gather_cde_wait", u)
    w_qb_p = _pad_heads(_join_column_shards(grp_c))
    xr, xi, y, y_ssm = _ssm_fwd(u, wb, wc, tabs_fwd, dskip, grp_d, b_glu, grp_e, tc)
    q, k, v = _mla_pre_fwd(lat, pos, invf, sgn, gqa, gkva, gq, gk, w_qb_p, grp_d, t)
    attn, lse = _attn_fwd(q, k, v, tq)
    (grp_a,) = gather.wait([3], "gather_a_wait", attn)
    y_mla, mixed, h = _merge_fwd(attn, y_ssm, gs, gm, x, grp_a, t)
    dh, hn, da, hid, dout, loss_blk, g_norm_mlp = _mlp_fwd_bwd(h, tgt, g2, grp_a, t_mlp)

    ga = _wgrad_into(hn, da, "w_up", "col", _wgrad_into(hid, dout, "w_down", "row"))
    dys, dym, dgs, dgm, dattn = _merge_bwd(dh, y_ssm, y_mla, gs, gm, grp_a, t)
    ga = _wgrad_into(attn, dym, "w_o_mla", "row", _wgrad_into(mixed, dh, "w_out", "row", ga))

    dq, dk, dv = _attn_bwd(q, k, v, attn, dattn, lse, tq, red_a.start_pair([ga]))
    d_lat, ql, dq0, ckn, dkv, g_qa, g_kva, g_q, g_k = _mla_pre_bwd(lat, pos, invf, sgn, gqa, gkva, gq, gk, w_qb_p, grp_d,
                                                                    dq, dk, dv, t, red_a.pair_done_start_scatter(dk))
    gc = _split_column_shards(_unpad_heads(_wgrad(ql, dq0, "wgrad_q_b")))

    d_u, adj, dy, z, z2, dpre, g_b_glu, g_d, g_lr, g_li = _ssm_bwd(
        dys, y, u, xr, xi, wb, wc, tabs_rev, dskip, grp_d, b_glu, grp_e, tc)
    gd = _wgrad_into(z, dpre, "w_glu", "row", _wgrad_into(ckn, dkv, "w_kv_b", "col"))
    ge = _wgrad_into(z2, dys, "w_o_ssm", "col")
    grad_x, xn, dproj, g_norm_mix = _in_proj_bwd(x, g1, w_in_p, d_u, d_lat, dgs, dgm, dh, t)
    gb = _unpad_w_in(_wgrad(xn, dproj, "wgrad_in"))

    red_a.start_join(red_a.scatter_done(gb))
    g_wb = _wgrad_strips(u, adj, adj, "wgrad_ssm_b", 1, red_rest.start_pair([gb, gc, gd, ge]))
    g_wct = _wgrad_strips(dy, xr, xi, "wgrad_ssm_c", 0, red_rest.pair_done_start_scatter(g_wb))
    g_ar, g_ai, g_ldt, g_br, g_bi, g_cr, g_ci = _ssm_param_bwd(a_re, a_im, log_dt, b_re, b_im, g_lr, g_li, g_wb, g_wct)

    g_small = {
        "norm_mix": g_norm_mix.reshape(-1), "norm_mlp": g_norm_mlp.reshape(-1), "q_a_norm": g_qa.reshape(-1),
        "kv_a_norm": g_kva.reshape(-1), "q_norm": g_q.reshape(-1)[:QK_HEAD], "k_norm": g_k.reshape(-1)[:QK_HEAD],
        "ssm_a_re": g_ar, "ssm_a_im": g_ai, "ssm_log_dt": g_ldt.reshape(-1),
        "ssm_b_re": from_gcp(g_br), "ssm_b_im": from_gcp(g_bi),
        "ssm_c_re": g_cr.reshape(SSM_GROUPS, SSM_GROUP_CH, SSM_STATE), "ssm_c_im": g_ci.reshape(SSM_GROUPS, SSM_GROUP_CH, SSM_STATE),
        "ssm_d": g_d.reshape(SSM_GROUPS, SSM_GROUP_CH), "b_glu": g_b_glu.reshape(-1),
    }
    return loss_blk[0, 0], grad_x, g_small


def kernel(x, positions, norm_mix, w_in, q_a_norm, kv_a_norm, w_q_b, w_kv_b, q_norm, k_norm, w_o_mla, ssm_a_re, ssm_a_im, ssm_log_dt, ssm_b_re, ssm_b_im, ssm_c_re, ssm_c_im, ssm_d, w_glu, b_glu, w_o_ssm, w_out, norm_mlp, w_up, w_down, loss_target, m_norm_mix, m_w_in, m_q_a_norm, m_kv_a_norm, m_w_q_b, m_w_kv_b, m_q_norm, m_k_norm, m_w_o_mla, m_ssm_a_re, m_ssm_a_im, m_ssm_log_dt, m_ssm_b_re, m_ssm_b_im, m_ssm_c_re, m_ssm_c_im, m_ssm_d, m_w_glu, m_b_glu, m_w_o_ssm, m_w_out, m_norm_mlp, m_w_up, m_w_down, v_norm_mix, v_w_in, v_q_a_norm, v_kv_a_norm, v_w_q_b, v_w_kv_b, v_q_norm, v_k_norm, v_w_o_mla, v_ssm_a_re, v_ssm_a_im, v_ssm_log_dt, v_ssm_b_re, v_ssm_b_im, v_ssm_c_re, v_ssm_c_im, v_ssm_d, v_w_glu, v_b_glu, v_w_o_ssm, v_w_out, v_norm_mlp, v_w_up, v_w_down):
    args = dict(locals())
    w = {n: args[n][0] for n in WEIGHT_ORDER}
    m = {n: args["m_" + n][0] for n in WEIGHT_ORDER}
    v = {n: args["v_" + n][0] for n in WEIGHT_ORDER}
    big_names = [n for n, *_ in BIG_WEIGHTS]
    small_names = [n for n, _ in SMALL_WEIGHTS]

    place = jnp.stack([2 * lax.axis_index("x") + lax.axis_index("y"), lax.axis_index("c")]).astype(jnp.int32)
    rest = ["b", "c", "d", "e"]

    (grp_b,) = _gather_weights([_cast_shards(w, "b", place)])
    gather = _SplitGather([_cast_shards(w, g, place) for g in ("c", "d", "e", "a")], grp_b)
    red_a = _SplitReduction("a", ["a"], place)
    red_rest = _SplitReduction("rest", rest, place)
    small = {n: w[n] for n in small_names}

    loss_local, grad_x, g_small = _local_step(x[0], positions[0], loss_target[0], grp_b, small, gather, red_a, red_rest)
    loss = lax.psum(loss_local, ("x", "y", "c"))

    grad_w, delta_w, new_m, new_v = {}, {}, {}, {}

    def update(names, reduced, token):
        for n in names:
            g, off, _, _ = _place_in_group(n)
            grad_w[n], delta_w[n], new_m[n], new_v[n] = _adamw(w[n], reduced[g], m[n], v[n], "adamw_" + n, off, token)

    small_sum = _all_sum_small(_pack_small(g_small))
    g_s, d_s, m_s, v_s = _adamw(_pack_small(small), small_sum, _pack_small({n: m[n] for n in small_names}),
                                _pack_small({n: v[n] for n in small_names}), "adamw_small", 0, small_sum)
    g_s, d_s, m_s, v_s = _unpack_small(g_s), _unpack_small(d_s), _unpack_small(m_s), _unpack_small(v_s)
    for n in small_names:
        grad_w[n], delta_w[n], new_m[n], new_v[n] = g_s[n], d_s[n], m_s[n], v_s[n]
    in_a = [n for n, _ in GROUPS["a"][1]]
    reduced_a = {"a": red_a.join_done(small_sum)[0]}
    update(in_a[:1], reduced_a, small_sum)
    join_started = red_rest.start_join(red_rest.scatter_done(new_v[in_a[0]]))
    update(in_a[1:], reduced_a, join_started)
    reduced_rest = dict(zip(rest, red_rest.join_done(new_v[in_a[-1]])))
    update([n for n in big_names if n not in in_a], reduced_rest, join_started)

    lead = lambda d: [d[n][None] for n in WEIGHT_ORDER]
    return (loss, grad_x[None], *lead(grad_w), *lead(delta_w), *lead(new_m), *lead(new_v))
```

```python
import math

import jax
import jax.numpy as jnp
import numpy as np
from jax import lax
from jax.experimental import pallas as pl
from jax.experimental.pallas import tpu as pltpu

F32 = jnp.float32
BF16 = jnp.bfloat16

D_MODEL = 1024
SSM_GROUPS = 32
SSM_GROUP_CH = 16
SSM_WIDTH = 512
SSM_STATE = 64
GP = SSM_GROUPS * SSM_STATE
N_HEADS = 8
QK_NOPE = 128
QK_ROPE = 64
QK_HEAD = 192
HEAD_PAD = 256
V_HEAD = 128
Q_LORA = 384
KV_LORA = 256
LAT_W = 768
D_IN = 3264
D_IN_PAD = 3328
D_FF = 4096
ROPE_THETA = 10000.0
EPS = 1e-6
ATT_SCALE = QK_HEAD ** -0.5

ADAM_LR = 0.001
ADAM_B1 = 0.9
ADAM_B2 = 0.999
ADAM_EPS = 1e-08
ADAM_WD = 0.01
ADAM_STEP = 10

VMEM_LIMIT_V7X = 56 * 1024 * 1024
MESH = pl.DeviceIdType.MESH

BIG_WEIGHTS = (
    ("w_in", 1024, 3264, "col"),
    ("w_q_b", 384, 1536, "col"),
    ("w_kv_b", 256, 2048, "col"),
    ("w_o_mla", 1024, 1024, "row"),
    ("w_glu", 512, 512, "row"),
    ("w_o_ssm", 512, 1024, "col"),
    ("w_out", 1024, 1024, "row"),
    ("w_up", 1024, 4096, "col"),
    ("w_down", 4096, 1024, "row"),
)
GROUPS = {
    "a": (1024, (("w_down", 1024), ("w_up", 1024), ("w_o_mla", 256), ("w_out", 256))),
    "b": (816, (("w_in", 1024),)),
    "c": (384, (("w_q_b", 384),)),
    "d": (512, (("w_kv_b", 256), ("w_glu", 128))),
    "e": (256, (("w_o_ssm", 512),)),
}


def _group_rows(group):
    return sum(r for _, r in GROUPS[group][1])


def _place_in_group(name):
    for group, (width, members) in GROUPS.items():
        off = 0
        for member, rows in members:
            if member == name:
                return group, off, rows, width
            off += rows
    raise KeyError(name)


SMALL_WEIGHTS = (
    ("norm_mix", (1024,)), ("q_a_norm", (384,)), ("kv_a_norm", (256,)), ("q_norm", (192,)), ("k_norm", (192,)),
    ("ssm_a_re", (32, 64)), ("ssm_a_im", (32, 64)), ("ssm_log_dt", (32,)),
    ("ssm_b_re", (32, 64, 16)), ("ssm_b_im", (32, 64, 16)), ("ssm_c_re", (32, 16, 64)), ("ssm_c_im", (32, 16, 64)),
    ("ssm_d", (32, 16)), ("b_glu", (512,)), ("norm_mlp", (1024,)),
)
WEIGHT_ORDER = ('norm_mix', 'w_in', 'q_a_norm', 'kv_a_norm', 'w_q_b', 'w_kv_b', 'q_norm', 'k_norm', 'w_o_mla', 'ssm_a_re',
                'ssm_a_im', 'ssm_log_dt', 'ssm_b_re', 'ssm_b_im', 'ssm_c_re', 'ssm_c_im', 'ssm_d', 'w_glu', 'b_glu',
                'w_o_ssm', 'w_out', 'norm_mlp', 'w_up', 'w_down')


def _cparams(*sem):
    return pltpu.CompilerParams(dimension_semantics=sem if sem else None, vmem_limit_bytes=VMEM_LIMIT_V7X)


def _resident(shape, index=None):
    index = (0,) * len(shape) if index is None else index
    return pl.BlockSpec(shape, lambda *_: index, pipeline_mode=pl.Buffered(1))


def _member_block(name):
    _, off, rows, width = _place_in_group(name)
    return _resident((4, rows, width), (0, off // rows, 0))


def _rows(t, width):
    return pl.BlockSpec((t, width), lambda i: (i, 0))


def _mm(a, b):
    return jnp.dot(a.astype(BF16), b.astype(BF16), preferred_element_type=F32)


def _mm_nt(a, b):
    return lax.dot_general(a.astype(BF16), b.astype(BF16), (((1,), (1,)), ((), ())), preferred_element_type=F32)


def _mm_tn(a, b):
    return lax.dot_general(a.astype(BF16), b.astype(BF16), (((0,), (0,)), ((), ())), preferred_element_type=F32)


def _rms_fwd(x, g, n):
    r = lax.rsqrt(jnp.sum(x * x, axis=-1, keepdims=True) * (1.0 / n) + EPS)
    return x * r * g


def _rms_bwd(x, g, dy, n):
    r = lax.rsqrt(jnp.sum(x * x, axis=-1, keepdims=True) * (1.0 / n) + EPS)
    xh = x * r
    dxh = dy * g
    dx = r * (dxh - xh * (jnp.sum(dxh * xh, axis=-1, keepdims=True) * (1.0 / n)))
    return dx, dy * xh


def _colsum(a):
    return jnp.sum(a, axis=0, keepdims=True)


def _accumulate(ref, value, first):
    @pl.when(first)
    def _():
        ref[...] = value

    @pl.when(jnp.logical_not(first))
    def _():
        ref[...] += value


def _sigmoid(a):
    return 1.0 / (1.0 + jnp.exp(-a))


GELU_C = math.sqrt(2.0 / math.pi)
GELU_A = 0.044715


def _gelu(y):
    return 0.5 * y * (1.0 + jnp.tanh(GELU_C * (y + GELU_A * y * y * y)))


def _gelu_grad(y):
    t = jnp.tanh(GELU_C * (y + GELU_A * y * y * y))
    return 0.5 * (1.0 + t) + 0.5 * y * (1.0 - t * t) * GELU_C * (1.0 + 3.0 * GELU_A * y * y)


def _in_proj_fwd(x, g1, w_in_p, t, token):
    l = x.shape[0]

    def body(x_ref, g_ref, w_ref, token_ref, u_ref, lat_ref, gs_ref, gm_ref):
        xn = _rms_fwd(x_ref[...], g_ref[...], D_MODEL).astype(BF16)
        u_ref[...] = _mm(xn, w_ref[:, 0:512])
        lat_ref[...] = _mm(xn, w_ref[:, 512:1280])
        gs_ref[...] = _mm(xn, w_ref[:, 1280:2304])
        gm_ref[...] = _mm(xn, w_ref[:, 2304:3328])

    return pl.pallas_call(
        body, name="in_proj_fwd", grid=(l // t,),
        in_specs=[_rows(t, D_MODEL), _resident((1, D_MODEL)), _resident((D_MODEL, D_IN_PAD)), ANY],
        out_specs=[_rows(t, 512), _rows(t, LAT_W), _rows(t, D_MODEL), _rows(t, D_MODEL)],
        out_shape=[jax.ShapeDtypeStruct((l, 512), F32), jax.ShapeDtypeStruct((l, LAT_W), F32),
                   jax.ShapeDtypeStruct((l, D_MODEL), F32), jax.ShapeDtypeStruct((l, D_MODEL), F32)],
        compiler_params=_cparams("parallel"),
    )(x, g1, w_in_p, token)


def _in_proj_bwd(x, g1, w_in_p, d_u, d_lat, d_gs, d_gm, dh, t):
    l = x.shape[0]

    def body(x_ref, g_ref, w_ref, du_ref, dlat_ref, dgs_ref, dgm_ref, dh_ref, gx_ref, xn_ref, dproj_ref, dg_ref):
        xv = x_ref[...]
        g = g_ref[...]
        xn_ref[...] = _rms_fwd(xv, g, D_MODEL).astype(BF16)
        dproj_ref[:, 0:512] = du_ref[...]
        dproj_ref[:, 512:1280] = dlat_ref[...]
        dproj_ref[:, 1280:2304] = dgs_ref[...]
        dproj_ref[:, 2304:3328] = dgm_ref[...]
        dxn = _mm_nt(dproj_ref[...], w_ref[...])
        dx, dg_rows = _rms_bwd(xv, g, dxn, D_MODEL)
        gx_ref[...] = dh_ref[...] + dx
        _accumulate(dg_ref, _colsum(dg_rows), pl.program_id(0) == 0)

    return pl.pallas_call(
        body, name="in_proj_bwd", grid=(l // t,),
        in_specs=[_rows(t, D_MODEL), _resident((1, D_MODEL)), _resident((D_MODEL, D_IN_PAD)), _rows(t, 512),
                  _rows(t, LAT_W), _rows(t, D_MODEL), _rows(t, D_MODEL), _rows(t, D_MODEL)],
        out_specs=[_rows(t, D_MODEL), _rows(t, D_MODEL), _rows(t, D_IN_PAD), pl.BlockSpec((1, D_MODEL), lambda i: (0, 0))],
        out_shape=[jax.ShapeDtypeStruct((l, D_MODEL), F32), jax.ShapeDtypeStruct((l, D_MODEL), BF16),
                   jax.ShapeDtypeStruct((l, D_IN_PAD), BF16), jax.ShapeDtypeStruct((1, D_MODEL), F32)],
        compiler_params=_cparams("arbitrary"),
    )(x, g1, w_in_p, d_u, d_lat, d_gs, d_gm, dh)


def _ssm_param_fn(a_re, a_im, log_dt, b_re, b_im):
    dt = jnp.exp(log_dt)
    er = jnp.exp(a_re * dt)
    lr = er * jnp.cos(a_im * dt)
    li = er * jnp.sin(a_im * dt)
    den = a_re * a_re + a_im * a_im
    nr = lr - 1.0
    kr = (nr * a_re + li * a_im) / den
    ki = (li * a_re - nr * a_im) / den
    rows = lambda k: jnp.broadcast_to(k[:, None, :], (SSM_GROUPS, SSM_GROUP_CH, SSM_STATE)).reshape(SSM_WIDTH, SSM_STATE)
    krt, kit = rows(kr), rows(ki)
    return lr, li, krt * b_re - kit * b_im, krt * b_im + kit * b_re


def _state_selector():
    row = lax.broadcasted_iota(jnp.int32, (SSM_STATE, GP), 0)
    col = lax.broadcasted_iota(jnp.int32, (SSM_STATE, GP), 1)
    return jnp.where(jnp.bitwise_and(col, SSM_STATE - 1) == row, 1.0, 0.0).astype(BF16)


def _own_group(rows, rows_per_group_log2):
    row = lax.broadcasted_iota(jnp.int32, (rows, GP), 0)
    col = lax.broadcasted_iota(jnp.int32, (rows, GP), 1)
    return jnp.right_shift(row, rows_per_group_log2) == jnp.right_shift(col, 6)


def _three_bf16(x):
    hi = x.astype(BF16)
    rest = x - hi.astype(F32)
    mid = rest.astype(BF16)
    return hi, mid, (rest - mid.astype(F32)).astype(BF16)


def _spread(x, sel):
    return sum(jnp.dot(part, sel, preferred_element_type=F32) for part in _three_bf16(x))


def _collect(xw, sel):
    return sum(lax.dot_general(part, sel, (((1,), (1,)), ((), ())), preferred_element_type=F32) for part in _three_bf16(xw))


def _ssm_param_fwd(a_re, a_im, log_dt, b_re, b_im, c_re, c_im):
    def body(ar_ref, ai_ref, ldt_ref, br_ref, bi_ref, cr_ref, ci_ref, wb_ref, wct_ref, tf_ref, tr_ref):
        lr, li, bbr, bbi = _ssm_param_fn(ar_ref[...], ai_ref[...], ldt_ref[...], br_ref[...], bi_ref[...])
        sel = _state_selector()
        own16 = _own_group(SSM_WIDTH, 4)
        own1 = _own_group(SSM_GROUPS, 0)
        block = lambda m: jnp.where(own16, jnp.dot(m.astype(BF16), sel, preferred_element_type=F32), 0.0).astype(BF16)
        wb_ref[:, 0:GP] = block(bbr)
        wb_ref[:, GP:2 * GP] = block(bbi)
        wct_ref[:, 0:GP] = block(cr_ref[...])
        wct_ref[:, GP:2 * GP] = block(-ci_ref[...])
        flat = lambda m: _colsum(jnp.where(own1, _spread(m, sel), 0.0))
        pr, pi = [], []
        qr, qi = lr, li
        for _ in range(8):
            pr.append(flat(qr))
            pi.append(flat(qi))
            qr, qi = qr * lr - qi * li, qr * li + qi * lr
        row = lax.broadcasted_iota(jnp.int32, (8, GP), 0)
        for n, k in enumerate((1, 2, 4)):
            tf_ref[2 * n] = jnp.where(row >= k, pr[k - 1], 0.0)
            tf_ref[2 * n + 1] = jnp.where(row >= k, pi[k - 1], 0.0)
            tr_ref[2 * n] = jnp.where(row < 8 - k, pr[k - 1], 0.0)
            tr_ref[2 * n + 1] = jnp.where(row < 8 - k, -pi[k - 1], 0.0)
        pick = lambda vals: sum(jnp.where(row == j, v, 0.0) for j, v in enumerate(vals))
        tf_ref[6] = pick(pr)
        tf_ref[7] = pick(pi)
        tr_ref[6] = pick(pr[::-1])
        tr_ref[7] = pick([-v for v in pi[::-1]])

    return pl.pallas_call(
        body, name="ssm_param_fwd",
        out_shape=[jax.ShapeDtypeStruct((SSM_WIDTH, 2 * GP), BF16), jax.ShapeDtypeStruct((SSM_WIDTH, 2 * GP), BF16),
                   jax.ShapeDtypeStruct((8, 8, GP), F32), jax.ShapeDtypeStruct((8, 8, GP), F32)],
        compiler_params=_cparams(),
    )(a_re, a_im, log_dt, b_re, b_im, c_re, c_im)


STRIP_CH = 128
STRIP_ST = 512
N_STRIPS = SSM_WIDTH // STRIP_CH


def _ssm_param_bwd(a_re, a_im, log_dt, b_re, b_im, g_lr, g_li, g_wb, g_wct):
    def body(ar_ref, ai_ref, ldt_ref, br_ref, bi_ref, glr_ref, gli_ref, gwb_ref, gwc_ref,
             o_ar, o_ai, o_ldt, o_br, o_bi, o_cr, o_ci):
        sel = _state_selector()
        own1 = _own_group(SSM_GROUPS, 0)
        row = lax.broadcasted_iota(jnp.int32, (SSM_WIDTH, STRIP_ST), 0)
        col = lax.broadcasted_iota(jnp.int32, (SSM_WIDTH, STRIP_ST), 1)
        own = jnp.bitwise_and(jnp.right_shift(row, 4), 7) == jnp.right_shift(col, 6)
        blocks = lambda m: _collect(jnp.where(own, m, 0.0), sel[:, 0:STRIP_ST])
        unflat = lambda v: _collect(jnp.where(own1, v, 0.0), sel)
        _, vjp = jax.vjp(_ssm_param_fn, ar_ref[...], ai_ref[...], ldt_ref[...], br_ref[...], bi_ref[...])
        d_ar, d_ai, d_ldt, d_br, d_bi = vjp((unflat(glr_ref[...]), unflat(gli_ref[...]),
                                             blocks(gwb_ref[:, 0:STRIP_ST]), blocks(gwb_ref[:, STRIP_ST:2 * STRIP_ST])))
        o_ar[...] = d_ar
        o_ai[...] = d_ai
        o_ldt[...] = d_ldt
        o_br[...] = d_br
        o_bi[...] = d_bi
        o_cr[...] = blocks(gwc_ref[:, 0:STRIP_ST])
        o_ci[...] = -blocks(gwc_ref[:, STRIP_ST:2 * STRIP_ST])

    g, p = SSM_GROUPS, SSM_STATE
    gp = jax.ShapeDtypeStruct((g, p), F32)
    gcp = jax.ShapeDtypeStruct((SSM_WIDTH, p), F32)
    return pl.pallas_call(
        body, name="ssm_param_bwd", out_shape=[gp, gp, jax.ShapeDtypeStruct((g, 1), F32), gcp, gcp, gcp, gcp],
        compiler_params=_cparams(),
    )(a_re, a_im, log_dt, b_re, b_im, g_lr, g_li, g_wb, g_wct)


def _strip(ref, j, im):
    return ref[STRIP_CH * j:STRIP_CH * (j + 1), im * GP + STRIP_ST * j:im * GP + STRIP_ST * (j + 1)]


def _wgrad_strips(a, b_re, b_im, name, im_block, token):
    l = a.shape[0]
    bl = min(l, 512)

    def body(a_ref, bre_ref, bim_ref, token_ref, o_ref):
        first = pl.program_id(0) == 0
        for j in range(N_STRIPS):
            aj = a_ref[:, STRIP_CH * j:STRIP_CH * (j + 1)]
            states = slice(STRIP_ST * j, STRIP_ST * (j + 1))
            _accumulate(o_ref.at[STRIP_CH * j:STRIP_CH * (j + 1), 0:STRIP_ST], _mm_tn(aj, bre_ref[:, states]), first)
            _accumulate(o_ref.at[STRIP_CH * j:STRIP_CH * (j + 1), STRIP_ST:2 * STRIP_ST], _mm_tn(aj, bim_ref[:, states]), first)

    return pl.pallas_call(
        body, name=name, grid=(l // bl,),
        in_specs=[pl.BlockSpec((bl, SSM_WIDTH), lambda k: (k, 0)), pl.BlockSpec((bl, GP), lambda k: (k, 0)),
                  pl.BlockSpec((bl, GP), lambda k: (k, im_block)), ANY],
        out_specs=pl.BlockSpec((SSM_WIDTH, 2 * STRIP_ST), lambda k: (0, 0)),
        out_shape=jax.ShapeDtypeStruct((SSM_WIDTH, 2 * STRIP_ST), F32),
        compiler_params=_cparams("arbitrary"),
    )(a, b_re, b_im, token)


SCAN_STRIP = 512


def _scan_chunk(inr_ref, ini_ref, outr_ref, outi_ref, cr_ref, ci_ref, tab_ref, tc, reverse):
    n_blocks = tc // 8

    def block(j, _):
        i = (n_blocks - 1 - j) if reverse else j
        rows = pl.ds(pl.multiple_of(i * 8, 8), 8)
        for s in range(GP // SCAN_STRIP):
            sl = pl.ds(s * SCAN_STRIP, SCAN_STRIP)
            xr = inr_ref[rows, sl]
            xi = ini_ref[rows, sl]
            for n, k in enumerate((1, 2, 4)):
                shift = (8 - k) if reverse else k
                sr = pltpu.roll(xr, shift, 0)
                si = pltpu.roll(xi, shift, 0)
                mr = tab_ref[2 * n, :, sl]
                mi = tab_ref[2 * n + 1, :, sl]
                xr, xi = xr + mr * sr - mi * si, xi + mr * si + mi * sr
            qr = tab_ref[6, :, sl]
            qi = tab_ref[7, :, sl]
            cr = cr_ref[:, sl]
            ci = ci_ref[:, sl]
            xr, xi = xr + qr * cr - qi * ci, xi + qr * ci + qi * cr
            outr_ref[rows, sl] = xr
            outi_ref[rows, sl] = xi
            edge = 0 if reverse else 7
            cr_ref[:, sl] = jnp.broadcast_to(xr[edge:edge + 1, :], (8, SCAN_STRIP))
            ci_ref[:, sl] = jnp.broadcast_to(xi[edge:edge + 1, :], (8, SCAN_STRIP))
        return 0

    lax.fori_loop(0, n_blocks, block, 0)


def _glu_pre(z, wg_ref):
    return sum(_mm(z[:, 128 * j:128 * (j + 1)], wg_ref[j]) for j in range(4))


def _ssm_fwd(u, wb, wc, tabs, dskip, grp_d, b_glu, grp_e, tc):
    l = u.shape[0]

    def body(u_ref, wb_ref, wc_ref, tab_ref, d_ref, wg_ref, bg_ref, wo_ref, xr_ref, xi_ref, y_ref, ys_ref,
             bur, bui, cr, ci):
        @pl.when(pl.program_id(0) == 0)
        def _():
            cr[...] = jnp.zeros_like(cr)
            ci[...] = jnp.zeros_like(ci)

        uv = u_ref[...]
        ub = uv.astype(BF16)
        for j in range(N_STRIPS):
            uj = ub[:, STRIP_CH * j:STRIP_CH * (j + 1)]
            states = slice(STRIP_ST * j, STRIP_ST * (j + 1))
            bur[:, states] = _mm(uj, _strip(wb_ref, j, 0))
            bui[:, states] = _mm(uj, _strip(wb_ref, j, 1))
        _scan_chunk(bur, bui, xr_ref, xi_ref, cr, ci, tab_ref, tc, False)
        y = jnp.concatenate(
            [_mm_nt(xr_ref[:, STRIP_ST * j:STRIP_ST * (j + 1)], _strip(wc_ref, j, 0))
             + _mm_nt(xi_ref[:, STRIP_ST * j:STRIP_ST * (j + 1)], _strip(wc_ref, j, 1)) for j in range(N_STRIPS)],
            axis=-1) + d_ref[...] * uv
        y_ref[...] = y
        z = _gelu(y)
        z2 = z * _sigmoid(_glu_pre(z, wg_ref) + bg_ref[...])
        for s in range(4):
            ys_ref[:, 256 * s:256 * (s + 1)] = _mm(z2, wo_ref[s])

    return pl.pallas_call(
        body, name="ssm_fwd", grid=(l // tc,),
        in_specs=[_rows(tc, 512), _resident((512, 2 * GP)), _resident((512, 2 * GP)), _resident((8, 8, GP)),
                  _resident((1, 512)), _member_block("w_glu"), _resident((1, 512)), _member_block("w_o_ssm")],
        out_specs=[_rows(tc, GP), _rows(tc, GP), _rows(tc, 512), _rows(tc, D_MODEL)],
        out_shape=[jax.ShapeDtypeStruct((l, GP), F32), jax.ShapeDtypeStruct((l, GP), F32),
                   jax.ShapeDtypeStruct((l, 512), F32), jax.ShapeDtypeStruct((l, D_MODEL), F32)],
        scratch_shapes=[pltpu.VMEM((tc, GP), F32), pltpu.VMEM((tc, GP), F32), pltpu.VMEM((8, GP), F32),
                        pltpu.VMEM((8, GP), F32)],
        compiler_params=_cparams("arbitrary"),
    )(u, wb, wc, tabs, dskip, grp_d, b_glu, grp_e)


def _ssm_bwd(dys, y, u, xr, xi, wb, wc, tabs_rev, dskip, grp_d, b_glu, grp_e, tc):
    l = u.shape[0]
    nc = l // tc

    def body(dys_ref, y_ref, u_ref, xr_ref, xi_ref, wb_ref, wc_ref, tab_ref, d_ref, wg_ref, bg_ref, wo_ref,
             du_ref, a_ref, dy_ref, z_ref, z2_ref, dpre_ref, gb_ref, gd_ref, glr_ref, gli_ref,
             dxr, dxi, ar, ai, cr, ci):
        first = pl.program_id(0) == 0

        @pl.when(first)
        def _():
            cr[...] = jnp.zeros_like(cr)
            ci[...] = jnp.zeros_like(ci)

        yv = y_ref[...]
        uv = u_ref[...]
        dz2 = sum(_mm_nt(dys_ref[:, 256 * j:256 * (j + 1)], wo_ref[j]) for j in range(4))
        z = _gelu(yv)
        s = _sigmoid(_glu_pre(z, wg_ref) + bg_ref[...])
        dpre = dz2 * z * s * (1.0 - s)
        dpreb = dpre.astype(BF16)
        dz = dz2 * s + jnp.concatenate([_mm_nt(dpreb, wg_ref[j]) for j in range(4)], axis=-1)
        dy = dz * _gelu_grad(yv)
        z_ref[...] = z.astype(BF16)
        z2_ref[...] = (z * s).astype(BF16)
        dpre_ref[...] = dpre.astype(BF16)
        dy_ref[...] = dy.astype(BF16)
        _accumulate(gb_ref, _colsum(dpre), first)
        _accumulate(gd_ref, _colsum(dy * uv), first)

        dyb = dy.astype(BF16)
        for j in range(N_STRIPS):
            dyj = dyb[:, STRIP_CH * j:STRIP_CH * (j + 1)]
            dxr[:, STRIP_ST * j:STRIP_ST * (j + 1)] = _mm(dyj, _strip(wc_ref, j, 0))
            dxi[:, STRIP_ST * j:STRIP_ST * (j + 1)] = _mm(dyj, _strip(wc_ref, j, 1))
        ar[pl.ds(tc, 8), :] = cr[...]
        ai[pl.ds(tc, 8), :] = ci[...]
        _scan_chunk(dxr, dxi, ar, ai, cr, ci, tab_ref, tc, True)
        a_ref[:, 0:GP] = ar[pl.ds(0, tc), :].astype(BF16)
        a_ref[:, GP:2 * GP] = ai[pl.ds(0, tc), :].astype(BF16)
        du_states = jnp.concatenate(
            [_mm_nt(a_ref[:, STRIP_ST * j:STRIP_ST * (j + 1)], _strip(wb_ref, j, 0))
             + _mm_nt(a_ref[:, GP + STRIP_ST * j:GP + STRIP_ST * (j + 1)], _strip(wb_ref, j, 1)) for j in range(N_STRIPS)],
            axis=-1)
        du_ref[...] = (dy * d_ref[...] + du_states).astype(BF16)
        anr = ar[pl.ds(1, tc), :]
        ani = ai[pl.ds(1, tc), :]
        xrv = xr_ref[...]
        xiv = xi_ref[...]
        _accumulate(glr_ref, _colsum(anr * xrv + ani * xiv), first)
        _accumulate(gli_ref, _colsum(ani * xrv - anr * xiv), first)

    rev = lambda w: pl.BlockSpec((tc, w), lambda i: (nc - 1 - i, 0))
    acc = lambda w: pl.BlockSpec((1, w), lambda i: (0, 0))
    bf = jax.ShapeDtypeStruct((l, 512), BF16)
    return pl.pallas_call(
        body, name="ssm_bwd", grid=(nc,),
        in_specs=[rev(D_MODEL), rev(512), rev(512), rev(GP), rev(GP), _resident((512, 2 * GP)), _resident((512, 2 * GP)),
                  _resident((8, 8, GP)), _resident((1, 512)), _member_block("w_glu"), _resident((1, 512)),
                  _member_block("w_o_ssm")],
        out_specs=[rev(512), rev(2 * GP), rev(512), rev(512), rev(512), rev(512), acc(512), acc(512), acc(GP), acc(GP)],
        out_shape=[bf, jax.ShapeDtypeStruct((l, 2 * GP), BF16), bf, bf, bf, bf,
                   jax.ShapeDtypeStruct((1, 512), F32), jax.ShapeDtypeStruct((1, 512), F32),
                   jax.ShapeDtypeStruct((1, GP), F32), jax.ShapeDtypeStruct((1, GP), F32)],
        scratch_shapes=[pltpu.VMEM((tc, GP), F32), pltpu.VMEM((tc, GP), F32), pltpu.VMEM((tc + 8, GP), F32),
                        pltpu.VMEM((tc + 8, GP), F32), pltpu.VMEM((8, GP), F32), pltpu.VMEM((8, GP), F32)],
        compiler_params=_cparams("arbitrary"),
    )(dys, y, u, xr, xi, wb, wc, tabs_rev, dskip, grp_d, b_glu, grp_e)


def _swap_halves(b):
    lane = lax.broadcasted_iota(jnp.int32, b.shape, 1)
    return jnp.where(lane < 32, pltpu.roll(b, 96, 1), pltpu.roll(b, 32, 1))


def _rope_tables(pos_ref, invf_ref, sgn_ref):
    ang = pos_ref[...].astype(F32) * invf_ref[...]
    return jnp.cos(ang), jnp.sin(ang) * sgn_ref[...]


def _mla_pre_fwd(lat, pos, invf, sgn, gqa, gkva, gq, gk, w_qb_p, w_kvb, t):
    l = lat.shape[0]

    def body(lat_ref, pos_ref, invf_ref, sgn_ref, gqa_ref, gkva_ref, gq_ref, gk_ref, wq_ref, wkv_ref, q_ref, k_ref, v_ref):
        cs, sn = _rope_tables(pos_ref, invf_ref, sgn_ref)
        ql = _rms_fwd(lat_ref[:, 0:Q_LORA], gqa_ref[...], Q_LORA)
        ckn = _rms_fwd(lat_ref[:, Q_LORA:Q_LORA + KV_LORA], gkva_ref[...], KV_LORA)
        kpe = lat_ref[:, 640:768]
        q0 = _mm(ql, wq_ref[...])
        cknb = ckn.astype(BF16)
        kv = jnp.concatenate([_mm(cknb, wkv_ref[s]) for s in range(4)], axis=-1)
        for h in range(N_HEADS):
            q1 = _rms_fwd(q0[:, HEAD_PAD * h:HEAD_PAD * (h + 1)], gq_ref[...], QK_HEAD)
            b = q1[:, 128:256]
            q_ref[h, :, 0:128] = (q1[:, 0:128] * ATT_SCALE).astype(BF16)
            q_ref[h, :, 128:256] = ((b * cs + _swap_halves(b) * sn) * ATT_SCALE).astype(BF16)
            k0 = jnp.concatenate([kv[:, 256 * h:256 * h + 128], kpe], axis=-1)
            k1 = _rms_fwd(k0, gk_ref[...], QK_HEAD)
            b = k1[:, 128:256]
            k_ref[h, :, 0:128] = k1[:, 0:128].astype(BF16)
            k_ref[h, :, 128:256] = (b * cs + _swap_halves(b) * sn).astype(BF16)
            v_ref[h] = kv[:, 256 * h + 128:256 * h + 256].astype(BF16)

    heads = lambda w: pl.BlockSpec((N_HEADS, t, w), lambda i: (0, i, 0))
    return pl.pallas_call(
        body, name="mla_pre_fwd", grid=(l // t,),
        in_specs=[_rows(t, LAT_W), _rows(t, 1), _resident((1, 128)), _resident((1, 128)), _resident((1, Q_LORA)),
                  _resident((1, KV_LORA)), _resident((1, HEAD_PAD)), _resident((1, HEAD_PAD)),
                  _resident((Q_LORA, N_HEADS * HEAD_PAD)), _member_block("w_kv_b")],
        out_specs=[heads(HEAD_PAD), heads(HEAD_PAD), heads(V_HEAD)],
        out_shape=[jax.ShapeDtypeStruct((N_HEADS, l, HEAD_PAD), BF16), jax.ShapeDtypeStruct((N_HEADS, l, HEAD_PAD), BF16),
                   jax.ShapeDtypeStruct((N_HEADS, l, V_HEAD), BF16)],
        compiler_params=_cparams("parallel"),
    )(lat, pos, invf, sgn, gqa, gkva, gq, gk, w_qb_p, w_kvb)


def _mla_pre_bwd(lat, pos, invf, sgn, gqa, gkva, gq, gk, w_qb_p, w_kvb, dq, dk, dv, t, token):
    l = lat.shape[0]

    def body(lat_ref, pos_ref, invf_ref, sgn_ref, gqa_ref, gkva_ref, gq_ref, gk_ref, wq_ref, wkv_ref, dq_ref, dk_ref, dv_ref,
             token_ref, dlat_ref, ql_ref, dq0_ref, ckn_ref, dkv_ref, ggqa_ref, ggkva_ref, ggq_ref, ggk_ref):
        first = pl.program_id(0) == 0
        cs, sn = _rope_tables(pos_ref, invf_ref, sgn_ref)
        q_lat = lat_ref[:, 0:Q_LORA]
        c_kv = lat_ref[:, Q_LORA:Q_LORA + KV_LORA]
        kpe = lat_ref[:, 640:768]
        ql = _rms_fwd(q_lat, gqa_ref[...], Q_LORA)
        ckn = _rms_fwd(c_kv, gkva_ref[...], KV_LORA)
        ql_ref[...] = ql.astype(BF16)
        ckn_ref[...] = ckn.astype(BF16)
        q0 = _mm(ql, wq_ref[...])
        cknb = ckn.astype(BF16)
        kv = jnp.concatenate([_mm(cknb, wkv_ref[s]) for s in range(4)], axis=-1)
        dkpe = jnp.zeros_like(kpe)
        ggq = jnp.zeros((1, HEAD_PAD), F32)
        ggk = jnp.zeros((1, HEAD_PAD), F32)

        def unrope(d):
            b = d[:, 128:256]
            return jnp.concatenate([d[:, 0:128], b * cs + _swap_halves(b * sn)], axis=-1)

        for h in range(N_HEADS):
            dq1 = unrope(dq_ref[h] * ATT_SCALE)
            dq0h, gq_rows = _rms_bwd(q0[:, HEAD_PAD * h:HEAD_PAD * (h + 1)], gq_ref[...], dq1, QK_HEAD)
            ggq = ggq + _colsum(gq_rows)
            dq0_ref[:, HEAD_PAD * h:HEAD_PAD * (h + 1)] = dq0h.astype(BF16)
            k0 = jnp.concatenate([kv[:, 256 * h:256 * h + 128], kpe], axis=-1)
            dk0, gk_rows = _rms_bwd(k0, gk_ref[...], unrope(dk_ref[h]), QK_HEAD)
            ggk = ggk + _colsum(gk_rows)
            dkpe = dkpe + dk0[:, 128:256]
            dkv_ref[:, 256 * h:256 * h + 128] = dk0[:, 0:128].astype(BF16)
            dkv_ref[:, 256 * h + 128:256 * h + 256] = dv_ref[h].astype(BF16)
        dql = _mm_nt(dq0_ref[...], wq_ref[...])
        dckn = sum(_mm_nt(dkv_ref[:, 512 * s:512 * (s + 1)], wkv_ref[s]) for s in range(4))
        dq_lat, gqa_rows = _rms_bwd(q_lat, gqa_ref[...], dql, Q_LORA)
        dc_kv, gkva_rows = _rms_bwd(c_kv, gkva_ref[...], dckn, KV_LORA)
        dlat_ref[:, 0:Q_LORA] = dq_lat.astype(BF16)
        dlat_ref[:, Q_LORA:Q_LORA + KV_LORA] = dc_kv.astype(BF16)
        dlat_ref[:, 640:768] = dkpe.astype(BF16)
        _accumulate(ggqa_ref, _colsum(gqa_rows), first)
        _accumulate(ggkva_ref, _colsum(gkva_rows), first)
        _accumulate(ggq_ref, ggq, first)
        _accumulate(ggk_ref, ggk, first)

    heads = lambda w: pl.BlockSpec((N_HEADS, t, w), lambda i: (0, i, 0))
    acc = lambda w: pl.BlockSpec((1, w), lambda i: (0, 0))
    return pl.pallas_call(
        body, name="mla_pre_bwd", grid=(l // t,),
        in_specs=[_rows(t, LAT_W), _rows(t, 1), _resident((1, 128)), _resident((1, 128)), _resident((1, Q_LORA)),
                  _resident((1, KV_LORA)), _resident((1, HEAD_PAD)), _resident((1, HEAD_PAD)),
                  _resident((Q_LORA, N_HEADS * HEAD_PAD)), _member_block("w_kv_b"),
                  heads(HEAD_PAD), heads(HEAD_PAD), heads(V_HEAD), ANY],
        out_specs=[_rows(t, LAT_W), _rows(t, Q_LORA), _rows(t, N_HEADS * HEAD_PAD), _rows(t, KV_LORA), _rows(t, N_HEADS * 256),
                   acc(Q_LORA), acc(KV_LORA), acc(HEAD_PAD), acc(HEAD_PAD)],
        out_shape=[jax.ShapeDtypeStruct((l, LAT_W), BF16), jax.ShapeDtypeStruct((l, Q_LORA), BF16),
                   jax.ShapeDtypeStruct((l, N_HEADS * HEAD_PAD), BF16), jax.ShapeDtypeStruct((l, KV_LORA), BF16),
                   jax.ShapeDtypeStruct((l, N_HEADS * 256), BF16), jax.ShapeDtypeStruct((1, Q_LORA), F32),
                   jax.ShapeDtypeStruct((1, KV_LORA), F32), jax.ShapeDtypeStruct((1, HEAD_PAD), F32),
                   jax.ShapeDtypeStruct((1, HEAD_PAD), F32)],
        compiler_params=_cparams("arbitrary"),
    )(lat, pos, invf, sgn, gqa, gkva, gq, gk, w_qb_p, w_kvb, dq, dk, dv, token)


def _causal(s, transposed):
    row = lax.broadcasted_iota(jnp.int32, s.shape, 0)
    col = lax.broadcasted_iota(jnp.int32, s.shape, 1)
    keep = (row <= col) if transposed else (col <= row)
    return jnp.where(keep, s, -jnp.inf)


def _as_row(col):
    n = col.shape[0]
    row = lax.broadcasted_iota(jnp.int32, (n, n), 0)
    lane = lax.broadcasted_iota(jnp.int32, (n, n), 1)
    return jnp.sum(jnp.where(row == lane, col, 0.0), axis=0, keepdims=True)


def _attn_fwd(q, k, v, tq):
    l = q.shape[1]

    hb = 2

    def body(q_ref, k_ref, v_ref, o_ref, lse_ref):
        qi = pl.program_id(1)
        qs = [q_ref[a] for a in range(hb)]

        def step(kb, carry, masked):
            rows = pl.ds(pl.multiple_of(kb * tq, tq), tq)
            out = []
            for a, (m, den, acc) in enumerate(carry):
                s = _mm_nt(qs[a], k_ref[a, rows, :])
                if masked:
                    s = _causal(s, False)
                m_new = jnp.maximum(m, jnp.max(s, axis=-1, keepdims=True))
                alpha = jnp.exp(m - m_new)
                p = jnp.exp(s - m_new)
                den = alpha * den + jnp.sum(p, axis=-1, keepdims=True)
                acc = alpha * acc + _mm(p, v_ref[a, rows, :])
                out.append((m_new, den, acc))
            return tuple(out)

        init = tuple((jnp.full((tq, 1), -jnp.inf, F32), jnp.zeros((tq, 1), F32), jnp.zeros((tq, V_HEAD), F32))
                     for _ in range(hb))
        carry = lax.fori_loop(0, qi, lambda kb, c: step(kb, c, False), init)
        for a, (m, den, acc) in enumerate(step(qi, carry, True)):
            o_ref[:, V_HEAD * a:V_HEAD * (a + 1)] = acc / den
            lse_ref[a, 0] = _as_row(m + jnp.log(den))

    return pl.pallas_call(
        body, name="attn_fwd", grid=(N_HEADS // hb, l // tq),
        in_specs=[pl.BlockSpec((hb, tq, HEAD_PAD), lambda h, i: (h, i, 0)), pl.BlockSpec((hb, l, HEAD_PAD), lambda h, i: (h, 0, 0)),
                  pl.BlockSpec((hb, l, V_HEAD), lambda h, i: (h, 0, 0))],
        out_specs=[pl.BlockSpec((tq, hb * V_HEAD), lambda h, i: (i, h)), pl.BlockSpec((hb, 1, 1, tq), lambda h, i: (h, i, 0, 0))],
        out_shape=[jax.ShapeDtypeStruct((l, N_HEADS * V_HEAD), F32), jax.ShapeDtypeStruct((N_HEADS, l // tq, 1, tq), F32)],
        compiler_params=_cparams("parallel", "arbitrary"),
    )(q, k, v)


def _attn_bwd(q, k, v, o, do, lse_t, tq, token):
    l = q.shape[1]
    nq = l // tq

    hb = 1

    def body(q_ref, k_ref, v_ref, o_ref, do_ref, lse_ref, token_ref, dq_ref, dk_ref, dv_ref):
        ki = pl.program_id(1)

        @pl.when(ki == 0)
        def _():
            dq_ref[...] = jnp.zeros_like(dq_ref)

        kblks = [k_ref[a] for a in range(hb)]
        vblks = [v_ref[a] for a in range(hb)]
        ones = jnp.ones((8, V_HEAD), BF16)

        def step(qb, carry, masked):
            rows = pl.ds(pl.multiple_of(qb * tq, tq), tq)
            out = []
            for a, (dk, dv) in enumerate(carry):
                cols = slice(V_HEAD * a, V_HEAD * (a + 1))
                qblk = q_ref[a, rows, :]
                dov = do_ref[rows, cols]
                dob = dov.astype(BF16)
                delta = sum(_mm_nt(ones, part) for part in _three_bf16(dov * o_ref[rows, cols]))[0:1, :]
                st = _mm_nt(kblks[a], qblk)
                if masked:
                    st = _causal(st, True)
                pt = jnp.exp(st - lse_ref[a, qb])
                dv = dv + _mm(pt, dob)
                dst = (pt * (_mm_nt(vblks[a], dob) - delta)).astype(BF16)
                dk = dk + _mm(dst, qblk)
                dq_ref[a, rows, :] += _mm_tn(dst, kblks[a])
                out.append((dk, dv))
            return tuple(out)

        init = tuple((jnp.zeros((tq, HEAD_PAD), F32), jnp.zeros((tq, V_HEAD), F32)) for _ in range(hb))
        carry = lax.fori_loop(ki + 1, nq, lambda qb, c: step(qb, c, False), step(ki, init, True))
        for a, (dk, dv) in enumerate(carry):
            dk_ref[a] = dk
            dv_ref[a] = dv

    return pl.pallas_call(
        body, name="attn_bwd", grid=(N_HEADS // hb, nq),
        in_specs=[pl.BlockSpec((hb, l, HEAD_PAD), lambda h, i: (h, 0, 0)), pl.BlockSpec((hb, tq, HEAD_PAD), lambda h, i: (h, i, 0)),
                  pl.BlockSpec((hb, tq, V_HEAD), lambda h, i: (h, i, 0)), pl.BlockSpec((l, hb * V_HEAD), lambda h, i: (0, h)),
                  pl.BlockSpec((l, hb * V_HEAD), lambda h, i: (0, h)), pl.BlockSpec((hb, nq, 1, tq), lambda h, i: (h, 0, 0, 0)), ANY],
        out_specs=[pl.BlockSpec((hb, l, HEAD_PAD), lambda h, i: (h, 0, 0)), pl.BlockSpec((hb, tq, HEAD_PAD), lambda h, i: (h, i, 0)),
                   pl.BlockSpec((hb, tq, V_HEAD), lambda h, i: (h, i, 0))],
        out_shape=[jax.ShapeDtypeStruct((N_HEADS, l, HEAD_PAD), F32), jax.ShapeDtypeStruct((N_HEADS, l, HEAD_PAD), F32),
                   jax.ShapeDtypeStruct((N_HEADS, l, V_HEAD), F32)],
        compiler_params=_cparams("parallel", "arbitrary"),
    )(q, k, v, o, do, lse_t, token)


def _row_shards_mm(a, w_ref):
    a = a.astype(BF16)
    return sum(_mm(a[:, 256 * j:256 * (j + 1)], w_ref[j]) for j in range(4))


def _row_shards_mm_nt(a, w_ref):
    a = a.astype(BF16)
    return jnp.concatenate([_mm_nt(a, w_ref[j]) for j in range(4)], axis=-1)


def _merge_fwd(attn, y_ssm, gs, gm, x, grp_a, t):
    l = x.shape[0]

    def body(attn_ref, ys_ref, gs_ref, gm_ref, x_ref, wo_ref, wout_ref, ym_ref, mixed_ref, h_ref):
        y_mla = _row_shards_mm(attn_ref[...], wo_ref)
        ym_ref[...] = y_mla
        mixed = (_sigmoid(gs_ref[...]) * ys_ref[...] + _sigmoid(gm_ref[...]) * y_mla).astype(BF16)
        mixed_ref[...] = mixed
        h_ref[...] = x_ref[...] + _row_shards_mm(mixed, wout_ref)

    r = lambda: _rows(t, D_MODEL)
    return pl.pallas_call(
        body, name="merge_fwd", grid=(l // t,),
        in_specs=[r(), r(), r(), r(), r(), _member_block("w_o_mla"), _member_block("w_out")],
        out_specs=[r(), r(), r()],
        out_shape=[jax.ShapeDtypeStruct((l, D_MODEL), F32), jax.ShapeDtypeStruct((l, D_MODEL), BF16),
                   jax.ShapeDtypeStruct((l, D_MODEL), F32)],
        compiler_params=_cparams("parallel"),
    )(attn, y_ssm, gs, gm, x, grp_a, grp_a)


def _merge_bwd(dh, y_ssm, y_mla, gs, gm, grp_a, t):
    l = dh.shape[0]

    def body(dh_ref, ys_ref, ym_ref, gs_ref, gm_ref, wo_ref, wout_ref, dys_ref, dym_ref, dgs_ref, dgm_ref, dattn_ref):
        dmixed = _row_shards_mm_nt(dh_ref[...], wout_ref)
        sg = _sigmoid(gs_ref[...])
        sm = _sigmoid(gm_ref[...])
        dys_ref[...] = (dmixed * sg).astype(BF16)
        dgs_ref[...] = (dmixed * ys_ref[...] * sg * (1.0 - sg)).astype(BF16)
        dym = (dmixed * sm).astype(BF16)
        dym_ref[...] = dym
        dgm_ref[...] = (dmixed * ym_ref[...] * sm * (1.0 - sm)).astype(BF16)
        dattn_ref[...] = _row_shards_mm_nt(dym, wo_ref)

    r = lambda: _rows(t, D_MODEL)
    bf = jax.ShapeDtypeStruct((l, D_MODEL), BF16)
    return pl.pallas_call(
        body, name="merge_bwd", grid=(l // t,),
        in_specs=[r(), r(), r(), r(), r(), _member_block("w_o_mla"), _member_block("w_out")],
        out_specs=[r(), r(), r(), r(), r()],
        out_shape=[bf, bf, bf, bf, jax.ShapeDtypeStruct((l, D_MODEL), F32)],
        compiler_params=_cparams("parallel"),
    )(dh, y_ssm, y_mla, gs, gm, grp_a, grp_a)


def _mlp_fwd_bwd(h, tgt, g2, grp_a, t):
    l = h.shape[0]

    def body(h_ref, tgt_ref, g_ref, wu_ref, wd_ref, dh_ref, hn_ref, da_ref, hid_ref, dout_ref, loss_ref, dg_ref):
        first = pl.program_id(0) == 0
        hv = h_ref[...]
        g = g_ref[...]
        hn = _rms_fwd(hv, g, D_MODEL).astype(BF16)
        hn_ref[...] = hn
        out = hv
        relus = []
        for s in range(4):
            cols = slice(1024 * s, 1024 * (s + 1))
            relu = jnp.maximum(_mm(hn, wu_ref[s]), 0.0)
            relus.append(relu)
            hid = (relu * relu).astype(BF16)
            hid_ref[:, cols] = hid
            out = out + _mm(hid, wd_ref[s])
        err = out - tgt_ref[...]
        _accumulate(loss_ref, jnp.full((8, 128), jnp.sum(err * err) * (0.5 / D_MODEL), F32), first)
        dout = err * (1.0 / D_MODEL)
        doutb = dout.astype(BF16)
        dout_ref[...] = doutb
        dhn = jnp.zeros_like(hv)
        for s in range(4):
            da = (_mm_nt(doutb, wd_ref[s]) * (2.0 * relus[s])).astype(BF16)
            da_ref[:, 1024 * s:1024 * (s + 1)] = da
            dhn = dhn + _mm_nt(da, wu_ref[s])
        dx, dg_rows = _rms_bwd(hv, g, dhn, D_MODEL)
        dh_ref[...] = dout + dx
        _accumulate(dg_ref, _colsum(dg_rows), first)

    r = lambda w: _rows(t, w)
    return pl.pallas_call(
        body, name="mlp_fwd_bwd", grid=(l // t,),
        in_specs=[r(D_MODEL), r(D_MODEL), _resident((1, D_MODEL)), _member_block("w_up"), _member_block("w_down")],
        out_specs=[r(D_MODEL), r(D_MODEL), r(D_FF), r(D_FF), r(D_MODEL), pl.BlockSpec((8, 128), lambda i: (0, 0)),
                   pl.BlockSpec((1, D_MODEL), lambda i: (0, 0))],
        out_shape=[jax.ShapeDtypeStruct((l, D_MODEL), F32), jax.ShapeDtypeStruct((l, D_MODEL), BF16),
                   jax.ShapeDtypeStruct((l, D_FF), BF16), jax.ShapeDtypeStruct((l, D_FF), BF16),
                   jax.ShapeDtypeStruct((l, D_MODEL), BF16), jax.ShapeDtypeStruct((8, 128), F32),
                   jax.ShapeDtypeStruct((1, D_MODEL), F32)],
        compiler_params=_cparams("arbitrary"),
    )(h, tgt, g2, grp_a, grp_a)


def _wgrad(a, b, name):
    l, m = a.shape
    n = b.shape[1]
    bm = m if m <= 512 else 512
    bl = min(l, 2048 if n <= 1024 else 1024)

    def body(a_ref, b_ref, o_ref):
        _accumulate(o_ref, _mm_tn(a_ref[...], b_ref[...]), pl.program_id(1) == 0)

    return pl.pallas_call(
        body, name=name, grid=(m // bm, l // bl),
        in_specs=[pl.BlockSpec((bl, bm), lambda i, j: (j, i)), pl.BlockSpec((bl, n), lambda i, j: (j, 0))],
        out_specs=pl.BlockSpec((bm, n), lambda i, j: (i, 0)),
        out_shape=jax.ShapeDtypeStruct((m, n), F32),
        compiler_params=_cparams("parallel", "arbitrary"),
    )(a, b)


def _wgrad_into(a, b, member, cut, dest=None):
    group, off, rs, cs = _place_in_group(member)
    l = a.shape[0]
    bm = min(rs, 512)
    bl = min(l, 2048)
    nb = rs // bm
    if cut == "row":
        a_spec = pl.BlockSpec((bl, bm), lambda j, i, k: (k, j * nb + i))
        b_spec = pl.BlockSpec((bl, cs), lambda j, i, k: (k, 0))
    else:
        a_spec = pl.BlockSpec((bl, bm), lambda j, i, k: (k, i))
        b_spec = pl.BlockSpec((bl, cs), lambda j, i, k: (k, j))

    def body(a_ref, b_ref, *rest):
        o_ref = rest[-1]
        part = _mm_tn(a_ref[...], b_ref[...])

        @pl.when(pl.program_id(2) == 0)
        def _():
            o_ref[0] = part

        @pl.when(pl.program_id(2) != 0)
        def _():
            o_ref[0] += part

    operands, in_specs, aliases = [a, b], [a_spec, b_spec], {}
    if dest is not None:
        operands.append(dest)
        in_specs.append(ANY)
        aliases = {2: 0}
    return pl.pallas_call(
        body, name="wgrad_" + member, grid=(4, nb, l // bl), in_specs=in_specs,
        out_specs=pl.BlockSpec((1, bm, cs), lambda j, i, k: (j, off // bm + i, 0)),
        out_shape=jax.ShapeDtypeStruct((4, _group_rows(group), cs), F32), input_output_aliases=aliases,
        compiler_params=_cparams("parallel", "parallel", "arbitrary"),
    )(*operands)


def _adamw(w, g, m, v, name, g_off, token):
    r, c = w.shape
    br = r
    for cand in (256, 128, 64, 32, 16, 8):
        if r % cand == 0 and g_off % cand == 0:
            br = cand
            break

    def body(w_ref, g_ref, m_ref, v_ref, token_ref, go_ref, d_ref, nm_ref, nv_ref):
        gv = g_ref[...]
        go_ref[...] = gv
        nm = ADAM_B1 * m_ref[...] + (1.0 - ADAM_B1) * gv
        nv = ADAM_B2 * v_ref[...] + (1.0 - ADAM_B2) * (gv * gv)
        m_hat = nm / (1.0 - ADAM_B1 ** ADAM_STEP)
        v_hat = nv / (1.0 - ADAM_B2 ** ADAM_STEP)
        d_ref[...] = -ADAM_LR * (m_hat / (jnp.sqrt(v_hat) + ADAM_EPS) + ADAM_WD * w_ref[...])
        nm_ref[...] = nm
        nv_ref[...] = nv

    spec = lambda: pl.BlockSpec((br, c), lambda i: (i, 0))
    g_spec = pl.BlockSpec((br, c), lambda i: (g_off // br + i, 0))
    shp = jax.ShapeDtypeStruct((r, c), F32)
    return pl.pallas_call(
        body, name=name, grid=(r // br,), in_specs=[spec(), g_spec, spec(), spec(), ANY],
        out_specs=[spec(), spec(), spec(), spec()], out_shape=[shp, shp, shp, shp], compiler_params=_cparams("parallel"),
    )(w, g, m, v, token)


def _place():
    return lax.axis_index("x"), lax.axis_index("y"), lax.axis_index("c")


def _other_chips(x, y):
    return [(1 - x, y), (x, 1 - y), (1 - x, 1 - y)]


ANY = pl.BlockSpec(memory_space=pl.ANY)


def _gather_weights(bufs):
    n = len(bufs)

    def body(*refs):
        outs, send_sems, recv_sems = refs[n:2 * n], refs[2 * n], refs[2 * n + 1]
        x, y, c = _place()
        chips = _other_chips(x, y)

        def part(g, px, py, pc):
            half = outs[g].shape[1] // 2
            return outs[g].at[2 * px + py, pl.ds(pl.multiple_of(pc * half, 16), half), :]

        def copy(k, src, dst, to):
            return pltpu.make_async_remote_copy(src_ref=src, dst_ref=dst, send_sem=send_sems.at[k], recv_sem=recv_sems.at[k],
                                                device_id=to, device_id_type=MESH)

        first = [copy(6 * g + j, part(g, x, y, c), part(g, x, y, c), (*chip, c)) for g in range(n) for j, chip in enumerate(chips)]
        for cp in first:
            cp.start()
        passed = []
        for g in range(n):
            for j, chip in enumerate(chips):
                landed = part(g, *chip, c)
                copy(6 * g + j, landed, landed, (x, y, c)).wait_recv()
                passed.append(copy(6 * g + 3 + j, landed, landed, (x, y, 1 - c)))
                passed[-1].start()
        for g in range(n):
            for j, chip in enumerate(chips):
                other = part(g, *chip, 1 - c)
                copy(6 * g + 3 + j, other, other, (x, y, c)).wait_recv()
        for cp in first + passed:
            cp.wait_send()

    return pl.pallas_call(
        body, name="gather_weights", in_specs=[ANY] * n, out_specs=[ANY] * n,
        out_shape=[jax.ShapeDtypeStruct(b.shape, b.dtype) for b in bufs], input_output_aliases={g: g for g in range(n)},
        scratch_shapes=[pltpu.SemaphoreType.DMA((6 * n,)), pltpu.SemaphoreType.DMA((6 * n,))],
    )(*bufs)


def _cast_shards(shards, group, place):
    width, members = GROUPS[group]
    rows = _group_rows(group)

    def body(place_ref, *refs):
        out = refs[-1]
        off = 0
        for ref, (_, r) in zip(refs[:-1], members):
            out[0, off:off + r, :] = ref[...].astype(BF16)
            off += r

    grid_spec = pltpu.PrefetchScalarGridSpec(
        num_scalar_prefetch=1, grid=(1,),
        in_specs=[pl.BlockSpec((r, width), lambda i, p: (0, 0)) for _, r in members],
        out_specs=pl.BlockSpec((1, rows, width), lambda i, p: (p[0], 0, 0)))
    return pl.pallas_call(
        body, name="cast_shards_" + group, grid_spec=grid_spec, out_shape=jax.ShapeDtypeStruct((4, rows, width), BF16),
        compiler_params=_cparams("arbitrary"),
    )(place, *[shards[name] for name, _ in members])


def _block_rows(h):
    return next(cand for cand in (256, 192, 128, 64, 32, 16) if h % cand == 0)


def _add_pair(buf, got, place, name):
    n, h, w = got.shape
    bh = _block_rows(h)
    nb = h // bh

    def body(place_ref, a_ref, b_ref, s_ref, sb_ref):
        s = a_ref[...] + b_ref[...]
        s_ref[...] = s
        sb_ref[...] = s.astype(BF16)

    spec = lambda: pl.BlockSpec((1, bh, w), lambda j, i, p: (j, i, 0))
    grid_spec = pltpu.PrefetchScalarGridSpec(
        num_scalar_prefetch=1, grid=(n, nb),
        in_specs=[pl.BlockSpec((1, bh, w), lambda j, i, p: (j, p[1] * nb + i, 0)), spec()], out_specs=[spec(), spec()])
    return pl.pallas_call(
        body, name=name, grid_spec=grid_spec,
        out_shape=[jax.ShapeDtypeStruct(got.shape, F32), jax.ShapeDtypeStruct(got.shape, BF16)],
        compiler_params=_cparams("parallel", "parallel"),
    )(place, buf, got)


def _add_received(pair, got, place, name):
    _, h, w = pair.shape
    bh = _block_rows(h)
    nb = h // bh

    def body(place_ref, own_ref, got_ref, o_ref):
        o_ref[...] = ((own_ref[0] + got_ref[0].astype(F32)) + got_ref[1].astype(F32)) + got_ref[2].astype(F32)

    grid_spec = pltpu.PrefetchScalarGridSpec(
        num_scalar_prefetch=1, grid=(nb,),
        in_specs=[pl.BlockSpec((1, bh, w), lambda i, p: (p[0], i, 0)), pl.BlockSpec((3, bh, w), lambda i, p: (0, i, 0))],
        out_specs=pl.BlockSpec((bh, w), lambda i, p: (p[1] * nb + i, 0)))
    return pl.pallas_call(
        body, name=name, grid_spec=grid_spec, out_shape=jax.ShapeDtypeStruct((2 * h, w), F32),
        compiler_params=_cparams("parallel"),
    )(place, pair, got)


def _swap_reduced_halves(bufs):
    n = len(bufs)

    def body(*refs):
        outs, send_sems, recv_sems = refs[n:2 * n], refs[2 * n], refs[2 * n + 1]
        x, y, c = _place()
        copies = []
        for g in range(n):
            half = outs[g].shape[0] // 2
            own = outs[g].at[pl.ds(pl.multiple_of(c * half, 8), half), :]
            copies.append(pltpu.make_async_remote_copy(src_ref=own, dst_ref=own, send_sem=send_sems.at[g],
                                                       recv_sem=recv_sems.at[g], device_id=(x, y, 1 - c), device_id_type=MESH))
        for cp in copies:
            cp.start()
        for g in range(n):
            half = outs[g].shape[0] // 2
            other = outs[g].at[pl.ds(pl.multiple_of((1 - c) * half, 8), half), :]
            pltpu.make_async_remote_copy(src_ref=other, dst_ref=other, send_sem=send_sems.at[g], recv_sem=recv_sems.at[g],
                                         device_id=(x, y, 1 - c), device_id_type=MESH).wait_recv()
        for cp in copies:
            cp.wait_send()

    return pl.pallas_call(
        body, name="swap_reduced_halves", in_specs=[ANY] * n, out_specs=[ANY] * n,
        out_shape=[jax.ShapeDtypeStruct(b.shape, b.dtype) for b in bufs], input_output_aliases={g: g for g in range(n)},
        scratch_shapes=[pltpu.SemaphoreType.DMA((n,)), pltpu.SemaphoreType.DMA((n,))],
    )(*bufs)


HBM = pl.BlockSpec(memory_space=pltpu.HBM)
SEM = pl.BlockSpec(memory_space=pltpu.SEMAPHORE)


def _copies_start(name, bufs, n_copies, plan, after=None):
    n = len(bufs)
    extra = [] if after is None else [after]

    def body(*refs):
        sems = refs[n + len(extra):n + len(extra) + 2 * n_copies]
        x, y, c = _place()
        for i, (src, dst, dev) in enumerate(plan(refs[:n], x, y, c)):
            pltpu.make_async_remote_copy(src_ref=src, dst_ref=dst, send_sem=sems[i], recv_sem=sems[n_copies + i],
                                         device_id=dev, device_id_type=MESH).start()
        token = refs[-1]
        token[...] = jnp.zeros_like(token)

    out = pl.pallas_call(
        body, name=name,
        out_shape=[pltpu.SemaphoreType.DMA(())] * (2 * n_copies) + [pltpu.HBM(b.shape, b.dtype) for b in bufs]
        + [jax.ShapeDtypeStruct((8, 128), F32)],
        in_specs=[HBM] * n + [ANY] * len(extra),
        out_specs=[SEM] * (2 * n_copies) + [HBM] * n + [pl.BlockSpec(memory_space=pltpu.VMEM)],
        input_output_aliases={i: 2 * n_copies + i for i in range(n)},
        compiler_params=pltpu.CompilerParams(has_side_effects=pltpu.SideEffectType.DATAFLOW_SIDE_EFFECTING),
    )(*[pltpu.with_memory_space_constraint(b, pltpu.HBM) for b in bufs], *extra)
    return list(out[:2 * n_copies]), list(out[2 * n_copies:-1]), out[-1]


def _copies_wait(name, bufs, sems, after, plan):
    n = len(bufs)
    k = len(sems) // 2

    def body(*refs):
        sem_refs = refs[n:n + 2 * k]
        x, y, c = _place()
        for i, (sent, landed, dev) in enumerate(plan(refs[:n], x, y, c)):
            cp = pltpu.make_async_remote_copy(src_ref=sent, dst_ref=landed, send_sem=sem_refs[i], recv_sem=sem_refs[k + i],
                                              device_id=dev, device_id_type=MESH)
            cp.wait_send()
            cp.wait_recv()

    return pl.pallas_call(
        body, name=name, out_shape=[pltpu.HBM(b.shape, b.dtype) for b in bufs],
        in_specs=[HBM] * n + [SEM] * (2 * k) + [ANY], out_specs=[HBM] * n, input_output_aliases={i: i for i in range(n)},
        compiler_params=pltpu.CompilerParams(has_side_effects=pltpu.SideEffectType.DATAFLOW_SIDE_EFFECTING),
    )(*bufs, *sems, after)


def _row_half(ref, which, axis):
    half = ref.shape[axis] // 2
    rows = pl.ds(pl.multiple_of(which * half, 8), half)
    return ref.at[rows, :] if axis == 0 else ref.at[:, rows, :]


class _SplitGather:
    def __init__(self, own, after):
        self.n = len(own)
        self.state = _copies_start("gather_start", own, 3 * self.n, self._sent, after)

    @staticmethod
    def _sent(refs, x, y, c):
        return [(w.at[2 * x + y], w.at[2 * x + y], (px, py, c)) for w in refs for px, py in _other_chips(x, y)]

    @staticmethod
    def _landed(refs, x, y, c):
        return [(w.at[2 * x + y], w.at[2 * px + py], (px, py, c)) for w in refs for px, py in _other_chips(x, y)]

    def token(self):
        return self.state[2]

    def wait(self, which, name, after):
        sems, bufs, _ = self.state
        k = 3 * self.n
        mine = [sems[3 * i + j] for i in which for j in range(3)] + [sems[k + 3 * i + j] for i in which for j in range(3)]
        return _copies_wait(name, [bufs[i] for i in which], mine, after, self._landed)


class _SplitReduction:
    def __init__(self, tag, groups, place):
        self.tag, self.groups, self.place = tag, groups, place

    def start_pair(self, bufs):
        n = len(bufs)
        lands = [lax.empty((4, b.shape[1] // 2, b.shape[2]), F32) for b in bufs]
        plan = lambda refs, x, y, c: [(_row_half(refs[i], 1 - c, 1), refs[n + i], (x, y, 1 - c)) for i in range(n)]
        self._pair = (_copies_start("pair_%s_start" % self.tag, bufs + lands, n, plan), plan, n)
        return self._pair[0][2]

    def pair_done_start_scatter(self, after):
        (sems, bufs, _), plan, n = self._pair
        out = _copies_wait("pair_%s_wait" % self.tag, bufs, sems, after, plan)
        pairs = [_add_pair(out[i], out[n + i], self.place, "add_pair_" + g) for i, g in enumerate(self.groups)]
        self._pair_f32 = [p[0] for p in pairs]
        lands = [lax.empty((3,) + p[1].shape[1:], BF16) for p in pairs]
        plan = lambda refs, x, y, c: [(refs[i].at[2 * px + py], refs[n + i].at[j], (px, py, c))
                                      for i in range(n) for j, (px, py) in enumerate(_other_chips(x, y))]
        self._scatter = (_copies_start("scatter_%s_start" % self.tag, [p[1] for p in pairs] + lands, 3 * n, plan), plan, n)
        return self._scatter[0][2]

    def scatter_done(self, after):
        (sems, bufs, _), plan, n = self._scatter
        out = _copies_wait("scatter_%s_wait" % self.tag, bufs, sems, after, plan)
        return [_add_received(self._pair_f32[i], out[n + i], self.place, "add_received_" + g)
                for i, g in enumerate(self.groups)]

    def start_join(self, halves):
        n = len(halves)
        sent = lambda refs, x, y, c: [(_row_half(r, c, 0), _row_half(r, c, 0), (x, y, 1 - c)) for r in refs]
        landed = lambda refs, x, y, c: [(_row_half(r, c, 0), _row_half(r, 1 - c, 0), (x, y, 1 - c)) for r in refs]
        self._join = (_copies_start("join_%s_start" % self.tag, halves, n, sent), landed)
        return self._join[0][2]

    def join_done(self, after):
        (sems, bufs, _), landed = self._join
        return _copies_wait("join_%s_wait" % self.tag, bufs, sems, after, landed)


def _pair_sum_small(mine):
    rows, w = mine.shape

    def body(in_ref, out_ref, sibling, send_sem, recv_sem):
        x, y, c = _place()
        swap = pltpu.make_async_remote_copy(src_ref=in_ref, dst_ref=sibling, send_sem=send_sem, recv_sem=recv_sem,
                                            device_id=(x, y, 1 - c), device_id_type=MESH)
        swap.start()
        swap.wait()
        out_ref[...] = in_ref[...] + sibling[...]

    return pl.pallas_call(
        body, name="pair_sum_small", out_shape=jax.ShapeDtypeStruct((rows, w), F32),
        in_specs=[pl.BlockSpec(memory_space=pltpu.VMEM)], out_specs=pl.BlockSpec(memory_space=pltpu.VMEM),
        scratch_shapes=[pltpu.VMEM((rows, w), F32), pltpu.SemaphoreType.DMA, pltpu.SemaphoreType.DMA],
        compiler_params=pltpu.CompilerParams(vmem_limit_bytes=VMEM_LIMIT_V7X),
    )(mine)


class _SplitChipSum:
    def __init__(self, pair, place):
        self.place = place
        slots = lax.empty((4,) + pair.shape, F32)
        sent = lambda refs, x, y, c: [(refs[0], refs[1].at[2 * x + y], (px, py, c)) for px, py in _other_chips(x, y)]
        self.landed = lambda refs, x, y, c: [(refs[0], refs[1].at[2 * px + py], (px, py, c)) for px, py in _other_chips(x, y)]
        self.state = _copies_start("small_sum_start", [pair, slots], 3, sent)

    def token(self):
        return self.state[2]

    def done(self, after):
        sems, bufs, _ = self.state
        pair, slots = _copies_wait("small_sum_wait", bufs, sems, after, self.landed)
        rows, w = pair.shape

        def body(place_ref, pair_ref, slots_ref, out_ref):
            for j in range(4):
                own = place_ref[0] == j

                @pl.when(own)
                def _():
                    out_ref[...] = pair_ref[...] if j == 0 else out_ref[...] + pair_ref[...]

                @pl.when(jnp.logical_not(own))
                def _():
                    out_ref[...] = slots_ref[j] if j == 0 else out_ref[...] + slots_ref[j]

        grid_spec = pltpu.PrefetchScalarGridSpec(
            num_scalar_prefetch=1, grid=(1,),
            in_specs=[pl.BlockSpec((rows, w), lambda i, p: (0, 0)), pl.BlockSpec((4, rows, w), lambda i, p: (0, 0, 0))],
            out_specs=pl.BlockSpec((rows, w), lambda i, p: (0, 0)))
        return pl.pallas_call(
            body, name="small_sum_add", grid_spec=grid_spec, out_shape=jax.ShapeDtypeStruct((rows, w), F32),
            compiler_params=_cparams("arbitrary"),
        )(self.place, pair, slots)


def _join_column_shards(g):
    return jnp.transpose(g, (1, 0, 2)).reshape(g.shape[1], 4 * g.shape[2])


def _split_column_shards(w):
    r = w.shape[0]
    return jnp.transpose(w.reshape(r, 4, w.shape[1] // 4), (1, 0, 2))


def _small_rows(shape):
    return -(-int(np.prod(shape)) // 1024)


def _pack_small(vals):
    segs = []
    for name, shape in SMALL_WEIGHTS:
        flat = vals[name].reshape(-1)
        segs.append(jnp.pad(flat, (0, _small_rows(shape) * 1024 - flat.shape[0])))
    total = sum(s.shape[0] for s in segs) // 1024
    segs.append(jnp.zeros((-total % 8 * 1024,), F32))
    return jnp.concatenate(segs).reshape(-1, 1024)


def _unpack_small(packed):
    out, off = {}, 0
    for name, shape in SMALL_WEIGHTS:
        rows = _small_rows(shape)
        out[name] = packed[off:off + rows].reshape(-1)[:int(np.prod(shape))].reshape(shape)
        off += rows
    return out


W_IN_SHARD = D_IN // 4
W_IN_GAP = 1216


def _pad_w_in(g):
    cut = W_IN_GAP - W_IN_SHARD
    return jnp.concatenate([g[0], g[1][:, :cut], jnp.zeros((g.shape[1], D_IN_PAD - D_IN), g.dtype), g[1][:, cut:], g[2], g[3]],
                           axis=1)


def _unpad_w_in(g):
    skip = D_IN_PAD - D_IN
    second = jnp.concatenate([g[:, W_IN_SHARD:W_IN_GAP], g[:, W_IN_GAP + skip:2 * W_IN_SHARD + skip]], axis=1)
    return jnp.stack([g[:, :W_IN_SHARD], second, g[:, 2 * W_IN_SHARD + skip:3 * W_IN_SHARD + skip],
                      g[:, 3 * W_IN_SHARD + skip:]])


def _pad_heads(w):
    r = w.shape[0]
    return jnp.pad(w.reshape(r, N_HEADS, QK_HEAD), ((0, 0), (0, 0), (0, HEAD_PAD - QK_HEAD))).reshape(r, N_HEADS * HEAD_PAD)


def _unpad_heads(g):
    r = g.shape[0]
    return g.reshape(r, N_HEADS, HEAD_PAD)[:, :, :QK_HEAD].reshape(r, N_HEADS * QK_HEAD)


def _local_step(x, positions, tgt, grp_b, small, gather, red_a, red_rest):
    l = x.shape[0]
    t = min(l, 512)
    t_mlp = min(l, 256)
    tq = min(l, 1024)
    tc = min(l, 256)
    row = lambda v: v.reshape(1, -1).astype(F32)

    w_in_p = _pad_w_in(grp_b)
    g1, g2 = row(small["norm_mix"]), row(small["norm_mlp"])
    gqa, gkva = row(small["q_a_norm"]), row(small["kv_a_norm"])
    gq = jnp.pad(row(small["q_norm"]), ((0, 0), (0, HEAD_PAD - QK_HEAD)))
    gk = jnp.pad(row(small["k_norm"]), ((0, 0), (0, HEAD_PAD - QK_HEAD)))
    half = QK_ROPE // 2
    inv_freq = ROPE_THETA ** (-jnp.arange(half, dtype=F32) / half)
    invf = jnp.concatenate([inv_freq, inv_freq, jnp.zeros((64,), F32)]).reshape(1, 128)
    sgn = jnp.concatenate([-jnp.ones((half,), F32), jnp.ones((half,), F32), jnp.zeros((64,), F32)]).reshape(1, 128)
    pos = positions.reshape(l, 1)

    a_re, a_im = small["ssm_a_re"], small["ssm_a_im"]
    log_dt = small["ssm_log_dt"].reshape(SSM_GROUPS, 1)
    to_gcp = lambda b: jnp.transpose(b, (0, 2, 1)).reshape(SSM_WIDTH, SSM_STATE)
    from_gcp = lambda b: jnp.transpose(b.reshape(SSM_GROUPS, SSM_GROUP_CH, SSM_STATE), (0, 2, 1))
    b_re, b_im = to_gcp(small["ssm_b_re"]), to_gcp(small["ssm_b_im"])
    c_re, c_im = small["ssm_c_re"].reshape(SSM_WIDTH, SSM_STATE), small["ssm_c_im"].reshape(SSM_WIDTH, SSM_STATE)
    wb, wc, tabs_fwd, tabs_rev = _ssm_param_fwd(a_re, a_im, log_dt, b_re, b_im, c_re, c_im)
    dskip = row(small["ssm_d"])
    b_glu = row(small["b_glu"])

    u, lat, gs, gm = _in_proj_fwd(x, g1, w_in_p, t, gather.token())
    grp_c, grp_d, grp_e = gather.wait([0, 1, 2], "gather_cde_wait", u)
    w_qb_p = _pad_heads(_join_column_shards(grp_c))
    xr, xi, y, y_ssm = _ssm_fwd(u, wb, wc, tabs_fwd, dskip, grp_d, b_glu, grp_e, tc)
    q, k, v = _mla_pre_fwd(lat, pos, invf, sgn, gqa, gkva, gq, gk, w_qb_p, grp_d, t)
    attn, lse = _attn_fwd(q, k, v, tq)
    (grp_a,) = gather.wait([3], "gather_a_wait", attn)
    y_mla, mixed, h = _merge_fwd(attn, y_ssm, gs, gm, x, grp_a, t)
    dh, hn, da, hid, dout, loss_blk, g_norm_mlp = _mlp_fwd_bwd(h, tgt, g2, grp_a, t_mlp)

    ga = _wgrad_into(hn, da, "w_up", "col", _wgrad_into(hid, dout, "w_down", "row"))
    dys, dym, dgs, dgm, dattn = _merge_bwd(dh, y_ssm, y_mla, gs, gm, grp_a, t)
    ga = _wgrad_into(attn, dym, "w_o_mla", "row", _wgrad_into(mixed, dh, "w_out", "row", ga))

    dq, dk, dv = _attn_bwd(q, k, v, attn, dattn, lse, tq, red_a.start_pair([ga]))
    d_lat, ql, dq0, ckn, dkv, g_qa, g_kva, g_q, g_k = _mla_pre_bwd(lat, pos, invf, sgn, gqa, gkva, gq, gk, w_qb_p, grp_d,
                                                                    dq, dk, dv, t, red_a.pair_done_start_scatter(dk))
    gc = _split_column_shards(_unpad_heads(_wgrad(ql, dq0, "wgrad_q_b")))

    d_u, adj, dy, z, z2, dpre, g_b_glu, g_d, g_lr, g_li = _ssm_bwd(
        dys, y, u, xr, xi, wb, wc, tabs_rev, dskip, grp_d, b_glu, grp_e, tc)
    gd = _wgrad_into(z, dpre, "w_glu", "row", _wgrad_into(ckn, dkv, "w_kv_b", "col"))
    ge = _wgrad_into(z2, dys, "w_o_ssm", "col")
    grad_x, xn, dproj, g_norm_mix = _in_proj_bwd(x, g1, w_in_p, d_u, d_lat, dgs, dgm, dh, t)
    gb = _unpad_w_in(_wgrad(xn, dproj, "wgrad_in"))

    red_a.start_join(red_a.scatter_done(gb))
    g_wb = _wgrad_strips(u, adj, adj, "wgrad_ssm_b", 1, red_rest.start_pair([gb, gc, gd, ge]))
    g_wct = _wgrad_strips(dy, xr, xi, "wgrad_ssm_c", 0, red_rest.pair_done_start_scatter(g_wb))
    g_ar, g_ai, g_ldt, g_br, g_bi, g_cr, g_ci = _ssm_param_bwd(a_re, a_im, log_dt, b_re, b_im, g_lr, g_li, g_wb, g_wct)

    g_small = {
        "norm_mix": g_norm_mix.reshape(-1), "norm_mlp": g_norm_mlp.reshape(-1), "q_a_norm": g_qa.reshape(-1),
        "kv_a_norm": g_kva.reshape(-1), "q_norm": g_q.reshape(-1)[:QK_HEAD], "k_norm": g_k.reshape(-1)[:QK_HEAD],
        "ssm_a_re": g_ar, "ssm_a_im": g_ai, "ssm_log_dt": g_ldt.reshape(-1),
        "ssm_b_re": from_gcp(g_br), "ssm_b_im": from_gcp(g_bi),
        "ssm_c_re": g_cr.reshape(SSM_GROUPS, SSM_GROUP_CH, SSM_STATE), "ssm_c_im": g_ci.reshape(SSM_GROUPS, SSM_GROUP_CH, SSM_STATE),
        "ssm_d": g_d.reshape(SSM_GROUPS, SSM_GROUP_CH), "b_glu": g_b_glu.reshape(-1),
    }
    return loss_blk[0, 0], grad_x, g_small


def kernel(x, positions, norm_mix, w_in, q_a_norm, kv_a_norm, w_q_b, w_kv_b, q_norm, k_norm, w_o_mla, ssm_a_re, ssm_a_im, ssm_log_dt, ssm_b_re, ssm_b_im, ssm_c_re, ssm_c_im, ssm_d, w_glu, b_glu, w_o_ssm, w_out, norm_mlp, w_up, w_down, loss_target, m_norm_mix, m_w_in, m_q_a_norm, m_kv_a_norm, m_w_q_b, m_w_kv_b, m_q_norm, m_k_norm, m_w_o_mla, m_ssm_a_re, m_ssm_a_im, m_ssm_log_dt, m_ssm_b_re, m_ssm_b_im, m_ssm_c_re, m_ssm_c_im, m_ssm_d, m_w_glu, m_b_glu, m_w_o_ssm, m_w_out, m_norm_mlp, m_w_up, m_w_down, v_norm_mix, v_w_in, v_q_a_norm, v_kv_a_norm, v_w_q_b, v_w_kv_b, v_q_norm, v_k_norm, v_w_o_mla, v_ssm_a_re, v_ssm_a_im, v_ssm_log_dt, v_ssm_b_re, v_ssm_b_im, v_ssm_c_re, v_ssm_c_im, v_ssm_d, v_w_glu, v_b_glu, v_w_o_ssm, v_w_out, v_norm_mlp, v_w_up, v_w_down):
    args = dict(locals())
    w = {n: args[n][0] for n in WEIGHT_ORDER}
    m = {n: args["m_" + n][0] for n in WEIGHT_ORDER}
    v = {n: args["v_" + n][0] for n in WEIGHT_ORDER}
    big_names = [n for n, *_ in BIG_WEIGHTS]
    small_names = [n for n, _ in SMALL_WEIGHTS]

    place = jnp.stack([2 * lax.axis_index("x") + lax.axis_index("y"), lax.axis_index("c")]).astype(jnp.int32)
    rest = ["b", "c", "d", "e"]

    (grp_b,) = _gather_weights([_cast_shards(w, "b", place)])
    gather = _SplitGather([_cast_shards(w, g, place) for g in ("c", "d", "e", "a")], grp_b)
    red_a = _SplitReduction("a", ["a"], place)
    red_rest = _SplitReduction("rest", rest, place)
    small = {n: w[n] for n in small_names}

    loss_local, grad_x, g_small = _local_step(x[0], positions[0], loss_target[0], grp_b, small, gather, red_a, red_rest)
    loss = lax.psum(loss_local, ("x", "y", "c"))

    grad_w, delta_w, new_m, new_v = {}, {}, {}, {}

    def update(names, reduced, token):
        for n in names:
            g, off, _, _ = _place_in_group(n)
            grad_w[n], delta_w[n], new_m[n], new_v[n] = _adamw(w[n], reduced[g], m[n], v[n], "adamw_" + n, off, token)

    chip_sum = _SplitChipSum(_pair_sum_small(_pack_small(g_small)), place)
    in_a = [n for n, _ in GROUPS["a"][1]]
    update(in_a, {"a": red_a.join_done(chip_sum.token())[0]}, chip_sum.token())
    small_sum = chip_sum.done(new_v[in_a[-1]])
    g_s, d_s, m_s, v_s = _adamw(_pack_small(small), small_sum, _pack_small({n: m[n] for n in small_names}),
                                _pack_small({n: v[n] for n in small_names}), "adamw_small", 0, small_sum)
    g_s, d_s, m_s, v_s = _unpack_small(g_s), _unpack_small(d_s), _unpack_small(m_s), _unpack_small(v_s)
    for n in small_names:
        grad_w[n], delta_w[n], new_m[n], new_v[n] = g_s[n], d_s[n], m_s[n], v_s[n]
    halves = red_rest.scatter_done(v_s[small_names[0]])
    reduced_rest = dict(zip(rest, _swap_reduced_halves(halves)))
    update([n for n in big_names if n not in in_a], reduced_rest, halves[0])

    lead = lambda d: [d[n][None] for n in WEIGHT_ORDER]
    return (loss, grad_x[None], *lead(grad_w), *lead(delta_w), *lead(new_m), *lead(new_v))
```

```python
import math

import jax
import jax.numpy as jnp
import numpy as np
from jax import lax
from jax.experimental import pallas as pl
from jax.experimental.pallas import tpu as pltpu

F32 = jnp.float32
BF16 = jnp.bfloat16

D_MODEL = 1024
SSM_GROUPS = 32
SSM_GROUP_CH = 16
SSM_WIDTH = 512
SSM_STATE = 64
GP = SSM_GROUPS * SSM_STATE
N_HEADS = 8
QK_NOPE = 128
QK_ROPE = 64
QK_HEAD = 192
HEAD_PAD = 256
V_HEAD = 128
Q_LORA = 384
KV_LORA = 256
LAT_W = 768
D_IN = 3264
D_IN_PAD = 3328
D_FF = 4096
ROPE_THETA = 10000.0
EPS = 1e-6
ATT_SCALE = QK_HEAD ** -0.5

ADAM_LR = 0.001
ADAM_B1 = 0.9
ADAM_B2 = 0.999
ADAM_EPS = 1e-08
ADAM_WD = 0.01
ADAM_STEP = 10

VMEM_LIMIT_V7X = 56 * 1024 * 1024
MESH = pl.DeviceIdType.MESH

BIG_WEIGHTS = (
    ("w_in", 1024, 3264, "col"),
    ("w_q_b", 384, 1536, "col"),
    ("w_kv_b", 256, 2048, "col"),
    ("w_o_mla", 1024, 1024, "row"),
    ("w_glu", 512, 512, "row"),
    ("w_o_ssm", 512, 1024, "col"),
    ("w_out", 1024, 1024, "row"),
    ("w_up", 1024, 4096, "col"),
    ("w_down", 4096, 1024, "row"),
)
GROUPS = {
    "a": (1024, (("w_down", 1024), ("w_up", 1024), ("w_o_mla", 256), ("w_out", 256))),
    "b": (816, (("w_in", 1024),)),
    "c": (384, (("w_q_b", 384),)),
    "d": (512, (("w_kv_b", 256), ("w_glu", 128))),
    "e": (256, (("w_o_ssm", 512),)),
}


def _group_rows(group):
    return sum(r for _, r in GROUPS[group][1])


def _place_in_group(name):
    for group, (width, members) in GROUPS.items():
        off = 0
        for member, rows in members:
            if member == name:
                return group, off, rows, width
            off += rows
    raise KeyError(name)


SMALL_WEIGHTS = (
    ("norm_mix", (1024,)), ("q_a_norm", (384,)), ("kv_a_norm", (256,)), ("q_norm", (192,)), ("k_norm", (192,)),
    ("ssm_a_re", (32, 64)), ("ssm_a_im", (32, 64)), ("ssm_log_dt", (32,)),
    ("ssm_b_re", (32, 64, 16)), ("ssm_b_im", (32, 64, 16)), ("ssm_c_re", (32, 16, 64)), ("ssm_c_im", (32, 16, 64)),
    ("ssm_d", (32, 16)), ("b_glu", (512,)), ("norm_mlp", (1024,)),
)
WEIGHT_ORDER = ('norm_mix', 'w_in', 'q_a_norm', 'kv_a_norm', 'w_q_b', 'w_kv_b', 'q_norm', 'k_norm', 'w_o_mla', 'ssm_a_re',
                'ssm_a_im', 'ssm_log_dt', 'ssm_b_re', 'ssm_b_im', 'ssm_c_re', 'ssm_c_im', 'ssm_d', 'w_glu', 'b_glu',
                'w_o_ssm', 'w_out', 'norm_mlp', 'w_up', 'w_down')


def _cparams(*sem):
    return pltpu.CompilerParams(dimension_semantics=sem if sem else None, vmem_limit_bytes=VMEM_LIMIT_V7X)


def _resident(shape, index=None):
    index = (0,) * len(shape) if index is None else index
    return pl.BlockSpec(shape, lambda *_: index, pipeline_mode=pl.Buffered(1))


def _member_block(name):
    _, off, rows, width = _place_in_group(name)
    return _resident((4, rows, width), (0, off // rows, 0))


def _rows(t, width):
    return pl.BlockSpec((t, width), lambda i: (i, 0))


def _mm(a, b):
    return jnp.dot(a.astype(BF16), b.astype(BF16), preferred_element_type=F32)


def _mm_nt(a, b):
    return lax.dot_general(a.astype(BF16), b.astype(BF16), (((1,), (1,)), ((), ())), preferred_element_type=F32)


def _mm_tn(a, b):
    return lax.dot_general(a.astype(BF16), b.astype(BF16), (((0,), (0,)), ((), ())), preferred_element_type=F32)


def _rms_fwd(x, g, n):
    r = lax.rsqrt(jnp.sum(x * x, axis=-1, keepdims=True) * (1.0 / n) + EPS)
    return x * r * g


def _rms_bwd(x, g, dy, n):
    r = lax.rsqrt(jnp.sum(x * x, axis=-1, keepdims=True) * (1.0 / n) + EPS)
    xh = x * r
    dxh = dy * g
    dx = r * (dxh - xh * (jnp.sum(dxh * xh, axis=-1, keepdims=True) * (1.0 / n)))
    return dx, dy * xh


def _colsum(a):
    return jnp.sum(a, axis=0, keepdims=True)


def _accumulate(ref, value, first):
    @pl.when(first)
    def _():
        ref[...] = value

    @pl.when(jnp.logical_not(first))
    def _():
        ref[...] += value


def _sigmoid(a):
    return 1.0 / (1.0 + jnp.exp(-a))


GELU_C = math.sqrt(2.0 / math.pi)
GELU_A = 0.044715


def _gelu(y):
    return 0.5 * y * (1.0 + jnp.tanh(GELU_C * (y + GELU_A * y * y * y)))


def _gelu_grad(y):
    t = jnp.tanh(GELU_C * (y + GELU_A * y * y * y))
    return 0.5 * (1.0 + t) + 0.5 * y * (1.0 - t * t) * GELU_C * (1.0 + 3.0 * GELU_A * y * y)


def _in_proj_fwd(x, g1, w_in_p, t, token):
    l = x.shape[0]

    def body(x_ref, g_ref, w_ref, token_ref, u_ref, lat_ref, gs_ref, gm_ref):
        xn = _rms_fwd(x_ref[...], g_ref[...], D_MODEL).astype(BF16)
        u_ref[...] = _mm(xn, w_ref[:, 0:512])
        lat_ref[...] = _mm(xn, w_ref[:, 512:1280])
        gs_ref[...] = _mm(xn, w_ref[:, 1280:2304])
        gm_ref[...] = _mm(xn, w_ref[:, 2304:3328])

    return pl.pallas_call(
        body, name="in_proj_fwd", grid=(l // t,),
        in_specs=[_rows(t, D_MODEL), _resident((1, D_MODEL)), _resident((D_MODEL, D_IN_PAD)), ANY],
        out_specs=[_rows(t, 512), _rows(t, LAT_W), _rows(t, D_MODEL), _rows(t, D_MODEL)],
        out_shape=[jax.ShapeDtypeStruct((l, 512), F32), jax.ShapeDtypeStruct((l, LAT_W), F32),
                   jax.ShapeDtypeStruct((l, D_MODEL), F32), jax.ShapeDtypeStruct((l, D_MODEL), F32)],
        compiler_params=_cparams("parallel"),
    )(x, g1, w_in_p, token)


def _in_proj_bwd(x, g1, w_in_p, d_u, d_lat, d_gs, d_gm, dh, t):
    l = x.shape[0]

    def body(x_ref, g_ref, w_ref, du_ref, dlat_ref, dgs_ref, dgm_ref, dh_ref, gx_ref, xn_ref, dproj_ref, dg_ref):
        xv = x_ref[...]
        g = g_ref[...]
        xn_ref[...] = _rms_fwd(xv, g, D_MODEL).astype(BF16)
        dproj_ref[:, 0:512] = du_ref[...]
        dproj_ref[:, 512:1280] = dlat_ref[...]
        dproj_ref[:, 1280:2304] = dgs_ref[...]
        dproj_ref[:, 2304:3328] = dgm_ref[...]
        dxn = _mm_nt(dproj_ref[...], w_ref[...])
        dx, dg_rows = _rms_bwd(xv, g, dxn, D_MODEL)
        gx_ref[...] = dh_ref[...] + dx
        _accumulate(dg_ref, _colsum(dg_rows), pl.program_id(0) == 0)

    return pl.pallas_call(
        body, name="in_proj_bwd", grid=(l // t,),
        in_specs=[_rows(t, D_MODEL), _resident((1, D_MODEL)), _resident((D_MODEL, D_IN_PAD)), _rows(t, 512),
                  _rows(t, LAT_W), _rows(t, D_MODEL), _rows(t, D_MODEL), _rows(t, D_MODEL)],
        out_specs=[_rows(t, D_MODEL), _rows(t, D_MODEL), _rows(t, D_IN_PAD), pl.BlockSpec((1, D_MODEL), lambda i: (0, 0))],
        out_shape=[jax.ShapeDtypeStruct((l, D_MODEL), F32), jax.ShapeDtypeStruct((l, D_MODEL), BF16),
                   jax.ShapeDtypeStruct((l, D_IN_PAD), BF16), jax.ShapeDtypeStruct((1, D_MODEL), F32)],
        compiler_params=_cparams("arbitrary"),
    )(x, g1, w_in_p, d_u, d_lat, d_gs, d_gm, dh)


def _ssm_param_fn(a_re, a_im, log_dt, b_re, b_im):
    dt = jnp.exp(log_dt)
    er = jnp.exp(a_re * dt)
    lr = er * jnp.cos(a_im * dt)
    li = er * jnp.sin(a_im * dt)
    den = a_re * a_re + a_im * a_im
    nr = lr - 1.0
    kr = (nr * a_re + li * a_im) / den
    ki = (li * a_re - nr * a_im) / den
    rows = lambda k: jnp.broadcast_to(k[:, None, :], (SSM_GROUPS, SSM_GROUP_CH, SSM_STATE)).reshape(SSM_WIDTH, SSM_STATE)
    krt, kit = rows(kr), rows(ki)
    return lr, li, krt * b_re - kit * b_im, krt * b_im + kit * b_re


def _state_selector():
    row = lax.broadcasted_iota(jnp.int32, (SSM_STATE, GP), 0)
    col = lax.broadcasted_iota(jnp.int32, (SSM_STATE, GP), 1)
    return jnp.where(jnp.bitwise_and(col, SSM_STATE - 1) == row, 1.0, 0.0).astype(BF16)


def _own_group(rows, rows_per_group_log2):
    row = lax.broadcasted_iota(jnp.int32, (rows, GP), 0)
    col = lax.broadcasted_iota(jnp.int32, (rows, GP), 1)
    return jnp.right_shift(row, rows_per_group_log2) == jnp.right_shift(col, 6)


def _three_bf16(x):
    hi = x.astype(BF16)
    rest = x - hi.astype(F32)
    mid = rest.astype(BF16)
    return hi, mid, (rest - mid.astype(F32)).astype(BF16)


def _spread(x, sel):
    return sum(jnp.dot(part, sel, preferred_element_type=F32) for part in _three_bf16(x))


def _collect(xw, sel):
    return sum(lax.dot_general(part, sel, (((1,), (1,)), ((), ())), preferred_element_type=F32) for part in _three_bf16(xw))


def _ssm_param_fwd(a_re, a_im, log_dt, b_re, b_im, c_re, c_im):
    def body(ar_ref, ai_ref, ldt_ref, br_ref, bi_ref, cr_ref, ci_ref, wb_ref, wct_ref, tf_ref, tr_ref):
        lr, li, bbr, bbi = _ssm_param_fn(ar_ref[...], ai_ref[...], ldt_ref[...], br_ref[...], bi_ref[...])
        sel = _state_selector()
        own16 = _own_group(SSM_WIDTH, 4)
        own1 = _own_group(SSM_GROUPS, 0)
        block = lambda m: jnp.where(own16, jnp.dot(m.astype(BF16), sel, preferred_element_type=F32), 0.0).astype(BF16)
        wb_ref[:, 0:GP] = block(bbr)
        wb_ref[:, GP:2 * GP] = block(bbi)
        wct_ref[:, 0:GP] = block(cr_ref[...])
        wct_ref[:, GP:2 * GP] = block(-ci_ref[...])
        flat = lambda m: _colsum(jnp.where(own1, _spread(m, sel), 0.0))
        pr, pi = [], []
        qr, qi = lr, li
        for _ in range(8):
            pr.append(flat(qr))
            pi.append(flat(qi))
            qr, qi = qr * lr - qi * li, qr * li + qi * lr
        row = lax.broadcasted_iota(jnp.int32, (8, GP), 0)
        for n, k in enumerate((1, 2, 4)):
            tf_ref[2 * n] = jnp.where(row >= k, pr[k - 1], 0.0)
            tf_ref[2 * n + 1] = jnp.where(row >= k, pi[k - 1], 0.0)
            tr_ref[2 * n] = jnp.where(row < 8 - k, pr[k - 1], 0.0)
            tr_ref[2 * n + 1] = jnp.where(row < 8 - k, -pi[k - 1], 0.0)
        pick = lambda vals: sum(jnp.where(row == j, v, 0.0) for j, v in enumerate(vals))
        tf_ref[6] = pick(pr)
        tf_ref[7] = pick(pi)
        tr_ref[6] = pick(pr[::-1])
        tr_ref[7] = pick([-v for v in pi[::-1]])

    return pl.pallas_call(
        body, name="ssm_param_fwd",
        out_shape=[jax.ShapeDtypeStruct((SSM_WIDTH, 2 * GP), BF16), jax.ShapeDtypeStruct((SSM_WIDTH, 2 * GP), BF16),
                   jax.ShapeDtypeStruct((8, 8, GP), F32), jax.ShapeDtypeStruct((8, 8, GP), F32)],
        compiler_params=_cparams(),
    )(a_re, a_im, log_dt, b_re, b_im, c_re, c_im)


STRIP_CH = 128
STRIP_ST = 512
N_STRIPS = SSM_WIDTH // STRIP_CH


def _ssm_param_bwd(a_re, a_im, log_dt, b_re, b_im, g_lr, g_li, g_wb, g_wct):
    def body(ar_ref, ai_ref, ldt_ref, br_ref, bi_ref, glr_ref, gli_ref, gwb_ref, gwc_ref,
             o_ar, o_ai, o_ldt, o_br, o_bi, o_cr, o_ci):
        sel = _state_selector()
        own1 = _own_group(SSM_GROUPS, 0)
        row = lax.broadcasted_iota(jnp.int32, (SSM_WIDTH, STRIP_ST), 0)
        col = lax.broadcasted_iota(jnp.int32, (SSM_WIDTH, STRIP_ST), 1)
        own = jnp.bitwise_and(jnp.right_shift(row, 4), 7) == jnp.right_shift(col, 6)
        blocks = lambda m: _collect(jnp.where(own, m, 0.0), sel[:, 0:STRIP_ST])
        unflat = lambda v: _collect(jnp.where(own1, v, 0.0), sel)
        _, vjp = jax.vjp(_ssm_param_fn, ar_ref[...], ai_ref[...], ldt_ref[...], br_ref[...], bi_ref[...])
        d_ar, d_ai, d_ldt, d_br, d_bi = vjp((unflat(glr_ref[...]), unflat(gli_ref[...]),
                                             blocks(gwb_ref[:, 0:STRIP_ST]), blocks(gwb_ref[:, STRIP_ST:2 * STRIP_ST])))
        o_ar[...] = d_ar
        o_ai[...] = d_ai
        o_ldt[...] = d_ldt
        o_br[...] = d_br
        o_bi[...] = d_bi
        o_cr[...] = blocks(gwc_ref[:, 0:STRIP_ST])
        o_ci[...] = -blocks(gwc_ref[:, STRIP_ST:2 * STRIP_ST])

    g, p = SSM_GROUPS, SSM_STATE
    gp = jax.ShapeDtypeStruct((g, p), F32)
    gcp = jax.ShapeDtypeStruct((SSM_WIDTH, p), F32)
    return pl.pallas_call(
        body, name="ssm_param_bwd", out_shape=[gp, gp, jax.ShapeDtypeStruct((g, 1), F32), gcp, gcp, gcp, gcp],
        compiler_params=_cparams(),
    )(a_re, a_im, log_dt, b_re, b_im, g_lr, g_li, g_wb, g_wct)


def _strip(ref, j, im):
    return ref[STRIP_CH * j:STRIP_CH * (j + 1), im * GP + STRIP_ST * j:im * GP + STRIP_ST * (j + 1)]


def _wgrad_strips(a, b_re, b_im, name, im_block, token):
    l = a.shape[0]
    bl = min(l, 512)

    def body(a_ref, bre_ref, bim_ref, token_ref, o_ref):
        first = pl.program_id(0) == 0
        for j in range(N_STRIPS):
            aj = a_ref[:, STRIP_CH * j:STRIP_CH * (j + 1)]
            states = slice(STRIP_ST * j, STRIP_ST * (j + 1))
            _accumulate(o_ref.at[STRIP_CH * j:STRIP_CH * (j + 1), 0:STRIP_ST], _mm_tn(aj, bre_ref[:, states]), first)
            _accumulate(o_ref.at[STRIP_CH * j:STRIP_CH * (j + 1), STRIP_ST:2 * STRIP_ST], _mm_tn(aj, bim_ref[:, states]), first)

    return pl.pallas_call(
        body, name=name, grid=(l // bl,),
        in_specs=[pl.BlockSpec((bl, SSM_WIDTH), lambda k: (k, 0)), pl.BlockSpec((bl, GP), lambda k: (k, 0)),
                  pl.BlockSpec((bl, GP), lambda k: (k, im_block)), ANY],
        out_specs=pl.BlockSpec((SSM_WIDTH, 2 * STRIP_ST), lambda k: (0, 0)),
        out_shape=jax.ShapeDtypeStruct((SSM_WIDTH, 2 * STRIP_ST), F32),
        compiler_params=_cparams("arbitrary"),
    )(a, b_re, b_im, token)


SCAN_STRIP = 512


def _scan_chunk(inr_ref, ini_ref, outr_ref, outi_ref, cr_ref, ci_ref, tab_ref, tc, reverse):
    n_blocks = tc // 8

    def block(j, _):
        i = (n_blocks - 1 - j) if reverse else j
        rows = pl.ds(pl.multiple_of(i * 8, 8), 8)
        for s in range(GP // SCAN_STRIP):
            sl = pl.ds(s * SCAN_STRIP, SCAN_STRIP)
            xr = inr_ref[rows, sl]
            xi = ini_ref[rows, sl]
            for n, k in enumerate((1, 2, 4)):
                shift = (8 - k) if reverse else k
                sr = pltpu.roll(xr, shift, 0)
                si = pltpu.roll(xi, shift, 0)
                mr = tab_ref[2 * n, :, sl]
                mi = tab_ref[2 * n + 1, :, sl]
                xr, xi = xr + mr * sr - mi * si, xi + mr * si + mi * sr
            qr = tab_ref[6, :, sl]
            qi = tab_ref[7, :, sl]
            cr = cr_ref[:, sl]
            ci = ci_ref[:, sl]
            xr, xi = xr + qr * cr - qi * ci, xi + qr * ci + qi * cr
            outr_ref[rows, sl] = xr
            outi_ref[rows, sl] = xi
            edge = 0 if reverse else 7
            cr_ref[:, sl] = jnp.broadcast_to(xr[edge:edge + 1, :], (8, SCAN_STRIP))
            ci_ref[:, sl] = jnp.broadcast_to(xi[edge:edge + 1, :], (8, SCAN_STRIP))
        return 0

    lax.fori_loop(0, n_blocks, block, 0)


def _glu_pre(z, wg_ref):
    return sum(_mm(z[:, 128 * j:128 * (j + 1)], wg_ref[j]) for j in range(4))


def _ssm_fwd(u, wb, wc, tabs, dskip, grp_d, b_glu, grp_e, tc):
    l = u.shape[0]

    def body(u_ref, wb_ref, wc_ref, tab_ref, d_ref, wg_ref, bg_ref, wo_ref, xr_ref, xi_ref, y_ref, ys_ref,
             bur, bui, cr, ci):
        @pl.when(pl.program_id(0) == 0)
        def _():
            cr[...] = jnp.zeros_like(cr)
            ci[...] = jnp.zeros_like(ci)

        uv = u_ref[...]
        ub = uv.astype(BF16)
        for j in range(N_STRIPS):
            uj = ub[:, STRIP_CH * j:STRIP_CH * (j + 1)]
            states = slice(STRIP_ST * j, STRIP_ST * (j + 1))
            bur[:, states] = _mm(uj, _strip(wb_ref, j, 0))
            bui[:, states] = _mm(uj, _strip(wb_ref, j, 1))
        _scan_chunk(bur, bui, xr_ref, xi_ref, cr, ci, tab_ref, tc, False)
        y = jnp.concatenate(
            [_mm_nt(xr_ref[:, STRIP_ST * j:STRIP_ST * (j + 1)], _strip(wc_ref, j, 0))
             + _mm_nt(xi_ref[:, STRIP_ST * j:STRIP_ST * (j + 1)], _strip(wc_ref, j, 1)) for j in range(N_STRIPS)],
            axis=-1) + d_ref[...] * uv
        y_ref[...] = y
        z = _gelu(y)
        z2 = z * _sigmoid(_glu_pre(z, wg_ref) + bg_ref[...])
        for s in range(4):
            ys_ref[:, 256 * s:256 * (s + 1)] = _mm(z2, wo_ref[s])

    return pl.pallas_call(
        body, name="ssm_fwd", grid=(l // tc,),
        in_specs=[_rows(tc, 512), _resident((512, 2 * GP)), _resident((512, 2 * GP)), _resident((8, 8, GP)),
                  _resident((1, 512)), _member_block("w_glu"), _resident((1, 512)), _member_block("w_o_ssm")],
        out_specs=[_rows(tc, GP), _rows(tc, GP), _rows(tc, 512), _rows(tc, D_MODEL)],
        out_shape=[jax.ShapeDtypeStruct((l, GP), F32), jax.ShapeDtypeStruct((l, GP), F32),
                   jax.ShapeDtypeStruct((l, 512), F32), jax.ShapeDtypeStruct((l, D_MODEL), F32)],
        scratch_shapes=[pltpu.VMEM((tc, GP), F32), pltpu.VMEM((tc, GP), F32), pltpu.VMEM((8, GP), F32),
                        pltpu.VMEM((8, GP), F32)],
        compiler_params=_cparams("arbitrary"),
    )(u, wb, wc, tabs, dskip, grp_d, b_glu, grp_e)


def _ssm_bwd(dys, y, u, xr, xi, wb, wc, tabs_rev, dskip, grp_d, b_glu, grp_e, tc):
    l = u.shape[0]
    nc = l // tc

    def body(dys_ref, y_ref, u_ref, xr_ref, xi_ref, wb_ref, wc_ref, tab_ref, d_ref, wg_ref, bg_ref, wo_ref,
             du_ref, a_ref, dy_ref, z_ref, z2_ref, dpre_ref, gb_ref, gd_ref, glr_ref, gli_ref,
             dxr, dxi, ar, ai, cr, ci):
        first = pl.program_id(0) == 0

        @pl.when(first)
        def _():
            cr[...] = jnp.zeros_like(cr)
            ci[...] = jnp.zeros_like(ci)

        yv = y_ref[...]
        uv = u_ref[...]
        dz2 = sum(_mm_nt(dys_ref[:, 256 * j:256 * (j + 1)], wo_ref[j]) for j in range(4))
        z = _gelu(yv)
        s = _sigmoid(_glu_pre(z, wg_ref) + bg_ref[...])
        dpre = dz2 * z * s * (1.0 - s)
        dpreb = dpre.astype(BF16)
        dz = dz2 * s + jnp.concatenate([_mm_nt(dpreb, wg_ref[j]) for j in range(4)], axis=-1)
        dy = dz * _gelu_grad(yv)
        z_ref[...] = z.astype(BF16)
        z2_ref[...] = (z * s).astype(BF16)
        dpre_ref[...] = dpre.astype(BF16)
        dy_ref[...] = dy.astype(BF16)
        _accumulate(gb_ref, _colsum(dpre), first)
        _accumulate(gd_ref, _colsum(dy * uv), first)

        dyb = dy.astype(BF16)
        for j in range(N_STRIPS):
            dyj = dyb[:, STRIP_CH * j:STRIP_CH * (j + 1)]
            dxr[:, STRIP_ST * j:STRIP_ST * (j + 1)] = _mm(dyj, _strip(wc_ref, j, 0))
            dxi[:, STRIP_ST * j:STRIP_ST * (j + 1)] = _mm(dyj, _strip(wc_ref, j, 1))
        ar[pl.ds(tc, 8), :] = cr[...]
        ai[pl.ds(tc, 8), :] = ci[...]
        _scan_chunk(dxr, dxi, ar, ai, cr, ci, tab_ref, tc, True)
        a_ref[:, 0:GP] = ar[pl.ds(0, tc), :].astype(BF16)
        a_ref[:, GP:2 * GP] = ai[pl.ds(0, tc), :].astype(BF16)
        du_states = jnp.concatenate(
            [_mm_nt(a_ref[:, STRIP_ST * j:STRIP_ST * (j + 1)], _strip(wb_ref, j, 0))
             + _mm_nt(a_ref[:, GP + STRIP_ST * j:GP + STRIP_ST * (j + 1)], _strip(wb_ref, j, 1)) for j in range(N_STRIPS)],
            axis=-1)
        du_ref[...] = (dy * d_ref[...] + du_states).astype(BF16)
        anr = ar[pl.ds(1, tc), :]
        ani = ai[pl.ds(1, tc), :]
        xrv = xr_ref[...]
        xiv = xi_ref[...]
        _accumulate(glr_ref, _colsum(anr * xrv + ani * xiv), first)
        _accumulate(gli_ref, _colsum(ani * xrv - anr * xiv), first)

    rev = lambda w: pl.BlockSpec((tc, w), lambda i: (nc - 1 - i, 0))
    acc = lambda w: pl.BlockSpec((1, w), lambda i: (0, 0))
    bf = jax.ShapeDtypeStruct((l, 512), BF16)
    return pl.pallas_call(
        body, name="ssm_bwd", grid=(nc,),
        in_specs=[rev(D_MODEL), rev(512), rev(512), rev(GP), rev(GP), _resident((512, 2 * GP)), _resident((512, 2 * GP)),
                  _resident((8, 8, GP)), _resident((1, 512)), _member_block("w_glu"), _resident((1, 512)),
                  _member_block("w_o_ssm")],
        out_specs=[rev(512), rev(2 * GP), rev(512), rev(512), rev(512), rev(512), acc(512), acc(512), acc(GP), acc(GP)],
        out_shape=[bf, jax.ShapeDtypeStruct((l, 2 * GP), BF16), bf, bf, bf, bf,
                   jax.ShapeDtypeStruct((1, 512), F32), jax.ShapeDtypeStruct((1, 512), F32),
                   jax.ShapeDtypeStruct((1, GP), F32), jax.ShapeDtypeStruct((1, GP), F32)],
        scratch_shapes=[pltpu.VMEM((tc, GP), F32), pltpu.VMEM((tc, GP), F32), pltpu.VMEM((tc + 8, GP), F32),
                        pltpu.VMEM((tc + 8, GP), F32), pltpu.VMEM((8, GP), F32), pltpu.VMEM((8, GP), F32)],
        compiler_params=_cparams("arbitrary"),
    )(dys, y, u, xr, xi, wb, wc, tabs_rev, dskip, grp_d, b_glu, grp_e)


def _swap_halves(b):
    lane = lax.broadcasted_iota(jnp.int32, b.shape, 1)
    return jnp.where(lane < 32, pltpu.roll(b, 96, 1), pltpu.roll(b, 32, 1))


def _rope_tables(pos_ref, invf_ref, sgn_ref):
    ang = pos_ref[...].astype(F32) * invf_ref[...]
    return jnp.cos(ang), jnp.sin(ang) * sgn_ref[...]


def _mla_pre_fwd(lat, pos, invf, sgn, gqa, gkva, gq, gk, w_qb_p, w_kvb, t):
    l = lat.shape[0]

    def body(lat_ref, pos_ref, invf_ref, sgn_ref, gqa_ref, gkva_ref, gq_ref, gk_ref, wq_ref, wkv_ref, q_ref, k_ref, v_ref):
        cs, sn = _rope_tables(pos_ref, invf_ref, sgn_ref)
        ql = _rms_fwd(lat_ref[:, 0:Q_LORA], gqa_ref[...], Q_LORA)
        ckn = _rms_fwd(lat_ref[:, Q_LORA:Q_LORA + KV_LORA], gkva_ref[...], KV_LORA)
        kpe = lat_ref[:, 640:768]
        q0 = _mm(ql, wq_ref[...])
        cknb = ckn.astype(BF16)
        kv = jnp.concatenate([_mm(cknb, wkv_ref[s]) for s in range(4)], axis=-1)
        for h in range(N_HEADS):
            q1 = _rms_fwd(q0[:, HEAD_PAD * h:HEAD_PAD * (h + 1)], gq_ref[...], QK_HEAD)
            b = q1[:, 128:256]
            q_ref[h, :, 0:128] = (q1[:, 0:128] * ATT_SCALE).astype(BF16)
            q_ref[h, :, 128:256] = ((b * cs + _swap_halves(b) * sn) * ATT_SCALE).astype(BF16)
            k0 = jnp.concatenate([kv[:, 256 * h:256 * h + 128], kpe], axis=-1)
            k1 = _rms_fwd(k0, gk_ref[...], QK_HEAD)
            b = k1[:, 128:256]
            k_ref[h, :, 0:128] = k1[:, 0:128].astype(BF16)
            k_ref[h, :, 128:256] = (b * cs + _swap_halves(b) * sn).astype(BF16)
            v_ref[h] = kv[:, 256 * h + 128:256 * h + 256].astype(BF16)

    heads = lambda w: pl.BlockSpec((N_HEADS, t, w), lambda i: (0, i, 0))
    return pl.pallas_call(
        body, name="mla_pre_fwd", grid=(l // t,),
        in_specs=[_rows(t, LAT_W), _rows(t, 1), _resident((1, 128)), _resident((1, 128)), _resident((1, Q_LORA)),
                  _resident((1, KV_LORA)), _resident((1, HEAD_PAD)), _resident((1, HEAD_PAD)),
                  _resident((Q_LORA, N_HEADS * HEAD_PAD)), _member_block("w_kv_b")],
        out_specs=[heads(HEAD_PAD), heads(HEAD_PAD), heads(V_HEAD)],
        out_shape=[jax.ShapeDtypeStruct((N_HEADS, l, HEAD_PAD), BF16), jax.ShapeDtypeStruct((N_HEADS, l, HEAD_PAD), BF16),
                   jax.ShapeDtypeStruct((N_HEADS, l, V_HEAD), BF16)],
        compiler_params=_cparams("parallel"),
    )(lat, pos, invf, sgn, gqa, gkva, gq, gk, w_qb_p, w_kvb)


def _mla_pre_bwd(lat, pos, invf, sgn, gqa, gkva, gq, gk, w_qb_p, w_kvb, dq, dk, dv, t, token):
    l = lat.shape[0]

    def body(lat_ref, pos_ref, invf_ref, sgn_ref, gqa_ref, gkva_ref, gq_ref, gk_ref, wq_ref, wkv_ref, dq_ref, dk_ref, dv_ref,
             token_ref, dlat_ref, ql_ref, dq0_ref, ckn_ref, dkv_ref, ggqa_ref, ggkva_ref, ggq_ref, ggk_ref):
        first = pl.program_id(0) == 0
        cs, sn = _rope_tables(pos_ref, invf_ref, sgn_ref)
        q_lat = lat_ref[:, 0:Q_LORA]
        c_kv = lat_ref[:, Q_LORA:Q_LORA + KV_LORA]
        kpe = lat_ref[:, 640:768]
        ql = _rms_fwd(q_lat, gqa_ref[...], Q_LORA)
        ckn = _rms_fwd(c_kv, gkva_ref[...], KV_LORA)
        ql_ref[...] = ql.astype(BF16)
        ckn_ref[...] = ckn.astype(BF16)
        q0 = _mm(ql, wq_ref[...])
        cknb = ckn.astype(BF16)
        kv = jnp.concatenate([_mm(cknb, wkv_ref[s]) for s in range(4)], axis=-1)
        dkpe = jnp.zeros_like(kpe)
        ggq = jnp.zeros((1, HEAD_PAD), F32)
        ggk = jnp.zeros((1, HEAD_PAD), F32)

        def unrope(d):
            b = d[:, 128:256]
            return jnp.concatenate([d[:, 0:128], b * cs + _swap_halves(b * sn)], axis=-1)

        for h in range(N_HEADS):
            dq1 = unrope(dq_ref[h] * ATT_SCALE)
            dq0h, gq_rows = _rms_bwd(q0[:, HEAD_PAD * h:HEAD_PAD * (h + 1)], gq_ref[...], dq1, QK_HEAD)
            ggq = ggq + _colsum(gq_rows)
            dq0_ref[:, HEAD_PAD * h:HEAD_PAD * (h + 1)] = dq0h.astype(BF16)
            k0 = jnp.concatenate([kv[:, 256 * h:256 * h + 128], kpe], axis=-1)
            dk0, gk_rows = _rms_bwd(k0, gk_ref[...], unrope(dk_ref[h]), QK_HEAD)
            ggk = ggk + _colsum(gk_rows)
            dkpe = dkpe + dk0[:, 128:256]
            dkv_ref[:, 256 * h:256 * h + 128] = dk0[:, 0:128].astype(BF16)
            dkv_ref[:, 256 * h + 128:256 * h + 256] = dv_ref[h].astype(BF16)
        dql = _mm_nt(dq0_ref[...], wq_ref[...])
        dckn = sum(_mm_nt(dkv_ref[:, 512 * s:512 * (s + 1)], wkv_ref[s]) for s in range(4))
        dq_lat, gqa_rows = _rms_bwd(q_lat, gqa_ref[...], dql, Q_LORA)
        dc_kv, gkva_rows = _rms_bwd(c_kv, gkva_ref[...], dckn, KV_LORA)
        dlat_ref[:, 0:Q_LORA] = dq_lat.astype(BF16)
        dlat_ref[:, Q_LORA:Q_LORA + KV_LORA] = dc_kv.astype(BF16)
        dlat_ref[:, 640:768] = dkpe.astype(BF16)
        _accumulate(ggqa_ref, _colsum(gqa_rows), first)
        _accumulate(ggkva_ref, _colsum(gkva_rows), first)
        _accumulate(ggq_ref, ggq, first)
        _accumulate(ggk_ref, ggk, first)

    heads = lambda w: pl.BlockSpec((N_HEADS, t, w), lambda i: (0, i, 0))
    acc = lambda w: pl.BlockSpec((1, w), lambda i: (0, 0))
    return pl.pallas_call(
        body, name="mla_pre_bwd", grid=(l // t,),
        in_specs=[_rows(t, LAT_W), _rows(t, 1), _resident((1, 128)), _resident((1, 128)), _resident((1, Q_LORA)),
                  _resident((1, KV_LORA)), _resident((1, HEAD_PAD)), _resident((1, HEAD_PAD)),
                  _resident((Q_LORA, N_HEADS * HEAD_PAD)), _member_block("w_kv_b"),
                  heads(HEAD_PAD), heads(HEAD_PAD), heads(V_HEAD), ANY],
        out_specs=[_rows(t, LAT_W), _rows(t, Q_LORA), _rows(t, N_HEADS * HEAD_PAD), _rows(t, KV_LORA), _rows(t, N_HEADS * 256),
                   acc(Q_LORA), acc(KV_LORA), acc(HEAD_PAD), acc(HEAD_PAD)],
        out_shape=[jax.ShapeDtypeStruct((l, LAT_W), BF16), jax.ShapeDtypeStruct((l, Q_LORA), BF16),
                   jax.ShapeDtypeStruct((l, N_HEADS * HEAD_PAD), BF16), jax.ShapeDtypeStruct((l, KV_LORA), BF16),
                   jax.ShapeDtypeStruct((l, N_HEADS * 256), BF16), jax.ShapeDtypeStruct((1, Q_LORA), F32),
                   jax.ShapeDtypeStruct((1, KV_LORA), F32), jax.ShapeDtypeStruct((1, HEAD_PAD), F32),
                   jax.ShapeDtypeStruct((1, HEAD_PAD), F32)],
        compiler_params=_cparams("arbitrary"),
    )(lat, pos, invf, sgn, gqa, gkva, gq, gk, w_qb_p, w_kvb, dq, dk, dv, token)


def _causal(s, transposed):
    row = lax.broadcasted_iota(jnp.int32, s.shape, 0)
    col = lax.broadcasted_iota(jnp.int32, s.shape, 1)
    keep = (row <= col) if transposed else (col <= row)
    return jnp.where(keep, s, -jnp.inf)


def _as_row(col):
    n = col.shape[0]
    row = lax.broadcasted_iota(jnp.int32, (n, n), 0)
    lane = lax.broadcasted_iota(jnp.int32, (n, n), 1)
    return jnp.sum(jnp.where(row == lane, col, 0.0), axis=0, keepdims=True)


def _attn_fwd(q, k, v, tq):
    l = q.shape[1]

    hb = 2

    def body(q_ref, k_ref, v_ref, o_ref, lse_ref):
        qi = pl.program_id(1)
        qs = [q_ref[a] for a in range(hb)]

        def step(kb, carry, masked):
            rows = pl.ds(pl.multiple_of(kb * tq, tq), tq)
            out = []
            for a, (m, den, acc) in enumerate(carry):
                s = _mm_nt(qs[a], k_ref[a, rows, :])
                if masked:
                    s = _causal(s, False)
                m_new = jnp.maximum(m, jnp.max(s, axis=-1, keepdims=True))
                alpha = jnp.exp(m - m_new)
                p = jnp.exp(s - m_new)
                den = alpha * den + jnp.sum(p, axis=-1, keepdims=True)
                acc = alpha * acc + _mm(p, v_ref[a, rows, :])
                out.append((m_new, den, acc))
            return tuple(out)

        init = tuple((jnp.full((tq, 1), -jnp.inf, F32), jnp.zeros((tq, 1), F32), jnp.zeros((tq, V_HEAD), F32))
                     for _ in range(hb))
        carry = lax.fori_loop(0, qi, lambda kb, c: step(kb, c, False), init)
        for a, (m, den, acc) in enumerate(step(qi, carry, True)):
            o_ref[:, V_HEAD * a:V_HEAD * (a + 1)] = acc / den
            lse_ref[a, 0] = _as_row(m + jnp.log(den))

    return pl.pallas_call(
        body, name="attn_fwd", grid=(N_HEADS // hb, l // tq),
        in_specs=[pl.BlockSpec((hb, tq, HEAD_PAD), lambda h, i: (h, i, 0)), pl.BlockSpec((hb, l, HEAD_PAD), lambda h, i: (h, 0, 0)),
                  pl.BlockSpec((hb, l, V_HEAD), lambda h, i: (h, 0, 0))],
        out_specs=[pl.BlockSpec((tq, hb * V_HEAD), lambda h, i: (i, h)), pl.BlockSpec((hb, 1, 1, tq), lambda h, i: (h, i, 0, 0))],
        out_shape=[jax.ShapeDtypeStruct((l, N_HEADS * V_HEAD), F32), jax.ShapeDtypeStruct((N_HEADS, l // tq, 1, tq), F32)],
        compiler_params=_cparams("parallel", "arbitrary"),
    )(q, k, v)


def _attn_bwd(q, k, v, o, do, lse_t, tq, token):
    l = q.shape[1]
    nq = l // tq

    hb = 1

    def body(q_ref, k_ref, v_ref, o_ref, do_ref, lse_ref, token_ref, dq_ref, dk_ref, dv_ref):
        ki = pl.program_id(1)

        @pl.when(ki == 0)
        def _():
            dq_ref[...] = jnp.zeros_like(dq_ref)

        kblks = [k_ref[a] for a in range(hb)]
        vblks = [v_ref[a] for a in range(hb)]
        ones = jnp.ones((8, V_HEAD), BF16)

        def step(qb, carry, masked):
            rows = pl.ds(pl.multiple_of(qb * tq, tq), tq)
            out = []
            for a, (dk, dv) in enumerate(carry):
                cols = slice(V_HEAD * a, V_HEAD * (a + 1))
                qblk = q_ref[a, rows, :]
                dov = do_ref[rows, cols]
                dob = dov.astype(BF16)
                delta = sum(_mm_nt(ones, part) for part in _three_bf16(dov * o_ref[rows, cols]))[0:1, :]
                st = _mm_nt(kblks[a], qblk)
                if masked:
                    st = _causal(st, True)
                pt = jnp.exp(st - lse_ref[a, qb])
                dv = dv + _mm(pt, dob)
                dst = (pt * (_mm_nt(vblks[a], dob) - delta)).astype(BF16)
                dk = dk + _mm(dst, qblk)
                dq_ref[a, rows, :] += _mm_tn(dst, kblks[a])
                out.append((dk, dv))
            return tuple(out)

        init = tuple((jnp.zeros((tq, HEAD_PAD), F32), jnp.zeros((tq, V_HEAD), F32)) for _ in range(hb))
        carry = lax.fori_loop(ki + 1, nq, lambda qb, c: step(qb, c, False), step(ki, init, True))
        for a, (dk, dv) in enumerate(carry):
            dk_ref[a] = dk
            dv_ref[a] = dv

    return pl.pallas_call(
        body, name="attn_bwd", grid=(N_HEADS // hb, nq),
        in_specs=[pl.BlockSpec((hb, l, HEAD_PAD), lambda h, i: (h, 0, 0)), pl.BlockSpec((hb, tq, HEAD_PAD), lambda h, i: (h, i, 0)),
                  pl.BlockSpec((hb, tq, V_HEAD), lambda h, i: (h, i, 0)), pl.BlockSpec((l, hb * V_HEAD), lambda h, i: (0, h)),
                  pl.BlockSpec((l, hb * V_HEAD), lambda h, i: (0, h)), pl.BlockSpec((hb, nq, 1, tq), lambda h, i: (h, 0, 0, 0)), ANY],
        out_specs=[pl.BlockSpec((hb, l, HEAD_PAD), lambda h, i: (h, 0, 0)), pl.BlockSpec((hb, tq, HEAD_PAD), lambda h, i: (h, i, 0)),
                   pl.BlockSpec((hb, tq, V_HEAD), lambda h, i: (h, i, 0))],
        out_shape=[jax.ShapeDtypeStruct((N_HEADS, l, HEAD_PAD), F32), jax.ShapeDtypeStruct((N_HEADS, l, HEAD_PAD), F32),
                   jax.ShapeDtypeStruct((N_HEADS, l, V_HEAD), F32)],
        compiler_params=_cparams("parallel", "arbitrary"),
    )(q, k, v, o, do, lse_t, token)


def _row_shards_mm(a, w_ref):
    a = a.astype(BF16)
    return sum(_mm(a[:, 256 * j:256 * (j + 1)], w_ref[j]) for j in range(4))


def _row_shards_mm_nt(a, w_ref):
    a = a.astype(BF16)
    return jnp.concatenate([_mm_nt(a, w_ref[j]) for j in range(4)], axis=-1)


def _merge_fwd(attn, y_ssm, gs, gm, x, grp_a, t):
    l = x.shape[0]

    def body(attn_ref, ys_ref, gs_ref, gm_ref, x_ref, wo_ref, wout_ref, ym_ref, mixed_ref, h_ref):
        y_mla = _row_shards_mm(attn_ref[...], wo_ref)
        ym_ref[...] = y_mla
        mixed = (_sigmoid(gs_ref[...]) * ys_ref[...] + _sigmoid(gm_ref[...]) * y_mla).astype(BF16)
        mixed_ref[...] = mixed
        h_ref[...] = x_ref[...] + _row_shards_mm(mixed, wout_ref)

    r = lambda: _rows(t, D_MODEL)
    return pl.pallas_call(
        body, name="merge_fwd", grid=(l // t,),
        in_specs=[r(), r(), r(), r(), r(), _member_block("w_o_mla"), _member_block("w_out")],
        out_specs=[r(), r(), r()],
        out_shape=[jax.ShapeDtypeStruct((l, D_MODEL), F32), jax.ShapeDtypeStruct((l, D_MODEL), BF16),
                   jax.ShapeDtypeStruct((l, D_MODEL), F32)],
        compiler_params=_cparams("parallel"),
    )(attn, y_ssm, gs, gm, x, grp_a, grp_a)


def _merge_bwd(dh, y_ssm, y_mla, gs, gm, grp_a, t):
    l = dh.shape[0]

    def body(dh_ref, ys_ref, ym_ref, gs_ref, gm_ref, wo_ref, wout_ref, dys_ref, dym_ref, dgs_ref, dgm_ref, dattn_ref):
        dmixed = _row_shards_mm_nt(dh_ref[...], wout_ref)
        sg = _sigmoid(gs_ref[...])
        sm = _sigmoid(gm_ref[...])
        dys_ref[...] = (dmixed * sg).astype(BF16)
        dgs_ref[...] = (dmixed * ys_ref[...] * sg * (1.0 - sg)).astype(BF16)
        dym = (dmixed * sm).astype(BF16)
        dym_ref[...] = dym
        dgm_ref[...] = (dmixed * ym_ref[...] * sm * (1.0 - sm)).astype(BF16)
        dattn_ref[...] = _row_shards_mm_nt(dym, wo_ref)

    r = lambda: _rows(t, D_MODEL)
    bf = jax.ShapeDtypeStruct((l, D_MODEL), BF16)
    return pl.pallas_call(
        body, name="merge_bwd", grid=(l // t,),
        in_specs=[r(), r(), r(), r(), r(), _member_block("w_o_mla"), _member_block("w_out")],
        out_specs=[r(), r(), r(), r(), r()],
        out_shape=[bf, bf, bf, bf, jax.ShapeDtypeStruct((l, D_MODEL), F32)],
        compiler_params=_cparams("parallel"),
    )(dh, y_ssm, y_mla, gs, gm, grp_a, grp_a)


def _mlp_fwd_bwd(h, tgt, g2, grp_a, t):
    l = h.shape[0]

    def body(h_ref, tgt_ref, g_ref, wu_ref, wd_ref, dh_ref, hn_ref, da_ref, hid_ref, dout_ref, loss_ref, dg_ref):
        first = pl.program_id(0) == 0
        hv = h_ref[...]
        g = g_ref[...]
        hn = _rms_fwd(hv, g, D_MODEL).astype(BF16)
        hn_ref[...] = hn
        out = hv
        relus = []
        for s in range(4):
            cols = slice(1024 * s, 1024 * (s + 1))
            relu = jnp.maximum(_mm(hn, wu_ref[s]), 0.0)
            relus.append(relu)
            hid = (relu * relu).astype(BF16)
            hid_ref[:, cols] = hid
            out = out + _mm(hid, wd_ref[s])
        err = out - tgt_ref[...]
        _accumulate(loss_ref, jnp.full((8, 128), jnp.sum(err * err) * (0.5 / D_MODEL), F32), first)
        dout = err * (1.0 / D_MODEL)
        doutb = dout.astype(BF16)
        dout_ref[...] = doutb
        dhn = jnp.zeros_like(hv)
        for s in range(4):
            da = (_mm_nt(doutb, wd_ref[s]) * (2.0 * relus[s])).astype(BF16)
            da_ref[:, 1024 * s:1024 * (s + 1)] = da
            dhn = dhn + _mm_nt(da, wu_ref[s])
        dx, dg_rows = _rms_bwd(hv, g, dhn, D_MODEL)
        dh_ref[...] = dout + dx
        _accumulate(dg_ref, _colsum(dg_rows), first)

    r = lambda w: _rows(t, w)
    return pl.pallas_call(
        body, name="mlp_fwd_bwd", grid=(l // t,),
        in_specs=[r(D_MODEL), r(D_MODEL), _resident((1, D_MODEL)), _member_block("w_up"), _member_block("w_down")],
        out_specs=[r(D_MODEL), r(D_MODEL), r(D_FF), r(D_FF), r(D_MODEL), pl.BlockSpec((8, 128), lambda i: (0, 0)),
                   pl.BlockSpec((1, D_MODEL), lambda i: (0, 0))],
        out_shape=[jax.ShapeDtypeStruct((l, D_MODEL), F32), jax.ShapeDtypeStruct((l, D_MODEL), BF16),
                   jax.ShapeDtypeStruct((l, D_FF), BF16), jax.ShapeDtypeStruct((l, D_FF), BF16),
                   jax.ShapeDtypeStruct((l, D_MODEL), BF16), jax.ShapeDtypeStruct((8, 128), F32),
                   jax.ShapeDtypeStruct((1, D_MODEL), F32)],
        compiler_params=_cparams("arbitrary"),
    )(h, tgt, g2, grp_a, grp_a)


def _wgrad(a, b, name):
    l, m = a.shape
    n = b.shape[1]
    bm = m if m <= 512 else 512
    bl = min(l, 2048 if n <= 1024 else 1024)

    def body(a_ref, b_ref, o_ref):
        _accumulate(o_ref, _mm_tn(a_ref[...], b_ref[...]), pl.program_id(1) == 0)

    return pl.pallas_call(
        body, name=name, grid=(m // bm, l // bl),
        in_specs=[pl.BlockSpec((bl, bm), lambda i, j: (j, i)), pl.BlockSpec((bl, n), lambda i, j: (j, 0))],
        out_specs=pl.BlockSpec((bm, n), lambda i, j: (i, 0)),
        out_shape=jax.ShapeDtypeStruct((m, n), F32),
        compiler_params=_cparams("parallel", "arbitrary"),
    )(a, b)


def _wgrad_into(a, b, member, cut, dest=None):
    group, off, rs, cs = _place_in_group(member)
    l = a.shape[0]
    bm = min(rs, 512)
    bl = min(l, 2048)
    nb = rs // bm
    if cut == "row":
        a_spec = pl.BlockSpec((bl, bm), lambda j, i, k: (k, j * nb + i))
        b_spec = pl.BlockSpec((bl, cs), lambda j, i, k: (k, 0))
    else:
        a_spec = pl.BlockSpec((bl, bm), lambda j, i, k: (k, i))
        b_spec = pl.BlockSpec((bl, cs), lambda j, i, k: (k, j))

    def body(a_ref, b_ref, *rest):
        o_ref = rest[-1]
        part = _mm_tn(a_ref[...], b_ref[...])

        @pl.when(pl.program_id(2) == 0)
        def _():
            o_ref[0] = part

        @pl.when(pl.program_id(2) != 0)
        def _():
            o_ref[0] += part

    operands, in_specs, aliases = [a, b], [a_spec, b_spec], {}
    if dest is not None:
        operands.append(dest)
        in_specs.append(ANY)
        aliases = {2: 0}
    return pl.pallas_call(
        body, name="wgrad_" + member, grid=(4, nb, l // bl), in_specs=in_specs,
        out_specs=pl.BlockSpec((1, bm, cs), lambda j, i, k: (j, off // bm + i, 0)),
        out_shape=jax.ShapeDtypeStruct((4, _group_rows(group), cs), F32), input_output_aliases=aliases,
        compiler_params=_cparams("parallel", "parallel", "arbitrary"),
    )(*operands)


def _adamw(w, g, m, v, name, g_off, token):
    r, c = w.shape
    br = r
    for cand in (256, 128, 64, 32, 16, 8):
        if r % cand == 0 and g_off % cand == 0:
            br = cand
            break

    def body(w_ref, g_ref, m_ref, v_ref, token_ref, go_ref, d_ref, nm_ref, nv_ref):
        gv = g_ref[...]
        go_ref[...] = gv
        nm = ADAM_B1 * m_ref[...] + (1.0 - ADAM_B1) * gv
        nv = ADAM_B2 * v_ref[...] + (1.0 - ADAM_B2) * (gv * gv)
        m_hat = nm / (1.0 - ADAM_B1 ** ADAM_STEP)
        v_hat = nv / (1.0 - ADAM_B2 ** ADAM_STEP)
        d_ref[...] = -ADAM_LR * (m_hat / (jnp.sqrt(v_hat) + ADAM_EPS) + ADAM_WD * w_ref[...])
        nm_ref[...] = nm
        nv_ref[...] = nv

    spec = lambda: pl.BlockSpec((br, c), lambda i: (i, 0))
    g_spec = pl.BlockSpec((br, c), lambda i: (g_off // br + i, 0))
    shp = jax.ShapeDtypeStruct((r, c), F32)
    return pl.pallas_call(
        body, name=name, grid=(r // br,), in_specs=[spec(), g_spec, spec(), spec(), ANY],
        out_specs=[spec(), spec(), spec(), spec()], out_shape=[shp, shp, shp, shp], compiler_params=_cparams("parallel"),
    )(w, g, m, v, token)


def _place():
    return lax.axis_index("x"), lax.axis_index("y"), lax.axis_index("c")


def _other_chips(x, y):
    return [(1 - x, y), (x, 1 - y), (1 - x, 1 - y)]


ANY = pl.BlockSpec(memory_space=pl.ANY)


def _gather_weights(bufs):
    n = len(bufs)

    def body(*refs):
        outs, send_sems, recv_sems = refs[n:2 * n], refs[2 * n], refs[2 * n + 1]
        x, y, c = _place()
        chips = _other_chips(x, y)

        def part(g, px, py, pc):
            half = outs[g].shape[1] // 2
            return outs[g].at[2 * px + py, pl.ds(pl.multiple_of(pc * half, 16), half), :]

        def copy(k, src, dst, to):
            return pltpu.make_async_remote_copy(src_ref=src, dst_ref=dst, send_sem=send_sems.at[k], recv_sem=recv_sems.at[k],
                                                device_id=to, device_id_type=MESH)

        first = [copy(6 * g + j, part(g, x, y, c), part(g, x, y, c), (*chip, c)) for g in range(n) for j, chip in enumerate(chips)]
        for cp in first:
            cp.start()
        passed = []
        for g in range(n):
            for j, chip in enumerate(chips):
                landed = part(g, *chip, c)
                copy(6 * g + j, landed, landed, (x, y, c)).wait_recv()
                passed.append(copy(6 * g + 3 + j, landed, landed, (x, y, 1 - c)))
                passed[-1].start()
        for g in range(n):
            for j, chip in enumerate(chips):
                other = part(g, *chip, 1 - c)
                copy(6 * g + 3 + j, other, other, (x, y, c)).wait_recv()
        for cp in first + passed:
            cp.wait_send()

    return pl.pallas_call(
        body, name="gather_weights", in_specs=[ANY] * n, out_specs=[ANY] * n,
        out_shape=[jax.ShapeDtypeStruct(b.shape, b.dtype) for b in bufs], input_output_aliases={g: g for g in range(n)},
        scratch_shapes=[pltpu.SemaphoreType.DMA((6 * n,)), pltpu.SemaphoreType.DMA((6 * n,))],
    )(*bufs)


def _cast_shards(shards, group, place):
    width, members = GROUPS[group]
    rows = _group_rows(group)

    def body(place_ref, *refs):
        out = refs[-1]
        off = 0
        for ref, (_, r) in zip(refs[:-1], members):
            out[0, off:off + r, :] = ref[...].astype(BF16)
            off += r

    grid_spec = pltpu.PrefetchScalarGridSpec(
        num_scalar_prefetch=1, grid=(1,),
        in_specs=[pl.BlockSpec((r, width), lambda i, p: (0, 0)) for _, r in members],
        out_specs=pl.BlockSpec((1, rows, width), lambda i, p: (p[0], 0, 0)))
    return pl.pallas_call(
        body, name="cast_shards_" + group, grid_spec=grid_spec, out_shape=jax.ShapeDtypeStruct((4, rows, width), BF16),
        compiler_params=_cparams("arbitrary"),
    )(place, *[shards[name] for name, _ in members])


def _block_rows(h):
    return next(cand for cand in (256, 192, 128, 64, 32, 16) if h % cand == 0)


def _add_pair(buf, got, place, name):
    n, h, w = got.shape
    bh = _block_rows(h)
    nb = h // bh

    def body(place_ref, a_ref, b_ref, s_ref, sb_ref):
        s = a_ref[...] + b_ref[...]
        s_ref[...] = s
        sb_ref[...] = s.astype(BF16)

    spec = lambda: pl.BlockSpec((1, bh, w), lambda j, i, p: (j, i, 0))
    grid_spec = pltpu.PrefetchScalarGridSpec(
        num_scalar_prefetch=1, grid=(n, nb),
        in_specs=[pl.BlockSpec((1, bh, w), lambda j, i, p: (j, p[1] * nb + i, 0)), spec()], out_specs=[spec(), spec()])
    return pl.pallas_call(
        body, name=name, grid_spec=grid_spec,
        out_shape=[jax.ShapeDtypeStruct(got.shape, F32), jax.ShapeDtypeStruct(got.shape, BF16)],
        compiler_params=_cparams("parallel", "parallel"),
    )(place, buf, got)


def _add_received(pair, got, place, name):
    _, h, w = pair.shape
    bh = _block_rows(h)
    nb = h // bh

    def body(place_ref, own_ref, got_ref, o_ref):
        o_ref[...] = ((own_ref[0] + got_ref[0].astype(F32)) + got_ref[1].astype(F32)) + got_ref[2].astype(F32)

    grid_spec = pltpu.PrefetchScalarGridSpec(
        num_scalar_prefetch=1, grid=(nb,),
        in_specs=[pl.BlockSpec((1, bh, w), lambda i, p: (p[0], i, 0)), pl.BlockSpec((3, bh, w), lambda i, p: (0, i, 0))],
        out_specs=pl.BlockSpec((bh, w), lambda i, p: (p[1] * nb + i, 0)))
    return pl.pallas_call(
        body, name=name, grid_spec=grid_spec, out_shape=jax.ShapeDtypeStruct((2 * h, w), F32),
        compiler_params=_cparams("parallel"),
    )(place, pair, got)


def _swap_reduced_halves(bufs):
    n = len(bufs)

    def body(*refs):
        outs, send_sems, recv_sems = refs[n:2 * n], refs[2 * n], refs[2 * n + 1]
        x, y, c = _place()
        copies = []
        for g in range(n):
            half = outs[g].shape[0] // 2
            own = outs[g].at[pl.ds(pl.multiple_of(c * half, 8), half), :]
            copies.append(pltpu.make_async_remote_copy(src_ref=own, dst_ref=own, send_sem=send_sems.at[g],
                                                       recv_sem=recv_sems.at[g], device_id=(x, y, 1 - c), device_id_type=MESH))
        for cp in copies:
            cp.start()
        for g in range(n):
            half = outs[g].shape[0] // 2
            other = outs[g].at[pl.ds(pl.multiple_of((1 - c) * half, 8), half), :]
            pltpu.make_async_remote_copy(src_ref=other, dst_ref=other, send_sem=send_sems.at[g], recv_sem=recv_sems.at[g],
                                         device_id=(x, y, 1 - c), device_id_type=MESH).wait_recv()
        for cp in copies:
            cp.wait_send()

    return pl.pallas_call(
        body, name="swap_reduced_halves", in_specs=[ANY] * n, out_specs=[ANY] * n,
        out_shape=[jax.ShapeDtypeStruct(b.shape, b.dtype) for b in bufs], input_output_aliases={g: g for g in range(n)},
        scratch_shapes=[pltpu.SemaphoreType.DMA((n,)), pltpu.SemaphoreType.DMA((n,))],
    )(*bufs)


HBM = pl.BlockSpec(memory_space=pltpu.HBM)
SEM = pl.BlockSpec(memory_space=pltpu.SEMAPHORE)


def _copies_start(name, bufs, n_copies, plan, after=None):
    n = len(bufs)
    extra = [] if after is None else [after]

    def body(*refs):
        sems = refs[n + len(extra):n + len(extra) + 2 * n_copies]
        x, y, c = _place()
        for i, (src, dst, dev) in enumerate(plan(refs[:n], x, y, c)):
            pltpu.make_async_remote_copy(src_ref=src, dst_ref=dst, send_sem=sems[i], recv_sem=sems[n_copies + i],
                                         device_id=dev, device_id_type=MESH).start()
        token = refs[-1]
        token[...] = jnp.zeros_like(token)

    out = pl.pallas_call(
        body, name=name,
        out_shape=[pltpu.SemaphoreType.DMA(())] * (2 * n_copies) + [pltpu.HBM(b.shape, b.dtype) for b in bufs]
        + [jax.ShapeDtypeStruct((8, 128), F32)],
        in_specs=[HBM] * n + [ANY] * len(extra),
        out_specs=[SEM] * (2 * n_copies) + [HBM] * n + [pl.BlockSpec(memory_space=pltpu.VMEM)],
        input_output_aliases={i: 2 * n_copies + i for i in range(n)},
        compiler_params=pltpu.CompilerParams(has_side_effects=pltpu.SideEffectType.DATAFLOW_SIDE_EFFECTING),
    )(*[pltpu.with_memory_space_constraint(b, pltpu.HBM) for b in bufs], *extra)
    return list(out[:2 * n_copies]), list(out[2 * n_copies:-1]), out[-1]


def _copies_wait(name, bufs, sems, after, plan):
    n = len(bufs)
    k = len(sems) // 2

    def body(*refs):
        sem_refs = refs[n:n + 2 * k]
        x, y, c = _place()
        for i, (sent, landed, dev) in enumerate(plan(refs[:n], x, y, c)):
            cp = pltpu.make_async_remote_copy(src_ref=sent, dst_ref=landed, send_sem=sem_refs[i], recv_sem=sem_refs[k + i],
                                              device_id=dev, device_id_type=MESH)
            cp.wait_send()
            cp.wait_recv()

    return pl.pallas_call(
        body, name=name, out_shape=[pltpu.HBM(b.shape, b.dtype) for b in bufs],
        in_specs=[HBM] * n + [SEM] * (2 * k) + [ANY], out_specs=[HBM] * n, input_output_aliases={i: i for i in range(n)},
        compiler_params=pltpu.CompilerParams(has_side_effects=pltpu.SideEffectType.DATAFLOW_SIDE_EFFECTING),
    )(*bufs, *sems, after)


def _row_half(ref, which, axis):
    half = ref.shape[axis] // 2
    rows = pl.ds(pl.multiple_of(which * half, 8), half)
    return ref.at[rows, :] if axis == 0 else ref.at[:, rows, :]


class _SplitGather:
    def __init__(self, own, after):
        self.n = len(own)
        self.state = _copies_start("gather_start", own, 3 * self.n, self._sent, after)

    @staticmethod
    def _sent(refs, x, y, c):
        return [(w.at[2 * x + y], w.at[2 * x + y], (px, py, c)) for w in refs for px, py in _other_chips(x, y)]

    @staticmethod
    def _landed(refs, x, y, c):
        return [(w.at[2 * x + y], w.at[2 * px + py], (px, py, c)) for w in refs for px, py in _other_chips(x, y)]

    def token(self):
        return self.state[2]

    def wait(self, which, name, after):
        sems, bufs, _ = self.state
        k = 3 * self.n
        mine = [sems[3 * i + j] for i in which for j in range(3)] + [sems[k + 3 * i + j] for i in which for j in range(3)]
        return _copies_wait(name, [bufs[i] for i in which], mine, after, self._landed)


class _SplitReduction:
    def __init__(self, tag, groups, place):
        self.tag, self.groups, self.place = tag, groups, place

    def start_pair(self, bufs):
        n = len(bufs)
        lands = [lax.empty((4, b.shape[1] // 2, b.shape[2]), F32) for b in bufs]
        plan = lambda refs, x, y, c: [(_row_half(refs[i], 1 - c, 1), refs[n + i], (x, y, 1 - c)) for i in range(n)]
        self._pair = (_copies_start("pair_%s_start" % self.tag, bufs + lands, n, plan), plan, n)
        return self._pair[0][2]

    def pair_done_start_scatter(self, after):
        (sems, bufs, _), plan, n = self._pair
        out = _copies_wait("pair_%s_wait" % self.tag, bufs, sems, after, plan)
        pairs = [_add_pair(out[i], out[n + i], self.place, "add_pair_" + g) for i, g in enumerate(self.groups)]
        self._pair_f32 = [p[0] for p in pairs]
        lands = [lax.empty((3,) + p[1].shape[1:], BF16) for p in pairs]
        plan = lambda refs, x, y, c: [(refs[i].at[2 * px + py], refs[n + i].at[j], (px, py, c))
                                      for i in range(n) for j, (px, py) in enumerate(_other_chips(x, y))]
        self._scatter = (_copies_start("scatter_%s_start" % self.tag, [p[1] for p in pairs] + lands, 3 * n, plan), plan, n)
        return self._scatter[0][2]

    def scatter_done(self, after):
        (sems, bufs, _), plan, n = self._scatter
        out = _copies_wait("scatter_%s_wait" % self.tag, bufs, sems, after, plan)
        return [_add_received(self._pair_f32[i], out[n + i], self.place, "add_received_" + g)
                for i, g in enumerate(self.groups)]

    def start_join(self, halves):
        n = len(halves)
        sent = lambda refs, x, y, c: [(_row_half(r, c, 0), _row_half(r, c, 0), (x, y, 1 - c)) for r in refs]
        landed = lambda refs, x, y, c: [(_row_half(r, c, 0), _row_half(r, 1 - c, 0), (x, y, 1 - c)) for r in refs]
        self._join = (_copies_start("join_%s_start" % self.tag, halves, n, sent), landed)
        return self._join[0][2]

    def join_done(self, after):
        (sems, bufs, _), landed = self._join
        return _copies_wait("join_%s_wait" % self.tag, bufs, sems, after, landed)


def _pair_sum_small(mine):
    rows, w = mine.shape

    def body(in_ref, out_ref, sibling, send_sem, recv_sem):
        x, y, c = _place()
        swap = pltpu.make_async_remote_copy(src_ref=in_ref, dst_ref=sibling, send_sem=send_sem, recv_sem=recv_sem,
                                            device_id=(x, y, 1 - c), device_id_type=MESH)
        swap.start()
        swap.wait()
        out_ref[...] = in_ref[...] + sibling[...]

    return pl.pallas_call(
        body, name="pair_sum_small", out_shape=jax.ShapeDtypeStruct((rows, w), F32),
        in_specs=[pl.BlockSpec(memory_space=pltpu.VMEM)], out_specs=pl.BlockSpec(memory_space=pltpu.VMEM),
        scratch_shapes=[pltpu.VMEM((rows, w), F32), pltpu.SemaphoreType.DMA, pltpu.SemaphoreType.DMA],
        compiler_params=pltpu.CompilerParams(vmem_limit_bytes=VMEM_LIMIT_V7X),
    )(mine)


class _SplitChipSum:
    def __init__(self, pair, place):
        self.place = place
        slots = lax.empty((4,) + pair.shape, F32)
        sent = lambda refs, x, y, c: [(refs[0], refs[1].at[2 * x + y], (px, py, c)) for px, py in _other_chips(x, y)]
        self.landed = lambda refs, x, y, c: [(refs[0], refs[1].at[2 * px + py], (px, py, c)) for px, py in _other_chips(x, y)]
        self.state = _copies_start("small_sum_start", [pair, slots], 3, sent)

    def token(self):
        return self.state[2]

    def done(self, after):
        sems, bufs, _ = self.state
        pair, slots = _copies_wait("small_sum_wait", bufs, sems, after, self.landed)
        rows, w = pair.shape

        def body(place_ref, pair_ref, slots_ref, out_ref):
            for j in range(4):
                own = place_ref[0] == j

                @pl.when(own)
                def _():
                    out_ref[...] = pair_ref[...] if j == 0 else out_ref[...] + pair_ref[...]

                @pl.when(jnp.logical_not(own))
                def _():
                    out_ref[...] = slots_ref[j] if j == 0 else out_ref[...] + slots_ref[j]

        grid_spec = pltpu.PrefetchScalarGridSpec(
            num_scalar_prefetch=1, grid=(1,),
            in_specs=[pl.BlockSpec((rows, w), lambda i, p: (0, 0)), pl.BlockSpec((4, rows, w), lambda i, p: (0, 0, 0))],
            out_specs=pl.BlockSpec((rows, w), lambda i, p: (0, 0)))
        return pl.pallas_call(
            body, name="small_sum_add", grid_spec=grid_spec, out_shape=jax.ShapeDtypeStruct((rows, w), F32),
            compiler_params=_cparams("arbitrary"),
        )(self.place, pair, slots)


def _join_column_shards(g):
    return jnp.transpose(g, (1, 0, 2)).reshape(g.shape[1], 4 * g.shape[2])


def _split_column_shards(w):
    r = w.shape[0]
    return jnp.transpose(w.reshape(r, 4, w.shape[1] // 4), (1, 0, 2))


def _small_rows(shape):
    return -(-int(np.prod(shape)) // 1024)


def _pack_small(vals):
    segs = []
    for name, shape in SMALL_WEIGHTS:
        flat = vals[name].reshape(-1)
        segs.append(jnp.pad(flat, (0, _small_rows(shape) * 1024 - flat.shape[0])))
    total = sum(s.shape[0] for s in segs) // 1024
    segs.append(jnp.zeros((-total % 8 * 1024,), F32))
    return jnp.concatenate(segs).reshape(-1, 1024)


def _unpack_small(packed):
    out, off = {}, 0
    for name, shape in SMALL_WEIGHTS:
        rows = _small_rows(shape)
        out[name] = packed[off:off + rows].reshape(-1)[:int(np.prod(shape))].reshape(shape)
        off += rows
    return out


W_IN_SHARD = D_IN // 4
W_IN_GAP = 1216


def _pad_w_in(g):
    cut = W_IN_GAP - W_IN_SHARD
    return jnp.concatenate([g[0], g[1][:, :cut], jnp.zeros((g.shape[1], D_IN_PAD - D_IN), g.dtype), g[1][:, cut:], g[2], g[3]],
                           axis=1)


def _unpad_w_in(g):
    skip = D_IN_PAD - D_IN
    second = jnp.concatenate([g[:, W_IN_SHARD:W_IN_GAP], g[:, W_IN_GAP + skip:2 * W_IN_SHARD + skip]], axis=1)
    return jnp.stack([g[:, :W_IN_SHARD], second, g[:, 2 * W_IN_SHARD + skip:3 * W_IN_SHARD + skip],
                      g[:, 3 * W_IN_SHARD + skip:]])


def _pad_heads(w):
    r = w.shape[0]
    return jnp.pad(w.reshape(r, N_HEADS, QK_HEAD), ((0, 0), (0, 0), (0, HEAD_PAD - QK_HEAD))).reshape(r, N_HEADS * HEAD_PAD)


def _unpad_heads(g):
    r = g.shape[0]
    return g.reshape(r, N_HEADS, HEAD_PAD)[:, :, :QK_HEAD].reshape(r, N_HEADS * QK_HEAD)


def _local_step(x, positions, tgt, grp_b, small, gather, red_a, red_rest):
    l = x.shape[0]
    t = min(l, 512)
    t_mlp = min(l, 256)
    tq = min(l, 1024)
    tc = min(l, 256)
    row = lambda v: v.reshape(1, -1).astype(F32)

    w_in_p = _pad_w_in(grp_b)
    g1, g2 = row(small["norm_mix"]), row(small["norm_mlp"])
    gqa, gkva = row(small["q_a_norm"]), row(small["kv_a_norm"])
    gq = jnp.pad(row(small["q_norm"]), ((0, 0), (0, HEAD_PAD - QK_HEAD)))
    gk = jnp.pad(row(small["k_norm"]), ((0, 0), (0, HEAD_PAD - QK_HEAD)))
    half = QK_ROPE // 2
    inv_freq = ROPE_THETA ** (-jnp.arange(half, dtype=F32) / half)
    invf = jnp.concatenate([inv_freq, inv_freq, jnp.zeros((64,), F32)]).reshape(1, 128)
    sgn = jnp.concatenate([-jnp.ones((half,), F32), jnp.ones((half,), F32), jnp.zeros((64,), F32)]).reshape(1, 128)
    pos = positions.reshape(l, 1)

    a_re, a_im = small["ssm_a_re"], small["ssm_a_im"]
    log_dt = small["ssm_log_dt"].reshape(SSM_GROUPS, 1)
    to_gcp = lambda b: jnp.transpose(b, (0, 2, 1)).reshape(SSM_WIDTH, SSM_STATE)
    from_gcp = lambda b: jnp.transpose(b.reshape(SSM_GROUPS, SSM_GROUP_CH, SSM_STATE), (0, 2, 1))
    b_re, b_im = to_gcp(small["ssm_b_re"]), to_gcp(small["ssm_b_im"])
    c_re, c_im = small["ssm_c_re"].reshape(SSM_WIDTH, SSM_STATE), small["ssm_c_im"].reshape(SSM_WIDTH, SSM_STATE)
    wb, wc, tabs_fwd, tabs_rev = _ssm_param_fwd(a_re, a_im, log_dt, b_re, b_im, c_re, c_im)
    dskip = row(small["ssm_d"])
    b_glu = row(small["b_glu"])

    u, lat, gs, gm = _in_proj_fwd(x, g1, w_in_p, t, gather.token())
    grp_c, grp_d, grp_e = gather.wait([0, 1, 2], "gather_cde_wait", u)
    w_qb_p = _pad_heads(_join_column_shards(grp_c))
    xr, xi, y, y_ssm = _ssm_fwd(u, wb, wc, tabs_fwd, dskip, grp_d, b_glu, grp_e, tc)
    q, k, v = _mla_pre_fwd(lat, pos, invf, sgn, gqa, gkva, gq, gk, w_qb_p, grp_d, t)
    attn, lse = _attn_fwd(q, k, v, tq)
    (grp_a,) = gather.wait([3], "gather_a_wait", attn)
    y_mla, mixed, h = _merge_fwd(attn, y_ssm, gs, gm, x, grp_a, t)
    dh, hn, da, hid, dout, loss_blk, g_norm_mlp = _mlp_fwd_bwd(h, tgt, g2, grp_a, t_mlp)

    ga = _wgrad_into(hn, da, "w_up", "col", _wgrad_into(hid, dout, "w_down", "row"))
    dys, dym, dgs, dgm, dattn = _merge_bwd(dh, y_ssm, y_mla, gs, gm, grp_a, t)
    ga = _wgrad_into(attn, dym, "w_o_mla", "row", _wgrad_into(mixed, dh, "w_out", "row", ga))

    dq, dk, dv = _attn_bwd(q, k, v, attn, dattn, lse, tq, red_a.start_pair([ga]))
    d_lat, ql, dq0, ckn, dkv, g_qa, g_kva, g_q, g_k = _mla_pre_bwd(lat, pos, invf, sgn, gqa, gkva, gq, gk, w_qb_p, grp_d,
                                                                    dq, dk, dv, t, red_a.pair_done_start_scatter(dk))
    gc = _split_column_shards(_unpad_heads(_wgrad(ql, dq0, "wgrad_q_b")))

    d_u, adj, dy, z, z2, dpre, g_b_glu, g_d, g_lr, g_li = _ssm_bwd(
        dys, y, u, xr, xi, wb, wc, tabs_rev, dskip, grp_d, b_glu, grp_e, tc)
    gd = _wgrad_into(z, dpre, "w_glu", "row", _wgrad_into(ckn, dkv, "w_kv_b", "col"))
    ge = _wgrad_into(z2, dys, "w_o_ssm", "col")
    grad_x, xn, dproj, g_norm_mix = _in_proj_bwd(x, g1, w_in_p, d_u, d_lat, dgs, dgm, dh, t)
    gb = _unpad_w_in(_wgrad(xn, dproj, "wgrad_in"))

    red_a.start_join(red_a.scatter_done(gb))
    g_wb = _wgrad_strips(u, adj, adj, "wgrad_ssm_b", 1, red_rest.start_pair([gb, gc, gd, ge]))
    g_wct = _wgrad_strips(dy, xr, xi, "wgrad_ssm_c", 0, red_rest.pair_done_start_scatter(g_wb))
    g_ar, g_ai, g_ldt, g_br, g_bi, g_cr, g_ci = _ssm_param_bwd(a_re, a_im, log_dt, b_re, b_im, g_lr, g_li, g_wb, g_wct)

    g_small = {
        "norm_mix": g_norm_mix.reshape(-1), "norm_mlp": g_norm_mlp.reshape(-1), "q_a_norm": g_qa.reshape(-1),
        "kv_a_norm": g_kva.reshape(-1), "q_norm": g_q.reshape(-1)[:QK_HEAD], "k_norm": g_k.reshape(-1)[:QK_HEAD],
        "ssm_a_re": g_ar, "ssm_a_im": g_ai, "ssm_log_dt": g_ldt.reshape(-1),
        "ssm_b_re": from_gcp(g_br), "ssm_b_im": from_gcp(g_bi),
        "ssm_c_re": g_cr.reshape(SSM_GROUPS, SSM_GROUP_CH, SSM_STATE), "ssm_c_im": g_ci.reshape(SSM_GROUPS, SSM_GROUP_CH, SSM_STATE),
        "ssm_d": g_d.reshape(SSM_GROUPS, SSM_GROUP_CH), "b_glu": g_b_glu.reshape(-1),
    }
    return loss_blk[0, 0], grad_x, g_small


def kernel(x, positions, norm_mix, w_in, q_a_norm, kv_a_norm, w_q_b, w_kv_b, q_norm, k_norm, w_o_mla, ssm_a_re, ssm_a_im, ssm_log_dt, ssm_b_re, ssm_b_im, ssm_c_re, ssm_c_im, ssm_d, w_glu, b_glu, w_o_ssm, w_out, norm_mlp, w_up, w_down, loss_target, m_norm_mix, m_w_in, m_q_a_norm, m_kv_a_norm, m_w_q_b, m_w_kv_b, m_q_norm, m_k_norm, m_w_o_mla, m_ssm_a_re, m_ssm_a_im, m_ssm_log_dt, m_ssm_b_re, m_ssm_b_im, m_ssm_c_re, m_ssm_c_im, m_ssm_d, m_w_glu, m_b_glu, m_w_o_ssm, m_w_out, m_norm_mlp, m_w_up, m_w_down, v_norm_mix, v_w_in, v_q_a_norm, v_kv_a_norm, v_w_q_b, v_w_kv_b, v_q_norm, v_k_norm, v_w_o_mla, v_ssm_a_re, v_ssm_a_im, v_ssm_log_dt, v_ssm_b_re, v_ssm_b_im, v_ssm_c_re, v_ssm_c_im, v_ssm_d, v_w_glu, v_b_glu, v_w_o_ssm, v_w_out, v_norm_mlp, v_w_up, v_w_down):
    args = dict(locals())
    w = {n: args[n][0] for n in WEIGHT_ORDER}
    m = {n: args["m_" + n][0] for n in WEIGHT_ORDER}
    v = {n: args["v_" + n][0] for n in WEIGHT_ORDER}
    big_names = [n for n, *_ in BIG_WEIGHTS]
    small_names = [n for n, _ in SMALL_WEIGHTS]

    place = jnp.stack([2 * lax.axis_index("x") + lax.axis_index("y"), lax.axis_index("c")]).astype(jnp.int32)
    rest = ["b", "c", "d", "e"]

    (grp_b,) = _gather_weights([_cast_shards(w, "b", place)])
    gather = _SplitGather([_cast_shards(w, g, place) for g in ("c", "d", "e", "a")], grp_b)
    red_a = _SplitReduction("a", ["a"], place)
    red_rest = _SplitReduction("rest", rest, place)
    small = {n: w[n] for n in small_names}

    loss_local, grad_x, g_small = _local_step(x[0], positions[0], loss_target[0], grp_b, small, gather, red_a, red_rest)
    loss = lax.psum(loss_local, ("x", "y", "c"))

    grad_w, delta_w, new_m, new_v = {}, {}, {}, {}

    def update(names, reduced, token):
        for n in names:
            g, off, _, _ = _place_in_group(n)
            grad_w[n], delta_w[n], new_m[n], new_v[n] = _adamw(w[n], reduced[g], m[n], v[n], "adamw_" + n, off, token)
            token = new_v[n]

    chip_sum = _SplitChipSum(_pair_sum_small(_pack_small(g_small)), place)
    in_a = [n for n, _ in GROUPS["a"][1]]
    update(in_a, {"a": red_a.join_done(chip_sum.token())[0]}, chip_sum.token())
    small_sum = chip_sum.done(new_v[in_a[-1]])
    g_s, d_s, m_s, v_s = _adamw(_pack_small(small), small_sum, _pack_small({n: m[n] for n in small_names}),
                                _pack_small({n: v[n] for n in small_names}), "adamw_small", 0, small_sum)
    g_s, d_s, m_s, v_s = _unpack_small(g_s), _unpack_small(d_s), _unpack_small(m_s), _unpack_small(v_s)
    for n in small_names:
        grad_w[n], delta_w[n], new_m[n], new_v[n] = g_s[n], d_s[n], m_s[n], v_s[n]
    halves = red_rest.scatter_done(v_s[small_names[0]])
    reduced_rest = dict(zip(rest, _swap_reduced_halves(halves)))
    update([n for n in big_names if n not in in_a], reduced_rest, halves[0])

    lead = lambda d: [d[n][None] for n in WEIGHT_ORDER]
    return (loss, grad_x[None], *lead(grad_w), *lead(delta_w), *lead(new_m), *lead(new_v))
```

```python
import math

import jax
import jax.numpy as jnp
import numpy as np
from jax import lax
from jax.experimental import pallas as pl
from jax.experimental.pallas import tpu as pltpu

F32 = jnp.float32
BF16 = jnp.bfloat16

D_MODEL = 1024
SSM_GROUPS = 32
SSM_GROUP_CH = 16
SSM_WIDTH = 512
SSM_STATE = 64
GP = SSM_GROUPS * SSM_STATE
N_HEADS = 8
QK_NOPE = 128
QK_ROPE = 64
QK_HEAD = 192
HEAD_PAD = 256
V_HEAD = 128
Q_LORA = 384
KV_LORA = 256
LAT_W = 768
D_IN = 3264
D_IN_PAD = 3328
D_FF = 4096
ROPE_THETA = 10000.0
EPS = 1e-6
ATT_SCALE = QK_HEAD ** -0.5

ADAM_LR = 0.001
ADAM_B1 = 0.9
ADAM_B2 = 0.999
ADAM_EPS = 1e-08
ADAM_WD = 0.01
ADAM_STEP = 10

VMEM_LIMIT_V7X = 56 * 1024 * 1024
MESH = pl.DeviceIdType.MESH

BIG_WEIGHTS = (
    ("w_in", 1024, 3264, "col"),
    ("w_q_b", 384, 1536, "col"),
    ("w_kv_b", 256, 2048, "col"),
    ("w_o_mla", 1024, 1024, "row"),
    ("w_glu", 512, 512, "row"),
    ("w_o_ssm", 512, 1024, "col"),
    ("w_out", 1024, 1024, "row"),
    ("w_up", 1024, 4096, "col"),
    ("w_down", 4096, 1024, "row"),
)
GROUPS = {
    "a": (1024, (("w_down", 1024), ("w_up", 1024), ("w_o_mla", 256), ("w_out", 256))),
    "b": (816, (("w_in", 1024),)),
    "c": (384, (("w_q_b", 384),)),
    "d": (512, (("w_kv_b", 256), ("w_glu", 128))),
    "e": (256, (("w_o_ssm", 512),)),
}


def _group_rows(group):
    return sum(r for _, r in GROUPS[group][1])


def _place_in_group(name):
    for group, (width, members) in GROUPS.items():
        off = 0
        for member, rows in members:
            if member == name:
                return group, off, rows, width
            off += rows
    raise KeyError(name)


SMALL_WEIGHTS = (
    ("norm_mix", (1024,)), ("q_a_norm", (384,)), ("kv_a_norm", (256,)), ("q_norm", (192,)), ("k_norm", (192,)),
    ("ssm_a_re", (32, 64)), ("ssm_a_im", (32, 64)), ("ssm_log_dt", (32,)),
    ("ssm_b_re", (32, 64, 16)), ("ssm_b_im", (32, 64, 16)), ("ssm_c_re", (32, 16, 64)), ("ssm_c_im", (32, 16, 64)),
    ("ssm_d", (32, 16)), ("b_glu", (512,)), ("norm_mlp", (1024,)),
)
WEIGHT_ORDER = ('norm_mix', 'w_in', 'q_a_norm', 'kv_a_norm', 'w_q_b', 'w_kv_b', 'q_norm', 'k_norm', 'w_o_mla', 'ssm_a_re',
                'ssm_a_im', 'ssm_log_dt', 'ssm_b_re', 'ssm_b_im', 'ssm_c_re', 'ssm_c_im', 'ssm_d', 'w_glu', 'b_glu',
                'w_o_ssm', 'w_out', 'norm_mlp', 'w_up', 'w_down')


def _cparams(*sem):
    return pltpu.CompilerParams(dimension_semantics=sem if sem else None, vmem_limit_bytes=VMEM_LIMIT_V7X)


def _resident(shape, index=None):
    index = (0,) * len(shape) if index is None else index
    return pl.BlockSpec(shape, lambda *_: index, pipeline_mode=pl.Buffered(1))


def _member_block(name):
    _, off, rows, width = _place_in_group(name)
    return _resident((4, rows, width), (0, off // rows, 0))


def _rows(t, width):
    return pl.BlockSpec((t, width), lambda i: (i, 0))


def _mm(a, b):
    return jnp.dot(a.astype(BF16), b.astype(BF16), preferred_element_type=F32)


def _mm_nt(a, b):
    return lax.dot_general(a.astype(BF16), b.astype(BF16), (((1,), (1,)), ((), ())), preferred_element_type=F32)


def _mm_tn(a, b):
    return lax.dot_general(a.astype(BF16), b.astype(BF16), (((0,), (0,)), ((), ())), preferred_element_type=F32)


def _rms_fwd(x, g, n):
    r = lax.rsqrt(jnp.sum(x * x, axis=-1, keepdims=True) * (1.0 / n) + EPS)
    return x * r * g


def _rms_bwd(x, g, dy, n):
    r = lax.rsqrt(jnp.sum(x * x, axis=-1, keepdims=True) * (1.0 / n) + EPS)
    xh = x * r
    dxh = dy * g
    dx = r * (dxh - xh * (jnp.sum(dxh * xh, axis=-1, keepdims=True) * (1.0 / n)))
    return dx, dy * xh


def _colsum(a):
    return jnp.sum(a, axis=0, keepdims=True)


def _accumulate(ref, value, first):
    @pl.when(first)
    def _():
        ref[...] = value

    @pl.when(jnp.logical_not(first))
    def _():
        ref[...] += value


def _sigmoid(a):
    return 1.0 / (1.0 + jnp.exp(-a))


GELU_C = math.sqrt(2.0 / math.pi)
GELU_A = 0.044715


def _gelu(y):
    return 0.5 * y * (1.0 + jnp.tanh(GELU_C * (y + GELU_A * y * y * y)))


def _gelu_grad(y):
    t = jnp.tanh(GELU_C * (y + GELU_A * y * y * y))
    return 0.5 * (1.0 + t) + 0.5 * y * (1.0 - t * t) * GELU_C * (1.0 + 3.0 * GELU_A * y * y)


def _in_proj_fwd(x, g1, w_in_p, t, token):
    l = x.shape[0]

    def body(x_ref, g_ref, w_ref, token_ref, u_ref, lat_ref, gs_ref, gm_ref):
        xn = _rms_fwd(x_ref[...], g_ref[...], D_MODEL).astype(BF16)
        u_ref[...] = _mm(xn, w_ref[:, 0:512])
        lat_ref[...] = _mm(xn, w_ref[:, 512:1280])
        gs_ref[...] = _mm(xn, w_ref[:, 1280:2304]).astype(BF16)
        gm_ref[...] = _mm(xn, w_ref[:, 2304:3328]).astype(BF16)

    return pl.pallas_call(
        body, name="in_proj_fwd", grid=(l // t,),
        in_specs=[_rows(t, D_MODEL), _resident((1, D_MODEL)), _resident((D_MODEL, D_IN_PAD)), ANY],
        out_specs=[_rows(t, 512), _rows(t, LAT_W), _rows(t, D_MODEL), _rows(t, D_MODEL)],
        out_shape=[jax.ShapeDtypeStruct((l, 512), F32), jax.ShapeDtypeStruct((l, LAT_W), F32),
                   jax.ShapeDtypeStruct((l, D_MODEL), BF16), jax.ShapeDtypeStruct((l, D_MODEL), BF16)],
        compiler_params=_cparams("parallel"),
    )(x, g1, w_in_p, token)


def _in_proj_bwd(x, g1, w_in_p, d_u, d_lat, d_gs, d_gm, dh, t):
    l = x.shape[0]

    def body(x_ref, g_ref, w_ref, du_ref, dlat_ref, dgs_ref, dgm_ref, dh_ref, gx_ref, xn_ref, dproj_ref, dg_ref):
        xv = x_ref[...]
        g = g_ref[...]
        xn_ref[...] = _rms_fwd(xv, g, D_MODEL).astype(BF16)
        dproj_ref[:, 0:512] = du_ref[...]
        dproj_ref[:, 512:1280] = dlat_ref[...]
        dproj_ref[:, 1280:2304] = dgs_ref[...]
        dproj_ref[:, 2304:3328] = dgm_ref[...]
        dxn = _mm_nt(dproj_ref[...], w_ref[...])
        dx, dg_rows = _rms_bwd(xv, g, dxn, D_MODEL)
        gx_ref[...] = dh_ref[...] + dx
        _accumulate(dg_ref, _colsum(dg_rows), pl.program_id(0) == 0)

    return pl.pallas_call(
        body, name="in_proj_bwd", grid=(l // t,),
        in_specs=[_rows(t, D_MODEL), _resident((1, D_MODEL)), _resident((D_MODEL, D_IN_PAD)), _rows(t, 512),
                  _rows(t, LAT_W), _rows(t, D_MODEL), _rows(t, D_MODEL), _rows(t, D_MODEL)],
        out_specs=[_rows(t, D_MODEL), _rows(t, D_MODEL), _rows(t, D_IN_PAD), pl.BlockSpec((1, D_MODEL), lambda i: (0, 0))],
        out_shape=[jax.ShapeDtypeStruct((l, D_MODEL), F32), jax.ShapeDtypeStruct((l, D_MODEL), BF16),
                   jax.ShapeDtypeStruct((l, D_IN_PAD), BF16), jax.ShapeDtypeStruct((1, D_MODEL), F32)],
        compiler_params=_cparams("arbitrary"),
    )(x, g1, w_in_p, d_u, d_lat, d_gs, d_gm, dh)


def _ssm_param_fn(a_re, a_im, log_dt, b_re, b_im):
    dt = jnp.exp(log_dt)
    er = jnp.exp(a_re * dt)
    lr = er * jnp.cos(a_im * dt)
    li = er * jnp.sin(a_im * dt)
    den = a_re * a_re + a_im * a_im
    nr = lr - 1.0
    kr = (nr * a_re + li * a_im) / den
    ki = (li * a_re - nr * a_im) / den
    rows = lambda k: jnp.broadcast_to(k[:, None, :], (SSM_GROUPS, SSM_GROUP_CH, SSM_STATE)).reshape(SSM_WIDTH, SSM_STATE)
    krt, kit = rows(kr), rows(ki)
    return lr, li, krt * b_re - kit * b_im, krt * b_im + kit * b_re


def _state_selector():
    row = lax.broadcasted_iota(jnp.int32, (SSM_STATE, GP), 0)
    col = lax.broadcasted_iota(jnp.int32, (SSM_STATE, GP), 1)
    return jnp.where(jnp.bitwise_and(col, SSM_STATE - 1) == row, 1.0, 0.0).astype(BF16)


def _own_group(rows, rows_per_group_log2):
    row = lax.broadcasted_iota(jnp.int32, (rows, GP), 0)
    col = lax.broadcasted_iota(jnp.int32, (rows, GP), 1)
    return jnp.right_shift(row, rows_per_group_log2) == jnp.right_shift(col, 6)


def _three_bf16(x):
    hi = x.astype(BF16)
    rest = x - hi.astype(F32)
    mid = rest.astype(BF16)
    return hi, mid, (rest - mid.astype(F32)).astype(BF16)


def _spread(x, sel):
    return sum(jnp.dot(part, sel, preferred_element_type=F32) for part in _three_bf16(x))


def _collect(xw, sel):
    return sum(lax.dot_general(part, sel, (((1,), (1,)), ((), ())), preferred_element_type=F32) for part in _three_bf16(xw))


def _ssm_param_fwd(a_re, a_im, log_dt, b_re, b_im, c_re, c_im):
    def body(ar_ref, ai_ref, ldt_ref, br_ref, bi_ref, cr_ref, ci_ref, wb_ref, wct_ref, tf_ref, tr_ref):
        lr, li, bbr, bbi = _ssm_param_fn(ar_ref[...], ai_ref[...], ldt_ref[...], br_ref[...], bi_ref[...])
        sel = _state_selector()
        own16 = _own_group(SSM_WIDTH, 4)
        own1 = _own_group(SSM_GROUPS, 0)
        block = lambda m: jnp.where(own16, jnp.dot(m.astype(BF16), sel, preferred_element_type=F32), 0.0).astype(BF16)
        wb_ref[:, 0:GP] = block(bbr)
        wb_ref[:, GP:2 * GP] = block(bbi)
        wct_ref[:, 0:GP] = block(cr_ref[...])
        wct_ref[:, GP:2 * GP] = block(-ci_ref[...])
        flat = lambda m: _colsum(jnp.where(own1, _spread(m, sel), 0.0))
        pr, pi = [], []
        qr, qi = lr, li
        for _ in range(8):
            pr.append(flat(qr))
            pi.append(flat(qi))
            qr, qi = qr * lr - qi * li, qr * li + qi * lr
        row = lax.broadcasted_iota(jnp.int32, (8, GP), 0)
        for n, k in enumerate((1, 2, 4)):
            tf_ref[2 * n] = jnp.where(row >= k, pr[k - 1], 0.0)
            tf_ref[2 * n + 1] = jnp.where(row >= k, pi[k - 1], 0.0)
            tr_ref[2 * n] = jnp.where(row < 8 - k, pr[k - 1], 0.0)
            tr_ref[2 * n + 1] = jnp.where(row < 8 - k, -pi[k - 1], 0.0)
        pick = lambda vals: sum(jnp.where(row == j, v, 0.0) for j, v in enumerate(vals))
        tf_ref[6] = pick(pr)
        tf_ref[7] = pick(pi)
        tr_ref[6] = pick(pr[::-1])
        tr_ref[7] = pick([-v for v in pi[::-1]])

    return pl.pallas_call(
        body, name="ssm_param_fwd",
        out_shape=[jax.ShapeDtypeStruct((SSM_WIDTH, 2 * GP), BF16), jax.ShapeDtypeStruct((SSM_WIDTH, 2 * GP), BF16),
                   jax.ShapeDtypeStruct((8, 8, GP), F32), jax.ShapeDtypeStruct((8, 8, GP), F32)],
        compiler_params=_cparams(),
    )(a_re, a_im, log_dt, b_re, b_im, c_re, c_im)


STRIP_CH = 128
STRIP_ST = 512
N_STRIPS = SSM_WIDTH // STRIP_CH


def _ssm_param_bwd(a_re, a_im, log_dt, b_re, b_im, g_lr, g_li, g_wb, g_wct):
    def body(ar_ref, ai_ref, ldt_ref, br_ref, bi_ref, glr_ref, gli_ref, gwb_ref, gwc_ref,
             o_ar, o_ai, o_ldt, o_br, o_bi, o_cr, o_ci):
        sel = _state_selector()
        own1 = _own_group(SSM_GROUPS, 0)
        row = lax.broadcasted_iota(jnp.int32, (SSM_WIDTH, STRIP_ST), 0)
        col = lax.broadcasted_iota(jnp.int32, (SSM_WIDTH, STRIP_ST), 1)
        own = jnp.bitwise_and(jnp.right_shift(row, 4), 7) == jnp.right_shift(col, 6)
        blocks = lambda m: _collect(jnp.where(own, m, 0.0), sel[:, 0:STRIP_ST])
        unflat = lambda v: _collect(jnp.where(own1, v, 0.0), sel)
        _, vjp = jax.vjp(_ssm_param_fn, ar_ref[...], ai_ref[...], ldt_ref[...], br_ref[...], bi_ref[...])
        d_ar, d_ai, d_ldt, d_br, d_bi = vjp((unflat(glr_ref[...]), unflat(gli_ref[...]),
                                             blocks(gwb_ref[:, 0:STRIP_ST]), blocks(gwb_ref[:, STRIP_ST:2 * STRIP_ST])))
        o_ar[...] = d_ar
        o_ai[...] = d_ai
        o_ldt[...] = d_ldt
        o_br[...] = d_br
        o_bi[...] = d_bi
        o_cr[...] = blocks(gwc_ref[:, 0:STRIP_ST])
        o_ci[...] = -blocks(gwc_ref[:, STRIP_ST:2 * STRIP_ST])

    g, p = SSM_GROUPS, SSM_STATE
    gp = jax.ShapeDtypeStruct((g, p), F32)
    gcp = jax.ShapeDtypeStruct((SSM_WIDTH, p), F32)
    return pl.pallas_call(
        body, name="ssm_param_bwd", out_shape=[gp, gp, jax.ShapeDtypeStruct((g, 1), F32), gcp, gcp, gcp, gcp],
        compiler_params=_cparams(),
    )(a_re, a_im, log_dt, b_re, b_im, g_lr, g_li, g_wb, g_wct)


def _strip(ref, j, im):
    return ref[STRIP_CH * j:STRIP_CH * (j + 1), im * GP + STRIP_ST * j:im * GP + STRIP_ST * (j + 1)]


def _wgrad_strips(a, b_re, b_im, name, im_block, token):
    l = a.shape[0]
    bl = min(l, 512)

    def body(a_ref, bre_ref, bim_ref, token_ref, o_ref):
        first = pl.program_id(0) == 0
        for j in range(N_STRIPS):
            aj = a_ref[:, STRIP_CH * j:STRIP_CH * (j + 1)]
            states = slice(STRIP_ST * j, STRIP_ST * (j + 1))
            _accumulate(o_ref.at[STRIP_CH * j:STRIP_CH * (j + 1), 0:STRIP_ST], _mm_tn(aj, bre_ref[:, states]), first)
            _accumulate(o_ref.at[STRIP_CH * j:STRIP_CH * (j + 1), STRIP_ST:2 * STRIP_ST], _mm_tn(aj, bim_ref[:, states]), first)

    return pl.pallas_call(
        body, name=name, grid=(l // bl,),
        in_specs=[pl.BlockSpec((bl, SSM_WIDTH), lambda k: (k, 0)), pl.BlockSpec((bl, GP), lambda k: (k, 0)),
                  pl.BlockSpec((bl, GP), lambda k: (k, im_block)), ANY],
        out_specs=pl.BlockSpec((SSM_WIDTH, 2 * STRIP_ST), lambda k: (0, 0)),
        out_shape=jax.ShapeDtypeStruct((SSM_WIDTH, 2 * STRIP_ST), F32),
        compiler_params=_cparams("arbitrary"),
    )(a, b_re, b_im, token)


SCAN_STRIP = 512


def _scan_chunk(inr_ref, ini_ref, outr_ref, outi_ref, cr_ref, ci_ref, tab_ref, tc, reverse):
    n_blocks = tc // 8

    def block(j, _):
        i = (n_blocks - 1 - j) if reverse else j
        rows = pl.ds(pl.multiple_of(i * 8, 8), 8)
        for s in range(GP // SCAN_STRIP):
            sl = pl.ds(s * SCAN_STRIP, SCAN_STRIP)
            xr = inr_ref[rows, sl]
            xi = ini_ref[rows, sl]
            for n, k in enumerate((1, 2, 4)):
                shift = (8 - k) if reverse else k
                sr = pltpu.roll(xr, shift, 0)
                si = pltpu.roll(xi, shift, 0)
                mr = tab_ref[2 * n, :, sl]
                mi = tab_ref[2 * n + 1, :, sl]
                xr, xi = xr + mr * sr - mi * si, xi + mr * si + mi * sr
            qr = tab_ref[6, :, sl]
            qi = tab_ref[7, :, sl]
            cr = cr_ref[:, sl]
            ci = ci_ref[:, sl]
            xr, xi = xr + qr * cr - qi * ci, xi + qr * ci + qi * cr
            outr_ref[rows, sl] = xr
            outi_ref[rows, sl] = xi
            edge = 0 if reverse else 7
            cr_ref[:, sl] = jnp.broadcast_to(xr[edge:edge + 1, :], (8, SCAN_STRIP))
            ci_ref[:, sl] = jnp.broadcast_to(xi[edge:edge + 1, :], (8, SCAN_STRIP))
        return 0

    lax.fori_loop(0, n_blocks, block, 0)


def _glu_pre(z, wg_ref):
    return sum(_mm(z[:, 128 * j:128 * (j + 1)], wg_ref[j]) for j in range(4))


def _ssm_fwd(u, wb, wc, tabs, dskip, grp_d, b_glu, grp_e, tc):
    l = u.shape[0]

    def body(u_ref, wb_ref, wc_ref, tab_ref, d_ref, wg_ref, bg_ref, wo_ref, xr_ref, xi_ref, y_ref, ys_ref,
             bur, bui, cr, ci):
        @pl.when(pl.program_id(0) == 0)
        def _():
            cr[...] = jnp.zeros_like(cr)
            ci[...] = jnp.zeros_like(ci)

        uv = u_ref[...]
        ub = uv.astype(BF16)
        for j in range(N_STRIPS):
            uj = ub[:, STRIP_CH * j:STRIP_CH * (j + 1)]
            states = slice(STRIP_ST * j, STRIP_ST * (j + 1))
            bur[:, states] = _mm(uj, _strip(wb_ref, j, 0))
            bui[:, states] = _mm(uj, _strip(wb_ref, j, 1))
        _scan_chunk(bur, bui, xr_ref, xi_ref, cr, ci, tab_ref, tc, False)
        y = jnp.concatenate(
            [_mm_nt(xr_ref[:, STRIP_ST * j:STRIP_ST * (j + 1)], _strip(wc_ref, j, 0))
             + _mm_nt(xi_ref[:, STRIP_ST * j:STRIP_ST * (j + 1)], _strip(wc_ref, j, 1)) for j in range(N_STRIPS)],
            axis=-1) + d_ref[...] * uv
        y_ref[...] = y
        z = _gelu(y)
        z2 = z * _sigmoid(_glu_pre(z, wg_ref) + bg_ref[...])
        for s in range(4):
            ys_ref[:, 256 * s:256 * (s + 1)] = _mm(z2, wo_ref[s])

    return pl.pallas_call(
        body, name="ssm_fwd", grid=(l // tc,),
        in_specs=[_rows(tc, 512), _resident((512, 2 * GP)), _resident((512, 2 * GP)), _resident((8, 8, GP)),
                  _resident((1, 512)), _member_block("w_glu"), _resident((1, 512)), _member_block("w_o_ssm")],
        out_specs=[_rows(tc, GP), _rows(tc, GP), _rows(tc, 512), _rows(tc, D_MODEL)],
        out_shape=[jax.ShapeDtypeStruct((l, GP), F32), jax.ShapeDtypeStruct((l, GP), F32),
                   jax.ShapeDtypeStruct((l, 512), F32), jax.ShapeDtypeStruct((l, D_MODEL), F32)],
        scratch_shapes=[pltpu.VMEM((tc, GP), F32), pltpu.VMEM((tc, GP), F32), pltpu.VMEM((8, GP), F32),
                        pltpu.VMEM((8, GP), F32)],
        compiler_params=_cparams("arbitrary"),
    )(u, wb, wc, tabs, dskip, grp_d, b_glu, grp_e)


def _ssm_bwd(dys, y, u, xr, xi, wb, wc, tabs_rev, dskip, grp_d, b_glu, grp_e, tc):
    l = u.shape[0]
    nc = l // tc

    def body(dys_ref, y_ref, u_ref, xr_ref, xi_ref, wb_ref, wc_ref, tab_ref, d_ref, wg_ref, bg_ref, wo_ref,
             du_ref, a_ref, dy_ref, z_ref, z2_ref, dpre_ref, gb_ref, gd_ref, glr_ref, gli_ref,
             dxr, dxi, ar, ai, cr, ci):
        first = pl.program_id(0) == 0

        @pl.when(first)
        def _():
            cr[...] = jnp.zeros_like(cr)
            ci[...] = jnp.zeros_like(ci)

        yv = y_ref[...]
        uv = u_ref[...]
        dz2 = sum(_mm_nt(dys_ref[:, 256 * j:256 * (j + 1)], wo_ref[j]) for j in range(4))
        z = _gelu(yv)
        s = _sigmoid(_glu_pre(z, wg_ref) + bg_ref[...])
        dpre = dz2 * z * s * (1.0 - s)
        dpreb = dpre.astype(BF16)
        dz = dz2 * s + jnp.concatenate([_mm_nt(dpreb, wg_ref[j]) for j in range(4)], axis=-1)
        dy = dz * _gelu_grad(yv)
        z_ref[...] = z.astype(BF16)
        z2_ref[...] = (z * s).astype(BF16)
        dpre_ref[...] = dpre.astype(BF16)
        dy_ref[...] = dy.astype(BF16)
        _accumulate(gb_ref, _colsum(dpre), first)
        _accumulate(gd_ref, _colsum(dy * uv), first)

        dyb = dy.astype(BF16)
        for j in range(N_STRIPS):
            dyj = dyb[:, STRIP_CH * j:STRIP_CH * (j + 1)]
            dxr[:, STRIP_ST * j:STRIP_ST * (j + 1)] = _mm(dyj, _strip(wc_ref, j, 0))
            dxi[:, STRIP_ST * j:STRIP_ST * (j + 1)] = _mm(dyj, _strip(wc_ref, j, 1))
        ar[pl.ds(tc, 8), :] = cr[...]
        ai[pl.ds(tc, 8), :] = ci[...]
        _scan_chunk(dxr, dxi, ar, ai, cr, ci, tab_ref, tc, True)
        a_ref[:, 0:GP] = ar[pl.ds(0, tc), :].astype(BF16)
        a_ref[:, GP:2 * GP] = ai[pl.ds(0, tc), :].astype(BF16)
        du_states = jnp.concatenate(
            [_mm_nt(a_ref[:, STRIP_ST * j:STRIP_ST * (j + 1)], _strip(wb_ref, j, 0))
             + _mm_nt(a_ref[:, GP + STRIP_ST * j:GP + STRIP_ST * (j + 1)], _strip(wb_ref, j, 1)) for j in range(N_STRIPS)],
            axis=-1)
        du_ref[...] = (dy * d_ref[...] + du_states).astype(BF16)
        anr = ar[pl.ds(1, tc), :]
        ani = ai[pl.ds(1, tc), :]
        xrv = xr_ref[...]
        xiv = xi_ref[...]
        _accumulate(glr_ref, _colsum(anr * xrv + ani * xiv), first)
        _accumulate(gli_ref, _colsum(ani * xrv - anr * xiv), first)

    rev = lambda w: pl.BlockSpec((tc, w), lambda i: (nc - 1 - i, 0))
    acc = lambda w: pl.BlockSpec((1, w), lambda i: (0, 0))
    bf = jax.ShapeDtypeStruct((l, 512), BF16)
    return pl.pallas_call(
        body, name="ssm_bwd", grid=(nc,),
        in_specs=[rev(D_MODEL), rev(512), rev(512), rev(GP), rev(GP), _resident((512, 2 * GP)), _resident((512, 2 * GP)),
                  _resident((8, 8, GP)), _resident((1, 512)), _member_block("w_glu"), _resident((1, 512)),
                  _member_block("w_o_ssm")],
        out_specs=[rev(512), rev(2 * GP), rev(512), rev(512), rev(512), rev(512), acc(512), acc(512), acc(GP), acc(GP)],
        out_shape=[bf, jax.ShapeDtypeStruct((l, 2 * GP), BF16), bf, bf, bf, bf,
                   jax.ShapeDtypeStruct((1, 512), F32), jax.ShapeDtypeStruct((1, 512), F32),
                   jax.ShapeDtypeStruct((1, GP), F32), jax.ShapeDtypeStruct((1, GP), F32)],
        scratch_shapes=[pltpu.VMEM((tc, GP), F32), pltpu.VMEM((tc, GP), F32), pltpu.VMEM((tc + 8, GP), F32),
                        pltpu.VMEM((tc + 8, GP), F32), pltpu.VMEM((8, GP), F32), pltpu.VMEM((8, GP), F32)],
        compiler_params=_cparams("arbitrary"),
    )(dys, y, u, xr, xi, wb, wc, tabs_rev, dskip, grp_d, b_glu, grp_e)


def _swap_halves(b):
    lane = lax.broadcasted_iota(jnp.int32, b.shape, 1)
    return jnp.where(lane < 32, pltpu.roll(b, 96, 1), pltpu.roll(b, 32, 1))


def _rope_tables(pos_ref, invf_ref, sgn_ref):
    ang = pos_ref[...].astype(F32) * invf_ref[...]
    return jnp.cos(ang), jnp.sin(ang) * sgn_ref[...]


def _mla_pre_fwd(lat, pos, invf, sgn, gqa, gkva, gq, gk, w_qb_p, w_kvb, t):
    l = lat.shape[0]

    def body(lat_ref, pos_ref, invf_ref, sgn_ref, gqa_ref, gkva_ref, gq_ref, gk_ref, wq_ref, wkv_ref, q_ref, k_ref, v_ref):
        cs, sn = _rope_tables(pos_ref, invf_ref, sgn_ref)
        ql = _rms_fwd(lat_ref[:, 0:Q_LORA], gqa_ref[...], Q_LORA)
        ckn = _rms_fwd(lat_ref[:, Q_LORA:Q_LORA + KV_LORA], gkva_ref[...], KV_LORA)
        kpe = lat_ref[:, 640:768]
        q0 = _mm(ql, wq_ref[...])
        cknb = ckn.astype(BF16)
        kv = jnp.concatenate([_mm(cknb, wkv_ref[s]) for s in range(4)], axis=-1)
        for h in range(N_HEADS):
            q1 = _rms_fwd(q0[:, HEAD_PAD * h:HEAD_PAD * (h + 1)], gq_ref[...], QK_HEAD)
            b = q1[:, 128:256]
            q_ref[h, :, 0:128] = (q1[:, 0:128] * ATT_SCALE).astype(BF16)
            q_ref[h, :, 128:256] = ((b * cs + _swap_halves(b) * sn) * ATT_SCALE).astype(BF16)
            k0 = jnp.concatenate([kv[:, 256 * h:256 * h + 128], kpe], axis=-1)
            k1 = _rms_fwd(k0, gk_ref[...], QK_HEAD)
            b = k1[:, 128:256]
            k_ref[h, :, 0:128] = k1[:, 0:128].astype(BF16)
            k_ref[h, :, 128:256] = (b * cs + _swap_halves(b) * sn).astype(BF16)
            v_ref[h] = kv[:, 256 * h + 128:256 * h + 256].astype(BF16)

    heads = lambda w: pl.BlockSpec((N_HEADS, t, w), lambda i: (0, i, 0))
    return pl.pallas_call(
        body, name="mla_pre_fwd", grid=(l // t,),
        in_specs=[_rows(t, LAT_W), _rows(t, 1), _resident((1, 128)), _resident((1, 128)), _resident((1, Q_LORA)),
                  _resident((1, KV_LORA)), _resident((1, HEAD_PAD)), _resident((1, HEAD_PAD)),
                  _resident((Q_LORA, N_HEADS * HEAD_PAD)), _member_block("w_kv_b")],
        out_specs=[heads(HEAD_PAD), heads(HEAD_PAD), heads(V_HEAD)],
        out_shape=[jax.ShapeDtypeStruct((N_HEADS, l, HEAD_PAD), BF16), jax.ShapeDtypeStruct((N_HEADS, l, HEAD_PAD), BF16),
                   jax.ShapeDtypeStruct((N_HEADS, l, V_HEAD), BF16)],
        compiler_params=_cparams("parallel"),
    )(lat, pos, invf, sgn, gqa, gkva, gq, gk, w_qb_p, w_kvb)


def _mla_pre_bwd(lat, pos, invf, sgn, gqa, gkva, gq, gk, w_qb_p, w_kvb, dq, dk, dv, t, token):
    l = lat.shape[0]

    def body(lat_ref, pos_ref, invf_ref, sgn_ref, gqa_ref, gkva_ref, gq_ref, gk_ref, wq_ref, wkv_ref, dq_ref, dk_ref, dv_ref,
             token_ref, dlat_ref, ql_ref, dq0_ref, ckn_ref, dkv_ref, ggqa_ref, ggkva_ref, ggq_ref, ggk_ref):
        first = pl.program_id(0) == 0
        cs, sn = _rope_tables(pos_ref, invf_ref, sgn_ref)
        q_lat = lat_ref[:, 0:Q_LORA]
        c_kv = lat_ref[:, Q_LORA:Q_LORA + KV_LORA]
        kpe = lat_ref[:, 640:768]
        ql = _rms_fwd(q_lat, gqa_ref[...], Q_LORA)
        ckn = _rms_fwd(c_kv, gkva_ref[...], KV_LORA)
        ql_ref[...] = ql.astype(BF16)
        ckn_ref[...] = ckn.astype(BF16)
        q0 = _mm(ql, wq_ref[...])
        cknb = ckn.astype(BF16)
        kv = jnp.concatenate([_mm(cknb, wkv_ref[s]) for s in range(4)], axis=-1)
        dkpe = jnp.zeros_like(kpe)
        ggq = jnp.zeros((1, HEAD_PAD), F32)
        ggk = jnp.zeros((1, HEAD_PAD), F32)

        def unrope(d):
            b = d[:, 128:256]
            return jnp.concatenate([d[:, 0:128], b * cs + _swap_halves(b * sn)], axis=-1)

        for h in range(N_HEADS):
            dq1 = unrope(dq_ref[h] * ATT_SCALE)
            dq0h, gq_rows = _rms_bwd(q0[:, HEAD_PAD * h:HEAD_PAD * (h + 1)], gq_ref[...], dq1, QK_HEAD)
            ggq = ggq + _colsum(gq_rows)
            dq0_ref[:, HEAD_PAD * h:HEAD_PAD * (h + 1)] = dq0h.astype(BF16)
            k0 = jnp.concatenate([kv[:, 256 * h:256 * h + 128], kpe], axis=-1)
            dk0, gk_rows = _rms_bwd(k0, gk_ref[...], unrope(dk_ref[h]), QK_HEAD)
            ggk = ggk + _colsum(gk_rows)
            dkpe = dkpe + dk0[:, 128:256]
            dkv_ref[:, 256 * h:256 * h + 128] = dk0[:, 0:128].astype(BF16)
            dkv_ref[:, 256 * h + 128:256 * h + 256] = dv_ref[h].astype(BF16)
        dql = _mm_nt(dq0_ref[...], wq_ref[...])
        dckn = sum(_mm_nt(dkv_ref[:, 512 * s:512 * (s + 1)], wkv_ref[s]) for s in range(4))
        dq_lat, gqa_rows = _rms_bwd(q_lat, gqa_ref[...], dql, Q_LORA)
        dc_kv, gkva_rows = _rms_bwd(c_kv, gkva_ref[...], dckn, KV_LORA)
        dlat_ref[:, 0:Q_LORA] = dq_lat.astype(BF16)
        dlat_ref[:, Q_LORA:Q_LORA + KV_LORA] = dc_kv.astype(BF16)
        dlat_ref[:, 640:768] = dkpe.astype(BF16)
        _accumulate(ggqa_ref, _colsum(gqa_rows), first)
        _accumulate(ggkva_ref, _colsum(gkva_rows), first)
        _accumulate(ggq_ref, ggq, first)
        _accumulate(ggk_ref, ggk, first)

    heads = lambda w: pl.BlockSpec((N_HEADS, t, w), lambda i: (0, i, 0))
    acc = lambda w: pl.BlockSpec((1, w), lambda i: (0, 0))
    return pl.pallas_call(
        body, name="mla_pre_bwd", grid=(l // t,),
        in_specs=[_rows(t, LAT_W), _rows(t, 1), _resident((1, 128)), _resident((1, 128)), _resident((1, Q_LORA)),
                  _resident((1, KV_LORA)), _resident((1, HEAD_PAD)), _resident((1, HEAD_PAD)),
                  _resident((Q_LORA, N_HEADS * HEAD_PAD)), _member_block("w_kv_b"),
                  heads(HEAD_PAD), heads(HEAD_PAD), heads(V_HEAD), ANY],
        out_specs=[_rows(t, LAT_W), _rows(t, Q_LORA), _rows(t, N_HEADS * HEAD_PAD), _rows(t, KV_LORA), _rows(t, N_HEADS * 256),
                   acc(Q_LORA), acc(KV_LORA), acc(HEAD_PAD), acc(HEAD_PAD)],
        out_shape=[jax.ShapeDtypeStruct((l, LAT_W), BF16), jax.ShapeDtypeStruct((l, Q_LORA), BF16),
                   jax.ShapeDtypeStruct((l, N_HEADS * HEAD_PAD), BF16), jax.ShapeDtypeStruct((l, KV_LORA), BF16),
                   jax.ShapeDtypeStruct((l, N_HEADS * 256), BF16), jax.ShapeDtypeStruct((1, Q_LORA), F32),
                   jax.ShapeDtypeStruct((1, KV_LORA), F32), jax.ShapeDtypeStruct((1, HEAD_PAD), F32),
                   jax.ShapeDtypeStruct((1, HEAD_PAD), F32)],
        compiler_params=_cparams("arbitrary"),
    )(lat, pos, invf, sgn, gqa, gkva, gq, gk, w_qb_p, w_kvb, dq, dk, dv, token)


def _causal(s, transposed):
    row = lax.broadcasted_iota(jnp.int32, s.shape, 0)
    col = lax.broadcasted_iota(jnp.int32, s.shape, 1)
    keep = (row <= col) if transposed else (col <= row)
    return jnp.where(keep, s, -jnp.inf)


def _as_row(col):
    n = col.shape[0]
    row = lax.broadcasted_iota(jnp.int32, (n, n), 0)
    lane = lax.broadcasted_iota(jnp.int32, (n, n), 1)
    return jnp.sum(jnp.where(row == lane, col, 0.0), axis=0, keepdims=True)


def _attn_fwd(q, k, v, tq):
    l = q.shape[1]

    hb = 2

    def body(q_ref, k_ref, v_ref, o_ref, lse_ref):
        qi = pl.program_id(1)
        qs = [q_ref[a] for a in range(hb)]

        def step(kb, carry, masked):
            rows = pl.ds(pl.multiple_of(kb * tq, tq), tq)
            out = []
            for a, (m, den, acc) in enumerate(carry):
                s = _mm_nt(qs[a], k_ref[a, rows, :])
                if masked:
                    s = _causal(s, False)
                m_new = jnp.maximum(m, jnp.max(s, axis=-1, keepdims=True))
                alpha = jnp.exp(m - m_new)
                p = jnp.exp(s - m_new)
                den = alpha * den + jnp.sum(p, axis=-1, keepdims=True)
                acc = alpha * acc + _mm(p, v_ref[a, rows, :])
                out.append((m_new, den, acc))
            return tuple(out)

        init = tuple((jnp.full((tq, 1), -jnp.inf, F32), jnp.zeros((tq, 1), F32), jnp.zeros((tq, V_HEAD), F32))
                     for _ in range(hb))
        carry = lax.fori_loop(0, qi, lambda kb, c: step(kb, c, False), init)
        for a, (m, den, acc) in enumerate(step(qi, carry, True)):
            o_ref[:, V_HEAD * a:V_HEAD * (a + 1)] = acc / den
            lse_ref[a, 0] = _as_row(m + jnp.log(den))

    return pl.pallas_call(
        body, name="attn_fwd", grid=(N_HEADS // hb, l // tq),
        in_specs=[pl.BlockSpec((hb, tq, HEAD_PAD), lambda h, i: (h, i, 0)), pl.BlockSpec((hb, l, HEAD_PAD), lambda h, i: (h, 0, 0)),
                  pl.BlockSpec((hb, l, V_HEAD), lambda h, i: (h, 0, 0))],
        out_specs=[pl.BlockSpec((tq, hb * V_HEAD), lambda h, i: (i, h)), pl.BlockSpec((hb, 1, 1, tq), lambda h, i: (h, i, 0, 0))],
        out_shape=[jax.ShapeDtypeStruct((l, N_HEADS * V_HEAD), F32), jax.ShapeDtypeStruct((N_HEADS, l // tq, 1, tq), F32)],
        compiler_params=_cparams("parallel", "arbitrary"),
    )(q, k, v)


def _attn_bwd(q, k, v, o, do, lse_t, tq, token):
    l = q.shape[1]
    nq = l // tq

    hb = 1

    def body(q_ref, k_ref, v_ref, o_ref, do_ref, lse_ref, token_ref, dq_ref, dk_ref, dv_ref):
        ki = pl.program_id(1)

        @pl.when(ki == 0)
        def _():
            dq_ref[...] = jnp.zeros_like(dq_ref)

        kblks = [k_ref[a] for a in range(hb)]
        vblks = [v_ref[a] for a in range(hb)]
        ones = jnp.ones((8, V_HEAD), BF16)

        def step(qb, carry, masked):
            rows = pl.ds(pl.multiple_of(qb * tq, tq), tq)
            out = []
            for a, (dk, dv) in enumerate(carry):
                cols = slice(V_HEAD * a, V_HEAD * (a + 1))
                qblk = q_ref[a, rows, :]
                dov = do_ref[rows, cols]
                dob = dov.astype(BF16)
                delta = sum(_mm_nt(ones, part) for part in _three_bf16(dov * o_ref[rows, cols]))[0:1, :]
                st = _mm_nt(kblks[a], qblk)
                if masked:
                    st = _causal(st, True)
                pt = jnp.exp(st - lse_ref[a, qb])
                dv = dv + _mm(pt, dob)
                dst = (pt * (_mm_nt(vblks[a], dob) - delta)).astype(BF16)
                dk = dk + _mm(dst, qblk)
                dq_ref[a, rows, :] += _mm_tn(dst, kblks[a])
                out.append((dk, dv))
            return tuple(out)

        init = tuple((jnp.zeros((tq, HEAD_PAD), F32), jnp.zeros((tq, V_HEAD), F32)) for _ in range(hb))
        carry = lax.fori_loop(ki + 1, nq, lambda qb, c: step(qb, c, False), step(ki, init, True))
        for a, (dk, dv) in enumerate(carry):
            dk_ref[a] = dk
            dv_ref[a] = dv

    return pl.pallas_call(
        body, name="attn_bwd", grid=(N_HEADS // hb, nq),
        in_specs=[pl.BlockSpec((hb, l, HEAD_PAD), lambda h, i: (h, 0, 0)), pl.BlockSpec((hb, tq, HEAD_PAD), lambda h, i: (h, i, 0)),
                  pl.BlockSpec((hb, tq, V_HEAD), lambda h, i: (h, i, 0)), pl.BlockSpec((l, hb * V_HEAD), lambda h, i: (0, h)),
                  pl.BlockSpec((l, hb * V_HEAD), lambda h, i: (0, h)), pl.BlockSpec((hb, nq, 1, tq), lambda h, i: (h, 0, 0, 0)), ANY],
        out_specs=[pl.BlockSpec((hb, l, HEAD_PAD), lambda h, i: (h, 0, 0)), pl.BlockSpec((hb, tq, HEAD_PAD), lambda h, i: (h, i, 0)),
                   pl.BlockSpec((hb, tq, V_HEAD), lambda h, i: (h, i, 0))],
        out_shape=[jax.ShapeDtypeStruct((N_HEADS, l, HEAD_PAD), F32), jax.ShapeDtypeStruct((N_HEADS, l, HEAD_PAD), F32),
                   jax.ShapeDtypeStruct((N_HEADS, l, V_HEAD), F32)],
        compiler_params=_cparams("parallel", "arbitrary"),
    )(q, k, v, o, do, lse_t, token)


def _row_shards_mm(a, w_ref):
    a = a.astype(BF16)
    return sum(_mm(a[:, 256 * j:256 * (j + 1)], w_ref[j]) for j in range(4))


def _row_shards_mm_nt(a, w_ref):
    a = a.astype(BF16)
    return jnp.concatenate([_mm_nt(a, w_ref[j]) for j in range(4)], axis=-1)


def _merge_fwd(attn, y_ssm, gs, gm, x, grp_a, t):
    l = x.shape[0]

    def body(attn_ref, ys_ref, gs_ref, gm_ref, x_ref, wo_ref, wout_ref, ym_ref, mixed_ref, h_ref):
        y_mla = _row_shards_mm(attn_ref[...], wo_ref)
        ym_ref[...] = y_mla.astype(BF16)
        mixed = (_sigmoid(gs_ref[...].astype(F32)) * ys_ref[...] + _sigmoid(gm_ref[...].astype(F32)) * y_mla).astype(BF16)
        mixed_ref[...] = mixed
        h_ref[...] = x_ref[...] + _row_shards_mm(mixed, wout_ref)

    r = lambda: _rows(t, D_MODEL)
    return pl.pallas_call(
        body, name="merge_fwd", grid=(l // t,),
        in_specs=[r(), r(), r(), r(), r(), _member_block("w_o_mla"), _member_block("w_out")],
        out_specs=[r(), r(), r()],
        out_shape=[jax.ShapeDtypeStruct((l, D_MODEL), BF16), jax.ShapeDtypeStruct((l, D_MODEL), BF16),
                   jax.ShapeDtypeStruct((l, D_MODEL), F32)],
        compiler_params=_cparams("parallel"),
    )(attn, y_ssm, gs, gm, x, grp_a, grp_a)


def _merge_bwd(dh, y_ssm, y_mla, gs, gm, grp_a, t):
    l = dh.shape[0]

    def body(dh_ref, ys_ref, ym_ref, gs_ref, gm_ref, wo_ref, wout_ref, dys_ref, dym_ref, dgs_ref, dgm_ref, dattn_ref):
        dmixed = _row_shards_mm_nt(dh_ref[...], wout_ref)
        sg = _sigmoid(gs_ref[...].astype(F32))
        sm = _sigmoid(gm_ref[...].astype(F32))
        dys_ref[...] = (dmixed * sg).astype(BF16)
        dgs_ref[...] = (dmixed * ys_ref[...] * sg * (1.0 - sg)).astype(BF16)
        dym = (dmixed * sm).astype(BF16)
        dym_ref[...] = dym
        dgm_ref[...] = (dmixed * ym_ref[...].astype(F32) * sm * (1.0 - sm)).astype(BF16)
        dattn_ref[...] = _row_shards_mm_nt(dym, wo_ref)

    r = lambda: _rows(t, D_MODEL)
    bf = jax.ShapeDtypeStruct((l, D_MODEL), BF16)
    return pl.pallas_call(
        body, name="merge_bwd", grid=(l // t,),
        in_specs=[r(), r(), r(), r(), r(), _member_block("w_o_mla"), _member_block("w_out")],
        out_specs=[r(), r(), r(), r(), r()],
        out_shape=[bf, bf, bf, bf, jax.ShapeDtypeStruct((l, D_MODEL), F32)],
        compiler_params=_cparams("parallel"),
    )(dh, y_ssm, y_mla, gs, gm, grp_a, grp_a)


def _mlp_fwd_bwd(h, tgt, g2, grp_a, t):
    l = h.shape[0]

    def body(h_ref, tgt_ref, g_ref, wu_ref, wd_ref, dh_ref, hn_ref, da_ref, hid_ref, dout_ref, loss_ref, dg_ref):
        first = pl.program_id(0) == 0
        hv = h_ref[...]
        g = g_ref[...]
        hn = _rms_fwd(hv, g, D_MODEL).astype(BF16)
        hn_ref[...] = hn
        out = hv
        relus = []
        for s in range(4):
            cols = slice(1024 * s, 1024 * (s + 1))
            relu = jnp.maximum(_mm(hn, wu_ref[s]), 0.0)
            relus.append(relu)
            hid = (relu * relu).astype(BF16)
            hid_ref[:, cols] = hid
            out = out + _mm(hid, wd_ref[s])
        err = out - tgt_ref[...]
        _accumulate(loss_ref, jnp.full((8, 128), jnp.sum(err * err) * (0.5 / D_MODEL), F32), first)
        dout = err * (1.0 / D_MODEL)
        doutb = dout.astype(BF16)
        dout_ref[...] = doutb
        dhn = jnp.zeros_like(hv)
        for s in range(4):
            da = (_mm_nt(doutb, wd_ref[s]) * (2.0 * relus[s])).astype(BF16)
            da_ref[:, 1024 * s:1024 * (s + 1)] = da
            dhn = dhn + _mm_nt(da, wu_ref[s])
        dx, dg_rows = _rms_bwd(hv, g, dhn, D_MODEL)
        dh_ref[...] = dout + dx
        _accumulate(dg_ref, _colsum(dg_rows), first)

    r = lambda w: _rows(t, w)
    return pl.pallas_call(
        body, name="mlp_fwd_bwd", grid=(l // t,),
        in_specs=[r(D_MODEL), r(D_MODEL), _resident((1, D_MODEL)), _member_block("w_up"), _member_block("w_down")],
        out_specs=[r(D_MODEL), r(D_MODEL), r(D_FF), r(D_FF), r(D_MODEL), pl.BlockSpec((8, 128), lambda i: (0, 0)),
                   pl.BlockSpec((1, D_MODEL), lambda i: (0, 0))],
        out_shape=[jax.ShapeDtypeStruct((l, D_MODEL), F32), jax.ShapeDtypeStruct((l, D_MODEL), BF16),
                   jax.ShapeDtypeStruct((l, D_FF), BF16), jax.ShapeDtypeStruct((l, D_FF), BF16),
                   jax.ShapeDtypeStruct((l, D_MODEL), BF16), jax.ShapeDtypeStruct((8, 128), F32),
                   jax.ShapeDtypeStruct((1, D_MODEL), F32)],
        compiler_params=_cparams("arbitrary"),
    )(h, tgt, g2, grp_a, grp_a)


def _wgrad(a, b, name):
    l, m = a.shape
    n = b.shape[1]
    bm = m if m <= 512 else 512
    bl = min(l, 2048 if n <= 1024 else 1024)

    def body(a_ref, b_ref, o_ref):
        _accumulate(o_ref, _mm_tn(a_ref[...], b_ref[...]), pl.program_id(1) == 0)

    return pl.pallas_call(
        body, name=name, grid=(m // bm, l // bl),
        in_specs=[pl.BlockSpec((bl, bm), lambda i, j: (j, i)), pl.BlockSpec((bl, n), lambda i, j: (j, 0))],
        out_specs=pl.BlockSpec((bm, n), lambda i, j: (i, 0)),
        out_shape=jax.ShapeDtypeStruct((m, n), F32),
        compiler_params=_cparams("parallel", "arbitrary"),
    )(a, b)


def _wgrad_into(a, b, member, cut, dest=None):
    group, off, rs, cs = _place_in_group(member)
    l = a.shape[0]
    bm = min(rs, 512)
    bl = min(l, 2048)
    nb = rs // bm
    if cut == "row":
        a_spec = pl.BlockSpec((bl, bm), lambda j, i, k: (k, j * nb + i))
        b_spec = pl.BlockSpec((bl, cs), lambda j, i, k: (k, 0))
    else:
        a_spec = pl.BlockSpec((bl, bm), lambda j, i, k: (k, i))
        b_spec = pl.BlockSpec((bl, cs), lambda j, i, k: (k, j))

    def body(a_ref, b_ref, *rest):
        o_ref = rest[-1]
        part = _mm_tn(a_ref[...], b_ref[...])

        @pl.when(pl.program_id(2) == 0)
        def _():
            o_ref[0] = part

        @pl.when(pl.program_id(2) != 0)
        def _():
            o_ref[0] += part

    operands, in_specs, aliases = [a, b], [a_spec, b_spec], {}
    if dest is not None:
        operands.append(dest)
        in_specs.append(ANY)
        aliases = {2: 0}
    return pl.pallas_call(
        body, name="wgrad_" + member, grid=(4, nb, l // bl), in_specs=in_specs,
        out_specs=pl.BlockSpec((1, bm, cs), lambda j, i, k: (j, off // bm + i, 0)),
        out_shape=jax.ShapeDtypeStruct((4, _group_rows(group), cs), F32), input_output_aliases=aliases,
        compiler_params=_cparams("parallel", "parallel", "arbitrary"),
    )(*operands)


def _adamw(w, g, m, v, name, g_off, token):
    r, c = w.shape
    br = r
    for cand in (256, 128, 64, 32, 16, 8):
        if r % cand == 0 and g_off % cand == 0:
            br = cand
            break

    def body(w_ref, g_ref, m_ref, v_ref, token_ref, go_ref, d_ref, nm_ref, nv_ref):
        gv = g_ref[...]
        go_ref[...] = gv
        nm = ADAM_B1 * m_ref[...] + (1.0 - ADAM_B1) * gv
        nv = ADAM_B2 * v_ref[...] + (1.0 - ADAM_B2) * (gv * gv)
        m_hat = nm / (1.0 - ADAM_B1 ** ADAM_STEP)
        v_hat = nv / (1.0 - ADAM_B2 ** ADAM_STEP)
        d_ref[...] = -ADAM_LR * (m_hat / (jnp.sqrt(v_hat) + ADAM_EPS) + ADAM_WD * w_ref[...])
        nm_ref[...] = nm
        nv_ref[...] = nv

    spec = lambda: pl.BlockSpec((br, c), lambda i: (i, 0))
    g_spec = pl.BlockSpec((br, c), lambda i: (g_off // br + i, 0))
    shp = jax.ShapeDtypeStruct((r, c), F32)
    return pl.pallas_call(
        body, name=name, grid=(r // br,), in_specs=[spec(), g_spec, spec(), spec(), ANY],
        out_specs=[spec(), spec(), spec(), spec()], out_shape=[shp, shp, shp, shp], compiler_params=_cparams("parallel"),
    )(w, g, m, v, token)


def _place():
    return lax.axis_index("x"), lax.axis_index("y"), lax.axis_index("c")


def _other_chips(x, y):
    return [(1 - x, y), (x, 1 - y), (1 - x, 1 - y)]


ANY = pl.BlockSpec(memory_space=pl.ANY)


def _gather_weights(bufs):
    n = len(bufs)

    def body(*refs):
        outs, send_sems, recv_sems = refs[n:2 * n], refs[2 * n], refs[2 * n + 1]
        x, y, c = _place()
        chips = _other_chips(x, y)

        def part(g, px, py, pc):
            half = outs[g].shape[1] // 2
            return outs[g].at[2 * px + py, pl.ds(pl.multiple_of(pc * half, 16), half), :]

        def copy(k, src, dst, to):
            return pltpu.make_async_remote_copy(src_ref=src, dst_ref=dst, send_sem=send_sems.at[k], recv_sem=recv_sems.at[k],
                                                device_id=to, device_id_type=MESH)

        first = [copy(6 * g + j, part(g, x, y, c), part(g, x, y, c), (*chip, c)) for g in range(n) for j, chip in enumerate(chips)]
        for cp in first:
            cp.start()
        passed = []
        for g in range(n):
            for j, chip in enumerate(chips):
                landed = part(g, *chip, c)
                copy(6 * g + j, landed, landed, (x, y, c)).wait_recv()
                passed.append(copy(6 * g + 3 + j, landed, landed, (x, y, 1 - c)))
                passed[-1].start()
        for g in range(n):
            for j, chip in enumerate(chips):
                other = part(g, *chip, 1 - c)
                copy(6 * g + 3 + j, other, other, (x, y, c)).wait_recv()
        for cp in first + passed:
            cp.wait_send()

    return pl.pallas_call(
        body, name="gather_weights", in_specs=[ANY] * n, out_specs=[ANY] * n,
        out_shape=[jax.ShapeDtypeStruct(b.shape, b.dtype) for b in bufs], input_output_aliases={g: g for g in range(n)},
        scratch_shapes=[pltpu.SemaphoreType.DMA((6 * n,)), pltpu.SemaphoreType.DMA((6 * n,))],
    )(*bufs)


def _cast_shards(shards, group, place):
    width, members = GROUPS[group]
    rows = _group_rows(group)

    def body(place_ref, *refs):
        out = refs[-1]
        off = 0
        for ref, (_, r) in zip(refs[:-1], members):
            out[0, off:off + r, :] = ref[...].astype(BF16)
            off += r

    grid_spec = pltpu.PrefetchScalarGridSpec(
        num_scalar_prefetch=1, grid=(1,),
        in_specs=[pl.BlockSpec((r, width), lambda i, p: (0, 0)) for _, r in members],
        out_specs=pl.BlockSpec((1, rows, width), lambda i, p: (p[0], 0, 0)))
    return pl.pallas_call(
        body, name="cast_shards_" + group, grid_spec=grid_spec, out_shape=jax.ShapeDtypeStruct((4, rows, width), BF16),
        compiler_params=_cparams("arbitrary"),
    )(place, *[shards[name] for name, _ in members])


def _block_rows(h):
    return next(cand for cand in (256, 192, 128, 64, 32, 16) if h % cand == 0)


def _add_pair(buf, got, place, name):
    n, h, w = got.shape
    bh = _block_rows(h)
    nb = h // bh

    def body(place_ref, a_ref, b_ref, s_ref, sb_ref):
        s = a_ref[...] + b_ref[...]
        s_ref[...] = s
        sb_ref[...] = s.astype(BF16)

    spec = lambda: pl.BlockSpec((1, bh, w), lambda j, i, p: (j, i, 0))
    grid_spec = pltpu.PrefetchScalarGridSpec(
        num_scalar_prefetch=1, grid=(n, nb),
        in_specs=[pl.BlockSpec((1, bh, w), lambda j, i, p: (j, p[1] * nb + i, 0)), spec()], out_specs=[spec(), spec()])
    return pl.pallas_call(
        body, name=name, grid_spec=grid_spec,
        out_shape=[jax.ShapeDtypeStruct(got.shape, F32), jax.ShapeDtypeStruct(got.shape, BF16)],
        compiler_params=_cparams("parallel", "parallel"),
    )(place, buf, got)


def _add_received(pair, got, place, name):
    _, h, w = pair.shape
    bh = _block_rows(h)
    nb = h // bh

    def body(place_ref, own_ref, got_ref, o_ref):
        o_ref[...] = ((own_ref[0] + got_ref[0].astype(F32)) + got_ref[1].astype(F32)) + got_ref[2].astype(F32)

    grid_spec = pltpu.PrefetchScalarGridSpec(
        num_scalar_prefetch=1, grid=(nb,),
        in_specs=[pl.BlockSpec((1, bh, w), lambda i, p: (p[0], i, 0)), pl.BlockSpec((3, bh, w), lambda i, p: (0, i, 0))],
        out_specs=pl.BlockSpec((bh, w), lambda i, p: (p[1] * nb + i, 0)))
    return pl.pallas_call(
        body, name=name, grid_spec=grid_spec, out_shape=jax.ShapeDtypeStruct((2 * h, w), F32),
        compiler_params=_cparams("parallel"),
    )(place, pair, got)


def _swap_reduced_halves(bufs):
    n = len(bufs)

    def body(*refs):
        outs, send_sems, recv_sems = refs[n:2 * n], refs[2 * n], refs[2 * n + 1]
        x, y, c = _place()
        copies = []
        for g in range(n):
            half = outs[g].shape[0] // 2
            own = outs[g].at[pl.ds(pl.multiple_of(c * half, 8), half), :]
            copies.append(pltpu.make_async_remote_copy(src_ref=own, dst_ref=own, send_sem=send_sems.at[g],
                                                       recv_sem=recv_sems.at[g], device_id=(x, y, 1 - c), device_id_type=MESH))
        for cp in copies:
            cp.start()
        for g in range(n):
            half = outs[g].shape[0] // 2
            other = outs[g].at[pl.ds(pl.multiple_of((1 - c) * half, 8), half), :]
            pltpu.make_async_remote_copy(src_ref=other, dst_ref=other, send_sem=send_sems.at[g], recv_sem=recv_sems.at[g],
                                         device_id=(x, y, 1 - c), device_id_type=MESH).wait_recv()
        for cp in copies:
            cp.wait_send()

    return pl.pallas_call(
        body, name="swap_reduced_halves", in_specs=[ANY] * n, out_specs=[ANY] * n,
        out_shape=[jax.ShapeDtypeStruct(b.shape, b.dtype) for b in bufs], input_output_aliases={g: g for g in range(n)},
        scratch_shapes=[pltpu.SemaphoreType.DMA((n,)), pltpu.SemaphoreType.DMA((n,))],
    )(*bufs)


HBM = pl.BlockSpec(memory_space=pltpu.HBM)
SEM = pl.BlockSpec(memory_space=pltpu.SEMAPHORE)


def _copies_start(name, bufs, n_copies, plan, after=None):
    n = len(bufs)
    extra = [] if after is None else [after]

    def body(*refs):
        sems = refs[n + len(extra):n + len(extra) + 2 * n_copies]
        x, y, c = _place()
        for i, (src, dst, dev) in enumerate(plan(refs[:n], x, y, c)):
            pltpu.make_async_remote_copy(src_ref=src, dst_ref=dst, send_sem=sems[i], recv_sem=sems[n_copies + i],
                                         device_id=dev, device_id_type=MESH).start()
        token = refs[-1]
        token[...] = jnp.zeros_like(token)

    out = pl.pallas_call(
        body, name=name,
        out_shape=[pltpu.SemaphoreType.DMA(())] * (2 * n_copies) + [pltpu.HBM(b.shape, b.dtype) for b in bufs]
        + [jax.ShapeDtypeStruct((8, 128), F32)],
        in_specs=[HBM] * n + [ANY] * len(extra),
        out_specs=[SEM] * (2 * n_copies) + [HBM] * n + [pl.BlockSpec(memory_space=pltpu.VMEM)],
        input_output_aliases={i: 2 * n_copies + i for i in range(n)},
        compiler_params=pltpu.CompilerParams(has_side_effects=pltpu.SideEffectType.DATAFLOW_SIDE_EFFECTING),
    )(*[pltpu.with_memory_space_constraint(b, pltpu.HBM) for b in bufs], *extra)
    return list(out[:2 * n_copies]), list(out[2 * n_copies:-1]), out[-1]


def _copies_wait(name, bufs, sems, after, plan):
    n = len(bufs)
    k = len(sems) // 2

    def body(*refs):
        sem_refs = refs[n:n + 2 * k]
        x, y, c = _place()
        for i, (sent, landed, dev) in enumerate(plan(refs[:n], x, y, c)):
            cp = pltpu.make_async_remote_copy(src_ref=sent, dst_ref=landed, send_sem=sem_refs[i], recv_sem=sem_refs[k + i],
                                              device_id=dev, device_id_type=MESH)
            cp.wait_send()
            cp.wait_recv()

    return pl.pallas_call(
        body, name=name, out_shape=[pltpu.HBM(b.shape, b.dtype) for b in bufs],
        in_specs=[HBM] * n + [SEM] * (2 * k) + [ANY], out_specs=[HBM] * n, input_output_aliases={i: i for i in range(n)},
        compiler_params=pltpu.CompilerParams(has_side_effects=pltpu.SideEffectType.DATAFLOW_SIDE_EFFECTING),
    )(*bufs, *sems, after)


def _row_half(ref, which, axis):
    half = ref.shape[axis] // 2
    rows = pl.ds(pl.multiple_of(which * half, 8), half)
    return ref.at[rows, :] if axis == 0 else ref.at[:, rows, :]


class _SplitGather:
    def __init__(self, own, after):
        self.n = len(own)
        self.state = _copies_start("gather_start", own, 3 * self.n, self._sent, after)

    @staticmethod
    def _sent(refs, x, y, c):
        return [(w.at[2 * x + y], w.at[2 * x + y], (px, py, c)) for w in refs for px, py in _other_chips(x, y)]

    @staticmethod
    def _landed(refs, x, y, c):
        return [(w.at[2 * x + y], w.at[2 * px + py], (px, py, c)) for w in refs for px, py in _other_chips(x, y)]

    def token(self):
        return self.state[2]

    def wait(self, which, name, after):
        sems, bufs, _ = self.state
        k = 3 * self.n
        mine = [sems[3 * i + j] for i in which for j in range(3)] + [sems[k + 3 * i + j] for i in which for j in range(3)]
        return _copies_wait(name, [bufs[i] for i in which], mine, after, self._landed)


class _SplitReduction:
    def __init__(self, tag, groups, place):
        self.tag, self.groups, self.place = tag, groups, place

    def start_pair(self, bufs):
        n = len(bufs)
        lands = [lax.empty((4, b.shape[1] // 2, b.shape[2]), F32) for b in bufs]
        plan = lambda refs, x, y, c: [(_row_half(refs[i], 1 - c, 1), refs[n + i], (x, y, 1 - c)) for i in range(n)]
        self._pair = (_copies_start("pair_%s_start" % self.tag, bufs + lands, n, plan), plan, n)
        return self._pair[0][2]

    def pair_done_start_scatter(self, after):
        (sems, bufs, _), plan, n = self._pair
        out = _copies_wait("pair_%s_wait" % self.tag, bufs, sems, after, plan)
        pairs = [_add_pair(out[i], out[n + i], self.place, "add_pair_" + g) for i, g in enumerate(self.groups)]
        self._pair_f32 = [p[0] for p in pairs]
        lands = [lax.empty((3,) + p[1].shape[1:], BF16) for p in pairs]
        plan = lambda refs, x, y, c: [(refs[i].at[2 * px + py], refs[n + i].at[j], (px, py, c))
                                      for i in range(n) for j, (px, py) in enumerate(_other_chips(x, y))]
        self._scatter = (_copies_start("scatter_%s_start" % self.tag, [p[1] for p in pairs] + lands, 3 * n, plan), plan, n)
        return self._scatter[0][2]

    def scatter_done(self, after):
        (sems, bufs, _), plan, n = self._scatter
        out = _copies_wait("scatter_%s_wait" % self.tag, bufs, sems, after, plan)
        return [_add_received(self._pair_f32[i], out[n + i], self.place, "add_received_" + g)
                for i, g in enumerate(self.groups)]

    def start_join(self, halves):
        n = len(halves)
        sent = lambda refs, x, y, c: [(_row_half(r, c, 0), _row_half(r, c, 0), (x, y, 1 - c)) for r in refs]
        landed = lambda refs, x, y, c: [(_row_half(r, c, 0), _row_half(r, 1 - c, 0), (x, y, 1 - c)) for r in refs]
        self._join = (_copies_start("join_%s_start" % self.tag, halves, n, sent), landed)
        return self._join[0][2]

    def join_done(self, after):
        (sems, bufs, _), landed = self._join
        return _copies_wait("join_%s_wait" % self.tag, bufs, sems, after, landed)


def _pair_sum_small(mine):
    rows, w = mine.shape

    def body(in_ref, out_ref, sibling, send_sem, recv_sem):
        x, y, c = _place()
        swap = pltpu.make_async_remote_copy(src_ref=in_ref, dst_ref=sibling, send_sem=send_sem, recv_sem=recv_sem,
                                            device_id=(x, y, 1 - c), device_id_type=MESH)
        swap.start()
        swap.wait()
        out_ref[...] = in_ref[...] + sibling[...]

    return pl.pallas_call(
        body, name="pair_sum_small", out_shape=jax.ShapeDtypeStruct((rows, w), F32),
        in_specs=[pl.BlockSpec(memory_space=pltpu.VMEM)], out_specs=pl.BlockSpec(memory_space=pltpu.VMEM),
        scratch_shapes=[pltpu.VMEM((rows, w), F32), pltpu.SemaphoreType.DMA, pltpu.SemaphoreType.DMA],
        compiler_params=pltpu.CompilerParams(vmem_limit_bytes=VMEM_LIMIT_V7X),
    )(mine)


class _SplitChipSum:
    def __init__(self, pair, place):
        self.place = place
        slots = lax.empty((4,) + pair.shape, F32)
        sent = lambda refs, x, y, c: [(refs[0], refs[1].at[2 * x + y], (px, py, c)) for px, py in _other_chips(x, y)]
        self.landed = lambda refs, x, y, c: [(refs[0], refs[1].at[2 * px + py], (px, py, c)) for px, py in _other_chips(x, y)]
        self.state = _copies_start("small_sum_start", [pair, slots], 3, sent)

    def token(self):
        return self.state[2]

    def done(self, after):
        sems, bufs, _ = self.state
        pair, slots = _copies_wait("small_sum_wait", bufs, sems, after, self.landed)
        rows, w = pair.shape

        def body(place_ref, pair_ref, slots_ref, out_ref):
            for j in range(4):
                own = place_ref[0] == j

                @pl.when(own)
                def _():
                    out_ref[...] = pair_ref[...] if j == 0 else out_ref[...] + pair_ref[...]

                @pl.when(jnp.logical_not(own))
                def _():
                    out_ref[...] = slots_ref[j] if j == 0 else out_ref[...] + slots_ref[j]

        grid_spec = pltpu.PrefetchScalarGridSpec(
            num_scalar_prefetch=1, grid=(1,),
            in_specs=[pl.BlockSpec((rows, w), lambda i, p: (0, 0)), pl.BlockSpec((4, rows, w), lambda i, p: (0, 0, 0))],
            out_specs=pl.BlockSpec((rows, w), lambda i, p: (0, 0)))
        return pl.pallas_call(
            body, name="small_sum_add", grid_spec=grid_spec, out_shape=jax.ShapeDtypeStruct((rows, w), F32),
            compiler_params=_cparams("arbitrary"),
        )(self.place, pair, slots)


def _join_column_shards(g):
    return jnp.transpose(g, (1, 0, 2)).reshape(g.shape[1], 4 * g.shape[2])


def _split_column_shards(w):
    r = w.shape[0]
    return jnp.transpose(w.reshape(r, 4, w.shape[1] // 4), (1, 0, 2))


def _small_rows(shape):
    return -(-int(np.prod(shape)) // 1024)


def _pack_small(vals):
    segs = []
    for name, shape in SMALL_WEIGHTS:
        flat = vals[name].reshape(-1)
        segs.append(jnp.pad(flat, (0, _small_rows(shape) * 1024 - flat.shape[0])))
    total = sum(s.shape[0] for s in segs) // 1024
    segs.append(jnp.zeros((-total % 8 * 1024,), F32))
    return jnp.concatenate(segs).reshape(-1, 1024)


def _unpack_small(packed):
    out, off = {}, 0
    for name, shape in SMALL_WEIGHTS:
        rows = _small_rows(shape)
        out[name] = packed[off:off + rows].reshape(-1)[:int(np.prod(shape))].reshape(shape)
        off += rows
    return out


W_IN_SHARD = D_IN // 4
W_IN_GAP = 1216


def _pad_w_in(g):
    cut = W_IN_GAP - W_IN_SHARD
    return jnp.concatenate([g[0], g[1][:, :cut], jnp.zeros((g.shape[1], D_IN_PAD - D_IN), g.dtype), g[1][:, cut:], g[2], g[3]],
                           axis=1)


def _unpad_w_in(g):
    skip = D_IN_PAD - D_IN
    second = jnp.concatenate([g[:, W_IN_SHARD:W_IN_GAP], g[:, W_IN_GAP + skip:2 * W_IN_SHARD + skip]], axis=1)
    return jnp.stack([g[:, :W_IN_SHARD], second, g[:, 2 * W_IN_SHARD + skip:3 * W_IN_SHARD + skip],
                      g[:, 3 * W_IN_SHARD + skip:]])


def _pad_heads(w):
    r = w.shape[0]
    return jnp.pad(w.reshape(r, N_HEADS, QK_HEAD), ((0, 0), (0, 0), (0, HEAD_PAD - QK_HEAD))).reshape(r, N_HEADS * HEAD_PAD)


def _unpad_heads(g):
    r = g.shape[0]
    return g.reshape(r, N_HEADS, HEAD_PAD)[:, :, :QK_HEAD].reshape(r, N_HEADS * QK_HEAD)


def _local_step(x, positions, tgt, grp_b, small, gather, red_a, red_rest):
    l = x.shape[0]
    t = min(l, 512)
    t_mlp = min(l, 256)
    tq = min(l, 1024)
    tc = min(l, 256)
    row = lambda v: v.reshape(1, -1).astype(F32)

    w_in_p = _pad_w_in(grp_b)
    g1, g2 = row(small["norm_mix"]), row(small["norm_mlp"])
    gqa, gkva = row(small["q_a_norm"]), row(small["kv_a_norm"])
    gq = jnp.pad(row(small["q_norm"]), ((0, 0), (0, HEAD_PAD - QK_HEAD)))
    gk = jnp.pad(row(small["k_norm"]), ((0, 0), (0, HEAD_PAD - QK_HEAD)))
    half = QK_ROPE // 2
    inv_freq = ROPE_THETA ** (-jnp.arange(half, dtype=F32) / half)
    invf = jnp.concatenate([inv_freq, inv_freq, jnp.zeros((64,), F32)]).reshape(1, 128)
    sgn = jnp.concatenate([-jnp.ones((half,), F32), jnp.ones((half,), F32), jnp.zeros((64,), F32)]).reshape(1, 128)
    pos = positions.reshape(l, 1)

    a_re, a_im = small["ssm_a_re"], small["ssm_a_im"]
    log_dt = small["ssm_log_dt"].reshape(SSM_GROUPS, 1)
    to_gcp = lambda b: jnp.transpose(b, (0, 2, 1)).reshape(SSM_WIDTH, SSM_STATE)
    from_gcp = lambda b: jnp.transpose(b.reshape(SSM_GROUPS, SSM_GROUP_CH, SSM_STATE), (0, 2, 1))
    b_re, b_im = to_gcp(small["ssm_b_re"]), to_gcp(small["ssm_b_im"])
    c_re, c_im = small["ssm_c_re"].reshape(SSM_WIDTH, SSM_STATE), small["ssm_c_im"].reshape(SSM_WIDTH, SSM_STATE)
    wb, wc, tabs_fwd, tabs_rev = _ssm_param_fwd(a_re, a_im, log_dt, b_re, b_im, c_re, c_im)
    dskip = row(small["ssm_d"])
    b_glu = row(small["b_glu"])

    u, lat, gs, gm = _in_proj_fwd(x, g1, w_in_p, t, gather.token())
    grp_c, grp_d, grp_e = gather.wait([0, 1, 2], "gather_cde_wait", u)
    w_qb_p = _pad_heads(_join_column_shards(grp_c))
    xr, xi, y, y_ssm = _ssm_fwd(u, wb, wc, tabs_fwd, dskip, grp_d, b_glu, grp_e, tc)
    q, k, v = _mla_pre_fwd(lat, pos, invf, sgn, gqa, gkva, gq, gk, w_qb_p, grp_d, t)
    attn, lse = _attn_fwd(q, k, v, tq)
    (grp_a,) = gather.wait([3], "gather_a_wait", attn)
    y_mla, mixed, h = _merge_fwd(attn, y_ssm, gs, gm, x, grp_a, t)
    dh, hn, da, hid, dout, loss_blk, g_norm_mlp = _mlp_fwd_bwd(h, tgt, g2, grp_a, t_mlp)

    ga = _wgrad_into(hn, da, "w_up", "col", _wgrad_into(hid, dout, "w_down", "row"))
    dys, dym, dgs, dgm, dattn = _merge_bwd(dh, y_ssm, y_mla, gs, gm, grp_a, t)
    ga = _wgrad_into(attn, dym, "w_o_mla", "row", _wgrad_into(mixed, dh, "w_out", "row", ga))

    dq, dk, dv = _attn_bwd(q, k, v, attn, dattn, lse, tq, red_a.start_pair([ga]))
    d_lat, ql, dq0, ckn, dkv, g_qa, g_kva, g_q, g_k = _mla_pre_bwd(lat, pos, invf, sgn, gqa, gkva, gq, gk, w_qb_p, grp_d,
                                                                    dq, dk, dv, t, red_a.pair_done_start_scatter(dk))
    gc = _split_column_shards(_unpad_heads(_wgrad(ql, dq0, "wgrad_q_b")))

    d_u, adj, dy, z, z2, dpre, g_b_glu, g_d, g_lr, g_li = _ssm_bwd(
        dys, y, u, xr, xi, wb, wc, tabs_rev, dskip, grp_d, b_glu, grp_e, tc)
    gd = _wgrad_into(z, dpre, "w_glu", "row", _wgrad_into(ckn, dkv, "w_kv_b", "col"))
    ge = _wgrad_into(z2, dys, "w_o_ssm", "col")
    grad_x, xn, dproj, g_norm_mix = _in_proj_bwd(x, g1, w_in_p, d_u, d_lat, dgs, dgm, dh, t)
    gb = _unpad_w_in(_wgrad(xn, dproj, "wgrad_in"))

    red_a.start_join(red_a.scatter_done(gb))
    g_wb = _wgrad_strips(u, adj, adj, "wgrad_ssm_b", 1, red_rest.start_pair([gb, gc, gd, ge]))
    g_wct = _wgrad_strips(dy, xr, xi, "wgrad_ssm_c", 0, red_rest.pair_done_start_scatter(g_wb))
    g_ar, g_ai, g_ldt, g_br, g_bi, g_cr, g_ci = _ssm_param_bwd(a_re, a_im, log_dt, b_re, b_im, g_lr, g_li, g_wb, g_wct)

    g_small = {
        "norm_mix": g_norm_mix.reshape(-1), "norm_mlp": g_norm_mlp.reshape(-1), "q_a_norm": g_qa.reshape(-1),
        "kv_a_norm": g_kva.reshape(-1), "q_norm": g_q.reshape(-1)[:QK_HEAD], "k_norm": g_k.reshape(-1)[:QK_HEAD],
        "ssm_a_re": g_ar, "ssm_a_im": g_ai, "ssm_log_dt": g_ldt.reshape(-1),
        "ssm_b_re": from_gcp(g_br), "ssm_b_im": from_gcp(g_bi),
        "ssm_c_re": g_cr.reshape(SSM_GROUPS, SSM_GROUP_CH, SSM_STATE), "ssm_c_im": g_ci.reshape(SSM_GROUPS, SSM_GROUP_CH, SSM_STATE),
        "ssm_d": g_d.reshape(SSM_GROUPS, SSM_GROUP_CH), "b_glu": g_b_glu.reshape(-1),
    }
    return loss_blk[0, 0], grad_x, g_small


def kernel(x, positions, norm_mix, w_in, q_a_norm, kv_a_norm, w_q_b, w_kv_b, q_norm, k_norm, w_o_mla, ssm_a_re, ssm_a_im, ssm_log_dt, ssm_b_re, ssm_b_im, ssm_c_re, ssm_c_im, ssm_d, w_glu, b_glu, w_o_ssm, w_out, norm_mlp, w_up, w_down, loss_target, m_norm_mix, m_w_in, m_q_a_norm, m_kv_a_norm, m_w_q_b, m_w_kv_b, m_q_norm, m_k_norm, m_w_o_mla, m_ssm_a_re, m_ssm_a_im, m_ssm_log_dt, m_ssm_b_re, m_ssm_b_im, m_ssm_c_re, m_ssm_c_im, m_ssm_d, m_w_glu, m_b_glu, m_w_o_ssm, m_w_out, m_norm_mlp, m_w_up, m_w_down, v_norm_mix, v_w_in, v_q_a_norm, v_kv_a_norm, v_w_q_b, v_w_kv_b, v_q_norm, v_k_norm, v_w_o_mla, v_ssm_a_re, v_ssm_a_im, v_ssm_log_dt, v_ssm_b_re, v_ssm_b_im, v_ssm_c_re, v_ssm_c_im, v_ssm_d, v_w_glu, v_b_glu, v_w_o_ssm, v_w_out, v_norm_mlp, v_w_up, v_w_down):
    args = dict(locals())
    w = {n: args[n][0] for n in WEIGHT_ORDER}
    m = {n: args["m_" + n][0] for n in WEIGHT_ORDER}
    v = {n: args["v_" + n][0] for n in WEIGHT_ORDER}
    big_names = [n for n, *_ in BIG_WEIGHTS]
    small_names = [n for n, _ in SMALL_WEIGHTS]

    place = jnp.stack([2 * lax.axis_index("x") + lax.axis_index("y"), lax.axis_index("c")]).astype(jnp.int32)
    rest = ["b", "c", "d", "e"]

    (grp_b,) = _gather_weights([_cast_shards(w, "b", place)])
    gather = _SplitGather([_cast_shards(w, g, place) for g in ("c", "d", "e", "a")], grp_b)
    red_a = _SplitReduction("a", ["a"], place)
    red_rest = _SplitReduction("rest", rest, place)
    small = {n: w[n] for n in small_names}

    loss_local, grad_x, g_small = _local_step(x[0], positions[0], loss_target[0], grp_b, small, gather, red_a, red_rest)
    loss = lax.psum(loss_local, ("x", "y", "c"))

    grad_w, delta_w, new_m, new_v = {}, {}, {}, {}

    def update(names, reduced, token):
        for n in names:
            g, off, _, _ = _place_in_group(n)
            grad_w[n], delta_w[n], new_m[n], new_v[n] = _adamw(w[n], reduced[g], m[n], v[n], "adamw_" + n, off, token)
            token = new_v[n]

    chip_sum = _SplitChipSum(_pair_sum_small(_pack_small(g_small)), place)
    in_a = [n for n, _ in GROUPS["a"][1]]
    update(in_a, {"a": red_a.join_done(chip_sum.token())[0]}, chip_sum.token())
    small_sum = chip_sum.done(new_v[in_a[-1]])
    g_s, d_s, m_s, v_s = _adamw(_pack_small(small), small_sum, _pack_small({n: m[n] for n in small_names}),
                                _pack_small({n: v[n] for n in small_names}), "adamw_small", 0, small_sum)
    g_s, d_s, m_s, v_s = _unpack_small(g_s), _unpack_small(d_s), _unpack_small(m_s), _unpack_small(v_s)
    for n in small_names:
        grad_w[n], delta_w[n], new_m[n], new_v[n] = g_s[n], d_s[n], m_s[n], v_s[n]
    halves = red_rest.scatter_done(v_s[small_names[0]])
    reduced_rest = dict(zip(rest, _swap_reduced_halves(halves)))
    update([n for n in big_names if n not in in_a], reduced_rest, halves[0])

    lead = lambda d: [d[n][None] for n in WEIGHT_ORDER]
    return (loss, grad_x[None], *lead(grad_w), *lead(delta_w), *lead(new_m), *lead(new_v))
```

```python
import math

import jax
import jax.numpy as jnp
import numpy as np
from jax import lax
from jax.experimental import pallas as pl
from jax.experimental.pallas import tpu as pltpu

F32 = jnp.float32
BF16 = jnp.bfloat16

D_MODEL = 1024
SSM_GROUPS = 32
SSM_GROUP_CH = 16
SSM_WIDTH = 512
SSM_STATE = 64
GP = SSM_GROUPS * SSM_STATE
N_HEADS = 8
QK_NOPE = 128
QK_ROPE = 64
QK_HEAD = 192
HEAD_PAD = 256
V_HEAD = 128
Q_LORA = 384
KV_LORA = 256
LAT_W = 768
D_IN = 3264
D_IN_PAD = 3328
D_FF = 4096
ROPE_THETA = 10000.0
EPS = 1e-6
ATT_SCALE = QK_HEAD ** -0.5

ADAM_LR = 0.001
ADAM_B1 = 0.9
ADAM_B2 = 0.999
ADAM_EPS = 1e-08
ADAM_WD = 0.01
ADAM_STEP = 10

VMEM_LIMIT_V7X = 56 * 1024 * 1024
MESH = pl.DeviceIdType.MESH

BIG_WEIGHTS = (
    ("w_in", 1024, 3264, "col"),
    ("w_q_b", 384, 1536, "col"),
    ("w_kv_b", 256, 2048, "col"),
    ("w_o_mla", 1024, 1024, "row"),
    ("w_glu", 512, 512, "row"),
    ("w_o_ssm", 512, 1024, "col"),
    ("w_out", 1024, 1024, "row"),
    ("w_up", 1024, 4096, "col"),
    ("w_down", 4096, 1024, "row"),
)
GROUPS = {
    "a": (1024, (("w_down", 1024), ("w_up", 1024), ("w_o_mla", 256), ("w_out", 256))),
    "b": (816, (("w_in", 1024),)),
    "c": (384, (("w_q_b", 384),)),
    "d": (512, (("w_kv_b", 256), ("w_glu", 128))),
    "e": (256, (("w_o_ssm", 512),)),
}


def _group_rows(group):
    return sum(r for _, r in GROUPS[group][1])


def _place_in_group(name):
    for group, (width, members) in GROUPS.items():
        off = 0
        for member, rows in members:
            if member == name:
                return group, off, rows, width
            off += rows
    raise KeyError(name)


SMALL_WEIGHTS = (
    ("norm_mix", (1024,)), ("q_a_norm", (384,)), ("kv_a_norm", (256,)), ("q_norm", (192,)), ("k_norm", (192,)),
    ("ssm_a_re", (32, 64)), ("ssm_a_im", (32, 64)), ("ssm_log_dt", (32,)),
    ("ssm_b_re", (32, 64, 16)), ("ssm_b_im", (32, 64, 16)), ("ssm_c_re", (32, 16, 64)), ("ssm_c_im", (32, 16, 64)),
    ("ssm_d", (32, 16)), ("b_glu", (512,)), ("norm_mlp", (1024,)),
)
WEIGHT_ORDER = ('norm_mix', 'w_in', 'q_a_norm', 'kv_a_norm', 'w_q_b', 'w_kv_b', 'q_norm', 'k_norm', 'w_o_mla', 'ssm_a_re',
                'ssm_a_im', 'ssm_log_dt', 'ssm_b_re', 'ssm_b_im', 'ssm_c_re', 'ssm_c_im', 'ssm_d', 'w_glu', 'b_glu',
                'w_o_ssm', 'w_out', 'norm_mlp', 'w_up', 'w_down')


def _cparams(*sem):
    return pltpu.CompilerParams(dimension_semantics=sem if sem else None, vmem_limit_bytes=VMEM_LIMIT_V7X)


def _resident(shape, index=None):
    index = (0,) * len(shape) if index is None else index
    return pl.BlockSpec(shape, lambda *_: index, pipeline_mode=pl.Buffered(1))


def _member_block(name):
    _, off, rows, width = _place_in_group(name)
    return _resident((4, rows, width), (0, off // rows, 0))


def _rows(t, width):
    return pl.BlockSpec((t, width), lambda i: (i, 0))


def _mm(a, b):
    return jnp.dot(a.astype(BF16), b.astype(BF16), preferred_element_type=F32)


def _mm_nt(a, b):
    return lax.dot_general(a.astype(BF16), b.astype(BF16), (((1,), (1,)), ((), ())), preferred_element_type=F32)


def _mm_tn(a, b):
    return lax.dot_general(a.astype(BF16), b.astype(BF16), (((0,), (0,)), ((), ())), preferred_element_type=F32)


def _rms_fwd(x, g, n):
    r = lax.rsqrt(jnp.sum(x * x, axis=-1, keepdims=True) * (1.0 / n) + EPS)
    return x * r * g


def _rms_bwd(x, g, dy, n):
    r = lax.rsqrt(jnp.sum(x * x, axis=-1, keepdims=True) * (1.0 / n) + EPS)
    xh = x * r
    dxh = dy * g
    dx = r * (dxh - xh * (jnp.sum(dxh * xh, axis=-1, keepdims=True) * (1.0 / n)))
    return dx, dy * xh


def _colsum(a):
    return jnp.sum(a, axis=0, keepdims=True)


def _accumulate(ref, value, first):
    @pl.when(first)
    def _():
        ref[...] = value

    @pl.when(jnp.logical_not(first))
    def _():
        ref[...] += value


def _sigmoid(a):
    return 1.0 / (1.0 + jnp.exp(-a))


GELU_C = math.sqrt(2.0 / math.pi)
GELU_A = 0.044715


def _gelu(y):
    return 0.5 * y * (1.0 + jnp.tanh(GELU_C * (y + GELU_A * y * y * y)))


def _gelu_grad(y):
    t = jnp.tanh(GELU_C * (y + GELU_A * y * y * y))
    return 0.5 * (1.0 + t) + 0.5 * y * (1.0 - t * t) * GELU_C * (1.0 + 3.0 * GELU_A * y * y)


def _in_proj_fwd(x, g1, w_in_p, t, token):
    l = x.shape[0]

    def body(x_ref, g_ref, w_ref, token_ref, u_ref, lat_ref, gs_ref, gm_ref):
        xn = _rms_fwd(x_ref[...], g_ref[...], D_MODEL).astype(BF16)
        u_ref[...] = _mm(xn, w_ref[:, 0:512])
        lat_ref[...] = _mm(xn, w_ref[:, 512:1280])
        gs_ref[...] = _mm(xn, w_ref[:, 1280:2304]).astype(BF16)
        gm_ref[...] = _mm(xn, w_ref[:, 2304:3328]).astype(BF16)

    return pl.pallas_call(
        body, name="in_proj_fwd", grid=(l // t,),
        in_specs=[_rows(t, D_MODEL), _resident((1, D_MODEL)), _resident((D_MODEL, D_IN_PAD)), ANY],
        out_specs=[_rows(t, 512), _rows(t, LAT_W), _rows(t, D_MODEL), _rows(t, D_MODEL)],
        out_shape=[jax.ShapeDtypeStruct((l, 512), F32), jax.ShapeDtypeStruct((l, LAT_W), F32),
                   jax.ShapeDtypeStruct((l, D_MODEL), BF16), jax.ShapeDtypeStruct((l, D_MODEL), BF16)],
        compiler_params=_cparams("parallel"),
    )(x, g1, w_in_p, token)


def _in_proj_bwd(x, g1, w_in_p, d_u, d_lat, d_gs, d_gm, dh, t):
    l = x.shape[0]

    def body(x_ref, g_ref, w_ref, du_ref, dlat_ref, dgs_ref, dgm_ref, dh_ref, gx_ref, xn_ref, dproj_ref, dg_ref):
        xv = x_ref[...]
        g = g_ref[...]
        xn_ref[...] = _rms_fwd(xv, g, D_MODEL).astype(BF16)
        dproj_ref[:, 0:512] = du_ref[...]
        dproj_ref[:, 512:1280] = dlat_ref[...]
        dproj_ref[:, 1280:2304] = dgs_ref[...]
        dproj_ref[:, 2304:3328] = dgm_ref[...]
        dxn = _mm_nt(dproj_ref[...], w_ref[...])
        dx, dg_rows = _rms_bwd(xv, g, dxn, D_MODEL)
        gx_ref[...] = dh_ref[...] + dx
        _accumulate(dg_ref, _colsum(dg_rows), pl.program_id(0) == 0)

    return pl.pallas_call(
        body, name="in_proj_bwd", grid=(l // t,),
        in_specs=[_rows(t, D_MODEL), _resident((1, D_MODEL)), _resident((D_MODEL, D_IN_PAD)), _rows(t, 512),
                  _rows(t, LAT_W), _rows(t, D_MODEL), _rows(t, D_MODEL), _rows(t, D_MODEL)],
        out_specs=[_rows(t, D_MODEL), _rows(t, D_MODEL), _rows(t, D_IN_PAD), pl.BlockSpec((1, D_MODEL), lambda i: (0, 0))],
        out_shape=[jax.ShapeDtypeStruct((l, D_MODEL), F32), jax.ShapeDtypeStruct((l, D_MODEL), BF16),
                   jax.ShapeDtypeStruct((l, D_IN_PAD), BF16), jax.ShapeDtypeStruct((1, D_MODEL), F32)],
        compiler_params=_cparams("arbitrary"),
    )(x, g1, w_in_p, d_u, d_lat, d_gs, d_gm, dh)


def _ssm_param_fn(a_re, a_im, log_dt, b_re, b_im):
    dt = jnp.exp(log_dt)
    er = jnp.exp(a_re * dt)
    lr = er * jnp.cos(a_im * dt)
    li = er * jnp.sin(a_im * dt)
    den = a_re * a_re + a_im * a_im
    nr = lr - 1.0
    kr = (nr * a_re + li * a_im) / den
    ki = (li * a_re - nr * a_im) / den
    rows = lambda k: jnp.broadcast_to(k[:, None, :], (SSM_GROUPS, SSM_GROUP_CH, SSM_STATE)).reshape(SSM_WIDTH, SSM_STATE)
    krt, kit = rows(kr), rows(ki)
    return lr, li, krt * b_re - kit * b_im, krt * b_im + kit * b_re


def _state_selector():
    row = lax.broadcasted_iota(jnp.int32, (SSM_STATE, GP), 0)
    col = lax.broadcasted_iota(jnp.int32, (SSM_STATE, GP), 1)
    return jnp.where(jnp.bitwise_and(col, SSM_STATE - 1) == row, 1.0, 0.0).astype(BF16)


def _own_group(rows, rows_per_group_log2):
    row = lax.broadcasted_iota(jnp.int32, (rows, GP), 0)
    col = lax.broadcasted_iota(jnp.int32, (rows, GP), 1)
    return jnp.right_shift(row, rows_per_group_log2) == jnp.right_shift(col, 6)


def _three_bf16(x):
    hi = x.astype(BF16)
    rest = x - hi.astype(F32)
    mid = rest.astype(BF16)
    return hi, mid, (rest - mid.astype(F32)).astype(BF16)


def _spread(x, sel):
    return sum(jnp.dot(part, sel, preferred_element_type=F32) for part in _three_bf16(x))


def _collect(xw, sel):
    return sum(lax.dot_general(part, sel, (((1,), (1,)), ((), ())), preferred_element_type=F32) for part in _three_bf16(xw))


def _ssm_param_fwd(a_re, a_im, log_dt, b_re, b_im, c_re, c_im):
    def body(ar_ref, ai_ref, ldt_ref, br_ref, bi_ref, cr_ref, ci_ref, wb_ref, wct_ref, tf_ref, tr_ref):
        lr, li, bbr, bbi = _ssm_param_fn(ar_ref[...], ai_ref[...], ldt_ref[...], br_ref[...], bi_ref[...])
        sel = _state_selector()
        own16 = _own_group(SSM_WIDTH, 4)
        own1 = _own_group(SSM_GROUPS, 0)
        block = lambda m: jnp.where(own16, jnp.dot(m.astype(BF16), sel, preferred_element_type=F32), 0.0).astype(BF16)
        wb_ref[:, 0:GP] = block(bbr)
        wb_ref[:, GP:2 * GP] = block(bbi)
        wct_ref[:, 0:GP] = block(cr_ref[...])
        wct_ref[:, GP:2 * GP] = block(-ci_ref[...])
        flat = lambda m: _colsum(jnp.where(own1, _spread(m, sel), 0.0))
        pr, pi = [], []
        qr, qi = lr, li
        for _ in range(8):
            pr.append(flat(qr))
            pi.append(flat(qi))
            qr, qi = qr * lr - qi * li, qr * li + qi * lr
        row = lax.broadcasted_iota(jnp.int32, (8, GP), 0)
        for n, k in enumerate((1, 2, 4)):
            tf_ref[2 * n] = jnp.where(row >= k, pr[k - 1], 0.0)
            tf_ref[2 * n + 1] = jnp.where(row >= k, pi[k - 1], 0.0)
            tr_ref[2 * n] = jnp.where(row < 8 - k, pr[k - 1], 0.0)
            tr_ref[2 * n + 1] = jnp.where(row < 8 - k, -pi[k - 1], 0.0)
        pick = lambda vals: sum(jnp.where(row == j, v, 0.0) for j, v in enumerate(vals))
        tf_ref[6] = pick(pr)
        tf_ref[7] = pick(pi)
        tr_ref[6] = pick(pr[::-1])
        tr_ref[7] = pick([-v for v in pi[::-1]])

    return pl.pallas_call(
        body, name="ssm_param_fwd",
        out_shape=[jax.ShapeDtypeStruct((SSM_WIDTH, 2 * GP), BF16), jax.ShapeDtypeStruct((SSM_WIDTH, 2 * GP), BF16),
                   jax.ShapeDtypeStruct((8, 8, GP), F32), jax.ShapeDtypeStruct((8, 8, GP), F32)],
        compiler_params=_cparams(),
    )(a_re, a_im, log_dt, b_re, b_im, c_re, c_im)


STRIP_CH = 128
STRIP_ST = 512
N_STRIPS = SSM_WIDTH // STRIP_CH


def _ssm_param_bwd(a_re, a_im, log_dt, b_re, b_im, g_lr, g_li, g_wb, g_wct):
    def body(ar_ref, ai_ref, ldt_ref, br_ref, bi_ref, glr_ref, gli_ref, gwb_ref, gwc_ref,
             o_ar, o_ai, o_ldt, o_br, o_bi, o_cr, o_ci):
        sel = _state_selector()
        own1 = _own_group(SSM_GROUPS, 0)
        row = lax.broadcasted_iota(jnp.int32, (SSM_WIDTH, STRIP_ST), 0)
        col = lax.broadcasted_iota(jnp.int32, (SSM_WIDTH, STRIP_ST), 1)
        own = jnp.bitwise_and(jnp.right_shift(row, 4), 7) == jnp.right_shift(col, 6)
        blocks = lambda m: _collect(jnp.where(own, m, 0.0), sel[:, 0:STRIP_ST])
        unflat = lambda v: _collect(jnp.where(own1, v, 0.0), sel)
        _, vjp = jax.vjp(_ssm_param_fn, ar_ref[...], ai_ref[...], ldt_ref[...], br_ref[...], bi_ref[...])
        d_ar, d_ai, d_ldt, d_br, d_bi = vjp((unflat(glr_ref[...]), unflat(gli_ref[...]),
                                             blocks(gwb_ref[:, 0:STRIP_ST]), blocks(gwb_ref[:, STRIP_ST:2 * STRIP_ST])))
        o_ar[...] = d_ar
        o_ai[...] = d_ai
        o_ldt[...] = d_ldt
        o_br[...] = d_br
        o_bi[...] = d_bi
        o_cr[...] = blocks(gwc_ref[:, 0:STRIP_ST])
        o_ci[...] = -blocks(gwc_ref[:, STRIP_ST:2 * STRIP_ST])

    g, p = SSM_GROUPS, SSM_STATE
    gp = jax.ShapeDtypeStruct((g, p), F32)
    gcp = jax.ShapeDtypeStruct((SSM_WIDTH, p), F32)
    return pl.pallas_call(
        body, name="ssm_param_bwd", out_shape=[gp, gp, jax.ShapeDtypeStruct((g, 1), F32), gcp, gcp, gcp, gcp],
        compiler_params=_cparams(),
    )(a_re, a_im, log_dt, b_re, b_im, g_lr, g_li, g_wb, g_wct)


def _strip(ref, j, im):
    return ref[STRIP_CH * j:STRIP_CH * (j + 1), im * GP + STRIP_ST * j:im * GP + STRIP_ST * (j + 1)]


def _wgrad_strips(a, b_re, b_im, name, im_block, token):
    l = a.shape[0]
    bl = min(l, 512)

    def body(a_ref, bre_ref, bim_ref, token_ref, o_ref):
        first = pl.program_id(0) == 0
        for j in range(N_STRIPS):
            aj = a_ref[:, STRIP_CH * j:STRIP_CH * (j + 1)]
            states = slice(STRIP_ST * j, STRIP_ST * (j + 1))
            _accumulate(o_ref.at[STRIP_CH * j:STRIP_CH * (j + 1), 0:STRIP_ST], _mm_tn(aj, bre_ref[:, states]), first)
            _accumulate(o_ref.at[STRIP_CH * j:STRIP_CH * (j + 1), STRIP_ST:2 * STRIP_ST], _mm_tn(aj, bim_ref[:, states]), first)

    return pl.pallas_call(
        body, name=name, grid=(l // bl,),
        in_specs=[pl.BlockSpec((bl, SSM_WIDTH), lambda k: (k, 0)), pl.BlockSpec((bl, GP), lambda k: (k, 0)),
                  pl.BlockSpec((bl, GP), lambda k: (k, im_block)), ANY],
        out_specs=pl.BlockSpec((SSM_WIDTH, 2 * STRIP_ST), lambda k: (0, 0)),
        out_shape=jax.ShapeDtypeStruct((SSM_WIDTH, 2 * STRIP_ST), F32),
        compiler_params=_cparams("arbitrary"),
    )(a, b_re, b_im, token)


SCAN_STRIP = 512


def _scan_chunk(inr_ref, ini_ref, outr_ref, outi_ref, cr_ref, ci_ref, tab_ref, tc, reverse):
    n_blocks = tc // 8

    def block(j, _):
        i = (n_blocks - 1 - j) if reverse else j
        rows = pl.ds(pl.multiple_of(i * 8, 8), 8)
        for s in range(GP // SCAN_STRIP):
            sl = pl.ds(s * SCAN_STRIP, SCAN_STRIP)
            xr = inr_ref[rows, sl]
            xi = ini_ref[rows, sl]
            for n, k in enumerate((1, 2, 4)):
                shift = (8 - k) if reverse else k
                sr = pltpu.roll(xr, shift, 0)
                si = pltpu.roll(xi, shift, 0)
                mr = tab_ref[2 * n, :, sl]
                mi = tab_ref[2 * n + 1, :, sl]
                xr, xi = xr + mr * sr - mi * si, xi + mr * si + mi * sr
            qr = tab_ref[6, :, sl]
            qi = tab_ref[7, :, sl]
            cr = cr_ref[:, sl]
            ci = ci_ref[:, sl]
            xr, xi = xr + qr * cr - qi * ci, xi + qr * ci + qi * cr
            outr_ref[rows, sl] = xr
            outi_ref[rows, sl] = xi
            edge = 0 if reverse else 7
            cr_ref[:, sl] = jnp.broadcast_to(xr[edge:edge + 1, :], (8, SCAN_STRIP))
            ci_ref[:, sl] = jnp.broadcast_to(xi[edge:edge + 1, :], (8, SCAN_STRIP))
        return 0

    lax.fori_loop(0, n_blocks, block, 0)


def _glu_pre(z, wg_ref):
    return sum(_mm(z[:, 128 * j:128 * (j + 1)], wg_ref[j]) for j in range(4))


def _ssm_fwd(u, wb, wc, tabs, dskip, grp_d, b_glu, grp_e, tc):
    l = u.shape[0]

    def body(u_ref, wb_ref, wc_ref, tab_ref, d_ref, wg_ref, bg_ref, wo_ref, xr_ref, xi_ref, y_ref, ys_ref,
             bur, bui, cr, ci):
        @pl.when(pl.program_id(0) == 0)
        def _():
            cr[...] = jnp.zeros_like(cr)
            ci[...] = jnp.zeros_like(ci)

        uv = u_ref[...]
        ub = uv.astype(BF16)
        for j in range(N_STRIPS):
            uj = ub[:, STRIP_CH * j:STRIP_CH * (j + 1)]
            states = slice(STRIP_ST * j, STRIP_ST * (j + 1))
            bur[:, states] = _mm(uj, _strip(wb_ref, j, 0))
            bui[:, states] = _mm(uj, _strip(wb_ref, j, 1))
        _scan_chunk(bur, bui, bur, bui, cr, ci, tab_ref, tc, False)
        xr_ref[...] = bur[...].astype(BF16)
        xi_ref[...] = bui[...].astype(BF16)
        y = jnp.concatenate(
            [_mm_nt(xr_ref[:, STRIP_ST * j:STRIP_ST * (j + 1)], _strip(wc_ref, j, 0))
             + _mm_nt(xi_ref[:, STRIP_ST * j:STRIP_ST * (j + 1)], _strip(wc_ref, j, 1)) for j in range(N_STRIPS)],
            axis=-1) + d_ref[...] * uv
        y_ref[...] = y
        z = _gelu(y)
        z2 = z * _sigmoid(_glu_pre(z, wg_ref) + bg_ref[...])
        for s in range(4):
            ys_ref[:, 256 * s:256 * (s + 1)] = _mm(z2, wo_ref[s]).astype(BF16)

    return pl.pallas_call(
        body, name="ssm_fwd", grid=(l // tc,),
        in_specs=[_rows(tc, 512), _resident((512, 2 * GP)), _resident((512, 2 * GP)), _resident((8, 8, GP)),
                  _resident((1, 512)), _member_block("w_glu"), _resident((1, 512)), _member_block("w_o_ssm")],
        out_specs=[_rows(tc, GP), _rows(tc, GP), _rows(tc, 512), _rows(tc, D_MODEL)],
        out_shape=[jax.ShapeDtypeStruct((l, GP), BF16), jax.ShapeDtypeStruct((l, GP), BF16),
                   jax.ShapeDtypeStruct((l, 512), F32), jax.ShapeDtypeStruct((l, D_MODEL), BF16)],
        scratch_shapes=[pltpu.VMEM((tc, GP), F32), pltpu.VMEM((tc, GP), F32), pltpu.VMEM((8, GP), F32),
                        pltpu.VMEM((8, GP), F32)],
        compiler_params=_cparams("arbitrary"),
    )(u, wb, wc, tabs, dskip, grp_d, b_glu, grp_e)


def _ssm_bwd(dys, y, u, xr, xi, wb, wc, tabs_rev, dskip, grp_d, b_glu, grp_e, tc):
    l = u.shape[0]
    nc = l // tc

    def body(dys_ref, y_ref, u_ref, xr_ref, xi_ref, wb_ref, wc_ref, tab_ref, d_ref, wg_ref, bg_ref, wo_ref,
             du_ref, a_ref, dy_ref, z_ref, z2_ref, dpre_ref, gb_ref, gd_ref, glr_ref, gli_ref,
             dxr, dxi, ar, ai, cr, ci):
        first = pl.program_id(0) == 0

        @pl.when(first)
        def _():
            cr[...] = jnp.zeros_like(cr)
            ci[...] = jnp.zeros_like(ci)

        yv = y_ref[...]
        uv = u_ref[...]
        dz2 = sum(_mm_nt(dys_ref[:, 256 * j:256 * (j + 1)], wo_ref[j]) for j in range(4))
        z = _gelu(yv)
        s = _sigmoid(_glu_pre(z, wg_ref) + bg_ref[...])
        dpre = dz2 * z * s * (1.0 - s)
        dpreb = dpre.astype(BF16)
        dz = dz2 * s + jnp.concatenate([_mm_nt(dpreb, wg_ref[j]) for j in range(4)], axis=-1)
        dy = dz * _gelu_grad(yv)
        z_ref[...] = z.astype(BF16)
        z2_ref[...] = (z * s).astype(BF16)
        dpre_ref[...] = dpre.astype(BF16)
        dy_ref[...] = dy.astype(BF16)
        _accumulate(gb_ref, _colsum(dpre), first)
        _accumulate(gd_ref, _colsum(dy * uv), first)

        dyb = dy.astype(BF16)
        for j in range(N_STRIPS):
            dyj = dyb[:, STRIP_CH * j:STRIP_CH * (j + 1)]
            dxr[:, STRIP_ST * j:STRIP_ST * (j + 1)] = _mm(dyj, _strip(wc_ref, j, 0))
            dxi[:, STRIP_ST * j:STRIP_ST * (j + 1)] = _mm(dyj, _strip(wc_ref, j, 1))
        ar[pl.ds(tc, 8), :] = cr[...]
        ai[pl.ds(tc, 8), :] = ci[...]
        _scan_chunk(dxr, dxi, ar, ai, cr, ci, tab_ref, tc, True)
        a_ref[:, 0:GP] = ar[pl.ds(0, tc), :].astype(BF16)
        a_ref[:, GP:2 * GP] = ai[pl.ds(0, tc), :].astype(BF16)
        du_states = jnp.concatenate(
            [_mm_nt(a_ref[:, STRIP_ST * j:STRIP_ST * (j + 1)], _strip(wb_ref, j, 0))
             + _mm_nt(a_ref[:, GP + STRIP_ST * j:GP + STRIP_ST * (j + 1)], _strip(wb_ref, j, 1)) for j in range(N_STRIPS)],
            axis=-1)
        du_ref[...] = (dy * d_ref[...] + du_states).astype(BF16)
        anr = ar[pl.ds(1, tc), :]
        ani = ai[pl.ds(1, tc), :]
        xrv = xr_ref[...].astype(F32)
        xiv = xi_ref[...].astype(F32)
        _accumulate(glr_ref, _colsum(anr * xrv + ani * xiv), first)
        _accumulate(gli_ref, _colsum(ani * xrv - anr * xiv), first)

    rev = lambda w: pl.BlockSpec((tc, w), lambda i: (nc - 1 - i, 0))
    acc = lambda w: pl.BlockSpec((1, w), lambda i: (0, 0))
    bf = jax.ShapeDtypeStruct((l, 512), BF16)
    return pl.pallas_call(
        body, name="ssm_bwd", grid=(nc,),
        in_specs=[rev(D_MODEL), rev(512), rev(512), rev(GP), rev(GP), _resident((512, 2 * GP)), _resident((512, 2 * GP)),
                  _resident((8, 8, GP)), _resident((1, 512)), _member_block("w_glu"), _resident((1, 512)),
                  _member_block("w_o_ssm")],
        out_specs=[rev(512), rev(2 * GP), rev(512), rev(512), rev(512), rev(512), acc(512), acc(512), acc(GP), acc(GP)],
        out_shape=[bf, jax.ShapeDtypeStruct((l, 2 * GP), BF16), bf, bf, bf, bf,
                   jax.ShapeDtypeStruct((1, 512), F32), jax.ShapeDtypeStruct((1, 512), F32),
                   jax.ShapeDtypeStruct((1, GP), F32), jax.ShapeDtypeStruct((1, GP), F32)],
        scratch_shapes=[pltpu.VMEM((tc, GP), F32), pltpu.VMEM((tc, GP), F32), pltpu.VMEM((tc + 8, GP), F32),
                        pltpu.VMEM((tc + 8, GP), F32), pltpu.VMEM((8, GP), F32), pltpu.VMEM((8, GP), F32)],
        compiler_params=_cparams("arbitrary"),
    )(dys, y, u, xr, xi, wb, wc, tabs_rev, dskip, grp_d, b_glu, grp_e)


def _swap_halves(b):
    lane = lax.broadcasted_iota(jnp.int32, b.shape, 1)
    return jnp.where(lane < 32, pltpu.roll(b, 96, 1), pltpu.roll(b, 32, 1))


def _rope_tables(pos_ref, invf_ref, sgn_ref):
    ang = pos_ref[...].astype(F32) * invf_ref[...]
    return jnp.cos(ang), jnp.sin(ang) * sgn_ref[...]


def _mla_pre_fwd(lat, pos, invf, sgn, gqa, gkva, gq, gk, w_qb_p, w_kvb, t):
    l = lat.shape[0]

    def body(lat_ref, pos_ref, invf_ref, sgn_ref, gqa_ref, gkva_ref, gq_ref, gk_ref, wq_ref, wkv_ref, q_ref, k_ref, v_ref):
        cs, sn = _rope_tables(pos_ref, invf_ref, sgn_ref)
        ql = _rms_fwd(lat_ref[:, 0:Q_LORA], gqa_ref[...], Q_LORA)
        ckn = _rms_fwd(lat_ref[:, Q_LORA:Q_LORA + KV_LORA], gkva_ref[...], KV_LORA)
        kpe = lat_ref[:, 640:768]
        q0 = _mm(ql, wq_ref[...])
        cknb = ckn.astype(BF16)
        kv = jnp.concatenate([_mm(cknb, wkv_ref[s]) for s in range(4)], axis=-1)
        for h in range(N_HEADS):
            q1 = _rms_fwd(q0[:, HEAD_PAD * h:HEAD_PAD * (h + 1)], gq_ref[...], QK_HEAD)
            b = q1[:, 128:256]
            q_ref[h, :, 0:128] = (q1[:, 0:128] * ATT_SCALE).astype(BF16)
            q_ref[h, :, 128:256] = ((b * cs + _swap_halves(b) * sn) * ATT_SCALE).astype(BF16)
            k0 = jnp.concatenate([kv[:, 256 * h:256 * h + 128], kpe], axis=-1)
            k1 = _rms_fwd(k0, gk_ref[...], QK_HEAD)
            b = k1[:, 128:256]
            k_ref[h, :, 0:128] = k1[:, 0:128].astype(BF16)
            k_ref[h, :, 128:256] = (b * cs + _swap_halves(b) * sn).astype(BF16)
            v_ref[h] = kv[:, 256 * h + 128:256 * h + 256].astype(BF16)

    heads = lambda w: pl.BlockSpec((N_HEADS, t, w), lambda i: (0, i, 0))
    return pl.pallas_call(
        body, name="mla_pre_fwd", grid=(l // t,),
        in_specs=[_rows(t, LAT_W), _rows(t, 1), _resident((1, 128)), _resident((1, 128)), _resident((1, Q_LORA)),
                  _resident((1, KV_LORA)), _resident((1, HEAD_PAD)), _resident((1, HEAD_PAD)),
                  _resident((Q_LORA, N_HEADS * HEAD_PAD)), _member_block("w_kv_b")],
        out_specs=[heads(HEAD_PAD), heads(HEAD_PAD), heads(V_HEAD)],
        out_shape=[jax.ShapeDtypeStruct((N_HEADS, l, HEAD_PAD), BF16), jax.ShapeDtypeStruct((N_HEADS, l, HEAD_PAD), BF16),
                   jax.ShapeDtypeStruct((N_HEADS, l, V_HEAD), BF16)],
        compiler_params=_cparams("parallel"),
    )(lat, pos, invf, sgn, gqa, gkva, gq, gk, w_qb_p, w_kvb)


def _mla_pre_bwd(lat, pos, invf, sgn, gqa, gkva, gq, gk, w_qb_p, w_kvb, dq, dk, dv, t, token):
    l = lat.shape[0]

    def body(lat_ref, pos_ref, invf_ref, sgn_ref, gqa_ref, gkva_ref, gq_ref, gk_ref, wq_ref, wkv_ref, dq_ref, dk_ref, dv_ref,
             token_ref, dlat_ref, ql_ref, dq0_ref, ckn_ref, dkv_ref, ggqa_ref, ggkva_ref, ggq_ref, ggk_ref):
        first = pl.program_id(0) == 0
        cs, sn = _rope_tables(pos_ref, invf_ref, sgn_ref)
        q_lat = lat_ref[:, 0:Q_LORA]
        c_kv = lat_ref[:, Q_LORA:Q_LORA + KV_LORA]
        kpe = lat_ref[:, 640:768]
        ql = _rms_fwd(q_lat, gqa_ref[...], Q_LORA)
        ckn = _rms_fwd(c_kv, gkva_ref[...], KV_LORA)
        ql_ref[...] = ql.astype(BF16)
        ckn_ref[...] = ckn.astype(BF16)
        q0 = _mm(ql, wq_ref[...])
        cknb = ckn.astype(BF16)
        kv = jnp.concatenate([_mm(cknb, wkv_ref[s]) for s in range(4)], axis=-1)
        dkpe = jnp.zeros_like(kpe)
        ggq = jnp.zeros((1, HEAD_PAD), F32)
        ggk = jnp.zeros((1, HEAD_PAD), F32)

        def unrope(d):
            b = d[:, 128:256]
            return jnp.concatenate([d[:, 0:128], b * cs + _swap_halves(b * sn)], axis=-1)

        for h in range(N_HEADS):
            dq1 = unrope(dq_ref[h] * ATT_SCALE)
            dq0h, gq_rows = _rms_bwd(q0[:, HEAD_PAD * h:HEAD_PAD * (h + 1)], gq_ref[...], dq1, QK_HEAD)
            ggq = ggq + _colsum(gq_rows)
            dq0_ref[:, HEAD_PAD * h:HEAD_PAD * (h + 1)] = dq0h.astype(BF16)
            k0 = jnp.concatenate([kv[:, 256 * h:256 * h + 128], kpe], axis=-1)
            dk0, gk_rows = _rms_bwd(k0, gk_ref[...], unrope(dk_ref[h]), QK_HEAD)
            ggk = ggk + _colsum(gk_rows)
            dkpe = dkpe + dk0[:, 128:256]
            dkv_ref[:, 256 * h:256 * h + 128] = dk0[:, 0:128].astype(BF16)
            dkv_ref[:, 256 * h + 128:256 * h + 256] = dv_ref[h].astype(BF16)
        dql = _mm_nt(dq0_ref[...], wq_ref[...])
        dckn = sum(_mm_nt(dkv_ref[:, 512 * s:512 * (s + 1)], wkv_ref[s]) for s in range(4))
        dq_lat, gqa_rows = _rms_bwd(q_lat, gqa_ref[...], dql, Q_LORA)
        dc_kv, gkva_rows = _rms_bwd(c_kv, gkva_ref[...], dckn, KV_LORA)
        dlat_ref[:, 0:Q_LORA] = dq_lat.astype(BF16)
        dlat_ref[:, Q_LORA:Q_LORA + KV_LORA] = dc_kv.astype(BF16)
        dlat_ref[:, 640:768] = dkpe.astype(BF16)
        _accumulate(ggqa_ref, _colsum(gqa_rows), first)
        _accumulate(ggkva_ref, _colsum(gkva_rows), first)
        _accumulate(ggq_ref, ggq, first)
        _accumulate(ggk_ref, ggk, first)

    heads = lambda w: pl.BlockSpec((N_HEADS, t, w), lambda i: (0, i, 0))
    acc = lambda w: pl.BlockSpec((1, w), lambda i: (0, 0))
    return pl.pallas_call(
        body, name="mla_pre_bwd", grid=(l // t,),
        in_specs=[_rows(t, LAT_W), _rows(t, 1), _resident((1, 128)), _resident((1, 128)), _resident((1, Q_LORA)),
                  _resident((1, KV_LORA)), _resident((1, HEAD_PAD)), _resident((1, HEAD_PAD)),
                  _resident((Q_LORA, N_HEADS * HEAD_PAD)), _member_block("w_kv_b"),
                  heads(HEAD_PAD), heads(HEAD_PAD), heads(V_HEAD), ANY],
        out_specs=[_rows(t, LAT_W), _rows(t, Q_LORA), _rows(t, N_HEADS * HEAD_PAD), _rows(t, KV_LORA), _rows(t, N_HEADS * 256),
                   acc(Q_LORA), acc(KV_LORA), acc(HEAD_PAD), acc(HEAD_PAD)],
        out_shape=[jax.ShapeDtypeStruct((l, LAT_W), BF16), jax.ShapeDtypeStruct((l, Q_LORA), BF16),
                   jax.ShapeDtypeStruct((l, N_HEADS * HEAD_PAD), BF16), jax.ShapeDtypeStruct((l, KV_LORA), BF16),
                   jax.ShapeDtypeStruct((l, N_HEADS * 256), BF16), jax.ShapeDtypeStruct((1, Q_LORA), F32),
                   jax.ShapeDtypeStruct((1, KV_LORA), F32), jax.ShapeDtypeStruct((1, HEAD_PAD), F32),
                   jax.ShapeDtypeStruct((1, HEAD_PAD), F32)],
        compiler_params=_cparams("arbitrary"),
    )(lat, pos, invf, sgn, gqa, gkva, gq, gk, w_qb_p, w_kvb, dq, dk, dv, token)


def _causal(s, transposed):
    row = lax.broadcasted_iota(jnp.int32, s.shape, 0)
    col = lax.broadcasted_iota(jnp.int32, s.shape, 1)
    keep = (row <= col) if transposed else (col <= row)
    return jnp.where(keep, s, -jnp.inf)


def _as_row(col):
    n = col.shape[0]
    row = lax.broadcasted_iota(jnp.int32, (n, n), 0)
    lane = lax.broadcasted_iota(jnp.int32, (n, n), 1)
    return jnp.sum(jnp.where(row == lane, col, 0.0), axis=0, keepdims=True)


def _attn_fwd(q, k, v, tq):
    l = q.shape[1]

    hb = 2

    def body(q_ref, k_ref, v_ref, o_ref, lse_ref):
        qi = pl.program_id(1)
        qs = [q_ref[a] for a in range(hb)]

        def step(kb, carry, masked):
            rows = pl.ds(pl.multiple_of(kb * tq, tq), tq)
            out = []
            for a, (m, den, acc) in enumerate(carry):
                s = _mm_nt(qs[a], k_ref[a, rows, :])
                if masked:
                    s = _causal(s, False)
                m_new = jnp.maximum(m, jnp.max(s, axis=-1, keepdims=True))
                alpha = jnp.exp(m - m_new)
                p = jnp.exp(s - m_new)
                den = alpha * den + jnp.sum(p, axis=-1, keepdims=True)
                acc = alpha * acc + _mm(p, v_ref[a, rows, :])
                out.append((m_new, den, acc))
            return tuple(out)

        init = tuple((jnp.full((tq, 1), -jnp.inf, F32), jnp.zeros((tq, 1), F32), jnp.zeros((tq, V_HEAD), F32))
                     for _ in range(hb))
        carry = lax.fori_loop(0, qi, lambda kb, c: step(kb, c, False), init)
        for a, (m, den, acc) in enumerate(step(qi, carry, True)):
            o_ref[:, V_HEAD * a:V_HEAD * (a + 1)] = acc / den
            lse_ref[a, 0] = _as_row(m + jnp.log(den))

    return pl.pallas_call(
        body, name="attn_fwd", grid=(N_HEADS // hb, l // tq),
        in_specs=[pl.BlockSpec((hb, tq, HEAD_PAD), lambda h, i: (h, i, 0)), pl.BlockSpec((hb, l, HEAD_PAD), lambda h, i: (h, 0, 0)),
                  pl.BlockSpec((hb, l, V_HEAD), lambda h, i: (h, 0, 0))],
        out_specs=[pl.BlockSpec((tq, hb * V_HEAD), lambda h, i: (i, h)), pl.BlockSpec((hb, 1, 1, tq), lambda h, i: (h, i, 0, 0))],
        out_shape=[jax.ShapeDtypeStruct((l, N_HEADS * V_HEAD), F32), jax.ShapeDtypeStruct((N_HEADS, l // tq, 1, tq), F32)],
        compiler_params=_cparams("parallel", "arbitrary"),
    )(q, k, v)


def _attn_bwd(q, k, v, o, do, lse_t, tq, token):
    l = q.shape[1]
    nq = l // tq

    hb = 1

    def body(q_ref, k_ref, v_ref, o_ref, do_ref, lse_ref, token_ref, dq_ref, dk_ref, dv_ref):
        ki = pl.program_id(1)

        @pl.when(ki == 0)
        def _():
            dq_ref[...] = jnp.zeros_like(dq_ref)

        kblks = [k_ref[a] for a in range(hb)]
        vblks = [v_ref[a] for a in range(hb)]
        ones = jnp.ones((8, V_HEAD), BF16)

        def step(qb, carry, masked):
            rows = pl.ds(pl.multiple_of(qb * tq, tq), tq)
            out = []
            for a, (dk, dv) in enumerate(carry):
                cols = slice(V_HEAD * a, V_HEAD * (a + 1))
                qblk = q_ref[a, rows, :]
                dov = do_ref[rows, cols]
                dob = dov.astype(BF16)
                delta = sum(_mm_nt(ones, part) for part in _three_bf16(dov * o_ref[rows, cols]))[0:1, :]
                st = _mm_nt(kblks[a], qblk)
                if masked:
                    st = _causal(st, True)
                pt = jnp.exp(st - lse_ref[a, qb])
                dv = dv + _mm(pt, dob)
                dst = (pt * (_mm_nt(vblks[a], dob) - delta)).astype(BF16)
                dk = dk + _mm(dst, qblk)
                dq_ref[a, rows, :] += _mm_tn(dst, kblks[a])
                out.append((dk, dv))
            return tuple(out)

        init = tuple((jnp.zeros((tq, HEAD_PAD), F32), jnp.zeros((tq, V_HEAD), F32)) for _ in range(hb))
        carry = lax.fori_loop(ki + 1, nq, lambda qb, c: step(qb, c, False), step(ki, init, True))
        for a, (dk, dv) in enumerate(carry):
            dk_ref[a] = dk
            dv_ref[a] = dv

    return pl.pallas_call(
        body, name="attn_bwd", grid=(N_HEADS // hb, nq),
        in_specs=[pl.BlockSpec((hb, l, HEAD_PAD), lambda h, i: (h, 0, 0)), pl.BlockSpec((hb, tq, HEAD_PAD), lambda h, i: (h, i, 0)),
                  pl.BlockSpec((hb, tq, V_HEAD), lambda h, i: (h, i, 0)), pl.BlockSpec((l, hb * V_HEAD), lambda h, i: (0, h)),
                  pl.BlockSpec((l, hb * V_HEAD), lambda h, i: (0, h)), pl.BlockSpec((hb, nq, 1, tq), lambda h, i: (h, 0, 0, 0)), ANY],
        out_specs=[pl.BlockSpec((hb, l, HEAD_PAD), lambda h, i: (h, 0, 0)), pl.BlockSpec((hb, tq, HEAD_PAD), lambda h, i: (h, i, 0)),
                   pl.BlockSpec((hb, tq, V_HEAD), lambda h, i: (h, i, 0))],
        out_shape=[jax.ShapeDtypeStruct((N_HEADS, l, HEAD_PAD), F32), jax.ShapeDtypeStruct((N_HEADS, l, HEAD_PAD), F32),
                   jax.ShapeDtypeStruct((N_HEADS, l, V_HEAD), F32)],
        compiler_params=_cparams("parallel", "arbitrary"),
    )(q, k, v, o, do, lse_t, token)


def _row_shards_mm(a, w_ref):
    a = a.astype(BF16)
    return sum(_mm(a[:, 256 * j:256 * (j + 1)], w_ref[j]) for j in range(4))


def _row_shards_mm_nt(a, w_ref):
    a = a.astype(BF16)
    return jnp.concatenate([_mm_nt(a, w_ref[j]) for j in range(4)], axis=-1)


def _merge_fwd(attn, y_ssm, gs, gm, x, grp_a, t):
    l = x.shape[0]

    def body(attn_ref, ys_ref, gs_ref, gm_ref, x_ref, wo_ref, wout_ref, ym_ref, mixed_ref, h_ref):
        y_mla = _row_shards_mm(attn_ref[...], wo_ref)
        ym_ref[...] = y_mla.astype(BF16)
        mixed = (_sigmoid(gs_ref[...].astype(F32)) * ys_ref[...].astype(F32)
                 + _sigmoid(gm_ref[...].astype(F32)) * y_mla).astype(BF16)
        mixed_ref[...] = mixed
        h_ref[...] = x_ref[...] + _row_shards_mm(mixed, wout_ref)

    r = lambda: _rows(t, D_MODEL)
    return pl.pallas_call(
        body, name="merge_fwd", grid=(l // t,),
        in_specs=[r(), r(), r(), r(), r(), _member_block("w_o_mla"), _member_block("w_out")],
        out_specs=[r(), r(), r()],
        out_shape=[jax.ShapeDtypeStruct((l, D_MODEL), BF16), jax.ShapeDtypeStruct((l, D_MODEL), BF16),
                   jax.ShapeDtypeStruct((l, D_MODEL), F32)],
        compiler_params=_cparams("parallel"),
    )(attn, y_ssm, gs, gm, x, grp_a, grp_a)


def _merge_bwd(dh, y_ssm, y_mla, gs, gm, grp_a, t):
    l = dh.shape[0]

    def body(dh_ref, ys_ref, ym_ref, gs_ref, gm_ref, wo_ref, wout_ref, dys_ref, dym_ref, dgs_ref, dgm_ref, dattn_ref):
        dmixed = _row_shards_mm_nt(dh_ref[...], wout_ref)
        sg = _sigmoid(gs_ref[...].astype(F32))
        sm = _sigmoid(gm_ref[...].astype(F32))
        dys_ref[...] = (dmixed * sg).astype(BF16)
        dgs_ref[...] = (dmixed * ys_ref[...].astype(F32) * sg * (1.0 - sg)).astype(BF16)
        dym = (dmixed * sm).astype(BF16)
        dym_ref[...] = dym
        dgm_ref[...] = (dmixed * ym_ref[...].astype(F32) * sm * (1.0 - sm)).astype(BF16)
        dattn_ref[...] = _row_shards_mm_nt(dym, wo_ref)

    r = lambda: _rows(t, D_MODEL)
    bf = jax.ShapeDtypeStruct((l, D_MODEL), BF16)
    return pl.pallas_call(
        body, name="merge_bwd", grid=(l // t,),
        in_specs=[r(), r(), r(), r(), r(), _member_block("w_o_mla"), _member_block("w_out")],
        out_specs=[r(), r(), r(), r(), r()],
        out_shape=[bf, bf, bf, bf, jax.ShapeDtypeStruct((l, D_MODEL), F32)],
        compiler_params=_cparams("parallel"),
    )(dh, y_ssm, y_mla, gs, gm, grp_a, grp_a)


def _mlp_fwd_bwd(h, tgt, g2, grp_a, t):
    l = h.shape[0]

    def body(h_ref, tgt_ref, g_ref, wu_ref, wd_ref, dh_ref, hn_ref, da_ref, hid_ref, dout_ref, loss_ref, dg_ref):
        first = pl.program_id(0) == 0
        hv = h_ref[...]
        g = g_ref[...]
        hn = _rms_fwd(hv, g, D_MODEL).astype(BF16)
        hn_ref[...] = hn
        out = hv
        relus = []
        for s in range(4):
            cols = slice(1024 * s, 1024 * (s + 1))
            relu = jnp.maximum(_mm(hn, wu_ref[s]), 0.0)
            relus.append(relu)
            hid = (relu * relu).astype(BF16)
            hid_ref[:, cols] = hid
            out = out + _mm(hid, wd_ref[s])
        err = out - tgt_ref[...]
        _accumulate(loss_ref, jnp.full((8, 128), jnp.sum(err * err) * (0.5 / D_MODEL), F32), first)
        dout = err * (1.0 / D_MODEL)
        doutb = dout.astype(BF16)
        dout_ref[...] = doutb
        dhn = jnp.zeros_like(hv)
        for s in range(4):
            da = (_mm_nt(doutb, wd_ref[s]) * (2.0 * relus[s])).astype(BF16)
            da_ref[:, 1024 * s:1024 * (s + 1)] = da
            dhn = dhn + _mm_nt(da, wu_ref[s])
        dx, dg_rows = _rms_bwd(hv, g, dhn, D_MODEL)
        dh_ref[...] = dout + dx
        _accumulate(dg_ref, _colsum(dg_rows), first)

    r = lambda w: _rows(t, w)
    return pl.pallas_call(
        body, name="mlp_fwd_bwd", grid=(l // t,),
        in_specs=[r(D_MODEL), r(D_MODEL), _resident((1, D_MODEL)), _member_block("w_up"), _member_block("w_down")],
        out_specs=[r(D_MODEL), r(D_MODEL), r(D_FF), r(D_FF), r(D_MODEL), pl.BlockSpec((8, 128), lambda i: (0, 0)),
                   pl.BlockSpec((1, D_MODEL), lambda i: (0, 0))],
        out_shape=[jax.ShapeDtypeStruct((l, D_MODEL), F32), jax.ShapeDtypeStruct((l, D_MODEL), BF16),
                   jax.ShapeDtypeStruct((l, D_FF), BF16), jax.ShapeDtypeStruct((l, D_FF), BF16),
                   jax.ShapeDtypeStruct((l, D_MODEL), BF16), jax.ShapeDtypeStruct((8, 128), F32),
                   jax.ShapeDtypeStruct((1, D_MODEL), F32)],
        compiler_params=_cparams("arbitrary"),
    )(h, tgt, g2, grp_a, grp_a)


def _wgrad(a, b, name):
    l, m = a.shape
    n = b.shape[1]
    bm = m if m <= 512 else 512
    bl = min(l, 2048 if n <= 1024 else 1024)

    def body(a_ref, b_ref, o_ref):
        _accumulate(o_ref, _mm_tn(a_ref[...], b_ref[...]), pl.program_id(1) == 0)

    return pl.pallas_call(
        body, name=name, grid=(m // bm, l // bl),
        in_specs=[pl.BlockSpec((bl, bm), lambda i, j: (j, i)), pl.BlockSpec((bl, n), lambda i, j: (j, 0))],
        out_specs=pl.BlockSpec((bm, n), lambda i, j: (i, 0)),
        out_shape=jax.ShapeDtypeStruct((m, n), F32),
        compiler_params=_cparams("parallel", "arbitrary"),
    )(a, b)


def _wgrad_into(a, b, member, cut, dest=None):
    group, off, rs, cs = _place_in_group(member)
    l = a.shape[0]
    bm = min(rs, 512)
    bl = min(l, 2048)
    nb = rs // bm
    if cut == "row":
        a_spec = pl.BlockSpec((bl, bm), lambda j, i, k: (k, j * nb + i))
        b_spec = pl.BlockSpec((bl, cs), lambda j, i, k: (k, 0))
    else:
        a_spec = pl.BlockSpec((bl, bm), lambda j, i, k: (k, i))
        b_spec = pl.BlockSpec((bl, cs), lambda j, i, k: (k, j))

    def body(a_ref, b_ref, *rest):
        o_ref = rest[-1]
        part = _mm_tn(a_ref[...], b_ref[...])

        @pl.when(pl.program_id(2) == 0)
        def _():
            o_ref[0] = part

        @pl.when(pl.program_id(2) != 0)
        def _():
            o_ref[0] += part

    operands, in_specs, aliases = [a, b], [a_spec, b_spec], {}
    if dest is not None:
        operands.append(dest)
        in_specs.append(ANY)
        aliases = {2: 0}
    return pl.pallas_call(
        body, name="wgrad_" + member, grid=(4, nb, l // bl), in_specs=in_specs,
        out_specs=pl.BlockSpec((1, bm, cs), lambda j, i, k: (j, off // bm + i, 0)),
        out_shape=jax.ShapeDtypeStruct((4, _group_rows(group), cs), F32), input_output_aliases=aliases,
        compiler_params=_cparams("parallel", "parallel", "arbitrary"),
    )(*operands)


def _adamw(w, g, m, v, name, g_off, token):
    r, c = w.shape
    br = r
    for cand in (256, 128, 64, 32, 16, 8):
        if r % cand == 0 and g_off % cand == 0:
            br = cand
            break

    def body(w_ref, g_ref, m_ref, v_ref, token_ref, go_ref, d_ref, nm_ref, nv_ref):
        gv = g_ref[...]
        go_ref[...] = gv
        nm = ADAM_B1 * m_ref[...] + (1.0 - ADAM_B1) * gv
        nv = ADAM_B2 * v_ref[...] + (1.0 - ADAM_B2) * (gv * gv)
        m_hat = nm / (1.0 - ADAM_B1 ** ADAM_STEP)
        v_hat = nv / (1.0 - ADAM_B2 ** ADAM_STEP)
        d_ref[...] = -ADAM_LR * (m_hat / (jnp.sqrt(v_hat) + ADAM_EPS) + ADAM_WD * w_ref[...])
        nm_ref[...] = nm
        nv_ref[...] = nv

    spec = lambda: pl.BlockSpec((br, c), lambda i: (i, 0))
    g_spec = pl.BlockSpec((br, c), lambda i: (g_off // br + i, 0))
    shp = jax.ShapeDtypeStruct((r, c), F32)
    return pl.pallas_call(
        body, name=name, grid=(r // br,), in_specs=[spec(), g_spec, spec(), spec(), ANY],
        out_specs=[spec(), spec(), spec(), spec()], out_shape=[shp, shp, shp, shp], compiler_params=_cparams("parallel"),
    )(w, g, m, v, token)


def _place():
    return lax.axis_index("x"), lax.axis_index("y"), lax.axis_index("c")


def _other_chips(x, y):
    return [(1 - x, y), (x, 1 - y), (1 - x, 1 - y)]


ANY = pl.BlockSpec(memory_space=pl.ANY)


def _gather_weights(bufs):
    n = len(bufs)

    def body(*refs):
        outs, send_sems, recv_sems = refs[n:2 * n], refs[2 * n], refs[2 * n + 1]
        x, y, c = _place()
        chips = _other_chips(x, y)

        def part(g, px, py, pc):
            half = outs[g].shape[1] // 2
            return outs[g].at[2 * px + py, pl.ds(pl.multiple_of(pc * half, 16), half), :]

        def copy(k, src, dst, to):
            return pltpu.make_async_remote_copy(src_ref=src, dst_ref=dst, send_sem=send_sems.at[k], recv_sem=recv_sems.at[k],
                                                device_id=to, device_id_type=MESH)

        first = [copy(6 * g + j, part(g, x, y, c), part(g, x, y, c), (*chip, c)) for g in range(n) for j, chip in enumerate(chips)]
        for cp in first:
            cp.start()
        passed = []
        for g in range(n):
            for j, chip in enumerate(chips):
                landed = part(g, *chip, c)
                copy(6 * g + j, landed, landed, (x, y, c)).wait_recv()
                passed.append(copy(6 * g + 3 + j, landed, landed, (x, y, 1 - c)))
                passed[-1].start()
        for g in range(n):
            for j, chip in enumerate(chips):
                other = part(g, *chip, 1 - c)
                copy(6 * g + 3 + j, other, other, (x, y, c)).wait_recv()
        for cp in first + passed:
            cp.wait_send()

    return pl.pallas_call(
        body, name="gather_weights", in_specs=[ANY] * n, out_specs=[ANY] * n,
        out_shape=[jax.ShapeDtypeStruct(b.shape, b.dtype) for b in bufs], input_output_aliases={g: g for g in range(n)},
        scratch_shapes=[pltpu.SemaphoreType.DMA((6 * n,)), pltpu.SemaphoreType.DMA((6 * n,))],
    )(*bufs)


def _cast_shards(shards, group, place):
    width, members = GROUPS[group]
    rows = _group_rows(group)

    def body(place_ref, *refs):
        out = refs[-1]
        off = 0
        for ref, (_, r) in zip(refs[:-1], members):
            out[0, off:off + r, :] = ref[...].astype(BF16)
            off += r

    grid_spec = pltpu.PrefetchScalarGridSpec(
        num_scalar_prefetch=1, grid=(1,),
        in_specs=[pl.BlockSpec((r, width), lambda i, p: (0, 0)) for _, r in members],
        out_specs=pl.BlockSpec((1, rows, width), lambda i, p: (p[0], 0, 0)))
    return pl.pallas_call(
        body, name="cast_shards_" + group, grid_spec=grid_spec, out_shape=jax.ShapeDtypeStruct((4, rows, width), BF16),
        compiler_params=_cparams("arbitrary"),
    )(place, *[shards[name] for name, _ in members])


def _block_rows(h):
    return next(cand for cand in (256, 192, 128, 64, 32, 16) if h % cand == 0)


def _add_pair(buf, got, place, name):
    n, h, w = got.shape
    bh = _block_rows(h)
    nb = h // bh

    def body(place_ref, a_ref, b_ref, s_ref, sb_ref):
        s = a_ref[...] + b_ref[...]
        s_ref[...] = s
        sb_ref[...] = s.astype(BF16)

    spec = lambda: pl.BlockSpec((1, bh, w), lambda j, i, p: (j, i, 0))
    grid_spec = pltpu.PrefetchScalarGridSpec(
        num_scalar_prefetch=1, grid=(n, nb),
        in_specs=[pl.BlockSpec((1, bh, w), lambda j, i, p: (j, p[1] * nb + i, 0)), spec()], out_specs=[spec(), spec()])
    return pl.pallas_call(
        body, name=name, grid_spec=grid_spec,
        out_shape=[jax.ShapeDtypeStruct(got.shape, F32), jax.ShapeDtypeStruct(got.shape, BF16)],
        compiler_params=_cparams("parallel", "parallel"),
    )(place, buf, got)


def _add_received(pair, got, place, name):
    _, h, w = pair.shape
    bh = _block_rows(h)
    nb = h // bh

    def body(place_ref, own_ref, got_ref, o_ref):
        o_ref[...] = ((own_ref[0] + got_ref[0].astype(F32)) + got_ref[1].astype(F32)) + got_ref[2].astype(F32)

    grid_spec = pltpu.PrefetchScalarGridSpec(
        num_scalar_prefetch=1, grid=(nb,),
        in_specs=[pl.BlockSpec((1, bh, w), lambda i, p: (p[0], i, 0)), pl.BlockSpec((3, bh, w), lambda i, p: (0, i, 0))],
        out_specs=pl.BlockSpec((bh, w), lambda i, p: (p[1] * nb + i, 0)))
    return pl.pallas_call(
        body, name=name, grid_spec=grid_spec, out_shape=jax.ShapeDtypeStruct((2 * h, w), F32),
        compiler_params=_cparams("parallel"),
    )(place, pair, got)


def _swap_reduced_halves(bufs):
    n = len(bufs)

    def body(*refs):
        outs, send_sems, recv_sems = refs[n:2 * n], refs[2 * n], refs[2 * n + 1]
        x, y, c = _place()
        copies = []
        for g in range(n):
            half = outs[g].shape[0] // 2
            own = outs[g].at[pl.ds(pl.multiple_of(c * half, 8), half), :]
            copies.append(pltpu.make_async_remote_copy(src_ref=own, dst_ref=own, send_sem=send_sems.at[g],
                                                       recv_sem=recv_sems.at[g], device_id=(x, y, 1 - c), device_id_type=MESH))
        for cp in copies:
            cp.start()
        for g in range(n):
            half = outs[g].shape[0] // 2
            other = outs[g].at[pl.ds(pl.multiple_of((1 - c) * half, 8), half), :]
            pltpu.make_async_remote_copy(src_ref=other, dst_ref=other, send_sem=send_sems.at[g], recv_sem=recv_sems.at[g],
                                         device_id=(x, y, 1 - c), device_id_type=MESH).wait_recv()
        for cp in copies:
            cp.wait_send()

    return pl.pallas_call(
        body, name="swap_reduced_halves", in_specs=[ANY] * n, out_specs=[ANY] * n,
        out_shape=[jax.ShapeDtypeStruct(b.shape, b.dtype) for b in bufs], input_output_aliases={g: g for g in range(n)},
        scratch_shapes=[pltpu.SemaphoreType.DMA((n,)), pltpu.SemaphoreType.DMA((n,))],
    )(*bufs)


HBM = pl.BlockSpec(memory_space=pltpu.HBM)
SEM = pl.BlockSpec(memory_space=pltpu.SEMAPHORE)


def _copies_start(name, bufs, n_copies, plan, after=None):
    n = len(bufs)
    extra = [] if after is None else [after]

    def body(*refs):
        sems = refs[n + len(extra):n + len(extra) + 2 * n_copies]
        x, y, c = _place()
        for i, (src, dst, dev) in enumerate(plan(refs[:n], x, y, c)):
            pltpu.make_async_remote_copy(src_ref=src, dst_ref=dst, send_sem=sems[i], recv_sem=sems[n_copies + i],
                                         device_id=dev, device_id_type=MESH).start()
        token = refs[-1]
        token[...] = jnp.zeros_like(token)

    out = pl.pallas_call(
        body, name=name,
        out_shape=[pltpu.SemaphoreType.DMA(())] * (2 * n_copies) + [pltpu.HBM(b.shape, b.dtype) for b in bufs]
        + [jax.ShapeDtypeStruct((8, 128), F32)],
        in_specs=[HBM] * n + [ANY] * len(extra),
        out_specs=[SEM] * (2 * n_copies) + [HBM] * n + [pl.BlockSpec(memory_space=pltpu.VMEM)],
        input_output_aliases={i: 2 * n_copies + i for i in range(n)},
        compiler_params=pltpu.CompilerParams(has_side_effects=pltpu.SideEffectType.DATAFLOW_SIDE_EFFECTING),
    )(*[pltpu.with_memory_space_constraint(b, pltpu.HBM) for b in bufs], *extra)
    return list(out[:2 * n_copies]), list(out[2 * n_copies:-1]), out[-1]


def _copies_wait(name, bufs, sems, after, plan):
    n = len(bufs)
    k = len(sems) // 2

    def body(*refs):
        sem_refs = refs[n:n + 2 * k]
        x, y, c = _place()
        for i, (sent, landed, dev) in enumerate(plan(refs[:n], x, y, c)):
            cp = pltpu.make_async_remote_copy(src_ref=sent, dst_ref=landed, send_sem=sem_refs[i], recv_sem=sem_refs[k + i],
                                              device_id=dev, device_id_type=MESH)
            cp.wait_send()
            cp.wait_recv()

    return pl.pallas_call(
        body, name=name, out_shape=[pltpu.HBM(b.shape, b.dtype) for b in bufs],
        in_specs=[HBM] * n + [SEM] * (2 * k) + [ANY], out_specs=[HBM] * n, input_output_aliases={i: i for i in range(n)},
        compiler_params=pltpu.CompilerParams(has_side_effects=pltpu.SideEffectType.DATAFLOW_SIDE_EFFECTING),
    )(*bufs, *sems, after)


def _row_half(ref, which, axis):
    half = ref.shape[axis] // 2
    rows = pl.ds(pl.multiple_of(which * half, 8), half)
    return ref.at[rows, :] if axis == 0 else ref.at[:, rows, :]


class _SplitGather:
    def __init__(self, own, after):
        self.n = len(own)
        self.state = _copies_start("gather_start", own, 3 * self.n, self._sent, after)

    @staticmethod
    def _sent(refs, x, y, c):
        return [(w.at[2 * x + y], w.at[2 * x + y], (px, py, c)) for w in refs for px, py in _other_chips(x, y)]

    @staticmethod
    def _landed(refs, x, y, c):
        return [(w.at[2 * x + y], w.at[2 * px + py], (px, py, c)) for w in refs for px, py in _other_chips(x, y)]

    def token(self):
        return self.state[2]

    def wait(self, which, name, after):
        sems, bufs, _ = self.state
        k = 3 * self.n
        mine = [sems[3 * i + j] for i in which for j in range(3)] + [sems[k + 3 * i + j] for i in which for j in range(3)]
        return _copies_wait(name, [bufs[i] for i in which], mine, after, self._landed)


class _SplitReduction:
    def __init__(self, tag, groups, place):
        self.tag, self.groups, self.place = tag, groups, place

    def start_pair(self, bufs):
        n = len(bufs)
        lands = [lax.empty((4, b.shape[1] // 2, b.shape[2]), F32) for b in bufs]
        plan = lambda refs, x, y, c: [(_row_half(refs[i], 1 - c, 1), refs[n + i], (x, y, 1 - c)) for i in range(n)]
        self._pair = (_copies_start("pair_%s_start" % self.tag, bufs + lands, n, plan), plan, n)
        return self._pair[0][2]

    def pair_done_start_scatter(self, after):
        (sems, bufs, _), plan, n = self._pair
        out = _copies_wait("pair_%s_wait" % self.tag, bufs, sems, after, plan)
        pairs = [_add_pair(out[i], out[n + i], self.place, "add_pair_" + g) for i, g in enumerate(self.groups)]
        self._pair_f32 = [p[0] for p in pairs]
        lands = [lax.empty((3,) + p[1].shape[1:], BF16) for p in pairs]
        plan = lambda refs, x, y, c: [(refs[i].at[2 * px + py], refs[n + i].at[j], (px, py, c))
                                      for i in range(n) for j, (px, py) in enumerate(_other_chips(x, y))]
        self._scatter = (_copies_start("scatter_%s_start" % self.tag, [p[1] for p in pairs] + lands, 3 * n, plan), plan, n)
        return self._scatter[0][2]

    def scatter_done(self, after):
        (sems, bufs, _), plan, n = self._scatter
        out = _copies_wait("scatter_%s_wait" % self.tag, bufs, sems, after, plan)
        return [_add_received(self._pair_f32[i], out[n + i], self.place, "add_received_" + g)
                for i, g in enumerate(self.groups)]

    def start_join(self, halves):
        n = len(halves)
        sent = lambda refs, x, y, c: [(_row_half(r, c, 0), _row_half(r, c, 0), (x, y, 1 - c)) for r in refs]
        landed = lambda refs, x, y, c: [(_row_half(r, c, 0), _row_half(r, 1 - c, 0), (x, y, 1 - c)) for r in refs]
        self._join = (_copies_start("join_%s_start" % self.tag, halves, n, sent), landed)
        return self._join[0][2]

    def join_done(self, after):
        (sems, bufs, _), landed = self._join
        return _copies_wait("join_%s_wait" % self.tag, bufs, sems, after, landed)


def _pair_sum_small(mine):
    rows, w = mine.shape

    def body(in_ref, out_ref, sibling, send_sem, recv_sem):
        x, y, c = _place()
        swap = pltpu.make_async_remote_copy(src_ref=in_ref, dst_ref=sibling, send_sem=send_sem, recv_sem=recv_sem,
                                            device_id=(x, y, 1 - c), device_id_type=MESH)
        swap.start()
        swap.wait()
        out_ref[...] = in_ref[...] + sibling[...]

    return pl.pallas_call(
        body, name="pair_sum_small", out_shape=jax.ShapeDtypeStruct((rows, w), F32),
        in_specs=[pl.BlockSpec(memory_space=pltpu.VMEM)], out_specs=pl.BlockSpec(memory_space=pltpu.VMEM),
        scratch_shapes=[pltpu.VMEM((rows, w), F32), pltpu.SemaphoreType.DMA, pltpu.SemaphoreType.DMA],
        compiler_params=pltpu.CompilerParams(vmem_limit_bytes=VMEM_LIMIT_V7X),
    )(mine)


class _SplitChipSum:
    def __init__(self, pair, place):
        self.place = place
        slots = lax.empty((4,) + pair.shape, F32)
        sent = lambda refs, x, y, c: [(refs[0], refs[1].at[2 * x + y], (px, py, c)) for px, py in _other_chips(x, y)]
        self.landed = lambda refs, x, y, c: [(refs[0], refs[1].at[2 * px + py], (px, py, c)) for px, py in _other_chips(x, y)]
        self.state = _copies_start("small_sum_start", [pair, slots], 3, sent)

    def token(self):
        return self.state[2]

    def done(self, after):
        sems, bufs, _ = self.state
        pair, slots = _copies_wait("small_sum_wait", bufs, sems, after, self.landed)
        rows, w = pair.shape

        def body(place_ref, pair_ref, slots_ref, out_ref):
            for j in range(4):
                own = place_ref[0] == j

                @pl.when(own)
                def _():
                    out_ref[...] = pair_ref[...] if j == 0 else out_ref[...] + pair_ref[...]

                @pl.when(jnp.logical_not(own))
                def _():
                    out_ref[...] = slots_ref[j] if j == 0 else out_ref[...] + slots_ref[j]

        grid_spec = pltpu.PrefetchScalarGridSpec(
            num_scalar_prefetch=1, grid=(1,),
            in_specs=[pl.BlockSpec((rows, w), lambda i, p: (0, 0)), pl.BlockSpec((4, rows, w), lambda i, p: (0, 0, 0))],
            out_specs=pl.BlockSpec((rows, w), lambda i, p: (0, 0)))
        return pl.pallas_call(
            body, name="small_sum_add", grid_spec=grid_spec, out_shape=jax.ShapeDtypeStruct((rows, w), F32),
            compiler_params=_cparams("arbitrary"),
        )(self.place, pair, slots)


def _join_column_shards(g):
    return jnp.transpose(g, (1, 0, 2)).reshape(g.shape[1], 4 * g.shape[2])


def _split_column_shards(w):
    r = w.shape[0]
    return jnp.transpose(w.reshape(r, 4, w.shape[1] // 4), (1, 0, 2))


def _small_rows(shape):
    return -(-int(np.prod(shape)) // 1024)


def _pack_small(vals):
    segs = []
    for name, shape in SMALL_WEIGHTS:
        flat = vals[name].reshape(-1)
        segs.append(jnp.pad(flat, (0, _small_rows(shape) * 1024 - flat.shape[0])))
    total = sum(s.shape[0] for s in segs) // 1024
    segs.append(jnp.zeros((-total % 8 * 1024,), F32))
    return jnp.concatenate(segs).reshape(-1, 1024)


def _unpack_small(packed):
    out, off = {}, 0
    for name, shape in SMALL_WEIGHTS:
        rows = _small_rows(shape)
        out[name] = packed[off:off + rows].reshape(-1)[:int(np.prod(shape))].reshape(shape)
        off += rows
    return out


W_IN_SHARD = D_IN // 4
W_IN_GAP = 1216


def _pad_w_in(g):
    cut = W_IN_GAP - W_IN_SHARD
    return jnp.concatenate([g[0], g[1][:, :cut], jnp.zeros((g.shape[1], D_IN_PAD - D_IN), g.dtype), g[1][:, cut:], g[2], g[3]],
                           axis=1)


def _unpad_w_in(g):
    skip = D_IN_PAD - D_IN
    second = jnp.concatenate([g[:, W_IN_SHARD:W_IN_GAP], g[:, W_IN_GAP + skip:2 * W_IN_SHARD + skip]], axis=1)
    return jnp.stack([g[:, :W_IN_SHARD], second, g[:, 2 * W_IN_SHARD + skip:3 * W_IN_SHARD + skip],
                      g[:, 3 * W_IN_SHARD + skip:]])


def _pad_heads(w):
    r = w.shape[0]
    return jnp.pad(w.reshape(r, N_HEADS, QK_HEAD), ((0, 0), (0, 0), (0, HEAD_PAD - QK_HEAD))).reshape(r, N_HEADS * HEAD_PAD)


def _unpad_heads(g):
    r = g.shape[0]
    return g.reshape(r, N_HEADS, HEAD_PAD)[:, :, :QK_HEAD].reshape(r, N_HEADS * QK_HEAD)


def _local_step(x, positions, tgt, grp_b, small, gather, red_a, red_rest):
    l = x.shape[0]
    t = min(l, 512)
    t_mlp = min(l, 256)
    tq = min(l, 1024)
    tc = min(l, 256)
    row = lambda v: v.reshape(1, -1).astype(F32)

    w_in_p = _pad_w_in(grp_b)
    g1, g2 = row(small["norm_mix"]), row(small["norm_mlp"])
    gqa, gkva = row(small["q_a_norm"]), row(small["kv_a_norm"])
    gq = jnp.pad(row(small["q_norm"]), ((0, 0), (0, HEAD_PAD - QK_HEAD)))
    gk = jnp.pad(row(small["k_norm"]), ((0, 0), (0, HEAD_PAD - QK_HEAD)))
    half = QK_ROPE // 2
    inv_freq = ROPE_THETA ** (-jnp.arange(half, dtype=F32) / half)
    invf = jnp.concatenate([inv_freq, inv_freq, jnp.zeros((64,), F32)]).reshape(1, 128)
    sgn = jnp.concatenate([-jnp.ones((half,), F32), jnp.ones((half,), F32), jnp.zeros((64,), F32)]).reshape(1, 128)
    pos = positions.reshape(l, 1)

    a_re, a_im = small["ssm_a_re"], small["ssm_a_im"]
    log_dt = small["ssm_log_dt"].reshape(SSM_GROUPS, 1)
    to_gcp = lambda b: jnp.transpose(b, (0, 2, 1)).reshape(SSM_WIDTH, SSM_STATE)
    from_gcp = lambda b: jnp.transpose(b.reshape(SSM_GROUPS, SSM_GROUP_CH, SSM_STATE), (0, 2, 1))
    b_re, b_im = to_gcp(small["ssm_b_re"]), to_gcp(small["ssm_b_im"])
    c_re, c_im = small["ssm_c_re"].reshape(SSM_WIDTH, SSM_STATE), small["ssm_c_im"].reshape(SSM_WIDTH, SSM_STATE)
    wb, wc, tabs_fwd, tabs_rev = _ssm_param_fwd(a_re, a_im, log_dt, b_re, b_im, c_re, c_im)
    dskip = row(small["ssm_d"])
    b_glu = row(small["b_glu"])

    u, lat, gs, gm = _in_proj_fwd(x, g1, w_in_p, t, gather.token())
    grp_c, grp_d, grp_e = gather.wait([0, 1, 2], "gather_cde_wait", u)
    w_qb_p = _pad_heads(_join_column_shards(grp_c))
    xr, xi, y, y_ssm = _ssm_fwd(u, wb, wc, tabs_fwd, dskip, grp_d, b_glu, grp_e, tc)
    q, k, v = _mla_pre_fwd(lat, pos, invf, sgn, gqa, gkva, gq, gk, w_qb_p, grp_d, t)
    attn, lse = _attn_fwd(q, k, v, tq)
    (grp_a,) = gather.wait([3], "gather_a_wait", attn)
    y_mla, mixed, h = _merge_fwd(attn, y_ssm, gs, gm, x, grp_a, t)
    dh, hn, da, hid, dout, loss_blk, g_norm_mlp = _mlp_fwd_bwd(h, tgt, g2, grp_a, t_mlp)

    ga = _wgrad_into(hn, da, "w_up", "col", _wgrad_into(hid, dout, "w_down", "row"))
    dys, dym, dgs, dgm, dattn = _merge_bwd(dh, y_ssm, y_mla, gs, gm, grp_a, t)
    ga = _wgrad_into(attn, dym, "w_o_mla", "row", _wgrad_into(mixed, dh, "w_out", "row", ga))

    dq, dk, dv = _attn_bwd(q, k, v, attn, dattn, lse, tq, red_a.start_pair([ga]))
    d_lat, ql, dq0, ckn, dkv, g_qa, g_kva, g_q, g_k = _mla_pre_bwd(lat, pos, invf, sgn, gqa, gkva, gq, gk, w_qb_p, grp_d,
                                                                    dq, dk, dv, t, red_a.pair_done_start_scatter(dk))
    gc = _split_column_shards(_unpad_heads(_wgrad(ql, dq0, "wgrad_q_b")))

    d_u, adj, dy, z, z2, dpre, g_b_glu, g_d, g_lr, g_li = _ssm_bwd(
        dys, y, u, xr, xi, wb, wc, tabs_rev, dskip, grp_d, b_glu, grp_e, tc)
    gd = _wgrad_into(z, dpre, "w_glu", "row", _wgrad_into(ckn, dkv, "w_kv_b", "col"))
    ge = _wgrad_into(z2, dys, "w_o_ssm", "col")
    grad_x, xn, dproj, g_norm_mix = _in_proj_bwd(x, g1, w_in_p, d_u, d_lat, dgs, dgm, dh, t)
    gb = _unpad_w_in(_wgrad(xn, dproj, "wgrad_in"))

    red_a.start_join(red_a.scatter_done(gb))
    g_wb = _wgrad_strips(u, adj, adj, "wgrad_ssm_b", 1, red_rest.start_pair([gb, gc, gd, ge]))
    g_wct = _wgrad_strips(dy, xr, xi, "wgrad_ssm_c", 0, red_rest.pair_done_start_scatter(g_wb))
    g_ar, g_ai, g_ldt, g_br, g_bi, g_cr, g_ci = _ssm_param_bwd(a_re, a_im, log_dt, b_re, b_im, g_lr, g_li, g_wb, g_wct)

    g_small = {
        "norm_mix": g_norm_mix.reshape(-1), "norm_mlp": g_norm_mlp.reshape(-1), "q_a_norm": g_qa.reshape(-1),
        "kv_a_norm": g_kva.reshape(-1), "q_norm": g_q.reshape(-1)[:QK_HEAD], "k_norm": g_k.reshape(-1)[:QK_HEAD],
        "ssm_a_re": g_ar, "ssm_a_im": g_ai, "ssm_log_dt": g_ldt.reshape(-1),
        "ssm_b_re": from_gcp(g_br), "ssm_b_im": from_gcp(g_bi),
        "ssm_c_re": g_cr.reshape(SSM_GROUPS, SSM_GROUP_CH, SSM_STATE), "ssm_c_im": g_ci.reshape(SSM_GROUPS, SSM_GROUP_CH, SSM_STATE),
        "ssm_d": g_d.reshape(SSM_GROUPS, SSM_GROUP_CH), "b_glu": g_b_glu.reshape(-1),
    }
    return loss_blk[0, 0], grad_x, g_small


def kernel(x, positions, norm_mix, w_in, q_a_norm, kv_a_norm, w_q_b, w_kv_b, q_norm, k_norm, w_o_mla, ssm_a_re, ssm_a_im, ssm_log_dt, ssm_b_re, ssm_b_im, ssm_c_re, ssm_c_im, ssm_d, w_glu, b_glu, w_o_ssm, w_out, norm_mlp, w_up, w_down, loss_target, m_norm_mix, m_w_in, m_q_a_norm, m_kv_a_norm, m_w_q_b, m_w_kv_b, m_q_norm, m_k_norm, m_w_o_mla, m_ssm_a_re, m_ssm_a_im, m_ssm_log_dt, m_ssm_b_re, m_ssm_b_im, m_ssm_c_re, m_ssm_c_im, m_ssm_d, m_w_glu, m_b_glu, m_w_o_ssm, m_w_out, m_norm_mlp, m_w_up, m_w_down, v_norm_mix, v_w_in, v_q_a_norm, v_kv_a_norm, v_w_q_b, v_w_kv_b, v_q_norm, v_k_norm, v_w_o_mla, v_ssm_a_re, v_ssm_a_im, v_ssm_log_dt, v_ssm_b_re, v_ssm_b_im, v_ssm_c_re, v_ssm_c_im, v_ssm_d, v_w_glu, v_b_glu, v_w_o_ssm, v_w_out, v_norm_mlp, v_w_up, v_w_down):
    args = dict(locals())
    w = {n: args[n][0] for n in WEIGHT_ORDER}
    m = {n: args["m_" + n][0] for n in WEIGHT_ORDER}
    v = {n: args["v_" + n][0] for n in WEIGHT_ORDER}
    big_names = [n for n, *_ in BIG_WEIGHTS]
    small_names = [n for n, _ in SMALL_WEIGHTS]

    place = jnp.stack([2 * lax.axis_index("x") + lax.axis_index("y"), lax.axis_index("c")]).astype(jnp.int32)
    rest = ["b", "c", "d", "e"]

    (grp_b,) = _gather_weights([_cast_shards(w, "b", place)])
    gather = _SplitGather([_cast_shards(w, g, place) for g in ("c", "d", "e", "a")], grp_b)
    red_a = _SplitReduction("a", ["a"], place)
    red_rest = _SplitReduction("rest", rest, place)
    small = {n: w[n] for n in small_names}

    loss_local, grad_x, g_small = _local_step(x[0], positions[0], loss_target[0], grp_b, small, gather, red_a, red_rest)
    loss = lax.psum(loss_local, ("x", "y", "c"))

    grad_w, delta_w, new_m, new_v = {}, {}, {}, {}

    def update(names, reduced, token):
        for n in names:
            g, off, _, _ = _place_in_group(n)
            grad_w[n], delta_w[n], new_m[n], new_v[n] = _adamw(w[n], reduced[g], m[n], v[n], "adamw_" + n, off, token)
            token = new_v[n]

    chip_sum = _SplitChipSum(_pair_sum_small(_pack_small(g_small)), place)
    in_a = [n for n, _ in GROUPS["a"][1]]
    update(in_a, {"a": red_a.join_done(chip_sum.token())[0]}, chip_sum.token())
    small_sum = chip_sum.done(new_v[in_a[-1]])
    g_s, d_s, m_s, v_s = _adamw(_pack_small(small), small_sum, _pack_small({n: m[n] for n in small_names}),
                                _pack_small({n: v[n] for n in small_names}), "adamw_small", 0, small_sum)
    g_s, d_s, m_s, v_s = _unpack_small(g_s), _unpack_small(d_s), _unpack_small(m_s), _unpack_small(v_s)
    for n in small_names:
        grad_w[n], delta_w[n], new_m[n], new_v[n] = g_s[n], d_s[n], m_s[n], v_s[n]
    halves = red_rest.scatter_done(v_s[small_names[0]])
    reduced_rest = dict(zip(rest, _swap_reduced_halves(halves)))
    update([n for n in big_names if n not in in_a], reduced_rest, halves[0])

    lead = lambda d: [d[n][None] for n in WEIGHT_ORDER]
    return (loss, grad_x[None], *lead(grad_w), *lead(delta_w), *lead(new_m), *lead(new_v))
```

```python
import math

import jax
import jax.numpy as jnp
import numpy as np
from jax import lax
from jax.experimental import pallas as pl
from jax.experimental.pallas import tpu as pltpu

F32 = jnp.float32
BF16 = jnp.bfloat16

D_MODEL = 1024
SSM_GROUPS = 32
SSM_GROUP_CH = 16
SSM_WIDTH = 512
SSM_STATE = 64
GP = SSM_GROUPS * SSM_STATE
N_HEADS = 8
QK_NOPE = 128
QK_ROPE = 64
QK_HEAD = 192
HEAD_PAD = 256
V_HEAD = 128
Q_LORA = 384
KV_LORA = 256
LAT_W = 768
D_IN = 3264
D_IN_PAD = 3328
D_FF = 4096
ROPE_THETA = 10000.0
EPS = 1e-6
ATT_SCALE = QK_HEAD ** -0.5

ADAM_LR = 0.001
ADAM_B1 = 0.9
ADAM_B2 = 0.999
ADAM_EPS = 1e-08
ADAM_WD = 0.01
ADAM_STEP = 10

VMEM_LIMIT_V7X = 56 * 1024 * 1024
MESH = pl.DeviceIdType.MESH

BIG_WEIGHTS = (
    ("w_in", 1024, 3264, "col"),
    ("w_q_b", 384, 1536, "col"),
    ("w_kv_b", 256, 2048, "col"),
    ("w_o_mla", 1024, 1024, "row"),
    ("w_glu", 512, 512, "row"),
    ("w_o_ssm", 512, 1024, "col"),
    ("w_out", 1024, 1024, "row"),
    ("w_up", 1024, 4096, "col"),
    ("w_down", 4096, 1024, "row"),
)
GROUPS = {
    "a": (1024, (("w_down", 1024), ("w_up", 1024), ("w_o_mla", 256), ("w_out", 256))),
    "b": (816, (("w_in", 1024),)),
    "c": (384, (("w_q_b", 384),)),
    "d": (512, (("w_kv_b", 256), ("w_glu", 128))),
    "e": (256, (("w_o_ssm", 512),)),
}


def _group_rows(group):
    return sum(r for _, r in GROUPS[group][1])


def _place_in_group(name):
    for group, (width, members) in GROUPS.items():
        off = 0
        for member, rows in members:
            if member == name:
                return group, off, rows, width
            off += rows
    raise KeyError(name)


SMALL_WEIGHTS = (
    ("norm_mix", (1024,)), ("q_a_norm", (384,)), ("kv_a_norm", (256,)), ("q_norm", (192,)), ("k_norm", (192,)),
    ("ssm_a_re", (32, 64)), ("ssm_a_im", (32, 64)), ("ssm_log_dt", (32,)),
    ("ssm_b_re", (32, 64, 16)), ("ssm_b_im", (32, 64, 16)), ("ssm_c_re", (32, 16, 64)), ("ssm_c_im", (32, 16, 64)),
    ("ssm_d", (32, 16)), ("b_glu", (512,)), ("norm_mlp", (1024,)),
)
WEIGHT_ORDER = ('norm_mix', 'w_in', 'q_a_norm', 'kv_a_norm', 'w_q_b', 'w_kv_b', 'q_norm', 'k_norm', 'w_o_mla', 'ssm_a_re',
                'ssm_a_im', 'ssm_log_dt', 'ssm_b_re', 'ssm_b_im', 'ssm_c_re', 'ssm_c_im', 'ssm_d', 'w_glu', 'b_glu',
                'w_o_ssm', 'w_out', 'norm_mlp', 'w_up', 'w_down')


def _cparams(*sem):
    return pltpu.CompilerParams(dimension_semantics=sem if sem else None, vmem_limit_bytes=VMEM_LIMIT_V7X)


def _resident(shape, index=None):
    index = (0,) * len(shape) if index is None else index
    return pl.BlockSpec(shape, lambda *_: index, pipeline_mode=pl.Buffered(1))


def _member_block(name):
    _, off, rows, width = _place_in_group(name)
    return _resident((4, rows, width), (0, off // rows, 0))


def _rows(t, width):
    return pl.BlockSpec((t, width), lambda i: (i, 0))


def _mm(a, b):
    return jnp.dot(a.astype(BF16), b.astype(BF16), preferred_element_type=F32)


def _mm_nt(a, b):
    return lax.dot_general(a.astype(BF16), b.astype(BF16), (((1,), (1,)), ((), ())), preferred_element_type=F32)


def _mm_tn(a, b):
    return lax.dot_general(a.astype(BF16), b.astype(BF16), (((0,), (0,)), ((), ())), preferred_element_type=F32)


def _rms_fwd(x, g, n):
    r = lax.rsqrt(jnp.sum(x * x, axis=-1, keepdims=True) * (1.0 / n) + EPS)
    return x * r * g


def _rms_bwd(x, g, dy, n):
    r = lax.rsqrt(jnp.sum(x * x, axis=-1, keepdims=True) * (1.0 / n) + EPS)
    xh = x * r
    dxh = dy * g
    dx = r * (dxh - xh * (jnp.sum(dxh * xh, axis=-1, keepdims=True) * (1.0 / n)))
    return dx, dy * xh


def _colsum(a):
    return jnp.sum(a, axis=0, keepdims=True)


def _accumulate(ref, value, first):
    @pl.when(first)
    def _():
        ref[...] = value

    @pl.when(jnp.logical_not(first))
    def _():
        ref[...] += value


def _sigmoid(a):
    return 1.0 / (1.0 + jnp.exp(-a))


GELU_C = math.sqrt(2.0 / math.pi)
GELU_A = 0.044715


def _gelu(y):
    return 0.5 * y * (1.0 + jnp.tanh(GELU_C * (y + GELU_A * y * y * y)))


def _gelu_grad(y):
    t = jnp.tanh(GELU_C * (y + GELU_A * y * y * y))
    return 0.5 * (1.0 + t) + 0.5 * y * (1.0 - t * t) * GELU_C * (1.0 + 3.0 * GELU_A * y * y)


def _in_proj_fwd(x, g1, w_in_p, t, token):
    l = x.shape[0]

    def body(x_ref, g_ref, w_ref, token_ref, u_ref, lat_ref, gs_ref, gm_ref):
        xn = _rms_fwd(x_ref[...], g_ref[...], D_MODEL).astype(BF16)
        u_ref[...] = _mm(xn, w_ref[:, 0:512])
        lat_ref[...] = _mm(xn, w_ref[:, 512:1280]).astype(BF16)
        gs_ref[...] = _mm(xn, w_ref[:, 1280:2304]).astype(BF16)
        gm_ref[...] = _mm(xn, w_ref[:, 2304:3328]).astype(BF16)

    return pl.pallas_call(
        body, name="in_proj_fwd", grid=(l // t,),
        in_specs=[_rows(t, D_MODEL), _resident((1, D_MODEL)), _resident((D_MODEL, D_IN_PAD)), ANY],
        out_specs=[_rows(t, 512), _rows(t, LAT_W), _rows(t, D_MODEL), _rows(t, D_MODEL)],
        out_shape=[jax.ShapeDtypeStruct((l, 512), F32), jax.ShapeDtypeStruct((l, LAT_W), BF16),
                   jax.ShapeDtypeStruct((l, D_MODEL), BF16), jax.ShapeDtypeStruct((l, D_MODEL), BF16)],
        compiler_params=_cparams("parallel"),
    )(x, g1, w_in_p, token)


def _in_proj_bwd(x, g1, w_in_p, d_u, d_lat, d_gs, d_gm, dh, t):
    l = x.shape[0]

    def body(x_ref, g_ref, w_ref, du_ref, dlat_ref, dgs_ref, dgm_ref, dh_ref, gx_ref, xn_ref, dproj_ref, dg_ref):
        xv = x_ref[...]
        g = g_ref[...]
        xn_ref[...] = _rms_fwd(xv, g, D_MODEL).astype(BF16)
        dproj_ref[:, 0:512] = du_ref[...]
        dproj_ref[:, 512:1280] = dlat_ref[...]
        dproj_ref[:, 1280:2304] = dgs_ref[...]
        dproj_ref[:, 2304:3328] = dgm_ref[...]
        dxn = _mm_nt(dproj_ref[...], w_ref[...])
        dx, dg_rows = _rms_bwd(xv, g, dxn, D_MODEL)
        gx_ref[...] = dh_ref[...] + dx
        _accumulate(dg_ref, _colsum(dg_rows), pl.program_id(0) == 0)

    return pl.pallas_call(
        body, name="in_proj_bwd", grid=(l // t,),
        in_specs=[_rows(t, D_MODEL), _resident((1, D_MODEL)), _resident((D_MODEL, D_IN_PAD)), _rows(t, 512),
                  _rows(t, LAT_W), _rows(t, D_MODEL), _rows(t, D_MODEL), _rows(t, D_MODEL)],
        out_specs=[_rows(t, D_MODEL), _rows(t, D_MODEL), _rows(t, D_IN_PAD), pl.BlockSpec((1, D_MODEL), lambda i: (0, 0))],
        out_shape=[jax.ShapeDtypeStruct((l, D_MODEL), F32), jax.ShapeDtypeStruct((l, D_MODEL), BF16),
                   jax.ShapeDtypeStruct((l, D_IN_PAD), BF16), jax.ShapeDtypeStruct((1, D_MODEL), F32)],
        compiler_params=_cparams("arbitrary"),
    )(x, g1, w_in_p, d_u, d_lat, d_gs, d_gm, dh)


def _ssm_param_fn(a_re, a_im, log_dt, b_re, b_im):
    dt = jnp.exp(log_dt)
    er = jnp.exp(a_re * dt)
    lr = er * jnp.cos(a_im * dt)
    li = er * jnp.sin(a_im * dt)
    den = a_re * a_re + a_im * a_im
    nr = lr - 1.0
    kr = (nr * a_re + li * a_im) / den
    ki = (li * a_re - nr * a_im) / den
    rows = lambda k: jnp.broadcast_to(k[:, None, :], (SSM_GROUPS, SSM_GROUP_CH, SSM_STATE)).reshape(SSM_WIDTH, SSM_STATE)
    krt, kit = rows(kr), rows(ki)
    return lr, li, krt * b_re - kit * b_im, krt * b_im + kit * b_re


def _state_selector():
    row = lax.broadcasted_iota(jnp.int32, (SSM_STATE, GP), 0)
    col = lax.broadcasted_iota(jnp.int32, (SSM_STATE, GP), 1)
    return jnp.where(jnp.bitwise_and(col, SSM_STATE - 1) == row, 1.0, 0.0).astype(BF16)


def _own_group(rows, rows_per_group_log2):
    row = lax.broadcasted_iota(jnp.int32, (rows, GP), 0)
    col = lax.broadcasted_iota(jnp.int32, (rows, GP), 1)
    return jnp.right_shift(row, rows_per_group_log2) == jnp.right_shift(col, 6)


def _three_bf16(x):
    hi = x.astype(BF16)
    rest = x - hi.astype(F32)
    mid = rest.astype(BF16)
    return hi, mid, (rest - mid.astype(F32)).astype(BF16)


def _spread(x, sel):
    return sum(jnp.dot(part, sel, preferred_element_type=F32) for part in _three_bf16(x))


def _collect(xw, sel):
    return sum(lax.dot_general(part, sel, (((1,), (1,)), ((), ())), preferred_element_type=F32) for part in _three_bf16(xw))


def _ssm_param_fwd(a_re, a_im, log_dt, b_re, b_im, c_re, c_im):
    def body(ar_ref, ai_ref, ldt_ref, br_ref, bi_ref, cr_ref, ci_ref, wb_ref, wct_ref, tf_ref, tr_ref):
        lr, li, bbr, bbi = _ssm_param_fn(ar_ref[...], ai_ref[...], ldt_ref[...], br_ref[...], bi_ref[...])
        sel = _state_selector()
        own16 = _own_group(SSM_WIDTH, 4)
        own1 = _own_group(SSM_GROUPS, 0)
        block = lambda m: jnp.where(own16, jnp.dot(m.astype(BF16), sel, preferred_element_type=F32), 0.0).astype(BF16)
        wb_ref[:, 0:GP] = block(bbr)
        wb_ref[:, GP:2 * GP] = block(bbi)
        wct_ref[:, 0:GP] = block(cr_ref[...])
        wct_ref[:, GP:2 * GP] = block(-ci_ref[...])
        flat = lambda m: _colsum(jnp.where(own1, _spread(m, sel), 0.0))
        pr, pi = [], []
        qr, qi = lr, li
        for _ in range(8):
            pr.append(flat(qr))
            pi.append(flat(qi))
            qr, qi = qr * lr - qi * li, qr * li + qi * lr
        row = lax.broadcasted_iota(jnp.int32, (8, GP), 0)
        for n, k in enumerate((1, 2, 4)):
            tf_ref[2 * n] = jnp.where(row >= k, pr[k - 1], 0.0)
            tf_ref[2 * n + 1] = jnp.where(row >= k, pi[k - 1], 0.0)
            tr_ref[2 * n] = jnp.where(row < 8 - k, pr[k - 1], 0.0)
            tr_ref[2 * n + 1] = jnp.where(row < 8 - k, -pi[k - 1], 0.0)
        pick = lambda vals: sum(jnp.where(row == j, v, 0.0) for j, v in enumerate(vals))
        tf_ref[6] = pick(pr)
        tf_ref[7] = pick(pi)
        tr_ref[6] = pick(pr[::-1])
        tr_ref[7] = pick([-v for v in pi[::-1]])

    return pl.pallas_call(
        body, name="ssm_param_fwd",
        out_shape=[jax.ShapeDtypeStruct((SSM_WIDTH, 2 * GP), BF16), jax.ShapeDtypeStruct((SSM_WIDTH, 2 * GP), BF16),
                   jax.ShapeDtypeStruct((8, 8, GP), F32), jax.ShapeDtypeStruct((8, 8, GP), F32)],
        compiler_params=_cparams(),
    )(a_re, a_im, log_dt, b_re, b_im, c_re, c_im)


STRIP_CH = 128
STRIP_ST = 512
N_STRIPS = SSM_WIDTH // STRIP_CH


def _ssm_param_bwd(a_re, a_im, log_dt, b_re, b_im, g_lr, g_li, g_wb, g_wct):
    def body(ar_ref, ai_ref, ldt_ref, br_ref, bi_ref, glr_ref, gli_ref, gwb_ref, gwc_ref,
             o_ar, o_ai, o_ldt, o_br, o_bi, o_cr, o_ci):
        sel = _state_selector()
        own1 = _own_group(SSM_GROUPS, 0)
        row = lax.broadcasted_iota(jnp.int32, (SSM_WIDTH, STRIP_ST), 0)
        col = lax.broadcasted_iota(jnp.int32, (SSM_WIDTH, STRIP_ST), 1)
        own = jnp.bitwise_and(jnp.right_shift(row, 4), 7) == jnp.right_shift(col, 6)
        blocks = lambda m: _collect(jnp.where(own, m, 0.0), sel[:, 0:STRIP_ST])
        unflat = lambda v: _collect(jnp.where(own1, v, 0.0), sel)
        _, vjp = jax.vjp(_ssm_param_fn, ar_ref[...], ai_ref[...], ldt_ref[...], br_ref[...], bi_ref[...])
        d_ar, d_ai, d_ldt, d_br, d_bi = vjp((unflat(glr_ref[...]), unflat(gli_ref[...]),
                                             blocks(gwb_ref[:, 0:STRIP_ST]), blocks(gwb_ref[:, STRIP_ST:2 * STRIP_ST])))
        o_ar[...] = d_ar
        o_ai[...] = d_ai
        o_ldt[...] = d_ldt
        o_br[...] = d_br
        o_bi[...] = d_bi
        o_cr[...] = blocks(gwc_ref[:, 0:STRIP_ST])
        o_ci[...] = -blocks(gwc_ref[:, STRIP_ST:2 * STRIP_ST])

    g, p = SSM_GROUPS, SSM_STATE
    gp = jax.ShapeDtypeStruct((g, p), F32)
    gcp = jax.ShapeDtypeStruct((SSM_WIDTH, p), F32)
    return pl.pallas_call(
        body, name="ssm_param_bwd", out_shape=[gp, gp, jax.ShapeDtypeStruct((g, 1), F32), gcp, gcp, gcp, gcp],
        compiler_params=_cparams(),
    )(a_re, a_im, log_dt, b_re, b_im, g_lr, g_li, g_wb, g_wct)


def _strip(ref, j, im):
    return ref[STRIP_CH * j:STRIP_CH * (j + 1), im * GP + STRIP_ST * j:im * GP + STRIP_ST * (j + 1)]


def _wgrad_strips(a, b_re, b_im, name, im_block, token):
    l = a.shape[0]
    bl = min(l, 512)

    def body(a_ref, bre_ref, bim_ref, token_ref, o_ref):
        first = pl.program_id(0) == 0
        for j in range(N_STRIPS):
            aj = a_ref[:, STRIP_CH * j:STRIP_CH * (j + 1)]
            states = slice(STRIP_ST * j, STRIP_ST * (j + 1))
            _accumulate(o_ref.at[STRIP_CH * j:STRIP_CH * (j + 1), 0:STRIP_ST], _mm_tn(aj, bre_ref[:, states]), first)
            _accumulate(o_ref.at[STRIP_CH * j:STRIP_CH * (j + 1), STRIP_ST:2 * STRIP_ST], _mm_tn(aj, bim_ref[:, states]), first)

    return pl.pallas_call(
        body, name=name, grid=(l // bl,),
        in_specs=[pl.BlockSpec((bl, SSM_WIDTH), lambda k: (k, 0)), pl.BlockSpec((bl, GP), lambda k: (k, 0)),
                  pl.BlockSpec((bl, GP), lambda k: (k, im_block)), ANY],
        out_specs=pl.BlockSpec((SSM_WIDTH, 2 * STRIP_ST), lambda k: (0, 0)),
        out_shape=jax.ShapeDtypeStruct((SSM_WIDTH, 2 * STRIP_ST), F32),
        compiler_params=_cparams("arbitrary"),
    )(a, b_re, b_im, token)


SCAN_STRIP = 512


def _scan_chunk(inr_ref, ini_ref, outr_ref, outi_ref, cr_ref, ci_ref, tab_ref, tc, reverse):
    n_blocks = tc // 8

    def block(j, _):
        i = (n_blocks - 1 - j) if reverse else j
        rows = pl.ds(pl.multiple_of(i * 8, 8), 8)
        for s in range(GP // SCAN_STRIP):
            sl = pl.ds(s * SCAN_STRIP, SCAN_STRIP)
            xr = inr_ref[rows, sl]
            xi = ini_ref[rows, sl]
            for n, k in enumerate((1, 2, 4)):
                shift = (8 - k) if reverse else k
                sr = pltpu.roll(xr, shift, 0)
                si = pltpu.roll(xi, shift, 0)
                mr = tab_ref[2 * n, :, sl]
                mi = tab_ref[2 * n + 1, :, sl]
                xr, xi = xr + mr * sr - mi * si, xi + mr * si + mi * sr
            qr = tab_ref[6, :, sl]
            qi = tab_ref[7, :, sl]
            cr = cr_ref[:, sl]
            ci = ci_ref[:, sl]
            xr, xi = xr + qr * cr - qi * ci, xi + qr * ci + qi * cr
            outr_ref[rows, sl] = xr
            outi_ref[rows, sl] = xi
            edge = 0 if reverse else 7
            cr_ref[:, sl] = jnp.broadcast_to(xr[edge:edge + 1, :], (8, SCAN_STRIP))
            ci_ref[:, sl] = jnp.broadcast_to(xi[edge:edge + 1, :], (8, SCAN_STRIP))
        return 0

    lax.fori_loop(0, n_blocks, block, 0)


def _glu_pre(z, wg_ref):
    return sum(_mm(z[:, 128 * j:128 * (j + 1)], wg_ref[j]) for j in range(4))


def _ssm_fwd(u, wb, wc, tabs, dskip, grp_d, b_glu, grp_e, tc):
    l = u.shape[0]

    def body(u_ref, wb_ref, wc_ref, tab_ref, d_ref, wg_ref, bg_ref, wo_ref, xr_ref, xi_ref, y_ref, ys_ref,
             bur, bui, cr, ci):
        @pl.when(pl.program_id(0) == 0)
        def _():
            cr[...] = jnp.zeros_like(cr)
            ci[...] = jnp.zeros_like(ci)

        uv = u_ref[...]
        ub = uv.astype(BF16)
        for j in range(N_STRIPS):
            uj = ub[:, STRIP_CH * j:STRIP_CH * (j + 1)]
            states = slice(STRIP_ST * j, STRIP_ST * (j + 1))
            bur[:, states] = _mm(uj, _strip(wb_ref, j, 0))
            bui[:, states] = _mm(uj, _strip(wb_ref, j, 1))
        _scan_chunk(bur, bui, bur, bui, cr, ci, tab_ref, tc, False)
        xr_ref[...] = bur[...].astype(BF16)
        xi_ref[...] = bui[...].astype(BF16)
        y = jnp.concatenate(
            [_mm_nt(xr_ref[:, STRIP_ST * j:STRIP_ST * (j + 1)], _strip(wc_ref, j, 0))
             + _mm_nt(xi_ref[:, STRIP_ST * j:STRIP_ST * (j + 1)], _strip(wc_ref, j, 1)) for j in range(N_STRIPS)],
            axis=-1) + d_ref[...] * uv
        y_ref[...] = y
        z = _gelu(y)
        z2 = z * _sigmoid(_glu_pre(z, wg_ref) + bg_ref[...])
        for s in range(4):
            ys_ref[:, 256 * s:256 * (s + 1)] = _mm(z2, wo_ref[s]).astype(BF16)

    return pl.pallas_call(
        body, name="ssm_fwd", grid=(l // tc,),
        in_specs=[_rows(tc, 512), _resident((512, 2 * GP)), _resident((512, 2 * GP)), _resident((8, 8, GP)),
                  _resident((1, 512)), _member_block("w_glu"), _resident((1, 512)), _member_block("w_o_ssm")],
        out_specs=[_rows(tc, GP), _rows(tc, GP), _rows(tc, 512), _rows(tc, D_MODEL)],
        out_shape=[jax.ShapeDtypeStruct((l, GP), BF16), jax.ShapeDtypeStruct((l, GP), BF16),
                   jax.ShapeDtypeStruct((l, 512), F32), jax.ShapeDtypeStruct((l, D_MODEL), BF16)],
        scratch_shapes=[pltpu.VMEM((tc, GP), F32), pltpu.VMEM((tc, GP), F32), pltpu.VMEM((8, GP), F32),
                        pltpu.VMEM((8, GP), F32)],
        compiler_params=_cparams("arbitrary"),
    )(u, wb, wc, tabs, dskip, grp_d, b_glu, grp_e)


def _ssm_bwd(dys, y, u, xr, xi, wb, wc, tabs_rev, dskip, grp_d, b_glu, grp_e, tc):
    l = u.shape[0]
    nc = l // tc

    def body(dys_ref, y_ref, u_ref, xr_ref, xi_ref, wb_ref, wc_ref, tab_ref, d_ref, wg_ref, bg_ref, wo_ref,
             du_ref, a_ref, dy_ref, z_ref, z2_ref, dpre_ref, gb_ref, gd_ref, glr_ref, gli_ref,
             dxr, dxi, ar, ai, cr, ci):
        first = pl.program_id(0) == 0

        @pl.when(first)
        def _():
            cr[...] = jnp.zeros_like(cr)
            ci[...] = jnp.zeros_like(ci)

        yv = y_ref[...]
        uv = u_ref[...]
        dz2 = sum(_mm_nt(dys_ref[:, 256 * j:256 * (j + 1)], wo_ref[j]) for j in range(4))
        z = _gelu(yv)
        s = _sigmoid(_glu_pre(z, wg_ref) + bg_ref[...])
        dpre = dz2 * z * s * (1.0 - s)
        dpreb = dpre.astype(BF16)
        dz = dz2 * s + jnp.concatenate([_mm_nt(dpreb, wg_ref[j]) for j in range(4)], axis=-1)
        dy = dz * _gelu_grad(yv)
        z_ref[...] = z.astype(BF16)
        z2_ref[...] = (z * s).astype(BF16)
        dpre_ref[...] = dpre.astype(BF16)
        dy_ref[...] = dy.astype(BF16)
        _accumulate(gb_ref, _colsum(dpre), first)
        _accumulate(gd_ref, _colsum(dy * uv), first)

        dyb = dy.astype(BF16)
        for j in range(N_STRIPS):
            dyj = dyb[:, STRIP_CH * j:STRIP_CH * (j + 1)]
            dxr[:, STRIP_ST * j:STRIP_ST * (j + 1)] = _mm(dyj, _strip(wc_ref, j, 0))
            dxi[:, STRIP_ST * j:STRIP_ST * (j + 1)] = _mm(dyj, _strip(wc_ref, j, 1))
        ar[pl.ds(tc, 8), :] = cr[...]
        ai[pl.ds(tc, 8), :] = ci[...]
        _scan_chunk(dxr, dxi, ar, ai, cr, ci, tab_ref, tc, True)
        a_ref[:, 0:GP] = ar[pl.ds(0, tc), :].astype(BF16)
        a_ref[:, GP:2 * GP] = ai[pl.ds(0, tc), :].astype(BF16)
        du_states = jnp.concatenate(
            [_mm_nt(a_ref[:, STRIP_ST * j:STRIP_ST * (j + 1)], _strip(wb_ref, j, 0))
             + _mm_nt(a_ref[:, GP + STRIP_ST * j:GP + STRIP_ST * (j + 1)], _strip(wb_ref, j, 1)) for j in range(N_STRIPS)],
            axis=-1)
        du_ref[...] = (dy * d_ref[...] + du_states).astype(BF16)
        anr = ar[pl.ds(1, tc), :]
        ani = ai[pl.ds(1, tc), :]
        xrv = xr_ref[...].astype(F32)
        xiv = xi_ref[...].astype(F32)
        _accumulate(glr_ref, _colsum(anr * xrv + ani * xiv), first)
        _accumulate(gli_ref, _colsum(ani * xrv - anr * xiv), first)

    rev = lambda w: pl.BlockSpec((tc, w), lambda i: (nc - 1 - i, 0))
    acc = lambda w: pl.BlockSpec((1, w), lambda i: (0, 0))
    bf = jax.ShapeDtypeStruct((l, 512), BF16)
    return pl.pallas_call(
        body, name="ssm_bwd", grid=(nc,),
        in_specs=[rev(D_MODEL), rev(512), rev(512), rev(GP), rev(GP), _resident((512, 2 * GP)), _resident((512, 2 * GP)),
                  _resident((8, 8, GP)), _resident((1, 512)), _member_block("w_glu"), _resident((1, 512)),
                  _member_block("w_o_ssm")],
        out_specs=[rev(512), rev(2 * GP), rev(512), rev(512), rev(512), rev(512), acc(512), acc(512), acc(GP), acc(GP)],
        out_shape=[bf, jax.ShapeDtypeStruct((l, 2 * GP), BF16), bf, bf, bf, bf,
                   jax.ShapeDtypeStruct((1, 512), F32), jax.ShapeDtypeStruct((1, 512), F32),
                   jax.ShapeDtypeStruct((1, GP), F32), jax.ShapeDtypeStruct((1, GP), F32)],
        scratch_shapes=[pltpu.VMEM((tc, GP), F32), pltpu.VMEM((tc, GP), F32), pltpu.VMEM((tc + 8, GP), F32),
                        pltpu.VMEM((tc + 8, GP), F32), pltpu.VMEM((8, GP), F32), pltpu.VMEM((8, GP), F32)],
        compiler_params=_cparams("arbitrary"),
    )(dys, y, u, xr, xi, wb, wc, tabs_rev, dskip, grp_d, b_glu, grp_e)


def _swap_halves(b):
    lane = lax.broadcasted_iota(jnp.int32, b.shape, 1)
    return jnp.where(lane < 32, pltpu.roll(b, 96, 1), pltpu.roll(b, 32, 1))


def _rope_tables(pos_ref, invf_ref, sgn_ref):
    ang = pos_ref[...].astype(F32) * invf_ref[...]
    return jnp.cos(ang), jnp.sin(ang) * sgn_ref[...]


def _mla_pre_fwd(lat, pos, invf, sgn, gqa, gkva, gq, gk, w_qb_p, w_kvb, t):
    l = lat.shape[0]

    def body(lat_ref, pos_ref, invf_ref, sgn_ref, gqa_ref, gkva_ref, gq_ref, gk_ref, wq_ref, wkv_ref, q_ref, k_ref, v_ref):
        cs, sn = _rope_tables(pos_ref, invf_ref, sgn_ref)
        ql = _rms_fwd(lat_ref[:, 0:Q_LORA].astype(F32), gqa_ref[...], Q_LORA)
        ckn = _rms_fwd(lat_ref[:, Q_LORA:Q_LORA + KV_LORA].astype(F32), gkva_ref[...], KV_LORA)
        kpe = lat_ref[:, 640:768].astype(F32)
        q0 = _mm(ql, wq_ref[...])
        cknb = ckn.astype(BF16)
        kv = jnp.concatenate([_mm(cknb, wkv_ref[s]) for s in range(4)], axis=-1)
        for h in range(N_HEADS):
            q1 = _rms_fwd(q0[:, HEAD_PAD * h:HEAD_PAD * (h + 1)], gq_ref[...], QK_HEAD)
            b = q1[:, 128:256]
            q_ref[h, :, 0:128] = (q1[:, 0:128] * ATT_SCALE).astype(BF16)
            q_ref[h, :, 128:256] = ((b * cs + _swap_halves(b) * sn) * ATT_SCALE).astype(BF16)
            k0 = jnp.concatenate([kv[:, 256 * h:256 * h + 128], kpe], axis=-1)
            k1 = _rms_fwd(k0, gk_ref[...], QK_HEAD)
            b = k1[:, 128:256]
            k_ref[h, :, 0:128] = k1[:, 0:128].astype(BF16)
            k_ref[h, :, 128:256] = (b * cs + _swap_halves(b) * sn).astype(BF16)
            v_ref[h] = kv[:, 256 * h + 128:256 * h + 256].astype(BF16)

    heads = lambda w: pl.BlockSpec((N_HEADS, t, w), lambda i: (0, i, 0))
    return pl.pallas_call(
        body, name="mla_pre_fwd", grid=(l // t,),
        in_specs=[_rows(t, LAT_W), _rows(t, 1), _resident((1, 128)), _resident((1, 128)), _resident((1, Q_LORA)),
                  _resident((1, KV_LORA)), _resident((1, HEAD_PAD)), _resident((1, HEAD_PAD)),
                  _resident((Q_LORA, N_HEADS * HEAD_PAD)), _member_block("w_kv_b")],
        out_specs=[heads(HEAD_PAD), heads(HEAD_PAD), heads(V_HEAD)],
        out_shape=[jax.ShapeDtypeStruct((N_HEADS, l, HEAD_PAD), BF16), jax.ShapeDtypeStruct((N_HEADS, l, HEAD_PAD), BF16),
                   jax.ShapeDtypeStruct((N_HEADS, l, V_HEAD), BF16)],
        compiler_params=_cparams("parallel"),
    )(lat, pos, invf, sgn, gqa, gkva, gq, gk, w_qb_p, w_kvb)


def _mla_pre_bwd(lat, pos, invf, sgn, gqa, gkva, gq, gk, w_qb_p, w_kvb, dq, dk, dv, t, token):
    l = lat.shape[0]

    def body(lat_ref, pos_ref, invf_ref, sgn_ref, gqa_ref, gkva_ref, gq_ref, gk_ref, wq_ref, wkv_ref, dq_ref, dk_ref, dv_ref,
             token_ref, dlat_ref, ql_ref, dq0_ref, ckn_ref, dkv_ref, ggqa_ref, ggkva_ref, ggq_ref, ggk_ref):
        first = pl.program_id(0) == 0
        cs, sn = _rope_tables(pos_ref, invf_ref, sgn_ref)
        q_lat = lat_ref[:, 0:Q_LORA].astype(F32)
        c_kv = lat_ref[:, Q_LORA:Q_LORA + KV_LORA].astype(F32)
        kpe = lat_ref[:, 640:768].astype(F32)
        ql = _rms_fwd(q_lat, gqa_ref[...], Q_LORA)
        ckn = _rms_fwd(c_kv, gkva_ref[...], KV_LORA)
        ql_ref[...] = ql.astype(BF16)
        ckn_ref[...] = ckn.astype(BF16)
        q0 = _mm(ql, wq_ref[...])
        cknb = ckn.astype(BF16)
        kv = jnp.concatenate([_mm(cknb, wkv_ref[s]) for s in range(4)], axis=-1)
        dkpe = jnp.zeros_like(kpe)
        ggq = jnp.zeros((1, HEAD_PAD), F32)
        ggk = jnp.zeros((1, HEAD_PAD), F32)

        def unrope(d):
            b = d[:, 128:256]
            return jnp.concatenate([d[:, 0:128], b * cs + _swap_halves(b * sn)], axis=-1)

        for h in range(N_HEADS):
            dq1 = unrope(dq_ref[h] * ATT_SCALE)
            dq0h, gq_rows = _rms_bwd(q0[:, HEAD_PAD * h:HEAD_PAD * (h + 1)], gq_ref[...], dq1, QK_HEAD)
            ggq = ggq + _colsum(gq_rows)
            dq0_ref[:, HEAD_PAD * h:HEAD_PAD * (h + 1)] = dq0h.astype(BF16)
            k0 = jnp.concatenate([kv[:, 256 * h:256 * h + 128], kpe], axis=-1)
            dk0, gk_rows = _rms_bwd(k0, gk_ref[...], unrope(dk_ref[h]), QK_HEAD)
            ggk = ggk + _colsum(gk_rows)
            dkpe = dkpe + dk0[:, 128:256]
            dkv_ref[:, 256 * h:256 * h + 128] = dk0[:, 0:128].astype(BF16)
            dkv_ref[:, 256 * h + 128:256 * h + 256] = dv_ref[h].astype(BF16)
        dql = _mm_nt(dq0_ref[...], wq_ref[...])
        dckn = sum(_mm_nt(dkv_ref[:, 512 * s:512 * (s + 1)], wkv_ref[s]) for s in range(4))
        dq_lat, gqa_rows = _rms_bwd(q_lat, gqa_ref[...], dql, Q_LORA)
        dc_kv, gkva_rows = _rms_bwd(c_kv, gkva_ref[...], dckn, KV_LORA)
        dlat_ref[:, 0:Q_LORA] = dq_lat.astype(BF16)
        dlat_ref[:, Q_LORA:Q_LORA + KV_LORA] = dc_kv.astype(BF16)
        dlat_ref[:, 640:768] = dkpe.astype(BF16)
        _accumulate(ggqa_ref, _colsum(gqa_rows), first)
        _accumulate(ggkva_ref, _colsum(gkva_rows), first)
        _accumulate(ggq_ref, ggq, first)
        _accumulate(ggk_ref, ggk, first)

    heads = lambda w: pl.BlockSpec((N_HEADS, t, w), lambda i: (0, i, 0))
    acc = lambda w: pl.BlockSpec((1, w), lambda i: (0, 0))
    return pl.pallas_call(
        body, name="mla_pre_bwd", grid=(l // t,),
        in_specs=[_rows(t, LAT_W), _rows(t, 1), _resident((1, 128)), _resident((1, 128)), _resident((1, Q_LORA)),
                  _resident((1, KV_LORA)), _resident((1, HEAD_PAD)), _resident((1, HEAD_PAD)),
                  _resident((Q_LORA, N_HEADS * HEAD_PAD)), _member_block("w_kv_b"),
                  heads(HEAD_PAD), heads(HEAD_PAD), heads(V_HEAD), ANY],
        out_specs=[_rows(t, LAT_W), _rows(t, Q_LORA), _rows(t, N_HEADS * HEAD_PAD), _rows(t, KV_LORA), _rows(t, N_HEADS * 256),
                   acc(Q_LORA), acc(KV_LORA), acc(HEAD_PAD), acc(HEAD_PAD)],
        out_shape=[jax.ShapeDtypeStruct((l, LAT_W), BF16), jax.ShapeDtypeStruct((l, Q_LORA), BF16),
                   jax.ShapeDtypeStruct((l, N_HEADS * HEAD_PAD), BF16), jax.ShapeDtypeStruct((l, KV_LORA), BF16),
                   jax.ShapeDtypeStruct((l, N_HEADS * 256), BF16), jax.ShapeDtypeStruct((1, Q_LORA), F32),
                   jax.ShapeDtypeStruct((1, KV_LORA), F32), jax.ShapeDtypeStruct((1, HEAD_PAD), F32),
                   jax.ShapeDtypeStruct((1, HEAD_PAD), F32)],
        compiler_params=_cparams("arbitrary"),
    )(lat, pos, invf, sgn, gqa, gkva, gq, gk, w_qb_p, w_kvb, dq, dk, dv, token)


def _causal(s, transposed):
    row = lax.broadcasted_iota(jnp.int32, s.shape, 0)
    col = lax.broadcasted_iota(jnp.int32, s.shape, 1)
    keep = (row <= col) if transposed else (col <= row)
    return jnp.where(keep, s, -jnp.inf)


def _as_row(col):
    n = col.shape[0]
    row = lax.broadcasted_iota(jnp.int32, (n, n), 0)
    lane = lax.broadcasted_iota(jnp.int32, (n, n), 1)
    return jnp.sum(jnp.where(row == lane, col, 0.0), axis=0, keepdims=True)


def _attn_fwd(q, k, v, tq):
    l = q.shape[1]

    hb = 2

    def body(q_ref, k_ref, v_ref, o_ref, lse_ref):
        qi = pl.program_id(1)
        qs = [q_ref[a] for a in range(hb)]

        def step(kb, carry, masked):
            rows = pl.ds(pl.multiple_of(kb * tq, tq), tq)
            out = []
            for a, (m, den, acc) in enumerate(carry):
                s = _mm_nt(qs[a], k_ref[a, rows, :])
                if masked:
                    s = _causal(s, False)
                m_new = jnp.maximum(m, jnp.max(s, axis=-1, keepdims=True))
                alpha = jnp.exp(m - m_new)
                p = jnp.exp(s - m_new)
                den = alpha * den + jnp.sum(p, axis=-1, keepdims=True)
                acc = alpha * acc + _mm(p, v_ref[a, rows, :])
                out.append((m_new, den, acc))
            return tuple(out)

        init = tuple((jnp.full((tq, 1), -jnp.inf, F32), jnp.zeros((tq, 1), F32), jnp.zeros((tq, V_HEAD), F32))
                     for _ in range(hb))
        carry = lax.fori_loop(0, qi, lambda kb, c: step(kb, c, False), init)
        for a, (m, den, acc) in enumerate(step(qi, carry, True)):
            o_ref[:, V_HEAD * a:V_HEAD * (a + 1)] = acc / den
            lse_ref[a, 0] = _as_row(m + jnp.log(den))

    return pl.pallas_call(
        body, name="attn_fwd", grid=(N_HEADS // hb, l // tq),
        in_specs=[pl.BlockSpec((hb, tq, HEAD_PAD), lambda h, i: (h, i, 0)), pl.BlockSpec((hb, l, HEAD_PAD), lambda h, i: (h, 0, 0)),
                  pl.BlockSpec((hb, l, V_HEAD), lambda h, i: (h, 0, 0))],
        out_specs=[pl.BlockSpec((tq, hb * V_HEAD), lambda h, i: (i, h)), pl.BlockSpec((hb, 1, 1, tq), lambda h, i: (h, i, 0, 0))],
        out_shape=[jax.ShapeDtypeStruct((l, N_HEADS * V_HEAD), F32), jax.ShapeDtypeStruct((N_HEADS, l // tq, 1, tq), F32)],
        compiler_params=_cparams("parallel", "arbitrary"),
    )(q, k, v)


def _attn_bwd(q, k, v, o, do, lse_t, tq, token):
    l = q.shape[1]
    nq = l // tq

    hb = 1

    def body(q_ref, k_ref, v_ref, o_ref, do_ref, lse_ref, token_ref, dq_ref, dk_ref, dv_ref):
        ki = pl.program_id(1)

        @pl.when(ki == 0)
        def _():
            dq_ref[...] = jnp.zeros_like(dq_ref)

        kblks = [k_ref[a] for a in range(hb)]
        vblks = [v_ref[a] for a in range(hb)]
        ones = jnp.ones((8, V_HEAD), BF16)

        def step(qb, carry, masked):
            rows = pl.ds(pl.multiple_of(qb * tq, tq), tq)
            out = []
            for a, (dk, dv) in enumerate(carry):
                cols = slice(V_HEAD * a, V_HEAD * (a + 1))
                qblk = q_ref[a, rows, :]
                dov = do_ref[rows, cols].astype(F32)
                dob = dov.astype(BF16)
                delta = sum(_mm_nt(ones, part) for part in _three_bf16(dov * o_ref[rows, cols]))[0:1, :]
                st = _mm_nt(kblks[a], qblk)
                if masked:
                    st = _causal(st, True)
                pt = jnp.exp(st - lse_ref[a, qb])
                dv = dv + _mm(pt, dob)
                dst = (pt * (_mm_nt(vblks[a], dob) - delta)).astype(BF16)
                dk = dk + _mm(dst, qblk)
                dq_ref[a, rows, :] += _mm_tn(dst, kblks[a])
                out.append((dk, dv))
            return tuple(out)

        init = tuple((jnp.zeros((tq, HEAD_PAD), F32), jnp.zeros((tq, V_HEAD), F32)) for _ in range(hb))
        carry = lax.fori_loop(ki + 1, nq, lambda qb, c: step(qb, c, False), step(ki, init, True))
        for a, (dk, dv) in enumerate(carry):
            dk_ref[a] = dk
            dv_ref[a] = dv

    return pl.pallas_call(
        body, name="attn_bwd", grid=(N_HEADS // hb, nq),
        in_specs=[pl.BlockSpec((hb, l, HEAD_PAD), lambda h, i: (h, 0, 0)), pl.BlockSpec((hb, tq, HEAD_PAD), lambda h, i: (h, i, 0)),
                  pl.BlockSpec((hb, tq, V_HEAD), lambda h, i: (h, i, 0)), pl.BlockSpec((l, hb * V_HEAD), lambda h, i: (0, h)),
                  pl.BlockSpec((l, hb * V_HEAD), lambda h, i: (0, h)), pl.BlockSpec((hb, nq, 1, tq), lambda h, i: (h, 0, 0, 0)), ANY],
        out_specs=[pl.BlockSpec((hb, l, HEAD_PAD), lambda h, i: (h, 0, 0)), pl.BlockSpec((hb, tq, HEAD_PAD), lambda h, i: (h, i, 0)),
                   pl.BlockSpec((hb, tq, V_HEAD), lambda h, i: (h, i, 0))],
        out_shape=[jax.ShapeDtypeStruct((N_HEADS, l, HEAD_PAD), F32), jax.ShapeDtypeStruct((N_HEADS, l, HEAD_PAD), F32),
                   jax.ShapeDtypeStruct((N_HEADS, l, V_HEAD), F32)],
        compiler_params=_cparams("parallel", "arbitrary"),
    )(q, k, v, o, do, lse_t, token)


def _row_shards_mm(a, w_ref):
    a = a.astype(BF16)
    return sum(_mm(a[:, 256 * j:256 * (j + 1)], w_ref[j]) for j in range(4))


def _row_shards_mm_nt(a, w_ref):
    a = a.astype(BF16)
    return jnp.concatenate([_mm_nt(a, w_ref[j]) for j in range(4)], axis=-1)


def _merge_fwd(attn, y_ssm, gs, gm, x, grp_a, t):
    l = x.shape[0]

    def body(attn_ref, ys_ref, gs_ref, gm_ref, x_ref, wo_ref, wout_ref, ym_ref, mixed_ref, h_ref):
        y_mla = _row_shards_mm(attn_ref[...], wo_ref)
        ym_ref[...] = y_mla.astype(BF16)
        mixed = (_sigmoid(gs_ref[...].astype(F32)) * ys_ref[...].astype(F32)
                 + _sigmoid(gm_ref[...].astype(F32)) * y_mla).astype(BF16)
        mixed_ref[...] = mixed
        h_ref[...] = x_ref[...] + _row_shards_mm(mixed, wout_ref)

    r = lambda: _rows(t, D_MODEL)
    return pl.pallas_call(
        body, name="merge_fwd", grid=(l // t,),
        in_specs=[r(), r(), r(), r(), r(), _member_block("w_o_mla"), _member_block("w_out")],
        out_specs=[r(), r(), r()],
        out_shape=[jax.ShapeDtypeStruct((l, D_MODEL), BF16), jax.ShapeDtypeStruct((l, D_MODEL), BF16),
                   jax.ShapeDtypeStruct((l, D_MODEL), F32)],
        compiler_params=_cparams("parallel"),
    )(attn, y_ssm, gs, gm, x, grp_a, grp_a)


def _merge_bwd(dh, y_ssm, y_mla, gs, gm, grp_a, t):
    l = dh.shape[0]

    def body(dh_ref, ys_ref, ym_ref, gs_ref, gm_ref, wo_ref, wout_ref, dys_ref, dym_ref, dgs_ref, dgm_ref, dattn_ref):
        dmixed = _row_shards_mm_nt(dh_ref[...], wout_ref)
        sg = _sigmoid(gs_ref[...].astype(F32))
        sm = _sigmoid(gm_ref[...].astype(F32))
        dys_ref[...] = (dmixed * sg).astype(BF16)
        dgs_ref[...] = (dmixed * ys_ref[...].astype(F32) * sg * (1.0 - sg)).astype(BF16)
        dym = (dmixed * sm).astype(BF16)
        dym_ref[...] = dym
        dgm_ref[...] = (dmixed * ym_ref[...].astype(F32) * sm * (1.0 - sm)).astype(BF16)
        dattn_ref[...] = _row_shards_mm_nt(dym, wo_ref).astype(BF16)

    r = lambda: _rows(t, D_MODEL)
    bf = jax.ShapeDtypeStruct((l, D_MODEL), BF16)
    return pl.pallas_call(
        body, name="merge_bwd", grid=(l // t,),
        in_specs=[r(), r(), r(), r(), r(), _member_block("w_o_mla"), _member_block("w_out")],
        out_specs=[r(), r(), r(), r(), r()],
        out_shape=[bf, bf, bf, bf, bf],
        compiler_params=_cparams("parallel"),
    )(dh, y_ssm, y_mla, gs, gm, grp_a, grp_a)


def _mlp_fwd_bwd(h, tgt, g2, grp_a, t):
    l = h.shape[0]

    def body(h_ref, tgt_ref, g_ref, wu_ref, wd_ref, dh_ref, hn_ref, da_ref, hid_ref, dout_ref, loss_ref, dg_ref):
        first = pl.program_id(0) == 0
        hv = h_ref[...]
        g = g_ref[...]
        hn = _rms_fwd(hv, g, D_MODEL).astype(BF16)
        hn_ref[...] = hn
        out = hv
        relus = []
        for s in range(4):
            cols = slice(1024 * s, 1024 * (s + 1))
            relu = jnp.maximum(_mm(hn, wu_ref[s]), 0.0)
            relus.append(relu)
            hid = (relu * relu).astype(BF16)
            hid_ref[:, cols] = hid
            out = out + _mm(hid, wd_ref[s])
        err = out - tgt_ref[...]
        _accumulate(loss_ref, jnp.full((8, 128), jnp.sum(err * err) * (0.5 / D_MODEL), F32), first)
        dout = err * (1.0 / D_MODEL)
        doutb = dout.astype(BF16)
        dout_ref[...] = doutb
        dhn = jnp.zeros_like(hv)
        for s in range(4):
            da = (_mm_nt(doutb, wd_ref[s]) * (2.0 * relus[s])).astype(BF16)
            da_ref[:, 1024 * s:1024 * (s + 1)] = da
            dhn = dhn + _mm_nt(da, wu_ref[s])
        dx, dg_rows = _rms_bwd(hv, g, dhn, D_MODEL)
        dh_ref[...] = dout + dx
        _accumulate(dg_ref, _colsum(dg_rows), first)

    r = lambda w: _rows(t, w)
    return pl.pallas_call(
        body, name="mlp_fwd_bwd", grid=(l // t,),
        in_specs=[r(D_MODEL), r(D_MODEL), _resident((1, D_MODEL)), _member_block("w_up"), _member_block("w_down")],
        out_specs=[r(D_MODEL), r(D_MODEL), r(D_FF), r(D_FF), r(D_MODEL), pl.BlockSpec((8, 128), lambda i: (0, 0)),
                   pl.BlockSpec((1, D_MODEL), lambda i: (0, 0))],
        out_shape=[jax.ShapeDtypeStruct((l, D_MODEL), F32), jax.ShapeDtypeStruct((l, D_MODEL), BF16),
                   jax.ShapeDtypeStruct((l, D_FF), BF16), jax.ShapeDtypeStruct((l, D_FF), BF16),
                   jax.ShapeDtypeStruct((l, D_MODEL), BF16), jax.ShapeDtypeStruct((8, 128), F32),
                   jax.ShapeDtypeStruct((1, D_MODEL), F32)],
        compiler_params=_cparams("arbitrary"),
    )(h, tgt, g2, grp_a, grp_a)


def _wgrad(a, b, name):
    l, m = a.shape
    n = b.shape[1]
    bm = m if m <= 512 else 512
    bl = min(l, 2048 if n <= 1024 else 1024)

    def body(a_ref, b_ref, o_ref):
        _accumulate(o_ref, _mm_tn(a_ref[...], b_ref[...]), pl.program_id(1) == 0)

    return pl.pallas_call(
        body, name=name, grid=(m // bm, l // bl),
        in_specs=[pl.BlockSpec((bl, bm), lambda i, j: (j, i)), pl.BlockSpec((bl, n), lambda i, j: (j, 0))],
        out_specs=pl.BlockSpec((bm, n), lambda i, j: (i, 0)),
        out_shape=jax.ShapeDtypeStruct((m, n), F32),
        compiler_params=_cparams("parallel", "arbitrary"),
    )(a, b)


def _wgrad_into(a, b, member, cut, dest=None):
    group, off, rs, cs = _place_in_group(member)
    l = a.shape[0]
    bm = min(rs, 512)
    bl = min(l, 2048)
    nb = rs // bm
    if cut == "row":
        a_spec = pl.BlockSpec((bl, bm), lambda j, i, k: (k, j * nb + i))
        b_spec = pl.BlockSpec((bl, cs), lambda j, i, k: (k, 0))
    else:
        a_spec = pl.BlockSpec((bl, bm), lambda j, i, k: (k, i))
        b_spec = pl.BlockSpec((bl, cs), lambda j, i, k: (k, j))

    def body(a_ref, b_ref, *rest):
        o_ref = rest[-1]
        part = _mm_tn(a_ref[...], b_ref[...])

        @pl.when(pl.program_id(2) == 0)
        def _():
            o_ref[0] = part

        @pl.when(pl.program_id(2) != 0)
        def _():
            o_ref[0] += part

    operands, in_specs, aliases = [a, b], [a_spec, b_spec], {}
    if dest is not None:
        operands.append(dest)
        in_specs.append(ANY)
        aliases = {2: 0}
    return pl.pallas_call(
        body, name="wgrad_" + member, grid=(4, nb, l // bl), in_specs=in_specs,
        out_specs=pl.BlockSpec((1, bm, cs), lambda j, i, k: (j, off // bm + i, 0)),
        out_shape=jax.ShapeDtypeStruct((4, _group_rows(group), cs), F32), input_output_aliases=aliases,
        compiler_params=_cparams("parallel", "parallel", "arbitrary"),
    )(*operands)


def _adamw(w, g, m, v, name, g_off, token):
    r, c = w.shape
    br = r
    for cand in (256, 128, 64, 32, 16, 8):
        if r % cand == 0 and g_off % cand == 0:
            br = cand
            break

    def body(w_ref, g_ref, m_ref, v_ref, token_ref, go_ref, d_ref, nm_ref, nv_ref):
        gv = g_ref[...]
        go_ref[...] = gv
        nm = ADAM_B1 * m_ref[...] + (1.0 - ADAM_B1) * gv
        nv = ADAM_B2 * v_ref[...] + (1.0 - ADAM_B2) * (gv * gv)
        m_hat = nm / (1.0 - ADAM_B1 ** ADAM_STEP)
        v_hat = nv / (1.0 - ADAM_B2 ** ADAM_STEP)
        d_ref[...] = -ADAM_LR * (m_hat / (jnp.sqrt(v_hat) + ADAM_EPS) + ADAM_WD * w_ref[...])
        nm_ref[...] = nm
        nv_ref[...] = nv

    spec = lambda: pl.BlockSpec((br, c), lambda i: (i, 0))
    g_spec = pl.BlockSpec((br, c), lambda i: (g_off // br + i, 0))
    shp = jax.ShapeDtypeStruct((r, c), F32)
    return pl.pallas_call(
        body, name=name, grid=(r // br,), in_specs=[spec(), g_spec, spec(), spec(), ANY],
        out_specs=[spec(), spec(), spec(), spec()], out_shape=[shp, shp, shp, shp], compiler_params=_cparams("parallel"),
    )(w, g, m, v, token)


def _place():
    return lax.axis_index("x"), lax.axis_index("y"), lax.axis_index("c")


def _other_chips(x, y):
    return [(1 - x, y), (x, 1 - y), (1 - x, 1 - y)]


ANY = pl.BlockSpec(memory_space=pl.ANY)


def _gather_weights(bufs):
    n = len(bufs)

    def body(*refs):
        outs, send_sems, recv_sems = refs[n:2 * n], refs[2 * n], refs[2 * n + 1]
        x, y, c = _place()
        chips = _other_chips(x, y)

        def part(g, px, py, pc):
            half = outs[g].shape[1] // 2
            return outs[g].at[2 * px + py, pl.ds(pl.multiple_of(pc * half, 16), half), :]

        def copy(k, src, dst, to):
            return pltpu.make_async_remote_copy(src_ref=src, dst_ref=dst, send_sem=send_sems.at[k], recv_sem=recv_sems.at[k],
                                                device_id=to, device_id_type=MESH)

        first = [copy(6 * g + j, part(g, x, y, c), part(g, x, y, c), (*chip, c)) for g in range(n) for j, chip in enumerate(chips)]
        for cp in first:
            cp.start()
        passed = []
        for g in range(n):
            for j, chip in enumerate(chips):
                landed = part(g, *chip, c)
                copy(6 * g + j, landed, landed, (x, y, c)).wait_recv()
                passed.append(copy(6 * g + 3 + j, landed, landed, (x, y, 1 - c)))
                passed[-1].start()
        for g in range(n):
            for j, chip in enumerate(chips):
                other = part(g, *chip, 1 - c)
                copy(6 * g + 3 + j, other, other, (x, y, c)).wait_recv()
        for cp in first + passed:
            cp.wait_send()

    return pl.pallas_call(
        body, name="gather_weights", in_specs=[ANY] * n, out_specs=[ANY] * n,
        out_shape=[jax.ShapeDtypeStruct(b.shape, b.dtype) for b in bufs], input_output_aliases={g: g for g in range(n)},
        scratch_shapes=[pltpu.SemaphoreType.DMA((6 * n,)), pltpu.SemaphoreType.DMA((6 * n,))],
    )(*bufs)


def _cast_shards(shards, group, place):
    width, members = GROUPS[group]
    rows = _group_rows(group)

    def body(place_ref, *refs):
        out = refs[-1]
        off = 0
        for ref, (_, r) in zip(refs[:-1], members):
            out[0, off:off + r, :] = ref[...].astype(BF16)
            off += r

    grid_spec = pltpu.PrefetchScalarGridSpec(
        num_scalar_prefetch=1, grid=(1,),
        in_specs=[pl.BlockSpec((r, width), lambda i, p: (0, 0)) for _, r in members],
        out_specs=pl.BlockSpec((1, rows, width), lambda i, p: (p[0], 0, 0)))
    return pl.pallas_call(
        body, name="cast_shards_" + group, grid_spec=grid_spec, out_shape=jax.ShapeDtypeStruct((4, rows, width), BF16),
        compiler_params=_cparams("arbitrary"),
    )(place, *[shards[name] for name, _ in members])


def _block_rows(h):
    return next(cand for cand in (256, 192, 128, 64, 32, 16) if h % cand == 0)


def _add_pair(buf, got, place, name):
    n, h, w = got.shape
    bh = _block_rows(h)
    nb = h // bh

    def body(place_ref, a_ref, b_ref, s_ref, sb_ref):
        s = a_ref[...] + b_ref[...]
        s_ref[...] = s
        sb_ref[...] = s.astype(BF16)

    spec = lambda: pl.BlockSpec((1, bh, w), lambda j, i, p: (j, i, 0))
    grid_spec = pltpu.PrefetchScalarGridSpec(
        num_scalar_prefetch=1, grid=(n, nb),
        in_specs=[pl.BlockSpec((1, bh, w), lambda j, i, p: (j, p[1] * nb + i, 0)), spec()], out_specs=[spec(), spec()])
    return pl.pallas_call(
        body, name=name, grid_spec=grid_spec,
        out_shape=[jax.ShapeDtypeStruct(got.shape, F32), jax.ShapeDtypeStruct(got.shape, BF16)],
        compiler_params=_cparams("parallel", "parallel"),
    )(place, buf, got)


def _add_received(pair, got, place, name):
    _, h, w = pair.shape
    bh = _block_rows(h)
    nb = h // bh

    def body(place_ref, own_ref, got_ref, o_ref):
        o_ref[...] = ((own_ref[0] + got_ref[0].astype(F32)) + got_ref[1].astype(F32)) + got_ref[2].astype(F32)

    grid_spec = pltpu.PrefetchScalarGridSpec(
        num_scalar_prefetch=1, grid=(nb,),
        in_specs=[pl.BlockSpec((1, bh, w), lambda i, p: (p[0], i, 0)), pl.BlockSpec((3, bh, w), lambda i, p: (0, i, 0))],
        out_specs=pl.BlockSpec((bh, w), lambda i, p: (p[1] * nb + i, 0)))
    return pl.pallas_call(
        body, name=name, grid_spec=grid_spec, out_shape=jax.ShapeDtypeStruct((2 * h, w), F32),
        compiler_params=_cparams("parallel"),
    )(place, pair, got)


def _swap_reduced_halves(bufs):
    n = len(bufs)

    def body(*refs):
        outs, send_sems, recv_sems = refs[n:2 * n], refs[2 * n], refs[2 * n + 1]
        x, y, c = _place()
        copies = []
        for g in range(n):
            half = outs[g].shape[0] // 2
            own = outs[g].at[pl.ds(pl.multiple_of(c * half, 8), half), :]
            copies.append(pltpu.make_async_remote_copy(src_ref=own, dst_ref=own, send_sem=send_sems.at[g],
                                                       recv_sem=recv_sems.at[g], device_id=(x, y, 1 - c), device_id_type=MESH))
        for cp in copies:
            cp.start()
        for g in range(n):
            half = outs[g].shape[0] // 2
            other = outs[g].at[pl.ds(pl.multiple_of((1 - c) * half, 8), half), :]
            pltpu.make_async_remote_copy(src_ref=other, dst_ref=other, send_sem=send_sems.at[g], recv_sem=recv_sems.at[g],
                                         device_id=(x, y, 1 - c), device_id_type=MESH).wait_recv()
        for cp in copies:
            cp.wait_send()

    return pl.pallas_call(
        body, name="swap_reduced_halves", in_specs=[ANY] * n, out_specs=[ANY] * n,
        out_shape=[jax.ShapeDtypeStruct(b.shape, b.dtype) for b in bufs], input_output_aliases={g: g for g in range(n)},
        scratch_shapes=[pltpu.SemaphoreType.DMA((n,)), pltpu.SemaphoreType.DMA((n,))],
    )(*bufs)


HBM = pl.BlockSpec(memory_space=pltpu.HBM)
SEM = pl.BlockSpec(memory_space=pltpu.SEMAPHORE)


def _copies_start(name, bufs, n_copies, plan, after=None):
    n = len(bufs)
    extra = [] if after is None else [after]

    def body(*refs):
        sems = refs[n + len(extra):n + len(extra) + 2 * n_copies]
        x, y, c = _place()
        for i, (src, dst, dev) in enumerate(plan(refs[:n], x, y, c)):
            pltpu.make_async_remote_copy(src_ref=src, dst_ref=dst, send_sem=sems[i], recv_sem=sems[n_copies + i],
                                         device_id=dev, device_id_type=MESH).start()
        token = refs[-1]
        token[...] = jnp.zeros_like(token)

    out = pl.pallas_call(
        body, name=name,
        out_shape=[pltpu.SemaphoreType.DMA(())] * (2 * n_copies) + [pltpu.HBM(b.shape, b.dtype) for b in bufs]
        + [jax.ShapeDtypeStruct((8, 128), F32)],
        in_specs=[HBM] * n + [ANY] * len(extra),
        out_specs=[SEM] * (2 * n_copies) + [HBM] * n + [pl.BlockSpec(memory_space=pltpu.VMEM)],
        input_output_aliases={i: 2 * n_copies + i for i in range(n)},
        compiler_params=pltpu.CompilerParams(has_side_effects=pltpu.SideEffectType.DATAFLOW_SIDE_EFFECTING),
    )(*[pltpu.with_memory_space_constraint(b, pltpu.HBM) for b in bufs], *extra)
    return list(out[:2 * n_copies]), list(out[2 * n_copies:-1]), out[-1]


def _copies_wait(name, bufs, sems, after, plan):
    n = len(bufs)
    k = len(sems) // 2

    def body(*refs):
        sem_refs = refs[n:n + 2 * k]
        x, y, c = _place()
        for i, (sent, landed, dev) in enumerate(plan(refs[:n], x, y, c)):
            cp = pltpu.make_async_remote_copy(src_ref=sent, dst_ref=landed, send_sem=sem_refs[i], recv_sem=sem_refs[k + i],
                                              device_id=dev, device_id_type=MESH)
            cp.wait_send()
            cp.wait_recv()

    return pl.pallas_call(
        body, name=name, out_shape=[pltpu.HBM(b.shape, b.dtype) for b in bufs],
        in_specs=[HBM] * n + [SEM] * (2 * k) + [ANY], out_specs=[HBM] * n, input_output_aliases={i: i for i in range(n)},
        compiler_params=pltpu.CompilerParams(has_side_effects=pltpu.SideEffectType.DATAFLOW_SIDE_EFFECTING),
    )(*bufs, *sems, after)


def _row_half(ref, which, axis):
    half = ref.shape[axis] // 2
    rows = pl.ds(pl.multiple_of(which * half, 8), half)
    return ref.at[rows, :] if axis == 0 else ref.at[:, rows, :]


class _SplitGather:
    def __init__(self, own, after):
        self.n = len(own)
        self.state = _copies_start("gather_start", own, 3 * self.n, self._sent, after)

    @staticmethod
    def _sent(refs, x, y, c):
        return [(w.at[2 * x + y], w.at[2 * x + y], (px, py, c)) for w in refs for px, py in _other_chips(x, y)]

    @staticmethod
    def _landed(refs, x, y, c):
        return [(w.at[2 * x + y], w.at[2 * px + py], (px, py, c)) for w in refs for px, py in _other_chips(x, y)]

    def token(self):
        return self.state[2]

    def wait(self, which, name, after):
        sems, bufs, _ = self.state
        k = 3 * self.n
        mine = [sems[3 * i + j] for i in which for j in range(3)] + [sems[k + 3 * i + j] for i in which for j in range(3)]
        return _copies_wait(name, [bufs[i] for i in which], mine, after, self._landed)


class _SplitReduction:
    def __init__(self, tag, groups, place):
        self.tag, self.groups, self.place = tag, groups, place

    def start_pair(self, bufs):
        n = len(bufs)
        lands = [lax.empty((4, b.shape[1] // 2, b.shape[2]), F32) for b in bufs]
        plan = lambda refs, x, y, c: [(_row_half(refs[i], 1 - c, 1), refs[n + i], (x, y, 1 - c)) for i in range(n)]
        self._pair = (_copies_start("pair_%s_start" % self.tag, bufs + lands, n, plan), plan, n)
        return self._pair[0][2]

    def pair_done_start_scatter(self, after):
        (sems, bufs, _), plan, n = self._pair
        out = _copies_wait("pair_%s_wait" % self.tag, bufs, sems, after, plan)
        pairs = [_add_pair(out[i], out[n + i], self.place, "add_pair_" + g) for i, g in enumerate(self.groups)]
        self._pair_f32 = [p[0] for p in pairs]
        lands = [lax.empty((3,) + p[1].shape[1:], BF16) for p in pairs]
        plan = lambda refs, x, y, c: [(refs[i].at[2 * px + py], refs[n + i].at[j], (px, py, c))
                                      for i in range(n) for j, (px, py) in enumerate(_other_chips(x, y))]
        self._scatter = (_copies_start("scatter_%s_start" % self.tag, [p[1] for p in pairs] + lands, 3 * n, plan), plan, n)
        return self._scatter[0][2]

    def scatter_done(self, after):
        (sems, bufs, _), plan, n = self._scatter
        out = _copies_wait("scatter_%s_wait" % self.tag, bufs, sems, after, plan)
        return [_add_received(self._pair_f32[i], out[n + i], self.place, "add_received_" + g)
                for i, g in enumerate(self.groups)]

    def start_join(self, halves):
        n = len(halves)
        sent = lambda refs, x, y, c: [(_row_half(r, c, 0), _row_half(r, c, 0), (x, y, 1 - c)) for r in refs]
        landed = lambda refs, x, y, c: [(_row_half(r, c, 0), _row_half(r, 1 - c, 0), (x, y, 1 - c)) for r in refs]
        self._join = (_copies_start("join_%s_start" % self.tag, halves, n, sent), landed)
        return self._join[0][2]

    def join_done(self, after):
        (sems, bufs, _), landed = self._join
        return _copies_wait("join_%s_wait" % self.tag, bufs, sems, after, landed)


def _pair_sum_small(mine):
    rows, w = mine.shape

    def body(in_ref, out_ref, sibling, send_sem, recv_sem):
        x, y, c = _place()
        swap = pltpu.make_async_remote_copy(src_ref=in_ref, dst_ref=sibling, send_sem=send_sem, recv_sem=recv_sem,
                                            device_id=(x, y, 1 - c), device_id_type=MESH)
        swap.start()
        swap.wait()
        out_ref[...] = in_ref[...] + sibling[...]

    return pl.pallas_call(
        body, name="pair_sum_small", out_shape=jax.ShapeDtypeStruct((rows, w), F32),
        in_specs=[pl.BlockSpec(memory_space=pltpu.VMEM)], out_specs=pl.BlockSpec(memory_space=pltpu.VMEM),
        scratch_shapes=[pltpu.VMEM((rows, w), F32), pltpu.SemaphoreType.DMA, pltpu.SemaphoreType.DMA],
        compiler_params=pltpu.CompilerParams(vmem_limit_bytes=VMEM_LIMIT_V7X),
    )(mine)


class _SplitChipSum:
    def __init__(self, pair, place):
        self.place = place
        slots = lax.empty((4,) + pair.shape, F32)
        sent = lambda refs, x, y, c: [(refs[0], refs[1].at[2 * x + y], (px, py, c)) for px, py in _other_chips(x, y)]
        self.landed = lambda refs, x, y, c: [(refs[0], refs[1].at[2 * px + py], (px, py, c)) for px, py in _other_chips(x, y)]
        self.state = _copies_start("small_sum_start", [pair, slots], 3, sent)

    def token(self):
        return self.state[2]

    def done(self, after):
        sems, bufs, _ = self.state
        pair, slots = _copies_wait("small_sum_wait", bufs, sems, after, self.landed)
        rows, w = pair.shape

        def body(place_ref, pair_ref, slots_ref, out_ref):
            for j in range(4):
                own = place_ref[0] == j

                @pl.when(own)
                def _():
                    out_ref[...] = pair_ref[...] if j == 0 else out_ref[...] + pair_ref[...]

                @pl.when(jnp.logical_not(own))
                def _():
                    out_ref[...] = slots_ref[j] if j == 0 else out_ref[...] + slots_ref[j]

        grid_spec = pltpu.PrefetchScalarGridSpec(
            num_scalar_prefetch=1, grid=(1,),
            in_specs=[pl.BlockSpec((rows, w), lambda i, p: (0, 0)), pl.BlockSpec((4, rows, w), lambda i, p: (0, 0, 0))],
            out_specs=pl.BlockSpec((rows, w), lambda i, p: (0, 0)))
        return pl.pallas_call(
            body, name="small_sum_add", grid_spec=grid_spec, out_shape=jax.ShapeDtypeStruct((rows, w), F32),
            compiler_params=_cparams("arbitrary"),
        )(self.place, pair, slots)


def _join_column_shards(g):
    return jnp.transpose(g, (1, 0, 2)).reshape(g.shape[1], 4 * g.shape[2])


def _split_column_shards(w):
    r = w.shape[0]
    return jnp.transpose(w.reshape(r, 4, w.shape[1] // 4), (1, 0, 2))


def _small_rows(shape):
    return -(-int(np.prod(shape)) // 1024)


def _pack_small(vals):
    segs = []
    for name, shape in SMALL_WEIGHTS:
        flat = vals[name].reshape(-1)
        segs.append(jnp.pad(flat, (0, _small_rows(shape) * 1024 - flat.shape[0])))
    total = sum(s.shape[0] for s in segs) // 1024
    segs.append(jnp.zeros((-total % 8 * 1024,), F32))
    return jnp.concatenate(segs).reshape(-1, 1024)


def _unpack_small(packed):
    out, off = {}, 0
    for name, shape in SMALL_WEIGHTS:
        rows = _small_rows(shape)
        out[name] = packed[off:off + rows].reshape(-1)[:int(np.prod(shape))].reshape(shape)
        off += rows
    return out


W_IN_SHARD = D_IN // 4
W_IN_GAP = 1216


def _pad_w_in(g):
    cut = W_IN_GAP - W_IN_SHARD
    return jnp.concatenate([g[0], g[1][:, :cut], jnp.zeros((g.shape[1], D_IN_PAD - D_IN), g.dtype), g[1][:, cut:], g[2], g[3]],
                           axis=1)


def _unpad_w_in(g):
    skip = D_IN_PAD - D_IN
    second = jnp.concatenate([g[:, W_IN_SHARD:W_IN_GAP], g[:, W_IN_GAP + skip:2 * W_IN_SHARD + skip]], axis=1)
    return jnp.stack([g[:, :W_IN_SHARD], second, g[:, 2 * W_IN_SHARD + skip:3 * W_IN_SHARD + skip],
                      g[:, 3 * W_IN_SHARD + skip:]])


def _pad_heads(w):
    r = w.shape[0]
    return jnp.pad(w.reshape(r, N_HEADS, QK_HEAD), ((0, 0), (0, 0), (0, HEAD_PAD - QK_HEAD))).reshape(r, N_HEADS * HEAD_PAD)


def _unpad_heads(g):
    r = g.shape[0]
    return g.reshape(r, N_HEADS, HEAD_PAD)[:, :, :QK_HEAD].reshape(r, N_HEADS * QK_HEAD)


def _local_step(x, positions, tgt, grp_b, small, gather, red_a, red_rest):
    l = x.shape[0]
    t = min(l, 512)
    t_mlp = min(l, 256)
    tq = min(l, 1024)
    tc = min(l, 256)
    row = lambda v: v.reshape(1, -1).astype(F32)

    w_in_p = _pad_w_in(grp_b)
    g1, g2 = row(small["norm_mix"]), row(small["norm_mlp"])
    gqa, gkva = row(small["q_a_norm"]), row(small["kv_a_norm"])
    gq = jnp.pad(row(small["q_norm"]), ((0, 0), (0, HEAD_PAD - QK_HEAD)))
    gk = jnp.pad(row(small["k_norm"]), ((0, 0), (0, HEAD_PAD - QK_HEAD)))
    half = QK_ROPE // 2
    inv_freq = ROPE_THETA ** (-jnp.arange(half, dtype=F32) / half)
    invf = jnp.concatenate([inv_freq, inv_freq, jnp.zeros((64,), F32)]).reshape(1, 128)
    sgn = jnp.concatenate([-jnp.ones((half,), F32), jnp.ones((half,), F32), jnp.zeros((64,), F32)]).reshape(1, 128)
    pos = positions.reshape(l, 1)

    a_re, a_im = small["ssm_a_re"], small["ssm_a_im"]
    log_dt = small["ssm_log_dt"].reshape(SSM_GROUPS, 1)
    to_gcp = lambda b: jnp.transpose(b, (0, 2, 1)).reshape(SSM_WIDTH, SSM_STATE)
    from_gcp = lambda b: jnp.transpose(b.reshape(SSM_GROUPS, SSM_GROUP_CH, SSM_STATE), (0, 2, 1))
    b_re, b_im = to_gcp(small["ssm_b_re"]), to_gcp(small["ssm_b_im"])
    c_re, c_im = small["ssm_c_re"].reshape(SSM_WIDTH, SSM_STATE), small["ssm_c_im"].reshape(SSM_WIDTH, SSM_STATE)
    wb, wc, tabs_fwd, tabs_rev = _ssm_param_fwd(a_re, a_im, log_dt, b_re, b_im, c_re, c_im)
    dskip = row(small["ssm_d"])
    b_glu = row(small["b_glu"])

    u, lat, gs, gm = _in_proj_fwd(x, g1, w_in_p, t, gather.token())
    grp_c, grp_d, grp_e = gather.wait([0, 1, 2], "gather_cde_wait", u)
    w_qb_p = _pad_heads(_join_column_shards(grp_c))
    xr, xi, y, y_ssm = _ssm_fwd(u, wb, wc, tabs_fwd, dskip, grp_d, b_glu, grp_e, tc)
    q, k, v = _mla_pre_fwd(lat, pos, invf, sgn, gqa, gkva, gq, gk, w_qb_p, grp_d, t)
    attn, lse = _attn_fwd(q, k, v, tq)
    (grp_a,) = gather.wait([3], "gather_a_wait", attn)
    y_mla, mixed, h = _merge_fwd(attn, y_ssm, gs, gm, x, grp_a, t)
    dh, hn, da, hid, dout, loss_blk, g_norm_mlp = _mlp_fwd_bwd(h, tgt, g2, grp_a, t_mlp)

    ga = _wgrad_into(hn, da, "w_up", "col", _wgrad_into(hid, dout, "w_down", "row"))
    dys, dym, dgs, dgm, dattn = _merge_bwd(dh, y_ssm, y_mla, gs, gm, grp_a, t)
    ga = _wgrad_into(attn, dym, "w_o_mla", "row", _wgrad_into(mixed, dh, "w_out", "row", ga))

    dq, dk, dv = _attn_bwd(q, k, v, attn, dattn, lse, tq, red_a.start_pair([ga]))
    d_lat, ql, dq0, ckn, dkv, g_qa, g_kva, g_q, g_k = _mla_pre_bwd(lat, pos, invf, sgn, gqa, gkva, gq, gk, w_qb_p, grp_d,
                                                                    dq, dk, dv, t, red_a.pair_done_start_scatter(dk))
    gc = _split_column_shards(_unpad_heads(_wgrad(ql, dq0, "wgrad_q_b")))

    d_u, adj, dy, z, z2, dpre, g_b_glu, g_d, g_lr, g_li = _ssm_bwd(
        dys, y, u, xr, xi, wb, wc, tabs_rev, dskip, grp_d, b_glu, grp_e, tc)
    gd = _wgrad_into(z, dpre, "w_glu", "row", _wgrad_into(ckn, dkv, "w_kv_b", "col"))
    ge = _wgrad_into(z2, dys, "w_o_ssm", "col")
    grad_x, xn, dproj, g_norm_mix = _in_proj_bwd(x, g1, w_in_p, d_u, d_lat, dgs, dgm, dh, t)
    gb = _unpad_w_in(_wgrad(xn, dproj, "wgrad_in"))

    red_a.start_join(red_a.scatter_done(gb))
    g_wb = _wgrad_strips(u, adj, adj, "wgrad_ssm_b", 1, red_rest.start_pair([gb, gc, gd, ge]))
    g_wct = _wgrad_strips(dy, xr, xi, "wgrad_ssm_c", 0, red_rest.pair_done_start_scatter(g_wb))
    g_ar, g_ai, g_ldt, g_br, g_bi, g_cr, g_ci = _ssm_param_bwd(a_re, a_im, log_dt, b_re, b_im, g_lr, g_li, g_wb, g_wct)

    g_small = {
        "norm_mix": g_norm_mix.reshape(-1), "norm_mlp": g_norm_mlp.reshape(-1), "q_a_norm": g_qa.reshape(-1),
        "kv_a_norm": g_kva.reshape(-1), "q_norm": g_q.reshape(-1)[:QK_HEAD], "k_norm": g_k.reshape(-1)[:QK_HEAD],
        "ssm_a_re": g_ar, "ssm_a_im": g_ai, "ssm_log_dt": g_ldt.reshape(-1),
        "ssm_b_re": from_gcp(g_br), "ssm_b_im": from_gcp(g_bi),
        "ssm_c_re": g_cr.reshape(SSM_GROUPS, SSM_GROUP_CH, SSM_STATE), "ssm_c_im": g_ci.reshape(SSM_GROUPS, SSM_GROUP_CH, SSM_STATE),
        "ssm_d": g_d.reshape(SSM_GROUPS, SSM_GROUP_CH), "b_glu": g_b_glu.reshape(-1),
    }
    return loss_blk[0, 0], grad_x, g_small


def kernel(x, positions, norm_mix, w_in, q_a_norm, kv_a_norm, w_q_b, w_kv_b, q_norm, k_norm, w_o_mla, ssm_a_re, ssm_a_im, ssm_log_dt, ssm_b_re, ssm_b_im, ssm_c_re, ssm_c_im, ssm_d, w_glu, b_glu, w_o_ssm, w_out, norm_mlp, w_up, w_down, loss_target, m_norm_mix, m_w_in, m_q_a_norm, m_kv_a_norm, m_w_q_b, m_w_kv_b, m_q_norm, m_k_norm, m_w_o_mla, m_ssm_a_re, m_ssm_a_im, m_ssm_log_dt, m_ssm_b_re, m_ssm_b_im, m_ssm_c_re, m_ssm_c_im, m_ssm_d, m_w_glu, m_b_glu, m_w_o_ssm, m_w_out, m_norm_mlp, m_w_up, m_w_down, v_norm_mix, v_w_in, v_q_a_norm, v_kv_a_norm, v_w_q_b, v_w_kv_b, v_q_norm, v_k_norm, v_w_o_mla, v_ssm_a_re, v_ssm_a_im, v_ssm_log_dt, v_ssm_b_re, v_ssm_b_im, v_ssm_c_re, v_ssm_c_im, v_ssm_d, v_w_glu, v_b_glu, v_w_o_ssm, v_w_out, v_norm_mlp, v_w_up, v_w_down):
    args = dict(locals())
    w = {n: args[n][0] for n in WEIGHT_ORDER}
    m = {n: args["m_" + n][0] for n in WEIGHT_ORDER}
    v = {n: args["v_" + n][0] for n in WEIGHT_ORDER}
    big_names = [n for n, *_ in BIG_WEIGHTS]
    small_names = [n for n, _ in SMALL_WEIGHTS]

    place = jnp.stack([2 * lax.axis_index("x") + lax.axis_index("y"), lax.axis_index("c")]).astype(jnp.int32)
    rest = ["b", "c", "d", "e"]

    (grp_b,) = _gather_weights([_cast_shards(w, "b", place)])
    gather = _SplitGather([_cast_shards(w, g, place) for g in ("c", "d", "e", "a")], grp_b)
    red_a = _SplitReduction("a", ["a"], place)
    red_rest = _SplitReduction("rest", rest, place)
    small = {n: w[n] for n in small_names}

    loss_local, grad_x, g_small = _local_step(x[0], positions[0], loss_target[0], grp_b, small, gather, red_a, red_rest)
    loss = lax.psum(loss_local, ("x", "y", "c"))

    grad_w, delta_w, new_m, new_v = {}, {}, {}, {}

    def update(names, reduced, token):
        for n in names:
            g, off, _, _ = _place_in_group(n)
            grad_w[n], delta_w[n], new_m[n], new_v[n] = _adamw(w[n], reduced[g], m[n], v[n], "adamw_" + n, off, token)
            token = new_v[n]

    chip_sum = _SplitChipSum(_pair_sum_small(_pack_small(g_small)), place)
    in_a = [n for n, _ in GROUPS["a"][1]]
    update(in_a, {"a": red_a.join_done(chip_sum.token())[0]}, chip_sum.token())
    small_sum = chip_sum.done(new_v[in_a[-1]])
    g_s, d_s, m_s, v_s = _adamw(_pack_small(small), small_sum, _pack_small({n: m[n] for n in small_names}),
                                _pack_small({n: v[n] for n in small_names}), "adamw_small", 0, small_sum)
    g_s, d_s, m_s, v_s = _unpack_small(g_s), _unpack_small(d_s), _unpack_small(m_s), _unpack_small(v_s)
    for n in small_names:
        grad_w[n], delta_w[n], new_m[n], new_v[n] = g_s[n], d_s[n], m_s[n], v_s[n]
    halves = red_rest.scatter_done(v_s[small_names[0]])
    reduced_rest = dict(zip(rest, _swap_reduced_halves(halves)))
    update([n for n in big_names if n not in in_a], reduced_rest, halves[0])

    lead = lambda d: [d[n][None] for n in WEIGHT_ORDER]
    return (loss, grad_x[None], *lead(grad_w), *lead(delta_w), *lead(new_m), *lead(new_v))
```

```python
import math

import jax
import jax.numpy as jnp
import numpy as np
from jax import lax
from jax.experimental import pallas as pl
from jax.experimental.pallas import tpu as pltpu

F32 = jnp.float32
BF16 = jnp.bfloat16

D_MODEL = 1024
SSM_GROUPS = 32
SSM_GROUP_CH = 16
SSM_WIDTH = 512
SSM_STATE = 64
GP = SSM_GROUPS * SSM_STATE
N_HEADS = 8
QK_NOPE = 128
QK_ROPE = 64
QK_HEAD = 192
HEAD_PAD = 256
V_HEAD = 128
Q_LORA = 384
KV_LORA = 256
LAT_W = 768
D_IN = 3264
D_IN_PAD = 3328
D_FF = 4096
ROPE_THETA = 10000.0
EPS = 1e-6
ATT_SCALE = QK_HEAD ** -0.5

ADAM_LR = 0.001
ADAM_B1 = 0.9
ADAM_B2 = 0.999
ADAM_EPS = 1e-08
ADAM_WD = 0.01
ADAM_STEP = 10

VMEM_LIMIT_V7X = 56 * 1024 * 1024
MESH = pl.DeviceIdType.MESH

BIG_WEIGHTS = (
    ("w_in", 1024, 3264, "col"),
    ("w_q_b", 384, 1536, "col"),
    ("w_kv_b", 256, 2048, "col"),
    ("w_o_mla", 1024, 1024, "row"),
    ("w_glu", 512, 512, "row"),
    ("w_o_ssm", 512, 1024, "col"),
    ("w_out", 1024, 1024, "row"),
    ("w_up", 1024, 4096, "col"),
    ("w_down", 4096, 1024, "row"),
)
GROUPS = {
    "a": (1024, (("w_down", 1024), ("w_up", 1024), ("w_o_mla", 256), ("w_out", 256))),
    "b": (816, (("w_in", 1024),)),
    "c": (384, (("w_q_b", 384),)),
    "d": (512, (("w_kv_b", 256), ("w_glu", 128))),
    "e": (256, (("w_o_ssm", 512),)),
}


def _group_rows(group):
    return sum(r for _, r in GROUPS[group][1])


def _place_in_group(name):
    for group, (width, members) in GROUPS.items():
        off = 0
        for member, rows in members:
            if member == name:
                return group, off, rows, width
            off += rows
    raise KeyError(name)


SMALL_WEIGHTS = (
    ("norm_mix", (1024,)), ("q_a_norm", (384,)), ("kv_a_norm", (256,)), ("q_norm", (192,)), ("k_norm", (192,)),
    ("ssm_a_re", (32, 64)), ("ssm_a_im", (32, 64)), ("ssm_log_dt", (32,)),
    ("ssm_b_re", (32, 64, 16)), ("ssm_b_im", (32, 64, 16)), ("ssm_c_re", (32, 16, 64)), ("ssm_c_im", (32, 16, 64)),
    ("ssm_d", (32, 16)), ("b_glu", (512,)), ("norm_mlp", (1024,)),
)
WEIGHT_ORDER = ('norm_mix', 'w_in', 'q_a_norm', 'kv_a_norm', 'w_q_b', 'w_kv_b', 'q_norm', 'k_norm', 'w_o_mla', 'ssm_a_re',
                'ssm_a_im', 'ssm_log_dt', 'ssm_b_re', 'ssm_b_im', 'ssm_c_re', 'ssm_c_im', 'ssm_d', 'w_glu', 'b_glu',
                'w_o_ssm', 'w_out', 'norm_mlp', 'w_up', 'w_down')


def _cparams(*sem):
    return pltpu.CompilerParams(dimension_semantics=sem if sem else None, vmem_limit_bytes=VMEM_LIMIT_V7X)


def _resident(shape, index=None):
    index = (0,) * len(shape) if index is None else index
    return pl.BlockSpec(shape, lambda *_: index, pipeline_mode=pl.Buffered(1))


def _member_block(name):
    _, off, rows, width = _place_in_group(name)
    return _resident((4, rows, width), (0, off // rows, 0))


def _rows(t, width):
    return pl.BlockSpec((t, width), lambda i: (i, 0))


def _mm(a, b):
    return jnp.dot(a.astype(BF16), b.astype(BF16), preferred_element_type=F32)


def _mm_nt(a, b):
    return lax.dot_general(a.astype(BF16), b.astype(BF16), (((1,), (1,)), ((), ())), preferred_element_type=F32)


def _mm_tn(a, b):
    return lax.dot_general(a.astype(BF16), b.astype(BF16), (((0,), (0,)), ((), ())), preferred_element_type=F32)


def _rms_fwd(x, g, n):
    r = lax.rsqrt(jnp.sum(x * x, axis=-1, keepdims=True) * (1.0 / n) + EPS)
    return x * r * g


def _rms_bwd(x, g, dy, n):
    r = lax.rsqrt(jnp.sum(x * x, axis=-1, keepdims=True) * (1.0 / n) + EPS)
    xh = x * r
    dxh = dy * g
    dx = r * (dxh - xh * (jnp.sum(dxh * xh, axis=-1, keepdims=True) * (1.0 / n)))
    return dx, dy * xh


def _colsum(a):
    return jnp.sum(a, axis=0, keepdims=True)


def _accumulate(ref, value, first):
    @pl.when(first)
    def _():
        ref[...] = value

    @pl.when(jnp.logical_not(first))
    def _():
        ref[...] += value


def _sigmoid(a):
    return 1.0 / (1.0 + jnp.exp(-a))


GELU_C = math.sqrt(2.0 / math.pi)
GELU_A = 0.044715


def _gelu(y):
    return 0.5 * y * (1.0 + jnp.tanh(GELU_C * (y + GELU_A * y * y * y)))


def _gelu_grad(y):
    t = jnp.tanh(GELU_C * (y + GELU_A * y * y * y))
    return 0.5 * (1.0 + t) + 0.5 * y * (1.0 - t * t) * GELU_C * (1.0 + 3.0 * GELU_A * y * y)


def _in_proj_fwd(x, g1, w_in_p, t, token):
    l = x.shape[0]

    def body(x_ref, g_ref, w_ref, token_ref, u_ref, lat_ref, gs_ref, gm_ref):
        xn = _rms_fwd(x_ref[...], g_ref[...], D_MODEL).astype(BF16)
        u_ref[...] = _mm(xn, w_ref[:, 0:512])
        lat_ref[...] = _mm(xn, w_ref[:, 512:1280])
        gs_ref[...] = _mm(xn, w_ref[:, 1280:2304]).astype(BF16)
        gm_ref[...] = _mm(xn, w_ref[:, 2304:3328]).astype(BF16)

    return pl.pallas_call(
        body, name="in_proj_fwd", grid=(l // t,),
        in_specs=[_rows(t, D_MODEL), _resident((1, D_MODEL)), _resident((D_MODEL, D_IN_PAD)), ANY],
        out_specs=[_rows(t, 512), _rows(t, LAT_W), _rows(t, D_MODEL), _rows(t, D_MODEL)],
        out_shape=[jax.ShapeDtypeStruct((l, 512), F32), jax.ShapeDtypeStruct((l, LAT_W), F32),
                   jax.ShapeDtypeStruct((l, D_MODEL), BF16), jax.ShapeDtypeStruct((l, D_MODEL), BF16)],
        compiler_params=_cparams("parallel"),
    )(x, g1, w_in_p, token)


def _in_proj_bwd(x, g1, w_in_p, d_u, d_lat, d_gs, d_gm, dh, t):
    l = x.shape[0]

    def body(x_ref, g_ref, w_ref, du_ref, dlat_ref, dgs_ref, dgm_ref, dh_ref, gx_ref, xn_ref, dproj_ref, dg_ref):
        xv = x_ref[...]
        g = g_ref[...]
        xn_ref[...] = _rms_fwd(xv, g, D_MODEL).astype(BF16)
        dproj_ref[:, 0:512] = du_ref[...]
        dproj_ref[:, 512:1280] = dlat_ref[...]
        dproj_ref[:, 1280:2304] = dgs_ref[...]
        dproj_ref[:, 2304:3328] = dgm_ref[...]
        dxn = _mm_nt(dproj_ref[...], w_ref[...])
        dx, dg_rows = _rms_bwd(xv, g, dxn, D_MODEL)
        gx_ref[...] = dh_ref[...] + dx
        _accumulate(dg_ref, _colsum(dg_rows), pl.program_id(0) == 0)

    return pl.pallas_call(
        body, name="in_proj_bwd", grid=(l // t,),
        in_specs=[_rows(t, D_MODEL), _resident((1, D_MODEL)), _resident((D_MODEL, D_IN_PAD)), _rows(t, 512),
                  _rows(t, LAT_W), _rows(t, D_MODEL), _rows(t, D_MODEL), _rows(t, D_MODEL)],
        out_specs=[_rows(t, D_MODEL), _rows(t, D_MODEL), _rows(t, D_IN_PAD), pl.BlockSpec((1, D_MODEL), lambda i: (0, 0))],
        out_shape=[jax.ShapeDtypeStruct((l, D_MODEL), F32), jax.ShapeDtypeStruct((l, D_MODEL), BF16),
                   jax.ShapeDtypeStruct((l, D_IN_PAD), BF16), jax.ShapeDtypeStruct((1, D_MODEL), F32)],
        compiler_params=_cparams("arbitrary"),
    )(x, g1, w_in_p, d_u, d_lat, d_gs, d_gm, dh)


def _ssm_param_fn(a_re, a_im, log_dt, b_re, b_im):
    dt = jnp.exp(log_dt)
    er = jnp.exp(a_re * dt)
    lr = er * jnp.cos(a_im * dt)
    li = er * jnp.sin(a_im * dt)
    den = a_re * a_re + a_im * a_im
    nr = lr - 1.0
    kr = (nr * a_re + li * a_im) / den
    ki = (li * a_re - nr * a_im) / den
    rows = lambda k: jnp.broadcast_to(k[:, None, :], (SSM_GROUPS, SSM_GROUP_CH, SSM_STATE)).reshape(SSM_WIDTH, SSM_STATE)
    krt, kit = rows(kr), rows(ki)
    return lr, li, krt * b_re - kit * b_im, krt * b_im + kit * b_re


def _state_selector():
    row = lax.broadcasted_iota(jnp.int32, (SSM_STATE, GP), 0)
    col = lax.broadcasted_iota(jnp.int32, (SSM_STATE, GP), 1)
    return jnp.where(jnp.bitwise_and(col, SSM_STATE - 1) == row, 1.0, 0.0).astype(BF16)


def _own_group(rows, rows_per_group_log2):
    row = lax.broadcasted_iota(jnp.int32, (rows, GP), 0)
    col = lax.broadcasted_iota(jnp.int32, (rows, GP), 1)
    return jnp.right_shift(row, rows_per_group_log2) == jnp.right_shift(col, 6)


def _three_bf16(x):
    hi = x.astype(BF16)
    rest = x - hi.astype(F32)
    mid = rest.astype(BF16)
    return hi, mid, (rest - mid.astype(F32)).astype(BF16)


def _spread(x, sel):
    return sum(jnp.dot(part, sel, preferred_element_type=F32) for part in _three_bf16(x))


def _collect(xw, sel):
    return sum(lax.dot_general(part, sel, (((1,), (1,)), ((), ())), preferred_element_type=F32) for part in _three_bf16(xw))


def _ssm_param_fwd(a_re, a_im, log_dt, b_re, b_im, c_re, c_im):
    def body(ar_ref, ai_ref, ldt_ref, br_ref, bi_ref, cr_ref, ci_ref, wb_ref, wct_ref, tf_ref, tr_ref):
        lr, li, bbr, bbi = _ssm_param_fn(ar_ref[...], ai_ref[...], ldt_ref[...], br_ref[...], bi_ref[...])
        sel = _state_selector()
        own16 = _own_group(SSM_WIDTH, 4)
        own1 = _own_group(SSM_GROUPS, 0)
        block = lambda m: jnp.where(own16, jnp.dot(m.astype(BF16), sel, preferred_element_type=F32), 0.0).astype(BF16)
        wb_ref[:, 0:GP] = block(bbr)
        wb_ref[:, GP:2 * GP] = block(bbi)
        wct_ref[:, 0:GP] = block(cr_ref[...])
        wct_ref[:, GP:2 * GP] = block(-ci_ref[...])
        flat = lambda m: _colsum(jnp.where(own1, _spread(m, sel), 0.0))
        pr, pi = [], []
        qr, qi = lr, li
        for _ in range(8):
            pr.append(flat(qr))
            pi.append(flat(qi))
            qr, qi = qr * lr - qi * li, qr * li + qi * lr
        row = lax.broadcasted_iota(jnp.int32, (8, GP), 0)
        for n, k in enumerate((1, 2, 4)):
            tf_ref[2 * n] = jnp.where(row >= k, pr[k - 1], 0.0)
            tf_ref[2 * n + 1] = jnp.where(row >= k, pi[k - 1], 0.0)
            tr_ref[2 * n] = jnp.where(row < 8 - k, pr[k - 1], 0.0)
            tr_ref[2 * n + 1] = jnp.where(row < 8 - k, -pi[k - 1], 0.0)
        pick = lambda vals: sum(jnp.where(row == j, v, 0.0) for j, v in enumerate(vals))
        tf_ref[6] = pick(pr)
        tf_ref[7] = pick(pi)
        tr_ref[6] = pick(pr[::-1])
        tr_ref[7] = pick([-v for v in pi[::-1]])

    return pl.pallas_call(
        body, name="ssm_param_fwd",
        out_shape=[jax.ShapeDtypeStruct((SSM_WIDTH, 2 * GP), BF16), jax.ShapeDtypeStruct((SSM_WIDTH, 2 * GP), BF16),
                   jax.ShapeDtypeStruct((8, 8, GP), F32), jax.ShapeDtypeStruct((8, 8, GP), F32)],
        compiler_params=_cparams(),
    )(a_re, a_im, log_dt, b_re, b_im, c_re, c_im)


STRIP_CH = 128
STRIP_ST = 512
N_STRIPS = SSM_WIDTH // STRIP_CH


def _ssm_param_bwd(a_re, a_im, log_dt, b_re, b_im, g_lr, g_li, g_wb, g_wct):
    def body(ar_ref, ai_ref, ldt_ref, br_ref, bi_ref, glr_ref, gli_ref, gwb_ref, gwc_ref,
             o_ar, o_ai, o_ldt, o_br, o_bi, o_cr, o_ci):
        sel = _state_selector()
        own1 = _own_group(SSM_GROUPS, 0)
        row = lax.broadcasted_iota(jnp.int32, (SSM_WIDTH, STRIP_ST), 0)
        col = lax.broadcasted_iota(jnp.int32, (SSM_WIDTH, STRIP_ST), 1)
        own = jnp.bitwise_and(jnp.right_shift(row, 4), 7) == jnp.right_shift(col, 6)
        blocks = lambda m: _collect(jnp.where(own, m, 0.0), sel[:, 0:STRIP_ST])
        unflat = lambda v: _collect(jnp.where(own1, v, 0.0), sel)
        _, vjp = jax.vjp(_ssm_param_fn, ar_ref[...], ai_ref[...], ldt_ref[...], br_ref[...], bi_ref[...])
        d_ar, d_ai, d_ldt, d_br, d_bi = vjp((unflat(glr_ref[...]), unflat(gli_ref[...]),
                                             blocks(gwb_ref[:, 0:STRIP_ST]), blocks(gwb_ref[:, STRIP_ST:2 * STRIP_ST])))
        o_ar[...] = d_ar
        o_ai[...] = d_ai
        o_ldt[...] = d_ldt
        o_br[...] = d_br
        o_bi[...] = d_bi
        o_cr[...] = blocks(gwc_ref[:, 0:STRIP_ST])
        o_ci[...] = -blocks(gwc_ref[:, STRIP_ST:2 * STRIP_ST])

    g, p = SSM_GROUPS, SSM_STATE
    gp = jax.ShapeDtypeStruct((g, p), F32)
    gcp = jax.ShapeDtypeStruct((SSM_WIDTH, p), F32)
    return pl.pallas_call(
        body, name="ssm_param_bwd", out_shape=[gp, gp, jax.ShapeDtypeStruct((g, 1), F32), gcp, gcp, gcp, gcp],
        compiler_params=_cparams(),
    )(a_re, a_im, log_dt, b_re, b_im, g_lr, g_li, g_wb, g_wct)


def _strip(ref, j, im):
    return ref[STRIP_CH * j:STRIP_CH * (j + 1), im * GP + STRIP_ST * j:im * GP + STRIP_ST * (j + 1)]


def _wgrad_strips(a, b_re, b_im, name, im_block, token):
    l = a.shape[0]
    bl = min(l, 512)

    def body(a_ref, bre_ref, bim_ref, token_ref, o_ref):
        first = pl.program_id(0) == 0
        for j in range(N_STRIPS):
            aj = a_ref[:, STRIP_CH * j:STRIP_CH * (j + 1)]
            states = slice(STRIP_ST * j, STRIP_ST * (j + 1))
            _accumulate(o_ref.at[STRIP_CH * j:STRIP_CH * (j + 1), 0:STRIP_ST], _mm_tn(aj, bre_ref[:, states]), first)
            _accumulate(o_ref.at[STRIP_CH * j:STRIP_CH * (j + 1), STRIP_ST:2 * STRIP_ST], _mm_tn(aj, bim_ref[:, states]), first)

    return pl.pallas_call(
        body, name=name, grid=(l // bl,),
        in_specs=[pl.BlockSpec((bl, SSM_WIDTH), lambda k: (k, 0)), pl.BlockSpec((bl, GP), lambda k: (k, 0)),
                  pl.BlockSpec((bl, GP), lambda k: (k, im_block)), ANY],
        out_specs=pl.BlockSpec((SSM_WIDTH, 2 * STRIP_ST), lambda k: (0, 0)),
        out_shape=jax.ShapeDtypeStruct((SSM_WIDTH, 2 * STRIP_ST), F32),
        compiler_params=_cparams("arbitrary"),
    )(a, b_re, b_im, token)


SCAN_STRIP = 512


def _scan_chunk(inr_ref, ini_ref, outr_ref, outi_ref, cr_ref, ci_ref, tab_ref, tc, reverse):
    n_blocks = tc // 8

    def block(j, _):
        i = (n_blocks - 1 - j) if reverse else j
        rows = pl.ds(pl.multiple_of(i * 8, 8), 8)
        for s in range(GP // SCAN_STRIP):
            sl = pl.ds(s * SCAN_STRIP, SCAN_STRIP)
            xr = inr_ref[rows, sl]
            xi = ini_ref[rows, sl]
            for n, k in enumerate((1, 2, 4)):
                shift = (8 - k) if reverse else k
                sr = pltpu.roll(xr, shift, 0)
                si = pltpu.roll(xi, shift, 0)
                mr = tab_ref[2 * n, :, sl]
                mi = tab_ref[2 * n + 1, :, sl]
                xr, xi = xr + mr * sr - mi * si, xi + mr * si + mi * sr
            qr = tab_ref[6, :, sl]
            qi = tab_ref[7, :, sl]
            cr = cr_ref[:, sl]
            ci = ci_ref[:, sl]
            xr, xi = xr + qr * cr - qi * ci, xi + qr * ci + qi * cr
            outr_ref[rows, sl] = xr
            outi_ref[rows, sl] = xi
            edge = 0 if reverse else 7
            cr_ref[:, sl] = jnp.broadcast_to(xr[edge:edge + 1, :], (8, SCAN_STRIP))
            ci_ref[:, sl] = jnp.broadcast_to(xi[edge:edge + 1, :], (8, SCAN_STRIP))
        return 0

    lax.fori_loop(0, n_blocks, block, 0)


def _glu_pre(z, wg_ref):
    return sum(_mm(z[:, 128 * j:128 * (j + 1)], wg_ref[j]) for j in range(4))


def _ssm_fwd(u, wb, wc, tabs, dskip, grp_d, b_glu, grp_e, tc):
    l = u.shape[0]

    def body(u_ref, wb_ref, wc_ref, tab_ref, d_ref, wg_ref, bg_ref, wo_ref, xr_ref, xi_ref, y_ref, ys_ref,
             bur, bui, cr, ci):
        @pl.when(pl.program_id(0) == 0)
        def _():
            cr[...] = jnp.zeros_like(cr)
            ci[...] = jnp.zeros_like(ci)

        uv = u_ref[...]
        ub = uv.astype(BF16)
        for j in range(N_STRIPS):
            uj = ub[:, STRIP_CH * j:STRIP_CH * (j + 1)]
            states = slice(STRIP_ST * j, STRIP_ST * (j + 1))
            bur[:, states] = _mm(uj, _strip(wb_ref, j, 0))
            bui[:, states] = _mm(uj, _strip(wb_ref, j, 1))
        _scan_chunk(bur, bui, bur, bui, cr, ci, tab_ref, tc, False)
        xr_ref[...] = bur[...].astype(BF16)
        xi_ref[...] = bui[...].astype(BF16)
        y = jnp.concatenate(
            [_mm_nt(xr_ref[:, STRIP_ST * j:STRIP_ST * (j + 1)], _strip(wc_ref, j, 0))
             + _mm_nt(xi_ref[:, STRIP_ST * j:STRIP_ST * (j + 1)], _strip(wc_ref, j, 1)) for j in range(N_STRIPS)],
            axis=-1) + d_ref[...] * uv
        y_ref[...] = y
        z = _gelu(y)
        z2 = z * _sigmoid(_glu_pre(z, wg_ref) + bg_ref[...])
        for s in range(4):
            ys_ref[:, 256 * s:256 * (s + 1)] = _mm(z2, wo_ref[s]).astype(BF16)

    return pl.pallas_call(
        body, name="ssm_fwd", grid=(l // tc,),
        in_specs=[_rows(tc, 512), _resident((512, 2 * GP)), _resident((512, 2 * GP)), _resident((8, 8, GP)),
                  _resident((1, 512)), _member_block("w_glu"), _resident((1, 512)), _member_block("w_o_ssm")],
        out_specs=[_rows(tc, GP), _rows(tc, GP), _rows(tc, 512), _rows(tc, D_MODEL)],
        out_shape=[jax.ShapeDtypeStruct((l, GP), BF16), jax.ShapeDtypeStruct((l, GP), BF16),
                   jax.ShapeDtypeStruct((l, 512), F32), jax.ShapeDtypeStruct((l, D_MODEL), BF16)],
        scratch_shapes=[pltpu.VMEM((tc, GP), F32), pltpu.VMEM((tc, GP), F32), pltpu.VMEM((8, GP), F32),
                        pltpu.VMEM((8, GP), F32)],
        compiler_params=_cparams("arbitrary"),
    )(u, wb, wc, tabs, dskip, grp_d, b_glu, grp_e)


def _ssm_bwd(dys, y, u, xr, xi, wb, wc, tabs_rev, dskip, grp_d, b_glu, grp_e, tc):
    l = u.shape[0]
    nc = l // tc

    def body(dys_ref, y_ref, u_ref, xr_ref, xi_ref, wb_ref, wc_ref, tab_ref, d_ref, wg_ref, bg_ref, wo_ref,
             du_ref, a_ref, dy_ref, z_ref, z2_ref, dpre_ref, gb_ref, gd_ref, glr_ref, gli_ref,
             dxr, dxi, ar, ai, cr, ci):
        first = pl.program_id(0) == 0

        @pl.when(first)
        def _():
            cr[...] = jnp.zeros_like(cr)
            ci[...] = jnp.zeros_like(ci)

        yv = y_ref[...]
        uv = u_ref[...]
        dz2 = sum(_mm_nt(dys_ref[:, 256 * j:256 * (j + 1)], wo_ref[j]) for j in range(4))
        z = _gelu(yv)
        s = _sigmoid(_glu_pre(z, wg_ref) + bg_ref[...])
        dpre = dz2 * z * s * (1.0 - s)
        dpreb = dpre.astype(BF16)
        dz = dz2 * s + jnp.concatenate([_mm_nt(dpreb, wg_ref[j]) for j in range(4)], axis=-1)
        dy = dz * _gelu_grad(yv)
        z_ref[...] = z.astype(BF16)
        z2_ref[...] = (z * s).astype(BF16)
        dpre_ref[...] = dpre.astype(BF16)
        dy_ref[...] = dy.astype(BF16)
        _accumulate(gb_ref, _colsum(dpre), first)
        _accumulate(gd_ref, _colsum(dy * uv), first)

        dyb = dy.astype(BF16)
        for j in range(N_STRIPS):
            dyj = dyb[:, STRIP_CH * j:STRIP_CH * (j + 1)]
            dxr[:, STRIP_ST * j:STRIP_ST * (j + 1)] = _mm(dyj, _strip(wc_ref, j, 0))
            dxi[:, STRIP_ST * j:STRIP_ST * (j + 1)] = _mm(dyj, _strip(wc_ref, j, 1))
        ar[pl.ds(tc, 8), :] = cr[...]
        ai[pl.ds(tc, 8), :] = ci[...]
        _scan_chunk(dxr, dxi, ar, ai, cr, ci, tab_ref, tc, True)
        a_ref[:, 0:GP] = ar[pl.ds(0, tc), :].astype(BF16)
        a_ref[:, GP:2 * GP] = ai[pl.ds(0, tc), :].astype(BF16)
        du_states = jnp.concatenate(
            [_mm_nt(a_ref[:, STRIP_ST * j:STRIP_ST * (j + 1)], _strip(wb_ref, j, 0))
             + _mm_nt(a_ref[:, GP + STRIP_ST * j:GP + STRIP_ST * (j + 1)], _strip(wb_ref, j, 1)) for j in range(N_STRIPS)],
            axis=-1)
        du_ref[...] = (dy * d_ref[...] + du_states).astype(BF16)
        anr = ar[pl.ds(1, tc), :]
        ani = ai[pl.ds(1, tc), :]
        xrv = xr_ref[...].astype(F32)
        xiv = xi_ref[...].astype(F32)
        _accumulate(glr_ref, _colsum(anr * xrv + ani * xiv), first)
        _accumulate(gli_ref, _colsum(ani * xrv - anr * xiv), first)

    rev = lambda w: pl.BlockSpec((tc, w), lambda i: (nc - 1 - i, 0))
    acc = lambda w: pl.BlockSpec((1, w), lambda i: (0, 0))
    bf = jax.ShapeDtypeStruct((l, 512), BF16)
    return pl.pallas_call(
        body, name="ssm_bwd", grid=(nc,),
        in_specs=[rev(D_MODEL), rev(512), rev(512), rev(GP), rev(GP), _resident((512, 2 * GP)), _resident((512, 2 * GP)),
                  _resident((8, 8, GP)), _resident((1, 512)), _member_block("w_glu"), _resident((1, 512)),
                  _member_block("w_o_ssm")],
        out_specs=[rev(512), rev(2 * GP), rev(512), rev(512), rev(512), rev(512), acc(512), acc(512), acc(GP), acc(GP)],
        out_shape=[bf, jax.ShapeDtypeStruct((l, 2 * GP), BF16), bf, bf, bf, bf,
                   jax.ShapeDtypeStruct((1, 512), F32), jax.ShapeDtypeStruct((1, 512), F32),
                   jax.ShapeDtypeStruct((1, GP), F32), jax.ShapeDtypeStruct((1, GP), F32)],
        scratch_shapes=[pltpu.VMEM((tc, GP), F32), pltpu.VMEM((tc, GP), F32), pltpu.VMEM((tc + 8, GP), F32),
                        pltpu.VMEM((tc + 8, GP), F32), pltpu.VMEM((8, GP), F32), pltpu.VMEM((8, GP), F32)],
        compiler_params=_cparams("arbitrary"),
    )(dys, y, u, xr, xi, wb, wc, tabs_rev, dskip, grp_d, b_glu, grp_e)


def _swap_halves(b):
    lane = lax.broadcasted_iota(jnp.int32, b.shape, 1)
    return jnp.where(lane < 32, pltpu.roll(b, 96, 1), pltpu.roll(b, 32, 1))


def _rope_tables(pos_ref, invf_ref, sgn_ref):
    ang = pos_ref[...].astype(F32) * invf_ref[...]
    return jnp.cos(ang), jnp.sin(ang) * sgn_ref[...]


def _mla_pre_fwd(lat, pos, invf, sgn, gqa, gkva, gq, gk, w_qb_p, w_kvb, t):
    l = lat.shape[0]

    def body(lat_ref, pos_ref, invf_ref, sgn_ref, gqa_ref, gkva_ref, gq_ref, gk_ref, wq_ref, wkv_ref, q_ref, k_ref, v_ref):
        cs, sn = _rope_tables(pos_ref, invf_ref, sgn_ref)
        ql = _rms_fwd(lat_ref[:, 0:Q_LORA], gqa_ref[...], Q_LORA)
        ckn = _rms_fwd(lat_ref[:, Q_LORA:Q_LORA + KV_LORA], gkva_ref[...], KV_LORA)
        kpe = lat_ref[:, 640:768]
        q0 = _mm(ql, wq_ref[...])
        cknb = ckn.astype(BF16)
        kv = jnp.concatenate([_mm(cknb, wkv_ref[s]) for s in range(4)], axis=-1)
        for h in range(N_HEADS):
            q1 = _rms_fwd(q0[:, HEAD_PAD * h:HEAD_PAD * (h + 1)], gq_ref[...], QK_HEAD)
            b = q1[:, 128:256]
            q_ref[h, :, 0:128] = (q1[:, 0:128] * ATT_SCALE).astype(BF16)
            q_ref[h, :, 128:256] = ((b * cs + _swap_halves(b) * sn) * ATT_SCALE).astype(BF16)
            k0 = jnp.concatenate([kv[:, 256 * h:256 * h + 128], kpe], axis=-1)
            k1 = _rms_fwd(k0, gk_ref[...], QK_HEAD)
            b = k1[:, 128:256]
            k_ref[h, :, 0:128] = k1[:, 0:128].astype(BF16)
            k_ref[h, :, 128:256] = (b * cs + _swap_halves(b) * sn).astype(BF16)
            v_ref[h] = kv[:, 256 * h + 128:256 * h + 256].astype(BF16)

    heads = lambda w: pl.BlockSpec((N_HEADS, t, w), lambda i: (0, i, 0))
    return pl.pallas_call(
        body, name="mla_pre_fwd", grid=(l // t,),
        in_specs=[_rows(t, LAT_W), _rows(t, 1), _resident((1, 128)), _resident((1, 128)), _resident((1, Q_LORA)),
                  _resident((1, KV_LORA)), _resident((1, HEAD_PAD)), _resident((1, HEAD_PAD)),
                  _resident((Q_LORA, N_HEADS * HEAD_PAD)), _member_block("w_kv_b")],
        out_specs=[heads(HEAD_PAD), heads(HEAD_PAD), heads(V_HEAD)],
        out_shape=[jax.ShapeDtypeStruct((N_HEADS, l, HEAD_PAD), BF16), jax.ShapeDtypeStruct((N_HEADS, l, HEAD_PAD), BF16),
                   jax.ShapeDtypeStruct((N_HEADS, l, V_HEAD), BF16)],
        compiler_params=_cparams("parallel"),
    )(lat, pos, invf, sgn, gqa, gkva, gq, gk, w_qb_p, w_kvb)


def _mla_pre_bwd(lat, pos, invf, sgn, gqa, gkva, gq, gk, w_qb_p, w_kvb, dq, dk, dv, t, token):
    l = lat.shape[0]

    def body(lat_ref, pos_ref, invf_ref, sgn_ref, gqa_ref, gkva_ref, gq_ref, gk_ref, wq_ref, wkv_ref, dq_ref, dk_ref, dv_ref,
             token_ref, dlat_ref, ql_ref, dq0_ref, ckn_ref, dkv_ref, ggqa_ref, ggkva_ref, ggq_ref, ggk_ref):
        first = pl.program_id(0) == 0
        cs, sn = _rope_tables(pos_ref, invf_ref, sgn_ref)
        q_lat = lat_ref[:, 0:Q_LORA]
        c_kv = lat_ref[:, Q_LORA:Q_LORA + KV_LORA]
        kpe = lat_ref[:, 640:768]
        ql = _rms_fwd(q_lat, gqa_ref[...], Q_LORA)
        ckn = _rms_fwd(c_kv, gkva_ref[...], KV_LORA)
        ql_ref[...] = ql.astype(BF16)
        ckn_ref[...] = ckn.astype(BF16)
        q0 = _mm(ql, wq_ref[...])
        cknb = ckn.astype(BF16)
        kv = jnp.concatenate([_mm(cknb, wkv_ref[s]) for s in range(4)], axis=-1)
        dkpe = jnp.zeros_like(kpe)
        ggq = jnp.zeros((1, HEAD_PAD), F32)
        ggk = jnp.zeros((1, HEAD_PAD), F32)

        def unrope(d):
            b = d[:, 128:256]
            return jnp.concatenate([d[:, 0:128], b * cs + _swap_halves(b * sn)], axis=-1)

        for h in range(N_HEADS):
            dq1 = unrope(dq_ref[h] * ATT_SCALE)
            dq0h, gq_rows = _rms_bwd(q0[:, HEAD_PAD * h:HEAD_PAD * (h + 1)], gq_ref[...], dq1, QK_HEAD)
            ggq = ggq + _colsum(gq_rows)
            dq0_ref[:, HEAD_PAD * h:HEAD_PAD * (h + 1)] = dq0h.astype(BF16)
            k0 = jnp.concatenate([kv[:, 256 * h:256 * h + 128], kpe], axis=-1)
            dk0, gk_rows = _rms_bwd(k0, gk_ref[...], unrope(dk_ref[h]), QK_HEAD)
            ggk = ggk + _colsum(gk_rows)
            dkpe = dkpe + dk0[:, 128:256]
            dkv_ref[:, 256 * h:256 * h + 128] = dk0[:, 0:128].astype(BF16)
            dkv_ref[:, 256 * h + 128:256 * h + 256] = dv_ref[h].astype(BF16)
        dql = _mm_nt(dq0_ref[...], wq_ref[...])
        dckn = sum(_mm_nt(dkv_ref[:, 512 * s:512 * (s + 1)], wkv_ref[s]) for s in range(4))
        dq_lat, gqa_rows = _rms_bwd(q_lat, gqa_ref[...], dql, Q_LORA)
        dc_kv, gkva_rows = _rms_bwd(c_kv, gkva_ref[...], dckn, KV_LORA)
        dlat_ref[:, 0:Q_LORA] = dq_lat.astype(BF16)
        dlat_ref[:, Q_LORA:Q_LORA + KV_LORA] = dc_kv.astype(BF16)
        dlat_ref[:, 640:768] = dkpe.astype(BF16)
        _accumulate(ggqa_ref, _colsum(gqa_rows), first)
        _accumulate(ggkva_ref, _colsum(gkva_rows), first)
        _accumulate(ggq_ref, ggq, first)
        _accumulate(ggk_ref, ggk, first)

    heads = lambda w: pl.BlockSpec((N_HEADS, t, w), lambda i: (0, i, 0))
    acc = lambda w: pl.BlockSpec((1, w), lambda i: (0, 0))
    return pl.pallas_call(
        body, name="mla_pre_bwd", grid=(l // t,),
        in_specs=[_rows(t, LAT_W), _rows(t, 1), _resident((1, 128)), _resident((1, 128)), _resident((1, Q_LORA)),
                  _resident((1, KV_LORA)), _resident((1, HEAD_PAD)), _resident((1, HEAD_PAD)),
                  _resident((Q_LORA, N_HEADS * HEAD_PAD)), _member_block("w_kv_b"),
                  heads(HEAD_PAD), heads(HEAD_PAD), heads(V_HEAD), ANY],
        out_specs=[_rows(t, LAT_W), _rows(t, Q_LORA), _rows(t, N_HEADS * HEAD_PAD), _rows(t, KV_LORA), _rows(t, N_HEADS * 256),
                   acc(Q_LORA), acc(KV_LORA), acc(HEAD_PAD), acc(HEAD_PAD)],
        out_shape=[jax.ShapeDtypeStruct((l, LAT_W), BF16), jax.ShapeDtypeStruct((l, Q_LORA), BF16),
                   jax.ShapeDtypeStruct((l, N_HEADS * HEAD_PAD), BF16), jax.ShapeDtypeStruct((l, KV_LORA), BF16),
                   jax.ShapeDtypeStruct((l, N_HEADS * 256), BF16), jax.ShapeDtypeStruct((1, Q_LORA), F32),
                   jax.ShapeDtypeStruct((1, KV_LORA), F32), jax.ShapeDtypeStruct((1, HEAD_PAD), F32),
                   jax.ShapeDtypeStruct((1, HEAD_PAD), F32)],
        compiler_params=_cparams("arbitrary"),
    )(lat, pos, invf, sgn, gqa, gkva, gq, gk, w_qb_p, w_kvb, dq, dk, dv, token)


def _causal(s, transposed):
    row = lax.broadcasted_iota(jnp.int32, s.shape, 0)
    col = lax.broadcasted_iota(jnp.int32, s.shape, 1)
    keep = (row <= col) if transposed else (col <= row)
    return jnp.where(keep, s, -jnp.inf)


def _as_row(col):
    n = col.shape[0]
    row = lax.broadcasted_iota(jnp.int32, (n, n), 0)
    lane = lax.broadcasted_iota(jnp.int32, (n, n), 1)
    return jnp.sum(jnp.where(row == lane, col, 0.0), axis=0, keepdims=True)


def _attn_fwd(q, k, v, tq):
    l = q.shape[1]

    hb = 2

    def body(q_ref, k_ref, v_ref, o_ref, lse_ref):
        qi = pl.program_id(1)
        qs = [q_ref[a] for a in range(hb)]

        def step(kb, carry, masked):
            rows = pl.ds(pl.multiple_of(kb * tq, tq), tq)
            out = []
            for a, (m, den, acc) in enumerate(carry):
                s = _mm_nt(qs[a], k_ref[a, rows, :])
                if masked:
                    s = _causal(s, False)
                m_new = jnp.maximum(m, jnp.max(s, axis=-1, keepdims=True))
                alpha = jnp.exp(m - m_new)
                p = jnp.exp(s - m_new)
                den = alpha * den + jnp.sum(p, axis=-1, keepdims=True)
                acc = alpha * acc + _mm(p, v_ref[a, rows, :])
                out.append((m_new, den, acc))
            return tuple(out)

        init = tuple((jnp.full((tq, 1), -jnp.inf, F32), jnp.zeros((tq, 1), F32), jnp.zeros((tq, V_HEAD), F32))
                     for _ in range(hb))
        carry = lax.fori_loop(0, qi, lambda kb, c: step(kb, c, False), init)
        for a, (m, den, acc) in enumerate(step(qi, carry, True)):
            o_ref[:, V_HEAD * a:V_HEAD * (a + 1)] = acc / den
            lse_ref[a, 0] = _as_row(m + jnp.log(den))

    return pl.pallas_call(
        body, name="attn_fwd", grid=(N_HEADS // hb, l // tq),
        in_specs=[pl.BlockSpec((hb, tq, HEAD_PAD), lambda h, i: (h, i, 0)), pl.BlockSpec((hb, l, HEAD_PAD), lambda h, i: (h, 0, 0)),
                  pl.BlockSpec((hb, l, V_HEAD), lambda h, i: (h, 0, 0))],
        out_specs=[pl.BlockSpec((tq, hb * V_HEAD), lambda h, i: (i, h)), pl.BlockSpec((hb, 1, 1, tq), lambda h, i: (h, i, 0, 0))],
        out_shape=[jax.ShapeDtypeStruct((l, N_HEADS * V_HEAD), F32), jax.ShapeDtypeStruct((N_HEADS, l // tq, 1, tq), F32)],
        compiler_params=_cparams("parallel", "arbitrary"),
    )(q, k, v)


def _attn_bwd(q, k, v, o, do, lse_t, tq, token):
    l = q.shape[1]
    nq = l // tq

    hb = 1

    def body(q_ref, k_ref, v_ref, o_ref, do_ref, lse_ref, token_ref, dq_ref, dk_ref, dv_ref):
        ki = pl.program_id(1)

        @pl.when(ki == 0)
        def _():
            dq_ref[...] = jnp.zeros_like(dq_ref)

        kblks = [k_ref[a] for a in range(hb)]
        vblks = [v_ref[a] for a in range(hb)]
        ones = jnp.ones((8, V_HEAD), BF16)

        def step(qb, carry, masked):
            rows = pl.ds(pl.multiple_of(qb * tq, tq), tq)
            out = []
            for a, (dk, dv) in enumerate(carry):
                cols = slice(V_HEAD * a, V_HEAD * (a + 1))
                qblk = q_ref[a, rows, :]
                dov = do_ref[rows, cols]
                dob = dov.astype(BF16)
                delta = sum(_mm_nt(ones, part) for part in _three_bf16(dov * o_ref[rows, cols]))[0:1, :]
                st = _mm_nt(kblks[a], qblk)
                if masked:
                    st = _causal(st, True)
                pt = jnp.exp(st - lse_ref[a, qb])
                dv = dv + _mm(pt, dob)
                dst = (pt * (_mm_nt(vblks[a], dob) - delta)).astype(BF16)
                dk = dk + _mm(dst, qblk)
                dq_ref[a, rows, :] += _mm_tn(dst, kblks[a])
                out.append((dk, dv))
            return tuple(out)

        init = tuple((jnp.zeros((tq, HEAD_PAD), F32), jnp.zeros((tq, V_HEAD), F32)) for _ in range(hb))
        carry = lax.fori_loop(ki + 1, nq, lambda qb, c: step(qb, c, False), step(ki, init, True))
        for a, (dk, dv) in enumerate(carry):
            dk_ref[a] = dk
            dv_ref[a] = dv

    return pl.pallas_call(
        body, name="attn_bwd", grid=(N_HEADS // hb, nq),
        in_specs=[pl.BlockSpec((hb, l, HEAD_PAD), lambda h, i: (h, 0, 0)), pl.BlockSpec((hb, tq, HEAD_PAD), lambda h, i: (h, i, 0)),
                  pl.BlockSpec((hb, tq, V_HEAD), lambda h, i: (h, i, 0)), pl.BlockSpec((l, hb * V_HEAD), lambda h, i: (0, h)),
                  pl.BlockSpec((l, hb * V_HEAD), lambda h, i: (0, h)), pl.BlockSpec((hb, nq, 1, tq), lambda h, i: (h, 0, 0, 0)), ANY],
        out_specs=[pl.BlockSpec((hb, l, HEAD_PAD), lambda h, i: (h, 0, 0)), pl.BlockSpec((hb, tq, HEAD_PAD), lambda h, i: (h, i, 0)),
                   pl.BlockSpec((hb, tq, V_HEAD), lambda h, i: (h, i, 0))],
        out_shape=[jax.ShapeDtypeStruct((N_HEADS, l, HEAD_PAD), F32), jax.ShapeDtypeStruct((N_HEADS, l, HEAD_PAD), F32),
                   jax.ShapeDtypeStruct((N_HEADS, l, V_HEAD), F32)],
        compiler_params=_cparams("parallel", "arbitrary"),
    )(q, k, v, o, do, lse_t, token)


def _row_shards_mm(a, w_ref):
    a = a.astype(BF16)
    return sum(_mm(a[:, 256 * j:256 * (j + 1)], w_ref[j]) for j in range(4))


def _row_shards_mm_nt(a, w_ref):
    a = a.astype(BF16)
    return jnp.concatenate([_mm_nt(a, w_ref[j]) for j in range(4)], axis=-1)


def _merge_fwd(attn, y_ssm, gs, gm, x, grp_a, t):
    l = x.shape[0]

    def body(attn_ref, ys_ref, gs_ref, gm_ref, x_ref, wo_ref, wout_ref, ym_ref, mixed_ref, h_ref):
        y_mla = _row_shards_mm(attn_ref[...], wo_ref)
        ym_ref[...] = y_mla.astype(BF16)
        mixed = (_sigmoid(gs_ref[...].astype(F32)) * ys_ref[...].astype(F32)
                 + _sigmoid(gm_ref[...].astype(F32)) * y_mla).astype(BF16)
        mixed_ref[...] = mixed
        h_ref[...] = x_ref[...] + _row_shards_mm(mixed, wout_ref)

    r = lambda: _rows(t, D_MODEL)
    return pl.pallas_call(
        body, name="merge_fwd", grid=(l // t,),
        in_specs=[r(), r(), r(), r(), r(), _member_block("w_o_mla"), _member_block("w_out")],
        out_specs=[r(), r(), r()],
        out_shape=[jax.ShapeDtypeStruct((l, D_MODEL), BF16), jax.ShapeDtypeStruct((l, D_MODEL), BF16),
                   jax.ShapeDtypeStruct((l, D_MODEL), F32)],
        compiler_params=_cparams("parallel"),
    )(attn, y_ssm, gs, gm, x, grp_a, grp_a)


def _merge_bwd(dh, y_ssm, y_mla, gs, gm, grp_a, t):
    l = dh.shape[0]

    def body(dh_ref, ys_ref, ym_ref, gs_ref, gm_ref, wo_ref, wout_ref, dys_ref, dym_ref, dgs_ref, dgm_ref, dattn_ref):
        dmixed = _row_shards_mm_nt(dh_ref[...], wout_ref)
        sg = _sigmoid(gs_ref[...].astype(F32))
        sm = _sigmoid(gm_ref[...].astype(F32))
        dys_ref[...] = (dmixed * sg).astype(BF16)
        dgs_ref[...] = (dmixed * ys_ref[...].astype(F32) * sg * (1.0 - sg)).astype(BF16)
        dym = (dmixed * sm).astype(BF16)
        dym_ref[...] = dym
        dgm_ref[...] = (dmixed * ym_ref[...].astype(F32) * sm * (1.0 - sm)).astype(BF16)
        dattn_ref[...] = _row_shards_mm_nt(dym, wo_ref)

    r = lambda: _rows(t, D_MODEL)
    bf = jax.ShapeDtypeStruct((l, D_MODEL), BF16)
    return pl.pallas_call(
        body, name="merge_bwd", grid=(l // t,),
        in_specs=[r(), r(), r(), r(), r(), _member_block("w_o_mla"), _member_block("w_out")],
        out_specs=[r(), r(), r(), r(), r()],
        out_shape=[bf, bf, bf, bf, jax.ShapeDtypeStruct((l, D_MODEL), F32)],
        compiler_params=_cparams("parallel"),
    )(dh, y_ssm, y_mla, gs, gm, grp_a, grp_a)


def _mlp_fwd_bwd(h, tgt, g2, grp_a, t):
    l = h.shape[0]

    def body(h_ref, tgt_ref, g_ref, wu_ref, wd_ref, dh_ref, hn_ref, da_ref, hid_ref, dout_ref, loss_ref, dg_ref):
        first = pl.program_id(0) == 0
        hv = h_ref[...]
        g = g_ref[...]
        hn = _rms_fwd(hv, g, D_MODEL).astype(BF16)
        hn_ref[...] = hn
        out = hv
        relus = []
        for s in range(4):
            cols = slice(1024 * s, 1024 * (s + 1))
            relu = jnp.maximum(_mm(hn, wu_ref[s]), 0.0)
            relus.append(relu)
            hid = (relu * relu).astype(BF16)
            hid_ref[:, cols] = hid
            out = out + _mm(hid, wd_ref[s])
        err = out - tgt_ref[...]
        _accumulate(loss_ref, jnp.full((8, 128), jnp.sum(err * err) * (0.5 / D_MODEL), F32), first)
        dout = err * (1.0 / D_MODEL)
        doutb = dout.astype(BF16)
        dout_ref[...] = doutb
        dhn = jnp.zeros_like(hv)
        for s in range(4):
            da = (_mm_nt(doutb, wd_ref[s]) * (2.0 * relus[s])).astype(BF16)
            da_ref[:, 1024 * s:1024 * (s + 1)] = da
            dhn = dhn + _mm_nt(da, wu_ref[s])
        dx, dg_rows = _rms_bwd(hv, g, dhn, D_MODEL)
        dh_ref[...] = dout + dx
        _accumulate(dg_ref, _colsum(dg_rows), first)

    r = lambda w: _rows(t, w)
    return pl.pallas_call(
        body, name="mlp_fwd_bwd", grid=(l // t,),
        in_specs=[r(D_MODEL), r(D_MODEL), _resident((1, D_MODEL)), _member_block("w_up"), _member_block("w_down")],
        out_specs=[r(D_MODEL), r(D_MODEL), r(D_FF), r(D_FF), r(D_MODEL), pl.BlockSpec((8, 128), lambda i: (0, 0)),
                   pl.BlockSpec((1, D_MODEL), lambda i: (0, 0))],
        out_shape=[jax.ShapeDtypeStruct((l, D_MODEL), F32), jax.ShapeDtypeStruct((l, D_MODEL), BF16),
                   jax.ShapeDtypeStruct((l, D_FF), BF16), jax.ShapeDtypeStruct((l, D_FF), BF16),
                   jax.ShapeDtypeStruct((l, D_MODEL), BF16), jax.ShapeDtypeStruct((8, 128), F32),
                   jax.ShapeDtypeStruct((1, D_MODEL), F32)],
        compiler_params=_cparams("arbitrary"),
    )(h, tgt, g2, grp_a, grp_a)


def _wgrad(a, b, name):
    l, m = a.shape
    n = b.shape[1]
    bm = m if m <= 512 else 512
    bl = min(l, 2048 if n <= 1024 else 1024)

    def body(a_ref, b_ref, o_ref):
        _accumulate(o_ref, _mm_tn(a_ref[...], b_ref[...]), pl.program_id(1) == 0)

    return pl.pallas_call(
        body, name=name, grid=(m // bm, l // bl),
        in_specs=[pl.BlockSpec((bl, bm), lambda i, j: (j, i)), pl.BlockSpec((bl, n), lambda i, j: (j, 0))],
        out_specs=pl.BlockSpec((bm, n), lambda i, j: (i, 0)),
        out_shape=jax.ShapeDtypeStruct((m, n), F32),
        compiler_params=_cparams("parallel", "arbitrary"),
    )(a, b)


def _wgrad_into(a, b, member, cut, dest=None):
    group, off, rs, cs = _place_in_group(member)
    l = a.shape[0]
    bm = min(rs, 512)
    bl = min(l, 2048)
    nb = rs // bm
    if cut == "row":
        a_spec = pl.BlockSpec((bl, bm), lambda j, i, k: (k, j * nb + i))
        b_spec = pl.BlockSpec((bl, cs), lambda j, i, k: (k, 0))
    else:
        a_spec = pl.BlockSpec((bl, bm), lambda j, i, k: (k, i))
        b_spec = pl.BlockSpec((bl, cs), lambda j, i, k: (k, j))

    def body(a_ref, b_ref, *rest):
        o_ref = rest[-1]
        part = _mm_tn(a_ref[...], b_ref[...])

        @pl.when(pl.program_id(2) == 0)
        def _():
            o_ref[0] = part

        @pl.when(pl.program_id(2) != 0)
        def _():
            o_ref[0] += part

    operands, in_specs, aliases = [a, b], [a_spec, b_spec], {}
    if dest is not None:
        operands.append(dest)
        in_specs.append(ANY)
        aliases = {2: 0}
    return pl.pallas_call(
        body, name="wgrad_" + member, grid=(4, nb, l // bl), in_specs=in_specs,
        out_specs=pl.BlockSpec((1, bm, cs), lambda j, i, k: (j, off // bm + i, 0)),
        out_shape=jax.ShapeDtypeStruct((4, _group_rows(group), cs), F32), input_output_aliases=aliases,
        compiler_params=_cparams("parallel", "parallel", "arbitrary"),
    )(*operands)


def _adamw(w, g, m, v, name, g_off, token):
    r, c = w.shape
    br = r
    for cand in (256, 128, 64, 32, 16, 8):
        if r % cand == 0 and g_off % cand == 0:
            br = cand
            break

    def body(w_ref, g_ref, m_ref, v_ref, token_ref, go_ref, d_ref, nm_ref, nv_ref):
        gv = g_ref[...]
        go_ref[...] = gv
        nm = ADAM_B1 * m_ref[...] + (1.0 - ADAM_B1) * gv
        nv = ADAM_B2 * v_ref[...] + (1.0 - ADAM_B2) * (gv * gv)
        m_hat = nm / (1.0 - ADAM_B1 ** ADAM_STEP)
        v_hat = nv / (1.0 - ADAM_B2 ** ADAM_STEP)
        d_ref[...] = -ADAM_LR * (m_hat / (jnp.sqrt(v_hat) + ADAM_EPS) + ADAM_WD * w_ref[...])
        nm_ref[...] = nm
        nv_ref[...] = nv

    spec = lambda: pl.BlockSpec((br, c), lambda i: (i, 0))
    g_spec = pl.BlockSpec((br, c), lambda i: (g_off // br + i, 0))
    shp = jax.ShapeDtypeStruct((r, c), F32)
    return pl.pallas_call(
        body, name=name, grid=(r // br,), in_specs=[spec(), g_spec, spec(), spec(), ANY],
        out_specs=[spec(), spec(), spec(), spec()], out_shape=[shp, shp, shp, shp], compiler_params=_cparams("parallel"),
    )(w, g, m, v, token)


def _place():
    return lax.axis_index("x"), lax.axis_index("y"), lax.axis_index("c")


def _other_chips(x, y):
    return [(1 - x, y), (x, 1 - y), (1 - x, 1 - y)]


ANY = pl.BlockSpec(memory_space=pl.ANY)


def _gather_weights(bufs):
    n = len(bufs)

    def body(*refs):
        outs, send_sems, recv_sems = refs[n:2 * n], refs[2 * n], refs[2 * n + 1]
        x, y, c = _place()
        chips = _other_chips(x, y)

        def part(g, px, py, pc):
            half = outs[g].shape[1] // 2
            return outs[g].at[2 * px + py, pl.ds(pl.multiple_of(pc * half, 16), half), :]

        def copy(k, src, dst, to):
            return pltpu.make_async_remote_copy(src_ref=src, dst_ref=dst, send_sem=send_sems.at[k], recv_sem=recv_sems.at[k],
                                                device_id=to, device_id_type=MESH)

        first = [copy(6 * g + j, part(g, x, y, c), part(g, x, y, c), (*chip, c)) for g in range(n) for j, chip in enumerate(chips)]
        for cp in first:
            cp.start()
        passed = []
        for g in range(n):
            for j, chip in enumerate(chips):
                landed = part(g, *chip, c)
                copy(6 * g + j, landed, landed, (x, y, c)).wait_recv()
                passed.append(copy(6 * g + 3 + j, landed, landed, (x, y, 1 - c)))
                passed[-1].start()
        for g in range(n):
            for j, chip in enumerate(chips):
                other = part(g, *chip, 1 - c)
                copy(6 * g + 3 + j, other, other, (x, y, c)).wait_recv()
        for cp in first + passed:
            cp.wait_send()

    return pl.pallas_call(
        body, name="gather_weights", in_specs=[ANY] * n, out_specs=[ANY] * n,
        out_shape=[jax.ShapeDtypeStruct(b.shape, b.dtype) for b in bufs], input_output_aliases={g: g for g in range(n)},
        scratch_shapes=[pltpu.SemaphoreType.DMA((6 * n,)), pltpu.SemaphoreType.DMA((6 * n,))],
    )(*bufs)


def _cast_shards(shards, group, place, after=None):
    width, members = GROUPS[group]
    rows = _group_rows(group)
    extra = [] if after is None else [after]

    def body(place_ref, *refs):
        out = refs[-1]
        off = 0
        for ref, (_, r) in zip(refs[:-1], members):
            out[0, off:off + r, :] = ref[...].astype(BF16)
            off += r

    grid_spec = pltpu.PrefetchScalarGridSpec(
        num_scalar_prefetch=1, grid=(1,),
        in_specs=[pl.BlockSpec((r, width), lambda i, p: (0, 0)) for _, r in members] + [ANY] * len(extra),
        out_specs=pl.BlockSpec((1, rows, width), lambda i, p: (p[0], 0, 0)))
    return pl.pallas_call(
        body, name="cast_shards_" + group, grid_spec=grid_spec, out_shape=jax.ShapeDtypeStruct((4, rows, width), BF16),
        compiler_params=_cparams("arbitrary"),
    )(place, *[shards[name] for name, _ in members], *extra)


def _block_rows(h):
    return next(cand for cand in (256, 192, 128, 64, 32, 16) if h % cand == 0)


def _add_pair(buf, got, place, name):
    n, h, w = got.shape
    bh = _block_rows(h)
    nb = h // bh

    def body(place_ref, a_ref, b_ref, s_ref, sb_ref):
        s = a_ref[...] + b_ref[...]
        s_ref[...] = s
        sb_ref[...] = s.astype(BF16)

    spec = lambda: pl.BlockSpec((1, bh, w), lambda j, i, p: (j, i, 0))
    grid_spec = pltpu.PrefetchScalarGridSpec(
        num_scalar_prefetch=1, grid=(n, nb),
        in_specs=[pl.BlockSpec((1, bh, w), lambda j, i, p: (j, p[1] * nb + i, 0)), spec()], out_specs=[spec(), spec()])
    return pl.pallas_call(
        body, name=name, grid_spec=grid_spec,
        out_shape=[jax.ShapeDtypeStruct(got.shape, F32), jax.ShapeDtypeStruct(got.shape, BF16)],
        compiler_params=_cparams("parallel", "parallel"),
    )(place, buf, got)


def _add_received(pair, got, place, name):
    _, h, w = pair.shape
    bh = _block_rows(h)
    nb = h // bh

    def body(place_ref, own_ref, got_ref, o_ref):
        o_ref[...] = ((own_ref[0] + got_ref[0].astype(F32)) + got_ref[1].astype(F32)) + got_ref[2].astype(F32)

    grid_spec = pltpu.PrefetchScalarGridSpec(
        num_scalar_prefetch=1, grid=(nb,),
        in_specs=[pl.BlockSpec((1, bh, w), lambda i, p: (p[0], i, 0)), pl.BlockSpec((3, bh, w), lambda i, p: (0, i, 0))],
        out_specs=pl.BlockSpec((bh, w), lambda i, p: (p[1] * nb + i, 0)))
    return pl.pallas_call(
        body, name=name, grid_spec=grid_spec, out_shape=jax.ShapeDtypeStruct((2 * h, w), F32),
        compiler_params=_cparams("parallel"),
    )(place, pair, got)


def _swap_reduced_halves(bufs):
    n = len(bufs)

    def body(*refs):
        outs, send_sems, recv_sems = refs[n:2 * n], refs[2 * n], refs[2 * n + 1]
        x, y, c = _place()
        copies = []
        for g in range(n):
            half = outs[g].shape[0] // 2
            own = outs[g].at[pl.ds(pl.multiple_of(c * half, 8), half), :]
            copies.append(pltpu.make_async_remote_copy(src_ref=own, dst_ref=own, send_sem=send_sems.at[g],
                                                       recv_sem=recv_sems.at[g], device_id=(x, y, 1 - c), device_id_type=MESH))
        for cp in copies:
            cp.start()
        for g in range(n):
            half = outs[g].shape[0] // 2
            other = outs[g].at[pl.ds(pl.multiple_of((1 - c) * half, 8), half), :]
            pltpu.make_async_remote_copy(src_ref=other, dst_ref=other, send_sem=send_sems.at[g], recv_sem=recv_sems.at[g],
                                         device_id=(x, y, 1 - c), device_id_type=MESH).wait_recv()
        for cp in copies:
            cp.wait_send()

    return pl.pallas_call(
        body, name="swap_reduced_halves", in_specs=[ANY] * n, out_specs=[ANY] * n,
        out_shape=[jax.ShapeDtypeStruct(b.shape, b.dtype) for b in bufs], input_output_aliases={g: g for g in range(n)},
        scratch_shapes=[pltpu.SemaphoreType.DMA((n,)), pltpu.SemaphoreType.DMA((n,))],
    )(*bufs)


HBM = pl.BlockSpec(memory_space=pltpu.HBM)
SEM = pl.BlockSpec(memory_space=pltpu.SEMAPHORE)


def _copies_start(name, bufs, n_copies, plan, after=None):
    n = len(bufs)
    extra = [] if after is None else [after]

    def body(*refs):
        sems = refs[n + len(extra):n + len(extra) + 2 * n_copies]
        x, y, c = _place()
        for i, (src, dst, dev) in enumerate(plan(refs[:n], x, y, c)):
            pltpu.make_async_remote_copy(src_ref=src, dst_ref=dst, send_sem=sems[i], recv_sem=sems[n_copies + i],
                                         device_id=dev, device_id_type=MESH).start()
        token = refs[-1]
        token[...] = jnp.zeros_like(token)

    out = pl.pallas_call(
        body, name=name,
        out_shape=[pltpu.SemaphoreType.DMA(())] * (2 * n_copies) + [pltpu.HBM(b.shape, b.dtype) for b in bufs]
        + [jax.ShapeDtypeStruct((8, 128), F32)],
        in_specs=[HBM] * n + [ANY] * len(extra),
        out_specs=[SEM] * (2 * n_copies) + [HBM] * n + [pl.BlockSpec(memory_space=pltpu.VMEM)],
        input_output_aliases={i: 2 * n_copies + i for i in range(n)},
        compiler_params=pltpu.CompilerParams(has_side_effects=pltpu.SideEffectType.DATAFLOW_SIDE_EFFECTING),
    )(*[pltpu.with_memory_space_constraint(b, pltpu.HBM) for b in bufs], *extra)
    return list(out[:2 * n_copies]), list(out[2 * n_copies:-1]), out[-1]


def _copies_wait(name, bufs, sems, after, plan):
    n = len(bufs)
    k = len(sems) // 2

    def body(*refs):
        sem_refs = refs[n:n + 2 * k]
        x, y, c = _place()
        for i, (sent, landed, dev) in enumerate(plan(refs[:n], x, y, c)):
            cp = pltpu.make_async_remote_copy(src_ref=sent, dst_ref=landed, send_sem=sem_refs[i], recv_sem=sem_refs[k + i],
                                              device_id=dev, device_id_type=MESH)
            cp.wait_send()
            cp.wait_recv()

    return pl.pallas_call(
        body, name=name, out_shape=[pltpu.HBM(b.shape, b.dtype) for b in bufs],
        in_specs=[HBM] * n + [SEM] * (2 * k) + [ANY], out_specs=[HBM] * n, input_output_aliases={i: i for i in range(n)},
        compiler_params=pltpu.CompilerParams(has_side_effects=pltpu.SideEffectType.DATAFLOW_SIDE_EFFECTING),
    )(*bufs, *sems, after)


def _row_half(ref, which, axis):
    half = ref.shape[axis] // 2
    rows = pl.ds(pl.multiple_of(which * half, 8), half)
    return ref.at[rows, :] if axis == 0 else ref.at[:, rows, :]


class _SplitGather:
    def __init__(self, own, after):
        self.n = len(own)
        self.state = _copies_start("gather_start", own, 3 * self.n, self._sent, after)

    @staticmethod
    def _sent(refs, x, y, c):
        return [(w.at[2 * x + y], w.at[2 * x + y], (px, py, c)) for w in refs for px, py in _other_chips(x, y)]

    @staticmethod
    def _landed(refs, x, y, c):
        return [(w.at[2 * x + y], w.at[2 * px + py], (px, py, c)) for w in refs for px, py in _other_chips(x, y)]

    def token(self):
        return self.state[2]

    def wait(self, which, name, after):
        sems, bufs, _ = self.state
        k = 3 * self.n
        mine = [sems[3 * i + j] for i in which for j in range(3)] + [sems[k + 3 * i + j] for i in which for j in range(3)]
        return _copies_wait(name, [bufs[i] for i in which], mine, after, self._landed)


def _slot_half(ref, px, py, pc):
    half = ref.shape[1] // 2
    return ref.at[2 * px + py, pl.ds(pl.multiple_of(pc * half, 16), half), :]


class _SplitGatherHalves:
    def __init__(self, own):
        self.state = _copies_start("gather_b_start", [own], 3, self._sent)

    @staticmethod
    def _sent(refs, x, y, c):
        (w,) = refs
        return [(_slot_half(w, x, y, c), _slot_half(w, x, y, c), (px, py, c)) for px, py in _other_chips(x, y)]

    def token(self):
        return self.state[2]

    @staticmethod
    def _landed(refs, x, y, c):
        (w,) = refs
        return [(_slot_half(w, x, y, c), _slot_half(w, px, py, c), (px, py, c)) for px, py in _other_chips(x, y)]

    def finish(self, after):
        sems, bufs, _ = self.state
        (buf,) = _copies_wait("gather_b_wait", bufs, sems, after, self._landed)

        def body(buf_ref, out_ref, send_sems, recv_sems):
            x, y, c = _place()
            chips = _other_chips(x, y)

            def to_sibling(j, part, to):
                return pltpu.make_async_remote_copy(src_ref=part, dst_ref=part, send_sem=send_sems.at[j],
                                                    recv_sem=recv_sems.at[j], device_id=to, device_id_type=MESH)

            passed = [to_sibling(j, _slot_half(out_ref, *chip, c), (x, y, 1 - c)) for j, chip in enumerate(chips)]
            for cp in passed:
                cp.start()
            for j, chip in enumerate(chips):
                to_sibling(j, _slot_half(out_ref, *chip, 1 - c), (x, y, c)).wait_recv()
            for cp in passed:
                cp.wait_send()

        return pl.pallas_call(
            body, name="gather_b_pass", in_specs=[ANY], out_specs=ANY, out_shape=jax.ShapeDtypeStruct(buf.shape, buf.dtype),
            input_output_aliases={0: 0}, scratch_shapes=[pltpu.SemaphoreType.DMA((3,)), pltpu.SemaphoreType.DMA((3,))],
        )(buf)


class _SplitReduction:
    def __init__(self, tag, groups, place):
        self.tag, self.groups, self.place = tag, groups, place

    def start_pair(self, bufs):
        n = len(bufs)
        lands = [lax.empty((4, b.shape[1] // 2, b.shape[2]), F32) for b in bufs]
        plan = lambda refs, x, y, c: [(_row_half(refs[i], 1 - c, 1), refs[n + i], (x, y, 1 - c)) for i in range(n)]
        self._pair = (_copies_start("pair_%s_start" % self.tag, bufs + lands, n, plan), plan, n)
        return self._pair[0][2]

    def pair_done_start_scatter(self, after):
        (sems, bufs, _), plan, n = self._pair
        out = _copies_wait("pair_%s_wait" % self.tag, bufs, sems, after, plan)
        pairs = [_add_pair(out[i], out[n + i], self.place, "add_pair_" + g) for i, g in enumerate(self.groups)]
        self._pair_f32 = [p[0] for p in pairs]
        lands = [lax.empty((3,) + p[1].shape[1:], BF16) for p in pairs]
        plan = lambda refs, x, y, c: [(refs[i].at[2 * px + py], refs[n + i].at[j], (px, py, c))
                                      for i in range(n) for j, (px, py) in enumerate(_other_chips(x, y))]
        self._scatter = (_copies_start("scatter_%s_start" % self.tag, [p[1] for p in pairs] + lands, 3 * n, plan), plan, n)
        return self._scatter[0][2]

    def scatter_done(self, after):
        (sems, bufs, _), plan, n = self._scatter
        out = _copies_wait("scatter_%s_wait" % self.tag, bufs, sems, after, plan)
        return [_add_received(self._pair_f32[i], out[n + i], self.place, "add_received_" + g)
                for i, g in enumerate(self.groups)]

    def start_join(self, halves):
        n = len(halves)
        sent = lambda refs, x, y, c: [(_row_half(r, c, 0), _row_half(r, c, 0), (x, y, 1 - c)) for r in refs]
        landed = lambda refs, x, y, c: [(_row_half(r, c, 0), _row_half(r, 1 - c, 0), (x, y, 1 - c)) for r in refs]
        self._join = (_copies_start("join_%s_start" % self.tag, halves, n, sent), landed)
        return self._join[0][2]

    def join_done(self, after):
        (sems, bufs, _), landed = self._join
        return _copies_wait("join_%s_wait" % self.tag, bufs, sems, after, landed)


def _pair_sum_small(mine):
    rows, w = mine.shape

    def body(in_ref, out_ref, sibling, send_sem, recv_sem):
        x, y, c = _place()
        swap = pltpu.make_async_remote_copy(src_ref=in_ref, dst_ref=sibling, send_sem=send_sem, recv_sem=recv_sem,
                                            device_id=(x, y, 1 - c), device_id_type=MESH)
        swap.start()
        swap.wait()
        out_ref[...] = in_ref[...] + sibling[...]

    return pl.pallas_call(
        body, name="pair_sum_small", out_shape=jax.ShapeDtypeStruct((rows, w), F32),
        in_specs=[pl.BlockSpec(memory_space=pltpu.VMEM)], out_specs=pl.BlockSpec(memory_space=pltpu.VMEM),
        scratch_shapes=[pltpu.VMEM((rows, w), F32), pltpu.SemaphoreType.DMA, pltpu.SemaphoreType.DMA],
        compiler_params=pltpu.CompilerParams(vmem_limit_bytes=VMEM_LIMIT_V7X),
    )(mine)


class _SplitChipSum:
    def __init__(self, pair, place):
        self.place = place
        slots = lax.empty((4,) + pair.shape, F32)
        sent = lambda refs, x, y, c: [(refs[0], refs[1].at[2 * x + y], (px, py, c)) for px, py in _other_chips(x, y)]
        self.landed = lambda refs, x, y, c: [(refs[0], refs[1].at[2 * px + py], (px, py, c)) for px, py in _other_chips(x, y)]
        self.state = _copies_start("small_sum_start", [pair, slots], 3, sent)

    def token(self):
        return self.state[2]

    def done(self, after):
        sems, bufs, _ = self.state
        pair, slots = _copies_wait("small_sum_wait", bufs, sems, after, self.landed)
        rows, w = pair.shape

        def body(place_ref, pair_ref, slots_ref, out_ref):
            for j in range(4):
                own = place_ref[0] == j

                @pl.when(own)
                def _():
                    out_ref[...] = pair_ref[...] if j == 0 else out_ref[...] + pair_ref[...]

                @pl.when(jnp.logical_not(own))
                def _():
                    out_ref[...] = slots_ref[j] if j == 0 else out_ref[...] + slots_ref[j]

        grid_spec = pltpu.PrefetchScalarGridSpec(
            num_scalar_prefetch=1, grid=(1,),
            in_specs=[pl.BlockSpec((rows, w), lambda i, p: (0, 0)), pl.BlockSpec((4, rows, w), lambda i, p: (0, 0, 0))],
            out_specs=pl.BlockSpec((rows, w), lambda i, p: (0, 0)))
        return pl.pallas_call(
            body, name="small_sum_add", grid_spec=grid_spec, out_shape=jax.ShapeDtypeStruct((rows, w), F32),
            compiler_params=_cparams("arbitrary"),
        )(self.place, pair, slots)


def _join_column_shards(g):
    return jnp.transpose(g, (1, 0, 2)).reshape(g.shape[1], 4 * g.shape[2])


def _split_column_shards(w):
    r = w.shape[0]
    return jnp.transpose(w.reshape(r, 4, w.shape[1] // 4), (1, 0, 2))


def _small_rows(shape):
    return -(-int(np.prod(shape)) // 1024)


def _pack_small(vals):
    segs = []
    for name, shape in SMALL_WEIGHTS:
        flat = vals[name].reshape(-1)
        segs.append(jnp.pad(flat, (0, _small_rows(shape) * 1024 - flat.shape[0])))
    total = sum(s.shape[0] for s in segs) // 1024
    segs.append(jnp.zeros((-total % 8 * 1024,), F32))
    return jnp.concatenate(segs).reshape(-1, 1024)


def _unpack_small(packed):
    out, off = {}, 0
    for name, shape in SMALL_WEIGHTS:
        rows = _small_rows(shape)
        out[name] = packed[off:off + rows].reshape(-1)[:int(np.prod(shape))].reshape(shape)
        off += rows
    return out


W_IN_SHARD = D_IN // 4
W_IN_GAP = 1216


def _pad_w_in(g):
    cut = W_IN_GAP - W_IN_SHARD
    return jnp.concatenate([g[0], g[1][:, :cut], jnp.zeros((g.shape[1], D_IN_PAD - D_IN), g.dtype), g[1][:, cut:], g[2], g[3]],
                           axis=1)


def _unpad_w_in(g):
    skip = D_IN_PAD - D_IN
    second = jnp.concatenate([g[:, W_IN_SHARD:W_IN_GAP], g[:, W_IN_GAP + skip:2 * W_IN_SHARD + skip]], axis=1)
    return jnp.stack([g[:, :W_IN_SHARD], second, g[:, 2 * W_IN_SHARD + skip:3 * W_IN_SHARD + skip],
                      g[:, 3 * W_IN_SHARD + skip:]])


def _pad_heads(w):
    r = w.shape[0]
    return jnp.pad(w.reshape(r, N_HEADS, QK_HEAD), ((0, 0), (0, 0), (0, HEAD_PAD - QK_HEAD))).reshape(r, N_HEADS * HEAD_PAD)


def _unpad_heads(g):
    r = g.shape[0]
    return g.reshape(r, N_HEADS, HEAD_PAD)[:, :, :QK_HEAD].reshape(r, N_HEADS * QK_HEAD)


def _local_step(x, positions, tgt, grp_b, small, gather, red_a, red_rest):
    l = x.shape[0]
    t = min(l, 512)
    t_mlp = min(l, 256)
    tq = min(l, 1024)
    tc = min(l, 256)
    row = lambda v: v.reshape(1, -1).astype(F32)

    w_in_p = _pad_w_in(grp_b)
    g1, g2 = row(small["norm_mix"]), row(small["norm_mlp"])
    gqa, gkva = row(small["q_a_norm"]), row(small["kv_a_norm"])
    gq = jnp.pad(row(small["q_norm"]), ((0, 0), (0, HEAD_PAD - QK_HEAD)))
    gk = jnp.pad(row(small["k_norm"]), ((0, 0), (0, HEAD_PAD - QK_HEAD)))
    half = QK_ROPE // 2
    inv_freq = ROPE_THETA ** (-jnp.arange(half, dtype=F32) / half)
    invf = jnp.concatenate([inv_freq, inv_freq, jnp.zeros((64,), F32)]).reshape(1, 128)
    sgn = jnp.concatenate([-jnp.ones((half,), F32), jnp.ones((half,), F32), jnp.zeros((64,), F32)]).reshape(1, 128)
    pos = positions.reshape(l, 1)

    a_re, a_im = small["ssm_a_re"], small["ssm_a_im"]
    log_dt = small["ssm_log_dt"].reshape(SSM_GROUPS, 1)
    to_gcp = lambda b: jnp.transpose(b, (0, 2, 1)).reshape(SSM_WIDTH, SSM_STATE)
    from_gcp = lambda b: jnp.transpose(b.reshape(SSM_GROUPS, SSM_GROUP_CH, SSM_STATE), (0, 2, 1))
    b_re, b_im = to_gcp(small["ssm_b_re"]), to_gcp(small["ssm_b_im"])
    c_re, c_im = small["ssm_c_re"].reshape(SSM_WIDTH, SSM_STATE), small["ssm_c_im"].reshape(SSM_WIDTH, SSM_STATE)
    wb, wc, tabs_fwd, tabs_rev = _ssm_param_fwd(a_re, a_im, log_dt, b_re, b_im, c_re, c_im)
    dskip = row(small["ssm_d"])
    b_glu = row(small["b_glu"])

    u, lat, gs, gm = _in_proj_fwd(x, g1, w_in_p, t, gather.token())
    grp_c, grp_d, grp_e = gather.wait([0, 1, 2], "gather_cde_wait", u)
    w_qb_p = _pad_heads(_join_column_shards(grp_c))
    xr, xi, y, y_ssm = _ssm_fwd(u, wb, wc, tabs_fwd, dskip, grp_d, b_glu, grp_e, tc)
    q, k, v = _mla_pre_fwd(lat, pos, invf, sgn, gqa, gkva, gq, gk, w_qb_p, grp_d, t)
    attn, lse = _attn_fwd(q, k, v, tq)
    (grp_a,) = gather.wait([3], "gather_a_wait", attn)
    y_mla, mixed, h = _merge_fwd(attn, y_ssm, gs, gm, x, grp_a, t)
    dh, hn, da, hid, dout, loss_blk, g_norm_mlp = _mlp_fwd_bwd(h, tgt, g2, grp_a, t_mlp)

    ga = _wgrad_into(hn, da, "w_up", "col", _wgrad_into(hid, dout, "w_down", "row"))
    dys, dym, dgs, dgm, dattn = _merge_bwd(dh, y_ssm, y_mla, gs, gm, grp_a, t)
    ga = _wgrad_into(attn, dym, "w_o_mla", "row", _wgrad_into(mixed, dh, "w_out", "row", ga))

    dq, dk, dv = _attn_bwd(q, k, v, attn, dattn, lse, tq, red_a.start_pair([ga]))
    d_lat, ql, dq0, ckn, dkv, g_qa, g_kva, g_q, g_k = _mla_pre_bwd(lat, pos, invf, sgn, gqa, gkva, gq, gk, w_qb_p, grp_d,
                                                                    dq, dk, dv, t, red_a.pair_done_start_scatter(dk))
    gc = _split_column_shards(_unpad_heads(_wgrad(ql, dq0, "wgrad_q_b")))

    d_u, adj, dy, z, z2, dpre, g_b_glu, g_d, g_lr, g_li = _ssm_bwd(
        dys, y, u, xr, xi, wb, wc, tabs_rev, dskip, grp_d, b_glu, grp_e, tc)
    gd = _wgrad_into(z, dpre, "w_glu", "row", _wgrad_into(ckn, dkv, "w_kv_b", "col"))
    ge = _wgrad_into(z2, dys, "w_o_ssm", "col")
    grad_x, xn, dproj, g_norm_mix = _in_proj_bwd(x, g1, w_in_p, d_u, d_lat, dgs, dgm, dh, t)
    gb = _unpad_w_in(_wgrad(xn, dproj, "wgrad_in"))

    red_a.start_join(red_a.scatter_done(gb))
    g_wb = _wgrad_strips(u, adj, adj, "wgrad_ssm_b", 1, red_rest.start_pair([gb, gc, gd, ge]))
    g_wct = _wgrad_strips(dy, xr, xi, "wgrad_ssm_c", 0, red_rest.pair_done_start_scatter(g_wb))
    g_ar, g_ai, g_ldt, g_br, g_bi, g_cr, g_ci = _ssm_param_bwd(a_re, a_im, log_dt, b_re, b_im, g_lr, g_li, g_wb, g_wct)

    g_small = {
        "norm_mix": g_norm_mix.reshape(-1), "norm_mlp": g_norm_mlp.reshape(-1), "q_a_norm": g_qa.reshape(-1),
        "kv_a_norm": g_kva.reshape(-1), "q_norm": g_q.reshape(-1)[:QK_HEAD], "k_norm": g_k.reshape(-1)[:QK_HEAD],
        "ssm_a_re": g_ar, "ssm_a_im": g_ai, "ssm_log_dt": g_ldt.reshape(-1),
        "ssm_b_re": from_gcp(g_br), "ssm_b_im": from_gcp(g_bi),
        "ssm_c_re": g_cr.reshape(SSM_GROUPS, SSM_GROUP_CH, SSM_STATE), "ssm_c_im": g_ci.reshape(SSM_GROUPS, SSM_GROUP_CH, SSM_STATE),
        "ssm_d": g_d.reshape(SSM_GROUPS, SSM_GROUP_CH), "b_glu": g_b_glu.reshape(-1),
    }
    return loss_blk[0, 0], grad_x, g_small


def kernel(x, positions, norm_mix, w_in, q_a_norm, kv_a_norm, w_q_b, w_kv_b, q_norm, k_norm, w_o_mla, ssm_a_re, ssm_a_im, ssm_log_dt, ssm_b_re, ssm_b_im, ssm_c_re, ssm_c_im, ssm_d, w_glu, b_glu, w_o_ssm, w_out, norm_mlp, w_up, w_down, loss_target, m_norm_mix, m_w_in, m_q_a_norm, m_kv_a_norm, m_w_q_b, m_w_kv_b, m_q_norm, m_k_norm, m_w_o_mla, m_ssm_a_re, m_ssm_a_im, m_ssm_log_dt, m_ssm_b_re, m_ssm_b_im, m_ssm_c_re, m_ssm_c_im, m_ssm_d, m_w_glu, m_b_glu, m_w_o_ssm, m_w_out, m_norm_mlp, m_w_up, m_w_down, v_norm_mix, v_w_in, v_q_a_norm, v_kv_a_norm, v_w_q_b, v_w_kv_b, v_q_norm, v_k_norm, v_w_o_mla, v_ssm_a_re, v_ssm_a_im, v_ssm_log_dt, v_ssm_b_re, v_ssm_b_im, v_ssm_c_re, v_ssm_c_im, v_ssm_d, v_w_glu, v_b_glu, v_w_o_ssm, v_w_out, v_norm_mlp, v_w_up, v_w_down):
    args = dict(locals())
    w = {n: args[n][0] for n in WEIGHT_ORDER}
    m = {n: args["m_" + n][0] for n in WEIGHT_ORDER}
    v = {n: args["v_" + n][0] for n in WEIGHT_ORDER}
    big_names = [n for n, *_ in BIG_WEIGHTS]
    small_names = [n for n, _ in SMALL_WEIGHTS]

    place = jnp.stack([2 * lax.axis_index("x") + lax.axis_index("y"), lax.axis_index("c")]).astype(jnp.int32)
    rest = ["b", "c", "d", "e"]

    gather_b = _SplitGatherHalves(_cast_shards(w, "b", place))
    gather = _SplitGather([_cast_shards(w, g, place, gather_b.token()) for g in ("c", "d", "e", "a")], gather_b.token())
    grp_b = gather_b.finish(gather.token())
    red_a = _SplitReduction("a", ["a"], place)
    red_rest = _SplitReduction("rest", rest, place)
    small = {n: w[n] for n in small_names}

    loss_local, grad_x, g_small = _local_step(x[0], positions[0], loss_target[0], grp_b, small, gather, red_a, red_rest)
    loss = lax.psum(loss_local, ("x", "y", "c"))

    grad_w, delta_w, new_m, new_v = {}, {}, {}, {}

    def update(names, reduced, token):
        for n in names:
            g, off, _, _ = _place_in_group(n)
            grad_w[n], delta_w[n], new_m[n], new_v[n] = _adamw(w[n], reduced[g], m[n], v[n], "adamw_" + n, off, token)
            token = new_v[n]

    chip_sum = _SplitChipSum(_pair_sum_small(_pack_small(g_small)), place)
    in_a = [n for n, _ in GROUPS["a"][1]]
    update(in_a, {"a": red_a.join_done(chip_sum.token())[0]}, chip_sum.token())
    small_sum = chip_sum.done(new_v[in_a[-1]])
    g_s, d_s, m_s, v_s = _adamw(_pack_small(small), small_sum, _pack_small({n: m[n] for n in small_names}),
                                _pack_small({n: v[n] for n in small_names}), "adamw_small", 0, small_sum)
    g_s, d_s, m_s, v_s = _unpack_small(g_s), _unpack_small(d_s), _unpack_small(m_s), _unpack_small(v_s)
    for n in small_names:
        grad_w[n], delta_w[n], new_m[n], new_v[n] = g_s[n], d_s[n], m_s[n], v_s[n]
    halves = red_rest.scatter_done(v_s[small_names[0]])
    reduced_rest = dict(zip(rest, _swap_reduced_halves(halves)))
    update([n for n in big_names if n not in in_a], reduced_rest, halves[0])

    lead = lambda d: [d[n][None] for n in WEIGHT_ORDER]
    return (loss, grad_x[None], *lead(grad_w), *lead(delta_w), *lead(new_m), *lead(new_v))
```

```python
import math

import jax
import jax.numpy as jnp
import numpy as np
from jax import lax
from jax.experimental import pallas as pl
from jax.experimental.pallas import tpu as pltpu

F32 = jnp.float32
BF16 = jnp.bfloat16

D_MODEL = 1024
SSM_GROUPS = 32
SSM_GROUP_CH = 16
SSM_WIDTH = 512
SSM_STATE = 64
GP = SSM_GROUPS * SSM_STATE
N_HEADS = 8
QK_NOPE = 128
QK_ROPE = 64
QK_HEAD = 192
HEAD_PAD = 256
V_HEAD = 128
Q_LORA = 384
KV_LORA = 256
LAT_W = 768
D_IN = 3264
D_IN_PAD = 3328
D_FF = 4096
ROPE_THETA = 10000.0
EPS = 1e-6
ATT_SCALE = QK_HEAD ** -0.5

ADAM_LR = 0.001
ADAM_B1 = 0.9
ADAM_B2 = 0.999
ADAM_EPS = 1e-08
ADAM_WD = 0.01
ADAM_STEP = 10

VMEM_LIMIT_V7X = 56 * 1024 * 1024
MESH = pl.DeviceIdType.MESH

BIG_WEIGHTS = (
    ("w_in", 1024, 3264, "col"),
    ("w_q_b", 384, 1536, "col"),
    ("w_kv_b", 256, 2048, "col"),
    ("w_o_mla", 1024, 1024, "row"),
    ("w_glu", 512, 512, "row"),
    ("w_o_ssm", 512, 1024, "col"),
    ("w_out", 1024, 1024, "row"),
    ("w_up", 1024, 4096, "col"),
    ("w_down", 4096, 1024, "row"),
)
GROUPS = {
    "a": (1024, (("w_down", 1024), ("w_up", 1024), ("w_o_mla", 256), ("w_out", 256))),
    "b": (816, (("w_in", 1024),)),
    "c": (384, (("w_q_b", 384),)),
    "d": (512, (("w_kv_b", 256), ("w_glu", 128))),
    "e": (256, (("w_o_ssm", 512),)),
}


def _group_rows(group):
    return sum(r for _, r in GROUPS[group][1])


def _place_in_group(name):
    for group, (width, members) in GROUPS.items():
        off = 0
        for member, rows in members:
            if member == name:
                return group, off, rows, width
            off += rows
    raise KeyError(name)


SMALL_WEIGHTS = (
    ("norm_mix", (1024,)), ("q_a_norm", (384,)), ("kv_a_norm", (256,)), ("q_norm", (192,)), ("k_norm", (192,)),
    ("ssm_a_re", (32, 64)), ("ssm_a_im", (32, 64)), ("ssm_log_dt", (32,)),
    ("ssm_b_re", (32, 64, 16)), ("ssm_b_im", (32, 64, 16)), ("ssm_c_re", (32, 16, 64)), ("ssm_c_im", (32, 16, 64)),
    ("ssm_d", (32, 16)), ("b_glu", (512,)), ("norm_mlp", (1024,)),
)
WEIGHT_ORDER = ('norm_mix', 'w_in', 'q_a_norm', 'kv_a_norm', 'w_q_b', 'w_kv_b', 'q_norm', 'k_norm', 'w_o_mla', 'ssm_a_re',
                'ssm_a_im', 'ssm_log_dt', 'ssm_b_re', 'ssm_b_im', 'ssm_c_re', 'ssm_c_im', 'ssm_d', 'w_glu', 'b_glu',
                'w_o_ssm', 'w_out', 'norm_mlp', 'w_up', 'w_down')


def _cparams(*sem):
    return pltpu.CompilerParams(dimension_semantics=sem if sem else None, vmem_limit_bytes=VMEM_LIMIT_V7X)


def _resident(shape, index=None):
    index = (0,) * len(shape) if index is None else index
    return pl.BlockSpec(shape, lambda *_: index, pipeline_mode=pl.Buffered(1))


def _member_block(name):
    _, off, rows, width = _place_in_group(name)
    return _resident((4, rows, width), (0, off // rows, 0))


def _rows(t, width):
    return pl.BlockSpec((t, width), lambda i: (i, 0))


def _mm(a, b):
    return jnp.dot(a.astype(BF16), b.astype(BF16), preferred_element_type=F32)


def _mm_nt(a, b):
    return lax.dot_general(a.astype(BF16), b.astype(BF16), (((1,), (1,)), ((), ())), preferred_element_type=F32)


def _mm_tn(a, b):
    return lax.dot_general(a.astype(BF16), b.astype(BF16), (((0,), (0,)), ((), ())), preferred_element_type=F32)


def _rms_fwd(x, g, n):
    r = lax.rsqrt(jnp.sum(x * x, axis=-1, keepdims=True) * (1.0 / n) + EPS)
    return x * r * g


def _rms_bwd(x, g, dy, n):
    r = lax.rsqrt(jnp.sum(x * x, axis=-1, keepdims=True) * (1.0 / n) + EPS)
    xh = x * r
    dxh = dy * g
    dx = r * (dxh - xh * (jnp.sum(dxh * xh, axis=-1, keepdims=True) * (1.0 / n)))
    return dx, dy * xh


def _colsum(a):
    return jnp.sum(a, axis=0, keepdims=True)


def _accumulate(ref, value, first):
    @pl.when(first)
    def _():
        ref[...] = value

    @pl.when(jnp.logical_not(first))
    def _():
        ref[...] += value


def _sigmoid(a):
    return 1.0 / (1.0 + jnp.exp(-a))


GELU_C = math.sqrt(2.0 / math.pi)
GELU_A = 0.044715


def _gelu(y):
    return 0.5 * y * (1.0 + jnp.tanh(GELU_C * (y + GELU_A * y * y * y)))


def _gelu_grad(y):
    t = jnp.tanh(GELU_C * (y + GELU_A * y * y * y))
    return 0.5 * (1.0 + t) + 0.5 * y * (1.0 - t * t) * GELU_C * (1.0 + 3.0 * GELU_A * y * y)


def _in_proj_fwd(x, g1, w_in_p, t, token):
    l = x.shape[0]

    def body(x_ref, g_ref, w_ref, token_ref, u_ref, lat_ref, gs_ref, gm_ref):
        xn = _rms_fwd(x_ref[...], g_ref[...], D_MODEL).astype(BF16)
        u_ref[...] = _mm(xn, w_ref[:, 0:512])
        lat_ref[...] = _mm(xn, w_ref[:, 512:1280])
        gs_ref[...] = _mm(xn, w_ref[:, 1280:2304]).astype(BF16)
        gm_ref[...] = _mm(xn, w_ref[:, 2304:3328]).astype(BF16)

    return pl.pallas_call(
        body, name="in_proj_fwd", grid=(l // t,),
        in_specs=[_rows(t, D_MODEL), _resident((1, D_MODEL)), _resident((D_MODEL, D_IN_PAD)), ANY],
        out_specs=[_rows(t, 512), _rows(t, LAT_W), _rows(t, D_MODEL), _rows(t, D_MODEL)],
        out_shape=[jax.ShapeDtypeStruct((l, 512), F32), jax.ShapeDtypeStruct((l, LAT_W), F32),
                   jax.ShapeDtypeStruct((l, D_MODEL), BF16), jax.ShapeDtypeStruct((l, D_MODEL), BF16)],
        compiler_params=_cparams("parallel"),
    )(x, g1, w_in_p, token)


def _in_proj_bwd(x, g1, w_in_p, d_u, d_lat, d_gs, d_gm, dh, t):
    l = x.shape[0]

    def body(x_ref, g_ref, w_ref, du_ref, dlat_ref, dgs_ref, dgm_ref, dh_ref, gx_ref, xn_ref, dproj_ref, dg_ref):
        xv = x_ref[...]
        g = g_ref[...]
        xn_ref[...] = _rms_fwd(xv, g, D_MODEL).astype(BF16)
        dproj_ref[:, 0:512] = du_ref[...]
        dproj_ref[:, 512:1280] = dlat_ref[...]
        dproj_ref[:, 1280:2304] = dgs_ref[...]
        dproj_ref[:, 2304:3328] = dgm_ref[...]
        dxn = _mm_nt(dproj_ref[...], w_ref[...])
        dx, dg_rows = _rms_bwd(xv, g, dxn, D_MODEL)
        gx_ref[...] = dh_ref[...] + dx
        _accumulate(dg_ref, _colsum(dg_rows), pl.program_id(0) == 0)

    return pl.pallas_call(
        body, name="in_proj_bwd", grid=(l // t,),
        in_specs=[_rows(t, D_MODEL), _resident((1, D_MODEL)), _resident((D_MODEL, D_IN_PAD)), _rows(t, 512),
                  _rows(t, LAT_W), _rows(t, D_MODEL), _rows(t, D_MODEL), _rows(t, D_MODEL)],
        out_specs=[_rows(t, D_MODEL), _rows(t, D_MODEL), _rows(t, D_IN_PAD), pl.BlockSpec((1, D_MODEL), lambda i: (0, 0))],
        out_shape=[jax.ShapeDtypeStruct((l, D_MODEL), F32), jax.ShapeDtypeStruct((l, D_MODEL), BF16),
                   jax.ShapeDtypeStruct((l, D_IN_PAD), BF16), jax.ShapeDtypeStruct((1, D_MODEL), F32)],
        compiler_params=_cparams("arbitrary"),
    )(x, g1, w_in_p, d_u, d_lat, d_gs, d_gm, dh)


def _ssm_param_fn(a_re, a_im, log_dt, b_re, b_im):
    dt = jnp.exp(log_dt)
    er = jnp.exp(a_re * dt)
    lr = er * jnp.cos(a_im * dt)
    li = er * jnp.sin(a_im * dt)
    den = a_re * a_re + a_im * a_im
    nr = lr - 1.0
    kr = (nr * a_re + li * a_im) / den
    ki = (li * a_re - nr * a_im) / den
    rows = lambda k: jnp.broadcast_to(k[:, None, :], (SSM_GROUPS, SSM_GROUP_CH, SSM_STATE)).reshape(SSM_WIDTH, SSM_STATE)
    krt, kit = rows(kr), rows(ki)
    return lr, li, krt * b_re - kit * b_im, krt * b_im + kit * b_re


def _state_selector():
    row = lax.broadcasted_iota(jnp.int32, (SSM_STATE, GP), 0)
    col = lax.broadcasted_iota(jnp.int32, (SSM_STATE, GP), 1)
    return jnp.where(jnp.bitwise_and(col, SSM_STATE - 1) == row, 1.0, 0.0).astype(BF16)


def _own_group(rows, rows_per_group_log2):
    row = lax.broadcasted_iota(jnp.int32, (rows, GP), 0)
    col = lax.broadcasted_iota(jnp.int32, (rows, GP), 1)
    return jnp.right_shift(row, rows_per_group_log2) == jnp.right_shift(col, 6)


def _three_bf16(x):
    hi = x.astype(BF16)
    rest = x - hi.astype(F32)
    mid = rest.astype(BF16)
    return hi, mid, (rest - mid.astype(F32)).astype(BF16)


def _spread(x, sel):
    return sum(jnp.dot(part, sel, preferred_element_type=F32) for part in _three_bf16(x))


def _collect(xw, sel):
    return sum(lax.dot_general(part, sel, (((1,), (1,)), ((), ())), preferred_element_type=F32) for part in _three_bf16(xw))


def _ssm_param_fwd(a_re, a_im, log_dt, b_re, b_im, c_re, c_im):
    def body(ar_ref, ai_ref, ldt_ref, br_ref, bi_ref, cr_ref, ci_ref, wb_ref, wct_ref, tf_ref, tr_ref):
        lr, li, bbr, bbi = _ssm_param_fn(ar_ref[...], ai_ref[...], ldt_ref[...], br_ref[...], bi_ref[...])
        sel = _state_selector()
        own16 = _own_group(SSM_WIDTH, 4)
        own1 = _own_group(SSM_GROUPS, 0)
        block = lambda m: jnp.where(own16, jnp.dot(m.astype(BF16), sel, preferred_element_type=F32), 0.0).astype(BF16)
        wb_ref[:, 0:GP] = block(bbr)
        wb_ref[:, GP:2 * GP] = block(bbi)
        wct_ref[:, 0:GP] = block(cr_ref[...])
        wct_ref[:, GP:2 * GP] = block(-ci_ref[...])
        flat = lambda m: _colsum(jnp.where(own1, _spread(m, sel), 0.0))
        pr, pi = [], []
        qr, qi = lr, li
        for _ in range(8):
            pr.append(flat(qr))
            pi.append(flat(qi))
            qr, qi = qr * lr - qi * li, qr * li + qi * lr
        row = lax.broadcasted_iota(jnp.int32, (8, GP), 0)
        for n, k in enumerate((1, 2, 4)):
            tf_ref[2 * n] = jnp.where(row >= k, pr[k - 1], 0.0)
            tf_ref[2 * n + 1] = jnp.where(row >= k, pi[k - 1], 0.0)
            tr_ref[2 * n] = jnp.where(row < 8 - k, pr[k - 1], 0.0)
            tr_ref[2 * n + 1] = jnp.where(row < 8 - k, -pi[k - 1], 0.0)
        pick = lambda vals: sum(jnp.where(row == j, v, 0.0) for j, v in enumerate(vals))
        tf_ref[6] = pick(pr)
        tf_ref[7] = pick(pi)
        tr_ref[6] = pick(pr[::-1])
        tr_ref[7] = pick([-v for v in pi[::-1]])

    return pl.pallas_call(
        body, name="ssm_param_fwd",
        out_shape=[jax.ShapeDtypeStruct((SSM_WIDTH, 2 * GP), BF16), jax.ShapeDtypeStruct((SSM_WIDTH, 2 * GP), BF16),
                   jax.ShapeDtypeStruct((8, 8, GP), F32), jax.ShapeDtypeStruct((8, 8, GP), F32)],
        compiler_params=_cparams(),
    )(a_re, a_im, log_dt, b_re, b_im, c_re, c_im)


STRIP_CH = 128
STRIP_ST = 512
N_STRIPS = SSM_WIDTH // STRIP_CH


def _ssm_param_bwd(a_re, a_im, log_dt, b_re, b_im, g_lr, g_li, g_wb, g_wct):
    def body(ar_ref, ai_ref, ldt_ref, br_ref, bi_ref, glr_ref, gli_ref, gwb_ref, gwc_ref,
             o_ar, o_ai, o_ldt, o_br, o_bi, o_cr, o_ci):
        sel = _state_selector()
        own1 = _own_group(SSM_GROUPS, 0)
        row = lax.broadcasted_iota(jnp.int32, (SSM_WIDTH, STRIP_ST), 0)
        col = lax.broadcasted_iota(jnp.int32, (SSM_WIDTH, STRIP_ST), 1)
        own = jnp.bitwise_and(jnp.right_shift(row, 4), 7) == jnp.right_shift(col, 6)
        blocks = lambda m: _collect(jnp.where(own, m, 0.0), sel[:, 0:STRIP_ST])
        unflat = lambda v: _collect(jnp.where(own1, v, 0.0), sel)
        _, vjp = jax.vjp(_ssm_param_fn, ar_ref[...], ai_ref[...], ldt_ref[...], br_ref[...], bi_ref[...])
        d_ar, d_ai, d_ldt, d_br, d_bi = vjp((unflat(glr_ref[...]), unflat(gli_ref[...]),
                                             blocks(gwb_ref[:, 0:STRIP_ST]), blocks(gwb_ref[:, STRIP_ST:2 * STRIP_ST])))
        o_ar[...] = d_ar
        o_ai[...] = d_ai
        o_ldt[...] = d_ldt
        o_br[...] = d_br
        o_bi[...] = d_bi
        o_cr[...] = blocks(gwc_ref[:, 0:STRIP_ST])
        o_ci[...] = -blocks(gwc_ref[:, STRIP_ST:2 * STRIP_ST])

    g, p = SSM_GROUPS, SSM_STATE
    gp = jax.ShapeDtypeStruct((g, p), F32)
    gcp = jax.ShapeDtypeStruct((SSM_WIDTH, p), F32)
    return pl.pallas_call(
        body, name="ssm_param_bwd", out_shape=[gp, gp, jax.ShapeDtypeStruct((g, 1), F32), gcp, gcp, gcp, gcp],
        compiler_params=_cparams(),
    )(a_re, a_im, log_dt, b_re, b_im, g_lr, g_li, g_wb, g_wct)


def _strip(ref, j, im):
    return ref[STRIP_CH * j:STRIP_CH * (j + 1), im * GP + STRIP_ST * j:im * GP + STRIP_ST * (j + 1)]


def _wgrad_strips(a, b_re, b_im, name, im_block, token):
    l = a.shape[0]
    bl = min(l, 512)

    def body(a_ref, bre_ref, bim_ref, token_ref, o_ref):
        first = pl.program_id(0) == 0
        for j in range(N_STRIPS):
            aj = a_ref[:, STRIP_CH * j:STRIP_CH * (j + 1)]
            states = slice(STRIP_ST * j, STRIP_ST * (j + 1))
            _accumulate(o_ref.at[STRIP_CH * j:STRIP_CH * (j + 1), 0:STRIP_ST], _mm_tn(aj, bre_ref[:, states]), first)
            _accumulate(o_ref.at[STRIP_CH * j:STRIP_CH * (j + 1), STRIP_ST:2 * STRIP_ST], _mm_tn(aj, bim_ref[:, states]), first)

    return pl.pallas_call(
        body, name=name, grid=(l // bl,),
        in_specs=[pl.BlockSpec((bl, SSM_WIDTH), lambda k: (k, 0)), pl.BlockSpec((bl, GP), lambda k: (k, 0)),
                  pl.BlockSpec((bl, GP), lambda k: (k, im_block)), ANY],
        out_specs=pl.BlockSpec((SSM_WIDTH, 2 * STRIP_ST), lambda k: (0, 0)),
        out_shape=jax.ShapeDtypeStruct((SSM_WIDTH, 2 * STRIP_ST), F32),
        compiler_params=_cparams("arbitrary"),
    )(a, b_re, b_im, token)


SCAN_STRIP = 512


def _scan_chunk(inr_ref, ini_ref, outr_ref, outi_ref, cr_ref, ci_ref, tab_ref, tc, reverse):
    n_blocks = tc // 8

    def block(j, _):
        i = (n_blocks - 1 - j) if reverse else j
        rows = pl.ds(pl.multiple_of(i * 8, 8), 8)
        for s in range(GP // SCAN_STRIP):
            sl = pl.ds(s * SCAN_STRIP, SCAN_STRIP)
            xr = inr_ref[rows, sl]
            xi = ini_ref[rows, sl]
            for n, k in enumerate((1, 2, 4)):
                shift = (8 - k) if reverse else k
                sr = pltpu.roll(xr, shift, 0)
                si = pltpu.roll(xi, shift, 0)
                mr = tab_ref[2 * n, :, sl]
                mi = tab_ref[2 * n + 1, :, sl]
                xr, xi = xr + mr * sr - mi * si, xi + mr * si + mi * sr
            qr = tab_ref[6, :, sl]
            qi = tab_ref[7, :, sl]
            cr = cr_ref[:, sl]
            ci = ci_ref[:, sl]
            xr, xi = xr + qr * cr - qi * ci, xi + qr * ci + qi * cr
            outr_ref[rows, sl] = xr
            outi_ref[rows, sl] = xi
            edge = 0 if reverse else 7
            cr_ref[:, sl] = jnp.broadcast_to(xr[edge:edge + 1, :], (8, SCAN_STRIP))
            ci_ref[:, sl] = jnp.broadcast_to(xi[edge:edge + 1, :], (8, SCAN_STRIP))
        return 0

    lax.fori_loop(0, n_blocks, block, 0)


def _glu_pre(z, wg_ref):
    return sum(_mm(z[:, 128 * j:128 * (j + 1)], wg_ref[j]) for j in range(4))


def _ssm_fwd(u, wb, wc, tabs, dskip, grp_d, b_glu, grp_e, tc):
    l = u.shape[0]

    def body(u_ref, wb_ref, wc_ref, tab_ref, d_ref, wg_ref, bg_ref, wo_ref, xr_ref, xi_ref, y_ref, ys_ref,
             bur, bui, cr, ci):
        @pl.when(pl.program_id(0) == 0)
        def _():
            cr[...] = jnp.zeros_like(cr)
            ci[...] = jnp.zeros_like(ci)

        uv = u_ref[...]
        ub = uv.astype(BF16)
        for j in range(N_STRIPS):
            uj = ub[:, STRIP_CH * j:STRIP_CH * (j + 1)]
            states = slice(STRIP_ST * j, STRIP_ST * (j + 1))
            bur[:, states] = _mm(uj, _strip(wb_ref, j, 0))
            bui[:, states] = _mm(uj, _strip(wb_ref, j, 1))
        _scan_chunk(bur, bui, bur, bui, cr, ci, tab_ref, tc, False)
        xr_ref[...] = bur[...].astype(BF16)
        xi_ref[...] = bui[...].astype(BF16)
        y = jnp.concatenate(
            [_mm_nt(xr_ref[:, STRIP_ST * j:STRIP_ST * (j + 1)], _strip(wc_ref, j, 0))
             + _mm_nt(xi_ref[:, STRIP_ST * j:STRIP_ST * (j + 1)], _strip(wc_ref, j, 1)) for j in range(N_STRIPS)],
            axis=-1) + d_ref[...] * uv
        y_ref[...] = y
        z = _gelu(y)
        z2 = z * _sigmoid(_glu_pre(z, wg_ref) + bg_ref[...])
        for s in range(4):
            ys_ref[:, 256 * s:256 * (s + 1)] = _mm(z2, wo_ref[s]).astype(BF16)

    return pl.pallas_call(
        body, name="ssm_fwd", grid=(l // tc,),
        in_specs=[_rows(tc, 512), _resident((512, 2 * GP)), _resident((512, 2 * GP)), _resident((8, 8, GP)),
                  _resident((1, 512)), _member_block("w_glu"), _resident((1, 512)), _member_block("w_o_ssm")],
        out_specs=[_rows(tc, GP), _rows(tc, GP), _rows(tc, 512), _rows(tc, D_MODEL)],
        out_shape=[jax.ShapeDtypeStruct((l, GP), BF16), jax.ShapeDtypeStruct((l, GP), BF16),
                   jax.ShapeDtypeStruct((l, 512), F32), jax.ShapeDtypeStruct((l, D_MODEL), BF16)],
        scratch_shapes=[pltpu.VMEM((tc, GP), F32), pltpu.VMEM((tc, GP), F32), pltpu.VMEM((8, GP), F32),
                        pltpu.VMEM((8, GP), F32)],
        compiler_params=_cparams("arbitrary"),
    )(u, wb, wc, tabs, dskip, grp_d, b_glu, grp_e)


def _ssm_bwd(dys, y, u, xr, xi, wb, wc, tabs_rev, dskip, grp_d, b_glu, grp_e, tc):
    l = u.shape[0]
    nc = l // tc

    def body(dys_ref, y_ref, u_ref, xr_ref, xi_ref, wb_ref, wc_ref, tab_ref, d_ref, wg_ref, bg_ref, wo_ref,
             du_ref, a_ref, dy_ref, z_ref, z2_ref, dpre_ref, gb_ref, gd_ref, glr_ref, gli_ref,
             dxr, dxi, ar, ai, cr, ci):
        first = pl.program_id(0) == 0

        @pl.when(first)
        def _():
            cr[...] = jnp.zeros_like(cr)
            ci[...] = jnp.zeros_like(ci)

        yv = y_ref[...]
        uv = u_ref[...]
        dz2 = sum(_mm_nt(dys_ref[:, 256 * j:256 * (j + 1)], wo_ref[j]) for j in range(4))
        z = _gelu(yv)
        s = _sigmoid(_glu_pre(z, wg_ref) + bg_ref[...])
        dpre = dz2 * z * s * (1.0 - s)
        dpreb = dpre.astype(BF16)
        dz = dz2 * s + jnp.concatenate([_mm_nt(dpreb, wg_ref[j]) for j in range(4)], axis=-1)
        dy = dz * _gelu_grad(yv)
        z_ref[...] = z.astype(BF16)
        z2_ref[...] = (z * s).astype(BF16)
        dpre_ref[...] = dpre.astype(BF16)
        dy_ref[...] = dy.astype(BF16)
        _accumulate(gb_ref, _colsum(dpre), first)
        _accumulate(gd_ref, _colsum(dy * uv), first)

        dyb = dy.astype(BF16)
        for j in range(N_STRIPS):
            dyj = dyb[:, STRIP_CH * j:STRIP_CH * (j + 1)]
            dxr[:, STRIP_ST * j:STRIP_ST * (j + 1)] = _mm(dyj, _strip(wc_ref, j, 0))
            dxi[:, STRIP_ST * j:STRIP_ST * (j + 1)] = _mm(dyj, _strip(wc_ref, j, 1))
        ar[pl.ds(tc, 8), :] = cr[...]
        ai[pl.ds(tc, 8), :] = ci[...]
        _scan_chunk(dxr, dxi, ar, ai, cr, ci, tab_ref, tc, True)
        a_ref[:, 0:GP] = ar[pl.ds(0, tc), :].astype(BF16)
        a_ref[:, GP:2 * GP] = ai[pl.ds(0, tc), :].astype(BF16)
        du_states = jnp.concatenate(
            [_mm_nt(a_ref[:, STRIP_ST * j:STRIP_ST * (j + 1)], _strip(wb_ref, j, 0))
             + _mm_nt(a_ref[:, GP + STRIP_ST * j:GP + STRIP_ST * (j + 1)], _strip(wb_ref, j, 1)) for j in range(N_STRIPS)],
            axis=-1)
        du_ref[...] = (dy * d_ref[...] + du_states).astype(BF16)
        anr = ar[pl.ds(1, tc), :]
        ani = ai[pl.ds(1, tc), :]
        xrv = xr_ref[...].astype(F32)
        xiv = xi_ref[...].astype(F32)
        _accumulate(glr_ref, _colsum(anr * xrv + ani * xiv), first)
        _accumulate(gli_ref, _colsum(ani * xrv - anr * xiv), first)

    rev = lambda w: pl.BlockSpec((tc, w), lambda i: (nc - 1 - i, 0))
    acc = lambda w: pl.BlockSpec((1, w), lambda i: (0, 0))
    bf = jax.ShapeDtypeStruct((l, 512), BF16)
    return pl.pallas_call(
        body, name="ssm_bwd", grid=(nc,),
        in_specs=[rev(D_MODEL), rev(512), rev(512), rev(GP), rev(GP), _resident((512, 2 * GP)), _resident((512, 2 * GP)),
                  _resident((8, 8, GP)), _resident((1, 512)), _member_block("w_glu"), _resident((1, 512)),
                  _member_block("w_o_ssm")],
        out_specs=[rev(512), rev(2 * GP), rev(512), rev(512), rev(512), rev(512), acc(512), acc(512), acc(GP), acc(GP)],
        out_shape=[bf, jax.ShapeDtypeStruct((l, 2 * GP), BF16), bf, bf, bf, bf,
                   jax.ShapeDtypeStruct((1, 512), F32), jax.ShapeDtypeStruct((1, 512), F32),
                   jax.ShapeDtypeStruct((1, GP), F32), jax.ShapeDtypeStruct((1, GP), F32)],
        scratch_shapes=[pltpu.VMEM((tc, GP), F32), pltpu.VMEM((tc, GP), F32), pltpu.VMEM((tc + 8, GP), F32),
                        pltpu.VMEM((tc + 8, GP), F32), pltpu.VMEM((8, GP), F32), pltpu.VMEM((8, GP), F32)],
        compiler_params=_cparams("arbitrary"),
    )(dys, y, u, xr, xi, wb, wc, tabs_rev, dskip, grp_d, b_glu, grp_e)


def _swap_halves(b):
    lane = lax.broadcasted_iota(jnp.int32, b.shape, 1)
    return jnp.where(lane < 32, pltpu.roll(b, 96, 1), pltpu.roll(b, 32, 1))


def _rope_tables(pos_ref, invf_ref, sgn_ref):
    ang = pos_ref[...].astype(F32) * invf_ref[...]
    return jnp.cos(ang), jnp.sin(ang) * sgn_ref[...]


def _mla_pre_fwd(lat, pos, invf, sgn, gqa, gkva, gq, gk, w_qb_p, w_kvb, t):
    l = lat.shape[0]

    def body(lat_ref, pos_ref, invf_ref, sgn_ref, gqa_ref, gkva_ref, gq_ref, gk_ref, wq_ref, wkv_ref, q_ref, k_ref, v_ref):
        cs, sn = _rope_tables(pos_ref, invf_ref, sgn_ref)
        ql = _rms_fwd(lat_ref[:, 0:Q_LORA], gqa_ref[...], Q_LORA)
        ckn = _rms_fwd(lat_ref[:, Q_LORA:Q_LORA + KV_LORA], gkva_ref[...], KV_LORA)
        kpe = lat_ref[:, 640:768]
        q0 = _mm(ql, wq_ref[...])
        cknb = ckn.astype(BF16)
        kv = jnp.concatenate([_mm(cknb, wkv_ref[s]) for s in range(4)], axis=-1)
        for h in range(N_HEADS):
            q1 = _rms_fwd(q0[:, HEAD_PAD * h:HEAD_PAD * (h + 1)], gq_ref[...], QK_HEAD)
            b = q1[:, 128:256]
            q_ref[h, :, 0:128] = (q1[:, 0:128] * ATT_SCALE).astype(BF16)
            q_ref[h, :, 128:256] = ((b * cs + _swap_halves(b) * sn) * ATT_SCALE).astype(BF16)
            k0 = jnp.concatenate([kv[:, 256 * h:256 * h + 128], kpe], axis=-1)
            k1 = _rms_fwd(k0, gk_ref[...], QK_HEAD)
            b = k1[:, 128:256]
            k_ref[h, :, 0:128] = k1[:, 0:128].astype(BF16)
            k_ref[h, :, 128:256] = (b * cs + _swap_halves(b) * sn).astype(BF16)
            v_ref[h] = kv[:, 256 * h + 128:256 * h + 256].astype(BF16)

    heads = lambda w: pl.BlockSpec((N_HEADS, t, w), lambda i: (0, i, 0))
    return pl.pallas_call(
        body, name="mla_pre_fwd", grid=(l // t,),
        in_specs=[_rows(t, LAT_W), _rows(t, 1), _resident((1, 128)), _resident((1, 128)), _resident((1, Q_LORA)),
                  _resident((1, KV_LORA)), _resident((1, HEAD_PAD)), _resident((1, HEAD_PAD)),
                  _resident((Q_LORA, N_HEADS * HEAD_PAD)), _member_block("w_kv_b")],
        out_specs=[heads(HEAD_PAD), heads(HEAD_PAD), heads(V_HEAD)],
        out_shape=[jax.ShapeDtypeStruct((N_HEADS, l, HEAD_PAD), BF16), jax.ShapeDtypeStruct((N_HEADS, l, HEAD_PAD), BF16),
                   jax.ShapeDtypeStruct((N_HEADS, l, V_HEAD), BF16)],
        compiler_params=_cparams("parallel"),
    )(lat, pos, invf, sgn, gqa, gkva, gq, gk, w_qb_p, w_kvb)


def _mla_pre_bwd(lat, pos, invf, sgn, gqa, gkva, gq, gk, w_qb_p, w_kvb, dq, dk, dv, t, token):
    l = lat.shape[0]

    def body(lat_ref, pos_ref, invf_ref, sgn_ref, gqa_ref, gkva_ref, gq_ref, gk_ref, wq_ref, wkv_ref, dq_ref, dk_ref, dv_ref,
             token_ref, dlat_ref, ql_ref, dq0_ref, ckn_ref, dkv_ref, ggqa_ref, ggkva_ref, ggq_ref, ggk_ref):
        first = pl.program_id(0) == 0
        cs, sn = _rope_tables(pos_ref, invf_ref, sgn_ref)
        q_lat = lat_ref[:, 0:Q_LORA]
        c_kv = lat_ref[:, Q_LORA:Q_LORA + KV_LORA]
        kpe = lat_ref[:, 640:768]
        ql = _rms_fwd(q_lat, gqa_ref[...], Q_LORA)
        ckn = _rms_fwd(c_kv, gkva_ref[...], KV_LORA)
        ql_ref[...] = ql.astype(BF16)
        ckn_ref[...] = ckn.astype(BF16)
        q0 = _mm(ql, wq_ref[...])
        cknb = ckn.astype(BF16)
        kv = jnp.concatenate([_mm(cknb, wkv_ref[s]) for s in range(4)], axis=-1)
        dkpe = jnp.zeros_like(kpe)
        ggq = jnp.zeros((1, HEAD_PAD), F32)
        ggk = jnp.zeros((1, HEAD_PAD), F32)

        def unrope(d):
            b = d[:, 128:256]
            return jnp.concatenate([d[:, 0:128], b * cs + _swap_halves(b * sn)], axis=-1)

        for h in range(N_HEADS):
            dq1 = unrope(dq_ref[h] * ATT_SCALE)
            dq0h, gq_rows = _rms_bwd(q0[:, HEAD_PAD * h:HEAD_PAD * (h + 1)], gq_ref[...], dq1, QK_HEAD)
            ggq = ggq + _colsum(gq_rows)
            dq0_ref[:, HEAD_PAD * h:HEAD_PAD * (h + 1)] = dq0h.astype(BF16)
            k0 = jnp.concatenate([kv[:, 256 * h:256 * h + 128], kpe], axis=-1)
            dk0, gk_rows = _rms_bwd(k0, gk_ref[...], unrope(dk_ref[h]), QK_HEAD)
            ggk = ggk + _colsum(gk_rows)
            dkpe = dkpe + dk0[:, 128:256]
            dkv_ref[:, 256 * h:256 * h + 128] = dk0[:, 0:128].astype(BF16)
            dkv_ref[:, 256 * h + 128:256 * h + 256] = dv_ref[h].astype(BF16)
        dql = _mm_nt(dq0_ref[...], wq_ref[...])
        dckn = sum(_mm_nt(dkv_ref[:, 512 * s:512 * (s + 1)], wkv_ref[s]) for s in range(4))
        dq_lat, gqa_rows = _rms_bwd(q_lat, gqa_ref[...], dql, Q_LORA)
        dc_kv, gkva_rows = _rms_bwd(c_kv, gkva_ref[...], dckn, KV_LORA)
        dlat_ref[:, 0:Q_LORA] = dq_lat.astype(BF16)
        dlat_ref[:, Q_LORA:Q_LORA + KV_LORA] = dc_kv.astype(BF16)
        dlat_ref[:, 640:768] = dkpe.astype(BF16)
        _accumulate(ggqa_ref, _colsum(gqa_rows), first)
        _accumulate(ggkva_ref, _colsum(gkva_rows), first)
        _accumulate(ggq_ref, ggq, first)
        _accumulate(ggk_ref, ggk, first)

    heads = lambda w: pl.BlockSpec((N_HEADS, t, w), lambda i: (0, i, 0))
    acc = lambda w: pl.BlockSpec((1, w), lambda i: (0, 0))
    return pl.pallas_call(
        body, name="mla_pre_bwd", grid=(l // t,),
        in_specs=[_rows(t, LAT_W), _rows(t, 1), _resident((1, 128)), _resident((1, 128)), _resident((1, Q_LORA)),
                  _resident((1, KV_LORA)), _resident((1, HEAD_PAD)), _resident((1, HEAD_PAD)),
                  _resident((Q_LORA, N_HEADS * HEAD_PAD)), _member_block("w_kv_b"),
                  heads(HEAD_PAD), heads(HEAD_PAD), heads(V_HEAD), ANY],
        out_specs=[_rows(t, LAT_W), _rows(t, Q_LORA), _rows(t, N_HEADS * HEAD_PAD), _rows(t, KV_LORA), _rows(t, N_HEADS * 256),
                   acc(Q_LORA), acc(KV_LORA), acc(HEAD_PAD), acc(HEAD_PAD)],
        out_shape=[jax.ShapeDtypeStruct((l, LAT_W), BF16), jax.ShapeDtypeStruct((l, Q_LORA), BF16),
                   jax.ShapeDtypeStruct((l, N_HEADS * HEAD_PAD), BF16), jax.ShapeDtypeStruct((l, KV_LORA), BF16),
                   jax.ShapeDtypeStruct((l, N_HEADS * 256), BF16), jax.ShapeDtypeStruct((1, Q_LORA), F32),
                   jax.ShapeDtypeStruct((1, KV_LORA), F32), jax.ShapeDtypeStruct((1, HEAD_PAD), F32),
                   jax.ShapeDtypeStruct((1, HEAD_PAD), F32)],
        compiler_params=_cparams("arbitrary"),
    )(lat, pos, invf, sgn, gqa, gkva, gq, gk, w_qb_p, w_kvb, dq, dk, dv, token)


def _causal(s, transposed):
    row = lax.broadcasted_iota(jnp.int32, s.shape, 0)
    col = lax.broadcasted_iota(jnp.int32, s.shape, 1)
    keep = (row <= col) if transposed else (col <= row)
    return jnp.where(keep, s, -jnp.inf)


def _as_row(col):
    n = col.shape[0]
    row = lax.broadcasted_iota(jnp.int32, (n, n), 0)
    lane = lax.broadcasted_iota(jnp.int32, (n, n), 1)
    return jnp.sum(jnp.where(row == lane, col, 0.0), axis=0, keepdims=True)


def _attn_fwd(q, k, v, tq):
    l = q.shape[1]

    hb = 2

    def body(q_ref, k_ref, v_ref, o_ref, lse_ref):
        qi = pl.program_id(1)
        qs = [q_ref[a] for a in range(hb)]

        def step(kb, carry, masked):
            rows = pl.ds(pl.multiple_of(kb * tq, tq), tq)
            out = []
            for a, (m, den, acc) in enumerate(carry):
                s = _mm_nt(qs[a], k_ref[a, rows, :])
                if masked:
                    s = _causal(s, False)
                m_new = jnp.maximum(m, jnp.max(s, axis=-1, keepdims=True))
                alpha = jnp.exp(m - m_new)
                p = jnp.exp(s - m_new)
                den = alpha * den + jnp.sum(p, axis=-1, keepdims=True)
                acc = alpha * acc + _mm(p, v_ref[a, rows, :])
                out.append((m_new, den, acc))
            return tuple(out)

        init = tuple((jnp.full((tq, 1), -jnp.inf, F32), jnp.zeros((tq, 1), F32), jnp.zeros((tq, V_HEAD), F32))
                     for _ in range(hb))
        carry = lax.fori_loop(0, qi, lambda kb, c: step(kb, c, False), init)
        for a, (m, den, acc) in enumerate(step(qi, carry, True)):
            o_ref[:, V_HEAD * a:V_HEAD * (a + 1)] = acc / den
            lse_ref[a, 0] = _as_row(m + jnp.log(den))

    return pl.pallas_call(
        body, name="attn_fwd", grid=(N_HEADS // hb, l // tq),
        in_specs=[pl.BlockSpec((hb, tq, HEAD_PAD), lambda h, i: (h, i, 0)), pl.BlockSpec((hb, l, HEAD_PAD), lambda h, i: (h, 0, 0)),
                  pl.BlockSpec((hb, l, V_HEAD), lambda h, i: (h, 0, 0))],
        out_specs=[pl.BlockSpec((tq, hb * V_HEAD), lambda h, i: (i, h)), pl.BlockSpec((hb, 1, 1, tq), lambda h, i: (h, i, 0, 0))],
        out_shape=[jax.ShapeDtypeStruct((l, N_HEADS * V_HEAD), F32), jax.ShapeDtypeStruct((N_HEADS, l // tq, 1, tq), F32)],
        compiler_params=_cparams("parallel", "arbitrary"),
    )(q, k, v)


def _attn_bwd(q, k, v, o, do, lse_t, tq, token):
    l = q.shape[1]
    nq = l // tq

    hb = 1

    def body(q_ref, k_ref, v_ref, o_ref, do_ref, lse_ref, token_ref, dq_ref, dk_ref, dv_ref):
        ki = pl.program_id(1)

        @pl.when(ki == 0)
        def _():
            dq_ref[...] = jnp.zeros_like(dq_ref)

        kblks = [k_ref[a] for a in range(hb)]
        vblks = [v_ref[a] for a in range(hb)]
        ones = jnp.ones((8, V_HEAD), BF16)

        def step(qb, carry, masked):
            rows = pl.ds(pl.multiple_of(qb * tq, tq), tq)
            out = []
            for a, (dk, dv) in enumerate(carry):
                cols = slice(V_HEAD * a, V_HEAD * (a + 1))
                qblk = q_ref[a, rows, :]
                dov = do_ref[rows, cols]
                dob = dov.astype(BF16)
                delta = sum(_mm_nt(ones, part) for part in _three_bf16(dov * o_ref[rows, cols]))[0:1, :]
                st = _mm_nt(kblks[a], qblk)
                if masked:
                    st = _causal(st, True)
                pt = jnp.exp(st - lse_ref[a, qb])
                dv = dv + _mm(pt, dob)
                dst = (pt * (_mm_nt(vblks[a], dob) - delta)).astype(BF16)
                dk = dk + _mm(dst, qblk)
                dq_ref[a, rows, :] += _mm_tn(dst, kblks[a])
                out.append((dk, dv))
            return tuple(out)

        init = tuple((jnp.zeros((tq, HEAD_PAD), F32), jnp.zeros((tq, V_HEAD), F32)) for _ in range(hb))
        carry = lax.fori_loop(ki + 1, nq, lambda qb, c: step(qb, c, False), step(ki, init, True))
        for a, (dk, dv) in enumerate(carry):
            dk_ref[a] = dk
            dv_ref[a] = dv

    return pl.pallas_call(
        body, name="attn_bwd", grid=(N_HEADS // hb, nq),
        in_specs=[pl.BlockSpec((hb, l, HEAD_PAD), lambda h, i: (h, 0, 0)), pl.BlockSpec((hb, tq, HEAD_PAD), lambda h, i: (h, i, 0)),
                  pl.BlockSpec((hb, tq, V_HEAD), lambda h, i: (h, i, 0)), pl.BlockSpec((l, hb * V_HEAD), lambda h, i: (0, h)),
                  pl.BlockSpec((l, hb * V_HEAD), lambda h, i: (0, h)), pl.BlockSpec((hb, nq, 1, tq), lambda h, i: (h, 0, 0, 0)), ANY],
        out_specs=[pl.BlockSpec((hb, l, HEAD_PAD), lambda h, i: (h, 0, 0)), pl.BlockSpec((hb, tq, HEAD_PAD), lambda h, i: (h, i, 0)),
                   pl.BlockSpec((hb, tq, V_HEAD), lambda h, i: (h, i, 0))],
        out_shape=[jax.ShapeDtypeStruct((N_HEADS, l, HEAD_PAD), F32), jax.ShapeDtypeStruct((N_HEADS, l, HEAD_PAD), F32),
                   jax.ShapeDtypeStruct((N_HEADS, l, V_HEAD), F32)],
        compiler_params=_cparams("parallel", "arbitrary"),
    )(q, k, v, o, do, lse_t, token)


def _row_shards_mm(a, w_ref):
    a = a.astype(BF16)
    return sum(_mm(a[:, 256 * j:256 * (j + 1)], w_ref[j]) for j in range(4))


def _row_shards_mm_nt(a, w_ref):
    a = a.astype(BF16)
    return jnp.concatenate([_mm_nt(a, w_ref[j]) for j in range(4)], axis=-1)


def _merge_fwd(attn, y_ssm, gs, gm, x, grp_a, t):
    l = x.shape[0]

    def body(attn_ref, ys_ref, gs_ref, gm_ref, x_ref, wo_ref, wout_ref, ym_ref, mixed_ref, h_ref):
        y_mla = _row_shards_mm(attn_ref[...], wo_ref)
        ym_ref[...] = y_mla.astype(BF16)
        mixed = (_sigmoid(gs_ref[...].astype(F32)) * ys_ref[...].astype(F32)
                 + _sigmoid(gm_ref[...].astype(F32)) * y_mla).astype(BF16)
        mixed_ref[...] = mixed
        h_ref[...] = x_ref[...] + _row_shards_mm(mixed, wout_ref)

    r = lambda: _rows(t, D_MODEL)
    return pl.pallas_call(
        body, name="merge_fwd", grid=(l // t,),
        in_specs=[r(), r(), r(), r(), r(), _member_block("w_o_mla"), _member_block("w_out")],
        out_specs=[r(), r(), r()],
        out_shape=[jax.ShapeDtypeStruct((l, D_MODEL), BF16), jax.ShapeDtypeStruct((l, D_MODEL), BF16),
                   jax.ShapeDtypeStruct((l, D_MODEL), F32)],
        compiler_params=_cparams("parallel"),
    )(attn, y_ssm, gs, gm, x, grp_a, grp_a)


def _merge_bwd(dh, y_ssm, y_mla, gs, gm, grp_a, t):
    l = dh.shape[0]

    def body(dh_ref, ys_ref, ym_ref, gs_ref, gm_ref, wo_ref, wout_ref, dys_ref, dym_ref, dgs_ref, dgm_ref, dattn_ref):
        dmixed = _row_shards_mm_nt(dh_ref[...], wout_ref)
        sg = _sigmoid(gs_ref[...].astype(F32))
        sm = _sigmoid(gm_ref[...].astype(F32))
        dys_ref[...] = (dmixed * sg).astype(BF16)
        dgs_ref[...] = (dmixed * ys_ref[...].astype(F32) * sg * (1.0 - sg)).astype(BF16)
        dym = (dmixed * sm).astype(BF16)
        dym_ref[...] = dym
        dgm_ref[...] = (dmixed * ym_ref[...].astype(F32) * sm * (1.0 - sm)).astype(BF16)
        dattn_ref[...] = _row_shards_mm_nt(dym, wo_ref)

    r = lambda: _rows(t, D_MODEL)
    bf = jax.ShapeDtypeStruct((l, D_MODEL), BF16)
    return pl.pallas_call(
        body, name="merge_bwd", grid=(l // t,),
        in_specs=[r(), r(), r(), r(), r(), _member_block("w_o_mla"), _member_block("w_out")],
        out_specs=[r(), r(), r(), r(), r()],
        out_shape=[bf, bf, bf, bf, jax.ShapeDtypeStruct((l, D_MODEL), F32)],
        compiler_params=_cparams("parallel"),
    )(dh, y_ssm, y_mla, gs, gm, grp_a, grp_a)


def _mlp_fwd_bwd(h, tgt, g2, grp_a, t):
    l = h.shape[0]

    def body(h_ref, tgt_ref, g_ref, wu_ref, wd_ref, dh_ref, hn_ref, da_ref, hid_ref, dout_ref, loss_ref, dg_ref):
        first = pl.program_id(0) == 0
        hv = h_ref[...]
        g = g_ref[...]
        hn = _rms_fwd(hv, g, D_MODEL).astype(BF16)
        hn_ref[...] = hn
        out = hv
        relus = []
        for s in range(4):
            cols = slice(1024 * s, 1024 * (s + 1))
            relu = jnp.maximum(_mm(hn, wu_ref[s]), 0.0)
            relus.append(relu)
            hid = (relu * relu).astype(BF16)
            hid_ref[:, cols] = hid
            out = out + _mm(hid, wd_ref[s])
        err = out - tgt_ref[...]
        _accumulate(loss_ref, jnp.full((8, 128), jnp.sum(err * err) * (0.5 / D_MODEL), F32), first)
        dout = err * (1.0 / D_MODEL)
        doutb = dout.astype(BF16)
        dout_ref[...] = doutb
        dhn = jnp.zeros_like(hv)
        for s in range(4):
            da = (_mm_nt(doutb, wd_ref[s]) * (2.0 * relus[s])).astype(BF16)
            da_ref[:, 1024 * s:1024 * (s + 1)] = da
            dhn = dhn + _mm_nt(da, wu_ref[s])
        dx, dg_rows = _rms_bwd(hv, g, dhn, D_MODEL)
        dh_ref[...] = dout + dx
        _accumulate(dg_ref, _colsum(dg_rows), first)

    r = lambda w: _rows(t, w)
    return pl.pallas_call(
        body, name="mlp_fwd_bwd", grid=(l // t,),
        in_specs=[r(D_MODEL), r(D_MODEL), _resident((1, D_MODEL)), _member_block("w_up"), _member_block("w_down")],
        out_specs=[r(D_MODEL), r(D_MODEL), r(D_FF), r(D_FF), r(D_MODEL), pl.BlockSpec((8, 128), lambda i: (0, 0)),
                   pl.BlockSpec((1, D_MODEL), lambda i: (0, 0))],
        out_shape=[jax.ShapeDtypeStruct((l, D_MODEL), F32), jax.ShapeDtypeStruct((l, D_MODEL), BF16),
                   jax.ShapeDtypeStruct((l, D_FF), BF16), jax.ShapeDtypeStruct((l, D_FF), BF16),
                   jax.ShapeDtypeStruct((l, D_MODEL), BF16), jax.ShapeDtypeStruct((8, 128), F32),
                   jax.ShapeDtypeStruct((1, D_MODEL), F32)],
        compiler_params=_cparams("arbitrary"),
    )(h, tgt, g2, grp_a, grp_a)


def _wgrad(a, b, name):
    l, m = a.shape
    n = b.shape[1]
    bm = m if m <= 512 else 512
    bl = min(l, 2048 if n <= 1024 else 1024)

    def body(a_ref, b_ref, o_ref):
        _accumulate(o_ref, _mm_tn(a_ref[...], b_ref[...]), pl.program_id(1) == 0)

    return pl.pallas_call(
        body, name=name, grid=(m // bm, l // bl),
        in_specs=[pl.BlockSpec((bl, bm), lambda i, j: (j, i)), pl.BlockSpec((bl, n), lambda i, j: (j, 0))],
        out_specs=pl.BlockSpec((bm, n), lambda i, j: (i, 0)),
        out_shape=jax.ShapeDtypeStruct((m, n), F32),
        compiler_params=_cparams("parallel", "arbitrary"),
    )(a, b)


def _wgrad_into(a, b, member, cut, dest=None):
    group, off, rs, cs = _place_in_group(member)
    l = a.shape[0]
    bm = min(rs, 512)
    bl = min(l, 2048)
    nb = rs // bm
    if cut == "row":
        a_spec = pl.BlockSpec((bl, bm), lambda j, i, k: (k, j * nb + i))
        b_spec = pl.BlockSpec((bl, cs), lambda j, i, k: (k, 0))
    else:
        a_spec = pl.BlockSpec((bl, bm), lambda j, i, k: (k, i))
        b_spec = pl.BlockSpec((bl, cs), lambda j, i, k: (k, j))

    def body(a_ref, b_ref, *rest):
        o_ref = rest[-1]
        part = _mm_tn(a_ref[...], b_ref[...])

        @pl.when(pl.program_id(2) == 0)
        def _():
            o_ref[0] = part

        @pl.when(pl.program_id(2) != 0)
        def _():
            o_ref[0] += part

    operands, in_specs, aliases = [a, b], [a_spec, b_spec], {}
    if dest is not None:
        operands.append(dest)
        in_specs.append(ANY)
        aliases = {2: 0}
    return pl.pallas_call(
        body, name="wgrad_" + member, grid=(4, nb, l // bl), in_specs=in_specs,
        out_specs=pl.BlockSpec((1, bm, cs), lambda j, i, k: (j, off // bm + i, 0)),
        out_shape=jax.ShapeDtypeStruct((4, _group_rows(group), cs), F32), input_output_aliases=aliases,
        compiler_params=_cparams("parallel", "parallel", "arbitrary"),
    )(*operands)


def _adamw(w, g, m, v, name, g_off, token):
    r, c = w.shape
    br = r
    for cand in (256, 128, 64, 32, 16, 8):
        if r % cand == 0 and g_off % cand == 0:
            br = cand
            break

    def body(w_ref, g_ref, m_ref, v_ref, token_ref, go_ref, d_ref, nm_ref, nv_ref):
        gv = g_ref[...]
        go_ref[...] = gv
        nm = ADAM_B1 * m_ref[...] + (1.0 - ADAM_B1) * gv
        nv = ADAM_B2 * v_ref[...] + (1.0 - ADAM_B2) * (gv * gv)
        m_hat = nm / (1.0 - ADAM_B1 ** ADAM_STEP)
        v_hat = nv / (1.0 - ADAM_B2 ** ADAM_STEP)
        d_ref[...] = -ADAM_LR * (m_hat / (jnp.sqrt(v_hat) + ADAM_EPS) + ADAM_WD * w_ref[...])
        nm_ref[...] = nm
        nv_ref[...] = nv

    spec = lambda: pl.BlockSpec((br, c), lambda i: (i, 0))
    g_spec = pl.BlockSpec((br, c), lambda i: (g_off // br + i, 0))
    shp = jax.ShapeDtypeStruct((r, c), F32)
    return pl.pallas_call(
        body, name=name, grid=(r // br,), in_specs=[spec(), g_spec, spec(), spec(), ANY],
        out_specs=[spec(), spec(), spec(), spec()], out_shape=[shp, shp, shp, shp], compiler_params=_cparams("parallel"),
    )(w, g, m, v, token)


def _place():
    return lax.axis_index("x"), lax.axis_index("y"), lax.axis_index("c")


def _other_chips(x, y):
    return [(1 - x, y), (x, 1 - y), (1 - x, 1 - y)]


ANY = pl.BlockSpec(memory_space=pl.ANY)


def _gather_weights(bufs):
    n = len(bufs)

    def body(*refs):
        outs, send_sems, recv_sems = refs[n:2 * n], refs[2 * n], refs[2 * n + 1]
        x, y, c = _place()
        chips = _other_chips(x, y)

        def part(g, px, py, pc):
            half = outs[g].shape[1] // 2
            return outs[g].at[2 * px + py, pl.ds(pl.multiple_of(pc * half, 16), half), :]

        def copy(k, src, dst, to):
            return pltpu.make_async_remote_copy(src_ref=src, dst_ref=dst, send_sem=send_sems.at[k], recv_sem=recv_sems.at[k],
                                                device_id=to, device_id_type=MESH)

        first = [copy(6 * g + j, part(g, x, y, c), part(g, x, y, c), (*chip, c)) for g in range(n) for j, chip in enumerate(chips)]
        for cp in first:
            cp.start()
        passed = []
        for g in range(n):
            for j, chip in enumerate(chips):
                landed = part(g, *chip, c)
                copy(6 * g + j, landed, landed, (x, y, c)).wait_recv()
                passed.append(copy(6 * g + 3 + j, landed, landed, (x, y, 1 - c)))
                passed[-1].start()
        for g in range(n):
            for j, chip in enumerate(chips):
                other = part(g, *chip, 1 - c)
                copy(6 * g + 3 + j, other, other, (x, y, c)).wait_recv()
        for cp in first + passed:
            cp.wait_send()

    return pl.pallas_call(
        body, name="gather_weights", in_specs=[ANY] * n, out_specs=[ANY] * n,
        out_shape=[jax.ShapeDtypeStruct(b.shape, b.dtype) for b in bufs], input_output_aliases={g: g for g in range(n)},
        scratch_shapes=[pltpu.SemaphoreType.DMA((6 * n,)), pltpu.SemaphoreType.DMA((6 * n,))],
    )(*bufs)


def _cast_shards(shards, group, place, after=None):
    width, members = GROUPS[group]
    rows = _group_rows(group)
    extra = [] if after is None else [after]

    def body(place_ref, *refs):
        out = refs[-1]
        off = 0
        for ref, (_, r) in zip(refs[:-1], members):
            out[0, off:off + r, :] = ref[...].astype(BF16)
            off += r

    grid_spec = pltpu.PrefetchScalarGridSpec(
        num_scalar_prefetch=1, grid=(1,),
        in_specs=[pl.BlockSpec((r, width), lambda i, p: (0, 0)) for _, r in members] + [ANY] * len(extra),
        out_specs=pl.BlockSpec((1, rows, width), lambda i, p: (p[0], 0, 0)))
    return pl.pallas_call(
        body, name="cast_shards_" + group, grid_spec=grid_spec, out_shape=jax.ShapeDtypeStruct((4, rows, width), BF16),
        compiler_params=_cparams("arbitrary"),
    )(place, *[shards[name] for name, _ in members], *extra)


def _block_rows(h):
    return next(cand for cand in (256, 192, 128, 64, 32, 16) if h % cand == 0)


def _add_pair(buf, got, place, name):
    n, h, w = got.shape
    bh = _block_rows(h)
    nb = h // bh

    def body(place_ref, a_ref, b_ref, s_ref, sb_ref):
        s = a_ref[...] + b_ref[...]
        s_ref[...] = s
        sb_ref[...] = s.astype(BF16)

    spec = lambda: pl.BlockSpec((1, bh, w), lambda j, i, p: (j, i, 0))
    grid_spec = pltpu.PrefetchScalarGridSpec(
        num_scalar_prefetch=1, grid=(n, nb),
        in_specs=[pl.BlockSpec((1, bh, w), lambda j, i, p: (j, p[1] * nb + i, 0)), spec()], out_specs=[spec(), spec()])
    return pl.pallas_call(
        body, name=name, grid_spec=grid_spec,
        out_shape=[jax.ShapeDtypeStruct(got.shape, F32), jax.ShapeDtypeStruct(got.shape, BF16)],
        compiler_params=_cparams("parallel", "parallel"),
    )(place, buf, got)


def _add_received(pair, got, place, name):
    _, h, w = pair.shape
    bh = _block_rows(h)
    nb = h // bh

    def body(place_ref, own_ref, got_ref, o_ref):
        o_ref[...] = ((own_ref[0] + got_ref[0].astype(F32)) + got_ref[1].astype(F32)) + got_ref[2].astype(F32)

    grid_spec = pltpu.PrefetchScalarGridSpec(
        num_scalar_prefetch=1, grid=(nb,),
        in_specs=[pl.BlockSpec((1, bh, w), lambda i, p: (p[0], i, 0)), pl.BlockSpec((3, bh, w), lambda i, p: (0, i, 0))],
        out_specs=pl.BlockSpec((bh, w), lambda i, p: (p[1] * nb + i, 0)))
    return pl.pallas_call(
        body, name=name, grid_spec=grid_spec, out_shape=jax.ShapeDtypeStruct((2 * h, w), F32),
        compiler_params=_cparams("parallel"),
    )(place, pair, got)


def _swap_reduced_halves(bufs):
    n = len(bufs)

    def body(*refs):
        outs, send_sems, recv_sems = refs[n:2 * n], refs[2 * n], refs[2 * n + 1]
        x, y, c = _place()
        copies = []
        for g in range(n):
            half = outs[g].shape[0] // 2
            own = outs[g].at[pl.ds(pl.multiple_of(c * half, 8), half), :]
            copies.append(pltpu.make_async_remote_copy(src_ref=own, dst_ref=own, send_sem=send_sems.at[g],
                                                       recv_sem=recv_sems.at[g], device_id=(x, y, 1 - c), device_id_type=MESH))
        for cp in copies:
            cp.start()
        for g in range(n):
            half = outs[g].shape[0] // 2
            other = outs[g].at[pl.ds(pl.multiple_of((1 - c) * half, 8), half), :]
            pltpu.make_async_remote_copy(src_ref=other, dst_ref=other, send_sem=send_sems.at[g], recv_sem=recv_sems.at[g],
                                         device_id=(x, y, 1 - c), device_id_type=MESH).wait_recv()
        for cp in copies:
            cp.wait_send()

    return pl.pallas_call(
        body, name="swap_reduced_halves", in_specs=[ANY] * n, out_specs=[ANY] * n,
        out_shape=[jax.ShapeDtypeStruct(b.shape, b.dtype) for b in bufs], input_output_aliases={g: g for g in range(n)},
        scratch_shapes=[pltpu.SemaphoreType.DMA((n,)), pltpu.SemaphoreType.DMA((n,))],
    )(*bufs)


HBM = pl.BlockSpec(memory_space=pltpu.HBM)
SEM = pl.BlockSpec(memory_space=pltpu.SEMAPHORE)


def _copies_start(name, bufs, n_copies, plan, after=None):
    n = len(bufs)
    extra = [] if after is None else [after]

    def body(*refs):
        sems = refs[n + len(extra):n + len(extra) + 2 * n_copies]
        x, y, c = _place()
        for i, (src, dst, dev) in enumerate(plan(refs[:n], x, y, c)):
            pltpu.make_async_remote_copy(src_ref=src, dst_ref=dst, send_sem=sems[i], recv_sem=sems[n_copies + i],
                                         device_id=dev, device_id_type=MESH).start()
        token = refs[-1]
        token[...] = jnp.zeros_like(token)

    out = pl.pallas_call(
        body, name=name,
        out_shape=[pltpu.SemaphoreType.DMA(())] * (2 * n_copies) + [pltpu.HBM(b.shape, b.dtype) for b in bufs]
        + [jax.ShapeDtypeStruct((8, 128), F32)],
        in_specs=[HBM] * n + [ANY] * len(extra),
        out_specs=[SEM] * (2 * n_copies) + [HBM] * n + [pl.BlockSpec(memory_space=pltpu.VMEM)],
        input_output_aliases={i: 2 * n_copies + i for i in range(n)},
        compiler_params=pltpu.CompilerParams(has_side_effects=pltpu.SideEffectType.DATAFLOW_SIDE_EFFECTING),
    )(*[pltpu.with_memory_space_constraint(b, pltpu.HBM) for b in bufs], *extra)
    return list(out[:2 * n_copies]), list(out[2 * n_copies:-1]), out[-1]


def _copies_wait(name, bufs, sems, after, plan):
    n = len(bufs)
    k = len(sems) // 2

    def body(*refs):
        sem_refs = refs[n:n + 2 * k]
        x, y, c = _place()
        for i, (sent, landed, dev) in enumerate(plan(refs[:n], x, y, c)):
            cp = pltpu.make_async_remote_copy(src_ref=sent, dst_ref=landed, send_sem=sem_refs[i], recv_sem=sem_refs[k + i],
                                              device_id=dev, device_id_type=MESH)
            cp.wait_send()
            cp.wait_recv()

    return pl.pallas_call(
        body, name=name, out_shape=[pltpu.HBM(b.shape, b.dtype) for b in bufs],
        in_specs=[HBM] * n + [SEM] * (2 * k) + [ANY], out_specs=[HBM] * n, input_output_aliases={i: i for i in range(n)},
        compiler_params=pltpu.CompilerParams(has_side_effects=pltpu.SideEffectType.DATAFLOW_SIDE_EFFECTING),
    )(*bufs, *sems, after)


def _row_half(ref, which, axis):
    half = ref.shape[axis] // 2
    rows = pl.ds(pl.multiple_of(which * half, 8), half)
    return ref.at[rows, :] if axis == 0 else ref.at[:, rows, :]


class _SplitGather:
    def __init__(self, own, after):
        self.n = len(own)
        self.state = _copies_start("gather_start", own, 3 * self.n, self._sent, after)

    @staticmethod
    def _sent(refs, x, y, c):
        return [(w.at[2 * x + y], w.at[2 * x + y], (px, py, c)) for w in refs for px, py in _other_chips(x, y)]

    @staticmethod
    def _landed(refs, x, y, c):
        return [(w.at[2 * x + y], w.at[2 * px + py], (px, py, c)) for w in refs for px, py in _other_chips(x, y)]

    def token(self):
        return self.state[2]

    def wait(self, which, name, after):
        sems, bufs, _ = self.state
        k = 3 * self.n
        mine = [sems[3 * i + j] for i in which for j in range(3)] + [sems[k + 3 * i + j] for i in which for j in range(3)]
        return _copies_wait(name, [bufs[i] for i in which], mine, after, self._landed)


def _slot_half(ref, px, py, pc):
    half = ref.shape[1] // 2
    return ref.at[2 * px + py, pl.ds(pl.multiple_of(pc * half, 16), half), :]


class _SplitGatherHalves:
    def __init__(self, own):
        self.state = _copies_start("gather_b_start", [own], 3, self._sent)

    @staticmethod
    def _sent(refs, x, y, c):
        (w,) = refs
        return [(_slot_half(w, x, y, c), _slot_half(w, x, y, c), (px, py, c)) for px, py in _other_chips(x, y)]

    def token(self):
        return self.state[2]

    @staticmethod
    def _landed(refs, x, y, c):
        (w,) = refs
        return [(_slot_half(w, x, y, c), _slot_half(w, px, py, c), (px, py, c)) for px, py in _other_chips(x, y)]

    def finish(self, after):
        sems, bufs, _ = self.state
        (buf,) = _copies_wait("gather_b_wait", bufs, sems, after, self._landed)

        def body(buf_ref, out_ref, send_sems, recv_sems):
            x, y, c = _place()
            chips = _other_chips(x, y)

            def to_sibling(j, part, to):
                return pltpu.make_async_remote_copy(src_ref=part, dst_ref=part, send_sem=send_sems.at[j],
                                                    recv_sem=recv_sems.at[j], device_id=to, device_id_type=MESH)

            passed = [to_sibling(j, _slot_half(out_ref, *chip, c), (x, y, 1 - c)) for j, chip in enumerate(chips)]
            for cp in passed:
                cp.start()
            for j, chip in enumerate(chips):
                to_sibling(j, _slot_half(out_ref, *chip, 1 - c), (x, y, c)).wait_recv()
            for cp in passed:
                cp.wait_send()

        return pl.pallas_call(
            body, name="gather_b_pass", in_specs=[ANY], out_specs=ANY, out_shape=jax.ShapeDtypeStruct(buf.shape, buf.dtype),
            input_output_aliases={0: 0}, scratch_shapes=[pltpu.SemaphoreType.DMA((3,)), pltpu.SemaphoreType.DMA((3,))],
        )(buf)


class _SplitReduction:
    def __init__(self, tag, groups, place):
        self.tag, self.groups, self.place = tag, groups, place

    def start_pair(self, bufs):
        n = len(bufs)
        lands = [lax.empty((4, b.shape[1] // 2, b.shape[2]), F32) for b in bufs]
        plan = lambda refs, x, y, c: [(_row_half(refs[i], 1 - c, 1), refs[n + i], (x, y, 1 - c)) for i in range(n)]
        self._pair = (_copies_start("pair_%s_start" % self.tag, bufs + lands, n, plan), plan, n)
        return self._pair[0][2]

    def pair_done_start_scatter(self, after):
        (sems, bufs, _), plan, n = self._pair
        out = _copies_wait("pair_%s_wait" % self.tag, bufs, sems, after, plan)
        pairs = [_add_pair(out[i], out[n + i], self.place, "add_pair_" + g) for i, g in enumerate(self.groups)]
        self._pair_f32 = [p[0] for p in pairs]
        lands = [lax.empty((3,) + p[1].shape[1:], BF16) for p in pairs]
        plan = lambda refs, x, y, c: [(refs[i].at[2 * px + py], refs[n + i].at[j], (px, py, c))
                                      for i in range(n) for j, (px, py) in enumerate(_other_chips(x, y))]
        self._scatter = (_copies_start("scatter_%s_start" % self.tag, [p[1] for p in pairs] + lands, 3 * n, plan), plan, n)
        return self._scatter[0][2]

    def scatter_done(self, after):
        (sems, bufs, _), plan, n = self._scatter
        out = _copies_wait("scatter_%s_wait" % self.tag, bufs, sems, after, plan)
        return [_add_received(self._pair_f32[i], out[n + i], self.place, "add_received_" + g)
                for i, g in enumerate(self.groups)]

    def start_join(self, halves):
        n = len(halves)
        sent = lambda refs, x, y, c: [(_row_half(r, c, 0), _row_half(r, c, 0), (x, y, 1 - c)) for r in refs]
        landed = lambda refs, x, y, c: [(_row_half(r, c, 0), _row_half(r, 1 - c, 0), (x, y, 1 - c)) for r in refs]
        self._join = (_copies_start("join_%s_start" % self.tag, halves, n, sent), landed)
        return self._join[0][2]

    def join_done(self, after):
        (sems, bufs, _), landed = self._join
        return _copies_wait("join_%s_wait" % self.tag, bufs, sems, after, landed)


def _pair_sum_small(mine):
    rows, w = mine.shape

    def body(in_ref, out_ref, sibling, send_sem, recv_sem):
        x, y, c = _place()
        swap = pltpu.make_async_remote_copy(src_ref=in_ref, dst_ref=sibling, send_sem=send_sem, recv_sem=recv_sem,
                                            device_id=(x, y, 1 - c), device_id_type=MESH)
        swap.start()
        swap.wait()
        out_ref[...] = in_ref[...] + sibling[...]

    return pl.pallas_call(
        body, name="pair_sum_small", out_shape=jax.ShapeDtypeStruct((rows, w), F32),
        in_specs=[pl.BlockSpec(memory_space=pltpu.VMEM)], out_specs=pl.BlockSpec(memory_space=pltpu.VMEM),
        scratch_shapes=[pltpu.VMEM((rows, w), F32), pltpu.SemaphoreType.DMA, pltpu.SemaphoreType.DMA],
        compiler_params=pltpu.CompilerParams(vmem_limit_bytes=VMEM_LIMIT_V7X),
    )(mine)


class _SplitChipSum:
    def __init__(self, pair, place):
        self.place = place
        slots = lax.empty((4,) + pair.shape, F32)
        sent = lambda refs, x, y, c: [(refs[0], refs[1].at[2 * x + y], (px, py, c)) for px, py in _other_chips(x, y)]
        self.landed = lambda refs, x, y, c: [(refs[0], refs[1].at[2 * px + py], (px, py, c)) for px, py in _other_chips(x, y)]
        self.state = _copies_start("small_sum_start", [pair, slots], 3, sent)

    def token(self):
        return self.state[2]

    def done(self, after):
        sems, bufs, _ = self.state
        pair, slots = _copies_wait("small_sum_wait", bufs, sems, after, self.landed)
        rows, w = pair.shape

        def body(place_ref, pair_ref, slots_ref, out_ref):
            for j in range(4):
                own = place_ref[0] == j

                @pl.when(own)
                def _():
                    out_ref[...] = pair_ref[...] if j == 0 else out_ref[...] + pair_ref[...]

                @pl.when(jnp.logical_not(own))
                def _():
                    out_ref[...] = slots_ref[j] if j == 0 else out_ref[...] + slots_ref[j]

        grid_spec = pltpu.PrefetchScalarGridSpec(
            num_scalar_prefetch=1, grid=(1,),
            in_specs=[pl.BlockSpec((rows, w), lambda i, p: (0, 0)), pl.BlockSpec((4, rows, w), lambda i, p: (0, 0, 0))],
            out_specs=pl.BlockSpec((rows, w), lambda i, p: (0, 0)))
        return pl.pallas_call(
            body, name="small_sum_add", grid_spec=grid_spec, out_shape=jax.ShapeDtypeStruct((rows, w), F32),
            compiler_params=_cparams("arbitrary"),
        )(self.place, pair, slots)


def _join_column_shards(g):
    return jnp.transpose(g, (1, 0, 2)).reshape(g.shape[1], 4 * g.shape[2])


def _split_column_shards(w):
    r = w.shape[0]
    return jnp.transpose(w.reshape(r, 4, w.shape[1] // 4), (1, 0, 2))


def _small_rows(shape):
    return -(-int(np.prod(shape)) // 1024)


def _pack_small(vals):
    segs = []
    for name, shape in SMALL_WEIGHTS:
        flat = vals[name].reshape(-1)
        segs.append(jnp.pad(flat, (0, _small_rows(shape) * 1024 - flat.shape[0])))
    total = sum(s.shape[0] for s in segs) // 1024
    segs.append(jnp.zeros((-total % 8 * 1024,), F32))
    return jnp.concatenate(segs).reshape(-1, 1024)


def _unpack_small(packed):
    out, off = {}, 0
    for name, shape in SMALL_WEIGHTS:
        rows = _small_rows(shape)
        out[name] = packed[off:off + rows].reshape(-1)[:int(np.prod(shape))].reshape(shape)
        off += rows
    return out


W_IN_SHARD = D_IN // 4
W_IN_GAP = 1216


def _pad_w_in(g):
    cut = W_IN_GAP - W_IN_SHARD
    return jnp.concatenate([g[0], g[1][:, :cut], jnp.zeros((g.shape[1], D_IN_PAD - D_IN), g.dtype), g[1][:, cut:], g[2], g[3]],
                           axis=1)


def _unpad_w_in(g):
    skip = D_IN_PAD - D_IN
    second = jnp.concatenate([g[:, W_IN_SHARD:W_IN_GAP], g[:, W_IN_GAP + skip:2 * W_IN_SHARD + skip]], axis=1)
    return jnp.stack([g[:, :W_IN_SHARD], second, g[:, 2 * W_IN_SHARD + skip:3 * W_IN_SHARD + skip],
                      g[:, 3 * W_IN_SHARD + skip:]])


def _pad_heads(w):
    r = w.shape[0]
    return jnp.pad(w.reshape(r, N_HEADS, QK_HEAD), ((0, 0), (0, 0), (0, HEAD_PAD - QK_HEAD))).reshape(r, N_HEADS * HEAD_PAD)


def _unpad_heads(g):
    r = g.shape[0]
    return g.reshape(r, N_HEADS, HEAD_PAD)[:, :, :QK_HEAD].reshape(r, N_HEADS * QK_HEAD)


def _local_step(x, positions, tgt, grp_b, small, gather, red_a, red_rest):
    l = x.shape[0]
    t = min(l, 512)
    t_mlp = min(l, 256)
    tq = min(l, 1024)
    tc = min(l, 256)
    row = lambda v: v.reshape(1, -1).astype(F32)

    w_in_p = _pad_w_in(grp_b)
    g1, g2 = row(small["norm_mix"]), row(small["norm_mlp"])
    gqa, gkva = row(small["q_a_norm"]), row(small["kv_a_norm"])
    gq = jnp.pad(row(small["q_norm"]), ((0, 0), (0, HEAD_PAD - QK_HEAD)))
    gk = jnp.pad(row(small["k_norm"]), ((0, 0), (0, HEAD_PAD - QK_HEAD)))
    half = QK_ROPE // 2
    inv_freq = ROPE_THETA ** (-jnp.arange(half, dtype=F32) / half)
    invf = jnp.concatenate([inv_freq, inv_freq, jnp.zeros((64,), F32)]).reshape(1, 128)
    sgn = jnp.concatenate([-jnp.ones((half,), F32), jnp.ones((half,), F32), jnp.zeros((64,), F32)]).reshape(1, 128)
    pos = positions.reshape(l, 1)

    a_re, a_im = small["ssm_a_re"], small["ssm_a_im"]
    log_dt = small["ssm_log_dt"].reshape(SSM_GROUPS, 1)
    to_gcp = lambda b: jnp.transpose(b, (0, 2, 1)).reshape(SSM_WIDTH, SSM_STATE)
    from_gcp = lambda b: jnp.transpose(b.reshape(SSM_GROUPS, SSM_GROUP_CH, SSM_STATE), (0, 2, 1))
    b_re, b_im = to_gcp(small["ssm_b_re"]), to_gcp(small["ssm_b_im"])
    c_re, c_im = small["ssm_c_re"].reshape(SSM_WIDTH, SSM_STATE), small["ssm_c_im"].reshape(SSM_WIDTH, SSM_STATE)
    wb, wc, tabs_fwd, tabs_rev = _ssm_param_fwd(a_re, a_im, log_dt, b_re, b_im, c_re, c_im)
    dskip = row(small["ssm_d"])
    b_glu = row(small["b_glu"])

    u, lat, gs, gm = _in_proj_fwd(x, g1, w_in_p, t, gather.token())
    grp_c, grp_d, grp_e = gather.wait([0, 1, 2], "gather_cde_wait", u)
    w_qb_p = _pad_heads(_join_column_shards(grp_c))
    xr, xi, y, y_ssm = _ssm_fwd(u, wb, wc, tabs_fwd, dskip, grp_d, b_glu, grp_e, tc)
    q, k, v = _mla_pre_fwd(lat, pos, invf, sgn, gqa, gkva, gq, gk, w_qb_p, grp_d, t)
    attn, lse = _attn_fwd(q, k, v, tq)
    (grp_a,) = gather.wait([3], "gather_a_wait", attn)
    y_mla, mixed, h = _merge_fwd(attn, y_ssm, gs, gm, x, grp_a, t)
    dh, hn, da, hid, dout, loss_blk, g_norm_mlp = _mlp_fwd_bwd(h, tgt, g2, grp_a, t_mlp)

    ga = _wgrad_into(hn, da, "w_up", "col", _wgrad_into(hid, dout, "w_down", "row"))
    dys, dym, dgs, dgm, dattn = _merge_bwd(dh, y_ssm, y_mla, gs, gm, grp_a, t)
    ga = _wgrad_into(attn, dym, "w_o_mla", "row", _wgrad_into(mixed, dh, "w_out", "row", ga))

    dq, dk, dv = _attn_bwd(q, k, v, attn, dattn, lse, tq, red_a.start_pair([ga]))
    d_lat, ql, dq0, ckn, dkv, g_qa, g_kva, g_q, g_k = _mla_pre_bwd(lat, pos, invf, sgn, gqa, gkva, gq, gk, w_qb_p, grp_d,
                                                                    dq, dk, dv, t, red_a.pair_done_start_scatter(dk))
    gc = _split_column_shards(_unpad_heads(_wgrad(ql, dq0, "wgrad_q_b")))

    d_u, adj, dy, z, z2, dpre, g_b_glu, g_d, g_lr, g_li = _ssm_bwd(
        dys, y, u, xr, xi, wb, wc, tabs_rev, dskip, grp_d, b_glu, grp_e, tc)
    gd = _wgrad_into(z, dpre, "w_glu", "row", _wgrad_into(ckn, dkv, "w_kv_b", "col"))
    ge = _wgrad_into(z2, dys, "w_o_ssm", "col")
    grad_x, xn, dproj, g_norm_mix = _in_proj_bwd(x, g1, w_in_p, d_u, d_lat, dgs, dgm, dh, t)
    gb = _unpad_w_in(_wgrad(xn, dproj, "wgrad_in"))

    red_a.start_join(red_a.scatter_done(gb))
    g_wb = _wgrad_strips(u, adj, adj, "wgrad_ssm_b", 1, red_rest.start_pair([gb, gc, gd, ge]))
    g_wct = _wgrad_strips(dy, xr, xi, "wgrad_ssm_c", 0, red_rest.pair_done_start_scatter(g_wb))
    g_ar, g_ai, g_ldt, g_br, g_bi, g_cr, g_ci = _ssm_param_bwd(a_re, a_im, log_dt, b_re, b_im, g_lr, g_li, g_wb, g_wct)

    g_small = {
        "norm_mix": g_norm_mix.reshape(-1), "norm_mlp": g_norm_mlp.reshape(-1), "q_a_norm": g_qa.reshape(-1),
        "kv_a_norm": g_kva.reshape(-1), "q_norm": g_q.reshape(-1)[:QK_HEAD], "k_norm": g_k.reshape(-1)[:QK_HEAD],
        "ssm_a_re": g_ar, "ssm_a_im": g_ai, "ssm_log_dt": g_ldt.reshape(-1),
        "ssm_b_re": from_gcp(g_br), "ssm_b_im": from_gcp(g_bi),
        "ssm_c_re": g_cr.reshape(SSM_GROUPS, SSM_GROUP_CH, SSM_STATE), "ssm_c_im": g_ci.reshape(SSM_GROUPS, SSM_GROUP_CH, SSM_STATE),
        "ssm_d": g_d.reshape(SSM_GROUPS, SSM_GROUP_CH), "b_glu": g_b_glu.reshape(-1),
    }
    return loss_blk[0, 0], grad_x, g_small


def kernel(x, positions, norm_mix, w_in, q_a_norm, kv_a_norm, w_q_b, w_kv_b, q_norm, k_norm, w_o_mla, ssm_a_re, ssm_a_im, ssm_log_dt, ssm_b_re, ssm_b_im, ssm_c_re, ssm_c_im, ssm_d, w_glu, b_glu, w_o_ssm, w_out, norm_mlp, w_up, w_down, loss_target, m_norm_mix, m_w_in, m_q_a_norm, m_kv_a_norm, m_w_q_b, m_w_kv_b, m_q_norm, m_k_norm, m_w_o_mla, m_ssm_a_re, m_ssm_a_im, m_ssm_log_dt, m_ssm_b_re, m_ssm_b_im, m_ssm_c_re, m_ssm_c_im, m_ssm_d, m_w_glu, m_b_glu, m_w_o_ssm, m_w_out, m_norm_mlp, m_w_up, m_w_down, v_norm_mix, v_w_in, v_q_a_norm, v_kv_a_norm, v_w_q_b, v_w_kv_b, v_q_norm, v_k_norm, v_w_o_mla, v_ssm_a_re, v_ssm_a_im, v_ssm_log_dt, v_ssm_b_re, v_ssm_b_im, v_ssm_c_re, v_ssm_c_im, v_ssm_d, v_w_glu, v_b_glu, v_w_o_ssm, v_w_out, v_norm_mlp, v_w_up, v_w_down):
    args = dict(locals())
    w = {n: args[n][0] for n in WEIGHT_ORDER}
    m = {n: args["m_" + n][0] for n in WEIGHT_ORDER}
    v = {n: args["v_" + n][0] for n in WEIGHT_ORDER}
    big_names = [n for n, *_ in BIG_WEIGHTS]
    small_names = [n for n, _ in SMALL_WEIGHTS]

    place = jnp.stack([2 * lax.axis_index("x") + lax.axis_index("y"), lax.axis_index("c")]).astype(jnp.int32)
    rest = ["b", "c", "d", "e"]

    gather_b = _SplitGatherHalves(_cast_shards(w, "b", place))
    cast = [_cast_shards(w, g, place, gather_b.token()) for g in ("c", "d", "e", "a")]
    grp_b = gather_b.finish(cast[-1])
    gather = _SplitGather(cast, grp_b)
    red_a = _SplitReduction("a", ["a"], place)
    red_rest = _SplitReduction("rest", rest, place)
    small = {n: w[n] for n in small_names}

    loss_local, grad_x, g_small = _local_step(x[0], positions[0], loss_target[0], grp_b, small, gather, red_a, red_rest)
    loss = lax.psum(loss_local, ("x", "y", "c"))

    grad_w, delta_w, new_m, new_v = {}, {}, {}, {}

    def update(names, reduced, token):
        for n in names:
            g, off, _, _ = _place_in_group(n)
            grad_w[n], delta_w[n], new_m[n], new_v[n] = _adamw(w[n], reduced[g], m[n], v[n], "adamw_" + n, off, token)
            token = new_v[n]

    chip_sum = _SplitChipSum(_pair_sum_small(_pack_small(g_small)), place)
    in_a = [n for n, _ in GROUPS["a"][1]]
    update(in_a, {"a": red_a.join_done(chip_sum.token())[0]}, chip_sum.token())
    small_sum = chip_sum.done(new_v[in_a[-1]])
    g_s, d_s, m_s, v_s = _adamw(_pack_small(small), small_sum, _pack_small({n: m[n] for n in small_names}),
                                _pack_small({n: v[n] for n in small_names}), "adamw_small", 0, small_sum)
    g_s, d_s, m_s, v_s = _unpack_small(g_s), _unpack_small(d_s), _unpack_small(m_s), _unpack_small(v_s)
    for n in small_names:
        grad_w[n], delta_w[n], new_m[n], new_v[n] = g_s[n], d_s[n], m_s[n], v_s[n]
    halves = red_rest.scatter_done(v_s[small_names[0]])
    reduced_rest = dict(zip(rest, _swap_reduced_halves(halves)))
    update([n for n in big_names if n not in in_a], reduced_rest, halves[0])

    lead = lambda d: [d[n][None] for n in WEIGHT_ORDER]
    return (loss, grad_x[None], *lead(grad_w), *lead(delta_w), *lead(new_m), *lead(new_v))
```

```python
import math

import jax
import jax.numpy as jnp
import numpy as np
from jax import lax
from jax.experimental import pallas as pl
from jax.experimental.pallas import tpu as pltpu

F32 = jnp.float32
BF16 = jnp.bfloat16

D_MODEL = 1024
SSM_GROUPS = 32
SSM_GROUP_CH = 16
SSM_WIDTH = 512
SSM_STATE = 64
GP = SSM_GROUPS * SSM_STATE
N_HEADS = 8
QK_NOPE = 128
QK_ROPE = 64
QK_HEAD = 192
HEAD_PAD = 256
V_HEAD = 128
Q_LORA = 384
KV_LORA = 256
LAT_W = 768
D_IN = 3264
D_IN_PAD = 3328
D_FF = 4096
ROPE_THETA = 10000.0
EPS = 1e-6
ATT_SCALE = QK_HEAD ** -0.5

ADAM_LR = 0.001
ADAM_B1 = 0.9
ADAM_B2 = 0.999
ADAM_EPS = 1e-08
ADAM_WD = 0.01
ADAM_STEP = 10

VMEM_LIMIT_V7X = 56 * 1024 * 1024
MESH = pl.DeviceIdType.MESH

BIG_WEIGHTS = (
    ("w_in", 1024, 3264, "col"),
    ("w_q_b", 384, 1536, "col"),
    ("w_kv_b", 256, 2048, "col"),
    ("w_o_mla", 1024, 1024, "row"),
    ("w_glu", 512, 512, "row"),
    ("w_o_ssm", 512, 1024, "col"),
    ("w_out", 1024, 1024, "row"),
    ("w_up", 1024, 4096, "col"),
    ("w_down", 4096, 1024, "row"),
)
GROUPS = {
    "a": (1024, (("w_down", 1024), ("w_up", 1024), ("w_o_mla", 256), ("w_out", 256))),
    "b": (816, (("w_in", 1024),)),
    "c": (384, (("w_q_b", 384),)),
    "d": (512, (("w_kv_b", 256), ("w_glu", 128))),
    "e": (256, (("w_o_ssm", 512),)),
}


def _group_rows(group):
    return sum(r for _, r in GROUPS[group][1])


def _place_in_group(name):
    for group, (width, members) in GROUPS.items():
        off = 0
        for member, rows in members:
            if member == name:
                return group, off, rows, width
            off += rows
    raise KeyError(name)


SMALL_WEIGHTS = (
    ("norm_mix", (1024,)), ("q_a_norm", (384,)), ("kv_a_norm", (256,)), ("q_norm", (192,)), ("k_norm", (192,)),
    ("ssm_a_re", (32, 64)), ("ssm_a_im", (32, 64)), ("ssm_log_dt", (32,)),
    ("ssm_b_re", (32, 64, 16)), ("ssm_b_im", (32, 64, 16)), ("ssm_c_re", (32, 16, 64)), ("ssm_c_im", (32, 16, 64)),
    ("ssm_d", (32, 16)), ("b_glu", (512,)), ("norm_mlp", (1024,)),
)
WEIGHT_ORDER = ('norm_mix', 'w_in', 'q_a_norm', 'kv_a_norm', 'w_q_b', 'w_kv_b', 'q_norm', 'k_norm', 'w_o_mla', 'ssm_a_re',
                'ssm_a_im', 'ssm_log_dt', 'ssm_b_re', 'ssm_b_im', 'ssm_c_re', 'ssm_c_im', 'ssm_d', 'w_glu', 'b_glu',
                'w_o_ssm', 'w_out', 'norm_mlp', 'w_up', 'w_down')


def _cparams(*sem):
    return pltpu.CompilerParams(dimension_semantics=sem if sem else None, vmem_limit_bytes=VMEM_LIMIT_V7X)


def _resident(shape, index=None):
    index = (0,) * len(shape) if index is None else index
    return pl.BlockSpec(shape, lambda *_: index, pipeline_mode=pl.Buffered(1))


def _member_block(name):
    _, off, rows, width = _place_in_group(name)
    return _resident((4, rows, width), (0, off // rows, 0))


def _rows(t, width):
    return pl.BlockSpec((t, width), lambda i: (i, 0))


def _mm(a, b):
    return jnp.dot(a.astype(BF16), b.astype(BF16), preferred_element_type=F32)


def _mm_nt(a, b):
    return lax.dot_general(a.astype(BF16), b.astype(BF16), (((1,), (1,)), ((), ())), preferred_element_type=F32)


def _mm_tn(a, b):
    return lax.dot_general(a.astype(BF16), b.astype(BF16), (((0,), (0,)), ((), ())), preferred_element_type=F32)


def _rms_fwd(x, g, n):
    r = lax.rsqrt(jnp.sum(x * x, axis=-1, keepdims=True) * (1.0 / n) + EPS)
    return x * r * g


def _rms_bwd(x, g, dy, n):
    r = lax.rsqrt(jnp.sum(x * x, axis=-1, keepdims=True) * (1.0 / n) + EPS)
    xh = x * r
    dxh = dy * g
    dx = r * (dxh - xh * (jnp.sum(dxh * xh, axis=-1, keepdims=True) * (1.0 / n)))
    return dx, dy * xh


def _colsum(a):
    return jnp.sum(a, axis=0, keepdims=True)


def _accumulate(ref, value, first):
    @pl.when(first)
    def _():
        ref[...] = value

    @pl.when(jnp.logical_not(first))
    def _():
        ref[...] += value


def _sigmoid(a):
    return 1.0 / (1.0 + jnp.exp(-a))


GELU_C = math.sqrt(2.0 / math.pi)
GELU_A = 0.044715


def _gelu(y):
    return 0.5 * y * (1.0 + jnp.tanh(GELU_C * (y + GELU_A * y * y * y)))


def _gelu_grad(y):
    t = jnp.tanh(GELU_C * (y + GELU_A * y * y * y))
    return 0.5 * (1.0 + t) + 0.5 * y * (1.0 - t * t) * GELU_C * (1.0 + 3.0 * GELU_A * y * y)


def _in_proj_fwd(x, g1, w_in_p, t, token):
    l = x.shape[0]

    def body(x_ref, g_ref, w_ref, token_ref, u_ref, lat_ref, gs_ref, gm_ref):
        xn = _rms_fwd(x_ref[...], g_ref[...], D_MODEL).astype(BF16)
        u_ref[...] = _mm(xn, w_ref[:, 0:512])
        lat_ref[...] = _mm(xn, w_ref[:, 512:1280])
        gs_ref[...] = _mm(xn, w_ref[:, 1280:2304]).astype(BF16)
        gm_ref[...] = _mm(xn, w_ref[:, 2304:3328]).astype(BF16)

    return pl.pallas_call(
        body, name="in_proj_fwd", grid=(l // t,),
        in_specs=[_rows(t, D_MODEL), _resident((1, D_MODEL)), _resident((D_MODEL, D_IN_PAD)), ANY],
        out_specs=[_rows(t, 512), _rows(t, LAT_W), _rows(t, D_MODEL), _rows(t, D_MODEL)],
        out_shape=[jax.ShapeDtypeStruct((l, 512), F32), jax.ShapeDtypeStruct((l, LAT_W), F32),
                   jax.ShapeDtypeStruct((l, D_MODEL), BF16), jax.ShapeDtypeStruct((l, D_MODEL), BF16)],
        compiler_params=_cparams("parallel"),
    )(x, g1, w_in_p, token)


def _in_proj_bwd(x, g1, w_in_p, d_u, d_lat, d_gs, d_gm, dh, t):
    l = x.shape[0]

    def body(x_ref, g_ref, w_ref, du_ref, dlat_ref, dgs_ref, dgm_ref, dh_ref, gx_ref, xn_ref, dproj_ref, dg_ref):
        xv = x_ref[...]
        g = g_ref[...]
        xn_ref[...] = _rms_fwd(xv, g, D_MODEL).astype(BF16)
        dproj_ref[:, 0:512] = du_ref[...]
        dproj_ref[:, 512:1280] = dlat_ref[...]
        dproj_ref[:, 1280:2304] = dgs_ref[...]
        dproj_ref[:, 2304:3328] = dgm_ref[...]
        dxn = _mm_nt(dproj_ref[...], w_ref[...])
        dx, dg_rows = _rms_bwd(xv, g, dxn, D_MODEL)
        gx_ref[...] = dh_ref[...] + dx
        _accumulate(dg_ref, _colsum(dg_rows), pl.program_id(0) == 0)

    return pl.pallas_call(
        body, name="in_proj_bwd", grid=(l // t,),
        in_specs=[_rows(t, D_MODEL), _resident((1, D_MODEL)), _resident((D_MODEL, D_IN_PAD)), _rows(t, 512),
                  _rows(t, LAT_W), _rows(t, D_MODEL), _rows(t, D_MODEL), _rows(t, D_MODEL)],
        out_specs=[_rows(t, D_MODEL), _rows(t, D_MODEL), _rows(t, D_IN_PAD), pl.BlockSpec((1, D_MODEL), lambda i: (0, 0))],
        out_shape=[jax.ShapeDtypeStruct((l, D_MODEL), F32), jax.ShapeDtypeStruct((l, D_MODEL), BF16),
                   jax.ShapeDtypeStruct((l, D_IN_PAD), BF16), jax.ShapeDtypeStruct((1, D_MODEL), F32)],
        compiler_params=_cparams("arbitrary"),
    )(x, g1, w_in_p, d_u, d_lat, d_gs, d_gm, dh)


def _ssm_param_fn(a_re, a_im, log_dt, b_re, b_im):
    dt = jnp.exp(log_dt)
    er = jnp.exp(a_re * dt)
    lr = er * jnp.cos(a_im * dt)
    li = er * jnp.sin(a_im * dt)
    den = a_re * a_re + a_im * a_im
    nr = lr - 1.0
    kr = (nr * a_re + li * a_im) / den
    ki = (li * a_re - nr * a_im) / den
    rows = lambda k: jnp.broadcast_to(k[:, None, :], (SSM_GROUPS, SSM_GROUP_CH, SSM_STATE)).reshape(SSM_WIDTH, SSM_STATE)
    krt, kit = rows(kr), rows(ki)
    return lr, li, krt * b_re - kit * b_im, krt * b_im + kit * b_re


def _state_selector():
    row = lax.broadcasted_iota(jnp.int32, (SSM_STATE, GP), 0)
    col = lax.broadcasted_iota(jnp.int32, (SSM_STATE, GP), 1)
    return jnp.where(jnp.bitwise_and(col, SSM_STATE - 1) == row, 1.0, 0.0).astype(BF16)


def _own_group(rows, rows_per_group_log2):
    row = lax.broadcasted_iota(jnp.int32, (rows, GP), 0)
    col = lax.broadcasted_iota(jnp.int32, (rows, GP), 1)
    return jnp.right_shift(row, rows_per_group_log2) == jnp.right_shift(col, 6)


def _three_bf16(x):
    hi = x.astype(BF16)
    rest = x - hi.astype(F32)
    mid = rest.astype(BF16)
    return hi, mid, (rest - mid.astype(F32)).astype(BF16)


def _spread(x, sel):
    return sum(jnp.dot(part, sel, preferred_element_type=F32) for part in _three_bf16(x))


def _collect(xw, sel):
    return sum(lax.dot_general(part, sel, (((1,), (1,)), ((), ())), preferred_element_type=F32) for part in _three_bf16(xw))


def _ssm_param_fwd(a_re, a_im, log_dt, b_re, b_im, c_re, c_im):
    def body(ar_ref, ai_ref, ldt_ref, br_ref, bi_ref, cr_ref, ci_ref, wb_ref, wct_ref, tf_ref, tr_ref):
        lr, li, bbr, bbi = _ssm_param_fn(ar_ref[...], ai_ref[...], ldt_ref[...], br_ref[...], bi_ref[...])
        sel = _state_selector()
        own16 = _own_group(SSM_WIDTH, 4)
        own1 = _own_group(SSM_GROUPS, 0)
        block = lambda m: jnp.where(own16, jnp.dot(m.astype(BF16), sel, preferred_element_type=F32), 0.0).astype(BF16)
        wb_ref[:, 0:GP] = block(bbr)
        wb_ref[:, GP:2 * GP] = block(bbi)
        wct_ref[:, 0:GP] = block(cr_ref[...])
        wct_ref[:, GP:2 * GP] = block(-ci_ref[...])
        flat = lambda m: _colsum(jnp.where(own1, _spread(m, sel), 0.0))
        pr, pi = [], []
        qr, qi = lr, li
        for _ in range(8):
            pr.append(flat(qr))
            pi.append(flat(qi))
            qr, qi = qr * lr - qi * li, qr * li + qi * lr
        row = lax.broadcasted_iota(jnp.int32, (8, GP), 0)
        for n, k in enumerate((1, 2, 4)):
            tf_ref[2 * n] = jnp.where(row >= k, pr[k - 1], 0.0)
            tf_ref[2 * n + 1] = jnp.where(row >= k, pi[k - 1], 0.0)
            tr_ref[2 * n] = jnp.where(row < 8 - k, pr[k - 1], 0.0)
            tr_ref[2 * n + 1] = jnp.where(row < 8 - k, -pi[k - 1], 0.0)
        pick = lambda vals: sum(jnp.where(row == j, v, 0.0) for j, v in enumerate(vals))
        tf_ref[6] = pick(pr)
        tf_ref[7] = pick(pi)
        tr_ref[6] = pick(pr[::-1])
        tr_ref[7] = pick([-v for v in pi[::-1]])

    return pl.pallas_call(
        body, name="ssm_param_fwd",
        out_shape=[jax.ShapeDtypeStruct((SSM_WIDTH, 2 * GP), BF16), jax.ShapeDtypeStruct((SSM_WIDTH, 2 * GP), BF16),
                   jax.ShapeDtypeStruct((8, 8, GP), F32), jax.ShapeDtypeStruct((8, 8, GP), F32)],
        compiler_params=_cparams(),
    )(a_re, a_im, log_dt, b_re, b_im, c_re, c_im)


STRIP_CH = 128
STRIP_ST = 512
N_STRIPS = SSM_WIDTH // STRIP_CH


def _ssm_param_bwd(a_re, a_im, log_dt, b_re, b_im, g_lr, g_li, g_wb, g_wct):
    def body(ar_ref, ai_ref, ldt_ref, br_ref, bi_ref, glr_ref, gli_ref, gwb_ref, gwc_ref,
             o_ar, o_ai, o_ldt, o_br, o_bi, o_cr, o_ci):
        sel = _state_selector()
        own1 = _own_group(SSM_GROUPS, 0)
        row = lax.broadcasted_iota(jnp.int32, (SSM_WIDTH, STRIP_ST), 0)
        col = lax.broadcasted_iota(jnp.int32, (SSM_WIDTH, STRIP_ST), 1)
        own = jnp.bitwise_and(jnp.right_shift(row, 4), 7) == jnp.right_shift(col, 6)
        blocks = lambda m: _collect(jnp.where(own, m, 0.0), sel[:, 0:STRIP_ST])
        unflat = lambda v: _collect(jnp.where(own1, v, 0.0), sel)
        _, vjp = jax.vjp(_ssm_param_fn, ar_ref[...], ai_ref[...], ldt_ref[...], br_ref[...], bi_ref[...])
        d_ar, d_ai, d_ldt, d_br, d_bi = vjp((unflat(glr_ref[...]), unflat(gli_ref[...]),
                                             blocks(gwb_ref[:, 0:STRIP_ST]), blocks(gwb_ref[:, STRIP_ST:2 * STRIP_ST])))
        o_ar[...] = d_ar
        o_ai[...] = d_ai
        o_ldt[...] = d_ldt
        o_br[...] = d_br
        o_bi[...] = d_bi
        o_cr[...] = blocks(gwc_ref[:, 0:STRIP_ST])
        o_ci[...] = -blocks(gwc_ref[:, STRIP_ST:2 * STRIP_ST])

    g, p = SSM_GROUPS, SSM_STATE
    gp = jax.ShapeDtypeStruct((g, p), F32)
    gcp = jax.ShapeDtypeStruct((SSM_WIDTH, p), F32)
    return pl.pallas_call(
        body, name="ssm_param_bwd", out_shape=[gp, gp, jax.ShapeDtypeStruct((g, 1), F32), gcp, gcp, gcp, gcp],
        compiler_params=_cparams(),
    )(a_re, a_im, log_dt, b_re, b_im, g_lr, g_li, g_wb, g_wct)


def _strip(ref, j, im):
    return ref[STRIP_CH * j:STRIP_CH * (j + 1), im * GP + STRIP_ST * j:im * GP + STRIP_ST * (j + 1)]


def _wgrad_strips(a, b_re, b_im, name, im_block, token):
    l = a.shape[0]
    bl = min(l, 512)

    def body(a_ref, bre_ref, bim_ref, token_ref, o_ref):
        first = pl.program_id(0) == 0
        for j in range(N_STRIPS):
            aj = a_ref[:, STRIP_CH * j:STRIP_CH * (j + 1)]
            states = slice(STRIP_ST * j, STRIP_ST * (j + 1))
            _accumulate(o_ref.at[STRIP_CH * j:STRIP_CH * (j + 1), 0:STRIP_ST], _mm_tn(aj, bre_ref[:, states]), first)
            _accumulate(o_ref.at[STRIP_CH * j:STRIP_CH * (j + 1), STRIP_ST:2 * STRIP_ST], _mm_tn(aj, bim_ref[:, states]), first)

    return pl.pallas_call(
        body, name=name, grid=(l // bl,),
        in_specs=[pl.BlockSpec((bl, SSM_WIDTH), lambda k: (k, 0)), pl.BlockSpec((bl, GP), lambda k: (k, 0)),
                  pl.BlockSpec((bl, GP), lambda k: (k, im_block)), ANY],
        out_specs=pl.BlockSpec((SSM_WIDTH, 2 * STRIP_ST), lambda k: (0, 0)),
        out_shape=jax.ShapeDtypeStruct((SSM_WIDTH, 2 * STRIP_ST), F32),
        compiler_params=_cparams("arbitrary"),
    )(a, b_re, b_im, token)


SCAN_STRIP = 512


def _scan_chunk(inr_ref, ini_ref, outr_ref, outi_ref, cr_ref, ci_ref, tab_ref, tc, reverse):
    n_blocks = tc // 8

    def block(j, _):
        i = (n_blocks - 1 - j) if reverse else j
        rows = pl.ds(pl.multiple_of(i * 8, 8), 8)
        for s in range(GP // SCAN_STRIP):
            sl = pl.ds(s * SCAN_STRIP, SCAN_STRIP)
            xr = inr_ref[rows, sl]
            xi = ini_ref[rows, sl]
            for n, k in enumerate((1, 2, 4)):
                shift = (8 - k) if reverse else k
                sr = pltpu.roll(xr, shift, 0)
                si = pltpu.roll(xi, shift, 0)
                mr = tab_ref[2 * n, :, sl]
                mi = tab_ref[2 * n + 1, :, sl]
                xr, xi = xr + mr * sr - mi * si, xi + mr * si + mi * sr
            qr = tab_ref[6, :, sl]
            qi = tab_ref[7, :, sl]
            cr = cr_ref[:, sl]
            ci = ci_ref[:, sl]
            xr, xi = xr + qr * cr - qi * ci, xi + qr * ci + qi * cr
            outr_ref[rows, sl] = xr
            outi_ref[rows, sl] = xi
            edge = 0 if reverse else 7
            cr_ref[:, sl] = jnp.broadcast_to(xr[edge:edge + 1, :], (8, SCAN_STRIP))
            ci_ref[:, sl] = jnp.broadcast_to(xi[edge:edge + 1, :], (8, SCAN_STRIP))
        return 0

    lax.fori_loop(0, n_blocks, block, 0)


def _glu_pre(z, wg_ref):
    return sum(_mm(z[:, 128 * j:128 * (j + 1)], wg_ref[j]) for j in range(4))


def _ssm_fwd(u, wb, wc, tabs, dskip, grp_d, b_glu, grp_e, tc):
    l = u.shape[0]

    def body(u_ref, wb_ref, wc_ref, tab_ref, d_ref, wg_ref, bg_ref, wo_ref, xr_ref, xi_ref, y_ref, ys_ref,
             bur, bui, cr, ci):
        @pl.when(pl.program_id(0) == 0)
        def _():
            cr[...] = jnp.zeros_like(cr)
            ci[...] = jnp.zeros_like(ci)

        uv = u_ref[...]
        ub = uv.astype(BF16)
        for j in range(N_STRIPS):
            uj = ub[:, STRIP_CH * j:STRIP_CH * (j + 1)]
            states = slice(STRIP_ST * j, STRIP_ST * (j + 1))
            bur[:, states] = _mm(uj, _strip(wb_ref, j, 0))
            bui[:, states] = _mm(uj, _strip(wb_ref, j, 1))
        _scan_chunk(bur, bui, bur, bui, cr, ci, tab_ref, tc, False)
        xr_ref[...] = bur[...].astype(BF16)
        xi_ref[...] = bui[...].astype(BF16)
        y = jnp.concatenate(
            [_mm_nt(xr_ref[:, STRIP_ST * j:STRIP_ST * (j + 1)], _strip(wc_ref, j, 0))
             + _mm_nt(xi_ref[:, STRIP_ST * j:STRIP_ST * (j + 1)], _strip(wc_ref, j, 1)) for j in range(N_STRIPS)],
            axis=-1) + d_ref[...] * uv
        y_ref[...] = y
        z = _gelu(y)
        z2 = z * _sigmoid(_glu_pre(z, wg_ref) + bg_ref[...])
        for s in range(4):
            ys_ref[:, 256 * s:256 * (s + 1)] = _mm(z2, wo_ref[s]).astype(BF16)

    return pl.pallas_call(
        body, name="ssm_fwd", grid=(l // tc,),
        in_specs=[_rows(tc, 512), _resident((512, 2 * GP)), _resident((512, 2 * GP)), _resident((8, 8, GP)),
                  _resident((1, 512)), _member_block("w_glu"), _resident((1, 512)), _member_block("w_o_ssm")],
        out_specs=[_rows(tc, GP), _rows(tc, GP), _rows(tc, 512), _rows(tc, D_MODEL)],
        out_shape=[jax.ShapeDtypeStruct((l, GP), BF16), jax.ShapeDtypeStruct((l, GP), BF16),
                   jax.ShapeDtypeStruct((l, 512), F32), jax.ShapeDtypeStruct((l, D_MODEL), BF16)],
        scratch_shapes=[pltpu.VMEM((tc, GP), F32), pltpu.VMEM((tc, GP), F32), pltpu.VMEM((8, GP), F32),
                        pltpu.VMEM((8, GP), F32)],
        compiler_params=_cparams("arbitrary"),
    )(u, wb, wc, tabs, dskip, grp_d, b_glu, grp_e)


def _ssm_bwd(dys, y, u, xr, xi, wb, wc, tabs_rev, dskip, grp_d, b_glu, grp_e, tc):
    l = u.shape[0]
    nc = l // tc

    def body(dys_ref, y_ref, u_ref, xr_ref, xi_ref, wb_ref, wc_ref, tab_ref, d_ref, wg_ref, bg_ref, wo_ref,
             du_ref, a_ref, dy_ref, z_ref, z2_ref, dpre_ref, gb_ref, gd_ref, glr_ref, gli_ref,
             dxr, dxi, ar, ai, cr, ci):
        first = pl.program_id(0) == 0

        @pl.when(first)
        def _():
            cr[...] = jnp.zeros_like(cr)
            ci[...] = jnp.zeros_like(ci)

        yv = y_ref[...]
        uv = u_ref[...]
        dz2 = sum(_mm_nt(dys_ref[:, 256 * j:256 * (j + 1)], wo_ref[j]) for j in range(4))
        z = _gelu(yv)
        s = _sigmoid(_glu_pre(z, wg_ref) + bg_ref[...])
        dpre = dz2 * z * s * (1.0 - s)
        dpreb = dpre.astype(BF16)
        dz = dz2 * s + jnp.concatenate([_mm_nt(dpreb, wg_ref[j]) for j in range(4)], axis=-1)
        dy = dz * _gelu_grad(yv)
        z_ref[...] = z.astype(BF16)
        z2_ref[...] = (z * s).astype(BF16)
        dpre_ref[...] = dpre.astype(BF16)
        dy_ref[...] = dy.astype(BF16)
        _accumulate(gb_ref, _colsum(dpre), first)
        _accumulate(gd_ref, _colsum(dy * uv), first)

        dyb = dy.astype(BF16)
        for j in range(N_STRIPS):
            dyj = dyb[:, STRIP_CH * j:STRIP_CH * (j + 1)]
            dxr[:, STRIP_ST * j:STRIP_ST * (j + 1)] = _mm(dyj, _strip(wc_ref, j, 0))
            dxi[:, STRIP_ST * j:STRIP_ST * (j + 1)] = _mm(dyj, _strip(wc_ref, j, 1))
        ar[pl.ds(tc, 8), :] = cr[...]
        ai[pl.ds(tc, 8), :] = ci[...]
        _scan_chunk(dxr, dxi, ar, ai, cr, ci, tab_ref, tc, True)
        a_ref[:, 0:GP] = ar[pl.ds(0, tc), :].astype(BF16)
        a_ref[:, GP:2 * GP] = ai[pl.ds(0, tc), :].astype(BF16)
        du_states = jnp.concatenate(
            [_mm_nt(a_ref[:, STRIP_ST * j:STRIP_ST * (j + 1)], _strip(wb_ref, j, 0))
             + _mm_nt(a_ref[:, GP + STRIP_ST * j:GP + STRIP_ST * (j + 1)], _strip(wb_ref, j, 1)) for j in range(N_STRIPS)],
            axis=-1)
        du_ref[...] = (dy * d_ref[...] + du_states).astype(BF16)
        anr = ar[pl.ds(1, tc), :]
        ani = ai[pl.ds(1, tc), :]
        xrv = xr_ref[...].astype(F32)
        xiv = xi_ref[...].astype(F32)
        _accumulate(glr_ref, _colsum(anr * xrv + ani * xiv), first)
        _accumulate(gli_ref, _colsum(ani * xrv - anr * xiv), first)

    rev = lambda w: pl.BlockSpec((tc, w), lambda i: (nc - 1 - i, 0))
    acc = lambda w: pl.BlockSpec((1, w), lambda i: (0, 0))
    bf = jax.ShapeDtypeStruct((l, 512), BF16)
    return pl.pallas_call(
        body, name="ssm_bwd", grid=(nc,),
        in_specs=[rev(D_MODEL), rev(512), rev(512), rev(GP), rev(GP), _resident((512, 2 * GP)), _resident((512, 2 * GP)),
                  _resident((8, 8, GP)), _resident((1, 512)), _member_block("w_glu"), _resident((1, 512)),
                  _member_block("w_o_ssm")],
        out_specs=[rev(512), rev(2 * GP), rev(512), rev(512), rev(512), rev(512), acc(512), acc(512), acc(GP), acc(GP)],
        out_shape=[bf, jax.ShapeDtypeStruct((l, 2 * GP), BF16), bf, bf, bf, bf,
                   jax.ShapeDtypeStruct((1, 512), F32), jax.ShapeDtypeStruct((1, 512), F32),
                   jax.ShapeDtypeStruct((1, GP), F32), jax.ShapeDtypeStruct((1, GP), F32)],
        scratch_shapes=[pltpu.VMEM((tc, GP), F32), pltpu.VMEM((tc, GP), F32), pltpu.VMEM((tc + 8, GP), F32),
                        pltpu.VMEM((tc + 8, GP), F32), pltpu.VMEM((8, GP), F32), pltpu.VMEM((8, GP), F32)],
        compiler_params=_cparams("arbitrary"),
    )(dys, y, u, xr, xi, wb, wc, tabs_rev, dskip, grp_d, b_glu, grp_e)


def _swap_halves(b):
    lane = lax.broadcasted_iota(jnp.int32, b.shape, 1)
    return jnp.where(lane < 32, pltpu.roll(b, 96, 1), pltpu.roll(b, 32, 1))


def _rope_tables(pos_ref, invf_ref, sgn_ref):
    ang = pos_ref[...].astype(F32) * invf_ref[...]
    return jnp.cos(ang), jnp.sin(ang) * sgn_ref[...]


def _mla_pre_fwd(lat, pos, invf, sgn, gqa, gkva, gq, gk, w_qb_p, w_kvb, t):
    l = lat.shape[0]

    def body(lat_ref, pos_ref, invf_ref, sgn_ref, gqa_ref, gkva_ref, gq_ref, gk_ref, wq_ref, wkv_ref, q_ref, k_ref, v_ref):
        cs, sn = _rope_tables(pos_ref, invf_ref, sgn_ref)
        ql = _rms_fwd(lat_ref[:, 0:Q_LORA], gqa_ref[...], Q_LORA)
        ckn = _rms_fwd(lat_ref[:, Q_LORA:Q_LORA + KV_LORA], gkva_ref[...], KV_LORA)
        kpe = lat_ref[:, 640:768]
        q0 = _mm(ql, wq_ref[...])
        cknb = ckn.astype(BF16)
        kv = jnp.concatenate([_mm(cknb, wkv_ref[s]) for s in range(4)], axis=-1)
        for h in range(N_HEADS):
            q1 = _rms_fwd(q0[:, HEAD_PAD * h:HEAD_PAD * (h + 1)], gq_ref[...], QK_HEAD)
            b = q1[:, 128:256]
            q_ref[h, :, 0:128] = (q1[:, 0:128] * ATT_SCALE).astype(BF16)
            q_ref[h, :, 128:256] = ((b * cs + _swap_halves(b) * sn) * ATT_SCALE).astype(BF16)
            k0 = jnp.concatenate([kv[:, 256 * h:256 * h + 128], kpe], axis=-1)
            k1 = _rms_fwd(k0, gk_ref[...], QK_HEAD)
            b = k1[:, 128:256]
            k_ref[h, :, 0:128] = k1[:, 0:128].astype(BF16)
            k_ref[h, :, 128:256] = (b * cs + _swap_halves(b) * sn).astype(BF16)
            v_ref[h] = kv[:, 256 * h + 128:256 * h + 256].astype(BF16)

    heads = lambda w: pl.BlockSpec((N_HEADS, t, w), lambda i: (0, i, 0))
    return pl.pallas_call(
        body, name="mla_pre_fwd", grid=(l // t,),
        in_specs=[_rows(t, LAT_W), _rows(t, 1), _resident((1, 128)), _resident((1, 128)), _resident((1, Q_LORA)),
                  _resident((1, KV_LORA)), _resident((1, HEAD_PAD)), _resident((1, HEAD_PAD)),
                  _resident((Q_LORA, N_HEADS * HEAD_PAD)), _member_block("w_kv_b")],
        out_specs=[heads(HEAD_PAD), heads(HEAD_PAD), heads(V_HEAD)],
        out_shape=[jax.ShapeDtypeStruct((N_HEADS, l, HEAD_PAD), BF16), jax.ShapeDtypeStruct((N_HEADS, l, HEAD_PAD), BF16),
                   jax.ShapeDtypeStruct((N_HEADS, l, V_HEAD), BF16)],
        compiler_params=_cparams("parallel"),
    )(lat, pos, invf, sgn, gqa, gkva, gq, gk, w_qb_p, w_kvb)


def _mla_pre_bwd(lat, pos, invf, sgn, gqa, gkva, gq, gk, w_qb_p, w_kvb, dq, dk, dv, t, token):
    l = lat.shape[0]

    def body(lat_ref, pos_ref, invf_ref, sgn_ref, gqa_ref, gkva_ref, gq_ref, gk_ref, wq_ref, wkv_ref, dq_ref, dk_ref, dv_ref,
             token_ref, dlat_ref, ql_ref, dq0_ref, ckn_ref, dkv_ref, ggqa_ref, ggkva_ref, ggq_ref, ggk_ref):
        first = pl.program_id(0) == 0
        cs, sn = _rope_tables(pos_ref, invf_ref, sgn_ref)
        q_lat = lat_ref[:, 0:Q_LORA]
        c_kv = lat_ref[:, Q_LORA:Q_LORA + KV_LORA]
        kpe = lat_ref[:, 640:768]
        ql = _rms_fwd(q_lat, gqa_ref[...], Q_LORA)
        ckn = _rms_fwd(c_kv, gkva_ref[...], KV_LORA)
        ql_ref[...] = ql.astype(BF16)
        ckn_ref[...] = ckn.astype(BF16)
        q0 = _mm(ql, wq_ref[...])
        cknb = ckn.astype(BF16)
        kv = jnp.concatenate([_mm(cknb, wkv_ref[s]) for s in range(4)], axis=-1)
        dkpe = jnp.zeros_like(kpe)
        ggq = jnp.zeros((1, HEAD_PAD), F32)
        ggk = jnp.zeros((1, HEAD_PAD), F32)

        def unrope(d):
            b = d[:, 128:256]
            return jnp.concatenate([d[:, 0:128], b * cs + _swap_halves(b * sn)], axis=-1)

        for h in range(N_HEADS):
            dq1 = unrope(dq_ref[h] * ATT_SCALE)
            dq0h, gq_rows = _rms_bwd(q0[:, HEAD_PAD * h:HEAD_PAD * (h + 1)], gq_ref[...], dq1, QK_HEAD)
            ggq = ggq + _colsum(gq_rows)
            dq0_ref[:, HEAD_PAD * h:HEAD_PAD * (h + 1)] = dq0h.astype(BF16)
            k0 = jnp.concatenate([kv[:, 256 * h:256 * h + 128], kpe], axis=-1)
            dk0, gk_rows = _rms_bwd(k0, gk_ref[...], unrope(dk_ref[h]), QK_HEAD)
            ggk = ggk + _colsum(gk_rows)
            dkpe = dkpe + dk0[:, 128:256]
            dkv_ref[:, 256 * h:256 * h + 128] = dk0[:, 0:128].astype(BF16)
            dkv_ref[:, 256 * h + 128:256 * h + 256] = dv_ref[h].astype(BF16)
        dql = _mm_nt(dq0_ref[...], wq_ref[...])
        dckn = sum(_mm_nt(dkv_ref[:, 512 * s:512 * (s + 1)], wkv_ref[s]) for s in range(4))
        dq_lat, gqa_rows = _rms_bwd(q_lat, gqa_ref[...], dql, Q_LORA)
        dc_kv, gkva_rows = _rms_bwd(c_kv, gkva_ref[...], dckn, KV_LORA)
        dlat_ref[:, 0:Q_LORA] = dq_lat.astype(BF16)
        dlat_ref[:, Q_LORA:Q_LORA + KV_LORA] = dc_kv.astype(BF16)
        dlat_ref[:, 640:768] = dkpe.astype(BF16)
        _accumulate(ggqa_ref, _colsum(gqa_rows), first)
        _accumulate(ggkva_ref, _colsum(gkva_rows), first)
        _accumulate(ggq_ref, ggq, first)
        _accumulate(ggk_ref, ggk, first)

    heads = lambda w: pl.BlockSpec((N_HEADS, t, w), lambda i: (0, i, 0))
    acc = lambda w: pl.BlockSpec((1, w), lambda i: (0, 0))
    return pl.pallas_call(
        body, name="mla_pre_bwd", grid=(l // t,),
        in_specs=[_rows(t, LAT_W), _rows(t, 1), _resident((1, 128)), _resident((1, 128)), _resident((1, Q_LORA)),
                  _resident((1, KV_LORA)), _resident((1, HEAD_PAD)), _resident((1, HEAD_PAD)),
                  _resident((Q_LORA, N_HEADS * HEAD_PAD)), _member_block("w_kv_b"),
                  heads(HEAD_PAD), heads(HEAD_PAD), heads(V_HEAD), ANY],
        out_specs=[_rows(t, LAT_W), _rows(t, Q_LORA), _rows(t, N_HEADS * HEAD_PAD), _rows(t, KV_LORA), _rows(t, N_HEADS * 256),
                   acc(Q_LORA), acc(KV_LORA), acc(HEAD_PAD), acc(HEAD_PAD)],
        out_shape=[jax.ShapeDtypeStruct((l, LAT_W), BF16), jax.ShapeDtypeStruct((l, Q_LORA), BF16),
                   jax.ShapeDtypeStruct((l, N_HEADS * HEAD_PAD), BF16), jax.ShapeDtypeStruct((l, KV_LORA), BF16),
                   jax.ShapeDtypeStruct((l, N_HEADS * 256), BF16), jax.ShapeDtypeStruct((1, Q_LORA), F32),
                   jax.ShapeDtypeStruct((1, KV_LORA), F32), jax.ShapeDtypeStruct((1, HEAD_PAD), F32),
                   jax.ShapeDtypeStruct((1, HEAD_PAD), F32)],
        compiler_params=_cparams("arbitrary"),
    )(lat, pos, invf, sgn, gqa, gkva, gq, gk, w_qb_p, w_kvb, dq, dk, dv, token)


def _causal(s, transposed):
    row = lax.broadcasted_iota(jnp.int32, s.shape, 0)
    col = lax.broadcasted_iota(jnp.int32, s.shape, 1)
    keep = (row <= col) if transposed else (col <= row)
    return jnp.where(keep, s, -jnp.inf)


def _as_row(col):
    n = col.shape[0]
    row = lax.broadcasted_iota(jnp.int32, (n, n), 0)
    lane = lax.broadcasted_iota(jnp.int32, (n, n), 1)
    return jnp.sum(jnp.where(row == lane, col, 0.0), axis=0, keepdims=True)


def _attn_fwd(q, k, v, tq):
    l = q.shape[1]

    hb = 2

    def body(q_ref, k_ref, v_ref, o_ref, lse_ref):
        qi = pl.program_id(1)
        qs = [q_ref[a] for a in range(hb)]

        def step(kb, carry, masked):
            rows = pl.ds(pl.multiple_of(kb * tq, tq), tq)
            out = []
            for a, (m, den, acc) in enumerate(carry):
                s = _mm_nt(qs[a], k_ref[a, rows, :])
                if masked:
                    s = _causal(s, False)
                m_new = jnp.maximum(m, jnp.max(s, axis=-1, keepdims=True))
                alpha = jnp.exp(m - m_new)
                p = jnp.exp(s - m_new)
                den = alpha * den + jnp.sum(p, axis=-1, keepdims=True)
                acc = alpha * acc + _mm(p, v_ref[a, rows, :])
                out.append((m_new, den, acc))
            return tuple(out)

        init = tuple((jnp.full((tq, 1), -jnp.inf, F32), jnp.zeros((tq, 1), F32), jnp.zeros((tq, V_HEAD), F32))
                     for _ in range(hb))
        carry = lax.fori_loop(0, qi, lambda kb, c: step(kb, c, False), init)
        for a, (m, den, acc) in enumerate(step(qi, carry, True)):
            o_ref[:, V_HEAD * a:V_HEAD * (a + 1)] = acc / den
            lse_ref[a, 0] = _as_row(m + jnp.log(den))

    return pl.pallas_call(
        body, name="attn_fwd", grid=(N_HEADS // hb, l // tq),
        in_specs=[pl.BlockSpec((hb, tq, HEAD_PAD), lambda h, i: (h, i, 0)), pl.BlockSpec((hb, l, HEAD_PAD), lambda h, i: (h, 0, 0)),
                  pl.BlockSpec((hb, l, V_HEAD), lambda h, i: (h, 0, 0))],
        out_specs=[pl.BlockSpec((tq, hb * V_HEAD), lambda h, i: (i, h)), pl.BlockSpec((hb, 1, 1, tq), lambda h, i: (h, i, 0, 0))],
        out_shape=[jax.ShapeDtypeStruct((l, N_HEADS * V_HEAD), F32), jax.ShapeDtypeStruct((N_HEADS, l // tq, 1, tq), F32)],
        compiler_params=_cparams("parallel", "arbitrary"),
    )(q, k, v)


def _attn_bwd(q, k, v, o, do, lse_t, tq, token):
    l = q.shape[1]
    nq = l // tq

    hb = 1

    def body(q_ref, k_ref, v_ref, o_ref, do_ref, lse_ref, token_ref, dq_ref, dk_ref, dv_ref):
        ki = pl.program_id(1)

        @pl.when(ki == 0)
        def _():
            dq_ref[...] = jnp.zeros_like(dq_ref)

        kblks = [k_ref[a] for a in range(hb)]
        vblks = [v_ref[a] for a in range(hb)]
        ones = jnp.ones((8, V_HEAD), BF16)

        def step(qb, carry, masked):
            rows = pl.ds(pl.multiple_of(qb * tq, tq), tq)
            out = []
            for a, (dk, dv) in enumerate(carry):
                cols = slice(V_HEAD * a, V_HEAD * (a + 1))
                qblk = q_ref[a, rows, :]
                dov = do_ref[rows, cols]
                dob = dov.astype(BF16)
                delta = sum(_mm_nt(ones, part) for part in _three_bf16(dov * o_ref[rows, cols]))[0:1, :]
                st = _mm_nt(kblks[a], qblk)
                if masked:
                    st = _causal(st, True)
                pt = jnp.exp(st - lse_ref[a, qb])
                dv = dv + _mm(pt, dob)
                dst = (pt * (_mm_nt(vblks[a], dob) - delta)).astype(BF16)
                dk = dk + _mm(dst, qblk)
                dq_ref[a, rows, :] += _mm_tn(dst, kblks[a])
                out.append((dk, dv))
            return tuple(out)

        init = tuple((jnp.zeros((tq, HEAD_PAD), F32), jnp.zeros((tq, V_HEAD), F32)) for _ in range(hb))
        carry = lax.fori_loop(ki + 1, nq, lambda qb, c: step(qb, c, False), step(ki, init, True))
        for a, (dk, dv) in enumerate(carry):
            dk_ref[a] = dk
            dv_ref[a] = dv

    return pl.pallas_call(
        body, name="attn_bwd", grid=(N_HEADS // hb, nq),
        in_specs=[pl.BlockSpec((hb, l, HEAD_PAD), lambda h, i: (h, 0, 0)), pl.BlockSpec((hb, tq, HEAD_PAD), lambda h, i: (h, i, 0)),
                  pl.BlockSpec((hb, tq, V_HEAD), lambda h, i: (h, i, 0)), pl.BlockSpec((l, hb * V_HEAD), lambda h, i: (0, h)),
                  pl.BlockSpec((l, hb * V_HEAD), lambda h, i: (0, h)), pl.BlockSpec((hb, nq, 1, tq), lambda h, i: (h, 0, 0, 0)), ANY],
        out_specs=[pl.BlockSpec((hb, l, HEAD_PAD), lambda h, i: (h, 0, 0)), pl.BlockSpec((hb, tq, HEAD_PAD), lambda h, i: (h, i, 0)),
                   pl.BlockSpec((hb, tq, V_HEAD), lambda h, i: (h, i, 0))],
        out_shape=[jax.ShapeDtypeStruct((N_HEADS, l, HEAD_PAD), F32), jax.ShapeDtypeStruct((N_HEADS, l, HEAD_PAD), F32),
                   jax.ShapeDtypeStruct((N_HEADS, l, V_HEAD), F32)],
        compiler_params=_cparams("parallel", "arbitrary"),
    )(q, k, v, o, do, lse_t, token)


def _row_shards_mm(a, w_ref):
    a = a.astype(BF16)
    return sum(_mm(a[:, 256 * j:256 * (j + 1)], w_ref[j]) for j in range(4))


def _row_shards_mm_nt(a, w_ref):
    a = a.astype(BF16)
    return jnp.concatenate([_mm_nt(a, w_ref[j]) for j in range(4)], axis=-1)


def _merge_fwd(attn, y_ssm, gs, gm, x, grp_a, t):
    l = x.shape[0]

    def body(attn_ref, ys_ref, gs_ref, gm_ref, x_ref, wo_ref, wout_ref, ym_ref, mixed_ref, h_ref):
        y_mla = _row_shards_mm(attn_ref[...], wo_ref)
        ym_ref[...] = y_mla.astype(BF16)
        mixed = (_sigmoid(gs_ref[...].astype(F32)) * ys_ref[...].astype(F32)
                 + _sigmoid(gm_ref[...].astype(F32)) * y_mla).astype(BF16)
        mixed_ref[...] = mixed
        h_ref[...] = x_ref[...] + _row_shards_mm(mixed, wout_ref)

    r = lambda: _rows(t, D_MODEL)
    return pl.pallas_call(
        body, name="merge_fwd", grid=(l // t,),
        in_specs=[r(), r(), r(), r(), r(), _member_block("w_o_mla"), _member_block("w_out")],
        out_specs=[r(), r(), r()],
        out_shape=[jax.ShapeDtypeStruct((l, D_MODEL), BF16), jax.ShapeDtypeStruct((l, D_MODEL), BF16),
                   jax.ShapeDtypeStruct((l, D_MODEL), F32)],
        compiler_params=_cparams("parallel"),
    )(attn, y_ssm, gs, gm, x, grp_a, grp_a)


def _merge_bwd(dh, y_ssm, y_mla, gs, gm, grp_a, t):
    l = dh.shape[0]

    def body(dh_ref, ys_ref, ym_ref, gs_ref, gm_ref, wo_ref, wout_ref, dys_ref, dym_ref, dgs_ref, dgm_ref, dattn_ref):
        dmixed = _row_shards_mm_nt(dh_ref[...], wout_ref)
        sg = _sigmoid(gs_ref[...].astype(F32))
        sm = _sigmoid(gm_ref[...].astype(F32))
        dys_ref[...] = (dmixed * sg).astype(BF16)
        dgs_ref[...] = (dmixed * ys_ref[...].astype(F32) * sg * (1.0 - sg)).astype(BF16)
        dym = (dmixed * sm).astype(BF16)
        dym_ref[...] = dym
        dgm_ref[...] = (dmixed * ym_ref[...].astype(F32) * sm * (1.0 - sm)).astype(BF16)
        dattn_ref[...] = _row_shards_mm_nt(dym, wo_ref)

    r = lambda: _rows(t, D_MODEL)
    bf = jax.ShapeDtypeStruct((l, D_MODEL), BF16)
    return pl.pallas_call(
        body, name="merge_bwd", grid=(l // t,),
        in_specs=[r(), r(), r(), r(), r(), _member_block("w_o_mla"), _member_block("w_out")],
        out_specs=[r(), r(), r(), r(), r()],
        out_shape=[bf, bf, bf, bf, jax.ShapeDtypeStruct((l, D_MODEL), F32)],
        compiler_params=_cparams("parallel"),
    )(dh, y_ssm, y_mla, gs, gm, grp_a, grp_a)


def _mlp_fwd_bwd(h, tgt, g2, grp_a, t):
    l = h.shape[0]

    def body(h_ref, tgt_ref, g_ref, wu_ref, wd_ref, dh_ref, hn_ref, da_ref, hid_ref, dout_ref, loss_ref, dg_ref):
        first = pl.program_id(0) == 0
        hv = h_ref[...]
        g = g_ref[...]
        hn = _rms_fwd(hv, g, D_MODEL).astype(BF16)
        hn_ref[...] = hn
        out = hv
        relus = []
        for s in range(4):
            cols = slice(1024 * s, 1024 * (s + 1))
            relu = jnp.maximum(_mm(hn, wu_ref[s]), 0.0)
            relus.append(relu)
            hid = (relu * relu).astype(BF16)
            hid_ref[:, cols] = hid
            out = out + _mm(hid, wd_ref[s])
        err = out - tgt_ref[...]
        _accumulate(loss_ref, jnp.full((8, 128), jnp.sum(err * err) * (0.5 / D_MODEL), F32), first)
        dout = err * (1.0 / D_MODEL)
        doutb = dout.astype(BF16)
        dout_ref[...] = doutb
        dhn = jnp.zeros_like(hv)
        for s in range(4):
            da = (_mm_nt(doutb, wd_ref[s]) * (2.0 * relus[s])).astype(BF16)
            da_ref[:, 1024 * s:1024 * (s + 1)] = da
            dhn = dhn + _mm_nt(da, wu_ref[s])
        dx, dg_rows = _rms_bwd(hv, g, dhn, D_MODEL)
        dh_ref[...] = dout + dx
        _accumulate(dg_ref, _colsum(dg_rows), first)

    r = lambda w: _rows(t, w)
    return pl.pallas_call(
        body, name="mlp_fwd_bwd", grid=(l // t,),
        in_specs=[r(D_MODEL), r(D_MODEL), _resident((1, D_MODEL)), _member_block("w_up"), _member_block("w_down")],
        out_specs=[r(D_MODEL), r(D_MODEL), r(D_FF), r(D_FF), r(D_MODEL), pl.BlockSpec((8, 128), lambda i: (0, 0)),
                   pl.BlockSpec((1, D_MODEL), lambda i: (0, 0))],
        out_shape=[jax.ShapeDtypeStruct((l, D_MODEL), F32), jax.ShapeDtypeStruct((l, D_MODEL), BF16),
                   jax.ShapeDtypeStruct((l, D_FF), BF16), jax.ShapeDtypeStruct((l, D_FF), BF16),
                   jax.ShapeDtypeStruct((l, D_MODEL), BF16), jax.ShapeDtypeStruct((8, 128), F32),
                   jax.ShapeDtypeStruct((1, D_MODEL), F32)],
        compiler_params=_cparams("arbitrary"),
    )(h, tgt, g2, grp_a, grp_a)


def _wgrad(a, b, name):
    l, m = a.shape
    n = b.shape[1]
    bm = m if m <= 512 else 512
    bl = min(l, 2048 if n <= 1024 else 1024)

    def body(a_ref, b_ref, o_ref):
        _accumulate(o_ref, _mm_tn(a_ref[...], b_ref[...]), pl.program_id(1) == 0)

    return pl.pallas_call(
        body, name=name, grid=(m // bm, l // bl),
        in_specs=[pl.BlockSpec((bl, bm), lambda i, j: (j, i)), pl.BlockSpec((bl, n), lambda i, j: (j, 0))],
        out_specs=pl.BlockSpec((bm, n), lambda i, j: (i, 0)),
        out_shape=jax.ShapeDtypeStruct((m, n), F32),
        compiler_params=_cparams("parallel", "arbitrary"),
    )(a, b)


def _wgrad_into(a, b, member, cut, dest=None):
    group, off, rs, cs = _place_in_group(member)
    l = a.shape[0]
    bm = min(rs, 512)
    bl = min(l, 2048)
    nb = rs // bm
    if cut == "row":
        a_spec = pl.BlockSpec((bl, bm), lambda j, i, k: (k, j * nb + i))
        b_spec = pl.BlockSpec((bl, cs), lambda j, i, k: (k, 0))
    else:
        a_spec = pl.BlockSpec((bl, bm), lambda j, i, k: (k, i))
        b_spec = pl.BlockSpec((bl, cs), lambda j, i, k: (k, j))

    def body(a_ref, b_ref, *rest):
        o_ref = rest[-1]
        part = _mm_tn(a_ref[...], b_ref[...])

        @pl.when(pl.program_id(2) == 0)
        def _():
            o_ref[0] = part

        @pl.when(pl.program_id(2) != 0)
        def _():
            o_ref[0] += part

    operands, in_specs, aliases = [a, b], [a_spec, b_spec], {}
    if dest is not None:
        operands.append(dest)
        in_specs.append(ANY)
        aliases = {2: 0}
    return pl.pallas_call(
        body, name="wgrad_" + member, grid=(4, nb, l // bl), in_specs=in_specs,
        out_specs=pl.BlockSpec((1, bm, cs), lambda j, i, k: (j, off // bm + i, 0)),
        out_shape=jax.ShapeDtypeStruct((4, _group_rows(group), cs), F32), input_output_aliases=aliases,
        compiler_params=_cparams("parallel", "parallel", "arbitrary"),
    )(*operands)


def _adamw(w, g, m, v, name, g_off, token):
    r, c = w.shape
    br = r
    for cand in (256, 128, 64, 32, 16, 8):
        if r % cand == 0 and g_off % cand == 0:
            br = cand
            break

    def body(w_ref, g_ref, m_ref, v_ref, token_ref, go_ref, d_ref, nm_ref, nv_ref):
        gv = g_ref[...]
        go_ref[...] = gv
        nm = ADAM_B1 * m_ref[...] + (1.0 - ADAM_B1) * gv
        nv = ADAM_B2 * v_ref[...] + (1.0 - ADAM_B2) * (gv * gv)
        m_hat = nm / (1.0 - ADAM_B1 ** ADAM_STEP)
        v_hat = nv / (1.0 - ADAM_B2 ** ADAM_STEP)
        d_ref[...] = -ADAM_LR * (m_hat / (jnp.sqrt(v_hat) + ADAM_EPS) + ADAM_WD * w_ref[...])
        nm_ref[...] = nm
        nv_ref[...] = nv

    spec = lambda: pl.BlockSpec((br, c), lambda i: (i, 0))
    g_spec = pl.BlockSpec((br, c), lambda i: (g_off // br + i, 0))
    shp = jax.ShapeDtypeStruct((r, c), F32)
    return pl.pallas_call(
        body, name=name, grid=(r // br,), in_specs=[spec(), g_spec, spec(), spec(), ANY],
        out_specs=[spec(), spec(), spec(), spec()], out_shape=[shp, shp, shp, shp], compiler_params=_cparams("parallel"),
    )(w, g, m, v, token)


def _place():
    return lax.axis_index("x"), lax.axis_index("y"), lax.axis_index("c")


def _other_chips(x, y):
    return [(1 - x, y), (x, 1 - y), (1 - x, 1 - y)]


ANY = pl.BlockSpec(memory_space=pl.ANY)


def _gather_weights(bufs):
    n = len(bufs)

    def body(*refs):
        outs, send_sems, recv_sems = refs[n:2 * n], refs[2 * n], refs[2 * n + 1]
        x, y, c = _place()
        chips = _other_chips(x, y)

        def part(g, px, py, pc):
            half = outs[g].shape[1] // 2
            return outs[g].at[2 * px + py, pl.ds(pl.multiple_of(pc * half, 16), half), :]

        def copy(k, src, dst, to):
            return pltpu.make_async_remote_copy(src_ref=src, dst_ref=dst, send_sem=send_sems.at[k], recv_sem=recv_sems.at[k],
                                                device_id=to, device_id_type=MESH)

        first = [copy(6 * g + j, part(g, x, y, c), part(g, x, y, c), (*chip, c)) for g in range(n) for j, chip in enumerate(chips)]
        for cp in first:
            cp.start()
        passed = []
        for g in range(n):
            for j, chip in enumerate(chips):
                landed = part(g, *chip, c)
                copy(6 * g + j, landed, landed, (x, y, c)).wait_recv()
                passed.append(copy(6 * g + 3 + j, landed, landed, (x, y, 1 - c)))
                passed[-1].start()
        for g in range(n):
            for j, chip in enumerate(chips):
                other = part(g, *chip, 1 - c)
                copy(6 * g + 3 + j, other, other, (x, y, c)).wait_recv()
        for cp in first + passed:
            cp.wait_send()

    return pl.pallas_call(
        body, name="gather_weights", in_specs=[ANY] * n, out_specs=[ANY] * n,
        out_shape=[jax.ShapeDtypeStruct(b.shape, b.dtype) for b in bufs], input_output_aliases={g: g for g in range(n)},
        scratch_shapes=[pltpu.SemaphoreType.DMA((6 * n,)), pltpu.SemaphoreType.DMA((6 * n,))],
    )(*bufs)


def _cast_shards(shards, group, place, after=None):
    width, members = GROUPS[group]
    rows = _group_rows(group)
    extra = [] if after is None else [after]

    def body(place_ref, *refs):
        out = refs[-1]
        off = 0
        for ref, (_, r) in zip(refs[:-1], members):
            out[0, off:off + r, :] = ref[...].astype(BF16)
            off += r

    grid_spec = pltpu.PrefetchScalarGridSpec(
        num_scalar_prefetch=1, grid=(1,),
        in_specs=[pl.BlockSpec((r, width), lambda i, p: (0, 0)) for _, r in members] + [ANY] * len(extra),
        out_specs=pl.BlockSpec((1, rows, width), lambda i, p: (p[0], 0, 0)))
    return pl.pallas_call(
        body, name="cast_shards_" + group, grid_spec=grid_spec, out_shape=jax.ShapeDtypeStruct((4, rows, width), BF16),
        compiler_params=_cparams("arbitrary"),
    )(place, *[shards[name] for name, _ in members], *extra)


def _block_rows(h):
    return next(cand for cand in (256, 192, 128, 64, 32, 16) if h % cand == 0)


def _add_pair(buf, got, place, name):
    n, h, w = got.shape
    bh = _block_rows(h)
    nb = h // bh

    def body(place_ref, a_ref, b_ref, s_ref, sb_ref):
        s = a_ref[...] + b_ref[...]
        s_ref[...] = s
        sb_ref[...] = s.astype(BF16)

    spec = lambda: pl.BlockSpec((1, bh, w), lambda j, i, p: (j, i, 0))
    grid_spec = pltpu.PrefetchScalarGridSpec(
        num_scalar_prefetch=1, grid=(n, nb),
        in_specs=[pl.BlockSpec((1, bh, w), lambda j, i, p: (j, p[1] * nb + i, 0)), spec()], out_specs=[spec(), spec()])
    return pl.pallas_call(
        body, name=name, grid_spec=grid_spec,
        out_shape=[jax.ShapeDtypeStruct(got.shape, F32), jax.ShapeDtypeStruct(got.shape, BF16)],
        compiler_params=_cparams("parallel", "parallel"),
    )(place, buf, got)


def _add_received(pair, got, place, name):
    _, h, w = pair.shape
    bh = _block_rows(h)
    nb = h // bh

    def body(place_ref, own_ref, got_ref, o_ref):
        o_ref[...] = ((own_ref[0] + got_ref[0].astype(F32)) + got_ref[1].astype(F32)) + got_ref[2].astype(F32)

    grid_spec = pltpu.PrefetchScalarGridSpec(
        num_scalar_prefetch=1, grid=(nb,),
        in_specs=[pl.BlockSpec((1, bh, w), lambda i, p: (p[0], i, 0)), pl.BlockSpec((3, bh, w), lambda i, p: (0, i, 0))],
        out_specs=pl.BlockSpec((bh, w), lambda i, p: (p[1] * nb + i, 0)))
    return pl.pallas_call(
        body, name=name, grid_spec=grid_spec, out_shape=jax.ShapeDtypeStruct((2 * h, w), F32),
        compiler_params=_cparams("parallel"),
    )(place, pair, got)


def _swap_reduced_halves(bufs):
    n = len(bufs)

    def body(*refs):
        outs, send_sems, recv_sems = refs[n:2 * n], refs[2 * n], refs[2 * n + 1]
        x, y, c = _place()
        copies = []
        for g in range(n):
            half = outs[g].shape[0] // 2
            own = outs[g].at[pl.ds(pl.multiple_of(c * half, 8), half), :]
            copies.append(pltpu.make_async_remote_copy(src_ref=own, dst_ref=own, send_sem=send_sems.at[g],
                                                       recv_sem=recv_sems.at[g], device_id=(x, y, 1 - c), device_id_type=MESH))
        for cp in copies:
            cp.start()
        for g in range(n):
            half = outs[g].shape[0] // 2
            other = outs[g].at[pl.ds(pl.multiple_of((1 - c) * half, 8), half), :]
            pltpu.make_async_remote_copy(src_ref=other, dst_ref=other, send_sem=send_sems.at[g], recv_sem=recv_sems.at[g],
                                         device_id=(x, y, 1 - c), device_id_type=MESH).wait_recv()
        for cp in copies:
            cp.wait_send()

    return pl.pallas_call(
        body, name="swap_reduced_halves", in_specs=[ANY] * n, out_specs=[ANY] * n,
        out_shape=[jax.ShapeDtypeStruct(b.shape, b.dtype) for b in bufs], input_output_aliases={g: g for g in range(n)},
        scratch_shapes=[pltpu.SemaphoreType.DMA((n,)), pltpu.SemaphoreType.DMA((n,))],
    )(*bufs)


HBM = pl.BlockSpec(memory_space=pltpu.HBM)
SEM = pl.BlockSpec(memory_space=pltpu.SEMAPHORE)


def _copies_start(name, bufs, n_copies, plan, after=None):
    n = len(bufs)
    extra = [] if after is None else [after]

    def body(*refs):
        sems = refs[n + len(extra):n + len(extra) + 2 * n_copies]
        x, y, c = _place()
        for i, (src, dst, dev) in enumerate(plan(refs[:n], x, y, c)):
            pltpu.make_async_remote_copy(src_ref=src, dst_ref=dst, send_sem=sems[i], recv_sem=sems[n_copies + i],
                                         device_id=dev, device_id_type=MESH).start()
        token = refs[-1]
        token[...] = jnp.zeros_like(token)

    out = pl.pallas_call(
        body, name=name,
        out_shape=[pltpu.SemaphoreType.DMA(())] * (2 * n_copies) + [pltpu.HBM(b.shape, b.dtype) for b in bufs]
        + [jax.ShapeDtypeStruct((8, 128), F32)],
        in_specs=[HBM] * n + [ANY] * len(extra),
        out_specs=[SEM] * (2 * n_copies) + [HBM] * n + [pl.BlockSpec(memory_space=pltpu.VMEM)],
        input_output_aliases={i: 2 * n_copies + i for i in range(n)},
        compiler_params=pltpu.CompilerParams(has_side_effects=pltpu.SideEffectType.DATAFLOW_SIDE_EFFECTING),
    )(*[pltpu.with_memory_space_constraint(b, pltpu.HBM) for b in bufs], *extra)
    return list(out[:2 * n_copies]), list(out[2 * n_copies:-1]), out[-1]


def _copies_wait(name, bufs, sems, after, plan):
    n = len(bufs)
    k = len(sems) // 2
    after = list(after) if isinstance(after, (list, tuple)) else [after]

    def body(*refs):
        sem_refs = refs[n:n + 2 * k]
        x, y, c = _place()
        for i, (sent, landed, dev) in enumerate(plan(refs[:n], x, y, c)):
            cp = pltpu.make_async_remote_copy(src_ref=sent, dst_ref=landed, send_sem=sem_refs[i], recv_sem=sem_refs[k + i],
                                              device_id=dev, device_id_type=MESH)
            cp.wait_send()
            cp.wait_recv()

    return pl.pallas_call(
        body, name=name, out_shape=[pltpu.HBM(b.shape, b.dtype) for b in bufs],
        in_specs=[HBM] * n + [SEM] * (2 * k) + [ANY] * len(after), out_specs=[HBM] * n,
        input_output_aliases={i: i for i in range(n)},
        compiler_params=pltpu.CompilerParams(has_side_effects=pltpu.SideEffectType.DATAFLOW_SIDE_EFFECTING),
    )(*bufs, *sems, *after)


def _row_half(ref, which, axis):
    half = ref.shape[axis] // 2
    rows = pl.ds(pl.multiple_of(which * half, 8), half)
    return ref.at[rows, :] if axis == 0 else ref.at[:, rows, :]


class _SplitGather:
    def __init__(self, own, after):
        self.n = len(own)
        self.state = _copies_start("gather_start", own, 3 * self.n, self._sent, after)

    @staticmethod
    def _sent(refs, x, y, c):
        return [(w.at[2 * x + y], w.at[2 * x + y], (px, py, c)) for w in refs for px, py in _other_chips(x, y)]

    @staticmethod
    def _landed(refs, x, y, c):
        return [(w.at[2 * x + y], w.at[2 * px + py], (px, py, c)) for w in refs for px, py in _other_chips(x, y)]

    def token(self):
        return self.state[2]

    def wait(self, which, name, after):
        sems, bufs, _ = self.state
        k = 3 * self.n
        mine = [sems[3 * i + j] for i in which for j in range(3)] + [sems[k + 3 * i + j] for i in which for j in range(3)]
        return _copies_wait(name, [bufs[i] for i in which], mine, after, self._landed)


def _slot_half(ref, px, py, pc):
    half = ref.shape[1] // 2
    return ref.at[2 * px + py, pl.ds(pl.multiple_of(pc * half, 16), half), :]


class _SplitGatherHalves:
    def __init__(self, own):
        self.state = _copies_start("gather_b_start", [own], 3, self._sent)

    @staticmethod
    def _sent(refs, x, y, c):
        (w,) = refs
        return [(_slot_half(w, x, y, c), _slot_half(w, x, y, c), (px, py, c)) for px, py in _other_chips(x, y)]

    def token(self):
        return self.state[2]

    @staticmethod
    def _landed(refs, x, y, c):
        (w,) = refs
        return [(_slot_half(w, x, y, c), _slot_half(w, px, py, c), (px, py, c)) for px, py in _other_chips(x, y)]

    def finish(self, after):
        sems, bufs, _ = self.state
        (buf,) = _copies_wait("gather_b_wait", bufs, sems, after, self._landed)

        def body(buf_ref, out_ref, send_sems, recv_sems):
            x, y, c = _place()
            chips = _other_chips(x, y)

            def to_sibling(j, part, to):
                return pltpu.make_async_remote_copy(src_ref=part, dst_ref=part, send_sem=send_sems.at[j],
                                                    recv_sem=recv_sems.at[j], device_id=to, device_id_type=MESH)

            passed = [to_sibling(j, _slot_half(out_ref, *chip, c), (x, y, 1 - c)) for j, chip in enumerate(chips)]
            for cp in passed:
                cp.start()
            for j, chip in enumerate(chips):
                to_sibling(j, _slot_half(out_ref, *chip, 1 - c), (x, y, c)).wait_recv()
            for cp in passed:
                cp.wait_send()

        return pl.pallas_call(
            body, name="gather_b_pass", in_specs=[ANY], out_specs=ANY, out_shape=jax.ShapeDtypeStruct(buf.shape, buf.dtype),
            input_output_aliases={0: 0}, scratch_shapes=[pltpu.SemaphoreType.DMA((3,)), pltpu.SemaphoreType.DMA((3,))],
        )(buf)


class _SplitReduction:
    def __init__(self, tag, groups, place):
        self.tag, self.groups, self.place = tag, groups, place

    def start_pair(self, bufs):
        n = len(bufs)
        lands = [lax.empty((4, b.shape[1] // 2, b.shape[2]), F32) for b in bufs]
        plan = lambda refs, x, y, c: [(_row_half(refs[i], 1 - c, 1), refs[n + i], (x, y, 1 - c)) for i in range(n)]
        self._pair = (_copies_start("pair_%s_start" % self.tag, bufs + lands, n, plan), plan, n)
        return self._pair[0][2]

    def pair_done_start_scatter(self, after):
        (sems, bufs, _), plan, n = self._pair
        out = _copies_wait("pair_%s_wait" % self.tag, bufs, sems, after, plan)
        pairs = [_add_pair(out[i], out[n + i], self.place, "add_pair_" + g) for i, g in enumerate(self.groups)]
        self._pair_f32 = [p[0] for p in pairs]
        lands = [lax.empty((3,) + p[1].shape[1:], BF16) for p in pairs]
        plan = lambda refs, x, y, c: [(refs[i].at[2 * px + py], refs[n + i].at[j], (px, py, c))
                                      for i in range(n) for j, (px, py) in enumerate(_other_chips(x, y))]
        self._scatter = (_copies_start("scatter_%s_start" % self.tag, [p[1] for p in pairs] + lands, 3 * n, plan), plan, n)
        return self._scatter[0][2]

    def scatter_done(self, after):
        (sems, bufs, _), plan, n = self._scatter
        out = _copies_wait("scatter_%s_wait" % self.tag, bufs, sems, after, plan)
        return [_add_received(self._pair_f32[i], out[n + i], self.place, "add_received_" + g)
                for i, g in enumerate(self.groups)]

    def start_join(self, halves):
        n = len(halves)
        sent = lambda refs, x, y, c: [(_row_half(r, c, 0), _row_half(r, c, 0), (x, y, 1 - c)) for r in refs]
        landed = lambda refs, x, y, c: [(_row_half(r, c, 0), _row_half(r, 1 - c, 0), (x, y, 1 - c)) for r in refs]
        self._join = (_copies_start("join_%s_start" % self.tag, halves, n, sent), landed)
        return self._join[0][2]

    def join_done(self, after):
        (sems, bufs, _), landed = self._join
        return _copies_wait("join_%s_wait" % self.tag, bufs, sems, after, landed)


def _pair_sum_small(mine):
    rows, w = mine.shape

    def body(in_ref, out_ref, sibling, send_sem, recv_sem):
        x, y, c = _place()
        swap = pltpu.make_async_remote_copy(src_ref=in_ref, dst_ref=sibling, send_sem=send_sem, recv_sem=recv_sem,
                                            device_id=(x, y, 1 - c), device_id_type=MESH)
        swap.start()
        swap.wait()
        out_ref[...] = in_ref[...] + sibling[...]

    return pl.pallas_call(
        body, name="pair_sum_small", out_shape=jax.ShapeDtypeStruct((rows, w), F32),
        in_specs=[pl.BlockSpec(memory_space=pltpu.VMEM)], out_specs=pl.BlockSpec(memory_space=pltpu.VMEM),
        scratch_shapes=[pltpu.VMEM((rows, w), F32), pltpu.SemaphoreType.DMA, pltpu.SemaphoreType.DMA],
        compiler_params=pltpu.CompilerParams(vmem_limit_bytes=VMEM_LIMIT_V7X),
    )(mine)


class _SplitChipSum:
    def __init__(self, pair, place):
        self.place = place
        slots = lax.empty((4,) + pair.shape, F32)
        sent = lambda refs, x, y, c: [(refs[0], refs[1].at[2 * x + y], (px, py, c)) for px, py in _other_chips(x, y)]
        self.landed = lambda refs, x, y, c: [(refs[0], refs[1].at[2 * px + py], (px, py, c)) for px, py in _other_chips(x, y)]
        self.state = _copies_start("small_sum_start", [pair, slots], 3, sent)

    def token(self):
        return self.state[2]

    def done(self, after):
        sems, bufs, _ = self.state
        pair, slots = _copies_wait("small_sum_wait", bufs, sems, after, self.landed)
        rows, w = pair.shape

        def body(place_ref, pair_ref, slots_ref, out_ref):
            for j in range(4):
                own = place_ref[0] == j

                @pl.when(own)
                def _():
                    out_ref[...] = pair_ref[...] if j == 0 else out_ref[...] + pair_ref[...]

                @pl.when(jnp.logical_not(own))
                def _():
                    out_ref[...] = slots_ref[j] if j == 0 else out_ref[...] + slots_ref[j]

        grid_spec = pltpu.PrefetchScalarGridSpec(
            num_scalar_prefetch=1, grid=(1,),
            in_specs=[pl.BlockSpec((rows, w), lambda i, p: (0, 0)), pl.BlockSpec((4, rows, w), lambda i, p: (0, 0, 0))],
            out_specs=pl.BlockSpec((rows, w), lambda i, p: (0, 0)))
        return pl.pallas_call(
            body, name="small_sum_add", grid_spec=grid_spec, out_shape=jax.ShapeDtypeStruct((rows, w), F32),
            compiler_params=_cparams("arbitrary"),
        )(self.place, pair, slots)


def _join_column_shards(g):
    return jnp.transpose(g, (1, 0, 2)).reshape(g.shape[1], 4 * g.shape[2])


def _split_column_shards(w):
    r = w.shape[0]
    return jnp.transpose(w.reshape(r, 4, w.shape[1] // 4), (1, 0, 2))


def _small_rows(shape):
    return -(-int(np.prod(shape)) // 1024)


def _pack_small(vals):
    segs = []
    for name, shape in SMALL_WEIGHTS:
        flat = vals[name].reshape(-1)
        segs.append(jnp.pad(flat, (0, _small_rows(shape) * 1024 - flat.shape[0])))
    total = sum(s.shape[0] for s in segs) // 1024
    segs.append(jnp.zeros((-total % 8 * 1024,), F32))
    return jnp.concatenate(segs).reshape(-1, 1024)


def _unpack_small(packed):
    out, off = {}, 0
    for name, shape in SMALL_WEIGHTS:
        rows = _small_rows(shape)
        out[name] = packed[off:off + rows].reshape(-1)[:int(np.prod(shape))].reshape(shape)
        off += rows
    return out


W_IN_SHARD = D_IN // 4
W_IN_GAP = 1216


def _pad_w_in(g):
    cut = W_IN_GAP - W_IN_SHARD
    return jnp.concatenate([g[0], g[1][:, :cut], jnp.zeros((g.shape[1], D_IN_PAD - D_IN), g.dtype), g[1][:, cut:], g[2], g[3]],
                           axis=1)


def _unpad_w_in(g):
    skip = D_IN_PAD - D_IN
    second = jnp.concatenate([g[:, W_IN_SHARD:W_IN_GAP], g[:, W_IN_GAP + skip:2 * W_IN_SHARD + skip]], axis=1)
    return jnp.stack([g[:, :W_IN_SHARD], second, g[:, 2 * W_IN_SHARD + skip:3 * W_IN_SHARD + skip],
                      g[:, 3 * W_IN_SHARD + skip:]])


def _pad_heads(w):
    r = w.shape[0]
    return jnp.pad(w.reshape(r, N_HEADS, QK_HEAD), ((0, 0), (0, 0), (0, HEAD_PAD - QK_HEAD))).reshape(r, N_HEADS * HEAD_PAD)


def _unpad_heads(g):
    r = g.shape[0]
    return g.reshape(r, N_HEADS, HEAD_PAD)[:, :, :QK_HEAD].reshape(r, N_HEADS * QK_HEAD)


def _local_step(x, positions, tgt, gather_b, small, gather, red_a, red_rest):
    l = x.shape[0]
    t = min(l, 512)
    t_mlp = min(l, 256)
    tq = min(l, 1024)
    tc = min(l, 256)
    row = lambda v: v.reshape(1, -1).astype(F32)

    g1, g2 = row(small["norm_mix"]), row(small["norm_mlp"])
    gqa, gkva = row(small["q_a_norm"]), row(small["kv_a_norm"])
    gq = jnp.pad(row(small["q_norm"]), ((0, 0), (0, HEAD_PAD - QK_HEAD)))
    gk = jnp.pad(row(small["k_norm"]), ((0, 0), (0, HEAD_PAD - QK_HEAD)))
    half = QK_ROPE // 2
    inv_freq = ROPE_THETA ** (-jnp.arange(half, dtype=F32) / half)
    invf = jnp.concatenate([inv_freq, inv_freq, jnp.zeros((64,), F32)]).reshape(1, 128)
    sgn = jnp.concatenate([-jnp.ones((half,), F32), jnp.ones((half,), F32), jnp.zeros((64,), F32)]).reshape(1, 128)
    pos = positions.reshape(l, 1)

    a_re, a_im = small["ssm_a_re"], small["ssm_a_im"]
    log_dt = small["ssm_log_dt"].reshape(SSM_GROUPS, 1)
    to_gcp = lambda b: jnp.transpose(b, (0, 2, 1)).reshape(SSM_WIDTH, SSM_STATE)
    from_gcp = lambda b: jnp.transpose(b.reshape(SSM_GROUPS, SSM_GROUP_CH, SSM_STATE), (0, 2, 1))
    b_re, b_im = to_gcp(small["ssm_b_re"]), to_gcp(small["ssm_b_im"])
    c_re, c_im = small["ssm_c_re"].reshape(SSM_WIDTH, SSM_STATE), small["ssm_c_im"].reshape(SSM_WIDTH, SSM_STATE)
    wb, wc, tabs_fwd, tabs_rev = _ssm_param_fwd(a_re, a_im, log_dt, b_re, b_im, c_re, c_im)
    dskip = row(small["ssm_d"])
    b_glu = row(small["b_glu"])
    w_in_p = _pad_w_in(gather_b.finish([gather.token(), wb]))

    u, lat, gs, gm = _in_proj_fwd(x, g1, w_in_p, t, gather.token())
    grp_c, grp_d, grp_e = gather.wait([0, 1, 2], "gather_cde_wait", u)
    w_qb_p = _pad_heads(_join_column_shards(grp_c))
    xr, xi, y, y_ssm = _ssm_fwd(u, wb, wc, tabs_fwd, dskip, grp_d, b_glu, grp_e, tc)
    q, k, v = _mla_pre_fwd(lat, pos, invf, sgn, gqa, gkva, gq, gk, w_qb_p, grp_d, t)
    attn, lse = _attn_fwd(q, k, v, tq)
    (grp_a,) = gather.wait([3], "gather_a_wait", attn)
    y_mla, mixed, h = _merge_fwd(attn, y_ssm, gs, gm, x, grp_a, t)
    dh, hn, da, hid, dout, loss_blk, g_norm_mlp = _mlp_fwd_bwd(h, tgt, g2, grp_a, t_mlp)

    ga = _wgrad_into(hn, da, "w_up", "col", _wgrad_into(hid, dout, "w_down", "row"))
    dys, dym, dgs, dgm, dattn = _merge_bwd(dh, y_ssm, y_mla, gs, gm, grp_a, t)
    ga = _wgrad_into(attn, dym, "w_o_mla", "row", _wgrad_into(mixed, dh, "w_out", "row", ga))

    dq, dk, dv = _attn_bwd(q, k, v, attn, dattn, lse, tq, red_a.start_pair([ga]))
    d_lat, ql, dq0, ckn, dkv, g_qa, g_kva, g_q, g_k = _mla_pre_bwd(lat, pos, invf, sgn, gqa, gkva, gq, gk, w_qb_p, grp_d,
                                                                    dq, dk, dv, t, red_a.pair_done_start_scatter(dk))
    gc = _split_column_shards(_unpad_heads(_wgrad(ql, dq0, "wgrad_q_b")))

    d_u, adj, dy, z, z2, dpre, g_b_glu, g_d, g_lr, g_li = _ssm_bwd(
        dys, y, u, xr, xi, wb, wc, tabs_rev, dskip, grp_d, b_glu, grp_e, tc)
    gd = _wgrad_into(z, dpre, "w_glu", "row", _wgrad_into(ckn, dkv, "w_kv_b", "col"))
    ge = _wgrad_into(z2, dys, "w_o_ssm", "col")
    grad_x, xn, dproj, g_norm_mix = _in_proj_bwd(x, g1, w_in_p, d_u, d_lat, dgs, dgm, dh, t)
    gb = _unpad_w_in(_wgrad(xn, dproj, "wgrad_in"))

    red_a.start_join(red_a.scatter_done(gb))
    g_wb = _wgrad_strips(u, adj, adj, "wgrad_ssm_b", 1, red_rest.start_pair([gb, gc, gd, ge]))
    g_wct = _wgrad_strips(dy, xr, xi, "wgrad_ssm_c", 0, red_rest.pair_done_start_scatter(g_wb))
    g_ar, g_ai, g_ldt, g_br, g_bi, g_cr, g_ci = _ssm_param_bwd(a_re, a_im, log_dt, b_re, b_im, g_lr, g_li, g_wb, g_wct)

    g_small = {
        "norm_mix": g_norm_mix.reshape(-1), "norm_mlp": g_norm_mlp.reshape(-1), "q_a_norm": g_qa.reshape(-1),
        "kv_a_norm": g_kva.reshape(-1), "q_norm": g_q.reshape(-1)[:QK_HEAD], "k_norm": g_k.reshape(-1)[:QK_HEAD],
        "ssm_a_re": g_ar, "ssm_a_im": g_ai, "ssm_log_dt": g_ldt.reshape(-1),
        "ssm_b_re": from_gcp(g_br), "ssm_b_im": from_gcp(g_bi),
        "ssm_c_re": g_cr.reshape(SSM_GROUPS, SSM_GROUP_CH, SSM_STATE), "ssm_c_im": g_ci.reshape(SSM_GROUPS, SSM_GROUP_CH, SSM_STATE),
        "ssm_d": g_d.reshape(SSM_GROUPS, SSM_GROUP_CH), "b_glu": g_b_glu.reshape(-1),
    }
    return loss_blk[0, 0], grad_x, g_small


def kernel(x, positions, norm_mix, w_in, q_a_norm, kv_a_norm, w_q_b, w_kv_b, q_norm, k_norm, w_o_mla, ssm_a_re, ssm_a_im, ssm_log_dt, ssm_b_re, ssm_b_im, ssm_c_re, ssm_c_im, ssm_d, w_glu, b_glu, w_o_ssm, w_out, norm_mlp, w_up, w_down, loss_target, m_norm_mix, m_w_in, m_q_a_norm, m_kv_a_norm, m_w_q_b, m_w_kv_b, m_q_norm, m_k_norm, m_w_o_mla, m_ssm_a_re, m_ssm_a_im, m_ssm_log_dt, m_ssm_b_re, m_ssm_b_im, m_ssm_c_re, m_ssm_c_im, m_ssm_d, m_w_glu, m_b_glu, m_w_o_ssm, m_w_out, m_norm_mlp, m_w_up, m_w_down, v_norm_mix, v_w_in, v_q_a_norm, v_kv_a_norm, v_w_q_b, v_w_kv_b, v_q_norm, v_k_norm, v_w_o_mla, v_ssm_a_re, v_ssm_a_im, v_ssm_log_dt, v_ssm_b_re, v_ssm_b_im, v_ssm_c_re, v_ssm_c_im, v_ssm_d, v_w_glu, v_b_glu, v_w_o_ssm, v_w_out, v_norm_mlp, v_w_up, v_w_down):
    args = dict(locals())
    w = {n: args[n][0] for n in WEIGHT_ORDER}
    m = {n: args["m_" + n][0] for n in WEIGHT_ORDER}
    v = {n: args["v_" + n][0] for n in WEIGHT_ORDER}
    big_names = [n for n, *_ in BIG_WEIGHTS]
    small_names = [n for n, _ in SMALL_WEIGHTS]

    place = jnp.stack([2 * lax.axis_index("x") + lax.axis_index("y"), lax.axis_index("c")]).astype(jnp.int32)
    rest = ["b", "c", "d", "e"]

    gather_b = _SplitGatherHalves(_cast_shards(w, "b", place))
    gather = _SplitGather([_cast_shards(w, g, place, gather_b.token()) for g in ("c", "d", "e", "a")], gather_b.token())
    red_a = _SplitReduction("a", ["a"], place)
    red_rest = _SplitReduction("rest", rest, place)
    small = {n: w[n] for n in small_names}

    loss_local, grad_x, g_small = _local_step(x[0], positions[0], loss_target[0], gather_b, small, gather, red_a, red_rest)
    loss = lax.psum(loss_local, ("x", "y", "c"))

    grad_w, delta_w, new_m, new_v = {}, {}, {}, {}

    def update(names, reduced, token):
        for n in names:
            g, off, _, _ = _place_in_group(n)
            grad_w[n], delta_w[n], new_m[n], new_v[n] = _adamw(w[n], reduced[g], m[n], v[n], "adamw_" + n, off, token)
            token = new_v[n]

    chip_sum = _SplitChipSum(_pair_sum_small(_pack_small(g_small)), place)
    in_a = [n for n, _ in GROUPS["a"][1]]
    update(in_a, {"a": red_a.join_done(chip_sum.token())[0]}, chip_sum.token())
    small_sum = chip_sum.done(new_v[in_a[-1]])
    g_s, d_s, m_s, v_s = _adamw(_pack_small(small), small_sum, _pack_small({n: m[n] for n in small_names}),
                                _pack_small({n: v[n] for n in small_names}), "adamw_small", 0, small_sum)
    g_s, d_s, m_s, v_s = _unpack_small(g_s), _unpack_small(d_s), _unpack_small(m_s), _unpack_small(v_s)
    for n in small_names:
        grad_w[n], delta_w[n], new_m[n], new_v[n] = g_s[n], d_s[n], m_s[n], v_s[n]
    halves = red_rest.scatter_done(v_s[small_names[0]])
    reduced_rest = dict(zip(rest, _swap_reduced_halves(halves)))
    update([n for n in big_names if n not in in_a], reduced_rest, halves[0])

    lead = lambda d: [d[n][None] for n in WEIGHT_ORDER]
    return (loss, grad_x[None], *lead(grad_w), *lead(delta_w), *lead(new_m), *lead(new_v))
```
